```python
import jax, jax.numpy as jnp
from jax import lax
import numpy as np

D_MODEL = 1024
BATCH = 8
SEQ = 4096
DEPTH = 1

PLE_DIM = 256
RET_HEADS = 4
RET_HEAD_DIM = 128
RET_WIDTH = RET_HEADS * RET_HEAD_DIM
FOX_HEADS = 8
FOX_HEAD_DIM = 64
FOX_WIDTH = FOX_HEADS * FOX_HEAD_DIM
D_FF = 2816
RET_CHUNK = 128
Q_BLOCK = 128
ROPE_BASE = 10000.0
EPS = 1e-6
IN_SIZES = (RET_WIDTH,) * 4 + (FOX_WIDTH,) * 3 + (FOX_HEADS,)
IN_COLS = 4 * RET_WIDTH + 3 * FOX_WIDTH + FOX_HEADS

kernel_name = 'hybrid_retention_forgetting_macaron_block'


def rms_norm(x, g):
    x32 = x.astype(jnp.float32)
    y = x32 * lax.rsqrt(jnp.mean(x32 * x32, axis=-1, keepdims=True) + EPS)
    return (y * g.astype(jnp.float32)).astype(x.dtype)


def swiglu(x, w_gate, w_up, w_down):
    return (jax.nn.silu(x @ w_gate) * (x @ w_up)) @ w_down


def rotary(x, pos):
    half = x.shape[-1] // 2
    inv_freq = 1.0 / (ROPE_BASE ** (jnp.arange(half, dtype=jnp.float32) / half))
    ang = pos.astype(jnp.float32)[..., None] * inv_freq
    cos = jnp.cos(ang)[:, :, None, :]
    sin = jnp.sin(ang)[:, :, None, :]
    x1 = x[..., 0::2]
    x2 = x[..., 1::2]
    return jnp.stack([x1 * cos - x2 * sin, x1 * sin + x2 * cos], axis=-1).reshape(x.shape)


def chunkwise_retention(q, k, v):
    b, s, h, dk = q.shape
    dv = v.shape[-1]
    c = RET_CHUNK
    n = s // c
    to_chunks = lambda t: t.reshape(b, n, c, h, t.shape[-1]).transpose(0, 3, 1, 2, 4)
    q = to_chunks(q)
    k = to_chunks(k) * (dk ** -0.5)
    v = to_chunks(v)
    log_gamma = jnp.log1p(-jnp.exp2(-5.0 - jnp.arange(h, dtype=jnp.float32)))
    idx = jnp.arange(c, dtype=jnp.float32)
    diff = idx[:, None] - idx[None, :]
    dmask = jnp.where(diff >= 0, jnp.exp(log_gamma[:, None, None] * jnp.maximum(diff, 0.0)), 0.0)
    scores = jnp.einsum('bhncd,bhnmd->bhncm', q, k) * dmask[None, :, None]
    y_inner = jnp.einsum('bhncm,bhnme->bhnce', scores, v)
    k_dec = k * jnp.exp(log_gamma[:, None] * (c - 1 - idx))[None, :, None, :, None]
    kv = jnp.einsum('bhnmd,bhnme->nbhde', k_dec, v)
    chunk_decay = jnp.exp(log_gamma * c)[None, :, None, None]

    def step(state, kv_n):
        return chunk_decay * state + kv_n, state

    _, states = lax.scan(step, jnp.zeros((b, h, dk, dv), jnp.float32), kv)
    q_dec = q * jnp.exp(log_gamma[:, None] * (idx + 1.0))[None, :, None, :, None]
    y_cross = jnp.einsum('bhncd,nbhde->bhnce', q_dec, states)
    y = y_inner + y_cross
    return y.transpose(0, 2, 3, 1, 4).reshape(b, s, h, dv)


def head_group_norm(y):
    mu = jnp.mean(y, axis=-1, keepdims=True)
    var = jnp.mean(jnp.square(y - mu), axis=-1, keepdims=True)
    return (y - mu) * lax.rsqrt(var + EPS)


def forgetting_attention(q, k, v, log_f):
    b, s, h, d = q.shape
    nb = s // Q_BLOCK
    cum = jnp.cumsum(log_f, axis=1).transpose(0, 2, 1)
    qh = q.transpose(0, 2, 1, 3) * (d ** -0.5)
    kh = k.transpose(0, 2, 1, 3)
    vh = v.transpose(0, 2, 1, 3)
    q_blocks = qh.reshape(b, h, nb, Q_BLOCK, d).transpose(2, 0, 1, 3, 4)
    c_blocks = cum.reshape(b, h, nb, Q_BLOCK).transpose(2, 0, 1, 3)
    k_pos = jnp.arange(s)

    def block(args):
        q_n, c_n, n = args
        logits = jnp.einsum('bhqd,bhkd->bhqk', q_n, kh) + c_n[..., None] - cum[:, :, None, :]
        q_pos = n * Q_BLOCK + jnp.arange(Q_BLOCK)
        logits = jnp.where(k_pos[None, :] <= q_pos[:, None], logits, -jnp.inf)
        probs = jax.nn.softmax(logits, axis=-1)
        return jnp.einsum('bhqk,bhkd->bhqd', probs, vh)

    o = lax.map(block, (q_blocks, c_blocks, jnp.arange(nb)))
    return o.transpose(1, 0, 3, 2, 4).reshape(b, s, h, d)


def _fwd_setup_inputs(seed: int = 0) -> dict:
    key = jax.random.key(seed)
    ks = jax.random.split(key, 24)
    f32 = jnp.float32

    def nrm(k, shape, fan_in):
        return jax.random.normal(k, shape, f32) * (fan_in ** -0.5)

    def gain(k, shape):
        return 1.0 + 0.05 * jax.random.normal(k, shape, f32)

    x = jax.random.normal(ks[0], (BATCH, SEQ, D_MODEL), f32)
    p = jax.random.normal(ks[1], (DEPTH, BATCH, SEQ, PLE_DIM), f32)
    positions = jnp.broadcast_to(jnp.arange(SEQ, dtype=jnp.int32)[None, :], (BATCH, SEQ))
    return {
        'x': x,
        'p': p,
        'positions': positions,
        'ln_ffn1': gain(ks[2], (DEPTH, D_MODEL)),
        'w_ffn1_gate': nrm(ks[3], (DEPTH, D_MODEL, D_FF), D_MODEL),
        'w_ffn1_up': nrm(ks[4], (DEPTH, D_MODEL, D_FF), D_MODEL),
        'w_ffn1_down': nrm(ks[5], (DEPTH, D_FF, D_MODEL), D_FF),
        'ln_mix': gain(ks[6], (DEPTH, D_MODEL)),
        'w_in': nrm(ks[7], (DEPTH, D_MODEL, IN_COLS), D_MODEL),
        'b_forget': jax.random.uniform(ks[8], (DEPTH, FOX_HEADS), f32, 1.0, 6.0),
        'w_merge': nrm(ks[9], (DEPTH, D_MODEL, 2 * D_MODEL), D_MODEL),
        'b_merge': 0.02 * jax.random.normal(ks[10], (DEPTH, 2 * D_MODEL), f32),
        'w_ret_out': nrm(ks[11], (DEPTH, RET_WIDTH, D_MODEL), RET_WIDTH),
        'w_fox_out': nrm(ks[12], (DEPTH, FOX_WIDTH, D_MODEL), FOX_WIDTH),
        'w_out': nrm(ks[13], (DEPTH, D_MODEL, D_MODEL), D_MODEL),
        'ln_ffn2': gain(ks[14], (DEPTH, D_MODEL)),
        'w_ffn2_gate': nrm(ks[15], (DEPTH, D_MODEL, D_FF), D_MODEL),
        'w_ffn2_up': nrm(ks[16], (DEPTH, D_MODEL, D_FF), D_MODEL),
        'w_ffn2_down': nrm(ks[17], (DEPTH, D_FF, D_MODEL), D_FF),
        'ln_ple': gain(ks[18], (DEPTH, D_MODEL)),
        'w_ple': nrm(ks[19], (DEPTH, PLE_DIM, D_MODEL), PLE_DIM),
        'w_ple_gate': nrm(ks[20], (DEPTH, D_MODEL, D_MODEL), D_MODEL),
        'ln_final': gain(ks[21], (D_MODEL,)),
    }


def _fwd_reference(x, p, positions, ln_ffn1, w_ffn1_gate, w_ffn1_up, w_ffn1_down, ln_mix, w_in,
              b_forget, w_merge, b_merge, w_ret_out, w_fox_out, w_out, ln_ffn2, w_ffn2_gate,
              w_ffn2_up, w_ffn2_down, ln_ple, w_ple, w_ple_gate, ln_final):
    dt = x.dtype
    b, s, _ = x.shape
    split_at = np.cumsum(IN_SIZES)[:-1].tolist()
    h = x
    for i in range(DEPTH):
        h = h + 0.5 * swiglu(rms_norm(h, ln_ffn1[i]), w_ffn1_gate[i], w_ffn1_up[i], w_ffn1_down[i])

        u = rms_norm(h, ln_mix[i])
        proj = u @ w_in[i]
        r_q, r_k, r_v, r_g, f_q, f_k, f_v, f_f = jnp.split(proj, split_at, axis=-1)

        rq = rotary(r_q.astype(jnp.float32).reshape(b, s, RET_HEADS, RET_HEAD_DIM), positions)
        rk = rotary(r_k.astype(jnp.float32).reshape(b, s, RET_HEADS, RET_HEAD_DIM), positions)
        rv = r_v.astype(jnp.float32).reshape(b, s, RET_HEADS, RET_HEAD_DIM)
        y_ret = head_group_norm(chunkwise_retention(rq, rk, rv)).reshape(b, s, RET_WIDTH)
        y_ret = (y_ret * jax.nn.silu(r_g.astype(jnp.float32))).astype(dt)
        z_a = y_ret @ w_ret_out[i]

        log_f = jax.nn.log_sigmoid(f_f.astype(jnp.float32) + b_forget[i].astype(jnp.float32))
        fq = f_q.astype(jnp.float32).reshape(b, s, FOX_HEADS, FOX_HEAD_DIM)
        fk = f_k.astype(jnp.float32).reshape(b, s, FOX_HEADS, FOX_HEAD_DIM)
        fv = f_v.astype(jnp.float32).reshape(b, s, FOX_HEADS, FOX_HEAD_DIM)
        y_fox = forgetting_attention(fq, fk, fv, log_f).reshape(b, s, FOX_WIDTH).astype(dt)
        z_b = y_fox @ w_fox_out[i]

        gates = jax.nn.sigmoid(u @ w_merge[i] + b_merge[i])
        g_a, g_b = jnp.split(gates, 2, axis=-1)
        h = h + (g_a * z_a + g_b * z_b) @ w_out[i]

        h = h + 0.5 * swiglu(rms_norm(h, ln_ffn2[i]), w_ffn2_gate[i], w_ffn2_up[i], w_ffn2_down[i])

        ple_gate = jax.nn.sigmoid(rms_norm(h, ln_ple[i]) @ w_ple_gate[i])
        h = h + ple_gate * (p[i].astype(dt) @ w_ple[i])
    return rms_norm(h, ln_final)


import jax as _jax
import jax.numpy as _jnp

TWIN_FORMAT = 'train_step'
FWD_PARAMS = ['x', 'p', 'positions', 'ln_ffn1', 'w_ffn1_gate', 'w_ffn1_up', 'w_ffn1_down', 'ln_mix', 'w_in', 'b_forget', 'w_merge', 'b_merge', 'w_ret_out', 'w_fox_out', 'w_out', 'ln_ffn2', 'w_ffn2_gate', 'w_ffn2_up', 'w_ffn2_down', 'ln_ple', 'w_ple', 'w_ple_gate', 'ln_final']
TWIN_WEIGHTS = ['ln_ffn1', 'w_ffn1_gate', 'w_ffn1_up', 'w_ffn1_down', 'ln_mix', 'w_in', 'b_forget', 'w_merge', 'b_merge', 'w_ret_out', 'w_fox_out', 'w_out', 'ln_ffn2', 'w_ffn2_gate', 'w_ffn2_up', 'w_ffn2_down', 'ln_ple', 'w_ple', 'w_ple_gate', 'ln_final']
TWIN_DIFF_INPUT = 'x'
TWIN_INPUTS = ['x', 'p', 'positions', 'ln_ffn1', 'w_ffn1_gate', 'w_ffn1_up', 'w_ffn1_down', 'ln_mix', 'w_in', 'b_forget', 'w_merge', 'b_merge', 'w_ret_out', 'w_fox_out', 'w_out', 'ln_ffn2', 'w_ffn2_gate', 'w_ffn2_up', 'w_ffn2_down', 'ln_ple', 'w_ple', 'w_ple_gate', 'ln_final', 'loss_target', 'm_ln_ffn1', 'm_w_ffn1_gate', 'm_w_ffn1_up', 'm_w_ffn1_down', 'm_ln_mix', 'm_w_in', 'm_b_forget', 'm_w_merge', 'm_b_merge', 'm_w_ret_out', 'm_w_fox_out', 'm_w_out', 'm_ln_ffn2', 'm_w_ffn2_gate', 'm_w_ffn2_up', 'm_w_ffn2_down', 'm_ln_ple', 'm_w_ple', 'm_w_ple_gate', 'm_ln_final', 'v_ln_ffn1', 'v_w_ffn1_gate', 'v_w_ffn1_up', 'v_w_ffn1_down', 'v_ln_mix', 'v_w_in', 'v_b_forget', 'v_w_merge', 'v_b_merge', 'v_w_ret_out', 'v_w_fox_out', 'v_w_out', 'v_ln_ffn2', 'v_w_ffn2_gate', 'v_w_ffn2_up', 'v_w_ffn2_down', 'v_ln_ple', 'v_w_ple', 'v_w_ple_gate', 'v_ln_final']
TWIN_OUTPUTS = ['loss', 'grad_x', 'grad_ln_ffn1', 'grad_w_ffn1_gate', 'grad_w_ffn1_up', 'grad_w_ffn1_down', 'grad_ln_mix', 'grad_w_in', 'grad_b_forget', 'grad_w_merge', 'grad_b_merge', 'grad_w_ret_out', 'grad_w_fox_out', 'grad_w_out', 'grad_ln_ffn2', 'grad_w_ffn2_gate', 'grad_w_ffn2_up', 'grad_w_ffn2_down', 'grad_ln_ple', 'grad_w_ple', 'grad_w_ple_gate', 'grad_ln_final', 'delta_ln_ffn1', 'delta_w_ffn1_gate', 'delta_w_ffn1_up', 'delta_w_ffn1_down', 'delta_ln_mix', 'delta_w_in', 'delta_b_forget', 'delta_w_merge', 'delta_b_merge', 'delta_w_ret_out', 'delta_w_fox_out', 'delta_w_out', 'delta_ln_ffn2', 'delta_w_ffn2_gate', 'delta_w_ffn2_up', 'delta_w_ffn2_down', 'delta_ln_ple', 'delta_w_ple', 'delta_w_ple_gate', 'delta_ln_final', 'new_m_ln_ffn1', 'new_m_w_ffn1_gate', 'new_m_w_ffn1_up', 'new_m_w_ffn1_down', 'new_m_ln_mix', 'new_m_w_in', 'new_m_b_forget', 'new_m_w_merge', 'new_m_b_merge', 'new_m_w_ret_out', 'new_m_w_fox_out', 'new_m_w_out', 'new_m_ln_ffn2', 'new_m_w_ffn2_gate', 'new_m_w_ffn2_up', 'new_m_w_ffn2_down', 'new_m_ln_ple', 'new_m_w_ple', 'new_m_w_ple_gate', 'new_m_ln_final', 'new_v_ln_ffn1', 'new_v_w_ffn1_gate', 'new_v_w_ffn1_up', 'new_v_w_ffn1_down', 'new_v_ln_mix', 'new_v_w_in', 'new_v_b_forget', 'new_v_w_merge', 'new_v_b_merge', 'new_v_w_ret_out', 'new_v_w_fox_out', 'new_v_w_out', 'new_v_ln_ffn2', 'new_v_w_ffn2_gate', 'new_v_w_ffn2_up', 'new_v_w_ffn2_down', 'new_v_ln_ple', 'new_v_w_ple', 'new_v_w_ple_gate', 'new_v_ln_final']
TWIN_LEAF_KINDS = {'loss': 'loss', 'grad_x': 'grad_x', 'grad_ln_ffn1': 'grad_w', 'grad_w_ffn1_gate': 'grad_w', 'grad_w_ffn1_up': 'grad_w', 'grad_w_ffn1_down': 'grad_w', 'grad_ln_mix': 'grad_w', 'grad_w_in': 'grad_w', 'grad_b_forget': 'grad_w', 'grad_w_merge': 'grad_w', 'grad_b_merge': 'grad_w', 'grad_w_ret_out': 'grad_w', 'grad_w_fox_out': 'grad_w', 'grad_w_out': 'grad_w', 'grad_ln_ffn2': 'grad_w', 'grad_w_ffn2_gate': 'grad_w', 'grad_w_ffn2_up': 'grad_w', 'grad_w_ffn2_down': 'grad_w', 'grad_ln_ple': 'grad_w', 'grad_w_ple': 'grad_w', 'grad_w_ple_gate': 'grad_w', 'grad_ln_final': 'grad_w', 'delta_ln_ffn1': 'delta_w', 'delta_w_ffn1_gate': 'delta_w', 'delta_w_ffn1_up': 'delta_w', 'delta_w_ffn1_down': 'delta_w', 'delta_ln_mix': 'delta_w', 'delta_w_in': 'delta_w', 'delta_b_forget': 'delta_w', 'delta_w_merge': 'delta_w', 'delta_b_merge': 'delta_w', 'delta_w_ret_out': 'delta_w', 'delta_w_fox_out': 'delta_w', 'delta_w_out': 'delta_w', 'delta_ln_ffn2': 'delta_w', 'delta_w_ffn2_gate': 'delta_w', 'delta_w_ffn2_up': 'delta_w', 'delta_w_ffn2_down': 'delta_w', 'delta_ln_ple': 'delta_w', 'delta_w_ple': 'delta_w', 'delta_w_ple_gate': 'delta_w', 'delta_ln_final': 'delta_w', 'new_m_ln_ffn1': 'new_m', 'new_m_w_ffn1_gate': 'new_m', 'new_m_w_ffn1_up': 'new_m', 'new_m_w_ffn1_down': 'new_m', 'new_m_ln_mix': 'new_m', 'new_m_w_in': 'new_m', 'new_m_b_forget': 'new_m', 'new_m_w_merge': 'new_m', 'new_m_b_merge': 'new_m', 'new_m_w_ret_out': 'new_m', 'new_m_w_fox_out': 'new_m', 'new_m_w_out': 'new_m', 'new_m_ln_ffn2': 'new_m', 'new_m_w_ffn2_gate': 'new_m', 'new_m_w_ffn2_up': 'new_m', 'new_m_w_ffn2_down': 'new_m', 'new_m_ln_ple': 'new_m', 'new_m_w_ple': 'new_m', 'new_m_w_ple_gate': 'new_m', 'new_m_ln_final': 'new_m', 'new_v_ln_ffn1': 'new_v', 'new_v_w_ffn1_gate': 'new_v', 'new_v_w_ffn1_up': 'new_v', 'new_v_w_ffn1_down': 'new_v', 'new_v_ln_mix': 'new_v', 'new_v_w_in': 'new_v', 'new_v_b_forget': 'new_v', 'new_v_w_merge': 'new_v', 'new_v_b_merge': 'new_v', 'new_v_w_ret_out': 'new_v', 'new_v_w_fox_out': 'new_v', 'new_v_w_out': 'new_v', 'new_v_ln_ffn2': 'new_v', 'new_v_w_ffn2_gate': 'new_v', 'new_v_w_ffn2_up': 'new_v', 'new_v_w_ffn2_down': 'new_v', 'new_v_ln_ple': 'new_v', 'new_v_w_ple': 'new_v', 'new_v_w_ple_gate': 'new_v', 'new_v_ln_final': 'new_v'}


def _forward(args):
    return _fwd_reference(*[args[k] for k in FWD_PARAMS])


def _output_shape():
    out = _jax.eval_shape(lambda: _forward(_fwd_setup_inputs(0)))
    return out.shape, out.dtype

N_MICROBATCH = 1
ADAM_LR = 0.001
ADAM_B1 = 0.9
ADAM_B2 = 0.999
ADAM_EPS = 1e-08
ADAM_WD = 0.01
ADAM_STEP = 10
PER_EXAMPLE_BATCH_AXIS = {'x': 0, 'p': 1, 'positions': 0, 'loss_target': 0}
SHARED_INPUTS = []
_WEIGHT_DTYPES = {'ln_ffn1': _jnp.float32, 'w_ffn1_gate': _jnp.float32, 'w_ffn1_up': _jnp.float32, 'w_ffn1_down': _jnp.float32, 'ln_mix': _jnp.float32, 'w_in': _jnp.float32, 'b_forget': _jnp.float32, 'w_merge': _jnp.float32, 'b_merge': _jnp.float32, 'w_ret_out': _jnp.float32, 'w_fox_out': _jnp.float32, 'w_out': _jnp.float32, 'ln_ffn2': _jnp.float32, 'w_ffn2_gate': _jnp.float32, 'w_ffn2_up': _jnp.float32, 'w_ffn2_down': _jnp.float32, 'ln_ple': _jnp.float32, 'w_ple': _jnp.float32, 'w_ple_gate': _jnp.float32, 'ln_final': _jnp.float32}
MOMENT_SCALE = {'ln_ffn1': 8.191263e-02, 'w_ffn1_gate': 3.477113e-02, 'w_ffn1_up': 3.371559e-02, 'w_ffn1_down': 5.580169e-02, 'ln_mix': 1.243272e-01, 'w_in': 5.925264e-02, 'b_forget': 1.929562e-01, 'w_merge': 1.635322e-02, 'b_merge': 1.696072e-02, 'w_ret_out': 4.970168e-02, 'w_fox_out': 3.226679e-02, 'w_out': 5.919003e-02, 'ln_ffn2': 6.004174e-02, 'w_ffn2_gate': 2.662172e-02, 'w_ffn2_up': 2.588796e-02, 'w_ffn2_down': 4.289396e-02, 'ln_ple': 2.992953e-02, 'w_ple': 7.773561e-02, 'w_ple_gate': 2.982800e-02, 'ln_final': 3.202848e+01}


def _to_microbatches(a, axis):
    t = _jnp.moveaxis(a, axis, 0)
    t = t.reshape((N_MICROBATCH, t.shape[0] // N_MICROBATCH) + t.shape[1:])
    return _jnp.moveaxis(t, 1, axis + 1)


def setup_inputs(seed: int = 0) -> dict:
    inp = _fwd_setup_inputs(seed)
    key = _jax.random.fold_in(_jax.random.key(seed), 7919)
    shape, _ = _output_shape()
    out = dict(inp)
    out["loss_target"] = _jax.random.normal(_jax.random.fold_in(key, 0), shape, _jnp.float32)
    for i, name in enumerate(TWIN_WEIGHTS):
        w = inp[name].astype(_jnp.float32)
        if MOMENT_SCALE is None:
            s = _jnp.sqrt(_jnp.mean(_jnp.square(w)) + 1e-30)
        else:
            s = MOMENT_SCALE[name]
        km, kv = _jax.random.split(_jax.random.fold_in(key, i + 1))
        out[name] = w
        out["m_" + name] = s * _jax.random.normal(km, w.shape, _jnp.float32)
        out["v_" + name] = (s * s) * _jax.random.uniform(kv, w.shape, _jnp.float32, 0.5, 1.5)
    if N_MICROBATCH > 1:
        for name, axis in PER_EXAMPLE_BATCH_AXIS.items():
            out[name] = _to_microbatches(out[name], axis)
    return {'x': out['x'], 'p': out['p'], 'positions': out['positions'], 'ln_ffn1': out['ln_ffn1'], 'w_ffn1_gate': out['w_ffn1_gate'], 'w_ffn1_up': out['w_ffn1_up'], 'w_ffn1_down': out['w_ffn1_down'], 'ln_mix': out['ln_mix'], 'w_in': out['w_in'], 'b_forget': out['b_forget'], 'w_merge': out['w_merge'], 'b_merge': out['b_merge'], 'w_ret_out': out['w_ret_out'], 'w_fox_out': out['w_fox_out'], 'w_out': out['w_out'], 'ln_ffn2': out['ln_ffn2'], 'w_ffn2_gate': out['w_ffn2_gate'], 'w_ffn2_up': out['w_ffn2_up'], 'w_ffn2_down': out['w_ffn2_down'], 'ln_ple': out['ln_ple'], 'w_ple': out['w_ple'], 'w_ple_gate': out['w_ple_gate'], 'ln_final': out['ln_final'], 'loss_target': out['loss_target'], 'm_ln_ffn1': out['m_ln_ffn1'], 'm_w_ffn1_gate': out['m_w_ffn1_gate'], 'm_w_ffn1_up': out['m_w_ffn1_up'], 'm_w_ffn1_down': out['m_w_ffn1_down'], 'm_ln_mix': out['m_ln_mix'], 'm_w_in': out['m_w_in'], 'm_b_forget': out['m_b_forget'], 'm_w_merge': out['m_w_merge'], 'm_b_merge': out['m_b_merge'], 'm_w_ret_out': out['m_w_ret_out'], 'm_w_fox_out': out['m_w_fox_out'], 'm_w_out': out['m_w_out'], 'm_ln_ffn2': out['m_ln_ffn2'], 'm_w_ffn2_gate': out['m_w_ffn2_gate'], 'm_w_ffn2_up': out['m_w_ffn2_up'], 'm_w_ffn2_down': out['m_w_ffn2_down'], 'm_ln_ple': out['m_ln_ple'], 'm_w_ple': out['m_w_ple'], 'm_w_ple_gate': out['m_w_ple_gate'], 'm_ln_final': out['m_ln_final'], 'v_ln_ffn1': out['v_ln_ffn1'], 'v_w_ffn1_gate': out['v_w_ffn1_gate'], 'v_w_ffn1_up': out['v_w_ffn1_up'], 'v_w_ffn1_down': out['v_w_ffn1_down'], 'v_ln_mix': out['v_ln_mix'], 'v_w_in': out['v_w_in'], 'v_b_forget': out['v_b_forget'], 'v_w_merge': out['v_w_merge'], 'v_b_merge': out['v_b_merge'], 'v_w_ret_out': out['v_w_ret_out'], 'v_w_fox_out': out['v_w_fox_out'], 'v_w_out': out['v_w_out'], 'v_ln_ffn2': out['v_ln_ffn2'], 'v_w_ffn2_gate': out['v_w_ffn2_gate'], 'v_w_ffn2_up': out['v_w_ffn2_up'], 'v_w_ffn2_down': out['v_w_ffn2_down'], 'v_ln_ple': out['v_ln_ple'], 'v_w_ple': out['v_w_ple'], 'v_w_ple_gate': out['v_w_ple_gate'], 'v_ln_final': out['v_ln_final']}


def _loss(weights, diff, rest, loss_target):
    with _jax.named_scope("forward"):
        args = {**rest, TWIN_DIFF_INPUT: diff, **{k: w.astype(_WEIGHT_DTYPES[k]) for k, w in weights.items()}}
        y = _forward(args)
    with _jax.named_scope("loss_head"):
        err = _jnp.square(y.astype(_jnp.float32) - loss_target)
        return 0.5 * _jnp.sum(_jnp.mean(err, axis=-1)) if err.ndim else 0.5 * err


def _adamw(w, g, m, v):
    m = ADAM_B1 * m + (1.0 - ADAM_B1) * g
    v = ADAM_B2 * v + (1.0 - ADAM_B2) * _jnp.square(g)
    m_hat = m / (1.0 - ADAM_B1 ** ADAM_STEP)
    v_hat = v / (1.0 - ADAM_B2 ** ADAM_STEP)
    delta = -ADAM_LR * (m_hat / (_jnp.sqrt(v_hat) + ADAM_EPS) + ADAM_WD * w)
    return delta, m, v


def reference(x, p, positions, ln_ffn1, w_ffn1_gate, w_ffn1_up, w_ffn1_down, ln_mix, w_in, b_forget, w_merge, b_merge, w_ret_out, w_fox_out, w_out, ln_ffn2, w_ffn2_gate, w_ffn2_up, w_ffn2_down, ln_ple, w_ple, w_ple_gate, ln_final, loss_target, m_ln_ffn1, m_w_ffn1_gate, m_w_ffn1_up, m_w_ffn1_down, m_ln_mix, m_w_in, m_b_forget, m_w_merge, m_b_merge, m_w_ret_out, m_w_fox_out, m_w_out, m_ln_ffn2, m_w_ffn2_gate, m_w_ffn2_up, m_w_ffn2_down, m_ln_ple, m_w_ple, m_w_ple_gate, m_ln_final, v_ln_ffn1, v_w_ffn1_gate, v_w_ffn1_up, v_w_ffn1_down, v_ln_mix, v_w_in, v_b_forget, v_w_merge, v_b_merge, v_w_ret_out, v_w_fox_out, v_w_out, v_ln_ffn2, v_w_ffn2_gate, v_w_ffn2_up, v_w_ffn2_down, v_ln_ple, v_w_ple, v_w_ple_gate, v_ln_final):
    given = dict(x=x, p=p, positions=positions, ln_ffn1=ln_ffn1, w_ffn1_gate=w_ffn1_gate, w_ffn1_up=w_ffn1_up, w_ffn1_down=w_ffn1_down, ln_mix=ln_mix, w_in=w_in, b_forget=b_forget, w_merge=w_merge, b_merge=b_merge, w_ret_out=w_ret_out, w_fox_out=w_fox_out, w_out=w_out, ln_ffn2=ln_ffn2, w_ffn2_gate=w_ffn2_gate, w_ffn2_up=w_ffn2_up, w_ffn2_down=w_ffn2_down, ln_ple=ln_ple, w_ple=w_ple, w_ple_gate=w_ple_gate, ln_final=ln_final, loss_target=loss_target, m_ln_ffn1=m_ln_ffn1, m_w_ffn1_gate=m_w_ffn1_gate, m_w_ffn1_up=m_w_ffn1_up, m_w_ffn1_down=m_w_ffn1_down, m_ln_mix=m_ln_mix, m_w_in=m_w_in, m_b_forget=m_b_forget, m_w_merge=m_w_merge, m_b_merge=m_b_merge, m_w_ret_out=m_w_ret_out, m_w_fox_out=m_w_fox_out, m_w_out=m_w_out, m_ln_ffn2=m_ln_ffn2, m_w_ffn2_gate=m_w_ffn2_gate, m_w_ffn2_up=m_w_ffn2_up, m_w_ffn2_down=m_w_ffn2_down, m_ln_ple=m_ln_ple, m_w_ple=m_w_ple, m_w_ple_gate=m_w_ple_gate, m_ln_final=m_ln_final, v_ln_ffn1=v_ln_ffn1, v_w_ffn1_gate=v_w_ffn1_gate, v_w_ffn1_up=v_w_ffn1_up, v_w_ffn1_down=v_w_ffn1_down, v_ln_mix=v_ln_mix, v_w_in=v_w_in, v_b_forget=v_b_forget, v_w_merge=v_w_merge, v_b_merge=v_b_merge, v_w_ret_out=v_w_ret_out, v_w_fox_out=v_w_fox_out, v_w_out=v_w_out, v_ln_ffn2=v_ln_ffn2, v_w_ffn2_gate=v_w_ffn2_gate, v_w_ffn2_up=v_w_ffn2_up, v_w_ffn2_down=v_w_ffn2_down, v_ln_ple=v_ln_ple, v_w_ple=v_w_ple, v_w_ple_gate=v_w_ple_gate, v_ln_final=v_ln_final)
    weights = {n: given[n] for n in TWIN_WEIGHTS}
    shared = {n: given[n] for n in SHARED_INPUTS}
    per_example = {n: given[n] for n in ['x', 'p', 'positions']}
    grad_fn = _jax.value_and_grad(_loss, argnums=(0, 1))

    def one_microbatch(ex, loss_target):
        ex = dict(ex)
        diff = ex.pop(TWIN_DIFF_INPUT)
        return grad_fn(weights, diff, {**shared, **ex}, loss_target)

    if N_MICROBATCH == 1:
        loss, (grad_w, grad_x) = one_microbatch(per_example, given["loss_target"])
    else:
        def body(carry, xs):
            loss_sum, grad_sum = carry
            l_k, (gw_k, gx_k) = one_microbatch(xs[0], xs[1])
            with _jax.named_scope("update"):
                return (loss_sum + l_k, _jax.tree.map(_jnp.add, grad_sum, gw_k)), gx_k

        init = (_jnp.zeros((), _jnp.float32), _jax.tree.map(_jnp.zeros_like, weights))
        (loss, grad_w), grad_x = _jax.lax.scan(body, init, (per_example, given["loss_target"]))
    with _jax.named_scope("update"):
        delta_w, new_m, new_v = {}, {}, {}
        for n in TWIN_WEIGHTS:
            delta_w[n], new_m[n], new_v[n] = _adamw(weights[n], grad_w[n], given["m_" + n], given["v_" + n])
    return (loss, grad_x, *[grad_w[n] for n in TWIN_WEIGHTS], *[delta_w[n] for n in TWIN_WEIGHTS],
            *[new_m[n] for n in TWIN_WEIGHTS], *[new_v[n] for n in TWIN_WEIGHTS])
```

```python
import functools

import jax
import jax.numpy as jnp
from jax import lax
from jax.experimental import pallas as pl
from jax.experimental.pallas import tpu as pltpu

F32 = jnp.float32
BF = jnp.bfloat16
MESH = pl.DeviceIdType.MESH

EPS = 1e-6
ROPE_BASE = 10000.0
N_CHIPS = 4
RET_HEADS = 4
RET_DIM = 128
RET_WIDTH = RET_HEADS * RET_DIM
RET_CHUNK = 128
RET_SCALE = RET_DIM ** -0.5
FOX_HEADS = 8
FOX_DIM = 64
FOX_WIDTH = FOX_HEADS * FOX_DIM
FOX_SCALE = FOX_DIM ** -0.5
IN_COLS = 4 * RET_WIDTH + 3 * FOX_WIDTH + FOX_HEADS
IN_PAD = 4096
FF_COL = 4 * RET_WIDTH + 3 * FOX_WIDTH
NEG = -1e30

ADAM_LR = 0.001
ADAM_B1 = 0.9
ADAM_B2 = 0.999
ADAM_EPS = 1e-08
ADAM_WD = 0.01
ADAM_STEP = 10

VMEM_LIMIT = 52 * 1024 * 1024

NT = (((1,), (1,)), ((), ()))
TN = (((0,), (0,)), ((), ()))


def _dot(a, b):
    return jnp.dot(a, b, preferred_element_type=F32)


def _dot_nt(a, b):
    return lax.dot_general(a, b, NT, preferred_element_type=F32)


def _dot_tn(a, b):
    return lax.dot_general(a, b, TN, preferred_element_type=F32)


def _cp(*sem):
    return pltpu.CompilerParams(dimension_semantics=sem, vmem_limit_bytes=VMEM_LIMIT)


def _rstd(xv):
    return lax.rsqrt(jnp.mean(xv * xv, axis=-1, keepdims=True) + EPS)


def _rms_bwd(dn, xv, r, ln):
    xh = xv * r
    dxh = dn * ln
    dx = r * (dxh - xh * jnp.mean(dxh * xh, axis=-1, keepdims=True))
    return dx, jnp.sum(dn * xh, axis=0, keepdims=True)


def _sigmoid(x):
    return jax.nn.sigmoid(x)


def _tile(n, pref):
    return pref if n % pref == 0 else n


HBM_SPEC = pl.BlockSpec(memory_space=pltpu.HBM)


def _peer_chips(x, y):
    return [(1 - x, y), (x, 1 - y), (1 - x, 1 - y)]


def _all_gather_chips(shards):
    n = len(shards)

    def body(*refs):
        ins, outs = refs[:n], refs[n:2 * n]
        send_sems, recv_sems, local_sems = refs[2 * n:]
        x, y, c = lax.axis_index("x"), lax.axis_index("y"), lax.axis_index("c")
        me = 2 * x + y
        peers = _peer_chips(x, y)
        local = [pltpu.make_async_copy(ins[g], outs[g].at[me], local_sems.at[g]) for g in range(n)]
        for cp in local:
            cp.start()

        def remote(g, j, slot):
            px, py = peers[j]
            return pltpu.make_async_remote_copy(
                src_ref=ins[g], dst_ref=outs[g].at[slot], send_sem=send_sems.at[g, j],
                recv_sem=recv_sems.at[g, j], device_id=(px, py, c), device_id_type=MESH)

        sends = [remote(g, j, me) for g in range(n) for j in range(3)]
        for cp in sends:
            cp.start()
        for g in range(n):
            for j in range(3):
                remote(g, j, 2 * peers[j][0] + peers[j][1]).wait_recv()
        for cp in sends:
            cp.wait_send()
        for cp in local:
            cp.wait()

    return pl.pallas_call(
        body, name="all_gather_chips",
        out_shape=[jax.ShapeDtypeStruct((N_CHIPS,) + s.shape, s.dtype) for s in shards],
        in_specs=[HBM_SPEC] * n, out_specs=[HBM_SPEC] * n,
        scratch_shapes=[pltpu.SemaphoreType.DMA((n, 3)), pltpu.SemaphoreType.DMA((n, 3)),
                        pltpu.SemaphoreType.DMA((n,))],
    )(*shards)


def _sibling_half_swap(grads):
    n = len(grads)

    def body(*refs):
        ins, outs = refs[:n], refs[n:2 * n]
        send_sems, recv_sems = refs[2 * n:]
        x, y, c = lax.axis_index("x"), lax.axis_index("y"), lax.axis_index("c")
        copies = []
        for g in range(n):
            half = ins[g].shape[1] // 2
            src = ins[g].at[:, pl.ds((1 - c) * half, half), :]
            copies.append(pltpu.make_async_remote_copy(
                src_ref=src, dst_ref=outs[g], send_sem=send_sems.at[g], recv_sem=recv_sems.at[g],
                device_id=(x, y, 1 - c), device_id_type=MESH))
        for cp in copies:
            cp.start()
        for cp in copies:
            cp.wait_recv()
        for cp in copies:
            cp.wait_send()

    return pl.pallas_call(
        body, name="sibling_half_swap",
        out_shape=[jax.ShapeDtypeStruct((N_CHIPS, s.shape[1] // 2, s.shape[2]), s.dtype) for s in grads],
        in_specs=[HBM_SPEC] * n, out_specs=[HBM_SPEC] * n,
        scratch_shapes=[pltpu.SemaphoreType.DMA((n,)), pltpu.SemaphoreType.DMA((n,))],
    )(*grads)


def _chip_exchange(parts):
    n = len(parts)

    def body(*refs):
        ins, outs = refs[:n], refs[n:2 * n]
        send_sems, recv_sems, local_sems = refs[2 * n:]
        x, y, c = lax.axis_index("x"), lax.axis_index("y"), lax.axis_index("c")
        me = 2 * x + y
        peers = _peer_chips(x, y)
        local = [pltpu.make_async_copy(ins[g].at[me], outs[g].at[me], local_sems.at[g]) for g in range(n)]
        for cp in local:
            cp.start()

        def remote(g, j, src_slot, dst_slot):
            px, py = peers[j]
            return pltpu.make_async_remote_copy(
                src_ref=ins[g].at[src_slot], dst_ref=outs[g].at[dst_slot], send_sem=send_sems.at[g, j],
                recv_sem=recv_sems.at[g, j], device_id=(px, py, c), device_id_type=MESH)

        sends = [remote(g, j, 2 * peers[j][0] + peers[j][1], me) for g in range(n) for j in range(3)]
        for cp in sends:
            cp.start()
        for g in range(n):
            for j in range(3):
                remote(g, j, me, 2 * peers[j][0] + peers[j][1]).wait_recv()
        for cp in sends:
            cp.wait_send()
        for cp in local:
            cp.wait()

    return pl.pallas_call(
        body, name="chip_exchange",
        out_shape=[jax.ShapeDtypeStruct(s.shape, s.dtype) for s in parts],
        in_specs=[HBM_SPEC] * n, out_specs=[HBM_SPEC] * n,
        scratch_shapes=[pltpu.SemaphoreType.DMA((n, 3)), pltpu.SemaphoreType.DMA((n, 3)),
                        pltpu.SemaphoreType.DMA((n,))],
    )(*parts)


def _sibling_join(halves):
    n = len(halves)

    def body(*refs):
        ins, outs = refs[:n], refs[n:2 * n]
        send_sems, recv_sems, local_sems = refs[2 * n:]
        x, y, c = lax.axis_index("x"), lax.axis_index("y"), lax.axis_index("c")
        local, sends, recvs = [], [], []
        for g in range(n):
            half = ins[g].shape[0]
            local.append(pltpu.make_async_copy(ins[g], outs[g].at[pl.ds(c * half, half), :], local_sems.at[g]))
            sends.append(pltpu.make_async_remote_copy(
                src_ref=ins[g], dst_ref=outs[g].at[pl.ds(c * half, half), :], send_sem=send_sems.at[g],
                recv_sem=recv_sems.at[g], device_id=(x, y, 1 - c), device_id_type=MESH))
            recvs.append(pltpu.make_async_remote_copy(
                src_ref=ins[g], dst_ref=outs[g].at[pl.ds((1 - c) * half, half), :], send_sem=send_sems.at[g],
                recv_sem=recv_sems.at[g], device_id=(x, y, 1 - c), device_id_type=MESH))
        for cp in local + sends:
            cp.start()
        for cp in recvs:
            cp.wait_recv()
        for cp in sends:
            cp.wait_send()
        for cp in local:
            cp.wait()

    return pl.pallas_call(
        body, name="sibling_join",
        out_shape=[jax.ShapeDtypeStruct((2 * s.shape[0], s.shape[1]), s.dtype) for s in halves],
        in_specs=[HBM_SPEC] * n, out_specs=[HBM_SPEC] * n,
        scratch_shapes=[pltpu.SemaphoreType.DMA((n,)), pltpu.SemaphoreType.DMA((n,)),
                        pltpu.SemaphoreType.DMA((n,))],
    )(*halves)


def _all_reduce_small(v):
    rows = v.shape[0]

    def body(v_ref, out_ref, buf, send_sems, recv_sems):
        x, y, c = lax.axis_index("x"), lax.axis_index("y"), lax.axis_index("c")
        me = 4 * x + 2 * y + c
        buf[me] = v_ref[...]
        flips = [(fx, fy, fc) for fx in (0, 1) for fy in (0, 1) for fc in (0, 1)][1:]

        def peer(k):
            fx, fy, fc = flips[k]
            px, py, pc = x ^ fx, y ^ fy, c ^ fc
            return (px, py, pc), 4 * px + 2 * py + pc

        def copy(k, slot):
            return pltpu.make_async_remote_copy(
                src_ref=buf.at[slot], dst_ref=buf.at[slot], send_sem=send_sems.at[k],
                recv_sem=recv_sems.at[k], device_id=peer(k)[0], device_id_type=MESH)

        sends = [copy(k, me) for k in range(7)]
        for cp in sends:
            cp.start()
        for k in range(7):
            copy(k, peer(k)[1]).wait_recv()
        for cp in sends:
            cp.wait_send()
        acc = buf[0]
        for d in range(1, 8):
            acc = acc + buf[d]
        out_ref[...] = acc

    return pl.pallas_call(
        body, name="all_reduce_small",
        out_shape=jax.ShapeDtypeStruct((rows, 128), F32),
        in_specs=[pl.BlockSpec(memory_space=pltpu.VMEM)],
        out_specs=pl.BlockSpec(memory_space=pltpu.VMEM),
        scratch_shapes=[pltpu.VMEM((8, rows, 128), F32), pltpu.SemaphoreType.DMA((7,)),
                        pltpu.SemaphoreType.DMA((7,))],
    )(v)


def _add_halves(mine, got):
    _, h, c = mine.shape
    th = _tile(h, 128)

    def body(a_ref, b_ref, o_ref):
        o_ref[...] = (a_ref[...].astype(F32) + b_ref[...].astype(F32)).astype(o_ref.dtype)

    spec = pl.BlockSpec((1, th, c), lambda j, i: (j, i, 0))
    return pl.pallas_call(
        body, name="add_halves", grid=(N_CHIPS, h // th),
        out_shape=jax.ShapeDtypeStruct(mine.shape, BF),
        in_specs=[spec, spec], out_specs=spec, compiler_params=_cp("parallel", "parallel"),
    )(mine, got)


def _sum_chips(parts):
    _, h, c = parts.shape
    th = _tile(h, 128)

    def body(p_ref, o_ref):
        acc = p_ref[0].astype(F32)
        for s in range(1, N_CHIPS):
            acc = acc + p_ref[s].astype(F32)
        o_ref[...] = acc

    return pl.pallas_call(
        body, name="sum_chips", grid=(h // th,),
        out_shape=jax.ShapeDtypeStruct((h, c), F32),
        in_specs=[pl.BlockSpec((N_CHIPS, th, c), lambda i: (0, i, 0))],
        out_specs=pl.BlockSpec((th, c), lambda i: (i, 0)), compiler_params=_cp("parallel"),
    )(parts)


def _adamw(w, g, m, v):
    r, c = w.shape
    tr = _tile(r, 256)
    c1 = 1.0 / (1.0 - ADAM_B1 ** ADAM_STEP)
    c2 = 1.0 / (1.0 - ADAM_B2 ** ADAM_STEP)

    def body(w_ref, g_ref, m_ref, v_ref, d_ref, nm_ref, nv_ref):
        gv = g_ref[...]
        nm = ADAM_B1 * m_ref[...] + (1.0 - ADAM_B1) * gv
        nv = ADAM_B2 * v_ref[...] + (1.0 - ADAM_B2) * (gv * gv)
        nm_ref[...] = nm
        nv_ref[...] = nv
        d_ref[...] = -ADAM_LR * ((nm * c1) / (jnp.sqrt(nv * c2) + ADAM_EPS) + ADAM_WD * w_ref[...])

    spec = pl.BlockSpec((tr, c), lambda i: (i, 0))
    sds = jax.ShapeDtypeStruct((r, c), F32)
    return pl.pallas_call(
        body, name="adamw", grid=(r // tr,), out_shape=[sds, sds, sds],
        in_specs=[spec] * 4, out_specs=[spec] * 3, compiler_params=_cp("parallel"),
    )(w, g, m, v)


def _wgrad(name, a, b, a_spec, b_spec, out_shape, out_spec, n_out, n_tok):
    def body(a_ref, b_ref, o_ref, acc):
        t = pl.program_id(1)

        @pl.when(t == 0)
        def _():
            acc[...] = jnp.zeros_like(acc)

        acc[...] += _dot_tn(a_ref[...], b_ref[...])

        @pl.when(t == n_tok - 1)
        def _():
            o_ref[...] = acc[...].astype(o_ref.dtype)

    m = [d for d in a_spec.block_shape if d is not None][-1]
    nn = [d for d in b_spec.block_shape if d is not None][-1]
    return pl.pallas_call(
        body, name=name, grid=(n_out, n_tok), out_shape=out_shape,
        in_specs=[a_spec, b_spec], out_specs=out_spec,
        scratch_shapes=[pltpu.VMEM((m, nn), F32)], compiler_params=_cp("parallel", "arbitrary"),
    )(a, b)


def _wgrad_cols(name, a, b, nb):
    t_tok, m = a.shape
    n = b.shape[1] // nb
    tk = _tile(t_tok, 512)
    return _wgrad(
        name, a, b,
        pl.BlockSpec((tk, m), lambda j, t: (t, 0)), pl.BlockSpec((tk, n), lambda j, t: (t, j)),
        jax.ShapeDtypeStruct((nb, m, n), BF), pl.BlockSpec((None, m, n), lambda j, t: (j, 0, 0)),
        nb, t_tok // tk)


def _wgrad_rows(name, a, b, nb):
    t_tok, n = b.shape
    m = a.shape[1] // nb
    tk = _tile(t_tok, 512)
    return _wgrad(
        name, a, b,
        pl.BlockSpec((tk, m), lambda j, t: (t, j)), pl.BlockSpec((tk, n), lambda j, t: (t, 0)),
        jax.ShapeDtypeStruct((nb, m, n), BF), pl.BlockSpec((None, m, n), lambda j, t: (j, 0, 0)),
        nb, t_tok // tk)


def _wgrad_a_shared(name, a, b4):
    t_tok, m = a.shape
    nb, _, n = b4.shape
    tk = _tile(t_tok, 512)
    return _wgrad(
        name, a, b4,
        pl.BlockSpec((tk, m), lambda j, t: (t, 0)), pl.BlockSpec((None, tk, n), lambda j, t: (j, t, 0)),
        jax.ShapeDtypeStruct((nb, m, n), BF), pl.BlockSpec((None, m, n), lambda j, t: (j, 0, 0)),
        nb, t_tok // tk)


def _wgrad_b_shared(name, a4, b):
    nb, t_tok, m = a4.shape
    n = b.shape[1]
    tk = _tile(t_tok, 512)
    return _wgrad(
        name, a4, b,
        pl.BlockSpec((None, tk, m), lambda j, t: (j, t, 0)), pl.BlockSpec((tk, n), lambda j, t: (t, 0)),
        jax.ShapeDtypeStruct((nb, m, n), BF), pl.BlockSpec((None, m, n), lambda j, t: (j, 0, 0)),
        nb, t_tok // tk)


def _w4_spec(k, r, c):
    return pl.BlockSpec((None, None, r, c), lambda i, j: (j, k, 0, 0))


def _ffn_fwd(h, ln, w_gu, kg, ku, w_d, kd):
    t_tok, d = h.shape
    f = w_gu.shape[-1]
    tm = _tile(t_tok, 512)

    def body(h_ref, ln_ref, wg_ref, wu_ref, wd_ref, ho_ref, n_ref, g_ref, u_ref, n_s, acc):
        j = pl.program_id(1)

        @pl.when(j == 0)
        def _():
            xv = h_ref[...]
            nv = (xv * _rstd(xv) * ln_ref[...]).astype(BF)
            n_s[...] = nv
            n_ref[...] = nv
            acc[...] = jnp.zeros_like(acc)

        nv = n_s[...]
        g = _dot(nv, wg_ref[...])
        u = _dot(nv, wu_ref[...])
        g_ref[...] = g.astype(BF)
        u_ref[...] = u.astype(BF)
        a = (g * _sigmoid(g) * u).astype(BF)
        acc[...] += _dot(a, wd_ref[...])

        @pl.when(j == N_CHIPS - 1)
        def _():
            ho_ref[...] = h_ref[...] + 0.5 * acc[...]

    row = pl.BlockSpec((tm, d), lambda i, j: (i, 0))
    gu = pl.BlockSpec((None, tm, f), lambda i, j: (j, i, 0))
    return pl.pallas_call(
        body, name="ffn_fwd", grid=(t_tok // tm, N_CHIPS),
        out_shape=[jax.ShapeDtypeStruct((t_tok, d), F32), jax.ShapeDtypeStruct((t_tok, d), BF),
                   jax.ShapeDtypeStruct((N_CHIPS, t_tok, f), BF), jax.ShapeDtypeStruct((N_CHIPS, t_tok, f), BF)],
        in_specs=[row, pl.BlockSpec((1, d), lambda i, j: (0, 0)),
                  _w4_spec(kg, d, f), _w4_spec(ku, d, f), _w4_spec(kd, f, d)],
        out_specs=[row, row, gu, gu],
        scratch_shapes=[pltpu.VMEM((tm, d), BF), pltpu.VMEM((tm, d), F32)],
        compiler_params=_cp("parallel", "arbitrary"),
    )(h, ln, w_gu, w_gu, w_d)


def _ffn_bwd(dho, h, ln, g4, u4, w_gu, kg, ku, w_d, kd):
    t_tok, d = h.shape
    f = w_gu.shape[-1]
    tm = _tile(t_tok, 512)

    def body(dho_ref, h_ref, ln_ref, g_ref, u_ref, wg_ref, wu_ref, wd_ref,
             dhi_ref, dln_ref, dg_ref, du_ref, a_ref, dhb_ref, dhb_s, dn_acc):
        i, j = pl.program_id(0), pl.program_id(1)

        @pl.when(j == 0)
        def _():
            dhb = (0.5 * dho_ref[...]).astype(BF)
            dhb_s[...] = dhb
            dhb_ref[...] = dhb
            dn_acc[...] = jnp.zeros_like(dn_acc)

        @pl.when((i == 0) & (j == 0))
        def _():
            dln_ref[...] = jnp.zeros_like(dln_ref)

        g = g_ref[...].astype(F32)
        u = u_ref[...].astype(F32)
        s = _sigmoid(g)
        sg = g * s
        a_ref[...] = (sg * u).astype(BF)
        da = _dot_nt(dhb_s[...], wd_ref[...])
        dg = (da * u * (s * (1.0 + g * (1.0 - s)))).astype(BF)
        du = (da * sg).astype(BF)
        dg_ref[...] = dg
        du_ref[...] = du
        dn_acc[...] += _dot_nt(dg, wg_ref[...]) + _dot_nt(du, wu_ref[...])

        @pl.when(j == N_CHIPS - 1)
        def _():
            xv = h_ref[...]
            dx, dln = _rms_bwd(dn_acc[...], xv, _rstd(xv), ln_ref[...])
            dln_ref[...] += dln
            dhi_ref[...] = dho_ref[...] + dx

    row = pl.BlockSpec((tm, d), lambda i, j: (i, 0))
    vec = pl.BlockSpec((1, d), lambda i, j: (0, 0))
    gu = pl.BlockSpec((None, tm, f), lambda i, j: (j, i, 0))
    gu_sds = jax.ShapeDtypeStruct((N_CHIPS, t_tok, f), BF)
    return pl.pallas_call(
        body, name="ffn_bwd", grid=(t_tok // tm, N_CHIPS),
        out_shape=[jax.ShapeDtypeStruct((t_tok, d), F32), jax.ShapeDtypeStruct((1, d), F32),
                   gu_sds, gu_sds, gu_sds, jax.ShapeDtypeStruct((t_tok, d), BF)],
        in_specs=[row, row, vec, gu, gu, _w4_spec(kg, d, f), _w4_spec(ku, d, f), _w4_spec(kd, f, d)],
        out_specs=[row, vec, gu, gu, gu, row],
        scratch_shapes=[pltpu.VMEM((tm, d), BF), pltpu.VMEM((tm, d), F32)],
        compiler_params=_cp("arbitrary", "arbitrary"),
    )(dho, h, ln, g4, u4, w_gu, w_gu, w_d)


def _rope_tables(pos_col, inv_freq2):
    t_tok = pos_col.shape[0]

    def body(p_ref, f_ref, cos_ref, sin_ref):
        ang = p_ref[...] * f_ref[...]
        lane = lax.broadcasted_iota(jnp.int32, ang.shape, 1)
        s = jnp.sin(ang)
        cos_ref[...] = jnp.cos(ang)
        sin_ref[...] = jnp.where((lane & 1) == 0, -s, s)

    sds = jax.ShapeDtypeStruct((t_tok, 128), F32)
    return pl.pallas_call(
        body, name="rope_tables", out_shape=[sds, sds],
        compiler_params=pltpu.CompilerParams(vmem_limit_bytes=VMEM_LIMIT),
    )(pos_col, inv_freq2)


def _swap_pairs(x):
    lane = lax.broadcasted_iota(jnp.int32, x.shape, 1)
    return jnp.where((lane & 1) == 0, pltpu.roll(x, 127, 1), pltpu.roll(x, 1, 1))


def _mix_in(h, ln, w_in, w_m, km, b_m, cos_t, sin_t):
    t_tok, d = h.shape
    cm = w_m.shape[-1]
    tm = _tile(t_tok, 256)

    def body(h_ref, ln_ref, win_ref, wm_ref, bm_ref, cos_ref, sin_ref,
             u_ref, rq_ref, rk_ref, rv_ref, rg_ref, fq_ref, fk_ref, fv_ref, ff_ref, ga_ref, gb_ref):
        xv = h_ref[...]
        ub = (xv * _rstd(xv) * ln_ref[...]).astype(BF)
        u_ref[...] = ub
        cosv, sinv = cos_ref[...], sin_ref[...]

        def sec(k):
            return _dot(ub, win_ref[:, k * 512:(k + 1) * 512])

        def rot(xh):
            return xh * cosv + _swap_pairs(xh) * sinv

        pq, pk = sec(0), sec(1)
        for hh in range(RET_HEADS):
            sl = slice(hh * RET_DIM, (hh + 1) * RET_DIM)
            rq_ref[:, sl] = rot(pq[:, sl]).astype(BF)
            rk_ref[:, sl] = (rot(pk[:, sl]) * RET_SCALE).astype(BF)
        rv_ref[...] = sec(2).astype(BF)
        rg_ref[...] = sec(3).astype(BF)
        fq_ref[...] = (sec(4) * FOX_SCALE).astype(BF)
        fk_ref[...] = sec(5).astype(BF)
        fv_ref[...] = sec(6).astype(BF)
        ff_ref[...] = _dot(ub, win_ref[:, FF_COL:FF_COL + 128])
        for j in range(N_CHIPS):
            gs = _sigmoid(_dot(ub, wm_ref[j]) + bm_ref[:, j * cm:(j + 1) * cm]).astype(BF)
            col = j * cm
            if col < d:
                ga_ref[:, col:col + cm] = gs
            else:
                gb_ref[:, col - d:col - d + cm] = gs

    row = lambda c: pl.BlockSpec((tm, c), lambda i: (i, 0))
    full = lambda *s: pl.BlockSpec(s, lambda i: (0,) * len(s))
    sds = lambda c, dt: jax.ShapeDtypeStruct((t_tok, c), dt)
    return pl.pallas_call(
        body, name="mix_in", grid=(t_tok // tm,),
        out_shape=[sds(d, BF)] + [sds(512, BF)] * 7 + [sds(128, F32), sds(d, BF), sds(d, BF)],
        in_specs=[row(d), full(1, d), full(d, IN_PAD),
                  pl.BlockSpec((N_CHIPS, None, d, cm), lambda i: (0, km, 0, 0)),
                  full(1, 2 * d), row(128), row(128)],
        out_specs=[row(d)] + [row(512)] * 7 + [row(128), row(d), row(d)],
        compiler_params=_cp("parallel"),
    )(h, ln, w_in, w_m, b_m, cos_t, sin_t)


def _split3(x):
    hi = x.astype(BF)
    r1 = x - hi.astype(F32)
    mid = r1.astype(BF)
    lo = (r1 - mid.astype(F32)).astype(BF)
    return hi, mid, lo


def _forget_fwd(ffl, b_pad):
    t_tok = ffl.shape[0]
    tb = _tile(t_tok, 256)

    def body(ff_ref, b_ref, cume_ref, cumt_ref, cum_s):
        r = lax.broadcasted_iota(jnp.int32, (tb, tb), 0)
        c = lax.broadcasted_iota(jnp.int32, (tb, tb), 1)
        tri = jnp.where(c <= r, 1.0, 0.0).astype(BF)
        carry = jnp.zeros((1, 128), F32)
        for i in range(t_tok // tb):
            z = ff_ref[i * tb:(i + 1) * tb, :] + b_ref[...]
            lf = jnp.minimum(z, 0.0) - jnp.log(1.0 + jnp.exp(-jnp.abs(z)))
            hi, mid, lo = _split3(lf)
            cs = _dot(tri, hi) + _dot(tri, mid) + _dot(tri, lo) + carry
            cum_s[i * tb:(i + 1) * tb, :] = cs
            carry = cs[tb - 1:tb, :]
        x = cum_s[...]
        hid = lax.broadcasted_iota(jnp.int32, (t_tok, FOX_WIDTH), 1) // FOX_DIM
        e = jnp.zeros((t_tok, FOX_WIDTH), F32)
        for hh in range(FOX_HEADS):
            e = jnp.where(hid == hh, x[:, hh:hh + 1], e)
        cume_ref[...] = e
        cumt_ref[...] = x.T[0:FOX_HEADS, :]

    return pl.pallas_call(
        body, name="forget_fwd",
        out_shape=[jax.ShapeDtypeStruct((t_tok, FOX_WIDTH), F32), jax.ShapeDtypeStruct((FOX_HEADS, t_tok), F32)],
        scratch_shapes=[pltpu.VMEM((t_tok, 128), F32)],
        compiler_params=pltpu.CompilerParams(vmem_limit_bytes=VMEM_LIMIT),
    )(ffl, b_pad)


def _forget_bwd(dcum_t, dcum_q, ffl, b_pad):
    t_tok = ffl.shape[0]
    tb = _tile(t_tok, 256)

    def body(dc_ref, dq_ref, ff_ref, b_ref, dff_ref, db_ref, pad_s, d_s):
        pad_s[...] = jnp.zeros_like(pad_s)
        pad_s[0:FOX_HEADS, :] = dc_ref[...]
        dsum = pad_s[...].T
        lane = lax.broadcasted_iota(jnp.int32, (t_tok, 128), 1)
        for hh in range(FOX_HEADS):
            dsum = dsum + jnp.where(lane == hh, dq_ref[:, hh * FOX_DIM:hh * FOX_DIM + 1], 0.0)
        d_s[...] = dsum
        r = lax.broadcasted_iota(jnp.int32, (tb, tb), 0)
        c = lax.broadcasted_iota(jnp.int32, (tb, tb), 1)
        tri = jnp.where(c >= r, 1.0, 0.0).astype(BF)
        carry = jnp.zeros((1, 128), F32)
        db = jnp.zeros((1, 128), F32)
        for i in reversed(range(t_tok // tb)):
            hi, mid, lo = _split3(d_s[i * tb:(i + 1) * tb, :])
            dlf = _dot(tri, hi) + _dot(tri, mid) + _dot(tri, lo) + carry
            carry = dlf[0:1, :]
            z = ff_ref[i * tb:(i + 1) * tb, :] + b_ref[...]
            dff = dlf * _sigmoid(-z)
            dff_ref[i * tb:(i + 1) * tb, :] = dff.astype(BF)
            db = db + jnp.sum(dff, axis=0, keepdims=True)
        db_ref[...] = db

    return pl.pallas_call(
        body, name="forget_bwd",
        out_shape=[jax.ShapeDtypeStruct((t_tok, 128), BF), jax.ShapeDtypeStruct((1, 128), F32)],
        scratch_shapes=[pltpu.VMEM((128, t_tok), F32), pltpu.VMEM((t_tok, 128), F32)],
        compiler_params=pltpu.CompilerParams(vmem_limit_bytes=VMEM_LIMIT),
    )(dcum_t, dcum_q, ffl, b_pad)


def _head_masks():
    lane = lax.broadcasted_iota(jnp.int32, (1, 128), 1)
    m0 = jnp.where(lane < FOX_DIM, 1.0, 0.0)
    return lane < FOX_DIM, [m0.astype(BF), (1.0 - m0).astype(BF)]


def _causal(s, qi, ki, t):
    rows = qi * t + lax.broadcasted_iota(jnp.int32, s.shape, 0)
    cols = ki * t + lax.broadcasted_iota(jnp.int32, s.shape, 1)
    return jnp.where(cols <= rows, s, NEG)


def _fox_fwd(fq, fk, fv, cum_e, cum_t3):
    t_tok = fq.shape[0]
    t = _tile(t_tok, 512)
    nq = t_tok // t
    npair = FOX_HEADS // 2

    def body(q_ref, k_ref, v_ref, cq_ref, ck_ref, o_ref, of_ref, lse_ref, m_s, l_s, acc_s):
        qi, ki = pl.program_id(1), pl.program_id(2)

        @pl.when(ki == 0)
        def _():
            m_s[...] = jnp.full_like(m_s, NEG)
            l_s[...] = jnp.zeros_like(l_s)
            acc_s[...] = jnp.zeros_like(acc_s)

        @pl.when(ki <= qi)
        def _():
            _, masks = _head_masks()
            q2, k2, v2 = q_ref[...], k_ref[...], v_ref[...]
            for hh in range(2):
                s = _dot_nt(q2 * masks[hh], k2)
                s = s + cq_ref[:, hh * FOX_DIM:hh * FOX_DIM + 1] - ck_ref[hh]
                s = _causal(s, qi, ki, t)
                m_prev = m_s[hh]
                m_new = jnp.maximum(m_prev, jnp.max(s, axis=1, keepdims=True))
                alpha = jnp.exp(m_prev - m_new)
                p = jnp.exp(s - m_new)
                l_s[hh] = alpha * l_s[hh] + jnp.sum(p, axis=1, keepdims=True)
                acc_s[hh] = alpha * acc_s[hh] + _dot(p.astype(BF), v2)
                m_s[hh] = m_new

        @pl.when(ki == nq - 1)
        def _():
            first, _ = _head_masks()
            o = jnp.where(first, acc_s[0] / l_s[0], acc_s[1] / l_s[1])
            o_ref[...] = o.astype(BF)
            of_ref[...] = o
            lse_ref[...] = jnp.where(first, m_s[0] + jnp.log(l_s[0]), m_s[1] + jnp.log(l_s[1]))

    qs = pl.BlockSpec((t, 128), lambda p, qi, ki: (qi, p))
    ks = pl.BlockSpec((t, 128), lambda p, qi, ki: (jnp.minimum(ki, qi), p))
    cks = pl.BlockSpec((2, 1, t), lambda p, qi, ki: (p, 0, jnp.minimum(ki, qi)))
    return pl.pallas_call(
        body, name="fox_fwd", grid=(npair, nq, nq),
        out_shape=[jax.ShapeDtypeStruct((t_tok, FOX_WIDTH), BF), jax.ShapeDtypeStruct((t_tok, FOX_WIDTH), F32),
                   jax.ShapeDtypeStruct((t_tok, FOX_WIDTH), F32)],
        in_specs=[qs, ks, ks, qs, cks], out_specs=[qs, qs, qs],
        scratch_shapes=[pltpu.VMEM((2, t, 1), F32), pltpu.VMEM((2, t, 1), F32), pltpu.VMEM((2, t, 128), F32)],
        compiler_params=_cp("parallel", "parallel", "arbitrary"),
    )(fq, fk, fv, cum_e, cum_t3)


def _fox_scores(q2m, k2, cq, ck, lse, qi, ki, t):
    s = _dot_nt(q2m, k2) + cq - ck
    return jnp.exp(_causal(s, qi, ki, t) - lse)


def _fox_bwd_kv(fq, fk, fv, do, cum_e, cum_t3, lse_e, delta_e):
    t_tok = fq.shape[0]
    t = _tile(t_tok, 512)
    nq = t_tok // t
    npair = FOX_HEADS // 2

    def body(q_ref, k_ref, v_ref, do_ref, cq_ref, ck_ref, lse_ref, dl_ref, dk_ref, dv_ref, dck_ref, dk_s, dv_s):
        ki, qi = pl.program_id(1), pl.program_id(2)

        @pl.when(qi == 0)
        def _():
            dk_s[...] = jnp.zeros_like(dk_s)
            dv_s[...] = jnp.zeros_like(dv_s)
            dck_ref[...] = jnp.zeros_like(dck_ref)

        @pl.when(qi >= ki)
        def _():
            _, masks = _head_masks()
            q2, k2, v2, do2 = q_ref[...], k_ref[...], v_ref[...], do_ref[...]
            for hh in range(2):
                c0 = hh * FOX_DIM
                qm, dom = q2 * masks[hh], do2 * masks[hh]
                p = _fox_scores(qm, k2, cq_ref[:, c0:c0 + 1], ck_ref[hh], lse_ref[:, c0:c0 + 1], qi, ki, t)
                dp = _dot_nt(dom, v2)
                ds = p * (dp - dl_ref[:, c0:c0 + 1])
                dv_s[...] += _dot_tn(p.astype(BF), dom)
                dk_s[...] += _dot_tn(ds.astype(BF), qm)
                dck_ref[hh] = dck_ref[hh] - jnp.sum(ds, axis=0, keepdims=True)

        @pl.when(qi == nq - 1)
        def _():
            dk_ref[...] = dk_s[...].astype(BF)
            dv_ref[...] = dv_s[...].astype(BF)

    qs = pl.BlockSpec((t, 128), lambda p, ki, qi: (jnp.maximum(qi, ki), p))
    ks = pl.BlockSpec((t, 128), lambda p, ki, qi: (ki, p))
    cks = pl.BlockSpec((2, 1, t), lambda p, ki, qi: (p, 0, ki))
    sds = jax.ShapeDtypeStruct((t_tok, FOX_WIDTH), BF)
    return pl.pallas_call(
        body, name="fox_bwd_kv", grid=(npair, nq, nq),
        out_shape=[sds, sds, jax.ShapeDtypeStruct((FOX_HEADS, 1, t_tok), F32)],
        in_specs=[qs, ks, ks, qs, qs, cks, qs, qs], out_specs=[ks, ks, cks],
        scratch_shapes=[pltpu.VMEM((t, 128), F32), pltpu.VMEM((t, 128), F32)],
        compiler_params=_cp("parallel", "parallel", "arbitrary"),
    )(fq, fk, fv, do, cum_e, cum_t3, lse_e, delta_e)


def _fox_bwd_q(fq, fk, fv, do, cum_e, cum_t3, lse_e, delta_e):
    t_tok = fq.shape[0]
    t = _tile(t_tok, 512)
    nq = t_tok // t
    npair = FOX_HEADS // 2

    def body(q_ref, k_ref, v_ref, do_ref, cq_ref, ck_ref, lse_ref, dl_ref, dq_ref, dcq_ref, dq_s, rs_s):
        qi, ki = pl.program_id(1), pl.program_id(2)

        @pl.when(ki == 0)
        def _():
            dq_s[...] = jnp.zeros_like(dq_s)
            rs_s[...] = jnp.zeros_like(rs_s)

        @pl.when(ki <= qi)
        def _():
            first, masks = _head_masks()
            q2, k2, v2, do2 = q_ref[...], k_ref[...], v_ref[...], do_ref[...]
            dq = []
            for hh in range(2):
                c0 = hh * FOX_DIM
                p = _fox_scores(q2 * masks[hh], k2, cq_ref[:, c0:c0 + 1], ck_ref[hh],
                                lse_ref[:, c0:c0 + 1], qi, ki, t)
                dp = _dot_nt(do2 * masks[hh], v2)
                ds = p * (dp - dl_ref[:, c0:c0 + 1])
                dq.append(_dot(ds.astype(BF), k2))
                rs_s[hh] = rs_s[hh] + jnp.sum(ds, axis=1, keepdims=True)
            dq_s[...] += jnp.where(first, dq[0], dq[1])

        @pl.when(ki == nq - 1)
        def _():
            dq_ref[...] = (dq_s[...] * FOX_SCALE).astype(BF)
            first, _ = _head_masks()
            dcq_ref[...] = jnp.where(first, rs_s[0], rs_s[1])

    qs = pl.BlockSpec((t, 128), lambda p, qi, ki: (qi, p))
    ks = pl.BlockSpec((t, 128), lambda p, qi, ki: (jnp.minimum(ki, qi), p))
    cks = pl.BlockSpec((2, 1, t), lambda p, qi, ki: (p, 0, jnp.minimum(ki, qi)))
    return pl.pallas_call(
        body, name="fox_bwd_q", grid=(npair, nq, nq),
        out_shape=[jax.ShapeDtypeStruct((t_tok, FOX_WIDTH), BF), jax.ShapeDtypeStruct((t_tok, FOX_WIDTH), F32)],
        in_specs=[qs, ks, ks, qs, qs, cks, qs, qs], out_specs=[qs, qs],
        scratch_shapes=[pltpu.VMEM((t, 128), F32), pltpu.VMEM((2, t, 1), F32)],
        compiler_params=_cp("parallel", "parallel", "arbitrary"),
    )(fq, fk, fv, do, cum_e, cum_t3, lse_e, delta_e)


def _ret_consts():
    c = RET_CHUNK
    log_gamma = jnp.log1p(-jnp.exp2(-5.0 - jnp.arange(RET_HEADS, dtype=F32)))
    idx = jnp.arange(c, dtype=F32)
    diff = idx[:, None] - idx[None, :]
    dmask = jnp.where(diff >= 0, jnp.exp(log_gamma[:, None, None] * jnp.maximum(diff, 0.0)), 0.0)
    qdec = jnp.exp(log_gamma[:, None] * (idx + 1.0))
    kdec = jnp.exp(log_gamma[:, None] * (c - 1 - idx))
    cdec = jnp.exp(log_gamma * c)
    bc = lambda v: jnp.broadcast_to(v[:, :, None], (RET_HEADS, c, RET_DIM))
    return dmask, bc(qdec), bc(kdec), jnp.broadcast_to(cdec[:, None, None], (RET_HEADS, c, RET_DIM))


def _group_norm(y):
    mu = jnp.mean(y, axis=-1, keepdims=True)
    yc = y - mu
    r = lax.rsqrt(jnp.mean(yc * yc, axis=-1, keepdims=True) + EPS)
    return yc * r, r


def _ret_fwd(rq, rk, rv, rg, consts):
    t_tok = rq.shape[0]
    nb = 4 if t_tok % (4 * RET_CHUNK) == 0 else 1
    tr = nb * RET_CHUNK
    n_steps = t_tok // tr
    c = RET_CHUNK

    def body(q_ref, k_ref, v_ref, g_ref, dm_ref, qd_ref, kd_ref, cd_ref, y_ref, yo_ref, st_ref, s_s):
        @pl.when(pl.program_id(1) == 0)
        def _():
            s_s[...] = jnp.zeros_like(s_s)

        dm, qd, kd, cd = dm_ref[...], qd_ref[...], kd_ref[...], cd_ref[...]
        for b in range(nb):
            rows = slice(b * c, (b + 1) * c)
            q, k, v = q_ref[rows, :], k_ref[rows, :], v_ref[rows, :]
            state = s_s[...]
            st_ref[b] = state
            sc = (_dot_nt(q, k) * dm).astype(BF)
            y = _dot(sc, v) + _dot((q.astype(F32) * qd).astype(BF), state.astype(BF))
            s_s[...] = cd * state + _dot_tn((k.astype(F32) * kd).astype(BF), v)
            y_ref[rows, :] = y
            yn, _ = _group_norm(y)
            gate = g_ref[rows, :].astype(F32)
            yo_ref[rows, :] = (yn * (gate * _sigmoid(gate))).astype(BF)

    blk = pl.BlockSpec((tr, RET_DIM), lambda h, i: (i, h))
    cst = pl.BlockSpec((None, c, RET_DIM), lambda h, i: (h, 0, 0))
    return pl.pallas_call(
        body, name="ret_fwd", grid=(RET_HEADS, n_steps),
        out_shape=[jax.ShapeDtypeStruct((t_tok, RET_WIDTH), F32), jax.ShapeDtypeStruct((t_tok, RET_WIDTH), BF),
                   jax.ShapeDtypeStruct((RET_HEADS, t_tok // c, RET_DIM, RET_DIM), F32)],
        in_specs=[blk] * 4 + [cst] * 4,
        out_specs=[blk, blk, pl.BlockSpec((None, nb, RET_DIM, RET_DIM), lambda h, i: (h, i, 0, 0))],
        scratch_shapes=[pltpu.VMEM((RET_DIM, RET_DIM), F32)],
        compiler_params=_cp("parallel", "arbitrary"),
    )(rq, rk, rv, rg, *consts)


def _ret_bwd(rq, rk, rv, rg, y_raw, dyo, states, consts, cos_t, sin_t):
    t_tok = rq.shape[0]
    nb = 4 if t_tok % (4 * RET_CHUNK) == 0 else 1
    tr = nb * RET_CHUNK
    n_steps = t_tok // tr
    c = RET_CHUNK

    def body(q_ref, k_ref, v_ref, g_ref, y_ref, dyo_ref, st_ref, dm_ref, qd_ref, kd_ref, cd_ref,
             cos_ref, sin_ref, dq_ref, dk_ref, dv_ref, dg_ref, ds_s):
        @pl.when(pl.program_id(1) == 0)
        def _():
            ds_s[...] = jnp.zeros_like(ds_s)

        dm, qd, kd, cd = dm_ref[...], qd_ref[...], kd_ref[...], cd_ref[...]
        for b in reversed(range(nb)):
            rows = slice(b * c, (b + 1) * c)
            q, k, v = q_ref[rows, :], k_ref[rows, :], v_ref[rows, :]
            cosv, sinv = cos_ref[rows, :], sin_ref[rows, :]
            yn, r = _group_norm(y_ref[rows, :])
            gate = g_ref[rows, :].astype(F32)
            sg = _sigmoid(gate)
            dyo = dyo_ref[rows, :]
            dg_ref[rows, :] = (dyo * yn * (sg * (1.0 + gate * (1.0 - sg)))).astype(BF)
            dyn = dyo * (gate * sg)
            dy = r * (dyn - jnp.mean(dyn, axis=-1, keepdims=True)
                      - yn * jnp.mean(dyn * yn, axis=-1, keepdims=True))
            dyb = dy.astype(BF)
            state_b = st_ref[b].astype(BF)
            dstate = ds_s[...]
            dstate_b = dstate.astype(BF)
            qdb = (q.astype(F32) * qd).astype(BF)
            kdb = (k.astype(F32) * kd).astype(BF)
            sc = (_dot_nt(q, k) * dm).astype(BF)
            dv = _dot_tn(sc, dyb) + _dot(kdb, dstate_b)
            dp = (_dot_nt(dyb, v) * dm).astype(BF)
            dq = _dot(dp, k) + _dot_nt(dyb, state_b) * qd
            dk = (_dot_tn(dp, q) + _dot_nt(v, dstate_b) * kd) * RET_SCALE
            ds_s[...] = cd * dstate + _dot_tn(qdb, dyb)
            dv_ref[rows, :] = dv.astype(BF)
            dq_ref[rows, :] = (dq * cosv - _swap_pairs(dq) * sinv).astype(BF)
            dk_ref[rows, :] = (dk * cosv - _swap_pairs(dk) * sinv).astype(BF)

    rev = lambda i: n_steps - 1 - i
    blk = pl.BlockSpec((tr, RET_DIM), lambda h, i: (rev(i), h))
    tab = pl.BlockSpec((tr, RET_DIM), lambda h, i: (rev(i), 0))
    cst = pl.BlockSpec((None, c, RET_DIM), lambda h, i: (h, 0, 0))
    sds = jax.ShapeDtypeStruct((t_tok, RET_WIDTH), BF)
    return pl.pallas_call(
        body, name="ret_bwd", grid=(RET_HEADS, n_steps), out_shape=[sds] * 4,
        in_specs=[blk] * 6 + [pl.BlockSpec((None, nb, RET_DIM, RET_DIM), lambda h, i: (h, rev(i), 0, 0))]
        + [cst] * 4 + [tab, tab],
        out_specs=[blk] * 4,
        scratch_shapes=[pltpu.VMEM((RET_DIM, RET_DIM), F32)],
        compiler_params=_cp("parallel", "arbitrary"),
    )(rq, rk, rv, rg, y_raw, dyo, states, *consts, cos_t, sin_t)


def _mix_out(h, y_ret, y_fox, ga, gb, w_rf, kr, kf, w_o, ko):
    t_tok, d = h.shape
    cz = w_rf.shape[-1]
    ro = w_o.shape[-2]
    tm = _tile(t_tok, 512)

    def body(h_ref, yr_ref, yf_ref, ga_ref, gb_ref, wr_ref, wf_ref, wo_ref, ho_ref, za_ref, zb_ref, mix_ref):
        yr, yf = yr_ref[...], yf_ref[...]
        for j in range(N_CHIPS):
            sl = slice(j * cz, (j + 1) * cz)
            za = _dot(yr, wr_ref[j])
            zb = _dot(yf, wf_ref[j])
            za_ref[:, sl] = za.astype(BF)
            zb_ref[:, sl] = zb.astype(BF)
            mix_ref[:, sl] = (ga_ref[:, sl].astype(F32) * za + gb_ref[:, sl].astype(F32) * zb).astype(BF)
        acc = h_ref[...]
        for j in range(N_CHIPS):
            acc = acc + _dot(mix_ref[:, j * ro:(j + 1) * ro], wo_ref[j])
        ho_ref[...] = acc

    row = lambda c: pl.BlockSpec((tm, c), lambda i: (i, 0))
    wsp = lambda k, r, c: pl.BlockSpec((N_CHIPS, None, r, c), lambda i: (0, k, 0, 0))
    sds = lambda dt: jax.ShapeDtypeStruct((t_tok, d), dt)
    return pl.pallas_call(
        body, name="mix_out", grid=(t_tok // tm,),
        out_shape=[sds(F32), sds(BF), sds(BF), sds(BF)],
        in_specs=[row(d), row(RET_WIDTH), row(FOX_WIDTH), row(d), row(d),
                  wsp(kr, RET_WIDTH, cz), wsp(kf, FOX_WIDTH, cz), wsp(ko, ro, d)],
        out_specs=[row(d)] * 4, compiler_params=_cp("parallel"),
    )(h, y_ret, y_fox, ga, gb, w_rf, w_rf, w_o)


def _mix_out_bwd(dh, za, zb, ga, gb, y_fox, w_rf, kr, kf, w_o, ko):
    t_tok, d = dh.shape
    cz = w_rf.shape[-1]
    ro = w_o.shape[-2]
    tm = _tile(t_tok, 256)

    def body(dh_ref, za_ref, zb_ref, ga_ref, gb_ref, yf_ref, wr_ref, wf_ref, wo_ref,
             dhb_ref, dgp_ref, dza_ref, dzb_ref, dyr_ref, dyf_ref, dl_ref, db_ref):
        @pl.when(pl.program_id(0) == 0)
        def _():
            db_ref[...] = jnp.zeros_like(db_ref)

        dhb = dh_ref[...].astype(BF)
        dhb_ref[...] = dhb
        dyr = jnp.zeros((tm, RET_WIDTH), F32)
        dyf = jnp.zeros((tm, FOX_WIDTH), F32)
        for j in range(N_CHIPS):
            sl = slice(j * ro, (j + 1) * ro)
            dmix = _dot_nt(dhb, wo_ref[j])
            ga, gb = ga_ref[:, sl].astype(F32), gb_ref[:, sl].astype(F32)
            dza = (dmix * ga).astype(BF)
            dzb = (dmix * gb).astype(BF)
            dza_ref[:, sl] = dza
            dzb_ref[:, sl] = dzb
            dga = dmix * za_ref[:, sl].astype(F32) * ga * (1.0 - ga)
            dgb = dmix * zb_ref[:, sl].astype(F32) * gb * (1.0 - gb)
            dgp_ref[:, sl] = dga.astype(BF)
            dgp_ref[:, d + j * ro:d + (j + 1) * ro] = dgb.astype(BF)
            db_ref[:, sl] += jnp.sum(dga, axis=0, keepdims=True)
            db_ref[:, d + j * ro:d + (j + 1) * ro] += jnp.sum(dgb, axis=0, keepdims=True)
        for j in range(N_CHIPS):
            sl = slice(j * cz, (j + 1) * cz)
            dyr = dyr + _dot_nt(dza_ref[:, sl], wr_ref[j])
            dyf = dyf + _dot_nt(dzb_ref[:, sl], wf_ref[j])
        dyr_ref[...] = dyr
        dyfb = dyf.astype(BF)
        dyf_ref[...] = dyfb
        prod = dyfb.astype(F32) * yf_ref[...]
        for pp in range(FOX_HEADS // 2):
            blk = prod[:, pp * 128:(pp + 1) * 128]
            first, _ = _head_masks()
            s0 = jnp.sum(jnp.where(first, blk, 0.0), axis=1, keepdims=True)
            s1 = jnp.sum(jnp.where(first, 0.0, blk), axis=1, keepdims=True)
            dl_ref[:, pp * 128:(pp + 1) * 128] = jnp.where(first, s0, s1)

    row = lambda c: pl.BlockSpec((tm, c), lambda i: (i, 0))
    wsp = lambda k, r, c: pl.BlockSpec((N_CHIPS, None, r, c), lambda i: (0, k, 0, 0))
    sds = lambda c, dt: jax.ShapeDtypeStruct((t_tok, c), dt)
    return pl.pallas_call(
        body, name="mix_out_bwd", grid=(t_tok // tm,),
        out_shape=[sds(d, BF), sds(2 * d, BF), sds(d, BF), sds(d, BF), sds(RET_WIDTH, F32),
                   sds(FOX_WIDTH, BF), sds(FOX_WIDTH, F32), jax.ShapeDtypeStruct((1, 2 * d), F32)],
        in_specs=[row(d)] * 5 + [row(FOX_WIDTH), wsp(kr, RET_WIDTH, cz), wsp(kf, FOX_WIDTH, cz), wsp(ko, ro, d)],
        out_specs=[row(d), row(2 * d), row(d), row(d), row(RET_WIDTH), row(FOX_WIDTH), row(FOX_WIDTH),
                   pl.BlockSpec((1, 2 * d), lambda i: (0, 0))],
        compiler_params=_cp("arbitrary"),
    )(dh, za, zb, ga, gb, y_fox, w_rf, w_rf, w_o)


def _mix_in_bwd(dh, h, ln, parts, dff, dgpre, w_in, w_m, km):
    t_tok, d = h.shape
    cm = w_m.shape[-1]
    tm = _tile(t_tok, 256)

    def body(dh_ref, h_ref, ln_ref, p0, p1, p2, p3, p4, p5, p6, dff_ref, dgp_ref, win_ref, wm_ref,
             dhi_ref, dln_ref, dproj_ref):
        @pl.when(pl.program_id(0) == 0)
        def _():
            dln_ref[...] = jnp.zeros_like(dln_ref)

        for k, pr in enumerate((p0, p1, p2, p3, p4, p5, p6)):
            dproj_ref[:, k * 512:(k + 1) * 512] = pr[...]
        dproj_ref[:, FF_COL:FF_COL + 128] = dff_ref[...]
        dproj_ref[:, FF_COL + 128:] = jnp.zeros((tm, IN_PAD - FF_COL - 128), BF)
        du = _dot_nt(dproj_ref[...], win_ref[...])
        for j in range(N_CHIPS):
            du = du + _dot_nt(dgp_ref[:, j * cm:(j + 1) * cm], wm_ref[j])
        xv = h_ref[...]
        dx, dln = _rms_bwd(du, xv, _rstd(xv), ln_ref[...])
        dln_ref[...] += dln
        dhi_ref[...] = dh_ref[...] + dx

    row = lambda c: pl.BlockSpec((tm, c), lambda i: (i, 0))
    full = lambda *s: pl.BlockSpec(s, lambda i: (0,) * len(s))
    return pl.pallas_call(
        body, name="mix_in_bwd", grid=(t_tok // tm,),
        out_shape=[jax.ShapeDtypeStruct((t_tok, d), F32), jax.ShapeDtypeStruct((1, d), F32),
                   jax.ShapeDtypeStruct((t_tok, IN_PAD), BF)],
        in_specs=[row(d), row(d), full(1, d)] + [row(512)] * 7 + [row(128), row(2 * d), full(d, IN_PAD),
                  pl.BlockSpec((N_CHIPS, None, d, cm), lambda i: (0, km, 0, 0))],
        out_specs=[row(d), full(1, d), row(IN_PAD)],
        compiler_params=_cp("arbitrary"),
    )(dh, h, ln, *parts, dff, dgpre, w_in, w_m)


def _tail(h, p, target, ln_ple, ln_fin, w_pg, kpg, w_pl):
    t_tok, d = h.shape
    pd = p.shape[1]
    rg = w_pg.shape[-2]
    cp = w_pl.shape[-1]
    tm = _tile(t_tok, 256)
    n_steps = t_tok // tm

    def body(h_ref, p_ref, t_ref, lp_ref, lf_ref, wg_ref, wp_ref,
             dh_ref, n_ref, dgp_ref, dpe_ref, pb_ref, loss_ref, dlf_ref, dlp_ref, pe_s, dn_s):
        i = pl.program_id(0)

        @pl.when(i == 0)
        def _():
            loss_ref[...] = jnp.zeros_like(loss_ref)
            dlf_ref[...] = jnp.zeros_like(dlf_ref)
            dlp_ref[...] = jnp.zeros_like(dlp_ref)

        xv = h_ref[...]
        r3 = _rstd(xv)
        nb = (xv * r3 * lp_ref[...]).astype(BF)
        n_ref[...] = nb
        pb = p_ref[...].astype(BF)
        pb_ref[...] = pb
        pgpre = jnp.zeros((tm, d), F32)
        for j in range(N_CHIPS):
            pgpre = pgpre + _dot(nb[:, j * rg:(j + 1) * rg], wg_ref[j])
            pe_s[:, j * cp:(j + 1) * cp] = _dot(pb, wp_ref[j])
        pg = _sigmoid(pgpre)
        pe = pe_s[...]
        h4 = xv + pg * pe
        r4 = _rstd(h4)
        err = h4 * r4 * lf_ref[...] - t_ref[...]
        loss_ref[...] += 0.5 * jnp.sum(jnp.sum(err * err, axis=1, keepdims=True), axis=0, keepdims=True) / d
        dh4, dlf = _rms_bwd(err * (1.0 / d), h4, r4, lf_ref[...])
        dlf_ref[...] += dlf
        dpe_ref[...] = (dh4 * pg).astype(BF)
        dgp = (dh4 * pe * pg * (1.0 - pg)).astype(BF)
        dgp_ref[...] = dgp
        for j in range(N_CHIPS):
            dn_s[:, j * rg:(j + 1) * rg] = _dot_nt(dgp, wg_ref[j])
        dx, dlp = _rms_bwd(dn_s[...], xv, r3, lp_ref[...])
        dlp_ref[...] += dlp
        dh_ref[...] = dh4 + dx

    row = lambda c: pl.BlockSpec((tm, c), lambda i: (i, 0))
    full = lambda *s: pl.BlockSpec(s, lambda i: (0,) * len(s))
    sds = lambda c, dt: jax.ShapeDtypeStruct((t_tok, c), dt)
    vec = jax.ShapeDtypeStruct((1, d), F32)
    return pl.pallas_call(
        body, name="tail", grid=(n_steps,),
        out_shape=[sds(d, F32), sds(d, BF), sds(d, BF), sds(d, BF), sds(pd, BF),
                   jax.ShapeDtypeStruct((1, 128), F32), vec, vec],
        in_specs=[row(d), row(pd), row(d), full(1, d), full(1, d),
                  pl.BlockSpec((N_CHIPS, None, rg, d), lambda i: (0, kpg, 0, 0)), full(N_CHIPS, pd, cp)],
        out_specs=[row(d), row(d), row(d), row(d), row(pd), full(1, 128), full(1, d), full(1, d)],
        scratch_shapes=[pltpu.VMEM((tm, d), F32), pltpu.VMEM((tm, d), F32)],
        compiler_params=_cp("arbitrary"),
    )(h, p, target, ln_ple, ln_fin, w_pg, w_pl)


def _local_step(x, p, pos, target, small, gathered):
    t_tok, d = x.shape
    gu, dn, w_in, mrg, rf, og, ple = (gathered[k] for k in ("gu", "dn", "w_in", "mrg", "rf", "og", "ple"))

    half = RET_DIM // 2
    inv_freq = 1.0 / (ROPE_BASE ** (jnp.arange(half, dtype=F32) / half))
    cos_t, sin_t = _rope_tables(pos.astype(F32).reshape(t_tok, 1), jnp.repeat(inv_freq, 2).reshape(1, RET_DIM))
    consts = _ret_consts()
    b_pad = jnp.pad(small["b_forget"], ((0, 0), (0, 128 - FOX_HEADS)))

    h1, n1, g1, u1 = _ffn_fwd(x, small["ln_ffn1"], gu, 0, 1, dn, 0)
    u, rq, rk, rv, rg, fq, fk, fv, ffl, ga, gb = _mix_in(h1, small["ln_mix"], w_in, mrg, 0, small["b_merge"],
                                                         cos_t, sin_t)
    cum_e, cum_t = _forget_fwd(ffl, b_pad)
    cum_t3 = cum_t.reshape(FOX_HEADS, 1, t_tok)
    y_raw, y_ret, states = _ret_fwd(rq, rk, rv, rg, consts)
    y_fox, y_fox32, lse_e = _fox_fwd(fq, fk, fv, cum_e, cum_t3)
    h2, za, zb, mix = _mix_out(h1, y_ret, y_fox, ga, gb, rf, 0, 1, og, 0)
    h3, n2, g2, u2 = _ffn_fwd(h2, small["ln_ffn2"], gu, 2, 3, dn, 1)

    dh3, n3, dpgpre, dpe, pb, loss, dln_final, dln_ple = _tail(
        h3, p, target, small["ln_ple"], small["ln_final"], og, 1, ple)
    g_ple_gate = _wgrad_rows("wgrad_rows_d", n3, dpgpre, N_CHIPS)
    g_ple = _wgrad_cols("wgrad_ple", pb, dpe, N_CHIPS)

    dh2, dln_ffn2, dg2, du2, a2, dhb3 = _ffn_bwd(dh3, h2, small["ln_ffn2"], g2, u2, gu, 2, 3, dn, 1)
    g_f2g = _wgrad_a_shared("wgrad_ffn_in", n2, dg2)
    g_f2u = _wgrad_a_shared("wgrad_ffn_in", n2, du2)
    g_f2d = _wgrad_b_shared("wgrad_ffn_out", a2, dhb3)

    dhb2, dgpre, dza, dzb, dy_ret, dy_fox, delta_e, db_merge = _mix_out_bwd(
        dh2, za, zb, ga, gb, y_fox32, rf, 0, 1, og, 0)
    g_out = _wgrad_rows("wgrad_rows_d", mix, dhb2, N_CHIPS)
    g_ret_out = _wgrad_cols("wgrad_branch_out", y_ret, dza, N_CHIPS)
    g_fox_out = _wgrad_cols("wgrad_branch_out", y_fox, dzb, N_CHIPS)
    drq, drk, drv, drg = _ret_bwd(rq, rk, rv, rg, y_raw, dy_ret, states, consts, cos_t, sin_t)
    dfk, dfv, dcum_t3 = _fox_bwd_kv(fq, fk, fv, dy_fox, cum_e, cum_t3, lse_e, delta_e)
    dfq, dcum_q = _fox_bwd_q(fq, fk, fv, dy_fox, cum_e, cum_t3, lse_e, delta_e)
    dff, db_forget = _forget_bwd(dcum_t3.reshape(FOX_HEADS, t_tok), dcum_q, ffl, b_pad)
    dh1, dln_mix, dproj = _mix_in_bwd(dh2, h1, small["ln_mix"], (drq, drk, drv, drg, dfq, dfk, dfv), dff, dgpre,
                                      w_in, mrg, 0)
    g_in = _wgrad_cols("wgrad_in", u, dproj, IN_PAD // 512)
    g_mrg = _wgrad_cols("wgrad_merge", u, dgpre, N_CHIPS)

    dx, dln_ffn1, dg1, du1, a1, dhb1 = _ffn_bwd(dh1, x, small["ln_ffn1"], g1, u1, gu, 0, 1, dn, 0)
    g_f1g = _wgrad_a_shared("wgrad_ffn_in", n1, dg1)
    g_f1u = _wgrad_a_shared("wgrad_ffn_in", n1, du1)
    g_f1d = _wgrad_b_shared("wgrad_ffn_out", a1, dhb1)

    g_in = jnp.transpose(g_in, (1, 0, 2)).reshape(d, IN_PAD)[:, :IN_COLS]
    g_in = jnp.transpose(g_in.reshape(d, N_CHIPS, IN_COLS // N_CHIPS), (1, 0, 2))

    grads = dict(w_ffn1_gate=g_f1g, w_ffn1_up=g_f1u, w_ffn1_down=g_f1d, w_in=g_in, w_merge=g_mrg,
                 w_ret_out=g_ret_out, w_fox_out=g_fox_out, w_out=g_out, w_ffn2_gate=g_f2g, w_ffn2_up=g_f2u,
                 w_ffn2_down=g_f2d, w_ple=g_ple, w_ple_gate=g_ple_gate)
    small_grads = dict(ln_ffn1=dln_ffn1, ln_mix=dln_mix, b_forget=db_forget[:, :FOX_HEADS], b_merge=db_merge,
                       ln_ffn2=dln_ffn2, ln_ple=dln_ple, ln_final=dln_final)
    return loss, dx, grads, small_grads


BIG = ["w_ffn1_gate", "w_ffn1_up", "w_ffn1_down", "w_in", "w_merge", "w_ret_out", "w_fox_out", "w_out",
       "w_ffn2_gate", "w_ffn2_up", "w_ffn2_down", "w_ple", "w_ple_gate"]
SMALL = ["ln_ffn1", "ln_mix", "b_forget", "b_merge", "ln_ffn2", "ln_ple", "ln_final"]
WEIGHTS = ["ln_ffn1", "w_ffn1_gate", "w_ffn1_up", "w_ffn1_down", "ln_mix", "w_in", "b_forget", "w_merge", "b_merge",
           "w_ret_out", "w_fox_out", "w_out", "ln_ffn2", "w_ffn2_gate", "w_ffn2_up", "w_ffn2_down", "ln_ple",
           "w_ple", "w_ple_gate", "ln_final"]


def _pack_small(vals):
    rows = []
    for name in SMALL:
        v = vals[name].reshape(-1)
        n = -(-v.shape[0] // 128) * 128
        rows.append(jnp.pad(v, (0, n - v.shape[0])).reshape(n // 128, 128))
    packed = jnp.concatenate(rows, axis=0)
    pad = -packed.shape[0] % 8
    return jnp.pad(packed, ((0, pad), (0, 0)))


def _unpack_small(packed, sizes):
    out, r = {}, 0
    for name in SMALL:
        n = sizes[name]
        nr = -(-n // 128)
        out[name] = packed[r:r + nr].reshape(1, nr * 128)[:, :n]
        r += nr
    return out


def kernel(x, p, positions, ln_ffn1, w_ffn1_gate, w_ffn1_up, w_ffn1_down, ln_mix, w_in, b_forget, w_merge, b_merge, w_ret_out, w_fox_out, w_out, ln_ffn2, w_ffn2_gate, w_ffn2_up, w_ffn2_down, ln_ple, w_ple, w_ple_gate, ln_final, loss_target, m_ln_ffn1, m_w_ffn1_gate, m_w_ffn1_up, m_w_ffn1_down, m_ln_mix, m_w_in, m_b_forget, m_w_merge, m_b_merge, m_w_ret_out, m_w_fox_out, m_w_out, m_ln_ffn2, m_w_ffn2_gate, m_w_ffn2_up, m_w_ffn2_down, m_ln_ple, m_w_ple, m_w_ple_gate, m_ln_final, v_ln_ffn1, v_w_ffn1_gate, v_w_ffn1_up, v_w_ffn1_down, v_ln_mix, v_w_in, v_b_forget, v_w_merge, v_b_merge, v_w_ret_out, v_w_fox_out, v_w_out, v_ln_ffn2, v_w_ffn2_gate, v_w_ffn2_up, v_w_ffn2_down, v_ln_ple, v_w_ple, v_w_ple_gate, v_ln_final):
    args = dict(locals())
    w = {n: args[n] for n in WEIGHTS}
    m = {n: args["m_" + n] for n in WEIGHTS}
    v = {n: args["v_" + n] for n in WEIGHTS}
    d = x.shape[-1]
    t_tok = x.shape[1]

    shard = {n: w[n][0].astype(BF) for n in BIG}
    full = dict(zip(BIG, _all_gather_chips([shard[n] for n in BIG])))
    stack = lambda names: jnp.stack([full[n] for n in names], axis=1)
    w_in_full = jnp.transpose(full["w_in"], (1, 0, 2)).reshape(d, IN_COLS)
    gathered = dict(
        gu=stack(["w_ffn1_gate", "w_ffn1_up", "w_ffn2_gate", "w_ffn2_up"]),
        dn=stack(["w_ffn1_down", "w_ffn2_down"]),
        w_in=jnp.pad(w_in_full, ((0, 0), (0, IN_PAD - IN_COLS))),
        mrg=stack(["w_merge"]), rf=stack(["w_ret_out", "w_fox_out"]), og=stack(["w_out", "w_ple_gate"]),
        ple=full["w_ple"])
    small = {n: w[n].reshape(1, -1) for n in SMALL}

    loss, grad_x, grads, small_grads = _local_step(
        x[0], p[0, 0], positions[0], loss_target[0], small, gathered)

    c = lax.axis_index("c")
    glist = [grads[n] for n in BIG]
    got = _sibling_half_swap(glist)
    mine = [lax.dynamic_slice_in_dim(g, c * (g.shape[1] // 2), g.shape[1] // 2, axis=1) for g in glist]
    parts = [_add_halves(a, b) for a, b in zip(mine, got)]
    recv = _chip_exchange(parts)
    halves = [_sum_chips(r) for r in recv]
    gsum = dict(zip(BIG, _sibling_join(halves)))
    sizes = {n: w[n].size for n in SMALL}
    gsum.update(_unpack_small(_all_reduce_small(_pack_small(small_grads)), sizes))

    loss = lax.psum(loss[0, 0], ("x", "y", "c"))

    grad_w, delta, new_m, new_v = [], [], [], []
    for n in WEIGHTS:
        shp = w[n].shape
        two_d = (shp[-2], shp[-1]) if len(shp) == 3 else (1, shp[-1])
        g2 = gsum[n].reshape(two_d)
        dl, nm, nv = _adamw(w[n].reshape(two_d), g2, m[n].reshape(two_d), v[n].reshape(two_d))
        grad_w.append(g2.reshape(shp))
        delta.append(dl.reshape(shp))
        new_m.append(nm.reshape(shp))
        new_v.append(nv.reshape(shp))
    return (loss, grad_x[None], *grad_w, *delta, *new_m, *new_v)
```

```python
import functools
import operator

import jax
import jax.numpy as jnp
from jax import lax
from jax.experimental import pallas as pl
from jax.experimental.pallas import tpu as pltpu

F32 = jnp.float32
BF = jnp.bfloat16
MESH = pl.DeviceIdType.MESH

EPS = 1e-6
ROPE_BASE = 10000.0
N_CHIPS = 4
RET_HEADS = 4
RET_DIM = 128
RET_WIDTH = RET_HEADS * RET_DIM
RET_CHUNK = 128
RET_SCALE = RET_DIM ** -0.5
FOX_HEADS = 8
FOX_DIM = 64
FOX_WIDTH = FOX_HEADS * FOX_DIM
FOX_SCALE = FOX_DIM ** -0.5
IN_COLS = 4 * RET_WIDTH + 3 * FOX_WIDTH + FOX_HEADS
IN_PAD = 4096
FF_COL = 4 * RET_WIDTH + 3 * FOX_WIDTH
NEG = -1e30

ADAM_LR = 0.001
ADAM_B1 = 0.9
ADAM_B2 = 0.999
ADAM_EPS = 1e-08
ADAM_WD = 0.01
ADAM_STEP = 10

VMEM_LIMIT = 52 * 1024 * 1024

NT = (((1,), (1,)), ((), ()))
TN = (((0,), (0,)), ((), ()))

HBM_SPEC = pl.BlockSpec(memory_space=pltpu.HBM)
VMEM_SPEC = pl.BlockSpec(memory_space=pltpu.VMEM)


def _dot(a, b):
    return jnp.dot(a, b, preferred_element_type=F32)


def _dot_nt(a, b):
    return lax.dot_general(a, b, NT, preferred_element_type=F32)


def _dot_tn(a, b):
    return lax.dot_general(a, b, TN, preferred_element_type=F32)


def _rstd(xv):
    return lax.rsqrt(jnp.mean(xv * xv, axis=-1, keepdims=True) + EPS)


def _rms_bwd(dn, xv, r, ln):
    xh = xv * r
    dxh = dn * ln
    dx = r * (dxh - xh * jnp.mean(dxh * xh, axis=-1, keepdims=True))
    return dx, jnp.sum(dn * xh, axis=0, keepdims=True)


def _sigmoid(x):
    return jax.nn.sigmoid(x)


def _tile(n, pref):
    return pref if n % pref == 0 else n


class _Comm:
    def __init__(self, ins, out_shapes, sems, start, wait):
        self.ins, self.out_shapes, self.sems, self.start, self.wait = list(ins), list(out_shapes), list(sems), start, wait


def _merge(comms):
    comms = [c for c in comms if c is not None]
    if not comms:
        return None
    bounds, ni, no, ns = [], 0, 0, 0
    for c in comms:
        bounds.append((ni, no, ns))
        ni, no, ns = ni + len(c.ins), no + len(c.out_shapes), ns + len(c.sems)

    def run(which):
        def f(ins, outs, sems):
            for c, (i, o, s) in zip(comms, bounds):
                getattr(c, which)(ins[i:i + len(c.ins)], outs[o:o + len(c.out_shapes)], sems[s:s + len(c.sems)])
        return f

    return _Comm([a for c in comms for a in c.ins], [a for c in comms for a in c.out_shapes],
                 [a for c in comms for a in c.sems], run("start"), run("wait"))


def _split_outs(comms, outs):
    res, o = [], 0
    for c in comms:
        if c is not None:
            res.append(list(outs[o:o + len(c.out_shapes)]))
            o += len(c.out_shapes)
    return res


def _pcall(body, args, *, name, out_shape, grid=(), in_specs=None, out_specs=None, scratch=(), comm=None):
    many = isinstance(out_shape, (list, tuple))
    outs = list(out_shape) if many else [out_shape]
    n_in, n_out, n_scr = len(args), len(outs), len(scratch)
    if in_specs is None:
        in_specs, out_specs = [VMEM_SPEC] * n_in, [VMEM_SPEC] * n_out
    else:
        in_specs, out_specs = list(in_specs), (list(out_specs) if many else [out_specs])
    params = pltpu.CompilerParams(dimension_semantics=("arbitrary",) * len(grid), vmem_limit_bytes=VMEM_LIMIT)
    if comm is None:
        res = pl.pallas_call(body, name=name, grid=grid, out_shape=outs, in_specs=in_specs, out_specs=out_specs,
                             scratch_shapes=list(scratch), compiler_params=params)(*args)
        return list(res) if many else res[0]
    ci, co = len(comm.ins), len(comm.out_shapes)

    def wrapped(*refs):
        a, ca = refs[:n_in], refs[n_in:n_in + ci]
        o = refs[n_in + ci:n_in + ci + n_out]
        cout = refs[n_in + ci + n_out:n_in + ci + n_out + co]
        s = refs[n_in + ci + n_out + co:n_in + ci + n_out + co + n_scr]
        csem = refs[n_in + ci + n_out + co + n_scr:]
        if grid:
            first = functools.reduce(operator.and_, [pl.program_id(k) == 0 for k in range(len(grid))])
            last = functools.reduce(operator.and_, [pl.program_id(k) == grid[k] - 1 for k in range(len(grid))])
            pl.when(first)(lambda: comm.start(ca, cout, csem))
            body(*a, *o, *s)
            pl.when(last)(lambda: comm.wait(ca, cout, csem))
        else:
            comm.start(ca, cout, csem)
            body(*a, *o, *s)
            comm.wait(ca, cout, csem)

    res = pl.pallas_call(
        wrapped, name=name, grid=grid, out_shape=outs + comm.out_shapes,
        in_specs=in_specs + [HBM_SPEC] * ci, out_specs=out_specs + [HBM_SPEC] * co,
        scratch_shapes=list(scratch) + comm.sems, compiler_params=params)(*args, *comm.ins)
    mine = list(res[:n_out])
    return (mine if many else mine[0]), list(res[n_out:])


def _peer_chips(x, y):
    return [(1 - x, y), (x, 1 - y), (1 - x, 1 - y)]


def _c_all_gather(shards):
    n = len(shards)

    def copies(ins, outs, sems):
        send_sems, recv_sems, local_sems = sems
        x, y, c = lax.axis_index("x"), lax.axis_index("y"), lax.axis_index("c")
        me = 2 * x + y
        peers = _peer_chips(x, y)

        def remote(g, j, slot):
            return pltpu.make_async_remote_copy(
                src_ref=ins[g], dst_ref=outs[g].at[slot], send_sem=send_sems.at[g, j],
                recv_sem=recv_sems.at[g, j], device_id=(*peers[j], c), device_id_type=MESH)

        local = [pltpu.make_async_copy(ins[g], outs[g].at[me], local_sems.at[g]) for g in range(n)]
        sends = [remote(g, j, me) for g in range(n) for j in range(3)]
        recvs = [remote(g, j, 2 * peers[j][0] + peers[j][1]) for g in range(n) for j in range(3)]
        return local, sends, recvs

    return _Comm(
        shards, [jax.ShapeDtypeStruct((N_CHIPS,) + s.shape, s.dtype) for s in shards],
        [pltpu.SemaphoreType.DMA((n, 3)), pltpu.SemaphoreType.DMA((n, 3)), pltpu.SemaphoreType.DMA((n,))],
        *_start_wait(copies))


def _start_wait(copies):
    def start(ins, outs, sems):
        local, sends, _ = copies(ins, outs, sems)
        for cp in local + sends:
            cp.start()

    def wait(ins, outs, sems):
        local, sends, recvs = copies(ins, outs, sems)
        for cp in recvs:
            cp.wait_recv()
        for cp in sends:
            cp.wait_send()
        for cp in local:
            cp.wait()

    return start, wait


def _c_half_swap(grads):
    n = len(grads)

    def copies(ins, outs, sems):
        send_sems, recv_sems = sems
        x, y, c = lax.axis_index("x"), lax.axis_index("y"), lax.axis_index("c")
        sends = []
        for g in range(n):
            half = ins[g].shape[1] // 2
            sends.append(pltpu.make_async_remote_copy(
                src_ref=ins[g].at[:, pl.ds((1 - c) * half, half), :], dst_ref=outs[g],
                send_sem=send_sems.at[g], recv_sem=recv_sems.at[g], device_id=(x, y, 1 - c), device_id_type=MESH))
        return [], sends, sends

    return _Comm(
        grads, [jax.ShapeDtypeStruct((N_CHIPS, s.shape[1] // 2, s.shape[2]), s.dtype) for s in grads],
        [pltpu.SemaphoreType.DMA((n,)), pltpu.SemaphoreType.DMA((n,))], *_start_wait(copies))


def _c_chip_exchange(parts):
    n = len(parts)

    def copies(ins, outs, sems):
        send_sems, recv_sems, local_sems = sems
        x, y, c = lax.axis_index("x"), lax.axis_index("y"), lax.axis_index("c")
        me = 2 * x + y
        peers = _peer_chips(x, y)

        def remote(g, j, src_slot, dst_slot):
            return pltpu.make_async_remote_copy(
                src_ref=ins[g].at[src_slot], dst_ref=outs[g].at[dst_slot], send_sem=send_sems.at[g, j],
                recv_sem=recv_sems.at[g, j], device_id=(*peers[j], c), device_id_type=MESH)

        chip = [2 * px + py for px, py in peers]
        local = [pltpu.make_async_copy(ins[g].at[me], outs[g].at[me], local_sems.at[g]) for g in range(n)]
        sends = [remote(g, j, chip[j], me) for g in range(n) for j in range(3)]
        recvs = [remote(g, j, me, chip[j]) for g in range(n) for j in range(3)]
        return local, sends, recvs

    return _Comm(
        parts, [jax.ShapeDtypeStruct(s.shape, s.dtype) for s in parts],
        [pltpu.SemaphoreType.DMA((n, 3)), pltpu.SemaphoreType.DMA((n, 3)), pltpu.SemaphoreType.DMA((n,))],
        *_start_wait(copies))


def _c_join(halves):
    n = len(halves)

    def copies(ins, outs, sems):
        send_sems, recv_sems, local_sems = sems
        x, y, c = lax.axis_index("x"), lax.axis_index("y"), lax.axis_index("c")

        def remote(g, slot):
            return pltpu.make_async_remote_copy(
                src_ref=ins[g], dst_ref=outs[g].at[slot], send_sem=send_sems.at[g], recv_sem=recv_sems.at[g],
                device_id=(x, y, 1 - c), device_id_type=MESH)

        local = [pltpu.make_async_copy(ins[g], outs[g].at[c], local_sems.at[g]) for g in range(n)]
        return local, [remote(g, c) for g in range(n)], [remote(g, 1 - c) for g in range(n)]

    return _Comm(
        halves, [jax.ShapeDtypeStruct((2,) + s.shape, s.dtype) for s in halves],
        [pltpu.SemaphoreType.DMA((n,)), pltpu.SemaphoreType.DMA((n,)), pltpu.SemaphoreType.DMA((n,))],
        *_start_wait(copies))


def _all_reduce_small(v):
    rows = v.shape[0]

    def body(v_ref, out_ref, buf, send_sems, recv_sems):
        x, y, c = lax.axis_index("x"), lax.axis_index("y"), lax.axis_index("c")
        me = 4 * x + 2 * y + c
        buf[me] = v_ref[...]
        flips = [(fx, fy, fc) for fx in (0, 1) for fy in (0, 1) for fc in (0, 1)][1:]

        def peer(k):
            fx, fy, fc = flips[k]
            px, py, pc = x ^ fx, y ^ fy, c ^ fc
            return (px, py, pc), 4 * px + 2 * py + pc

        def copy(k, slot):
            return pltpu.make_async_remote_copy(
                src_ref=buf.at[slot], dst_ref=buf.at[slot], send_sem=send_sems.at[k],
                recv_sem=recv_sems.at[k], device_id=peer(k)[0], device_id_type=MESH)

        sends = [copy(k, me) for k in range(7)]
        for cp in sends:
            cp.start()
        for k in range(7):
            copy(k, peer(k)[1]).wait_recv()
        for cp in sends:
            cp.wait_send()
        acc = buf[0]
        for d in range(1, 8):
            acc = acc + buf[d]
        out_ref[...] = acc

    return _pcall(body, [v], name="all_reduce_small", out_shape=jax.ShapeDtypeStruct((rows, 128), F32),
                  scratch=[pltpu.VMEM((8, rows, 128), F32), pltpu.SemaphoreType.DMA((7,)),
                           pltpu.SemaphoreType.DMA((7,))])


def _add_halves(g, got):
    _, h, c = got.shape
    th = _tile(h, 128)
    nh = h // th
    half = lax.axis_index("c") * nh

    def body(h_ref, a_ref, b_ref, o_ref):
        o_ref[...] = (a_ref[...].astype(F32) + b_ref[...].astype(F32)).astype(o_ref.dtype)

    spec = pl.BlockSpec((1, th, c), lambda j, i, h_ref: (j, i, 0))
    mine = pl.BlockSpec((1, th, c), lambda j, i, h_ref: (j, h_ref[0] + i, 0))
    return _pcall_prefetch(body, half, [g, got], name="add_halves", grid=(N_CHIPS, nh),
                           out_shape=jax.ShapeDtypeStruct(got.shape, BF), in_specs=[mine, spec], out_specs=spec)


def _pcall_prefetch(body, scalar, args, *, name, grid, out_shape, in_specs, out_specs):
    return pl.pallas_call(
        body, name=name, out_shape=out_shape,
        grid_spec=pltpu.PrefetchScalarGridSpec(num_scalar_prefetch=1, grid=grid, in_specs=in_specs,
                                               out_specs=out_specs),
        compiler_params=pltpu.CompilerParams(dimension_semantics=("arbitrary",) * len(grid),
                                             vmem_limit_bytes=VMEM_LIMIT),
    )(jnp.reshape(scalar, (1,)).astype(jnp.int32), *args)


def _sum_chips(parts, comm=None):
    _, h, c = parts.shape
    th = _tile(h, 128)

    def body(p_ref, o_ref):
        acc = p_ref[0].astype(F32)
        for s in range(1, N_CHIPS):
            acc = acc + p_ref[s].astype(F32)
        o_ref[...] = acc

    return _pcall(body, [parts], name="sum_chips", grid=(h // th,), out_shape=jax.ShapeDtypeStruct((h, c), F32),
                  in_specs=[pl.BlockSpec((N_CHIPS, th, c), lambda i: (0, i, 0))],
                  out_specs=pl.BlockSpec((th, c), lambda i: (i, 0)), comm=comm)


def _adamw(w, g, m, v, comm=None):
    r, c = w.shape
    tr = _tile(r, 256)
    c1 = 1.0 / (1.0 - ADAM_B1 ** ADAM_STEP)
    c2 = 1.0 / (1.0 - ADAM_B2 ** ADAM_STEP)

    def body(w_ref, g_ref, m_ref, v_ref, d_ref, nm_ref, nv_ref):
        gv = g_ref[...]
        nm = ADAM_B1 * m_ref[...] + (1.0 - ADAM_B1) * gv
        nv = ADAM_B2 * v_ref[...] + (1.0 - ADAM_B2) * (gv * gv)
        nm_ref[...] = nm
        nv_ref[...] = nv
        d_ref[...] = -ADAM_LR * ((nm * c1) / (jnp.sqrt(nv * c2) + ADAM_EPS) + ADAM_WD * w_ref[...])

    spec = pl.BlockSpec((tr, c), lambda i: (i, 0))
    sds = jax.ShapeDtypeStruct((r, c), F32)
    return _pcall(body, [w, g, m, v], name="adamw", grid=(r // tr,), out_shape=[sds, sds, sds],
                  in_specs=[spec] * 4, out_specs=[spec] * 3, comm=comm)


def _wgrad(name, a, b, a_spec, b_spec, m, n, nb, n_tok, comm):
    def body(a_ref, b_ref, o_ref, acc):
        t = pl.program_id(1)

        @pl.when(t == 0)
        def _():
            acc[...] = jnp.zeros_like(acc)

        acc[...] += _dot_tn(a_ref[...], b_ref[...])

        @pl.when(t == n_tok - 1)
        def _():
            o_ref[...] = acc[...].astype(o_ref.dtype)

    return _pcall(body, [a, b], name=name, grid=(nb, n_tok), out_shape=jax.ShapeDtypeStruct((nb, m, n), BF),
                  in_specs=[a_spec, b_spec], out_specs=pl.BlockSpec((None, m, n), lambda j, t: (j, 0, 0)),
                  scratch=[pltpu.VMEM((m, n), F32)], comm=comm)


def _wgrad_cols(name, a, b, nb, comm=None):
    t_tok, m = a.shape
    n = b.shape[1] // nb
    tk = _tile(t_tok, 512)
    return _wgrad(name, a, b, pl.BlockSpec((tk, m), lambda j, t: (t, 0)), pl.BlockSpec((tk, n), lambda j, t: (t, j)),
                  m, n, nb, t_tok // tk, comm)


def _wgrad_rows(name, a, b, nb, comm=None):
    t_tok, n = b.shape
    m = a.shape[1] // nb
    tk = _tile(t_tok, 512)
    return _wgrad(name, a, b, pl.BlockSpec((tk, m), lambda j, t: (t, j)), pl.BlockSpec((tk, n), lambda j, t: (t, 0)),
                  m, n, nb, t_tok // tk, comm)


def _wgrad_a_shared(name, a, b4, comm=None):
    t_tok, m = a.shape
    nb, _, n = b4.shape
    tk = _tile(t_tok, 512)
    return _wgrad(name, a, b4, pl.BlockSpec((tk, m), lambda j, t: (t, 0)),
                  pl.BlockSpec((None, tk, n), lambda j, t: (j, t, 0)), m, n, nb, t_tok // tk, comm)


def _wgrad_b_shared(name, a4, b, comm=None):
    nb, t_tok, m = a4.shape
    n = b.shape[1]
    tk = _tile(t_tok, 512)
    return _wgrad(name, a4, b, pl.BlockSpec((None, tk, m), lambda j, t: (j, t, 0)),
                  pl.BlockSpec((tk, n), lambda j, t: (t, 0)), m, n, nb, t_tok // tk, comm)


def _w4_spec(r, c):
    return pl.BlockSpec((None, r, c), lambda i, j: (j, 0, 0))


def _ffn_fwd(h, ln, wg4, wu4, wd4, comm=None):
    t_tok, d = h.shape
    f = wg4.shape[-1]
    tm = _tile(t_tok, 512)

    def body(h_ref, ln_ref, wg_ref, wu_ref, wd_ref, ho_ref, n_ref, g_ref, u_ref, n_s, acc):
        j = pl.program_id(1)

        @pl.when(j == 0)
        def _():
            xv = h_ref[...]
            nv = (xv * _rstd(xv) * ln_ref[...]).astype(BF)
            n_s[...] = nv
            n_ref[...] = nv
            acc[...] = jnp.zeros_like(acc)

        nv = n_s[...]
        g = _dot(nv, wg_ref[...])
        u = _dot(nv, wu_ref[...])
        g_ref[...] = g.astype(BF)
        u_ref[...] = u.astype(BF)
        a = (g * _sigmoid(g) * u).astype(BF)
        acc[...] += _dot(a, wd_ref[...])

        @pl.when(j == N_CHIPS - 1)
        def _():
            ho_ref[...] = h_ref[...] + 0.5 * acc[...]

    row = pl.BlockSpec((tm, d), lambda i, j: (i, 0))
    gu = pl.BlockSpec((None, tm, f), lambda i, j: (j, i, 0))
    gu_sds = jax.ShapeDtypeStruct((N_CHIPS, t_tok, f), BF)
    return _pcall(
        body, [h, ln, wg4, wu4, wd4], name="ffn_fwd", grid=(t_tok // tm, N_CHIPS),
        out_shape=[jax.ShapeDtypeStruct((t_tok, d), F32), jax.ShapeDtypeStruct((t_tok, d), BF), gu_sds, gu_sds],
        in_specs=[row, pl.BlockSpec((1, d), lambda i, j: (0, 0)), _w4_spec(d, f), _w4_spec(d, f), _w4_spec(f, d)],
        out_specs=[row, row, gu, gu],
        scratch=[pltpu.VMEM((tm, d), BF), pltpu.VMEM((tm, d), F32)], comm=comm)


def _ffn_bwd(dho, h, ln, g4, u4, wg4, wu4, wd4, comm=None):
    t_tok, d = h.shape
    f = wg4.shape[-1]
    tm = _tile(t_tok, 512)

    def body(dho_ref, h_ref, ln_ref, g_ref, u_ref, wg_ref, wu_ref, wd_ref,
             dhi_ref, dln_ref, dg_ref, du_ref, a_ref, dhb_ref, dhb_s, dn_acc):
        i, j = pl.program_id(0), pl.program_id(1)

        @pl.when(j == 0)
        def _():
            dhb = (0.5 * dho_ref[...]).astype(BF)
            dhb_s[...] = dhb
            dhb_ref[...] = dhb
            dn_acc[...] = jnp.zeros_like(dn_acc)

        @pl.when((i == 0) & (j == 0))
        def _():
            dln_ref[...] = jnp.zeros_like(dln_ref)

        g = g_ref[...].astype(F32)
        u = u_ref[...].astype(F32)
        s = _sigmoid(g)
        sg = g * s
        a_ref[...] = (sg * u).astype(BF)
        da = _dot_nt(dhb_s[...], wd_ref[...])
        dg = (da * u * (s * (1.0 + g * (1.0 - s)))).astype(BF)
        du = (da * sg).astype(BF)
        dg_ref[...] = dg
        du_ref[...] = du
        dn_acc[...] += _dot_nt(dg, wg_ref[...]) + _dot_nt(du, wu_ref[...])

        @pl.when(j == N_CHIPS - 1)
        def _():
            xv = h_ref[...]
            dx, dln = _rms_bwd(dn_acc[...], xv, _rstd(xv), ln_ref[...])
            dln_ref[...] += dln
            dhi_ref[...] = dho_ref[...] + dx

    row = pl.BlockSpec((tm, d), lambda i, j: (i, 0))
    vec = pl.BlockSpec((1, d), lambda i, j: (0, 0))
    gu = pl.BlockSpec((None, tm, f), lambda i, j: (j, i, 0))
    gu_sds = jax.ShapeDtypeStruct((N_CHIPS, t_tok, f), BF)
    return _pcall(
        body, [dho, h, ln, g4, u4, wg4, wu4, wd4], name="ffn_bwd", grid=(t_tok // tm, N_CHIPS),
        out_shape=[jax.ShapeDtypeStruct((t_tok, d), F32), jax.ShapeDtypeStruct((1, d), F32),
                   gu_sds, gu_sds, gu_sds, jax.ShapeDtypeStruct((t_tok, d), BF)],
        in_specs=[row, row, vec, gu, gu, _w4_spec(d, f), _w4_spec(d, f), _w4_spec(f, d)],
        out_specs=[row, vec, gu, gu, gu, row],
        scratch=[pltpu.VMEM((tm, d), BF), pltpu.VMEM((tm, d), F32)], comm=comm)


def _rope_tables(pos_col, inv_freq2, comm=None):
    t_tok = pos_col.shape[0]

    def body(p_ref, f_ref, cos_ref, sin_ref):
        ang = p_ref[...] * f_ref[...]
        lane = lax.broadcasted_iota(jnp.int32, ang.shape, 1)
        s = jnp.sin(ang)
        cos_ref[...] = jnp.cos(ang)
        sin_ref[...] = jnp.where((lane & 1) == 0, -s, s)

    sds = jax.ShapeDtypeStruct((t_tok, 128), F32)
    return _pcall(body, [pos_col, inv_freq2], name="rope_tables", out_shape=[sds, sds], comm=comm)


def _swap_pairs(x):
    lane = lax.broadcasted_iota(jnp.int32, x.shape, 1)
    return jnp.where((lane & 1) == 0, pltpu.roll(x, 127, 1), pltpu.roll(x, 1, 1))


def _mix_in(h, ln, w_in, wm4, b_m, cos_t, sin_t, comm=None):
    t_tok, d = h.shape
    cm = wm4.shape[-1]
    tm = _tile(t_tok, 256)

    def body(h_ref, ln_ref, win_ref, wm_ref, bm_ref, cos_ref, sin_ref,
             u_ref, rq_ref, rk_ref, rv_ref, rg_ref, fq_ref, fk_ref, fv_ref, ff_ref, ga_ref, gb_ref):
        xv = h_ref[...]
        ub = (xv * _rstd(xv) * ln_ref[...]).astype(BF)
        u_ref[...] = ub
        cosv, sinv = cos_ref[...], sin_ref[...]

        def sec(k):
            return _dot(ub, win_ref[:, k * 512:(k + 1) * 512])

        def rot(xh):
            return xh * cosv + _swap_pairs(xh) * sinv

        pq, pk = sec(0), sec(1)
        for hh in range(RET_HEADS):
            sl = slice(hh * RET_DIM, (hh + 1) * RET_DIM)
            rq_ref[:, sl] = rot(pq[:, sl]).astype(BF)
            rk_ref[:, sl] = (rot(pk[:, sl]) * RET_SCALE).astype(BF)
        rv_ref[...] = sec(2).astype(BF)
        rg_ref[...] = sec(3).astype(BF)
        fq_ref[...] = (sec(4) * FOX_SCALE).astype(BF)
        fk_ref[...] = sec(5).astype(BF)
        fv_ref[...] = sec(6).astype(BF)
        ff_ref[...] = _dot(ub, win_ref[:, FF_COL:FF_COL + 128])
        for j in range(N_CHIPS):
            gs = _sigmoid(_dot(ub, wm_ref[j]) + bm_ref[:, j * cm:(j + 1) * cm]).astype(BF)
            col = j * cm
            if col < d:
                ga_ref[:, col:col + cm] = gs
            else:
                gb_ref[:, col - d:col - d + cm] = gs

    row = lambda c: pl.BlockSpec((tm, c), lambda i: (i, 0))
    full = lambda *s: pl.BlockSpec(s, lambda i: (0,) * len(s))
    sds = lambda c, dt: jax.ShapeDtypeStruct((t_tok, c), dt)
    return _pcall(
        body, [h, ln, w_in, wm4, b_m, cos_t, sin_t], name="mix_in", grid=(t_tok // tm,),
        out_shape=[sds(d, BF)] + [sds(512, BF)] * 7 + [sds(128, F32), sds(d, BF), sds(d, BF)],
        in_specs=[row(d), full(1, d), full(d, IN_PAD), full(N_CHIPS, d, cm), full(1, 2 * d), row(128), row(128)],
        out_specs=[row(d)] + [row(512)] * 7 + [row(128), row(d), row(d)], comm=comm)


def _split3(x):
    hi = x.astype(BF)
    r1 = x - hi.astype(F32)
    mid = r1.astype(BF)
    lo = (r1 - mid.astype(F32)).astype(BF)
    return hi, mid, lo


def _forget_fwd(ffl, b_pad):
    t_tok = ffl.shape[0]
    tb = _tile(t_tok, 256)

    def body(ff_ref, b_ref, cume_ref, cumt_ref, cum_s):
        r = lax.broadcasted_iota(jnp.int32, (tb, tb), 0)
        c = lax.broadcasted_iota(jnp.int32, (tb, tb), 1)
        tri = jnp.where(c <= r, 1.0, 0.0).astype(BF)
        carry = jnp.zeros((1, 128), F32)
        for i in range(t_tok // tb):
            z = ff_ref[i * tb:(i + 1) * tb, :] + b_ref[...]
            lf = jnp.minimum(z, 0.0) - jnp.log(1.0 + jnp.exp(-jnp.abs(z)))
            hi, mid, lo = _split3(lf)
            cs = _dot(tri, hi) + _dot(tri, mid) + _dot(tri, lo) + carry
            cum_s[i * tb:(i + 1) * tb, :] = cs
            carry = cs[tb - 1:tb, :]
        x = cum_s[...]
        hid = lax.shift_right_logical(lax.broadcasted_iota(jnp.int32, (t_tok, FOX_WIDTH), 1), 6)
        e = jnp.zeros((t_tok, FOX_WIDTH), F32)
        for hh in range(FOX_HEADS):
            e = jnp.where(hid == hh, x[:, hh:hh + 1], e)
        cume_ref[...] = e
        cumt_ref[...] = x.T[0:FOX_HEADS, :]

    return _pcall(
        body, [ffl, b_pad], name="forget_fwd",
        out_shape=[jax.ShapeDtypeStruct((t_tok, FOX_WIDTH), F32), jax.ShapeDtypeStruct((FOX_HEADS, t_tok), F32)],
        scratch=[pltpu.VMEM((t_tok, 128), F32)])


def _forget_bwd(dcum_t, dcum_q, ffl, b_pad):
    t_tok = ffl.shape[0]
    tb = _tile(t_tok, 256)

    def body(dc_ref, dq_ref, ff_ref, b_ref, dff_ref, db_ref, pad_s, d_s):
        pad_s[...] = jnp.zeros_like(pad_s)
        pad_s[0:FOX_HEADS, :] = dc_ref[...]
        dsum = pad_s[...].T
        lane = lax.broadcasted_iota(jnp.int32, (t_tok, 128), 1)
        for hh in range(FOX_HEADS):
            dsum = dsum + jnp.where(lane == hh, dq_ref[:, hh * FOX_DIM:hh * FOX_DIM + 1], 0.0)
        d_s[...] = dsum
        r = lax.broadcasted_iota(jnp.int32, (tb, tb), 0)
        c = lax.broadcasted_iota(jnp.int32, (tb, tb), 1)
        tri = jnp.where(c >= r, 1.0, 0.0).astype(BF)
        carry = jnp.zeros((1, 128), F32)
        db = jnp.zeros((1, 128), F32)
        for i in reversed(range(t_tok // tb)):
            hi, mid, lo = _split3(d_s[i * tb:(i + 1) * tb, :])
            dlf = _dot(tri, hi) + _dot(tri, mid) + _dot(tri, lo) + carry
            carry = dlf[0:1, :]
            z = ff_ref[i * tb:(i + 1) * tb, :] + b_ref[...]
            dff = dlf * _sigmoid(-z)
            dff_ref[i * tb:(i + 1) * tb, :] = dff.astype(BF)
            db = db + jnp.sum(dff, axis=0, keepdims=True)
        db_ref[...] = db

    return _pcall(
        body, [dcum_t, dcum_q, ffl, b_pad], name="forget_bwd",
        out_shape=[jax.ShapeDtypeStruct((t_tok, 128), BF), jax.ShapeDtypeStruct((1, 128), F32)],
        scratch=[pltpu.VMEM((128, t_tok), F32), pltpu.VMEM((t_tok, 128), F32)])


def _head_masks():
    lane = lax.broadcasted_iota(jnp.int32, (1, 128), 1)
    m0 = jnp.where(lane < FOX_DIM, 1.0, 0.0)
    return lane < FOX_DIM, [m0.astype(BF), (1.0 - m0).astype(BF)]


def _causal(s, qi, ki, t):
    rows = qi * t + lax.broadcasted_iota(jnp.int32, s.shape, 0)
    cols = ki * t + lax.broadcasted_iota(jnp.int32, s.shape, 1)
    return jnp.where(cols <= rows, s, NEG)


def _fox_fwd(fq, fk, fv, cum_e, cum_t3, comm=None):
    t_tok = fq.shape[0]
    t = _tile(t_tok, 512)
    nq = t_tok // t
    npair = FOX_HEADS // 2

    def body(q_ref, k_ref, v_ref, cq_ref, ck_ref, o_ref, of_ref, lse_ref, m_s, l_s, acc_s):
        qi, ki = pl.program_id(1), pl.program_id(2)

        @pl.when(ki == 0)
        def _():
            m_s[...] = jnp.full_like(m_s, NEG)
            l_s[...] = jnp.zeros_like(l_s)
            acc_s[...] = jnp.zeros_like(acc_s)

        @pl.when(ki <= qi)
        def _():
            _, masks = _head_masks()
            q2, k2, v2 = q_ref[...], k_ref[...], v_ref[...]
            for hh in range(2):
                s = _dot_nt(q2 * masks[hh], k2)
                s = s + cq_ref[:, hh * FOX_DIM:hh * FOX_DIM + 1] - ck_ref[hh]
                s = _causal(s, qi, ki, t)
                m_prev = m_s[hh]
                m_new = jnp.maximum(m_prev, jnp.max(s, axis=1, keepdims=True))
                alpha = jnp.exp(m_prev - m_new)
                p = jnp.exp(s - m_new)
                l_s[hh] = alpha * l_s[hh] + jnp.sum(p, axis=1, keepdims=True)
                acc_s[hh] = alpha * acc_s[hh] + _dot(p.astype(BF), v2)
                m_s[hh] = m_new

        @pl.when(ki == nq - 1)
        def _():
            first, _ = _head_masks()
            o = jnp.where(first, acc_s[0] / l_s[0], acc_s[1] / l_s[1])
            o_ref[...] = o.astype(BF)
            of_ref[...] = o
            lse_ref[...] = jnp.where(first, m_s[0] + jnp.log(l_s[0]), m_s[1] + jnp.log(l_s[1]))

    qs = pl.BlockSpec((t, 128), lambda p, qi, ki: (qi, p))
    ks = pl.BlockSpec((t, 128), lambda p, qi, ki: (jnp.minimum(ki, qi), p))
    cks = pl.BlockSpec((2, 1, t), lambda p, qi, ki: (p, 0, jnp.minimum(ki, qi)))
    return _pcall(
        body, [fq, fk, fv, cum_e, cum_t3], name="fox_fwd", grid=(npair, nq, nq),
        out_shape=[jax.ShapeDtypeStruct((t_tok, FOX_WIDTH), BF), jax.ShapeDtypeStruct((t_tok, FOX_WIDTH), F32),
                   jax.ShapeDtypeStruct((t_tok, FOX_WIDTH), F32)],
        in_specs=[qs, ks, ks, qs, cks], out_specs=[qs, qs, qs],
        scratch=[pltpu.VMEM((2, t, 1), F32), pltpu.VMEM((2, t, 1), F32), pltpu.VMEM((2, t, 128), F32)], comm=comm)


def _fox_scores(q2m, k2, cq, ck, lse, qi, ki, t):
    s = _dot_nt(q2m, k2) + cq - ck
    return jnp.exp(_causal(s, qi, ki, t) - lse)


def _fox_bwd_kv(fq, fk, fv, do, cum_e, cum_t3, lse_e, delta_e, comm=None):
    t_tok = fq.shape[0]
    t = _tile(t_tok, 512)
    nq = t_tok // t
    npair = FOX_HEADS // 2

    def body(q_ref, k_ref, v_ref, do_ref, cq_ref, ck_ref, lse_ref, dl_ref, dk_ref, dv_ref, dck_ref, dk_s, dv_s):
        ki, qi = pl.program_id(1), pl.program_id(2)

        @pl.when(qi == 0)
        def _():
            dk_s[...] = jnp.zeros_like(dk_s)
            dv_s[...] = jnp.zeros_like(dv_s)
            dck_ref[...] = jnp.zeros_like(dck_ref)

        @pl.when(qi >= ki)
        def _():
            _, masks = _head_masks()
            q2, k2, v2, do2 = q_ref[...], k_ref[...], v_ref[...], do_ref[...]
            for hh in range(2):
                c0 = hh * FOX_DIM
                qm, dom = q2 * masks[hh], do2 * masks[hh]
                p = _fox_scores(qm, k2, cq_ref[:, c0:c0 + 1], ck_ref[hh], lse_ref[:, c0:c0 + 1], qi, ki, t)
                dp = _dot_nt(dom, v2)
                ds = p * (dp - dl_ref[:, c0:c0 + 1])
                dv_s[...] += _dot_tn(p.astype(BF), dom)
                dk_s[...] += _dot_tn(ds.astype(BF), qm)
                dck_ref[hh] = dck_ref[hh] - jnp.sum(ds, axis=0, keepdims=True)

        @pl.when(qi == nq - 1)
        def _():
            dk_ref[...] = dk_s[...].astype(BF)
            dv_ref[...] = dv_s[...].astype(BF)

    qs = pl.BlockSpec((t, 128), lambda p, ki, qi: (jnp.maximum(qi, ki), p))
    ks = pl.BlockSpec((t, 128), lambda p, ki, qi: (ki, p))
    cks = pl.BlockSpec((2, 1, t), lambda p, ki, qi: (p, 0, ki))
    sds = jax.ShapeDtypeStruct((t_tok, FOX_WIDTH), BF)
    return _pcall(
        body, [fq, fk, fv, do, cum_e, cum_t3, lse_e, delta_e], name="fox_bwd_kv", grid=(npair, nq, nq),
        out_shape=[sds, sds, jax.ShapeDtypeStruct((FOX_HEADS, 1, t_tok), F32)],
        in_specs=[qs, ks, ks, qs, qs, cks, qs, qs], out_specs=[ks, ks, cks],
        scratch=[pltpu.VMEM((t, 128), F32), pltpu.VMEM((t, 128), F32)], comm=comm)


def _fox_bwd_q(fq, fk, fv, do, cum_e, cum_t3, lse_e, delta_e, comm=None):
    t_tok = fq.shape[0]
    t = _tile(t_tok, 512)
    nq = t_tok // t
    npair = FOX_HEADS // 2

    def body(q_ref, k_ref, v_ref, do_ref, cq_ref, ck_ref, lse_ref, dl_ref, dq_ref, dcq_ref, dq_s, rs_s):
        qi, ki = pl.program_id(1), pl.program_id(2)

        @pl.when(ki == 0)
        def _():
            dq_s[...] = jnp.zeros_like(dq_s)
            rs_s[...] = jnp.zeros_like(rs_s)

        @pl.when(ki <= qi)
        def _():
            first, masks = _head_masks()
            q2, k2, v2, do2 = q_ref[...], k_ref[...], v_ref[...], do_ref[...]
            dq = []
            for hh in range(2):
                c0 = hh * FOX_DIM
                p = _fox_scores(q2 * masks[hh], k2, cq_ref[:, c0:c0 + 1], ck_ref[hh],
                                lse_ref[:, c0:c0 + 1], qi, ki, t)
                dp = _dot_nt(do2 * masks[hh], v2)
                ds = p * (dp - dl_ref[:, c0:c0 + 1])
                dq.append(_dot(ds.astype(BF), k2))
                rs_s[hh] = rs_s[hh] + jnp.sum(ds, axis=1, keepdims=True)
            dq_s[...] += jnp.where(first, dq[0], dq[1])

        @pl.when(ki == nq - 1)
        def _():
            dq_ref[...] = (dq_s[...] * FOX_SCALE).astype(BF)
            first, _ = _head_masks()
            dcq_ref[...] = jnp.where(first, rs_s[0], rs_s[1])

    qs = pl.BlockSpec((t, 128), lambda p, qi, ki: (qi, p))
    ks = pl.BlockSpec((t, 128), lambda p, qi, ki: (jnp.minimum(ki, qi), p))
    cks = pl.BlockSpec((2, 1, t), lambda p, qi, ki: (p, 0, jnp.minimum(ki, qi)))
    return _pcall(
        body, [fq, fk, fv, do, cum_e, cum_t3, lse_e, delta_e], name="fox_bwd_q", grid=(npair, nq, nq),
        out_shape=[jax.ShapeDtypeStruct((t_tok, FOX_WIDTH), BF), jax.ShapeDtypeStruct((t_tok, FOX_WIDTH), F32)],
        in_specs=[qs, ks, ks, qs, qs, cks, qs, qs], out_specs=[qs, qs],
        scratch=[pltpu.VMEM((t, 128), F32), pltpu.VMEM((2, t, 1), F32)], comm=comm)


def _ret_consts():
    c = RET_CHUNK
    log_gamma = jnp.log1p(-jnp.exp2(-5.0 - jnp.arange(RET_HEADS, dtype=F32)))
    idx = jnp.arange(c, dtype=F32)
    diff = idx[:, None] - idx[None, :]
    dmask = jnp.where(diff >= 0, jnp.exp(log_gamma[:, None, None] * jnp.maximum(diff, 0.0)), 0.0)
    qdec = jnp.exp(log_gamma[:, None] * (idx + 1.0))
    kdec = jnp.exp(log_gamma[:, None] * (c - 1 - idx))
    cdec = jnp.exp(log_gamma * c)
    bc = lambda v: jnp.broadcast_to(v[:, :, None], (RET_HEADS, c, RET_DIM))
    return dmask, bc(qdec), bc(kdec), jnp.broadcast_to(cdec[:, None, None], (RET_HEADS, c, RET_DIM))


def _group_norm(y):
    mu = jnp.mean(y, axis=-1, keepdims=True)
    yc = y - mu
    r = lax.rsqrt(jnp.mean(yc * yc, axis=-1, keepdims=True) + EPS)
    return yc * r, r


def _ret_fwd(rq, rk, rv, rg, consts, comm=None):
    t_tok = rq.shape[0]
    nb = 4 if t_tok % (4 * RET_CHUNK) == 0 else 1
    tr = nb * RET_CHUNK
    n_steps = t_tok // tr
    c = RET_CHUNK

    def body(q_ref, k_ref, v_ref, g_ref, dm_ref, qd_ref, kd_ref, cd_ref, y_ref, yo_ref, st_ref, s_s):
        @pl.when(pl.program_id(1) == 0)
        def _():
            s_s[...] = jnp.zeros_like(s_s)

        dm, qd, kd, cd = dm_ref[...], qd_ref[...], kd_ref[...], cd_ref[...]
        for b in range(nb):
            rows = slice(b * c, (b + 1) * c)
            q, k, v = q_ref[rows, :], k_ref[rows, :], v_ref[rows, :]
            state = s_s[...]
            st_ref[b] = state
            sc = (_dot_nt(q, k) * dm).astype(BF)
            y = _dot(sc, v) + _dot((q.astype(F32) * qd).astype(BF), state.astype(BF))
            s_s[...] = cd * state + _dot_tn((k.astype(F32) * kd).astype(BF), v)
            y_ref[rows, :] = y
            yn, _ = _group_norm(y)
            gate = g_ref[rows, :].astype(F32)
            yo_ref[rows, :] = (yn * (gate * _sigmoid(gate))).astype(BF)

    blk = pl.BlockSpec((tr, RET_DIM), lambda h, i: (i, h))
    cst = pl.BlockSpec((None, c, RET_DIM), lambda h, i: (h, 0, 0))
    return _pcall(
        body, [rq, rk, rv, rg, *consts], name="ret_fwd", grid=(RET_HEADS, n_steps),
        out_shape=[jax.ShapeDtypeStruct((t_tok, RET_WIDTH), F32), jax.ShapeDtypeStruct((t_tok, RET_WIDTH), BF),
                   jax.ShapeDtypeStruct((RET_HEADS, t_tok // c, RET_DIM, RET_DIM), F32)],
        in_specs=[blk] * 4 + [cst] * 4,
        out_specs=[blk, blk, pl.BlockSpec((None, nb, RET_DIM, RET_DIM), lambda h, i: (h, i, 0, 0))],
        scratch=[pltpu.VMEM((RET_DIM, RET_DIM), F32)], comm=comm)


def _ret_bwd(rq, rk, rv, rg, y_raw, dyo, states, consts, cos_t, sin_t, comm=None):
    t_tok = rq.shape[0]
    nb = 4 if t_tok % (4 * RET_CHUNK) == 0 else 1
    tr = nb * RET_CHUNK
    n_steps = t_tok // tr
    c = RET_CHUNK

    def body(q_ref, k_ref, v_ref, g_ref, y_ref, dyo_ref, st_ref, dm_ref, qd_ref, kd_ref, cd_ref,
             cos_ref, sin_ref, dq_ref, dk_ref, dv_ref, dg_ref, ds_s):
        @pl.when(pl.program_id(1) == 0)
        def _():
            ds_s[...] = jnp.zeros_like(ds_s)

        dm, qd, kd, cd = dm_ref[...], qd_ref[...], kd_ref[...], cd_ref[...]
        for b in reversed(range(nb)):
            rows = slice(b * c, (b + 1) * c)
            q, k, v = q_ref[rows, :], k_ref[rows, :], v_ref[rows, :]
            cosv, sinv = cos_ref[rows, :], sin_ref[rows, :]
            yn, r = _group_norm(y_ref[rows, :])
            gate = g_ref[rows, :].astype(F32)
            sg = _sigmoid(gate)
            dyo = dyo_ref[rows, :]
            dg_ref[rows, :] = (dyo * yn * (sg * (1.0 + gate * (1.0 - sg)))).astype(BF)
            dyn = dyo * (gate * sg)
            dy = r * (dyn - jnp.mean(dyn, axis=-1, keepdims=True)
                      - yn * jnp.mean(dyn * yn, axis=-1, keepdims=True))
            dyb = dy.astype(BF)
            state_b = st_ref[b].astype(BF)
            dstate = ds_s[...]
            dstate_b = dstate.astype(BF)
            qdb = (q.astype(F32) * qd).astype(BF)
            kdb = (k.astype(F32) * kd).astype(BF)
            sc = (_dot_nt(q, k) * dm).astype(BF)
            dv = _dot_tn(sc, dyb) + _dot(kdb, dstate_b)
            dp = (_dot_nt(dyb, v) * dm).astype(BF)
            dq = _dot(dp, k) + _dot_nt(dyb, state_b) * qd
            dk = (_dot_tn(dp, q) + _dot_nt(v, dstate_b) * kd) * RET_SCALE
            ds_s[...] = cd * dstate + _dot_tn(qdb, dyb)
            dv_ref[rows, :] = dv.astype(BF)
            dq_ref[rows, :] = (dq * cosv - _swap_pairs(dq) * sinv).astype(BF)
            dk_ref[rows, :] = (dk * cosv - _swap_pairs(dk) * sinv).astype(BF)

    rev = lambda i: n_steps - 1 - i
    blk = pl.BlockSpec((tr, RET_DIM), lambda h, i: (rev(i), h))
    tab = pl.BlockSpec((tr, RET_DIM), lambda h, i: (rev(i), 0))
    cst = pl.BlockSpec((None, c, RET_DIM), lambda h, i: (h, 0, 0))
    sds = jax.ShapeDtypeStruct((t_tok, RET_WIDTH), BF)
    return _pcall(
        body, [rq, rk, rv, rg, y_raw, dyo, states, *consts, cos_t, sin_t], name="ret_bwd",
        grid=(RET_HEADS, n_steps), out_shape=[sds] * 4,
        in_specs=[blk] * 6 + [pl.BlockSpec((None, nb, RET_DIM, RET_DIM), lambda h, i: (h, rev(i), 0, 0))]
        + [cst] * 4 + [tab, tab],
        out_specs=[blk] * 4, scratch=[pltpu.VMEM((RET_DIM, RET_DIM), F32)], comm=comm)


def _mix_out(h, y_ret, y_fox, ga, gb, wr4, wf4, wo4, comm=None):
    t_tok, d = h.shape
    cz = wr4.shape[-1]
    ro = wo4.shape[-2]
    tm = _tile(t_tok, 512)

    def body(h_ref, yr_ref, yf_ref, ga_ref, gb_ref, wr_ref, wf_ref, wo_ref, ho_ref, za_ref, zb_ref, mix_ref):
        yr, yf = yr_ref[...], yf_ref[...]
        for j in range(N_CHIPS):
            sl = slice(j * cz, (j + 1) * cz)
            za = _dot(yr, wr_ref[j])
            zb = _dot(yf, wf_ref[j])
            za_ref[:, sl] = za.astype(BF)
            zb_ref[:, sl] = zb.astype(BF)
            mix_ref[:, sl] = (ga_ref[:, sl].astype(F32) * za + gb_ref[:, sl].astype(F32) * zb).astype(BF)
        acc = h_ref[...]
        for j in range(N_CHIPS):
            acc = acc + _dot(mix_ref[:, j * ro:(j + 1) * ro], wo_ref[j])
        ho_ref[...] = acc

    row = lambda c: pl.BlockSpec((tm, c), lambda i: (i, 0))
    full = lambda *s: pl.BlockSpec(s, lambda i: (0,) * len(s))
    sds = lambda dt: jax.ShapeDtypeStruct((t_tok, d), dt)
    return _pcall(
        body, [h, y_ret, y_fox, ga, gb, wr4, wf4, wo4], name="mix_out", grid=(t_tok // tm,),
        out_shape=[sds(F32), sds(BF), sds(BF), sds(BF)],
        in_specs=[row(d), row(RET_WIDTH), row(FOX_WIDTH), row(d), row(d),
                  full(N_CHIPS, RET_WIDTH, cz), full(N_CHIPS, FOX_WIDTH, cz), full(N_CHIPS, ro, d)],
        out_specs=[row(d)] * 4, comm=comm)


def _mix_out_bwd(dh, za, zb, ga, gb, y_fox, wr4, wf4, wo4, comm=None):
    t_tok, d = dh.shape
    cz = wr4.shape[-1]
    ro = wo4.shape[-2]
    tm = _tile(t_tok, 256)

    def body(dh_ref, za_ref, zb_ref, ga_ref, gb_ref, yf_ref, wr_ref, wf_ref, wo_ref,
             dhb_ref, dgp_ref, dza_ref, dzb_ref, dyr_ref, dyf_ref, dl_ref, db_ref):
        @pl.when(pl.program_id(0) == 0)
        def _():
            db_ref[...] = jnp.zeros_like(db_ref)

        dhb = dh_ref[...].astype(BF)
        dhb_ref[...] = dhb
        dyr = jnp.zeros((tm, RET_WIDTH), F32)
        dyf = jnp.zeros((tm, FOX_WIDTH), F32)
        for j in range(N_CHIPS):
            sl = slice(j * ro, (j + 1) * ro)
            dmix = _dot_nt(dhb, wo_ref[j])
            ga, gb = ga_ref[:, sl].astype(F32), gb_ref[:, sl].astype(F32)
            dza = (dmix * ga).astype(BF)
            dzb = (dmix * gb).astype(BF)
            dza_ref[:, sl] = dza
            dzb_ref[:, sl] = dzb
            dga = dmix * za_ref[:, sl].astype(F32) * ga * (1.0 - ga)
            dgb = dmix * zb_ref[:, sl].astype(F32) * gb * (1.0 - gb)
            dgp_ref[:, sl] = dga.astype(BF)
            dgp_ref[:, d + j * ro:d + (j + 1) * ro] = dgb.astype(BF)
            db_ref[:, sl] += jnp.sum(dga, axis=0, keepdims=True)
            db_ref[:, d + j * ro:d + (j + 1) * ro] += jnp.sum(dgb, axis=0, keepdims=True)
        for j in range(N_CHIPS):
            sl = slice(j * cz, (j + 1) * cz)
            dyr = dyr + _dot_nt(dza_ref[:, sl], wr_ref[j])
            dyf = dyf + _dot_nt(dzb_ref[:, sl], wf_ref[j])
        dyr_ref[...] = dyr
        dyfb = dyf.astype(BF)
        dyf_ref[...] = dyfb
        prod = dyfb.astype(F32) * yf_ref[...]
        for pp in range(FOX_HEADS // 2):
            blk = prod[:, pp * 128:(pp + 1) * 128]
            first, _ = _head_masks()
            s0 = jnp.sum(jnp.where(first, blk, 0.0), axis=1, keepdims=True)
            s1 = jnp.sum(jnp.where(first, 0.0, blk), axis=1, keepdims=True)
            dl_ref[:, pp * 128:(pp + 1) * 128] = jnp.where(first, s0, s1)

    row = lambda c: pl.BlockSpec((tm, c), lambda i: (i, 0))
    full = lambda *s: pl.BlockSpec(s, lambda i: (0,) * len(s))
    sds = lambda c, dt: jax.ShapeDtypeStruct((t_tok, c), dt)
    return _pcall(
        body, [dh, za, zb, ga, gb, y_fox, wr4, wf4, wo4], name="mix_out_bwd", grid=(t_tok // tm,),
        out_shape=[sds(d, BF), sds(2 * d, BF), sds(d, BF), sds(d, BF), sds(RET_WIDTH, F32),
                   sds(FOX_WIDTH, BF), sds(FOX_WIDTH, F32), jax.ShapeDtypeStruct((1, 2 * d), F32)],
        in_specs=[row(d)] * 5 + [row(FOX_WIDTH), full(N_CHIPS, RET_WIDTH, cz), full(N_CHIPS, FOX_WIDTH, cz),
                                 full(N_CHIPS, ro, d)],
        out_specs=[row(d), row(2 * d), row(d), row(d), row(RET_WIDTH), row(FOX_WIDTH), row(FOX_WIDTH),
                   full(1, 2 * d)],
        comm=comm)


def _mix_in_bwd(dh, h, ln, parts, dff, dgpre, w_in, wm4, comm=None):
    t_tok, d = h.shape
    cm = wm4.shape[-1]
    tm = _tile(t_tok, 256)

    def body(dh_ref, h_ref, ln_ref, p0, p1, p2, p3, p4, p5, p6, dff_ref, dgp_ref, win_ref, wm_ref,
             dhi_ref, dln_ref, dproj_ref):
        @pl.when(pl.program_id(0) == 0)
        def _():
            dln_ref[...] = jnp.zeros_like(dln_ref)

        for k, pr in enumerate((p0, p1, p2, p3, p4, p5, p6)):
            dproj_ref[:, k * 512:(k + 1) * 512] = pr[...]
        dproj_ref[:, FF_COL:FF_COL + 128] = dff_ref[...]
        dproj_ref[:, FF_COL + 128:] = jnp.zeros((tm, IN_PAD - FF_COL - 128), BF)
        du = _dot_nt(dproj_ref[...], win_ref[...])
        for j in range(N_CHIPS):
            du = du + _dot_nt(dgp_ref[:, j * cm:(j + 1) * cm], wm_ref[j])
        xv = h_ref[...]
        dx, dln = _rms_bwd(du, xv, _rstd(xv), ln_ref[...])
        dln_ref[...] += dln
        dhi_ref[...] = dh_ref[...] + dx

    row = lambda c: pl.BlockSpec((tm, c), lambda i: (i, 0))
    full = lambda *s: pl.BlockSpec(s, lambda i: (0,) * len(s))
    return _pcall(
        body, [dh, h, ln, *parts, dff, dgpre, w_in, wm4], name="mix_in_bwd", grid=(t_tok // tm,),
        out_shape=[jax.ShapeDtypeStruct((t_tok, d), F32), jax.ShapeDtypeStruct((1, d), F32),
                   jax.ShapeDtypeStruct((t_tok, IN_PAD), BF)],
        in_specs=[row(d), row(d), full(1, d)] + [row(512)] * 7 + [row(128), row(2 * d), full(d, IN_PAD),
                                                                   full(N_CHIPS, d, cm)],
        out_specs=[row(d), full(1, d), row(IN_PAD)], comm=comm)


def _tail(h, p, target, ln_ple, ln_fin, wpg4, wpl4, comm=None):
    t_tok, d = h.shape
    pd = p.shape[1]
    rg = wpg4.shape[-2]
    cp = wpl4.shape[-1]
    tm = _tile(t_tok, 256)

    def body(h_ref, p_ref, t_ref, lp_ref, lf_ref, wg_ref, wp_ref,
             dh_ref, n_ref, dgp_ref, dpe_ref, pb_ref, loss_ref, dlf_ref, dlp_ref, pe_s, dn_s):
        @pl.when(pl.program_id(0) == 0)
        def _():
            loss_ref[...] = jnp.zeros_like(loss_ref)
            dlf_ref[...] = jnp.zeros_like(dlf_ref)
            dlp_ref[...] = jnp.zeros_like(dlp_ref)

        xv = h_ref[...]
        r3 = _rstd(xv)
        nb = (xv * r3 * lp_ref[...]).astype(BF)
        n_ref[...] = nb
        pb = p_ref[...].astype(BF)
        pb_ref[...] = pb
        pgpre = jnp.zeros((tm, d), F32)
        for j in range(N_CHIPS):
            pgpre = pgpre + _dot(nb[:, j * rg:(j + 1) * rg], wg_ref[j])
            pe_s[:, j * cp:(j + 1) * cp] = _dot(pb, wp_ref[j])
        pg = _sigmoid(pgpre)
        pe = pe_s[...]
        h4 = xv + pg * pe
        r4 = _rstd(h4)
        err = h4 * r4 * lf_ref[...] - t_ref[...]
        loss_ref[...] += 0.5 * jnp.sum(jnp.sum(err * err, axis=1, keepdims=True), axis=0, keepdims=True) / d
        dh4, dlf = _rms_bwd(err * (1.0 / d), h4, r4, lf_ref[...])
        dlf_ref[...] += dlf
        dpe_ref[...] = (dh4 * pg).astype(BF)
        dgp = (dh4 * pe * pg * (1.0 - pg)).astype(BF)
        dgp_ref[...] = dgp
        for j in range(N_CHIPS):
            dn_s[:, j * rg:(j + 1) * rg] = _dot_nt(dgp, wg_ref[j])
        dx, dlp = _rms_bwd(dn_s[...], xv, r3, lp_ref[...])
        dlp_ref[...] += dlp
        dh_ref[...] = dh4 + dx

    row = lambda c: pl.BlockSpec((tm, c), lambda i: (i, 0))
    full = lambda *s: pl.BlockSpec(s, lambda i: (0,) * len(s))
    sds = lambda c, dt: jax.ShapeDtypeStruct((t_tok, c), dt)
    vec = jax.ShapeDtypeStruct((1, d), F32)
    return _pcall(
        body, [h, p, target, ln_ple, ln_fin, wpg4, wpl4], name="tail", grid=(t_tok // tm,),
        out_shape=[sds(d, F32), sds(d, BF), sds(d, BF), sds(d, BF), sds(pd, BF),
                   jax.ShapeDtypeStruct((1, 128), F32), vec, vec],
        in_specs=[row(d), row(pd), row(d), full(1, d), full(1, d), full(N_CHIPS, rg, d), full(N_CHIPS, pd, cp)],
        out_specs=[row(d), row(d), row(d), row(d), row(pd), full(1, 128), full(1, d), full(1, d)],
        scratch=[pltpu.VMEM((tm, d), F32), pltpu.VMEM((tm, d), F32)], comm=comm)


BIG = ["w_ffn1_gate", "w_ffn1_up", "w_ffn1_down", "w_in", "w_merge", "w_ret_out", "w_fox_out", "w_out",
       "w_ffn2_gate", "w_ffn2_up", "w_ffn2_down", "w_ple", "w_ple_gate"]
SMALL = ["ln_ffn1", "ln_mix", "b_forget", "b_merge", "ln_ffn2", "ln_ple", "ln_final"]
WEIGHTS = ["ln_ffn1", "w_ffn1_gate", "w_ffn1_up", "w_ffn1_down", "ln_mix", "w_in", "b_forget", "w_merge", "b_merge",
           "w_ret_out", "w_fox_out", "w_out", "ln_ffn2", "w_ffn2_gate", "w_ffn2_up", "w_ffn2_down", "ln_ple",
           "w_ple", "w_ple_gate", "ln_final"]


def _pack_small(vals):
    rows = []
    for name in SMALL:
        v = vals[name].reshape(-1)
        n = -(-v.shape[0] // 128) * 128
        rows.append(jnp.pad(v, (0, n - v.shape[0])).reshape(n // 128, 128))
    packed = jnp.concatenate(rows, axis=0)
    pad = -packed.shape[0] % 8
    return jnp.pad(packed, ((0, pad), (0, 0)))


def _unpack_small(packed, sizes):
    out, r = {}, 0
    for name in SMALL:
        n = sizes[name]
        nr = -(-n // 128)
        out[name] = packed[r:r + nr].reshape(1, nr * 128)[:, :n]
        r += nr
    return out


class _Stage:
    def __init__(self, comm, finish):
        self.comm, self.finish, self.result = comm, finish, None


def _hosted(fn, *a, stages=()):
    if not stages:
        return fn(*a)
    outs, couts = fn(*a, comm=_merge([st.comm for st in stages]))
    for st, o in zip(stages, _split_outs([st.comm for st in stages], couts)):
        st.result = st.finish(o)
    return outs


class _Reducer:
    def __init__(self):
        self.done = {}

    def swap(self, grads):
        names = list(grads)
        return _Stage(_c_half_swap([grads[n] for n in names]),
                      lambda outs: {n: _add_halves(grads[n], o) for n, o in zip(names, outs)})

    def exchange(self, parts):
        names = list(parts)
        return _Stage(_c_chip_exchange([parts[n] for n in names]),
                      lambda outs: {n: _sum_chips(o) for n, o in zip(names, outs)})

    def join(self, halves):
        names = list(halves)
        return _Stage(_c_join([halves[n] for n in names]), lambda outs: self.done.update(zip(names, outs)))


def kernel(x, p, positions, ln_ffn1, w_ffn1_gate, w_ffn1_up, w_ffn1_down, ln_mix, w_in, b_forget, w_merge, b_merge, w_ret_out, w_fox_out, w_out, ln_ffn2, w_ffn2_gate, w_ffn2_up, w_ffn2_down, ln_ple, w_ple, w_ple_gate, ln_final, loss_target, m_ln_ffn1, m_w_ffn1_gate, m_w_ffn1_up, m_w_ffn1_down, m_ln_mix, m_w_in, m_b_forget, m_w_merge, m_b_merge, m_w_ret_out, m_w_fox_out, m_w_out, m_ln_ffn2, m_w_ffn2_gate, m_w_ffn2_up, m_w_ffn2_down, m_ln_ple, m_w_ple, m_w_ple_gate, m_ln_final, v_ln_ffn1, v_w_ffn1_gate, v_w_ffn1_up, v_w_ffn1_down, v_ln_mix, v_w_in, v_b_forget, v_w_merge, v_b_merge, v_w_ret_out, v_w_fox_out, v_w_out, v_ln_ffn2, v_w_ffn2_gate, v_w_ffn2_up, v_w_ffn2_down, v_ln_ple, v_w_ple, v_w_ple_gate, v_ln_final):
    args = dict(locals())
    w = {n: args[n] for n in WEIGHTS}
    m = {n: args["m_" + n] for n in WEIGHTS}
    v = {n: args["v_" + n] for n in WEIGHTS}
    d = x.shape[-1]
    t_tok = x.shape[1]
    xs, ps, target = x[0], p[0, 0], loss_target[0]
    small = {n: w[n].reshape(1, -1) for n in SMALL}
    shard = {n: w[n][0].astype(BF) for n in BIG}
    full = {}

    def gather(names):
        return _Stage(_c_all_gather([shard[n] for n in names]), lambda outs: full.update(zip(names, outs)))

    half = RET_DIM // 2
    inv_freq = 1.0 / (ROPE_BASE ** (jnp.arange(half, dtype=F32) / half))
    cos_t, sin_t = _hosted(_rope_tables, positions[0].astype(F32).reshape(t_tok, 1),
                           jnp.repeat(inv_freq, 2).reshape(1, RET_DIM),
                           stages=[gather(["w_ffn1_gate", "w_ffn1_up", "w_ffn1_down"])])
    consts = _ret_consts()
    b_pad = jnp.pad(small["b_forget"], ((0, 0), (0, 128 - FOX_HEADS)))

    h1, n1, g1, u1 = _hosted(_ffn_fwd, xs, small["ln_ffn1"], full["w_ffn1_gate"], full["w_ffn1_up"],
                             full["w_ffn1_down"], stages=[gather(["w_in", "w_merge"])])
    w_in_full = jnp.pad(jnp.transpose(full["w_in"], (1, 0, 2)).reshape(d, IN_COLS), ((0, 0), (0, IN_PAD - IN_COLS)))
    u, rq, rk, rv, rg, fq, fk, fv, ffl, ga, gb = _hosted(
        _mix_in, h1, small["ln_mix"], w_in_full, full["w_merge"], small["b_merge"], cos_t, sin_t,
        stages=[gather(["w_ret_out", "w_fox_out", "w_out", "w_ple_gate", "w_ple"])])
    cum_e, cum_t = _forget_fwd(ffl, b_pad)
    cum_t3 = cum_t.reshape(FOX_HEADS, 1, t_tok)
    y_raw, y_ret, states = _ret_fwd(rq, rk, rv, rg, consts)
    y_fox, y_fox32, lse_e = _hosted(_fox_fwd, fq, fk, fv, cum_e, cum_t3,
                                    stages=[gather(["w_ffn2_gate", "w_ffn2_up", "w_ffn2_down"])])
    h2, za, zb, mix = _mix_out(h1, y_ret, y_fox, ga, gb, full["w_ret_out"], full["w_fox_out"], full["w_out"])
    h3, n2, g2, u2 = _ffn_fwd(h2, small["ln_ffn2"], full["w_ffn2_gate"], full["w_ffn2_up"], full["w_ffn2_down"])

    red = _Reducer()
    dh3, n3, dpgpre, dpe, pb, loss, dln_final, dln_ple = _tail(
        h3, ps, target, small["ln_ple"], small["ln_final"], full["w_ple_gate"], full["w_ple"])
    g_ple = dict(w_ple_gate=_wgrad_rows("wgrad_ple_gate", n3, dpgpre, N_CHIPS),
                 w_ple=_wgrad_cols("wgrad_ple", pb, dpe, N_CHIPS))

    sw_ple = red.swap(g_ple)
    dh2, dln_ffn2, dg2, du2, a2, dhb3 = _hosted(
        _ffn_bwd, dh3, h2, small["ln_ffn2"], g2, u2, full["w_ffn2_gate"], full["w_ffn2_up"], full["w_ffn2_down"],
        stages=[sw_ple])
    ex_ple = red.exchange(sw_ple.result)
    g_f2 = dict(w_ffn2_gate=_hosted(_wgrad_a_shared, "wgrad_ffn2_gate", n2, dg2, stages=[ex_ple]))
    g_f2["w_ffn2_up"] = _wgrad_a_shared("wgrad_ffn2_up", n2, du2)
    g_f2["w_ffn2_down"] = _wgrad_b_shared("wgrad_ffn2_down", a2, dhb3)

    sw_f2 = red.swap(g_f2)
    dhb2, dgpre, dza, dzb, dy_ret, dy_fox, delta_e, db_merge = _hosted(
        _mix_out_bwd, dh2, za, zb, ga, gb, y_fox32, full["w_ret_out"], full["w_fox_out"], full["w_out"],
        stages=[sw_f2, red.join(ex_ple.result)])
    g_br = dict(w_out=_wgrad_rows("wgrad_out", mix, dhb2, N_CHIPS),
                w_ret_out=_wgrad_cols("wgrad_ret_out", y_ret, dza, N_CHIPS),
                w_fox_out=_wgrad_cols("wgrad_fox_out", y_fox, dzb, N_CHIPS))

    sw_br = red.swap(g_br)
    drq, drk, drv, drg = _hosted(_ret_bwd, rq, rk, rv, rg, y_raw, dy_ret, states, consts, cos_t, sin_t,
                                 stages=[sw_br])
    ex_f2, ex_br = red.exchange(sw_f2.result), red.exchange(sw_br.result)
    dfk, dfv, dcum_t3 = _hosted(_fox_bwd_kv, fq, fk, fv, dy_fox, cum_e, cum_t3, lse_e, delta_e,
                                stages=[ex_f2, ex_br])
    dfq, dcum_q = _hosted(_fox_bwd_q, fq, fk, fv, dy_fox, cum_e, cum_t3, lse_e, delta_e,
                          stages=[red.join(ex_f2.result), red.join(ex_br.result)])
    dff, db_forget = _forget_bwd(dcum_t3.reshape(FOX_HEADS, t_tok), dcum_q, ffl, b_pad)
    dh1, dln_mix, dproj = _mix_in_bwd(dh2, h1, small["ln_mix"], (drq, drk, drv, drg, dfq, dfk, dfv), dff, dgpre,
                                      w_in_full, full["w_merge"])

    g_in = _wgrad_cols("wgrad_in", u, dproj, IN_PAD // 512)
    g_in = jnp.transpose(g_in, (1, 0, 2)).reshape(d, IN_PAD)[:, :IN_COLS]
    g_in = jnp.transpose(g_in.reshape(d, N_CHIPS, IN_COLS // N_CHIPS), (1, 0, 2))
    sw_in = red.swap(dict(w_in=g_in))
    g_mrg = _hosted(_wgrad_cols, "wgrad_merge", u, dgpre, N_CHIPS, stages=[sw_in])

    sw_mrg, ex_in = red.swap(dict(w_merge=g_mrg)), red.exchange(sw_in.result)
    dx, dln_ffn1, dg1, du1, a1, dhb1 = _hosted(
        _ffn_bwd, dh1, xs, small["ln_ffn1"], g1, u1, full["w_ffn1_gate"], full["w_ffn1_up"], full["w_ffn1_down"],
        stages=[sw_mrg, ex_in])

    ex_mrg = red.exchange(sw_mrg.result)
    g_f1g = _hosted(_wgrad_a_shared, "wgrad_ffn1_gate", n1, dg1, stages=[ex_mrg, red.join(ex_in.result)])
    sw_f1g = red.swap(dict(w_ffn1_gate=g_f1g))
    g_f1u = _hosted(_wgrad_a_shared, "wgrad_ffn1_up", n1, du1, stages=[sw_f1g])
    ex_f1g, sw_f1u = red.exchange(sw_f1g.result), red.swap(dict(w_ffn1_up=g_f1u))
    g_f1d = _hosted(_wgrad_b_shared, "wgrad_ffn1_down", a1, dhb1,
                    stages=[ex_f1g, sw_f1u, red.join(ex_mrg.result)])

    small_grads = dict(ln_ffn1=dln_ffn1, ln_mix=dln_mix, b_forget=db_forget[:, :FOX_HEADS], b_merge=db_merge,
                       ln_ffn2=dln_ffn2, ln_ple=dln_ple, ln_final=dln_final)
    sizes = {n: w[n].size for n in SMALL}
    gsum = _unpack_small(_all_reduce_small(_pack_small(small_grads)), sizes)
    loss = lax.psum(loss[0, 0], ("x", "y", "c"))

    results = {}

    def update(n, stages=()):
        shp = w[n].shape
        two_d = (shp[-2], shp[-1]) if len(shp) == 3 else (1, shp[-1])
        g2d = (gsum[n] if n in gsum else red.done[n]).reshape(two_d)
        dl, nm, nv = _hosted(_adamw, w[n].reshape(two_d), g2d, m[n].reshape(two_d), v[n].reshape(two_d),
                             stages=stages)
        results[n] = tuple(a.reshape(shp) for a in (g2d, dl, nm, nv))

    ex_f1u, sw_f1d = red.exchange(sw_f1u.result), red.swap(dict(w_ffn1_down=g_f1d))
    update("w_ffn2_gate", stages=[ex_f1u, sw_f1d, red.join(ex_f1g.result)])
    ex_f1d = red.exchange(sw_f1d.result)
    update("w_ffn2_up", stages=[ex_f1d, red.join(ex_f1u.result)])
    update("w_ffn2_down", stages=[red.join(ex_f1d.result)])
    for n in WEIGHTS:
        if n not in results:
            update(n)

    outs = [[results[n][k] for n in WEIGHTS] for k in range(4)]
    return (loss, dx[None], *outs[0], *outs[1], *outs[2], *outs[3])
```

```python
import functools
import operator

import jax
import jax.numpy as jnp
from jax import lax
from jax.experimental import pallas as pl
from jax.experimental.pallas import tpu as pltpu

F32 = jnp.float32
BF = jnp.bfloat16
MESH = pl.DeviceIdType.MESH

EPS = 1e-6
ROPE_BASE = 10000.0
N_CHIPS = 4
RET_HEADS = 4
RET_DIM = 128
RET_WIDTH = RET_HEADS * RET_DIM
RET_CHUNK = 128
RET_SCALE = RET_DIM ** -0.5
FOX_HEADS = 8
FOX_DIM = 64
FOX_WIDTH = FOX_HEADS * FOX_DIM
FOX_SCALE = FOX_DIM ** -0.5
IN_COLS = 4 * RET_WIDTH + 3 * FOX_WIDTH + FOX_HEADS
IN_PAD = 4096
FF_COL = 4 * RET_WIDTH + 3 * FOX_WIDTH
NEG = -1e30

ADAM_LR = 0.001
ADAM_B1 = 0.9
ADAM_B2 = 0.999
ADAM_EPS = 1e-08
ADAM_WD = 0.01
ADAM_STEP = 10

VMEM_LIMIT = 52 * 1024 * 1024

NT = (((1,), (1,)), ((), ()))
TN = (((0,), (0,)), ((), ()))

HBM_SPEC = pl.BlockSpec(memory_space=pltpu.HBM)
VMEM_SPEC = pl.BlockSpec(memory_space=pltpu.VMEM)


def _dot(a, b):
    return jnp.dot(a, b, preferred_element_type=F32)


def _dot_nt(a, b):
    return lax.dot_general(a, b, NT, preferred_element_type=F32)


def _dot_tn(a, b):
    return lax.dot_general(a, b, TN, preferred_element_type=F32)


def _rstd(xv):
    return lax.rsqrt(jnp.mean(xv * xv, axis=-1, keepdims=True) + EPS)


def _rms_bwd(dn, xv, r, ln):
    xh = xv * r
    dxh = dn * ln
    dx = r * (dxh - xh * jnp.mean(dxh * xh, axis=-1, keepdims=True))
    return dx, jnp.sum(dn * xh, axis=0, keepdims=True)


def _sigmoid(x):
    return jax.nn.sigmoid(x)


def _tile(n, pref):
    return pref if n % pref == 0 else n


class _Comm:
    def __init__(self, ins, out_shapes, sems, start, wait):
        self.ins, self.out_shapes, self.sems, self.start, self.wait = list(ins), list(out_shapes), list(sems), start, wait


def _merge(comms):
    comms = [c for c in comms if c is not None]
    if not comms:
        return None
    bounds, ni, no, ns = [], 0, 0, 0
    for c in comms:
        bounds.append((ni, no, ns))
        ni, no, ns = ni + len(c.ins), no + len(c.out_shapes), ns + len(c.sems)

    def run(which):
        def f(ins, outs, sems):
            for c, (i, o, s) in zip(comms, bounds):
                getattr(c, which)(ins[i:i + len(c.ins)], outs[o:o + len(c.out_shapes)], sems[s:s + len(c.sems)])
        return f

    return _Comm([a for c in comms for a in c.ins], [a for c in comms for a in c.out_shapes],
                 [a for c in comms for a in c.sems], run("start"), run("wait"))


def _split_outs(comms, outs):
    res, o = [], 0
    for c in comms:
        if c is not None:
            res.append(list(outs[o:o + len(c.out_shapes)]))
            o += len(c.out_shapes)
    return res


def _pcall(body, args, *, name, out_shape, grid=(), in_specs=None, out_specs=None, scratch=(), comm=None):
    many = isinstance(out_shape, (list, tuple))
    outs = list(out_shape) if many else [out_shape]
    n_in, n_out, n_scr = len(args), len(outs), len(scratch)
    if in_specs is None:
        in_specs, out_specs = [VMEM_SPEC] * n_in, [VMEM_SPEC] * n_out
    else:
        in_specs, out_specs = list(in_specs), (list(out_specs) if many else [out_specs])
    params = pltpu.CompilerParams(dimension_semantics=("arbitrary",) * len(grid), vmem_limit_bytes=VMEM_LIMIT)
    if comm is None:
        res = pl.pallas_call(body, name=name, grid=grid, out_shape=outs, in_specs=in_specs, out_specs=out_specs,
                             scratch_shapes=list(scratch), compiler_params=params)(*args)
        return list(res) if many else res[0]
    ci, co = len(comm.ins), len(comm.out_shapes)

    def wrapped(*refs):
        a, ca = refs[:n_in], refs[n_in:n_in + ci]
        o = refs[n_in + ci:n_in + ci + n_out]
        cout = refs[n_in + ci + n_out:n_in + ci + n_out + co]
        s = refs[n_in + ci + n_out + co:n_in + ci + n_out + co + n_scr]
        csem = refs[n_in + ci + n_out + co + n_scr:]
        if grid:
            first = functools.reduce(operator.and_, [pl.program_id(k) == 0 for k in range(len(grid))])
            last = functools.reduce(operator.and_, [pl.program_id(k) == grid[k] - 1 for k in range(len(grid))])
            pl.when(first)(lambda: comm.start(ca, cout, csem))
            body(*a, *o, *s)
            pl.when(last)(lambda: comm.wait(ca, cout, csem))
        else:
            comm.start(ca, cout, csem)
            body(*a, *o, *s)
            comm.wait(ca, cout, csem)

    res = pl.pallas_call(
        wrapped, name=name, grid=grid, out_shape=outs + comm.out_shapes,
        in_specs=in_specs + [HBM_SPEC] * ci, out_specs=out_specs + [HBM_SPEC] * co,
        scratch_shapes=list(scratch) + comm.sems, compiler_params=params)(*args, *comm.ins)
    mine = list(res[:n_out])
    return (mine if many else mine[0]), list(res[n_out:])


def _peer_chips(x, y):
    return [(1 - x, y), (x, 1 - y), (1 - x, 1 - y)]


def _c_all_gather(shards):
    n = len(shards)

    def copies(ins, outs, sems):
        send_sems, recv_sems, fwd_send, fwd_recv, local_sems = sems
        x, y, c = lax.axis_index("x"), lax.axis_index("y"), lax.axis_index("c")
        me = 2 * x + y
        peers = _peer_chips(x, y)
        chip = [2 * px + py for px, py in peers]

        def ici(g, j, slot):
            return pltpu.make_async_remote_copy(
                src_ref=ins[g].at[c], dst_ref=outs[g].at[slot, c], send_sem=send_sems.at[g, j],
                recv_sem=recv_sems.at[g, j], device_id=(*peers[j], c), device_id_type=MESH)

        def d2d(g, j, half):
            return pltpu.make_async_remote_copy(
                src_ref=outs[g].at[chip[j], half], dst_ref=outs[g].at[chip[j], half], send_sem=fwd_send.at[g, j],
                recv_sem=fwd_recv.at[g, j], device_id=(x, y, 1 - c), device_id_type=MESH)

        pairs = [(g, j) for g in range(n) for j in range(3)]
        local = [pltpu.make_async_copy(ins[g], outs[g].at[me], local_sems.at[g]) for g in range(n)]
        sends = [ici(g, j, me) for g, j in pairs]
        recvs = [ici(g, j, chip[j]) for g, j in pairs]
        passes = [d2d(g, j, c) for g, j in pairs]
        passed = [d2d(g, j, 1 - c) for g, j in pairs]
        return local, sends, recvs, passes, passed

    def start(ins, outs, sems):
        local, sends, _, _, _ = copies(ins, outs, sems)
        for cp in local + sends:
            cp.start()

    def wait(ins, outs, sems):
        local, sends, recvs, passes, passed = copies(ins, outs, sems)
        for rcv, fwd in zip(recvs, passes):
            rcv.wait_recv()
            fwd.start()
        for cp in passed:
            cp.wait_recv()
        for cp in sends + passes:
            cp.wait_send()
        for cp in local:
            cp.wait()

    pair_sems = pltpu.SemaphoreType.DMA((n, 3))
    return _Comm(
        shards, [jax.ShapeDtypeStruct((N_CHIPS,) + s.shape, s.dtype) for s in shards],
        [pair_sems, pair_sems, pair_sems, pair_sems, pltpu.SemaphoreType.DMA((n,))], start, wait)


def _start_wait(copies):
    def start(ins, outs, sems):
        local, sends, _ = copies(ins, outs, sems)
        for cp in local + sends:
            cp.start()

    def wait(ins, outs, sems):
        local, sends, recvs = copies(ins, outs, sems)
        for cp in recvs:
            cp.wait_recv()
        for cp in sends:
            cp.wait_send()
        for cp in local:
            cp.wait()

    return start, wait


def _c_half_swap(grads):
    n = len(grads)

    def copies(ins, outs, sems):
        send_sems, recv_sems = sems
        x, y, c = lax.axis_index("x"), lax.axis_index("y"), lax.axis_index("c")
        sends = []
        for g in range(n):
            half = ins[g].shape[1] // 2
            sends.append(pltpu.make_async_remote_copy(
                src_ref=ins[g].at[:, pl.ds((1 - c) * half, half), :], dst_ref=outs[g],
                send_sem=send_sems.at[g], recv_sem=recv_sems.at[g], device_id=(x, y, 1 - c), device_id_type=MESH))
        return [], sends, sends

    return _Comm(
        grads, [jax.ShapeDtypeStruct((N_CHIPS, s.shape[1] // 2, s.shape[2]), s.dtype) for s in grads],
        [pltpu.SemaphoreType.DMA((n,)), pltpu.SemaphoreType.DMA((n,))], *_start_wait(copies))


def _c_chip_exchange(parts):
    n = len(parts)

    def copies(ins, outs, sems):
        send_sems, recv_sems, local_sems = sems
        x, y, c = lax.axis_index("x"), lax.axis_index("y"), lax.axis_index("c")
        me = 2 * x + y
        peers = _peer_chips(x, y)

        def remote(g, j, src_slot, dst_slot):
            return pltpu.make_async_remote_copy(
                src_ref=ins[g].at[src_slot], dst_ref=outs[g].at[dst_slot], send_sem=send_sems.at[g, j],
                recv_sem=recv_sems.at[g, j], device_id=(*peers[j], c), device_id_type=MESH)

        chip = [2 * px + py for px, py in peers]
        local = [pltpu.make_async_copy(ins[g].at[me], outs[g].at[me], local_sems.at[g]) for g in range(n)]
        sends = [remote(g, j, chip[j], me) for g in range(n) for j in range(3)]
        recvs = [remote(g, j, me, chip[j]) for g in range(n) for j in range(3)]
        return local, sends, recvs

    return _Comm(
        parts, [jax.ShapeDtypeStruct(s.shape, s.dtype) for s in parts],
        [pltpu.SemaphoreType.DMA((n, 3)), pltpu.SemaphoreType.DMA((n, 3)), pltpu.SemaphoreType.DMA((n,))],
        *_start_wait(copies))


def _c_join(halves):
    n = len(halves)

    def copies(ins, outs, sems):
        send_sems, recv_sems, local_sems = sems
        x, y, c = lax.axis_index("x"), lax.axis_index("y"), lax.axis_index("c")

        def remote(g, slot):
            return pltpu.make_async_remote_copy(
                src_ref=ins[g], dst_ref=outs[g].at[slot], send_sem=send_sems.at[g], recv_sem=recv_sems.at[g],
                device_id=(x, y, 1 - c), device_id_type=MESH)

        local = [pltpu.make_async_copy(ins[g], outs[g].at[c], local_sems.at[g]) for g in range(n)]
        return local, [remote(g, c) for g in range(n)], [remote(g, 1 - c) for g in range(n)]

    return _Comm(
        halves, [jax.ShapeDtypeStruct((2,) + s.shape, s.dtype) for s in halves],
        [pltpu.SemaphoreType.DMA((n,)), pltpu.SemaphoreType.DMA((n,)), pltpu.SemaphoreType.DMA((n,))],
        *_start_wait(copies))


def _all_reduce_small(v):
    rows = v.shape[0]

    def body(v_ref, out_ref, buf, send_sems, recv_sems):
        x, y, c = lax.axis_index("x"), lax.axis_index("y"), lax.axis_index("c")
        me = 4 * x + 2 * y + c
        buf[me] = v_ref[...]
        flips = [(fx, fy, fc) for fx in (0, 1) for fy in (0, 1) for fc in (0, 1)][1:]

        def peer(k):
            fx, fy, fc = flips[k]
            px, py, pc = x ^ fx, y ^ fy, c ^ fc
            return (px, py, pc), 4 * px + 2 * py + pc

        def copy(k, slot):
            return pltpu.make_async_remote_copy(
                src_ref=buf.at[slot], dst_ref=buf.at[slot], send_sem=send_sems.at[k],
                recv_sem=recv_sems.at[k], device_id=peer(k)[0], device_id_type=MESH)

        sends = [copy(k, me) for k in range(7)]
        for cp in sends:
            cp.start()
        for k in range(7):
            copy(k, peer(k)[1]).wait_recv()
        for cp in sends:
            cp.wait_send()
        acc = buf[0]
        for d in range(1, 8):
            acc = acc + buf[d]
        out_ref[...] = acc

    return _pcall(body, [v], name="all_reduce_small", out_shape=jax.ShapeDtypeStruct((rows, 128), F32),
                  scratch=[pltpu.VMEM((8, rows, 128), F32), pltpu.SemaphoreType.DMA((7,)),
                           pltpu.SemaphoreType.DMA((7,))])


def _add_halves(g, got):
    _, h, c = got.shape
    th = _tile(h, 128)
    nh = h // th
    half = lax.axis_index("c") * nh

    def body(h_ref, a_ref, b_ref, o_ref):
        o_ref[...] = (a_ref[...].astype(F32) + b_ref[...].astype(F32)).astype(o_ref.dtype)

    spec = pl.BlockSpec((1, th, c), lambda j, i, h_ref: (j, i, 0))
    mine = pl.BlockSpec((1, th, c), lambda j, i, h_ref: (j, h_ref[0] + i, 0))
    return _pcall_prefetch(body, half, [g, got], name="add_halves", grid=(N_CHIPS, nh),
                           out_shape=jax.ShapeDtypeStruct(got.shape, BF), in_specs=[mine, spec], out_specs=spec)


def _pcall_prefetch(body, scalar, args, *, name, grid, out_shape, in_specs, out_specs):
    return pl.pallas_call(
        body, name=name, out_shape=out_shape,
        grid_spec=pltpu.PrefetchScalarGridSpec(num_scalar_prefetch=1, grid=grid, in_specs=in_specs,
                                               out_specs=out_specs),
        compiler_params=pltpu.CompilerParams(dimension_semantics=("arbitrary",) * len(grid),
                                             vmem_limit_bytes=VMEM_LIMIT),
    )(jnp.reshape(scalar, (1,)).astype(jnp.int32), *args)


def _sum_chips(parts, comm=None):
    _, h, c = parts.shape
    th = _tile(h, 128)

    def body(p_ref, o_ref):
        acc = p_ref[0].astype(F32)
        for s in range(1, N_CHIPS):
            acc = acc + p_ref[s].astype(F32)
        o_ref[...] = acc

    return _pcall(body, [parts], name="sum_chips", grid=(h // th,), out_shape=jax.ShapeDtypeStruct((h, c), F32),
                  in_specs=[pl.BlockSpec((N_CHIPS, th, c), lambda i: (0, i, 0))],
                  out_specs=pl.BlockSpec((th, c), lambda i: (i, 0)), comm=comm)


def _adamw(w, g, m, v, comm=None):
    r, c = w.shape
    tr = _tile(r, 256)
    c1 = 1.0 / (1.0 - ADAM_B1 ** ADAM_STEP)
    c2 = 1.0 / (1.0 - ADAM_B2 ** ADAM_STEP)

    def body(w_ref, g_ref, m_ref, v_ref, d_ref, nm_ref, nv_ref):
        gv = g_ref[...]
        nm = ADAM_B1 * m_ref[...] + (1.0 - ADAM_B1) * gv
        nv = ADAM_B2 * v_ref[...] + (1.0 - ADAM_B2) * (gv * gv)
        nm_ref[...] = nm
        nv_ref[...] = nv
        d_ref[...] = -ADAM_LR * ((nm * c1) / (jnp.sqrt(nv * c2) + ADAM_EPS) + ADAM_WD * w_ref[...])

    spec = pl.BlockSpec((tr, c), lambda i: (i, 0))
    sds = jax.ShapeDtypeStruct((r, c), F32)
    return _pcall(body, [w, g, m, v], name="adamw", grid=(r // tr,), out_shape=[sds, sds, sds],
                  in_specs=[spec] * 4, out_specs=[spec] * 3, comm=comm)


def _wgrad(name, a, b, a_spec, b_spec, m, n, nb, n_tok, comm):
    def body(a_ref, b_ref, o_ref, acc):
        t = pl.program_id(1)

        @pl.when(t == 0)
        def _():
            acc[...] = jnp.zeros_like(acc)

        acc[...] += _dot_tn(a_ref[...], b_ref[...])

        @pl.when(t == n_tok - 1)
        def _():
            o_ref[...] = acc[...].astype(o_ref.dtype)

    return _pcall(body, [a, b], name=name, grid=(nb, n_tok), out_shape=jax.ShapeDtypeStruct((nb, m, n), BF),
                  in_specs=[a_spec, b_spec], out_specs=pl.BlockSpec((None, m, n), lambda j, t: (j, 0, 0)),
                  scratch=[pltpu.VMEM((m, n), F32)], comm=comm)


def _wgrad_cols(name, a, b, nb, comm=None):
    t_tok, m = a.shape
    n = b.shape[1] // nb
    tk = _tile(t_tok, 512)
    return _wgrad(name, a, b, pl.BlockSpec((tk, m), lambda j, t: (t, 0)), pl.BlockSpec((tk, n), lambda j, t: (t, j)),
                  m, n, nb, t_tok // tk, comm)


def _wgrad_rows(name, a, b, nb, comm=None):
    t_tok, n = b.shape
    m = a.shape[1] // nb
    tk = _tile(t_tok, 512)
    return _wgrad(name, a, b, pl.BlockSpec((tk, m), lambda j, t: (t, j)), pl.BlockSpec((tk, n), lambda j, t: (t, 0)),
                  m, n, nb, t_tok // tk, comm)


def _wgrad_a_shared(name, a, b4, comm=None):
    t_tok, m = a.shape
    nb, _, n = b4.shape
    tk = _tile(t_tok, 512)
    return _wgrad(name, a, b4, pl.BlockSpec((tk, m), lambda j, t: (t, 0)),
                  pl.BlockSpec((None, tk, n), lambda j, t: (j, t, 0)), m, n, nb, t_tok // tk, comm)


def _wgrad_b_shared(name, a4, b, comm=None):
    nb, t_tok, m = a4.shape
    n = b.shape[1]
    tk = _tile(t_tok, 512)
    return _wgrad(name, a4, b, pl.BlockSpec((None, tk, m), lambda j, t: (j, t, 0)),
                  pl.BlockSpec((tk, n), lambda j, t: (t, 0)), m, n, nb, t_tok // tk, comm)


def _w4_spec(r, c):
    return pl.BlockSpec((None, r, c), lambda i, j: (j, 0, 0))


def _ffn_fwd(h, ln, wg4, wu4, wd4, comm=None):
    t_tok, d = h.shape
    f = wg4.shape[-1]
    tm = _tile(t_tok, 512)

    def body(h_ref, ln_ref, wg_ref, wu_ref, wd_ref, ho_ref, n_ref, g_ref, u_ref, n_s, acc):
        j = pl.program_id(1)

        @pl.when(j == 0)
        def _():
            xv = h_ref[...]
            nv = (xv * _rstd(xv) * ln_ref[...]).astype(BF)
            n_s[...] = nv
            n_ref[...] = nv
            acc[...] = jnp.zeros_like(acc)

        nv = n_s[...]
        g = _dot(nv, wg_ref[...])
        u = _dot(nv, wu_ref[...])
        g_ref[...] = g.astype(BF)
        u_ref[...] = u.astype(BF)
        a = (g * _sigmoid(g) * u).astype(BF)
        acc[...] += _dot(a, wd_ref[...])

        @pl.when(j == N_CHIPS - 1)
        def _():
            ho_ref[...] = h_ref[...] + 0.5 * acc[...]

    row = pl.BlockSpec((tm, d), lambda i, j: (i, 0))
    gu = pl.BlockSpec((None, tm, f), lambda i, j: (j, i, 0))
    gu_sds = jax.ShapeDtypeStruct((N_CHIPS, t_tok, f), BF)
    return _pcall(
        body, [h, ln, wg4, wu4, wd4], name="ffn_fwd", grid=(t_tok // tm, N_CHIPS),
        out_shape=[jax.ShapeDtypeStruct((t_tok, d), F32), jax.ShapeDtypeStruct((t_tok, d), BF), gu_sds, gu_sds],
        in_specs=[row, pl.BlockSpec((1, d), lambda i, j: (0, 0)), _w4_spec(d, f), _w4_spec(d, f), _w4_spec(f, d)],
        out_specs=[row, row, gu, gu],
        scratch=[pltpu.VMEM((tm, d), BF), pltpu.VMEM((tm, d), F32)], comm=comm)


def _ffn_bwd(dho, h, ln, g4, u4, wg4, wu4, wd4, comm=None):
    t_tok, d = h.shape
    f = wg4.shape[-1]
    tm = _tile(t_tok, 512)

    def body(dho_ref, h_ref, ln_ref, g_ref, u_ref, wg_ref, wu_ref, wd_ref,
             dhi_ref, dln_ref, dg_ref, du_ref, a_ref, dhb_ref, dhb_s, dn_acc):
        i, j = pl.program_id(0), pl.program_id(1)

        @pl.when(j == 0)
        def _():
            dhb = (0.5 * dho_ref[...]).astype(BF)
            dhb_s[...] = dhb
            dhb_ref[...] = dhb
            dn_acc[...] = jnp.zeros_like(dn_acc)

        @pl.when((i == 0) & (j == 0))
        def _():
            dln_ref[...] = jnp.zeros_like(dln_ref)

        g = g_ref[...].astype(F32)
        u = u_ref[...].astype(F32)
        s = _sigmoid(g)
        sg = g * s
        a_ref[...] = (sg * u).astype(BF)
        da = _dot_nt(dhb_s[...], wd_ref[...])
        dg = (da * u * (s * (1.0 + g * (1.0 - s)))).astype(BF)
        du = (da * sg).astype(BF)
        dg_ref[...] = dg
        du_ref[...] = du
        dn_acc[...] += _dot_nt(dg, wg_ref[...]) + _dot_nt(du, wu_ref[...])

        @pl.when(j == N_CHIPS - 1)
        def _():
            xv = h_ref[...]
            dx, dln = _rms_bwd(dn_acc[...], xv, _rstd(xv), ln_ref[...])
            dln_ref[...] += dln
            dhi_ref[...] = dho_ref[...] + dx

    row = pl.BlockSpec((tm, d), lambda i, j: (i, 0))
    vec = pl.BlockSpec((1, d), lambda i, j: (0, 0))
    gu = pl.BlockSpec((None, tm, f), lambda i, j: (j, i, 0))
    gu_sds = jax.ShapeDtypeStruct((N_CHIPS, t_tok, f), BF)
    return _pcall(
        body, [dho, h, ln, g4, u4, wg4, wu4, wd4], name="ffn_bwd", grid=(t_tok // tm, N_CHIPS),
        out_shape=[jax.ShapeDtypeStruct((t_tok, d), F32), jax.ShapeDtypeStruct((1, d), F32),
                   gu_sds, gu_sds, gu_sds, jax.ShapeDtypeStruct((t_tok, d), BF)],
        in_specs=[row, row, vec, gu, gu, _w4_spec(d, f), _w4_spec(d, f), _w4_spec(f, d)],
        out_specs=[row, vec, gu, gu, gu, row],
        scratch=[pltpu.VMEM((tm, d), BF), pltpu.VMEM((tm, d), F32)], comm=comm)


def _rope_tables(pos_col, inv_freq2, comm=None):
    t_tok = pos_col.shape[0]

    def body(p_ref, f_ref, cos_ref, sin_ref):
        ang = p_ref[...] * f_ref[...]
        lane = lax.broadcasted_iota(jnp.int32, ang.shape, 1)
        s = jnp.sin(ang)
        cos_ref[...] = jnp.cos(ang)
        sin_ref[...] = jnp.where((lane & 1) == 0, -s, s)

    sds = jax.ShapeDtypeStruct((t_tok, 128), F32)
    return _pcall(body, [pos_col, inv_freq2], name="rope_tables", out_shape=[sds, sds], comm=comm)


def _swap_pairs(x):
    lane = lax.broadcasted_iota(jnp.int32, x.shape, 1)
    return jnp.where((lane & 1) == 0, pltpu.roll(x, 127, 1), pltpu.roll(x, 1, 1))


def _mix_in(h, ln, w_in, wm4, b_m, cos_t, sin_t, comm=None):
    t_tok, d = h.shape
    cm = wm4.shape[-1]
    tm = _tile(t_tok, 256)

    def body(h_ref, ln_ref, win_ref, wm_ref, bm_ref, cos_ref, sin_ref,
             u_ref, rq_ref, rk_ref, rv_ref, rg_ref, fq_ref, fk_ref, fv_ref, ff_ref, ga_ref, gb_ref):
        xv = h_ref[...]
        ub = (xv * _rstd(xv) * ln_ref[...]).astype(BF)
        u_ref[...] = ub
        cosv, sinv = cos_ref[...], sin_ref[...]

        def sec(k):
            return _dot(ub, win_ref[:, k * 512:(k + 1) * 512])

        def rot(xh):
            return xh * cosv + _swap_pairs(xh) * sinv

        pq, pk = sec(0), sec(1)
        for hh in range(RET_HEADS):
            sl = slice(hh * RET_DIM, (hh + 1) * RET_DIM)
            rq_ref[:, sl] = rot(pq[:, sl]).astype(BF)
            rk_ref[:, sl] = (rot(pk[:, sl]) * RET_SCALE).astype(BF)
        rv_ref[...] = sec(2).astype(BF)
        rg_ref[...] = sec(3).astype(BF)
        fq_ref[...] = (sec(4) * FOX_SCALE).astype(BF)
        fk_ref[...] = sec(5).astype(BF)
        fv_ref[...] = sec(6).astype(BF)
        ff_ref[...] = _dot(ub, win_ref[:, FF_COL:FF_COL + 128])
        for j in range(N_CHIPS):
            gs = _sigmoid(_dot(ub, wm_ref[j]) + bm_ref[:, j * cm:(j + 1) * cm]).astype(BF)
            col = j * cm
            if col < d:
                ga_ref[:, col:col + cm] = gs
            else:
                gb_ref[:, col - d:col - d + cm] = gs

    row = lambda c: pl.BlockSpec((tm, c), lambda i: (i, 0))
    full = lambda *s: pl.BlockSpec(s, lambda i: (0,) * len(s))
    sds = lambda c, dt: jax.ShapeDtypeStruct((t_tok, c), dt)
    return _pcall(
        body, [h, ln, w_in, wm4, b_m, cos_t, sin_t], name="mix_in", grid=(t_tok // tm,),
        out_shape=[sds(d, BF)] + [sds(512, BF)] * 7 + [sds(128, F32), sds(d, BF), sds(d, BF)],
        in_specs=[row(d), full(1, d), full(d, IN_PAD), full(N_CHIPS, d, cm), full(1, 2 * d), row(128), row(128)],
        out_specs=[row(d)] + [row(512)] * 7 + [row(128), row(d), row(d)], comm=comm)


def _split3(x):
    hi = x.astype(BF)
    r1 = x - hi.astype(F32)
    mid = r1.astype(BF)
    lo = (r1 - mid.astype(F32)).astype(BF)
    return hi, mid, lo


def _aug_lane():
    return lax.broadcasted_iota(jnp.int32, (1, 128), 1) & (FOX_DIM - 1)


def _aug_put(base, k0, parts):
    w = _aug_lane()
    for i, part in enumerate(parts):
        base = jnp.where(w == k0 + i, part, base)
    return base


def _forget_fwd(ffl, b_pad):
    t_tok = ffl.shape[0]
    tb = _tile(t_tok, 256)

    def body(ff_ref, b_ref, aq_ref, ak_ref, cum_s):
        r = lax.broadcasted_iota(jnp.int32, (tb, tb), 0)
        c = lax.broadcasted_iota(jnp.int32, (tb, tb), 1)
        tri = jnp.where(c <= r, 1.0, 0.0).astype(BF)
        carry = jnp.zeros((1, 128), F32)
        for i in range(t_tok // tb):
            z = ff_ref[i * tb:(i + 1) * tb, :] + b_ref[...]
            lf = jnp.minimum(z, 0.0) - jnp.log(1.0 + jnp.exp(-jnp.abs(z)))
            hi, mid, lo = _split3(lf)
            cs = _dot(tri, hi) + _dot(tri, mid) + _dot(tri, lo) + carry
            cum_s[i * tb:(i + 1) * tb, :] = cs
            carry = cs[tb - 1:tb, :]
        x = cum_s[...]
        first = lax.broadcasted_iota(jnp.int32, (1, 128), 1) < FOX_DIM
        w = _aug_lane()
        one = jnp.ones((t_tok, 128), BF)
        zero = jnp.zeros((t_tok, 128), BF)
        for pp in range(FOX_HEADS // 2):
            other = jnp.where(first, x[:, 2 * pp + 1:2 * pp + 2], x[:, 2 * pp:2 * pp + 1])
            parts = _split3(other)
            aq = jnp.where((w >= 3) & (w < 6), one, zero)
            ak = jnp.where((w < 3) | ((w >= 6) & (w < 9)), one, zero)
            aq_ref[:, pp * 128:(pp + 1) * 128] = _aug_put(aq, 0, parts)
            ak_ref[:, pp * 128:(pp + 1) * 128] = _aug_put(ak, 3, [-q for q in parts])

    sds = jax.ShapeDtypeStruct((t_tok, FOX_WIDTH), BF)
    return _pcall(body, [ffl, b_pad], name="forget_fwd", out_shape=[sds, sds],
                  scratch=[pltpu.VMEM((t_tok, 128), F32)])


def _fox_aug_lse(aq, lse_e):
    t_tok = aq.shape[0]
    tm = _tile(t_tok, 512)

    def body(aq_ref, lse_ref, o_ref):
        for pp in range(FOX_HEADS // 2):
            sl = slice(pp * 128, (pp + 1) * 128)
            other = pltpu.roll(lse_ref[:, sl], FOX_DIM, 1)
            o_ref[:, sl] = _aug_put(aq_ref[:, sl], 6, _split3(-other))

    spec = pl.BlockSpec((tm, FOX_WIDTH), lambda i: (i, 0))
    return _pcall(body, [aq, lse_e], name="fox_aug_lse", grid=(t_tok // tm,),
                  out_shape=jax.ShapeDtypeStruct((t_tok, FOX_WIDTH), BF), in_specs=[spec, spec], out_specs=spec)


def _forget_bwd(dcum_t, dcum_q, ffl, b_pad):
    t_tok = ffl.shape[0]
    tb = _tile(t_tok, 256)

    def body(dc_ref, dq_ref, ff_ref, b_ref, dff_ref, db_ref, pad_s, d_s):
        pad_s[...] = jnp.zeros_like(pad_s)
        pad_s[0:FOX_HEADS, :] = dc_ref[...]
        dsum = pad_s[...].T
        lane = lax.broadcasted_iota(jnp.int32, (t_tok, 128), 1)
        for hh in range(FOX_HEADS):
            dsum = dsum + jnp.where(lane == hh, dq_ref[:, hh * FOX_DIM:hh * FOX_DIM + 1], 0.0)
        d_s[...] = dsum
        r = lax.broadcasted_iota(jnp.int32, (tb, tb), 0)
        c = lax.broadcasted_iota(jnp.int32, (tb, tb), 1)
        tri = jnp.where(c >= r, 1.0, 0.0).astype(BF)
        carry = jnp.zeros((1, 128), F32)
        db = jnp.zeros((1, 128), F32)
        for i in reversed(range(t_tok // tb)):
            hi, mid, lo = _split3(d_s[i * tb:(i + 1) * tb, :])
            dlf = _dot(tri, hi) + _dot(tri, mid) + _dot(tri, lo) + carry
            carry = dlf[0:1, :]
            z = ff_ref[i * tb:(i + 1) * tb, :] + b_ref[...]
            dff = dlf * _sigmoid(-z)
            dff_ref[i * tb:(i + 1) * tb, :] = dff.astype(BF)
            db = db + jnp.sum(dff, axis=0, keepdims=True)
        db_ref[...] = db

    return _pcall(
        body, [dcum_t, dcum_q, ffl, b_pad], name="forget_bwd",
        out_shape=[jax.ShapeDtypeStruct((t_tok, 128), BF), jax.ShapeDtypeStruct((1, 128), F32)],
        scratch=[pltpu.VMEM((128, t_tok), F32), pltpu.VMEM((t_tok, 128), F32)])


def _first_half():
    return lax.broadcasted_iota(jnp.int32, (1, 128), 1) < FOX_DIM


def _head_rows(x2, a2, hh):
    return jnp.where(_first_half(), x2, a2) if hh == 0 else jnp.where(_first_half(), a2, x2)


def _head_only(x2, hh):
    zero = jnp.zeros_like(x2)
    return jnp.where(_first_half(), x2, zero) if hh == 0 else jnp.where(_first_half(), zero, x2)


def _causal_diag(s):
    rows = lax.broadcasted_iota(jnp.int32, s.shape, 0)
    cols = lax.broadcasted_iota(jnp.int32, s.shape, 1)
    return jnp.where(cols <= rows, s, NEG)


def _diag_or_below(qi, ki, step):
    pl.when(ki < qi)(lambda: step(False))
    pl.when(ki == qi)(lambda: step(True))


def _fox_fwd(fq, fk, fv, aq, ak, comm=None):
    t_tok = fq.shape[0]
    t = _tile(t_tok, 512)
    nq = t_tok // t
    npair = FOX_HEADS // 2

    def body(q_ref, k_ref, v_ref, aq_ref, ak_ref, o_ref, of_ref, lse_ref, m_s, l_s, acc_s):
        qi, ki = pl.program_id(1), pl.program_id(2)

        @pl.when(ki == 0)
        def _():
            m_s[...] = jnp.full_like(m_s, NEG)
            l_s[...] = jnp.zeros_like(l_s)
            acc_s[...] = jnp.zeros_like(acc_s)

        def step(diag):
            q2, k2, v2, aq2, ak2 = q_ref[...], k_ref[...], v_ref[...], aq_ref[...], ak_ref[...]
            for hh in range(2):
                s = _dot_nt(_head_rows(q2, aq2, hh), _head_rows(k2, ak2, hh))
                if diag:
                    s = _causal_diag(s)
                m_prev = m_s[hh]
                m_new = jnp.maximum(m_prev, jnp.max(s, axis=1, keepdims=True))
                alpha = jnp.exp(m_prev - m_new)
                p = jnp.exp(s - jnp.tile(m_new, (1, t // 128)))
                l_s[hh] = alpha * l_s[hh] + jnp.sum(p, axis=1, keepdims=True)
                acc_s[hh] = alpha * acc_s[hh] + _dot(p.astype(BF), v2)
                m_s[hh] = m_new

        _diag_or_below(qi, ki, step)

        @pl.when(ki == nq - 1)
        def _():
            first = _first_half()
            o = jnp.where(first, acc_s[0] / l_s[0], acc_s[1] / l_s[1])
            o_ref[...] = o.astype(BF)
            of_ref[...] = o
            lse_ref[...] = jnp.where(first, m_s[0] + jnp.log(l_s[0]), m_s[1] + jnp.log(l_s[1]))

    qs = pl.BlockSpec((t, 128), lambda p, qi, ki: (qi, p))
    ks = pl.BlockSpec((t, 128), lambda p, qi, ki: (jnp.minimum(ki, qi), p))
    stat = pltpu.VMEM((2, t, 128), F32)
    return _pcall(
        body, [fq, fk, fv, aq, ak], name="fox_fwd", grid=(npair, nq, nq),
        out_shape=[jax.ShapeDtypeStruct((t_tok, FOX_WIDTH), BF), jax.ShapeDtypeStruct((t_tok, FOX_WIDTH), F32),
                   jax.ShapeDtypeStruct((t_tok, FOX_WIDTH), F32)],
        in_specs=[qs, ks, ks, qs, ks], out_specs=[qs, qs, qs], scratch=[stat, stat, stat], comm=comm)


def _fox_ds(q2, k2, v2, do2, aq2, ak2, ad2, hh, diag):
    s = _dot_nt(_head_rows(q2, aq2, hh), _head_rows(k2, ak2, hh))
    if diag:
        s = _causal_diag(s)
    p = jnp.exp(s)
    av = jnp.where(_aug_lane() < 3, 1.0, 0.0).astype(BF)
    dp = _dot_nt(_head_rows(do2, ad2, hh), _head_rows(v2, jnp.broadcast_to(av, v2.shape), hh))
    return p, p * dp


def _fox_bwd_kv(fq, fk, fv, do, aqb, ak, ad, comm=None):
    t_tok = fq.shape[0]
    t = _tile(t_tok, 512)
    nq = t_tok // t
    npair = FOX_HEADS // 2

    def body(q_ref, k_ref, v_ref, do_ref, aq_ref, ak_ref, ad_ref, dk_ref, dv_ref, dck_ref, dk_s, dv_s):
        ki, qi = pl.program_id(1), pl.program_id(2)

        @pl.when(qi == 0)
        def _():
            dk_s[...] = jnp.zeros_like(dk_s)
            dv_s[...] = jnp.zeros_like(dv_s)
            dck_ref[...] = jnp.zeros_like(dck_ref)

        def step(diag):
            q2, k2, v2, do2 = q_ref[...], k_ref[...], v_ref[...], do_ref[...]
            for hh in range(2):
                p, ds = _fox_ds(q2, k2, v2, do2, aq_ref[...], ak_ref[...], ad_ref[...], hh, diag)
                dv_s[...] += _dot_tn(p.astype(BF), _head_only(do2, hh))
                dk_s[...] += _dot_tn(ds.astype(BF), _head_only(q2, hh))
                dck_ref[hh] = dck_ref[hh] - jnp.sum(ds, axis=0, keepdims=True)

        _diag_or_below(qi, ki, step)

        @pl.when(qi == nq - 1)
        def _():
            dk_ref[...] = dk_s[...].astype(BF)
            dv_ref[...] = dv_s[...].astype(BF)

    qs = pl.BlockSpec((t, 128), lambda p, ki, qi: (jnp.maximum(qi, ki), p))
    ks = pl.BlockSpec((t, 128), lambda p, ki, qi: (ki, p))
    cks = pl.BlockSpec((2, 1, t), lambda p, ki, qi: (p, 0, ki))
    sds = jax.ShapeDtypeStruct((t_tok, FOX_WIDTH), BF)
    return _pcall(
        body, [fq, fk, fv, do, aqb, ak, ad], name="fox_bwd_kv", grid=(npair, nq, nq),
        out_shape=[sds, sds, jax.ShapeDtypeStruct((FOX_HEADS, 1, t_tok), F32)],
        in_specs=[qs, ks, ks, qs, qs, ks, qs], out_specs=[ks, ks, cks],
        scratch=[pltpu.VMEM((t, 128), F32), pltpu.VMEM((t, 128), F32)], comm=comm)


def _fox_bwd_q(fq, fk, fv, do, aqb, ak, ad, comm=None):
    t_tok = fq.shape[0]
    t = _tile(t_tok, 512)
    nq = t_tok // t
    npair = FOX_HEADS // 2

    def body(q_ref, k_ref, v_ref, do_ref, aq_ref, ak_ref, ad_ref, dq_ref, dcq_ref, dq_s, rs_s):
        qi, ki = pl.program_id(1), pl.program_id(2)

        @pl.when(ki == 0)
        def _():
            dq_s[...] = jnp.zeros_like(dq_s)
            rs_s[...] = jnp.zeros_like(rs_s)

        def step(diag):
            q2, k2, v2, do2 = q_ref[...], k_ref[...], v_ref[...], do_ref[...]
            dq = []
            for hh in range(2):
                _, ds = _fox_ds(q2, k2, v2, do2, aq_ref[...], ak_ref[...], ad_ref[...], hh, diag)
                dq.append(_dot(ds.astype(BF), k2))
                rs_s[hh] = rs_s[hh] + jnp.sum(ds, axis=1, keepdims=True)
            dq_s[...] += jnp.where(_first_half(), dq[0], dq[1])

        _diag_or_below(qi, ki, step)

        @pl.when(ki == nq - 1)
        def _():
            dq_ref[...] = (dq_s[...] * FOX_SCALE).astype(BF)
            dcq_ref[...] = jnp.where(_first_half(), rs_s[0], rs_s[1])

    qs = pl.BlockSpec((t, 128), lambda p, qi, ki: (qi, p))
    ks = pl.BlockSpec((t, 128), lambda p, qi, ki: (jnp.minimum(ki, qi), p))
    return _pcall(
        body, [fq, fk, fv, do, aqb, ak, ad], name="fox_bwd_q", grid=(npair, nq, nq),
        out_shape=[jax.ShapeDtypeStruct((t_tok, FOX_WIDTH), BF), jax.ShapeDtypeStruct((t_tok, FOX_WIDTH), F32)],
        in_specs=[qs, ks, ks, qs, qs, ks, qs], out_specs=[qs, qs],
        scratch=[pltpu.VMEM((t, 128), F32), pltpu.VMEM((2, t, 128), F32)], comm=comm)


def _ret_consts():
    c = RET_CHUNK
    log_gamma = jnp.log1p(-jnp.exp2(-5.0 - jnp.arange(RET_HEADS, dtype=F32)))
    idx = jnp.arange(c, dtype=F32)
    diff = idx[:, None] - idx[None, :]
    dmask = jnp.where(diff >= 0, jnp.exp(log_gamma[:, None, None] * jnp.maximum(diff, 0.0)), 0.0)
    qdec = jnp.exp(log_gamma[:, None] * (idx + 1.0))
    kdec = jnp.exp(log_gamma[:, None] * (c - 1 - idx))
    cdec = jnp.exp(log_gamma * c)
    bc = lambda v: jnp.broadcast_to(v[:, :, None], (RET_HEADS, c, RET_DIM))
    return dmask, bc(qdec), bc(kdec), jnp.broadcast_to(cdec[:, None, None], (RET_HEADS, c, RET_DIM))


def _group_norm(y):
    mu = jnp.mean(y, axis=-1, keepdims=True)
    yc = y - mu
    r = lax.rsqrt(jnp.mean(yc * yc, axis=-1, keepdims=True) + EPS)
    return yc * r, r


def _ret_fwd(rq, rk, rv, rg, consts, comm=None):
    t_tok = rq.shape[0]
    nb = 4 if t_tok % (4 * RET_CHUNK) == 0 else 1
    tr = nb * RET_CHUNK
    n_steps = t_tok // tr
    c = RET_CHUNK

    def body(q_ref, k_ref, v_ref, g_ref, dm_ref, qd_ref, kd_ref, cd_ref, y_ref, yo_ref, st_ref, s_s):
        @pl.when(pl.program_id(1) == 0)
        def _():
            s_s[...] = jnp.zeros_like(s_s)

        dm, qd, kd, cd = dm_ref[...], qd_ref[...], kd_ref[...], cd_ref[...]
        for b in range(nb):
            rows = slice(b * c, (b + 1) * c)
            q, k, v = q_ref[rows, :], k_ref[rows, :], v_ref[rows, :]
            state = s_s[...]
            st_ref[b] = state
            sc = (_dot_nt(q, k) * dm).astype(BF)
            y = _dot(sc, v) + _dot((q.astype(F32) * qd).astype(BF), state.astype(BF))
            s_s[...] = cd * state + _dot_tn((k.astype(F32) * kd).astype(BF), v)
            y_ref[rows, :] = y
            yn, _ = _group_norm(y)
            gate = g_ref[rows, :].astype(F32)
            yo_ref[rows, :] = (yn * (gate * _sigmoid(gate))).astype(BF)

    blk = pl.BlockSpec((tr, RET_DIM), lambda h, i: (i, h))
    cst = pl.BlockSpec((None, c, RET_DIM), lambda h, i: (h, 0, 0))
    return _pcall(
        body, [rq, rk, rv, rg, *consts], name="ret_fwd", grid=(RET_HEADS, n_steps),
        out_shape=[jax.ShapeDtypeStruct((t_tok, RET_WIDTH), F32), jax.ShapeDtypeStruct((t_tok, RET_WIDTH), BF),
                   jax.ShapeDtypeStruct((RET_HEADS, t_tok // c, RET_DIM, RET_DIM), F32)],
        in_specs=[blk] * 4 + [cst] * 4,
        out_specs=[blk, blk, pl.BlockSpec((None, nb, RET_DIM, RET_DIM), lambda h, i: (h, i, 0, 0))],
        scratch=[pltpu.VMEM((RET_DIM, RET_DIM), F32)], comm=comm)


def _ret_bwd(rq, rk, rv, rg, y_raw, dyo, states, consts, cos_t, sin_t, comm=None):
    t_tok = rq.shape[0]
    nb = 4 if t_tok % (4 * RET_CHUNK) == 0 else 1
    tr = nb * RET_CHUNK
    n_steps = t_tok // tr
    c = RET_CHUNK

    def body(q_ref, k_ref, v_ref, g_ref, y_ref, dyo_ref, st_ref, dm_ref, qd_ref, kd_ref, cd_ref,
             cos_ref, sin_ref, dq_ref, dk_ref, dv_ref, dg_ref, ds_s):
        @pl.when(pl.program_id(1) == 0)
        def _():
            ds_s[...] = jnp.zeros_like(ds_s)

        dm, qd, kd, cd = dm_ref[...], qd_ref[...], kd_ref[...], cd_ref[...]
        for b in reversed(range(nb)):
            rows = slice(b * c, (b + 1) * c)
            q, k, v = q_ref[rows, :], k_ref[rows, :], v_ref[rows, :]
            cosv, sinv = cos_ref[rows, :], sin_ref[rows, :]
            yn, r = _group_norm(y_ref[rows, :])
            gate = g_ref[rows, :].astype(F32)
            sg = _sigmoid(gate)
            dyo = dyo_ref[rows, :]
            dg_ref[rows, :] = (dyo * yn * (sg * (1.0 + gate * (1.0 - sg)))).astype(BF)
            dyn = dyo * (gate * sg)
            dy = r * (dyn - jnp.mean(dyn, axis=-1, keepdims=True)
                      - yn * jnp.mean(dyn * yn, axis=-1, keepdims=True))
            dyb = dy.astype(BF)
            state_b = st_ref[b].astype(BF)
            dstate = ds_s[...]
            dstate_b = dstate.astype(BF)
            qdb = (q.astype(F32) * qd).astype(BF)
            kdb = (k.astype(F32) * kd).astype(BF)
            sc = (_dot_nt(q, k) * dm).astype(BF)
            dv = _dot_tn(sc, dyb) + _dot(kdb, dstate_b)
            dp = (_dot_nt(dyb, v) * dm).astype(BF)
            dq = _dot(dp, k) + _dot_nt(dyb, state_b) * qd
            dk = (_dot_tn(dp, q) + _dot_nt(v, dstate_b) * kd) * RET_SCALE
            ds_s[...] = cd * dstate + _dot_tn(qdb, dyb)
            dv_ref[rows, :] = dv.astype(BF)
            dq_ref[rows, :] = (dq * cosv - _swap_pairs(dq) * sinv).astype(BF)
            dk_ref[rows, :] = (dk * cosv - _swap_pairs(dk) * sinv).astype(BF)

    rev = lambda i: n_steps - 1 - i
    blk = pl.BlockSpec((tr, RET_DIM), lambda h, i: (rev(i), h))
    tab = pl.BlockSpec((tr, RET_DIM), lambda h, i: (rev(i), 0))
    cst = pl.BlockSpec((None, c, RET_DIM), lambda h, i: (h, 0, 0))
    sds = jax.ShapeDtypeStruct((t_tok, RET_WIDTH), BF)
    return _pcall(
        body, [rq, rk, rv, rg, y_raw, dyo, states, *consts, cos_t, sin_t], name="ret_bwd",
        grid=(RET_HEADS, n_steps), out_shape=[sds] * 4,
        in_specs=[blk] * 6 + [pl.BlockSpec((None, nb, RET_DIM, RET_DIM), lambda h, i: (h, rev(i), 0, 0))]
        + [cst] * 4 + [tab, tab],
        out_specs=[blk] * 4, scratch=[pltpu.VMEM((RET_DIM, RET_DIM), F32)], comm=comm)


def _mix_out(h, y_ret, y_fox, ga, gb, wr4, wf4, wo4, comm=None):
    t_tok, d = h.shape
    cz = wr4.shape[-1]
    ro = wo4.shape[-2]
    tm = _tile(t_tok, 512)

    def body(h_ref, yr_ref, yf_ref, ga_ref, gb_ref, wr_ref, wf_ref, wo_ref, ho_ref, za_ref, zb_ref, mix_ref):
        yr, yf = yr_ref[...], yf_ref[...]
        for j in range(N_CHIPS):
            sl = slice(j * cz, (j + 1) * cz)
            za = _dot(yr, wr_ref[j])
            zb = _dot(yf, wf_ref[j])
            za_ref[:, sl] = za.astype(BF)
            zb_ref[:, sl] = zb.astype(BF)
            mix_ref[:, sl] = (ga_ref[:, sl].astype(F32) * za + gb_ref[:, sl].astype(F32) * zb).astype(BF)
        acc = h_ref[...]
        for j in range(N_CHIPS):
            acc = acc + _dot(mix_ref[:, j * ro:(j + 1) * ro], wo_ref[j])
        ho_ref[...] = acc

    row = lambda c: pl.BlockSpec((tm, c), lambda i: (i, 0))
    full = lambda *s: pl.BlockSpec(s, lambda i: (0,) * len(s))
    sds = lambda dt: jax.ShapeDtypeStruct((t_tok, d), dt)
    return _pcall(
        body, [h, y_ret, y_fox, ga, gb, wr4, wf4, wo4], name="mix_out", grid=(t_tok // tm,),
        out_shape=[sds(F32), sds(BF), sds(BF), sds(BF)],
        in_specs=[row(d), row(RET_WIDTH), row(FOX_WIDTH), row(d), row(d),
                  full(N_CHIPS, RET_WIDTH, cz), full(N_CHIPS, FOX_WIDTH, cz), full(N_CHIPS, ro, d)],
        out_specs=[row(d)] * 4, comm=comm)


def _mix_out_bwd(dh, za, zb, ga, gb, y_fox, wr4, wf4, wo4, comm=None):
    t_tok, d = dh.shape
    cz = wr4.shape[-1]
    ro = wo4.shape[-2]
    tm = _tile(t_tok, 256)

    def body(dh_ref, za_ref, zb_ref, ga_ref, gb_ref, yf_ref, wr_ref, wf_ref, wo_ref,
             dhb_ref, dgp_ref, dza_ref, dzb_ref, dyr_ref, dyf_ref, dl_ref, db_ref):
        @pl.when(pl.program_id(0) == 0)
        def _():
            db_ref[...] = jnp.zeros_like(db_ref)

        dhb = dh_ref[...].astype(BF)
        dhb_ref[...] = dhb
        dyr = jnp.zeros((tm, RET_WIDTH), F32)
        dyf = jnp.zeros((tm, FOX_WIDTH), F32)
        for j in range(N_CHIPS):
            sl = slice(j * ro, (j + 1) * ro)
            dmix = _dot_nt(dhb, wo_ref[j])
            ga, gb = ga_ref[:, sl].astype(F32), gb_ref[:, sl].astype(F32)
            dza = (dmix * ga).astype(BF)
            dzb = (dmix * gb).astype(BF)
            dza_ref[:, sl] = dza
            dzb_ref[:, sl] = dzb
            dga = dmix * za_ref[:, sl].astype(F32) * ga * (1.0 - ga)
            dgb = dmix * zb_ref[:, sl].astype(F32) * gb * (1.0 - gb)
            dgp_ref[:, sl] = dga.astype(BF)
            dgp_ref[:, d + j * ro:d + (j + 1) * ro] = dgb.astype(BF)
            db_ref[:, sl] += jnp.sum(dga, axis=0, keepdims=True)
            db_ref[:, d + j * ro:d + (j + 1) * ro] += jnp.sum(dgb, axis=0, keepdims=True)
        for j in range(N_CHIPS):
            sl = slice(j * cz, (j + 1) * cz)
            dyr = dyr + _dot_nt(dza_ref[:, sl], wr_ref[j])
            dyf = dyf + _dot_nt(dzb_ref[:, sl], wf_ref[j])
        dyr_ref[...] = dyr
        dyfb = dyf.astype(BF)
        dyf_ref[...] = dyfb
        prod = dyfb.astype(F32) * yf_ref[...]
        first = _first_half()
        for pp in range(FOX_HEADS // 2):
            blk = prod[:, pp * 128:(pp + 1) * 128]
            s0 = jnp.sum(jnp.where(first, blk, 0.0), axis=1, keepdims=True)
            s1 = jnp.sum(jnp.where(first, 0.0, blk), axis=1, keepdims=True)
            parts = _split3(-jnp.where(first, s1, s0))
            dl_ref[:, pp * 128:(pp + 1) * 128] = _aug_put(jnp.zeros((tm, 128), BF), 0, parts)

    row = lambda c: pl.BlockSpec((tm, c), lambda i: (i, 0))
    full = lambda *s: pl.BlockSpec(s, lambda i: (0,) * len(s))
    sds = lambda c, dt: jax.ShapeDtypeStruct((t_tok, c), dt)
    return _pcall(
        body, [dh, za, zb, ga, gb, y_fox, wr4, wf4, wo4], name="mix_out_bwd", grid=(t_tok // tm,),
        out_shape=[sds(d, BF), sds(2 * d, BF), sds(d, BF), sds(d, BF), sds(RET_WIDTH, F32),
                   sds(FOX_WIDTH, BF), sds(FOX_WIDTH, BF), jax.ShapeDtypeStruct((1, 2 * d), F32)],
        in_specs=[row(d)] * 5 + [row(FOX_WIDTH), full(N_CHIPS, RET_WIDTH, cz), full(N_CHIPS, FOX_WIDTH, cz),
                                 full(N_CHIPS, ro, d)],
        out_specs=[row(d), row(2 * d), row(d), row(d), row(RET_WIDTH), row(FOX_WIDTH), row(FOX_WIDTH),
                   full(1, 2 * d)],
        comm=comm)


def _mix_in_bwd(dh, h, ln, parts, dff, dgpre, w_in, wm4, comm=None):
    t_tok, d = h.shape
    cm = wm4.shape[-1]
    tm = _tile(t_tok, 256)

    def body(dh_ref, h_ref, ln_ref, p0, p1, p2, p3, p4, p5, p6, dff_ref, dgp_ref, win_ref, wm_ref,
             dhi_ref, dln_ref, dproj_ref):
        @pl.when(pl.program_id(0) == 0)
        def _():
            dln_ref[...] = jnp.zeros_like(dln_ref)

        for k, pr in enumerate((p0, p1, p2, p3, p4, p5, p6)):
            dproj_ref[:, k * 512:(k + 1) * 512] = pr[...]
        dproj_ref[:, FF_COL:FF_COL + 128] = dff_ref[...]
        dproj_ref[:, FF_COL + 128:] = jnp.zeros((tm, IN_PAD - FF_COL - 128), BF)
        du = _dot_nt(dproj_ref[...], win_ref[...])
        for j in range(N_CHIPS):
            du = du + _dot_nt(dgp_ref[:, j * cm:(j + 1) * cm], wm_ref[j])
        xv = h_ref[...]
        dx, dln = _rms_bwd(du, xv, _rstd(xv), ln_ref[...])
        dln_ref[...] += dln
        dhi_ref[...] = dh_ref[...] + dx

    row = lambda c: pl.BlockSpec((tm, c), lambda i: (i, 0))
    full = lambda *s: pl.BlockSpec(s, lambda i: (0,) * len(s))
    return _pcall(
        body, [dh, h, ln, *parts, dff, dgpre, w_in, wm4], name="mix_in_bwd", grid=(t_tok // tm,),
        out_shape=[jax.ShapeDtypeStruct((t_tok, d), F32), jax.ShapeDtypeStruct((1, d), F32),
                   jax.ShapeDtypeStruct((t_tok, IN_PAD), BF)],
        in_specs=[row(d), row(d), full(1, d)] + [row(512)] * 7 + [row(128), row(2 * d), full(d, IN_PAD),
                                                                   full(N_CHIPS, d, cm)],
        out_specs=[row(d), full(1, d), row(IN_PAD)], comm=comm)


def _tail(h, p, target, ln_ple, ln_fin, wpg4, wpl4, comm=None):
    t_tok, d = h.shape
    pd = p.shape[1]
    rg = wpg4.shape[-2]
    cp = wpl4.shape[-1]
    tm = _tile(t_tok, 256)

    def body(h_ref, p_ref, t_ref, lp_ref, lf_ref, wg_ref, wp_ref,
             dh_ref, n_ref, dgp_ref, dpe_ref, pb_ref, loss_ref, dlf_ref, dlp_ref, pe_s, dn_s):
        @pl.when(pl.program_id(0) == 0)
        def _():
            loss_ref[...] = jnp.zeros_like(loss_ref)
            dlf_ref[...] = jnp.zeros_like(dlf_ref)
            dlp_ref[...] = jnp.zeros_like(dlp_ref)

        xv = h_ref[...]
        r3 = _rstd(xv)
        nb = (xv * r3 * lp_ref[...]).astype(BF)
        n_ref[...] = nb
        pb = p_ref[...].astype(BF)
        pb_ref[...] = pb
        pgpre = jnp.zeros((tm, d), F32)
        for j in range(N_CHIPS):
            pgpre = pgpre + _dot(nb[:, j * rg:(j + 1) * rg], wg_ref[j])
            pe_s[:, j * cp:(j + 1) * cp] = _dot(pb, wp_ref[j])
        pg = _sigmoid(pgpre)
        pe = pe_s[...]
        h4 = xv + pg * pe
        r4 = _rstd(h4)
        err = h4 * r4 * lf_ref[...] - t_ref[...]
        loss_ref[...] += 0.5 * jnp.sum(jnp.sum(err * err, axis=1, keepdims=True), axis=0, keepdims=True) / d
        dh4, dlf = _rms_bwd(err * (1.0 / d), h4, r4, lf_ref[...])
        dlf_ref[...] += dlf
        dpe_ref[...] = (dh4 * pg).astype(BF)
        dgp = (dh4 * pe * pg * (1.0 - pg)).astype(BF)
        dgp_ref[...] = dgp
        for j in range(N_CHIPS):
            dn_s[:, j * rg:(j + 1) * rg] = _dot_nt(dgp, wg_ref[j])
        dx, dlp = _rms_bwd(dn_s[...], xv, r3, lp_ref[...])
        dlp_ref[...] += dlp
        dh_ref[...] = dh4 + dx

    row = lambda c: pl.BlockSpec((tm, c), lambda i: (i, 0))
    full = lambda *s: pl.BlockSpec(s, lambda i: (0,) * len(s))
    sds = lambda c, dt: jax.ShapeDtypeStruct((t_tok, c), dt)
    vec = jax.ShapeDtypeStruct((1, d), F32)
    return _pcall(
        body, [h, p, target, ln_ple, ln_fin, wpg4, wpl4], name="tail", grid=(t_tok // tm,),
        out_shape=[sds(d, F32), sds(d, BF), sds(d, BF), sds(d, BF), sds(pd, BF),
                   jax.ShapeDtypeStruct((1, 128), F32), vec, vec],
        in_specs=[row(d), row(pd), row(d), full(1, d), full(1, d), full(N_CHIPS, rg, d), full(N_CHIPS, pd, cp)],
        out_specs=[row(d), row(d), row(d), row(d), row(pd), full(1, 128), full(1, d), full(1, d)],
        scratch=[pltpu.VMEM((tm, d), F32), pltpu.VMEM((tm, d), F32)], comm=comm)


BIG = ["w_ffn1_gate", "w_ffn1_up", "w_ffn1_down", "w_in", "w_merge", "w_ret_out", "w_fox_out", "w_out",
       "w_ffn2_gate", "w_ffn2_up", "w_ffn2_down", "w_ple", "w_ple_gate"]
SMALL = ["ln_ffn1", "ln_mix", "b_forget", "b_merge", "ln_ffn2", "ln_ple", "ln_final"]
WEIGHTS = ["ln_ffn1", "w_ffn1_gate", "w_ffn1_up", "w_ffn1_down", "ln_mix", "w_in", "b_forget", "w_merge", "b_merge",
           "w_ret_out", "w_fox_out", "w_out", "ln_ffn2", "w_ffn2_gate", "w_ffn2_up", "w_ffn2_down", "ln_ple",
           "w_ple", "w_ple_gate", "ln_final"]


def _pack_small(vals):
    rows = []
    for name in SMALL:
        v = vals[name].reshape(-1)
        n = -(-v.shape[0] // 128) * 128
        rows.append(jnp.pad(v, (0, n - v.shape[0])).reshape(n // 128, 128))
    packed = jnp.concatenate(rows, axis=0)
    pad = -packed.shape[0] % 8
    return jnp.pad(packed, ((0, pad), (0, 0)))


def _unpack_small(packed, sizes):
    out, r = {}, 0
    for name in SMALL:
        n = sizes[name]
        nr = -(-n // 128)
        out[name] = packed[r:r + nr].reshape(1, nr * 128)[:, :n]
        r += nr
    return out


class _Stage:
    def __init__(self, comm, finish):
        self.comm, self.finish, self.result = comm, finish, None


def _hosted(fn, *a, stages=()):
    if not stages:
        return fn(*a)
    outs, couts = fn(*a, comm=_merge([st.comm for st in stages]))
    for st, o in zip(stages, _split_outs([st.comm for st in stages], couts)):
        st.result = st.finish(o)
    return outs


class _Reducer:
    def __init__(self):
        self.done = {}

    def swap(self, grads):
        names = list(grads)
        return _Stage(_c_half_swap([grads[n] for n in names]),
                      lambda outs: {n: _add_halves(grads[n], o) for n, o in zip(names, outs)})

    def exchange(self, parts):
        names = list(parts)
        return _Stage(_c_chip_exchange([parts[n] for n in names]),
                      lambda outs: {n: _sum_chips(o) for n, o in zip(names, outs)})

    def join(self, halves):
        names = list(halves)
        return _Stage(_c_join([halves[n] for n in names]), lambda outs: self.done.update(zip(names, outs)))


def kernel(x, p, positions, ln_ffn1, w_ffn1_gate, w_ffn1_up, w_ffn1_down, ln_mix, w_in, b_forget, w_merge, b_merge, w_ret_out, w_fox_out, w_out, ln_ffn2, w_ffn2_gate, w_ffn2_up, w_ffn2_down, ln_ple, w_ple, w_ple_gate, ln_final, loss_target, m_ln_ffn1, m_w_ffn1_gate, m_w_ffn1_up, m_w_ffn1_down, m_ln_mix, m_w_in, m_b_forget, m_w_merge, m_b_merge, m_w_ret_out, m_w_fox_out, m_w_out, m_ln_ffn2, m_w_ffn2_gate, m_w_ffn2_up, m_w_ffn2_down, m_ln_ple, m_w_ple, m_w_ple_gate, m_ln_final, v_ln_ffn1, v_w_ffn1_gate, v_w_ffn1_up, v_w_ffn1_down, v_ln_mix, v_w_in, v_b_forget, v_w_merge, v_b_merge, v_w_ret_out, v_w_fox_out, v_w_out, v_ln_ffn2, v_w_ffn2_gate, v_w_ffn2_up, v_w_ffn2_down, v_ln_ple, v_w_ple, v_w_ple_gate, v_ln_final):
    args = dict(locals())
    w = {n: args[n] for n in WEIGHTS}
    m = {n: args["m_" + n] for n in WEIGHTS}
    v = {n: args["v_" + n] for n in WEIGHTS}
    d = x.shape[-1]
    t_tok = x.shape[1]
    xs, ps, target = x[0], p[0, 0], loss_target[0]
    small = {n: w[n].reshape(1, -1) for n in SMALL}
    shard = {n: w[n][0].astype(BF).reshape(2, w[n].shape[1] // 2, w[n].shape[2]) for n in BIG}
    full = {}

    def gather(names):
        def finish(outs):
            full.update({n: o.reshape((N_CHIPS,) + w[n].shape[1:]) for n, o in zip(names, outs)})

        return _Stage(_c_all_gather([shard[n] for n in names]), finish)

    half = RET_DIM // 2
    inv_freq = 1.0 / (ROPE_BASE ** (jnp.arange(half, dtype=F32) / half))
    cos_t, sin_t = _hosted(_rope_tables, positions[0].astype(F32).reshape(t_tok, 1),
                           jnp.repeat(inv_freq, 2).reshape(1, RET_DIM),
                           stages=[gather(["w_ffn1_gate", "w_ffn1_up", "w_ffn1_down"])])
    consts = _ret_consts()
    b_pad = jnp.pad(small["b_forget"], ((0, 0), (0, 128 - FOX_HEADS)))

    h1, n1, g1, u1 = _hosted(_ffn_fwd, xs, small["ln_ffn1"], full["w_ffn1_gate"], full["w_ffn1_up"],
                             full["w_ffn1_down"], stages=[gather(["w_in", "w_merge"])])
    w_in_full = jnp.pad(jnp.transpose(full["w_in"], (1, 0, 2)).reshape(d, IN_COLS), ((0, 0), (0, IN_PAD - IN_COLS)))
    u, rq, rk, rv, rg, fq, fk, fv, ffl, ga, gb = _hosted(
        _mix_in, h1, small["ln_mix"], w_in_full, full["w_merge"], small["b_merge"], cos_t, sin_t,
        stages=[gather(["w_ret_out", "w_fox_out", "w_out", "w_ple_gate", "w_ple"])])
    aq, ak = _forget_fwd(ffl, b_pad)
    y_raw, y_ret, states = _ret_fwd(rq, rk, rv, rg, consts)
    y_fox, y_fox32, lse_e = _hosted(_fox_fwd, fq, fk, fv, aq, ak,
                                    stages=[gather(["w_ffn2_gate", "w_ffn2_up", "w_ffn2_down"])])
    aqb = _fox_aug_lse(aq, lse_e)
    h2, za, zb, mix = _mix_out(h1, y_ret, y_fox, ga, gb, full["w_ret_out"], full["w_fox_out"], full["w_out"])
    h3, n2, g2, u2 = _ffn_fwd(h2, small["ln_ffn2"], full["w_ffn2_gate"], full["w_ffn2_up"], full["w_ffn2_down"])

    red = _Reducer()
    dh3, n3, dpgpre, dpe, pb, loss, dln_final, dln_ple = _tail(
        h3, ps, target, small["ln_ple"], small["ln_final"], full["w_ple_gate"], full["w_ple"])
    g_ple = dict(w_ple_gate=_wgrad_rows("wgrad_ple_gate", n3, dpgpre, N_CHIPS),
                 w_ple=_wgrad_cols("wgrad_ple", pb, dpe, N_CHIPS))

    sw_ple = red.swap(g_ple)
    dh2, dln_ffn2, dg2, du2, a2, dhb3 = _hosted(
        _ffn_bwd, dh3, h2, small["ln_ffn2"], g2, u2, full["w_ffn2_gate"], full["w_ffn2_up"], full["w_ffn2_down"],
        stages=[sw_ple])
    ex_ple = red.exchange(sw_ple.result)
    g_f2 = dict(w_ffn2_gate=_hosted(_wgrad_a_shared, "wgrad_ffn2_gate", n2, dg2, stages=[ex_ple]))
    g_f2["w_ffn2_up"] = _wgrad_a_shared("wgrad_ffn2_up", n2, du2)
    g_f2["w_ffn2_down"] = _wgrad_b_shared("wgrad_ffn2_down", a2, dhb3)

    sw_f2 = red.swap(g_f2)
    dhb2, dgpre, dza, dzb, dy_ret, dy_fox, ad, db_merge = _hosted(
        _mix_out_bwd, dh2, za, zb, ga, gb, y_fox32, full["w_ret_out"], full["w_fox_out"], full["w_out"],
        stages=[sw_f2, red.join(ex_ple.result)])
    g_br = dict(w_out=_wgrad_rows("wgrad_out", mix, dhb2, N_CHIPS),
                w_ret_out=_wgrad_cols("wgrad_ret_out", y_ret, dza, N_CHIPS),
                w_fox_out=_wgrad_cols("wgrad_fox_out", y_fox, dzb, N_CHIPS))

    sw_br = red.swap(g_br)
    drq, drk, drv, drg = _hosted(_ret_bwd, rq, rk, rv, rg, y_raw, dy_ret, states, consts, cos_t, sin_t,
                                 stages=[sw_br])
    ex_f2, ex_br = red.exchange(sw_f2.result), red.exchange(sw_br.result)
    dfk, dfv, dcum_t3 = _hosted(_fox_bwd_kv, fq, fk, fv, dy_fox, aqb, ak, ad, stages=[ex_f2, ex_br])
    dfq, dcum_q = _hosted(_fox_bwd_q, fq, fk, fv, dy_fox, aqb, ak, ad,
                          stages=[red.join(ex_f2.result), red.join(ex_br.result)])
    dff, db_forget = _forget_bwd(dcum_t3.reshape(FOX_HEADS, t_tok), dcum_q, ffl, b_pad)
    dh1, dln_mix, dproj = _mix_in_bwd(dh2, h1, small["ln_mix"], (drq, drk, drv, drg, dfq, dfk, dfv), dff, dgpre,
                                      w_in_full, full["w_merge"])

    g_in = _wgrad_cols("wgrad_in", u, dproj, IN_PAD // 512)
    g_in = jnp.transpose(g_in, (1, 0, 2)).reshape(d, IN_PAD)[:, :IN_COLS]
    g_in = jnp.transpose(g_in.reshape(d, N_CHIPS, IN_COLS // N_CHIPS), (1, 0, 2))
    sw_in = red.swap(dict(w_in=g_in))
    g_mrg = _hosted(_wgrad_cols, "wgrad_merge", u, dgpre, N_CHIPS, stages=[sw_in])

    sw_mrg, ex_in = red.swap(dict(w_merge=g_mrg)), red.exchange(sw_in.result)
    dx, dln_ffn1, dg1, du1, a1, dhb1 = _hosted(
        _ffn_bwd, dh1, xs, small["ln_ffn1"], g1, u1, full["w_ffn1_gate"], full["w_ffn1_up"], full["w_ffn1_down"],
        stages=[sw_mrg, ex_in])

    ex_mrg = red.exchange(sw_mrg.result)
    g_f1g = _hosted(_wgrad_a_shared, "wgrad_ffn1_gate", n1, dg1, stages=[ex_mrg, red.join(ex_in.result)])
    sw_f1g = red.swap(dict(w_ffn1_gate=g_f1g))
    g_f1u = _hosted(_wgrad_a_shared, "wgrad_ffn1_up", n1, du1, stages=[sw_f1g])
    ex_f1g, sw_f1u = red.exchange(sw_f1g.result), red.swap(dict(w_ffn1_up=g_f1u))
    g_f1d = _hosted(_wgrad_b_shared, "wgrad_ffn1_down", a1, dhb1,
                    stages=[ex_f1g, sw_f1u, red.join(ex_mrg.result)])

    small_grads = dict(ln_ffn1=dln_ffn1, ln_mix=dln_mix, b_forget=db_forget[:, :FOX_HEADS], b_merge=db_merge,
                       ln_ffn2=dln_ffn2, ln_ple=dln_ple, ln_final=dln_final)
    sizes = {n: w[n].size for n in SMALL}
    gsum = _unpack_small(_all_reduce_small(_pack_small(small_grads)), sizes)
    loss = lax.psum(loss[0, 0], ("x", "y", "c"))

    results = {}

    def update(n, stages=()):
        shp = w[n].shape
        two_d = (shp[-2], shp[-1]) if len(shp) == 3 else (1, shp[-1])
        g2d = (gsum[n] if n in gsum else red.done[n]).reshape(two_d)
        dl, nm, nv = _hosted(_adamw, w[n].reshape(two_d), g2d, m[n].reshape(two_d), v[n].reshape(two_d),
                             stages=stages)
        results[n] = tuple(a.reshape(shp) for a in (g2d, dl, nm, nv))

    ex_f1u, sw_f1d = red.exchange(sw_f1u.result), red.swap(dict(w_ffn1_down=g_f1d))
    update("w_ffn2_gate", stages=[ex_f1u, sw_f1d, red.join(ex_f1g.result)])
    ex_f1d = red.exchange(sw_f1d.result)
    update("w_ffn2_up", stages=[ex_f1d, red.join(ex_f1u.result)])
    update("w_ffn2_down", stages=[red.join(ex_f1d.result)])
    for n in WEIGHTS:
        if n not in results:
            update(n)

    outs = [[results[n][k] for n in WEIGHTS] for k in range(4)]
    return (loss, dx[None], *outs[0], *outs[1], *outs[2], *outs[3])
```

```python
import functools
import operator

import jax
import jax.numpy as jnp
from jax import lax
from jax.experimental import pallas as pl
from jax.experimental.pallas import tpu as pltpu

F32 = jnp.float32
BF = jnp.bfloat16
MESH = pl.DeviceIdType.MESH

EPS = 1e-6
ROPE_BASE = 10000.0
N_CHIPS = 4
RET_HEADS = 4
RET_DIM = 128
RET_WIDTH = RET_HEADS * RET_DIM
RET_CHUNK = 128
RET_SCALE = RET_DIM ** -0.5
FOX_HEADS = 8
FOX_DIM = 64
FOX_WIDTH = FOX_HEADS * FOX_DIM
FOX_SCALE = FOX_DIM ** -0.5
IN_COLS = 4 * RET_WIDTH + 3 * FOX_WIDTH + FOX_HEADS
IN_PAD = 4096
FF_COL = 4 * RET_WIDTH + 3 * FOX_WIDTH
NEG = -1e30

ADAM_LR = 0.001
ADAM_B1 = 0.9
ADAM_B2 = 0.999
ADAM_EPS = 1e-08
ADAM_WD = 0.01
ADAM_STEP = 10

VMEM_LIMIT = 52 * 1024 * 1024

NT = (((1,), (1,)), ((), ()))
TN = (((0,), (0,)), ((), ()))

HBM_SPEC = pl.BlockSpec(memory_space=pltpu.HBM)
VMEM_SPEC = pl.BlockSpec(memory_space=pltpu.VMEM)


def _dot(a, b):
    return jnp.dot(a, b, preferred_element_type=F32)


def _dot_nt(a, b):
    return lax.dot_general(a, b, NT, preferred_element_type=F32)


def _dot_tn(a, b):
    return lax.dot_general(a, b, TN, preferred_element_type=F32)


def _rstd(xv):
    return lax.rsqrt(jnp.mean(xv * xv, axis=-1, keepdims=True) + EPS)


def _rms_bwd(dn, xv, r, ln):
    xh = xv * r
    dxh = dn * ln
    dx = r * (dxh - xh * jnp.mean(dxh * xh, axis=-1, keepdims=True))
    return dx, jnp.sum(dn * xh, axis=0, keepdims=True)


def _sigmoid(x):
    return jax.nn.sigmoid(x)


def _tile(n, pref):
    return pref if n % pref == 0 else n


def _row_tile(n, cap):
    best = [t for t in range(16, min(n, cap) + 1, 16) if n % t == 0]
    return best[-1] if best else n


class _Comm:
    def __init__(self, ins, out_shapes, sems, start, wait):
        self.ins, self.out_shapes, self.sems, self.start, self.wait = list(ins), list(out_shapes), list(sems), start, wait


def _merge(comms):
    comms = [c for c in comms if c is not None]
    if not comms:
        return None
    bounds, ni, no, ns = [], 0, 0, 0
    for c in comms:
        bounds.append((ni, no, ns))
        ni, no, ns = ni + len(c.ins), no + len(c.out_shapes), ns + len(c.sems)

    def run(which):
        def f(ins, outs, sems):
            for c, (i, o, s) in zip(comms, bounds):
                getattr(c, which)(ins[i:i + len(c.ins)], outs[o:o + len(c.out_shapes)], sems[s:s + len(c.sems)])
        return f

    return _Comm([a for c in comms for a in c.ins], [a for c in comms for a in c.out_shapes],
                 [a for c in comms for a in c.sems], run("start"), run("wait"))


def _split_outs(comms, outs):
    res, o = [], 0
    for c in comms:
        if c is not None:
            res.append(list(outs[o:o + len(c.out_shapes)]))
            o += len(c.out_shapes)
    return res


def _pcall(body, args, *, name, out_shape, grid=(), in_specs=None, out_specs=None, scratch=(), comm=None):
    many = isinstance(out_shape, (list, tuple))
    outs = list(out_shape) if many else [out_shape]
    n_in, n_out, n_scr = len(args), len(outs), len(scratch)
    if in_specs is None:
        in_specs, out_specs = [VMEM_SPEC] * n_in, [VMEM_SPEC] * n_out
    else:
        in_specs, out_specs = list(in_specs), (list(out_specs) if many else [out_specs])
    params = pltpu.CompilerParams(dimension_semantics=("arbitrary",) * len(grid), vmem_limit_bytes=VMEM_LIMIT)
    if comm is None:
        res = pl.pallas_call(body, name=name, grid=grid, out_shape=outs, in_specs=in_specs, out_specs=out_specs,
                             scratch_shapes=list(scratch), compiler_params=params)(*args)
        return list(res) if many else res[0]
    ci, co = len(comm.ins), len(comm.out_shapes)

    def wrapped(*refs):
        a, ca = refs[:n_in], refs[n_in:n_in + ci]
        o = refs[n_in + ci:n_in + ci + n_out]
        cout = refs[n_in + ci + n_out:n_in + ci + n_out + co]
        s = refs[n_in + ci + n_out + co:n_in + ci + n_out + co + n_scr]
        csem = refs[n_in + ci + n_out + co + n_scr:]
        if grid:
            first = functools.reduce(operator.and_, [pl.program_id(k) == 0 for k in range(len(grid))])
            last = functools.reduce(operator.and_, [pl.program_id(k) == grid[k] - 1 for k in range(len(grid))])
            pl.when(first)(lambda: comm.start(ca, cout, csem))
            body(*a, *o, *s)
            pl.when(last)(lambda: comm.wait(ca, cout, csem))
        else:
            comm.start(ca, cout, csem)
            body(*a, *o, *s)
            comm.wait(ca, cout, csem)

    res = pl.pallas_call(
        wrapped, name=name, grid=grid, out_shape=outs + comm.out_shapes,
        in_specs=in_specs + [HBM_SPEC] * ci, out_specs=out_specs + [HBM_SPEC] * co,
        scratch_shapes=list(scratch) + comm.sems, compiler_params=params)(*args, *comm.ins)
    mine = list(res[:n_out])
    return (mine if many else mine[0]), list(res[n_out:])


def _peer_chips(x, y):
    return [(1 - x, y), (x, 1 - y), (1 - x, 1 - y)]


def _c_all_gather(shards):
    n = len(shards)

    def copies(ins, outs, sems):
        send_sems, recv_sems, fwd_send, fwd_recv, local_sems = sems
        x, y, c = lax.axis_index("x"), lax.axis_index("y"), lax.axis_index("c")
        me = 2 * x + y
        peers = _peer_chips(x, y)
        chip = [2 * px + py for px, py in peers]

        def ici(g, j, slot):
            return pltpu.make_async_remote_copy(
                src_ref=ins[g].at[c], dst_ref=outs[g].at[slot, c], send_sem=send_sems.at[g, j],
                recv_sem=recv_sems.at[g, j], device_id=(*peers[j], c), device_id_type=MESH)

        def d2d(g, j, half):
            return pltpu.make_async_remote_copy(
                src_ref=outs[g].at[chip[j], half], dst_ref=outs[g].at[chip[j], half], send_sem=fwd_send.at[g, j],
                recv_sem=fwd_recv.at[g, j], device_id=(x, y, 1 - c), device_id_type=MESH)

        pairs = [(g, j) for g in range(n) for j in range(3)]
        local = [pltpu.make_async_copy(ins[g], outs[g].at[me], local_sems.at[g]) for g in range(n)]
        sends = [ici(g, j, me) for g, j in pairs]
        recvs = [ici(g, j, chip[j]) for g, j in pairs]
        passes = [d2d(g, j, c) for g, j in pairs]
        passed = [d2d(g, j, 1 - c) for g, j in pairs]
        return local, sends, recvs, passes, passed

    def start(ins, outs, sems):
        local, sends, _, _, _ = copies(ins, outs, sems)
        for cp in local + sends:
            cp.start()

    def wait(ins, outs, sems):
        local, sends, recvs, passes, passed = copies(ins, outs, sems)
        for rcv, fwd in zip(recvs, passes):
            rcv.wait_recv()
            fwd.start()
        for cp in passed:
            cp.wait_recv()
        for cp in sends + passes:
            cp.wait_send()
        for cp in local:
            cp.wait()

    pair_sems = pltpu.SemaphoreType.DMA((n, 3))
    return _Comm(
        shards, [jax.ShapeDtypeStruct((N_CHIPS,) + s.shape, s.dtype) for s in shards],
        [pair_sems, pair_sems, pair_sems, pair_sems, pltpu.SemaphoreType.DMA((n,))], start, wait)


def _start_wait(copies):
    def start(ins, outs, sems):
        local, sends, _ = copies(ins, outs, sems)
        for cp in local + sends:
            cp.start()

    def wait(ins, outs, sems):
        local, sends, recvs = copies(ins, outs, sems)
        for cp in recvs:
            cp.wait_recv()
        for cp in sends:
            cp.wait_send()
        for cp in local:
            cp.wait()

    return start, wait


def _c_half_swap(grads):
    n = len(grads)

    def copies(ins, outs, sems):
        send_sems, recv_sems = sems
        x, y, c = lax.axis_index("x"), lax.axis_index("y"), lax.axis_index("c")
        sends = []
        for g in range(n):
            half = ins[g].shape[1] // 2
            sends.append(pltpu.make_async_remote_copy(
                src_ref=ins[g].at[:, pl.ds((1 - c) * half, half), :], dst_ref=outs[g],
                send_sem=send_sems.at[g], recv_sem=recv_sems.at[g], device_id=(x, y, 1 - c), device_id_type=MESH))
        return [], sends, sends

    return _Comm(
        grads, [jax.ShapeDtypeStruct((N_CHIPS, s.shape[1] // 2, s.shape[2]), s.dtype) for s in grads],
        [pltpu.SemaphoreType.DMA((n,)), pltpu.SemaphoreType.DMA((n,))], *_start_wait(copies))


def _c_chip_exchange(parts):
    n = len(parts)

    def copies(ins, outs, sems):
        send_sems, recv_sems, local_sems = sems
        x, y, c = lax.axis_index("x"), lax.axis_index("y"), lax.axis_index("c")
        me = 2 * x + y
        peers = _peer_chips(x, y)

        def remote(g, j, src_slot, dst_slot):
            return pltpu.make_async_remote_copy(
                src_ref=ins[g].at[src_slot], dst_ref=outs[g].at[dst_slot], send_sem=send_sems.at[g, j],
                recv_sem=recv_sems.at[g, j], device_id=(*peers[j], c), device_id_type=MESH)

        chip = [2 * px + py for px, py in peers]
        local = [pltpu.make_async_copy(ins[g].at[me], outs[g].at[me], local_sems.at[g]) for g in range(n)]
        sends = [remote(g, j, chip[j], me) for g in range(n) for j in range(3)]
        recvs = [remote(g, j, me, chip[j]) for g in range(n) for j in range(3)]
        return local, sends, recvs

    return _Comm(
        parts, [jax.ShapeDtypeStruct(s.shape, s.dtype) for s in parts],
        [pltpu.SemaphoreType.DMA((n, 3)), pltpu.SemaphoreType.DMA((n, 3)), pltpu.SemaphoreType.DMA((n,))],
        *_start_wait(copies))


def _c_join(halves):
    n = len(halves)

    def copies(ins, outs, sems):
        send_sems, recv_sems, local_sems = sems
        x, y, c = lax.axis_index("x"), lax.axis_index("y"), lax.axis_index("c")

        def remote(g, slot):
            return pltpu.make_async_remote_copy(
                src_ref=ins[g], dst_ref=outs[g].at[slot], send_sem=send_sems.at[g], recv_sem=recv_sems.at[g],
                device_id=(x, y, 1 - c), device_id_type=MESH)

        local = [pltpu.make_async_copy(ins[g], outs[g].at[c], local_sems.at[g]) for g in range(n)]
        return local, [remote(g, c) for g in range(n)], [remote(g, 1 - c) for g in range(n)]

    return _Comm(
        halves, [jax.ShapeDtypeStruct((2,) + s.shape, s.dtype) for s in halves],
        [pltpu.SemaphoreType.DMA((n,)), pltpu.SemaphoreType.DMA((n,)), pltpu.SemaphoreType.DMA((n,))],
        *_start_wait(copies))


def _all_reduce_small(v):
    rows = v.shape[0]

    def body(v_ref, out_ref, buf, send_sems, recv_sems):
        x, y, c = lax.axis_index("x"), lax.axis_index("y"), lax.axis_index("c")
        me = 4 * x + 2 * y + c
        buf[me] = v_ref[...]
        flips = [(fx, fy, fc) for fx in (0, 1) for fy in (0, 1) for fc in (0, 1)][1:]

        def peer(k):
            fx, fy, fc = flips[k]
            px, py, pc = x ^ fx, y ^ fy, c ^ fc
            return (px, py, pc), 4 * px + 2 * py + pc

        def copy(k, slot):
            return pltpu.make_async_remote_copy(
                src_ref=buf.at[slot], dst_ref=buf.at[slot], send_sem=send_sems.at[k],
                recv_sem=recv_sems.at[k], device_id=peer(k)[0], device_id_type=MESH)

        sends = [copy(k, me) for k in range(7)]
        for cp in sends:
            cp.start()
        for k in range(7):
            copy(k, peer(k)[1]).wait_recv()
        for cp in sends:
            cp.wait_send()
        acc = buf[0]
        for d in range(1, 8):
            acc = acc + buf[d]
        out_ref[...] = acc

    return _pcall(body, [v], name="all_reduce_small", out_shape=jax.ShapeDtypeStruct((rows, 128), F32),
                  scratch=[pltpu.VMEM((8, rows, 128), F32), pltpu.SemaphoreType.DMA((7,)),
                           pltpu.SemaphoreType.DMA((7,))])


def _add_halves(g, got):
    _, h, c = got.shape
    th = _row_tile(h, 512)
    nh = h // th
    half = lax.axis_index("c") * nh

    def body(h_ref, a_ref, b_ref, o_ref):
        o_ref[...] = (a_ref[...].astype(F32) + b_ref[...].astype(F32)).astype(o_ref.dtype)

    spec = pl.BlockSpec((1, th, c), lambda j, i, h_ref: (j, i, 0))
    mine = pl.BlockSpec((1, th, c), lambda j, i, h_ref: (j, h_ref[0] + i, 0))
    return _pcall_prefetch(body, half, [g, got], name="add_halves", grid=(N_CHIPS, nh),
                           out_shape=jax.ShapeDtypeStruct(got.shape, BF), in_specs=[mine, spec], out_specs=spec)


def _pcall_prefetch(body, scalar, args, *, name, grid, out_shape, in_specs, out_specs):
    return pl.pallas_call(
        body, name=name, out_shape=out_shape,
        grid_spec=pltpu.PrefetchScalarGridSpec(num_scalar_prefetch=1, grid=grid, in_specs=in_specs,
                                               out_specs=out_specs),
        compiler_params=pltpu.CompilerParams(dimension_semantics=("arbitrary",) * len(grid),
                                             vmem_limit_bytes=VMEM_LIMIT),
    )(jnp.reshape(scalar, (1,)).astype(jnp.int32), *args)


def _sum_chips(parts, comm=None):
    _, h, c = parts.shape
    th = _row_tile(h, 512)

    def body(p_ref, o_ref):
        acc = p_ref[0].astype(F32)
        for s in range(1, N_CHIPS):
            acc = acc + p_ref[s].astype(F32)
        o_ref[...] = acc

    return _pcall(body, [parts], name="sum_chips", grid=(h // th,), out_shape=jax.ShapeDtypeStruct((h, c), F32),
                  in_specs=[pl.BlockSpec((N_CHIPS, th, c), lambda i: (0, i, 0))],
                  out_specs=pl.BlockSpec((th, c), lambda i: (i, 0)), comm=comm)


def _adamw(w, g, m, v, comm=None):
    r, c = w.shape
    tr = _row_tile(r, 512)
    c1 = 1.0 / (1.0 - ADAM_B1 ** ADAM_STEP)
    c2 = 1.0 / (1.0 - ADAM_B2 ** ADAM_STEP)

    def body(w_ref, g_ref, m_ref, v_ref, d_ref, nm_ref, nv_ref):
        gv = g_ref[...]
        nm = ADAM_B1 * m_ref[...] + (1.0 - ADAM_B1) * gv
        nv = ADAM_B2 * v_ref[...] + (1.0 - ADAM_B2) * (gv * gv)
        nm_ref[...] = nm
        nv_ref[...] = nv
        d_ref[...] = -ADAM_LR * ((nm * c1) / (jnp.sqrt(nv * c2) + ADAM_EPS) + ADAM_WD * w_ref[...])

    spec = pl.BlockSpec((tr, c), lambda i: (i, 0))
    sds = jax.ShapeDtypeStruct((r, c), F32)
    return _pcall(body, [w, g, m, v], name="adamw", grid=(r // tr,), out_shape=[sds, sds, sds],
                  in_specs=[spec] * 4, out_specs=[spec] * 3, comm=comm)


def _wgrad(name, a, b, a_spec, b_spec, m, n, nb, comm):
    def body(a_ref, b_ref, o_ref):
        o_ref[...] = _dot_tn(a_ref[...], b_ref[...]).astype(o_ref.dtype)

    return _pcall(body, [a, b], name=name, grid=(nb,), out_shape=jax.ShapeDtypeStruct((nb, m, n), BF),
                  in_specs=[a_spec, b_spec], out_specs=pl.BlockSpec((None, m, n), lambda j: (j, 0, 0)), comm=comm)


def _wgrad_cols(name, a, b, nb, comm=None):
    t_tok, m = a.shape
    n = b.shape[1] // nb
    return _wgrad(name, a, b, pl.BlockSpec((t_tok, m), lambda j: (0, 0)), pl.BlockSpec((t_tok, n), lambda j: (0, j)),
                  m, n, nb, comm)


def _wgrad_rows(name, a, b, nb, comm=None):
    t_tok, n = b.shape
    m = a.shape[1] // nb
    return _wgrad(name, a, b, pl.BlockSpec((t_tok, m), lambda j: (0, j)), pl.BlockSpec((t_tok, n), lambda j: (0, 0)),
                  m, n, nb, comm)


def _wgrad_a_shared(name, a, b4, comm=None):
    t_tok, m = a.shape
    nb, _, n = b4.shape
    return _wgrad(name, a, b4, pl.BlockSpec((t_tok, m), lambda j: (0, 0)),
                  pl.BlockSpec((None, t_tok, n), lambda j: (j, 0, 0)), m, n, nb, comm)


def _wgrad_b_shared(name, a4, b, comm=None):
    nb, t_tok, m = a4.shape
    n = b.shape[1]
    return _wgrad(name, a4, b, pl.BlockSpec((None, t_tok, m), lambda j: (j, 0, 0)),
                  pl.BlockSpec((t_tok, n), lambda j: (0, 0)), m, n, nb, comm)


def _w4_spec(r, c):
    return pl.BlockSpec((None, r, c), lambda i, j: (j, 0, 0))


def _ffn_fwd(h, ln, wg4, wu4, wd4, comm=None):
    t_tok, d = h.shape
    f = wg4.shape[-2]
    tm = _tile(t_tok, 512)

    def body(h_ref, ln_ref, wg_ref, wu_ref, wd_ref, ho_ref, n_ref, g_ref, u_ref, n_s, acc):
        j = pl.program_id(1)

        @pl.when(j == 0)
        def _():
            xv = h_ref[...]
            nv = (xv * _rstd(xv) * ln_ref[...]).astype(BF)
            n_s[...] = nv
            n_ref[...] = nv
            acc[...] = jnp.zeros_like(acc)

        nv = n_s[...]
        g = _dot_nt(nv, wg_ref[...])
        u = _dot_nt(nv, wu_ref[...])
        g_ref[...] = g.astype(BF)
        u_ref[...] = u.astype(BF)
        a = (g * _sigmoid(g) * u).astype(BF)
        acc[...] += _dot(a, wd_ref[...])

        @pl.when(j == N_CHIPS - 1)
        def _():
            ho_ref[...] = h_ref[...] + 0.5 * acc[...]

    row = pl.BlockSpec((tm, d), lambda i, j: (i, 0))
    gu = pl.BlockSpec((None, tm, f), lambda i, j: (j, i, 0))
    gu_sds = jax.ShapeDtypeStruct((N_CHIPS, t_tok, f), BF)
    return _pcall(
        body, [h, ln, wg4, wu4, wd4], name="ffn_fwd", grid=(t_tok // tm, N_CHIPS),
        out_shape=[jax.ShapeDtypeStruct((t_tok, d), F32), jax.ShapeDtypeStruct((t_tok, d), BF), gu_sds, gu_sds],
        in_specs=[row, pl.BlockSpec((1, d), lambda i, j: (0, 0)), _w4_spec(f, d), _w4_spec(f, d), _w4_spec(f, d)],
        out_specs=[row, row, gu, gu],
        scratch=[pltpu.VMEM((tm, d), BF), pltpu.VMEM((tm, d), F32)], comm=comm)


def _ffn_bwd(dho, h, ln, g4, u4, wg4, wu4, wd4, comm=None):
    t_tok, d = h.shape
    f = wg4.shape[-2]
    tm = _tile(t_tok, 512)

    def body(dho_ref, h_ref, ln_ref, g_ref, u_ref, wg_ref, wu_ref, wd_ref,
             dhi_ref, dln_ref, dg_ref, du_ref, a_ref, dhb_ref, dhb_s, dn_acc):
        i, j = pl.program_id(0), pl.program_id(1)

        @pl.when(j == 0)
        def _():
            dhb = (0.5 * dho_ref[...]).astype(BF)
            dhb_s[...] = dhb
            dhb_ref[...] = dhb
            dn_acc[...] = jnp.zeros_like(dn_acc)

        @pl.when((i == 0) & (j == 0))
        def _():
            dln_ref[...] = jnp.zeros_like(dln_ref)

        g = g_ref[...].astype(F32)
        u = u_ref[...].astype(F32)
        s = _sigmoid(g)
        sg = g * s
        a_ref[...] = (sg * u).astype(BF)
        da = _dot_nt(dhb_s[...], wd_ref[...])
        dg = (da * u * (s * (1.0 + g * (1.0 - s)))).astype(BF)
        du = (da * sg).astype(BF)
        dg_ref[...] = dg
        du_ref[...] = du
        dn_acc[...] += _dot(dg, wg_ref[...]) + _dot(du, wu_ref[...])

        @pl.when(j == N_CHIPS - 1)
        def _():
            xv = h_ref[...]
            dx, dln = _rms_bwd(dn_acc[...], xv, _rstd(xv), ln_ref[...])
            dln_ref[...] += dln
            dhi_ref[...] = dho_ref[...] + dx

    row = pl.BlockSpec((tm, d), lambda i, j: (i, 0))
    vec = pl.BlockSpec((1, d), lambda i, j: (0, 0))
    gu = pl.BlockSpec((None, tm, f), lambda i, j: (j, i, 0))
    gu_sds = jax.ShapeDtypeStruct((N_CHIPS, t_tok, f), BF)
    return _pcall(
        body, [dho, h, ln, g4, u4, wg4, wu4, wd4], name="ffn_bwd", grid=(t_tok // tm, N_CHIPS),
        out_shape=[jax.ShapeDtypeStruct((t_tok, d), F32), jax.ShapeDtypeStruct((1, d), F32),
                   gu_sds, gu_sds, gu_sds, jax.ShapeDtypeStruct((t_tok, d), BF)],
        in_specs=[row, row, vec, gu, gu, _w4_spec(f, d), _w4_spec(f, d), _w4_spec(f, d)],
        out_specs=[row, vec, gu, gu, gu, row],
        scratch=[pltpu.VMEM((tm, d), BF), pltpu.VMEM((tm, d), F32)], comm=comm)


def _rope_tables(pos_col, inv_freq2, comm=None):
    t_tok = pos_col.shape[0]

    def body(p_ref, f_ref, cos_ref, sin_ref):
        ang = p_ref[...] * f_ref[...]
        lane = lax.broadcasted_iota(jnp.int32, ang.shape, 1)
        s = jnp.sin(ang)
        cos_ref[...] = jnp.cos(ang)
        sin_ref[...] = jnp.where((lane & 1) == 0, -s, s)

    sds = jax.ShapeDtypeStruct((t_tok, 128), F32)
    return _pcall(body, [pos_col, inv_freq2], name="rope_tables", out_shape=[sds, sds], comm=comm)


def _swap_pairs(x):
    lane = lax.broadcasted_iota(jnp.int32, x.shape, 1)
    return jnp.where((lane & 1) == 0, pltpu.roll(x, 127, 1), pltpu.roll(x, 1, 1))


def _mix_in(h, ln, w_in, wm4, b_m, cos_t, sin_t, comm=None):
    t_tok, d = h.shape
    cm = wm4.shape[-1]
    tm = _tile(t_tok, 256)

    def body(h_ref, ln_ref, win_ref, wm_ref, bm_ref, cos_ref, sin_ref,
             u_ref, rq_ref, rk_ref, rv_ref, rg_ref, fq_ref, fk_ref, fv_ref, ff_ref, ga_ref, gb_ref):
        xv = h_ref[...]
        ub = (xv * _rstd(xv) * ln_ref[...]).astype(BF)
        u_ref[...] = ub
        cosv, sinv = cos_ref[...], sin_ref[...]

        def sec(k):
            return _dot_nt(ub, win_ref[k * 512:(k + 1) * 512, :])

        def rot(xh):
            return xh * cosv + _swap_pairs(xh) * sinv

        pq, pk = sec(0), sec(1)
        for hh in range(RET_HEADS):
            sl = slice(hh * RET_DIM, (hh + 1) * RET_DIM)
            rq_ref[:, sl] = rot(pq[:, sl]).astype(BF)
            rk_ref[:, sl] = (rot(pk[:, sl]) * RET_SCALE).astype(BF)
        rv_ref[...] = sec(2).astype(BF)
        rg_ref[...] = sec(3).astype(BF)
        fq_ref[...] = (sec(4) * FOX_SCALE).astype(BF)
        fk_ref[...] = sec(5).astype(BF)
        fv_ref[...] = sec(6).astype(BF)
        ff_ref[...] = _dot_nt(ub, win_ref[FF_COL:FF_COL + 128, :])
        for j in range(N_CHIPS):
            gs = _sigmoid(_dot(ub, wm_ref[j]) + bm_ref[:, j * cm:(j + 1) * cm]).astype(BF)
            col = j * cm
            if col < d:
                ga_ref[:, col:col + cm] = gs
            else:
                gb_ref[:, col - d:col - d + cm] = gs

    row = lambda c: pl.BlockSpec((tm, c), lambda i: (i, 0))
    full = lambda *s: pl.BlockSpec(s, lambda i: (0,) * len(s))
    sds = lambda c, dt: jax.ShapeDtypeStruct((t_tok, c), dt)
    return _pcall(
        body, [h, ln, w_in, wm4, b_m, cos_t, sin_t], name="mix_in", grid=(t_tok // tm,),
        out_shape=[sds(d, BF)] + [sds(512, BF)] * 7 + [sds(128, F32), sds(d, BF), sds(d, BF)],
        in_specs=[row(d), full(1, d), full(IN_PAD, d), full(N_CHIPS, d, cm), full(1, 2 * d), row(128), row(128)],
        out_specs=[row(d)] + [row(512)] * 7 + [row(128), row(d), row(d)], comm=comm)


def _split3(x):
    hi = x.astype(BF)
    r1 = x - hi.astype(F32)
    mid = r1.astype(BF)
    lo = (r1 - mid.astype(F32)).astype(BF)
    return hi, mid, lo


def _aug_lane():
    return lax.broadcasted_iota(jnp.int32, (1, 128), 1) & (FOX_DIM - 1)


def _aug_put(base, k0, parts):
    w = _aug_lane()
    for i, part in enumerate(parts):
        base = jnp.where(w == k0 + i, part, base)
    return base


def _forget_fwd(ffl, b_pad):
    t_tok = ffl.shape[0]
    tb = _tile(t_tok, 256)

    def body(ff_ref, b_ref, aq_ref, ak_ref, cum_s):
        r = lax.broadcasted_iota(jnp.int32, (tb, tb), 0)
        c = lax.broadcasted_iota(jnp.int32, (tb, tb), 1)
        tri = jnp.where(c <= r, 1.0, 0.0).astype(BF)
        carry = jnp.zeros((1, 128), F32)
        for i in range(t_tok // tb):
            z = ff_ref[i * tb:(i + 1) * tb, :] + b_ref[...]
            lf = jnp.minimum(z, 0.0) - jnp.log(1.0 + jnp.exp(-jnp.abs(z)))
            hi, mid, lo = _split3(lf)
            cs = _dot(tri, hi) + _dot(tri, mid) + _dot(tri, lo) + carry
            cum_s[i * tb:(i + 1) * tb, :] = cs
            carry = cs[tb - 1:tb, :]
        x = cum_s[...]
        first = lax.broadcasted_iota(jnp.int32, (1, 128), 1) < FOX_DIM
        w = _aug_lane()
        one = jnp.ones((t_tok, 128), BF)
        zero = jnp.zeros((t_tok, 128), BF)
        for pp in range(FOX_HEADS // 2):
            other = jnp.where(first, x[:, 2 * pp + 1:2 * pp + 2], x[:, 2 * pp:2 * pp + 1])
            parts = _split3(other)
            aq = jnp.where((w >= 3) & (w < 6), one, zero)
            ak = jnp.where((w < 3) | ((w >= 6) & (w < 9)), one, zero)
            aq_ref[:, pp * 128:(pp + 1) * 128] = _aug_put(aq, 0, parts)
            ak_ref[:, pp * 128:(pp + 1) * 128] = _aug_put(ak, 3, [-q for q in parts])

    sds = jax.ShapeDtypeStruct((t_tok, FOX_WIDTH), BF)
    return _pcall(body, [ffl, b_pad], name="forget_fwd", out_shape=[sds, sds],
                  scratch=[pltpu.VMEM((t_tok, 128), F32)])


def _fox_aug_lse(aq, lse_e):
    t_tok = aq.shape[0]
    tm = _tile(t_tok, 512)

    def body(aq_ref, lse_ref, o_ref):
        for pp in range(FOX_HEADS // 2):
            sl = slice(pp * 128, (pp + 1) * 128)
            other = pltpu.roll(lse_ref[:, sl], FOX_DIM, 1)
            o_ref[:, sl] = _aug_put(aq_ref[:, sl], 6, _split3(-other))

    spec = pl.BlockSpec((tm, FOX_WIDTH), lambda i: (i, 0))
    return _pcall(body, [aq, lse_e], name="fox_aug_lse", grid=(t_tok // tm,),
                  out_shape=jax.ShapeDtypeStruct((t_tok, FOX_WIDTH), BF), in_specs=[spec, spec], out_specs=spec)


def _forget_bwd(dcum_t, dcum_q, ffl, b_pad):
    t_tok = ffl.shape[0]
    tb = _tile(t_tok, 256)

    def body(dc_ref, dq_ref, ff_ref, b_ref, dff_ref, db_ref, pad_s, d_s):
        pad_s[...] = jnp.zeros_like(pad_s)
        pad_s[0:FOX_HEADS, :] = dc_ref[...]
        dsum = pad_s[...].T
        lane = lax.broadcasted_iota(jnp.int32, (t_tok, 128), 1)
        for hh in range(FOX_HEADS):
            dsum = dsum + jnp.where(lane == hh, dq_ref[:, hh * FOX_DIM:hh * FOX_DIM + 1], 0.0)
        d_s[...] = dsum
        r = lax.broadcasted_iota(jnp.int32, (tb, tb), 0)
        c = lax.broadcasted_iota(jnp.int32, (tb, tb), 1)
        tri = jnp.where(c >= r, 1.0, 0.0).astype(BF)
        carry = jnp.zeros((1, 128), F32)
        db = jnp.zeros((1, 128), F32)
        for i in reversed(range(t_tok // tb)):
            hi, mid, lo = _split3(d_s[i * tb:(i + 1) * tb, :])
            dlf = _dot(tri, hi) + _dot(tri, mid) + _dot(tri, lo) + carry
            carry = dlf[0:1, :]
            z = ff_ref[i * tb:(i + 1) * tb, :] + b_ref[...]
            dff = dlf * _sigmoid(-z)
            dff_ref[i * tb:(i + 1) * tb, :] = dff.astype(BF)
            db = db + jnp.sum(dff, axis=0, keepdims=True)
        db_ref[...] = db

    return _pcall(
        body, [dcum_t, dcum_q, ffl, b_pad], name="forget_bwd",
        out_shape=[jax.ShapeDtypeStruct((t_tok, 128), BF), jax.ShapeDtypeStruct((1, 128), F32)],
        scratch=[pltpu.VMEM((128, t_tok), F32), pltpu.VMEM((t_tok, 128), F32)])


def _first_half():
    return lax.broadcasted_iota(jnp.int32, (1, 128), 1) < FOX_DIM


def _head_rows(x2, a2, hh):
    return jnp.where(_first_half(), x2, a2) if hh == 0 else jnp.where(_first_half(), a2, x2)


def _head_only(x2, hh):
    zero = jnp.zeros_like(x2)
    return jnp.where(_first_half(), x2, zero) if hh == 0 else jnp.where(_first_half(), zero, x2)


def _causal_diag(s):
    rows = lax.broadcasted_iota(jnp.int32, s.shape, 0)
    cols = lax.broadcasted_iota(jnp.int32, s.shape, 1)
    return jnp.where(cols <= rows, s, NEG)


def _diag_or_below(qi, ki, step):
    pl.when(ki < qi)(lambda: step(False))
    pl.when(ki == qi)(lambda: step(True))


def _fox_fwd(fq, fk, fv, aq, ak, comm=None):
    t_tok = fq.shape[0]
    t = _tile(t_tok, 512)
    nq = t_tok // t
    npair = FOX_HEADS // 2

    def body(q_ref, k_ref, v_ref, aq_ref, ak_ref, o_ref, of_ref, lse_ref, m_s, l_s, acc_s):
        qi, ki = pl.program_id(1), pl.program_id(2)

        @pl.when(ki == 0)
        def _():
            m_s[...] = jnp.full_like(m_s, NEG)
            l_s[...] = jnp.zeros_like(l_s)
            acc_s[...] = jnp.zeros_like(acc_s)

        def step(diag):
            q2, k2, v2, aq2, ak2 = q_ref[...], k_ref[...], v_ref[...], aq_ref[...], ak_ref[...]
            for hh in range(2):
                s = _dot_nt(_head_rows(q2, aq2, hh), _head_rows(k2, ak2, hh))
                if diag:
                    s = _causal_diag(s)
                m_prev = m_s[hh]
                m_new = jnp.maximum(m_prev, jnp.max(s, axis=1, keepdims=True))
                alpha = jnp.exp(m_prev - m_new)
                p = jnp.exp(s - jnp.tile(m_new, (1, t // 128)))
                l_s[hh] = alpha * l_s[hh] + jnp.sum(p, axis=1, keepdims=True)
                acc_s[hh] = alpha * acc_s[hh] + _dot(p.astype(BF), v2)
                m_s[hh] = m_new

        _diag_or_below(qi, ki, step)

        @pl.when(ki == nq - 1)
        def _():
            first = _first_half()
            o = jnp.where(first, acc_s[0] / l_s[0], acc_s[1] / l_s[1])
            o_ref[...] = o.astype(BF)
            of_ref[...] = o
            lse_ref[...] = jnp.where(first, m_s[0] + jnp.log(l_s[0]), m_s[1] + jnp.log(l_s[1]))

    qs = pl.BlockSpec((t, 128), lambda p, qi, ki: (qi, p))
    ks = pl.BlockSpec((t, 128), lambda p, qi, ki: (jnp.minimum(ki, qi), p))
    stat = pltpu.VMEM((2, t, 128), F32)
    return _pcall(
        body, [fq, fk, fv, aq, ak], name="fox_fwd", grid=(npair, nq, nq),
        out_shape=[jax.ShapeDtypeStruct((t_tok, FOX_WIDTH), BF), jax.ShapeDtypeStruct((t_tok, FOX_WIDTH), F32),
                   jax.ShapeDtypeStruct((t_tok, FOX_WIDTH), F32)],
        in_specs=[qs, ks, ks, qs, ks], out_specs=[qs, qs, qs], scratch=[stat, stat, stat], comm=comm)


def _fox_ds(q2, k2, v2, do2, aq2, ak2, ad2, hh, diag):
    s = _dot_nt(_head_rows(q2, aq2, hh), _head_rows(k2, ak2, hh))
    if diag:
        s = _causal_diag(s)
    p = jnp.exp(s)
    av = jnp.where(_aug_lane() < 3, 1.0, 0.0).astype(BF)
    dp = _dot_nt(_head_rows(do2, ad2, hh), _head_rows(v2, jnp.broadcast_to(av, v2.shape), hh))
    return p, p * dp


def _fox_bwd_kv(fq, fk, fv, do, aqb, ak, ad, comm=None):
    t_tok = fq.shape[0]
    t = _tile(t_tok, 512)
    nq = t_tok // t
    npair = FOX_HEADS // 2

    def body(q_ref, k_ref, v_ref, do_ref, aq_ref, ak_ref, ad_ref, dk_ref, dv_ref, dck_ref, dk_s, dv_s):
        ki, qi = pl.program_id(1), pl.program_id(2)

        @pl.when(qi == 0)
        def _():
            dk_s[...] = jnp.zeros_like(dk_s)
            dv_s[...] = jnp.zeros_like(dv_s)
            dck_ref[...] = jnp.zeros_like(dck_ref)

        def step(diag):
            q2, k2, v2, do2 = q_ref[...], k_ref[...], v_ref[...], do_ref[...]
            for hh in range(2):
                p, ds = _fox_ds(q2, k2, v2, do2, aq_ref[...], ak_ref[...], ad_ref[...], hh, diag)
                dv_s[...] += _dot_tn(p.astype(BF), _head_only(do2, hh))
                dk_s[...] += _dot_tn(ds.astype(BF), _head_only(q2, hh))
                dck_ref[hh] = dck_ref[hh] - jnp.sum(ds, axis=0, keepdims=True)

        _diag_or_below(qi, ki, step)

        @pl.when(qi == nq - 1)
        def _():
            dk_ref[...] = dk_s[...].astype(BF)
            dv_ref[...] = dv_s[...].astype(BF)

    qs = pl.BlockSpec((t, 128), lambda p, ki, qi: (jnp.maximum(qi, ki), p))
    ks = pl.BlockSpec((t, 128), lambda p, ki, qi: (ki, p))
    cks = pl.BlockSpec((2, 1, t), lambda p, ki, qi: (p, 0, ki))
    sds = jax.ShapeDtypeStruct((t_tok, FOX_WIDTH), BF)
    return _pcall(
        body, [fq, fk, fv, do, aqb, ak, ad], name="fox_bwd_kv", grid=(npair, nq, nq),
        out_shape=[sds, sds, jax.ShapeDtypeStruct((FOX_HEADS, 1, t_tok), F32)],
        in_specs=[qs, ks, ks, qs, qs, ks, qs], out_specs=[ks, ks, cks],
        scratch=[pltpu.VMEM((t, 128), F32), pltpu.VMEM((t, 128), F32)], comm=comm)


def _fox_bwd_q(fq, fk, fv, do, aqb, ak, ad, comm=None):
    t_tok = fq.shape[0]
    t = _tile(t_tok, 512)
    nq = t_tok // t
    npair = FOX_HEADS // 2

    def body(q_ref, k_ref, v_ref, do_ref, aq_ref, ak_ref, ad_ref, dq_ref, dcq_ref, dq_s, rs_s):
        qi, ki = pl.program_id(1), pl.program_id(2)

        @pl.when(ki == 0)
        def _():
            dq_s[...] = jnp.zeros_like(dq_s)
            rs_s[...] = jnp.zeros_like(rs_s)

        def step(diag):
            q2, k2, v2, do2 = q_ref[...], k_ref[...], v_ref[...], do_ref[...]
            dq = []
            for hh in range(2):
                _, ds = _fox_ds(q2, k2, v2, do2, aq_ref[...], ak_ref[...], ad_ref[...], hh, diag)
                dq.append(_dot(ds.astype(BF), k2))
                rs_s[hh] = rs_s[hh] + jnp.sum(ds, axis=1, keepdims=True)
            dq_s[...] += jnp.where(_first_half(), dq[0], dq[1])

        _diag_or_below(qi, ki, step)

        @pl.when(ki == nq - 1)
        def _():
            dq_ref[...] = (dq_s[...] * FOX_SCALE).astype(BF)
            dcq_ref[...] = jnp.where(_first_half(), rs_s[0], rs_s[1])

    qs = pl.BlockSpec((t, 128), lambda p, qi, ki: (qi, p))
    ks = pl.BlockSpec((t, 128), lambda p, qi, ki: (jnp.minimum(ki, qi), p))
    return _pcall(
        body, [fq, fk, fv, do, aqb, ak, ad], name="fox_bwd_q", grid=(npair, nq, nq),
        out_shape=[jax.ShapeDtypeStruct((t_tok, FOX_WIDTH), BF), jax.ShapeDtypeStruct((t_tok, FOX_WIDTH), F32)],
        in_specs=[qs, ks, ks, qs, qs, ks, qs], out_specs=[qs, qs],
        scratch=[pltpu.VMEM((t, 128), F32), pltpu.VMEM((2, t, 128), F32)], comm=comm)


def _ret_consts():
    c = RET_CHUNK
    log_gamma = jnp.log1p(-jnp.exp2(-5.0 - jnp.arange(RET_HEADS, dtype=F32)))
    idx = jnp.arange(c, dtype=F32)
    diff = idx[:, None] - idx[None, :]
    dmask = jnp.where(diff >= 0, jnp.exp(log_gamma[:, None, None] * jnp.maximum(diff, 0.0)), 0.0)
    qdec = jnp.exp(log_gamma[:, None] * (idx + 1.0))
    kdec = jnp.exp(log_gamma[:, None] * (c - 1 - idx))
    cdec = jnp.exp(log_gamma * c)
    bc = lambda v: jnp.broadcast_to(v[:, :, None], (RET_HEADS, c, RET_DIM))
    return dmask, bc(qdec), bc(kdec), jnp.broadcast_to(cdec[:, None, None], (RET_HEADS, c, RET_DIM))


def _group_norm(y):
    mu = jnp.mean(y, axis=-1, keepdims=True)
    yc = y - mu
    r = lax.rsqrt(jnp.mean(yc * yc, axis=-1, keepdims=True) + EPS)
    return yc * r, r


def _ret_fwd(rq, rk, rv, rg, consts, comm=None):
    t_tok = rq.shape[0]
    nb = 4 if t_tok % (4 * RET_CHUNK) == 0 else 1
    tr = nb * RET_CHUNK
    n_steps = t_tok // tr
    c = RET_CHUNK

    def body(q_ref, k_ref, v_ref, g_ref, dm_ref, qd_ref, kd_ref, cd_ref, y_ref, yo_ref, st_ref, s_s):
        @pl.when(pl.program_id(1) == 0)
        def _():
            s_s[...] = jnp.zeros_like(s_s)

        dm, qd, kd, cd = dm_ref[...], qd_ref[...], kd_ref[...], cd_ref[...]
        for b in range(nb):
            rows = slice(b * c, (b + 1) * c)
            q, k, v = q_ref[rows, :], k_ref[rows, :], v_ref[rows, :]
            state = s_s[...]
            st_ref[b] = state
            sc = (_dot_nt(q, k) * dm).astype(BF)
            y = _dot(sc, v) + _dot((q.astype(F32) * qd).astype(BF), state.astype(BF))
            s_s[...] = cd * state + _dot_tn((k.astype(F32) * kd).astype(BF), v)
            y_ref[rows, :] = y
            yn, _ = _group_norm(y)
            gate = g_ref[rows, :].astype(F32)
            yo_ref[rows, :] = (yn * (gate * _sigmoid(gate))).astype(BF)

    blk = pl.BlockSpec((tr, RET_DIM), lambda h, i: (i, h))
    cst = pl.BlockSpec((None, c, RET_DIM), lambda h, i: (h, 0, 0))
    return _pcall(
        body, [rq, rk, rv, rg, *consts], name="ret_fwd", grid=(RET_HEADS, n_steps),
        out_shape=[jax.ShapeDtypeStruct((t_tok, RET_WIDTH), F32), jax.ShapeDtypeStruct((t_tok, RET_WIDTH), BF),
                   jax.ShapeDtypeStruct((RET_HEADS, t_tok // c, RET_DIM, RET_DIM), F32)],
        in_specs=[blk] * 4 + [cst] * 4,
        out_specs=[blk, blk, pl.BlockSpec((None, nb, RET_DIM, RET_DIM), lambda h, i: (h, i, 0, 0))],
        scratch=[pltpu.VMEM((RET_DIM, RET_DIM), F32)], comm=comm)


def _ret_bwd(rq, rk, rv, rg, y_raw, dyo, states, consts, cos_t, sin_t, comm=None):
    t_tok = rq.shape[0]
    nb = 4 if t_tok % (4 * RET_CHUNK) == 0 else 1
    tr = nb * RET_CHUNK
    n_steps = t_tok // tr
    c = RET_CHUNK

    def body(q_ref, k_ref, v_ref, g_ref, y_ref, dyo_ref, st_ref, dm_ref, qd_ref, kd_ref, cd_ref,
             cos_ref, sin_ref, dq_ref, dk_ref, dv_ref, dg_ref, ds_s):
        @pl.when(pl.program_id(1) == 0)
        def _():
            ds_s[...] = jnp.zeros_like(ds_s)

        dm, qd, kd, cd = dm_ref[...], qd_ref[...], kd_ref[...], cd_ref[...]
        for b in reversed(range(nb)):
            rows = slice(b * c, (b + 1) * c)
            q, k, v = q_ref[rows, :], k_ref[rows, :], v_ref[rows, :]
            cosv, sinv = cos_ref[rows, :], sin_ref[rows, :]
            yn, r = _group_norm(y_ref[rows, :])
            gate = g_ref[rows, :].astype(F32)
            sg = _sigmoid(gate)
            dyo = dyo_ref[rows, :]
            dg_ref[rows, :] = (dyo * yn * (sg * (1.0 + gate * (1.0 - sg)))).astype(BF)
            dyn = dyo * (gate * sg)
            dy = r * (dyn - jnp.mean(dyn, axis=-1, keepdims=True)
                      - yn * jnp.mean(dyn * yn, axis=-1, keepdims=True))
            dyb = dy.astype(BF)
            state_b = st_ref[b].astype(BF)
            dstate = ds_s[...]
            dstate_b = dstate.astype(BF)
            qdb = (q.astype(F32) * qd).astype(BF)
            kdb = (k.astype(F32) * kd).astype(BF)
            sc = (_dot_nt(q, k) * dm).astype(BF)
            dv = _dot_tn(sc, dyb) + _dot(kdb, dstate_b)
            dp = (_dot_nt(dyb, v) * dm).astype(BF)
            dq = _dot(dp, k) + _dot_nt(dyb, state_b) * qd
            dk = (_dot_tn(dp, q) + _dot_nt(v, dstate_b) * kd) * RET_SCALE
            ds_s[...] = cd * dstate + _dot_tn(qdb, dyb)
            dv_ref[rows, :] = dv.astype(BF)
            dq_ref[rows, :] = (dq * cosv - _swap_pairs(dq) * sinv).astype(BF)
            dk_ref[rows, :] = (dk * cosv - _swap_pairs(dk) * sinv).astype(BF)

    rev = lambda i: n_steps - 1 - i
    blk = pl.BlockSpec((tr, RET_DIM), lambda h, i: (rev(i), h))
    tab = pl.BlockSpec((tr, RET_DIM), lambda h, i: (rev(i), 0))
    cst = pl.BlockSpec((None, c, RET_DIM), lambda h, i: (h, 0, 0))
    sds = jax.ShapeDtypeStruct((t_tok, RET_WIDTH), BF)
    return _pcall(
        body, [rq, rk, rv, rg, y_raw, dyo, states, *consts, cos_t, sin_t], name="ret_bwd",
        grid=(RET_HEADS, n_steps), out_shape=[sds] * 4,
        in_specs=[blk] * 6 + [pl.BlockSpec((None, nb, RET_DIM, RET_DIM), lambda h, i: (h, rev(i), 0, 0))]
        + [cst] * 4 + [tab, tab],
        out_specs=[blk] * 4, scratch=[pltpu.VMEM((RET_DIM, RET_DIM), F32)], comm=comm)


def _mix_out(h, y_ret, y_fox, ga, gb, wr4, wf4, wo4, comm=None):
    t_tok, d = h.shape
    cz = wr4.shape[-1]
    ro = wo4.shape[-2]
    tm = _tile(t_tok, 512)

    def body(h_ref, yr_ref, yf_ref, ga_ref, gb_ref, wr_ref, wf_ref, wo_ref, ho_ref, za_ref, zb_ref, mix_ref):
        yr, yf = yr_ref[...], yf_ref[...]
        for j in range(N_CHIPS):
            sl = slice(j * cz, (j + 1) * cz)
            za = _dot(yr, wr_ref[j])
            zb = _dot(yf, wf_ref[j])
            za_ref[:, sl] = za.astype(BF)
            zb_ref[:, sl] = zb.astype(BF)
            mix_ref[:, sl] = (ga_ref[:, sl].astype(F32) * za + gb_ref[:, sl].astype(F32) * zb).astype(BF)
        acc = h_ref[...]
        for j in range(N_CHIPS):
            acc = acc + _dot(mix_ref[:, j * ro:(j + 1) * ro], wo_ref[j])
        ho_ref[...] = acc

    row = lambda c: pl.BlockSpec((tm, c), lambda i: (i, 0))
    full = lambda *s: pl.BlockSpec(s, lambda i: (0,) * len(s))
    sds = lambda dt: jax.ShapeDtypeStruct((t_tok, d), dt)
    return _pcall(
        body, [h, y_ret, y_fox, ga, gb, wr4, wf4, wo4], name="mix_out", grid=(t_tok // tm,),
        out_shape=[sds(F32), sds(BF), sds(BF), sds(BF)],
        in_specs=[row(d), row(RET_WIDTH), row(FOX_WIDTH), row(d), row(d),
                  full(N_CHIPS, RET_WIDTH, cz), full(N_CHIPS, FOX_WIDTH, cz), full(N_CHIPS, ro, d)],
        out_specs=[row(d)] * 4, comm=comm)


def _mix_out_bwd(dh, za, zb, ga, gb, y_fox, wr4, wf4, wo4, comm=None):
    t_tok, d = dh.shape
    cz = wr4.shape[-1]
    ro = wo4.shape[-2]
    tm = _tile(t_tok, 256)

    def body(dh_ref, za_ref, zb_ref, ga_ref, gb_ref, yf_ref, wr_ref, wf_ref, wo_ref,
             dhb_ref, dgp_ref, dza_ref, dzb_ref, dyr_ref, dyf_ref, dl_ref, db_ref):
        @pl.when(pl.program_id(0) == 0)
        def _():
            db_ref[...] = jnp.zeros_like(db_ref)

        dhb = dh_ref[...].astype(BF)
        dhb_ref[...] = dhb
        dyr = jnp.zeros((tm, RET_WIDTH), F32)
        dyf = jnp.zeros((tm, FOX_WIDTH), F32)
        for j in range(N_CHIPS):
            sl = slice(j * ro, (j + 1) * ro)
            dmix = _dot_nt(dhb, wo_ref[j])
            ga, gb = ga_ref[:, sl].astype(F32), gb_ref[:, sl].astype(F32)
            dza = (dmix * ga).astype(BF)
            dzb = (dmix * gb).astype(BF)
            dza_ref[:, sl] = dza
            dzb_ref[:, sl] = dzb
            dga = dmix * za_ref[:, sl].astype(F32) * ga * (1.0 - ga)
            dgb = dmix * zb_ref[:, sl].astype(F32) * gb * (1.0 - gb)
            dgp_ref[:, sl] = dga.astype(BF)
            dgp_ref[:, d + j * ro:d + (j + 1) * ro] = dgb.astype(BF)
            db_ref[:, sl] += jnp.sum(dga, axis=0, keepdims=True)
            db_ref[:, d + j * ro:d + (j + 1) * ro] += jnp.sum(dgb, axis=0, keepdims=True)
        for j in range(N_CHIPS):
            sl = slice(j * cz, (j + 1) * cz)
            dyr = dyr + _dot_nt(dza_ref[:, sl], wr_ref[j])
            dyf = dyf + _dot_nt(dzb_ref[:, sl], wf_ref[j])
        dyr_ref[...] = dyr
        dyfb = dyf.astype(BF)
        dyf_ref[...] = dyfb
        prod = dyfb.astype(F32) * yf_ref[...]
        first = _first_half()
        for pp in range(FOX_HEADS // 2):
            blk = prod[:, pp * 128:(pp + 1) * 128]
            s0 = jnp.sum(jnp.where(first, blk, 0.0), axis=1, keepdims=True)
            s1 = jnp.sum(jnp.where(first, 0.0, blk), axis=1, keepdims=True)
            parts = _split3(-jnp.where(first, s1, s0))
            dl_ref[:, pp * 128:(pp + 1) * 128] = _aug_put(jnp.zeros((tm, 128), BF), 0, parts)

    row = lambda c: pl.BlockSpec((tm, c), lambda i: (i, 0))
    full = lambda *s: pl.BlockSpec(s, lambda i: (0,) * len(s))
    sds = lambda c, dt: jax.ShapeDtypeStruct((t_tok, c), dt)
    return _pcall(
        body, [dh, za, zb, ga, gb, y_fox, wr4, wf4, wo4], name="mix_out_bwd", grid=(t_tok // tm,),
        out_shape=[sds(d, BF), sds(2 * d, BF), sds(d, BF), sds(d, BF), sds(RET_WIDTH, F32),
                   sds(FOX_WIDTH, BF), sds(FOX_WIDTH, BF), jax.ShapeDtypeStruct((1, 2 * d), F32)],
        in_specs=[row(d)] * 5 + [row(FOX_WIDTH), full(N_CHIPS, RET_WIDTH, cz), full(N_CHIPS, FOX_WIDTH, cz),
                                 full(N_CHIPS, ro, d)],
        out_specs=[row(d), row(2 * d), row(d), row(d), row(RET_WIDTH), row(FOX_WIDTH), row(FOX_WIDTH),
                   full(1, 2 * d)],
        comm=comm)


def _mix_in_bwd(dh, h, ln, parts, dff, dgpre, w_in, wm4, comm=None):
    t_tok, d = h.shape
    cm = wm4.shape[-1]
    tm = _tile(t_tok, 256)

    def body(dh_ref, h_ref, ln_ref, p0, p1, p2, p3, p4, p5, p6, dff_ref, dgp_ref, win_ref, wm_ref,
             dhi_ref, dln_ref, dproj_ref):
        @pl.when(pl.program_id(0) == 0)
        def _():
            dln_ref[...] = jnp.zeros_like(dln_ref)

        for k, pr in enumerate((p0, p1, p2, p3, p4, p5, p6)):
            dproj_ref[:, k * 512:(k + 1) * 512] = pr[...]
        dproj_ref[:, FF_COL:FF_COL + 128] = dff_ref[...]
        dproj_ref[:, FF_COL + 128:] = jnp.zeros((tm, IN_PAD - FF_COL - 128), BF)
        du = _dot(dproj_ref[...], win_ref[...])
        for j in range(N_CHIPS):
            du = du + _dot_nt(dgp_ref[:, j * cm:(j + 1) * cm], wm_ref[j])
        xv = h_ref[...]
        dx, dln = _rms_bwd(du, xv, _rstd(xv), ln_ref[...])
        dln_ref[...] += dln
        dhi_ref[...] = dh_ref[...] + dx

    row = lambda c: pl.BlockSpec((tm, c), lambda i: (i, 0))
    full = lambda *s: pl.BlockSpec(s, lambda i: (0,) * len(s))
    return _pcall(
        body, [dh, h, ln, *parts, dff, dgpre, w_in, wm4], name="mix_in_bwd", grid=(t_tok // tm,),
        out_shape=[jax.ShapeDtypeStruct((t_tok, d), F32), jax.ShapeDtypeStruct((1, d), F32),
                   jax.ShapeDtypeStruct((t_tok, IN_PAD), BF)],
        in_specs=[row(d), row(d), full(1, d)] + [row(512)] * 7 + [row(128), row(2 * d), full(IN_PAD, d),
                                                                   full(N_CHIPS, d, cm)],
        out_specs=[row(d), full(1, d), row(IN_PAD)], comm=comm)


def _tail(h, p, target, ln_ple, ln_fin, wpg4, wpl4, comm=None):
    t_tok, d = h.shape
    pd = p.shape[1]
    rg = wpg4.shape[-2]
    cp = wpl4.shape[-1]
    tm = _tile(t_tok, 256)

    def body(h_ref, p_ref, t_ref, lp_ref, lf_ref, wg_ref, wp_ref,
             dh_ref, n_ref, dgp_ref, dpe_ref, pb_ref, loss_ref, dlf_ref, dlp_ref, pe_s, dn_s):
        @pl.when(pl.program_id(0) == 0)
        def _():
            loss_ref[...] = jnp.zeros_like(loss_ref)
            dlf_ref[...] = jnp.zeros_like(dlf_ref)
            dlp_ref[...] = jnp.zeros_like(dlp_ref)

        xv = h_ref[...]
        r3 = _rstd(xv)
        nb = (xv * r3 * lp_ref[...]).astype(BF)
        n_ref[...] = nb
        pb = p_ref[...].astype(BF)
        pb_ref[...] = pb
        pgpre = jnp.zeros((tm, d), F32)
        for j in range(N_CHIPS):
            pgpre = pgpre + _dot(nb[:, j * rg:(j + 1) * rg], wg_ref[j])
            pe_s[:, j * cp:(j + 1) * cp] = _dot(pb, wp_ref[j])
        pg = _sigmoid(pgpre)
        pe = pe_s[...]
        h4 = xv + pg * pe
        r4 = _rstd(h4)
        err = h4 * r4 * lf_ref[...] - t_ref[...]
        loss_ref[...] += 0.5 * jnp.sum(jnp.sum(err * err, axis=1, keepdims=True), axis=0, keepdims=True) / d
        dh4, dlf = _rms_bwd(err * (1.0 / d), h4, r4, lf_ref[...])
        dlf_ref[...] += dlf
        dpe_ref[...] = (dh4 * pg).astype(BF)
        dgp = (dh4 * pe * pg * (1.0 - pg)).astype(BF)
        dgp_ref[...] = dgp
        for j in range(N_CHIPS):
            dn_s[:, j * rg:(j + 1) * rg] = _dot_nt(dgp, wg_ref[j])
        dx, dlp = _rms_bwd(dn_s[...], xv, r3, lp_ref[...])
        dlp_ref[...] += dlp
        dh_ref[...] = dh4 + dx

    row = lambda c: pl.BlockSpec((tm, c), lambda i: (i, 0))
    full = lambda *s: pl.BlockSpec(s, lambda i: (0,) * len(s))
    sds = lambda c, dt: jax.ShapeDtypeStruct((t_tok, c), dt)
    vec = jax.ShapeDtypeStruct((1, d), F32)
    return _pcall(
        body, [h, p, target, ln_ple, ln_fin, wpg4, wpl4], name="tail", grid=(t_tok // tm,),
        out_shape=[sds(d, F32), sds(d, BF), sds(d, BF), sds(d, BF), sds(pd, BF),
                   jax.ShapeDtypeStruct((1, 128), F32), vec, vec],
        in_specs=[row(d), row(pd), row(d), full(1, d), full(1, d), full(N_CHIPS, rg, d), full(N_CHIPS, pd, cp)],
        out_specs=[row(d), row(d), row(d), row(d), row(pd), full(1, 128), full(1, d), full(1, d)],
        scratch=[pltpu.VMEM((tm, d), F32), pltpu.VMEM((tm, d), F32)], comm=comm)


BIG = ["w_ffn1_gate", "w_ffn1_up", "w_ffn1_down", "w_in", "w_merge", "w_ret_out", "w_fox_out", "w_out",
       "w_ffn2_gate", "w_ffn2_up", "w_ffn2_down", "w_ple", "w_ple_gate"]
SMALL = ["ln_ffn1", "ln_mix", "b_forget", "b_merge", "ln_ffn2", "ln_ple", "ln_final"]
WEIGHTS = ["ln_ffn1", "w_ffn1_gate", "w_ffn1_up", "w_ffn1_down", "ln_mix", "w_in", "b_forget", "w_merge", "b_merge",
           "w_ret_out", "w_fox_out", "w_out", "ln_ffn2", "w_ffn2_gate", "w_ffn2_up", "w_ffn2_down", "ln_ple",
           "w_ple", "w_ple_gate", "ln_final"]


TRANSPOSED = {"w_ffn1_gate", "w_ffn1_up", "w_ffn2_gate", "w_ffn2_up", "w_in"}
IN_ROWS_PAD = -(-(IN_COLS // N_CHIPS) // 32) * 32


def _pack_small(vals, loss_row):
    rows = [loss_row]
    for name in SMALL:
        v = vals[name].reshape(-1)
        n = -(-v.shape[0] // 128) * 128
        rows.append(jnp.pad(v, (0, n - v.shape[0])).reshape(n // 128, 128))
    packed = jnp.concatenate(rows, axis=0)
    pad = -packed.shape[0] % 8
    return jnp.pad(packed, ((0, pad), (0, 0)))


def _unpack_small(packed, sizes):
    out, r = {}, 1
    for name in SMALL:
        n = sizes[name]
        nr = -(-n // 128)
        out[name] = packed[r:r + nr].reshape(1, nr * 128)[:, :n]
        r += nr
    return out


class _Stage:
    def __init__(self, comm, finish):
        self.comm, self.finish, self.result = comm, finish, None


def _hosted(fn, *a, stages=()):
    if not stages:
        return fn(*a)
    outs, couts = fn(*a, comm=_merge([st.comm for st in stages]))
    for st, o in zip(stages, _split_outs([st.comm for st in stages], couts)):
        st.result = st.finish(o)
    return outs


class _Reducer:
    def __init__(self):
        self.done = {}

    def swap(self, grads):
        names = list(grads)
        return _Stage(_c_half_swap([grads[n] for n in names]),
                      lambda outs: {n: _add_halves(grads[n], o) for n, o in zip(names, outs)})

    def exchange(self, parts):
        names = list(parts)
        return _Stage(_c_chip_exchange([parts[n] for n in names]),
                      lambda outs: {n: _sum_chips(o) for n, o in zip(names, outs)})

    def join(self, halves):
        names = list(halves)
        return _Stage(_c_join([halves[n] for n in names]), lambda outs: self.done.update(zip(names, outs)))


def kernel(x, p, positions, ln_ffn1, w_ffn1_gate, w_ffn1_up, w_ffn1_down, ln_mix, w_in, b_forget, w_merge, b_merge, w_ret_out, w_fox_out, w_out, ln_ffn2, w_ffn2_gate, w_ffn2_up, w_ffn2_down, ln_ple, w_ple, w_ple_gate, ln_final, loss_target, m_ln_ffn1, m_w_ffn1_gate, m_w_ffn1_up, m_w_ffn1_down, m_ln_mix, m_w_in, m_b_forget, m_w_merge, m_b_merge, m_w_ret_out, m_w_fox_out, m_w_out, m_ln_ffn2, m_w_ffn2_gate, m_w_ffn2_up, m_w_ffn2_down, m_ln_ple, m_w_ple, m_w_ple_gate, m_ln_final, v_ln_ffn1, v_w_ffn1_gate, v_w_ffn1_up, v_w_ffn1_down, v_ln_mix, v_w_in, v_b_forget, v_w_merge, v_b_merge, v_w_ret_out, v_w_fox_out, v_w_out, v_ln_ffn2, v_w_ffn2_gate, v_w_ffn2_up, v_w_ffn2_down, v_ln_ple, v_w_ple, v_w_ple_gate, v_ln_final):
    args = dict(locals())
    w = {n: args[n] for n in WEIGHTS}
    m = {n: args["m_" + n] for n in WEIGHTS}
    v = {n: args["v_" + n] for n in WEIGHTS}
    d = x.shape[-1]
    t_tok = x.shape[1]
    xs, ps, target = x[0], p[0, 0], loss_target[0]
    small = {n: w[n].reshape(1, -1) for n in SMALL}

    def to2d(n, a):
        if n in TRANSPOSED:
            return a[0].T
        return a.reshape(a.shape[-2], a.shape[-1]) if a.ndim == 3 else a.reshape(1, -1)

    def from2d(n, a):
        return a.T[None] if n in TRANSPOSED else a.reshape(w[n].shape)

    def padded(n, a):
        return jnp.pad(a, ((0, IN_ROWS_PAD - a.shape[0]), (0, 0))) if n == "w_in" else a

    shard = {}
    for n in BIG:
        s2 = padded(n, to2d(n, w[n]).astype(BF))
        shard[n] = s2.reshape(2, s2.shape[0] // 2, s2.shape[1])
    full = {}

    def gather(names):
        def finish(outs):
            full.update({n: o.reshape(N_CHIPS, 2 * o.shape[2], o.shape[3]) for n, o in zip(names, outs)})

        return _Stage(_c_all_gather([shard[n] for n in names]), finish)

    half = RET_DIM // 2
    inv_freq = 1.0 / (ROPE_BASE ** (jnp.arange(half, dtype=F32) / half))
    cos_t, sin_t = _hosted(_rope_tables, positions[0].astype(F32).reshape(t_tok, 1),
                           jnp.repeat(inv_freq, 2).reshape(1, RET_DIM),
                           stages=[gather(["w_ffn1_gate", "w_ffn1_up", "w_ffn1_down"])])
    consts = _ret_consts()
    b_pad = jnp.pad(small["b_forget"], ((0, 0), (0, 128 - FOX_HEADS)))

    h1, n1, g1, u1 = _hosted(_ffn_fwd, xs, small["ln_ffn1"], full["w_ffn1_gate"], full["w_ffn1_up"],
                             full["w_ffn1_down"], stages=[gather(["w_in", "w_merge"])])
    w_in_full = jnp.pad(full["w_in"][:, :IN_COLS // N_CHIPS].reshape(IN_COLS, d), ((0, IN_PAD - IN_COLS), (0, 0)))
    u, rq, rk, rv, rg, fq, fk, fv, ffl, ga, gb = _hosted(
        _mix_in, h1, small["ln_mix"], w_in_full, full["w_merge"], small["b_merge"], cos_t, sin_t,
        stages=[gather(["w_ret_out", "w_fox_out", "w_out", "w_ple_gate", "w_ple"])])
    aq, ak = _forget_fwd(ffl, b_pad)
    y_raw, y_ret, states = _ret_fwd(rq, rk, rv, rg, consts)
    y_fox, y_fox32, lse_e = _hosted(_fox_fwd, fq, fk, fv, aq, ak,
                                    stages=[gather(["w_ffn2_gate", "w_ffn2_up", "w_ffn2_down"])])
    aqb = _fox_aug_lse(aq, lse_e)
    h2, za, zb, mix = _mix_out(h1, y_ret, y_fox, ga, gb, full["w_ret_out"], full["w_fox_out"], full["w_out"])
    h3, n2, g2, u2 = _ffn_fwd(h2, small["ln_ffn2"], full["w_ffn2_gate"], full["w_ffn2_up"], full["w_ffn2_down"])

    red = _Reducer()
    dh3, n3, dpgpre, dpe, pb, loss, dln_final, dln_ple = _tail(
        h3, ps, target, small["ln_ple"], small["ln_final"], full["w_ple_gate"], full["w_ple"])
    g_ple = dict(w_ple_gate=_wgrad_rows("wgrad_ple_gate", n3, dpgpre, N_CHIPS),
                 w_ple=_wgrad_cols("wgrad_ple", pb, dpe, N_CHIPS))

    sw_ple = red.swap(g_ple)
    dh2, dln_ffn2, dg2, du2, a2, dhb3 = _hosted(
        _ffn_bwd, dh3, h2, small["ln_ffn2"], g2, u2, full["w_ffn2_gate"], full["w_ffn2_up"], full["w_ffn2_down"],
        stages=[sw_ple])
    ex_ple = red.exchange(sw_ple.result)
    g_f2 = dict(w_ffn2_gate=_hosted(_wgrad_b_shared, "wgrad_ffn2_gate", dg2, n2, stages=[ex_ple]))
    g_f2["w_ffn2_up"] = _wgrad_b_shared("wgrad_ffn2_up", du2, n2)
    g_f2["w_ffn2_down"] = _wgrad_b_shared("wgrad_ffn2_down", a2, dhb3)

    sw_f2 = red.swap(g_f2)
    dhb2, dgpre, dza, dzb, dy_ret, dy_fox, ad, db_merge = _hosted(
        _mix_out_bwd, dh2, za, zb, ga, gb, y_fox32, full["w_ret_out"], full["w_fox_out"], full["w_out"],
        stages=[sw_f2, red.join(ex_ple.result)])
    g_br = dict(w_out=_wgrad_rows("wgrad_out", mix, dhb2, N_CHIPS),
                w_ret_out=_wgrad_cols("wgrad_ret_out", y_ret, dza, N_CHIPS),
                w_fox_out=_wgrad_cols("wgrad_fox_out", y_fox, dzb, N_CHIPS))

    sw_br = red.swap(g_br)
    drq, drk, drv, drg = _hosted(_ret_bwd, rq, rk, rv, rg, y_raw, dy_ret, states, consts, cos_t, sin_t,
                                 stages=[sw_br])
    ex_f2, ex_br = red.exchange(sw_f2.result), red.exchange(sw_br.result)
    dfk, dfv, dcum_t3 = _hosted(_fox_bwd_kv, fq, fk, fv, dy_fox, aqb, ak, ad, stages=[ex_f2, ex_br])
    dfq, dcum_q = _hosted(_fox_bwd_q, fq, fk, fv, dy_fox, aqb, ak, ad,
                          stages=[red.join(ex_f2.result), red.join(ex_br.result)])
    dff, db_forget = _forget_bwd(dcum_t3.reshape(FOX_HEADS, t_tok), dcum_q, ffl, b_pad)
    dh1, dln_mix, dproj = _mix_in_bwd(dh2, h1, small["ln_mix"], (drq, drk, drv, drg, dfq, dfk, dfv), dff, dgpre,
                                      w_in_full, full["w_merge"])

    g_in = _wgrad_rows("wgrad_in", dproj, u, IN_PAD // 512)
    g_in = g_in.reshape(IN_PAD, d)[:IN_COLS].reshape(N_CHIPS, IN_COLS // N_CHIPS, d)
    g_in = jnp.pad(g_in, ((0, 0), (0, IN_ROWS_PAD - IN_COLS // N_CHIPS), (0, 0)))
    sw_in = red.swap(dict(w_in=g_in))
    g_mrg = _hosted(_wgrad_cols, "wgrad_merge", u, dgpre, N_CHIPS, stages=[sw_in])

    sw_mrg, ex_in = red.swap(dict(w_merge=g_mrg)), red.exchange(sw_in.result)
    dx, dln_ffn1, dg1, du1, a1, dhb1 = _hosted(
        _ffn_bwd, dh1, xs, small["ln_ffn1"], g1, u1, full["w_ffn1_gate"], full["w_ffn1_up"], full["w_ffn1_down"],
        stages=[sw_mrg, ex_in])

    ex_mrg = red.exchange(sw_mrg.result)
    g_f1g = _hosted(_wgrad_b_shared, "wgrad_ffn1_gate", dg1, n1, stages=[ex_mrg, red.join(ex_in.result)])
    sw_f1g = red.swap(dict(w_ffn1_gate=g_f1g))
    g_f1u = _hosted(_wgrad_b_shared, "wgrad_ffn1_up", du1, n1, stages=[sw_f1g])
    ex_f1g, sw_f1u = red.exchange(sw_f1g.result), red.swap(dict(w_ffn1_up=g_f1u))
    g_f1d = _hosted(_wgrad_b_shared, "wgrad_ffn1_down", a1, dhb1,
                    stages=[ex_f1g, sw_f1u, red.join(ex_mrg.result)])

    small_grads = dict(ln_ffn1=dln_ffn1, ln_mix=dln_mix, b_forget=db_forget[:, :FOX_HEADS], b_merge=db_merge,
                       ln_ffn2=dln_ffn2, ln_ple=dln_ple, ln_final=dln_final)
    sizes = {n: w[n].size for n in SMALL}
    reduced = _all_reduce_small(_pack_small(small_grads, loss))
    gsum = _unpack_small(reduced, sizes)
    loss = reduced[0, 0]

    results = {}

    def update(n, stages=()):
        w2 = to2d(n, w[n])
        if n in gsum:
            g2 = gsum[n]
        else:
            g2 = red.done[n].reshape(-1, w2.shape[1])[:w2.shape[0]]
        dl, nm, nv = _hosted(_adamw, w2, g2, to2d(n, m[n]), to2d(n, v[n]), stages=stages)
        results[n] = tuple(from2d(n, a) for a in (g2, dl, nm, nv))

    ex_f1u, sw_f1d = red.exchange(sw_f1u.result), red.swap(dict(w_ffn1_down=g_f1d))
    update("w_ffn2_gate", stages=[ex_f1u, sw_f1d, red.join(ex_f1g.result)])
    ex_f1d = red.exchange(sw_f1d.result)
    update("w_ffn2_up", stages=[ex_f1d, red.join(ex_f1u.result)])
    update("w_ffn2_down", stages=[red.join(ex_f1d.result)])
    for n in WEIGHTS:
        if n not in results:
            update(n)

    outs = [[results[n][k] for n in WEIGHTS] for k in range(4)]
    return (loss, dx[None], *outs[0], *outs[1], *outs[2], *outs[3])
```

```python
import functools
import operator

import jax
import jax.numpy as jnp
from jax import lax
from jax.experimental import pallas as pl
from jax.experimental.pallas import tpu as pltpu

F32 = jnp.float32
BF = jnp.bfloat16
MESH = pl.DeviceIdType.MESH

EPS = 1e-6
ROPE_BASE = 10000.0
N_CHIPS = 4
RET_HEADS = 4
RET_DIM = 128
RET_WIDTH = RET_HEADS * RET_DIM
RET_CHUNK = 128
RET_SCALE = RET_DIM ** -0.5
FOX_HEADS = 8
FOX_DIM = 64
FOX_WIDTH = FOX_HEADS * FOX_DIM
FOX_SCALE = FOX_DIM ** -0.5
IN_COLS = 4 * RET_WIDTH + 3 * FOX_WIDTH + FOX_HEADS
IN_PAD = 4096
FF_COL = 4 * RET_WIDTH + 3 * FOX_WIDTH
NEG = -1e30

ADAM_LR = 0.001
ADAM_B1 = 0.9
ADAM_B2 = 0.999
ADAM_EPS = 1e-08
ADAM_WD = 0.01
ADAM_STEP = 10

VMEM_LIMIT = 52 * 1024 * 1024

NT = (((1,), (1,)), ((), ()))
TN = (((0,), (0,)), ((), ()))

HBM_SPEC = pl.BlockSpec(memory_space=pltpu.HBM)
VMEM_SPEC = pl.BlockSpec(memory_space=pltpu.VMEM)


def _dot(a, b):
    return jnp.dot(a, b, preferred_element_type=F32)


def _dot_nt(a, b):
    return lax.dot_general(a, b, NT, preferred_element_type=F32)


def _dot_tn(a, b):
    return lax.dot_general(a, b, TN, preferred_element_type=F32)


def _rstd(xv):
    return lax.rsqrt(jnp.mean(xv * xv, axis=-1, keepdims=True) + EPS)


def _rms_bwd(dn, xv, r, ln):
    xh = xv * r
    dxh = dn * ln
    dx = r * (dxh - xh * jnp.mean(dxh * xh, axis=-1, keepdims=True))
    return dx, jnp.sum(dn * xh, axis=0, keepdims=True)


def _sigmoid(x):
    return jax.nn.sigmoid(x)


def _tile(n, pref):
    return pref if n % pref == 0 else n


def _row_tile(n, cap):
    best = [t for t in range(16, min(n, cap) + 1, 16) if n % t == 0]
    return best[-1] if best else n


class _Comm:
    def __init__(self, ins, out_shapes, sems, start, wait):
        self.ins, self.out_shapes, self.sems, self.start, self.wait = list(ins), list(out_shapes), list(sems), start, wait


def _merge(comms):
    comms = [c for c in comms if c is not None]
    if not comms:
        return None
    bounds, ni, no, ns = [], 0, 0, 0
    for c in comms:
        bounds.append((ni, no, ns))
        ni, no, ns = ni + len(c.ins), no + len(c.out_shapes), ns + len(c.sems)

    def run(which):
        def f(ins, outs, sems):
            for c, (i, o, s) in zip(comms, bounds):
                getattr(c, which)(ins[i:i + len(c.ins)], outs[o:o + len(c.out_shapes)], sems[s:s + len(c.sems)])
        return f

    return _Comm([a for c in comms for a in c.ins], [a for c in comms for a in c.out_shapes],
                 [a for c in comms for a in c.sems], run("start"), run("wait"))


def _split_outs(comms, outs):
    res, o = [], 0
    for c in comms:
        if c is not None:
            res.append(list(outs[o:o + len(c.out_shapes)]))
            o += len(c.out_shapes)
    return res


def _pcall(body, args, *, name, out_shape, grid=(), in_specs=None, out_specs=None, scratch=(), comm=None):
    many = isinstance(out_shape, (list, tuple))
    outs = list(out_shape) if many else [out_shape]
    n_in, n_out, n_scr = len(args), len(outs), len(scratch)
    if in_specs is None:
        in_specs, out_specs = [VMEM_SPEC] * n_in, [VMEM_SPEC] * n_out
    else:
        in_specs, out_specs = list(in_specs), (list(out_specs) if many else [out_specs])
    params = pltpu.CompilerParams(dimension_semantics=("arbitrary",) * len(grid), vmem_limit_bytes=VMEM_LIMIT)
    if comm is None:
        res = pl.pallas_call(body, name=name, grid=grid, out_shape=outs, in_specs=in_specs, out_specs=out_specs,
                             scratch_shapes=list(scratch), compiler_params=params)(*args)
        return list(res) if many else res[0]
    ci, co = len(comm.ins), len(comm.out_shapes)

    def wrapped(*refs):
        a, ca = refs[:n_in], refs[n_in:n_in + ci]
        o = refs[n_in + ci:n_in + ci + n_out]
        cout = refs[n_in + ci + n_out:n_in + ci + n_out + co]
        s = refs[n_in + ci + n_out + co:n_in + ci + n_out + co + n_scr]
        csem = refs[n_in + ci + n_out + co + n_scr:]
        if grid:
            first = functools.reduce(operator.and_, [pl.program_id(k) == 0 for k in range(len(grid))])
            last = functools.reduce(operator.and_, [pl.program_id(k) == grid[k] - 1 for k in range(len(grid))])
            pl.when(first)(lambda: comm.start(ca, cout, csem))
            body(*a, *o, *s)
            pl.when(last)(lambda: comm.wait(ca, cout, csem))
        else:
            comm.start(ca, cout, csem)
            body(*a, *o, *s)
            comm.wait(ca, cout, csem)

    res = pl.pallas_call(
        wrapped, name=name, grid=grid, out_shape=outs + comm.out_shapes,
        in_specs=in_specs + [HBM_SPEC] * ci, out_specs=out_specs + [HBM_SPEC] * co,
        scratch_shapes=list(scratch) + comm.sems, compiler_params=params)(*args, *comm.ins)
    mine = list(res[:n_out])
    return (mine if many else mine[0]), list(res[n_out:])


def _peer_chips(x, y):
    return [(1 - x, y), (x, 1 - y), (1 - x, 1 - y)]


def _c_all_gather(shards):
    n = len(shards)

    def copies(ins, outs, sems):
        send_sems, recv_sems, fwd_send, fwd_recv, local_sems = sems
        x, y, c = lax.axis_index("x"), lax.axis_index("y"), lax.axis_index("c")
        me = 2 * x + y
        peers = _peer_chips(x, y)
        chip = [2 * px + py for px, py in peers]

        def ici(g, j, slot):
            return pltpu.make_async_remote_copy(
                src_ref=ins[g].at[c], dst_ref=outs[g].at[slot, c], send_sem=send_sems.at[g, j],
                recv_sem=recv_sems.at[g, j], device_id=(*peers[j], c), device_id_type=MESH)

        def d2d(g, j, half):
            return pltpu.make_async_remote_copy(
                src_ref=outs[g].at[chip[j], half], dst_ref=outs[g].at[chip[j], half], send_sem=fwd_send.at[g, j],
                recv_sem=fwd_recv.at[g, j], device_id=(x, y, 1 - c), device_id_type=MESH)

        pairs = [(g, j) for g in range(n) for j in range(3)]
        local = [pltpu.make_async_copy(ins[g], outs[g].at[me], local_sems.at[g]) for g in range(n)]
        sends = [ici(g, j, me) for g, j in pairs]
        recvs = [ici(g, j, chip[j]) for g, j in pairs]
        passes = [d2d(g, j, c) for g, j in pairs]
        passed = [d2d(g, j, 1 - c) for g, j in pairs]
        return local, sends, recvs, passes, passed

    def start(ins, outs, sems):
        local, sends, _, _, _ = copies(ins, outs, sems)
        for cp in local + sends:
            cp.start()

    def wait(ins, outs, sems):
        local, sends, recvs, passes, passed = copies(ins, outs, sems)
        for rcv, fwd in zip(recvs, passes):
            rcv.wait_recv()
            fwd.start()
        for cp in passed:
            cp.wait_recv()
        for cp in sends + passes:
            cp.wait_send()
        for cp in local:
            cp.wait()

    pair_sems = pltpu.SemaphoreType.DMA((n, 3))
    return _Comm(
        shards, [jax.ShapeDtypeStruct((N_CHIPS,) + s.shape, s.dtype) for s in shards],
        [pair_sems, pair_sems, pair_sems, pair_sems, pltpu.SemaphoreType.DMA((n,))], start, wait)


def _start_wait(copies):
    def start(ins, outs, sems):
        local, sends, _ = copies(ins, outs, sems)
        for cp in local + sends:
            cp.start()

    def wait(ins, outs, sems):
        local, sends, recvs = copies(ins, outs, sems)
        for cp in recvs:
            cp.wait_recv()
        for cp in sends:
            cp.wait_send()
        for cp in local:
            cp.wait()

    return start, wait


def _c_half_swap(grads):
    n = len(grads)

    def copies(ins, outs, sems):
        send_sems, recv_sems = sems
        x, y, c = lax.axis_index("x"), lax.axis_index("y"), lax.axis_index("c")
        sends = []
        for g in range(n):
            half = ins[g].shape[1] // 2
            sends.append(pltpu.make_async_remote_copy(
                src_ref=ins[g].at[:, pl.ds((1 - c) * half, half), :], dst_ref=outs[g],
                send_sem=send_sems.at[g], recv_sem=recv_sems.at[g], device_id=(x, y, 1 - c), device_id_type=MESH))
        return [], sends, sends

    return _Comm(
        grads, [jax.ShapeDtypeStruct((N_CHIPS, s.shape[1] // 2, s.shape[2]), s.dtype) for s in grads],
        [pltpu.SemaphoreType.DMA((n,)), pltpu.SemaphoreType.DMA((n,))], *_start_wait(copies))


def _c_chip_exchange(parts):
    n = len(parts)

    def copies(ins, outs, sems):
        send_sems, recv_sems, local_sems = sems
        x, y, c = lax.axis_index("x"), lax.axis_index("y"), lax.axis_index("c")
        me = 2 * x + y
        peers = _peer_chips(x, y)

        def remote(g, j, src_slot, dst_slot):
            return pltpu.make_async_remote_copy(
                src_ref=ins[g].at[src_slot], dst_ref=outs[g].at[dst_slot], send_sem=send_sems.at[g, j],
                recv_sem=recv_sems.at[g, j], device_id=(*peers[j], c), device_id_type=MESH)

        chip = [2 * px + py for px, py in peers]
        local = [pltpu.make_async_copy(ins[g].at[me], outs[g].at[me], local_sems.at[g]) for g in range(n)]
        sends = [remote(g, j, chip[j], me) for g in range(n) for j in range(3)]
        recvs = [remote(g, j, me, chip[j]) for g in range(n) for j in range(3)]
        return local, sends, recvs

    return _Comm(
        parts, [jax.ShapeDtypeStruct(s.shape, s.dtype) for s in parts],
        [pltpu.SemaphoreType.DMA((n, 3)), pltpu.SemaphoreType.DMA((n, 3)), pltpu.SemaphoreType.DMA((n,))],
        *_start_wait(copies))


def _c_join(halves):
    n = len(halves)

    def copies(ins, outs, sems):
        send_sems, recv_sems, local_sems = sems
        x, y, c = lax.axis_index("x"), lax.axis_index("y"), lax.axis_index("c")

        def remote(g, slot):
            return pltpu.make_async_remote_copy(
                src_ref=ins[g], dst_ref=outs[g].at[slot], send_sem=send_sems.at[g], recv_sem=recv_sems.at[g],
                device_id=(x, y, 1 - c), device_id_type=MESH)

        local = [pltpu.make_async_copy(ins[g], outs[g].at[c], local_sems.at[g]) for g in range(n)]
        return local, [remote(g, c) for g in range(n)], [remote(g, 1 - c) for g in range(n)]

    return _Comm(
        halves, [jax.ShapeDtypeStruct((2,) + s.shape, s.dtype) for s in halves],
        [pltpu.SemaphoreType.DMA((n,)), pltpu.SemaphoreType.DMA((n,)), pltpu.SemaphoreType.DMA((n,))],
        *_start_wait(copies))


def _all_reduce_small(v):
    rows = v.shape[0]

    def body(v_ref, out_ref, buf, send_sems, recv_sems):
        x, y, c = lax.axis_index("x"), lax.axis_index("y"), lax.axis_index("c")
        me = 4 * x + 2 * y + c
        buf[me] = v_ref[...]
        flips = [(fx, fy, fc) for fx in (0, 1) for fy in (0, 1) for fc in (0, 1)][1:]

        def peer(k):
            fx, fy, fc = flips[k]
            px, py, pc = x ^ fx, y ^ fy, c ^ fc
            return (px, py, pc), 4 * px + 2 * py + pc

        def copy(k, slot):
            return pltpu.make_async_remote_copy(
                src_ref=buf.at[slot], dst_ref=buf.at[slot], send_sem=send_sems.at[k],
                recv_sem=recv_sems.at[k], device_id=peer(k)[0], device_id_type=MESH)

        sends = [copy(k, me) for k in range(7)]
        for cp in sends:
            cp.start()
        for k in range(7):
            copy(k, peer(k)[1]).wait_recv()
        for cp in sends:
            cp.wait_send()
        acc = buf[0]
        for d in range(1, 8):
            acc = acc + buf[d]
        out_ref[...] = acc

    return _pcall(body, [v], name="all_reduce_small", out_shape=jax.ShapeDtypeStruct((rows, 128), F32),
                  scratch=[pltpu.VMEM((8, rows, 128), F32), pltpu.SemaphoreType.DMA((7,)),
                           pltpu.SemaphoreType.DMA((7,))])


def _add_halves(g, got):
    _, h, c = got.shape
    th = _row_tile(h, 512)
    nh = h // th
    half = lax.axis_index("c") * nh

    def body(h_ref, a_ref, b_ref, o_ref):
        o_ref[...] = (a_ref[...].astype(F32) + b_ref[...].astype(F32)).astype(o_ref.dtype)

    spec = pl.BlockSpec((1, th, c), lambda j, i, h_ref: (j, i, 0))
    mine = pl.BlockSpec((1, th, c), lambda j, i, h_ref: (j, h_ref[0] + i, 0))
    return _pcall_prefetch(body, half, [g, got], name="add_halves", grid=(N_CHIPS, nh),
                           out_shape=jax.ShapeDtypeStruct(got.shape, BF), in_specs=[mine, spec], out_specs=spec)


def _pcall_prefetch(body, scalar, args, *, name, grid, out_shape, in_specs, out_specs):
    return pl.pallas_call(
        body, name=name, out_shape=out_shape,
        grid_spec=pltpu.PrefetchScalarGridSpec(num_scalar_prefetch=1, grid=grid, in_specs=in_specs,
                                               out_specs=out_specs),
        compiler_params=pltpu.CompilerParams(dimension_semantics=("arbitrary",) * len(grid),
                                             vmem_limit_bytes=VMEM_LIMIT),
    )(jnp.reshape(scalar, (1,)).astype(jnp.int32), *args)


def _sum_chips(parts, comm=None):
    _, h, c = parts.shape
    th = _row_tile(h, 512)

    def body(p_ref, o_ref):
        acc = p_ref[0].astype(F32)
        for s in range(1, N_CHIPS):
            acc = acc + p_ref[s].astype(F32)
        o_ref[...] = acc

    return _pcall(body, [parts], name="sum_chips", grid=(h // th,), out_shape=jax.ShapeDtypeStruct((h, c), F32),
                  in_specs=[pl.BlockSpec((N_CHIPS, th, c), lambda i: (0, i, 0))],
                  out_specs=pl.BlockSpec((th, c), lambda i: (i, 0)), comm=comm)


def _adamw(w, g, m, v, comm=None):
    r, c = w.shape
    tr = _row_tile(r, 512)
    c1 = 1.0 / (1.0 - ADAM_B1 ** ADAM_STEP)
    c2 = 1.0 / (1.0 - ADAM_B2 ** ADAM_STEP)

    def body(w_ref, g_ref, m_ref, v_ref, d_ref, nm_ref, nv_ref):
        gv = g_ref[...]
        nm = ADAM_B1 * m_ref[...] + (1.0 - ADAM_B1) * gv
        nv = ADAM_B2 * v_ref[...] + (1.0 - ADAM_B2) * (gv * gv)
        nm_ref[...] = nm
        nv_ref[...] = nv
        d_ref[...] = -ADAM_LR * ((nm * c1) / (jnp.sqrt(nv * c2) + ADAM_EPS) + ADAM_WD * w_ref[...])

    spec = pl.BlockSpec((tr, c), lambda i: (i, 0))
    sds = jax.ShapeDtypeStruct((r, c), F32)
    return _pcall(body, [w, g, m, v], name="adamw", grid=(r // tr,), out_shape=[sds, sds, sds],
                  in_specs=[spec] * 4, out_specs=[spec] * 3, comm=comm)


def _wgrad(name, a, b, a_spec, b_spec, m, n, nb, comm):
    def body(a_ref, b_ref, o_ref):
        o_ref[...] = _dot_tn(a_ref[...], b_ref[...]).astype(o_ref.dtype)

    return _pcall(body, [a, b], name=name, grid=(nb,), out_shape=jax.ShapeDtypeStruct((nb, m, n), BF),
                  in_specs=[a_spec, b_spec], out_specs=pl.BlockSpec((None, m, n), lambda j: (j, 0, 0)), comm=comm)


def _wgrad_cols(name, a, b, nb, comm=None):
    t_tok, m = a.shape
    n = b.shape[1] // nb
    return _wgrad(name, a, b, pl.BlockSpec((t_tok, m), lambda j: (0, 0)), pl.BlockSpec((t_tok, n), lambda j: (0, j)),
                  m, n, nb, comm)


def _wgrad_rows(name, a, b, nb, comm=None):
    t_tok, n = b.shape
    m = a.shape[1] // nb
    return _wgrad(name, a, b, pl.BlockSpec((t_tok, m), lambda j: (0, j)), pl.BlockSpec((t_tok, n), lambda j: (0, 0)),
                  m, n, nb, comm)


def _wgrad_a_shared(name, a, b4, comm=None):
    t_tok, m = a.shape
    nb, _, n = b4.shape
    return _wgrad(name, a, b4, pl.BlockSpec((t_tok, m), lambda j: (0, 0)),
                  pl.BlockSpec((None, t_tok, n), lambda j: (j, 0, 0)), m, n, nb, comm)


def _wgrad_b_shared(name, a4, b, comm=None):
    nb, t_tok, m = a4.shape
    n = b.shape[1]
    return _wgrad(name, a4, b, pl.BlockSpec((None, t_tok, m), lambda j: (j, 0, 0)),
                  pl.BlockSpec((t_tok, n), lambda j: (0, 0)), m, n, nb, comm)


def _w4_spec(r, c):
    return pl.BlockSpec((None, r, c), lambda i, j: (j, 0, 0))


FFN_ROW_CHUNK = 256


def _row_chunks(tm):
    rc = FFN_ROW_CHUNK if tm % FFN_ROW_CHUNK == 0 else tm
    return [slice(r, r + rc) for r in range(0, tm, rc)]


def _ffn_fwd(h, ln, wg4, wu4, wd4, comm=None):
    t_tok, d = h.shape
    f = wg4.shape[-2]
    tm = _tile(t_tok, 512)

    def body(h_ref, ln_ref, wg_ref, wu_ref, wd_ref, ho_ref, n_ref, g_ref, u_ref, n_s, acc):
        j = pl.program_id(1)

        @pl.when(j == 0)
        def _():
            xv = h_ref[...]
            nv = (xv * _rstd(xv) * ln_ref[...]).astype(BF)
            n_s[...] = nv
            n_ref[...] = nv
            acc[...] = jnp.zeros_like(acc)

        nv = n_s[...]
        g = _dot_nt(nv, wg_ref[...])
        u = _dot_nt(nv, wu_ref[...])
        g_ref[...] = g.astype(BF)
        u_ref[...] = u.astype(BF)
        a = (g * _sigmoid(g) * u).astype(BF)
        acc[...] += _dot(a, wd_ref[...])

        @pl.when(j == N_CHIPS - 1)
        def _():
            ho_ref[...] = h_ref[...] + 0.5 * acc[...]

    row = pl.BlockSpec((tm, d), lambda i, j: (i, 0))
    gu = pl.BlockSpec((None, tm, f), lambda i, j: (j, i, 0))
    gu_sds = jax.ShapeDtypeStruct((N_CHIPS, t_tok, f), BF)
    return _pcall(
        body, [h, ln, wg4, wu4, wd4], name="ffn_fwd", grid=(t_tok // tm, N_CHIPS),
        out_shape=[jax.ShapeDtypeStruct((t_tok, d), F32), jax.ShapeDtypeStruct((t_tok, d), BF), gu_sds, gu_sds],
        in_specs=[row, pl.BlockSpec((1, d), lambda i, j: (0, 0)), _w4_spec(f, d), _w4_spec(f, d), _w4_spec(f, d)],
        out_specs=[row, row, gu, gu],
        scratch=[pltpu.VMEM((tm, d), BF), pltpu.VMEM((tm, d), F32)], comm=comm)


def _ffn_bwd(dho, h, ln, g4, u4, wg4, wu4, wd4, comm=None):
    t_tok, d = h.shape
    f = wg4.shape[-2]
    tm = _tile(t_tok, 512)

    def body(dho_ref, h_ref, ln_ref, g_ref, u_ref, wg_ref, wu_ref, wd_ref,
             dhi_ref, dln_ref, dg_ref, du_ref, a_ref, dhb_ref, dhb_s, dn_acc):
        i, j = pl.program_id(0), pl.program_id(1)

        @pl.when(j == 0)
        def _():
            dhb = (0.5 * dho_ref[...]).astype(BF)
            dhb_s[...] = dhb
            dhb_ref[...] = dhb
            dn_acc[...] = jnp.zeros_like(dn_acc)

        @pl.when((i == 0) & (j == 0))
        def _():
            dln_ref[...] = jnp.zeros_like(dln_ref)

        for rows in _row_chunks(tm):
            g = g_ref[rows, :].astype(F32)
            u = u_ref[rows, :].astype(F32)
            s = _sigmoid(g)
            sg = g * s
            a_ref[rows, :] = (sg * u).astype(BF)
            da = _dot_nt(dhb_s[rows, :], wd_ref[...])
            dg = (da * u * (s * (1.0 + g * (1.0 - s)))).astype(BF)
            du = (da * sg).astype(BF)
            dg_ref[rows, :] = dg
            du_ref[rows, :] = du
            dn_acc[rows, :] += _dot(dg, wg_ref[...]) + _dot(du, wu_ref[...])

        @pl.when(j == N_CHIPS - 1)
        def _():
            xv = h_ref[...]
            dx, dln = _rms_bwd(dn_acc[...], xv, _rstd(xv), ln_ref[...])
            dln_ref[...] += dln
            dhi_ref[...] = dho_ref[...] + dx

    row = pl.BlockSpec((tm, d), lambda i, j: (i, 0))
    vec = pl.BlockSpec((1, d), lambda i, j: (0, 0))
    gu = pl.BlockSpec((None, tm, f), lambda i, j: (j, i, 0))
    gu_sds = jax.ShapeDtypeStruct((N_CHIPS, t_tok, f), BF)
    return _pcall(
        body, [dho, h, ln, g4, u4, wg4, wu4, wd4], name="ffn_bwd", grid=(t_tok // tm, N_CHIPS),
        out_shape=[jax.ShapeDtypeStruct((t_tok, d), F32), jax.ShapeDtypeStruct((1, d), F32),
                   gu_sds, gu_sds, gu_sds, jax.ShapeDtypeStruct((t_tok, d), BF)],
        in_specs=[row, row, vec, gu, gu, _w4_spec(f, d), _w4_spec(f, d), _w4_spec(f, d)],
        out_specs=[row, vec, gu, gu, gu, row],
        scratch=[pltpu.VMEM((tm, d), BF), pltpu.VMEM((tm, d), F32)], comm=comm)


def _rope_tables(pos_col, inv_freq2, comm=None):
    t_tok = pos_col.shape[0]

    def body(p_ref, f_ref, cos_ref, sin_ref):
        ang = p_ref[...] * f_ref[...]
        lane = lax.broadcasted_iota(jnp.int32, ang.shape, 1)
        s = jnp.sin(ang)
        cos_ref[...] = jnp.cos(ang)
        sin_ref[...] = jnp.where((lane & 1) == 0, -s, s)

    sds = jax.ShapeDtypeStruct((t_tok, 128), F32)
    return _pcall(body, [pos_col, inv_freq2], name="rope_tables", out_shape=[sds, sds], comm=comm)


def _swap_pairs(x):
    lane = lax.broadcasted_iota(jnp.int32, x.shape, 1)
    return jnp.where((lane & 1) == 0, pltpu.roll(x, 127, 1), pltpu.roll(x, 1, 1))


def _mix_in(h, ln, w_in, wm4, b_m, cos_t, sin_t, comm=None):
    t_tok, d = h.shape
    cm = wm4.shape[-1]
    tm = _tile(t_tok, 256)

    def body(h_ref, ln_ref, win_ref, wm_ref, bm_ref, cos_ref, sin_ref,
             u_ref, rq_ref, rk_ref, rv_ref, rg_ref, fq_ref, fk_ref, fv_ref, ff_ref, ga_ref, gb_ref):
        xv = h_ref[...]
        ub = (xv * _rstd(xv) * ln_ref[...]).astype(BF)
        u_ref[...] = ub
        cosv, sinv = cos_ref[...], sin_ref[...]

        def sec(k):
            return _dot_nt(ub, win_ref[k * 512:(k + 1) * 512, :])

        def rot(xh):
            return xh * cosv + _swap_pairs(xh) * sinv

        pq, pk = sec(0), sec(1)
        for hh in range(RET_HEADS):
            sl = slice(hh * RET_DIM, (hh + 1) * RET_DIM)
            rq_ref[:, sl] = rot(pq[:, sl]).astype(BF)
            rk_ref[:, sl] = (rot(pk[:, sl]) * RET_SCALE).astype(BF)
        rv_ref[...] = sec(2).astype(BF)
        rg_ref[...] = sec(3).astype(BF)
        fq_ref[...] = (sec(4) * FOX_SCALE).astype(BF)
        fk_ref[...] = sec(5).astype(BF)
        fv_ref[...] = sec(6).astype(BF)
        ff_ref[...] = _dot_nt(ub, win_ref[FF_COL:FF_COL + 128, :])
        for j in range(N_CHIPS):
            gs = _sigmoid(_dot(ub, wm_ref[j]) + bm_ref[:, j * cm:(j + 1) * cm]).astype(BF)
            col = j * cm
            if col < d:
                ga_ref[:, col:col + cm] = gs
            else:
                gb_ref[:, col - d:col - d + cm] = gs

    row = lambda c: pl.BlockSpec((tm, c), lambda i: (i, 0))
    full = lambda *s: pl.BlockSpec(s, lambda i: (0,) * len(s))
    sds = lambda c, dt: jax.ShapeDtypeStruct((t_tok, c), dt)
    return _pcall(
        body, [h, ln, w_in, wm4, b_m, cos_t, sin_t], name="mix_in", grid=(t_tok // tm,),
        out_shape=[sds(d, BF)] + [sds(512, BF)] * 7 + [sds(128, F32), sds(d, BF), sds(d, BF)],
        in_specs=[row(d), full(1, d), full(IN_PAD, d), full(N_CHIPS, d, cm), full(1, 2 * d), row(128), row(128)],
        out_specs=[row(d)] + [row(512)] * 7 + [row(128), row(d), row(d)], comm=comm)


def _split3(x):
    hi = x.astype(BF)
    r1 = x - hi.astype(F32)
    mid = r1.astype(BF)
    lo = (r1 - mid.astype(F32)).astype(BF)
    return hi, mid, lo


def _aug_lane():
    return lax.broadcasted_iota(jnp.int32, (1, 128), 1) & (FOX_DIM - 1)


def _aug_put(base, k0, parts):
    w = _aug_lane()
    for i, part in enumerate(parts):
        base = jnp.where(w == k0 + i, part, base)
    return base


def _forget_fwd(ffl, b_pad):
    t_tok = ffl.shape[0]
    tb = _tile(t_tok, 256)

    def body(ff_ref, b_ref, aq_ref, ak_ref, cum_s):
        r = lax.broadcasted_iota(jnp.int32, (tb, tb), 0)
        c = lax.broadcasted_iota(jnp.int32, (tb, tb), 1)
        tri = jnp.where(c <= r, 1.0, 0.0).astype(BF)
        carry = jnp.zeros((1, 128), F32)
        for i in range(t_tok // tb):
            z = ff_ref[i * tb:(i + 1) * tb, :] + b_ref[...]
            lf = jnp.minimum(z, 0.0) - jnp.log(1.0 + jnp.exp(-jnp.abs(z)))
            hi, mid, lo = _split3(lf)
            cs = _dot(tri, hi) + _dot(tri, mid) + _dot(tri, lo) + carry
            cum_s[i * tb:(i + 1) * tb, :] = cs
            carry = cs[tb - 1:tb, :]
        x = cum_s[...]
        first = lax.broadcasted_iota(jnp.int32, (1, 128), 1) < FOX_DIM
        w = _aug_lane()
        one = jnp.ones((t_tok, 128), BF)
        zero = jnp.zeros((t_tok, 128), BF)
        for pp in range(FOX_HEADS // 2):
            other = jnp.where(first, x[:, 2 * pp + 1:2 * pp + 2], x[:, 2 * pp:2 * pp + 1])
            parts = _split3(other)
            aq = jnp.where((w >= 3) & (w < 6), one, zero)
            ak = jnp.where((w < 3) | ((w >= 6) & (w < 9)), one, zero)
            aq_ref[:, pp * 128:(pp + 1) * 128] = _aug_put(aq, 0, parts)
            ak_ref[:, pp * 128:(pp + 1) * 128] = _aug_put(ak, 3, [-q for q in parts])

    sds = jax.ShapeDtypeStruct((t_tok, FOX_WIDTH), BF)
    return _pcall(body, [ffl, b_pad], name="forget_fwd", out_shape=[sds, sds],
                  scratch=[pltpu.VMEM((t_tok, 128), F32)])


def _fox_aug_lse(aq, lse_e):
    t_tok = aq.shape[0]
    tm = _tile(t_tok, 512)

    def body(aq_ref, lse_ref, o_ref):
        for pp in range(FOX_HEADS // 2):
            sl = slice(pp * 128, (pp + 1) * 128)
            other = pltpu.roll(lse_ref[:, sl], FOX_DIM, 1)
            o_ref[:, sl] = _aug_put(aq_ref[:, sl], 6, _split3(-other))

    spec = pl.BlockSpec((tm, FOX_WIDTH), lambda i: (i, 0))
    return _pcall(body, [aq, lse_e], name="fox_aug_lse", grid=(t_tok // tm,),
                  out_shape=jax.ShapeDtypeStruct((t_tok, FOX_WIDTH), BF), in_specs=[spec, spec], out_specs=spec)


def _forget_bwd(dcum_t, dcum_q, ffl, b_pad):
    t_tok = ffl.shape[0]
    tb = _tile(t_tok, 256)

    def body(dc_ref, dq_ref, ff_ref, b_ref, dff_ref, db_ref, pad_s, d_s):
        pad_s[...] = jnp.zeros_like(pad_s)
        pad_s[0:FOX_HEADS, :] = dc_ref[...]
        dsum = pad_s[...].T
        lane = lax.broadcasted_iota(jnp.int32, (t_tok, 128), 1)
        for hh in range(FOX_HEADS):
            dsum = dsum + jnp.where(lane == hh, dq_ref[:, hh * FOX_DIM:hh * FOX_DIM + 1], 0.0)
        d_s[...] = dsum
        r = lax.broadcasted_iota(jnp.int32, (tb, tb), 0)
        c = lax.broadcasted_iota(jnp.int32, (tb, tb), 1)
        tri = jnp.where(c >= r, 1.0, 0.0).astype(BF)
        carry = jnp.zeros((1, 128), F32)
        db = jnp.zeros((1, 128), F32)
        for i in reversed(range(t_tok // tb)):
            hi, mid, lo = _split3(d_s[i * tb:(i + 1) * tb, :])
            dlf = _dot(tri, hi) + _dot(tri, mid) + _dot(tri, lo) + carry
            carry = dlf[0:1, :]
            z = ff_ref[i * tb:(i + 1) * tb, :] + b_ref[...]
            dff = dlf * _sigmoid(-z)
            dff_ref[i * tb:(i + 1) * tb, :] = dff.astype(BF)
            db = db + jnp.sum(dff, axis=0, keepdims=True)
        db_ref[...] = db

    return _pcall(
        body, [dcum_t, dcum_q, ffl, b_pad], name="forget_bwd",
        out_shape=[jax.ShapeDtypeStruct((t_tok, 128), BF), jax.ShapeDtypeStruct((1, 128), F32)],
        scratch=[pltpu.VMEM((128, t_tok), F32), pltpu.VMEM((t_tok, 128), F32)])


def _first_half():
    return lax.broadcasted_iota(jnp.int32, (1, 128), 1) < FOX_DIM


def _head_rows(x2, a2, hh):
    return jnp.where(_first_half(), x2, a2) if hh == 0 else jnp.where(_first_half(), a2, x2)


def _head_only(x2, hh):
    zero = jnp.zeros_like(x2)
    return jnp.where(_first_half(), x2, zero) if hh == 0 else jnp.where(_first_half(), zero, x2)


def _causal_diag(s):
    rows = lax.broadcasted_iota(jnp.int32, s.shape, 0)
    cols = lax.broadcasted_iota(jnp.int32, s.shape, 1)
    return jnp.where(cols <= rows, s, NEG)


def _diag_or_below(qi, ki, step):
    pl.when(ki < qi)(lambda: step(False))
    pl.when(ki == qi)(lambda: step(True))


def _tri_rows(s, n):
    qi = sum((s >= r * (r + 1) // 2).astype(jnp.int32) for r in range(1, n))
    return qi, s - (qi * (qi + 1)) // 2


def _tri_cols(s, n):
    ki = sum((s >= k * n - k * (k - 1) // 2).astype(jnp.int32) for k in range(1, n))
    return ki, ki + s - (ki * n - (ki * (ki - 1)) // 2)


def _fox_fwd(fq, fk, fv, aq, ak, comm=None):
    t_tok = fq.shape[0]
    t = _tile(t_tok, 512)
    nq = t_tok // t
    npair = FOX_HEADS // 2

    def body(q_ref, k_ref, v_ref, aq_ref, ak_ref, o_ref, of_ref, lse_ref, m_s, l_s, acc_s):
        qi, ki = _tri_rows(pl.program_id(1), nq)

        @pl.when(ki == 0)
        def _():
            m_s[...] = jnp.full_like(m_s, NEG)
            l_s[...] = jnp.zeros_like(l_s)
            acc_s[...] = jnp.zeros_like(acc_s)

        def step(diag):
            q2, k2, v2, aq2, ak2 = q_ref[...], k_ref[...], v_ref[...], aq_ref[...], ak_ref[...]
            for hh in range(2):
                s = _dot_nt(_head_rows(q2, aq2, hh), _head_rows(k2, ak2, hh))
                if diag:
                    s = _causal_diag(s)
                m_prev = m_s[hh]
                m_new = jnp.maximum(m_prev, jnp.max(s, axis=1, keepdims=True))
                alpha = jnp.exp(m_prev - m_new)
                p = jnp.exp(s - jnp.tile(m_new, (1, t // 128)))
                l_s[hh] = alpha * l_s[hh] + jnp.sum(p, axis=1, keepdims=True)
                acc_s[hh] = alpha * acc_s[hh] + _dot(p.astype(BF), v2)
                m_s[hh] = m_new

        _diag_or_below(qi, ki, step)

        @pl.when(ki == qi)
        def _():
            first = _first_half()
            o = jnp.where(first, acc_s[0] / l_s[0], acc_s[1] / l_s[1])
            o_ref[...] = o.astype(BF)
            of_ref[...] = o
            lse_ref[...] = jnp.where(first, m_s[0] + jnp.log(l_s[0]), m_s[1] + jnp.log(l_s[1]))

    qs = pl.BlockSpec((t, 128), lambda p, s: (_tri_rows(s, nq)[0], p))
    ks = pl.BlockSpec((t, 128), lambda p, s: (_tri_rows(s, nq)[1], p))
    stat = pltpu.VMEM((2, t, 128), F32)
    return _pcall(
        body, [fq, fk, fv, aq, ak], name="fox_fwd", grid=(npair, nq * (nq + 1) // 2),
        out_shape=[jax.ShapeDtypeStruct((t_tok, FOX_WIDTH), BF), jax.ShapeDtypeStruct((t_tok, FOX_WIDTH), F32),
                   jax.ShapeDtypeStruct((t_tok, FOX_WIDTH), F32)],
        in_specs=[qs, ks, ks, qs, ks], out_specs=[qs, qs, qs], scratch=[stat, stat, stat], comm=comm)


def _fox_ds(q2, k2, v2, do2, aq2, ak2, ad2, hh, diag):
    s = _dot_nt(_head_rows(q2, aq2, hh), _head_rows(k2, ak2, hh))
    if diag:
        s = _causal_diag(s)
    p = jnp.exp(s)
    av = jnp.where(_aug_lane() < 3, 1.0, 0.0).astype(BF)
    dp = _dot_nt(_head_rows(do2, ad2, hh), _head_rows(v2, jnp.broadcast_to(av, v2.shape), hh))
    return p, p * dp


def _fox_bwd(fq, fk, fv, do, aqb, ak, ad, comm=None):
    t_tok = fq.shape[0]
    t = _tile(t_tok, 512)
    nq = t_tok // t
    npair = FOX_HEADS // 2
    n_steps = nq * (nq + 1) // 2

    def body(q_ref, k_ref, v_ref, do_ref, aq_ref, ak_ref, ad_ref, dq_ref, dk_ref, dv_ref, dck_ref, dcq_ref,
             dk_s, dv_s, dq_s, rs_s):
        step_id = pl.program_id(1)
        ki, qi = _tri_cols(step_id, nq)

        @pl.when(step_id == 0)
        def _():
            dq_s[...] = jnp.zeros_like(dq_s)
            rs_s[...] = jnp.zeros_like(rs_s)

        @pl.when(qi == ki)
        def _():
            dk_s[...] = jnp.zeros_like(dk_s)
            dv_s[...] = jnp.zeros_like(dv_s)
            dck_ref[...] = jnp.zeros_like(dck_ref)

        rows = pl.ds(qi * t if isinstance(qi, int) else pl.multiple_of(qi * t, t), t)

        def step(diag):
            q2, k2, v2, do2 = q_ref[...], k_ref[...], v_ref[...], do_ref[...]
            dq = []
            for hh in range(2):
                p, ds = _fox_ds(q2, k2, v2, do2, aq_ref[...], ak_ref[...], ad_ref[...], hh, diag)
                dsb = ds.astype(BF)
                dv_s[...] += _dot_tn(p.astype(BF), _head_only(do2, hh))
                dk_s[...] += _dot_tn(dsb, _head_only(q2, hh))
                dq.append(_dot(dsb, k2))
                dck_ref[hh] = dck_ref[hh] - jnp.sum(ds, axis=0, keepdims=True)
                rs_s[hh, rows, :] = rs_s[hh, rows, :] + jnp.sum(ds, axis=1, keepdims=True)
            dq_s[rows, :] = dq_s[rows, :] + jnp.where(_first_half(), dq[0], dq[1])

        _diag_or_below(qi, ki, step)

        @pl.when(qi == nq - 1)
        def _():
            dk_ref[...] = dk_s[...].astype(BF)
            dv_ref[...] = dv_s[...].astype(BF)

        @pl.when(step_id == n_steps - 1)
        def _():
            dq_ref[...] = (dq_s[...] * FOX_SCALE).astype(BF)
            dcq_ref[...] = jnp.where(_first_half(), rs_s[0], rs_s[1])

    qs = pl.BlockSpec((t, 128), lambda p, s: (_tri_cols(s, nq)[1], p))
    ks = pl.BlockSpec((t, 128), lambda p, s: (_tri_cols(s, nq)[0], p))
    cks = pl.BlockSpec((2, 1, t), lambda p, s: (p, 0, _tri_cols(s, nq)[0]))
    seq = pl.BlockSpec((t_tok, 128), lambda p, s: (0, p))
    sds = jax.ShapeDtypeStruct((t_tok, FOX_WIDTH), BF)
    return _pcall(
        body, [fq, fk, fv, do, aqb, ak, ad], name="fox_bwd", grid=(npair, n_steps),
        out_shape=[sds, sds, sds, jax.ShapeDtypeStruct((FOX_HEADS, 1, t_tok), F32),
                   jax.ShapeDtypeStruct((t_tok, FOX_WIDTH), F32)],
        in_specs=[qs, ks, ks, qs, qs, ks, qs], out_specs=[seq, ks, ks, cks, seq],
        scratch=[pltpu.VMEM((t, 128), F32), pltpu.VMEM((t, 128), F32), pltpu.VMEM((t_tok, 128), F32),
                 pltpu.VMEM((2, t_tok, 128), F32)], comm=comm)


def _ret_consts():
    c = RET_CHUNK
    log_gamma = jnp.log1p(-jnp.exp2(-5.0 - jnp.arange(RET_HEADS, dtype=F32)))
    idx = jnp.arange(c, dtype=F32)
    diff = idx[:, None] - idx[None, :]
    dmask = jnp.where(diff >= 0, jnp.exp(log_gamma[:, None, None] * jnp.maximum(diff, 0.0)), 0.0)
    qdec = jnp.exp(log_gamma[:, None] * (idx + 1.0))
    kdec = jnp.exp(log_gamma[:, None] * (c - 1 - idx))
    cdec = jnp.exp(log_gamma * c)
    bc = lambda v: jnp.broadcast_to(v[:, :, None], (RET_HEADS, c, RET_DIM))
    return dmask, bc(qdec), bc(kdec), jnp.broadcast_to(cdec[:, None, None], (RET_HEADS, c, RET_DIM))


def _group_norm(y):
    mu = jnp.mean(y, axis=-1, keepdims=True)
    yc = y - mu
    r = lax.rsqrt(jnp.mean(yc * yc, axis=-1, keepdims=True) + EPS)
    return yc * r, r


def _ret_fwd(rq, rk, rv, rg, consts, comm=None):
    t_tok = rq.shape[0]
    nb = 4 if t_tok % (4 * RET_CHUNK) == 0 else 1
    tr = nb * RET_CHUNK
    n_steps = t_tok // tr
    c = RET_CHUNK

    def body(q_ref, k_ref, v_ref, g_ref, dm_ref, qd_ref, kd_ref, cd_ref, y_ref, yo_ref, st_ref, s_s):
        @pl.when(pl.program_id(1) == 0)
        def _():
            s_s[...] = jnp.zeros_like(s_s)

        dm, qd, kd, cd = dm_ref[...], qd_ref[...], kd_ref[...], cd_ref[...]
        for b in range(nb):
            rows = slice(b * c, (b + 1) * c)
            q, k, v = q_ref[rows, :], k_ref[rows, :], v_ref[rows, :]
            state = s_s[...]
            st_ref[b] = state
            sc = (_dot_nt(q, k) * dm).astype(BF)
            y = _dot(sc, v) + _dot((q.astype(F32) * qd).astype(BF), state.astype(BF))
            s_s[...] = cd * state + _dot_tn((k.astype(F32) * kd).astype(BF), v)
            y_ref[rows, :] = y
            yn, _ = _group_norm(y)
            gate = g_ref[rows, :].astype(F32)
            yo_ref[rows, :] = (yn * (gate * _sigmoid(gate))).astype(BF)

    blk = pl.BlockSpec((tr, RET_DIM), lambda h, i: (i, h))
    cst = pl.BlockSpec((None, c, RET_DIM), lambda h, i: (h, 0, 0))
    return _pcall(
        body, [rq, rk, rv, rg, *consts], name="ret_fwd", grid=(RET_HEADS, n_steps),
        out_shape=[jax.ShapeDtypeStruct((t_tok, RET_WIDTH), F32), jax.ShapeDtypeStruct((t_tok, RET_WIDTH), BF),
                   jax.ShapeDtypeStruct((RET_HEADS, t_tok // c, RET_DIM, RET_DIM), F32)],
        in_specs=[blk] * 4 + [cst] * 4,
        out_specs=[blk, blk, pl.BlockSpec((None, nb, RET_DIM, RET_DIM), lambda h, i: (h, i, 0, 0))],
        scratch=[pltpu.VMEM((RET_DIM, RET_DIM), F32)], comm=comm)


def _ret_bwd(rq, rk, rv, rg, y_raw, dyo, states, consts, cos_t, sin_t, comm=None):
    t_tok = rq.shape[0]
    nb = 4 if t_tok % (4 * RET_CHUNK) == 0 else 1
    tr = nb * RET_CHUNK
    n_steps = t_tok // tr
    c = RET_CHUNK

    def body(q_ref, k_ref, v_ref, g_ref, y_ref, dyo_ref, st_ref, dm_ref, qd_ref, kd_ref, cd_ref,
             cos_ref, sin_ref, dq_ref, dk_ref, dv_ref, dg_ref, ds_s):
        @pl.when(pl.program_id(1) == 0)
        def _():
            ds_s[...] = jnp.zeros_like(ds_s)

        dm, qd, kd, cd = dm_ref[...], qd_ref[...], kd_ref[...], cd_ref[...]
        for b in reversed(range(nb)):
            rows = slice(b * c, (b + 1) * c)
            q, k, v = q_ref[rows, :], k_ref[rows, :], v_ref[rows, :]
            cosv, sinv = cos_ref[rows, :], sin_ref[rows, :]
            yn, r = _group_norm(y_ref[rows, :])
            gate = g_ref[rows, :].astype(F32)
            sg = _sigmoid(gate)
            dyo = dyo_ref[rows, :]
            dg_ref[rows, :] = (dyo * yn * (sg * (1.0 + gate * (1.0 - sg)))).astype(BF)
            dyn = dyo * (gate * sg)
            dy = r * (dyn - jnp.mean(dyn, axis=-1, keepdims=True)
                      - yn * jnp.mean(dyn * yn, axis=-1, keepdims=True))
            dyb = dy.astype(BF)
            state_b = st_ref[b].astype(BF)
            dstate = ds_s[...]
            dstate_b = dstate.astype(BF)
            qdb = (q.astype(F32) * qd).astype(BF)
            kdb = (k.astype(F32) * kd).astype(BF)
            sc = (_dot_nt(q, k) * dm).astype(BF)
            dv = _dot_tn(sc, dyb) + _dot(kdb, dstate_b)
            dp = (_dot_nt(dyb, v) * dm).astype(BF)
            dq = _dot(dp, k) + _dot_nt(dyb, state_b) * qd
            dk = (_dot_tn(dp, q) + _dot_nt(v, dstate_b) * kd) * RET_SCALE
            ds_s[...] = cd * dstate + _dot_tn(qdb, dyb)
            dv_ref[rows, :] = dv.astype(BF)
            dq_ref[rows, :] = (dq * cosv - _swap_pairs(dq) * sinv).astype(BF)
            dk_ref[rows, :] = (dk * cosv - _swap_pairs(dk) * sinv).astype(BF)

    rev = lambda i: n_steps - 1 - i
    blk = pl.BlockSpec((tr, RET_DIM), lambda h, i: (rev(i), h))
    tab = pl.BlockSpec((tr, RET_DIM), lambda h, i: (rev(i), 0))
    cst = pl.BlockSpec((None, c, RET_DIM), lambda h, i: (h, 0, 0))
    sds = jax.ShapeDtypeStruct((t_tok, RET_WIDTH), BF)
    return _pcall(
        body, [rq, rk, rv, rg, y_raw, dyo, states, *consts, cos_t, sin_t], name="ret_bwd",
        grid=(RET_HEADS, n_steps), out_shape=[sds] * 4,
        in_specs=[blk] * 6 + [pl.BlockSpec((None, nb, RET_DIM, RET_DIM), lambda h, i: (h, rev(i), 0, 0))]
        + [cst] * 4 + [tab, tab],
        out_specs=[blk] * 4, scratch=[pltpu.VMEM((RET_DIM, RET_DIM), F32)], comm=comm)


def _mix_out(h, y_ret, y_fox, ga, gb, wr4, wf4, wo4, comm=None):
    t_tok, d = h.shape
    cz = wr4.shape[-1]
    ro = wo4.shape[-2]
    tm = _tile(t_tok, 512)

    def body(h_ref, yr_ref, yf_ref, ga_ref, gb_ref, wr_ref, wf_ref, wo_ref, ho_ref, za_ref, zb_ref, mix_ref):
        yr, yf = yr_ref[...], yf_ref[...]
        for j in range(N_CHIPS):
            sl = slice(j * cz, (j + 1) * cz)
            za = _dot(yr, wr_ref[j])
            zb = _dot(yf, wf_ref[j])
            za_ref[:, sl] = za.astype(BF)
            zb_ref[:, sl] = zb.astype(BF)
            mix_ref[:, sl] = (ga_ref[:, sl].astype(F32) * za + gb_ref[:, sl].astype(F32) * zb).astype(BF)
        acc = h_ref[...]
        for j in range(N_CHIPS):
            acc = acc + _dot(mix_ref[:, j * ro:(j + 1) * ro], wo_ref[j])
        ho_ref[...] = acc

    row = lambda c: pl.BlockSpec((tm, c), lambda i: (i, 0))
    full = lambda *s: pl.BlockSpec(s, lambda i: (0,) * len(s))
    sds = lambda dt: jax.ShapeDtypeStruct((t_tok, d), dt)
    return _pcall(
        body, [h, y_ret, y_fox, ga, gb, wr4, wf4, wo4], name="mix_out", grid=(t_tok // tm,),
        out_shape=[sds(F32), sds(BF), sds(BF), sds(BF)],
        in_specs=[row(d), row(RET_WIDTH), row(FOX_WIDTH), row(d), row(d),
                  full(N_CHIPS, RET_WIDTH, cz), full(N_CHIPS, FOX_WIDTH, cz), full(N_CHIPS, ro, d)],
        out_specs=[row(d)] * 4, comm=comm)


def _mix_out_bwd(dh, za, zb, ga, gb, y_fox, wr4, wf4, wo4, comm=None):
    t_tok, d = dh.shape
    cz = wr4.shape[-1]
    ro = wo4.shape[-2]
    tm = _tile(t_tok, 256)

    def body(dh_ref, za_ref, zb_ref, ga_ref, gb_ref, yf_ref, wr_ref, wf_ref, wo_ref,
             dhb_ref, dgp_ref, dza_ref, dzb_ref, dyr_ref, dyf_ref, dl_ref, db_ref):
        @pl.when(pl.program_id(0) == 0)
        def _():
            db_ref[...] = jnp.zeros_like(db_ref)

        dhb = dh_ref[...].astype(BF)
        dhb_ref[...] = dhb
        dyr = jnp.zeros((tm, RET_WIDTH), F32)
        dyf = jnp.zeros((tm, FOX_WIDTH), F32)
        for j in range(N_CHIPS):
            sl = slice(j * ro, (j + 1) * ro)
            dmix = _dot_nt(dhb, wo_ref[j])
            ga, gb = ga_ref[:, sl].astype(F32), gb_ref[:, sl].astype(F32)
            dza = (dmix * ga).astype(BF)
            dzb = (dmix * gb).astype(BF)
            dza_ref[:, sl] = dza
            dzb_ref[:, sl] = dzb
            dga = dmix * za_ref[:, sl].astype(F32) * ga * (1.0 - ga)
            dgb = dmix * zb_ref[:, sl].astype(F32) * gb * (1.0 - gb)
            dgp_ref[:, sl] = dga.astype(BF)
            dgp_ref[:, d + j * ro:d + (j + 1) * ro] = dgb.astype(BF)
            db_ref[:, sl] += jnp.sum(dga, axis=0, keepdims=True)
            db_ref[:, d + j * ro:d + (j + 1) * ro] += jnp.sum(dgb, axis=0, keepdims=True)
        for j in range(N_CHIPS):
            sl = slice(j * cz, (j + 1) * cz)
            dyr = dyr + _dot_nt(dza_ref[:, sl], wr_ref[j])
            dyf = dyf + _dot_nt(dzb_ref[:, sl], wf_ref[j])
        dyr_ref[...] = dyr
        dyfb = dyf.astype(BF)
        dyf_ref[...] = dyfb
        prod = dyfb.astype(F32) * yf_ref[...]
        first = _first_half()
        for pp in range(FOX_HEADS // 2):
            blk = prod[:, pp * 128:(pp + 1) * 128]
            s0 = jnp.sum(jnp.where(first, blk, 0.0), axis=1, keepdims=True)
            s1 = jnp.sum(jnp.where(first, 0.0, blk), axis=1, keepdims=True)
            parts = _split3(-jnp.where(first, s1, s0))
            dl_ref[:, pp * 128:(pp + 1) * 128] = _aug_put(jnp.zeros((tm, 128), BF), 0, parts)

    row = lambda c: pl.BlockSpec((tm, c), lambda i: (i, 0))
    full = lambda *s: pl.BlockSpec(s, lambda i: (0,) * len(s))
    sds = lambda c, dt: jax.ShapeDtypeStruct((t_tok, c), dt)
    return _pcall(
        body, [dh, za, zb, ga, gb, y_fox, wr4, wf4, wo4], name="mix_out_bwd", grid=(t_tok // tm,),
        out_shape=[sds(d, BF), sds(2 * d, BF), sds(d, BF), sds(d, BF), sds(RET_WIDTH, F32),
                   sds(FOX_WIDTH, BF), sds(FOX_WIDTH, BF), jax.ShapeDtypeStruct((1, 2 * d), F32)],
        in_specs=[row(d)] * 5 + [row(FOX_WIDTH), full(N_CHIPS, RET_WIDTH, cz), full(N_CHIPS, FOX_WIDTH, cz),
                                 full(N_CHIPS, ro, d)],
        out_specs=[row(d), row(2 * d), row(d), row(d), row(RET_WIDTH), row(FOX_WIDTH), row(FOX_WIDTH),
                   full(1, 2 * d)],
        comm=comm)


def _mix_in_bwd(dh, h, ln, parts, dff, dgpre, w_in, wm4, comm=None):
    t_tok, d = h.shape
    cm = wm4.shape[-1]
    tm = _tile(t_tok, 256)

    def body(dh_ref, h_ref, ln_ref, p0, p1, p2, p3, p4, p5, p6, dff_ref, dgp_ref, win_ref, wm_ref,
             dhi_ref, dln_ref, dproj_ref):
        @pl.when(pl.program_id(0) == 0)
        def _():
            dln_ref[...] = jnp.zeros_like(dln_ref)

        for k, pr in enumerate((p0, p1, p2, p3, p4, p5, p6)):
            dproj_ref[:, k * 512:(k + 1) * 512] = pr[...]
        dproj_ref[:, FF_COL:FF_COL + 128] = dff_ref[...]
        dproj_ref[:, FF_COL + 128:] = jnp.zeros((tm, IN_PAD - FF_COL - 128), BF)
        du = _dot(dproj_ref[...], win_ref[...])
        for j in range(N_CHIPS):
            du = du + _dot_nt(dgp_ref[:, j * cm:(j + 1) * cm], wm_ref[j])
        xv = h_ref[...]
        dx, dln = _rms_bwd(du, xv, _rstd(xv), ln_ref[...])
        dln_ref[...] += dln
        dhi_ref[...] = dh_ref[...] + dx

    row = lambda c: pl.BlockSpec((tm, c), lambda i: (i, 0))
    full = lambda *s: pl.BlockSpec(s, lambda i: (0,) * len(s))
    return _pcall(
        body, [dh, h, ln, *parts, dff, dgpre, w_in, wm4], name="mix_in_bwd", grid=(t_tok // tm,),
        out_shape=[jax.ShapeDtypeStruct((t_tok, d), F32), jax.ShapeDtypeStruct((1, d), F32),
                   jax.ShapeDtypeStruct((t_tok, IN_PAD), BF)],
        in_specs=[row(d), row(d), full(1, d)] + [row(512)] * 7 + [row(128), row(2 * d), full(IN_PAD, d),
                                                                   full(N_CHIPS, d, cm)],
        out_specs=[row(d), full(1, d), row(IN_PAD)], comm=comm)


def _tail(h, p, target, ln_ple, ln_fin, wpg4, wpl4, comm=None):
    t_tok, d = h.shape
    pd = p.shape[1]
    rg = wpg4.shape[-2]
    cp = wpl4.shape[-1]
    tm = _tile(t_tok, 256)

    def body(h_ref, p_ref, t_ref, lp_ref, lf_ref, wg_ref, wp_ref,
             dh_ref, n_ref, dgp_ref, dpe_ref, pb_ref, loss_ref, dlf_ref, dlp_ref, pe_s, dn_s):
        @pl.when(pl.program_id(0) == 0)
        def _():
            loss_ref[...] = jnp.zeros_like(loss_ref)
            dlf_ref[...] = jnp.zeros_like(dlf_ref)
            dlp_ref[...] = jnp.zeros_like(dlp_ref)

        xv = h_ref[...]
        r3 = _rstd(xv)
        nb = (xv * r3 * lp_ref[...]).astype(BF)
        n_ref[...] = nb
        pb = p_ref[...].astype(BF)
        pb_ref[...] = pb
        pgpre = jnp.zeros((tm, d), F32)
        for j in range(N_CHIPS):
            pgpre = pgpre + _dot(nb[:, j * rg:(j + 1) * rg], wg_ref[j])
            pe_s[:, j * cp:(j + 1) * cp] = _dot(pb, wp_ref[j])
        pg = _sigmoid(pgpre)
        pe = pe_s[...]
        h4 = xv + pg * pe
        r4 = _rstd(h4)
        err = h4 * r4 * lf_ref[...] - t_ref[...]
        loss_ref[...] += 0.5 * jnp.sum(jnp.sum(err * err, axis=1, keepdims=True), axis=0, keepdims=True) / d
        dh4, dlf = _rms_bwd(err * (1.0 / d), h4, r4, lf_ref[...])
        dlf_ref[...] += dlf
        dpe_ref[...] = (dh4 * pg).astype(BF)
        dgp = (dh4 * pe * pg * (1.0 - pg)).astype(BF)
        dgp_ref[...] = dgp
        for j in range(N_CHIPS):
            dn_s[:, j * rg:(j + 1) * rg] = _dot_nt(dgp, wg_ref[j])
        dx, dlp = _rms_bwd(dn_s[...], xv, r3, lp_ref[...])
        dlp_ref[...] += dlp
        dh_ref[...] = dh4 + dx

    row = lambda c: pl.BlockSpec((tm, c), lambda i: (i, 0))
    full = lambda *s: pl.BlockSpec(s, lambda i: (0,) * len(s))
    sds = lambda c, dt: jax.ShapeDtypeStruct((t_tok, c), dt)
    vec = jax.ShapeDtypeStruct((1, d), F32)
    return _pcall(
        body, [h, p, target, ln_ple, ln_fin, wpg4, wpl4], name="tail", grid=(t_tok // tm,),
        out_shape=[sds(d, F32), sds(d, BF), sds(d, BF), sds(d, BF), sds(pd, BF),
                   jax.ShapeDtypeStruct((1, 128), F32), vec, vec],
        in_specs=[row(d), row(pd), row(d), full(1, d), full(1, d), full(N_CHIPS, rg, d), full(N_CHIPS, pd, cp)],
        out_specs=[row(d), row(d), row(d), row(d), row(pd), full(1, 128), full(1, d), full(1, d)],
        scratch=[pltpu.VMEM((tm, d), F32), pltpu.VMEM((tm, d), F32)], comm=comm)


BIG = ["w_ffn1_gate", "w_ffn1_up", "w_ffn1_down", "w_in", "w_merge", "w_ret_out", "w_fox_out", "w_out",
       "w_ffn2_gate", "w_ffn2_up", "w_ffn2_down", "w_ple", "w_ple_gate"]
SMALL = ["ln_ffn1", "ln_mix", "b_forget", "b_merge", "ln_ffn2", "ln_ple", "ln_final"]
WEIGHTS = ["ln_ffn1", "w_ffn1_gate", "w_ffn1_up", "w_ffn1_down", "ln_mix", "w_in", "b_forget", "w_merge", "b_merge",
           "w_ret_out", "w_fox_out", "w_out", "ln_ffn2", "w_ffn2_gate", "w_ffn2_up", "w_ffn2_down", "ln_ple",
           "w_ple", "w_ple_gate", "ln_final"]


TRANSPOSED = {"w_ffn1_gate", "w_ffn1_up", "w_ffn2_gate", "w_ffn2_up", "w_in"}
IN_ROWS_PAD = -(-(IN_COLS // N_CHIPS) // 32) * 32


def _pack_small(vals, loss_row):
    rows = [loss_row]
    for name in SMALL:
        v = vals[name].reshape(-1)
        n = -(-v.shape[0] // 128) * 128
        rows.append(jnp.pad(v, (0, n - v.shape[0])).reshape(n // 128, 128))
    packed = jnp.concatenate(rows, axis=0)
    pad = -packed.shape[0] % 8
    return jnp.pad(packed, ((0, pad), (0, 0)))


def _unpack_small(packed, sizes):
    out, r = {}, 1
    for name in SMALL:
        n = sizes[name]
        nr = -(-n // 128)
        out[name] = packed[r:r + nr].reshape(1, nr * 128)[:, :n]
        r += nr
    return out


class _Stage:
    def __init__(self, comm, finish):
        self.comm, self.finish, self.result = comm, finish, None


def _hosted(fn, *a, stages=()):
    if not stages:
        return fn(*a)
    outs, couts = fn(*a, comm=_merge([st.comm for st in stages]))
    for st, o in zip(stages, _split_outs([st.comm for st in stages], couts)):
        st.result = st.finish(o)
    return outs


class _Reducer:
    def __init__(self):
        self.done = {}

    def swap(self, grads):
        names = list(grads)
        return _Stage(_c_half_swap([grads[n] for n in names]),
                      lambda outs: {n: _add_halves(grads[n], o) for n, o in zip(names, outs)})

    def exchange(self, parts):
        names = list(parts)
        return _Stage(_c_chip_exchange([parts[n] for n in names]),
                      lambda outs: {n: _sum_chips(o) for n, o in zip(names, outs)})

    def join(self, halves):
        names = list(halves)
        return _Stage(_c_join([halves[n] for n in names]), lambda outs: self.done.update(zip(names, outs)))


def kernel(x, p, positions, ln_ffn1, w_ffn1_gate, w_ffn1_up, w_ffn1_down, ln_mix, w_in, b_forget, w_merge, b_merge, w_ret_out, w_fox_out, w_out, ln_ffn2, w_ffn2_gate, w_ffn2_up, w_ffn2_down, ln_ple, w_ple, w_ple_gate, ln_final, loss_target, m_ln_ffn1, m_w_ffn1_gate, m_w_ffn1_up, m_w_ffn1_down, m_ln_mix, m_w_in, m_b_forget, m_w_merge, m_b_merge, m_w_ret_out, m_w_fox_out, m_w_out, m_ln_ffn2, m_w_ffn2_gate, m_w_ffn2_up, m_w_ffn2_down, m_ln_ple, m_w_ple, m_w_ple_gate, m_ln_final, v_ln_ffn1, v_w_ffn1_gate, v_w_ffn1_up, v_w_ffn1_down, v_ln_mix, v_w_in, v_b_forget, v_w_merge, v_b_merge, v_w_ret_out, v_w_fox_out, v_w_out, v_ln_ffn2, v_w_ffn2_gate, v_w_ffn2_up, v_w_ffn2_down, v_ln_ple, v_w_ple, v_w_ple_gate, v_ln_final):
    args = dict(locals())
    w = {n: args[n] for n in WEIGHTS}
    m = {n: args["m_" + n] for n in WEIGHTS}
    v = {n: args["v_" + n] for n in WEIGHTS}
    d = x.shape[-1]
    t_tok = x.shape[1]
    xs, ps, target = x[0], p[0, 0], loss_target[0]
    small = {n: w[n].reshape(1, -1) for n in SMALL}

    def to2d(n, a):
        if n in TRANSPOSED:
            return a[0].T
        return a.reshape(a.shape[-2], a.shape[-1]) if a.ndim == 3 else a.reshape(1, -1)

    def from2d(n, a):
        return a.T[None] if n in TRANSPOSED else a.reshape(w[n].shape)

    def padded(n, a):
        return jnp.pad(a, ((0, IN_ROWS_PAD - a.shape[0]), (0, 0))) if n == "w_in" else a

    shard = {}
    for n in BIG:
        s2 = padded(n, to2d(n, w[n]).astype(BF))
        shard[n] = s2.reshape(2, s2.shape[0] // 2, s2.shape[1])
    full = {}

    def gather(names):
        def finish(outs):
            full.update({n: o.reshape(N_CHIPS, 2 * o.shape[2], o.shape[3]) for n, o in zip(names, outs)})

        return _Stage(_c_all_gather([shard[n] for n in names]), finish)

    half = RET_DIM // 2
    inv_freq = 1.0 / (ROPE_BASE ** (jnp.arange(half, dtype=F32) / half))
    cos_t, sin_t = _hosted(_rope_tables, positions[0].astype(F32).reshape(t_tok, 1),
                           jnp.repeat(inv_freq, 2).reshape(1, RET_DIM),
                           stages=[gather(["w_ffn1_gate", "w_ffn1_up", "w_ffn1_down"])])
    consts = _ret_consts()
    b_pad = jnp.pad(small["b_forget"], ((0, 0), (0, 128 - FOX_HEADS)))

    h1, n1, g1, u1 = _hosted(_ffn_fwd, xs, small["ln_ffn1"], full["w_ffn1_gate"], full["w_ffn1_up"],
                             full["w_ffn1_down"], stages=[gather(["w_in", "w_merge"])])
    w_in_full = jnp.pad(full["w_in"][:, :IN_COLS // N_CHIPS].reshape(IN_COLS, d), ((0, IN_PAD - IN_COLS), (0, 0)))
    u, rq, rk, rv, rg, fq, fk, fv, ffl, ga, gb = _hosted(
        _mix_in, h1, small["ln_mix"], w_in_full, full["w_merge"], small["b_merge"], cos_t, sin_t,
        stages=[gather(["w_ret_out", "w_fox_out", "w_out", "w_ple_gate", "w_ple"])])
    aq, ak = _forget_fwd(ffl, b_pad)
    y_raw, y_ret, states = _ret_fwd(rq, rk, rv, rg, consts)
    y_fox, y_fox32, lse_e = _hosted(_fox_fwd, fq, fk, fv, aq, ak,
                                    stages=[gather(["w_ffn2_gate", "w_ffn2_up", "w_ffn2_down"])])
    aqb = _fox_aug_lse(aq, lse_e)
    h2, za, zb, mix = _mix_out(h1, y_ret, y_fox, ga, gb, full["w_ret_out"], full["w_fox_out"], full["w_out"])
    h3, n2, g2, u2 = _ffn_fwd(h2, small["ln_ffn2"], full["w_ffn2_gate"], full["w_ffn2_up"], full["w_ffn2_down"])

    red = _Reducer()
    dh3, n3, dpgpre, dpe, pb, loss, dln_final, dln_ple = _tail(
        h3, ps, target, small["ln_ple"], small["ln_final"], full["w_ple_gate"], full["w_ple"])
    g_ple = dict(w_ple_gate=_wgrad_rows("wgrad_ple_gate", n3, dpgpre, N_CHIPS),
                 w_ple=_wgrad_cols("wgrad_ple", pb, dpe, N_CHIPS))

    sw_ple = red.swap(g_ple)
    dh2, dln_ffn2, dg2, du2, a2, dhb3 = _hosted(
        _ffn_bwd, dh3, h2, small["ln_ffn2"], g2, u2, full["w_ffn2_gate"], full["w_ffn2_up"], full["w_ffn2_down"],
        stages=[sw_ple])
    ex_ple = red.exchange(sw_ple.result)
    g_f2 = dict(w_ffn2_gate=_hosted(_wgrad_b_shared, "wgrad_ffn2_gate", dg2, n2, stages=[ex_ple]))
    g_f2["w_ffn2_up"] = _wgrad_b_shared("wgrad_ffn2_up", du2, n2)
    g_f2["w_ffn2_down"] = _wgrad_b_shared("wgrad_ffn2_down", a2, dhb3)

    sw_f2 = red.swap(g_f2)
    dhb2, dgpre, dza, dzb, dy_ret, dy_fox, ad, db_merge = _hosted(
        _mix_out_bwd, dh2, za, zb, ga, gb, y_fox32, full["w_ret_out"], full["w_fox_out"], full["w_out"],
        stages=[sw_f2, red.join(ex_ple.result)])
    g_br = dict(w_out=_wgrad_rows("wgrad_out", mix, dhb2, N_CHIPS),
                w_ret_out=_wgrad_cols("wgrad_ret_out", y_ret, dza, N_CHIPS),
                w_fox_out=_wgrad_cols("wgrad_fox_out", y_fox, dzb, N_CHIPS))

    sw_br = red.swap(g_br)
    drq, drk, drv, drg = _hosted(_ret_bwd, rq, rk, rv, rg, y_raw, dy_ret, states, consts, cos_t, sin_t,
                                 stages=[sw_br])
    ex_f2, ex_br = red.exchange(sw_f2.result), red.exchange(sw_br.result)
    dfq, dfk, dfv, dcum_t3, dcum_q = _hosted(_fox_bwd, fq, fk, fv, dy_fox, aqb, ak, ad, stages=[ex_f2, ex_br])
    dff, db_forget = _forget_bwd(dcum_t3.reshape(FOX_HEADS, t_tok), dcum_q, ffl, b_pad)
    dh1, dln_mix, dproj = _hosted(
        _mix_in_bwd, dh2, h1, small["ln_mix"], (drq, drk, drv, drg, dfq, dfk, dfv), dff, dgpre, w_in_full,
        full["w_merge"], stages=[red.join(ex_f2.result), red.join(ex_br.result)])

    g_in = _wgrad_rows("wgrad_in", dproj, u, IN_PAD // 512)
    g_in = g_in.reshape(IN_PAD, d)[:IN_COLS].reshape(N_CHIPS, IN_COLS // N_CHIPS, d)
    g_in = jnp.pad(g_in, ((0, 0), (0, IN_ROWS_PAD - IN_COLS // N_CHIPS), (0, 0)))
    sw_in = red.swap(dict(w_in=g_in))
    g_mrg = _hosted(_wgrad_cols, "wgrad_merge", u, dgpre, N_CHIPS, stages=[sw_in])

    sw_mrg, ex_in = red.swap(dict(w_merge=g_mrg)), red.exchange(sw_in.result)
    dx, dln_ffn1, dg1, du1, a1, dhb1 = _hosted(
        _ffn_bwd, dh1, xs, small["ln_ffn1"], g1, u1, full["w_ffn1_gate"], full["w_ffn1_up"], full["w_ffn1_down"],
        stages=[sw_mrg, ex_in])

    ex_mrg = red.exchange(sw_mrg.result)
    g_f1g = _hosted(_wgrad_b_shared, "wgrad_ffn1_gate", dg1, n1, stages=[ex_mrg, red.join(ex_in.result)])
    sw_f1g = red.swap(dict(w_ffn1_gate=g_f1g))
    g_f1u = _hosted(_wgrad_b_shared, "wgrad_ffn1_up", du1, n1, stages=[sw_f1g])
    ex_f1g, sw_f1u = red.exchange(sw_f1g.result), red.swap(dict(w_ffn1_up=g_f1u))
    g_f1d = _hosted(_wgrad_b_shared, "wgrad_ffn1_down", a1, dhb1,
                    stages=[ex_f1g, sw_f1u, red.join(ex_mrg.result)])

    small_grads = dict(ln_ffn1=dln_ffn1, ln_mix=dln_mix, b_forget=db_forget[:, :FOX_HEADS], b_merge=db_merge,
                       ln_ffn2=dln_ffn2, ln_ple=dln_ple, ln_final=dln_final)
    sizes = {n: w[n].size for n in SMALL}
    reduced = _all_reduce_small(_pack_small(small_grads, loss))
    gsum = _unpack_small(reduced, sizes)
    loss = reduced[0, 0]

    results = {}

    def update(n, stages=()):
        w2 = to2d(n, w[n])
        if n in gsum:
            g2 = gsum[n]
        else:
            g2 = red.done[n].reshape(-1, w2.shape[1])[:w2.shape[0]]
        dl, nm, nv = _hosted(_adamw, w2, g2, to2d(n, m[n]), to2d(n, v[n]), stages=stages)
        results[n] = tuple(from2d(n, a) for a in (g2, dl, nm, nv))

    ex_f1u, sw_f1d = red.exchange(sw_f1u.result), red.swap(dict(w_ffn1_down=g_f1d))
    update("w_ffn2_gate", stages=[ex_f1u, sw_f1d, red.join(ex_f1g.result)])
    ex_f1d = red.exchange(sw_f1d.result)
    update("w_ffn2_up", stages=[ex_f1d, red.join(ex_f1u.result)])
    update("w_ffn2_down", stages=[red.join(ex_f1d.result)])
    for n in WEIGHTS:
        if n not in results:
            update(n)

    outs = [[results[n][k] for n in WEIGHTS] for k in range(4)]
    return (loss, dx[None], *outs[0], *outs[1], *outs[2], *outs[3])
```

```python
import functools
import operator

import jax
import jax.numpy as jnp
from jax import lax
from jax.experimental import pallas as pl
from jax.experimental.pallas import tpu as pltpu

F32 = jnp.float32
BF = jnp.bfloat16
MESH = pl.DeviceIdType.MESH

EPS = 1e-6
ROPE_BASE = 10000.0
N_CHIPS = 4
RET_HEADS = 4
RET_DIM = 128
RET_WIDTH = RET_HEADS * RET_DIM
RET_CHUNK = 128
RET_SCALE = RET_DIM ** -0.5
FOX_HEADS = 8
FOX_DIM = 64
FOX_WIDTH = FOX_HEADS * FOX_DIM
FOX_SCALE = FOX_DIM ** -0.5
IN_COLS = 4 * RET_WIDTH + 3 * FOX_WIDTH + FOX_HEADS
IN_PAD = 4096
FF_COL = 4 * RET_WIDTH + 3 * FOX_WIDTH
NEG = -1e30

ADAM_LR = 0.001
ADAM_B1 = 0.9
ADAM_B2 = 0.999
ADAM_EPS = 1e-08
ADAM_WD = 0.01
ADAM_STEP = 10

VMEM_LIMIT = 52 * 1024 * 1024

NT = (((1,), (1,)), ((), ()))
TN = (((0,), (0,)), ((), ()))

HBM_SPEC = pl.BlockSpec(memory_space=pltpu.HBM)
VMEM_SPEC = pl.BlockSpec(memory_space=pltpu.VMEM)


def _dot(a, b):
    return jnp.dot(a, b, preferred_element_type=F32)


def _dot_nt(a, b):
    return lax.dot_general(a, b, NT, preferred_element_type=F32)


def _dot_tn(a, b):
    return lax.dot_general(a, b, TN, preferred_element_type=F32)


def _rstd(xv):
    return lax.rsqrt(jnp.mean(xv * xv, axis=-1, keepdims=True) + EPS)


def _rms_bwd(dn, xv, r, ln):
    xh = xv * r
    dxh = dn * ln
    dx = r * (dxh - xh * jnp.mean(dxh * xh, axis=-1, keepdims=True))
    return dx, jnp.sum(dn * xh, axis=0, keepdims=True)


def _sigmoid(x):
    return jax.nn.sigmoid(x)


def _tile(n, pref):
    return pref if n % pref == 0 else n


def _row_tile(n, cap):
    best = [t for t in range(16, min(n, cap) + 1, 16) if n % t == 0]
    return best[-1] if best else n


class _Comm:
    def __init__(self, ins, out_shapes, sems, start, wait, aliases=None):
        self.ins, self.out_shapes, self.sems, self.start, self.wait = list(ins), list(out_shapes), list(sems), start, wait
        self.aliases = dict(aliases or {})


def _merge(comms):
    comms = [c for c in comms if c is not None]
    if not comms:
        return None
    bounds, ni, no, ns = [], 0, 0, 0
    for c in comms:
        bounds.append((ni, no, ns))
        ni, no, ns = ni + len(c.ins), no + len(c.out_shapes), ns + len(c.sems)

    def run(which):
        def f(ins, outs, sems):
            for c, (i, o, s) in zip(comms, bounds):
                getattr(c, which)(ins[i:i + len(c.ins)], outs[o:o + len(c.out_shapes)], sems[s:s + len(c.sems)])
        return f

    aliases = {i + a: o + b for c, (i, o, _) in zip(comms, bounds) for a, b in c.aliases.items()}
    return _Comm([a for c in comms for a in c.ins], [a for c in comms for a in c.out_shapes],
                 [a for c in comms for a in c.sems], run("start"), run("wait"), aliases)


def _split_outs(comms, outs):
    res, o = [], 0
    for c in comms:
        if c is not None:
            res.append(list(outs[o:o + len(c.out_shapes)]))
            o += len(c.out_shapes)
    return res


def _pcall(body, args, *, name, out_shape, grid=(), in_specs=None, out_specs=None, scratch=(), comm=None,
           prefetch=()):
    many = isinstance(out_shape, (list, tuple))
    outs = list(out_shape) if many else [out_shape]
    n_pre, n_in, n_out, n_scr = len(prefetch), len(args), len(outs), len(scratch)
    if in_specs is None:
        in_specs, out_specs = [VMEM_SPEC] * n_in, [VMEM_SPEC] * n_out
    else:
        in_specs, out_specs = list(in_specs), (list(out_specs) if many else [out_specs])
    params = pltpu.CompilerParams(dimension_semantics=("arbitrary",) * len(grid), vmem_limit_bytes=VMEM_LIMIT)
    scalars = [jnp.reshape(s, (1,)).astype(jnp.int32) for s in prefetch]
    ci, co = (len(comm.ins), len(comm.out_shapes)) if comm is not None else (0, 0)

    def wrapped(*refs):
        pre, refs = refs[:n_pre], refs[n_pre:]
        a, ca = refs[:n_in], refs[n_in:n_in + ci]
        o = refs[n_in + ci:n_in + ci + n_out]
        cout = refs[n_in + ci + n_out:n_in + ci + n_out + co]
        s = refs[n_in + ci + n_out + co:n_in + ci + n_out + co + n_scr]
        csem = refs[n_in + ci + n_out + co + n_scr:]
        if comm is None:
            body(*pre, *a, *o, *s)
        elif grid:
            first = functools.reduce(operator.and_, [pl.program_id(k) == 0 for k in range(len(grid))])
            last = functools.reduce(operator.and_, [pl.program_id(k) == grid[k] - 1 for k in range(len(grid))])
            pl.when(first)(lambda: comm.start(ca, cout, csem))
            body(*pre, *a, *o, *s)
            pl.when(last)(lambda: comm.wait(ca, cout, csem))
        else:
            comm.start(ca, cout, csem)
            body(*pre, *a, *o, *s)
            comm.wait(ca, cout, csem)

    c_ins, c_outs, c_sems, aliases = ([], [], [], {}) if comm is None else (
        comm.ins, comm.out_shapes, comm.sems, {n_pre + n_in + i: n_out + o for i, o in comm.aliases.items()})
    all_in, all_out = in_specs + [HBM_SPEC] * ci, out_specs + [HBM_SPEC] * co
    all_scr = list(scratch) + c_sems
    if n_pre:
        spec = dict(grid_spec=pltpu.PrefetchScalarGridSpec(
            num_scalar_prefetch=n_pre, grid=grid, in_specs=all_in, out_specs=all_out, scratch_shapes=all_scr))
    else:
        spec = dict(grid=grid, in_specs=all_in, out_specs=all_out, scratch_shapes=all_scr)
    res = pl.pallas_call(wrapped, name=name, out_shape=outs + c_outs, input_output_aliases=aliases,
                         compiler_params=params, **spec)(*scalars, *args, *c_ins)
    mine = list(res[:n_out])
    mine = mine if many else mine[0]
    return mine if comm is None else (mine, list(res[n_out:]))


def _peer_chips(x, y):
    return [(1 - x, y), (x, 1 - y), (1 - x, 1 - y)]


def _c_all_gather(bufs):
    n = len(bufs)

    def copies(ins, outs, sems):
        send_sems, recv_sems, fwd_send, fwd_recv = sems
        x, y, c = lax.axis_index("x"), lax.axis_index("y"), lax.axis_index("c")
        me = 2 * x + y
        peers = _peer_chips(x, y)
        chip = [2 * px + py for px, py in peers]

        def ici(g, j, slot):
            return pltpu.make_async_remote_copy(
                src_ref=outs[g].at[me, c], dst_ref=outs[g].at[slot, c], send_sem=send_sems.at[g, j],
                recv_sem=recv_sems.at[g, j], device_id=(*peers[j], c), device_id_type=MESH)

        def d2d(g, j, half):
            return pltpu.make_async_remote_copy(
                src_ref=outs[g].at[chip[j], half], dst_ref=outs[g].at[chip[j], half], send_sem=fwd_send.at[g, j],
                recv_sem=fwd_recv.at[g, j], device_id=(x, y, 1 - c), device_id_type=MESH)

        pairs = [(g, j) for g in range(n) for j in range(3)]
        sends = [ici(g, j, me) for g, j in pairs]
        recvs = [ici(g, j, chip[j]) for g, j in pairs]
        passes = [d2d(g, j, c) for g, j in pairs]
        passed = [d2d(g, j, 1 - c) for g, j in pairs]
        return sends, recvs, passes, passed

    def start(ins, outs, sems):
        for cp in copies(ins, outs, sems)[0]:
            cp.start()

    def wait(ins, outs, sems):
        sends, recvs, passes, passed = copies(ins, outs, sems)
        for rcv, fwd in zip(recvs, passes):
            rcv.wait_recv()
            fwd.start()
        for cp in passed:
            cp.wait_recv()
        for cp in sends + passes:
            cp.wait_send()

    pair_sems = pltpu.SemaphoreType.DMA((n, 3))
    return _Comm(bufs, [jax.ShapeDtypeStruct(s.shape, s.dtype) for s in bufs], [pair_sems] * 4, start, wait,
                 aliases={g: g for g in range(n)})


def _start_wait(copies):
    def start(ins, outs, sems):
        local, sends, _ = copies(ins, outs, sems)
        for cp in local + sends:
            cp.start()

    def wait(ins, outs, sems):
        local, sends, recvs = copies(ins, outs, sems)
        for cp in recvs:
            cp.wait_recv()
        for cp in sends:
            cp.wait_send()
        for cp in local:
            cp.wait()

    return start, wait


def _c_half_swap(grads):
    n = len(grads)

    def copies(ins, outs, sems):
        send_sems, recv_sems = sems
        x, y, c = lax.axis_index("x"), lax.axis_index("y"), lax.axis_index("c")
        sends = []
        for g in range(n):
            half = ins[g].shape[1] // 2
            sends.append(pltpu.make_async_remote_copy(
                src_ref=ins[g].at[:, pl.ds((1 - c) * half, half), :], dst_ref=outs[g],
                send_sem=send_sems.at[g], recv_sem=recv_sems.at[g], device_id=(x, y, 1 - c), device_id_type=MESH))
        return [], sends, sends

    return _Comm(
        grads, [jax.ShapeDtypeStruct((N_CHIPS, s.shape[1] // 2, s.shape[2]), s.dtype) for s in grads],
        [pltpu.SemaphoreType.DMA((n,)), pltpu.SemaphoreType.DMA((n,))], *_start_wait(copies))


def _c_chip_exchange(parts):
    n = len(parts)

    def copies(ins, outs, sems):
        send_sems, recv_sems = sems
        x, y, c = lax.axis_index("x"), lax.axis_index("y"), lax.axis_index("c")
        peers = _peer_chips(x, y)

        def remote(g, j):
            return pltpu.make_async_remote_copy(
                src_ref=ins[g].at[2 * peers[j][0] + peers[j][1]], dst_ref=outs[g].at[j],
                send_sem=send_sems.at[g, j], recv_sem=recv_sems.at[g, j], device_id=(*peers[j], c),
                device_id_type=MESH)

        sends = [remote(g, j) for g in range(n) for j in range(3)]
        return [], sends, sends

    return _Comm(
        parts, [jax.ShapeDtypeStruct((3,) + s.shape[1:], s.dtype) for s in parts],
        [pltpu.SemaphoreType.DMA((n, 3)), pltpu.SemaphoreType.DMA((n, 3))], *_start_wait(copies))


def _c_join(halves):
    n = len(halves)

    def copies(ins, outs, sems):
        send_sems, recv_sems = sems
        x, y, c = lax.axis_index("x"), lax.axis_index("y"), lax.axis_index("c")
        sends = [pltpu.make_async_remote_copy(
            src_ref=ins[g], dst_ref=outs[g], send_sem=send_sems.at[g], recv_sem=recv_sems.at[g],
            device_id=(x, y, 1 - c), device_id_type=MESH) for g in range(n)]
        return [], sends, sends

    return _Comm(
        halves, [jax.ShapeDtypeStruct(s.shape, s.dtype) for s in halves],
        [pltpu.SemaphoreType.DMA((n,)), pltpu.SemaphoreType.DMA((n,))], *_start_wait(copies))


def _all_reduce_small(v):
    rows = v.shape[0]

    def body(v_ref, out_ref, buf, send_sems, recv_sems):
        x, y, c = lax.axis_index("x"), lax.axis_index("y"), lax.axis_index("c")
        me = 4 * x + 2 * y + c
        buf[me] = v_ref[...]
        flips = [(fx, fy, fc) for fx in (0, 1) for fy in (0, 1) for fc in (0, 1)][1:]

        def peer(k):
            fx, fy, fc = flips[k]
            px, py, pc = x ^ fx, y ^ fy, c ^ fc
            return (px, py, pc), 4 * px + 2 * py + pc

        def copy(k, slot):
            return pltpu.make_async_remote_copy(
                src_ref=buf.at[slot], dst_ref=buf.at[slot], send_sem=send_sems.at[k],
                recv_sem=recv_sems.at[k], device_id=peer(k)[0], device_id_type=MESH)

        sends = [copy(k, me) for k in range(7)]
        for cp in sends:
            cp.start()
        for k in range(7):
            copy(k, peer(k)[1]).wait_recv()
        for cp in sends:
            cp.wait_send()
        acc = buf[0]
        for d in range(1, 8):
            acc = acc + buf[d]
        out_ref[...] = acc

    return _pcall(body, [v], name="all_reduce_small", out_shape=jax.ShapeDtypeStruct((rows, 128), F32),
                  scratch=[pltpu.VMEM((8, rows, 128), F32), pltpu.SemaphoreType.DMA((7,)),
                           pltpu.SemaphoreType.DMA((7,))])


def _add_halves(g, got):
    _, h, c = got.shape
    th = _row_tile(h, 512)
    nh = h // th
    half = lax.axis_index("c") * nh

    def body(h_ref, a_ref, b_ref, o_ref):
        o_ref[...] = (a_ref[...].astype(F32) + b_ref[...].astype(F32)).astype(o_ref.dtype)

    spec = pl.BlockSpec((1, th, c), lambda j, i, h_ref: (j, i, 0))
    mine = pl.BlockSpec((1, th, c), lambda j, i, h_ref: (j, h_ref[0] + i, 0))
    return _pcall(body, [g, got], name="add_halves", grid=(N_CHIPS, nh), prefetch=[half],
                  out_shape=jax.ShapeDtypeStruct(got.shape, BF), in_specs=[mine, spec], out_specs=spec)


def _sum_chips(parts, recv):
    _, h, c = parts.shape
    th = _row_tile(h, 512)
    me = 2 * lax.axis_index("x") + lax.axis_index("y")

    def body(me_ref, p_ref, r_ref, o_ref):
        acc = p_ref[0].astype(F32)
        for s in range(N_CHIPS - 1):
            acc = acc + r_ref[s].astype(F32)
        o_ref[...] = acc

    return _pcall(body, [parts, recv], name="sum_chips", grid=(h // th,), prefetch=[me],
                  out_shape=jax.ShapeDtypeStruct((h, c), F32),
                  in_specs=[pl.BlockSpec((1, th, c), lambda i, me_ref: (me_ref[0], i, 0)),
                            pl.BlockSpec((N_CHIPS - 1, th, c), lambda i, me_ref: (0, i, 0))],
                  out_specs=pl.BlockSpec((th, c), lambda i, me_ref: (i, 0)))


def _adam_update(w, gv, m, v, d_ref, nm_ref, nv_ref):
    c1 = 1.0 / (1.0 - ADAM_B1 ** ADAM_STEP)
    c2 = 1.0 / (1.0 - ADAM_B2 ** ADAM_STEP)
    nm = ADAM_B1 * m + (1.0 - ADAM_B1) * gv
    nv = ADAM_B2 * v + (1.0 - ADAM_B2) * (gv * gv)
    nm_ref[...] = nm
    nv_ref[...] = nv
    d_ref[...] = -ADAM_LR * ((nm * c1) / (jnp.sqrt(nv * c2) + ADAM_EPS) + ADAM_WD * w)


def _adamw(w, g, m, v, comm=None):
    r, c = w.shape
    tr = _row_tile(r, 512)

    def body(w_ref, g_ref, m_ref, v_ref, d_ref, nm_ref, nv_ref):
        _adam_update(w_ref[...], g_ref[...], m_ref[...], v_ref[...], d_ref, nm_ref, nv_ref)

    spec = pl.BlockSpec((tr, c), lambda i: (i, 0))
    sds = jax.ShapeDtypeStruct((r, c), F32)
    return _pcall(body, [w, g, m, v], name="adamw", grid=(r // tr,), out_shape=[sds, sds, sds],
                  in_specs=[spec] * 4, out_specs=[spec] * 3, comm=comm)


def _adamw_halves(w, g_mine, g_other, m, v, comm=None):
    r, c = w.shape
    h = r // 2
    tr = _row_tile(h, 512)
    nb = h // tr
    core = lax.axis_index("c")

    def body(c_ref, w_ref, gm_ref, go_ref, m_ref, v_ref, g_ref, d_ref, nm_ref, nv_ref):
        gv = jnp.where(pl.program_id(0) == c_ref[0], gm_ref[...], go_ref[...])
        g_ref[...] = gv
        _adam_update(w_ref[...], gv, m_ref[...], v_ref[...], d_ref, nm_ref, nv_ref)

    full = pl.BlockSpec((tr, c), lambda hh, i, c_ref: (hh * nb + i, 0))
    half = pl.BlockSpec((tr, c), lambda hh, i, c_ref: (i, 0))
    sds = jax.ShapeDtypeStruct((r, c), F32)
    return _pcall(body, [w, g_mine, g_other, m, v], name="adamw_halves", grid=(2, nb), prefetch=[core],
                  out_shape=[sds] * 4, in_specs=[full, half, half, full, full], out_specs=[full] * 4, comm=comm)


def _wgrad(name, a, b, a_spec, b_spec, m, n, nb, comm):
    def body(a_ref, b_ref, o_ref):
        o_ref[...] = _dot_tn(a_ref[...], b_ref[...]).astype(o_ref.dtype)

    return _pcall(body, [a, b], name=name, grid=(nb,), out_shape=jax.ShapeDtypeStruct((nb, m, n), BF),
                  in_specs=[a_spec, b_spec], out_specs=pl.BlockSpec((None, m, n), lambda j: (j, 0, 0)), comm=comm)


def _wgrad_cols(name, a, b, nb, comm=None):
    t_tok, m = a.shape
    n = b.shape[1] // nb
    return _wgrad(name, a, b, pl.BlockSpec((t_tok, m), lambda j: (0, 0)), pl.BlockSpec((t_tok, n), lambda j: (0, j)),
                  m, n, nb, comm)


def _wgrad_rows(name, a, b, nb, comm=None):
    t_tok, n = b.shape
    m = a.shape[1] // nb
    return _wgrad(name, a, b, pl.BlockSpec((t_tok, m), lambda j: (0, j)), pl.BlockSpec((t_tok, n), lambda j: (0, 0)),
                  m, n, nb, comm)


def _wgrad_a_shared(name, a, b4, comm=None):
    t_tok, m = a.shape
    nb, _, n = b4.shape
    return _wgrad(name, a, b4, pl.BlockSpec((t_tok, m), lambda j: (0, 0)),
                  pl.BlockSpec((None, t_tok, n), lambda j: (j, 0, 0)), m, n, nb, comm)


def _wgrad_b_shared(name, a4, b, comm=None):
    nb, t_tok, m = a4.shape
    n = b.shape[1]
    return _wgrad(name, a4, b, pl.BlockSpec((None, t_tok, m), lambda j: (j, 0, 0)),
                  pl.BlockSpec((t_tok, n), lambda j: (0, 0)), m, n, nb, comm)


def _w4_spec(r, c):
    return pl.BlockSpec((None, r, c), lambda i, j: (j, 0, 0))


FFN_ROW_CHUNK = 256


def _row_chunks(tm):
    rc = FFN_ROW_CHUNK if tm % FFN_ROW_CHUNK == 0 else tm
    return [slice(r, r + rc) for r in range(0, tm, rc)]


def _ffn_fwd(h, ln, wg4, wu4, wd4, comm=None):
    t_tok, d = h.shape
    f = wg4.shape[-2]
    tm = _tile(t_tok, 512)

    def body(h_ref, ln_ref, wg_ref, wu_ref, wd_ref, ho_ref, n_ref, g_ref, u_ref, n_s, acc):
        j = pl.program_id(1)

        @pl.when(j == 0)
        def _():
            xv = h_ref[...]
            nv = (xv * _rstd(xv) * ln_ref[...]).astype(BF)
            n_s[...] = nv
            n_ref[...] = nv
            acc[...] = jnp.zeros_like(acc)

        nv = n_s[...]
        g = _dot_nt(nv, wg_ref[...])
        u = _dot_nt(nv, wu_ref[...])
        g_ref[...] = g.astype(BF)
        u_ref[...] = u.astype(BF)
        a = (g * _sigmoid(g) * u).astype(BF)
        acc[...] += _dot(a, wd_ref[...])

        @pl.when(j == N_CHIPS - 1)
        def _():
            ho_ref[...] = h_ref[...] + 0.5 * acc[...]

    row = pl.BlockSpec((tm, d), lambda i, j: (i, 0))
    gu = pl.BlockSpec((None, tm, f), lambda i, j: (j, i, 0))
    gu_sds = jax.ShapeDtypeStruct((N_CHIPS, t_tok, f), BF)
    return _pcall(
        body, [h, ln, wg4, wu4, wd4], name="ffn_fwd", grid=(t_tok // tm, N_CHIPS),
        out_shape=[jax.ShapeDtypeStruct((t_tok, d), F32), jax.ShapeDtypeStruct((t_tok, d), BF), gu_sds, gu_sds],
        in_specs=[row, pl.BlockSpec((1, d), lambda i, j: (0, 0)), _w4_spec(f, d), _w4_spec(f, d), _w4_spec(f, d)],
        out_specs=[row, row, gu, gu],
        scratch=[pltpu.VMEM((tm, d), BF), pltpu.VMEM((tm, d), F32)], comm=comm)


def _ffn_bwd(dho, h, ln, g4, u4, wg4, wu4, wd4, comm=None):
    t_tok, d = h.shape
    f = wg4.shape[-2]
    tm = _tile(t_tok, 512)

    def body(dho_ref, h_ref, ln_ref, g_ref, u_ref, wg_ref, wu_ref, wd_ref,
             dhi_ref, dln_ref, dg_ref, du_ref, a_ref, dhb_ref, dhb_s, dn_acc):
        i, j = pl.program_id(0), pl.program_id(1)

        @pl.when(j == 0)
        def _():
            dhb = (0.5 * dho_ref[...]).astype(BF)
            dhb_s[...] = dhb
            dhb_ref[...] = dhb
            dn_acc[...] = jnp.zeros_like(dn_acc)

        @pl.when((i == 0) & (j == 0))
        def _():
            dln_ref[...] = jnp.zeros_like(dln_ref)

        for rows in _row_chunks(tm):
            g = g_ref[rows, :].astype(F32)
            u = u_ref[rows, :].astype(F32)
            s = _sigmoid(g)
            sg = g * s
            a_ref[rows, :] = (sg * u).astype(BF)
            da = _dot_nt(dhb_s[rows, :], wd_ref[...])
            dg = (da * u * (s * (1.0 + g * (1.0 - s)))).astype(BF)
            du = (da * sg).astype(BF)
            dg_ref[rows, :] = dg
            du_ref[rows, :] = du
            dn_acc[rows, :] += _dot(dg, wg_ref[...]) + _dot(du, wu_ref[...])

        @pl.when(j == N_CHIPS - 1)
        def _():
            xv = h_ref[...]
            dx, dln = _rms_bwd(dn_acc[...], xv, _rstd(xv), ln_ref[...])
            dln_ref[...] += dln
            dhi_ref[...] = dho_ref[...] + dx

    row = pl.BlockSpec((tm, d), lambda i, j: (i, 0))
    vec = pl.BlockSpec((1, d), lambda i, j: (0, 0))
    gu = pl.BlockSpec((None, tm, f), lambda i, j: (j, i, 0))
    gu_sds = jax.ShapeDtypeStruct((N_CHIPS, t_tok, f), BF)
    return _pcall(
        body, [dho, h, ln, g4, u4, wg4, wu4, wd4], name="ffn_bwd", grid=(t_tok // tm, N_CHIPS),
        out_shape=[jax.ShapeDtypeStruct((t_tok, d), F32), jax.ShapeDtypeStruct((1, d), F32),
                   gu_sds, gu_sds, gu_sds, jax.ShapeDtypeStruct((t_tok, d), BF)],
        in_specs=[row, row, vec, gu, gu, _w4_spec(f, d), _w4_spec(f, d), _w4_spec(f, d)],
        out_specs=[row, vec, gu, gu, gu, row],
        scratch=[pltpu.VMEM((tm, d), BF), pltpu.VMEM((tm, d), F32)], comm=comm)


def _rope_tables(pos_col, inv_freq2, comm=None):
    t_tok = pos_col.shape[0]

    def body(p_ref, f_ref, cos_ref, sin_ref):
        ang = p_ref[...] * f_ref[...]
        lane = lax.broadcasted_iota(jnp.int32, ang.shape, 1)
        s = jnp.sin(ang)
        cos_ref[...] = jnp.cos(ang)
        sin_ref[...] = jnp.where((lane & 1) == 0, -s, s)

    sds = jax.ShapeDtypeStruct((t_tok, 128), F32)
    return _pcall(body, [pos_col, inv_freq2], name="rope_tables", out_shape=[sds, sds], comm=comm)


def _swap_pairs(x):
    lane = lax.broadcasted_iota(jnp.int32, x.shape, 1)
    return jnp.where((lane & 1) == 0, pltpu.roll(x, 127, 1), pltpu.roll(x, 1, 1))


def _mix_in(h, ln, w_in, wm4, b_m, cos_t, sin_t, comm=None):
    t_tok, d = h.shape
    cm = wm4.shape[-1]
    tm = _tile(t_tok, 256)

    def body(h_ref, ln_ref, win_ref, wm_ref, bm_ref, cos_ref, sin_ref,
             u_ref, rq_ref, rk_ref, rv_ref, rg_ref, fq_ref, fk_ref, fv_ref, ff_ref, ga_ref, gb_ref):
        xv = h_ref[...]
        ub = (xv * _rstd(xv) * ln_ref[...]).astype(BF)
        u_ref[...] = ub
        cosv, sinv = cos_ref[...], sin_ref[...]

        def sec(k):
            return _dot_nt(ub, win_ref[k * 512:(k + 1) * 512, :])

        def rot(xh):
            return xh * cosv + _swap_pairs(xh) * sinv

        pq, pk = sec(0), sec(1)
        for hh in range(RET_HEADS):
            sl = slice(hh * RET_DIM, (hh + 1) * RET_DIM)
            rq_ref[:, sl] = rot(pq[:, sl]).astype(BF)
            rk_ref[:, sl] = (rot(pk[:, sl]) * RET_SCALE).astype(BF)
        rv_ref[...] = sec(2).astype(BF)
        rg_ref[...] = sec(3).astype(BF)
        fq_ref[...] = (sec(4) * FOX_SCALE).astype(BF)
        fk_ref[...] = sec(5).astype(BF)
        fv_ref[...] = sec(6).astype(BF)
        ff_ref[...] = _dot_nt(ub, win_ref[FF_COL:FF_COL + 128, :])
        for j in range(N_CHIPS):
            gs = _sigmoid(_dot(ub, wm_ref[j]) + bm_ref[:, j * cm:(j + 1) * cm]).astype(BF)
            col = j * cm
            if col < d:
                ga_ref[:, col:col + cm] = gs
            else:
                gb_ref[:, col - d:col - d + cm] = gs

    row = lambda c: pl.BlockSpec((tm, c), lambda i: (i, 0))
    full = lambda *s: pl.BlockSpec(s, lambda i: (0,) * len(s))
    sds = lambda c, dt: jax.ShapeDtypeStruct((t_tok, c), dt)
    return _pcall(
        body, [h, ln, w_in, wm4, b_m, cos_t, sin_t], name="mix_in", grid=(t_tok // tm,),
        out_shape=[sds(d, BF)] + [sds(512, BF)] * 7 + [sds(128, F32), sds(d, BF), sds(d, BF)],
        in_specs=[row(d), full(1, d), full(IN_PAD, d), full(N_CHIPS, d, cm), full(1, 2 * d), row(128), row(128)],
        out_specs=[row(d)] + [row(512)] * 7 + [row(128), row(d), row(d)], comm=comm)


def _split3(x):
    hi = x.astype(BF)
    r1 = x - hi.astype(F32)
    mid = r1.astype(BF)
    lo = (r1 - mid.astype(F32)).astype(BF)
    return hi, mid, lo


def _aug_lane():
    return lax.broadcasted_iota(jnp.int32, (1, 128), 1) & (FOX_DIM - 1)


def _aug_put(base, k0, parts):
    w = _aug_lane()
    for i, part in enumerate(parts):
        base = jnp.where(w == k0 + i, part, base)
    return base


def _forget_fwd(ffl, b_pad):
    t_tok = ffl.shape[0]
    tb = _tile(t_tok, 256)

    def body(ff_ref, b_ref, aq_ref, ak_ref, cum_s):
        r = lax.broadcasted_iota(jnp.int32, (tb, tb), 0)
        c = lax.broadcasted_iota(jnp.int32, (tb, tb), 1)
        tri = jnp.where(c <= r, 1.0, 0.0).astype(BF)
        carry = jnp.zeros((1, 128), F32)
        for i in range(t_tok // tb):
            z = ff_ref[i * tb:(i + 1) * tb, :] + b_ref[...]
            lf = jnp.minimum(z, 0.0) - jnp.log(1.0 + jnp.exp(-jnp.abs(z)))
            hi, mid, lo = _split3(lf)
            cs = _dot(tri, hi) + _dot(tri, mid) + _dot(tri, lo) + carry
            cum_s[i * tb:(i + 1) * tb, :] = cs
            carry = cs[tb - 1:tb, :]
        x = cum_s[...]
        first = lax.broadcasted_iota(jnp.int32, (1, 128), 1) < FOX_DIM
        w = _aug_lane()
        one = jnp.ones((t_tok, 128), BF)
        zero = jnp.zeros((t_tok, 128), BF)
        for pp in range(FOX_HEADS // 2):
            other = jnp.where(first, x[:, 2 * pp + 1:2 * pp + 2], x[:, 2 * pp:2 * pp + 1])
            parts = _split3(other)
            aq = jnp.where((w >= 3) & (w < 6), one, zero)
            ak = jnp.where((w < 3) | ((w >= 6) & (w < 9)), one, zero)
            aq_ref[:, pp * 128:(pp + 1) * 128] = _aug_put(aq, 0, parts)
            ak_ref[:, pp * 128:(pp + 1) * 128] = _aug_put(ak, 3, [-q for q in parts])

    sds = jax.ShapeDtypeStruct((t_tok, FOX_WIDTH), BF)
    return _pcall(body, [ffl, b_pad], name="forget_fwd", out_shape=[sds, sds],
                  scratch=[pltpu.VMEM((t_tok, 128), F32)])


def _fox_aug_lse(aq, lse_e):
    t_tok = aq.shape[0]
    tm = _tile(t_tok, 512)

    def body(aq_ref, lse_ref, o_ref):
        for pp in range(FOX_HEADS // 2):
            sl = slice(pp * 128, (pp + 1) * 128)
            other = pltpu.roll(lse_ref[:, sl], FOX_DIM, 1)
            o_ref[:, sl] = _aug_put(aq_ref[:, sl], 6, _split3(-other))

    spec = pl.BlockSpec((tm, FOX_WIDTH), lambda i: (i, 0))
    return _pcall(body, [aq, lse_e], name="fox_aug_lse", grid=(t_tok // tm,),
                  out_shape=jax.ShapeDtypeStruct((t_tok, FOX_WIDTH), BF), in_specs=[spec, spec], out_specs=spec)


def _forget_bwd(dcum_t, dcum_q, ffl, b_pad):
    t_tok = ffl.shape[0]
    tb = _tile(t_tok, 256)

    def body(dc_ref, dq_ref, ff_ref, b_ref, dff_ref, db_ref, pad_s, d_s):
        pad_s[...] = jnp.zeros_like(pad_s)
        pad_s[0:FOX_HEADS, :] = dc_ref[...]
        dsum = pad_s[...].T
        lane = lax.broadcasted_iota(jnp.int32, (t_tok, 128), 1)
        for hh in range(FOX_HEADS):
            dsum = dsum + jnp.where(lane == hh, dq_ref[:, hh * FOX_DIM:hh * FOX_DIM + 1], 0.0)
        d_s[...] = dsum
        r = lax.broadcasted_iota(jnp.int32, (tb, tb), 0)
        c = lax.broadcasted_iota(jnp.int32, (tb, tb), 1)
        tri = jnp.where(c >= r, 1.0, 0.0).astype(BF)
        carry = jnp.zeros((1, 128), F32)
        db = jnp.zeros((1, 128), F32)
        for i in reversed(range(t_tok // tb)):
            hi, mid, lo = _split3(d_s[i * tb:(i + 1) * tb, :])
            dlf = _dot(tri, hi) + _dot(tri, mid) + _dot(tri, lo) + carry
            carry = dlf[0:1, :]
            z = ff_ref[i * tb:(i + 1) * tb, :] + b_ref[...]
            dff = dlf * _sigmoid(-z)
            dff_ref[i * tb:(i + 1) * tb, :] = dff.astype(BF)
            db = db + jnp.sum(dff, axis=0, keepdims=True)
        db_ref[...] = db

    return _pcall(
        body, [dcum_t, dcum_q, ffl, b_pad], name="forget_bwd",
        out_shape=[jax.ShapeDtypeStruct((t_tok, 128), BF), jax.ShapeDtypeStruct((1, 128), F32)],
        scratch=[pltpu.VMEM((128, t_tok), F32), pltpu.VMEM((t_tok, 128), F32)])


def _first_half():
    return lax.broadcasted_iota(jnp.int32, (1, 128), 1) < FOX_DIM


def _head_rows(x2, a2, hh):
    return jnp.where(_first_half(), x2, a2) if hh == 0 else jnp.where(_first_half(), a2, x2)


def _head_only(x2, hh):
    zero = jnp.zeros_like(x2)
    return jnp.where(_first_half(), x2, zero) if hh == 0 else jnp.where(_first_half(), zero, x2)


def _causal_diag(s):
    rows = lax.broadcasted_iota(jnp.int32, s.shape, 0)
    cols = lax.broadcasted_iota(jnp.int32, s.shape, 1)
    return jnp.where(cols <= rows, s, NEG)


def _diag_or_below(qi, ki, step):
    pl.when(ki < qi)(lambda: step(False))
    pl.when(ki == qi)(lambda: step(True))


def _tri_rows(s, n):
    qi = sum((s >= r * (r + 1) // 2).astype(jnp.int32) for r in range(1, n))
    return qi, s - (qi * (qi + 1)) // 2


def _tri_cols(s, n):
    ki = sum((s >= k * n - k * (k - 1) // 2).astype(jnp.int32) for k in range(1, n))
    return ki, ki + s - (ki * n - (ki * (ki - 1)) // 2)


def _fox_fwd(fq, fk, fv, aq, ak, comm=None):
    t_tok = fq.shape[0]
    t = _tile(t_tok, 512)
    nq = t_tok // t
    npair = FOX_HEADS // 2

    def body(q_ref, k_ref, v_ref, aq_ref, ak_ref, o_ref, of_ref, lse_ref, m_s, l_s, acc_s):
        qi, ki = _tri_rows(pl.program_id(1), nq)

        @pl.when(ki == 0)
        def _():
            m_s[...] = jnp.full_like(m_s, NEG)
            l_s[...] = jnp.zeros_like(l_s)
            acc_s[...] = jnp.zeros_like(acc_s)

        def step(diag):
            q2, k2, v2, aq2, ak2 = q_ref[...], k_ref[...], v_ref[...], aq_ref[...], ak_ref[...]
            for hh in range(2):
                s = _dot_nt(_head_rows(q2, aq2, hh), _head_rows(k2, ak2, hh))
                if diag:
                    s = _causal_diag(s)
                m_prev = m_s[hh]
                m_new = jnp.maximum(m_prev, jnp.max(s, axis=1, keepdims=True))
                alpha = jnp.exp(m_prev - m_new)
                p = jnp.exp(s - jnp.tile(m_new, (1, t // 128)))
                l_s[hh] = alpha * l_s[hh] + jnp.sum(p, axis=1, keepdims=True)
                acc_s[hh] = alpha * acc_s[hh] + _dot(p.astype(BF), v2)
                m_s[hh] = m_new

        _diag_or_below(qi, ki, step)

        @pl.when(ki == qi)
        def _():
            first = _first_half()
            o = jnp.where(first, acc_s[0] / l_s[0], acc_s[1] / l_s[1])
            o_ref[...] = o.astype(BF)
            of_ref[...] = o
            lse_ref[...] = jnp.where(first, m_s[0] + jnp.log(l_s[0]), m_s[1] + jnp.log(l_s[1]))

    qs = pl.BlockSpec((t, 128), lambda p, s: (_tri_rows(s, nq)[0], p))
    ks = pl.BlockSpec((t, 128), lambda p, s: (_tri_rows(s, nq)[1], p))
    stat = pltpu.VMEM((2, t, 128), F32)
    return _pcall(
        body, [fq, fk, fv, aq, ak], name="fox_fwd", grid=(npair, nq * (nq + 1) // 2),
        out_shape=[jax.ShapeDtypeStruct((t_tok, FOX_WIDTH), BF), jax.ShapeDtypeStruct((t_tok, FOX_WIDTH), F32),
                   jax.ShapeDtypeStruct((t_tok, FOX_WIDTH), F32)],
        in_specs=[qs, ks, ks, qs, ks], out_specs=[qs, qs, qs], scratch=[stat, stat, stat], comm=comm)


def _fox_ds(q2, k2, v2, do2, aq2, ak2, ad2, hh, diag):
    s = _dot_nt(_head_rows(q2, aq2, hh), _head_rows(k2, ak2, hh))
    if diag:
        s = _causal_diag(s)
    p = jnp.exp(s)
    av = jnp.where(_aug_lane() < 3, 1.0, 0.0).astype(BF)
    dp = _dot_nt(_head_rows(do2, ad2, hh), _head_rows(v2, jnp.broadcast_to(av, v2.shape), hh))
    return p, p * dp


def _fox_bwd(fq, fk, fv, do, aqb, ak, ad, comm=None):
    t_tok = fq.shape[0]
    t = _tile(t_tok, 512)
    nq = t_tok // t
    npair = FOX_HEADS // 2
    n_steps = nq * (nq + 1) // 2

    def body(q_ref, k_ref, v_ref, do_ref, aq_ref, ak_ref, ad_ref, dq_ref, dk_ref, dv_ref, dck_ref, dcq_ref,
             dk_s, dv_s, dq_s, rs_s):
        step_id = pl.program_id(1)
        ki, qi = _tri_cols(step_id, nq)

        @pl.when(step_id == 0)
        def _():
            dq_s[...] = jnp.zeros_like(dq_s)
            rs_s[...] = jnp.zeros_like(rs_s)

        @pl.when(qi == ki)
        def _():
            dk_s[...] = jnp.zeros_like(dk_s)
            dv_s[...] = jnp.zeros_like(dv_s)
            dck_ref[...] = jnp.zeros_like(dck_ref)

        rows = pl.ds(qi * t if isinstance(qi, int) else pl.multiple_of(qi * t, t), t)

        def step(diag):
            q2, k2, v2, do2 = q_ref[...], k_ref[...], v_ref[...], do_ref[...]
            dq = []
            for hh in range(2):
                p, ds = _fox_ds(q2, k2, v2, do2, aq_ref[...], ak_ref[...], ad_ref[...], hh, diag)
                dsb = ds.astype(BF)
                dv_s[...] += _dot_tn(p.astype(BF), _head_only(do2, hh))
                dk_s[...] += _dot_tn(dsb, _head_only(q2, hh))
                dq.append(_dot(dsb, k2))
                dck_ref[hh] = dck_ref[hh] - jnp.sum(ds, axis=0, keepdims=True)
                rs_s[hh, rows, :] = rs_s[hh, rows, :] + jnp.sum(ds, axis=1, keepdims=True)
            dq_s[rows, :] = dq_s[rows, :] + jnp.where(_first_half(), dq[0], dq[1])

        _diag_or_below(qi, ki, step)

        @pl.when(qi == nq - 1)
        def _():
            dk_ref[...] = dk_s[...].astype(BF)
            dv_ref[...] = dv_s[...].astype(BF)

        @pl.when(step_id == n_steps - 1)
        def _():
            dq_ref[...] = (dq_s[...] * FOX_SCALE).astype(BF)
            dcq_ref[...] = jnp.where(_first_half(), rs_s[0], rs_s[1])

    qs = pl.BlockSpec((t, 128), lambda p, s: (_tri_cols(s, nq)[1], p))
    ks = pl.BlockSpec((t, 128), lambda p, s: (_tri_cols(s, nq)[0], p))
    cks = pl.BlockSpec((2, 1, t), lambda p, s: (p, 0, _tri_cols(s, nq)[0]))
    seq = pl.BlockSpec((t_tok, 128), lambda p, s: (0, p))
    sds = jax.ShapeDtypeStruct((t_tok, FOX_WIDTH), BF)
    return _pcall(
        body, [fq, fk, fv, do, aqb, ak, ad], name="fox_bwd", grid=(npair, n_steps),
        out_shape=[sds, sds, sds, jax.ShapeDtypeStruct((FOX_HEADS, 1, t_tok), F32),
                   jax.ShapeDtypeStruct((t_tok, FOX_WIDTH), F32)],
        in_specs=[qs, ks, ks, qs, qs, ks, qs], out_specs=[seq, ks, ks, cks, seq],
        scratch=[pltpu.VMEM((t, 128), F32), pltpu.VMEM((t, 128), F32), pltpu.VMEM((t_tok, 128), F32),
                 pltpu.VMEM((2, t_tok, 128), F32)], comm=comm)


def _ret_consts():
    c = RET_CHUNK
    log_gamma = jnp.log1p(-jnp.exp2(-5.0 - jnp.arange(RET_HEADS, dtype=F32)))
    idx = jnp.arange(c, dtype=F32)
    diff = idx[:, None] - idx[None, :]
    dmask = jnp.where(diff >= 0, jnp.exp(log_gamma[:, None, None] * jnp.maximum(diff, 0.0)), 0.0)
    qdec = jnp.exp(log_gamma[:, None] * (idx + 1.0))
    kdec = jnp.exp(log_gamma[:, None] * (c - 1 - idx))
    cdec = jnp.exp(log_gamma * c)
    bc = lambda v: jnp.broadcast_to(v[:, :, None], (RET_HEADS, c, RET_DIM))
    return dmask, bc(qdec), bc(kdec), jnp.broadcast_to(cdec[:, None, None], (RET_HEADS, c, RET_DIM))


def _group_norm(y):
    mu = jnp.mean(y, axis=-1, keepdims=True)
    yc = y - mu
    r = lax.rsqrt(jnp.mean(yc * yc, axis=-1, keepdims=True) + EPS)
    return yc * r, r


def _ret_fwd(rq, rk, rv, rg, consts, comm=None):
    t_tok = rq.shape[0]
    nb = 4 if t_tok % (4 * RET_CHUNK) == 0 else 1
    tr = nb * RET_CHUNK
    n_steps = t_tok // tr
    c = RET_CHUNK

    def body(q_ref, k_ref, v_ref, g_ref, dm_ref, qd_ref, kd_ref, cd_ref, y_ref, yo_ref, st_ref, s_s):
        @pl.when(pl.program_id(1) == 0)
        def _():
            s_s[...] = jnp.zeros_like(s_s)

        dm, qd, kd, cd = dm_ref[...], qd_ref[...], kd_ref[...], cd_ref[...]
        for b in range(nb):
            rows = slice(b * c, (b + 1) * c)
            q, k, v = q_ref[rows, :], k_ref[rows, :], v_ref[rows, :]
            state = s_s[...]
            st_ref[b] = state
            sc = (_dot_nt(q, k) * dm).astype(BF)
            y = _dot(sc, v) + _dot((q.astype(F32) * qd).astype(BF), state.astype(BF))
            s_s[...] = cd * state + _dot_tn((k.astype(F32) * kd).astype(BF), v)
            y_ref[rows, :] = y
            yn, _ = _group_norm(y)
            gate = g_ref[rows, :].astype(F32)
            yo_ref[rows, :] = (yn * (gate * _sigmoid(gate))).astype(BF)

    blk = pl.BlockSpec((tr, RET_DIM), lambda h, i: (i, h))
    cst = pl.BlockSpec((None, c, RET_DIM), lambda h, i: (h, 0, 0))
    return _pcall(
        body, [rq, rk, rv, rg, *consts], name="ret_fwd", grid=(RET_HEADS, n_steps),
        out_shape=[jax.ShapeDtypeStruct((t_tok, RET_WIDTH), F32), jax.ShapeDtypeStruct((t_tok, RET_WIDTH), BF),
                   jax.ShapeDtypeStruct((RET_HEADS, t_tok // c, RET_DIM, RET_DIM), F32)],
        in_specs=[blk] * 4 + [cst] * 4,
        out_specs=[blk, blk, pl.BlockSpec((None, nb, RET_DIM, RET_DIM), lambda h, i: (h, i, 0, 0))],
        scratch=[pltpu.VMEM((RET_DIM, RET_DIM), F32)], comm=comm)


def _ret_bwd(rq, rk, rv, rg, y_raw, dyo, states, consts, cos_t, sin_t, comm=None):
    t_tok = rq.shape[0]
    nb = 4 if t_tok % (4 * RET_CHUNK) == 0 else 1
    tr = nb * RET_CHUNK
    n_steps = t_tok // tr
    c = RET_CHUNK

    def body(q_ref, k_ref, v_ref, g_ref, y_ref, dyo_ref, st_ref, dm_ref, qd_ref, kd_ref, cd_ref,
             cos_ref, sin_ref, dq_ref, dk_ref, dv_ref, dg_ref, ds_s):
        @pl.when(pl.program_id(1) == 0)
        def _():
            ds_s[...] = jnp.zeros_like(ds_s)

        dm, qd, kd, cd = dm_ref[...], qd_ref[...], kd_ref[...], cd_ref[...]
        for b in reversed(range(nb)):
            rows = slice(b * c, (b + 1) * c)
            q, k, v = q_ref[rows, :], k_ref[rows, :], v_ref[rows, :]
            cosv, sinv = cos_ref[rows, :], sin_ref[rows, :]
            yn, r = _group_norm(y_ref[rows, :])
            gate = g_ref[rows, :].astype(F32)
            sg = _sigmoid(gate)
            dyo = dyo_ref[rows, :]
            dg_ref[rows, :] = (dyo * yn * (sg * (1.0 + gate * (1.0 - sg)))).astype(BF)
            dyn = dyo * (gate * sg)
            dy = r * (dyn - jnp.mean(dyn, axis=-1, keepdims=True)
                      - yn * jnp.mean(dyn * yn, axis=-1, keepdims=True))
            dyb = dy.astype(BF)
            state_b = st_ref[b].astype(BF)
            dstate = ds_s[...]
            dstate_b = dstate.astype(BF)
            qdb = (q.astype(F32) * qd).astype(BF)
            kdb = (k.astype(F32) * kd).astype(BF)
            sc = (_dot_nt(q, k) * dm).astype(BF)
            dv = _dot_tn(sc, dyb) + _dot(kdb, dstate_b)
            dp = (_dot_nt(dyb, v) * dm).astype(BF)
            dq = _dot(dp, k) + _dot_nt(dyb, state_b) * qd
            dk = (_dot_tn(dp, q) + _dot_nt(v, dstate_b) * kd) * RET_SCALE
            ds_s[...] = cd * dstate + _dot_tn(qdb, dyb)
            dv_ref[rows, :] = dv.astype(BF)
            dq_ref[rows, :] = (dq * cosv - _swap_pairs(dq) * sinv).astype(BF)
            dk_ref[rows, :] = (dk * cosv - _swap_pairs(dk) * sinv).astype(BF)

    rev = lambda i: n_steps - 1 - i
    blk = pl.BlockSpec((tr, RET_DIM), lambda h, i: (rev(i), h))
    tab = pl.BlockSpec((tr, RET_DIM), lambda h, i: (rev(i), 0))
    cst = pl.BlockSpec((None, c, RET_DIM), lambda h, i: (h, 0, 0))
    sds = jax.ShapeDtypeStruct((t_tok, RET_WIDTH), BF)
    return _pcall(
        body, [rq, rk, rv, rg, y_raw, dyo, states, *consts, cos_t, sin_t], name="ret_bwd",
        grid=(RET_HEADS, n_steps), out_shape=[sds] * 4,
        in_specs=[blk] * 6 + [pl.BlockSpec((None, nb, RET_DIM, RET_DIM), lambda h, i: (h, rev(i), 0, 0))]
        + [cst] * 4 + [tab, tab],
        out_specs=[blk] * 4, scratch=[pltpu.VMEM((RET_DIM, RET_DIM), F32)], comm=comm)


def _mix_out(h, y_ret, y_fox, ga, gb, wr4, wf4, wo4, comm=None):
    t_tok, d = h.shape
    cz = wr4.shape[-1]
    ro = wo4.shape[-2]
    tm = _tile(t_tok, 512)

    def body(h_ref, yr_ref, yf_ref, ga_ref, gb_ref, wr_ref, wf_ref, wo_ref, ho_ref, za_ref, zb_ref, mix_ref):
        yr, yf = yr_ref[...], yf_ref[...]
        for j in range(N_CHIPS):
            sl = slice(j * cz, (j + 1) * cz)
            za = _dot(yr, wr_ref[j])
            zb = _dot(yf, wf_ref[j])
            za_ref[:, sl] = za.astype(BF)
            zb_ref[:, sl] = zb.astype(BF)
            mix_ref[:, sl] = (ga_ref[:, sl].astype(F32) * za + gb_ref[:, sl].astype(F32) * zb).astype(BF)
        acc = h_ref[...]
        for j in range(N_CHIPS):
            acc = acc + _dot(mix_ref[:, j * ro:(j + 1) * ro], wo_ref[j])
        ho_ref[...] = acc

    row = lambda c: pl.BlockSpec((tm, c), lambda i: (i, 0))
    full = lambda *s: pl.BlockSpec(s, lambda i: (0,) * len(s))
    sds = lambda dt: jax.ShapeDtypeStruct((t_tok, d), dt)
    return _pcall(
        body, [h, y_ret, y_fox, ga, gb, wr4, wf4, wo4], name="mix_out", grid=(t_tok // tm,),
        out_shape=[sds(F32), sds(BF), sds(BF), sds(BF)],
        in_specs=[row(d), row(RET_WIDTH), row(FOX_WIDTH), row(d), row(d),
                  full(N_CHIPS, RET_WIDTH, cz), full(N_CHIPS, FOX_WIDTH, cz), full(N_CHIPS, ro, d)],
        out_specs=[row(d)] * 4, comm=comm)


def _mix_out_bwd(dh, za, zb, ga, gb, y_fox, wr4, wf4, wo4, comm=None):
    t_tok, d = dh.shape
    cz = wr4.shape[-1]
    ro = wo4.shape[-2]
    tm = _tile(t_tok, 256)

    def body(dh_ref, za_ref, zb_ref, ga_ref, gb_ref, yf_ref, wr_ref, wf_ref, wo_ref,
             dhb_ref, dgp_ref, dza_ref, dzb_ref, dyr_ref, dyf_ref, dl_ref, db_ref):
        @pl.when(pl.program_id(0) == 0)
        def _():
            db_ref[...] = jnp.zeros_like(db_ref)

        dhb = dh_ref[...].astype(BF)
        dhb_ref[...] = dhb
        dyr = jnp.zeros((tm, RET_WIDTH), F32)
        dyf = jnp.zeros((tm, FOX_WIDTH), F32)
        for j in range(N_CHIPS):
            sl = slice(j * ro, (j + 1) * ro)
            dmix = _dot_nt(dhb, wo_ref[j])
            ga, gb = ga_ref[:, sl].astype(F32), gb_ref[:, sl].astype(F32)
            dza = (dmix * ga).astype(BF)
            dzb = (dmix * gb).astype(BF)
            dza_ref[:, sl] = dza
            dzb_ref[:, sl] = dzb
            dga = dmix * za_ref[:, sl].astype(F32) * ga * (1.0 - ga)
            dgb = dmix * zb_ref[:, sl].astype(F32) * gb * (1.0 - gb)
            dgp_ref[:, sl] = dga.astype(BF)
            dgp_ref[:, d + j * ro:d + (j + 1) * ro] = dgb.astype(BF)
            db_ref[:, sl] += jnp.sum(dga, axis=0, keepdims=True)
            db_ref[:, d + j * ro:d + (j + 1) * ro] += jnp.sum(dgb, axis=0, keepdims=True)
        for j in range(N_CHIPS):
            sl = slice(j * cz, (j + 1) * cz)
            dyr = dyr + _dot_nt(dza_ref[:, sl], wr_ref[j])
            dyf = dyf + _dot_nt(dzb_ref[:, sl], wf_ref[j])
        dyr_ref[...] = dyr
        dyfb = dyf.astype(BF)
        dyf_ref[...] = dyfb
        prod = dyfb.astype(F32) * yf_ref[...]
        first = _first_half()
        for pp in range(FOX_HEADS // 2):
            blk = prod[:, pp * 128:(pp + 1) * 128]
            s0 = jnp.sum(jnp.where(first, blk, 0.0), axis=1, keepdims=True)
            s1 = jnp.sum(jnp.where(first, 0.0, blk), axis=1, keepdims=True)
            parts = _split3(-jnp.where(first, s1, s0))
            dl_ref[:, pp * 128:(pp + 1) * 128] = _aug_put(jnp.zeros((tm, 128), BF), 0, parts)

    row = lambda c: pl.BlockSpec((tm, c), lambda i: (i, 0))
    full = lambda *s: pl.BlockSpec(s, lambda i: (0,) * len(s))
    sds = lambda c, dt: jax.ShapeDtypeStruct((t_tok, c), dt)
    return _pcall(
        body, [dh, za, zb, ga, gb, y_fox, wr4, wf4, wo4], name="mix_out_bwd", grid=(t_tok // tm,),
        out_shape=[sds(d, BF), sds(2 * d, BF), sds(d, BF), sds(d, BF), sds(RET_WIDTH, F32),
                   sds(FOX_WIDTH, BF), sds(FOX_WIDTH, BF), jax.ShapeDtypeStruct((1, 2 * d), F32)],
        in_specs=[row(d)] * 5 + [row(FOX_WIDTH), full(N_CHIPS, RET_WIDTH, cz), full(N_CHIPS, FOX_WIDTH, cz),
                                 full(N_CHIPS, ro, d)],
        out_specs=[row(d), row(2 * d), row(d), row(d), row(RET_WIDTH), row(FOX_WIDTH), row(FOX_WIDTH),
                   full(1, 2 * d)],
        comm=comm)


def _mix_in_bwd(dh, h, ln, parts, dff, dgpre, w_in, wm4, comm=None):
    t_tok, d = h.shape
    cm = wm4.shape[-1]
    tm = _tile(t_tok, 256)

    def body(dh_ref, h_ref, ln_ref, p0, p1, p2, p3, p4, p5, p6, dff_ref, dgp_ref, win_ref, wm_ref,
             dhi_ref, dln_ref, dproj_ref):
        @pl.when(pl.program_id(0) == 0)
        def _():
            dln_ref[...] = jnp.zeros_like(dln_ref)

        for k, pr in enumerate((p0, p1, p2, p3, p4, p5, p6)):
            dproj_ref[:, k * 512:(k + 1) * 512] = pr[...]
        dproj_ref[:, FF_COL:FF_COL + 128] = dff_ref[...]
        dproj_ref[:, FF_COL + 128:] = jnp.zeros((tm, IN_PAD - FF_COL - 128), BF)
        du = _dot(dproj_ref[...], win_ref[...])
        for j in range(N_CHIPS):
            du = du + _dot_nt(dgp_ref[:, j * cm:(j + 1) * cm], wm_ref[j])
        xv = h_ref[...]
        dx, dln = _rms_bwd(du, xv, _rstd(xv), ln_ref[...])
        dln_ref[...] += dln
        dhi_ref[...] = dh_ref[...] + dx

    row = lambda c: pl.BlockSpec((tm, c), lambda i: (i, 0))
    full = lambda *s: pl.BlockSpec(s, lambda i: (0,) * len(s))
    return _pcall(
        body, [dh, h, ln, *parts, dff, dgpre, w_in, wm4], name="mix_in_bwd", grid=(t_tok // tm,),
        out_shape=[jax.ShapeDtypeStruct((t_tok, d), F32), jax.ShapeDtypeStruct((1, d), F32),
                   jax.ShapeDtypeStruct((t_tok, IN_PAD), BF)],
        in_specs=[row(d), row(d), full(1, d)] + [row(512)] * 7 + [row(128), row(2 * d), full(IN_PAD, d),
                                                                   full(N_CHIPS, d, cm)],
        out_specs=[row(d), full(1, d), row(IN_PAD)], comm=comm)


def _tail(h, p, target, ln_ple, ln_fin, wpg4, wpl4, comm=None):
    t_tok, d = h.shape
    pd = p.shape[1]
    rg = wpg4.shape[-2]
    cp = wpl4.shape[-1]
    tm = _tile(t_tok, 256)

    def body(h_ref, p_ref, t_ref, lp_ref, lf_ref, wg_ref, wp_ref,
             dh_ref, n_ref, dgp_ref, dpe_ref, pb_ref, loss_ref, dlf_ref, dlp_ref, pe_s, dn_s):
        @pl.when(pl.program_id(0) == 0)
        def _():
            loss_ref[...] = jnp.zeros_like(loss_ref)
            dlf_ref[...] = jnp.zeros_like(dlf_ref)
            dlp_ref[...] = jnp.zeros_like(dlp_ref)

        xv = h_ref[...]
        r3 = _rstd(xv)
        nb = (xv * r3 * lp_ref[...]).astype(BF)
        n_ref[...] = nb
        pb = p_ref[...].astype(BF)
        pb_ref[...] = pb
        pgpre = jnp.zeros((tm, d), F32)
        for j in range(N_CHIPS):
            pgpre = pgpre + _dot(nb[:, j * rg:(j + 1) * rg], wg_ref[j])
            pe_s[:, j * cp:(j + 1) * cp] = _dot(pb, wp_ref[j])
        pg = _sigmoid(pgpre)
        pe = pe_s[...]
        h4 = xv + pg * pe
        r4 = _rstd(h4)
        err = h4 * r4 * lf_ref[...] - t_ref[...]
        loss_ref[...] += 0.5 * jnp.sum(jnp.sum(err * err, axis=1, keepdims=True), axis=0, keepdims=True) / d
        dh4, dlf = _rms_bwd(err * (1.0 / d), h4, r4, lf_ref[...])
        dlf_ref[...] += dlf
        dpe_ref[...] = (dh4 * pg).astype(BF)
        dgp = (dh4 * pe * pg * (1.0 - pg)).astype(BF)
        dgp_ref[...] = dgp
        for j in range(N_CHIPS):
            dn_s[:, j * rg:(j + 1) * rg] = _dot_nt(dgp, wg_ref[j])
        dx, dlp = _rms_bwd(dn_s[...], xv, r3, lp_ref[...])
        dlp_ref[...] += dlp
        dh_ref[...] = dh4 + dx

    row = lambda c: pl.BlockSpec((tm, c), lambda i: (i, 0))
    full = lambda *s: pl.BlockSpec(s, lambda i: (0,) * len(s))
    sds = lambda c, dt: jax.ShapeDtypeStruct((t_tok, c), dt)
    vec = jax.ShapeDtypeStruct((1, d), F32)
    return _pcall(
        body, [h, p, target, ln_ple, ln_fin, wpg4, wpl4], name="tail", grid=(t_tok // tm,),
        out_shape=[sds(d, F32), sds(d, BF), sds(d, BF), sds(d, BF), sds(pd, BF),
                   jax.ShapeDtypeStruct((1, 128), F32), vec, vec],
        in_specs=[row(d), row(pd), row(d), full(1, d), full(1, d), full(N_CHIPS, rg, d), full(N_CHIPS, pd, cp)],
        out_specs=[row(d), row(d), row(d), row(d), row(pd), full(1, 128), full(1, d), full(1, d)],
        scratch=[pltpu.VMEM((tm, d), F32), pltpu.VMEM((tm, d), F32)], comm=comm)


BIG = ["w_ffn1_gate", "w_ffn1_up", "w_ffn1_down", "w_in", "w_merge", "w_ret_out", "w_fox_out", "w_out",
       "w_ffn2_gate", "w_ffn2_up", "w_ffn2_down", "w_ple", "w_ple_gate"]
SMALL = ["ln_ffn1", "ln_mix", "b_forget", "b_merge", "ln_ffn2", "ln_ple", "ln_final"]
WEIGHTS = ["ln_ffn1", "w_ffn1_gate", "w_ffn1_up", "w_ffn1_down", "ln_mix", "w_in", "b_forget", "w_merge", "b_merge",
           "w_ret_out", "w_fox_out", "w_out", "ln_ffn2", "w_ffn2_gate", "w_ffn2_up", "w_ffn2_down", "ln_ple",
           "w_ple", "w_ple_gate", "ln_final"]


TRANSPOSED = {"w_ffn1_gate", "w_ffn1_up", "w_ffn2_gate", "w_ffn2_up", "w_in"}
IN_ROWS_PAD = -(-(IN_COLS // N_CHIPS) // 32) * 32


def _pack_small(vals, loss_row):
    rows = [loss_row]
    for name in SMALL:
        v = vals[name].reshape(-1)
        n = -(-v.shape[0] // 128) * 128
        rows.append(jnp.pad(v, (0, n - v.shape[0])).reshape(n // 128, 128))
    packed = jnp.concatenate(rows, axis=0)
    pad = -packed.shape[0] % 8
    return jnp.pad(packed, ((0, pad), (0, 0)))


def _unpack_small(packed, sizes):
    out, r = {}, 1
    for name in SMALL:
        n = sizes[name]
        nr = -(-n // 128)
        out[name] = packed[r:r + nr].reshape(1, nr * 128)[:, :n]
        r += nr
    return out


class _Stage:
    def __init__(self, comm, finish):
        self.comm, self.finish, self.result = comm, finish, None


def _hosted(fn, *a, stages=()):
    if not stages:
        return fn(*a)
    outs, couts = fn(*a, comm=_merge([st.comm for st in stages]))
    for st, o in zip(stages, _split_outs([st.comm for st in stages], couts)):
        st.result = st.finish(o)
    return outs


class _Reducer:
    def __init__(self):
        self.done = {}

    def swap(self, grads):
        names = list(grads)
        return _Stage(_c_half_swap([grads[n] for n in names]),
                      lambda outs: {n: _add_halves(grads[n], o) for n, o in zip(names, outs)})

    def exchange(self, parts):
        names = list(parts)
        return _Stage(_c_chip_exchange([parts[n] for n in names]),
                      lambda outs: {n: _sum_chips(parts[n], o) for n, o in zip(names, outs)})

    def join(self, halves):
        names = list(halves)
        return _Stage(_c_join([halves[n] for n in names]),
                      lambda outs: self.done.update({n: (halves[n], o) for n, o in zip(names, outs)}))


def kernel(x, p, positions, ln_ffn1, w_ffn1_gate, w_ffn1_up, w_ffn1_down, ln_mix, w_in, b_forget, w_merge, b_merge, w_ret_out, w_fox_out, w_out, ln_ffn2, w_ffn2_gate, w_ffn2_up, w_ffn2_down, ln_ple, w_ple, w_ple_gate, ln_final, loss_target, m_ln_ffn1, m_w_ffn1_gate, m_w_ffn1_up, m_w_ffn1_down, m_ln_mix, m_w_in, m_b_forget, m_w_merge, m_b_merge, m_w_ret_out, m_w_fox_out, m_w_out, m_ln_ffn2, m_w_ffn2_gate, m_w_ffn2_up, m_w_ffn2_down, m_ln_ple, m_w_ple, m_w_ple_gate, m_ln_final, v_ln_ffn1, v_w_ffn1_gate, v_w_ffn1_up, v_w_ffn1_down, v_ln_mix, v_w_in, v_b_forget, v_w_merge, v_b_merge, v_w_ret_out, v_w_fox_out, v_w_out, v_ln_ffn2, v_w_ffn2_gate, v_w_ffn2_up, v_w_ffn2_down, v_ln_ple, v_w_ple, v_w_ple_gate, v_ln_final):
    args = dict(locals())
    w = {n: args[n] for n in WEIGHTS}
    m = {n: args["m_" + n] for n in WEIGHTS}
    v = {n: args["v_" + n] for n in WEIGHTS}
    d = x.shape[-1]
    t_tok = x.shape[1]
    xs, ps, target = x[0], p[0, 0], loss_target[0]
    small = {n: w[n].reshape(1, -1) for n in SMALL}

    def to2d(n, a):
        if n in TRANSPOSED:
            return a[0].T
        return a.reshape(a.shape[-2], a.shape[-1]) if a.ndim == 3 else a.reshape(1, -1)

    def from2d(n, a):
        return a.T[None] if n in TRANSPOSED else a.reshape(w[n].shape)

    def padded(n, a):
        return jnp.pad(a, ((0, IN_ROWS_PAD - a.shape[0]), (0, 0))) if n == "w_in" else a

    core = lax.axis_index("c")
    me = 2 * lax.axis_index("x") + lax.axis_index("y")
    shard = {}
    for n in BIG:
        s2 = padded(n, to2d(n, w[n]).astype(BF))
        shard[n] = s2.reshape(1, 2, s2.shape[0] // 2, s2.shape[1])
    full = {}

    def gather(names):
        bufs = [lax.dynamic_update_slice(jnp.zeros((N_CHIPS,) + shard[n].shape[1:], BF), shard[n], (me, 0, 0, 0))
                for n in names]

        def finish(outs):
            full.update({n: o.reshape(N_CHIPS, 2 * o.shape[2], o.shape[3]) for n, o in zip(names, outs)})

        return _Stage(_c_all_gather(bufs), finish)

    half = RET_DIM // 2
    inv_freq = 1.0 / (ROPE_BASE ** (jnp.arange(half, dtype=F32) / half))
    cos_t, sin_t = _hosted(_rope_tables, positions[0].astype(F32).reshape(t_tok, 1),
                           jnp.repeat(inv_freq, 2).reshape(1, RET_DIM),
                           stages=[gather(["w_ffn1_gate", "w_ffn1_up", "w_ffn1_down"])])
    consts = _ret_consts()
    b_pad = jnp.pad(small["b_forget"], ((0, 0), (0, 128 - FOX_HEADS)))

    h1, n1, g1, u1 = _hosted(_ffn_fwd, xs, small["ln_ffn1"], full["w_ffn1_gate"], full["w_ffn1_up"],
                             full["w_ffn1_down"], stages=[gather(["w_in", "w_merge"])])
    w_in_full = jnp.pad(full["w_in"][:, :IN_COLS // N_CHIPS].reshape(IN_COLS, d), ((0, IN_PAD - IN_COLS), (0, 0)))
    u, rq, rk, rv, rg, fq, fk, fv, ffl, ga, gb = _hosted(
        _mix_in, h1, small["ln_mix"], w_in_full, full["w_merge"], small["b_merge"], cos_t, sin_t,
        stages=[gather(["w_ret_out", "w_fox_out", "w_out", "w_ple_gate", "w_ple"])])
    aq, ak = _forget_fwd(ffl, b_pad)
    y_raw, y_ret, states = _ret_fwd(rq, rk, rv, rg, consts)
    y_fox, y_fox32, lse_e = _hosted(_fox_fwd, fq, fk, fv, aq, ak,
                                    stages=[gather(["w_ffn2_gate", "w_ffn2_up", "w_ffn2_down"])])
    aqb = _fox_aug_lse(aq, lse_e)
    h2, za, zb, mix = _mix_out(h1, y_ret, y_fox, ga, gb, full["w_ret_out"], full["w_fox_out"], full["w_out"])
    h3, n2, g2, u2 = _ffn_fwd(h2, small["ln_ffn2"], full["w_ffn2_gate"], full["w_ffn2_up"], full["w_ffn2_down"])

    red = _Reducer()
    dh3, n3, dpgpre, dpe, pb, loss, dln_final, dln_ple = _tail(
        h3, ps, target, small["ln_ple"], small["ln_final"], full["w_ple_gate"], full["w_ple"])
    g_ple = dict(w_ple_gate=_wgrad_rows("wgrad_ple_gate", n3, dpgpre, N_CHIPS),
                 w_ple=_wgrad_cols("wgrad_ple", pb, dpe, N_CHIPS))

    sw_ple = red.swap(g_ple)
    dh2, dln_ffn2, dg2, du2, a2, dhb3 = _hosted(
        _ffn_bwd, dh3, h2, small["ln_ffn2"], g2, u2, full["w_ffn2_gate"], full["w_ffn2_up"], full["w_ffn2_down"],
        stages=[sw_ple])
    ex_ple = red.exchange(sw_ple.result)
    g_f2 = dict(w_ffn2_gate=_hosted(_wgrad_b_shared, "wgrad_ffn2_gate", dg2, n2, stages=[ex_ple]))
    g_f2["w_ffn2_up"] = _wgrad_b_shared("wgrad_ffn2_up", du2, n2)
    g_f2["w_ffn2_down"] = _wgrad_b_shared("wgrad_ffn2_down", a2, dhb3)

    sw_f2 = red.swap(g_f2)
    dhb2, dgpre, dza, dzb, dy_ret, dy_fox, ad, db_merge = _hosted(
        _mix_out_bwd, dh2, za, zb, ga, gb, y_fox32, full["w_ret_out"], full["w_fox_out"], full["w_out"],
        stages=[sw_f2, red.join(ex_ple.result)])
    g_br = dict(w_out=_wgrad_rows("wgrad_out", mix, dhb2, N_CHIPS),
                w_ret_out=_wgrad_cols("wgrad_ret_out", y_ret, dza, N_CHIPS),
                w_fox_out=_wgrad_cols("wgrad_fox_out", y_fox, dzb, N_CHIPS))

    sw_br = red.swap(g_br)
    drq, drk, drv, drg = _hosted(_ret_bwd, rq, rk, rv, rg, y_raw, dy_ret, states, consts, cos_t, sin_t,
                                 stages=[sw_br])
    ex_f2, ex_br = red.exchange(sw_f2.result), red.exchange(sw_br.result)
    dfq, dfk, dfv, dcum_t3, dcum_q = _hosted(_fox_bwd, fq, fk, fv, dy_fox, aqb, ak, ad, stages=[ex_f2, ex_br])
    dff, db_forget = _forget_bwd(dcum_t3.reshape(FOX_HEADS, t_tok), dcum_q, ffl, b_pad)
    dh1, dln_mix, dproj = _hosted(
        _mix_in_bwd, dh2, h1, small["ln_mix"], (drq, drk, drv, drg, dfq, dfk, dfv), dff, dgpre, w_in_full,
        full["w_merge"], stages=[red.join(ex_f2.result), red.join(ex_br.result)])

    g_in = _wgrad_rows("wgrad_in", dproj, u, IN_PAD // 512)
    g_in = g_in.reshape(IN_PAD, d)[:IN_COLS].reshape(N_CHIPS, IN_COLS // N_CHIPS, d)
    g_in = jnp.pad(g_in, ((0, 0), (0, IN_ROWS_PAD - IN_COLS // N_CHIPS), (0, 0)))
    sw_in = red.swap(dict(w_in=g_in))
    g_mrg = _hosted(_wgrad_cols, "wgrad_merge", u, dgpre, N_CHIPS, stages=[sw_in])

    sw_mrg, ex_in = red.swap(dict(w_merge=g_mrg)), red.exchange(sw_in.result)
    dx, dln_ffn1, dg1, du1, a1, dhb1 = _hosted(
        _ffn_bwd, dh1, xs, small["ln_ffn1"], g1, u1, full["w_ffn1_gate"], full["w_ffn1_up"], full["w_ffn1_down"],
        stages=[sw_mrg, ex_in])

    ex_mrg = red.exchange(sw_mrg.result)
    g_f1g = _hosted(_wgrad_b_shared, "wgrad_ffn1_gate", dg1, n1, stages=[ex_mrg, red.join(ex_in.result)])
    sw_f1g = red.swap(dict(w_ffn1_gate=g_f1g))
    g_f1u = _hosted(_wgrad_b_shared, "wgrad_ffn1_up", du1, n1, stages=[sw_f1g])
    ex_f1g, sw_f1u = red.exchange(sw_f1g.result), red.swap(dict(w_ffn1_up=g_f1u))
    g_f1d = _hosted(_wgrad_b_shared, "wgrad_ffn1_down", a1, dhb1,
                    stages=[ex_f1g, sw_f1u, red.join(ex_mrg.result)])

    small_grads = dict(ln_ffn1=dln_ffn1, ln_mix=dln_mix, b_forget=db_forget[:, :FOX_HEADS], b_merge=db_merge,
                       ln_ffn2=dln_ffn2, ln_ple=dln_ple, ln_final=dln_final)
    sizes = {n: w[n].size for n in SMALL}
    reduced = _all_reduce_small(_pack_small(small_grads, loss))
    gsum = _unpack_small(reduced, sizes)
    loss = reduced[0, 0]

    results = {}

    def update(n, stages=()):
        w2, m2, v2 = to2d(n, w[n]), to2d(n, m[n]), to2d(n, v[n])
        if n in gsum or n == "w_in":
            if n in gsum:
                g2 = gsum[n]
            else:
                mine, other = red.done[n]
                g2 = jnp.where(core == 0, jnp.concatenate([mine, other]), jnp.concatenate([other, mine]))
                g2 = g2[:w2.shape[0]]
            dl, nm, nv = _hosted(_adamw, w2, g2, m2, v2, stages=stages)
        else:
            g2, dl, nm, nv = _hosted(_adamw_halves, w2, *red.done[n], m2, v2, stages=stages)
        results[n] = tuple(from2d(n, a) for a in (g2, dl, nm, nv))

    ex_f1u, sw_f1d = red.exchange(sw_f1u.result), red.swap(dict(w_ffn1_down=g_f1d))
    update("w_ffn2_gate", stages=[ex_f1u, sw_f1d, red.join(ex_f1g.result)])
    ex_f1d = red.exchange(sw_f1d.result)
    update("w_ffn2_up", stages=[ex_f1d, red.join(ex_f1u.result)])
    update("w_ffn2_down", stages=[red.join(ex_f1d.result)])
    for n in WEIGHTS:
        if n not in results:
            update(n)

    outs = [[results[n][k] for n in WEIGHTS] for k in range(4)]
    return (loss, dx[None], *outs[0], *outs[1], *outs[2], *outs[3])
```

```python
import functools
import operator

import jax
import jax.numpy as jnp
from jax import lax
from jax.experimental import pallas as pl
from jax.experimental.pallas import tpu as pltpu

F32 = jnp.float32
BF = jnp.bfloat16
MESH = pl.DeviceIdType.MESH

EPS = 1e-6
ROPE_BASE = 10000.0
N_CHIPS = 4
RET_HEADS = 4
RET_DIM = 128
RET_WIDTH = RET_HEADS * RET_DIM
RET_CHUNK = 128
RET_SCALE = RET_DIM ** -0.5
FOX_HEADS = 8
FOX_DIM = 64
FOX_WIDTH = FOX_HEADS * FOX_DIM
FOX_SCALE = FOX_DIM ** -0.5
IN_COLS = 4 * RET_WIDTH + 3 * FOX_WIDTH + FOX_HEADS
IN_PAD = 4096
FF_COL = 4 * RET_WIDTH + 3 * FOX_WIDTH
NEG = -1e30

ADAM_LR = 0.001
ADAM_B1 = 0.9
ADAM_B2 = 0.999
ADAM_EPS = 1e-08
ADAM_WD = 0.01
ADAM_STEP = 10

VMEM_LIMIT = 52 * 1024 * 1024

NT = (((1,), (1,)), ((), ()))
TN = (((0,), (0,)), ((), ()))

HBM_SPEC = pl.BlockSpec(memory_space=pltpu.HBM)
VMEM_SPEC = pl.BlockSpec(memory_space=pltpu.VMEM)


def _dot(a, b):
    return jnp.dot(a, b, preferred_element_type=F32)


def _dot_nt(a, b):
    return lax.dot_general(a, b, NT, preferred_element_type=F32)


def _dot_tn(a, b):
    return lax.dot_general(a, b, TN, preferred_element_type=F32)


def _rstd(xv):
    return lax.rsqrt(jnp.mean(xv * xv, axis=-1, keepdims=True) + EPS)


def _rms_bwd(dn, xv, r, ln):
    xh = xv * r
    dxh = dn * ln
    dx = r * (dxh - xh * jnp.mean(dxh * xh, axis=-1, keepdims=True))
    return dx, jnp.sum(dn * xh, axis=0, keepdims=True)


def _sigmoid(x):
    return jax.nn.sigmoid(x)


def _tile(n, pref):
    return pref if n % pref == 0 else n


def _row_tile(n, cap):
    best = [t for t in range(16, min(n, cap) + 1, 16) if n % t == 0]
    return best[-1] if best else n


class _Comm:
    def __init__(self, ins, out_shapes, sems, start, wait, aliases=None):
        self.ins, self.out_shapes, self.sems, self.start, self.wait = list(ins), list(out_shapes), list(sems), start, wait
        self.aliases = dict(aliases or {})


def _merge(comms):
    comms = [c for c in comms if c is not None]
    if not comms:
        return None
    bounds, ni, no, ns = [], 0, 0, 0
    for c in comms:
        bounds.append((ni, no, ns))
        ni, no, ns = ni + len(c.ins), no + len(c.out_shapes), ns + len(c.sems)

    def run(which):
        def f(ins, outs, sems):
            for c, (i, o, s) in zip(comms, bounds):
                getattr(c, which)(ins[i:i + len(c.ins)], outs[o:o + len(c.out_shapes)], sems[s:s + len(c.sems)])
        return f

    aliases = {i + a: o + b for c, (i, o, _) in zip(comms, bounds) for a, b in c.aliases.items()}
    return _Comm([a for c in comms for a in c.ins], [a for c in comms for a in c.out_shapes],
                 [a for c in comms for a in c.sems], run("start"), run("wait"), aliases)


def _split_outs(comms, outs):
    res, o = [], 0
    for c in comms:
        if c is not None:
            res.append(list(outs[o:o + len(c.out_shapes)]))
            o += len(c.out_shapes)
    return res


def _pcall(body, args, *, name, out_shape, grid=(), in_specs=None, out_specs=None, scratch=(), comm=None,
           prefetch=()):
    many = isinstance(out_shape, (list, tuple))
    outs = list(out_shape) if many else [out_shape]
    n_pre, n_in, n_out, n_scr = len(prefetch), len(args), len(outs), len(scratch)
    if in_specs is None:
        in_specs, out_specs = [VMEM_SPEC] * n_in, [VMEM_SPEC] * n_out
    else:
        in_specs, out_specs = list(in_specs), (list(out_specs) if many else [out_specs])
    params = pltpu.CompilerParams(dimension_semantics=("arbitrary",) * len(grid), vmem_limit_bytes=VMEM_LIMIT)
    scalars = [jnp.reshape(s, (1,)).astype(jnp.int32) for s in prefetch]
    ci, co = (len(comm.ins), len(comm.out_shapes)) if comm is not None else (0, 0)

    def wrapped(*refs):
        pre, refs = refs[:n_pre], refs[n_pre:]
        a, ca = refs[:n_in], refs[n_in:n_in + ci]
        o = refs[n_in + ci:n_in + ci + n_out]
        cout = refs[n_in + ci + n_out:n_in + ci + n_out + co]
        s = refs[n_in + ci + n_out + co:n_in + ci + n_out + co + n_scr]
        csem = refs[n_in + ci + n_out + co + n_scr:]
        if comm is None:
            body(*pre, *a, *o, *s)
        elif grid:
            first = functools.reduce(operator.and_, [pl.program_id(k) == 0 for k in range(len(grid))])
            last = functools.reduce(operator.and_, [pl.program_id(k) == grid[k] - 1 for k in range(len(grid))])
            pl.when(first)(lambda: comm.start(ca, cout, csem))
            body(*pre, *a, *o, *s)
            pl.when(last)(lambda: comm.wait(ca, cout, csem))
        else:
            comm.start(ca, cout, csem)
            body(*pre, *a, *o, *s)
            comm.wait(ca, cout, csem)

    c_ins, c_outs, c_sems, aliases = ([], [], [], {}) if comm is None else (
        comm.ins, comm.out_shapes, comm.sems, {n_pre + n_in + i: n_out + o for i, o in comm.aliases.items()})
    all_in, all_out = in_specs + [HBM_SPEC] * ci, out_specs + [HBM_SPEC] * co
    all_scr = list(scratch) + c_sems
    if n_pre:
        spec = dict(grid_spec=pltpu.PrefetchScalarGridSpec(
            num_scalar_prefetch=n_pre, grid=grid, in_specs=all_in, out_specs=all_out, scratch_shapes=all_scr))
    else:
        spec = dict(grid=grid, in_specs=all_in, out_specs=all_out, scratch_shapes=all_scr)
    res = pl.pallas_call(wrapped, name=name, out_shape=outs + c_outs, input_output_aliases=aliases,
                         compiler_params=params, **spec)(*scalars, *args, *c_ins)
    mine = list(res[:n_out])
    mine = mine if many else mine[0]
    return mine if comm is None else (mine, list(res[n_out:]))


def _peer_chips(x, y):
    return [(1 - x, y), (x, 1 - y), (1 - x, 1 - y)]


def _c_all_gather(bufs):
    n = len(bufs)

    def copies(ins, outs, sems):
        send_sems, recv_sems, fwd_send, fwd_recv = sems
        x, y, c = lax.axis_index("x"), lax.axis_index("y"), lax.axis_index("c")
        me = 2 * x + y
        peers = _peer_chips(x, y)
        chip = [2 * px + py for px, py in peers]

        def ici(g, j, slot):
            return pltpu.make_async_remote_copy(
                src_ref=outs[g].at[me, c], dst_ref=outs[g].at[slot, c], send_sem=send_sems.at[g, j],
                recv_sem=recv_sems.at[g, j], device_id=(*peers[j], c), device_id_type=MESH)

        def d2d(g, j, half):
            return pltpu.make_async_remote_copy(
                src_ref=outs[g].at[chip[j], half], dst_ref=outs[g].at[chip[j], half], send_sem=fwd_send.at[g, j],
                recv_sem=fwd_recv.at[g, j], device_id=(x, y, 1 - c), device_id_type=MESH)

        pairs = [(g, j) for g in range(n) for j in range(3)]
        sends = [ici(g, j, me) for g, j in pairs]
        recvs = [ici(g, j, chip[j]) for g, j in pairs]
        passes = [d2d(g, j, c) for g, j in pairs]
        passed = [d2d(g, j, 1 - c) for g, j in pairs]
        return sends, recvs, passes, passed

    def start(ins, outs, sems):
        for cp in copies(ins, outs, sems)[0]:
            cp.start()

    def wait(ins, outs, sems):
        sends, recvs, passes, passed = copies(ins, outs, sems)
        for rcv, fwd in zip(recvs, passes):
            rcv.wait_recv()
            fwd.start()
        for cp in passed:
            cp.wait_recv()
        for cp in sends + passes:
            cp.wait_send()

    pair_sems = pltpu.SemaphoreType.DMA((n, 3))
    return _Comm(bufs, [jax.ShapeDtypeStruct(s.shape, s.dtype) for s in bufs], [pair_sems] * 4, start, wait,
                 aliases={g: g for g in range(n)})


def _start_wait(copies):
    def start(ins, outs, sems):
        local, sends, _ = copies(ins, outs, sems)
        for cp in local + sends:
            cp.start()

    def wait(ins, outs, sems):
        local, sends, recvs = copies(ins, outs, sems)
        for cp in recvs:
            cp.wait_recv()
        for cp in sends:
            cp.wait_send()
        for cp in local:
            cp.wait()

    return start, wait


def _c_half_swap(grads):
    n = len(grads)

    def copies(ins, outs, sems):
        send_sems, recv_sems = sems
        x, y, c = lax.axis_index("x"), lax.axis_index("y"), lax.axis_index("c")
        sends = []
        for g in range(n):
            half = ins[g].shape[1] // 2
            sends.append(pltpu.make_async_remote_copy(
                src_ref=ins[g].at[:, pl.ds((1 - c) * half, half), :], dst_ref=outs[g],
                send_sem=send_sems.at[g], recv_sem=recv_sems.at[g], device_id=(x, y, 1 - c), device_id_type=MESH))
        return [], sends, sends

    return _Comm(
        grads, [jax.ShapeDtypeStruct((N_CHIPS, s.shape[1] // 2, s.shape[2]), s.dtype) for s in grads],
        [pltpu.SemaphoreType.DMA((n,)), pltpu.SemaphoreType.DMA((n,))], *_start_wait(copies))


def _c_chip_exchange(parts):
    n = len(parts)

    def copies(ins, outs, sems):
        send_sems, recv_sems = sems
        x, y, c = lax.axis_index("x"), lax.axis_index("y"), lax.axis_index("c")
        peers = _peer_chips(x, y)

        def remote(g, j):
            return pltpu.make_async_remote_copy(
                src_ref=ins[g].at[2 * peers[j][0] + peers[j][1]], dst_ref=outs[g].at[j],
                send_sem=send_sems.at[g, j], recv_sem=recv_sems.at[g, j], device_id=(*peers[j], c),
                device_id_type=MESH)

        sends = [remote(g, j) for g in range(n) for j in range(3)]
        return [], sends, sends

    return _Comm(
        parts, [jax.ShapeDtypeStruct((3,) + s.shape[1:], s.dtype) for s in parts],
        [pltpu.SemaphoreType.DMA((n, 3)), pltpu.SemaphoreType.DMA((n, 3))], *_start_wait(copies))


def _c_join(halves):
    n = len(halves)

    def copies(ins, outs, sems):
        send_sems, recv_sems = sems
        x, y, c = lax.axis_index("x"), lax.axis_index("y"), lax.axis_index("c")
        sends = [pltpu.make_async_remote_copy(
            src_ref=ins[g], dst_ref=outs[g], send_sem=send_sems.at[g], recv_sem=recv_sems.at[g],
            device_id=(x, y, 1 - c), device_id_type=MESH) for g in range(n)]
        return [], sends, sends

    return _Comm(
        halves, [jax.ShapeDtypeStruct(s.shape, s.dtype) for s in halves],
        [pltpu.SemaphoreType.DMA((n,)), pltpu.SemaphoreType.DMA((n,))], *_start_wait(copies))


def _all_reduce_small(v):
    rows = v.shape[0]

    def body(v_ref, out_ref, buf, send_sems, recv_sems):
        x, y, c = lax.axis_index("x"), lax.axis_index("y"), lax.axis_index("c")
        me = 4 * x + 2 * y + c
        buf[me] = v_ref[...]
        flips = [(fx, fy, fc) for fx in (0, 1) for fy in (0, 1) for fc in (0, 1)][1:]

        def peer(k):
            fx, fy, fc = flips[k]
            px, py, pc = x ^ fx, y ^ fy, c ^ fc
            return (px, py, pc), 4 * px + 2 * py + pc

        def copy(k, slot):
            return pltpu.make_async_remote_copy(
                src_ref=buf.at[slot], dst_ref=buf.at[slot], send_sem=send_sems.at[k],
                recv_sem=recv_sems.at[k], device_id=peer(k)[0], device_id_type=MESH)

        sends = [copy(k, me) for k in range(7)]
        for cp in sends:
            cp.start()
        for k in range(7):
            copy(k, peer(k)[1]).wait_recv()
        for cp in sends:
            cp.wait_send()
        acc = buf[0]
        for d in range(1, 8):
            acc = acc + buf[d]
        out_ref[...] = acc

    return _pcall(body, [v], name="all_reduce_small", out_shape=jax.ShapeDtypeStruct((rows, 128), F32),
                  scratch=[pltpu.VMEM((8, rows, 128), F32), pltpu.SemaphoreType.DMA((7,)),
                           pltpu.SemaphoreType.DMA((7,))])


def _add_halves(g, got):
    _, h, c = got.shape
    th = _row_tile(h, 512)
    nh = h // th
    half = lax.axis_index("c") * nh

    def body(h_ref, a_ref, b_ref, o_ref):
        o_ref[...] = (a_ref[...].astype(F32) + b_ref[...].astype(F32)).astype(o_ref.dtype)

    spec = pl.BlockSpec((1, th, c), lambda j, i, h_ref: (j, i, 0))
    mine = pl.BlockSpec((1, th, c), lambda j, i, h_ref: (j, h_ref[0] + i, 0))
    return _pcall(body, [g, got], name="add_halves", grid=(N_CHIPS, nh), prefetch=[half],
                  out_shape=jax.ShapeDtypeStruct(got.shape, BF), in_specs=[mine, spec], out_specs=spec)


def _sum_chips(parts, recv):
    _, h, c = parts.shape
    th = _row_tile(h, 512)
    me = 2 * lax.axis_index("x") + lax.axis_index("y")

    def body(me_ref, p_ref, r_ref, o_ref):
        acc = p_ref[0].astype(F32)
        for s in range(N_CHIPS - 1):
            acc = acc + r_ref[s].astype(F32)
        o_ref[...] = acc

    return _pcall(body, [parts, recv], name="sum_chips", grid=(h // th,), prefetch=[me],
                  out_shape=jax.ShapeDtypeStruct((h, c), F32),
                  in_specs=[pl.BlockSpec((1, th, c), lambda i, me_ref: (me_ref[0], i, 0)),
                            pl.BlockSpec((N_CHIPS - 1, th, c), lambda i, me_ref: (0, i, 0))],
                  out_specs=pl.BlockSpec((th, c), lambda i, me_ref: (i, 0)))


def _adam_update(w, gv, m, v, d_ref, nm_ref, nv_ref):
    c1 = 1.0 / (1.0 - ADAM_B1 ** ADAM_STEP)
    c2 = 1.0 / (1.0 - ADAM_B2 ** ADAM_STEP)
    nm = ADAM_B1 * m + (1.0 - ADAM_B1) * gv
    nv = ADAM_B2 * v + (1.0 - ADAM_B2) * (gv * gv)
    nm_ref[...] = nm
    nv_ref[...] = nv
    d_ref[...] = -ADAM_LR * ((nm * c1) / (jnp.sqrt(nv * c2) + ADAM_EPS) + ADAM_WD * w)


def _adamw(w, g, m, v, comm=None):
    r, c = w.shape
    tr = _row_tile(r, 512)

    def body(w_ref, g_ref, m_ref, v_ref, d_ref, nm_ref, nv_ref):
        _adam_update(w_ref[...], g_ref[...], m_ref[...], v_ref[...], d_ref, nm_ref, nv_ref)

    spec = pl.BlockSpec((tr, c), lambda i: (i, 0))
    sds = jax.ShapeDtypeStruct((r, c), F32)
    return _pcall(body, [w, g, m, v], name="adamw", grid=(r // tr,), out_shape=[sds, sds, sds],
                  in_specs=[spec] * 4, out_specs=[spec] * 3, comm=comm)


def _adamw_halves(items, comm=None):
    k = len(items)
    r, c = items[0][0].shape
    h = r // 2
    tr = _row_tile(h, min(512, (VMEM_LIMIT * 3 // 4) // (k * 9 * 2 * 4 * c)))
    nb = h // tr
    core = lax.axis_index("c")

    def body(c_ref, *refs):
        ins, outs = refs[:5 * k], refs[5 * k:]
        for q in range(k):
            w_ref, gm_ref, go_ref, m_ref, v_ref = ins[5 * q:5 * q + 5]
            g_ref, d_ref, nm_ref, nv_ref = outs[4 * q:4 * q + 4]
            gv = jnp.where(pl.program_id(0) == c_ref[0], gm_ref[...], go_ref[...])
            g_ref[...] = gv
            _adam_update(w_ref[...], gv, m_ref[...], v_ref[...], d_ref, nm_ref, nv_ref)

    full = pl.BlockSpec((tr, c), lambda hh, i, c_ref: (hh * nb + i, 0))
    half = pl.BlockSpec((tr, c), lambda hh, i, c_ref: (i, 0))
    sds = jax.ShapeDtypeStruct((r, c), F32)
    return _pcall(body, [a for it in items for a in it], name="adamw_halves", grid=(2, nb), prefetch=[core],
                  out_shape=[sds] * (4 * k), in_specs=[full, half, half, full, full] * k, out_specs=[full] * (4 * k),
                  comm=comm)


def _wgrad(name, a, b, a_spec, b_spec, m, n, nb, comm):
    def body(a_ref, b_ref, o_ref):
        o_ref[...] = _dot_tn(a_ref[...], b_ref[...]).astype(o_ref.dtype)

    return _pcall(body, [a, b], name=name, grid=(nb,), out_shape=jax.ShapeDtypeStruct((nb, m, n), BF),
                  in_specs=[a_spec, b_spec], out_specs=pl.BlockSpec((None, m, n), lambda j: (j, 0, 0)), comm=comm)


def _wgrad_cols(name, a, b, nb, comm=None):
    t_tok, m = a.shape
    n = b.shape[1] // nb
    return _wgrad(name, a, b, pl.BlockSpec((t_tok, m), lambda j: (0, 0)), pl.BlockSpec((t_tok, n), lambda j: (0, j)),
                  m, n, nb, comm)


def _wgrad_rows(name, a, b, nb, comm=None):
    t_tok, n = b.shape
    m = a.shape[1] // nb
    return _wgrad(name, a, b, pl.BlockSpec((t_tok, m), lambda j: (0, j)), pl.BlockSpec((t_tok, n), lambda j: (0, 0)),
                  m, n, nb, comm)


def _wgrad_a_shared(name, a, b4, comm=None):
    t_tok, m = a.shape
    nb, _, n = b4.shape
    return _wgrad(name, a, b4, pl.BlockSpec((t_tok, m), lambda j: (0, 0)),
                  pl.BlockSpec((None, t_tok, n), lambda j: (j, 0, 0)), m, n, nb, comm)


def _wgrad_b_shared(name, a4, b, comm=None):
    nb, t_tok, m = a4.shape
    n = b.shape[1]
    return _wgrad(name, a4, b, pl.BlockSpec((None, t_tok, m), lambda j: (j, 0, 0)),
                  pl.BlockSpec((t_tok, n), lambda j: (0, 0)), m, n, nb, comm)


def _w4_spec(r, c):
    return pl.BlockSpec((None, r, c), lambda i, j: (j, 0, 0))


FFN_ROW_CHUNK = 256


def _row_chunks(tm):
    rc = FFN_ROW_CHUNK if tm % FFN_ROW_CHUNK == 0 else tm
    return [slice(r, r + rc) for r in range(0, tm, rc)]


def _ffn_fwd(h, ln, wg4, wu4, wd4, comm=None):
    t_tok, d = h.shape
    f = wg4.shape[-2]
    tm = _tile(t_tok, 512)

    def body(h_ref, ln_ref, wg_ref, wu_ref, wd_ref, ho_ref, n_ref, g_ref, u_ref, n_s, acc):
        j = pl.program_id(1)

        @pl.when(j == 0)
        def _():
            xv = h_ref[...]
            nv = (xv * _rstd(xv) * ln_ref[...]).astype(BF)
            n_s[...] = nv
            n_ref[...] = nv
            acc[...] = jnp.zeros_like(acc)

        nv = n_s[...]
        g = _dot_nt(nv, wg_ref[...])
        u = _dot_nt(nv, wu_ref[...])
        g_ref[...] = g.astype(BF)
        u_ref[...] = u.astype(BF)
        a = (g * _sigmoid(g) * u).astype(BF)
        acc[...] += _dot(a, wd_ref[...])

        @pl.when(j == N_CHIPS - 1)
        def _():
            ho_ref[...] = h_ref[...] + 0.5 * acc[...]

    row = pl.BlockSpec((tm, d), lambda i, j: (i, 0))
    gu = pl.BlockSpec((None, tm, f), lambda i, j: (j, i, 0))
    gu_sds = jax.ShapeDtypeStruct((N_CHIPS, t_tok, f), BF)
    return _pcall(
        body, [h, ln, wg4, wu4, wd4], name="ffn_fwd", grid=(t_tok // tm, N_CHIPS),
        out_shape=[jax.ShapeDtypeStruct((t_tok, d), F32), jax.ShapeDtypeStruct((t_tok, d), BF), gu_sds, gu_sds],
        in_specs=[row, pl.BlockSpec((1, d), lambda i, j: (0, 0)), _w4_spec(f, d), _w4_spec(f, d), _w4_spec(f, d)],
        out_specs=[row, row, gu, gu],
        scratch=[pltpu.VMEM((tm, d), BF), pltpu.VMEM((tm, d), F32)], comm=comm)


def _ffn_bwd(dho, h, ln, g4, u4, wg4, wu4, wd4, comm=None):
    t_tok, d = h.shape
    f = wg4.shape[-2]
    tm = _tile(t_tok, 512)

    def body(dho_ref, h_ref, ln_ref, g_ref, u_ref, wg_ref, wu_ref, wd_ref,
             dhi_ref, dln_ref, dg_ref, du_ref, a_ref, dhb_ref, dhb_s, dn_acc):
        i, j = pl.program_id(0), pl.program_id(1)

        @pl.when(j == 0)
        def _():
            dhb = (0.5 * dho_ref[...]).astype(BF)
            dhb_s[...] = dhb
            dhb_ref[...] = dhb
            dn_acc[...] = jnp.zeros_like(dn_acc)

        @pl.when((i == 0) & (j == 0))
        def _():
            dln_ref[...] = jnp.zeros_like(dln_ref)

        for rows in _row_chunks(tm):
            g = g_ref[rows, :].astype(F32)
            u = u_ref[rows, :].astype(F32)
            s = _sigmoid(g)
            sg = g * s
            a_ref[rows, :] = (sg * u).astype(BF)
            da = _dot_nt(dhb_s[rows, :], wd_ref[...])
            dg = (da * u * (s * (1.0 + g * (1.0 - s)))).astype(BF)
            du = (da * sg).astype(BF)
            dg_ref[rows, :] = dg
            du_ref[rows, :] = du
            dn_acc[rows, :] += _dot(dg, wg_ref[...]) + _dot(du, wu_ref[...])

        @pl.when(j == N_CHIPS - 1)
        def _():
            xv = h_ref[...]
            dx, dln = _rms_bwd(dn_acc[...], xv, _rstd(xv), ln_ref[...])
            dln_ref[...] += dln
            dhi_ref[...] = dho_ref[...] + dx

    row = pl.BlockSpec((tm, d), lambda i, j: (i, 0))
    vec = pl.BlockSpec((1, d), lambda i, j: (0, 0))
    gu = pl.BlockSpec((None, tm, f), lambda i, j: (j, i, 0))
    gu_sds = jax.ShapeDtypeStruct((N_CHIPS, t_tok, f), BF)
    return _pcall(
        body, [dho, h, ln, g4, u4, wg4, wu4, wd4], name="ffn_bwd", grid=(t_tok // tm, N_CHIPS),
        out_shape=[jax.ShapeDtypeStruct((t_tok, d), F32), jax.ShapeDtypeStruct((1, d), F32),
                   gu_sds, gu_sds, gu_sds, jax.ShapeDtypeStruct((t_tok, d), BF)],
        in_specs=[row, row, vec, gu, gu, _w4_spec(f, d), _w4_spec(f, d), _w4_spec(f, d)],
        out_specs=[row, vec, gu, gu, gu, row],
        scratch=[pltpu.VMEM((tm, d), BF), pltpu.VMEM((tm, d), F32)], comm=comm)


def _rope_tables(pos_col, inv_freq2, comm=None):
    t_tok = pos_col.shape[0]

    def body(p_ref, f_ref, cos_ref, sin_ref):
        ang = p_ref[...] * f_ref[...]
        lane = lax.broadcasted_iota(jnp.int32, ang.shape, 1)
        s = jnp.sin(ang)
        cos_ref[...] = jnp.cos(ang)
        sin_ref[...] = jnp.where((lane & 1) == 0, -s, s)

    sds = jax.ShapeDtypeStruct((t_tok, 128), F32)
    return _pcall(body, [pos_col, inv_freq2], name="rope_tables", out_shape=[sds, sds], comm=comm)


def _swap_pairs(x):
    lane = lax.broadcasted_iota(jnp.int32, x.shape, 1)
    return jnp.where((lane & 1) == 0, pltpu.roll(x, 127, 1), pltpu.roll(x, 1, 1))


def _mix_in(h, ln, w_in, wm4, b_m, cos_t, sin_t, comm=None):
    t_tok, d = h.shape
    cm = wm4.shape[-1]
    tm = _tile(t_tok, 256)

    def body(h_ref, ln_ref, win_ref, wm_ref, bm_ref, cos_ref, sin_ref,
             u_ref, rq_ref, rk_ref, rv_ref, rg_ref, fq_ref, fk_ref, fv_ref, ff_ref, ga_ref, gb_ref):
        xv = h_ref[...]
        ub = (xv * _rstd(xv) * ln_ref[...]).astype(BF)
        u_ref[...] = ub
        cosv, sinv = cos_ref[...], sin_ref[...]

        def sec(k):
            return _dot_nt(ub, win_ref[k * 512:(k + 1) * 512, :])

        def rot(xh):
            return xh * cosv + _swap_pairs(xh) * sinv

        pq, pk = sec(0), sec(1)
        for hh in range(RET_HEADS):
            sl = slice(hh * RET_DIM, (hh + 1) * RET_DIM)
            rq_ref[:, sl] = rot(pq[:, sl]).astype(BF)
            rk_ref[:, sl] = (rot(pk[:, sl]) * RET_SCALE).astype(BF)
        rv_ref[...] = sec(2).astype(BF)
        rg_ref[...] = sec(3).astype(BF)
        fq_ref[...] = (sec(4) * FOX_SCALE).astype(BF)
        fk_ref[...] = sec(5).astype(BF)
        fv_ref[...] = sec(6).astype(BF)
        ff_ref[...] = _dot_nt(ub, win_ref[FF_COL:FF_COL + 128, :])
        for j in range(N_CHIPS):
            gs = _sigmoid(_dot(ub, wm_ref[j]) + bm_ref[:, j * cm:(j + 1) * cm]).astype(BF)
            col = j * cm
            if col < d:
                ga_ref[:, col:col + cm] = gs
            else:
                gb_ref[:, col - d:col - d + cm] = gs

    row = lambda c: pl.BlockSpec((tm, c), lambda i: (i, 0))
    full = lambda *s: pl.BlockSpec(s, lambda i: (0,) * len(s))
    sds = lambda c, dt: jax.ShapeDtypeStruct((t_tok, c), dt)
    return _pcall(
        body, [h, ln, w_in, wm4, b_m, cos_t, sin_t], name="mix_in", grid=(t_tok // tm,),
        out_shape=[sds(d, BF)] + [sds(512, BF)] * 7 + [sds(128, F32), sds(d, BF), sds(d, BF)],
        in_specs=[row(d), full(1, d), full(IN_PAD, d), full(N_CHIPS, d, cm), full(1, 2 * d), row(128), row(128)],
        out_specs=[row(d)] + [row(512)] * 7 + [row(128), row(d), row(d)], comm=comm)


def _split3(x):
    hi = x.astype(BF)
    r1 = x - hi.astype(F32)
    mid = r1.astype(BF)
    lo = (r1 - mid.astype(F32)).astype(BF)
    return hi, mid, lo


def _aug_lane():
    return lax.broadcasted_iota(jnp.int32, (1, 128), 1) & (FOX_DIM - 1)


def _aug_put(base, k0, parts):
    w = _aug_lane()
    for i, part in enumerate(parts):
        base = jnp.where(w == k0 + i, part, base)
    return base


def _forget_fwd(ffl, b_pad):
    t_tok = ffl.shape[0]
    tb = _tile(t_tok, 256)

    def body(ff_ref, b_ref, aq_ref, ak_ref, cum_s):
        r = lax.broadcasted_iota(jnp.int32, (tb, tb), 0)
        c = lax.broadcasted_iota(jnp.int32, (tb, tb), 1)
        tri = jnp.where(c <= r, 1.0, 0.0).astype(BF)
        carry = jnp.zeros((1, 128), F32)
        for i in range(t_tok // tb):
            z = ff_ref[i * tb:(i + 1) * tb, :] + b_ref[...]
            lf = jnp.minimum(z, 0.0) - jnp.log(1.0 + jnp.exp(-jnp.abs(z)))
            hi, mid, lo = _split3(lf)
            cs = _dot(tri, hi) + _dot(tri, mid) + _dot(tri, lo) + carry
            cum_s[i * tb:(i + 1) * tb, :] = cs
            carry = cs[tb - 1:tb, :]
        x = cum_s[...]
        first = lax.broadcasted_iota(jnp.int32, (1, 128), 1) < FOX_DIM
        w = _aug_lane()
        one = jnp.ones((t_tok, 128), BF)
        zero = jnp.zeros((t_tok, 128), BF)
        for pp in range(FOX_HEADS // 2):
            other = jnp.where(first, x[:, 2 * pp + 1:2 * pp + 2], x[:, 2 * pp:2 * pp + 1])
            parts = _split3(other)
            aq = jnp.where((w >= 3) & (w < 6), one, zero)
            ak = jnp.where((w < 3) | ((w >= 6) & (w < 9)), one, zero)
            aq_ref[:, pp * 128:(pp + 1) * 128] = _aug_put(aq, 0, parts)
            ak_ref[:, pp * 128:(pp + 1) * 128] = _aug_put(ak, 3, [-q for q in parts])

    sds = jax.ShapeDtypeStruct((t_tok, FOX_WIDTH), BF)
    return _pcall(body, [ffl, b_pad], name="forget_fwd", out_shape=[sds, sds],
                  scratch=[pltpu.VMEM((t_tok, 128), F32)])


def _fox_aug_lse(aq, lse_e):
    t_tok = aq.shape[0]
    tm = _tile(t_tok, 512)

    def body(aq_ref, lse_ref, o_ref):
        for pp in range(FOX_HEADS // 2):
            sl = slice(pp * 128, (pp + 1) * 128)
            other = pltpu.roll(lse_ref[:, sl], FOX_DIM, 1)
            o_ref[:, sl] = _aug_put(aq_ref[:, sl], 6, _split3(-other))

    spec = pl.BlockSpec((tm, FOX_WIDTH), lambda i: (i, 0))
    return _pcall(body, [aq, lse_e], name="fox_aug_lse", grid=(t_tok // tm,),
                  out_shape=jax.ShapeDtypeStruct((t_tok, FOX_WIDTH), BF), in_specs=[spec, spec], out_specs=spec)


def _forget_bwd(dcum_t, dcum_q, ffl, b_pad):
    t_tok = ffl.shape[0]
    tb = _tile(t_tok, 256)

    def body(dc_ref, dq_ref, ff_ref, b_ref, dff_ref, db_ref, pad_s, d_s):
        pad_s[...] = jnp.zeros_like(pad_s)
        pad_s[0:FOX_HEADS, :] = dc_ref[...]
        dsum = pad_s[...].T
        lane = lax.broadcasted_iota(jnp.int32, (t_tok, 128), 1)
        for hh in range(FOX_HEADS):
            dsum = dsum + jnp.where(lane == hh, dq_ref[:, hh * FOX_DIM:hh * FOX_DIM + 1], 0.0)
        d_s[...] = dsum
        r = lax.broadcasted_iota(jnp.int32, (tb, tb), 0)
        c = lax.broadcasted_iota(jnp.int32, (tb, tb), 1)
        tri = jnp.where(c >= r, 1.0, 0.0).astype(BF)
        carry = jnp.zeros((1, 128), F32)
        db = jnp.zeros((1, 128), F32)
        for i in reversed(range(t_tok // tb)):
            hi, mid, lo = _split3(d_s[i * tb:(i + 1) * tb, :])
            dlf = _dot(tri, hi) + _dot(tri, mid) + _dot(tri, lo) + carry
            carry = dlf[0:1, :]
            z = ff_ref[i * tb:(i + 1) * tb, :] + b_ref[...]
            dff = dlf * _sigmoid(-z)
            dff_ref[i * tb:(i + 1) * tb, :] = dff.astype(BF)
            db = db + jnp.sum(dff, axis=0, keepdims=True)
        db_ref[...] = db

    return _pcall(
        body, [dcum_t, dcum_q, ffl, b_pad], name="forget_bwd",
        out_shape=[jax.ShapeDtypeStruct((t_tok, 128), BF), jax.ShapeDtypeStruct((1, 128), F32)],
        scratch=[pltpu.VMEM((128, t_tok), F32), pltpu.VMEM((t_tok, 128), F32)])


def _first_half():
    return lax.broadcasted_iota(jnp.int32, (1, 128), 1) < FOX_DIM


def _head_rows(x2, a2, hh):
    return jnp.where(_first_half(), x2, a2) if hh == 0 else jnp.where(_first_half(), a2, x2)


def _head_only(x2, hh):
    zero = jnp.zeros_like(x2)
    return jnp.where(_first_half(), x2, zero) if hh == 0 else jnp.where(_first_half(), zero, x2)


def _causal_diag(s):
    rows = lax.broadcasted_iota(jnp.int32, s.shape, 0)
    cols = lax.broadcasted_iota(jnp.int32, s.shape, 1)
    return jnp.where(cols <= rows, s, NEG)


def _diag_or_below(qi, ki, step):
    pl.when(ki < qi)(lambda: step(False))
    pl.when(ki == qi)(lambda: step(True))


def _tri_rows(s, n):
    qi = sum((s >= r * (r + 1) // 2).astype(jnp.int32) for r in range(1, n))
    return qi, s - (qi * (qi + 1)) // 2


def _tri_cols(s, n):
    ki = sum((s >= k * n - k * (k - 1) // 2).astype(jnp.int32) for k in range(1, n))
    return ki, ki + s - (ki * n - (ki * (ki - 1)) // 2)


def _fox_fwd(fq, fk, fv, aq, ak, comm=None):
    t_tok = fq.shape[0]
    t = _tile(t_tok, 512)
    nq = t_tok // t
    npair = FOX_HEADS // 2

    def body(q_ref, k_ref, v_ref, aq_ref, ak_ref, o_ref, of_ref, lse_ref, m_s, l_s, acc_s):
        qi, ki = _tri_rows(pl.program_id(1), nq)

        @pl.when(ki == 0)
        def _():
            m_s[...] = jnp.full_like(m_s, NEG)
            l_s[...] = jnp.zeros_like(l_s)
            acc_s[...] = jnp.zeros_like(acc_s)

        def step(diag):
            q2, k2, v2, aq2, ak2 = q_ref[...], k_ref[...], v_ref[...], aq_ref[...], ak_ref[...]
            for hh in range(2):
                s = _dot_nt(_head_rows(q2, aq2, hh), _head_rows(k2, ak2, hh))
                if diag:
                    s = _causal_diag(s)
                m_prev = m_s[hh]
                m_new = jnp.maximum(m_prev, jnp.max(s, axis=1, keepdims=True))
                alpha = jnp.exp(m_prev - m_new)
                p = jnp.exp(s - jnp.tile(m_new, (1, t // 128)))
                l_s[hh] = alpha * l_s[hh] + jnp.sum(p, axis=1, keepdims=True)
                acc_s[hh] = alpha * acc_s[hh] + _dot(p.astype(BF), v2)
                m_s[hh] = m_new

        _diag_or_below(qi, ki, step)

        @pl.when(ki == qi)
        def _():
            first = _first_half()
            o = jnp.where(first, acc_s[0] / l_s[0], acc_s[1] / l_s[1])
            o_ref[...] = o.astype(BF)
            of_ref[...] = o
            lse_ref[...] = jnp.where(first, m_s[0] + jnp.log(l_s[0]), m_s[1] + jnp.log(l_s[1]))

    qs = pl.BlockSpec((t, 128), lambda p, s: (_tri_rows(s, nq)[0], p))
    ks = pl.BlockSpec((t, 128), lambda p, s: (_tri_rows(s, nq)[1], p))
    stat = pltpu.VMEM((2, t, 128), F32)
    return _pcall(
        body, [fq, fk, fv, aq, ak], name="fox_fwd", grid=(npair, nq * (nq + 1) // 2),
        out_shape=[jax.ShapeDtypeStruct((t_tok, FOX_WIDTH), BF), jax.ShapeDtypeStruct((t_tok, FOX_WIDTH), F32),
                   jax.ShapeDtypeStruct((t_tok, FOX_WIDTH), F32)],
        in_specs=[qs, ks, ks, qs, ks], out_specs=[qs, qs, qs], scratch=[stat, stat, stat], comm=comm)


def _fox_ds(q2, k2, v2, do2, aq2, ak2, ad2, hh, diag):
    s = _dot_nt(_head_rows(q2, aq2, hh), _head_rows(k2, ak2, hh))
    if diag:
        s = _causal_diag(s)
    p = jnp.exp(s)
    av = jnp.where(_aug_lane() < 3, 1.0, 0.0).astype(BF)
    dp = _dot_nt(_head_rows(do2, ad2, hh), _head_rows(v2, jnp.broadcast_to(av, v2.shape), hh))
    return p, p * dp


def _fox_bwd(fq, fk, fv, do, aqb, ak, ad, comm=None):
    t_tok = fq.shape[0]
    t = _tile(t_tok, 512)
    nq = t_tok // t
    npair = FOX_HEADS // 2
    n_steps = nq * (nq + 1) // 2

    def body(q_ref, k_ref, v_ref, do_ref, aq_ref, ak_ref, ad_ref, dq_ref, dk_ref, dv_ref, dck_ref, dcq_ref,
             dk_s, dv_s, dq_s, rs_s):
        step_id = pl.program_id(1)
        ki, qi = _tri_cols(step_id, nq)

        @pl.when(step_id == 0)
        def _():
            dq_s[...] = jnp.zeros_like(dq_s)
            rs_s[...] = jnp.zeros_like(rs_s)

        @pl.when(qi == ki)
        def _():
            dk_s[...] = jnp.zeros_like(dk_s)
            dv_s[...] = jnp.zeros_like(dv_s)
            dck_ref[...] = jnp.zeros_like(dck_ref)

        rows = pl.ds(qi * t if isinstance(qi, int) else pl.multiple_of(qi * t, t), t)

        def step(diag):
            q2, k2, v2, do2 = q_ref[...], k_ref[...], v_ref[...], do_ref[...]
            dq = []
            for hh in range(2):
                p, ds = _fox_ds(q2, k2, v2, do2, aq_ref[...], ak_ref[...], ad_ref[...], hh, diag)
                dsb = ds.astype(BF)
                dv_s[...] += _dot_tn(p.astype(BF), _head_only(do2, hh))
                dk_s[...] += _dot_tn(dsb, _head_only(q2, hh))
                dq.append(_dot(dsb, k2))
                dck_ref[hh] = dck_ref[hh] - jnp.sum(ds, axis=0, keepdims=True)
                rs_s[hh, rows, :] = rs_s[hh, rows, :] + jnp.sum(ds, axis=1, keepdims=True)
            dq_s[rows, :] = dq_s[rows, :] + jnp.where(_first_half(), dq[0], dq[1])

        _diag_or_below(qi, ki, step)

        @pl.when(qi == nq - 1)
        def _():
            dk_ref[...] = dk_s[...].astype(BF)
            dv_ref[...] = dv_s[...].astype(BF)

        @pl.when(step_id == n_steps - 1)
        def _():
            dq_ref[...] = (dq_s[...] * FOX_SCALE).astype(BF)
            dcq_ref[...] = jnp.where(_first_half(), rs_s[0], rs_s[1])

    qs = pl.BlockSpec((t, 128), lambda p, s: (_tri_cols(s, nq)[1], p))
    ks = pl.BlockSpec((t, 128), lambda p, s: (_tri_cols(s, nq)[0], p))
    cks = pl.BlockSpec((2, 1, t), lambda p, s: (p, 0, _tri_cols(s, nq)[0]))
    seq = pl.BlockSpec((t_tok, 128), lambda p, s: (0, p))
    sds = jax.ShapeDtypeStruct((t_tok, FOX_WIDTH), BF)
    return _pcall(
        body, [fq, fk, fv, do, aqb, ak, ad], name="fox_bwd", grid=(npair, n_steps),
        out_shape=[sds, sds, sds, jax.ShapeDtypeStruct((FOX_HEADS, 1, t_tok), F32),
                   jax.ShapeDtypeStruct((t_tok, FOX_WIDTH), F32)],
        in_specs=[qs, ks, ks, qs, qs, ks, qs], out_specs=[seq, ks, ks, cks, seq],
        scratch=[pltpu.VMEM((t, 128), F32), pltpu.VMEM((t, 128), F32), pltpu.VMEM((t_tok, 128), F32),
                 pltpu.VMEM((2, t_tok, 128), F32)], comm=comm)


def _ret_consts():
    c = RET_CHUNK
    log_gamma = jnp.log1p(-jnp.exp2(-5.0 - jnp.arange(RET_HEADS, dtype=F32)))
    idx = jnp.arange(c, dtype=F32)
    diff = idx[:, None] - idx[None, :]
    dmask = jnp.where(diff >= 0, jnp.exp(log_gamma[:, None, None] * jnp.maximum(diff, 0.0)), 0.0)
    qdec = jnp.exp(log_gamma[:, None] * (idx + 1.0))
    kdec = jnp.exp(log_gamma[:, None] * (c - 1 - idx))
    cdec = jnp.exp(log_gamma * c)
    bc = lambda v: jnp.broadcast_to(v[:, :, None], (RET_HEADS, c, RET_DIM))
    return dmask, bc(qdec), bc(kdec), jnp.broadcast_to(cdec[:, None, None], (RET_HEADS, c, RET_DIM))


def _group_norm(y):
    mu = jnp.mean(y, axis=-1, keepdims=True)
    yc = y - mu
    r = lax.rsqrt(jnp.mean(yc * yc, axis=-1, keepdims=True) + EPS)
    return yc * r, r


def _ret_fwd(rq, rk, rv, rg, consts, comm=None):
    t_tok = rq.shape[0]
    nb = 4 if t_tok % (4 * RET_CHUNK) == 0 else 1
    tr = nb * RET_CHUNK
    n_steps = t_tok // tr
    c = RET_CHUNK

    def body(q_ref, k_ref, v_ref, g_ref, dm_ref, qd_ref, kd_ref, cd_ref, y_ref, yo_ref, st_ref, s_s):
        @pl.when(pl.program_id(0) == 0)
        def _():
            s_s[...] = jnp.zeros_like(s_s)

        for b in range(nb):
            rows = slice(b * c, (b + 1) * c)
            for hh in range(RET_HEADS):
                cols = slice(hh * RET_DIM, (hh + 1) * RET_DIM)
                q, k, v = q_ref[rows, cols], k_ref[rows, cols], v_ref[rows, cols]
                state = s_s[hh]
                st_ref[hh, b] = state
                sc = (_dot_nt(q, k) * dm_ref[hh]).astype(BF)
                y = _dot(sc, v) + _dot((q.astype(F32) * qd_ref[hh]).astype(BF), state.astype(BF))
                s_s[hh] = cd_ref[hh] * state + _dot_tn((k.astype(F32) * kd_ref[hh]).astype(BF), v)
                y_ref[rows, cols] = y
                yn, _ = _group_norm(y)
                gate = g_ref[rows, cols].astype(F32)
                yo_ref[rows, cols] = (yn * (gate * _sigmoid(gate))).astype(BF)

    blk = pl.BlockSpec((tr, RET_WIDTH), lambda i: (i, 0))
    cst = pl.BlockSpec((RET_HEADS, c, RET_DIM), lambda i: (0, 0, 0))
    return _pcall(
        body, [rq, rk, rv, rg, *consts], name="ret_fwd", grid=(n_steps,),
        out_shape=[jax.ShapeDtypeStruct((t_tok, RET_WIDTH), F32), jax.ShapeDtypeStruct((t_tok, RET_WIDTH), BF),
                   jax.ShapeDtypeStruct((RET_HEADS, t_tok // c, RET_DIM, RET_DIM), F32)],
        in_specs=[blk] * 4 + [cst] * 4,
        out_specs=[blk, blk, pl.BlockSpec((RET_HEADS, nb, RET_DIM, RET_DIM), lambda i: (0, i, 0, 0))],
        scratch=[pltpu.VMEM((RET_HEADS, RET_DIM, RET_DIM), F32)], comm=comm)


def _ret_bwd(rq, rk, rv, rg, y_raw, dyo, states, consts, cos_t, sin_t, comm=None):
    t_tok = rq.shape[0]
    nb = 4 if t_tok % (4 * RET_CHUNK) == 0 else 1
    tr = nb * RET_CHUNK
    n_steps = t_tok // tr
    c = RET_CHUNK

    def body(q_ref, k_ref, v_ref, g_ref, y_ref, dyo_ref, st_ref, dm_ref, qd_ref, kd_ref, cd_ref,
             cos_ref, sin_ref, dq_ref, dk_ref, dv_ref, dg_ref, ds_s):
        @pl.when(pl.program_id(0) == 0)
        def _():
            ds_s[...] = jnp.zeros_like(ds_s)

        for b in reversed(range(nb)):
            rows = slice(b * c, (b + 1) * c)
            cosv, sinv = cos_ref[rows, :], sin_ref[rows, :]
            for hh in range(RET_HEADS):
                cols = slice(hh * RET_DIM, (hh + 1) * RET_DIM)
                dm, qd, kd, cd = dm_ref[hh], qd_ref[hh], kd_ref[hh], cd_ref[hh]
                q, k, v = q_ref[rows, cols], k_ref[rows, cols], v_ref[rows, cols]
                yn, r = _group_norm(y_ref[rows, cols])
                gate = g_ref[rows, cols].astype(F32)
                sg = _sigmoid(gate)
                dyo = dyo_ref[rows, cols]
                dg_ref[rows, cols] = (dyo * yn * (sg * (1.0 + gate * (1.0 - sg)))).astype(BF)
                dyn = dyo * (gate * sg)
                dy = r * (dyn - jnp.mean(dyn, axis=-1, keepdims=True)
                          - yn * jnp.mean(dyn * yn, axis=-1, keepdims=True))
                dyb = dy.astype(BF)
                state_b = st_ref[hh, b].astype(BF)
                dstate = ds_s[hh]
                dstate_b = dstate.astype(BF)
                qdb = (q.astype(F32) * qd).astype(BF)
                kdb = (k.astype(F32) * kd).astype(BF)
                sc = (_dot_nt(q, k) * dm).astype(BF)
                dv = _dot_tn(sc, dyb) + _dot(kdb, dstate_b)
                dp = (_dot_nt(dyb, v) * dm).astype(BF)
                dq = _dot(dp, k) + _dot_nt(dyb, state_b) * qd
                dk = (_dot_tn(dp, q) + _dot_nt(v, dstate_b) * kd) * RET_SCALE
                ds_s[hh] = cd * dstate + _dot_tn(qdb, dyb)
                dv_ref[rows, cols] = dv.astype(BF)
                dq_ref[rows, cols] = (dq * cosv - _swap_pairs(dq) * sinv).astype(BF)
                dk_ref[rows, cols] = (dk * cosv - _swap_pairs(dk) * sinv).astype(BF)

    rev = lambda i: n_steps - 1 - i
    blk = pl.BlockSpec((tr, RET_WIDTH), lambda i: (rev(i), 0))
    tab = pl.BlockSpec((tr, RET_DIM), lambda i: (rev(i), 0))
    cst = pl.BlockSpec((RET_HEADS, c, RET_DIM), lambda i: (0, 0, 0))
    sds = jax.ShapeDtypeStruct((t_tok, RET_WIDTH), BF)
    return _pcall(
        body, [rq, rk, rv, rg, y_raw, dyo, states, *consts, cos_t, sin_t], name="ret_bwd",
        grid=(n_steps,), out_shape=[sds] * 4,
        in_specs=[blk] * 6 + [pl.BlockSpec((RET_HEADS, nb, RET_DIM, RET_DIM), lambda i: (0, rev(i), 0, 0))]
        + [cst] * 4 + [tab, tab],
        out_specs=[blk] * 4, scratch=[pltpu.VMEM((RET_HEADS, RET_DIM, RET_DIM), F32)], comm=comm)


def _mix_out(h, y_ret, y_fox, ga, gb, wr4, wf4, wo4, comm=None):
    t_tok, d = h.shape
    cz = wr4.shape[-1]
    ro = wo4.shape[-2]
    tm = _tile(t_tok, 512)

    def body(h_ref, yr_ref, yf_ref, ga_ref, gb_ref, wr_ref, wf_ref, wo_ref, ho_ref, za_ref, zb_ref, mix_ref):
        yr, yf = yr_ref[...], yf_ref[...]
        for j in range(N_CHIPS):
            sl = slice(j * cz, (j + 1) * cz)
            za = _dot(yr, wr_ref[j])
            zb = _dot(yf, wf_ref[j])
            za_ref[:, sl] = za.astype(BF)
            zb_ref[:, sl] = zb.astype(BF)
            mix_ref[:, sl] = (ga_ref[:, sl].astype(F32) * za + gb_ref[:, sl].astype(F32) * zb).astype(BF)
        acc = h_ref[...]
        for j in range(N_CHIPS):
            acc = acc + _dot(mix_ref[:, j * ro:(j + 1) * ro], wo_ref[j])
        ho_ref[...] = acc

    row = lambda c: pl.BlockSpec((tm, c), lambda i: (i, 0))
    full = lambda *s: pl.BlockSpec(s, lambda i: (0,) * len(s))
    sds = lambda dt: jax.ShapeDtypeStruct((t_tok, d), dt)
    return _pcall(
        body, [h, y_ret, y_fox, ga, gb, wr4, wf4, wo4], name="mix_out", grid=(t_tok // tm,),
        out_shape=[sds(F32), sds(BF), sds(BF), sds(BF)],
        in_specs=[row(d), row(RET_WIDTH), row(FOX_WIDTH), row(d), row(d),
                  full(N_CHIPS, RET_WIDTH, cz), full(N_CHIPS, FOX_WIDTH, cz), full(N_CHIPS, ro, d)],
        out_specs=[row(d)] * 4, comm=comm)


def _mix_out_bwd(dh, za, zb, ga, gb, y_fox, wr4, wf4, wo4, comm=None):
    t_tok, d = dh.shape
    cz = wr4.shape[-1]
    ro = wo4.shape[-2]
    tm = _tile(t_tok, 256)

    def body(dh_ref, za_ref, zb_ref, ga_ref, gb_ref, yf_ref, wr_ref, wf_ref, wo_ref,
             dhb_ref, dgp_ref, dza_ref, dzb_ref, dyr_ref, dyf_ref, dl_ref, db_ref):
        @pl.when(pl.program_id(0) == 0)
        def _():
            db_ref[...] = jnp.zeros_like(db_ref)

        dhb = dh_ref[...].astype(BF)
        dhb_ref[...] = dhb
        dyr = jnp.zeros((tm, RET_WIDTH), F32)
        dyf = jnp.zeros((tm, FOX_WIDTH), F32)
        for j in range(N_CHIPS):
            sl = slice(j * ro, (j + 1) * ro)
            dmix = _dot_nt(dhb, wo_ref[j])
            ga, gb = ga_ref[:, sl].astype(F32), gb_ref[:, sl].astype(F32)
            dza = (dmix * ga).astype(BF)
            dzb = (dmix * gb).astype(BF)
            dza_ref[:, sl] = dza
            dzb_ref[:, sl] = dzb
            dga = dmix * za_ref[:, sl].astype(F32) * ga * (1.0 - ga)
            dgb = dmix * zb_ref[:, sl].astype(F32) * gb * (1.0 - gb)
            dgp_ref[:, sl] = dga.astype(BF)
            dgp_ref[:, d + j * ro:d + (j + 1) * ro] = dgb.astype(BF)
            db_ref[:, sl] += jnp.sum(dga, axis=0, keepdims=True)
            db_ref[:, d + j * ro:d + (j + 1) * ro] += jnp.sum(dgb, axis=0, keepdims=True)
        for j in range(N_CHIPS):
            sl = slice(j * cz, (j + 1) * cz)
            dyr = dyr + _dot_nt(dza_ref[:, sl], wr_ref[j])
            dyf = dyf + _dot_nt(dzb_ref[:, sl], wf_ref[j])
        dyr_ref[...] = dyr
        dyfb = dyf.astype(BF)
        dyf_ref[...] = dyfb
        prod = dyfb.astype(F32) * yf_ref[...]
        first = _first_half()
        for pp in range(FOX_HEADS // 2):
            blk = prod[:, pp * 128:(pp + 1) * 128]
            s0 = jnp.sum(jnp.where(first, blk, 0.0), axis=1, keepdims=True)
            s1 = jnp.sum(jnp.where(first, 0.0, blk), axis=1, keepdims=True)
            parts = _split3(-jnp.where(first, s1, s0))
            dl_ref[:, pp * 128:(pp + 1) * 128] = _aug_put(jnp.zeros((tm, 128), BF), 0, parts)

    row = lambda c: pl.BlockSpec((tm, c), lambda i: (i, 0))
    full = lambda *s: pl.BlockSpec(s, lambda i: (0,) * len(s))
    sds = lambda c, dt: jax.ShapeDtypeStruct((t_tok, c), dt)
    return _pcall(
        body, [dh, za, zb, ga, gb, y_fox, wr4, wf4, wo4], name="mix_out_bwd", grid=(t_tok // tm,),
        out_shape=[sds(d, BF), sds(2 * d, BF), sds(d, BF), sds(d, BF), sds(RET_WIDTH, F32),
                   sds(FOX_WIDTH, BF), sds(FOX_WIDTH, BF), jax.ShapeDtypeStruct((1, 2 * d), F32)],
        in_specs=[row(d)] * 5 + [row(FOX_WIDTH), full(N_CHIPS, RET_WIDTH, cz), full(N_CHIPS, FOX_WIDTH, cz),
                                 full(N_CHIPS, ro, d)],
        out_specs=[row(d), row(2 * d), row(d), row(d), row(RET_WIDTH), row(FOX_WIDTH), row(FOX_WIDTH),
                   full(1, 2 * d)],
        comm=comm)


def _mix_in_bwd(dh, h, ln, parts, dff, dgpre, w_in, wm4, comm=None):
    t_tok, d = h.shape
    cm = wm4.shape[-1]
    tm = _tile(t_tok, 256)

    def body(dh_ref, h_ref, ln_ref, p0, p1, p2, p3, p4, p5, p6, dff_ref, dgp_ref, win_ref, wm_ref,
             dhi_ref, dln_ref, dproj_ref):
        @pl.when(pl.program_id(0) == 0)
        def _():
            dln_ref[...] = jnp.zeros_like(dln_ref)

        for k, pr in enumerate((p0, p1, p2, p3, p4, p5, p6)):
            dproj_ref[:, k * 512:(k + 1) * 512] = pr[...]
        dproj_ref[:, FF_COL:FF_COL + 128] = dff_ref[...]
        dproj_ref[:, FF_COL + 128:] = jnp.zeros((tm, IN_PAD - FF_COL - 128), BF)
        du = _dot(dproj_ref[...], win_ref[...])
        for j in range(N_CHIPS):
            du = du + _dot_nt(dgp_ref[:, j * cm:(j + 1) * cm], wm_ref[j])
        xv = h_ref[...]
        dx, dln = _rms_bwd(du, xv, _rstd(xv), ln_ref[...])
        dln_ref[...] += dln
        dhi_ref[...] = dh_ref[...] + dx

    row = lambda c: pl.BlockSpec((tm, c), lambda i: (i, 0))
    full = lambda *s: pl.BlockSpec(s, lambda i: (0,) * len(s))
    return _pcall(
        body, [dh, h, ln, *parts, dff, dgpre, w_in, wm4], name="mix_in_bwd", grid=(t_tok // tm,),
        out_shape=[jax.ShapeDtypeStruct((t_tok, d), F32), jax.ShapeDtypeStruct((1, d), F32),
                   jax.ShapeDtypeStruct((t_tok, IN_PAD), BF)],
        in_specs=[row(d), row(d), full(1, d)] + [row(512)] * 7 + [row(128), row(2 * d), full(IN_PAD, d),
                                                                   full(N_CHIPS, d, cm)],
        out_specs=[row(d), full(1, d), row(IN_PAD)], comm=comm)


def _tail(h, p, target, ln_ple, ln_fin, wpg4, wpl4, comm=None):
    t_tok, d = h.shape
    pd = p.shape[1]
    rg = wpg4.shape[-2]
    cp = wpl4.shape[-1]
    tm = _tile(t_tok, 256)

    def body(h_ref, p_ref, t_ref, lp_ref, lf_ref, wg_ref, wp_ref,
             dh_ref, n_ref, dgp_ref, dpe_ref, pb_ref, loss_ref, dlf_ref, dlp_ref, pe_s, dn_s):
        @pl.when(pl.program_id(0) == 0)
        def _():
            loss_ref[...] = jnp.zeros_like(loss_ref)
            dlf_ref[...] = jnp.zeros_like(dlf_ref)
            dlp_ref[...] = jnp.zeros_like(dlp_ref)

        xv = h_ref[...]
        r3 = _rstd(xv)
        nb = (xv * r3 * lp_ref[...]).astype(BF)
        n_ref[...] = nb
        pb = p_ref[...].astype(BF)
        pb_ref[...] = pb
        pgpre = jnp.zeros((tm, d), F32)
        for j in range(N_CHIPS):
            pgpre = pgpre + _dot(nb[:, j * rg:(j + 1) * rg], wg_ref[j])
            pe_s[:, j * cp:(j + 1) * cp] = _dot(pb, wp_ref[j])
        pg = _sigmoid(pgpre)
        pe = pe_s[...]
        h4 = xv + pg * pe
        r4 = _rstd(h4)
        err = h4 * r4 * lf_ref[...] - t_ref[...]
        loss_ref[...] += 0.5 * jnp.sum(jnp.sum(err * err, axis=1, keepdims=True), axis=0, keepdims=True) / d
        dh4, dlf = _rms_bwd(err * (1.0 / d), h4, r4, lf_ref[...])
        dlf_ref[...] += dlf
        dpe_ref[...] = (dh4 * pg).astype(BF)
        dgp = (dh4 * pe * pg * (1.0 - pg)).astype(BF)
        dgp_ref[...] = dgp
        for j in range(N_CHIPS):
            dn_s[:, j * rg:(j + 1) * rg] = _dot_nt(dgp, wg_ref[j])
        dx, dlp = _rms_bwd(dn_s[...], xv, r3, lp_ref[...])
        dlp_ref[...] += dlp
        dh_ref[...] = dh4 + dx

    row = lambda c: pl.BlockSpec((tm, c), lambda i: (i, 0))
    full = lambda *s: pl.BlockSpec(s, lambda i: (0,) * len(s))
    sds = lambda c, dt: jax.ShapeDtypeStruct((t_tok, c), dt)
    vec = jax.ShapeDtypeStruct((1, d), F32)
    return _pcall(
        body, [h, p, target, ln_ple, ln_fin, wpg4, wpl4], name="tail", grid=(t_tok // tm,),
        out_shape=[sds(d, F32), sds(d, BF), sds(d, BF), sds(d, BF), sds(pd, BF),
                   jax.ShapeDtypeStruct((1, 128), F32), vec, vec],
        in_specs=[row(d), row(pd), row(d), full(1, d), full(1, d), full(N_CHIPS, rg, d), full(N_CHIPS, pd, cp)],
        out_specs=[row(d), row(d), row(d), row(d), row(pd), full(1, 128), full(1, d), full(1, d)],
        scratch=[pltpu.VMEM((tm, d), F32), pltpu.VMEM((tm, d), F32)], comm=comm)


BIG = ["w_ffn1_gate", "w_ffn1_up", "w_ffn1_down", "w_in", "w_merge", "w_ret_out", "w_fox_out", "w_out",
       "w_ffn2_gate", "w_ffn2_up", "w_ffn2_down", "w_ple", "w_ple_gate"]
SMALL = ["ln_ffn1", "ln_mix", "b_forget", "b_merge", "ln_ffn2", "ln_ple", "ln_final"]
WEIGHTS = ["ln_ffn1", "w_ffn1_gate", "w_ffn1_up", "w_ffn1_down", "ln_mix", "w_in", "b_forget", "w_merge", "b_merge",
           "w_ret_out", "w_fox_out", "w_out", "ln_ffn2", "w_ffn2_gate", "w_ffn2_up", "w_ffn2_down", "ln_ple",
           "w_ple", "w_ple_gate", "ln_final"]


TRANSPOSED = {"w_ffn1_gate", "w_ffn1_up", "w_ffn2_gate", "w_ffn2_up", "w_in"}
IN_ROWS_PAD = -(-(IN_COLS // N_CHIPS) // 32) * 32


def _pack_small(vals, loss_row):
    rows = [loss_row]
    for name in SMALL:
        v = vals[name].reshape(-1)
        n = -(-v.shape[0] // 128) * 128
        rows.append(jnp.pad(v, (0, n - v.shape[0])).reshape(n // 128, 128))
    packed = jnp.concatenate(rows, axis=0)
    pad = -packed.shape[0] % 8
    return jnp.pad(packed, ((0, pad), (0, 0)))


def _unpack_small(packed, sizes):
    out, r = {}, 1
    for name in SMALL:
        n = sizes[name]
        nr = -(-n // 128)
        out[name] = packed[r:r + nr].reshape(1, nr * 128)[:, :n]
        r += nr
    return out


class _Stage:
    def __init__(self, comm, finish):
        self.comm, self.finish, self.result = comm, finish, None


def _hosted(fn, *a, stages=()):
    if not stages:
        return fn(*a)
    outs, couts = fn(*a, comm=_merge([st.comm for st in stages]))
    for st, o in zip(stages, _split_outs([st.comm for st in stages], couts)):
        st.result = st.finish(o)
    return outs


class _Reducer:
    def __init__(self):
        self.done = {}

    def swap(self, grads):
        names = list(grads)
        return _Stage(_c_half_swap([grads[n] for n in names]),
                      lambda outs: {n: _add_halves(grads[n], o) for n, o in zip(names, outs)})

    def exchange(self, parts):
        names = list(parts)
        return _Stage(_c_chip_exchange([parts[n] for n in names]),
                      lambda outs: {n: _sum_chips(parts[n], o) for n, o in zip(names, outs)})

    def join(self, halves):
        names = list(halves)
        return _Stage(_c_join([halves[n] for n in names]),
                      lambda outs: self.done.update({n: (halves[n], o) for n, o in zip(names, outs)}))


def kernel(x, p, positions, ln_ffn1, w_ffn1_gate, w_ffn1_up, w_ffn1_down, ln_mix, w_in, b_forget, w_merge, b_merge, w_ret_out, w_fox_out, w_out, ln_ffn2, w_ffn2_gate, w_ffn2_up, w_ffn2_down, ln_ple, w_ple, w_ple_gate, ln_final, loss_target, m_ln_ffn1, m_w_ffn1_gate, m_w_ffn1_up, m_w_ffn1_down, m_ln_mix, m_w_in, m_b_forget, m_w_merge, m_b_merge, m_w_ret_out, m_w_fox_out, m_w_out, m_ln_ffn2, m_w_ffn2_gate, m_w_ffn2_up, m_w_ffn2_down, m_ln_ple, m_w_ple, m_w_ple_gate, m_ln_final, v_ln_ffn1, v_w_ffn1_gate, v_w_ffn1_up, v_w_ffn1_down, v_ln_mix, v_w_in, v_b_forget, v_w_merge, v_b_merge, v_w_ret_out, v_w_fox_out, v_w_out, v_ln_ffn2, v_w_ffn2_gate, v_w_ffn2_up, v_w_ffn2_down, v_ln_ple, v_w_ple, v_w_ple_gate, v_ln_final):
    args = dict(locals())
    w = {n: args[n] for n in WEIGHTS}
    m = {n: args["m_" + n] for n in WEIGHTS}
    v = {n: args["v_" + n] for n in WEIGHTS}
    d = x.shape[-1]
    t_tok = x.shape[1]
    xs, ps, target = x[0], p[0, 0], loss_target[0]
    small = {n: w[n].reshape(1, -1) for n in SMALL}

    def to2d(n, a):
        if n in TRANSPOSED:
            return a[0].T
        return a.reshape(a.shape[-2], a.shape[-1]) if a.ndim == 3 else a.reshape(1, -1)

    def from2d(n, a):
        return a.T[None] if n in TRANSPOSED else a.reshape(w[n].shape)

    def padded(n, a):
        return jnp.pad(a, ((0, IN_ROWS_PAD - a.shape[0]), (0, 0))) if n == "w_in" else a

    core = lax.axis_index("c")
    me = 2 * lax.axis_index("x") + lax.axis_index("y")
    shard = {}
    for n in BIG:
        s2 = padded(n, to2d(n, w[n]).astype(BF))
        shard[n] = s2.reshape(1, 2, s2.shape[0] // 2, s2.shape[1])
    full = {}

    def gather(names):
        bufs = [lax.dynamic_update_slice(jnp.zeros((N_CHIPS,) + shard[n].shape[1:], BF), shard[n], (me, 0, 0, 0))
                for n in names]

        def finish(outs):
            full.update({n: o.reshape(N_CHIPS, 2 * o.shape[2], o.shape[3]) for n, o in zip(names, outs)})

        return _Stage(_c_all_gather(bufs), finish)

    half = RET_DIM // 2
    inv_freq = 1.0 / (ROPE_BASE ** (jnp.arange(half, dtype=F32) / half))
    cos_t, sin_t = _hosted(_rope_tables, positions[0].astype(F32).reshape(t_tok, 1),
                           jnp.repeat(inv_freq, 2).reshape(1, RET_DIM),
                           stages=[gather(["w_ffn1_gate", "w_ffn1_up", "w_ffn1_down"])])
    consts = _ret_consts()
    b_pad = jnp.pad(small["b_forget"], ((0, 0), (0, 128 - FOX_HEADS)))

    h1, n1, g1, u1 = _hosted(_ffn_fwd, xs, small["ln_ffn1"], full["w_ffn1_gate"], full["w_ffn1_up"],
                             full["w_ffn1_down"], stages=[gather(["w_in", "w_merge"])])
    w_in_full = jnp.pad(full["w_in"][:, :IN_COLS // N_CHIPS].reshape(IN_COLS, d), ((0, IN_PAD - IN_COLS), (0, 0)))
    u, rq, rk, rv, rg, fq, fk, fv, ffl, ga, gb = _hosted(
        _mix_in, h1, small["ln_mix"], w_in_full, full["w_merge"], small["b_merge"], cos_t, sin_t,
        stages=[gather(["w_ret_out", "w_fox_out", "w_out", "w_ple_gate", "w_ple"])])
    aq, ak = _forget_fwd(ffl, b_pad)
    y_raw, y_ret, states = _ret_fwd(rq, rk, rv, rg, consts)
    y_fox, y_fox32, lse_e = _hosted(_fox_fwd, fq, fk, fv, aq, ak,
                                    stages=[gather(["w_ffn2_gate", "w_ffn2_up", "w_ffn2_down"])])
    aqb = _fox_aug_lse(aq, lse_e)
    h2, za, zb, mix = _mix_out(h1, y_ret, y_fox, ga, gb, full["w_ret_out"], full["w_fox_out"], full["w_out"])
    h3, n2, g2, u2 = _ffn_fwd(h2, small["ln_ffn2"], full["w_ffn2_gate"], full["w_ffn2_up"], full["w_ffn2_down"])

    red = _Reducer()
    dh3, n3, dpgpre, dpe, pb, loss, dln_final, dln_ple = _tail(
        h3, ps, target, small["ln_ple"], small["ln_final"], full["w_ple_gate"], full["w_ple"])
    g_ple = dict(w_ple_gate=_wgrad_rows("wgrad_ple_gate", n3, dpgpre, N_CHIPS),
                 w_ple=_wgrad_cols("wgrad_ple", pb, dpe, N_CHIPS))

    sw_ple = red.swap(g_ple)
    dh2, dln_ffn2, dg2, du2, a2, dhb3 = _hosted(
        _ffn_bwd, dh3, h2, small["ln_ffn2"], g2, u2, full["w_ffn2_gate"], full["w_ffn2_up"], full["w_ffn2_down"],
        stages=[sw_ple])
    ex_ple = red.exchange(sw_ple.result)
    g_f2 = dict(w_ffn2_gate=_hosted(_wgrad_b_shared, "wgrad_ffn2_gate", dg2, n2, stages=[ex_ple]))
    g_f2["w_ffn2_up"] = _wgrad_b_shared("wgrad_ffn2_up", du2, n2)
    g_f2["w_ffn2_down"] = _wgrad_b_shared("wgrad_ffn2_down", a2, dhb3)

    sw_f2 = red.swap(g_f2)
    dhb2, dgpre, dza, dzb, dy_ret, dy_fox, ad, db_merge = _hosted(
        _mix_out_bwd, dh2, za, zb, ga, gb, y_fox32, full["w_ret_out"], full["w_fox_out"], full["w_out"],
        stages=[sw_f2, red.join(ex_ple.result)])
    g_br = dict(w_out=_wgrad_rows("wgrad_out", mix, dhb2, N_CHIPS),
                w_ret_out=_wgrad_cols("wgrad_ret_out", y_ret, dza, N_CHIPS),
                w_fox_out=_wgrad_cols("wgrad_fox_out", y_fox, dzb, N_CHIPS))

    sw_br = red.swap(g_br)
    drq, drk, drv, drg = _hosted(_ret_bwd, rq, rk, rv, rg, y_raw, dy_ret, states, consts, cos_t, sin_t,
                                 stages=[sw_br])
    ex_f2, ex_br = red.exchange(sw_f2.result), red.exchange(sw_br.result)
    dfq, dfk, dfv, dcum_t3, dcum_q = _hosted(_fox_bwd, fq, fk, fv, dy_fox, aqb, ak, ad, stages=[ex_f2, ex_br])
    dff, db_forget = _forget_bwd(dcum_t3.reshape(FOX_HEADS, t_tok), dcum_q, ffl, b_pad)
    dh1, dln_mix, dproj = _hosted(
        _mix_in_bwd, dh2, h1, small["ln_mix"], (drq, drk, drv, drg, dfq, dfk, dfv), dff, dgpre, w_in_full,
        full["w_merge"], stages=[red.join(ex_f2.result), red.join(ex_br.result)])

    dx, dln_ffn1, dg1, du1, a1, dhb1 = _ffn_bwd(
        dh1, xs, small["ln_ffn1"], g1, u1, full["w_ffn1_gate"], full["w_ffn1_up"], full["w_ffn1_down"])
    g_f1g = _wgrad_b_shared("wgrad_ffn1_gate", dg1, n1)
    sw_f1g = red.swap(dict(w_ffn1_gate=g_f1g))
    g_f1u = _hosted(_wgrad_b_shared, "wgrad_ffn1_up", du1, n1, stages=[sw_f1g])
    ex_f1g, sw_f1u = red.exchange(sw_f1g.result), red.swap(dict(w_ffn1_up=g_f1u))
    g_f1d = _hosted(_wgrad_b_shared, "wgrad_ffn1_down", a1, dhb1, stages=[ex_f1g, sw_f1u])

    ex_f1u, sw_f1d = red.exchange(sw_f1u.result), red.swap(dict(w_ffn1_down=g_f1d))
    g_in = _hosted(_wgrad_rows, "wgrad_in", dproj, u, IN_PAD // 512,
                   stages=[ex_f1u, sw_f1d, red.join(ex_f1g.result)])
    g_in = g_in.reshape(IN_PAD, d)[:IN_COLS].reshape(N_CHIPS, IN_COLS // N_CHIPS, d)
    g_in = jnp.pad(g_in, ((0, 0), (0, IN_ROWS_PAD - IN_COLS // N_CHIPS), (0, 0)))
    ex_f1d, sw_in = red.exchange(sw_f1d.result), red.swap(dict(w_in=g_in))
    g_mrg = _hosted(_wgrad_cols, "wgrad_merge", u, dgpre, N_CHIPS,
                    stages=[ex_f1d, sw_in, red.join(ex_f1u.result)])

    small_grads = dict(ln_ffn1=dln_ffn1, ln_mix=dln_mix, b_forget=db_forget[:, :FOX_HEADS], b_merge=db_merge,
                       ln_ffn2=dln_ffn2, ln_ple=dln_ple, ln_final=dln_final)
    sizes = {n: w[n].size for n in SMALL}
    reduced = _all_reduce_small(_pack_small(small_grads, loss))
    gsum = _unpack_small(reduced, sizes)
    loss = reduced[0, 0]

    results = {}

    def update(names, stages=()):
        w2, m2, v2 = ([to2d(n, a[n]) for n in names] for a in (w, m, v))
        n = names[0]
        if n in gsum or n == "w_in":
            if n in gsum:
                g2 = gsum[n]
            else:
                mine, other = red.done[n]
                g2 = jnp.where(core == 0, jnp.concatenate([mine, other]), jnp.concatenate([other, mine]))
                g2 = g2[:w2[0].shape[0]]
            res = [g2] + _hosted(_adamw, w2[0], g2, m2[0], v2[0], stages=stages)
        else:
            res = _hosted(_adamw_halves, [(w2[q], *red.done[names[q]], m2[q], v2[q]) for q in range(len(names))],
                          stages=stages)
        for q, name in enumerate(names):
            results[name] = tuple(from2d(name, a) for a in res[4 * q:4 * q + 4])

    ex_in, sw_mrg = red.exchange(sw_in.result), red.swap(dict(w_merge=g_mrg))
    update(["w_ffn2_gate", "w_ffn2_up", "w_ffn2_down"], stages=[ex_in, sw_mrg, red.join(ex_f1d.result)])
    ex_mrg = red.exchange(sw_mrg.result)
    update(["w_ffn1_gate", "w_ffn1_up"], stages=[ex_mrg, red.join(ex_in.result)])
    update(["w_ffn1_down"], stages=[red.join(ex_mrg.result)])
    update(["w_out", "w_ple_gate"])
    update(["w_ret_out", "w_fox_out"])
    for n in WEIGHTS:
        if n not in results:
            update([n])

    outs = [[results[n][k] for n in WEIGHTS] for k in range(4)]
    return (loss, dx[None], *outs[0], *outs[1], *outs[2], *outs[3])
```

```python
import functools
import operator

import jax
import jax.numpy as jnp
from jax import lax
from jax.experimental import pallas as pl
from jax.experimental.pallas import tpu as pltpu

F32 = jnp.float32
BF = jnp.bfloat16
MESH = pl.DeviceIdType.MESH

EPS = 1e-6
ROPE_BASE = 10000.0
N_CHIPS = 4
RET_HEADS = 4
RET_DIM = 128
RET_WIDTH = RET_HEADS * RET_DIM
RET_CHUNK = 128
RET_SCALE = RET_DIM ** -0.5
FOX_HEADS = 8
FOX_DIM = 64
FOX_WIDTH = FOX_HEADS * FOX_DIM
FOX_SCALE = FOX_DIM ** -0.5
IN_COLS = 4 * RET_WIDTH + 3 * FOX_WIDTH + FOX_HEADS
IN_PAD = 4096
FF_COL = 4 * RET_WIDTH + 3 * FOX_WIDTH
NEG = -1e30

ADAM_LR = 0.001
ADAM_B1 = 0.9
ADAM_B2 = 0.999
ADAM_EPS = 1e-08
ADAM_WD = 0.01
ADAM_STEP = 10

VMEM_LIMIT = 52 * 1024 * 1024

NT = (((1,), (1,)), ((), ()))
TN = (((0,), (0,)), ((), ()))

HBM_SPEC = pl.BlockSpec(memory_space=pltpu.HBM)
VMEM_SPEC = pl.BlockSpec(memory_space=pltpu.VMEM)


def _dot(a, b):
    return jnp.dot(a, b, preferred_element_type=F32)


def _dot_nt(a, b):
    return lax.dot_general(a, b, NT, preferred_element_type=F32)


def _dot_tn(a, b):
    return lax.dot_general(a, b, TN, preferred_element_type=F32)


def _rstd(xv):
    return lax.rsqrt(jnp.mean(xv * xv, axis=-1, keepdims=True) + EPS)


def _rms_bwd(dn, xv, r, ln):
    xh = xv * r
    dxh = dn * ln
    dx = r * (dxh - xh * jnp.mean(dxh * xh, axis=-1, keepdims=True))
    return dx, jnp.sum(dn * xh, axis=0, keepdims=True)


def _sigmoid(x):
    return jax.nn.sigmoid(x)


def _tile(n, pref):
    return pref if n % pref == 0 else n


def _row_tile(n, cap):
    best = [t for t in range(16, min(n, cap) + 1, 16) if n % t == 0]
    return best[-1] if best else n


class _Comm:
    def __init__(self, ins, out_shapes, sems, start, wait, aliases=None):
        self.ins, self.out_shapes, self.sems, self.start, self.wait = list(ins), list(out_shapes), list(sems), start, wait
        self.aliases = dict(aliases or {})


def _merge(comms):
    comms = [c for c in comms if c is not None]
    if not comms:
        return None
    bounds, ni, no, ns = [], 0, 0, 0
    for c in comms:
        bounds.append((ni, no, ns))
        ni, no, ns = ni + len(c.ins), no + len(c.out_shapes), ns + len(c.sems)

    def run(which):
        def f(ins, outs, sems):
            for c, (i, o, s) in zip(comms, bounds):
                getattr(c, which)(ins[i:i + len(c.ins)], outs[o:o + len(c.out_shapes)], sems[s:s + len(c.sems)])
        return f

    aliases = {i + a: o + b for c, (i, o, _) in zip(comms, bounds) for a, b in c.aliases.items()}
    return _Comm([a for c in comms for a in c.ins], [a for c in comms for a in c.out_shapes],
                 [a for c in comms for a in c.sems], run("start"), run("wait"), aliases)


def _split_outs(comms, outs):
    res, o = [], 0
    for c in comms:
        if c is not None:
            res.append(list(outs[o:o + len(c.out_shapes)]))
            o += len(c.out_shapes)
    return res


def _pcall(body, args, *, name, out_shape, grid=(), in_specs=None, out_specs=None, scratch=(), comm=None,
           prefetch=()):
    many = isinstance(out_shape, (list, tuple))
    outs = list(out_shape) if many else [out_shape]
    n_pre, n_in, n_out, n_scr = len(prefetch), len(args), len(outs), len(scratch)
    if in_specs is None:
        in_specs, out_specs = [VMEM_SPEC] * n_in, [VMEM_SPEC] * n_out
    else:
        in_specs, out_specs = list(in_specs), (list(out_specs) if many else [out_specs])
    params = pltpu.CompilerParams(dimension_semantics=("arbitrary",) * len(grid), vmem_limit_bytes=VMEM_LIMIT)
    scalars = [jnp.reshape(s, (1,)).astype(jnp.int32) for s in prefetch]
    ci, co = (len(comm.ins), len(comm.out_shapes)) if comm is not None else (0, 0)

    def wrapped(*refs):
        pre, refs = refs[:n_pre], refs[n_pre:]
        a, ca = refs[:n_in], refs[n_in:n_in + ci]
        o = refs[n_in + ci:n_in + ci + n_out]
        cout = refs[n_in + ci + n_out:n_in + ci + n_out + co]
        s = refs[n_in + ci + n_out + co:n_in + ci + n_out + co + n_scr]
        csem = refs[n_in + ci + n_out + co + n_scr:]
        if comm is None:
            body(*pre, *a, *o, *s)
        elif grid:
            first = functools.reduce(operator.and_, [pl.program_id(k) == 0 for k in range(len(grid))])
            last = functools.reduce(operator.and_, [pl.program_id(k) == grid[k] - 1 for k in range(len(grid))])
            pl.when(first)(lambda: comm.start(ca, cout, csem))
            body(*pre, *a, *o, *s)
            pl.when(last)(lambda: comm.wait(ca, cout, csem))
        else:
            comm.start(ca, cout, csem)
            body(*pre, *a, *o, *s)
            comm.wait(ca, cout, csem)

    c_ins, c_outs, c_sems, aliases = ([], [], [], {}) if comm is None else (
        comm.ins, comm.out_shapes, comm.sems, {n_pre + n_in + i: n_out + o for i, o in comm.aliases.items()})
    all_in, all_out = in_specs + [HBM_SPEC] * ci, out_specs + [HBM_SPEC] * co
    all_scr = list(scratch) + c_sems
    if grid:
        args = [pltpu.with_memory_space_constraint(a, pltpu.HBM) for a in args]
    c_ins = [pltpu.with_memory_space_constraint(a, pltpu.HBM) for a in c_ins]
    if n_pre:
        spec = dict(grid_spec=pltpu.PrefetchScalarGridSpec(
            num_scalar_prefetch=n_pre, grid=grid, in_specs=all_in, out_specs=all_out, scratch_shapes=all_scr))
    else:
        spec = dict(grid=grid, in_specs=all_in, out_specs=all_out, scratch_shapes=all_scr)
    res = pl.pallas_call(wrapped, name=name, out_shape=outs + c_outs, input_output_aliases=aliases,
                         compiler_params=params, **spec)(*scalars, *args, *c_ins)
    mine = list(res[:n_out])
    mine = mine if many else mine[0]
    return mine if comm is None else (mine, list(res[n_out:]))


def _peer_chips(x, y):
    return [(1 - x, y), (x, 1 - y), (1 - x, 1 - y)]


def _c_all_gather(bufs):
    n = len(bufs)

    def copies(ins, outs, sems):
        send_sems, recv_sems, fwd_send, fwd_recv = sems
        x, y, c = lax.axis_index("x"), lax.axis_index("y"), lax.axis_index("c")
        me = 2 * x + y
        peers = _peer_chips(x, y)
        chip = [2 * px + py for px, py in peers]

        def ici(g, j, slot):
            return pltpu.make_async_remote_copy(
                src_ref=outs[g].at[me, c], dst_ref=outs[g].at[slot, c], send_sem=send_sems.at[g, j],
                recv_sem=recv_sems.at[g, j], device_id=(*peers[j], c), device_id_type=MESH)

        def d2d(g, j, half):
            return pltpu.make_async_remote_copy(
                src_ref=outs[g].at[chip[j], half], dst_ref=outs[g].at[chip[j], half], send_sem=fwd_send.at[g, j],
                recv_sem=fwd_recv.at[g, j], device_id=(x, y, 1 - c), device_id_type=MESH)

        pairs = [(g, j) for g in range(n) for j in range(3)]
        sends = [ici(g, j, me) for g, j in pairs]
        recvs = [ici(g, j, chip[j]) for g, j in pairs]
        passes = [d2d(g, j, c) for g, j in pairs]
        passed = [d2d(g, j, 1 - c) for g, j in pairs]
        return sends, recvs, passes, passed

    def start(ins, outs, sems):
        for cp in copies(ins, outs, sems)[0]:
            cp.start()

    def wait(ins, outs, sems):
        sends, recvs, passes, passed = copies(ins, outs, sems)
        for rcv, fwd in zip(recvs, passes):
            rcv.wait_recv()
            fwd.start()
        for cp in passed:
            cp.wait_recv()
        for cp in sends + passes:
            cp.wait_send()

    pair_sems = pltpu.SemaphoreType.DMA((n, 3))
    return _Comm(bufs, [jax.ShapeDtypeStruct(s.shape, s.dtype) for s in bufs], [pair_sems] * 4, start, wait,
                 aliases={g: g for g in range(n)})


def _start_wait(copies):
    def start(ins, outs, sems):
        local, sends, _ = copies(ins, outs, sems)
        for cp in local + sends:
            cp.start()

    def wait(ins, outs, sems):
        local, sends, recvs = copies(ins, outs, sems)
        for cp in recvs:
            cp.wait_recv()
        for cp in sends:
            cp.wait_send()
        for cp in local:
            cp.wait()

    return start, wait


def _c_half_swap(grads):
    n = len(grads)

    def copies(ins, outs, sems):
        send_sems, recv_sems = sems
        x, y, c = lax.axis_index("x"), lax.axis_index("y"), lax.axis_index("c")
        sends = []
        for g in range(n):
            half = ins[g].shape[1] // 2
            sends.append(pltpu.make_async_remote_copy(
                src_ref=ins[g].at[:, pl.ds((1 - c) * half, half), :], dst_ref=outs[g],
                send_sem=send_sems.at[g], recv_sem=recv_sems.at[g], device_id=(x, y, 1 - c), device_id_type=MESH))
        return [], sends, sends

    return _Comm(
        grads, [jax.ShapeDtypeStruct((N_CHIPS, s.shape[1] // 2, s.shape[2]), s.dtype) for s in grads],
        [pltpu.SemaphoreType.DMA((n,)), pltpu.SemaphoreType.DMA((n,))], *_start_wait(copies))


def _c_chip_exchange(parts):
    n = len(parts)

    def copies(ins, outs, sems):
        send_sems, recv_sems = sems
        x, y, c = lax.axis_index("x"), lax.axis_index("y"), lax.axis_index("c")
        peers = _peer_chips(x, y)

        def remote(g, j):
            return pltpu.make_async_remote_copy(
                src_ref=ins[g].at[2 * peers[j][0] + peers[j][1]], dst_ref=outs[g].at[j],
                send_sem=send_sems.at[g, j], recv_sem=recv_sems.at[g, j], device_id=(*peers[j], c),
                device_id_type=MESH)

        sends = [remote(g, j) for g in range(n) for j in range(3)]
        return [], sends, sends

    return _Comm(
        parts, [jax.ShapeDtypeStruct((3,) + s.shape[1:], s.dtype) for s in parts],
        [pltpu.SemaphoreType.DMA((n, 3)), pltpu.SemaphoreType.DMA((n, 3))], *_start_wait(copies))


def _c_join(halves):
    n = len(halves)

    def copies(ins, outs, sems):
        send_sems, recv_sems = sems
        x, y, c = lax.axis_index("x"), lax.axis_index("y"), lax.axis_index("c")
        sends = [pltpu.make_async_remote_copy(
            src_ref=ins[g], dst_ref=outs[g], send_sem=send_sems.at[g], recv_sem=recv_sems.at[g],
            device_id=(x, y, 1 - c), device_id_type=MESH) for g in range(n)]
        return [], sends, sends

    return _Comm(
        halves, [jax.ShapeDtypeStruct(s.shape, s.dtype) for s in halves],
        [pltpu.SemaphoreType.DMA((n,)), pltpu.SemaphoreType.DMA((n,))], *_start_wait(copies))


def _exchange_only(comm=None):
    def body(o_ref):
        o_ref[...] = jnp.zeros_like(o_ref)

    return _pcall(body, [], name="exchange_only", out_shape=jax.ShapeDtypeStruct((8, 128), F32), comm=comm)


def _all_reduce_small(v, comm=None):
    rows = v.shape[0]

    def body(v_ref, out_ref, buf, send_sems, recv_sems):
        x, y, c = lax.axis_index("x"), lax.axis_index("y"), lax.axis_index("c")
        me = 4 * x + 2 * y + c
        buf[me] = v_ref[...]
        flips = [(fx, fy, fc) for fx in (0, 1) for fy in (0, 1) for fc in (0, 1)][1:]

        def peer(k):
            fx, fy, fc = flips[k]
            px, py, pc = x ^ fx, y ^ fy, c ^ fc
            return (px, py, pc), 4 * px + 2 * py + pc

        def copy(k, slot):
            return pltpu.make_async_remote_copy(
                src_ref=buf.at[slot], dst_ref=buf.at[slot], send_sem=send_sems.at[k],
                recv_sem=recv_sems.at[k], device_id=peer(k)[0], device_id_type=MESH)

        sends = [copy(k, me) for k in range(7)]
        for cp in sends:
            cp.start()
        for k in range(7):
            copy(k, peer(k)[1]).wait_recv()
        for cp in sends:
            cp.wait_send()
        acc = buf[0]
        for d in range(1, 8):
            acc = acc + buf[d]
        out_ref[...] = acc

    return _pcall(body, [v], name="all_reduce_small", out_shape=jax.ShapeDtypeStruct((rows, 128), F32),
                  scratch=[pltpu.VMEM((8, rows, 128), F32), pltpu.SemaphoreType.DMA((7,)),
                           pltpu.SemaphoreType.DMA((7,))], comm=comm)


def _add_halves(g, got):
    _, h, c = got.shape
    th = _row_tile(h, 512)
    nh = h // th
    half = lax.axis_index("c") * nh

    def body(h_ref, a_ref, b_ref, o_ref):
        o_ref[...] = (a_ref[...].astype(F32) + b_ref[...].astype(F32)).astype(o_ref.dtype)

    spec = pl.BlockSpec((1, th, c), lambda j, i, h_ref: (j, i, 0))
    mine = pl.BlockSpec((1, th, c), lambda j, i, h_ref: (j, h_ref[0] + i, 0))
    return _pcall(body, [g, got], name="add_halves", grid=(N_CHIPS, nh), prefetch=[half],
                  out_shape=jax.ShapeDtypeStruct(got.shape, BF), in_specs=[mine, spec], out_specs=spec)


def _sum_chips(parts, recv):
    _, h, c = parts.shape
    th = _row_tile(h, 512)
    me = 2 * lax.axis_index("x") + lax.axis_index("y")

    def body(me_ref, p_ref, r_ref, o_ref):
        acc = p_ref[0].astype(F32)
        for s in range(N_CHIPS - 1):
            acc = acc + r_ref[s].astype(F32)
        o_ref[...] = acc

    return _pcall(body, [parts, recv], name="sum_chips", grid=(h // th,), prefetch=[me],
                  out_shape=jax.ShapeDtypeStruct((h, c), F32),
                  in_specs=[pl.BlockSpec((1, th, c), lambda i, me_ref: (me_ref[0], i, 0)),
                            pl.BlockSpec((N_CHIPS - 1, th, c), lambda i, me_ref: (0, i, 0))],
                  out_specs=pl.BlockSpec((th, c), lambda i, me_ref: (i, 0)))


def _adam_update(w, gv, m, v, d_ref, nm_ref, nv_ref):
    c1 = 1.0 / (1.0 - ADAM_B1 ** ADAM_STEP)
    c2 = 1.0 / (1.0 - ADAM_B2 ** ADAM_STEP)
    nm = ADAM_B1 * m + (1.0 - ADAM_B1) * gv
    nv = ADAM_B2 * v + (1.0 - ADAM_B2) * (gv * gv)
    nm_ref[...] = nm
    nv_ref[...] = nv
    d_ref[...] = -ADAM_LR * ((nm * c1) / (jnp.sqrt(nv * c2) + ADAM_EPS) + ADAM_WD * w)


def _adamw(w, g, m, v, comm=None):
    r, c = w.shape
    tr = _row_tile(r, 512)

    def body(w_ref, g_ref, m_ref, v_ref, d_ref, nm_ref, nv_ref):
        _adam_update(w_ref[...], g_ref[...], m_ref[...], v_ref[...], d_ref, nm_ref, nv_ref)

    spec = pl.BlockSpec((tr, c), lambda i: (i, 0))
    sds = jax.ShapeDtypeStruct((r, c), F32)
    return _pcall(body, [w, g, m, v], name="adamw", grid=(r // tr,), out_shape=[sds, sds, sds],
                  in_specs=[spec] * 4, out_specs=[spec] * 3, comm=comm)


def _adamw_halves(items, comm=None):
    k = len(items)
    r, c = items[0][0].shape
    h = r // 2
    tr = _row_tile(h, min(512, (VMEM_LIMIT * 3 // 4) // (k * 9 * 2 * 4 * c)))
    nb = h // tr
    core = lax.axis_index("c")

    def body(c_ref, *refs):
        ins, outs = refs[:5 * k], refs[5 * k:]
        for q in range(k):
            w_ref, gm_ref, go_ref, m_ref, v_ref = ins[5 * q:5 * q + 5]
            g_ref, d_ref, nm_ref, nv_ref = outs[4 * q:4 * q + 4]
            gv = jnp.where(pl.program_id(0) == c_ref[0], gm_ref[...], go_ref[...])
            g_ref[...] = gv
            _adam_update(w_ref[...], gv, m_ref[...], v_ref[...], d_ref, nm_ref, nv_ref)

    full = pl.BlockSpec((tr, c), lambda hh, i, c_ref: (hh * nb + i, 0))
    half = pl.BlockSpec((tr, c), lambda hh, i, c_ref: (i, 0))
    sds = jax.ShapeDtypeStruct((r, c), F32)
    return _pcall(body, [a for it in items for a in it], name="adamw_halves", grid=(2, nb), prefetch=[core],
                  out_shape=[sds] * (4 * k), in_specs=[full, half, half, full, full] * k, out_specs=[full] * (4 * k),
                  comm=comm)


def _wgrad(name, a, b, a_spec, b_spec, m, n, nb, comm):
    def body(a_ref, b_ref, o_ref):
        o_ref[...] = _dot_tn(a_ref[...], b_ref[...]).astype(o_ref.dtype)

    return _pcall(body, [a, b], name=name, grid=(nb,), out_shape=jax.ShapeDtypeStruct((nb, m, n), BF),
                  in_specs=[a_spec, b_spec], out_specs=pl.BlockSpec((None, m, n), lambda j: (j, 0, 0)), comm=comm)


def _wgrad_cols(name, a, b, nb, comm=None):
    t_tok, m = a.shape
    n = b.shape[1] // nb
    return _wgrad(name, a, b, pl.BlockSpec((t_tok, m), lambda j: (0, 0)), pl.BlockSpec((t_tok, n), lambda j: (0, j)),
                  m, n, nb, comm)


def _wgrad_rows(name, a, b, nb, comm=None):
    t_tok, n = b.shape
    m = a.shape[1] // nb
    return _wgrad(name, a, b, pl.BlockSpec((t_tok, m), lambda j: (0, j)), pl.BlockSpec((t_tok, n), lambda j: (0, 0)),
                  m, n, nb, comm)


def _wgrad_a_shared(name, a, b4, comm=None):
    t_tok, m = a.shape
    nb, _, n = b4.shape
    return _wgrad(name, a, b4, pl.BlockSpec((t_tok, m), lambda j: (0, 0)),
                  pl.BlockSpec((None, t_tok, n), lambda j: (j, 0, 0)), m, n, nb, comm)


def _wgrad_b_shared(name, a4, b, comm=None):
    nb, t_tok, m = a4.shape
    n = b.shape[1]
    return _wgrad(name, a4, b, pl.BlockSpec((None, t_tok, m), lambda j: (j, 0, 0)),
                  pl.BlockSpec((t_tok, n), lambda j: (0, 0)), m, n, nb, comm)


def _w4_spec(r, c):
    return pl.BlockSpec((None, r, c), lambda i, j: (j, 0, 0))


FFN_ROW_CHUNK = 256


def _row_chunks(tm):
    rc = FFN_ROW_CHUNK if tm % FFN_ROW_CHUNK == 0 else tm
    return [slice(r, r + rc) for r in range(0, tm, rc)]


def _ffn_fwd(h, ln, wg4, wu4, wd4, comm=None):
    t_tok, d = h.shape
    f = wg4.shape[-2]
    tm = _tile(t_tok, 512)

    def body(h_ref, ln_ref, wg_ref, wu_ref, wd_ref, ho_ref, n_ref, g_ref, u_ref, n_s, acc):
        j = pl.program_id(1)

        @pl.when(j == 0)
        def _():
            xv = h_ref[...]
            nv = (xv * _rstd(xv) * ln_ref[...]).astype(BF)
            n_s[...] = nv
            n_ref[...] = nv
            acc[...] = jnp.zeros_like(acc)

        nv = n_s[...]
        g = _dot_nt(nv, wg_ref[...])
        u = _dot_nt(nv, wu_ref[...])
        g_ref[...] = g.astype(BF)
        u_ref[...] = u.astype(BF)
        a = (g * _sigmoid(g) * u).astype(BF)
        acc[...] += _dot(a, wd_ref[...])

        @pl.when(j == N_CHIPS - 1)
        def _():
            ho_ref[...] = h_ref[...] + 0.5 * acc[...]

    row = pl.BlockSpec((tm, d), lambda i, j: (i, 0))
    gu = pl.BlockSpec((None, tm, f), lambda i, j: (j, i, 0))
    gu_sds = jax.ShapeDtypeStruct((N_CHIPS, t_tok, f), BF)
    return _pcall(
        body, [h, ln, wg4, wu4, wd4], name="ffn_fwd", grid=(t_tok // tm, N_CHIPS),
        out_shape=[jax.ShapeDtypeStruct((t_tok, d), F32), jax.ShapeDtypeStruct((t_tok, d), BF), gu_sds, gu_sds],
        in_specs=[row, pl.BlockSpec((1, d), lambda i, j: (0, 0)), _w4_spec(f, d), _w4_spec(f, d), _w4_spec(f, d)],
        out_specs=[row, row, gu, gu],
        scratch=[pltpu.VMEM((tm, d), BF), pltpu.VMEM((tm, d), F32)], comm=comm)


def _ffn_bwd(dho, h, ln, g4, u4, wg4, wu4, wd4, comm=None):
    t_tok, d = h.shape
    f = wg4.shape[-2]
    tm = _tile(t_tok, 512)

    def body(dho_ref, h_ref, ln_ref, g_ref, u_ref, wg_ref, wu_ref, wd_ref,
             dhi_ref, dln_ref, dg_ref, du_ref, a_ref, dhb_ref, dhb_s, dn_acc):
        i, j = pl.program_id(0), pl.program_id(1)

        @pl.when(j == 0)
        def _():
            dhb = (0.5 * dho_ref[...]).astype(BF)
            dhb_s[...] = dhb
            dhb_ref[...] = dhb
            dn_acc[...] = jnp.zeros_like(dn_acc)

        @pl.when((i == 0) & (j == 0))
        def _():
            dln_ref[...] = jnp.zeros_like(dln_ref)

        for rows in _row_chunks(tm):
            g = g_ref[rows, :].astype(F32)
            u = u_ref[rows, :].astype(F32)
            s = _sigmoid(g)
            sg = g * s
            a_ref[rows, :] = (sg * u).astype(BF)
            da = _dot_nt(dhb_s[rows, :], wd_ref[...])
            dg = (da * u * (s * (1.0 + g * (1.0 - s)))).astype(BF)
            du = (da * sg).astype(BF)
            dg_ref[rows, :] = dg
            du_ref[rows, :] = du
            dn_acc[rows, :] += _dot(dg, wg_ref[...]) + _dot(du, wu_ref[...])

        @pl.when(j == N_CHIPS - 1)
        def _():
            xv = h_ref[...]
            dx, dln = _rms_bwd(dn_acc[...], xv, _rstd(xv), ln_ref[...])
            dln_ref[...] += dln
            dhi_ref[...] = dho_ref[...] + dx

    row = pl.BlockSpec((tm, d), lambda i, j: (i, 0))
    vec = pl.BlockSpec((1, d), lambda i, j: (0, 0))
    gu = pl.BlockSpec((None, tm, f), lambda i, j: (j, i, 0))
    gu_sds = jax.ShapeDtypeStruct((N_CHIPS, t_tok, f), BF)
    return _pcall(
        body, [dho, h, ln, g4, u4, wg4, wu4, wd4], name="ffn_bwd", grid=(t_tok // tm, N_CHIPS),
        out_shape=[jax.ShapeDtypeStruct((t_tok, d), F32), jax.ShapeDtypeStruct((1, d), F32),
                   gu_sds, gu_sds, gu_sds, jax.ShapeDtypeStruct((t_tok, d), BF)],
        in_specs=[row, row, vec, gu, gu, _w4_spec(f, d), _w4_spec(f, d), _w4_spec(f, d)],
        out_specs=[row, vec, gu, gu, gu, row],
        scratch=[pltpu.VMEM((tm, d), BF), pltpu.VMEM((tm, d), F32)], comm=comm)


def _rope_tables(pos_col, inv_freq2, comm=None):
    t_tok = pos_col.shape[0]

    def body(p_ref, f_ref, cos_ref, sin_ref):
        ang = p_ref[...] * f_ref[...]
        lane = lax.broadcasted_iota(jnp.int32, ang.shape, 1)
        s = jnp.sin(ang)
        cos_ref[...] = jnp.cos(ang)
        sin_ref[...] = jnp.where((lane & 1) == 0, -s, s)

    sds = jax.ShapeDtypeStruct((t_tok, 128), F32)
    return _pcall(body, [pos_col, inv_freq2], name="rope_tables", out_shape=[sds, sds], comm=comm)


def _swap_pairs(x):
    lane = lax.broadcasted_iota(jnp.int32, x.shape, 1)
    return jnp.where((lane & 1) == 0, pltpu.roll(x, 127, 1), pltpu.roll(x, 1, 1))


def _mix_in(h, ln, w_in, wm4, b_m, cos_t, sin_t, comm=None):
    t_tok, d = h.shape
    cm = wm4.shape[-1]
    tm = _tile(t_tok, 256)

    def body(h_ref, ln_ref, win_ref, wm_ref, bm_ref, cos_ref, sin_ref,
             u_ref, rq_ref, rk_ref, rv_ref, rg_ref, fq_ref, fk_ref, fv_ref, ff_ref, ga_ref, gb_ref):
        xv = h_ref[...]
        ub = (xv * _rstd(xv) * ln_ref[...]).astype(BF)
        u_ref[...] = ub
        cosv, sinv = cos_ref[...], sin_ref[...]

        def sec(k):
            return _dot_nt(ub, win_ref[k * 512:(k + 1) * 512, :])

        def rot(xh):
            return xh * cosv + _swap_pairs(xh) * sinv

        pq, pk = sec(0), sec(1)
        for hh in range(RET_HEADS):
            sl = slice(hh * RET_DIM, (hh + 1) * RET_DIM)
            rq_ref[:, sl] = rot(pq[:, sl]).astype(BF)
            rk_ref[:, sl] = (rot(pk[:, sl]) * RET_SCALE).astype(BF)
        rv_ref[...] = sec(2).astype(BF)
        rg_ref[...] = sec(3).astype(BF)
        fq_ref[...] = (sec(4) * FOX_SCALE).astype(BF)
        fk_ref[...] = sec(5).astype(BF)
        fv_ref[...] = sec(6).astype(BF)
        ff_ref[...] = _dot_nt(ub, win_ref[FF_COL:FF_COL + 128, :])
        for j in range(N_CHIPS):
            gs = _sigmoid(_dot(ub, wm_ref[j]) + bm_ref[:, j * cm:(j + 1) * cm]).astype(BF)
            col = j * cm
            if col < d:
                ga_ref[:, col:col + cm] = gs
            else:
                gb_ref[:, col - d:col - d + cm] = gs

    row = lambda c: pl.BlockSpec((tm, c), lambda i: (i, 0))
    full = lambda *s: pl.BlockSpec(s, lambda i: (0,) * len(s))
    sds = lambda c, dt: jax.ShapeDtypeStruct((t_tok, c), dt)
    return _pcall(
        body, [h, ln, w_in, wm4, b_m, cos_t, sin_t], name="mix_in", grid=(t_tok // tm,),
        out_shape=[sds(d, BF)] + [sds(512, BF)] * 7 + [sds(128, F32), sds(d, BF), sds(d, BF)],
        in_specs=[row(d), full(1, d), full(IN_PAD, d), full(N_CHIPS, d, cm), full(1, 2 * d), row(128), row(128)],
        out_specs=[row(d)] + [row(512)] * 7 + [row(128), row(d), row(d)], comm=comm)


def _split3(x):
    hi = x.astype(BF)
    r1 = x - hi.astype(F32)
    mid = r1.astype(BF)
    lo = (r1 - mid.astype(F32)).astype(BF)
    return hi, mid, lo


def _aug_lane():
    return lax.broadcasted_iota(jnp.int32, (1, 128), 1) & (FOX_DIM - 1)


def _aug_put(base, k0, parts):
    w = _aug_lane()
    for i, part in enumerate(parts):
        base = jnp.where(w == k0 + i, part, base)
    return base


def _forget_fwd(ffl, b_pad):
    t_tok = ffl.shape[0]
    tb = _tile(t_tok, 256)

    def body(ff_ref, b_ref, aq_ref, ak_ref, cum_s):
        r = lax.broadcasted_iota(jnp.int32, (tb, tb), 0)
        c = lax.broadcasted_iota(jnp.int32, (tb, tb), 1)
        tri = jnp.where(c <= r, 1.0, 0.0).astype(BF)
        carry = jnp.zeros((1, 128), F32)
        for i in range(t_tok // tb):
            z = ff_ref[i * tb:(i + 1) * tb, :] + b_ref[...]
            lf = jnp.minimum(z, 0.0) - jnp.log(1.0 + jnp.exp(-jnp.abs(z)))
            hi, mid, lo = _split3(lf)
            cs = _dot(tri, hi) + _dot(tri, mid) + _dot(tri, lo) + carry
            cum_s[i * tb:(i + 1) * tb, :] = cs
            carry = cs[tb - 1:tb, :]
        x = cum_s[...]
        first = lax.broadcasted_iota(jnp.int32, (1, 128), 1) < FOX_DIM
        w = _aug_lane()
        one = jnp.ones((t_tok, 128), BF)
        zero = jnp.zeros((t_tok, 128), BF)
        for pp in range(FOX_HEADS // 2):
            other = jnp.where(first, x[:, 2 * pp + 1:2 * pp + 2], x[:, 2 * pp:2 * pp + 1])
            parts = _split3(other)
            aq = jnp.where((w >= 3) & (w < 6), one, zero)
            ak = jnp.where((w < 3) | ((w >= 6) & (w < 9)), one, zero)
            aq_ref[:, pp * 128:(pp + 1) * 128] = _aug_put(aq, 0, parts)
            ak_ref[:, pp * 128:(pp + 1) * 128] = _aug_put(ak, 3, [-q for q in parts])

    sds = jax.ShapeDtypeStruct((t_tok, FOX_WIDTH), BF)
    return _pcall(body, [ffl, b_pad], name="forget_fwd", out_shape=[sds, sds],
                  scratch=[pltpu.VMEM((t_tok, 128), F32)])


def _fox_aug_lse(aq, lse_e):
    t_tok = aq.shape[0]
    tm = _tile(t_tok, 512)

    def body(aq_ref, lse_ref, o_ref):
        for pp in range(FOX_HEADS // 2):
            sl = slice(pp * 128, (pp + 1) * 128)
            other = pltpu.roll(lse_ref[:, sl], FOX_DIM, 1)
            o_ref[:, sl] = _aug_put(aq_ref[:, sl], 6, _split3(-other))

    spec = pl.BlockSpec((tm, FOX_WIDTH), lambda i: (i, 0))
    return _pcall(body, [aq, lse_e], name="fox_aug_lse", grid=(t_tok // tm,),
                  out_shape=jax.ShapeDtypeStruct((t_tok, FOX_WIDTH), BF), in_specs=[spec, spec], out_specs=spec)


def _forget_bwd(dcum_t, dcum_q, ffl, b_pad):
    t_tok = ffl.shape[0]
    tb = _tile(t_tok, 256)

    def body(dc_ref, dq_ref, ff_ref, b_ref, dff_ref, db_ref, pad_s, d_s):
        pad_s[...] = jnp.zeros_like(pad_s)
        pad_s[0:FOX_HEADS, :] = dc_ref[...]
        dsum = pad_s[...].T
        lane = lax.broadcasted_iota(jnp.int32, (t_tok, 128), 1)
        for hh in range(FOX_HEADS):
            dsum = dsum + jnp.where(lane == hh, dq_ref[:, hh * FOX_DIM:hh * FOX_DIM + 1], 0.0)
        d_s[...] = dsum
        r = lax.broadcasted_iota(jnp.int32, (tb, tb), 0)
        c = lax.broadcasted_iota(jnp.int32, (tb, tb), 1)
        tri = jnp.where(c >= r, 1.0, 0.0).astype(BF)
        carry = jnp.zeros((1, 128), F32)
        db = jnp.zeros((1, 128), F32)
        for i in reversed(range(t_tok // tb)):
            hi, mid, lo = _split3(d_s[i * tb:(i + 1) * tb, :])
            dlf = _dot(tri, hi) + _dot(tri, mid) + _dot(tri, lo) + carry
            carry = dlf[0:1, :]
            z = ff_ref[i * tb:(i + 1) * tb, :] + b_ref[...]
            dff = dlf * _sigmoid(-z)
            dff_ref[i * tb:(i + 1) * tb, :] = dff.astype(BF)
            db = db + jnp.sum(dff, axis=0, keepdims=True)
        db_ref[...] = db

    return _pcall(
        body, [dcum_t, dcum_q, ffl, b_pad], name="forget_bwd",
        out_shape=[jax.ShapeDtypeStruct((t_tok, 128), BF), jax.ShapeDtypeStruct((1, 128), F32)],
        scratch=[pltpu.VMEM((128, t_tok), F32), pltpu.VMEM((t_tok, 128), F32)])


def _first_half():
    return lax.broadcasted_iota(jnp.int32, (1, 128), 1) < FOX_DIM


def _head_rows(x2, a2, hh):
    return jnp.where(_first_half(), x2, a2) if hh == 0 else jnp.where(_first_half(), a2, x2)


def _head_only(x2, hh):
    zero = jnp.zeros_like(x2)
    return jnp.where(_first_half(), x2, zero) if hh == 0 else jnp.where(_first_half(), zero, x2)


def _causal_diag(s):
    rows = lax.broadcasted_iota(jnp.int32, s.shape, 0)
    cols = lax.broadcasted_iota(jnp.int32, s.shape, 1)
    return jnp.where(cols <= rows, s, NEG)


def _diag_or_below(qi, ki, step):
    pl.when(ki < qi)(lambda: step(False))
    pl.when(ki == qi)(lambda: step(True))


def _tri_rows(s, n):
    qi = sum((s >= r * (r + 1) // 2).astype(jnp.int32) for r in range(1, n))
    return qi, s - (qi * (qi + 1)) // 2


def _tri_cols(s, n):
    ki = sum((s >= k * n - k * (k - 1) // 2).astype(jnp.int32) for k in range(1, n))
    return ki, ki + s - (ki * n - (ki * (ki - 1)) // 2)


def _fox_fwd(fq, fk, fv, aq, ak, comm=None):
    t_tok = fq.shape[0]
    t = _tile(t_tok, 512)
    nq = t_tok // t
    npair = FOX_HEADS // 2

    def body(q_ref, k_ref, v_ref, aq_ref, ak_ref, o_ref, of_ref, lse_ref, m_s, l_s, acc_s):
        qi, ki = _tri_rows(pl.program_id(1), nq)

        @pl.when(ki == 0)
        def _():
            m_s[...] = jnp.full_like(m_s, NEG)
            l_s[...] = jnp.zeros_like(l_s)
            acc_s[...] = jnp.zeros_like(acc_s)

        def step(diag):
            q2, k2, v2, aq2, ak2 = q_ref[...], k_ref[...], v_ref[...], aq_ref[...], ak_ref[...]
            for hh in range(2):
                s = _dot_nt(_head_rows(q2, aq2, hh), _head_rows(k2, ak2, hh))
                if diag:
                    s = _causal_diag(s)
                m_prev = m_s[hh]
                m_new = jnp.maximum(m_prev, jnp.max(s, axis=1, keepdims=True))
                alpha = jnp.exp(m_prev - m_new)
                p = jnp.exp(s - jnp.tile(m_new, (1, t // 128)))
                l_s[hh] = alpha * l_s[hh] + jnp.sum(p, axis=1, keepdims=True)
                acc_s[hh] = alpha * acc_s[hh] + _dot(p.astype(BF), v2)
                m_s[hh] = m_new

        _diag_or_below(qi, ki, step)

        @pl.when(ki == qi)
        def _():
            first = _first_half()
            o = jnp.where(first, acc_s[0] / l_s[0], acc_s[1] / l_s[1])
            o_ref[...] = o.astype(BF)
            of_ref[...] = o
            lse_ref[...] = jnp.where(first, m_s[0] + jnp.log(l_s[0]), m_s[1] + jnp.log(l_s[1]))

    qs = pl.BlockSpec((t, 128), lambda p, s: (_tri_rows(s, nq)[0], p))
    ks = pl.BlockSpec((t, 128), lambda p, s: (_tri_rows(s, nq)[1], p))
    stat = pltpu.VMEM((2, t, 128), F32)
    return _pcall(
        body, [fq, fk, fv, aq, ak], name="fox_fwd", grid=(npair, nq * (nq + 1) // 2),
        out_shape=[jax.ShapeDtypeStruct((t_tok, FOX_WIDTH), BF), jax.ShapeDtypeStruct((t_tok, FOX_WIDTH), F32),
                   jax.ShapeDtypeStruct((t_tok, FOX_WIDTH), F32)],
        in_specs=[qs, ks, ks, qs, ks], out_specs=[qs, qs, qs], scratch=[stat, stat, stat], comm=comm)


def _fox_ds(q2, k2, v2, do2, aq2, ak2, ad2, hh, diag):
    s = _dot_nt(_head_rows(q2, aq2, hh), _head_rows(k2, ak2, hh))
    if diag:
        s = _causal_diag(s)
    p = jnp.exp(s)
    av = jnp.where(_aug_lane() < 3, 1.0, 0.0).astype(BF)
    dp = _dot_nt(_head_rows(do2, ad2, hh), _head_rows(v2, jnp.broadcast_to(av, v2.shape), hh))
    return p, p * dp


def _fox_bwd(fq, fk, fv, do, aqb, ak, ad, comm=None):
    t_tok = fq.shape[0]
    t = _tile(t_tok, 512)
    nq = t_tok // t
    npair = FOX_HEADS // 2
    n_steps = nq * (nq + 1) // 2

    def body(q_ref, k_ref, v_ref, do_ref, aq_ref, ak_ref, ad_ref, dq_ref, dk_ref, dv_ref, dck_ref, dcq_ref,
             dk_s, dv_s, dq_s, rs_s):
        step_id = pl.program_id(1)
        ki, qi = _tri_cols(step_id, nq)

        @pl.when(step_id == 0)
        def _():
            dq_s[...] = jnp.zeros_like(dq_s)
            rs_s[...] = jnp.zeros_like(rs_s)

        @pl.when(qi == ki)
        def _():
            dk_s[...] = jnp.zeros_like(dk_s)
            dv_s[...] = jnp.zeros_like(dv_s)
            dck_ref[...] = jnp.zeros_like(dck_ref)

        rows = pl.ds(qi * t if isinstance(qi, int) else pl.multiple_of(qi * t, t), t)

        def step(diag):
            q2, k2, v2, do2 = q_ref[...], k_ref[...], v_ref[...], do_ref[...]
            dq = []
            for hh in range(2):
                p, ds = _fox_ds(q2, k2, v2, do2, aq_ref[...], ak_ref[...], ad_ref[...], hh, diag)
                dsb = ds.astype(BF)
                dv_s[...] += _dot_tn(p.astype(BF), _head_only(do2, hh))
                dk_s[...] += _dot_tn(dsb, _head_only(q2, hh))
                dq.append(_dot(dsb, k2))
                dck_ref[hh] = dck_ref[hh] - jnp.sum(ds, axis=0, keepdims=True)
                rs_s[hh, rows, :] = rs_s[hh, rows, :] + jnp.sum(ds, axis=1, keepdims=True)
            dq_s[rows, :] = dq_s[rows, :] + jnp.where(_first_half(), dq[0], dq[1])

        _diag_or_below(qi, ki, step)

        @pl.when(qi == nq - 1)
        def _():
            dk_ref[...] = dk_s[...].astype(BF)
            dv_ref[...] = dv_s[...].astype(BF)

        @pl.when(step_id == n_steps - 1)
        def _():
            dq_ref[...] = (dq_s[...] * FOX_SCALE).astype(BF)
            dcq_ref[...] = jnp.where(_first_half(), rs_s[0], rs_s[1])

    qs = pl.BlockSpec((t, 128), lambda p, s: (_tri_cols(s, nq)[1], p))
    ks = pl.BlockSpec((t, 128), lambda p, s: (_tri_cols(s, nq)[0], p))
    cks = pl.BlockSpec((2, 1, t), lambda p, s: (p, 0, _tri_cols(s, nq)[0]))
    seq = pl.BlockSpec((t_tok, 128), lambda p, s: (0, p))
    sds = jax.ShapeDtypeStruct((t_tok, FOX_WIDTH), BF)
    return _pcall(
        body, [fq, fk, fv, do, aqb, ak, ad], name="fox_bwd", grid=(npair, n_steps),
        out_shape=[sds, sds, sds, jax.ShapeDtypeStruct((FOX_HEADS, 1, t_tok), F32),
                   jax.ShapeDtypeStruct((t_tok, FOX_WIDTH), F32)],
        in_specs=[qs, ks, ks, qs, qs, ks, qs], out_specs=[seq, ks, ks, cks, seq],
        scratch=[pltpu.VMEM((t, 128), F32), pltpu.VMEM((t, 128), F32), pltpu.VMEM((t_tok, 128), F32),
                 pltpu.VMEM((2, t_tok, 128), F32)], comm=comm)


def _ret_consts():
    c = RET_CHUNK
    log_gamma = jnp.log1p(-jnp.exp2(-5.0 - jnp.arange(RET_HEADS, dtype=F32)))
    idx = jnp.arange(c, dtype=F32)
    diff = idx[:, None] - idx[None, :]
    dmask = jnp.where(diff >= 0, jnp.exp(log_gamma[:, None, None] * jnp.maximum(diff, 0.0)), 0.0)
    qdec = jnp.exp(log_gamma[:, None] * (idx + 1.0))
    kdec = jnp.exp(log_gamma[:, None] * (c - 1 - idx))
    cdec = jnp.exp(log_gamma * c)
    bc = lambda v: jnp.broadcast_to(v[:, :, None], (RET_HEADS, c, RET_DIM))
    return dmask, bc(qdec), bc(kdec), jnp.broadcast_to(cdec[:, None, None], (RET_HEADS, c, RET_DIM))


def _group_norm(y):
    mu = jnp.mean(y, axis=-1, keepdims=True)
    yc = y - mu
    r = lax.rsqrt(jnp.mean(yc * yc, axis=-1, keepdims=True) + EPS)
    return yc * r, r


def _ret_fwd(rq, rk, rv, rg, consts, comm=None):
    t_tok = rq.shape[0]
    nb = 4 if t_tok % (4 * RET_CHUNK) == 0 else 1
    tr = nb * RET_CHUNK
    n_steps = t_tok // tr
    c = RET_CHUNK

    def body(q_ref, k_ref, v_ref, g_ref, dm_ref, qd_ref, kd_ref, cd_ref, y_ref, yo_ref, st_ref, s_s):
        @pl.when(pl.program_id(0) == 0)
        def _():
            s_s[...] = jnp.zeros_like(s_s)

        for b in range(nb):
            rows = slice(b * c, (b + 1) * c)
            for hh in range(RET_HEADS):
                cols = slice(hh * RET_DIM, (hh + 1) * RET_DIM)
                q, k, v = q_ref[rows, cols], k_ref[rows, cols], v_ref[rows, cols]
                state = s_s[hh]
                st_ref[hh, b] = state
                sc = (_dot_nt(q, k) * dm_ref[hh]).astype(BF)
                y = _dot(sc, v) + _dot((q.astype(F32) * qd_ref[hh]).astype(BF), state.astype(BF))
                s_s[hh] = cd_ref[hh] * state + _dot_tn((k.astype(F32) * kd_ref[hh]).astype(BF), v)
                y_ref[rows, cols] = y
                yn, _ = _group_norm(y)
                gate = g_ref[rows, cols].astype(F32)
                yo_ref[rows, cols] = (yn * (gate * _sigmoid(gate))).astype(BF)

    blk = pl.BlockSpec((tr, RET_WIDTH), lambda i: (i, 0))
    cst = pl.BlockSpec((RET_HEADS, c, RET_DIM), lambda i: (0, 0, 0))
    return _pcall(
        body, [rq, rk, rv, rg, *consts], name="ret_fwd", grid=(n_steps,),
        out_shape=[jax.ShapeDtypeStruct((t_tok, RET_WIDTH), F32), jax.ShapeDtypeStruct((t_tok, RET_WIDTH), BF),
                   jax.ShapeDtypeStruct((RET_HEADS, t_tok // c, RET_DIM, RET_DIM), F32)],
        in_specs=[blk] * 4 + [cst] * 4,
        out_specs=[blk, blk, pl.BlockSpec((RET_HEADS, nb, RET_DIM, RET_DIM), lambda i: (0, i, 0, 0))],
        scratch=[pltpu.VMEM((RET_HEADS, RET_DIM, RET_DIM), F32)], comm=comm)


def _ret_bwd(rq, rk, rv, rg, y_raw, dyo, states, consts, cos_t, sin_t, comm=None):
    t_tok = rq.shape[0]
    nb = 4 if t_tok % (4 * RET_CHUNK) == 0 else 1
    tr = nb * RET_CHUNK
    n_steps = t_tok // tr
    c = RET_CHUNK

    def body(q_ref, k_ref, v_ref, g_ref, y_ref, dyo_ref, st_ref, dm_ref, qd_ref, kd_ref, cd_ref,
             cos_ref, sin_ref, dq_ref, dk_ref, dv_ref, dg_ref, ds_s):
        @pl.when(pl.program_id(0) == 0)
        def _():
            ds_s[...] = jnp.zeros_like(ds_s)

        for b in reversed(range(nb)):
            rows = slice(b * c, (b + 1) * c)
            cosv, sinv = cos_ref[rows, :], sin_ref[rows, :]
            for hh in range(RET_HEADS):
                cols = slice(hh * RET_DIM, (hh + 1) * RET_DIM)
                dm, qd, kd, cd = dm_ref[hh], qd_ref[hh], kd_ref[hh], cd_ref[hh]
                q, k, v = q_ref[rows, cols], k_ref[rows, cols], v_ref[rows, cols]
                yn, r = _group_norm(y_ref[rows, cols])
                gate = g_ref[rows, cols].astype(F32)
                sg = _sigmoid(gate)
                dyo = dyo_ref[rows, cols]
                dg_ref[rows, cols] = (dyo * yn * (sg * (1.0 + gate * (1.0 - sg)))).astype(BF)
                dyn = dyo * (gate * sg)
                dy = r * (dyn - jnp.mean(dyn, axis=-1, keepdims=True)
                          - yn * jnp.mean(dyn * yn, axis=-1, keepdims=True))
                dyb = dy.astype(BF)
                state_b = st_ref[hh, b].astype(BF)
                dstate = ds_s[hh]
                dstate_b = dstate.astype(BF)
                qdb = (q.astype(F32) * qd).astype(BF)
                kdb = (k.astype(F32) * kd).astype(BF)
                sc = (_dot_nt(q, k) * dm).astype(BF)
                dv = _dot_tn(sc, dyb) + _dot(kdb, dstate_b)
                dp = (_dot_nt(dyb, v) * dm).astype(BF)
                dq = _dot(dp, k) + _dot_nt(dyb, state_b) * qd
                dk = (_dot_tn(dp, q) + _dot_nt(v, dstate_b) * kd) * RET_SCALE
                ds_s[hh] = cd * dstate + _dot_tn(qdb, dyb)
                dv_ref[rows, cols] = dv.astype(BF)
                dq_ref[rows, cols] = (dq * cosv - _swap_pairs(dq) * sinv).astype(BF)
                dk_ref[rows, cols] = (dk * cosv - _swap_pairs(dk) * sinv).astype(BF)

    rev = lambda i: n_steps - 1 - i
    blk = pl.BlockSpec((tr, RET_WIDTH), lambda i: (rev(i), 0))
    tab = pl.BlockSpec((tr, RET_DIM), lambda i: (rev(i), 0))
    cst = pl.BlockSpec((RET_HEADS, c, RET_DIM), lambda i: (0, 0, 0))
    sds = jax.ShapeDtypeStruct((t_tok, RET_WIDTH), BF)
    return _pcall(
        body, [rq, rk, rv, rg, y_raw, dyo, states, *consts, cos_t, sin_t], name="ret_bwd",
        grid=(n_steps,), out_shape=[sds] * 4,
        in_specs=[blk] * 6 + [pl.BlockSpec((RET_HEADS, nb, RET_DIM, RET_DIM), lambda i: (0, rev(i), 0, 0))]
        + [cst] * 4 + [tab, tab],
        out_specs=[blk] * 4, scratch=[pltpu.VMEM((RET_HEADS, RET_DIM, RET_DIM), F32)], comm=comm)


def _mix_out(h, y_ret, y_fox, ga, gb, wr4, wf4, wo4, comm=None):
    t_tok, d = h.shape
    cz = wr4.shape[-1]
    ro = wo4.shape[-2]
    tm = _tile(t_tok, 512)

    def body(h_ref, yr_ref, yf_ref, ga_ref, gb_ref, wr_ref, wf_ref, wo_ref, ho_ref, za_ref, zb_ref, mix_ref):
        yr, yf = yr_ref[...], yf_ref[...]
        for j in range(N_CHIPS):
            sl = slice(j * cz, (j + 1) * cz)
            za = _dot(yr, wr_ref[j])
            zb = _dot(yf, wf_ref[j])
            za_ref[:, sl] = za.astype(BF)
            zb_ref[:, sl] = zb.astype(BF)
            mix_ref[:, sl] = (ga_ref[:, sl].astype(F32) * za + gb_ref[:, sl].astype(F32) * zb).astype(BF)
        acc = h_ref[...]
        for j in range(N_CHIPS):
            acc = acc + _dot(mix_ref[:, j * ro:(j + 1) * ro], wo_ref[j])
        ho_ref[...] = acc

    row = lambda c: pl.BlockSpec((tm, c), lambda i: (i, 0))
    full = lambda *s: pl.BlockSpec(s, lambda i: (0,) * len(s))
    sds = lambda dt: jax.ShapeDtypeStruct((t_tok, d), dt)
    return _pcall(
        body, [h, y_ret, y_fox, ga, gb, wr4, wf4, wo4], name="mix_out", grid=(t_tok // tm,),
        out_shape=[sds(F32), sds(BF), sds(BF), sds(BF)],
        in_specs=[row(d), row(RET_WIDTH), row(FOX_WIDTH), row(d), row(d),
                  full(N_CHIPS, RET_WIDTH, cz), full(N_CHIPS, FOX_WIDTH, cz), full(N_CHIPS, ro, d)],
        out_specs=[row(d)] * 4, comm=comm)


def _mix_out_bwd(dh, za, zb, ga, gb, y_fox, wr4, wf4, wo4, comm=None):
    t_tok, d = dh.shape
    cz = wr4.shape[-1]
    ro = wo4.shape[-2]
    tm = _tile(t_tok, 256)

    def body(dh_ref, za_ref, zb_ref, ga_ref, gb_ref, yf_ref, wr_ref, wf_ref, wo_ref,
             dhb_ref, dgp_ref, dza_ref, dzb_ref, dyr_ref, dyf_ref, dl_ref, db_ref):
        @pl.when(pl.program_id(0) == 0)
        def _():
            db_ref[...] = jnp.zeros_like(db_ref)

        dhb = dh_ref[...].astype(BF)
        dhb_ref[...] = dhb
        dyr = jnp.zeros((tm, RET_WIDTH), F32)
        dyf = jnp.zeros((tm, FOX_WIDTH), F32)
        for j in range(N_CHIPS):
            sl = slice(j * ro, (j + 1) * ro)
            dmix = _dot_nt(dhb, wo_ref[j])
            ga, gb = ga_ref[:, sl].astype(F32), gb_ref[:, sl].astype(F32)
            dza = (dmix * ga).astype(BF)
            dzb = (dmix * gb).astype(BF)
            dza_ref[:, sl] = dza
            dzb_ref[:, sl] = dzb
            dga = dmix * za_ref[:, sl].astype(F32) * ga * (1.0 - ga)
            dgb = dmix * zb_ref[:, sl].astype(F32) * gb * (1.0 - gb)
            dgp_ref[:, sl] = dga.astype(BF)
            dgp_ref[:, d + j * ro:d + (j + 1) * ro] = dgb.astype(BF)
            db_ref[:, sl] += jnp.sum(dga, axis=0, keepdims=True)
            db_ref[:, d + j * ro:d + (j + 1) * ro] += jnp.sum(dgb, axis=0, keepdims=True)
        for j in range(N_CHIPS):
            sl = slice(j * cz, (j + 1) * cz)
            dyr = dyr + _dot_nt(dza_ref[:, sl], wr_ref[j])
            dyf = dyf + _dot_nt(dzb_ref[:, sl], wf_ref[j])
        dyr_ref[...] = dyr
        dyfb = dyf.astype(BF)
        dyf_ref[...] = dyfb
        prod = dyfb.astype(F32) * yf_ref[...]
        first = _first_half()
        for pp in range(FOX_HEADS // 2):
            blk = prod[:, pp * 128:(pp + 1) * 128]
            s0 = jnp.sum(jnp.where(first, blk, 0.0), axis=1, keepdims=True)
            s1 = jnp.sum(jnp.where(first, 0.0, blk), axis=1, keepdims=True)
            parts = _split3(-jnp.where(first, s1, s0))
            dl_ref[:, pp * 128:(pp + 1) * 128] = _aug_put(jnp.zeros((tm, 128), BF), 0, parts)

    row = lambda c: pl.BlockSpec((tm, c), lambda i: (i, 0))
    full = lambda *s: pl.BlockSpec(s, lambda i: (0,) * len(s))
    sds = lambda c, dt: jax.ShapeDtypeStruct((t_tok, c), dt)
    return _pcall(
        body, [dh, za, zb, ga, gb, y_fox, wr4, wf4, wo4], name="mix_out_bwd", grid=(t_tok // tm,),
        out_shape=[sds(d, BF), sds(2 * d, BF), sds(d, BF), sds(d, BF), sds(RET_WIDTH, F32),
                   sds(FOX_WIDTH, BF), sds(FOX_WIDTH, BF), jax.ShapeDtypeStruct((1, 2 * d), F32)],
        in_specs=[row(d)] * 5 + [row(FOX_WIDTH), full(N_CHIPS, RET_WIDTH, cz), full(N_CHIPS, FOX_WIDTH, cz),
                                 full(N_CHIPS, ro, d)],
        out_specs=[row(d), row(2 * d), row(d), row(d), row(RET_WIDTH), row(FOX_WIDTH), row(FOX_WIDTH),
                   full(1, 2 * d)],
        comm=comm)


def _mix_in_bwd(dh, h, ln, parts, dff, dgpre, w_in, wm4, comm=None):
    t_tok, d = h.shape
    cm = wm4.shape[-1]
    tm = _tile(t_tok, 256)

    def body(dh_ref, h_ref, ln_ref, p0, p1, p2, p3, p4, p5, p6, dff_ref, dgp_ref, win_ref, wm_ref,
             dhi_ref, dln_ref, dproj_ref):
        @pl.when(pl.program_id(0) == 0)
        def _():
            dln_ref[...] = jnp.zeros_like(dln_ref)

        for k, pr in enumerate((p0, p1, p2, p3, p4, p5, p6)):
            dproj_ref[:, k * 512:(k + 1) * 512] = pr[...]
        dproj_ref[:, FF_COL:FF_COL + 128] = dff_ref[...]
        dproj_ref[:, FF_COL + 128:] = jnp.zeros((tm, IN_PAD - FF_COL - 128), BF)
        du = _dot(dproj_ref[...], win_ref[...])
        for j in range(N_CHIPS):
            du = du + _dot_nt(dgp_ref[:, j * cm:(j + 1) * cm], wm_ref[j])
        xv = h_ref[...]
        dx, dln = _rms_bwd(du, xv, _rstd(xv), ln_ref[...])
        dln_ref[...] += dln
        dhi_ref[...] = dh_ref[...] + dx

    row = lambda c: pl.BlockSpec((tm, c), lambda i: (i, 0))
    full = lambda *s: pl.BlockSpec(s, lambda i: (0,) * len(s))
    return _pcall(
        body, [dh, h, ln, *parts, dff, dgpre, w_in, wm4], name="mix_in_bwd", grid=(t_tok // tm,),
        out_shape=[jax.ShapeDtypeStruct((t_tok, d), F32), jax.ShapeDtypeStruct((1, d), F32),
                   jax.ShapeDtypeStruct((t_tok, IN_PAD), BF)],
        in_specs=[row(d), row(d), full(1, d)] + [row(512)] * 7 + [row(128), row(2 * d), full(IN_PAD, d),
                                                                   full(N_CHIPS, d, cm)],
        out_specs=[row(d), full(1, d), row(IN_PAD)], comm=comm)


def _tail(h, p, target, ln_ple, ln_fin, wpg4, wpl4, comm=None):
    t_tok, d = h.shape
    pd = p.shape[1]
    rg = wpg4.shape[-2]
    cp = wpl4.shape[-1]
    tm = _tile(t_tok, 256)

    def body(h_ref, p_ref, t_ref, lp_ref, lf_ref, wg_ref, wp_ref,
             dh_ref, n_ref, dgp_ref, dpe_ref, pb_ref, loss_ref, dlf_ref, dlp_ref, pe_s, dn_s):
        @pl.when(pl.program_id(0) == 0)
        def _():
            loss_ref[...] = jnp.zeros_like(loss_ref)
            dlf_ref[...] = jnp.zeros_like(dlf_ref)
            dlp_ref[...] = jnp.zeros_like(dlp_ref)

        xv = h_ref[...]
        r3 = _rstd(xv)
        nb = (xv * r3 * lp_ref[...]).astype(BF)
        n_ref[...] = nb
        pb = p_ref[...].astype(BF)
        pb_ref[...] = pb
        pgpre = jnp.zeros((tm, d), F32)
        for j in range(N_CHIPS):
            pgpre = pgpre + _dot(nb[:, j * rg:(j + 1) * rg], wg_ref[j])
            pe_s[:, j * cp:(j + 1) * cp] = _dot(pb, wp_ref[j])
        pg = _sigmoid(pgpre)
        pe = pe_s[...]
        h4 = xv + pg * pe
        r4 = _rstd(h4)
        err = h4 * r4 * lf_ref[...] - t_ref[...]
        loss_ref[...] += 0.5 * jnp.sum(jnp.sum(err * err, axis=1, keepdims=True), axis=0, keepdims=True) / d
        dh4, dlf = _rms_bwd(err * (1.0 / d), h4, r4, lf_ref[...])
        dlf_ref[...] += dlf
        dpe_ref[...] = (dh4 * pg).astype(BF)
        dgp = (dh4 * pe * pg * (1.0 - pg)).astype(BF)
        dgp_ref[...] = dgp
        for j in range(N_CHIPS):
            dn_s[:, j * rg:(j + 1) * rg] = _dot_nt(dgp, wg_ref[j])
        dx, dlp = _rms_bwd(dn_s[...], xv, r3, lp_ref[...])
        dlp_ref[...] += dlp
        dh_ref[...] = dh4 + dx

    row = lambda c: pl.BlockSpec((tm, c), lambda i: (i, 0))
    full = lambda *s: pl.BlockSpec(s, lambda i: (0,) * len(s))
    sds = lambda c, dt: jax.ShapeDtypeStruct((t_tok, c), dt)
    vec = jax.ShapeDtypeStruct((1, d), F32)
    return _pcall(
        body, [h, p, target, ln_ple, ln_fin, wpg4, wpl4], name="tail", grid=(t_tok // tm,),
        out_shape=[sds(d, F32), sds(d, BF), sds(d, BF), sds(d, BF), sds(pd, BF),
                   jax.ShapeDtypeStruct((1, 128), F32), vec, vec],
        in_specs=[row(d), row(pd), row(d), full(1, d), full(1, d), full(N_CHIPS, rg, d), full(N_CHIPS, pd, cp)],
        out_specs=[row(d), row(d), row(d), row(d), row(pd), full(1, 128), full(1, d), full(1, d)],
        scratch=[pltpu.VMEM((tm, d), F32), pltpu.VMEM((tm, d), F32)], comm=comm)


BIG = ["w_ffn1_gate", "w_ffn1_up", "w_ffn1_down", "w_in", "w_merge", "w_ret_out", "w_fox_out", "w_out",
       "w_ffn2_gate", "w_ffn2_up", "w_ffn2_down", "w_ple", "w_ple_gate"]
SMALL = ["ln_ffn1", "ln_mix", "b_forget", "b_merge", "ln_ffn2", "ln_ple", "ln_final"]
WEIGHTS = ["ln_ffn1", "w_ffn1_gate", "w_ffn1_up", "w_ffn1_down", "ln_mix", "w_in", "b_forget", "w_merge", "b_merge",
           "w_ret_out", "w_fox_out", "w_out", "ln_ffn2", "w_ffn2_gate", "w_ffn2_up", "w_ffn2_down", "ln_ple",
           "w_ple", "w_ple_gate", "ln_final"]


TRANSPOSED = {"w_ffn1_gate", "w_ffn1_up", "w_ffn2_gate", "w_ffn2_up", "w_in"}
IN_ROWS_PAD = -(-(IN_COLS // N_CHIPS) // 32) * 32


def _pack_small(vals, loss_row):
    rows = [loss_row]
    for name in SMALL:
        v = vals[name].reshape(-1)
        n = -(-v.shape[0] // 128) * 128
        rows.append(jnp.pad(v, (0, n - v.shape[0])).reshape(n // 128, 128))
    packed = jnp.concatenate(rows, axis=0)
    pad = -packed.shape[0] % 8
    return jnp.pad(packed, ((0, pad), (0, 0)))


def _unpack_small(packed, sizes):
    out, r = {}, 1
    for name in SMALL:
        n = sizes[name]
        nr = -(-n // 128)
        out[name] = packed[r:r + nr].reshape(1, nr * 128)[:, :n]
        r += nr
    return out


class _Stage:
    def __init__(self, comm, finish):
        self.comm, self.finish, self.result = comm, finish, None


def _hosted(fn, *a, stages=()):
    if not stages:
        return fn(*a)
    outs, couts = fn(*a, comm=_merge([st.comm for st in stages]))
    for st, o in zip(stages, _split_outs([st.comm for st in stages], couts)):
        st.result = st.finish(o)
    return outs


class _Reducer:
    def __init__(self):
        self.done = {}

    def swap(self, grads):
        names = list(grads)
        return _Stage(_c_half_swap([grads[n] for n in names]),
                      lambda outs: {n: _add_halves(grads[n], o) for n, o in zip(names, outs)})

    def exchange(self, parts):
        names = list(parts)
        return _Stage(_c_chip_exchange([parts[n] for n in names]),
                      lambda outs: {n: _sum_chips(parts[n], o) for n, o in zip(names, outs)})

    def join(self, halves):
        names = list(halves)
        return _Stage(_c_join([halves[n] for n in names]),
                      lambda outs: self.done.update({n: (halves[n], o) for n, o in zip(names, outs)}))


def kernel(x, p, positions, ln_ffn1, w_ffn1_gate, w_ffn1_up, w_ffn1_down, ln_mix, w_in, b_forget, w_merge, b_merge, w_ret_out, w_fox_out, w_out, ln_ffn2, w_ffn2_gate, w_ffn2_up, w_ffn2_down, ln_ple, w_ple, w_ple_gate, ln_final, loss_target, m_ln_ffn1, m_w_ffn1_gate, m_w_ffn1_up, m_w_ffn1_down, m_ln_mix, m_w_in, m_b_forget, m_w_merge, m_b_merge, m_w_ret_out, m_w_fox_out, m_w_out, m_ln_ffn2, m_w_ffn2_gate, m_w_ffn2_up, m_w_ffn2_down, m_ln_ple, m_w_ple, m_w_ple_gate, m_ln_final, v_ln_ffn1, v_w_ffn1_gate, v_w_ffn1_up, v_w_ffn1_down, v_ln_mix, v_w_in, v_b_forget, v_w_merge, v_b_merge, v_w_ret_out, v_w_fox_out, v_w_out, v_ln_ffn2, v_w_ffn2_gate, v_w_ffn2_up, v_w_ffn2_down, v_ln_ple, v_w_ple, v_w_ple_gate, v_ln_final):
    args = dict(locals())
    w = {n: args[n] for n in WEIGHTS}
    m = {n: args["m_" + n] for n in WEIGHTS}
    v = {n: args["v_" + n] for n in WEIGHTS}
    d = x.shape[-1]
    t_tok = x.shape[1]
    xs, ps, target = x[0], p[0, 0], loss_target[0]
    small = {n: w[n].reshape(1, -1) for n in SMALL}

    def to2d(n, a):
        if n in TRANSPOSED:
            return a[0].T
        return a.reshape(a.shape[-2], a.shape[-1]) if a.ndim == 3 else a.reshape(1, -1)

    def from2d(n, a):
        return a.T[None] if n in TRANSPOSED else a.reshape(w[n].shape)

    def padded(n, a):
        return jnp.pad(a, ((0, IN_ROWS_PAD - a.shape[0]), (0, 0))) if n == "w_in" else a

    core = lax.axis_index("c")
    me = 2 * lax.axis_index("x") + lax.axis_index("y")
    shard = {}
    for n in BIG:
        s2 = padded(n, to2d(n, w[n]).astype(BF))
        shard[n] = s2.reshape(1, 2, s2.shape[0] // 2, s2.shape[1])
    full = {}

    def gather(names):
        bufs = [lax.dynamic_update_slice(jnp.zeros((N_CHIPS,) + shard[n].shape[1:], BF), shard[n], (me, 0, 0, 0))
                for n in names]

        def finish(outs):
            full.update({n: o.reshape(N_CHIPS, 2 * o.shape[2], o.shape[3]) for n, o in zip(names, outs)})

        return _Stage(_c_all_gather(bufs), finish)

    half = RET_DIM // 2
    inv_freq = 1.0 / (ROPE_BASE ** (jnp.arange(half, dtype=F32) / half))
    cos_t, sin_t = _hosted(_rope_tables, positions[0].astype(F32).reshape(t_tok, 1),
                           jnp.repeat(inv_freq, 2).reshape(1, RET_DIM),
                           stages=[gather(["w_ffn1_gate", "w_ffn1_up", "w_ffn1_down"])])
    consts = _ret_consts()
    b_pad = jnp.pad(small["b_forget"], ((0, 0), (0, 128 - FOX_HEADS)))

    h1, n1, g1, u1 = _hosted(_ffn_fwd, xs, small["ln_ffn1"], full["w_ffn1_gate"], full["w_ffn1_up"],
                             full["w_ffn1_down"], stages=[gather(["w_in", "w_merge"])])
    w_in_full = jnp.pad(full["w_in"][:, :IN_COLS // N_CHIPS].reshape(IN_COLS, d), ((0, IN_PAD - IN_COLS), (0, 0)))
    u, rq, rk, rv, rg, fq, fk, fv, ffl, ga, gb = _hosted(
        _mix_in, h1, small["ln_mix"], w_in_full, full["w_merge"], small["b_merge"], cos_t, sin_t,
        stages=[gather(["w_ret_out", "w_fox_out", "w_out", "w_ple_gate", "w_ple"])])
    aq, ak = _forget_fwd(ffl, b_pad)
    y_raw, y_ret, states = _ret_fwd(rq, rk, rv, rg, consts)
    y_fox, y_fox32, lse_e = _hosted(_fox_fwd, fq, fk, fv, aq, ak,
                                    stages=[gather(["w_ffn2_gate", "w_ffn2_up", "w_ffn2_down"])])
    aqb = _fox_aug_lse(aq, lse_e)
    h2, za, zb, mix = _mix_out(h1, y_ret, y_fox, ga, gb, full["w_ret_out"], full["w_fox_out"], full["w_out"])
    h3, n2, g2, u2 = _ffn_fwd(h2, small["ln_ffn2"], full["w_ffn2_gate"], full["w_ffn2_up"], full["w_ffn2_down"])

    red = _Reducer()
    dh3, n3, dpgpre, dpe, pb, loss, dln_final, dln_ple = _tail(
        h3, ps, target, small["ln_ple"], small["ln_final"], full["w_ple_gate"], full["w_ple"])
    g_ple = dict(w_ple_gate=_wgrad_rows("wgrad_ple_gate", n3, dpgpre, N_CHIPS),
                 w_ple=_wgrad_cols("wgrad_ple", pb, dpe, N_CHIPS))

    sw_ple = red.swap(g_ple)
    dh2, dln_ffn2, dg2, du2, a2, dhb3 = _hosted(
        _ffn_bwd, dh3, h2, small["ln_ffn2"], g2, u2, full["w_ffn2_gate"], full["w_ffn2_up"], full["w_ffn2_down"],
        stages=[sw_ple])
    ex_ple = red.exchange(sw_ple.result)
    g_f2 = dict(w_ffn2_gate=_hosted(_wgrad_b_shared, "wgrad_ffn2_gate", dg2, n2, stages=[ex_ple]))
    g_f2["w_ffn2_up"] = _wgrad_b_shared("wgrad_ffn2_up", du2, n2)
    g_f2["w_ffn2_down"] = _wgrad_b_shared("wgrad_ffn2_down", a2, dhb3)

    sw_f2 = red.swap(g_f2)
    dhb2, dgpre, dza, dzb, dy_ret, dy_fox, ad, db_merge = _hosted(
        _mix_out_bwd, dh2, za, zb, ga, gb, y_fox32, full["w_ret_out"], full["w_fox_out"], full["w_out"],
        stages=[sw_f2, red.join(ex_ple.result)])
    g_br = dict(w_out=_wgrad_rows("wgrad_out", mix, dhb2, N_CHIPS),
                w_ret_out=_wgrad_cols("wgrad_ret_out", y_ret, dza, N_CHIPS),
                w_fox_out=_wgrad_cols("wgrad_fox_out", y_fox, dzb, N_CHIPS))

    sw_br = red.swap(g_br)
    drq, drk, drv, drg = _hosted(_ret_bwd, rq, rk, rv, rg, y_raw, dy_ret, states, consts, cos_t, sin_t,
                                 stages=[sw_br])
    ex_f2, ex_br = red.exchange(sw_f2.result), red.exchange(sw_br.result)
    dfq, dfk, dfv, dcum_t3, dcum_q = _hosted(_fox_bwd, fq, fk, fv, dy_fox, aqb, ak, ad, stages=[ex_f2, ex_br])
    dff, db_forget = _forget_bwd(dcum_t3.reshape(FOX_HEADS, t_tok), dcum_q, ffl, b_pad)
    dh1, dln_mix, dproj = _hosted(
        _mix_in_bwd, dh2, h1, small["ln_mix"], (drq, drk, drv, drg, dfq, dfk, dfv), dff, dgpre, w_in_full,
        full["w_merge"], stages=[red.join(ex_f2.result), red.join(ex_br.result)])

    dx, dln_ffn1, dg1, du1, a1, dhb1 = _ffn_bwd(
        dh1, xs, small["ln_ffn1"], g1, u1, full["w_ffn1_gate"], full["w_ffn1_up"], full["w_ffn1_down"])
    g_f1g = _wgrad_b_shared("wgrad_ffn1_gate", dg1, n1)
    sw_f1g = red.swap(dict(w_ffn1_gate=g_f1g))
    g_f1u = _hosted(_wgrad_b_shared, "wgrad_ffn1_up", du1, n1, stages=[sw_f1g])
    ex_f1g, sw_f1u = red.exchange(sw_f1g.result), red.swap(dict(w_ffn1_up=g_f1u))
    g_f1d = _hosted(_wgrad_b_shared, "wgrad_ffn1_down", a1, dhb1, stages=[ex_f1g, sw_f1u])

    ex_f1u, sw_f1d = red.exchange(sw_f1u.result), red.swap(dict(w_ffn1_down=g_f1d))
    g_in = _hosted(_wgrad_rows, "wgrad_in", dproj, u, IN_PAD // 512,
                   stages=[ex_f1u, sw_f1d, red.join(ex_f1g.result)])
    g_in = g_in.reshape(IN_PAD, d)[:IN_COLS].reshape(N_CHIPS, IN_COLS // N_CHIPS, d)
    g_in = jnp.pad(g_in, ((0, 0), (0, IN_ROWS_PAD - IN_COLS // N_CHIPS), (0, 0)))
    ex_f1d, sw_in = red.exchange(sw_f1d.result), red.swap(dict(w_in=g_in))
    g_mrg = _hosted(_wgrad_cols, "wgrad_merge", u, dgpre, N_CHIPS,
                    stages=[ex_f1d, sw_in, red.join(ex_f1u.result)])

    small_grads = dict(ln_ffn1=dln_ffn1, ln_mix=dln_mix, b_forget=db_forget[:, :FOX_HEADS], b_merge=db_merge,
                       ln_ffn2=dln_ffn2, ln_ple=dln_ple, ln_final=dln_final)
    sizes = {n: w[n].size for n in SMALL}
    ex_in, sw_mrg = red.exchange(sw_in.result), red.swap(dict(w_merge=g_mrg))
    reduced = _hosted(_all_reduce_small, _pack_small(small_grads, loss),
                      stages=[ex_in, sw_mrg, red.join(ex_f1d.result)])
    gsum = _unpack_small(reduced, sizes)
    loss = reduced[0, 0]
    ex_mrg = red.exchange(sw_mrg.result)
    _hosted(_exchange_only, stages=[ex_mrg, red.join(ex_in.result)])
    _hosted(_exchange_only, stages=[red.join(ex_mrg.result)])

    results = {}

    def update(names, stages=()):
        w2, m2, v2 = ([to2d(n, a[n]) for n in names] for a in (w, m, v))
        n = names[0]
        if n in gsum or n == "w_in":
            if n in gsum:
                g2 = gsum[n]
            else:
                mine, other = red.done[n]
                g2 = jnp.where(core == 0, jnp.concatenate([mine, other]), jnp.concatenate([other, mine]))
                g2 = g2[:w2[0].shape[0]]
            res = [g2] + _hosted(_adamw, w2[0], g2, m2[0], v2[0], stages=stages)
        else:
            res = _hosted(_adamw_halves, [(w2[q], *red.done[names[q]], m2[q], v2[q]) for q in range(len(names))],
                          stages=stages)
        for q, name in enumerate(names):
            results[name] = tuple(from2d(name, a) for a in res[4 * q:4 * q + 4])

    update(["w_ffn2_gate", "w_ffn2_up", "w_ffn2_down"])
    update(["w_ffn1_gate", "w_ffn1_up", "w_ffn1_down"])
    update(["w_out", "w_ple_gate"])
    update(["w_ret_out", "w_fox_out"])
    for n in WEIGHTS:
        if n not in results:
            update([n])

    outs = [[results[n][k] for n in WEIGHTS] for k in range(4)]
    return (loss, dx[None], *outs[0], *outs[1], *outs[2], *outs[3])
```

```python
import functools
import operator

import jax
import jax.numpy as jnp
from jax import lax
from jax.experimental import pallas as pl
from jax.experimental.pallas import tpu as pltpu

F32 = jnp.float32
BF = jnp.bfloat16
MESH = pl.DeviceIdType.MESH

EPS = 1e-6
ROPE_BASE = 10000.0
N_CHIPS = 4
RET_HEADS = 4
RET_DIM = 128
RET_WIDTH = RET_HEADS * RET_DIM
RET_CHUNK = 128
RET_SCALE = RET_DIM ** -0.5
FOX_HEADS = 8
FOX_DIM = 64
FOX_WIDTH = FOX_HEADS * FOX_DIM
FOX_SCALE = FOX_DIM ** -0.5
IN_COLS = 4 * RET_WIDTH + 3 * FOX_WIDTH + FOX_HEADS
IN_PAD = 4096
FF_COL = 4 * RET_WIDTH + 3 * FOX_WIDTH
NEG = -1e30

ADAM_LR = 0.001
ADAM_B1 = 0.9
ADAM_B2 = 0.999
ADAM_EPS = 1e-08
ADAM_WD = 0.01
ADAM_STEP = 10

VMEM_LIMIT = 52 * 1024 * 1024

NT = (((1,), (1,)), ((), ()))
TN = (((0,), (0,)), ((), ()))

HBM_SPEC = pl.BlockSpec(memory_space=pltpu.HBM)
VMEM_SPEC = pl.BlockSpec(memory_space=pltpu.VMEM)


def _dot(a, b):
    return jnp.dot(a, b, preferred_element_type=F32)


def _dot_nt(a, b):
    return lax.dot_general(a, b, NT, preferred_element_type=F32)


def _dot_tn(a, b):
    return lax.dot_general(a, b, TN, preferred_element_type=F32)


def _rstd(xv):
    return lax.rsqrt(jnp.mean(xv * xv, axis=-1, keepdims=True) + EPS)


def _rms_bwd(dn, xv, r, ln):
    xh = xv * r
    dxh = dn * ln
    dx = r * (dxh - xh * jnp.mean(dxh * xh, axis=-1, keepdims=True))
    return dx, jnp.sum(dn * xh, axis=0, keepdims=True)


def _sigmoid(x):
    return jax.nn.sigmoid(x)


def _tile(n, pref):
    return pref if n % pref == 0 else n


def _row_tile(n, cap):
    best = [t for t in range(16, min(n, cap) + 1, 16) if n % t == 0]
    return best[-1] if best else n


class _Comm:
    def __init__(self, ins, out_shapes, sems, start, wait, aliases=None):
        self.ins, self.out_shapes, self.sems, self.start, self.wait = list(ins), list(out_shapes), list(sems), start, wait
        self.aliases = dict(aliases or {})


def _merge(comms):
    comms = [c for c in comms if c is not None]
    if not comms:
        return None
    bounds, ni, no, ns = [], 0, 0, 0
    for c in comms:
        bounds.append((ni, no, ns))
        ni, no, ns = ni + len(c.ins), no + len(c.out_shapes), ns + len(c.sems)

    def run(which):
        def f(ins, outs, sems):
            for c, (i, o, s) in zip(comms, bounds):
                getattr(c, which)(ins[i:i + len(c.ins)], outs[o:o + len(c.out_shapes)], sems[s:s + len(c.sems)])
        return f

    aliases = {i + a: o + b for c, (i, o, _) in zip(comms, bounds) for a, b in c.aliases.items()}
    return _Comm([a for c in comms for a in c.ins], [a for c in comms for a in c.out_shapes],
                 [a for c in comms for a in c.sems], run("start"), run("wait"), aliases)


def _split_outs(comms, outs):
    res, o = [], 0
    for c in comms:
        if c is not None:
            res.append(list(outs[o:o + len(c.out_shapes)]))
            o += len(c.out_shapes)
    return res


def _pcall(body, args, *, name, out_shape, grid=(), in_specs=None, out_specs=None, scratch=(), comm=None,
           prefetch=()):
    many = isinstance(out_shape, (list, tuple))
    outs = list(out_shape) if many else [out_shape]
    n_pre, n_in, n_out, n_scr = len(prefetch), len(args), len(outs), len(scratch)
    if in_specs is None:
        in_specs, out_specs = [VMEM_SPEC] * n_in, [VMEM_SPEC] * n_out
    else:
        in_specs, out_specs = list(in_specs), (list(out_specs) if many else [out_specs])
    params = pltpu.CompilerParams(dimension_semantics=("arbitrary",) * len(grid), vmem_limit_bytes=VMEM_LIMIT)
    scalars = [jnp.reshape(s, (1,)).astype(jnp.int32) for s in prefetch]
    ci, co = (len(comm.ins), len(comm.out_shapes)) if comm is not None else (0, 0)

    def wrapped(*refs):
        pre, refs = refs[:n_pre], refs[n_pre:]
        a, ca = refs[:n_in], refs[n_in:n_in + ci]
        o = refs[n_in + ci:n_in + ci + n_out]
        cout = refs[n_in + ci + n_out:n_in + ci + n_out + co]
        s = refs[n_in + ci + n_out + co:n_in + ci + n_out + co + n_scr]
        csem = refs[n_in + ci + n_out + co + n_scr:]
        if comm is None:
            body(*pre, *a, *o, *s)
        elif grid:
            first = functools.reduce(operator.and_, [pl.program_id(k) == 0 for k in range(len(grid))])
            last = functools.reduce(operator.and_, [pl.program_id(k) == grid[k] - 1 for k in range(len(grid))])
            pl.when(first)(lambda: comm.start(ca, cout, csem))
            body(*pre, *a, *o, *s)
            pl.when(last)(lambda: comm.wait(ca, cout, csem))
        else:
            comm.start(ca, cout, csem)
            body(*pre, *a, *o, *s)
            comm.wait(ca, cout, csem)

    c_ins, c_outs, c_sems, aliases = ([], [], [], {}) if comm is None else (
        comm.ins, comm.out_shapes, comm.sems, {n_pre + n_in + i: n_out + o for i, o in comm.aliases.items()})
    all_in, all_out = in_specs + [HBM_SPEC] * ci, out_specs + [HBM_SPEC] * co
    all_scr = list(scratch) + c_sems
    if grid:
        args = [pltpu.with_memory_space_constraint(a, pltpu.HBM) for a in args]
    c_ins = [pltpu.with_memory_space_constraint(a, pltpu.HBM) for a in c_ins]
    if n_pre:
        spec = dict(grid_spec=pltpu.PrefetchScalarGridSpec(
            num_scalar_prefetch=n_pre, grid=grid, in_specs=all_in, out_specs=all_out, scratch_shapes=all_scr))
    else:
        spec = dict(grid=grid, in_specs=all_in, out_specs=all_out, scratch_shapes=all_scr)
    res = pl.pallas_call(wrapped, name=name, out_shape=outs + c_outs, input_output_aliases=aliases,
                         compiler_params=params, **spec)(*scalars, *args, *c_ins)
    mine = list(res[:n_out])
    mine = mine if many else mine[0]
    return mine if comm is None else (mine, list(res[n_out:]))


def _peer_chips(x, y):
    return [(1 - x, y), (x, 1 - y), (1 - x, 1 - y)]


def _c_all_gather(bufs):
    n = len(bufs)

    def copies(ins, outs, sems):
        send_sems, recv_sems, fwd_send, fwd_recv = sems
        x, y, c = lax.axis_index("x"), lax.axis_index("y"), lax.axis_index("c")
        me = 2 * x + y
        peers = _peer_chips(x, y)
        chip = [2 * px + py for px, py in peers]

        def ici(g, j, slot):
            return pltpu.make_async_remote_copy(
                src_ref=outs[g].at[me, c], dst_ref=outs[g].at[slot, c], send_sem=send_sems.at[g, j],
                recv_sem=recv_sems.at[g, j], device_id=(*peers[j], c), device_id_type=MESH)

        def d2d(g, j, half):
            return pltpu.make_async_remote_copy(
                src_ref=outs[g].at[chip[j], half], dst_ref=outs[g].at[chip[j], half], send_sem=fwd_send.at[g, j],
                recv_sem=fwd_recv.at[g, j], device_id=(x, y, 1 - c), device_id_type=MESH)

        pairs = [(g, j) for g in range(n) for j in range(3)]
        sends = [ici(g, j, me) for g, j in pairs]
        recvs = [ici(g, j, chip[j]) for g, j in pairs]
        passes = [d2d(g, j, c) for g, j in pairs]
        passed = [d2d(g, j, 1 - c) for g, j in pairs]
        return sends, recvs, passes, passed

    def start(ins, outs, sems):
        for cp in copies(ins, outs, sems)[0]:
            cp.start()

    def wait(ins, outs, sems):
        sends, recvs, passes, passed = copies(ins, outs, sems)
        for rcv, fwd in zip(recvs, passes):
            rcv.wait_recv()
            fwd.start()
        for cp in passed:
            cp.wait_recv()
        for cp in sends + passes:
            cp.wait_send()

    pair_sems = pltpu.SemaphoreType.DMA((n, 3))
    return _Comm(bufs, [jax.ShapeDtypeStruct(s.shape, s.dtype) for s in bufs], [pair_sems] * 4, start, wait,
                 aliases={g: g for g in range(n)})


def _start_wait(copies):
    def start(ins, outs, sems):
        local, sends, _ = copies(ins, outs, sems)
        for cp in local + sends:
            cp.start()

    def wait(ins, outs, sems):
        local, sends, recvs = copies(ins, outs, sems)
        for cp in recvs:
            cp.wait_recv()
        for cp in sends:
            cp.wait_send()
        for cp in local:
            cp.wait()

    return start, wait


def _c_half_swap(grads):
    n = len(grads)

    def copies(ins, outs, sems):
        send_sems, recv_sems = sems
        x, y, c = lax.axis_index("x"), lax.axis_index("y"), lax.axis_index("c")
        sends = []
        for g in range(n):
            half = ins[g].shape[1] // 2
            sends.append(pltpu.make_async_remote_copy(
                src_ref=ins[g].at[:, pl.ds((1 - c) * half, half), :], dst_ref=outs[g],
                send_sem=send_sems.at[g], recv_sem=recv_sems.at[g], device_id=(x, y, 1 - c), device_id_type=MESH))
        return [], sends, sends

    return _Comm(
        grads, [jax.ShapeDtypeStruct((N_CHIPS, s.shape[1] // 2, s.shape[2]), s.dtype) for s in grads],
        [pltpu.SemaphoreType.DMA((n,)), pltpu.SemaphoreType.DMA((n,))], *_start_wait(copies))


def _c_chip_exchange(parts):
    n = len(parts)

    def copies(ins, outs, sems):
        send_sems, recv_sems = sems
        x, y, c = lax.axis_index("x"), lax.axis_index("y"), lax.axis_index("c")
        peers = _peer_chips(x, y)

        def remote(g, j):
            return pltpu.make_async_remote_copy(
                src_ref=ins[g].at[2 * peers[j][0] + peers[j][1]], dst_ref=outs[g].at[j],
                send_sem=send_sems.at[g, j], recv_sem=recv_sems.at[g, j], device_id=(*peers[j], c),
                device_id_type=MESH)

        sends = [remote(g, j) for g in range(n) for j in range(3)]
        return [], sends, sends

    return _Comm(
        parts, [jax.ShapeDtypeStruct((3,) + s.shape[1:], s.dtype) for s in parts],
        [pltpu.SemaphoreType.DMA((n, 3)), pltpu.SemaphoreType.DMA((n, 3))], *_start_wait(copies))


def _c_join(halves):
    n = len(halves)

    def copies(ins, outs, sems):
        send_sems, recv_sems = sems
        x, y, c = lax.axis_index("x"), lax.axis_index("y"), lax.axis_index("c")
        sends = [pltpu.make_async_remote_copy(
            src_ref=ins[g], dst_ref=outs[g], send_sem=send_sems.at[g], recv_sem=recv_sems.at[g],
            device_id=(x, y, 1 - c), device_id_type=MESH) for g in range(n)]
        return [], sends, sends

    return _Comm(
        halves, [jax.ShapeDtypeStruct(s.shape, s.dtype) for s in halves],
        [pltpu.SemaphoreType.DMA((n,)), pltpu.SemaphoreType.DMA((n,))], *_start_wait(copies))


def _exchange_only(comm=None):
    def body(o_ref):
        o_ref[...] = jnp.zeros_like(o_ref)

    return _pcall(body, [], name="exchange_only", out_shape=jax.ShapeDtypeStruct((8, 128), F32), comm=comm)


def _all_reduce_small(v, comm=None):
    rows = v.shape[0]

    def body(v_ref, out_ref, buf, send_sems, recv_sems):
        x, y, c = lax.axis_index("x"), lax.axis_index("y"), lax.axis_index("c")
        me = 4 * x + 2 * y + c
        buf[me] = v_ref[...]
        flips = [(fx, fy, fc) for fx in (0, 1) for fy in (0, 1) for fc in (0, 1)][1:]

        def peer(k):
            fx, fy, fc = flips[k]
            px, py, pc = x ^ fx, y ^ fy, c ^ fc
            return (px, py, pc), 4 * px + 2 * py + pc

        def copy(k, slot):
            return pltpu.make_async_remote_copy(
                src_ref=buf.at[slot], dst_ref=buf.at[slot], send_sem=send_sems.at[k],
                recv_sem=recv_sems.at[k], device_id=peer(k)[0], device_id_type=MESH)

        sends = [copy(k, me) for k in range(7)]
        for cp in sends:
            cp.start()
        for k in range(7):
            copy(k, peer(k)[1]).wait_recv()
        for cp in sends:
            cp.wait_send()
        acc = buf[0]
        for d in range(1, 8):
            acc = acc + buf[d]
        out_ref[...] = acc

    return _pcall(body, [v], name="all_reduce_small", out_shape=jax.ShapeDtypeStruct((rows, 128), F32),
                  scratch=[pltpu.VMEM((8, rows, 128), F32), pltpu.SemaphoreType.DMA((7,)),
                           pltpu.SemaphoreType.DMA((7,))], comm=comm)


def _add_halves(g, got):
    _, h, c = got.shape
    th = _row_tile(h, 512)
    nh = h // th
    half = lax.axis_index("c") * nh

    def body(h_ref, a_ref, b_ref, o_ref):
        o_ref[...] = (a_ref[...].astype(F32) + b_ref[...].astype(F32)).astype(o_ref.dtype)

    spec = pl.BlockSpec((1, th, c), lambda j, i, h_ref: (j, i, 0))
    mine = pl.BlockSpec((1, th, c), lambda j, i, h_ref: (j, h_ref[0] + i, 0))
    return _pcall(body, [g, got], name="add_halves", grid=(N_CHIPS, nh), prefetch=[half],
                  out_shape=jax.ShapeDtypeStruct(got.shape, BF), in_specs=[mine, spec], out_specs=spec)


def _sum_chips(parts, recv):
    _, h, c = parts.shape
    th = _row_tile(h, 512)
    me = 2 * lax.axis_index("x") + lax.axis_index("y")

    def body(me_ref, p_ref, r_ref, o_ref):
        acc = p_ref[0].astype(F32)
        for s in range(N_CHIPS - 1):
            acc = acc + r_ref[s].astype(F32)
        o_ref[...] = acc

    return _pcall(body, [parts, recv], name="sum_chips", grid=(h // th,), prefetch=[me],
                  out_shape=jax.ShapeDtypeStruct((h, c), F32),
                  in_specs=[pl.BlockSpec((1, th, c), lambda i, me_ref: (me_ref[0], i, 0)),
                            pl.BlockSpec((N_CHIPS - 1, th, c), lambda i, me_ref: (0, i, 0))],
                  out_specs=pl.BlockSpec((th, c), lambda i, me_ref: (i, 0)))


def _adam_update(w, gv, m, v, d_ref, nm_ref, nv_ref):
    c1 = 1.0 / (1.0 - ADAM_B1 ** ADAM_STEP)
    c2 = 1.0 / (1.0 - ADAM_B2 ** ADAM_STEP)
    nm = ADAM_B1 * m + (1.0 - ADAM_B1) * gv
    nv = ADAM_B2 * v + (1.0 - ADAM_B2) * (gv * gv)
    nm_ref[...] = nm
    nv_ref[...] = nv
    d_ref[...] = -ADAM_LR * ((nm * c1) / (jnp.sqrt(nv * c2) + ADAM_EPS) + ADAM_WD * w)


def _adamw(w, g, m, v, comm=None):
    r, c = w.shape
    tr = _row_tile(r, 512)

    def body(w_ref, g_ref, m_ref, v_ref, d_ref, nm_ref, nv_ref):
        _adam_update(w_ref[...], g_ref[...], m_ref[...], v_ref[...], d_ref, nm_ref, nv_ref)

    spec = pl.BlockSpec((tr, c), lambda i: (i, 0))
    sds = jax.ShapeDtypeStruct((r, c), F32)
    return _pcall(body, [w, g, m, v], name="adamw", grid=(r // tr,), out_shape=[sds, sds, sds],
                  in_specs=[spec] * 4, out_specs=[spec] * 3, comm=comm)


def _adamw_halves(items, comm=None):
    k = len(items)
    r, c = items[0][0].shape
    h = r // 2
    tr = _row_tile(h, min(512, (VMEM_LIMIT * 3 // 4) // (k * 9 * 2 * 4 * c)))
    nb = h // tr
    core = lax.axis_index("c")

    def body(c_ref, *refs):
        ins, outs = refs[:5 * k], refs[5 * k:]
        for q in range(k):
            w_ref, gm_ref, go_ref, m_ref, v_ref = ins[5 * q:5 * q + 5]
            g_ref, d_ref, nm_ref, nv_ref = outs[4 * q:4 * q + 4]
            gv = jnp.where(pl.program_id(0) == c_ref[0], gm_ref[...], go_ref[...])
            g_ref[...] = gv
            _adam_update(w_ref[...], gv, m_ref[...], v_ref[...], d_ref, nm_ref, nv_ref)

    full = pl.BlockSpec((tr, c), lambda hh, i, c_ref: (hh * nb + i, 0))
    half = pl.BlockSpec((tr, c), lambda hh, i, c_ref: (i, 0))
    sds = jax.ShapeDtypeStruct((r, c), F32)
    return _pcall(body, [a for it in items for a in it], name="adamw_halves", grid=(2, nb), prefetch=[core],
                  out_shape=[sds] * (4 * k), in_specs=[full, half, half, full, full] * k, out_specs=[full] * (4 * k),
                  comm=comm)


def _wgrad(name, a, b, a_spec, b_spec, m, n, nb, comm):
    def body(a_ref, b_ref, o_ref):
        o_ref[...] = _dot_tn(a_ref[...], b_ref[...]).astype(o_ref.dtype)

    return _pcall(body, [a, b], name=name, grid=(nb,), out_shape=jax.ShapeDtypeStruct((nb, m, n), BF),
                  in_specs=[a_spec, b_spec], out_specs=pl.BlockSpec((None, m, n), lambda j: (j, 0, 0)), comm=comm)


def _wgrad_cols(name, a, b, nb, comm=None):
    t_tok, m = a.shape
    n = b.shape[1] // nb
    return _wgrad(name, a, b, pl.BlockSpec((t_tok, m), lambda j: (0, 0)), pl.BlockSpec((t_tok, n), lambda j: (0, j)),
                  m, n, nb, comm)


def _wgrad_rows(name, a, b, nb, comm=None):
    t_tok, n = b.shape
    m = a.shape[1] // nb
    return _wgrad(name, a, b, pl.BlockSpec((t_tok, m), lambda j: (0, j)), pl.BlockSpec((t_tok, n), lambda j: (0, 0)),
                  m, n, nb, comm)


def _wgrad_a_shared(name, a, b4, comm=None):
    t_tok, m = a.shape
    nb, _, n = b4.shape
    return _wgrad(name, a, b4, pl.BlockSpec((t_tok, m), lambda j: (0, 0)),
                  pl.BlockSpec((None, t_tok, n), lambda j: (j, 0, 0)), m, n, nb, comm)


def _wgrad_b_shared(name, a4, b, comm=None):
    nb, t_tok, m = a4.shape
    n = b.shape[1]
    return _wgrad(name, a4, b, pl.BlockSpec((None, t_tok, m), lambda j: (j, 0, 0)),
                  pl.BlockSpec((t_tok, n), lambda j: (0, 0)), m, n, nb, comm)


def _w4_spec(r, c):
    return pl.BlockSpec((None, r, c), lambda i, j: (j, 0, 0))


FFN_ROW_CHUNK = 256


def _row_chunks(tm):
    rc = FFN_ROW_CHUNK if tm % FFN_ROW_CHUNK == 0 else tm
    return [slice(r, r + rc) for r in range(0, tm, rc)]


def _ffn_fwd(h, ln, wg4, wu4, wd4, comm=None):
    t_tok, d = h.shape
    f = wg4.shape[-2]
    tm = _tile(t_tok, 512)

    def body(h_ref, ln_ref, wg_ref, wu_ref, wd_ref, ho_ref, n_ref, g_ref, u_ref, n_s, acc):
        j = pl.program_id(1)

        @pl.when(j == 0)
        def _():
            xv = h_ref[...]
            nv = (xv * _rstd(xv) * ln_ref[...]).astype(BF)
            n_s[...] = nv
            n_ref[...] = nv
            acc[...] = jnp.zeros_like(acc)

        nv = n_s[...]
        g = _dot_nt(nv, wg_ref[...])
        u = _dot_nt(nv, wu_ref[...])
        g_ref[...] = g.astype(BF)
        u_ref[...] = u.astype(BF)
        a = (g * _sigmoid(g) * u).astype(BF)
        acc[...] += _dot(a, wd_ref[...])

        @pl.when(j == N_CHIPS - 1)
        def _():
            ho_ref[...] = h_ref[...] + 0.5 * acc[...]

    row = pl.BlockSpec((tm, d), lambda i, j: (i, 0))
    gu = pl.BlockSpec((None, tm, f), lambda i, j: (j, i, 0))
    gu_sds = jax.ShapeDtypeStruct((N_CHIPS, t_tok, f), BF)
    return _pcall(
        body, [h, ln, wg4, wu4, wd4], name="ffn_fwd", grid=(t_tok // tm, N_CHIPS),
        out_shape=[jax.ShapeDtypeStruct((t_tok, d), F32), jax.ShapeDtypeStruct((t_tok, d), BF), gu_sds, gu_sds],
        in_specs=[row, pl.BlockSpec((1, d), lambda i, j: (0, 0)), _w4_spec(f, d), _w4_spec(f, d), _w4_spec(f, d)],
        out_specs=[row, row, gu, gu],
        scratch=[pltpu.VMEM((tm, d), BF), pltpu.VMEM((tm, d), F32)], comm=comm)


def _ffn_bwd(dho, h, ln, g4, u4, wg4, wu4, wd4, comm=None):
    t_tok, d = h.shape
    f = wg4.shape[-2]
    tm = _tile(t_tok, 512)

    def body(dho_ref, h_ref, ln_ref, g_ref, u_ref, wg_ref, wu_ref, wd_ref,
             dhi_ref, dln_ref, dg_ref, du_ref, a_ref, dhb_ref, dhb_s, dn_acc):
        i, j = pl.program_id(0), pl.program_id(1)

        @pl.when(j == 0)
        def _():
            dhb = (0.5 * dho_ref[...]).astype(BF)
            dhb_s[...] = dhb
            dhb_ref[...] = dhb
            dn_acc[...] = jnp.zeros_like(dn_acc)

        @pl.when((i == 0) & (j == 0))
        def _():
            dln_ref[...] = jnp.zeros_like(dln_ref)

        for rows in _row_chunks(tm):
            g = g_ref[rows, :].astype(F32)
            u = u_ref[rows, :].astype(F32)
            s = _sigmoid(g)
            sg = g * s
            a_ref[rows, :] = (sg * u).astype(BF)
            da = _dot_nt(dhb_s[rows, :], wd_ref[...])
            dg = (da * u * (s * (1.0 + g * (1.0 - s)))).astype(BF)
            du = (da * sg).astype(BF)
            dg_ref[rows, :] = dg
            du_ref[rows, :] = du
            dn_acc[rows, :] += _dot(dg, wg_ref[...]) + _dot(du, wu_ref[...])

        @pl.when(j == N_CHIPS - 1)
        def _():
            xv = h_ref[...]
            dx, dln = _rms_bwd(dn_acc[...], xv, _rstd(xv), ln_ref[...])
            dln_ref[...] += dln
            dhi_ref[...] = dho_ref[...] + dx

    row = pl.BlockSpec((tm, d), lambda i, j: (i, 0))
    vec = pl.BlockSpec((1, d), lambda i, j: (0, 0))
    gu = pl.BlockSpec((None, tm, f), lambda i, j: (j, i, 0))
    gu_sds = jax.ShapeDtypeStruct((N_CHIPS, t_tok, f), BF)
    return _pcall(
        body, [dho, h, ln, g4, u4, wg4, wu4, wd4], name="ffn_bwd", grid=(t_tok // tm, N_CHIPS),
        out_shape=[jax.ShapeDtypeStruct((t_tok, d), F32), jax.ShapeDtypeStruct((1, d), F32),
                   gu_sds, gu_sds, gu_sds, jax.ShapeDtypeStruct((t_tok, d), BF)],
        in_specs=[row, row, vec, gu, gu, _w4_spec(f, d), _w4_spec(f, d), _w4_spec(f, d)],
        out_specs=[row, vec, gu, gu, gu, row],
        scratch=[pltpu.VMEM((tm, d), BF), pltpu.VMEM((tm, d), F32)], comm=comm)


def _rope_tables(pos_col, inv_freq2, comm=None):
    t_tok = pos_col.shape[0]

    def body(p_ref, f_ref, cos_ref, sin_ref):
        ang = p_ref[...] * f_ref[...]
        lane = lax.broadcasted_iota(jnp.int32, ang.shape, 1)
        s = jnp.sin(ang)
        cos_ref[...] = jnp.cos(ang)
        sin_ref[...] = jnp.where((lane & 1) == 0, -s, s)

    sds = jax.ShapeDtypeStruct((t_tok, 128), F32)
    return _pcall(body, [pos_col, inv_freq2], name="rope_tables", out_shape=[sds, sds], comm=comm)


def _swap_pairs(x):
    lane = lax.broadcasted_iota(jnp.int32, x.shape, 1)
    return jnp.where((lane & 1) == 0, pltpu.roll(x, 127, 1), pltpu.roll(x, 1, 1))


def _mix_in(h, ln, w_in, wm4, b_m, cos_t, sin_t, comm=None):
    t_tok, d = h.shape
    cm = wm4.shape[-1]
    tm = _tile(t_tok, 256)

    def body(h_ref, ln_ref, win_ref, wm_ref, bm_ref, cos_ref, sin_ref,
             u_ref, rq_ref, rk_ref, rv_ref, rg_ref, fq_ref, fk_ref, fv_ref, ff_ref, ga_ref, gb_ref):
        xv = h_ref[...]
        ub = (xv * _rstd(xv) * ln_ref[...]).astype(BF)
        u_ref[...] = ub
        cosv, sinv = cos_ref[...], sin_ref[...]

        def sec(k):
            return _dot_nt(ub, win_ref[k * 512:(k + 1) * 512, :])

        def rot(xh):
            return xh * cosv + _swap_pairs(xh) * sinv

        pq, pk = sec(0), sec(1)
        for hh in range(RET_HEADS):
            sl = slice(hh * RET_DIM, (hh + 1) * RET_DIM)
            rq_ref[:, sl] = rot(pq[:, sl]).astype(BF)
            rk_ref[:, sl] = (rot(pk[:, sl]) * RET_SCALE).astype(BF)
        rv_ref[...] = sec(2).astype(BF)
        rg_ref[...] = sec(3).astype(BF)
        fq_ref[...] = (sec(4) * FOX_SCALE).astype(BF)
        fk_ref[...] = sec(5).astype(BF)
        fv_ref[...] = sec(6).astype(BF)
        ff_ref[...] = _dot_nt(ub, win_ref[FF_COL:FF_COL + 128, :])
        for j in range(N_CHIPS):
            gs = _sigmoid(_dot(ub, wm_ref[j]) + bm_ref[:, j * cm:(j + 1) * cm]).astype(BF)
            col = j * cm
            if col < d:
                ga_ref[:, col:col + cm] = gs
            else:
                gb_ref[:, col - d:col - d + cm] = gs

    row = lambda c: pl.BlockSpec((tm, c), lambda i: (i, 0))
    full = lambda *s: pl.BlockSpec(s, lambda i: (0,) * len(s))
    sds = lambda c, dt: jax.ShapeDtypeStruct((t_tok, c), dt)
    return _pcall(
        body, [h, ln, w_in, wm4, b_m, cos_t, sin_t], name="mix_in", grid=(t_tok // tm,),
        out_shape=[sds(d, BF)] + [sds(512, BF)] * 7 + [sds(128, F32), sds(d, BF), sds(d, BF)],
        in_specs=[row(d), full(1, d), full(IN_PAD, d), full(N_CHIPS, d, cm), full(1, 2 * d), row(128), row(128)],
        out_specs=[row(d)] + [row(512)] * 7 + [row(128), row(d), row(d)], comm=comm)


def _split3(x):
    hi = x.astype(BF)
    r1 = x - hi.astype(F32)
    mid = r1.astype(BF)
    lo = (r1 - mid.astype(F32)).astype(BF)
    return hi, mid, lo


def _aug_lane():
    return lax.broadcasted_iota(jnp.int32, (1, 128), 1) & (FOX_DIM - 1)


def _aug_put(base, k0, parts):
    w = _aug_lane()
    for i, part in enumerate(parts):
        base = jnp.where(w == k0 + i, part, base)
    return base


def _forget_fwd(ffl, b_pad):
    t_tok = ffl.shape[0]
    tb = _tile(t_tok, 256)

    def body(ff_ref, b_ref, aq_ref, ak_ref, cum_s):
        r = lax.broadcasted_iota(jnp.int32, (tb, tb), 0)
        c = lax.broadcasted_iota(jnp.int32, (tb, tb), 1)
        tri = jnp.where(c <= r, 1.0, 0.0).astype(BF)
        carry = jnp.zeros((1, 128), F32)
        for i in range(t_tok // tb):
            z = ff_ref[i * tb:(i + 1) * tb, :] + b_ref[...]
            lf = jnp.minimum(z, 0.0) - jnp.log(1.0 + jnp.exp(-jnp.abs(z)))
            hi, mid, lo = _split3(lf)
            cs = _dot(tri, hi) + _dot(tri, mid) + _dot(tri, lo) + carry
            cum_s[i * tb:(i + 1) * tb, :] = cs
            carry = cs[tb - 1:tb, :]
        x = cum_s[...]
        first = lax.broadcasted_iota(jnp.int32, (1, 128), 1) < FOX_DIM
        w = _aug_lane()
        one = jnp.ones((t_tok, 128), BF)
        zero = jnp.zeros((t_tok, 128), BF)
        for pp in range(FOX_HEADS // 2):
            other = jnp.where(first, x[:, 2 * pp + 1:2 * pp + 2], x[:, 2 * pp:2 * pp + 1])
            parts = _split3(other)
            aq = jnp.where((w >= 3) & (w < 6), one, zero)
            ak = jnp.where((w < 3) | ((w >= 6) & (w < 9)), one, zero)
            aq_ref[:, pp * 128:(pp + 1) * 128] = _aug_put(aq, 0, parts)
            ak_ref[:, pp * 128:(pp + 1) * 128] = _aug_put(ak, 3, [-q for q in parts])

    sds = jax.ShapeDtypeStruct((t_tok, FOX_WIDTH), BF)
    return _pcall(body, [ffl, b_pad], name="forget_fwd", out_shape=[sds, sds],
                  scratch=[pltpu.VMEM((t_tok, 128), F32)])


def _fox_aug_lse(aq, lse_e):
    t_tok = aq.shape[0]
    tm = _tile(t_tok, 512)

    def body(aq_ref, lse_ref, o_ref):
        for pp in range(FOX_HEADS // 2):
            sl = slice(pp * 128, (pp + 1) * 128)
            other = pltpu.roll(lse_ref[:, sl], FOX_DIM, 1)
            o_ref[:, sl] = _aug_put(aq_ref[:, sl], 6, _split3(-other))

    spec = pl.BlockSpec((tm, FOX_WIDTH), lambda i: (i, 0))
    return _pcall(body, [aq, lse_e], name="fox_aug_lse", grid=(t_tok // tm,),
                  out_shape=jax.ShapeDtypeStruct((t_tok, FOX_WIDTH), BF), in_specs=[spec, spec], out_specs=spec)


def _forget_bwd(dcum_t, dcum_q, ffl, b_pad):
    t_tok = ffl.shape[0]
    tb = _tile(t_tok, 256)

    def body(dc_ref, dq_ref, ff_ref, b_ref, dff_ref, db_ref, pad_s, d_s):
        pad_s[...] = jnp.zeros_like(pad_s)
        pad_s[0:FOX_HEADS, :] = dc_ref[...]
        dsum = pad_s[...].T
        lane = lax.broadcasted_iota(jnp.int32, (t_tok, 128), 1)
        for hh in range(FOX_HEADS):
            dsum = dsum + jnp.where(lane == hh, dq_ref[:, hh * FOX_DIM:hh * FOX_DIM + 1], 0.0)
        d_s[...] = dsum
        r = lax.broadcasted_iota(jnp.int32, (tb, tb), 0)
        c = lax.broadcasted_iota(jnp.int32, (tb, tb), 1)
        tri = jnp.where(c >= r, 1.0, 0.0).astype(BF)
        carry = jnp.zeros((1, 128), F32)
        db = jnp.zeros((1, 128), F32)
        for i in reversed(range(t_tok // tb)):
            hi, mid, lo = _split3(d_s[i * tb:(i + 1) * tb, :])
            dlf = _dot(tri, hi) + _dot(tri, mid) + _dot(tri, lo) + carry
            carry = dlf[0:1, :]
            z = ff_ref[i * tb:(i + 1) * tb, :] + b_ref[...]
            dff = dlf * _sigmoid(-z)
            dff_ref[i * tb:(i + 1) * tb, :] = dff.astype(BF)
            db = db + jnp.sum(dff, axis=0, keepdims=True)
        db_ref[...] = db

    return _pcall(
        body, [dcum_t, dcum_q, ffl, b_pad], name="forget_bwd",
        out_shape=[jax.ShapeDtypeStruct((t_tok, 128), BF), jax.ShapeDtypeStruct((1, 128), F32)],
        scratch=[pltpu.VMEM((128, t_tok), F32), pltpu.VMEM((t_tok, 128), F32)])


def _first_half():
    return lax.broadcasted_iota(jnp.int32, (1, 128), 1) < FOX_DIM


def _head_rows(x2, a2, hh):
    return jnp.where(_first_half(), x2, a2) if hh == 0 else jnp.where(_first_half(), a2, x2)


def _head_only(x2, hh):
    zero = jnp.zeros_like(x2)
    return jnp.where(_first_half(), x2, zero) if hh == 0 else jnp.where(_first_half(), zero, x2)


def _causal_diag(s):
    rows = lax.broadcasted_iota(jnp.int32, s.shape, 0)
    cols = lax.broadcasted_iota(jnp.int32, s.shape, 1)
    return jnp.where(cols <= rows, s, NEG)


def _diag_or_below(qi, ki, step):
    pl.when(ki < qi)(lambda: step(False))
    pl.when(ki == qi)(lambda: step(True))


def _tri_rows(s, n):
    qi = sum((s >= r * (r + 1) // 2).astype(jnp.int32) for r in range(1, n))
    return qi, s - (qi * (qi + 1)) // 2


def _tri_cols(s, n):
    ki = sum((s >= k * n - k * (k - 1) // 2).astype(jnp.int32) for k in range(1, n))
    return ki, ki + s - (ki * n - (ki * (ki - 1)) // 2)


def _fox_fwd(fq, fk, fv, aq, ak, comm=None):
    t_tok = fq.shape[0]
    t = _tile(t_tok, 512)
    nq = t_tok // t
    npair = FOX_HEADS // 2

    def body(q_ref, k_ref, v_ref, aq_ref, ak_ref, o_ref, of_ref, lse_ref, m_s, l_s, acc_s):
        qi, ki = _tri_rows(pl.program_id(1), nq)

        @pl.when(ki == 0)
        def _():
            m_s[...] = jnp.full_like(m_s, NEG)
            l_s[...] = jnp.zeros_like(l_s)
            acc_s[...] = jnp.zeros_like(acc_s)

        def step(diag):
            q2, k2, v2, aq2, ak2 = q_ref[...], k_ref[...], v_ref[...], aq_ref[...], ak_ref[...]
            for hh in range(2):
                s = _dot_nt(_head_rows(q2, aq2, hh), _head_rows(k2, ak2, hh))
                if diag:
                    s = _causal_diag(s)
                m_prev = m_s[hh]
                m_new = jnp.maximum(m_prev, jnp.max(s, axis=1, keepdims=True))
                alpha = jnp.exp(m_prev - m_new)
                p = jnp.exp(s - jnp.tile(m_new, (1, t // 128)))
                l_s[hh] = alpha * l_s[hh] + jnp.sum(p, axis=1, keepdims=True)
                acc_s[hh] = alpha * acc_s[hh] + _dot(p.astype(BF), v2)
                m_s[hh] = m_new

        _diag_or_below(qi, ki, step)

        @pl.when(ki == qi)
        def _():
            first = _first_half()
            o = jnp.where(first, acc_s[0] / l_s[0], acc_s[1] / l_s[1])
            o_ref[...] = o.astype(BF)
            of_ref[...] = o
            lse_ref[...] = jnp.where(first, m_s[0] + jnp.log(l_s[0]), m_s[1] + jnp.log(l_s[1]))

    qs = pl.BlockSpec((t, 128), lambda p, s: (_tri_rows(s, nq)[0], p))
    ks = pl.BlockSpec((t, 128), lambda p, s: (_tri_rows(s, nq)[1], p))
    stat = pltpu.VMEM((2, t, 128), F32)
    return _pcall(
        body, [fq, fk, fv, aq, ak], name="fox_fwd", grid=(npair, nq * (nq + 1) // 2),
        out_shape=[jax.ShapeDtypeStruct((t_tok, FOX_WIDTH), BF), jax.ShapeDtypeStruct((t_tok, FOX_WIDTH), F32),
                   jax.ShapeDtypeStruct((t_tok, FOX_WIDTH), F32)],
        in_specs=[qs, ks, ks, qs, ks], out_specs=[qs, qs, qs], scratch=[stat, stat, stat], comm=comm)


def _fox_ds(q2, k2, v2, do2, aq2, ak2, ad2, hh, diag):
    s = _dot_nt(_head_rows(q2, aq2, hh), _head_rows(k2, ak2, hh))
    if diag:
        s = _causal_diag(s)
    p = jnp.exp(s)
    av = jnp.where(_aug_lane() < 3, 1.0, 0.0).astype(BF)
    dp = _dot_nt(_head_rows(do2, ad2, hh), _head_rows(v2, jnp.broadcast_to(av, v2.shape), hh))
    return p, p * dp


def _fox_bwd(fq, fk, fv, do, aqb, ak, ad, comm=None):
    t_tok = fq.shape[0]
    t = _tile(t_tok, 512)
    nq = t_tok // t
    npair = FOX_HEADS // 2
    n_steps = nq * (nq + 1) // 2

    def body(q_ref, k_ref, v_ref, do_ref, aq_ref, ak_ref, ad_ref, dq_ref, dk_ref, dv_ref, dck_ref, dcq_ref,
             dk_s, dv_s, dq_s, rs_s):
        step_id = pl.program_id(1)
        ki, qi = _tri_cols(step_id, nq)

        @pl.when(step_id == 0)
        def _():
            dq_s[...] = jnp.zeros_like(dq_s)
            rs_s[...] = jnp.zeros_like(rs_s)

        @pl.when(qi == ki)
        def _():
            dk_s[...] = jnp.zeros_like(dk_s)
            dv_s[...] = jnp.zeros_like(dv_s)
            dck_ref[...] = jnp.zeros_like(dck_ref)

        rows = pl.ds(qi * t if isinstance(qi, int) else pl.multiple_of(qi * t, t), t)

        def step(diag):
            q2, k2, v2, do2 = q_ref[...], k_ref[...], v_ref[...], do_ref[...]
            dq = []
            for hh in range(2):
                p, ds = _fox_ds(q2, k2, v2, do2, aq_ref[...], ak_ref[...], ad_ref[...], hh, diag)
                dsb = ds.astype(BF)
                dv_s[...] += _dot_tn(p.astype(BF), _head_only(do2, hh))
                dk_s[...] += _dot_tn(dsb, _head_only(q2, hh))
                dq.append(_dot(dsb, k2))
                dck_ref[hh] = dck_ref[hh] - jnp.sum(ds, axis=0, keepdims=True)
                rs_s[hh, rows, :] = rs_s[hh, rows, :] + jnp.sum(ds, axis=1, keepdims=True)
            dq_s[rows, :] = dq_s[rows, :] + jnp.where(_first_half(), dq[0], dq[1])

        _diag_or_below(qi, ki, step)

        @pl.when(qi == nq - 1)
        def _():
            dk_ref[...] = dk_s[...].astype(BF)
            dv_ref[...] = dv_s[...].astype(BF)

        @pl.when(step_id == n_steps - 1)
        def _():
            dq_ref[...] = (dq_s[...] * FOX_SCALE).astype(BF)
            dcq_ref[...] = jnp.where(_first_half(), rs_s[0], rs_s[1])

    qs = pl.BlockSpec((t, 128), lambda p, s: (_tri_cols(s, nq)[1], p))
    ks = pl.BlockSpec((t, 128), lambda p, s: (_tri_cols(s, nq)[0], p))
    cks = pl.BlockSpec((2, 1, t), lambda p, s: (p, 0, _tri_cols(s, nq)[0]))
    seq = pl.BlockSpec((t_tok, 128), lambda p, s: (0, p))
    sds = jax.ShapeDtypeStruct((t_tok, FOX_WIDTH), BF)
    return _pcall(
        body, [fq, fk, fv, do, aqb, ak, ad], name="fox_bwd", grid=(npair, n_steps),
        out_shape=[sds, sds, sds, jax.ShapeDtypeStruct((FOX_HEADS, 1, t_tok), F32),
                   jax.ShapeDtypeStruct((t_tok, FOX_WIDTH), F32)],
        in_specs=[qs, ks, ks, qs, qs, ks, qs], out_specs=[seq, ks, ks, cks, seq],
        scratch=[pltpu.VMEM((t, 128), F32), pltpu.VMEM((t, 128), F32), pltpu.VMEM((t_tok, 128), F32),
                 pltpu.VMEM((2, t_tok, 128), F32)], comm=comm)


def _ret_consts():
    c = RET_CHUNK
    log_gamma = jnp.log1p(-jnp.exp2(-5.0 - jnp.arange(RET_HEADS, dtype=F32)))
    idx = jnp.arange(c, dtype=F32)
    diff = idx[:, None] - idx[None, :]
    dmask = jnp.where(diff >= 0, jnp.exp(log_gamma[:, None, None] * jnp.maximum(diff, 0.0)), 0.0)
    qdec = jnp.exp(log_gamma[:, None] * (idx + 1.0))
    kdec = jnp.exp(log_gamma[:, None] * (c - 1 - idx))
    cdec = jnp.exp(log_gamma * c)
    bc = lambda v: jnp.broadcast_to(v[:, :, None], (RET_HEADS, c, RET_DIM))
    return dmask, bc(qdec), bc(kdec), jnp.broadcast_to(cdec[:, None, None], (RET_HEADS, c, RET_DIM))


def _group_norm(y):
    mu = jnp.mean(y, axis=-1, keepdims=True)
    yc = y - mu
    r = lax.rsqrt(jnp.mean(yc * yc, axis=-1, keepdims=True) + EPS)
    return yc * r, r


def _ret_fwd(rq, rk, rv, rg, consts, comm=None):
    t_tok = rq.shape[0]
    nb = 4 if t_tok % (4 * RET_CHUNK) == 0 else 1
    tr = nb * RET_CHUNK
    n_steps = t_tok // tr
    c = RET_CHUNK

    def body(q_ref, k_ref, v_ref, g_ref, dm_ref, qd_ref, kd_ref, cd_ref, y_ref, yo_ref, st_ref, s_s):
        @pl.when(pl.program_id(0) == 0)
        def _():
            s_s[...] = jnp.zeros_like(s_s)

        for b in range(nb):
            rows = slice(b * c, (b + 1) * c)
            for hh in range(RET_HEADS):
                cols = slice(hh * RET_DIM, (hh + 1) * RET_DIM)
                q, k, v = q_ref[rows, cols], k_ref[rows, cols], v_ref[rows, cols]
                state = s_s[hh]
                st_ref[hh, b] = state
                sc = (_dot_nt(q, k) * dm_ref[hh]).astype(BF)
                y = _dot(sc, v) + _dot((q.astype(F32) * qd_ref[hh]).astype(BF), state.astype(BF))
                s_s[hh] = cd_ref[hh] * state + _dot_tn((k.astype(F32) * kd_ref[hh]).astype(BF), v)
                y_ref[rows, cols] = y
                yn, _ = _group_norm(y)
                gate = g_ref[rows, cols].astype(F32)
                yo_ref[rows, cols] = (yn * (gate * _sigmoid(gate))).astype(BF)

    blk = pl.BlockSpec((tr, RET_WIDTH), lambda i: (i, 0))
    cst = pl.BlockSpec((RET_HEADS, c, RET_DIM), lambda i: (0, 0, 0))
    return _pcall(
        body, [rq, rk, rv, rg, *consts], name="ret_fwd", grid=(n_steps,),
        out_shape=[jax.ShapeDtypeStruct((t_tok, RET_WIDTH), F32), jax.ShapeDtypeStruct((t_tok, RET_WIDTH), BF),
                   jax.ShapeDtypeStruct((RET_HEADS, t_tok // c, RET_DIM, RET_DIM), F32)],
        in_specs=[blk] * 4 + [cst] * 4,
        out_specs=[blk, blk, pl.BlockSpec((RET_HEADS, nb, RET_DIM, RET_DIM), lambda i: (0, i, 0, 0))],
        scratch=[pltpu.VMEM((RET_HEADS, RET_DIM, RET_DIM), F32)], comm=comm)


def _ret_bwd(rq, rk, rv, rg, y_raw, dyo, states, consts, cos_t, sin_t, comm=None):
    t_tok = rq.shape[0]
    nb = 4 if t_tok % (4 * RET_CHUNK) == 0 else 1
    tr = nb * RET_CHUNK
    n_steps = t_tok // tr
    c = RET_CHUNK

    def body(q_ref, k_ref, v_ref, g_ref, y_ref, dyo_ref, st_ref, dm_ref, qd_ref, kd_ref, cd_ref,
             cos_ref, sin_ref, dq_ref, dk_ref, dv_ref, dg_ref, ds_s):
        @pl.when(pl.program_id(0) == 0)
        def _():
            ds_s[...] = jnp.zeros_like(ds_s)

        for b in reversed(range(nb)):
            rows = slice(b * c, (b + 1) * c)
            cosv, sinv = cos_ref[rows, :], sin_ref[rows, :]
            for hh in range(RET_HEADS):
                cols = slice(hh * RET_DIM, (hh + 1) * RET_DIM)
                dm, qd, kd, cd = dm_ref[hh], qd_ref[hh], kd_ref[hh], cd_ref[hh]
                q, k, v = q_ref[rows, cols], k_ref[rows, cols], v_ref[rows, cols]
                yn, r = _group_norm(y_ref[rows, cols])
                gate = g_ref[rows, cols].astype(F32)
                sg = _sigmoid(gate)
                dyo = dyo_ref[rows, cols]
                dg_ref[rows, cols] = (dyo * yn * (sg * (1.0 + gate * (1.0 - sg)))).astype(BF)
                dyn = dyo * (gate * sg)
                dy = r * (dyn - jnp.mean(dyn, axis=-1, keepdims=True)
                          - yn * jnp.mean(dyn * yn, axis=-1, keepdims=True))
                dyb = dy.astype(BF)
                state_b = st_ref[hh, b].astype(BF)
                dstate = ds_s[hh]
                dstate_b = dstate.astype(BF)
                qdb = (q.astype(F32) * qd).astype(BF)
                kdb = (k.astype(F32) * kd).astype(BF)
                sc = (_dot_nt(q, k) * dm).astype(BF)
                dv = _dot_tn(sc, dyb) + _dot(kdb, dstate_b)
                dp = (_dot_nt(dyb, v) * dm).astype(BF)
                dq = _dot(dp, k) + _dot_nt(dyb, state_b) * qd
                dk = (_dot_tn(dp, q) + _dot_nt(v, dstate_b) * kd) * RET_SCALE
                ds_s[hh] = cd * dstate + _dot_tn(qdb, dyb)
                dv_ref[rows, cols] = dv.astype(BF)
                dq_ref[rows, cols] = (dq * cosv - _swap_pairs(dq) * sinv).astype(BF)
                dk_ref[rows, cols] = (dk * cosv - _swap_pairs(dk) * sinv).astype(BF)

    rev = lambda i: n_steps - 1 - i
    blk = pl.BlockSpec((tr, RET_WIDTH), lambda i: (rev(i), 0))
    tab = pl.BlockSpec((tr, RET_DIM), lambda i: (rev(i), 0))
    cst = pl.BlockSpec((RET_HEADS, c, RET_DIM), lambda i: (0, 0, 0))
    sds = jax.ShapeDtypeStruct((t_tok, RET_WIDTH), BF)
    return _pcall(
        body, [rq, rk, rv, rg, y_raw, dyo, states, *consts, cos_t, sin_t], name="ret_bwd",
        grid=(n_steps,), out_shape=[sds] * 4,
        in_specs=[blk] * 6 + [pl.BlockSpec((RET_HEADS, nb, RET_DIM, RET_DIM), lambda i: (0, rev(i), 0, 0))]
        + [cst] * 4 + [tab, tab],
        out_specs=[blk] * 4, scratch=[pltpu.VMEM((RET_HEADS, RET_DIM, RET_DIM), F32)], comm=comm)


def _mix_out(h, y_ret, y_fox, ga, gb, wr4, wf4, wo4, comm=None):
    t_tok, d = h.shape
    cz = wr4.shape[-1]
    ro = wo4.shape[-2]
    tm = _tile(t_tok, 512)

    def body(h_ref, yr_ref, yf_ref, ga_ref, gb_ref, wr_ref, wf_ref, wo_ref, ho_ref, za_ref, zb_ref, mix_ref):
        yr, yf = yr_ref[...], yf_ref[...]
        for j in range(N_CHIPS):
            sl = slice(j * cz, (j + 1) * cz)
            za = _dot(yr, wr_ref[j])
            zb = _dot(yf, wf_ref[j])
            za_ref[:, sl] = za.astype(BF)
            zb_ref[:, sl] = zb.astype(BF)
            mix_ref[:, sl] = (ga_ref[:, sl].astype(F32) * za + gb_ref[:, sl].astype(F32) * zb).astype(BF)
        acc = h_ref[...]
        for j in range(N_CHIPS):
            acc = acc + _dot(mix_ref[:, j * ro:(j + 1) * ro], wo_ref[j])
        ho_ref[...] = acc

    row = lambda c: pl.BlockSpec((tm, c), lambda i: (i, 0))
    full = lambda *s: pl.BlockSpec(s, lambda i: (0,) * len(s))
    sds = lambda dt: jax.ShapeDtypeStruct((t_tok, d), dt)
    return _pcall(
        body, [h, y_ret, y_fox, ga, gb, wr4, wf4, wo4], name="mix_out", grid=(t_tok // tm,),
        out_shape=[sds(F32), sds(BF), sds(BF), sds(BF)],
        in_specs=[row(d), row(RET_WIDTH), row(FOX_WIDTH), row(d), row(d),
                  full(N_CHIPS, RET_WIDTH, cz), full(N_CHIPS, FOX_WIDTH, cz), full(N_CHIPS, ro, d)],
        out_specs=[row(d)] * 4, comm=comm)


def _mix_out_bwd(dh, za, zb, ga, gb, y_fox, wr4, wf4, wo4, comm=None):
    t_tok, d = dh.shape
    cz = wr4.shape[-1]
    ro = wo4.shape[-2]
    tm = _tile(t_tok, 256)

    def body(dh_ref, za_ref, zb_ref, ga_ref, gb_ref, yf_ref, wr_ref, wf_ref, wo_ref,
             dhb_ref, dgp_ref, dza_ref, dzb_ref, dyr_ref, dyf_ref, dl_ref, db_ref):
        @pl.when(pl.program_id(0) == 0)
        def _():
            db_ref[...] = jnp.zeros_like(db_ref)

        dhb = dh_ref[...].astype(BF)
        dhb_ref[...] = dhb
        dyr = jnp.zeros((tm, RET_WIDTH), F32)
        dyf = jnp.zeros((tm, FOX_WIDTH), F32)
        for j in range(N_CHIPS):
            sl = slice(j * ro, (j + 1) * ro)
            dmix = _dot_nt(dhb, wo_ref[j])
            ga, gb = ga_ref[:, sl].astype(F32), gb_ref[:, sl].astype(F32)
            dza = (dmix * ga).astype(BF)
            dzb = (dmix * gb).astype(BF)
            dza_ref[:, sl] = dza
            dzb_ref[:, sl] = dzb
            dga = dmix * za_ref[:, sl].astype(F32) * ga * (1.0 - ga)
            dgb = dmix * zb_ref[:, sl].astype(F32) * gb * (1.0 - gb)
            dgp_ref[:, sl] = dga.astype(BF)
            dgp_ref[:, d + j * ro:d + (j + 1) * ro] = dgb.astype(BF)
            db_ref[:, sl] += jnp.sum(dga, axis=0, keepdims=True)
            db_ref[:, d + j * ro:d + (j + 1) * ro] += jnp.sum(dgb, axis=0, keepdims=True)
        for j in range(N_CHIPS):
            sl = slice(j * cz, (j + 1) * cz)
            dyr = dyr + _dot_nt(dza_ref[:, sl], wr_ref[j])
            dyf = dyf + _dot_nt(dzb_ref[:, sl], wf_ref[j])
        dyr_ref[...] = dyr
        dyfb = dyf.astype(BF)
        dyf_ref[...] = dyfb
        prod = dyfb.astype(F32) * yf_ref[...]
        first = _first_half()
        for pp in range(FOX_HEADS // 2):
            blk = prod[:, pp * 128:(pp + 1) * 128]
            s0 = jnp.sum(jnp.where(first, blk, 0.0), axis=1, keepdims=True)
            s1 = jnp.sum(jnp.where(first, 0.0, blk), axis=1, keepdims=True)
            parts = _split3(-jnp.where(first, s1, s0))
            dl_ref[:, pp * 128:(pp + 1) * 128] = _aug_put(jnp.zeros((tm, 128), BF), 0, parts)

    row = lambda c: pl.BlockSpec((tm, c), lambda i: (i, 0))
    full = lambda *s: pl.BlockSpec(s, lambda i: (0,) * len(s))
    sds = lambda c, dt: jax.ShapeDtypeStruct((t_tok, c), dt)
    return _pcall(
        body, [dh, za, zb, ga, gb, y_fox, wr4, wf4, wo4], name="mix_out_bwd", grid=(t_tok // tm,),
        out_shape=[sds(d, BF), sds(2 * d, BF), sds(d, BF), sds(d, BF), sds(RET_WIDTH, F32),
                   sds(FOX_WIDTH, BF), sds(FOX_WIDTH, BF), jax.ShapeDtypeStruct((1, 2 * d), F32)],
        in_specs=[row(d)] * 5 + [row(FOX_WIDTH), full(N_CHIPS, RET_WIDTH, cz), full(N_CHIPS, FOX_WIDTH, cz),
                                 full(N_CHIPS, ro, d)],
        out_specs=[row(d), row(2 * d), row(d), row(d), row(RET_WIDTH), row(FOX_WIDTH), row(FOX_WIDTH),
                   full(1, 2 * d)],
        comm=comm)


def _mix_in_bwd(dh, h, ln, parts, dff, dgpre, w_in, wm4, comm=None):
    t_tok, d = h.shape
    cm = wm4.shape[-1]
    tm = _tile(t_tok, 256)

    def body(dh_ref, h_ref, ln_ref, p0, p1, p2, p3, p4, p5, p6, dff_ref, dgp_ref, win_ref, wm_ref,
             dhi_ref, dln_ref, dproj_ref):
        @pl.when(pl.program_id(0) == 0)
        def _():
            dln_ref[...] = jnp.zeros_like(dln_ref)

        for k, pr in enumerate((p0, p1, p2, p3, p4, p5, p6)):
            dproj_ref[:, k * 512:(k + 1) * 512] = pr[...]
        dproj_ref[:, FF_COL:FF_COL + 128] = dff_ref[...]
        dproj_ref[:, FF_COL + 128:] = jnp.zeros((tm, IN_PAD - FF_COL - 128), BF)
        du = _dot(dproj_ref[...], win_ref[...])
        for j in range(N_CHIPS):
            du = du + _dot_nt(dgp_ref[:, j * cm:(j + 1) * cm], wm_ref[j])
        xv = h_ref[...]
        dx, dln = _rms_bwd(du, xv, _rstd(xv), ln_ref[...])
        dln_ref[...] += dln
        dhi_ref[...] = dh_ref[...] + dx

    row = lambda c: pl.BlockSpec((tm, c), lambda i: (i, 0))
    full = lambda *s: pl.BlockSpec(s, lambda i: (0,) * len(s))
    return _pcall(
        body, [dh, h, ln, *parts, dff, dgpre, w_in, wm4], name="mix_in_bwd", grid=(t_tok // tm,),
        out_shape=[jax.ShapeDtypeStruct((t_tok, d), F32), jax.ShapeDtypeStruct((1, d), F32),
                   jax.ShapeDtypeStruct((t_tok, IN_PAD), BF)],
        in_specs=[row(d), row(d), full(1, d)] + [row(512)] * 7 + [row(128), row(2 * d), full(IN_PAD, d),
                                                                   full(N_CHIPS, d, cm)],
        out_specs=[row(d), full(1, d), row(IN_PAD)], comm=comm)


def _tail(h, p, target, ln_ple, ln_fin, wpg4, wpl4, comm=None):
    t_tok, d = h.shape
    pd = p.shape[1]
    rg = wpg4.shape[-2]
    cp = wpl4.shape[-1]
    tm = _tile(t_tok, 256)

    def body(h_ref, p_ref, t_ref, lp_ref, lf_ref, wg_ref, wp_ref,
             dh_ref, n_ref, dgp_ref, dpe_ref, pb_ref, loss_ref, dlf_ref, dlp_ref, pe_s, dn_s):
        @pl.when(pl.program_id(0) == 0)
        def _():
            loss_ref[...] = jnp.zeros_like(loss_ref)
            dlf_ref[...] = jnp.zeros_like(dlf_ref)
            dlp_ref[...] = jnp.zeros_like(dlp_ref)

        xv = h_ref[...]
        r3 = _rstd(xv)
        nb = (xv * r3 * lp_ref[...]).astype(BF)
        n_ref[...] = nb
        pb = p_ref[...].astype(BF)
        pb_ref[...] = pb
        pgpre = jnp.zeros((tm, d), F32)
        for j in range(N_CHIPS):
            pgpre = pgpre + _dot(nb[:, j * rg:(j + 1) * rg], wg_ref[j])
            pe_s[:, j * cp:(j + 1) * cp] = _dot(pb, wp_ref[j])
        pg = _sigmoid(pgpre)
        pe = pe_s[...]
        h4 = xv + pg * pe
        r4 = _rstd(h4)
        err = h4 * r4 * lf_ref[...] - t_ref[...]
        loss_ref[...] += 0.5 * jnp.sum(jnp.sum(err * err, axis=1, keepdims=True), axis=0, keepdims=True) / d
        dh4, dlf = _rms_bwd(err * (1.0 / d), h4, r4, lf_ref[...])
        dlf_ref[...] += dlf
        dpe_ref[...] = (dh4 * pg).astype(BF)
        dgp = (dh4 * pe * pg * (1.0 - pg)).astype(BF)
        dgp_ref[...] = dgp
        for j in range(N_CHIPS):
            dn_s[:, j * rg:(j + 1) * rg] = _dot_nt(dgp, wg_ref[j])
        dx, dlp = _rms_bwd(dn_s[...], xv, r3, lp_ref[...])
        dlp_ref[...] += dlp
        dh_ref[...] = dh4 + dx

    row = lambda c: pl.BlockSpec((tm, c), lambda i: (i, 0))
    full = lambda *s: pl.BlockSpec(s, lambda i: (0,) * len(s))
    sds = lambda c, dt: jax.ShapeDtypeStruct((t_tok, c), dt)
    vec = jax.ShapeDtypeStruct((1, d), F32)
    return _pcall(
        body, [h, p, target, ln_ple, ln_fin, wpg4, wpl4], name="tail", grid=(t_tok // tm,),
        out_shape=[sds(d, F32), sds(d, BF), sds(d, BF), sds(d, BF), sds(pd, BF),
                   jax.ShapeDtypeStruct((1, 128), F32), vec, vec],
        in_specs=[row(d), row(pd), row(d), full(1, d), full(1, d), full(N_CHIPS, rg, d), full(N_CHIPS, pd, cp)],
        out_specs=[row(d), row(d), row(d), row(d), row(pd), full(1, 128), full(1, d), full(1, d)],
        scratch=[pltpu.VMEM((tm, d), F32), pltpu.VMEM((tm, d), F32)], comm=comm)


BIG = ["w_ffn1_gate", "w_ffn1_up", "w_ffn1_down", "w_in", "w_merge", "w_ret_out", "w_fox_out", "w_out",
       "w_ffn2_gate", "w_ffn2_up", "w_ffn2_down", "w_ple", "w_ple_gate"]
SMALL = ["ln_ffn1", "ln_mix", "b_forget", "b_merge", "ln_ffn2", "ln_ple", "ln_final"]
WEIGHTS = ["ln_ffn1", "w_ffn1_gate", "w_ffn1_up", "w_ffn1_down", "ln_mix", "w_in", "b_forget", "w_merge", "b_merge",
           "w_ret_out", "w_fox_out", "w_out", "ln_ffn2", "w_ffn2_gate", "w_ffn2_up", "w_ffn2_down", "ln_ple",
           "w_ple", "w_ple_gate", "ln_final"]


TRANSPOSED = {"w_ffn1_gate", "w_ffn1_up", "w_ffn2_gate", "w_ffn2_up", "w_in"}
IN_ROWS_PAD = -(-(IN_COLS // N_CHIPS) // 32) * 32


def _pack_small(vals, loss_row):
    rows = [loss_row]
    for name in SMALL:
        v = vals[name].reshape(-1)
        n = -(-v.shape[0] // 128) * 128
        rows.append(jnp.pad(v, (0, n - v.shape[0])).reshape(n // 128, 128))
    packed = jnp.concatenate(rows, axis=0)
    pad = -packed.shape[0] % 8
    return jnp.pad(packed, ((0, pad), (0, 0)))


def _unpack_small(packed, sizes):
    out, r = {}, 1
    for name in SMALL:
        n = sizes[name]
        nr = -(-n // 128)
        out[name] = packed[r:r + nr].reshape(1, nr * 128)[:, :n]
        r += nr
    return out


class _Stage:
    def __init__(self, comm, finish):
        self.comm, self.finish, self.result = comm, finish, None


def _hosted(fn, *a, stages=()):
    if not stages:
        return fn(*a)
    outs, couts = fn(*a, comm=_merge([st.comm for st in stages]))
    for st, o in zip(stages, _split_outs([st.comm for st in stages], couts)):
        st.result = st.finish(o)
    return outs


class _Reducer:
    def __init__(self):
        self.done = {}

    def swap(self, grads):
        names = list(grads)
        return _Stage(_c_half_swap([grads[n] for n in names]),
                      lambda outs: {n: _add_halves(grads[n], o) for n, o in zip(names, outs)})

    def exchange(self, parts):
        names = list(parts)
        return _Stage(_c_chip_exchange([parts[n] for n in names]),
                      lambda outs: {n: _sum_chips(parts[n], o) for n, o in zip(names, outs)})

    def join(self, halves):
        names = list(halves)
        return _Stage(_c_join([halves[n] for n in names]),
                      lambda outs: self.done.update({n: (halves[n], o) for n, o in zip(names, outs)}))


def kernel(x, p, positions, ln_ffn1, w_ffn1_gate, w_ffn1_up, w_ffn1_down, ln_mix, w_in, b_forget, w_merge, b_merge, w_ret_out, w_fox_out, w_out, ln_ffn2, w_ffn2_gate, w_ffn2_up, w_ffn2_down, ln_ple, w_ple, w_ple_gate, ln_final, loss_target, m_ln_ffn1, m_w_ffn1_gate, m_w_ffn1_up, m_w_ffn1_down, m_ln_mix, m_w_in, m_b_forget, m_w_merge, m_b_merge, m_w_ret_out, m_w_fox_out, m_w_out, m_ln_ffn2, m_w_ffn2_gate, m_w_ffn2_up, m_w_ffn2_down, m_ln_ple, m_w_ple, m_w_ple_gate, m_ln_final, v_ln_ffn1, v_w_ffn1_gate, v_w_ffn1_up, v_w_ffn1_down, v_ln_mix, v_w_in, v_b_forget, v_w_merge, v_b_merge, v_w_ret_out, v_w_fox_out, v_w_out, v_ln_ffn2, v_w_ffn2_gate, v_w_ffn2_up, v_w_ffn2_down, v_ln_ple, v_w_ple, v_w_ple_gate, v_ln_final):
    args = dict(locals())
    w = {n: args[n] for n in WEIGHTS}
    m = {n: args["m_" + n] for n in WEIGHTS}
    v = {n: args["v_" + n] for n in WEIGHTS}
    d = x.shape[-1]
    t_tok = x.shape[1]
    xs, ps, target = x[0], p[0, 0], loss_target[0]
    small = {n: w[n].reshape(1, -1) for n in SMALL}

    def to2d(n, a):
        if n in TRANSPOSED:
            return a[0].T
        return a.reshape(a.shape[-2], a.shape[-1]) if a.ndim == 3 else a.reshape(1, -1)

    def from2d(n, a):
        return a.T[None] if n in TRANSPOSED else a.reshape(w[n].shape)

    def padded(n, a):
        return jnp.pad(a, ((0, IN_ROWS_PAD - a.shape[0]), (0, 0))) if n == "w_in" else a

    core = lax.axis_index("c")
    me = 2 * lax.axis_index("x") + lax.axis_index("y")
    shard = {}
    for n in BIG:
        s2 = padded(n, to2d(n, w[n]).astype(BF))
        shard[n] = s2.reshape(1, 2, s2.shape[0] // 2, s2.shape[1])
    full = {}

    def gather(names):
        bufs = [lax.dynamic_update_slice(jnp.zeros((N_CHIPS,) + shard[n].shape[1:], BF), shard[n], (me, 0, 0, 0))
                for n in names]

        def finish(outs):
            full.update({n: o.reshape(N_CHIPS, 2 * o.shape[2], o.shape[3]) for n, o in zip(names, outs)})

        return _Stage(_c_all_gather(bufs), finish)

    half = RET_DIM // 2
    inv_freq = 1.0 / (ROPE_BASE ** (jnp.arange(half, dtype=F32) / half))
    cos_t, sin_t = _hosted(_rope_tables, positions[0].astype(F32).reshape(t_tok, 1),
                           jnp.repeat(inv_freq, 2).reshape(1, RET_DIM),
                           stages=[gather(["w_ffn1_gate", "w_ffn1_up", "w_ffn1_down"])])
    consts = _ret_consts()
    b_pad = jnp.pad(small["b_forget"], ((0, 0), (0, 128 - FOX_HEADS)))

    h1, n1, g1, u1 = _hosted(
        _ffn_fwd, xs, small["ln_ffn1"], full["w_ffn1_gate"], full["w_ffn1_up"], full["w_ffn1_down"],
        stages=[gather(["w_in", "w_merge", "w_ret_out", "w_fox_out", "w_out", "w_ple_gate", "w_ple"])])
    w_in_full = jnp.pad(full["w_in"][:, :IN_COLS // N_CHIPS].reshape(IN_COLS, d), ((0, IN_PAD - IN_COLS), (0, 0)))
    u, rq, rk, rv, rg, fq, fk, fv, ffl, ga, gb = _mix_in(
        h1, small["ln_mix"], w_in_full, full["w_merge"], small["b_merge"], cos_t, sin_t)
    aq, ak = _forget_fwd(ffl, b_pad)
    y_raw, y_ret, states = _ret_fwd(rq, rk, rv, rg, consts)
    y_fox, y_fox32, lse_e = _hosted(_fox_fwd, fq, fk, fv, aq, ak,
                                    stages=[gather(["w_ffn2_gate", "w_ffn2_up", "w_ffn2_down"])])
    aqb = _fox_aug_lse(aq, lse_e)
    h2, za, zb, mix = _mix_out(h1, y_ret, y_fox, ga, gb, full["w_ret_out"], full["w_fox_out"], full["w_out"])
    h3, n2, g2, u2 = _ffn_fwd(h2, small["ln_ffn2"], full["w_ffn2_gate"], full["w_ffn2_up"], full["w_ffn2_down"])

    red = _Reducer()
    dh3, n3, dpgpre, dpe, pb, loss, dln_final, dln_ple = _tail(
        h3, ps, target, small["ln_ple"], small["ln_final"], full["w_ple_gate"], full["w_ple"])
    g_f2 = dict(w_ple_gate=_wgrad_rows("wgrad_ple_gate", n3, dpgpre, N_CHIPS),
                w_ple=_wgrad_cols("wgrad_ple", pb, dpe, N_CHIPS))
    dh2, dln_ffn2, dg2, du2, a2, dhb3 = _ffn_bwd(
        dh3, h2, small["ln_ffn2"], g2, u2, full["w_ffn2_gate"], full["w_ffn2_up"], full["w_ffn2_down"])
    g_f2["w_ffn2_gate"] = _wgrad_b_shared("wgrad_ffn2_gate", dg2, n2)
    g_f2["w_ffn2_up"] = _wgrad_b_shared("wgrad_ffn2_up", du2, n2)
    g_f2["w_ffn2_down"] = _wgrad_b_shared("wgrad_ffn2_down", a2, dhb3)

    sw_f2 = red.swap(g_f2)
    dhb2, dgpre, dza, dzb, dy_ret, dy_fox, ad, db_merge = _hosted(
        _mix_out_bwd, dh2, za, zb, ga, gb, y_fox32, full["w_ret_out"], full["w_fox_out"], full["w_out"],
        stages=[sw_f2])
    g_br = dict(w_out=_wgrad_rows("wgrad_out", mix, dhb2, N_CHIPS),
                w_ret_out=_wgrad_cols("wgrad_ret_out", y_ret, dza, N_CHIPS),
                w_fox_out=_wgrad_cols("wgrad_fox_out", y_fox, dzb, N_CHIPS))

    sw_br = red.swap(g_br)
    drq, drk, drv, drg = _hosted(_ret_bwd, rq, rk, rv, rg, y_raw, dy_ret, states, consts, cos_t, sin_t,
                                 stages=[sw_br])
    ex_f2, ex_br = red.exchange(sw_f2.result), red.exchange(sw_br.result)
    dfq, dfk, dfv, dcum_t3, dcum_q = _hosted(_fox_bwd, fq, fk, fv, dy_fox, aqb, ak, ad, stages=[ex_f2, ex_br])
    dff, db_forget = _forget_bwd(dcum_t3.reshape(FOX_HEADS, t_tok), dcum_q, ffl, b_pad)
    dh1, dln_mix, dproj = _hosted(
        _mix_in_bwd, dh2, h1, small["ln_mix"], (drq, drk, drv, drg, dfq, dfk, dfv), dff, dgpre, w_in_full,
        full["w_merge"], stages=[red.join(ex_f2.result), red.join(ex_br.result)])

    dx, dln_ffn1, dg1, du1, a1, dhb1 = _ffn_bwd(
        dh1, xs, small["ln_ffn1"], g1, u1, full["w_ffn1_gate"], full["w_ffn1_up"], full["w_ffn1_down"])
    g_f1g = _wgrad_b_shared("wgrad_ffn1_gate", dg1, n1)
    sw_f1g = red.swap(dict(w_ffn1_gate=g_f1g))
    g_f1u = _hosted(_wgrad_b_shared, "wgrad_ffn1_up", du1, n1, stages=[sw_f1g])
    ex_f1g, sw_f1u = red.exchange(sw_f1g.result), red.swap(dict(w_ffn1_up=g_f1u))
    g_f1d = _hosted(_wgrad_b_shared, "wgrad_ffn1_down", a1, dhb1, stages=[ex_f1g, sw_f1u])

    ex_f1u, sw_f1d = red.exchange(sw_f1u.result), red.swap(dict(w_ffn1_down=g_f1d))
    g_in = _hosted(_wgrad_rows, "wgrad_in", dproj, u, IN_PAD // 512,
                   stages=[ex_f1u, sw_f1d, red.join(ex_f1g.result)])
    g_in = g_in.reshape(IN_PAD, d)[:IN_COLS].reshape(N_CHIPS, IN_COLS // N_CHIPS, d)
    g_in = jnp.pad(g_in, ((0, 0), (0, IN_ROWS_PAD - IN_COLS // N_CHIPS), (0, 0)))
    ex_f1d, sw_in = red.exchange(sw_f1d.result), red.swap(dict(w_in=g_in))
    g_mrg = _hosted(_wgrad_cols, "wgrad_merge", u, dgpre, N_CHIPS,
                    stages=[ex_f1d, sw_in, red.join(ex_f1u.result)])

    small_grads = dict(ln_ffn1=dln_ffn1, ln_mix=dln_mix, b_forget=db_forget[:, :FOX_HEADS], b_merge=db_merge,
                       ln_ffn2=dln_ffn2, ln_ple=dln_ple, ln_final=dln_final)
    sizes = {n: w[n].size for n in SMALL}
    ex_in, sw_mrg = red.exchange(sw_in.result), red.swap(dict(w_merge=g_mrg))
    reduced = _hosted(_all_reduce_small, _pack_small(small_grads, loss),
                      stages=[ex_in, sw_mrg, red.join(ex_f1d.result)])
    gsum = _unpack_small(reduced, sizes)
    loss = reduced[0, 0]
    ex_mrg = red.exchange(sw_mrg.result)
    _hosted(_exchange_only, stages=[ex_mrg, red.join(ex_in.result)])
    _hosted(_exchange_only, stages=[red.join(ex_mrg.result)])

    results = {}

    def update(names, stages=()):
        w2, m2, v2 = ([to2d(n, a[n]) for n in names] for a in (w, m, v))
        n = names[0]
        if n in gsum or n == "w_in":
            if n in gsum:
                g2 = gsum[n]
            else:
                mine, other = red.done[n]
                g2 = jnp.where(core == 0, jnp.concatenate([mine, other]), jnp.concatenate([other, mine]))
                g2 = g2[:w2[0].shape[0]]
            if n == "w_in":
                lin = lambda a: a.reshape(-1, 128)
                res = _hosted(_adamw, lin(w2[0]), lin(g2), lin(m2[0]), lin(v2[0]), stages=stages)
                res = [g2] + [a.reshape(g2.shape) for a in res]
            else:
                res = [g2] + _hosted(_adamw, w2[0], g2, m2[0], v2[0], stages=stages)
        else:
            res = _hosted(_adamw_halves, [(w2[q], *red.done[names[q]], m2[q], v2[q]) for q in range(len(names))],
                          stages=stages)
        for q, name in enumerate(names):
            results[name] = tuple(from2d(name, a) for a in res[4 * q:4 * q + 4])

    update(["w_ffn2_gate", "w_ffn2_up", "w_ffn2_down"])
    update(["w_ffn1_gate", "w_ffn1_up", "w_ffn1_down"])
    update(["w_out", "w_ple_gate"])
    update(["w_ret_out", "w_fox_out"])
    for n in WEIGHTS:
        if n not in results:
            update([n])

    outs = [[results[n][k] for n in WEIGHTS] for k in range(4)]
    return (loss, dx[None], *outs[0], *outs[1], *outs[2], *outs[3])
```

```python
import functools
import operator

import jax
import jax.numpy as jnp
from jax import lax
from jax.experimental import pallas as pl
from jax.experimental.pallas import tpu as pltpu

F32 = jnp.float32
BF = jnp.bfloat16
MESH = pl.DeviceIdType.MESH

EPS = 1e-6
ROPE_BASE = 10000.0
N_CHIPS = 4
RET_HEADS = 4
RET_DIM = 128
RET_WIDTH = RET_HEADS * RET_DIM
RET_CHUNK = 128
RET_SCALE = RET_DIM ** -0.5
FOX_HEADS = 8
FOX_DIM = 64
FOX_WIDTH = FOX_HEADS * FOX_DIM
FOX_SCALE = FOX_DIM ** -0.5
IN_COLS = 4 * RET_WIDTH + 3 * FOX_WIDTH + FOX_HEADS
IN_PAD = 4096
FF_COL = 4 * RET_WIDTH + 3 * FOX_WIDTH
NEG = -1e30

ADAM_LR = 0.001
ADAM_B1 = 0.9
ADAM_B2 = 0.999
ADAM_EPS = 1e-08
ADAM_WD = 0.01
ADAM_STEP = 10

VMEM_LIMIT = 52 * 1024 * 1024

NT = (((1,), (1,)), ((), ()))
TN = (((0,), (0,)), ((), ()))

HBM_SPEC = pl.BlockSpec(memory_space=pltpu.HBM)
VMEM_SPEC = pl.BlockSpec(memory_space=pltpu.VMEM)


def _dot(a, b):
    return jnp.dot(a, b, preferred_element_type=F32)


def _dot_nt(a, b):
    return lax.dot_general(a, b, NT, preferred_element_type=F32)


def _dot_tn(a, b):
    return lax.dot_general(a, b, TN, preferred_element_type=F32)


def _rstd(xv):
    return lax.rsqrt(jnp.mean(xv * xv, axis=-1, keepdims=True) + EPS)


def _rms_bwd(dn, xv, r, ln):
    xh = xv * r
    dxh = dn * ln
    dx = r * (dxh - xh * jnp.mean(dxh * xh, axis=-1, keepdims=True))
    return dx, jnp.sum(dn * xh, axis=0, keepdims=True)


def _sigmoid(x):
    return jax.nn.sigmoid(x)


def _tile(n, pref):
    return pref if n % pref == 0 else n


def _row_tile(n, cap):
    best = [t for t in range(16, min(n, cap) + 1, 16) if n % t == 0]
    return best[-1] if best else n


class _Comm:
    def __init__(self, ins, out_shapes, sems, start, wait, aliases=None):
        self.ins, self.out_shapes, self.sems, self.start, self.wait = list(ins), list(out_shapes), list(sems), start, wait
        self.aliases = dict(aliases or {})


def _merge(comms):
    comms = [c for c in comms if c is not None]
    if not comms:
        return None
    bounds, ni, no, ns = [], 0, 0, 0
    for c in comms:
        bounds.append((ni, no, ns))
        ni, no, ns = ni + len(c.ins), no + len(c.out_shapes), ns + len(c.sems)

    def run(which):
        def f(ins, outs, sems):
            for c, (i, o, s) in zip(comms, bounds):
                getattr(c, which)(ins[i:i + len(c.ins)], outs[o:o + len(c.out_shapes)], sems[s:s + len(c.sems)])
        return f

    aliases = {i + a: o + b for c, (i, o, _) in zip(comms, bounds) for a, b in c.aliases.items()}
    return _Comm([a for c in comms for a in c.ins], [a for c in comms for a in c.out_shapes],
                 [a for c in comms for a in c.sems], run("start"), run("wait"), aliases)


def _split_outs(comms, outs):
    res, o = [], 0
    for c in comms:
        if c is not None:
            res.append(list(outs[o:o + len(c.out_shapes)]))
            o += len(c.out_shapes)
    return res


def _pcall(body, args, *, name, out_shape, grid=(), in_specs=None, out_specs=None, scratch=(), comm=None,
           prefetch=()):
    many = isinstance(out_shape, (list, tuple))
    outs = list(out_shape) if many else [out_shape]
    n_pre, n_in, n_out, n_scr = len(prefetch), len(args), len(outs), len(scratch)
    if in_specs is None:
        in_specs, out_specs = [VMEM_SPEC] * n_in, [VMEM_SPEC] * n_out
    else:
        in_specs, out_specs = list(in_specs), (list(out_specs) if many else [out_specs])
    params = pltpu.CompilerParams(dimension_semantics=("arbitrary",) * len(grid), vmem_limit_bytes=VMEM_LIMIT)
    scalars = [jnp.reshape(s, (1,)).astype(jnp.int32) for s in prefetch]
    ci, co = (len(comm.ins), len(comm.out_shapes)) if comm is not None else (0, 0)

    def wrapped(*refs):
        pre, refs = refs[:n_pre], refs[n_pre:]
        a, ca = refs[:n_in], refs[n_in:n_in + ci]
        o = refs[n_in + ci:n_in + ci + n_out]
        cout = refs[n_in + ci + n_out:n_in + ci + n_out + co]
        s = refs[n_in + ci + n_out + co:n_in + ci + n_out + co + n_scr]
        csem = refs[n_in + ci + n_out + co + n_scr:]
        if comm is None:
            body(*pre, *a, *o, *s)
        elif grid:
            first = functools.reduce(operator.and_, [pl.program_id(k) == 0 for k in range(len(grid))])
            last = functools.reduce(operator.and_, [pl.program_id(k) == grid[k] - 1 for k in range(len(grid))])
            pl.when(first)(lambda: comm.start(ca, cout, csem))
            body(*pre, *a, *o, *s)
            pl.when(last)(lambda: comm.wait(ca, cout, csem))
        else:
            comm.start(ca, cout, csem)
            body(*pre, *a, *o, *s)
            comm.wait(ca, cout, csem)

    c_ins, c_outs, c_sems, aliases = ([], [], [], {}) if comm is None else (
        comm.ins, comm.out_shapes, comm.sems, {n_pre + n_in + i: n_out + o for i, o in comm.aliases.items()})
    all_in, all_out = in_specs + [HBM_SPEC] * ci, out_specs + [HBM_SPEC] * co
    all_scr = list(scratch) + c_sems
    if grid:
        args = [pltpu.with_memory_space_constraint(a, pltpu.HBM) for a in args]
    c_ins = [pltpu.with_memory_space_constraint(a, pltpu.HBM) for a in c_ins]
    if n_pre:
        spec = dict(grid_spec=pltpu.PrefetchScalarGridSpec(
            num_scalar_prefetch=n_pre, grid=grid, in_specs=all_in, out_specs=all_out, scratch_shapes=all_scr))
    else:
        spec = dict(grid=grid, in_specs=all_in, out_specs=all_out, scratch_shapes=all_scr)
    res = pl.pallas_call(wrapped, name=name, out_shape=outs + c_outs, input_output_aliases=aliases,
                         compiler_params=params, **spec)(*scalars, *args, *c_ins)
    mine = list(res[:n_out])
    mine = mine if many else mine[0]
    return mine if comm is None else (mine, list(res[n_out:]))


def _peer_chips(x, y):
    return [(1 - x, y), (x, 1 - y), (1 - x, 1 - y)]


def _c_all_gather(bufs):
    n = len(bufs)

    def copies(ins, outs, sems):
        send_sems, recv_sems, fwd_send, fwd_recv = sems
        x, y, c = lax.axis_index("x"), lax.axis_index("y"), lax.axis_index("c")
        me = 2 * x + y
        peers = _peer_chips(x, y)
        chip = [2 * px + py for px, py in peers]

        def ici(g, j, slot):
            return pltpu.make_async_remote_copy(
                src_ref=outs[g].at[me, c], dst_ref=outs[g].at[slot, c], send_sem=send_sems.at[g, j],
                recv_sem=recv_sems.at[g, j], device_id=(*peers[j], c), device_id_type=MESH)

        def d2d(g, j, half):
            return pltpu.make_async_remote_copy(
                src_ref=outs[g].at[chip[j], half], dst_ref=outs[g].at[chip[j], half], send_sem=fwd_send.at[g, j],
                recv_sem=fwd_recv.at[g, j], device_id=(x, y, 1 - c), device_id_type=MESH)

        pairs = [(g, j) for g in range(n) for j in range(3)]
        sends = [ici(g, j, me) for g, j in pairs]
        recvs = [ici(g, j, chip[j]) for g, j in pairs]
        passes = [d2d(g, j, c) for g, j in pairs]
        passed = [d2d(g, j, 1 - c) for g, j in pairs]
        return sends, recvs, passes, passed

    def start(ins, outs, sems):
        for cp in copies(ins, outs, sems)[0]:
            cp.start()

    def wait(ins, outs, sems):
        sends, recvs, passes, passed = copies(ins, outs, sems)
        for rcv, fwd in zip(recvs, passes):
            rcv.wait_recv()
            fwd.start()
        for cp in passed:
            cp.wait_recv()
        for cp in sends + passes:
            cp.wait_send()

    pair_sems = pltpu.SemaphoreType.DMA((n, 3))
    return _Comm(bufs, [jax.ShapeDtypeStruct(s.shape, s.dtype) for s in bufs], [pair_sems] * 4, start, wait,
                 aliases={g: g for g in range(n)})


def _start_wait(copies):
    def start(ins, outs, sems):
        local, sends, _ = copies(ins, outs, sems)
        for cp in local + sends:
            cp.start()

    def wait(ins, outs, sems):
        local, sends, recvs = copies(ins, outs, sems)
        for cp in recvs:
            cp.wait_recv()
        for cp in sends:
            cp.wait_send()
        for cp in local:
            cp.wait()

    return start, wait


def _c_half_swap(grads):
    n = len(grads)

    def copies(ins, outs, sems):
        send_sems, recv_sems = sems
        x, y, c = lax.axis_index("x"), lax.axis_index("y"), lax.axis_index("c")
        sends = []
        for g in range(n):
            half = ins[g].shape[1] // 2
            sends.append(pltpu.make_async_remote_copy(
                src_ref=ins[g].at[:, pl.ds((1 - c) * half, half), :], dst_ref=outs[g],
                send_sem=send_sems.at[g], recv_sem=recv_sems.at[g], device_id=(x, y, 1 - c), device_id_type=MESH))
        return [], sends, sends

    return _Comm(
        grads, [jax.ShapeDtypeStruct((N_CHIPS, s.shape[1] // 2, s.shape[2]), s.dtype) for s in grads],
        [pltpu.SemaphoreType.DMA((n,)), pltpu.SemaphoreType.DMA((n,))], *_start_wait(copies))


def _c_chip_exchange(parts):
    n = len(parts)

    def copies(ins, outs, sems):
        send_sems, recv_sems = sems
        x, y, c = lax.axis_index("x"), lax.axis_index("y"), lax.axis_index("c")
        peers = _peer_chips(x, y)

        def remote(g, j):
            return pltpu.make_async_remote_copy(
                src_ref=ins[g].at[2 * peers[j][0] + peers[j][1]], dst_ref=outs[g].at[j],
                send_sem=send_sems.at[g, j], recv_sem=recv_sems.at[g, j], device_id=(*peers[j], c),
                device_id_type=MESH)

        sends = [remote(g, j) for g in range(n) for j in range(3)]
        return [], sends, sends

    return _Comm(
        parts, [jax.ShapeDtypeStruct((3,) + s.shape[1:], s.dtype) for s in parts],
        [pltpu.SemaphoreType.DMA((n, 3)), pltpu.SemaphoreType.DMA((n, 3))], *_start_wait(copies))


def _c_join(halves):
    n = len(halves)

    def copies(ins, outs, sems):
        send_sems, recv_sems = sems
        x, y, c = lax.axis_index("x"), lax.axis_index("y"), lax.axis_index("c")
        sends = [pltpu.make_async_remote_copy(
            src_ref=ins[g], dst_ref=outs[g], send_sem=send_sems.at[g], recv_sem=recv_sems.at[g],
            device_id=(x, y, 1 - c), device_id_type=MESH) for g in range(n)]
        return [], sends, sends

    return _Comm(
        halves, [jax.ShapeDtypeStruct(s.shape, s.dtype) for s in halves],
        [pltpu.SemaphoreType.DMA((n,)), pltpu.SemaphoreType.DMA((n,))], *_start_wait(copies))


def _exchange_only(comm=None):
    def body(o_ref):
        o_ref[...] = jnp.zeros_like(o_ref)

    return _pcall(body, [], name="exchange_only", out_shape=jax.ShapeDtypeStruct((8, 128), F32), comm=comm)


def _all_reduce_small(v, comm=None):
    rows = v.shape[0]

    def body(v_ref, out_ref, buf, send_sems, recv_sems):
        x, y, c = lax.axis_index("x"), lax.axis_index("y"), lax.axis_index("c")
        me = 4 * x + 2 * y + c
        buf[me] = v_ref[...]
        flips = [(fx, fy, fc) for fx in (0, 1) for fy in (0, 1) for fc in (0, 1)][1:]

        def peer(k):
            fx, fy, fc = flips[k]
            px, py, pc = x ^ fx, y ^ fy, c ^ fc
            return (px, py, pc), 4 * px + 2 * py + pc

        def copy(k, slot):
            return pltpu.make_async_remote_copy(
                src_ref=buf.at[slot], dst_ref=buf.at[slot], send_sem=send_sems.at[k],
                recv_sem=recv_sems.at[k], device_id=peer(k)[0], device_id_type=MESH)

        sends = [copy(k, me) for k in range(7)]
        for cp in sends:
            cp.start()
        for k in range(7):
            copy(k, peer(k)[1]).wait_recv()
        for cp in sends:
            cp.wait_send()
        acc = buf[0]
        for d in range(1, 8):
            acc = acc + buf[d]
        out_ref[...] = acc

    return _pcall(body, [v], name="all_reduce_small", out_shape=jax.ShapeDtypeStruct((rows, 128), F32),
                  scratch=[pltpu.VMEM((8, rows, 128), F32), pltpu.SemaphoreType.DMA((7,)),
                           pltpu.SemaphoreType.DMA((7,))], comm=comm)


def _add_halves(g, got):
    _, h, c = got.shape
    th = _row_tile(h, 512)
    nh = h // th
    half = lax.axis_index("c") * nh

    def body(h_ref, a_ref, b_ref, o_ref):
        o_ref[...] = (a_ref[...].astype(F32) + b_ref[...].astype(F32)).astype(o_ref.dtype)

    spec = pl.BlockSpec((1, th, c), lambda j, i, h_ref: (j, i, 0))
    mine = pl.BlockSpec((1, th, c), lambda j, i, h_ref: (j, h_ref[0] + i, 0))
    return _pcall(body, [g, got], name="add_halves", grid=(N_CHIPS, nh), prefetch=[half],
                  out_shape=jax.ShapeDtypeStruct(got.shape, BF), in_specs=[mine, spec], out_specs=spec)


def _sum_chips(parts, recv):
    _, h, c = parts.shape
    th = _row_tile(h, 512)
    me = 2 * lax.axis_index("x") + lax.axis_index("y")

    def body(me_ref, p_ref, r_ref, o_ref):
        acc = p_ref[0].astype(F32)
        for s in range(N_CHIPS - 1):
            acc = acc + r_ref[s].astype(F32)
        o_ref[...] = acc

    return _pcall(body, [parts, recv], name="sum_chips", grid=(h // th,), prefetch=[me],
                  out_shape=jax.ShapeDtypeStruct((h, c), F32),
                  in_specs=[pl.BlockSpec((1, th, c), lambda i, me_ref: (me_ref[0], i, 0)),
                            pl.BlockSpec((N_CHIPS - 1, th, c), lambda i, me_ref: (0, i, 0))],
                  out_specs=pl.BlockSpec((th, c), lambda i, me_ref: (i, 0)))


def _adam_update(w, gv, m, v, d_ref, nm_ref, nv_ref):
    c1 = 1.0 / (1.0 - ADAM_B1 ** ADAM_STEP)
    c2 = 1.0 / (1.0 - ADAM_B2 ** ADAM_STEP)
    nm = ADAM_B1 * m + (1.0 - ADAM_B1) * gv
    nv = ADAM_B2 * v + (1.0 - ADAM_B2) * (gv * gv)
    nm_ref[...] = nm
    nv_ref[...] = nv
    d_ref[...] = -ADAM_LR * ((nm * c1) / (jnp.sqrt(nv * c2) + ADAM_EPS) + ADAM_WD * w)


def _adamw(w, g, m, v, comm=None):
    r, c = w.shape
    tr = _row_tile(r, 512)

    def body(w_ref, g_ref, m_ref, v_ref, d_ref, nm_ref, nv_ref):
        _adam_update(w_ref[...], g_ref[...], m_ref[...], v_ref[...], d_ref, nm_ref, nv_ref)

    spec = pl.BlockSpec((tr, c), lambda i: (i, 0))
    sds = jax.ShapeDtypeStruct((r, c), F32)
    return _pcall(body, [w, g, m, v], name="adamw", grid=(r // tr,), out_shape=[sds, sds, sds],
                  in_specs=[spec] * 4, out_specs=[spec] * 3, comm=comm)


def _adamw_halves(items, comm=None):
    k = len(items)
    r, c = items[0][0].shape
    h = r // 2
    tr = _row_tile(h, min(512, (VMEM_LIMIT * 3 // 4) // (k * 9 * 2 * 4 * c)))
    nb = h // tr
    core = lax.axis_index("c")

    def body(c_ref, *refs):
        ins, outs = refs[:5 * k], refs[5 * k:]
        for q in range(k):
            w_ref, gm_ref, go_ref, m_ref, v_ref = ins[5 * q:5 * q + 5]
            g_ref, d_ref, nm_ref, nv_ref = outs[4 * q:4 * q + 4]
            gv = jnp.where(pl.program_id(0) == c_ref[0], gm_ref[...], go_ref[...])
            g_ref[...] = gv
            _adam_update(w_ref[...], gv, m_ref[...], v_ref[...], d_ref, nm_ref, nv_ref)

    full = pl.BlockSpec((tr, c), lambda hh, i, c_ref: (hh * nb + i, 0))
    half = pl.BlockSpec((tr, c), lambda hh, i, c_ref: (i, 0))
    sds = jax.ShapeDtypeStruct((r, c), F32)
    return _pcall(body, [a for it in items for a in it], name="adamw_halves", grid=(2, nb), prefetch=[core],
                  out_shape=[sds] * (4 * k), in_specs=[full, half, half, full, full] * k, out_specs=[full] * (4 * k),
                  comm=comm)


def _wgrad(name, a, b, a_spec, b_spec, m, n, nb, comm):
    def body(a_ref, b_ref, o_ref):
        o_ref[...] = _dot_tn(a_ref[...], b_ref[...]).astype(o_ref.dtype)

    return _pcall(body, [a, b], name=name, grid=(nb,), out_shape=jax.ShapeDtypeStruct((nb, m, n), BF),
                  in_specs=[a_spec, b_spec], out_specs=pl.BlockSpec((None, m, n), lambda j: (j, 0, 0)), comm=comm)


def _wgrad_cols(name, a, b, nb, comm=None):
    t_tok, m = a.shape
    n = b.shape[1] // nb
    return _wgrad(name, a, b, pl.BlockSpec((t_tok, m), lambda j: (0, 0)), pl.BlockSpec((t_tok, n), lambda j: (0, j)),
                  m, n, nb, comm)


def _wgrad_rows(name, a, b, nb, comm=None):
    t_tok, n = b.shape
    m = a.shape[1] // nb
    return _wgrad(name, a, b, pl.BlockSpec((t_tok, m), lambda j: (0, j)), pl.BlockSpec((t_tok, n), lambda j: (0, 0)),
                  m, n, nb, comm)


def _wgrad_a_shared(name, a, b4, comm=None):
    t_tok, m = a.shape
    nb, _, n = b4.shape
    return _wgrad(name, a, b4, pl.BlockSpec((t_tok, m), lambda j: (0, 0)),
                  pl.BlockSpec((None, t_tok, n), lambda j: (j, 0, 0)), m, n, nb, comm)


def _wgrad_b_shared(name, a4, b, comm=None):
    nb, t_tok, m = a4.shape
    n = b.shape[1]
    return _wgrad(name, a4, b, pl.BlockSpec((None, t_tok, m), lambda j: (j, 0, 0)),
                  pl.BlockSpec((t_tok, n), lambda j: (0, 0)), m, n, nb, comm)


def _w4_spec(r, c):
    return pl.BlockSpec((None, r, c), lambda i, j: (j, 0, 0))


FFN_ROW_CHUNK = 256


def _row_chunks(tm):
    rc = FFN_ROW_CHUNK if tm % FFN_ROW_CHUNK == 0 else tm
    return [slice(r, r + rc) for r in range(0, tm, rc)]


def _ffn_fwd(h, ln, wg4, wu4, wd4, comm=None):
    t_tok, d = h.shape
    f = wg4.shape[-2]
    tm = _tile(t_tok, 512)

    def body(h_ref, ln_ref, wg_ref, wu_ref, wd_ref, ho_ref, n_ref, g_ref, u_ref, n_s, acc):
        j = pl.program_id(1)

        @pl.when(j == 0)
        def _():
            xv = h_ref[...]
            nv = (xv * _rstd(xv) * ln_ref[...]).astype(BF)
            n_s[...] = nv
            n_ref[...] = nv
            acc[...] = jnp.zeros_like(acc)

        nv = n_s[...]
        g = _dot_nt(nv, wg_ref[...])
        u = _dot_nt(nv, wu_ref[...])
        g_ref[...] = g.astype(BF)
        u_ref[...] = u.astype(BF)
        a = (g * _sigmoid(g) * u).astype(BF)
        acc[...] += _dot(a, wd_ref[...])

        @pl.when(j == N_CHIPS - 1)
        def _():
            ho_ref[...] = h_ref[...] + 0.5 * acc[...]

    row = pl.BlockSpec((tm, d), lambda i, j: (i, 0))
    gu = pl.BlockSpec((None, tm, f), lambda i, j: (j, i, 0))
    gu_sds = jax.ShapeDtypeStruct((N_CHIPS, t_tok, f), BF)
    return _pcall(
        body, [h, ln, wg4, wu4, wd4], name="ffn_fwd", grid=(t_tok // tm, N_CHIPS),
        out_shape=[jax.ShapeDtypeStruct((t_tok, d), F32), jax.ShapeDtypeStruct((t_tok, d), BF), gu_sds, gu_sds],
        in_specs=[row, pl.BlockSpec((1, d), lambda i, j: (0, 0)), _w4_spec(f, d), _w4_spec(f, d), _w4_spec(f, d)],
        out_specs=[row, row, gu, gu],
        scratch=[pltpu.VMEM((tm, d), BF), pltpu.VMEM((tm, d), F32)], comm=comm)


def _ffn_bwd(dho, h, ln, g4, u4, wg4, wu4, wd4, comm=None):
    t_tok, d = h.shape
    f = wg4.shape[-2]
    tm = _tile(t_tok, 512)

    def body(dho_ref, h_ref, ln_ref, g_ref, u_ref, wg_ref, wu_ref, wd_ref,
             dhi_ref, dln_ref, dg_ref, du_ref, a_ref, dhb_ref, dhb_s, dn_acc):
        i, j = pl.program_id(0), pl.program_id(1)

        @pl.when(j == 0)
        def _():
            dhb = (0.5 * dho_ref[...]).astype(BF)
            dhb_s[...] = dhb
            dhb_ref[...] = dhb
            dn_acc[...] = jnp.zeros_like(dn_acc)

        @pl.when((i == 0) & (j == 0))
        def _():
            dln_ref[...] = jnp.zeros_like(dln_ref)

        for rows in _row_chunks(tm):
            g = g_ref[rows, :].astype(F32)
            u = u_ref[rows, :].astype(F32)
            s = _sigmoid(g)
            sg = g * s
            a_ref[rows, :] = (sg * u).astype(BF)
            da = _dot_nt(dhb_s[rows, :], wd_ref[...])
            dg = (da * u * (s * (1.0 + g * (1.0 - s)))).astype(BF)
            du = (da * sg).astype(BF)
            dg_ref[rows, :] = dg
            du_ref[rows, :] = du
            dn_acc[rows, :] += _dot(dg, wg_ref[...]) + _dot(du, wu_ref[...])

        @pl.when(j == N_CHIPS - 1)
        def _():
            xv = h_ref[...]
            dx, dln = _rms_bwd(dn_acc[...], xv, _rstd(xv), ln_ref[...])
            dln_ref[...] += dln
            dhi_ref[...] = dho_ref[...] + dx

    row = pl.BlockSpec((tm, d), lambda i, j: (i, 0))
    vec = pl.BlockSpec((1, d), lambda i, j: (0, 0))
    gu = pl.BlockSpec((None, tm, f), lambda i, j: (j, i, 0))
    gu_sds = jax.ShapeDtypeStruct((N_CHIPS, t_tok, f), BF)
    return _pcall(
        body, [dho, h, ln, g4, u4, wg4, wu4, wd4], name="ffn_bwd", grid=(t_tok // tm, N_CHIPS),
        out_shape=[jax.ShapeDtypeStruct((t_tok, d), F32), jax.ShapeDtypeStruct((1, d), F32),
                   gu_sds, gu_sds, gu_sds, jax.ShapeDtypeStruct((t_tok, d), BF)],
        in_specs=[row, row, vec, gu, gu, _w4_spec(f, d), _w4_spec(f, d), _w4_spec(f, d)],
        out_specs=[row, vec, gu, gu, gu, row],
        scratch=[pltpu.VMEM((tm, d), BF), pltpu.VMEM((tm, d), F32)], comm=comm)


def _rope_tables(pos_col, inv_freq2, comm=None):
    t_tok = pos_col.shape[0]

    def body(p_ref, f_ref, cos_ref, sin_ref):
        ang = p_ref[...] * f_ref[...]
        lane = lax.broadcasted_iota(jnp.int32, ang.shape, 1)
        s = jnp.sin(ang)
        cos_ref[...] = jnp.cos(ang)
        sin_ref[...] = jnp.where((lane & 1) == 0, -s, s)

    sds = jax.ShapeDtypeStruct((t_tok, 128), F32)
    return _pcall(body, [pos_col, inv_freq2], name="rope_tables", out_shape=[sds, sds], comm=comm)


def _swap_pairs(x):
    lane = lax.broadcasted_iota(jnp.int32, x.shape, 1)
    return jnp.where((lane & 1) == 0, pltpu.roll(x, 127, 1), pltpu.roll(x, 1, 1))


def _mix_in(h, ln, w_in, wm4, b_m, cos_t, sin_t, comm=None):
    t_tok, d = h.shape
    cm = wm4.shape[-1]
    tm = _tile(t_tok, 256)

    def body(h_ref, ln_ref, win_ref, wm_ref, bm_ref, cos_ref, sin_ref,
             u_ref, rq_ref, rk_ref, rv_ref, rg_ref, fq_ref, fk_ref, fv_ref, ff_ref, ga_ref, gb_ref):
        xv = h_ref[...]
        ub = (xv * _rstd(xv) * ln_ref[...]).astype(BF)
        u_ref[...] = ub
        cosv, sinv = cos_ref[...], sin_ref[...]

        def sec(k):
            return _dot_nt(ub, win_ref[k * 512:(k + 1) * 512, :])

        def rot(xh):
            return xh * cosv + _swap_pairs(xh) * sinv

        pq, pk = sec(0), sec(1)
        for hh in range(RET_HEADS):
            sl = slice(hh * RET_DIM, (hh + 1) * RET_DIM)
            rq_ref[:, sl] = rot(pq[:, sl]).astype(BF)
            rk_ref[:, sl] = (rot(pk[:, sl]) * RET_SCALE).astype(BF)
        rv_ref[...] = sec(2).astype(BF)
        rg_ref[...] = sec(3).astype(BF)
        fq_ref[...] = (sec(4) * FOX_SCALE).astype(BF)
        fk_ref[...] = sec(5).astype(BF)
        fv_ref[...] = sec(6).astype(BF)
        ff_ref[...] = _dot_nt(ub, win_ref[FF_COL:FF_COL + 128, :])
        for j in range(N_CHIPS):
            gs = _sigmoid(_dot(ub, wm_ref[j]) + bm_ref[:, j * cm:(j + 1) * cm]).astype(BF)
            col = j * cm
            if col < d:
                ga_ref[:, col:col + cm] = gs
            else:
                gb_ref[:, col - d:col - d + cm] = gs

    row = lambda c: pl.BlockSpec((tm, c), lambda i: (i, 0))
    full = lambda *s: pl.BlockSpec(s, lambda i: (0,) * len(s))
    sds = lambda c, dt: jax.ShapeDtypeStruct((t_tok, c), dt)
    return _pcall(
        body, [h, ln, w_in, wm4, b_m, cos_t, sin_t], name="mix_in", grid=(t_tok // tm,),
        out_shape=[sds(d, BF)] + [sds(512, BF)] * 7 + [sds(128, F32), sds(d, BF), sds(d, BF)],
        in_specs=[row(d), full(1, d), full(IN_PAD, d), full(N_CHIPS, d, cm), full(1, 2 * d), row(128), row(128)],
        out_specs=[row(d)] + [row(512)] * 7 + [row(128), row(d), row(d)], comm=comm)


def _split3(x):
    hi = x.astype(BF)
    r1 = x - hi.astype(F32)
    mid = r1.astype(BF)
    lo = (r1 - mid.astype(F32)).astype(BF)
    return hi, mid, lo


def _aug_lane():
    return lax.broadcasted_iota(jnp.int32, (1, 128), 1) & (FOX_DIM - 1)


def _aug_put(base, k0, parts):
    w = _aug_lane()
    for i, part in enumerate(parts):
        base = jnp.where(w == k0 + i, part, base)
    return base


def _forget_fwd(ffl, b_pad):
    t_tok = ffl.shape[0]
    tb = _tile(t_tok, 256)

    def body(ff_ref, b_ref, aq_ref, ak_ref, cum_s):
        r = lax.broadcasted_iota(jnp.int32, (tb, tb), 0)
        c = lax.broadcasted_iota(jnp.int32, (tb, tb), 1)
        tri = jnp.where(c <= r, 1.0, 0.0).astype(BF)
        carry = jnp.zeros((1, 128), F32)
        for i in range(t_tok // tb):
            z = ff_ref[i * tb:(i + 1) * tb, :] + b_ref[...]
            lf = jnp.minimum(z, 0.0) - jnp.log(1.0 + jnp.exp(-jnp.abs(z)))
            hi, mid, lo = _split3(lf)
            cs = _dot(tri, hi) + _dot(tri, mid) + _dot(tri, lo) + carry
            cum_s[i * tb:(i + 1) * tb, :] = cs
            carry = cs[tb - 1:tb, :]
        x = cum_s[...]
        first = lax.broadcasted_iota(jnp.int32, (1, 128), 1) < FOX_DIM
        w = _aug_lane()
        one = jnp.ones((t_tok, 128), BF)
        zero = jnp.zeros((t_tok, 128), BF)
        for pp in range(FOX_HEADS // 2):
            other = jnp.where(first, x[:, 2 * pp + 1:2 * pp + 2], x[:, 2 * pp:2 * pp + 1])
            parts = _split3(other)
            aq = jnp.where((w >= 3) & (w < 6), one, zero)
            ak = jnp.where((w < 3) | ((w >= 6) & (w < 9)), one, zero)
            aq_ref[:, pp * 128:(pp + 1) * 128] = _aug_put(aq, 0, parts)
            ak_ref[:, pp * 128:(pp + 1) * 128] = _aug_put(ak, 3, [-q for q in parts])

    sds = jax.ShapeDtypeStruct((t_tok, FOX_WIDTH), BF)
    return _pcall(body, [ffl, b_pad], name="forget_fwd", out_shape=[sds, sds],
                  scratch=[pltpu.VMEM((t_tok, 128), F32)])


def _forget_bwd(dcum_t, dcum_q, ffl, b_pad):
    t_tok = ffl.shape[0]
    tb = _tile(t_tok, 256)

    def body(dc_ref, dq_ref, ff_ref, b_ref, dff_ref, db_ref, pad_s, d_s):
        pad_s[...] = jnp.zeros_like(pad_s)
        pad_s[0:FOX_HEADS, :] = dc_ref[...]
        dsum = pad_s[...].T
        lane = lax.broadcasted_iota(jnp.int32, (t_tok, 128), 1)
        for hh in range(FOX_HEADS):
            dsum = dsum + jnp.where(lane == hh, dq_ref[:, hh * FOX_DIM:hh * FOX_DIM + 1], 0.0)
        d_s[...] = dsum
        r = lax.broadcasted_iota(jnp.int32, (tb, tb), 0)
        c = lax.broadcasted_iota(jnp.int32, (tb, tb), 1)
        tri = jnp.where(c >= r, 1.0, 0.0).astype(BF)
        carry = jnp.zeros((1, 128), F32)
        db = jnp.zeros((1, 128), F32)
        for i in reversed(range(t_tok // tb)):
            hi, mid, lo = _split3(d_s[i * tb:(i + 1) * tb, :])
            dlf = _dot(tri, hi) + _dot(tri, mid) + _dot(tri, lo) + carry
            carry = dlf[0:1, :]
            z = ff_ref[i * tb:(i + 1) * tb, :] + b_ref[...]
            dff = dlf * _sigmoid(-z)
            dff_ref[i * tb:(i + 1) * tb, :] = dff.astype(BF)
            db = db + jnp.sum(dff, axis=0, keepdims=True)
        db_ref[...] = db

    return _pcall(
        body, [dcum_t, dcum_q, ffl, b_pad], name="forget_bwd",
        out_shape=[jax.ShapeDtypeStruct((t_tok, 128), BF), jax.ShapeDtypeStruct((1, 128), F32)],
        scratch=[pltpu.VMEM((128, t_tok), F32), pltpu.VMEM((t_tok, 128), F32)])


def _first_half():
    return lax.broadcasted_iota(jnp.int32, (1, 128), 1) < FOX_DIM


def _head_rows(x2, a2, hh):
    return jnp.where(_first_half(), x2, a2) if hh == 0 else jnp.where(_first_half(), a2, x2)


def _head_only(x2, hh):
    zero = jnp.zeros_like(x2)
    return jnp.where(_first_half(), x2, zero) if hh == 0 else jnp.where(_first_half(), zero, x2)


def _causal_diag(s):
    rows = lax.broadcasted_iota(jnp.int32, s.shape, 0)
    cols = lax.broadcasted_iota(jnp.int32, s.shape, 1)
    return jnp.where(cols <= rows, s, NEG)


def _diag_or_below(qi, ki, step):
    pl.when(ki < qi)(lambda: step(False))
    pl.when(ki == qi)(lambda: step(True))


def _tri_rows(s, n):
    qi = sum((s >= r * (r + 1) // 2).astype(jnp.int32) for r in range(1, n))
    return qi, s - (qi * (qi + 1)) // 2


def _tri_cols(s, n):
    ki = sum((s >= k * n - k * (k - 1) // 2).astype(jnp.int32) for k in range(1, n))
    return ki, ki + s - (ki * n - (ki * (ki - 1)) // 2)


def _fox_fwd(fq, fk, fv, aq, ak, comm=None):
    t_tok = fq.shape[0]
    t = _tile(t_tok, 512)
    nq = t_tok // t
    npair = FOX_HEADS // 2

    def body(q_ref, k_ref, v_ref, aq_ref, ak_ref, o_ref, of_ref, aqb_ref, m_s, l_s, acc_s):
        qi, ki = _tri_rows(pl.program_id(1), nq)

        @pl.when(ki == 0)
        def _():
            m_s[...] = jnp.full_like(m_s, NEG)
            l_s[...] = jnp.zeros_like(l_s)
            acc_s[...] = jnp.zeros_like(acc_s)

        def step(diag):
            q2, k2, v2, aq2, ak2 = q_ref[...], k_ref[...], v_ref[...], aq_ref[...], ak_ref[...]
            for hh in range(2):
                s = _dot_nt(_head_rows(q2, aq2, hh), _head_rows(k2, ak2, hh))
                if diag:
                    s = _causal_diag(s)
                m_prev = m_s[hh]
                m_new = jnp.maximum(m_prev, jnp.max(s, axis=1, keepdims=True))
                alpha = jnp.exp(m_prev - m_new)
                p = jnp.exp(s - jnp.tile(m_new, (1, t // 128)))
                l_s[hh] = alpha * l_s[hh] + jnp.sum(p, axis=1, keepdims=True)
                acc_s[hh] = alpha * acc_s[hh] + _dot(p.astype(BF), v2)
                m_s[hh] = m_new

        _diag_or_below(qi, ki, step)

        @pl.when(ki == qi)
        def _():
            first = _first_half()
            o = jnp.where(first, acc_s[0] / l_s[0], acc_s[1] / l_s[1])
            o_ref[...] = o.astype(BF)
            of_ref[...] = o
            other = jnp.where(first, m_s[1] + jnp.log(l_s[1]), m_s[0] + jnp.log(l_s[0]))
            aqb_ref[...] = _aug_put(aq_ref[...], 6, _split3(-other))

    qs = pl.BlockSpec((t, 128), lambda p, s: (_tri_rows(s, nq)[0], p))
    ks = pl.BlockSpec((t, 128), lambda p, s: (_tri_rows(s, nq)[1], p))
    stat = pltpu.VMEM((2, t, 128), F32)
    return _pcall(
        body, [fq, fk, fv, aq, ak], name="fox_fwd", grid=(npair, nq * (nq + 1) // 2),
        out_shape=[jax.ShapeDtypeStruct((t_tok, FOX_WIDTH), BF), jax.ShapeDtypeStruct((t_tok, FOX_WIDTH), F32),
                   jax.ShapeDtypeStruct((t_tok, FOX_WIDTH), BF)],
        in_specs=[qs, ks, ks, qs, ks], out_specs=[qs, qs, qs], scratch=[stat, stat, stat], comm=comm)


def _fox_ds(q2, k2, v2, do2, aq2, ak2, ad2, hh, diag):
    s = _dot_nt(_head_rows(q2, aq2, hh), _head_rows(k2, ak2, hh))
    if diag:
        s = _causal_diag(s)
    p = jnp.exp(s)
    av = jnp.where(_aug_lane() < 3, 1.0, 0.0).astype(BF)
    dp = _dot_nt(_head_rows(do2, ad2, hh), _head_rows(v2, jnp.broadcast_to(av, v2.shape), hh))
    return p, p * dp


def _fox_bwd(fq, fk, fv, do, aqb, ak, ad, comm=None):
    t_tok = fq.shape[0]
    t = _tile(t_tok, 512)
    nq = t_tok // t
    npair = FOX_HEADS // 2
    n_steps = nq * (nq + 1) // 2

    def body(q_ref, k_ref, v_ref, do_ref, aq_ref, ak_ref, ad_ref, dq_ref, dk_ref, dv_ref, dck_ref, dcq_ref,
             dk_s, dv_s, dq_s, rs_s):
        step_id = pl.program_id(1)
        ki, qi = _tri_cols(step_id, nq)

        @pl.when(step_id == 0)
        def _():
            dq_s[...] = jnp.zeros_like(dq_s)
            rs_s[...] = jnp.zeros_like(rs_s)

        @pl.when(qi == ki)
        def _():
            dk_s[...] = jnp.zeros_like(dk_s)
            dv_s[...] = jnp.zeros_like(dv_s)
            dck_ref[...] = jnp.zeros_like(dck_ref)

        rows = pl.ds(qi * t if isinstance(qi, int) else pl.multiple_of(qi * t, t), t)

        def step(diag):
            q2, k2, v2, do2 = q_ref[...], k_ref[...], v_ref[...], do_ref[...]
            dq = []
            for hh in range(2):
                p, ds = _fox_ds(q2, k2, v2, do2, aq_ref[...], ak_ref[...], ad_ref[...], hh, diag)
                dsb = ds.astype(BF)
                dv_s[...] += _dot_tn(p.astype(BF), _head_only(do2, hh))
                dk_s[...] += _dot_tn(dsb, _head_only(q2, hh))
                dq.append(_dot(dsb, k2))
                dck_ref[hh] = dck_ref[hh] - jnp.sum(ds, axis=0, keepdims=True)
                rs_s[hh, rows, :] = rs_s[hh, rows, :] + jnp.sum(ds, axis=1, keepdims=True)
            dq_s[rows, :] = dq_s[rows, :] + jnp.where(_first_half(), dq[0], dq[1])

        _diag_or_below(qi, ki, step)

        @pl.when(qi == nq - 1)
        def _():
            dk_ref[...] = dk_s[...].astype(BF)
            dv_ref[...] = dv_s[...].astype(BF)

        @pl.when(step_id == n_steps - 1)
        def _():
            dq_ref[...] = (dq_s[...] * FOX_SCALE).astype(BF)
            dcq_ref[...] = jnp.where(_first_half(), rs_s[0], rs_s[1])

    qs = pl.BlockSpec((t, 128), lambda p, s: (_tri_cols(s, nq)[1], p))
    ks = pl.BlockSpec((t, 128), lambda p, s: (_tri_cols(s, nq)[0], p))
    cks = pl.BlockSpec((2, 1, t), lambda p, s: (p, 0, _tri_cols(s, nq)[0]))
    seq = pl.BlockSpec((t_tok, 128), lambda p, s: (0, p))
    sds = jax.ShapeDtypeStruct((t_tok, FOX_WIDTH), BF)
    return _pcall(
        body, [fq, fk, fv, do, aqb, ak, ad], name="fox_bwd", grid=(npair, n_steps),
        out_shape=[sds, sds, sds, jax.ShapeDtypeStruct((FOX_HEADS, 1, t_tok), F32),
                   jax.ShapeDtypeStruct((t_tok, FOX_WIDTH), F32)],
        in_specs=[qs, ks, ks, qs, qs, ks, qs], out_specs=[seq, ks, ks, cks, seq],
        scratch=[pltpu.VMEM((t, 128), F32), pltpu.VMEM((t, 128), F32), pltpu.VMEM((t_tok, 128), F32),
                 pltpu.VMEM((2, t_tok, 128), F32)], comm=comm)


def _ret_consts():
    c = RET_CHUNK
    log_gamma = jnp.log1p(-jnp.exp2(-5.0 - jnp.arange(RET_HEADS, dtype=F32)))
    idx = jnp.arange(c, dtype=F32)
    diff = idx[:, None] - idx[None, :]
    dmask = jnp.where(diff >= 0, jnp.exp(log_gamma[:, None, None] * jnp.maximum(diff, 0.0)), 0.0)
    qdec = jnp.exp(log_gamma[:, None] * (idx + 1.0))
    kdec = jnp.exp(log_gamma[:, None] * (c - 1 - idx))
    cdec = jnp.exp(log_gamma * c)
    bc = lambda v: jnp.broadcast_to(v[:, :, None], (RET_HEADS, c, RET_DIM))
    return dmask, bc(qdec), bc(kdec), jnp.broadcast_to(cdec[:, None, None], (RET_HEADS, c, RET_DIM))


def _group_norm(y):
    mu = jnp.mean(y, axis=-1, keepdims=True)
    yc = y - mu
    r = lax.rsqrt(jnp.mean(yc * yc, axis=-1, keepdims=True) + EPS)
    return yc * r, r


def _ret_fwd(rq, rk, rv, rg, consts, comm=None):
    t_tok = rq.shape[0]
    nb = 4 if t_tok % (4 * RET_CHUNK) == 0 else 1
    tr = nb * RET_CHUNK
    n_steps = t_tok // tr
    c = RET_CHUNK

    def body(q_ref, k_ref, v_ref, g_ref, dm_ref, qd_ref, kd_ref, cd_ref, y_ref, yo_ref, st_ref, s_s):
        @pl.when(pl.program_id(0) == 0)
        def _():
            s_s[...] = jnp.zeros_like(s_s)

        for b in range(nb):
            rows = slice(b * c, (b + 1) * c)
            for hh in range(RET_HEADS):
                cols = slice(hh * RET_DIM, (hh + 1) * RET_DIM)
                q, k, v = q_ref[rows, cols], k_ref[rows, cols], v_ref[rows, cols]
                state = s_s[hh]
                st_ref[hh, b] = state
                sc = (_dot_nt(q, k) * dm_ref[hh]).astype(BF)
                y = _dot(sc, v) + _dot((q.astype(F32) * qd_ref[hh]).astype(BF), state.astype(BF))
                s_s[hh] = cd_ref[hh] * state + _dot_tn((k.astype(F32) * kd_ref[hh]).astype(BF), v)
                y_ref[rows, cols] = y
                yn, _ = _group_norm(y)
                gate = g_ref[rows, cols].astype(F32)
                yo_ref[rows, cols] = (yn * (gate * _sigmoid(gate))).astype(BF)

    blk = pl.BlockSpec((tr, RET_WIDTH), lambda i: (i, 0))
    cst = pl.BlockSpec((RET_HEADS, c, RET_DIM), lambda i: (0, 0, 0))
    return _pcall(
        body, [rq, rk, rv, rg, *consts], name="ret_fwd", grid=(n_steps,),
        out_shape=[jax.ShapeDtypeStruct((t_tok, RET_WIDTH), F32), jax.ShapeDtypeStruct((t_tok, RET_WIDTH), BF),
                   jax.ShapeDtypeStruct((RET_HEADS, t_tok // c, RET_DIM, RET_DIM), F32)],
        in_specs=[blk] * 4 + [cst] * 4,
        out_specs=[blk, blk, pl.BlockSpec((RET_HEADS, nb, RET_DIM, RET_DIM), lambda i: (0, i, 0, 0))],
        scratch=[pltpu.VMEM((RET_HEADS, RET_DIM, RET_DIM), F32)], comm=comm)


def _ret_bwd(rq, rk, rv, rg, y_raw, dyo, states, consts, cos_t, sin_t, comm=None):
    t_tok = rq.shape[0]
    nb = 4 if t_tok % (4 * RET_CHUNK) == 0 else 1
    tr = nb * RET_CHUNK
    n_steps = t_tok // tr
    c = RET_CHUNK

    def body(q_ref, k_ref, v_ref, g_ref, y_ref, dyo_ref, st_ref, dm_ref, qd_ref, kd_ref, cd_ref,
             cos_ref, sin_ref, dq_ref, dk_ref, dv_ref, dg_ref, ds_s):
        @pl.when(pl.program_id(0) == 0)
        def _():
            ds_s[...] = jnp.zeros_like(ds_s)

        for b in reversed(range(nb)):
            rows = slice(b * c, (b + 1) * c)
            cosv, sinv = cos_ref[rows, :], sin_ref[rows, :]
            for hh in range(RET_HEADS):
                cols = slice(hh * RET_DIM, (hh + 1) * RET_DIM)
                dm, qd, kd, cd = dm_ref[hh], qd_ref[hh], kd_ref[hh], cd_ref[hh]
                q, k, v = q_ref[rows, cols], k_ref[rows, cols], v_ref[rows, cols]
                yn, r = _group_norm(y_ref[rows, cols])
                gate = g_ref[rows, cols].astype(F32)
                sg = _sigmoid(gate)
                dyo = dyo_ref[rows, cols]
                dg_ref[rows, cols] = (dyo * yn * (sg * (1.0 + gate * (1.0 - sg)))).astype(BF)
                dyn = dyo * (gate * sg)
                dy = r * (dyn - jnp.mean(dyn, axis=-1, keepdims=True)
                          - yn * jnp.mean(dyn * yn, axis=-1, keepdims=True))
                dyb = dy.astype(BF)
                state_b = st_ref[hh, b].astype(BF)
                dstate = ds_s[hh]
                dstate_b = dstate.astype(BF)
                qdb = (q.astype(F32) * qd).astype(BF)
                kdb = (k.astype(F32) * kd).astype(BF)
                sc = (_dot_nt(q, k) * dm).astype(BF)
                dv = _dot_tn(sc, dyb) + _dot(kdb, dstate_b)
                dp = (_dot_nt(dyb, v) * dm).astype(BF)
                dq = _dot(dp, k) + _dot_nt(dyb, state_b) * qd
                dk = (_dot_tn(dp, q) + _dot_nt(v, dstate_b) * kd) * RET_SCALE
                ds_s[hh] = cd * dstate + _dot_tn(qdb, dyb)
                dv_ref[rows, cols] = dv.astype(BF)
                dq_ref[rows, cols] = (dq * cosv - _swap_pairs(dq) * sinv).astype(BF)
                dk_ref[rows, cols] = (dk * cosv - _swap_pairs(dk) * sinv).astype(BF)

    rev = lambda i: n_steps - 1 - i
    blk = pl.BlockSpec((tr, RET_WIDTH), lambda i: (rev(i), 0))
    tab = pl.BlockSpec((tr, RET_DIM), lambda i: (rev(i), 0))
    cst = pl.BlockSpec((RET_HEADS, c, RET_DIM), lambda i: (0, 0, 0))
    sds = jax.ShapeDtypeStruct((t_tok, RET_WIDTH), BF)
    return _pcall(
        body, [rq, rk, rv, rg, y_raw, dyo, states, *consts, cos_t, sin_t], name="ret_bwd",
        grid=(n_steps,), out_shape=[sds] * 4,
        in_specs=[blk] * 6 + [pl.BlockSpec((RET_HEADS, nb, RET_DIM, RET_DIM), lambda i: (0, rev(i), 0, 0))]
        + [cst] * 4 + [tab, tab],
        out_specs=[blk] * 4, scratch=[pltpu.VMEM((RET_HEADS, RET_DIM, RET_DIM), F32)], comm=comm)


def _mix_out(h, y_ret, y_fox, ga, gb, wr4, wf4, wo4, comm=None):
    t_tok, d = h.shape
    cz = wr4.shape[-1]
    ro = wo4.shape[-2]
    tm = _tile(t_tok, 512)

    def body(h_ref, yr_ref, yf_ref, ga_ref, gb_ref, wr_ref, wf_ref, wo_ref, ho_ref, za_ref, zb_ref, mix_ref):
        yr, yf = yr_ref[...], yf_ref[...]
        for j in range(N_CHIPS):
            sl = slice(j * cz, (j + 1) * cz)
            za = _dot(yr, wr_ref[j])
            zb = _dot(yf, wf_ref[j])
            za_ref[:, sl] = za.astype(BF)
            zb_ref[:, sl] = zb.astype(BF)
            mix_ref[:, sl] = (ga_ref[:, sl].astype(F32) * za + gb_ref[:, sl].astype(F32) * zb).astype(BF)
        acc = h_ref[...]
        for j in range(N_CHIPS):
            acc = acc + _dot(mix_ref[:, j * ro:(j + 1) * ro], wo_ref[j])
        ho_ref[...] = acc

    row = lambda c: pl.BlockSpec((tm, c), lambda i: (i, 0))
    full = lambda *s: pl.BlockSpec(s, lambda i: (0,) * len(s))
    sds = lambda dt: jax.ShapeDtypeStruct((t_tok, d), dt)
    return _pcall(
        body, [h, y_ret, y_fox, ga, gb, wr4, wf4, wo4], name="mix_out", grid=(t_tok // tm,),
        out_shape=[sds(F32), sds(BF), sds(BF), sds(BF)],
        in_specs=[row(d), row(RET_WIDTH), row(FOX_WIDTH), row(d), row(d),
                  full(N_CHIPS, RET_WIDTH, cz), full(N_CHIPS, FOX_WIDTH, cz), full(N_CHIPS, ro, d)],
        out_specs=[row(d)] * 4, comm=comm)


def _mix_out_bwd(dh, za, zb, ga, gb, y_fox, wr4, wf4, wo4, comm=None):
    t_tok, d = dh.shape
    cz = wr4.shape[-1]
    ro = wo4.shape[-2]
    tm = _tile(t_tok, 256)

    def body(dh_ref, za_ref, zb_ref, ga_ref, gb_ref, yf_ref, wr_ref, wf_ref, wo_ref,
             dhb_ref, dgp_ref, dza_ref, dzb_ref, dyr_ref, dyf_ref, dl_ref, db_ref):
        @pl.when(pl.program_id(0) == 0)
        def _():
            db_ref[...] = jnp.zeros_like(db_ref)

        dhb = dh_ref[...].astype(BF)
        dhb_ref[...] = dhb
        dyr = jnp.zeros((tm, RET_WIDTH), F32)
        dyf = jnp.zeros((tm, FOX_WIDTH), F32)
        for j in range(N_CHIPS):
            sl = slice(j * ro, (j + 1) * ro)
            dmix = _dot_nt(dhb, wo_ref[j])
            ga, gb = ga_ref[:, sl].astype(F32), gb_ref[:, sl].astype(F32)
            dza = (dmix * ga).astype(BF)
            dzb = (dmix * gb).astype(BF)
            dza_ref[:, sl] = dza
            dzb_ref[:, sl] = dzb
            dga = dmix * za_ref[:, sl].astype(F32) * ga * (1.0 - ga)
            dgb = dmix * zb_ref[:, sl].astype(F32) * gb * (1.0 - gb)
            dgp_ref[:, sl] = dga.astype(BF)
            dgp_ref[:, d + j * ro:d + (j + 1) * ro] = dgb.astype(BF)
            db_ref[:, sl] += jnp.sum(dga, axis=0, keepdims=True)
            db_ref[:, d + j * ro:d + (j + 1) * ro] += jnp.sum(dgb, axis=0, keepdims=True)
        for j in range(N_CHIPS):
            sl = slice(j * cz, (j + 1) * cz)
            dyr = dyr + _dot_nt(dza_ref[:, sl], wr_ref[j])
            dyf = dyf + _dot_nt(dzb_ref[:, sl], wf_ref[j])
        dyr_ref[...] = dyr
        dyfb = dyf.astype(BF)
        dyf_ref[...] = dyfb
        prod = dyfb.astype(F32) * yf_ref[...]
        first = _first_half()
        for pp in range(FOX_HEADS // 2):
            blk = prod[:, pp * 128:(pp + 1) * 128]
            s0 = jnp.sum(jnp.where(first, blk, 0.0), axis=1, keepdims=True)
            s1 = jnp.sum(jnp.where(first, 0.0, blk), axis=1, keepdims=True)
            parts = _split3(-jnp.where(first, s1, s0))
            dl_ref[:, pp * 128:(pp + 1) * 128] = _aug_put(jnp.zeros((tm, 128), BF), 0, parts)

    row = lambda c: pl.BlockSpec((tm, c), lambda i: (i, 0))
    full = lambda *s: pl.BlockSpec(s, lambda i: (0,) * len(s))
    sds = lambda c, dt: jax.ShapeDtypeStruct((t_tok, c), dt)
    return _pcall(
        body, [dh, za, zb, ga, gb, y_fox, wr4, wf4, wo4], name="mix_out_bwd", grid=(t_tok // tm,),
        out_shape=[sds(d, BF), sds(2 * d, BF), sds(d, BF), sds(d, BF), sds(RET_WIDTH, F32),
                   sds(FOX_WIDTH, BF), sds(FOX_WIDTH, BF), jax.ShapeDtypeStruct((1, 2 * d), F32)],
        in_specs=[row(d)] * 5 + [row(FOX_WIDTH), full(N_CHIPS, RET_WIDTH, cz), full(N_CHIPS, FOX_WIDTH, cz),
                                 full(N_CHIPS, ro, d)],
        out_specs=[row(d), row(2 * d), row(d), row(d), row(RET_WIDTH), row(FOX_WIDTH), row(FOX_WIDTH),
                   full(1, 2 * d)],
        comm=comm)


def _mix_in_bwd(dh, h, ln, parts, dff, dgpre, w_in, wm4, comm=None):
    t_tok, d = h.shape
    cm = wm4.shape[-1]
    tm = _tile(t_tok, 256)

    def body(dh_ref, h_ref, ln_ref, p0, p1, p2, p3, p4, p5, p6, dff_ref, dgp_ref, win_ref, wm_ref,
             dhi_ref, dln_ref, dproj_ref):
        @pl.when(pl.program_id(0) == 0)
        def _():
            dln_ref[...] = jnp.zeros_like(dln_ref)

        for k, pr in enumerate((p0, p1, p2, p3, p4, p5, p6)):
            dproj_ref[:, k * 512:(k + 1) * 512] = pr[...]
        dproj_ref[:, FF_COL:FF_COL + 128] = dff_ref[...]
        dproj_ref[:, FF_COL + 128:] = jnp.zeros((tm, IN_PAD - FF_COL - 128), BF)
        du = _dot(dproj_ref[...], win_ref[...])
        for j in range(N_CHIPS):
            du = du + _dot_nt(dgp_ref[:, j * cm:(j + 1) * cm], wm_ref[j])
        xv = h_ref[...]
        dx, dln = _rms_bwd(du, xv, _rstd(xv), ln_ref[...])
        dln_ref[...] += dln
        dhi_ref[...] = dh_ref[...] + dx

    row = lambda c: pl.BlockSpec((tm, c), lambda i: (i, 0))
    full = lambda *s: pl.BlockSpec(s, lambda i: (0,) * len(s))
    return _pcall(
        body, [dh, h, ln, *parts, dff, dgpre, w_in, wm4], name="mix_in_bwd", grid=(t_tok // tm,),
        out_shape=[jax.ShapeDtypeStruct((t_tok, d), F32), jax.ShapeDtypeStruct((1, d), F32),
                   jax.ShapeDtypeStruct((t_tok, IN_PAD), BF)],
        in_specs=[row(d), row(d), full(1, d)] + [row(512)] * 7 + [row(128), row(2 * d), full(IN_PAD, d),
                                                                   full(N_CHIPS, d, cm)],
        out_specs=[row(d), full(1, d), row(IN_PAD)], comm=comm)


def _tail(h, p, target, ln_ple, ln_fin, wpg4, wpl4, comm=None):
    t_tok, d = h.shape
    pd = p.shape[1]
    rg = wpg4.shape[-2]
    cp = wpl4.shape[-1]
    tm = _tile(t_tok, 256)

    def body(h_ref, p_ref, t_ref, lp_ref, lf_ref, wg_ref, wp_ref,
             dh_ref, n_ref, dgp_ref, dpe_ref, pb_ref, loss_ref, dlf_ref, dlp_ref, pe_s, dn_s):
        @pl.when(pl.program_id(0) == 0)
        def _():
            loss_ref[...] = jnp.zeros_like(loss_ref)
            dlf_ref[...] = jnp.zeros_like(dlf_ref)
            dlp_ref[...] = jnp.zeros_like(dlp_ref)

        xv = h_ref[...]
        r3 = _rstd(xv)
        nb = (xv * r3 * lp_ref[...]).astype(BF)
        n_ref[...] = nb
        pb = p_ref[...].astype(BF)
        pb_ref[...] = pb
        pgpre = jnp.zeros((tm, d), F32)
        for j in range(N_CHIPS):
            pgpre = pgpre + _dot(nb[:, j * rg:(j + 1) * rg], wg_ref[j])
            pe_s[:, j * cp:(j + 1) * cp] = _dot(pb, wp_ref[j])
        pg = _sigmoid(pgpre)
        pe = pe_s[...]
        h4 = xv + pg * pe
        r4 = _rstd(h4)
        err = h4 * r4 * lf_ref[...] - t_ref[...]
        loss_ref[...] += 0.5 * jnp.sum(jnp.sum(err * err, axis=1, keepdims=True), axis=0, keepdims=True) / d
        dh4, dlf = _rms_bwd(err * (1.0 / d), h4, r4, lf_ref[...])
        dlf_ref[...] += dlf
        dpe_ref[...] = (dh4 * pg).astype(BF)
        dgp = (dh4 * pe * pg * (1.0 - pg)).astype(BF)
        dgp_ref[...] = dgp
        for j in range(N_CHIPS):
            dn_s[:, j * rg:(j + 1) * rg] = _dot_nt(dgp, wg_ref[j])
        dx, dlp = _rms_bwd(dn_s[...], xv, r3, lp_ref[...])
        dlp_ref[...] += dlp
        dh_ref[...] = dh4 + dx

    row = lambda c: pl.BlockSpec((tm, c), lambda i: (i, 0))
    full = lambda *s: pl.BlockSpec(s, lambda i: (0,) * len(s))
    sds = lambda c, dt: jax.ShapeDtypeStruct((t_tok, c), dt)
    vec = jax.ShapeDtypeStruct((1, d), F32)
    return _pcall(
        body, [h, p, target, ln_ple, ln_fin, wpg4, wpl4], name="tail", grid=(t_tok // tm,),
        out_shape=[sds(d, F32), sds(d, BF), sds(d, BF), sds(d, BF), sds(pd, BF),
                   jax.ShapeDtypeStruct((1, 128), F32), vec, vec],
        in_specs=[row(d), row(pd), row(d), full(1, d), full(1, d), full(N_CHIPS, rg, d), full(N_CHIPS, pd, cp)],
        out_specs=[row(d), row(d), row(d), row(d), row(pd), full(1, 128), full(1, d), full(1, d)],
        scratch=[pltpu.VMEM((tm, d), F32), pltpu.VMEM((tm, d), F32)], comm=comm)


BIG = ["w_ffn1_gate", "w_ffn1_up", "w_ffn1_down", "w_in", "w_merge", "w_ret_out", "w_fox_out", "w_out",
       "w_ffn2_gate", "w_ffn2_up", "w_ffn2_down", "w_ple", "w_ple_gate"]
SMALL = ["ln_ffn1", "ln_mix", "b_forget", "b_merge", "ln_ffn2", "ln_ple", "ln_final"]
WEIGHTS = ["ln_ffn1", "w_ffn1_gate", "w_ffn1_up", "w_ffn1_down", "ln_mix", "w_in", "b_forget", "w_merge", "b_merge",
           "w_ret_out", "w_fox_out", "w_out", "ln_ffn2", "w_ffn2_gate", "w_ffn2_up", "w_ffn2_down", "ln_ple",
           "w_ple", "w_ple_gate", "ln_final"]


TRANSPOSED = {"w_ffn1_gate", "w_ffn1_up", "w_ffn2_gate", "w_ffn2_up", "w_in"}
IN_ROWS_PAD = -(-(IN_COLS // N_CHIPS) // 32) * 32


def _pack_small(vals, loss_row):
    rows = [loss_row]
    for name in SMALL:
        v = vals[name].reshape(-1)
        n = -(-v.shape[0] // 128) * 128
        rows.append(jnp.pad(v, (0, n - v.shape[0])).reshape(n // 128, 128))
    packed = jnp.concatenate(rows, axis=0)
    pad = -packed.shape[0] % 8
    return jnp.pad(packed, ((0, pad), (0, 0)))


def _unpack_small(packed, sizes):
    out, r = {}, 1
    for name in SMALL:
        n = sizes[name]
        nr = -(-n // 128)
        out[name] = packed[r:r + nr].reshape(1, nr * 128)[:, :n]
        r += nr
    return out


class _Stage:
    def __init__(self, comm, finish):
        self.comm, self.finish, self.result = comm, finish, None


def _hosted(fn, *a, stages=()):
    if not stages:
        return fn(*a)
    outs, couts = fn(*a, comm=_merge([st.comm for st in stages]))
    for st, o in zip(stages, _split_outs([st.comm for st in stages], couts)):
        st.result = st.finish(o)
    return outs


class _Reducer:
    def __init__(self):
        self.done = {}

    def swap(self, grads):
        names = list(grads)
        return _Stage(_c_half_swap([grads[n] for n in names]),
                      lambda outs: {n: _add_halves(grads[n], o) for n, o in zip(names, outs)})

    def exchange(self, parts):
        names = list(parts)
        return _Stage(_c_chip_exchange([parts[n] for n in names]),
                      lambda outs: {n: _sum_chips(parts[n], o) for n, o in zip(names, outs)})

    def join(self, halves):
        names = list(halves)
        return _Stage(_c_join([halves[n] for n in names]),
                      lambda outs: self.done.update({n: (halves[n], o) for n, o in zip(names, outs)}))


def kernel(x, p, positions, ln_ffn1, w_ffn1_gate, w_ffn1_up, w_ffn1_down, ln_mix, w_in, b_forget, w_merge, b_merge, w_ret_out, w_fox_out, w_out, ln_ffn2, w_ffn2_gate, w_ffn2_up, w_ffn2_down, ln_ple, w_ple, w_ple_gate, ln_final, loss_target, m_ln_ffn1, m_w_ffn1_gate, m_w_ffn1_up, m_w_ffn1_down, m_ln_mix, m_w_in, m_b_forget, m_w_merge, m_b_merge, m_w_ret_out, m_w_fox_out, m_w_out, m_ln_ffn2, m_w_ffn2_gate, m_w_ffn2_up, m_w_ffn2_down, m_ln_ple, m_w_ple, m_w_ple_gate, m_ln_final, v_ln_ffn1, v_w_ffn1_gate, v_w_ffn1_up, v_w_ffn1_down, v_ln_mix, v_w_in, v_b_forget, v_w_merge, v_b_merge, v_w_ret_out, v_w_fox_out, v_w_out, v_ln_ffn2, v_w_ffn2_gate, v_w_ffn2_up, v_w_ffn2_down, v_ln_ple, v_w_ple, v_w_ple_gate, v_ln_final):
    args = dict(locals())
    w = {n: args[n] for n in WEIGHTS}
    m = {n: args["m_" + n] for n in WEIGHTS}
    v = {n: args["v_" + n] for n in WEIGHTS}
    d = x.shape[-1]
    t_tok = x.shape[1]
    xs, ps, target = x[0], p[0, 0], loss_target[0]
    small = {n: w[n].reshape(1, -1) for n in SMALL}

    def to2d(n, a):
        if n in TRANSPOSED:
            return a[0].T
        return a.reshape(a.shape[-2], a.shape[-1]) if a.ndim == 3 else a.reshape(1, -1)

    def from2d(n, a):
        return a.T[None] if n in TRANSPOSED else a.reshape(w[n].shape)

    def padded(n, a):
        return jnp.pad(a, ((0, IN_ROWS_PAD - a.shape[0]), (0, 0))) if n == "w_in" else a

    core = lax.axis_index("c")
    me = 2 * lax.axis_index("x") + lax.axis_index("y")
    shard = {}
    for n in BIG:
        s2 = padded(n, to2d(n, w[n]).astype(BF))
        shard[n] = s2.reshape(1, 2, s2.shape[0] // 2, s2.shape[1])
    full = {}

    def gather(names):
        bufs = [lax.dynamic_update_slice(jnp.zeros((N_CHIPS,) + shard[n].shape[1:], BF), shard[n], (me, 0, 0, 0))
                for n in names]

        def finish(outs):
            full.update({n: o.reshape(N_CHIPS, 2 * o.shape[2], o.shape[3]) for n, o in zip(names, outs)})

        return _Stage(_c_all_gather(bufs), finish)

    half = RET_DIM // 2
    inv_freq = 1.0 / (ROPE_BASE ** (jnp.arange(half, dtype=F32) / half))
    cos_t, sin_t = _hosted(_rope_tables, positions[0].astype(F32).reshape(t_tok, 1),
                           jnp.repeat(inv_freq, 2).reshape(1, RET_DIM),
                           stages=[gather(["w_ffn1_gate", "w_ffn1_up", "w_ffn1_down"])])
    consts = _ret_consts()
    b_pad = jnp.pad(small["b_forget"], ((0, 0), (0, 128 - FOX_HEADS)))

    h1, n1, g1, u1 = _hosted(
        _ffn_fwd, xs, small["ln_ffn1"], full["w_ffn1_gate"], full["w_ffn1_up"], full["w_ffn1_down"],
        stages=[gather(["w_in", "w_merge", "w_ret_out", "w_fox_out", "w_out", "w_ple_gate", "w_ple"])])
    w_in_full = jnp.pad(full["w_in"][:, :IN_COLS // N_CHIPS].reshape(IN_COLS, d), ((0, IN_PAD - IN_COLS), (0, 0)))
    u, rq, rk, rv, rg, fq, fk, fv, ffl, ga, gb = _mix_in(
        h1, small["ln_mix"], w_in_full, full["w_merge"], small["b_merge"], cos_t, sin_t)
    aq, ak = _forget_fwd(ffl, b_pad)
    y_raw, y_ret, states = _ret_fwd(rq, rk, rv, rg, consts)
    y_fox, y_fox32, aqb = _hosted(_fox_fwd, fq, fk, fv, aq, ak,
                                  stages=[gather(["w_ffn2_gate", "w_ffn2_up", "w_ffn2_down"])])
    h2, za, zb, mix = _mix_out(h1, y_ret, y_fox, ga, gb, full["w_ret_out"], full["w_fox_out"], full["w_out"])
    h3, n2, g2, u2 = _ffn_fwd(h2, small["ln_ffn2"], full["w_ffn2_gate"], full["w_ffn2_up"], full["w_ffn2_down"])

    red = _Reducer()
    dh3, n3, dpgpre, dpe, pb, loss, dln_final, dln_ple = _tail(
        h3, ps, target, small["ln_ple"], small["ln_final"], full["w_ple_gate"], full["w_ple"])
    g_f2 = dict(w_ple_gate=_wgrad_rows("wgrad_ple_gate", n3, dpgpre, N_CHIPS),
                w_ple=_wgrad_cols("wgrad_ple", pb, dpe, N_CHIPS))
    dh2, dln_ffn2, dg2, du2, a2, dhb3 = _ffn_bwd(
        dh3, h2, small["ln_ffn2"], g2, u2, full["w_ffn2_gate"], full["w_ffn2_up"], full["w_ffn2_down"])
    g_f2["w_ffn2_gate"] = _wgrad_b_shared("wgrad_ffn2_gate", dg2, n2)
    g_f2["w_ffn2_up"] = _wgrad_b_shared("wgrad_ffn2_up", du2, n2)
    g_f2["w_ffn2_down"] = _wgrad_b_shared("wgrad_ffn2_down", a2, dhb3)

    sw_f2 = red.swap(g_f2)
    dhb2, dgpre, dza, dzb, dy_ret, dy_fox, ad, db_merge = _hosted(
        _mix_out_bwd, dh2, za, zb, ga, gb, y_fox32, full["w_ret_out"], full["w_fox_out"], full["w_out"],
        stages=[sw_f2])
    g_br = dict(w_out=_wgrad_rows("wgrad_out", mix, dhb2, N_CHIPS),
                w_ret_out=_wgrad_cols("wgrad_ret_out", y_ret, dza, N_CHIPS),
                w_fox_out=_wgrad_cols("wgrad_fox_out", y_fox, dzb, N_CHIPS))

    sw_br = red.swap(g_br)
    drq, drk, drv, drg = _hosted(_ret_bwd, rq, rk, rv, rg, y_raw, dy_ret, states, consts, cos_t, sin_t,
                                 stages=[sw_br])
    ex_f2, ex_br = red.exchange(sw_f2.result), red.exchange(sw_br.result)
    dfq, dfk, dfv, dcum_t3, dcum_q = _hosted(_fox_bwd, fq, fk, fv, dy_fox, aqb, ak, ad, stages=[ex_f2, ex_br])
    dff, db_forget = _forget_bwd(dcum_t3.reshape(FOX_HEADS, t_tok), dcum_q, ffl, b_pad)
    dh1, dln_mix, dproj = _hosted(
        _mix_in_bwd, dh2, h1, small["ln_mix"], (drq, drk, drv, drg, dfq, dfk, dfv), dff, dgpre, w_in_full,
        full["w_merge"], stages=[red.join(ex_f2.result), red.join(ex_br.result)])

    dx, dln_ffn1, dg1, du1, a1, dhb1 = _ffn_bwd(
        dh1, xs, small["ln_ffn1"], g1, u1, full["w_ffn1_gate"], full["w_ffn1_up"], full["w_ffn1_down"])
    g_f1g = _wgrad_b_shared("wgrad_ffn1_gate", dg1, n1)
    sw_f1g = red.swap(dict(w_ffn1_gate=g_f1g))
    g_f1u = _hosted(_wgrad_b_shared, "wgrad_ffn1_up", du1, n1, stages=[sw_f1g])
    ex_f1g, sw_f1u = red.exchange(sw_f1g.result), red.swap(dict(w_ffn1_up=g_f1u))
    g_f1d = _hosted(_wgrad_b_shared, "wgrad_ffn1_down", a1, dhb1, stages=[ex_f1g, sw_f1u])

    ex_f1u, sw_f1d = red.exchange(sw_f1u.result), red.swap(dict(w_ffn1_down=g_f1d))
    g_in = _hosted(_wgrad_rows, "wgrad_in", dproj, u, IN_PAD // 512,
                   stages=[ex_f1u, sw_f1d, red.join(ex_f1g.result)])
    g_in = g_in.reshape(IN_PAD, d)[:IN_COLS].reshape(N_CHIPS, IN_COLS // N_CHIPS, d)
    g_in = jnp.pad(g_in, ((0, 0), (0, IN_ROWS_PAD - IN_COLS // N_CHIPS), (0, 0)))
    ex_f1d, sw_in = red.exchange(sw_f1d.result), red.swap(dict(w_in=g_in))
    g_mrg = _hosted(_wgrad_cols, "wgrad_merge", u, dgpre, N_CHIPS,
                    stages=[ex_f1d, sw_in, red.join(ex_f1u.result)])

    small_grads = dict(ln_ffn1=dln_ffn1, ln_mix=dln_mix, b_forget=db_forget[:, :FOX_HEADS], b_merge=db_merge,
                       ln_ffn2=dln_ffn2, ln_ple=dln_ple, ln_final=dln_final)
    sizes = {n: w[n].size for n in SMALL}
    ex_in, sw_mrg = red.exchange(sw_in.result), red.swap(dict(w_merge=g_mrg))
    reduced = _hosted(_all_reduce_small, _pack_small(small_grads, loss),
                      stages=[ex_in, sw_mrg, red.join(ex_f1d.result)])
    gsum = _unpack_small(reduced, sizes)
    loss = reduced[0, 0]
    ex_mrg = red.exchange(sw_mrg.result)
    _hosted(_exchange_only, stages=[ex_mrg, red.join(ex_in.result)])
    _hosted(_exchange_only, stages=[red.join(ex_mrg.result)])

    results = {}

    def update(names, stages=()):
        w2, m2, v2 = ([to2d(n, a[n]) for n in names] for a in (w, m, v))
        n = names[0]
        if n in gsum or n == "w_in":
            if n in gsum:
                g2 = gsum[n]
            else:
                mine, other = red.done[n]
                g2 = jnp.where(core == 0, jnp.concatenate([mine, other]), jnp.concatenate([other, mine]))
                g2 = g2[:w2[0].shape[0]]
            res = [g2] + _hosted(_adamw, w2[0], g2, m2[0], v2[0], stages=stages)
        else:
            res = _hosted(_adamw_halves, [(w2[q], *red.done[names[q]], m2[q], v2[q]) for q in range(len(names))],
                          stages=stages)
        for q, name in enumerate(names):
            results[name] = tuple(from2d(name, a) for a in res[4 * q:4 * q + 4])

    update(["w_ffn2_gate", "w_ffn2_up", "w_ffn2_down"])
    update(["w_ffn1_gate", "w_ffn1_up", "w_ffn1_down"])
    update(["w_out", "w_ple_gate"])
    update(["w_ret_out", "w_fox_out"])
    for n in WEIGHTS:
        if n not in results:
            update([n])

    outs = [[results[n][k] for n in WEIGHTS] for k in range(4)]
    return (loss, dx[None], *outs[0], *outs[1], *outs[2], *outs[3])
```

```python
import functools
import operator

import jax
import jax.numpy as jnp
from jax import lax
from jax.experimental import pallas as pl
from jax.experimental.pallas import tpu as pltpu
from jax.experimental.pallas import tpu_sc as plsc

F32 = jnp.float32
BF = jnp.bfloat16
MESH = pl.DeviceIdType.MESH

EPS = 1e-6
ROPE_BASE = 10000.0
N_CHIPS = 4
RET_HEADS = 4
RET_DIM = 128
RET_WIDTH = RET_HEADS * RET_DIM
RET_CHUNK = 128
RET_SCALE = RET_DIM ** -0.5
FOX_HEADS = 8
FOX_DIM = 64
FOX_WIDTH = FOX_HEADS * FOX_DIM
FOX_SCALE = FOX_DIM ** -0.5
IN_COLS = 4 * RET_WIDTH + 3 * FOX_WIDTH + FOX_HEADS
IN_PAD = 4096
FF_COL = 4 * RET_WIDTH + 3 * FOX_WIDTH
NEG = -1e30

ADAM_LR = 0.001
ADAM_B1 = 0.9
ADAM_B2 = 0.999
ADAM_EPS = 1e-08
ADAM_WD = 0.01
ADAM_STEP = 10

VMEM_LIMIT = 52 * 1024 * 1024

NT = (((1,), (1,)), ((), ()))
TN = (((0,), (0,)), ((), ()))

HBM_SPEC = pl.BlockSpec(memory_space=pltpu.HBM)
VMEM_SPEC = pl.BlockSpec(memory_space=pltpu.VMEM)


def _dot(a, b):
    return jnp.dot(a, b, preferred_element_type=F32)


def _dot_nt(a, b):
    return lax.dot_general(a, b, NT, preferred_element_type=F32)


def _dot_tn(a, b):
    return lax.dot_general(a, b, TN, preferred_element_type=F32)


def _rstd(xv):
    return lax.rsqrt(jnp.mean(xv * xv, axis=-1, keepdims=True) + EPS)


def _rms_bwd(dn, xv, r, ln):
    xh = xv * r
    dxh = dn * ln
    dx = r * (dxh - xh * jnp.mean(dxh * xh, axis=-1, keepdims=True))
    return dx, jnp.sum(dn * xh, axis=0, keepdims=True)


def _sigmoid(x):
    return jax.nn.sigmoid(x)


def _tile(n, pref):
    return pref if n % pref == 0 else n


def _row_tile(n, cap):
    best = [t for t in range(16, min(n, cap) + 1, 16) if n % t == 0]
    return best[-1] if best else n


class _Comm:
    def __init__(self, ins, out_shapes, sems, start, wait, aliases=None):
        self.ins, self.out_shapes, self.sems, self.start, self.wait = list(ins), list(out_shapes), list(sems), start, wait
        self.aliases = dict(aliases or {})


def _merge(comms):
    comms = [c for c in comms if c is not None]
    if not comms:
        return None
    bounds, ni, no, ns = [], 0, 0, 0
    for c in comms:
        bounds.append((ni, no, ns))
        ni, no, ns = ni + len(c.ins), no + len(c.out_shapes), ns + len(c.sems)

    def run(which):
        def f(ins, outs, sems):
            for c, (i, o, s) in zip(comms, bounds):
                getattr(c, which)(ins[i:i + len(c.ins)], outs[o:o + len(c.out_shapes)], sems[s:s + len(c.sems)])
        return f

    aliases = {i + a: o + b for c, (i, o, _) in zip(comms, bounds) for a, b in c.aliases.items()}
    return _Comm([a for c in comms for a in c.ins], [a for c in comms for a in c.out_shapes],
                 [a for c in comms for a in c.sems], run("start"), run("wait"), aliases)


def _split_outs(comms, outs):
    res, o = [], 0
    for c in comms:
        if c is not None:
            res.append(list(outs[o:o + len(c.out_shapes)]))
            o += len(c.out_shapes)
    return res


def _pcall(body, args, *, name, out_shape, grid=(), in_specs=None, out_specs=None, scratch=(), comm=None,
           prefetch=()):
    many = isinstance(out_shape, (list, tuple))
    outs = list(out_shape) if many else [out_shape]
    n_pre, n_in, n_out, n_scr = len(prefetch), len(args), len(outs), len(scratch)
    if in_specs is None:
        in_specs, out_specs = [VMEM_SPEC] * n_in, [VMEM_SPEC] * n_out
    else:
        in_specs, out_specs = list(in_specs), (list(out_specs) if many else [out_specs])
    params = pltpu.CompilerParams(dimension_semantics=("arbitrary",) * len(grid), vmem_limit_bytes=VMEM_LIMIT)
    scalars = [jnp.reshape(s, (1,)).astype(jnp.int32) for s in prefetch]
    ci, co = (len(comm.ins), len(comm.out_shapes)) if comm is not None else (0, 0)

    def wrapped(*refs):
        pre, refs = refs[:n_pre], refs[n_pre:]
        a, ca = refs[:n_in], refs[n_in:n_in + ci]
        o = refs[n_in + ci:n_in + ci + n_out]
        cout = refs[n_in + ci + n_out:n_in + ci + n_out + co]
        s = refs[n_in + ci + n_out + co:n_in + ci + n_out + co + n_scr]
        csem = refs[n_in + ci + n_out + co + n_scr:]
        if comm is None:
            body(*pre, *a, *o, *s)
        elif grid:
            first = functools.reduce(operator.and_, [pl.program_id(k) == 0 for k in range(len(grid))])
            last = functools.reduce(operator.and_, [pl.program_id(k) == grid[k] - 1 for k in range(len(grid))])
            pl.when(first)(lambda: comm.start(ca, cout, csem))
            body(*pre, *a, *o, *s)
            pl.when(last)(lambda: comm.wait(ca, cout, csem))
        else:
            comm.start(ca, cout, csem)
            body(*pre, *a, *o, *s)
            comm.wait(ca, cout, csem)

    c_ins, c_outs, c_sems, aliases = ([], [], [], {}) if comm is None else (
        comm.ins, comm.out_shapes, comm.sems, {n_pre + n_in + i: n_out + o for i, o in comm.aliases.items()})
    all_in, all_out = in_specs + [HBM_SPEC] * ci, out_specs + [HBM_SPEC] * co
    all_scr = list(scratch) + c_sems
    if grid:
        args = [pltpu.with_memory_space_constraint(a, pltpu.HBM) for a in args]
    c_ins = [pltpu.with_memory_space_constraint(a, pltpu.HBM) for a in c_ins]
    if n_pre:
        spec = dict(grid_spec=pltpu.PrefetchScalarGridSpec(
            num_scalar_prefetch=n_pre, grid=grid, in_specs=all_in, out_specs=all_out, scratch_shapes=all_scr))
    else:
        spec = dict(grid=grid, in_specs=all_in, out_specs=all_out, scratch_shapes=all_scr)
    res = pl.pallas_call(wrapped, name=name, out_shape=outs + c_outs, input_output_aliases=aliases,
                         compiler_params=params, **spec)(*scalars, *args, *c_ins)
    mine = list(res[:n_out])
    mine = mine if many else mine[0]
    return mine if comm is None else (mine, list(res[n_out:]))


def _peer_chips(x, y):
    return [(1 - x, y), (x, 1 - y), (1 - x, 1 - y)]


def _c_all_gather(bufs):
    n = len(bufs)

    def copies(ins, outs, sems):
        send_sems, recv_sems, fwd_send, fwd_recv = sems
        x, y, c = lax.axis_index("x"), lax.axis_index("y"), lax.axis_index("c")
        me = 2 * x + y
        peers = _peer_chips(x, y)
        chip = [2 * px + py for px, py in peers]

        def ici(g, j, slot):
            return pltpu.make_async_remote_copy(
                src_ref=outs[g].at[me, c], dst_ref=outs[g].at[slot, c], send_sem=send_sems.at[g, j],
                recv_sem=recv_sems.at[g, j], device_id=(*peers[j], c), device_id_type=MESH)

        def d2d(g, j, half):
            return pltpu.make_async_remote_copy(
                src_ref=outs[g].at[chip[j], half], dst_ref=outs[g].at[chip[j], half], send_sem=fwd_send.at[g, j],
                recv_sem=fwd_recv.at[g, j], device_id=(x, y, 1 - c), device_id_type=MESH)

        pairs = [(g, j) for g in range(n) for j in range(3)]
        sends = [ici(g, j, me) for g, j in pairs]
        recvs = [ici(g, j, chip[j]) for g, j in pairs]
        passes = [d2d(g, j, c) for g, j in pairs]
        passed = [d2d(g, j, 1 - c) for g, j in pairs]
        return sends, recvs, passes, passed

    def start(ins, outs, sems):
        for cp in copies(ins, outs, sems)[0]:
            cp.start()

    def wait(ins, outs, sems):
        sends, recvs, passes, passed = copies(ins, outs, sems)
        for rcv, fwd in zip(recvs, passes):
            rcv.wait_recv()
            fwd.start()
        for cp in passed:
            cp.wait_recv()
        for cp in sends + passes:
            cp.wait_send()

    pair_sems = pltpu.SemaphoreType.DMA((n, 3))
    return _Comm(bufs, [jax.ShapeDtypeStruct(s.shape, s.dtype) for s in bufs], [pair_sems] * 4, start, wait,
                 aliases={g: g for g in range(n)})


def _start_wait(copies):
    def start(ins, outs, sems):
        local, sends, _ = copies(ins, outs, sems)
        for cp in local + sends:
            cp.start()

    def wait(ins, outs, sems):
        local, sends, recvs = copies(ins, outs, sems)
        for cp in recvs:
            cp.wait_recv()
        for cp in sends:
            cp.wait_send()
        for cp in local:
            cp.wait()

    return start, wait


def _c_half_swap(grads):
    n = len(grads)

    def copies(ins, outs, sems):
        send_sems, recv_sems = sems
        x, y, c = lax.axis_index("x"), lax.axis_index("y"), lax.axis_index("c")
        sends = []
        for g in range(n):
            half = ins[g].shape[1] // 2
            sends.append(pltpu.make_async_remote_copy(
                src_ref=ins[g].at[:, pl.ds((1 - c) * half, half), :], dst_ref=outs[g],
                send_sem=send_sems.at[g], recv_sem=recv_sems.at[g], device_id=(x, y, 1 - c), device_id_type=MESH))
        return [], sends, sends

    return _Comm(
        grads, [jax.ShapeDtypeStruct((N_CHIPS, s.shape[1] // 2, s.shape[2]), s.dtype) for s in grads],
        [pltpu.SemaphoreType.DMA((n,)), pltpu.SemaphoreType.DMA((n,))], *_start_wait(copies))


def _c_chip_exchange(parts):
    n = len(parts)

    def copies(ins, outs, sems):
        send_sems, recv_sems = sems
        x, y, c = lax.axis_index("x"), lax.axis_index("y"), lax.axis_index("c")
        peers = _peer_chips(x, y)

        def remote(g, j):
            return pltpu.make_async_remote_copy(
                src_ref=ins[g].at[2 * peers[j][0] + peers[j][1]], dst_ref=outs[g].at[j],
                send_sem=send_sems.at[g, j], recv_sem=recv_sems.at[g, j], device_id=(*peers[j], c),
                device_id_type=MESH)

        sends = [remote(g, j) for g in range(n) for j in range(3)]
        return [], sends, sends

    return _Comm(
        parts, [jax.ShapeDtypeStruct((3,) + s.shape[1:], s.dtype) for s in parts],
        [pltpu.SemaphoreType.DMA((n, 3)), pltpu.SemaphoreType.DMA((n, 3))], *_start_wait(copies))


def _c_join(halves):
    n = len(halves)

    def copies(ins, outs, sems):
        send_sems, recv_sems = sems
        x, y, c = lax.axis_index("x"), lax.axis_index("y"), lax.axis_index("c")
        sends = [pltpu.make_async_remote_copy(
            src_ref=ins[g], dst_ref=outs[g], send_sem=send_sems.at[g], recv_sem=recv_sems.at[g],
            device_id=(x, y, 1 - c), device_id_type=MESH) for g in range(n)]
        return [], sends, sends

    return _Comm(
        halves, [jax.ShapeDtypeStruct(s.shape, s.dtype) for s in halves],
        [pltpu.SemaphoreType.DMA((n,)), pltpu.SemaphoreType.DMA((n,))], *_start_wait(copies))


def _exchange_only(comm=None):
    def body(o_ref):
        o_ref[...] = jnp.zeros_like(o_ref)

    return _pcall(body, [], name="exchange_only", out_shape=jax.ShapeDtypeStruct((8, 128), F32), comm=comm)


def _all_reduce_small(v, comm=None):
    rows = v.shape[0]

    def body(v_ref, out_ref, buf, send_sems, recv_sems):
        x, y, c = lax.axis_index("x"), lax.axis_index("y"), lax.axis_index("c")
        me = 4 * x + 2 * y + c
        buf[me] = v_ref[...]
        flips = [(fx, fy, fc) for fx in (0, 1) for fy in (0, 1) for fc in (0, 1)][1:]

        def peer(k):
            fx, fy, fc = flips[k]
            px, py, pc = x ^ fx, y ^ fy, c ^ fc
            return (px, py, pc), 4 * px + 2 * py + pc

        def copy(k, slot):
            return pltpu.make_async_remote_copy(
                src_ref=buf.at[slot], dst_ref=buf.at[slot], send_sem=send_sems.at[k],
                recv_sem=recv_sems.at[k], device_id=peer(k)[0], device_id_type=MESH)

        sends = [copy(k, me) for k in range(7)]
        for cp in sends:
            cp.start()
        for k in range(7):
            copy(k, peer(k)[1]).wait_recv()
        for cp in sends:
            cp.wait_send()
        acc = buf[0]
        for d in range(1, 8):
            acc = acc + buf[d]
        out_ref[...] = acc

    return _pcall(body, [v], name="all_reduce_small", out_shape=jax.ShapeDtypeStruct((rows, 128), F32),
                  scratch=[pltpu.VMEM((8, rows, 128), F32), pltpu.SemaphoreType.DMA((7,)),
                           pltpu.SemaphoreType.DMA((7,))], comm=comm)


def _add_halves(g, got):
    _, h, c = got.shape
    th = _row_tile(h, 512)
    nh = h // th
    half = lax.axis_index("c") * nh

    def body(h_ref, a_ref, b_ref, o_ref):
        o_ref[...] = (a_ref[...].astype(F32) + b_ref[...].astype(F32)).astype(o_ref.dtype)

    spec = pl.BlockSpec((1, th, c), lambda j, i, h_ref: (j, i, 0))
    mine = pl.BlockSpec((1, th, c), lambda j, i, h_ref: (j, h_ref[0] + i, 0))
    return _pcall(body, [g, got], name="add_halves", grid=(N_CHIPS, nh), prefetch=[half],
                  out_shape=jax.ShapeDtypeStruct(got.shape, BF), in_specs=[mine, spec], out_specs=spec)


def _sum_chips(parts, recv):
    _, h, c = parts.shape
    th = _row_tile(h, 512)
    me = 2 * lax.axis_index("x") + lax.axis_index("y")

    def body(me_ref, p_ref, r_ref, o_ref):
        acc = p_ref[0].astype(F32)
        for s in range(N_CHIPS - 1):
            acc = acc + r_ref[s].astype(F32)
        o_ref[...] = acc

    return _pcall(body, [parts, recv], name="sum_chips", grid=(h // th,), prefetch=[me],
                  out_shape=jax.ShapeDtypeStruct((h, c), F32),
                  in_specs=[pl.BlockSpec((1, th, c), lambda i, me_ref: (me_ref[0], i, 0)),
                            pl.BlockSpec((N_CHIPS - 1, th, c), lambda i, me_ref: (0, i, 0))],
                  out_specs=pl.BlockSpec((th, c), lambda i, me_ref: (i, 0)))


def _adam_update(w, gv, m, v, d_ref, nm_ref, nv_ref):
    c1 = 1.0 / (1.0 - ADAM_B1 ** ADAM_STEP)
    c2 = 1.0 / (1.0 - ADAM_B2 ** ADAM_STEP)
    nm = ADAM_B1 * m + (1.0 - ADAM_B1) * gv
    nv = ADAM_B2 * v + (1.0 - ADAM_B2) * (gv * gv)
    nm_ref[...] = nm
    nv_ref[...] = nv
    d_ref[...] = -ADAM_LR * ((nm * c1) / (jnp.sqrt(nv * c2) + ADAM_EPS) + ADAM_WD * w)


def _adamw(w, g, m, v, comm=None):
    r, c = w.shape
    tr = _row_tile(r, 512)

    def body(w_ref, g_ref, m_ref, v_ref, d_ref, nm_ref, nv_ref):
        _adam_update(w_ref[...], g_ref[...], m_ref[...], v_ref[...], d_ref, nm_ref, nv_ref)

    spec = pl.BlockSpec((tr, c), lambda i: (i, 0))
    sds = jax.ShapeDtypeStruct((r, c), F32)
    return _pcall(body, [w, g, m, v], name="adamw", grid=(r // tr,), out_shape=[sds, sds, sds],
                  in_specs=[spec] * 4, out_specs=[spec] * 3, comm=comm)


def _adamw_halves(items, comm=None):
    k = len(items)
    r, c = items[0][0].shape
    h = r // 2
    tr = _row_tile(h, min(512, (VMEM_LIMIT * 3 // 4) // (k * 9 * 2 * 4 * c)))
    nb = h // tr
    core = lax.axis_index("c")

    def body(c_ref, *refs):
        ins, outs = refs[:5 * k], refs[5 * k:]
        for q in range(k):
            w_ref, gm_ref, go_ref, m_ref, v_ref = ins[5 * q:5 * q + 5]
            g_ref, d_ref, nm_ref, nv_ref = outs[4 * q:4 * q + 4]
            gv = jnp.where(pl.program_id(0) == c_ref[0], gm_ref[...], go_ref[...])
            g_ref[...] = gv
            _adam_update(w_ref[...], gv, m_ref[...], v_ref[...], d_ref, nm_ref, nv_ref)

    full = pl.BlockSpec((tr, c), lambda hh, i, c_ref: (hh * nb + i, 0))
    half = pl.BlockSpec((tr, c), lambda hh, i, c_ref: (i, 0))
    sds = jax.ShapeDtypeStruct((r, c), F32)
    return _pcall(body, [a for it in items for a in it], name="adamw_halves", grid=(2, nb), prefetch=[core],
                  out_shape=[sds] * (4 * k), in_specs=[full, half, half, full, full] * k, out_specs=[full] * (4 * k),
                  comm=comm)


SC_CORES, SC_TILES, SC_LANES = 2, 16, 16
SC_BLOCK_ROWS, SC_BLOCK_COLS = 8, 512


def _sc_adamw_halves(items):
    k = len(items)
    r, c = items[0][0].shape
    h = r // 2
    bc = min(c, SC_BLOCK_COLS)
    c1 = 1.0 / (1.0 - ADAM_B1 ** ADAM_STEP)
    c2 = 1.0 / (1.0 - ADAM_B2 ** ADAM_STEP)
    mesh = plsc.VectorSubcoreMesh(core_axis_name="sc_core", subcore_axis_name="sc_tile",
                                  num_cores=SC_CORES, num_subcores=SC_TILES)
    spec = pl.BlockSpec(block_shape=(SC_BLOCK_ROWS, bc), index_map=lambda i, j: (i, j))

    def block(w_v, gin_v, m_v, v_v, g_v, d_v, nm_v, nv_v):
        @pl.loop(0, SC_BLOCK_ROWS)
        def _(row):
            @pl.loop(0, bc, step=SC_LANES)
            def _(col):
                at = (pl.ds(row, 1), pl.ds(col, SC_LANES))
                gv = gin_v.at[*at][...]
                nm = ADAM_B1 * m_v.at[*at][...] + (1.0 - ADAM_B1) * gv
                nv = ADAM_B2 * v_v.at[*at][...] + (1.0 - ADAM_B2) * (gv * gv)
                g_v.at[*at][...] = gv
                nm_v.at[*at][...] = nm
                nv_v.at[*at][...] = nv
                d_v.at[*at][...] = -ADAM_LR * ((nm * c1) / (jnp.sqrt(nv * c2) + ADAM_EPS) + ADAM_WD * w_v.at[*at][...])

    def kern(*refs):
        ins, outs = refs[:5 * k], refs[5 * k:]
        core = lax.axis_index("c")

        def half(q, hh, mine):
            w_hbm, gm_hbm, go_hbm, m_hbm, v_hbm = ins[5 * q:5 * q + 5]
            rows = pl.ds(hh * h, h)
            pltpu.emit_pipeline(
                block, grid=(h // SC_BLOCK_ROWS, c // bc), in_specs=[spec] * 4, out_specs=[spec] * 4,
                core_axis_name=("sc_core", "sc_tile"), dimension_semantics=(pltpu.PARALLEL, pltpu.PARALLEL),
                trace_scopes=False,
            )(w_hbm.at[rows, :], gm_hbm if mine else go_hbm, m_hbm.at[rows, :], v_hbm.at[rows, :],
              *(o.at[rows, :] for o in outs[4 * q:4 * q + 4]))

        for q in range(k):
            for hh in range(2):
                pl.when(core == hh)(lambda q=q, hh=hh: half(q, hh, True))
                pl.when(core != hh)(lambda q=q, hh=hh: half(q, hh, False))

    sds = jax.ShapeDtypeStruct((r, c), F32)
    return pl.kernel(kern, out_type=[sds] * (4 * k), mesh=mesh, scratch_types=[], name="sc_adamw_halves")(
        *(a for it in items for a in it))


def _wgrad(name, a, b, a_spec, b_spec, m, n, nb, comm):
    def body(a_ref, b_ref, o_ref):
        o_ref[...] = _dot_tn(a_ref[...], b_ref[...]).astype(o_ref.dtype)

    return _pcall(body, [a, b], name=name, grid=(nb,), out_shape=jax.ShapeDtypeStruct((nb, m, n), BF),
                  in_specs=[a_spec, b_spec], out_specs=pl.BlockSpec((None, m, n), lambda j: (j, 0, 0)), comm=comm)


def _wgrad_cols(name, a, b, nb, comm=None):
    t_tok, m = a.shape
    n = b.shape[1] // nb
    return _wgrad(name, a, b, pl.BlockSpec((t_tok, m), lambda j: (0, 0)), pl.BlockSpec((t_tok, n), lambda j: (0, j)),
                  m, n, nb, comm)


def _wgrad_rows(name, a, b, nb, comm=None):
    t_tok, n = b.shape
    m = a.shape[1] // nb
    return _wgrad(name, a, b, pl.BlockSpec((t_tok, m), lambda j: (0, j)), pl.BlockSpec((t_tok, n), lambda j: (0, 0)),
                  m, n, nb, comm)


def _wgrad_a_shared(name, a, b4, comm=None):
    t_tok, m = a.shape
    nb, _, n = b4.shape
    return _wgrad(name, a, b4, pl.BlockSpec((t_tok, m), lambda j: (0, 0)),
                  pl.BlockSpec((None, t_tok, n), lambda j: (j, 0, 0)), m, n, nb, comm)


def _wgrad_b_shared(name, a4, b, comm=None):
    nb, t_tok, m = a4.shape
    n = b.shape[1]
    return _wgrad(name, a4, b, pl.BlockSpec((None, t_tok, m), lambda j: (j, 0, 0)),
                  pl.BlockSpec((t_tok, n), lambda j: (0, 0)), m, n, nb, comm)


def _w4_spec(r, c):
    return pl.BlockSpec((None, r, c), lambda i, j: (j, 0, 0))


FFN_ROW_CHUNK = 256


def _row_chunks(tm):
    rc = FFN_ROW_CHUNK if tm % FFN_ROW_CHUNK == 0 else tm
    return [slice(r, r + rc) for r in range(0, tm, rc)]


def _ffn_fwd(h, ln, wg4, wu4, wd4, comm=None):
    t_tok, d = h.shape
    f = wg4.shape[-2]
    tm = _tile(t_tok, 512)

    def body(h_ref, ln_ref, wg_ref, wu_ref, wd_ref, ho_ref, n_ref, g_ref, u_ref, n_s, acc):
        j = pl.program_id(1)

        @pl.when(j == 0)
        def _():
            xv = h_ref[...]
            nv = (xv * _rstd(xv) * ln_ref[...]).astype(BF)
            n_s[...] = nv
            n_ref[...] = nv
            acc[...] = jnp.zeros_like(acc)

        nv = n_s[...]
        g = _dot_nt(nv, wg_ref[...])
        u = _dot_nt(nv, wu_ref[...])
        g_ref[...] = g.astype(BF)
        u_ref[...] = u.astype(BF)
        a = (g * _sigmoid(g) * u).astype(BF)
        acc[...] += _dot(a, wd_ref[...])

        @pl.when(j == N_CHIPS - 1)
        def _():
            ho_ref[...] = h_ref[...] + 0.5 * acc[...]

    row = pl.BlockSpec((tm, d), lambda i, j: (i, 0))
    gu = pl.BlockSpec((None, tm, f), lambda i, j: (j, i, 0))
    gu_sds = jax.ShapeDtypeStruct((N_CHIPS, t_tok, f), BF)
    return _pcall(
        body, [h, ln, wg4, wu4, wd4], name="ffn_fwd", grid=(t_tok // tm, N_CHIPS),
        out_shape=[jax.ShapeDtypeStruct((t_tok, d), F32), jax.ShapeDtypeStruct((t_tok, d), BF), gu_sds, gu_sds],
        in_specs=[row, pl.BlockSpec((1, d), lambda i, j: (0, 0)), _w4_spec(f, d), _w4_spec(f, d), _w4_spec(f, d)],
        out_specs=[row, row, gu, gu],
        scratch=[pltpu.VMEM((tm, d), BF), pltpu.VMEM((tm, d), F32)], comm=comm)


def _ffn_bwd(dho, h, ln, g4, u4, wg4, wu4, wd4, comm=None):
    t_tok, d = h.shape
    f = wg4.shape[-2]
    tm = _tile(t_tok, 512)

    def body(dho_ref, h_ref, ln_ref, g_ref, u_ref, wg_ref, wu_ref, wd_ref,
             dhi_ref, dln_ref, dg_ref, du_ref, a_ref, dhb_ref, dhb_s, dn_acc):
        i, j = pl.program_id(0), pl.program_id(1)

        @pl.when(j == 0)
        def _():
            dhb = (0.5 * dho_ref[...]).astype(BF)
            dhb_s[...] = dhb
            dhb_ref[...] = dhb
            dn_acc[...] = jnp.zeros_like(dn_acc)

        @pl.when((i == 0) & (j == 0))
        def _():
            dln_ref[...] = jnp.zeros_like(dln_ref)

        for rows in _row_chunks(tm):
            g = g_ref[rows, :].astype(F32)
            u = u_ref[rows, :].astype(F32)
            s = _sigmoid(g)
            sg = g * s
            a_ref[rows, :] = (sg * u).astype(BF)
            da = _dot_nt(dhb_s[rows, :], wd_ref[...])
            dg = (da * u * (s * (1.0 + g * (1.0 - s)))).astype(BF)
            du = (da * sg).astype(BF)
            dg_ref[rows, :] = dg
            du_ref[rows, :] = du
            dn_acc[rows, :] += _dot(dg, wg_ref[...]) + _dot(du, wu_ref[...])

        @pl.when(j == N_CHIPS - 1)
        def _():
            xv = h_ref[...]
            dx, dln = _rms_bwd(dn_acc[...], xv, _rstd(xv), ln_ref[...])
            dln_ref[...] += dln
            dhi_ref[...] = dho_ref[...] + dx

    row = pl.BlockSpec((tm, d), lambda i, j: (i, 0))
    vec = pl.BlockSpec((1, d), lambda i, j: (0, 0))
    gu = pl.BlockSpec((None, tm, f), lambda i, j: (j, i, 0))
    gu_sds = jax.ShapeDtypeStruct((N_CHIPS, t_tok, f), BF)
    return _pcall(
        body, [dho, h, ln, g4, u4, wg4, wu4, wd4], name="ffn_bwd", grid=(t_tok // tm, N_CHIPS),
        out_shape=[jax.ShapeDtypeStruct((t_tok, d), F32), jax.ShapeDtypeStruct((1, d), F32),
                   gu_sds, gu_sds, gu_sds, jax.ShapeDtypeStruct((t_tok, d), BF)],
        in_specs=[row, row, vec, gu, gu, _w4_spec(f, d), _w4_spec(f, d), _w4_spec(f, d)],
        out_specs=[row, vec, gu, gu, gu, row],
        scratch=[pltpu.VMEM((tm, d), BF), pltpu.VMEM((tm, d), F32)], comm=comm)


def _rope_tables(pos_col, inv_freq2, comm=None):
    t_tok = pos_col.shape[0]

    def body(p_ref, f_ref, cos_ref, sin_ref):
        ang = p_ref[...] * f_ref[...]
        lane = lax.broadcasted_iota(jnp.int32, ang.shape, 1)
        s = jnp.sin(ang)
        cos_ref[...] = jnp.cos(ang)
        sin_ref[...] = jnp.where((lane & 1) == 0, -s, s)

    sds = jax.ShapeDtypeStruct((t_tok, 128), F32)
    return _pcall(body, [pos_col, inv_freq2], name="rope_tables", out_shape=[sds, sds], comm=comm)


def _swap_pairs(x):
    lane = lax.broadcasted_iota(jnp.int32, x.shape, 1)
    return jnp.where((lane & 1) == 0, pltpu.roll(x, 127, 1), pltpu.roll(x, 1, 1))


def _mix_in(h, ln, w_in, wm4, b_m, cos_t, sin_t, comm=None):
    t_tok, d = h.shape
    cm = wm4.shape[-1]
    tm = _tile(t_tok, 256)

    def body(h_ref, ln_ref, win_ref, wm_ref, bm_ref, cos_ref, sin_ref,
             u_ref, rq_ref, rk_ref, rv_ref, rg_ref, fq_ref, fk_ref, fv_ref, ff_ref, ga_ref, gb_ref):
        xv = h_ref[...]
        ub = (xv * _rstd(xv) * ln_ref[...]).astype(BF)
        u_ref[...] = ub
        cosv, sinv = cos_ref[...], sin_ref[...]

        def sec(k):
            return _dot_nt(ub, win_ref[k * 512:(k + 1) * 512, :])

        def rot(xh):
            return xh * cosv + _swap_pairs(xh) * sinv

        pq, pk = sec(0), sec(1)
        for hh in range(RET_HEADS):
            sl = slice(hh * RET_DIM, (hh + 1) * RET_DIM)
            rq_ref[:, sl] = rot(pq[:, sl]).astype(BF)
            rk_ref[:, sl] = (rot(pk[:, sl]) * RET_SCALE).astype(BF)
        rv_ref[...] = sec(2).astype(BF)
        rg_ref[...] = sec(3).astype(BF)
        fq_ref[...] = (sec(4) * FOX_SCALE).astype(BF)
        fk_ref[...] = sec(5).astype(BF)
        fv_ref[...] = sec(6).astype(BF)
        ff_ref[...] = _dot_nt(ub, win_ref[FF_COL:FF_COL + 128, :])
        for j in range(N_CHIPS):
            gs = _sigmoid(_dot(ub, wm_ref[j]) + bm_ref[:, j * cm:(j + 1) * cm]).astype(BF)
            col = j * cm
            if col < d:
                ga_ref[:, col:col + cm] = gs
            else:
                gb_ref[:, col - d:col - d + cm] = gs

    row = lambda c: pl.BlockSpec((tm, c), lambda i: (i, 0))
    full = lambda *s: pl.BlockSpec(s, lambda i: (0,) * len(s))
    sds = lambda c, dt: jax.ShapeDtypeStruct((t_tok, c), dt)
    return _pcall(
        body, [h, ln, w_in, wm4, b_m, cos_t, sin_t], name="mix_in", grid=(t_tok // tm,),
        out_shape=[sds(d, BF)] + [sds(512, BF)] * 7 + [sds(128, F32), sds(d, BF), sds(d, BF)],
        in_specs=[row(d), full(1, d), full(IN_PAD, d), full(N_CHIPS, d, cm), full(1, 2 * d), row(128), row(128)],
        out_specs=[row(d)] + [row(512)] * 7 + [row(128), row(d), row(d)], comm=comm)


def _split3(x):
    hi = x.astype(BF)
    r1 = x - hi.astype(F32)
    mid = r1.astype(BF)
    lo = (r1 - mid.astype(F32)).astype(BF)
    return hi, mid, lo


def _aug_lane():
    return lax.broadcasted_iota(jnp.int32, (1, 128), 1) & (FOX_DIM - 1)


def _aug_put(base, k0, parts):
    w = _aug_lane()
    for i, part in enumerate(parts):
        base = jnp.where(w == k0 + i, part, base)
    return base


def _forget_fwd(ffl, b_pad):
    t_tok = ffl.shape[0]
    tb = _tile(t_tok, 256)

    def body(ff_ref, b_ref, aq_ref, ak_ref, cum_s):
        r = lax.broadcasted_iota(jnp.int32, (tb, tb), 0)
        c = lax.broadcasted_iota(jnp.int32, (tb, tb), 1)
        tri = jnp.where(c <= r, 1.0, 0.0).astype(BF)
        carry = jnp.zeros((1, 128), F32)
        for i in range(t_tok // tb):
            z = ff_ref[i * tb:(i + 1) * tb, :] + b_ref[...]
            lf = jnp.minimum(z, 0.0) - jnp.log(1.0 + jnp.exp(-jnp.abs(z)))
            hi, mid, lo = _split3(lf)
            cs = _dot(tri, hi) + _dot(tri, mid) + _dot(tri, lo) + carry
            cum_s[i * tb:(i + 1) * tb, :] = cs
            carry = cs[tb - 1:tb, :]
        x = cum_s[...]
        first = lax.broadcasted_iota(jnp.int32, (1, 128), 1) < FOX_DIM
        w = _aug_lane()
        one = jnp.ones((t_tok, 128), BF)
        zero = jnp.zeros((t_tok, 128), BF)
        for pp in range(FOX_HEADS // 2):
            other = jnp.where(first, x[:, 2 * pp + 1:2 * pp + 2], x[:, 2 * pp:2 * pp + 1])
            parts = _split3(other)
            aq = jnp.where((w >= 3) & (w < 6), one, zero)
            ak = jnp.where((w < 3) | ((w >= 6) & (w < 9)), one, zero)
            aq_ref[:, pp * 128:(pp + 1) * 128] = _aug_put(aq, 0, parts)
            ak_ref[:, pp * 128:(pp + 1) * 128] = _aug_put(ak, 3, [-q for q in parts])

    sds = jax.ShapeDtypeStruct((t_tok, FOX_WIDTH), BF)
    return _pcall(body, [ffl, b_pad], name="forget_fwd", out_shape=[sds, sds],
                  scratch=[pltpu.VMEM((t_tok, 128), F32)])


def _forget_bwd(dcum_t, dcum_q, ffl, b_pad):
    t_tok = ffl.shape[0]
    tb = _tile(t_tok, 256)

    def body(dc_ref, dq_ref, ff_ref, b_ref, dff_ref, db_ref, pad_s, d_s):
        pad_s[...] = jnp.zeros_like(pad_s)
        pad_s[0:FOX_HEADS, :] = dc_ref[...]
        dsum = pad_s[...].T
        lane = lax.broadcasted_iota(jnp.int32, (t_tok, 128), 1)
        for hh in range(FOX_HEADS):
            dsum = dsum + jnp.where(lane == hh, dq_ref[:, hh * FOX_DIM:hh * FOX_DIM + 1], 0.0)
        d_s[...] = dsum
        r = lax.broadcasted_iota(jnp.int32, (tb, tb), 0)
        c = lax.broadcasted_iota(jnp.int32, (tb, tb), 1)
        tri = jnp.where(c >= r, 1.0, 0.0).astype(BF)
        carry = jnp.zeros((1, 128), F32)
        db = jnp.zeros((1, 128), F32)
        for i in reversed(range(t_tok // tb)):
            hi, mid, lo = _split3(d_s[i * tb:(i + 1) * tb, :])
            dlf = _dot(tri, hi) + _dot(tri, mid) + _dot(tri, lo) + carry
            carry = dlf[0:1, :]
            z = ff_ref[i * tb:(i + 1) * tb, :] + b_ref[...]
            dff = dlf * _sigmoid(-z)
            dff_ref[i * tb:(i + 1) * tb, :] = dff.astype(BF)
            db = db + jnp.sum(dff, axis=0, keepdims=True)
        db_ref[...] = db

    return _pcall(
        body, [dcum_t, dcum_q, ffl, b_pad], name="forget_bwd",
        out_shape=[jax.ShapeDtypeStruct((t_tok, 128), BF), jax.ShapeDtypeStruct((1, 128), F32)],
        scratch=[pltpu.VMEM((128, t_tok), F32), pltpu.VMEM((t_tok, 128), F32)])


def _first_half():
    return lax.broadcasted_iota(jnp.int32, (1, 128), 1) < FOX_DIM


def _head_rows(x2, a2, hh):
    return jnp.where(_first_half(), x2, a2) if hh == 0 else jnp.where(_first_half(), a2, x2)


def _head_only(x2, hh):
    zero = jnp.zeros_like(x2)
    return jnp.where(_first_half(), x2, zero) if hh == 0 else jnp.where(_first_half(), zero, x2)


def _causal_diag(s):
    rows = lax.broadcasted_iota(jnp.int32, s.shape, 0)
    cols = lax.broadcasted_iota(jnp.int32, s.shape, 1)
    return jnp.where(cols <= rows, s, NEG)


def _diag_or_below(qi, ki, step):
    pl.when(ki < qi)(lambda: step(False))
    pl.when(ki == qi)(lambda: step(True))


def _tri_rows(s, n):
    qi = sum((s >= r * (r + 1) // 2).astype(jnp.int32) for r in range(1, n))
    return qi, s - (qi * (qi + 1)) // 2


def _tri_cols(s, n):
    ki = sum((s >= k * n - k * (k - 1) // 2).astype(jnp.int32) for k in range(1, n))
    return ki, ki + s - (ki * n - (ki * (ki - 1)) // 2)


def _fox_fwd(fq, fk, fv, aq, ak, comm=None):
    t_tok = fq.shape[0]
    t = _tile(t_tok, 512)
    nq = t_tok // t
    npair = FOX_HEADS // 2

    def body(q_ref, k_ref, v_ref, aq_ref, ak_ref, o_ref, of_ref, aqb_ref, m_s, l_s, acc_s):
        qi, ki = _tri_rows(pl.program_id(1), nq)

        @pl.when(ki == 0)
        def _():
            m_s[...] = jnp.full_like(m_s, NEG)
            l_s[...] = jnp.zeros_like(l_s)
            acc_s[...] = jnp.zeros_like(acc_s)

        def step(diag):
            q2, k2, v2, aq2, ak2 = q_ref[...], k_ref[...], v_ref[...], aq_ref[...], ak_ref[...]
            for hh in range(2):
                s = _dot_nt(_head_rows(q2, aq2, hh), _head_rows(k2, ak2, hh))
                if diag:
                    s = _causal_diag(s)
                m_prev = m_s[hh]
                m_new = jnp.maximum(m_prev, jnp.max(s, axis=1, keepdims=True))
                alpha = jnp.exp(m_prev - m_new)
                p = jnp.exp(s - jnp.tile(m_new, (1, t // 128)))
                l_s[hh] = alpha * l_s[hh] + jnp.sum(p, axis=1, keepdims=True)
                acc_s[hh] = alpha * acc_s[hh] + _dot(p.astype(BF), v2)
                m_s[hh] = m_new

        _diag_or_below(qi, ki, step)

        @pl.when(ki == qi)
        def _():
            first = _first_half()
            o = jnp.where(first, acc_s[0] / l_s[0], acc_s[1] / l_s[1])
            o_ref[...] = o.astype(BF)
            of_ref[...] = o
            other = jnp.where(first, m_s[1] + jnp.log(l_s[1]), m_s[0] + jnp.log(l_s[0]))
            aqb_ref[...] = _aug_put(aq_ref[...], 6, _split3(-other))

    qs = pl.BlockSpec((t, 128), lambda p, s: (_tri_rows(s, nq)[0], p))
    ks = pl.BlockSpec((t, 128), lambda p, s: (_tri_rows(s, nq)[1], p))
    stat = pltpu.VMEM((2, t, 128), F32)
    return _pcall(
        body, [fq, fk, fv, aq, ak], name="fox_fwd", grid=(npair, nq * (nq + 1) // 2),
        out_shape=[jax.ShapeDtypeStruct((t_tok, FOX_WIDTH), BF), jax.ShapeDtypeStruct((t_tok, FOX_WIDTH), F32),
                   jax.ShapeDtypeStruct((t_tok, FOX_WIDTH), BF)],
        in_specs=[qs, ks, ks, qs, ks], out_specs=[qs, qs, qs], scratch=[stat, stat, stat], comm=comm)


def _fox_ds(q2, k2, v2, do2, aq2, ak2, ad2, hh, diag):
    s = _dot_nt(_head_rows(q2, aq2, hh), _head_rows(k2, ak2, hh))
    if diag:
        s = _causal_diag(s)
    p = jnp.exp(s)
    av = jnp.where(_aug_lane() < 3, 1.0, 0.0).astype(BF)
    dp = _dot_nt(_head_rows(do2, ad2, hh), _head_rows(v2, jnp.broadcast_to(av, v2.shape), hh))
    return p, p * dp


def _fox_bwd(fq, fk, fv, do, aqb, ak, ad, comm=None):
    t_tok = fq.shape[0]
    t = _tile(t_tok, 512)
    nq = t_tok // t
    npair = FOX_HEADS // 2
    n_steps = nq * (nq + 1) // 2

    def body(q_ref, k_ref, v_ref, do_ref, aq_ref, ak_ref, ad_ref, dq_ref, dk_ref, dv_ref, dck_ref, dcq_ref,
             dk_s, dv_s, dq_s, rs_s):
        step_id = pl.program_id(1)
        ki, qi = _tri_cols(step_id, nq)

        @pl.when(step_id == 0)
        def _():
            dq_s[...] = jnp.zeros_like(dq_s)
            rs_s[...] = jnp.zeros_like(rs_s)

        @pl.when(qi == ki)
        def _():
            dk_s[...] = jnp.zeros_like(dk_s)
            dv_s[...] = jnp.zeros_like(dv_s)
            dck_ref[...] = jnp.zeros_like(dck_ref)

        rows = pl.ds(qi * t if isinstance(qi, int) else pl.multiple_of(qi * t, t), t)

        def step(diag):
            q2, k2, v2, do2 = q_ref[...], k_ref[...], v_ref[...], do_ref[...]
            dq = []
            for hh in range(2):
                p, ds = _fox_ds(q2, k2, v2, do2, aq_ref[...], ak_ref[...], ad_ref[...], hh, diag)
                dsb = ds.astype(BF)
                dv_s[...] += _dot_tn(p.astype(BF), _head_only(do2, hh))
                dk_s[...] += _dot_tn(dsb, _head_only(q2, hh))
                dq.append(_dot(dsb, k2))
                dck_ref[hh] = dck_ref[hh] - jnp.sum(ds, axis=0, keepdims=True)
                rs_s[hh, rows, :] = rs_s[hh, rows, :] + jnp.sum(ds, axis=1, keepdims=True)
            dq_s[rows, :] = dq_s[rows, :] + jnp.where(_first_half(), dq[0], dq[1])

        _diag_or_below(qi, ki, step)

        @pl.when(qi == nq - 1)
        def _():
            dk_ref[...] = dk_s[...].astype(BF)
            dv_ref[...] = dv_s[...].astype(BF)

        @pl.when(step_id == n_steps - 1)
        def _():
            dq_ref[...] = (dq_s[...] * FOX_SCALE).astype(BF)
            dcq_ref[...] = jnp.where(_first_half(), rs_s[0], rs_s[1])

    qs = pl.BlockSpec((t, 128), lambda p, s: (_tri_cols(s, nq)[1], p))
    ks = pl.BlockSpec((t, 128), lambda p, s: (_tri_cols(s, nq)[0], p))
    cks = pl.BlockSpec((2, 1, t), lambda p, s: (p, 0, _tri_cols(s, nq)[0]))
    seq = pl.BlockSpec((t_tok, 128), lambda p, s: (0, p))
    sds = jax.ShapeDtypeStruct((t_tok, FOX_WIDTH), BF)
    return _pcall(
        body, [fq, fk, fv, do, aqb, ak, ad], name="fox_bwd", grid=(npair, n_steps),
        out_shape=[sds, sds, sds, jax.ShapeDtypeStruct((FOX_HEADS, 1, t_tok), F32),
                   jax.ShapeDtypeStruct((t_tok, FOX_WIDTH), F32)],
        in_specs=[qs, ks, ks, qs, qs, ks, qs], out_specs=[seq, ks, ks, cks, seq],
        scratch=[pltpu.VMEM((t, 128), F32), pltpu.VMEM((t, 128), F32), pltpu.VMEM((t_tok, 128), F32),
                 pltpu.VMEM((2, t_tok, 128), F32)], comm=comm)


def _ret_consts():
    c = RET_CHUNK
    log_gamma = jnp.log1p(-jnp.exp2(-5.0 - jnp.arange(RET_HEADS, dtype=F32)))
    idx = jnp.arange(c, dtype=F32)
    diff = idx[:, None] - idx[None, :]
    dmask = jnp.where(diff >= 0, jnp.exp(log_gamma[:, None, None] * jnp.maximum(diff, 0.0)), 0.0)
    qdec = jnp.exp(log_gamma[:, None] * (idx + 1.0))
    kdec = jnp.exp(log_gamma[:, None] * (c - 1 - idx))
    cdec = jnp.exp(log_gamma * c)
    bc = lambda v: jnp.broadcast_to(v[:, :, None], (RET_HEADS, c, RET_DIM))
    return dmask, bc(qdec), bc(kdec), jnp.broadcast_to(cdec[:, None, None], (RET_HEADS, c, RET_DIM))


def _group_norm(y):
    mu = jnp.mean(y, axis=-1, keepdims=True)
    yc = y - mu
    r = lax.rsqrt(jnp.mean(yc * yc, axis=-1, keepdims=True) + EPS)
    return yc * r, r


def _ret_fwd(rq, rk, rv, rg, consts, comm=None):
    t_tok = rq.shape[0]
    nb = 4 if t_tok % (4 * RET_CHUNK) == 0 else 1
    tr = nb * RET_CHUNK
    n_steps = t_tok // tr
    c = RET_CHUNK

    def body(q_ref, k_ref, v_ref, g_ref, dm_ref, qd_ref, kd_ref, cd_ref, y_ref, yo_ref, st_ref, s_s):
        @pl.when(pl.program_id(0) == 0)
        def _():
            s_s[...] = jnp.zeros_like(s_s)

        for b in range(nb):
            rows = slice(b * c, (b + 1) * c)
            for hh in range(RET_HEADS):
                cols = slice(hh * RET_DIM, (hh + 1) * RET_DIM)
                q, k, v = q_ref[rows, cols], k_ref[rows, cols], v_ref[rows, cols]
                state = s_s[hh]
                st_ref[hh, b] = state
                sc = (_dot_nt(q, k) * dm_ref[hh]).astype(BF)
                y = _dot(sc, v) + _dot((q.astype(F32) * qd_ref[hh]).astype(BF), state.astype(BF))
                s_s[hh] = cd_ref[hh] * state + _dot_tn((k.astype(F32) * kd_ref[hh]).astype(BF), v)
                y_ref[rows, cols] = y
                yn, _ = _group_norm(y)
                gate = g_ref[rows, cols].astype(F32)
                yo_ref[rows, cols] = (yn * (gate * _sigmoid(gate))).astype(BF)

    blk = pl.BlockSpec((tr, RET_WIDTH), lambda i: (i, 0))
    cst = pl.BlockSpec((RET_HEADS, c, RET_DIM), lambda i: (0, 0, 0))
    return _pcall(
        body, [rq, rk, rv, rg, *consts], name="ret_fwd", grid=(n_steps,),
        out_shape=[jax.ShapeDtypeStruct((t_tok, RET_WIDTH), F32), jax.ShapeDtypeStruct((t_tok, RET_WIDTH), BF),
                   jax.ShapeDtypeStruct((RET_HEADS, t_tok // c, RET_DIM, RET_DIM), F32)],
        in_specs=[blk] * 4 + [cst] * 4,
        out_specs=[blk, blk, pl.BlockSpec((RET_HEADS, nb, RET_DIM, RET_DIM), lambda i: (0, i, 0, 0))],
        scratch=[pltpu.VMEM((RET_HEADS, RET_DIM, RET_DIM), F32)], comm=comm)


def _ret_bwd(rq, rk, rv, rg, y_raw, dyo, states, consts, cos_t, sin_t, comm=None):
    t_tok = rq.shape[0]
    nb = 4 if t_tok % (4 * RET_CHUNK) == 0 else 1
    tr = nb * RET_CHUNK
    n_steps = t_tok // tr
    c = RET_CHUNK

    def body(q_ref, k_ref, v_ref, g_ref, y_ref, dyo_ref, st_ref, dm_ref, qd_ref, kd_ref, cd_ref,
             cos_ref, sin_ref, dq_ref, dk_ref, dv_ref, dg_ref, ds_s):
        @pl.when(pl.program_id(0) == 0)
        def _():
            ds_s[...] = jnp.zeros_like(ds_s)

        for b in reversed(range(nb)):
            rows = slice(b * c, (b + 1) * c)
            cosv, sinv = cos_ref[rows, :], sin_ref[rows, :]
            for hh in range(RET_HEADS):
                cols = slice(hh * RET_DIM, (hh + 1) * RET_DIM)
                dm, qd, kd, cd = dm_ref[hh], qd_ref[hh], kd_ref[hh], cd_ref[hh]
                q, k, v = q_ref[rows, cols], k_ref[rows, cols], v_ref[rows, cols]
                yn, r = _group_norm(y_ref[rows, cols])
                gate = g_ref[rows, cols].astype(F32)
                sg = _sigmoid(gate)
                dyo = dyo_ref[rows, cols]
                dg_ref[rows, cols] = (dyo * yn * (sg * (1.0 + gate * (1.0 - sg)))).astype(BF)
                dyn = dyo * (gate * sg)
                dy = r * (dyn - jnp.mean(dyn, axis=-1, keepdims=True)
                          - yn * jnp.mean(dyn * yn, axis=-1, keepdims=True))
                dyb = dy.astype(BF)
                state_b = st_ref[hh, b].astype(BF)
                dstate = ds_s[hh]
                dstate_b = dstate.astype(BF)
                qdb = (q.astype(F32) * qd).astype(BF)
                kdb = (k.astype(F32) * kd).astype(BF)
                sc = (_dot_nt(q, k) * dm).astype(BF)
                dv = _dot_tn(sc, dyb) + _dot(kdb, dstate_b)
                dp = (_dot_nt(dyb, v) * dm).astype(BF)
                dq = _dot(dp, k) + _dot_nt(dyb, state_b) * qd
                dk = (_dot_tn(dp, q) + _dot_nt(v, dstate_b) * kd) * RET_SCALE
                ds_s[hh] = cd * dstate + _dot_tn(qdb, dyb)
                dv_ref[rows, cols] = dv.astype(BF)
                dq_ref[rows, cols] = (dq * cosv - _swap_pairs(dq) * sinv).astype(BF)
                dk_ref[rows, cols] = (dk * cosv - _swap_pairs(dk) * sinv).astype(BF)

    rev = lambda i: n_steps - 1 - i
    blk = pl.BlockSpec((tr, RET_WIDTH), lambda i: (rev(i), 0))
    tab = pl.BlockSpec((tr, RET_DIM), lambda i: (rev(i), 0))
    cst = pl.BlockSpec((RET_HEADS, c, RET_DIM), lambda i: (0, 0, 0))
    sds = jax.ShapeDtypeStruct((t_tok, RET_WIDTH), BF)
    return _pcall(
        body, [rq, rk, rv, rg, y_raw, dyo, states, *consts, cos_t, sin_t], name="ret_bwd",
        grid=(n_steps,), out_shape=[sds] * 4,
        in_specs=[blk] * 6 + [pl.BlockSpec((RET_HEADS, nb, RET_DIM, RET_DIM), lambda i: (0, rev(i), 0, 0))]
        + [cst] * 4 + [tab, tab],
        out_specs=[blk] * 4, scratch=[pltpu.VMEM((RET_HEADS, RET_DIM, RET_DIM), F32)], comm=comm)


def _mix_out(h, y_ret, y_fox, ga, gb, wr4, wf4, wo4, comm=None):
    t_tok, d = h.shape
    cz = wr4.shape[-1]
    ro = wo4.shape[-2]
    tm = _tile(t_tok, 512)

    def body(h_ref, yr_ref, yf_ref, ga_ref, gb_ref, wr_ref, wf_ref, wo_ref, ho_ref, za_ref, zb_ref, mix_ref):
        yr, yf = yr_ref[...], yf_ref[...]
        for j in range(N_CHIPS):
            sl = slice(j * cz, (j + 1) * cz)
            za = _dot(yr, wr_ref[j])
            zb = _dot(yf, wf_ref[j])
            za_ref[:, sl] = za.astype(BF)
            zb_ref[:, sl] = zb.astype(BF)
            mix_ref[:, sl] = (ga_ref[:, sl].astype(F32) * za + gb_ref[:, sl].astype(F32) * zb).astype(BF)
        acc = h_ref[...]
        for j in range(N_CHIPS):
            acc = acc + _dot(mix_ref[:, j * ro:(j + 1) * ro], wo_ref[j])
        ho_ref[...] = acc

    row = lambda c: pl.BlockSpec((tm, c), lambda i: (i, 0))
    full = lambda *s: pl.BlockSpec(s, lambda i: (0,) * len(s))
    sds = lambda dt: jax.ShapeDtypeStruct((t_tok, d), dt)
    return _pcall(
        body, [h, y_ret, y_fox, ga, gb, wr4, wf4, wo4], name="mix_out", grid=(t_tok // tm,),
        out_shape=[sds(F32), sds(BF), sds(BF), sds(BF)],
        in_specs=[row(d), row(RET_WIDTH), row(FOX_WIDTH), row(d), row(d),
                  full(N_CHIPS, RET_WIDTH, cz), full(N_CHIPS, FOX_WIDTH, cz), full(N_CHIPS, ro, d)],
        out_specs=[row(d)] * 4, comm=comm)


def _mix_out_bwd(dh, za, zb, ga, gb, y_fox, wr4, wf4, wo4, comm=None):
    t_tok, d = dh.shape
    cz = wr4.shape[-1]
    ro = wo4.shape[-2]
    tm = _tile(t_tok, 256)

    def body(dh_ref, za_ref, zb_ref, ga_ref, gb_ref, yf_ref, wr_ref, wf_ref, wo_ref,
             dhb_ref, dgp_ref, dza_ref, dzb_ref, dyr_ref, dyf_ref, dl_ref, db_ref):
        @pl.when(pl.program_id(0) == 0)
        def _():
            db_ref[...] = jnp.zeros_like(db_ref)

        dhb = dh_ref[...].astype(BF)
        dhb_ref[...] = dhb
        dyr = jnp.zeros((tm, RET_WIDTH), F32)
        dyf = jnp.zeros((tm, FOX_WIDTH), F32)
        for j in range(N_CHIPS):
            sl = slice(j * ro, (j + 1) * ro)
            dmix = _dot_nt(dhb, wo_ref[j])
            ga, gb = ga_ref[:, sl].astype(F32), gb_ref[:, sl].astype(F32)
            dza = (dmix * ga).astype(BF)
            dzb = (dmix * gb).astype(BF)
            dza_ref[:, sl] = dza
            dzb_ref[:, sl] = dzb
            dga = dmix * za_ref[:, sl].astype(F32) * ga * (1.0 - ga)
            dgb = dmix * zb_ref[:, sl].astype(F32) * gb * (1.0 - gb)
            dgp_ref[:, sl] = dga.astype(BF)
            dgp_ref[:, d + j * ro:d + (j + 1) * ro] = dgb.astype(BF)
            db_ref[:, sl] += jnp.sum(dga, axis=0, keepdims=True)
            db_ref[:, d + j * ro:d + (j + 1) * ro] += jnp.sum(dgb, axis=0, keepdims=True)
        for j in range(N_CHIPS):
            sl = slice(j * cz, (j + 1) * cz)
            dyr = dyr + _dot_nt(dza_ref[:, sl], wr_ref[j])
            dyf = dyf + _dot_nt(dzb_ref[:, sl], wf_ref[j])
        dyr_ref[...] = dyr
        dyfb = dyf.astype(BF)
        dyf_ref[...] = dyfb
        prod = dyfb.astype(F32) * yf_ref[...]
        first = _first_half()
        for pp in range(FOX_HEADS // 2):
            blk = prod[:, pp * 128:(pp + 1) * 128]
            s0 = jnp.sum(jnp.where(first, blk, 0.0), axis=1, keepdims=True)
            s1 = jnp.sum(jnp.where(first, 0.0, blk), axis=1, keepdims=True)
            parts = _split3(-jnp.where(first, s1, s0))
            dl_ref[:, pp * 128:(pp + 1) * 128] = _aug_put(jnp.zeros((tm, 128), BF), 0, parts)

    row = lambda c: pl.BlockSpec((tm, c), lambda i: (i, 0))
    full = lambda *s: pl.BlockSpec(s, lambda i: (0,) * len(s))
    sds = lambda c, dt: jax.ShapeDtypeStruct((t_tok, c), dt)
    return _pcall(
        body, [dh, za, zb, ga, gb, y_fox, wr4, wf4, wo4], name="mix_out_bwd", grid=(t_tok // tm,),
        out_shape=[sds(d, BF), sds(2 * d, BF), sds(d, BF), sds(d, BF), sds(RET_WIDTH, F32),
                   sds(FOX_WIDTH, BF), sds(FOX_WIDTH, BF), jax.ShapeDtypeStruct((1, 2 * d), F32)],
        in_specs=[row(d)] * 5 + [row(FOX_WIDTH), full(N_CHIPS, RET_WIDTH, cz), full(N_CHIPS, FOX_WIDTH, cz),
                                 full(N_CHIPS, ro, d)],
        out_specs=[row(d), row(2 * d), row(d), row(d), row(RET_WIDTH), row(FOX_WIDTH), row(FOX_WIDTH),
                   full(1, 2 * d)],
        comm=comm)


def _mix_in_bwd(dh, h, ln, parts, dff, dgpre, w_in, wm4, comm=None):
    t_tok, d = h.shape
    cm = wm4.shape[-1]
    tm = _tile(t_tok, 256)

    def body(dh_ref, h_ref, ln_ref, p0, p1, p2, p3, p4, p5, p6, dff_ref, dgp_ref, win_ref, wm_ref,
             dhi_ref, dln_ref, dproj_ref):
        @pl.when(pl.program_id(0) == 0)
        def _():
            dln_ref[...] = jnp.zeros_like(dln_ref)

        for k, pr in enumerate((p0, p1, p2, p3, p4, p5, p6)):
            dproj_ref[:, k * 512:(k + 1) * 512] = pr[...]
        dproj_ref[:, FF_COL:FF_COL + 128] = dff_ref[...]
        dproj_ref[:, FF_COL + 128:] = jnp.zeros((tm, IN_PAD - FF_COL - 128), BF)
        du = _dot(dproj_ref[...], win_ref[...])
        for j in range(N_CHIPS):
            du = du + _dot_nt(dgp_ref[:, j * cm:(j + 1) * cm], wm_ref[j])
        xv = h_ref[...]
        dx, dln = _rms_bwd(du, xv, _rstd(xv), ln_ref[...])
        dln_ref[...] += dln
        dhi_ref[...] = dh_ref[...] + dx

    row = lambda c: pl.BlockSpec((tm, c), lambda i: (i, 0))
    full = lambda *s: pl.BlockSpec(s, lambda i: (0,) * len(s))
    return _pcall(
        body, [dh, h, ln, *parts, dff, dgpre, w_in, wm4], name="mix_in_bwd", grid=(t_tok // tm,),
        out_shape=[jax.ShapeDtypeStruct((t_tok, d), F32), jax.ShapeDtypeStruct((1, d), F32),
                   jax.ShapeDtypeStruct((t_tok, IN_PAD), BF)],
        in_specs=[row(d), row(d), full(1, d)] + [row(512)] * 7 + [row(128), row(2 * d), full(IN_PAD, d),
                                                                   full(N_CHIPS, d, cm)],
        out_specs=[row(d), full(1, d), row(IN_PAD)], comm=comm)


def _tail(h, p, target, ln_ple, ln_fin, wpg4, wpl4, comm=None):
    t_tok, d = h.shape
    pd = p.shape[1]
    rg = wpg4.shape[-2]
    cp = wpl4.shape[-1]
    tm = _tile(t_tok, 256)

    def body(h_ref, p_ref, t_ref, lp_ref, lf_ref, wg_ref, wp_ref,
             dh_ref, n_ref, dgp_ref, dpe_ref, pb_ref, loss_ref, dlf_ref, dlp_ref, pe_s, dn_s):
        @pl.when(pl.program_id(0) == 0)
        def _():
            loss_ref[...] = jnp.zeros_like(loss_ref)
            dlf_ref[...] = jnp.zeros_like(dlf_ref)
            dlp_ref[...] = jnp.zeros_like(dlp_ref)

        xv = h_ref[...]
        r3 = _rstd(xv)
        nb = (xv * r3 * lp_ref[...]).astype(BF)
        n_ref[...] = nb
        pb = p_ref[...].astype(BF)
        pb_ref[...] = pb
        pgpre = jnp.zeros((tm, d), F32)
        for j in range(N_CHIPS):
            pgpre = pgpre + _dot(nb[:, j * rg:(j + 1) * rg], wg_ref[j])
            pe_s[:, j * cp:(j + 1) * cp] = _dot(pb, wp_ref[j])
        pg = _sigmoid(pgpre)
        pe = pe_s[...]
        h4 = xv + pg * pe
        r4 = _rstd(h4)
        err = h4 * r4 * lf_ref[...] - t_ref[...]
        loss_ref[...] += 0.5 * jnp.sum(jnp.sum(err * err, axis=1, keepdims=True), axis=0, keepdims=True) / d
        dh4, dlf = _rms_bwd(err * (1.0 / d), h4, r4, lf_ref[...])
        dlf_ref[...] += dlf
        dpe_ref[...] = (dh4 * pg).astype(BF)
        dgp = (dh4 * pe * pg * (1.0 - pg)).astype(BF)
        dgp_ref[...] = dgp
        for j in range(N_CHIPS):
            dn_s[:, j * rg:(j + 1) * rg] = _dot_nt(dgp, wg_ref[j])
        dx, dlp = _rms_bwd(dn_s[...], xv, r3, lp_ref[...])
        dlp_ref[...] += dlp
        dh_ref[...] = dh4 + dx

    row = lambda c: pl.BlockSpec((tm, c), lambda i: (i, 0))
    full = lambda *s: pl.BlockSpec(s, lambda i: (0,) * len(s))
    sds = lambda c, dt: jax.ShapeDtypeStruct((t_tok, c), dt)
    vec = jax.ShapeDtypeStruct((1, d), F32)
    return _pcall(
        body, [h, p, target, ln_ple, ln_fin, wpg4, wpl4], name="tail", grid=(t_tok // tm,),
        out_shape=[sds(d, F32), sds(d, BF), sds(d, BF), sds(d, BF), sds(pd, BF),
                   jax.ShapeDtypeStruct((1, 128), F32), vec, vec],
        in_specs=[row(d), row(pd), row(d), full(1, d), full(1, d), full(N_CHIPS, rg, d), full(N_CHIPS, pd, cp)],
        out_specs=[row(d), row(d), row(d), row(d), row(pd), full(1, 128), full(1, d), full(1, d)],
        scratch=[pltpu.VMEM((tm, d), F32), pltpu.VMEM((tm, d), F32)], comm=comm)


BIG = ["w_ffn1_gate", "w_ffn1_up", "w_ffn1_down", "w_in", "w_merge", "w_ret_out", "w_fox_out", "w_out",
       "w_ffn2_gate", "w_ffn2_up", "w_ffn2_down", "w_ple", "w_ple_gate"]
SMALL = ["ln_ffn1", "ln_mix", "b_forget", "b_merge", "ln_ffn2", "ln_ple", "ln_final"]
WEIGHTS = ["ln_ffn1", "w_ffn1_gate", "w_ffn1_up", "w_ffn1_down", "ln_mix", "w_in", "b_forget", "w_merge", "b_merge",
           "w_ret_out", "w_fox_out", "w_out", "ln_ffn2", "w_ffn2_gate", "w_ffn2_up", "w_ffn2_down", "ln_ple",
           "w_ple", "w_ple_gate", "ln_final"]


TRANSPOSED = {"w_ffn1_gate", "w_ffn1_up", "w_ffn2_gate", "w_ffn2_up", "w_in"}
IN_ROWS_PAD = -(-(IN_COLS // N_CHIPS) // 32) * 32


def _pack_small(vals, loss_row):
    rows = [loss_row]
    for name in SMALL:
        v = vals[name].reshape(-1)
        n = -(-v.shape[0] // 128) * 128
        rows.append(jnp.pad(v, (0, n - v.shape[0])).reshape(n // 128, 128))
    packed = jnp.concatenate(rows, axis=0)
    pad = -packed.shape[0] % 8
    return jnp.pad(packed, ((0, pad), (0, 0)))


def _unpack_small(packed, sizes):
    out, r = {}, 1
    for name in SMALL:
        n = sizes[name]
        nr = -(-n // 128)
        out[name] = packed[r:r + nr].reshape(1, nr * 128)[:, :n]
        r += nr
    return out


class _Stage:
    def __init__(self, comm, finish):
        self.comm, self.finish, self.result = comm, finish, None


def _hosted(fn, *a, stages=()):
    if not stages:
        return fn(*a)
    outs, couts = fn(*a, comm=_merge([st.comm for st in stages]))
    for st, o in zip(stages, _split_outs([st.comm for st in stages], couts)):
        st.result = st.finish(o)
    return outs


class _Reducer:
    def __init__(self):
        self.done = {}

    def swap(self, grads):
        names = list(grads)
        return _Stage(_c_half_swap([grads[n] for n in names]),
                      lambda outs: {n: _add_halves(grads[n], o) for n, o in zip(names, outs)})

    def exchange(self, parts):
        names = list(parts)
        return _Stage(_c_chip_exchange([parts[n] for n in names]),
                      lambda outs: {n: _sum_chips(parts[n], o) for n, o in zip(names, outs)})

    def join(self, halves):
        names = list(halves)
        return _Stage(_c_join([halves[n] for n in names]),
                      lambda outs: self.done.update({n: (halves[n], o) for n, o in zip(names, outs)}))


def kernel(x, p, positions, ln_ffn1, w_ffn1_gate, w_ffn1_up, w_ffn1_down, ln_mix, w_in, b_forget, w_merge, b_merge, w_ret_out, w_fox_out, w_out, ln_ffn2, w_ffn2_gate, w_ffn2_up, w_ffn2_down, ln_ple, w_ple, w_ple_gate, ln_final, loss_target, m_ln_ffn1, m_w_ffn1_gate, m_w_ffn1_up, m_w_ffn1_down, m_ln_mix, m_w_in, m_b_forget, m_w_merge, m_b_merge, m_w_ret_out, m_w_fox_out, m_w_out, m_ln_ffn2, m_w_ffn2_gate, m_w_ffn2_up, m_w_ffn2_down, m_ln_ple, m_w_ple, m_w_ple_gate, m_ln_final, v_ln_ffn1, v_w_ffn1_gate, v_w_ffn1_up, v_w_ffn1_down, v_ln_mix, v_w_in, v_b_forget, v_w_merge, v_b_merge, v_w_ret_out, v_w_fox_out, v_w_out, v_ln_ffn2, v_w_ffn2_gate, v_w_ffn2_up, v_w_ffn2_down, v_ln_ple, v_w_ple, v_w_ple_gate, v_ln_final):
    args = dict(locals())
    w = {n: args[n] for n in WEIGHTS}
    m = {n: args["m_" + n] for n in WEIGHTS}
    v = {n: args["v_" + n] for n in WEIGHTS}
    d = x.shape[-1]
    t_tok = x.shape[1]
    xs, ps, target = x[0], p[0, 0], loss_target[0]
    small = {n: w[n].reshape(1, -1) for n in SMALL}

    def to2d(n, a):
        if n in TRANSPOSED:
            return a[0].T
        return a.reshape(a.shape[-2], a.shape[-1]) if a.ndim == 3 else a.reshape(1, -1)

    def from2d(n, a):
        return a.T[None] if n in TRANSPOSED else a.reshape(w[n].shape)

    def padded(n, a):
        return jnp.pad(a, ((0, IN_ROWS_PAD - a.shape[0]), (0, 0))) if n == "w_in" else a

    core = lax.axis_index("c")
    me = 2 * lax.axis_index("x") + lax.axis_index("y")
    shard = {}
    for n in BIG:
        s2 = padded(n, to2d(n, w[n]).astype(BF))
        shard[n] = s2.reshape(1, 2, s2.shape[0] // 2, s2.shape[1])
    full = {}

    def gather(names):
        bufs = [lax.dynamic_update_slice(jnp.zeros((N_CHIPS,) + shard[n].shape[1:], BF), shard[n], (me, 0, 0, 0))
                for n in names]

        def finish(outs):
            full.update({n: o.reshape(N_CHIPS, 2 * o.shape[2], o.shape[3]) for n, o in zip(names, outs)})

        return _Stage(_c_all_gather(bufs), finish)

    half = RET_DIM // 2
    inv_freq = 1.0 / (ROPE_BASE ** (jnp.arange(half, dtype=F32) / half))
    cos_t, sin_t = _hosted(_rope_tables, positions[0].astype(F32).reshape(t_tok, 1),
                           jnp.repeat(inv_freq, 2).reshape(1, RET_DIM),
                           stages=[gather(["w_ffn1_gate", "w_ffn1_up", "w_ffn1_down"])])
    consts = _ret_consts()
    b_pad = jnp.pad(small["b_forget"], ((0, 0), (0, 128 - FOX_HEADS)))

    h1, n1, g1, u1 = _hosted(
        _ffn_fwd, xs, small["ln_ffn1"], full["w_ffn1_gate"], full["w_ffn1_up"], full["w_ffn1_down"],
        stages=[gather(["w_in", "w_merge", "w_ret_out", "w_fox_out", "w_out", "w_ple_gate", "w_ple"])])
    w_in_full = jnp.pad(full["w_in"][:, :IN_COLS // N_CHIPS].reshape(IN_COLS, d), ((0, IN_PAD - IN_COLS), (0, 0)))
    u, rq, rk, rv, rg, fq, fk, fv, ffl, ga, gb = _mix_in(
        h1, small["ln_mix"], w_in_full, full["w_merge"], small["b_merge"], cos_t, sin_t)
    aq, ak = _forget_fwd(ffl, b_pad)
    y_raw, y_ret, states = _ret_fwd(rq, rk, rv, rg, consts)
    y_fox, y_fox32, aqb = _hosted(_fox_fwd, fq, fk, fv, aq, ak,
                                  stages=[gather(["w_ffn2_gate", "w_ffn2_up", "w_ffn2_down"])])
    h2, za, zb, mix = _mix_out(h1, y_ret, y_fox, ga, gb, full["w_ret_out"], full["w_fox_out"], full["w_out"])
    h3, n2, g2, u2 = _ffn_fwd(h2, small["ln_ffn2"], full["w_ffn2_gate"], full["w_ffn2_up"], full["w_ffn2_down"])

    red = _Reducer()
    dh3, n3, dpgpre, dpe, pb, loss, dln_final, dln_ple = _tail(
        h3, ps, target, small["ln_ple"], small["ln_final"], full["w_ple_gate"], full["w_ple"])
    g_f2 = dict(w_ple_gate=_wgrad_rows("wgrad_ple_gate", n3, dpgpre, N_CHIPS),
                w_ple=_wgrad_cols("wgrad_ple", pb, dpe, N_CHIPS))
    dh2, dln_ffn2, dg2, du2, a2, dhb3 = _ffn_bwd(
        dh3, h2, small["ln_ffn2"], g2, u2, full["w_ffn2_gate"], full["w_ffn2_up"], full["w_ffn2_down"])
    g_f2["w_ffn2_gate"] = _wgrad_b_shared("wgrad_ffn2_gate", dg2, n2)
    g_f2["w_ffn2_up"] = _wgrad_b_shared("wgrad_ffn2_up", du2, n2)
    g_f2["w_ffn2_down"] = _wgrad_b_shared("wgrad_ffn2_down", a2, dhb3)

    sw_f2 = red.swap(g_f2)
    dhb2, dgpre, dza, dzb, dy_ret, dy_fox, ad, db_merge = _hosted(
        _mix_out_bwd, dh2, za, zb, ga, gb, y_fox32, full["w_ret_out"], full["w_fox_out"], full["w_out"],
        stages=[sw_f2])
    g_br = dict(w_out=_wgrad_rows("wgrad_out", mix, dhb2, N_CHIPS),
                w_ret_out=_wgrad_cols("wgrad_ret_out", y_ret, dza, N_CHIPS),
                w_fox_out=_wgrad_cols("wgrad_fox_out", y_fox, dzb, N_CHIPS))

    sw_br = red.swap(g_br)
    drq, drk, drv, drg = _hosted(_ret_bwd, rq, rk, rv, rg, y_raw, dy_ret, states, consts, cos_t, sin_t,
                                 stages=[sw_br])
    ex_f2, ex_br = red.exchange(sw_f2.result), red.exchange(sw_br.result)
    dfq, dfk, dfv, dcum_t3, dcum_q = _hosted(_fox_bwd, fq, fk, fv, dy_fox, aqb, ak, ad, stages=[ex_f2, ex_br])
    dff, db_forget = _forget_bwd(dcum_t3.reshape(FOX_HEADS, t_tok), dcum_q, ffl, b_pad)
    dh1, dln_mix, dproj = _hosted(
        _mix_in_bwd, dh2, h1, small["ln_mix"], (drq, drk, drv, drg, dfq, dfk, dfv), dff, dgpre, w_in_full,
        full["w_merge"], stages=[red.join(ex_f2.result), red.join(ex_br.result)])

    results = {}
    for names in (["w_ffn2_gate", "w_ffn2_up", "w_ffn2_down"], ["w_out", "w_ple_gate"], ["w_ret_out", "w_fox_out"],
                  ["w_ple"]):
        res = _sc_adamw_halves([(to2d(n, w[n]), *red.done[n], to2d(n, m[n]), to2d(n, v[n])) for n in names])
        for q, n in enumerate(names):
            results[n] = tuple(from2d(n, a) for a in res[4 * q:4 * q + 4])

    dx, dln_ffn1, dg1, du1, a1, dhb1 = _ffn_bwd(
        dh1, xs, small["ln_ffn1"], g1, u1, full["w_ffn1_gate"], full["w_ffn1_up"], full["w_ffn1_down"])
    g_f1g = _wgrad_b_shared("wgrad_ffn1_gate", dg1, n1)
    sw_f1g = red.swap(dict(w_ffn1_gate=g_f1g))
    g_f1u = _hosted(_wgrad_b_shared, "wgrad_ffn1_up", du1, n1, stages=[sw_f1g])
    ex_f1g, sw_f1u = red.exchange(sw_f1g.result), red.swap(dict(w_ffn1_up=g_f1u))
    g_f1d = _hosted(_wgrad_b_shared, "wgrad_ffn1_down", a1, dhb1, stages=[ex_f1g, sw_f1u])

    ex_f1u, sw_f1d = red.exchange(sw_f1u.result), red.swap(dict(w_ffn1_down=g_f1d))
    g_in = _hosted(_wgrad_rows, "wgrad_in", dproj, u, IN_PAD // 512,
                   stages=[ex_f1u, sw_f1d, red.join(ex_f1g.result)])
    g_in = g_in.reshape(IN_PAD, d)[:IN_COLS].reshape(N_CHIPS, IN_COLS // N_CHIPS, d)
    g_in = jnp.pad(g_in, ((0, 0), (0, IN_ROWS_PAD - IN_COLS // N_CHIPS), (0, 0)))
    ex_f1d, sw_in = red.exchange(sw_f1d.result), red.swap(dict(w_in=g_in))
    g_mrg = _hosted(_wgrad_cols, "wgrad_merge", u, dgpre, N_CHIPS,
                    stages=[ex_f1d, sw_in, red.join(ex_f1u.result)])

    small_grads = dict(ln_ffn1=dln_ffn1, ln_mix=dln_mix, b_forget=db_forget[:, :FOX_HEADS], b_merge=db_merge,
                       ln_ffn2=dln_ffn2, ln_ple=dln_ple, ln_final=dln_final)
    sizes = {n: w[n].size for n in SMALL}
    ex_in, sw_mrg = red.exchange(sw_in.result), red.swap(dict(w_merge=g_mrg))
    reduced = _hosted(_all_reduce_small, _pack_small(small_grads, loss),
                      stages=[ex_in, sw_mrg, red.join(ex_f1d.result)])
    gsum = _unpack_small(reduced, sizes)
    loss = reduced[0, 0]
    ex_mrg = red.exchange(sw_mrg.result)
    _hosted(_exchange_only, stages=[ex_mrg, red.join(ex_in.result)])
    _hosted(_exchange_only, stages=[red.join(ex_mrg.result)])

    def update(names, stages=()):
        w2, m2, v2 = ([to2d(n, a[n]) for n in names] for a in (w, m, v))
        n = names[0]
        if n in gsum or n == "w_in":
            if n in gsum:
                g2 = gsum[n]
            else:
                mine, other = red.done[n]
                g2 = jnp.where(core == 0, jnp.concatenate([mine, other]), jnp.concatenate([other, mine]))
                g2 = g2[:w2[0].shape[0]]
            res = [g2] + _hosted(_adamw, w2[0], g2, m2[0], v2[0], stages=stages)
        else:
            res = _hosted(_adamw_halves, [(w2[q], *red.done[names[q]], m2[q], v2[q]) for q in range(len(names))],
                          stages=stages)
        for q, name in enumerate(names):
            results[name] = tuple(from2d(name, a) for a in res[4 * q:4 * q + 4])

    update(["w_ffn1_gate", "w_ffn1_up", "w_ffn1_down"])
    for n in WEIGHTS:
        if n not in results:
            update([n])

    outs = [[results[n][k] for n in WEIGHTS] for k in range(4)]
    return (loss, dx[None], *outs[0], *outs[1], *outs[2], *outs[3])
```

```python
import functools
import operator

import jax
import jax.numpy as jnp
from jax import lax
from jax.experimental import pallas as pl
from jax.experimental.pallas import tpu as pltpu
from jax.experimental.pallas import tpu_sc as plsc

F32 = jnp.float32
BF = jnp.bfloat16
MESH = pl.DeviceIdType.MESH

EPS = 1e-6
ROPE_BASE = 10000.0
N_CHIPS = 4
RET_HEADS = 4
RET_DIM = 128
RET_WIDTH = RET_HEADS * RET_DIM
RET_CHUNK = 128
RET_SCALE = RET_DIM ** -0.5
FOX_HEADS = 8
FOX_DIM = 64
FOX_WIDTH = FOX_HEADS * FOX_DIM
FOX_SCALE = FOX_DIM ** -0.5
IN_COLS = 4 * RET_WIDTH + 3 * FOX_WIDTH + FOX_HEADS
IN_PAD = 4096
FF_COL = 4 * RET_WIDTH + 3 * FOX_WIDTH
NEG = -1e30

ADAM_LR = 0.001
ADAM_B1 = 0.9
ADAM_B2 = 0.999
ADAM_EPS = 1e-08
ADAM_WD = 0.01
ADAM_STEP = 10

VMEM_LIMIT = 52 * 1024 * 1024

NT = (((1,), (1,)), ((), ()))
TN = (((0,), (0,)), ((), ()))

HBM_SPEC = pl.BlockSpec(memory_space=pltpu.HBM)
VMEM_SPEC = pl.BlockSpec(memory_space=pltpu.VMEM)


def _dot(a, b):
    return jnp.dot(a, b, preferred_element_type=F32)


def _dot_nt(a, b):
    return lax.dot_general(a, b, NT, preferred_element_type=F32)


def _dot_tn(a, b):
    return lax.dot_general(a, b, TN, preferred_element_type=F32)


def _rstd(xv):
    return lax.rsqrt(jnp.mean(xv * xv, axis=-1, keepdims=True) + EPS)


def _rms_bwd(dn, xv, r, ln):
    xh = xv * r
    dxh = dn * ln
    dx = r * (dxh - xh * jnp.mean(dxh * xh, axis=-1, keepdims=True))
    return dx, jnp.sum(dn * xh, axis=0, keepdims=True)


def _sigmoid(x):
    return jax.nn.sigmoid(x)


def _tile(n, pref):
    return pref if n % pref == 0 else n


def _row_tile(n, cap):
    best = [t for t in range(16, min(n, cap) + 1, 16) if n % t == 0]
    return best[-1] if best else n


class _Comm:
    def __init__(self, ins, out_shapes, sems, start, wait, aliases=None):
        self.ins, self.out_shapes, self.sems, self.start, self.wait = list(ins), list(out_shapes), list(sems), start, wait
        self.aliases = dict(aliases or {})


def _merge(comms):
    comms = [c for c in comms if c is not None]
    if not comms:
        return None
    bounds, ni, no, ns = [], 0, 0, 0
    for c in comms:
        bounds.append((ni, no, ns))
        ni, no, ns = ni + len(c.ins), no + len(c.out_shapes), ns + len(c.sems)

    def run(which):
        def f(ins, outs, sems):
            for c, (i, o, s) in zip(comms, bounds):
                getattr(c, which)(ins[i:i + len(c.ins)], outs[o:o + len(c.out_shapes)], sems[s:s + len(c.sems)])
        return f

    aliases = {i + a: o + b for c, (i, o, _) in zip(comms, bounds) for a, b in c.aliases.items()}
    return _Comm([a for c in comms for a in c.ins], [a for c in comms for a in c.out_shapes],
                 [a for c in comms for a in c.sems], run("start"), run("wait"), aliases)


def _split_outs(comms, outs):
    res, o = [], 0
    for c in comms:
        if c is not None:
            res.append(list(outs[o:o + len(c.out_shapes)]))
            o += len(c.out_shapes)
    return res


def _pcall(body, args, *, name, out_shape, grid=(), in_specs=None, out_specs=None, scratch=(), comm=None,
           prefetch=()):
    many = isinstance(out_shape, (list, tuple))
    outs = list(out_shape) if many else [out_shape]
    n_pre, n_in, n_out, n_scr = len(prefetch), len(args), len(outs), len(scratch)
    if in_specs is None:
        in_specs, out_specs = [VMEM_SPEC] * n_in, [VMEM_SPEC] * n_out
    else:
        in_specs, out_specs = list(in_specs), (list(out_specs) if many else [out_specs])
    params = pltpu.CompilerParams(dimension_semantics=("arbitrary",) * len(grid), vmem_limit_bytes=VMEM_LIMIT)
    scalars = [jnp.reshape(s, (1,)).astype(jnp.int32) for s in prefetch]
    ci, co = (len(comm.ins), len(comm.out_shapes)) if comm is not None else (0, 0)

    def wrapped(*refs):
        pre, refs = refs[:n_pre], refs[n_pre:]
        a, ca = refs[:n_in], refs[n_in:n_in + ci]
        o = refs[n_in + ci:n_in + ci + n_out]
        cout = refs[n_in + ci + n_out:n_in + ci + n_out + co]
        s = refs[n_in + ci + n_out + co:n_in + ci + n_out + co + n_scr]
        csem = refs[n_in + ci + n_out + co + n_scr:]
        if comm is None:
            body(*pre, *a, *o, *s)
        elif grid:
            first = functools.reduce(operator.and_, [pl.program_id(k) == 0 for k in range(len(grid))])
            last = functools.reduce(operator.and_, [pl.program_id(k) == grid[k] - 1 for k in range(len(grid))])
            pl.when(first)(lambda: comm.start(ca, cout, csem))
            body(*pre, *a, *o, *s)
            pl.when(last)(lambda: comm.wait(ca, cout, csem))
        else:
            comm.start(ca, cout, csem)
            body(*pre, *a, *o, *s)
            comm.wait(ca, cout, csem)

    c_ins, c_outs, c_sems, aliases = ([], [], [], {}) if comm is None else (
        comm.ins, comm.out_shapes, comm.sems, {n_pre + n_in + i: n_out + o for i, o in comm.aliases.items()})
    all_in, all_out = in_specs + [HBM_SPEC] * ci, out_specs + [HBM_SPEC] * co
    all_scr = list(scratch) + c_sems
    if grid:
        args = [pltpu.with_memory_space_constraint(a, pltpu.HBM) for a in args]
    c_ins = [pltpu.with_memory_space_constraint(a, pltpu.HBM) for a in c_ins]
    if n_pre:
        spec = dict(grid_spec=pltpu.PrefetchScalarGridSpec(
            num_scalar_prefetch=n_pre, grid=grid, in_specs=all_in, out_specs=all_out, scratch_shapes=all_scr))
    else:
        spec = dict(grid=grid, in_specs=all_in, out_specs=all_out, scratch_shapes=all_scr)
    res = pl.pallas_call(wrapped, name=name, out_shape=outs + c_outs, input_output_aliases=aliases,
                         compiler_params=params, **spec)(*scalars, *args, *c_ins)
    mine = list(res[:n_out])
    mine = mine if many else mine[0]
    return mine if comm is None else (mine, list(res[n_out:]))


def _peer_chips(x, y):
    return [(1 - x, y), (x, 1 - y), (1 - x, 1 - y)]


def _c_all_gather(bufs):
    n = len(bufs)

    def copies(ins, outs, sems):
        send_sems, recv_sems, fwd_send, fwd_recv = sems
        x, y, c = lax.axis_index("x"), lax.axis_index("y"), lax.axis_index("c")
        me = 2 * x + y
        peers = _peer_chips(x, y)
        chip = [2 * px + py for px, py in peers]

        def ici(g, j, slot):
            return pltpu.make_async_remote_copy(
                src_ref=outs[g].at[me, c], dst_ref=outs[g].at[slot, c], send_sem=send_sems.at[g, j],
                recv_sem=recv_sems.at[g, j], device_id=(*peers[j], c), device_id_type=MESH)

        def d2d(g, j, half):
            return pltpu.make_async_remote_copy(
                src_ref=outs[g].at[chip[j], half], dst_ref=outs[g].at[chip[j], half], send_sem=fwd_send.at[g, j],
                recv_sem=fwd_recv.at[g, j], device_id=(x, y, 1 - c), device_id_type=MESH)

        pairs = [(g, j) for g in range(n) for j in range(3)]
        sends = [ici(g, j, me) for g, j in pairs]
        recvs = [ici(g, j, chip[j]) for g, j in pairs]
        passes = [d2d(g, j, c) for g, j in pairs]
        passed = [d2d(g, j, 1 - c) for g, j in pairs]
        return sends, recvs, passes, passed

    def start(ins, outs, sems):
        for cp in copies(ins, outs, sems)[0]:
            cp.start()

    def wait(ins, outs, sems):
        sends, recvs, passes, passed = copies(ins, outs, sems)
        for rcv, fwd in zip(recvs, passes):
            rcv.wait_recv()
            fwd.start()
        for cp in passed:
            cp.wait_recv()
        for cp in sends + passes:
            cp.wait_send()

    pair_sems = pltpu.SemaphoreType.DMA((n, 3))
    return _Comm(bufs, [jax.ShapeDtypeStruct(s.shape, s.dtype) for s in bufs], [pair_sems] * 4, start, wait,
                 aliases={g: g for g in range(n)})


def _start_wait(copies):
    def start(ins, outs, sems):
        local, sends, _ = copies(ins, outs, sems)
        for cp in local + sends:
            cp.start()

    def wait(ins, outs, sems):
        local, sends, recvs = copies(ins, outs, sems)
        for cp in recvs:
            cp.wait_recv()
        for cp in sends:
            cp.wait_send()
        for cp in local:
            cp.wait()

    return start, wait


def _c_half_swap(grads):
    n = len(grads)

    def copies(ins, outs, sems):
        send_sems, recv_sems = sems
        x, y, c = lax.axis_index("x"), lax.axis_index("y"), lax.axis_index("c")
        sends = []
        for g in range(n):
            half = ins[g].shape[1] // 2
            sends.append(pltpu.make_async_remote_copy(
                src_ref=ins[g].at[:, pl.ds((1 - c) * half, half), :], dst_ref=outs[g],
                send_sem=send_sems.at[g], recv_sem=recv_sems.at[g], device_id=(x, y, 1 - c), device_id_type=MESH))
        return [], sends, sends

    return _Comm(
        grads, [jax.ShapeDtypeStruct((N_CHIPS, s.shape[1] // 2, s.shape[2]), s.dtype) for s in grads],
        [pltpu.SemaphoreType.DMA((n,)), pltpu.SemaphoreType.DMA((n,))], *_start_wait(copies))


def _c_chip_exchange(parts):
    n = len(parts)

    def copies(ins, outs, sems):
        send_sems, recv_sems = sems
        x, y, c = lax.axis_index("x"), lax.axis_index("y"), lax.axis_index("c")
        peers = _peer_chips(x, y)

        def remote(g, j):
            return pltpu.make_async_remote_copy(
                src_ref=ins[g].at[2 * peers[j][0] + peers[j][1]], dst_ref=outs[g].at[j],
                send_sem=send_sems.at[g, j], recv_sem=recv_sems.at[g, j], device_id=(*peers[j], c),
                device_id_type=MESH)

        sends = [remote(g, j) for g in range(n) for j in range(3)]
        return [], sends, sends

    return _Comm(
        parts, [jax.ShapeDtypeStruct((3,) + s.shape[1:], s.dtype) for s in parts],
        [pltpu.SemaphoreType.DMA((n, 3)), pltpu.SemaphoreType.DMA((n, 3))], *_start_wait(copies))


def _c_join(halves):
    n = len(halves)

    def copies(ins, outs, sems):
        send_sems, recv_sems = sems
        x, y, c = lax.axis_index("x"), lax.axis_index("y"), lax.axis_index("c")
        sends = [pltpu.make_async_remote_copy(
            src_ref=ins[g], dst_ref=outs[g], send_sem=send_sems.at[g], recv_sem=recv_sems.at[g],
            device_id=(x, y, 1 - c), device_id_type=MESH) for g in range(n)]
        return [], sends, sends

    return _Comm(
        halves, [jax.ShapeDtypeStruct(s.shape, s.dtype) for s in halves],
        [pltpu.SemaphoreType.DMA((n,)), pltpu.SemaphoreType.DMA((n,))], *_start_wait(copies))


def _exchange_only(comm=None):
    def body(o_ref):
        o_ref[...] = jnp.zeros_like(o_ref)

    return _pcall(body, [], name="exchange_only", out_shape=jax.ShapeDtypeStruct((8, 128), F32), comm=comm)


def _all_reduce_small(v, comm=None):
    rows = v.shape[0]

    def body(v_ref, out_ref, buf, send_sems, recv_sems):
        x, y, c = lax.axis_index("x"), lax.axis_index("y"), lax.axis_index("c")
        me = 4 * x + 2 * y + c
        buf[me] = v_ref[...]
        flips = [(fx, fy, fc) for fx in (0, 1) for fy in (0, 1) for fc in (0, 1)][1:]

        def peer(k):
            fx, fy, fc = flips[k]
            px, py, pc = x ^ fx, y ^ fy, c ^ fc
            return (px, py, pc), 4 * px + 2 * py + pc

        def copy(k, slot):
            return pltpu.make_async_remote_copy(
                src_ref=buf.at[slot], dst_ref=buf.at[slot], send_sem=send_sems.at[k],
                recv_sem=recv_sems.at[k], device_id=peer(k)[0], device_id_type=MESH)

        sends = [copy(k, me) for k in range(7)]
        for cp in sends:
            cp.start()
        for k in range(7):
            copy(k, peer(k)[1]).wait_recv()
        for cp in sends:
            cp.wait_send()
        acc = buf[0]
        for d in range(1, 8):
            acc = acc + buf[d]
        out_ref[...] = acc

    return _pcall(body, [v], name="all_reduce_small", out_shape=jax.ShapeDtypeStruct((rows, 128), F32),
                  scratch=[pltpu.VMEM((8, rows, 128), F32), pltpu.SemaphoreType.DMA((7,)),
                           pltpu.SemaphoreType.DMA((7,))], comm=comm)


def _add_halves(g, got):
    _, h, c = got.shape
    th = _row_tile(h, 512)
    nh = h // th
    half = lax.axis_index("c") * nh

    def body(h_ref, a_ref, b_ref, o_ref):
        o_ref[...] = (a_ref[...].astype(F32) + b_ref[...].astype(F32)).astype(o_ref.dtype)

    spec = pl.BlockSpec((1, th, c), lambda j, i, h_ref: (j, i, 0))
    mine = pl.BlockSpec((1, th, c), lambda j, i, h_ref: (j, h_ref[0] + i, 0))
    return _pcall(body, [g, got], name="add_halves", grid=(N_CHIPS, nh), prefetch=[half],
                  out_shape=jax.ShapeDtypeStruct(got.shape, BF), in_specs=[mine, spec], out_specs=spec)


def _sum_chips(parts, recv):
    _, h, c = parts.shape
    th = _row_tile(h, 512)
    me = 2 * lax.axis_index("x") + lax.axis_index("y")

    def body(me_ref, p_ref, r_ref, o_ref):
        acc = p_ref[0].astype(F32)
        for s in range(N_CHIPS - 1):
            acc = acc + r_ref[s].astype(F32)
        o_ref[...] = acc

    return _pcall(body, [parts, recv], name="sum_chips", grid=(h // th,), prefetch=[me],
                  out_shape=jax.ShapeDtypeStruct((h, c), F32),
                  in_specs=[pl.BlockSpec((1, th, c), lambda i, me_ref: (me_ref[0], i, 0)),
                            pl.BlockSpec((N_CHIPS - 1, th, c), lambda i, me_ref: (0, i, 0))],
                  out_specs=pl.BlockSpec((th, c), lambda i, me_ref: (i, 0)))


def _adam_update(w, gv, m, v, d_ref, nm_ref, nv_ref):
    c1 = 1.0 / (1.0 - ADAM_B1 ** ADAM_STEP)
    c2 = 1.0 / (1.0 - ADAM_B2 ** ADAM_STEP)
    nm = ADAM_B1 * m + (1.0 - ADAM_B1) * gv
    nv = ADAM_B2 * v + (1.0 - ADAM_B2) * (gv * gv)
    nm_ref[...] = nm
    nv_ref[...] = nv
    d_ref[...] = -ADAM_LR * ((nm * c1) / (jnp.sqrt(nv * c2) + ADAM_EPS) + ADAM_WD * w)


def _adamw(w, g, m, v, comm=None):
    r, c = w.shape
    tr = _row_tile(r, 512)

    def body(w_ref, g_ref, m_ref, v_ref, d_ref, nm_ref, nv_ref):
        _adam_update(w_ref[...], g_ref[...], m_ref[...], v_ref[...], d_ref, nm_ref, nv_ref)

    spec = pl.BlockSpec((tr, c), lambda i: (i, 0))
    sds = jax.ShapeDtypeStruct((r, c), F32)
    return _pcall(body, [w, g, m, v], name="adamw", grid=(r // tr,), out_shape=[sds, sds, sds],
                  in_specs=[spec] * 4, out_specs=[spec] * 3, comm=comm)


def _adamw_halves(items, comm=None):
    k = len(items)
    r, c = items[0][0].shape
    h = r // 2
    tr = _row_tile(h, min(512, (VMEM_LIMIT * 3 // 4) // (k * 9 * 2 * 4 * c)))
    nb = h // tr
    core = lax.axis_index("c")

    def body(c_ref, *refs):
        ins, outs = refs[:5 * k], refs[5 * k:]
        for q in range(k):
            w_ref, gm_ref, go_ref, m_ref, v_ref = ins[5 * q:5 * q + 5]
            g_ref, d_ref, nm_ref, nv_ref = outs[4 * q:4 * q + 4]
            gv = jnp.where(pl.program_id(0) == c_ref[0], gm_ref[...], go_ref[...])
            g_ref[...] = gv
            _adam_update(w_ref[...], gv, m_ref[...], v_ref[...], d_ref, nm_ref, nv_ref)

    full = pl.BlockSpec((tr, c), lambda hh, i, c_ref: (hh * nb + i, 0))
    half = pl.BlockSpec((tr, c), lambda hh, i, c_ref: (i, 0))
    sds = jax.ShapeDtypeStruct((r, c), F32)
    return _pcall(body, [a for it in items for a in it], name="adamw_halves", grid=(2, nb), prefetch=[core],
                  out_shape=[sds] * (4 * k), in_specs=[full, half, half, full, full] * k, out_specs=[full] * (4 * k),
                  comm=comm)


SC_CORES, SC_TILES, SC_LANES = 2, 16, 16
SC_BLOCK_ROWS, SC_BLOCK_COLS = 8, 512


def _sc_adamw_halves(items):
    k = len(items)
    c1 = 1.0 / (1.0 - ADAM_B1 ** ADAM_STEP)
    c2 = 1.0 / (1.0 - ADAM_B2 ** ADAM_STEP)
    mesh = plsc.VectorSubcoreMesh(core_axis_name="sc_core", subcore_axis_name="sc_tile",
                                  num_cores=SC_CORES, num_subcores=SC_TILES)

    def block_fn(bc):
        def block(w_v, gin_v, m_v, v_v, g_v, d_v, nm_v, nv_v):
            @pl.loop(0, SC_BLOCK_ROWS)
            def _(row):
                @pl.loop(0, bc, step=SC_LANES)
                def _(col):
                    at = (pl.ds(row, 1), pl.ds(col, SC_LANES))
                    gv = gin_v.at[*at][...]
                    nm = ADAM_B1 * m_v.at[*at][...] + (1.0 - ADAM_B1) * gv
                    nv = ADAM_B2 * v_v.at[*at][...] + (1.0 - ADAM_B2) * (gv * gv)
                    g_v.at[*at][...] = gv
                    nm_v.at[*at][...] = nm
                    nv_v.at[*at][...] = nv
                    d_v.at[*at][...] = -ADAM_LR * ((nm * c1) / (jnp.sqrt(nv * c2) + ADAM_EPS)
                                                   + ADAM_WD * w_v.at[*at][...])

        return block

    def kern(*refs):
        ins, outs = refs[:5 * k], refs[5 * k:]
        core = lax.axis_index("c")

        def half(q, hh, mine):
            w_hbm, gm_hbm, go_hbm, m_hbm, v_hbm = ins[5 * q:5 * q + 5]
            r, c = w_hbm.shape
            h, bc = r // 2, min(c, SC_BLOCK_COLS)
            spec = pl.BlockSpec(block_shape=(SC_BLOCK_ROWS, bc), index_map=lambda i, j: (i, j))
            rows = pl.ds(hh * h, h)
            pltpu.emit_pipeline(
                block_fn(bc), grid=(h // SC_BLOCK_ROWS, c // bc), in_specs=[spec] * 4, out_specs=[spec] * 4,
                core_axis_name=("sc_core", "sc_tile"), dimension_semantics=(pltpu.PARALLEL, pltpu.PARALLEL),
                trace_scopes=False,
            )(w_hbm.at[rows, :], gm_hbm if mine else go_hbm, m_hbm.at[rows, :], v_hbm.at[rows, :],
              *(o.at[rows, :] for o in outs[4 * q:4 * q + 4]))

        for q in range(k):
            for hh in range(2):
                pl.when(core == hh)(lambda q=q, hh=hh: half(q, hh, True))
                pl.when(core != hh)(lambda q=q, hh=hh: half(q, hh, False))

    out_type = [jax.ShapeDtypeStruct(it[0].shape, F32) for it in items for _ in range(4)]
    return pl.kernel(kern, out_type=out_type, mesh=mesh, scratch_types=[], name="sc_adamw_halves")(
        *(a for it in items for a in it))


def _wgrad(name, a, b, a_spec, b_spec, m, n, nb, comm):
    def body(a_ref, b_ref, o_ref):
        o_ref[...] = _dot_tn(a_ref[...], b_ref[...]).astype(o_ref.dtype)

    return _pcall(body, [a, b], name=name, grid=(nb,), out_shape=jax.ShapeDtypeStruct((nb, m, n), BF),
                  in_specs=[a_spec, b_spec], out_specs=pl.BlockSpec((None, m, n), lambda j: (j, 0, 0)), comm=comm)


def _wgrad_cols(name, a, b, nb, comm=None):
    t_tok, m = a.shape
    n = b.shape[1] // nb
    return _wgrad(name, a, b, pl.BlockSpec((t_tok, m), lambda j: (0, 0)), pl.BlockSpec((t_tok, n), lambda j: (0, j)),
                  m, n, nb, comm)


def _wgrad_rows(name, a, b, nb, comm=None):
    t_tok, n = b.shape
    m = a.shape[1] // nb
    return _wgrad(name, a, b, pl.BlockSpec((t_tok, m), lambda j: (0, j)), pl.BlockSpec((t_tok, n), lambda j: (0, 0)),
                  m, n, nb, comm)


def _wgrad_a_shared(name, a, b4, comm=None):
    t_tok, m = a.shape
    nb, _, n = b4.shape
    return _wgrad(name, a, b4, pl.BlockSpec((t_tok, m), lambda j: (0, 0)),
                  pl.BlockSpec((None, t_tok, n), lambda j: (j, 0, 0)), m, n, nb, comm)


def _wgrad_b_shared(name, a4, b, comm=None):
    nb, t_tok, m = a4.shape
    n = b.shape[1]
    return _wgrad(name, a4, b, pl.BlockSpec((None, t_tok, m), lambda j: (j, 0, 0)),
                  pl.BlockSpec((t_tok, n), lambda j: (0, 0)), m, n, nb, comm)


def _w4_spec(r, c):
    return pl.BlockSpec((None, r, c), lambda i, j: (j, 0, 0))


FFN_ROW_CHUNK = 256


def _row_chunks(tm):
    rc = FFN_ROW_CHUNK if tm % FFN_ROW_CHUNK == 0 else tm
    return [slice(r, r + rc) for r in range(0, tm, rc)]


def _ffn_fwd(h, ln, wg4, wu4, wd4, comm=None):
    t_tok, d = h.shape
    f = wg4.shape[-2]
    tm = _tile(t_tok, 512)

    def body(h_ref, ln_ref, wg_ref, wu_ref, wd_ref, ho_ref, n_ref, g_ref, u_ref, n_s, acc):
        j = pl.program_id(1)

        @pl.when(j == 0)
        def _():
            xv = h_ref[...]
            nv = (xv * _rstd(xv) * ln_ref[...]).astype(BF)
            n_s[...] = nv
            n_ref[...] = nv
            acc[...] = jnp.zeros_like(acc)

        nv = n_s[...]
        g = _dot_nt(nv, wg_ref[...])
        u = _dot_nt(nv, wu_ref[...])
        g_ref[...] = g.astype(BF)
        u_ref[...] = u.astype(BF)
        a = (g * _sigmoid(g) * u).astype(BF)
        acc[...] += _dot(a, wd_ref[...])

        @pl.when(j == N_CHIPS - 1)
        def _():
            ho_ref[...] = h_ref[...] + 0.5 * acc[...]

    row = pl.BlockSpec((tm, d), lambda i, j: (i, 0))
    gu = pl.BlockSpec((None, tm, f), lambda i, j: (j, i, 0))
    gu_sds = jax.ShapeDtypeStruct((N_CHIPS, t_tok, f), BF)
    return _pcall(
        body, [h, ln, wg4, wu4, wd4], name="ffn_fwd", grid=(t_tok // tm, N_CHIPS),
        out_shape=[jax.ShapeDtypeStruct((t_tok, d), F32), jax.ShapeDtypeStruct((t_tok, d), BF), gu_sds, gu_sds],
        in_specs=[row, pl.BlockSpec((1, d), lambda i, j: (0, 0)), _w4_spec(f, d), _w4_spec(f, d), _w4_spec(f, d)],
        out_specs=[row, row, gu, gu],
        scratch=[pltpu.VMEM((tm, d), BF), pltpu.VMEM((tm, d), F32)], comm=comm)


def _ffn_bwd(dho, h, ln, g4, u4, wg4, wu4, wd4, comm=None):
    t_tok, d = h.shape
    f = wg4.shape[-2]
    tm = _tile(t_tok, 512)

    def body(dho_ref, h_ref, ln_ref, g_ref, u_ref, wg_ref, wu_ref, wd_ref,
             dhi_ref, dln_ref, dg_ref, du_ref, a_ref, dhb_ref, dhb_s, dn_acc):
        i, j = pl.program_id(0), pl.program_id(1)

        @pl.when(j == 0)
        def _():
            dhb = (0.5 * dho_ref[...]).astype(BF)
            dhb_s[...] = dhb
            dhb_ref[...] = dhb
            dn_acc[...] = jnp.zeros_like(dn_acc)

        @pl.when((i == 0) & (j == 0))
        def _():
            dln_ref[...] = jnp.zeros_like(dln_ref)

        for rows in _row_chunks(tm):
            g = g_ref[rows, :].astype(F32)
            u = u_ref[rows, :].astype(F32)
            s = _sigmoid(g)
            sg = g * s
            a_ref[rows, :] = (sg * u).astype(BF)
            da = _dot_nt(dhb_s[rows, :], wd_ref[...])
            dg = (da * u * (s * (1.0 + g * (1.0 - s)))).astype(BF)
            du = (da * sg).astype(BF)
            dg_ref[rows, :] = dg
            du_ref[rows, :] = du
            dn_acc[rows, :] += _dot(dg, wg_ref[...]) + _dot(du, wu_ref[...])

        @pl.when(j == N_CHIPS - 1)
        def _():
            xv = h_ref[...]
            dx, dln = _rms_bwd(dn_acc[...], xv, _rstd(xv), ln_ref[...])
            dln_ref[...] += dln
            dhi_ref[...] = dho_ref[...] + dx

    row = pl.BlockSpec((tm, d), lambda i, j: (i, 0))
    vec = pl.BlockSpec((1, d), lambda i, j: (0, 0))
    gu = pl.BlockSpec((None, tm, f), lambda i, j: (j, i, 0))
    gu_sds = jax.ShapeDtypeStruct((N_CHIPS, t_tok, f), BF)
    return _pcall(
        body, [dho, h, ln, g4, u4, wg4, wu4, wd4], name="ffn_bwd", grid=(t_tok // tm, N_CHIPS),
        out_shape=[jax.ShapeDtypeStruct((t_tok, d), F32), jax.ShapeDtypeStruct((1, d), F32),
                   gu_sds, gu_sds, gu_sds, jax.ShapeDtypeStruct((t_tok, d), BF)],
        in_specs=[row, row, vec, gu, gu, _w4_spec(f, d), _w4_spec(f, d), _w4_spec(f, d)],
        out_specs=[row, vec, gu, gu, gu, row],
        scratch=[pltpu.VMEM((tm, d), BF), pltpu.VMEM((tm, d), F32)], comm=comm)


def _rope_tables(pos_col, inv_freq2, comm=None):
    t_tok = pos_col.shape[0]

    def body(p_ref, f_ref, cos_ref, sin_ref):
        ang = p_ref[...] * f_ref[...]
        lane = lax.broadcasted_iota(jnp.int32, ang.shape, 1)
        s = jnp.sin(ang)
        cos_ref[...] = jnp.cos(ang)
        sin_ref[...] = jnp.where((lane & 1) == 0, -s, s)

    sds = jax.ShapeDtypeStruct((t_tok, 128), F32)
    return _pcall(body, [pos_col, inv_freq2], name="rope_tables", out_shape=[sds, sds], comm=comm)


def _swap_pairs(x):
    lane = lax.broadcasted_iota(jnp.int32, x.shape, 1)
    return jnp.where((lane & 1) == 0, pltpu.roll(x, 127, 1), pltpu.roll(x, 1, 1))


def _mix_in(h, ln, w_in, wm4, b_m, cos_t, sin_t, comm=None):
    t_tok, d = h.shape
    cm = wm4.shape[-1]
    tm = _tile(t_tok, 256)

    def body(h_ref, ln_ref, win_ref, wm_ref, bm_ref, cos_ref, sin_ref,
             u_ref, rq_ref, rk_ref, rv_ref, rg_ref, fq_ref, fk_ref, fv_ref, ff_ref, ga_ref, gb_ref):
        xv = h_ref[...]
        ub = (xv * _rstd(xv) * ln_ref[...]).astype(BF)
        u_ref[...] = ub
        cosv, sinv = cos_ref[...], sin_ref[...]

        def sec(k):
            return _dot_nt(ub, win_ref[k * 512:(k + 1) * 512, :])

        def rot(xh):
            return xh * cosv + _swap_pairs(xh) * sinv

        pq, pk = sec(0), sec(1)
        for hh in range(RET_HEADS):
            sl = slice(hh * RET_DIM, (hh + 1) * RET_DIM)
            rq_ref[:, sl] = rot(pq[:, sl]).astype(BF)
            rk_ref[:, sl] = (rot(pk[:, sl]) * RET_SCALE).astype(BF)
        rv_ref[...] = sec(2).astype(BF)
        rg_ref[...] = sec(3).astype(BF)
        fq_ref[...] = (sec(4) * FOX_SCALE).astype(BF)
        fk_ref[...] = sec(5).astype(BF)
        fv_ref[...] = sec(6).astype(BF)
        ff_ref[...] = _dot_nt(ub, win_ref[FF_COL:FF_COL + 128, :])
        for j in range(N_CHIPS):
            gs = _sigmoid(_dot(ub, wm_ref[j]) + bm_ref[:, j * cm:(j + 1) * cm]).astype(BF)
            col = j * cm
            if col < d:
                ga_ref[:, col:col + cm] = gs
            else:
                gb_ref[:, col - d:col - d + cm] = gs

    row = lambda c: pl.BlockSpec((tm, c), lambda i: (i, 0))
    full = lambda *s: pl.BlockSpec(s, lambda i: (0,) * len(s))
    sds = lambda c, dt: jax.ShapeDtypeStruct((t_tok, c), dt)
    return _pcall(
        body, [h, ln, w_in, wm4, b_m, cos_t, sin_t], name="mix_in", grid=(t_tok // tm,),
        out_shape=[sds(d, BF)] + [sds(512, BF)] * 7 + [sds(128, F32), sds(d, BF), sds(d, BF)],
        in_specs=[row(d), full(1, d), full(IN_PAD, d), full(N_CHIPS, d, cm), full(1, 2 * d), row(128), row(128)],
        out_specs=[row(d)] + [row(512)] * 7 + [row(128), row(d), row(d)], comm=comm)


def _split3(x):
    hi = x.astype(BF)
    r1 = x - hi.astype(F32)
    mid = r1.astype(BF)
    lo = (r1 - mid.astype(F32)).astype(BF)
    return hi, mid, lo


def _aug_lane():
    return lax.broadcasted_iota(jnp.int32, (1, 128), 1) & (FOX_DIM - 1)


def _aug_put(base, k0, parts):
    w = _aug_lane()
    for i, part in enumerate(parts):
        base = jnp.where(w == k0 + i, part, base)
    return base


def _forget_fwd(ffl, b_pad):
    t_tok = ffl.shape[0]
    tb = _tile(t_tok, 256)

    def body(ff_ref, b_ref, aq_ref, ak_ref, cum_s):
        r = lax.broadcasted_iota(jnp.int32, (tb, tb), 0)
        c = lax.broadcasted_iota(jnp.int32, (tb, tb), 1)
        tri = jnp.where(c <= r, 1.0, 0.0).astype(BF)
        carry = jnp.zeros((1, 128), F32)
        for i in range(t_tok // tb):
            z = ff_ref[i * tb:(i + 1) * tb, :] + b_ref[...]
            lf = jnp.minimum(z, 0.0) - jnp.log(1.0 + jnp.exp(-jnp.abs(z)))
            hi, mid, lo = _split3(lf)
            cs = _dot(tri, hi) + _dot(tri, mid) + _dot(tri, lo) + carry
            cum_s[i * tb:(i + 1) * tb, :] = cs
            carry = cs[tb - 1:tb, :]
        x = cum_s[...]
        first = lax.broadcasted_iota(jnp.int32, (1, 128), 1) < FOX_DIM
        w = _aug_lane()
        one = jnp.ones((t_tok, 128), BF)
        zero = jnp.zeros((t_tok, 128), BF)
        for pp in range(FOX_HEADS // 2):
            other = jnp.where(first, x[:, 2 * pp + 1:2 * pp + 2], x[:, 2 * pp:2 * pp + 1])
            parts = _split3(other)
            aq = jnp.where((w >= 3) & (w < 6), one, zero)
            ak = jnp.where((w < 3) | ((w >= 6) & (w < 9)), one, zero)
            aq_ref[:, pp * 128:(pp + 1) * 128] = _aug_put(aq, 0, parts)
            ak_ref[:, pp * 128:(pp + 1) * 128] = _aug_put(ak, 3, [-q for q in parts])

    sds = jax.ShapeDtypeStruct((t_tok, FOX_WIDTH), BF)
    return _pcall(body, [ffl, b_pad], name="forget_fwd", out_shape=[sds, sds],
                  scratch=[pltpu.VMEM((t_tok, 128), F32)])


def _forget_bwd(dcum_t, dcum_q, ffl, b_pad):
    t_tok = ffl.shape[0]
    tb = _tile(t_tok, 256)

    def body(dc_ref, dq_ref, ff_ref, b_ref, dff_ref, db_ref, pad_s, d_s):
        pad_s[...] = jnp.zeros_like(pad_s)
        pad_s[0:FOX_HEADS, :] = dc_ref[...]
        dsum = pad_s[...].T
        lane = lax.broadcasted_iota(jnp.int32, (t_tok, 128), 1)
        for hh in range(FOX_HEADS):
            dsum = dsum + jnp.where(lane == hh, dq_ref[:, hh * FOX_DIM:hh * FOX_DIM + 1], 0.0)
        d_s[...] = dsum
        r = lax.broadcasted_iota(jnp.int32, (tb, tb), 0)
        c = lax.broadcasted_iota(jnp.int32, (tb, tb), 1)
        tri = jnp.where(c >= r, 1.0, 0.0).astype(BF)
        carry = jnp.zeros((1, 128), F32)
        db = jnp.zeros((1, 128), F32)
        for i in reversed(range(t_tok // tb)):
            hi, mid, lo = _split3(d_s[i * tb:(i + 1) * tb, :])
            dlf = _dot(tri, hi) + _dot(tri, mid) + _dot(tri, lo) + carry
            carry = dlf[0:1, :]
            z = ff_ref[i * tb:(i + 1) * tb, :] + b_ref[...]
            dff = dlf * _sigmoid(-z)
            dff_ref[i * tb:(i + 1) * tb, :] = dff.astype(BF)
            db = db + jnp.sum(dff, axis=0, keepdims=True)
        db_ref[...] = db

    return _pcall(
        body, [dcum_t, dcum_q, ffl, b_pad], name="forget_bwd",
        out_shape=[jax.ShapeDtypeStruct((t_tok, 128), BF), jax.ShapeDtypeStruct((1, 128), F32)],
        scratch=[pltpu.VMEM((128, t_tok), F32), pltpu.VMEM((t_tok, 128), F32)])


def _first_half():
    return lax.broadcasted_iota(jnp.int32, (1, 128), 1) < FOX_DIM


def _head_rows(x2, a2, hh):
    return jnp.where(_first_half(), x2, a2) if hh == 0 else jnp.where(_first_half(), a2, x2)


def _head_only(x2, hh):
    zero = jnp.zeros_like(x2)
    return jnp.where(_first_half(), x2, zero) if hh == 0 else jnp.where(_first_half(), zero, x2)


def _causal_diag(s):
    rows = lax.broadcasted_iota(jnp.int32, s.shape, 0)
    cols = lax.broadcasted_iota(jnp.int32, s.shape, 1)
    return jnp.where(cols <= rows, s, NEG)


def _diag_or_below(qi, ki, step):
    pl.when(ki < qi)(lambda: step(False))
    pl.when(ki == qi)(lambda: step(True))


def _tri_rows(s, n):
    qi = sum((s >= r * (r + 1) // 2).astype(jnp.int32) for r in range(1, n))
    return qi, s - (qi * (qi + 1)) // 2


def _tri_cols(s, n):
    ki = sum((s >= k * n - k * (k - 1) // 2).astype(jnp.int32) for k in range(1, n))
    return ki, ki + s - (ki * n - (ki * (ki - 1)) // 2)


def _fox_fwd(fq, fk, fv, aq, ak, comm=None):
    t_tok = fq.shape[0]
    t = _tile(t_tok, 512)
    nq = t_tok // t
    npair = FOX_HEADS // 2

    def body(q_ref, k_ref, v_ref, aq_ref, ak_ref, o_ref, of_ref, aqb_ref, m_s, l_s, acc_s):
        qi, ki = _tri_rows(pl.program_id(1), nq)

        @pl.when(ki == 0)
        def _():
            m_s[...] = jnp.full_like(m_s, NEG)
            l_s[...] = jnp.zeros_like(l_s)
            acc_s[...] = jnp.zeros_like(acc_s)

        def step(diag):
            q2, k2, v2, aq2, ak2 = q_ref[...], k_ref[...], v_ref[...], aq_ref[...], ak_ref[...]
            for hh in range(2):
                s = _dot_nt(_head_rows(q2, aq2, hh), _head_rows(k2, ak2, hh))
                if diag:
                    s = _causal_diag(s)
                m_prev = m_s[hh]
                m_new = jnp.maximum(m_prev, jnp.max(s, axis=1, keepdims=True))
                alpha = jnp.exp(m_prev - m_new)
                p = jnp.exp(s - jnp.tile(m_new, (1, t // 128)))
                l_s[hh] = alpha * l_s[hh] + jnp.sum(p, axis=1, keepdims=True)
                acc_s[hh] = alpha * acc_s[hh] + _dot(p.astype(BF), v2)
                m_s[hh] = m_new

        _diag_or_below(qi, ki, step)

        @pl.when(ki == qi)
        def _():
            first = _first_half()
            o = jnp.where(first, acc_s[0] / l_s[0], acc_s[1] / l_s[1])
            o_ref[...] = o.astype(BF)
            of_ref[...] = o
            other = jnp.where(first, m_s[1] + jnp.log(l_s[1]), m_s[0] + jnp.log(l_s[0]))
            aqb_ref[...] = _aug_put(aq_ref[...], 6, _split3(-other))

    qs = pl.BlockSpec((t, 128), lambda p, s: (_tri_rows(s, nq)[0], p))
    ks = pl.BlockSpec((t, 128), lambda p, s: (_tri_rows(s, nq)[1], p))
    stat = pltpu.VMEM((2, t, 128), F32)
    return _pcall(
        body, [fq, fk, fv, aq, ak], name="fox_fwd", grid=(npair, nq * (nq + 1) // 2),
        out_shape=[jax.ShapeDtypeStruct((t_tok, FOX_WIDTH), BF), jax.ShapeDtypeStruct((t_tok, FOX_WIDTH), F32),
                   jax.ShapeDtypeStruct((t_tok, FOX_WIDTH), BF)],
        in_specs=[qs, ks, ks, qs, ks], out_specs=[qs, qs, qs], scratch=[stat, stat, stat], comm=comm)


def _fox_ds(q2, k2, v2, do2, aq2, ak2, ad2, hh, diag):
    s = _dot_nt(_head_rows(q2, aq2, hh), _head_rows(k2, ak2, hh))
    if diag:
        s = _causal_diag(s)
    p = jnp.exp(s)
    av = jnp.where(_aug_lane() < 3, 1.0, 0.0).astype(BF)
    dp = _dot_nt(_head_rows(do2, ad2, hh), _head_rows(v2, jnp.broadcast_to(av, v2.shape), hh))
    return p, p * dp


def _fox_bwd(fq, fk, fv, do, aqb, ak, ad, comm=None):
    t_tok = fq.shape[0]
    t = _tile(t_tok, 512)
    nq = t_tok // t
    npair = FOX_HEADS // 2
    n_steps = nq * (nq + 1) // 2

    def body(q_ref, k_ref, v_ref, do_ref, aq_ref, ak_ref, ad_ref, dq_ref, dk_ref, dv_ref, dck_ref, dcq_ref,
             dk_s, dv_s, dq_s, rs_s):
        step_id = pl.program_id(1)
        ki, qi = _tri_cols(step_id, nq)

        @pl.when(step_id == 0)
        def _():
            dq_s[...] = jnp.zeros_like(dq_s)
            rs_s[...] = jnp.zeros_like(rs_s)

        @pl.when(qi == ki)
        def _():
            dk_s[...] = jnp.zeros_like(dk_s)
            dv_s[...] = jnp.zeros_like(dv_s)
            dck_ref[...] = jnp.zeros_like(dck_ref)

        rows = pl.ds(qi * t if isinstance(qi, int) else pl.multiple_of(qi * t, t), t)

        def step(diag):
            q2, k2, v2, do2 = q_ref[...], k_ref[...], v_ref[...], do_ref[...]
            dq = []
            for hh in range(2):
                p, ds = _fox_ds(q2, k2, v2, do2, aq_ref[...], ak_ref[...], ad_ref[...], hh, diag)
                dsb = ds.astype(BF)
                dv_s[...] += _dot_tn(p.astype(BF), _head_only(do2, hh))
                dk_s[...] += _dot_tn(dsb, _head_only(q2, hh))
                dq.append(_dot(dsb, k2))
                dck_ref[hh] = dck_ref[hh] - jnp.sum(ds, axis=0, keepdims=True)
                rs_s[hh, rows, :] = rs_s[hh, rows, :] + jnp.sum(ds, axis=1, keepdims=True)
            dq_s[rows, :] = dq_s[rows, :] + jnp.where(_first_half(), dq[0], dq[1])

        _diag_or_below(qi, ki, step)

        @pl.when(qi == nq - 1)
        def _():
            dk_ref[...] = dk_s[...].astype(BF)
            dv_ref[...] = dv_s[...].astype(BF)

        @pl.when(step_id == n_steps - 1)
        def _():
            dq_ref[...] = (dq_s[...] * FOX_SCALE).astype(BF)
            dcq_ref[...] = jnp.where(_first_half(), rs_s[0], rs_s[1])

    qs = pl.BlockSpec((t, 128), lambda p, s: (_tri_cols(s, nq)[1], p))
    ks = pl.BlockSpec((t, 128), lambda p, s: (_tri_cols(s, nq)[0], p))
    cks = pl.BlockSpec((2, 1, t), lambda p, s: (p, 0, _tri_cols(s, nq)[0]))
    seq = pl.BlockSpec((t_tok, 128), lambda p, s: (0, p))
    sds = jax.ShapeDtypeStruct((t_tok, FOX_WIDTH), BF)
    return _pcall(
        body, [fq, fk, fv, do, aqb, ak, ad], name="fox_bwd", grid=(npair, n_steps),
        out_shape=[sds, sds, sds, jax.ShapeDtypeStruct((FOX_HEADS, 1, t_tok), F32),
                   jax.ShapeDtypeStruct((t_tok, FOX_WIDTH), F32)],
        in_specs=[qs, ks, ks, qs, qs, ks, qs], out_specs=[seq, ks, ks, cks, seq],
        scratch=[pltpu.VMEM((t, 128), F32), pltpu.VMEM((t, 128), F32), pltpu.VMEM((t_tok, 128), F32),
                 pltpu.VMEM((2, t_tok, 128), F32)], comm=comm)


def _ret_consts():
    c = RET_CHUNK
    log_gamma = jnp.log1p(-jnp.exp2(-5.0 - jnp.arange(RET_HEADS, dtype=F32)))
    idx = jnp.arange(c, dtype=F32)
    diff = idx[:, None] - idx[None, :]
    dmask = jnp.where(diff >= 0, jnp.exp(log_gamma[:, None, None] * jnp.maximum(diff, 0.0)), 0.0)
    qdec = jnp.exp(log_gamma[:, None] * (idx + 1.0))
    kdec = jnp.exp(log_gamma[:, None] * (c - 1 - idx))
    cdec = jnp.exp(log_gamma * c)
    bc = lambda v: jnp.broadcast_to(v[:, :, None], (RET_HEADS, c, RET_DIM))
    return dmask, bc(qdec), bc(kdec), jnp.broadcast_to(cdec[:, None, None], (RET_HEADS, c, RET_DIM))


def _group_norm(y):
    mu = jnp.mean(y, axis=-1, keepdims=True)
    yc = y - mu
    r = lax.rsqrt(jnp.mean(yc * yc, axis=-1, keepdims=True) + EPS)
    return yc * r, r


def _ret_fwd(rq, rk, rv, rg, consts, comm=None):
    t_tok = rq.shape[0]
    nb = 4 if t_tok % (4 * RET_CHUNK) == 0 else 1
    tr = nb * RET_CHUNK
    n_steps = t_tok // tr
    c = RET_CHUNK

    def body(q_ref, k_ref, v_ref, g_ref, dm_ref, qd_ref, kd_ref, cd_ref, y_ref, yo_ref, st_ref, s_s):
        @pl.when(pl.program_id(0) == 0)
        def _():
            s_s[...] = jnp.zeros_like(s_s)

        for b in range(nb):
            rows = slice(b * c, (b + 1) * c)
            for hh in range(RET_HEADS):
                cols = slice(hh * RET_DIM, (hh + 1) * RET_DIM)
                q, k, v = q_ref[rows, cols], k_ref[rows, cols], v_ref[rows, cols]
                state = s_s[hh]
                st_ref[hh, b] = state
                sc = (_dot_nt(q, k) * dm_ref[hh]).astype(BF)
                y = _dot(sc, v) + _dot((q.astype(F32) * qd_ref[hh]).astype(BF), state.astype(BF))
                s_s[hh] = cd_ref[hh] * state + _dot_tn((k.astype(F32) * kd_ref[hh]).astype(BF), v)
                y_ref[rows, cols] = y
                yn, _ = _group_norm(y)
                gate = g_ref[rows, cols].astype(F32)
                yo_ref[rows, cols] = (yn * (gate * _sigmoid(gate))).astype(BF)

    blk = pl.BlockSpec((tr, RET_WIDTH), lambda i: (i, 0))
    cst = pl.BlockSpec((RET_HEADS, c, RET_DIM), lambda i: (0, 0, 0))
    return _pcall(
        body, [rq, rk, rv, rg, *consts], name="ret_fwd", grid=(n_steps,),
        out_shape=[jax.ShapeDtypeStruct((t_tok, RET_WIDTH), F32), jax.ShapeDtypeStruct((t_tok, RET_WIDTH), BF),
                   jax.ShapeDtypeStruct((RET_HEADS, t_tok // c, RET_DIM, RET_DIM), F32)],
        in_specs=[blk] * 4 + [cst] * 4,
        out_specs=[blk, blk, pl.BlockSpec((RET_HEADS, nb, RET_DIM, RET_DIM), lambda i: (0, i, 0, 0))],
        scratch=[pltpu.VMEM((RET_HEADS, RET_DIM, RET_DIM), F32)], comm=comm)


def _ret_bwd(rq, rk, rv, rg, y_raw, dyo, states, consts, cos_t, sin_t, comm=None):
    t_tok = rq.shape[0]
    nb = 4 if t_tok % (4 * RET_CHUNK) == 0 else 1
    tr = nb * RET_CHUNK
    n_steps = t_tok // tr
    c = RET_CHUNK

    def body(q_ref, k_ref, v_ref, g_ref, y_ref, dyo_ref, st_ref, dm_ref, qd_ref, kd_ref, cd_ref,
             cos_ref, sin_ref, dq_ref, dk_ref, dv_ref, dg_ref, ds_s):
        @pl.when(pl.program_id(0) == 0)
        def _():
            ds_s[...] = jnp.zeros_like(ds_s)

        for b in reversed(range(nb)):
            rows = slice(b * c, (b + 1) * c)
            cosv, sinv = cos_ref[rows, :], sin_ref[rows, :]
            for hh in range(RET_HEADS):
                cols = slice(hh * RET_DIM, (hh + 1) * RET_DIM)
                dm, qd, kd, cd = dm_ref[hh], qd_ref[hh], kd_ref[hh], cd_ref[hh]
                q, k, v = q_ref[rows, cols], k_ref[rows, cols], v_ref[rows, cols]
                yn, r = _group_norm(y_ref[rows, cols])
                gate = g_ref[rows, cols].astype(F32)
                sg = _sigmoid(gate)
                dyo = dyo_ref[rows, cols]
                dg_ref[rows, cols] = (dyo * yn * (sg * (1.0 + gate * (1.0 - sg)))).astype(BF)
                dyn = dyo * (gate * sg)
                dy = r * (dyn - jnp.mean(dyn, axis=-1, keepdims=True)
                          - yn * jnp.mean(dyn * yn, axis=-1, keepdims=True))
                dyb = dy.astype(BF)
                state_b = st_ref[hh, b].astype(BF)
                dstate = ds_s[hh]
                dstate_b = dstate.astype(BF)
                qdb = (q.astype(F32) * qd).astype(BF)
                kdb = (k.astype(F32) * kd).astype(BF)
                sc = (_dot_nt(q, k) * dm).astype(BF)
                dv = _dot_tn(sc, dyb) + _dot(kdb, dstate_b)
                dp = (_dot_nt(dyb, v) * dm).astype(BF)
                dq = _dot(dp, k) + _dot_nt(dyb, state_b) * qd
                dk = (_dot_tn(dp, q) + _dot_nt(v, dstate_b) * kd) * RET_SCALE
                ds_s[hh] = cd * dstate + _dot_tn(qdb, dyb)
                dv_ref[rows, cols] = dv.astype(BF)
                dq_ref[rows, cols] = (dq * cosv - _swap_pairs(dq) * sinv).astype(BF)
                dk_ref[rows, cols] = (dk * cosv - _swap_pairs(dk) * sinv).astype(BF)

    rev = lambda i: n_steps - 1 - i
    blk = pl.BlockSpec((tr, RET_WIDTH), lambda i: (rev(i), 0))
    tab = pl.BlockSpec((tr, RET_DIM), lambda i: (rev(i), 0))
    cst = pl.BlockSpec((RET_HEADS, c, RET_DIM), lambda i: (0, 0, 0))
    sds = jax.ShapeDtypeStruct((t_tok, RET_WIDTH), BF)
    return _pcall(
        body, [rq, rk, rv, rg, y_raw, dyo, states, *consts, cos_t, sin_t], name="ret_bwd",
        grid=(n_steps,), out_shape=[sds] * 4,
        in_specs=[blk] * 6 + [pl.BlockSpec((RET_HEADS, nb, RET_DIM, RET_DIM), lambda i: (0, rev(i), 0, 0))]
        + [cst] * 4 + [tab, tab],
        out_specs=[blk] * 4, scratch=[pltpu.VMEM((RET_HEADS, RET_DIM, RET_DIM), F32)], comm=comm)


def _mix_out(h, y_ret, y_fox, ga, gb, wr4, wf4, wo4, comm=None):
    t_tok, d = h.shape
    cz = wr4.shape[-1]
    ro = wo4.shape[-2]
    tm = _tile(t_tok, 512)

    def body(h_ref, yr_ref, yf_ref, ga_ref, gb_ref, wr_ref, wf_ref, wo_ref, ho_ref, za_ref, zb_ref, mix_ref):
        yr, yf = yr_ref[...], yf_ref[...]
        for j in range(N_CHIPS):
            sl = slice(j * cz, (j + 1) * cz)
            za = _dot(yr, wr_ref[j])
            zb = _dot(yf, wf_ref[j])
            za_ref[:, sl] = za.astype(BF)
            zb_ref[:, sl] = zb.astype(BF)
            mix_ref[:, sl] = (ga_ref[:, sl].astype(F32) * za + gb_ref[:, sl].astype(F32) * zb).astype(BF)
        acc = h_ref[...]
        for j in range(N_CHIPS):
            acc = acc + _dot(mix_ref[:, j * ro:(j + 1) * ro], wo_ref[j])
        ho_ref[...] = acc

    row = lambda c: pl.BlockSpec((tm, c), lambda i: (i, 0))
    full = lambda *s: pl.BlockSpec(s, lambda i: (0,) * len(s))
    sds = lambda dt: jax.ShapeDtypeStruct((t_tok, d), dt)
    return _pcall(
        body, [h, y_ret, y_fox, ga, gb, wr4, wf4, wo4], name="mix_out", grid=(t_tok // tm,),
        out_shape=[sds(F32), sds(BF), sds(BF), sds(BF)],
        in_specs=[row(d), row(RET_WIDTH), row(FOX_WIDTH), row(d), row(d),
                  full(N_CHIPS, RET_WIDTH, cz), full(N_CHIPS, FOX_WIDTH, cz), full(N_CHIPS, ro, d)],
        out_specs=[row(d)] * 4, comm=comm)


def _mix_out_bwd(dh, za, zb, ga, gb, y_fox, wr4, wf4, wo4, comm=None):
    t_tok, d = dh.shape
    cz = wr4.shape[-1]
    ro = wo4.shape[-2]
    tm = _tile(t_tok, 256)

    def body(dh_ref, za_ref, zb_ref, ga_ref, gb_ref, yf_ref, wr_ref, wf_ref, wo_ref,
             dhb_ref, dgp_ref, dza_ref, dzb_ref, dyr_ref, dyf_ref, dl_ref, db_ref):
        @pl.when(pl.program_id(0) == 0)
        def _():
            db_ref[...] = jnp.zeros_like(db_ref)

        dhb = dh_ref[...].astype(BF)
        dhb_ref[...] = dhb
        dyr = jnp.zeros((tm, RET_WIDTH), F32)
        dyf = jnp.zeros((tm, FOX_WIDTH), F32)
        for j in range(N_CHIPS):
            sl = slice(j * ro, (j + 1) * ro)
            dmix = _dot_nt(dhb, wo_ref[j])
            ga, gb = ga_ref[:, sl].astype(F32), gb_ref[:, sl].astype(F32)
            dza = (dmix * ga).astype(BF)
            dzb = (dmix * gb).astype(BF)
            dza_ref[:, sl] = dza
            dzb_ref[:, sl] = dzb
            dga = dmix * za_ref[:, sl].astype(F32) * ga * (1.0 - ga)
            dgb = dmix * zb_ref[:, sl].astype(F32) * gb * (1.0 - gb)
            dgp_ref[:, sl] = dga.astype(BF)
            dgp_ref[:, d + j * ro:d + (j + 1) * ro] = dgb.astype(BF)
            db_ref[:, sl] += jnp.sum(dga, axis=0, keepdims=True)
            db_ref[:, d + j * ro:d + (j + 1) * ro] += jnp.sum(dgb, axis=0, keepdims=True)
        for j in range(N_CHIPS):
            sl = slice(j * cz, (j + 1) * cz)
            dyr = dyr + _dot_nt(dza_ref[:, sl], wr_ref[j])
            dyf = dyf + _dot_nt(dzb_ref[:, sl], wf_ref[j])
        dyr_ref[...] = dyr
        dyfb = dyf.astype(BF)
        dyf_ref[...] = dyfb
        prod = dyfb.astype(F32) * yf_ref[...]
        first = _first_half()
        for pp in range(FOX_HEADS // 2):
            blk = prod[:, pp * 128:(pp + 1) * 128]
            s0 = jnp.sum(jnp.where(first, blk, 0.0), axis=1, keepdims=True)
            s1 = jnp.sum(jnp.where(first, 0.0, blk), axis=1, keepdims=True)
            parts = _split3(-jnp.where(first, s1, s0))
            dl_ref[:, pp * 128:(pp + 1) * 128] = _aug_put(jnp.zeros((tm, 128), BF), 0, parts)

    row = lambda c: pl.BlockSpec((tm, c), lambda i: (i, 0))
    full = lambda *s: pl.BlockSpec(s, lambda i: (0,) * len(s))
    sds = lambda c, dt: jax.ShapeDtypeStruct((t_tok, c), dt)
    return _pcall(
        body, [dh, za, zb, ga, gb, y_fox, wr4, wf4, wo4], name="mix_out_bwd", grid=(t_tok // tm,),
        out_shape=[sds(d, BF), sds(2 * d, BF), sds(d, BF), sds(d, BF), sds(RET_WIDTH, F32),
                   sds(FOX_WIDTH, BF), sds(FOX_WIDTH, BF), jax.ShapeDtypeStruct((1, 2 * d), F32)],
        in_specs=[row(d)] * 5 + [row(FOX_WIDTH), full(N_CHIPS, RET_WIDTH, cz), full(N_CHIPS, FOX_WIDTH, cz),
                                 full(N_CHIPS, ro, d)],
        out_specs=[row(d), row(2 * d), row(d), row(d), row(RET_WIDTH), row(FOX_WIDTH), row(FOX_WIDTH),
                   full(1, 2 * d)],
        comm=comm)


def _mix_in_bwd(dh, h, ln, parts, dff, dgpre, w_in, wm4, comm=None):
    t_tok, d = h.shape
    cm = wm4.shape[-1]
    tm = _tile(t_tok, 256)

    def body(dh_ref, h_ref, ln_ref, p0, p1, p2, p3, p4, p5, p6, dff_ref, dgp_ref, win_ref, wm_ref,
             dhi_ref, dln_ref, dproj_ref):
        @pl.when(pl.program_id(0) == 0)
        def _():
            dln_ref[...] = jnp.zeros_like(dln_ref)

        for k, pr in enumerate((p0, p1, p2, p3, p4, p5, p6)):
            dproj_ref[:, k * 512:(k + 1) * 512] = pr[...]
        dproj_ref[:, FF_COL:FF_COL + 128] = dff_ref[...]
        dproj_ref[:, FF_COL + 128:] = jnp.zeros((tm, IN_PAD - FF_COL - 128), BF)
        du = _dot(dproj_ref[...], win_ref[...])
        for j in range(N_CHIPS):
            du = du + _dot_nt(dgp_ref[:, j * cm:(j + 1) * cm], wm_ref[j])
        xv = h_ref[...]
        dx, dln = _rms_bwd(du, xv, _rstd(xv), ln_ref[...])
        dln_ref[...] += dln
        dhi_ref[...] = dh_ref[...] + dx

    row = lambda c: pl.BlockSpec((tm, c), lambda i: (i, 0))
    full = lambda *s: pl.BlockSpec(s, lambda i: (0,) * len(s))
    return _pcall(
        body, [dh, h, ln, *parts, dff, dgpre, w_in, wm4], name="mix_in_bwd", grid=(t_tok // tm,),
        out_shape=[jax.ShapeDtypeStruct((t_tok, d), F32), jax.ShapeDtypeStruct((1, d), F32),
                   jax.ShapeDtypeStruct((t_tok, IN_PAD), BF)],
        in_specs=[row(d), row(d), full(1, d)] + [row(512)] * 7 + [row(128), row(2 * d), full(IN_PAD, d),
                                                                   full(N_CHIPS, d, cm)],
        out_specs=[row(d), full(1, d), row(IN_PAD)], comm=comm)


def _tail(h, p, target, ln_ple, ln_fin, wpg4, wpl4, comm=None):
    t_tok, d = h.shape
    pd = p.shape[1]
    rg = wpg4.shape[-2]
    cp = wpl4.shape[-1]
    tm = _tile(t_tok, 256)

    def body(h_ref, p_ref, t_ref, lp_ref, lf_ref, wg_ref, wp_ref,
             dh_ref, n_ref, dgp_ref, dpe_ref, pb_ref, loss_ref, dlf_ref, dlp_ref, pe_s, dn_s):
        @pl.when(pl.program_id(0) == 0)
        def _():
            loss_ref[...] = jnp.zeros_like(loss_ref)
            dlf_ref[...] = jnp.zeros_like(dlf_ref)
            dlp_ref[...] = jnp.zeros_like(dlp_ref)

        xv = h_ref[...]
        r3 = _rstd(xv)
        nb = (xv * r3 * lp_ref[...]).astype(BF)
        n_ref[...] = nb
        pb = p_ref[...].astype(BF)
        pb_ref[...] = pb
        pgpre = jnp.zeros((tm, d), F32)
        for j in range(N_CHIPS):
            pgpre = pgpre + _dot(nb[:, j * rg:(j + 1) * rg], wg_ref[j])
            pe_s[:, j * cp:(j + 1) * cp] = _dot(pb, wp_ref[j])
        pg = _sigmoid(pgpre)
        pe = pe_s[...]
        h4 = xv + pg * pe
        r4 = _rstd(h4)
        err = h4 * r4 * lf_ref[...] - t_ref[...]
        loss_ref[...] += 0.5 * jnp.sum(jnp.sum(err * err, axis=1, keepdims=True), axis=0, keepdims=True) / d
        dh4, dlf = _rms_bwd(err * (1.0 / d), h4, r4, lf_ref[...])
        dlf_ref[...] += dlf
        dpe_ref[...] = (dh4 * pg).astype(BF)
        dgp = (dh4 * pe * pg * (1.0 - pg)).astype(BF)
        dgp_ref[...] = dgp
        for j in range(N_CHIPS):
            dn_s[:, j * rg:(j + 1) * rg] = _dot_nt(dgp, wg_ref[j])
        dx, dlp = _rms_bwd(dn_s[...], xv, r3, lp_ref[...])
        dlp_ref[...] += dlp
        dh_ref[...] = dh4 + dx

    row = lambda c: pl.BlockSpec((tm, c), lambda i: (i, 0))
    full = lambda *s: pl.BlockSpec(s, lambda i: (0,) * len(s))
    sds = lambda c, dt: jax.ShapeDtypeStruct((t_tok, c), dt)
    vec = jax.ShapeDtypeStruct((1, d), F32)
    return _pcall(
        body, [h, p, target, ln_ple, ln_fin, wpg4, wpl4], name="tail", grid=(t_tok // tm,),
        out_shape=[sds(d, F32), sds(d, BF), sds(d, BF), sds(d, BF), sds(pd, BF),
                   jax.ShapeDtypeStruct((1, 128), F32), vec, vec],
        in_specs=[row(d), row(pd), row(d), full(1, d), full(1, d), full(N_CHIPS, rg, d), full(N_CHIPS, pd, cp)],
        out_specs=[row(d), row(d), row(d), row(d), row(pd), full(1, 128), full(1, d), full(1, d)],
        scratch=[pltpu.VMEM((tm, d), F32), pltpu.VMEM((tm, d), F32)], comm=comm)


BIG = ["w_ffn1_gate", "w_ffn1_up", "w_ffn1_down", "w_in", "w_merge", "w_ret_out", "w_fox_out", "w_out",
       "w_ffn2_gate", "w_ffn2_up", "w_ffn2_down", "w_ple", "w_ple_gate"]
SMALL = ["ln_ffn1", "ln_mix", "b_forget", "b_merge", "ln_ffn2", "ln_ple", "ln_final"]
WEIGHTS = ["ln_ffn1", "w_ffn1_gate", "w_ffn1_up", "w_ffn1_down", "ln_mix", "w_in", "b_forget", "w_merge", "b_merge",
           "w_ret_out", "w_fox_out", "w_out", "ln_ffn2", "w_ffn2_gate", "w_ffn2_up", "w_ffn2_down", "ln_ple",
           "w_ple", "w_ple_gate", "ln_final"]


TRANSPOSED = {"w_ffn1_gate", "w_ffn1_up", "w_ffn2_gate", "w_ffn2_up", "w_in"}
IN_ROWS_PAD = -(-(IN_COLS // N_CHIPS) // 32) * 32


def _pack_small(vals, loss_row):
    rows = [loss_row]
    for name in SMALL:
        v = vals[name].reshape(-1)
        n = -(-v.shape[0] // 128) * 128
        rows.append(jnp.pad(v, (0, n - v.shape[0])).reshape(n // 128, 128))
    packed = jnp.concatenate(rows, axis=0)
    pad = -packed.shape[0] % 8
    return jnp.pad(packed, ((0, pad), (0, 0)))


def _unpack_small(packed, sizes):
    out, r = {}, 1
    for name in SMALL:
        n = sizes[name]
        nr = -(-n // 128)
        out[name] = packed[r:r + nr].reshape(1, nr * 128)[:, :n]
        r += nr
    return out


class _Stage:
    def __init__(self, comm, finish):
        self.comm, self.finish, self.result = comm, finish, None


def _hosted(fn, *a, stages=()):
    if not stages:
        return fn(*a)
    outs, couts = fn(*a, comm=_merge([st.comm for st in stages]))
    for st, o in zip(stages, _split_outs([st.comm for st in stages], couts)):
        st.result = st.finish(o)
    return outs


class _Reducer:
    def __init__(self):
        self.done = {}

    def swap(self, grads):
        names = list(grads)
        return _Stage(_c_half_swap([grads[n] for n in names]),
                      lambda outs: {n: _add_halves(grads[n], o) for n, o in zip(names, outs)})

    def exchange(self, parts):
        names = list(parts)
        return _Stage(_c_chip_exchange([parts[n] for n in names]),
                      lambda outs: {n: _sum_chips(parts[n], o) for n, o in zip(names, outs)})

    def join(self, halves):
        names = list(halves)
        return _Stage(_c_join([halves[n] for n in names]),
                      lambda outs: self.done.update({n: (halves[n], o) for n, o in zip(names, outs)}))


def kernel(x, p, positions, ln_ffn1, w_ffn1_gate, w_ffn1_up, w_ffn1_down, ln_mix, w_in, b_forget, w_merge, b_merge, w_ret_out, w_fox_out, w_out, ln_ffn2, w_ffn2_gate, w_ffn2_up, w_ffn2_down, ln_ple, w_ple, w_ple_gate, ln_final, loss_target, m_ln_ffn1, m_w_ffn1_gate, m_w_ffn1_up, m_w_ffn1_down, m_ln_mix, m_w_in, m_b_forget, m_w_merge, m_b_merge, m_w_ret_out, m_w_fox_out, m_w_out, m_ln_ffn2, m_w_ffn2_gate, m_w_ffn2_up, m_w_ffn2_down, m_ln_ple, m_w_ple, m_w_ple_gate, m_ln_final, v_ln_ffn1, v_w_ffn1_gate, v_w_ffn1_up, v_w_ffn1_down, v_ln_mix, v_w_in, v_b_forget, v_w_merge, v_b_merge, v_w_ret_out, v_w_fox_out, v_w_out, v_ln_ffn2, v_w_ffn2_gate, v_w_ffn2_up, v_w_ffn2_down, v_ln_ple, v_w_ple, v_w_ple_gate, v_ln_final):
    args = dict(locals())
    w = {n: args[n] for n in WEIGHTS}
    m = {n: args["m_" + n] for n in WEIGHTS}
    v = {n: args["v_" + n] for n in WEIGHTS}
    d = x.shape[-1]
    t_tok = x.shape[1]
    xs, ps, target = x[0], p[0, 0], loss_target[0]
    small = {n: w[n].reshape(1, -1) for n in SMALL}

    def to2d(n, a):
        if n in TRANSPOSED:
            return a[0].T
        return a.reshape(a.shape[-2], a.shape[-1]) if a.ndim == 3 else a.reshape(1, -1)

    def from2d(n, a):
        return a.T[None] if n in TRANSPOSED else a.reshape(w[n].shape)

    def padded(n, a):
        return jnp.pad(a, ((0, IN_ROWS_PAD - a.shape[0]), (0, 0))) if n == "w_in" else a

    core = lax.axis_index("c")
    me = 2 * lax.axis_index("x") + lax.axis_index("y")
    shard = {}
    for n in BIG:
        s2 = padded(n, to2d(n, w[n]).astype(BF))
        shard[n] = s2.reshape(1, 2, s2.shape[0] // 2, s2.shape[1])
    full = {}

    def gather(names):
        bufs = [lax.dynamic_update_slice(jnp.zeros((N_CHIPS,) + shard[n].shape[1:], BF), shard[n], (me, 0, 0, 0))
                for n in names]

        def finish(outs):
            full.update({n: o.reshape(N_CHIPS, 2 * o.shape[2], o.shape[3]) for n, o in zip(names, outs)})

        return _Stage(_c_all_gather(bufs), finish)

    half = RET_DIM // 2
    inv_freq = 1.0 / (ROPE_BASE ** (jnp.arange(half, dtype=F32) / half))
    cos_t, sin_t = _hosted(_rope_tables, positions[0].astype(F32).reshape(t_tok, 1),
                           jnp.repeat(inv_freq, 2).reshape(1, RET_DIM),
                           stages=[gather(["w_ffn1_gate", "w_ffn1_up", "w_ffn1_down"])])
    consts = _ret_consts()
    b_pad = jnp.pad(small["b_forget"], ((0, 0), (0, 128 - FOX_HEADS)))

    h1, n1, g1, u1 = _hosted(
        _ffn_fwd, xs, small["ln_ffn1"], full["w_ffn1_gate"], full["w_ffn1_up"], full["w_ffn1_down"],
        stages=[gather(["w_in", "w_merge", "w_ret_out", "w_fox_out", "w_out", "w_ple_gate", "w_ple"])])
    w_in_full = jnp.pad(full["w_in"][:, :IN_COLS // N_CHIPS].reshape(IN_COLS, d), ((0, IN_PAD - IN_COLS), (0, 0)))
    u, rq, rk, rv, rg, fq, fk, fv, ffl, ga, gb = _mix_in(
        h1, small["ln_mix"], w_in_full, full["w_merge"], small["b_merge"], cos_t, sin_t)
    aq, ak = _forget_fwd(ffl, b_pad)
    y_raw, y_ret, states = _ret_fwd(rq, rk, rv, rg, consts)
    y_fox, y_fox32, aqb = _hosted(_fox_fwd, fq, fk, fv, aq, ak,
                                  stages=[gather(["w_ffn2_gate", "w_ffn2_up", "w_ffn2_down"])])
    h2, za, zb, mix = _mix_out(h1, y_ret, y_fox, ga, gb, full["w_ret_out"], full["w_fox_out"], full["w_out"])
    h3, n2, g2, u2 = _ffn_fwd(h2, small["ln_ffn2"], full["w_ffn2_gate"], full["w_ffn2_up"], full["w_ffn2_down"])

    red = _Reducer()
    dh3, n3, dpgpre, dpe, pb, loss, dln_final, dln_ple = _tail(
        h3, ps, target, small["ln_ple"], small["ln_final"], full["w_ple_gate"], full["w_ple"])
    g_f2 = dict(w_ple_gate=_wgrad_rows("wgrad_ple_gate", n3, dpgpre, N_CHIPS),
                w_ple=_wgrad_cols("wgrad_ple", pb, dpe, N_CHIPS))
    dh2, dln_ffn2, dg2, du2, a2, dhb3 = _ffn_bwd(
        dh3, h2, small["ln_ffn2"], g2, u2, full["w_ffn2_gate"], full["w_ffn2_up"], full["w_ffn2_down"])
    g_f2["w_ffn2_gate"] = _wgrad_b_shared("wgrad_ffn2_gate", dg2, n2)
    g_f2["w_ffn2_up"] = _wgrad_b_shared("wgrad_ffn2_up", du2, n2)
    g_f2["w_ffn2_down"] = _wgrad_b_shared("wgrad_ffn2_down", a2, dhb3)

    sw_f2 = red.swap(g_f2)
    dhb2, dgpre, dza, dzb, dy_ret, dy_fox, ad, db_merge = _hosted(
        _mix_out_bwd, dh2, za, zb, ga, gb, y_fox32, full["w_ret_out"], full["w_fox_out"], full["w_out"],
        stages=[sw_f2])
    g_br = dict(w_out=_wgrad_rows("wgrad_out", mix, dhb2, N_CHIPS),
                w_ret_out=_wgrad_cols("wgrad_ret_out", y_ret, dza, N_CHIPS),
                w_fox_out=_wgrad_cols("wgrad_fox_out", y_fox, dzb, N_CHIPS))

    sw_br = red.swap(g_br)
    drq, drk, drv, drg = _hosted(_ret_bwd, rq, rk, rv, rg, y_raw, dy_ret, states, consts, cos_t, sin_t,
                                 stages=[sw_br])
    ex_f2, ex_br = red.exchange(sw_f2.result), red.exchange(sw_br.result)
    dfq, dfk, dfv, dcum_t3, dcum_q = _hosted(_fox_bwd, fq, fk, fv, dy_fox, aqb, ak, ad, stages=[ex_f2, ex_br])
    dff, db_forget = _forget_bwd(dcum_t3.reshape(FOX_HEADS, t_tok), dcum_q, ffl, b_pad)
    dh1, dln_mix, dproj = _hosted(
        _mix_in_bwd, dh2, h1, small["ln_mix"], (drq, drk, drv, drg, dfq, dfk, dfv), dff, dgpre, w_in_full,
        full["w_merge"], stages=[red.join(ex_f2.result), red.join(ex_br.result)])

    results = {}
    early = ["w_ffn2_gate", "w_ffn2_up", "w_ffn2_down", "w_out", "w_ple_gate", "w_ret_out", "w_fox_out", "w_ple"]
    res = _sc_adamw_halves([(to2d(n, w[n]), *red.done[n], to2d(n, m[n]), to2d(n, v[n])) for n in early])
    for q, n in enumerate(early):
        results[n] = tuple(from2d(n, a) for a in res[4 * q:4 * q + 4])

    dx, dln_ffn1, dg1, du1, a1, dhb1 = _ffn_bwd(
        dh1, xs, small["ln_ffn1"], g1, u1, full["w_ffn1_gate"], full["w_ffn1_up"], full["w_ffn1_down"])
    g_f1g = _wgrad_b_shared("wgrad_ffn1_gate", dg1, n1)
    sw_f1g = red.swap(dict(w_ffn1_gate=g_f1g))
    g_f1u = _hosted(_wgrad_b_shared, "wgrad_ffn1_up", du1, n1, stages=[sw_f1g])
    ex_f1g, sw_f1u = red.exchange(sw_f1g.result), red.swap(dict(w_ffn1_up=g_f1u))
    g_f1d = _hosted(_wgrad_b_shared, "wgrad_ffn1_down", a1, dhb1, stages=[ex_f1g, sw_f1u])

    ex_f1u, sw_f1d = red.exchange(sw_f1u.result), red.swap(dict(w_ffn1_down=g_f1d))
    g_in = _hosted(_wgrad_rows, "wgrad_in", dproj, u, IN_PAD // 512,
                   stages=[ex_f1u, sw_f1d, red.join(ex_f1g.result)])
    g_in = g_in.reshape(IN_PAD, d)[:IN_COLS].reshape(N_CHIPS, IN_COLS // N_CHIPS, d)
    g_in = jnp.pad(g_in, ((0, 0), (0, IN_ROWS_PAD - IN_COLS // N_CHIPS), (0, 0)))
    ex_f1d, sw_in = red.exchange(sw_f1d.result), red.swap(dict(w_in=g_in))
    g_mrg = _hosted(_wgrad_cols, "wgrad_merge", u, dgpre, N_CHIPS,
                    stages=[ex_f1d, sw_in, red.join(ex_f1u.result)])

    small_grads = dict(ln_ffn1=dln_ffn1, ln_mix=dln_mix, b_forget=db_forget[:, :FOX_HEADS], b_merge=db_merge,
                       ln_ffn2=dln_ffn2, ln_ple=dln_ple, ln_final=dln_final)
    sizes = {n: w[n].size for n in SMALL}
    ex_in, sw_mrg = red.exchange(sw_in.result), red.swap(dict(w_merge=g_mrg))
    reduced = _hosted(_all_reduce_small, _pack_small(small_grads, loss),
                      stages=[ex_in, sw_mrg, red.join(ex_f1d.result)])
    gsum = _unpack_small(reduced, sizes)
    loss = reduced[0, 0]
    ex_mrg = red.exchange(sw_mrg.result)
    _hosted(_exchange_only, stages=[ex_mrg, red.join(ex_in.result)])
    _hosted(_exchange_only, stages=[red.join(ex_mrg.result)])

    def update(names, stages=()):
        w2, m2, v2 = ([to2d(n, a[n]) for n in names] for a in (w, m, v))
        n = names[0]
        if n in gsum or n == "w_in":
            if n in gsum:
                g2 = gsum[n]
            else:
                mine, other = red.done[n]
                g2 = jnp.where(core == 0, jnp.concatenate([mine, other]), jnp.concatenate([other, mine]))
                g2 = g2[:w2[0].shape[0]]
            res = [g2] + _hosted(_adamw, w2[0], g2, m2[0], v2[0], stages=stages)
        else:
            res = _hosted(_adamw_halves, [(w2[q], *red.done[names[q]], m2[q], v2[q]) for q in range(len(names))],
                          stages=stages)
        for q, name in enumerate(names):
            results[name] = tuple(from2d(name, a) for a in res[4 * q:4 * q + 4])

    update(["w_ffn1_gate", "w_ffn1_up", "w_ffn1_down"])
    for n in WEIGHTS:
        if n not in results:
            update([n])

    outs = [[results[n][k] for n in WEIGHTS] for k in range(4)]
    return (loss, dx[None], *outs[0], *outs[1], *outs[2], *outs[3])
```

```python
import functools
import operator

import jax
import jax.numpy as jnp
from jax import lax
from jax.experimental import pallas as pl
from jax.experimental.pallas import tpu as pltpu
from jax.experimental.pallas import tpu_sc as plsc

F32 = jnp.float32
BF = jnp.bfloat16
MESH = pl.DeviceIdType.MESH

EPS = 1e-6
ROPE_BASE = 10000.0
N_CHIPS = 4
RET_HEADS = 4
RET_DIM = 128
RET_WIDTH = RET_HEADS * RET_DIM
RET_CHUNK = 128
RET_SCALE = RET_DIM ** -0.5
FOX_HEADS = 8
FOX_DIM = 64
FOX_WIDTH = FOX_HEADS * FOX_DIM
FOX_SCALE = FOX_DIM ** -0.5
IN_COLS = 4 * RET_WIDTH + 3 * FOX_WIDTH + FOX_HEADS
IN_PAD = 4096
FF_COL = 4 * RET_WIDTH + 3 * FOX_WIDTH
NEG = -1e30

ADAM_LR = 0.001
ADAM_B1 = 0.9
ADAM_B2 = 0.999
ADAM_EPS = 1e-08
ADAM_WD = 0.01
ADAM_STEP = 10

VMEM_LIMIT = 52 * 1024 * 1024

NT = (((1,), (1,)), ((), ()))
TN = (((0,), (0,)), ((), ()))

HBM_SPEC = pl.BlockSpec(memory_space=pltpu.HBM)
VMEM_SPEC = pl.BlockSpec(memory_space=pltpu.VMEM)


def _dot(a, b):
    return jnp.dot(a, b, preferred_element_type=F32)


def _dot_nt(a, b):
    return lax.dot_general(a, b, NT, preferred_element_type=F32)


def _dot_tn(a, b):
    return lax.dot_general(a, b, TN, preferred_element_type=F32)


def _rstd(xv):
    return lax.rsqrt(jnp.mean(xv * xv, axis=-1, keepdims=True) + EPS)


def _rms_bwd(dn, xv, r, ln):
    xh = xv * r
    dxh = dn * ln
    dx = r * (dxh - xh * jnp.mean(dxh * xh, axis=-1, keepdims=True))
    return dx, jnp.sum(dn * xh, axis=0, keepdims=True)


def _sigmoid(x):
    return jax.nn.sigmoid(x)


def _tile(n, pref):
    return pref if n % pref == 0 else n


def _row_tile(n, cap):
    best = [t for t in range(16, min(n, cap) + 1, 16) if n % t == 0]
    return best[-1] if best else n


class _Comm:
    def __init__(self, ins, out_shapes, sems, start, wait, aliases=None):
        self.ins, self.out_shapes, self.sems, self.start, self.wait = list(ins), list(out_shapes), list(sems), start, wait
        self.aliases = dict(aliases or {})


def _merge(comms):
    comms = [c for c in comms if c is not None]
    if not comms:
        return None
    bounds, ni, no, ns = [], 0, 0, 0
    for c in comms:
        bounds.append((ni, no, ns))
        ni, no, ns = ni + len(c.ins), no + len(c.out_shapes), ns + len(c.sems)

    def run(which):
        def f(ins, outs, sems):
            for c, (i, o, s) in zip(comms, bounds):
                getattr(c, which)(ins[i:i + len(c.ins)], outs[o:o + len(c.out_shapes)], sems[s:s + len(c.sems)])
        return f

    aliases = {i + a: o + b for c, (i, o, _) in zip(comms, bounds) for a, b in c.aliases.items()}
    return _Comm([a for c in comms for a in c.ins], [a for c in comms for a in c.out_shapes],
                 [a for c in comms for a in c.sems], run("start"), run("wait"), aliases)


def _split_outs(comms, outs):
    res, o = [], 0
    for c in comms:
        if c is not None:
            res.append(list(outs[o:o + len(c.out_shapes)]))
            o += len(c.out_shapes)
    return res


def _pcall(body, args, *, name, out_shape, grid=(), in_specs=None, out_specs=None, scratch=(), comm=None,
           prefetch=()):
    many = isinstance(out_shape, (list, tuple))
    outs = list(out_shape) if many else [out_shape]
    n_pre, n_in, n_out, n_scr = len(prefetch), len(args), len(outs), len(scratch)
    if in_specs is None:
        in_specs, out_specs = [VMEM_SPEC] * n_in, [VMEM_SPEC] * n_out
    else:
        in_specs, out_specs = list(in_specs), (list(out_specs) if many else [out_specs])
    params = pltpu.CompilerParams(dimension_semantics=("arbitrary",) * len(grid), vmem_limit_bytes=VMEM_LIMIT)
    scalars = [jnp.reshape(s, (1,)).astype(jnp.int32) for s in prefetch]
    ci, co = (len(comm.ins), len(comm.out_shapes)) if comm is not None else (0, 0)

    def wrapped(*refs):
        pre, refs = refs[:n_pre], refs[n_pre:]
        a, ca = refs[:n_in], refs[n_in:n_in + ci]
        o = refs[n_in + ci:n_in + ci + n_out]
        cout = refs[n_in + ci + n_out:n_in + ci + n_out + co]
        s = refs[n_in + ci + n_out + co:n_in + ci + n_out + co + n_scr]
        csem = refs[n_in + ci + n_out + co + n_scr:]
        if comm is None:
            body(*pre, *a, *o, *s)
        elif grid:
            first = functools.reduce(operator.and_, [pl.program_id(k) == 0 for k in range(len(grid))])
            last = functools.reduce(operator.and_, [pl.program_id(k) == grid[k] - 1 for k in range(len(grid))])
            pl.when(first)(lambda: comm.start(ca, cout, csem))
            body(*pre, *a, *o, *s)
            pl.when(last)(lambda: comm.wait(ca, cout, csem))
        else:
            comm.start(ca, cout, csem)
            body(*pre, *a, *o, *s)
            comm.wait(ca, cout, csem)

    c_ins, c_outs, c_sems, aliases = ([], [], [], {}) if comm is None else (
        comm.ins, comm.out_shapes, comm.sems, {n_pre + n_in + i: n_out + o for i, o in comm.aliases.items()})
    all_in, all_out = in_specs + [HBM_SPEC] * ci, out_specs + [HBM_SPEC] * co
    all_scr = list(scratch) + c_sems
    if grid:
        args = [pltpu.with_memory_space_constraint(a, pltpu.HBM) for a in args]
    c_ins = [pltpu.with_memory_space_constraint(a, pltpu.HBM) for a in c_ins]
    if n_pre:
        spec = dict(grid_spec=pltpu.PrefetchScalarGridSpec(
            num_scalar_prefetch=n_pre, grid=grid, in_specs=all_in, out_specs=all_out, scratch_shapes=all_scr))
    else:
        spec = dict(grid=grid, in_specs=all_in, out_specs=all_out, scratch_shapes=all_scr)
    res = pl.pallas_call(wrapped, name=name, out_shape=outs + c_outs, input_output_aliases=aliases,
                         compiler_params=params, **spec)(*scalars, *args, *c_ins)
    mine = list(res[:n_out])
    mine = mine if many else mine[0]
    return mine if comm is None else (mine, list(res[n_out:]))


def _peer_chips(x, y):
    return [(1 - x, y), (x, 1 - y), (1 - x, 1 - y)]


def _c_all_gather(bufs):
    n = len(bufs)

    def copies(ins, outs, sems):
        send_sems, recv_sems, fwd_send, fwd_recv = sems
        x, y, c = lax.axis_index("x"), lax.axis_index("y"), lax.axis_index("c")
        me = 2 * x + y
        peers = _peer_chips(x, y)
        chip = [2 * px + py for px, py in peers]

        def ici(g, j, slot):
            return pltpu.make_async_remote_copy(
                src_ref=outs[g].at[me, c], dst_ref=outs[g].at[slot, c], send_sem=send_sems.at[g, j],
                recv_sem=recv_sems.at[g, j], device_id=(*peers[j], c), device_id_type=MESH)

        def d2d(g, j, half):
            return pltpu.make_async_remote_copy(
                src_ref=outs[g].at[chip[j], half], dst_ref=outs[g].at[chip[j], half], send_sem=fwd_send.at[g, j],
                recv_sem=fwd_recv.at[g, j], device_id=(x, y, 1 - c), device_id_type=MESH)

        pairs = [(g, j) for g in range(n) for j in range(3)]
        sends = [ici(g, j, me) for g, j in pairs]
        recvs = [ici(g, j, chip[j]) for g, j in pairs]
        passes = [d2d(g, j, c) for g, j in pairs]
        passed = [d2d(g, j, 1 - c) for g, j in pairs]
        return sends, recvs, passes, passed

    def start(ins, outs, sems):
        for cp in copies(ins, outs, sems)[0]:
            cp.start()

    def wait(ins, outs, sems):
        sends, recvs, passes, passed = copies(ins, outs, sems)
        for rcv, fwd in zip(recvs, passes):
            rcv.wait_recv()
            fwd.start()
        for cp in passed:
            cp.wait_recv()
        for cp in sends + passes:
            cp.wait_send()

    pair_sems = pltpu.SemaphoreType.DMA((n, 3))
    return _Comm(bufs, [jax.ShapeDtypeStruct(s.shape, s.dtype) for s in bufs], [pair_sems] * 4, start, wait,
                 aliases={g: g for g in range(n)})


def _start_wait(copies):
    def start(ins, outs, sems):
        local, sends, _ = copies(ins, outs, sems)
        for cp in local + sends:
            cp.start()

    def wait(ins, outs, sems):
        local, sends, recvs = copies(ins, outs, sems)
        for cp in recvs:
            cp.wait_recv()
        for cp in sends:
            cp.wait_send()
        for cp in local:
            cp.wait()

    return start, wait


def _c_half_swap(grads):
    n = len(grads)

    def copies(ins, outs, sems):
        send_sems, recv_sems = sems
        x, y, c = lax.axis_index("x"), lax.axis_index("y"), lax.axis_index("c")
        sends = []
        for g in range(n):
            half = ins[g].shape[1] // 2
            sends.append(pltpu.make_async_remote_copy(
                src_ref=ins[g].at[:, pl.ds((1 - c) * half, half), :], dst_ref=outs[g],
                send_sem=send_sems.at[g], recv_sem=recv_sems.at[g], device_id=(x, y, 1 - c), device_id_type=MESH))
        return [], sends, sends

    return _Comm(
        grads, [jax.ShapeDtypeStruct((N_CHIPS, s.shape[1] // 2, s.shape[2]), s.dtype) for s in grads],
        [pltpu.SemaphoreType.DMA((n,)), pltpu.SemaphoreType.DMA((n,))], *_start_wait(copies))


def _c_chip_exchange(parts):
    n = len(parts)

    def copies(ins, outs, sems):
        send_sems, recv_sems = sems
        x, y, c = lax.axis_index("x"), lax.axis_index("y"), lax.axis_index("c")
        peers = _peer_chips(x, y)

        def remote(g, j):
            return pltpu.make_async_remote_copy(
                src_ref=ins[g].at[2 * peers[j][0] + peers[j][1]], dst_ref=outs[g].at[j],
                send_sem=send_sems.at[g, j], recv_sem=recv_sems.at[g, j], device_id=(*peers[j], c),
                device_id_type=MESH)

        sends = [remote(g, j) for g in range(n) for j in range(3)]
        return [], sends, sends

    return _Comm(
        parts, [jax.ShapeDtypeStruct((3,) + s.shape[1:], s.dtype) for s in parts],
        [pltpu.SemaphoreType.DMA((n, 3)), pltpu.SemaphoreType.DMA((n, 3))], *_start_wait(copies))


def _c_join(halves):
    n = len(halves)

    def copies(ins, outs, sems):
        send_sems, recv_sems = sems
        x, y, c = lax.axis_index("x"), lax.axis_index("y"), lax.axis_index("c")
        sends = [pltpu.make_async_remote_copy(
            src_ref=ins[g], dst_ref=outs[g], send_sem=send_sems.at[g], recv_sem=recv_sems.at[g],
            device_id=(x, y, 1 - c), device_id_type=MESH) for g in range(n)]
        return [], sends, sends

    return _Comm(
        halves, [jax.ShapeDtypeStruct(s.shape, s.dtype) for s in halves],
        [pltpu.SemaphoreType.DMA((n,)), pltpu.SemaphoreType.DMA((n,))], *_start_wait(copies))


def _exchange_only(comm=None):
    def body(o_ref):
        o_ref[...] = jnp.zeros_like(o_ref)

    return _pcall(body, [], name="exchange_only", out_shape=jax.ShapeDtypeStruct((8, 128), F32), comm=comm)


def _all_reduce_small(v, comm=None):
    rows = v.shape[0]

    def body(v_ref, out_ref, buf, send_sems, recv_sems):
        x, y, c = lax.axis_index("x"), lax.axis_index("y"), lax.axis_index("c")
        me = 4 * x + 2 * y + c
        buf[me] = v_ref[...]
        flips = [(fx, fy, fc) for fx in (0, 1) for fy in (0, 1) for fc in (0, 1)][1:]

        def peer(k):
            fx, fy, fc = flips[k]
            px, py, pc = x ^ fx, y ^ fy, c ^ fc
            return (px, py, pc), 4 * px + 2 * py + pc

        def copy(k, slot):
            return pltpu.make_async_remote_copy(
                src_ref=buf.at[slot], dst_ref=buf.at[slot], send_sem=send_sems.at[k],
                recv_sem=recv_sems.at[k], device_id=peer(k)[0], device_id_type=MESH)

        sends = [copy(k, me) for k in range(7)]
        for cp in sends:
            cp.start()
        for k in range(7):
            copy(k, peer(k)[1]).wait_recv()
        for cp in sends:
            cp.wait_send()
        acc = buf[0]
        for d in range(1, 8):
            acc = acc + buf[d]
        out_ref[...] = acc

    return _pcall(body, [v], name="all_reduce_small", out_shape=jax.ShapeDtypeStruct((rows, 128), F32),
                  scratch=[pltpu.VMEM((8, rows, 128), F32), pltpu.SemaphoreType.DMA((7,)),
                           pltpu.SemaphoreType.DMA((7,))], comm=comm)


def _add_halves(g, got):
    _, h, c = got.shape
    th = _row_tile(h, 512)
    nh = h // th
    half = lax.axis_index("c") * nh

    def body(h_ref, a_ref, b_ref, o_ref):
        o_ref[...] = (a_ref[...].astype(F32) + b_ref[...].astype(F32)).astype(o_ref.dtype)

    spec = pl.BlockSpec((1, th, c), lambda j, i, h_ref: (j, i, 0))
    mine = pl.BlockSpec((1, th, c), lambda j, i, h_ref: (j, h_ref[0] + i, 0))
    return _pcall(body, [g, got], name="add_halves", grid=(N_CHIPS, nh), prefetch=[half],
                  out_shape=jax.ShapeDtypeStruct(got.shape, BF), in_specs=[mine, spec], out_specs=spec)


def _sum_chips(parts, recv):
    _, h, c = parts.shape
    th = _row_tile(h, 512)
    me = 2 * lax.axis_index("x") + lax.axis_index("y")

    def body(me_ref, p_ref, r_ref, o_ref):
        acc = p_ref[0].astype(F32)
        for s in range(N_CHIPS - 1):
            acc = acc + r_ref[s].astype(F32)
        o_ref[...] = acc

    return _pcall(body, [parts, recv], name="sum_chips", grid=(h // th,), prefetch=[me],
                  out_shape=jax.ShapeDtypeStruct((h, c), F32),
                  in_specs=[pl.BlockSpec((1, th, c), lambda i, me_ref: (me_ref[0], i, 0)),
                            pl.BlockSpec((N_CHIPS - 1, th, c), lambda i, me_ref: (0, i, 0))],
                  out_specs=pl.BlockSpec((th, c), lambda i, me_ref: (i, 0)))


def _adam_update(w, gv, m, v, d_ref, nm_ref, nv_ref):
    c1 = 1.0 / (1.0 - ADAM_B1 ** ADAM_STEP)
    c2 = 1.0 / (1.0 - ADAM_B2 ** ADAM_STEP)
    nm = ADAM_B1 * m + (1.0 - ADAM_B1) * gv
    nv = ADAM_B2 * v + (1.0 - ADAM_B2) * (gv * gv)
    nm_ref[...] = nm
    nv_ref[...] = nv
    d_ref[...] = -ADAM_LR * ((nm * c1) / (jnp.sqrt(nv * c2) + ADAM_EPS) + ADAM_WD * w)


def _adamw(w, g, m, v, comm=None):
    r, c = w.shape[0], w.shape[-1]
    tr = _row_tile(r, 512)

    def body(w_ref, g_ref, m_ref, v_ref, d_ref, nm_ref, nv_ref):
        _adam_update(w_ref[...], g_ref[...], m_ref[...], v_ref[...], d_ref, nm_ref, nv_ref)

    mid = (1,) * (w.ndim - 2)
    spec = pl.BlockSpec((tr,) + mid + (c,), lambda i: (i,) + (0,) * (w.ndim - 1))
    sds = jax.ShapeDtypeStruct(w.shape, F32)
    return _pcall(body, [w, g, m, v], name="adamw", grid=(r // tr,), out_shape=[sds, sds, sds],
                  in_specs=[spec] * 4, out_specs=[spec] * 3, comm=comm)


def _adamw_halves(items, comm=None):
    k = len(items)
    r, c = items[0][0].shape
    h = r // 2
    tr = _row_tile(h, min(512, (VMEM_LIMIT * 3 // 4) // (k * 9 * 2 * 4 * c)))
    nb = h // tr
    core = lax.axis_index("c")

    def body(c_ref, *refs):
        ins, outs = refs[:5 * k], refs[5 * k:]
        for q in range(k):
            w_ref, gm_ref, go_ref, m_ref, v_ref = ins[5 * q:5 * q + 5]
            g_ref, d_ref, nm_ref, nv_ref = outs[4 * q:4 * q + 4]
            gv = jnp.where(pl.program_id(0) == c_ref[0], gm_ref[...], go_ref[...])
            g_ref[...] = gv
            _adam_update(w_ref[...], gv, m_ref[...], v_ref[...], d_ref, nm_ref, nv_ref)

    full = pl.BlockSpec((tr, c), lambda hh, i, c_ref: (hh * nb + i, 0))
    half = pl.BlockSpec((tr, c), lambda hh, i, c_ref: (i, 0))
    sds = jax.ShapeDtypeStruct((r, c), F32)
    return _pcall(body, [a for it in items for a in it], name="adamw_halves", grid=(2, nb), prefetch=[core],
                  out_shape=[sds] * (4 * k), in_specs=[full, half, half, full, full] * k, out_specs=[full] * (4 * k),
                  comm=comm)


SC_CORES, SC_TILES, SC_LANES = 2, 16, 16
SC_BLOCK_ROWS, SC_BLOCK_COLS = 8, 512


def _sc_adamw_halves(items):
    k = len(items)
    r, c = items[0][0].shape
    h = r // 2
    bc = min(c, SC_BLOCK_COLS)
    c1 = 1.0 / (1.0 - ADAM_B1 ** ADAM_STEP)
    c2 = 1.0 / (1.0 - ADAM_B2 ** ADAM_STEP)
    mesh = plsc.VectorSubcoreMesh(core_axis_name="sc_core", subcore_axis_name="sc_tile",
                                  num_cores=SC_CORES, num_subcores=SC_TILES)
    spec = pl.BlockSpec(block_shape=(SC_BLOCK_ROWS, bc), index_map=lambda i, j: (i, j))

    def block(w_v, gin_v, m_v, v_v, g_v, d_v, nm_v, nv_v):
        @pl.loop(0, SC_BLOCK_ROWS)
        def _(row):
            @pl.loop(0, bc, step=SC_LANES)
            def _(col):
                at = (pl.ds(row, 1), pl.ds(col, SC_LANES))
                gv = gin_v.at[*at][...]
                nm = ADAM_B1 * m_v.at[*at][...] + (1.0 - ADAM_B1) * gv
                nv = ADAM_B2 * v_v.at[*at][...] + (1.0 - ADAM_B2) * (gv * gv)
                g_v.at[*at][...] = gv
                nm_v.at[*at][...] = nm
                nv_v.at[*at][...] = nv
                d_v.at[*at][...] = -ADAM_LR * ((nm * c1) / (jnp.sqrt(nv * c2) + ADAM_EPS) + ADAM_WD * w_v.at[*at][...])

    def kern(*refs):
        ins, outs = refs[:5 * k], refs[5 * k:]
        core = lax.axis_index("c")

        def half(q, hh, mine):
            w_hbm, gm_hbm, go_hbm, m_hbm, v_hbm = ins[5 * q:5 * q + 5]
            rows = pl.ds(hh * h, h)
            pltpu.emit_pipeline(
                block, grid=(h // SC_BLOCK_ROWS, c // bc), in_specs=[spec] * 4, out_specs=[spec] * 4,
                core_axis_name=("sc_core", "sc_tile"), dimension_semantics=(pltpu.PARALLEL, pltpu.PARALLEL),
                trace_scopes=False,
            )(w_hbm.at[rows, :], gm_hbm if mine else go_hbm, m_hbm.at[rows, :], v_hbm.at[rows, :],
              *(o.at[rows, :] for o in outs[4 * q:4 * q + 4]))

        for q in range(k):
            for hh in range(2):
                pl.when(core == hh)(lambda q=q, hh=hh: half(q, hh, True))
                pl.when(core != hh)(lambda q=q, hh=hh: half(q, hh, False))

    sds = jax.ShapeDtypeStruct((r, c), F32)
    return pl.kernel(kern, out_type=[sds] * (4 * k), mesh=mesh, scratch_types=[], name="sc_adamw_halves")(
        *(a for it in items for a in it))


def _wgrad(name, a, b, a_spec, b_spec, m, n, nb, comm):
    def body(a_ref, b_ref, o_ref):
        o_ref[...] = _dot_tn(a_ref[...], b_ref[...]).astype(o_ref.dtype)

    return _pcall(body, [a, b], name=name, grid=(nb,), out_shape=jax.ShapeDtypeStruct((nb, m, n), BF),
                  in_specs=[a_spec, b_spec], out_specs=pl.BlockSpec((None, m, n), lambda j: (j, 0, 0)), comm=comm)


def _wgrad_cols(name, a, b, nb, comm=None):
    t_tok, m = a.shape
    n = b.shape[1] // nb
    return _wgrad(name, a, b, pl.BlockSpec((t_tok, m), lambda j: (0, 0)), pl.BlockSpec((t_tok, n), lambda j: (0, j)),
                  m, n, nb, comm)


def _wgrad_rows(name, a, b, nb, comm=None):
    t_tok, n = b.shape
    m = a.shape[1] // nb
    return _wgrad(name, a, b, pl.BlockSpec((t_tok, m), lambda j: (0, j)), pl.BlockSpec((t_tok, n), lambda j: (0, 0)),
                  m, n, nb, comm)


def _wgrad_a_shared(name, a, b4, comm=None):
    t_tok, m = a.shape
    nb, _, n = b4.shape
    return _wgrad(name, a, b4, pl.BlockSpec((t_tok, m), lambda j: (0, 0)),
                  pl.BlockSpec((None, t_tok, n), lambda j: (j, 0, 0)), m, n, nb, comm)


def _wgrad_b_shared(name, a4, b, comm=None):
    nb, t_tok, m = a4.shape
    n = b.shape[1]
    return _wgrad(name, a4, b, pl.BlockSpec((None, t_tok, m), lambda j: (j, 0, 0)),
                  pl.BlockSpec((t_tok, n), lambda j: (0, 0)), m, n, nb, comm)


def _w4_spec(r, c):
    return pl.BlockSpec((None, r, c), lambda i, j: (j, 0, 0))


FFN_ROW_CHUNK = 256


def _row_chunks(tm):
    rc = FFN_ROW_CHUNK if tm % FFN_ROW_CHUNK == 0 else tm
    return [slice(r, r + rc) for r in range(0, tm, rc)]


def _ffn_fwd(h, ln, wg4, wu4, wd4, comm=None):
    t_tok, d = h.shape
    f = wg4.shape[-2]
    tm = _tile(t_tok, 512)

    def body(h_ref, ln_ref, wg_ref, wu_ref, wd_ref, ho_ref, n_ref, g_ref, u_ref, n_s, acc):
        j = pl.program_id(1)

        @pl.when(j == 0)
        def _():
            xv = h_ref[...]
            nv = (xv * _rstd(xv) * ln_ref[...]).astype(BF)
            n_s[...] = nv
            n_ref[...] = nv
            acc[...] = jnp.zeros_like(acc)

        nv = n_s[...]
        g = _dot_nt(nv, wg_ref[...])
        u = _dot_nt(nv, wu_ref[...])
        g_ref[...] = g.astype(BF)
        u_ref[...] = u.astype(BF)
        a = (g * _sigmoid(g) * u).astype(BF)
        acc[...] += _dot(a, wd_ref[...])

        @pl.when(j == N_CHIPS - 1)
        def _():
            ho_ref[...] = h_ref[...] + 0.5 * acc[...]

    row = pl.BlockSpec((tm, d), lambda i, j: (i, 0))
    gu = pl.BlockSpec((None, tm, f), lambda i, j: (j, i, 0))
    gu_sds = jax.ShapeDtypeStruct((N_CHIPS, t_tok, f), BF)
    return _pcall(
        body, [h, ln, wg4, wu4, wd4], name="ffn_fwd", grid=(t_tok // tm, N_CHIPS),
        out_shape=[jax.ShapeDtypeStruct((t_tok, d), F32), jax.ShapeDtypeStruct((t_tok, d), BF), gu_sds, gu_sds],
        in_specs=[row, pl.BlockSpec((1, d), lambda i, j: (0, 0)), _w4_spec(f, d), _w4_spec(f, d), _w4_spec(f, d)],
        out_specs=[row, row, gu, gu],
        scratch=[pltpu.VMEM((tm, d), BF), pltpu.VMEM((tm, d), F32)], comm=comm)


def _ffn_bwd(dho, h, ln, g4, u4, wg4, wu4, wd4, comm=None):
    t_tok, d = h.shape
    f = wg4.shape[-2]
    tm = _tile(t_tok, 512)

    def body(dho_ref, h_ref, ln_ref, g_ref, u_ref, wg_ref, wu_ref, wd_ref,
             dhi_ref, dln_ref, dg_ref, du_ref, a_ref, dhb_ref, dhb_s, dn_acc):
        i, j = pl.program_id(0), pl.program_id(1)

        @pl.when(j == 0)
        def _():
            dhb = (0.5 * dho_ref[...]).astype(BF)
            dhb_s[...] = dhb
            dhb_ref[...] = dhb
            dn_acc[...] = jnp.zeros_like(dn_acc)

        @pl.when((i == 0) & (j == 0))
        def _():
            dln_ref[...] = jnp.zeros_like(dln_ref)

        for rows in _row_chunks(tm):
            g = g_ref[rows, :].astype(F32)
            u = u_ref[rows, :].astype(F32)
            s = _sigmoid(g)
            sg = g * s
            a_ref[rows, :] = (sg * u).astype(BF)
            da = _dot_nt(dhb_s[rows, :], wd_ref[...])
            dg = (da * u * (s * (1.0 + g * (1.0 - s)))).astype(BF)
            du = (da * sg).astype(BF)
            dg_ref[rows, :] = dg
            du_ref[rows, :] = du
            dn_acc[rows, :] += _dot(dg, wg_ref[...]) + _dot(du, wu_ref[...])

        @pl.when(j == N_CHIPS - 1)
        def _():
            xv = h_ref[...]
            dx, dln = _rms_bwd(dn_acc[...], xv, _rstd(xv), ln_ref[...])
            dln_ref[...] += dln
            dhi_ref[...] = dho_ref[...] + dx

    row = pl.BlockSpec((tm, d), lambda i, j: (i, 0))
    vec = pl.BlockSpec((1, d), lambda i, j: (0, 0))
    gu = pl.BlockSpec((None, tm, f), lambda i, j: (j, i, 0))
    gu_sds = jax.ShapeDtypeStruct((N_CHIPS, t_tok, f), BF)
    return _pcall(
        body, [dho, h, ln, g4, u4, wg4, wu4, wd4], name="ffn_bwd", grid=(t_tok // tm, N_CHIPS),
        out_shape=[jax.ShapeDtypeStruct((t_tok, d), F32), jax.ShapeDtypeStruct((1, d), F32),
                   gu_sds, gu_sds, gu_sds, jax.ShapeDtypeStruct((t_tok, d), BF)],
        in_specs=[row, row, vec, gu, gu, _w4_spec(f, d), _w4_spec(f, d), _w4_spec(f, d)],
        out_specs=[row, vec, gu, gu, gu, row],
        scratch=[pltpu.VMEM((tm, d), BF), pltpu.VMEM((tm, d), F32)], comm=comm)


def _rope_tables(pos_col, inv_freq2, comm=None):
    t_tok = pos_col.shape[0]

    def body(p_ref, f_ref, cos_ref, sin_ref):
        ang = p_ref[...] * f_ref[...]
        lane = lax.broadcasted_iota(jnp.int32, ang.shape, 1)
        s = jnp.sin(ang)
        cos_ref[...] = jnp.cos(ang)
        sin_ref[...] = jnp.where((lane & 1) == 0, -s, s)

    sds = jax.ShapeDtypeStruct((t_tok, 128), F32)
    return _pcall(body, [pos_col, inv_freq2], name="rope_tables", out_shape=[sds, sds], comm=comm)


def _swap_pairs(x):
    lane = lax.broadcasted_iota(jnp.int32, x.shape, 1)
    return jnp.where((lane & 1) == 0, pltpu.roll(x, 127, 1), pltpu.roll(x, 1, 1))


def _mix_in(h, ln, w_in, wm4, b_m, cos_t, sin_t, comm=None):
    t_tok, d = h.shape
    cm = wm4.shape[-1]
    tm = _tile(t_tok, 256)

    def body(h_ref, ln_ref, win_ref, wm_ref, bm_ref, cos_ref, sin_ref,
             u_ref, rq_ref, rk_ref, rv_ref, rg_ref, fq_ref, fk_ref, fv_ref, ff_ref, ga_ref, gb_ref):
        xv = h_ref[...]
        ub = (xv * _rstd(xv) * ln_ref[...]).astype(BF)
        u_ref[...] = ub
        cosv, sinv = cos_ref[...], sin_ref[...]

        def sec(k):
            return _dot_nt(ub, win_ref[k * 512:(k + 1) * 512, :])

        def rot(xh):
            return xh * cosv + _swap_pairs(xh) * sinv

        pq, pk = sec(0), sec(1)
        for hh in range(RET_HEADS):
            sl = slice(hh * RET_DIM, (hh + 1) * RET_DIM)
            rq_ref[:, sl] = rot(pq[:, sl]).astype(BF)
            rk_ref[:, sl] = (rot(pk[:, sl]) * RET_SCALE).astype(BF)
        rv_ref[...] = sec(2).astype(BF)
        rg_ref[...] = sec(3).astype(BF)
        fq_ref[...] = (sec(4) * FOX_SCALE).astype(BF)
        fk_ref[...] = sec(5).astype(BF)
        fv_ref[...] = sec(6).astype(BF)
        ff_ref[...] = _dot_nt(ub, win_ref[FF_COL:FF_COL + 128, :])
        for j in range(N_CHIPS):
            gs = _sigmoid(_dot(ub, wm_ref[j]) + bm_ref[:, j * cm:(j + 1) * cm]).astype(BF)
            col = j * cm
            if col < d:
                ga_ref[:, col:col + cm] = gs
            else:
                gb_ref[:, col - d:col - d + cm] = gs

    row = lambda c: pl.BlockSpec((tm, c), lambda i: (i, 0))
    full = lambda *s: pl.BlockSpec(s, lambda i: (0,) * len(s))
    sds = lambda c, dt: jax.ShapeDtypeStruct((t_tok, c), dt)
    return _pcall(
        body, [h, ln, w_in, wm4, b_m, cos_t, sin_t], name="mix_in", grid=(t_tok // tm,),
        out_shape=[sds(d, BF)] + [sds(512, BF)] * 7 + [sds(128, F32), sds(d, BF), sds(d, BF)],
        in_specs=[row(d), full(1, d), full(IN_PAD, d), full(N_CHIPS, d, cm), full(1, 2 * d), row(128), row(128)],
        out_specs=[row(d)] + [row(512)] * 7 + [row(128), row(d), row(d)], comm=comm)


def _split3(x):
    hi = x.astype(BF)
    r1 = x - hi.astype(F32)
    mid = r1.astype(BF)
    lo = (r1 - mid.astype(F32)).astype(BF)
    return hi, mid, lo


def _aug_lane():
    return lax.broadcasted_iota(jnp.int32, (1, 128), 1) & (FOX_DIM - 1)


def _aug_put(base, k0, parts):
    w = _aug_lane()
    for i, part in enumerate(parts):
        base = jnp.where(w == k0 + i, part, base)
    return base


def _forget_fwd(ffl, b_pad):
    t_tok = ffl.shape[0]
    tb = _tile(t_tok, 256)

    def body(ff_ref, b_ref, aq_ref, ak_ref, cum_s):
        r = lax.broadcasted_iota(jnp.int32, (tb, tb), 0)
        c = lax.broadcasted_iota(jnp.int32, (tb, tb), 1)
        tri = jnp.where(c <= r, 1.0, 0.0).astype(BF)
        carry = jnp.zeros((1, 128), F32)
        for i in range(t_tok // tb):
            z = ff_ref[i * tb:(i + 1) * tb, :] + b_ref[...]
            lf = jnp.minimum(z, 0.0) - jnp.log(1.0 + jnp.exp(-jnp.abs(z)))
            hi, mid, lo = _split3(lf)
            cs = _dot(tri, hi) + _dot(tri, mid) + _dot(tri, lo) + carry
            cum_s[i * tb:(i + 1) * tb, :] = cs
            carry = cs[tb - 1:tb, :]
        x = cum_s[...]
        first = lax.broadcasted_iota(jnp.int32, (1, 128), 1) < FOX_DIM
        w = _aug_lane()
        one = jnp.ones((t_tok, 128), BF)
        zero = jnp.zeros((t_tok, 128), BF)
        for pp in range(FOX_HEADS // 2):
            other = jnp.where(first, x[:, 2 * pp + 1:2 * pp + 2], x[:, 2 * pp:2 * pp + 1])
            parts = _split3(other)
            aq = jnp.where((w >= 3) & (w < 6), one, zero)
            ak = jnp.where((w < 3) | ((w >= 6) & (w < 9)), one, zero)
            aq_ref[:, pp * 128:(pp + 1) * 128] = _aug_put(aq, 0, parts)
            ak_ref[:, pp * 128:(pp + 1) * 128] = _aug_put(ak, 3, [-q for q in parts])

    sds = jax.ShapeDtypeStruct((t_tok, FOX_WIDTH), BF)
    return _pcall(body, [ffl, b_pad], name="forget_fwd", out_shape=[sds, sds],
                  scratch=[pltpu.VMEM((t_tok, 128), F32)])


def _forget_bwd(dcum_t, dcum_q, ffl, b_pad):
    t_tok = ffl.shape[0]
    tb = _tile(t_tok, 256)

    def body(dc_ref, dq_ref, ff_ref, b_ref, dff_ref, db_ref, pad_s, d_s):
        pad_s[...] = jnp.zeros_like(pad_s)
        pad_s[0:FOX_HEADS, :] = dc_ref[...]
        dsum = pad_s[...].T
        lane = lax.broadcasted_iota(jnp.int32, (t_tok, 128), 1)
        for hh in range(FOX_HEADS):
            dsum = dsum + jnp.where(lane == hh, dq_ref[:, hh * FOX_DIM:hh * FOX_DIM + 1], 0.0)
        d_s[...] = dsum
        r = lax.broadcasted_iota(jnp.int32, (tb, tb), 0)
        c = lax.broadcasted_iota(jnp.int32, (tb, tb), 1)
        tri = jnp.where(c >= r, 1.0, 0.0).astype(BF)
        carry = jnp.zeros((1, 128), F32)
        db = jnp.zeros((1, 128), F32)
        for i in reversed(range(t_tok // tb)):
            hi, mid, lo = _split3(d_s[i * tb:(i + 1) * tb, :])
            dlf = _dot(tri, hi) + _dot(tri, mid) + _dot(tri, lo) + carry
            carry = dlf[0:1, :]
            z = ff_ref[i * tb:(i + 1) * tb, :] + b_ref[...]
            dff = dlf * _sigmoid(-z)
            dff_ref[i * tb:(i + 1) * tb, :] = dff.astype(BF)
            db = db + jnp.sum(dff, axis=0, keepdims=True)
        db_ref[...] = db

    return _pcall(
        body, [dcum_t, dcum_q, ffl, b_pad], name="forget_bwd",
        out_shape=[jax.ShapeDtypeStruct((t_tok, 128), BF), jax.ShapeDtypeStruct((1, 128), F32)],
        scratch=[pltpu.VMEM((128, t_tok), F32), pltpu.VMEM((t_tok, 128), F32)])


def _first_half():
    return lax.broadcasted_iota(jnp.int32, (1, 128), 1) < FOX_DIM


def _head_rows(x2, a2, hh):
    return jnp.where(_first_half(), x2, a2) if hh == 0 else jnp.where(_first_half(), a2, x2)


def _head_only(x2, hh):
    zero = jnp.zeros_like(x2)
    return jnp.where(_first_half(), x2, zero) if hh == 0 else jnp.where(_first_half(), zero, x2)


def _causal_diag(s):
    rows = lax.broadcasted_iota(jnp.int32, s.shape, 0)
    cols = lax.broadcasted_iota(jnp.int32, s.shape, 1)
    return jnp.where(cols <= rows, s, NEG)


def _diag_or_below(qi, ki, step):
    pl.when(ki < qi)(lambda: step(False))
    pl.when(ki == qi)(lambda: step(True))


def _tri_rows(s, n):
    qi = sum((s >= r * (r + 1) // 2).astype(jnp.int32) for r in range(1, n))
    return qi, s - (qi * (qi + 1)) // 2


def _tri_cols(s, n):
    ki = sum((s >= k * n - k * (k - 1) // 2).astype(jnp.int32) for k in range(1, n))
    return ki, ki + s - (ki * n - (ki * (ki - 1)) // 2)


def _fox_fwd(fq, fk, fv, aq, ak, comm=None):
    t_tok = fq.shape[0]
    t = _tile(t_tok, 512)
    nq = t_tok // t
    npair = FOX_HEADS // 2

    def body(q_ref, k_ref, v_ref, aq_ref, ak_ref, o_ref, of_ref, aqb_ref, m_s, l_s, acc_s):
        qi, ki = _tri_rows(pl.program_id(1), nq)

        @pl.when(ki == 0)
        def _():
            m_s[...] = jnp.full_like(m_s, NEG)
            l_s[...] = jnp.zeros_like(l_s)
            acc_s[...] = jnp.zeros_like(acc_s)

        def step(diag):
            q2, k2, v2, aq2, ak2 = q_ref[...], k_ref[...], v_ref[...], aq_ref[...], ak_ref[...]
            for hh in range(2):
                s = _dot_nt(_head_rows(q2, aq2, hh), _head_rows(k2, ak2, hh))
                if diag:
                    s = _causal_diag(s)
                m_prev = m_s[hh]
                m_new = jnp.maximum(m_prev, jnp.max(s, axis=1, keepdims=True))
                alpha = jnp.exp(m_prev - m_new)
                p = jnp.exp(s - jnp.tile(m_new, (1, t // 128)))
                l_s[hh] = alpha * l_s[hh] + jnp.sum(p, axis=1, keepdims=True)
                acc_s[hh] = alpha * acc_s[hh] + _dot(p.astype(BF), v2)
                m_s[hh] = m_new

        _diag_or_below(qi, ki, step)

        @pl.when(ki == qi)
        def _():
            first = _first_half()
            o = jnp.where(first, acc_s[0] / l_s[0], acc_s[1] / l_s[1])
            o_ref[...] = o.astype(BF)
            of_ref[...] = o
            other = jnp.where(first, m_s[1] + jnp.log(l_s[1]), m_s[0] + jnp.log(l_s[0]))
            aqb_ref[...] = _aug_put(aq_ref[...], 6, _split3(-other))

    qs = pl.BlockSpec((t, 128), lambda p, s: (_tri_rows(s, nq)[0], p))
    ks = pl.BlockSpec((t, 128), lambda p, s: (_tri_rows(s, nq)[1], p))
    stat = pltpu.VMEM((2, t, 128), F32)
    return _pcall(
        body, [fq, fk, fv, aq, ak], name="fox_fwd", grid=(npair, nq * (nq + 1) // 2),
        out_shape=[jax.ShapeDtypeStruct((t_tok, FOX_WIDTH), BF), jax.ShapeDtypeStruct((t_tok, FOX_WIDTH), F32),
                   jax.ShapeDtypeStruct((t_tok, FOX_WIDTH), BF)],
        in_specs=[qs, ks, ks, qs, ks], out_specs=[qs, qs, qs], scratch=[stat, stat, stat], comm=comm)


def _fox_ds(q2, k2, v2, do2, aq2, ak2, ad2, hh, diag):
    s = _dot_nt(_head_rows(q2, aq2, hh), _head_rows(k2, ak2, hh))
    if diag:
        s = _causal_diag(s)
    p = jnp.exp(s)
    av = jnp.where(_aug_lane() < 3, 1.0, 0.0).astype(BF)
    dp = _dot_nt(_head_rows(do2, ad2, hh), _head_rows(v2, jnp.broadcast_to(av, v2.shape), hh))
    return p, p * dp


def _fox_bwd(fq, fk, fv, do, aqb, ak, ad, comm=None):
    t_tok = fq.shape[0]
    t = _tile(t_tok, 512)
    nq = t_tok // t
    npair = FOX_HEADS // 2
    n_steps = nq * (nq + 1) // 2

    def body(q_ref, k_ref, v_ref, do_ref, aq_ref, ak_ref, ad_ref, dq_ref, dk_ref, dv_ref, dck_ref, dcq_ref,
             dk_s, dv_s, dq_s, rs_s):
        step_id = pl.program_id(1)
        ki, qi = _tri_cols(step_id, nq)

        @pl.when(step_id == 0)
        def _():
            dq_s[...] = jnp.zeros_like(dq_s)
            rs_s[...] = jnp.zeros_like(rs_s)

        @pl.when(qi == ki)
        def _():
            dk_s[...] = jnp.zeros_like(dk_s)
            dv_s[...] = jnp.zeros_like(dv_s)
            dck_ref[...] = jnp.zeros_like(dck_ref)

        rows = pl.ds(qi * t if isinstance(qi, int) else pl.multiple_of(qi * t, t), t)

        def step(diag):
            q2, k2, v2, do2 = q_ref[...], k_ref[...], v_ref[...], do_ref[...]
            dq = []
            for hh in range(2):
                p, ds = _fox_ds(q2, k2, v2, do2, aq_ref[...], ak_ref[...], ad_ref[...], hh, diag)
                dsb = ds.astype(BF)
                dv_s[...] += _dot_tn(p.astype(BF), _head_only(do2, hh))
                dk_s[...] += _dot_tn(dsb, _head_only(q2, hh))
                dq.append(_dot(dsb, k2))
                dck_ref[hh] = dck_ref[hh] - jnp.sum(ds, axis=0, keepdims=True)
                rs_s[hh, rows, :] = rs_s[hh, rows, :] + jnp.sum(ds, axis=1, keepdims=True)
            dq_s[rows, :] = dq_s[rows, :] + jnp.where(_first_half(), dq[0], dq[1])

        _diag_or_below(qi, ki, step)

        @pl.when(qi == nq - 1)
        def _():
            dk_ref[...] = dk_s[...].astype(BF)
            dv_ref[...] = dv_s[...].astype(BF)

        @pl.when(step_id == n_steps - 1)
        def _():
            dq_ref[...] = (dq_s[...] * FOX_SCALE).astype(BF)
            dcq_ref[...] = jnp.where(_first_half(), rs_s[0], rs_s[1])

    qs = pl.BlockSpec((t, 128), lambda p, s: (_tri_cols(s, nq)[1], p))
    ks = pl.BlockSpec((t, 128), lambda p, s: (_tri_cols(s, nq)[0], p))
    cks = pl.BlockSpec((2, 1, t), lambda p, s: (p, 0, _tri_cols(s, nq)[0]))
    seq = pl.BlockSpec((t_tok, 128), lambda p, s: (0, p))
    sds = jax.ShapeDtypeStruct((t_tok, FOX_WIDTH), BF)
    return _pcall(
        body, [fq, fk, fv, do, aqb, ak, ad], name="fox_bwd", grid=(npair, n_steps),
        out_shape=[sds, sds, sds, jax.ShapeDtypeStruct((FOX_HEADS, 1, t_tok), F32),
                   jax.ShapeDtypeStruct((t_tok, FOX_WIDTH), F32)],
        in_specs=[qs, ks, ks, qs, qs, ks, qs], out_specs=[seq, ks, ks, cks, seq],
        scratch=[pltpu.VMEM((t, 128), F32), pltpu.VMEM((t, 128), F32), pltpu.VMEM((t_tok, 128), F32),
                 pltpu.VMEM((2, t_tok, 128), F32)], comm=comm)


def _ret_consts():
    c = RET_CHUNK
    log_gamma = jnp.log1p(-jnp.exp2(-5.0 - jnp.arange(RET_HEADS, dtype=F32)))
    idx = jnp.arange(c, dtype=F32)
    diff = idx[:, None] - idx[None, :]
    dmask = jnp.where(diff >= 0, jnp.exp(log_gamma[:, None, None] * jnp.maximum(diff, 0.0)), 0.0)
    qdec = jnp.exp(log_gamma[:, None] * (idx + 1.0))
    kdec = jnp.exp(log_gamma[:, None] * (c - 1 - idx))
    cdec = jnp.exp(log_gamma * c)
    bc = lambda v: jnp.broadcast_to(v[:, :, None], (RET_HEADS, c, RET_DIM))
    return dmask, bc(qdec), bc(kdec), jnp.broadcast_to(cdec[:, None, None], (RET_HEADS, c, RET_DIM))


def _group_norm(y):
    mu = jnp.mean(y, axis=-1, keepdims=True)
    yc = y - mu
    r = lax.rsqrt(jnp.mean(yc * yc, axis=-1, keepdims=True) + EPS)
    return yc * r, r


def _ret_fwd(rq, rk, rv, rg, consts, comm=None):
    t_tok = rq.shape[0]
    nb = 4 if t_tok % (4 * RET_CHUNK) == 0 else 1
    tr = nb * RET_CHUNK
    n_steps = t_tok // tr
    c = RET_CHUNK

    def body(q_ref, k_ref, v_ref, g_ref, dm_ref, qd_ref, kd_ref, cd_ref, y_ref, yo_ref, st_ref, s_s):
        @pl.when(pl.program_id(0) == 0)
        def _():
            s_s[...] = jnp.zeros_like(s_s)

        for b in range(nb):
            rows = slice(b * c, (b + 1) * c)
            for hh in range(RET_HEADS):
                cols = slice(hh * RET_DIM, (hh + 1) * RET_DIM)
                q, k, v = q_ref[rows, cols], k_ref[rows, cols], v_ref[rows, cols]
                state = s_s[hh]
                st_ref[hh, b] = state
                sc = (_dot_nt(q, k) * dm_ref[hh]).astype(BF)
                y = _dot(sc, v) + _dot((q.astype(F32) * qd_ref[hh]).astype(BF), state.astype(BF))
                s_s[hh] = cd_ref[hh] * state + _dot_tn((k.astype(F32) * kd_ref[hh]).astype(BF), v)
                y_ref[rows, cols] = y
                yn, _ = _group_norm(y)
                gate = g_ref[rows, cols].astype(F32)
                yo_ref[rows, cols] = (yn * (gate * _sigmoid(gate))).astype(BF)

    blk = pl.BlockSpec((tr, RET_WIDTH), lambda i: (i, 0))
    cst = pl.BlockSpec((RET_HEADS, c, RET_DIM), lambda i: (0, 0, 0))
    return _pcall(
        body, [rq, rk, rv, rg, *consts], name="ret_fwd", grid=(n_steps,),
        out_shape=[jax.ShapeDtypeStruct((t_tok, RET_WIDTH), F32), jax.ShapeDtypeStruct((t_tok, RET_WIDTH), BF),
                   jax.ShapeDtypeStruct((RET_HEADS, t_tok // c, RET_DIM, RET_DIM), F32)],
        in_specs=[blk] * 4 + [cst] * 4,
        out_specs=[blk, blk, pl.BlockSpec((RET_HEADS, nb, RET_DIM, RET_DIM), lambda i: (0, i, 0, 0))],
        scratch=[pltpu.VMEM((RET_HEADS, RET_DIM, RET_DIM), F32)], comm=comm)


def _ret_bwd(rq, rk, rv, rg, y_raw, dyo, states, consts, cos_t, sin_t, comm=None):
    t_tok = rq.shape[0]
    nb = 4 if t_tok % (4 * RET_CHUNK) == 0 else 1
    tr = nb * RET_CHUNK
    n_steps = t_tok // tr
    c = RET_CHUNK

    def body(q_ref, k_ref, v_ref, g_ref, y_ref, dyo_ref, st_ref, dm_ref, qd_ref, kd_ref, cd_ref,
             cos_ref, sin_ref, dq_ref, dk_ref, dv_ref, dg_ref, ds_s):
        @pl.when(pl.program_id(0) == 0)
        def _():
            ds_s[...] = jnp.zeros_like(ds_s)

        for b in reversed(range(nb)):
            rows = slice(b * c, (b + 1) * c)
            cosv, sinv = cos_ref[rows, :], sin_ref[rows, :]
            for hh in range(RET_HEADS):
                cols = slice(hh * RET_DIM, (hh + 1) * RET_DIM)
                dm, qd, kd, cd = dm_ref[hh], qd_ref[hh], kd_ref[hh], cd_ref[hh]
                q, k, v = q_ref[rows, cols], k_ref[rows, cols], v_ref[rows, cols]
                yn, r = _group_norm(y_ref[rows, cols])
                gate = g_ref[rows, cols].astype(F32)
                sg = _sigmoid(gate)
                dyo = dyo_ref[rows, cols]
                dg_ref[rows, cols] = (dyo * yn * (sg * (1.0 + gate * (1.0 - sg)))).astype(BF)
                dyn = dyo * (gate * sg)
                dy = r * (dyn - jnp.mean(dyn, axis=-1, keepdims=True)
                          - yn * jnp.mean(dyn * yn, axis=-1, keepdims=True))
                dyb = dy.astype(BF)
                state_b = st_ref[hh, b].astype(BF)
                dstate = ds_s[hh]
                dstate_b = dstate.astype(BF)
                qdb = (q.astype(F32) * qd).astype(BF)
                kdb = (k.astype(F32) * kd).astype(BF)
                sc = (_dot_nt(q, k) * dm).astype(BF)
                dv = _dot_tn(sc, dyb) + _dot(kdb, dstate_b)
                dp = (_dot_nt(dyb, v) * dm).astype(BF)
                dq = _dot(dp, k) + _dot_nt(dyb, state_b) * qd
                dk = (_dot_tn(dp, q) + _dot_nt(v, dstate_b) * kd) * RET_SCALE
                ds_s[hh] = cd * dstate + _dot_tn(qdb, dyb)
                dv_ref[rows, cols] = dv.astype(BF)
                dq_ref[rows, cols] = (dq * cosv - _swap_pairs(dq) * sinv).astype(BF)
                dk_ref[rows, cols] = (dk * cosv - _swap_pairs(dk) * sinv).astype(BF)

    rev = lambda i: n_steps - 1 - i
    blk = pl.BlockSpec((tr, RET_WIDTH), lambda i: (rev(i), 0))
    tab = pl.BlockSpec((tr, RET_DIM), lambda i: (rev(i), 0))
    cst = pl.BlockSpec((RET_HEADS, c, RET_DIM), lambda i: (0, 0, 0))
    sds = jax.ShapeDtypeStruct((t_tok, RET_WIDTH), BF)
    return _pcall(
        body, [rq, rk, rv, rg, y_raw, dyo, states, *consts, cos_t, sin_t], name="ret_bwd",
        grid=(n_steps,), out_shape=[sds] * 4,
        in_specs=[blk] * 6 + [pl.BlockSpec((RET_HEADS, nb, RET_DIM, RET_DIM), lambda i: (0, rev(i), 0, 0))]
        + [cst] * 4 + [tab, tab],
        out_specs=[blk] * 4, scratch=[pltpu.VMEM((RET_HEADS, RET_DIM, RET_DIM), F32)], comm=comm)


def _mix_out(h, y_ret, y_fox, ga, gb, wr4, wf4, wo4, comm=None):
    t_tok, d = h.shape
    cz = wr4.shape[-1]
    ro = wo4.shape[-2]
    tm = _tile(t_tok, 512)

    def body(h_ref, yr_ref, yf_ref, ga_ref, gb_ref, wr_ref, wf_ref, wo_ref, ho_ref, za_ref, zb_ref, mix_ref):
        yr, yf = yr_ref[...], yf_ref[...]
        for j in range(N_CHIPS):
            sl = slice(j * cz, (j + 1) * cz)
            za = _dot(yr, wr_ref[j])
            zb = _dot(yf, wf_ref[j])
            za_ref[:, sl] = za.astype(BF)
            zb_ref[:, sl] = zb.astype(BF)
            mix_ref[:, sl] = (ga_ref[:, sl].astype(F32) * za + gb_ref[:, sl].astype(F32) * zb).astype(BF)
        acc = h_ref[...]
        for j in range(N_CHIPS):
            acc = acc + _dot(mix_ref[:, j * ro:(j + 1) * ro], wo_ref[j])
        ho_ref[...] = acc

    row = lambda c: pl.BlockSpec((tm, c), lambda i: (i, 0))
    full = lambda *s: pl.BlockSpec(s, lambda i: (0,) * len(s))
    sds = lambda dt: jax.ShapeDtypeStruct((t_tok, d), dt)
    return _pcall(
        body, [h, y_ret, y_fox, ga, gb, wr4, wf4, wo4], name="mix_out", grid=(t_tok // tm,),
        out_shape=[sds(F32), sds(BF), sds(BF), sds(BF)],
        in_specs=[row(d), row(RET_WIDTH), row(FOX_WIDTH), row(d), row(d),
                  full(N_CHIPS, RET_WIDTH, cz), full(N_CHIPS, FOX_WIDTH, cz), full(N_CHIPS, ro, d)],
        out_specs=[row(d)] * 4, comm=comm)


def _mix_out_bwd(dh, za, zb, ga, gb, y_fox, wr4, wf4, wo4, comm=None):
    t_tok, d = dh.shape
    cz = wr4.shape[-1]
    ro = wo4.shape[-2]
    tm = _tile(t_tok, 256)

    def body(dh_ref, za_ref, zb_ref, ga_ref, gb_ref, yf_ref, wr_ref, wf_ref, wo_ref,
             dhb_ref, dgp_ref, dza_ref, dzb_ref, dyr_ref, dyf_ref, dl_ref, db_ref):
        @pl.when(pl.program_id(0) == 0)
        def _():
            db_ref[...] = jnp.zeros_like(db_ref)

        dhb = dh_ref[...].astype(BF)
        dhb_ref[...] = dhb
        dyr = jnp.zeros((tm, RET_WIDTH), F32)
        dyf = jnp.zeros((tm, FOX_WIDTH), F32)
        for j in range(N_CHIPS):
            sl = slice(j * ro, (j + 1) * ro)
            dmix = _dot_nt(dhb, wo_ref[j])
            ga, gb = ga_ref[:, sl].astype(F32), gb_ref[:, sl].astype(F32)
            dza = (dmix * ga).astype(BF)
            dzb = (dmix * gb).astype(BF)
            dza_ref[:, sl] = dza
            dzb_ref[:, sl] = dzb
            dga = dmix * za_ref[:, sl].astype(F32) * ga * (1.0 - ga)
            dgb = dmix * zb_ref[:, sl].astype(F32) * gb * (1.0 - gb)
            dgp_ref[:, sl] = dga.astype(BF)
            dgp_ref[:, d + j * ro:d + (j + 1) * ro] = dgb.astype(BF)
            db_ref[:, sl] += jnp.sum(dga, axis=0, keepdims=True)
            db_ref[:, d + j * ro:d + (j + 1) * ro] += jnp.sum(dgb, axis=0, keepdims=True)
        for j in range(N_CHIPS):
            sl = slice(j * cz, (j + 1) * cz)
            dyr = dyr + _dot_nt(dza_ref[:, sl], wr_ref[j])
            dyf = dyf + _dot_nt(dzb_ref[:, sl], wf_ref[j])
        dyr_ref[...] = dyr
        dyfb = dyf.astype(BF)
        dyf_ref[...] = dyfb
        prod = dyfb.astype(F32) * yf_ref[...]
        first = _first_half()
        for pp in range(FOX_HEADS // 2):
            blk = prod[:, pp * 128:(pp + 1) * 128]
            s0 = jnp.sum(jnp.where(first, blk, 0.0), axis=1, keepdims=True)
            s1 = jnp.sum(jnp.where(first, 0.0, blk), axis=1, keepdims=True)
            parts = _split3(-jnp.where(first, s1, s0))
            dl_ref[:, pp * 128:(pp + 1) * 128] = _aug_put(jnp.zeros((tm, 128), BF), 0, parts)

    row = lambda c: pl.BlockSpec((tm, c), lambda i: (i, 0))
    full = lambda *s: pl.BlockSpec(s, lambda i: (0,) * len(s))
    sds = lambda c, dt: jax.ShapeDtypeStruct((t_tok, c), dt)
    return _pcall(
        body, [dh, za, zb, ga, gb, y_fox, wr4, wf4, wo4], name="mix_out_bwd", grid=(t_tok // tm,),
        out_shape=[sds(d, BF), sds(2 * d, BF), sds(d, BF), sds(d, BF), sds(RET_WIDTH, F32),
                   sds(FOX_WIDTH, BF), sds(FOX_WIDTH, BF), jax.ShapeDtypeStruct((1, 2 * d), F32)],
        in_specs=[row(d)] * 5 + [row(FOX_WIDTH), full(N_CHIPS, RET_WIDTH, cz), full(N_CHIPS, FOX_WIDTH, cz),
                                 full(N_CHIPS, ro, d)],
        out_specs=[row(d), row(2 * d), row(d), row(d), row(RET_WIDTH), row(FOX_WIDTH), row(FOX_WIDTH),
                   full(1, 2 * d)],
        comm=comm)


def _mix_in_bwd(dh, h, ln, parts, dff, dgpre, w_in, wm4, comm=None):
    t_tok, d = h.shape
    cm = wm4.shape[-1]
    tm = _tile(t_tok, 256)

    def body(dh_ref, h_ref, ln_ref, p0, p1, p2, p3, p4, p5, p6, dff_ref, dgp_ref, win_ref, wm_ref,
             dhi_ref, dln_ref, dproj_ref):
        @pl.when(pl.program_id(0) == 0)
        def _():
            dln_ref[...] = jnp.zeros_like(dln_ref)

        for k, pr in enumerate((p0, p1, p2, p3, p4, p5, p6)):
            dproj_ref[:, k * 512:(k + 1) * 512] = pr[...]
        dproj_ref[:, FF_COL:FF_COL + 128] = dff_ref[...]
        dproj_ref[:, FF_COL + 128:] = jnp.zeros((tm, IN_PAD - FF_COL - 128), BF)
        du = _dot(dproj_ref[...], win_ref[...])
        for j in range(N_CHIPS):
            du = du + _dot_nt(dgp_ref[:, j * cm:(j + 1) * cm], wm_ref[j])
        xv = h_ref[...]
        dx, dln = _rms_bwd(du, xv, _rstd(xv), ln_ref[...])
        dln_ref[...] += dln
        dhi_ref[...] = dh_ref[...] + dx

    row = lambda c: pl.BlockSpec((tm, c), lambda i: (i, 0))
    full = lambda *s: pl.BlockSpec(s, lambda i: (0,) * len(s))
    return _pcall(
        body, [dh, h, ln, *parts, dff, dgpre, w_in, wm4], name="mix_in_bwd", grid=(t_tok // tm,),
        out_shape=[jax.ShapeDtypeStruct((t_tok, d), F32), jax.ShapeDtypeStruct((1, d), F32),
                   jax.ShapeDtypeStruct((t_tok, IN_PAD), BF)],
        in_specs=[row(d), row(d), full(1, d)] + [row(512)] * 7 + [row(128), row(2 * d), full(IN_PAD, d),
                                                                   full(N_CHIPS, d, cm)],
        out_specs=[row(d), full(1, d), row(IN_PAD)], comm=comm)


def _tail(h, p, target, ln_ple, ln_fin, wpg4, wpl4, comm=None):
    t_tok, d = h.shape
    pd = p.shape[1]
    rg = wpg4.shape[-2]
    cp = wpl4.shape[-1]
    tm = _tile(t_tok, 256)

    def body(h_ref, p_ref, t_ref, lp_ref, lf_ref, wg_ref, wp_ref,
             dh_ref, n_ref, dgp_ref, dpe_ref, pb_ref, loss_ref, dlf_ref, dlp_ref, pe_s, dn_s):
        @pl.when(pl.program_id(0) == 0)
        def _():
            loss_ref[...] = jnp.zeros_like(loss_ref)
            dlf_ref[...] = jnp.zeros_like(dlf_ref)
            dlp_ref[...] = jnp.zeros_like(dlp_ref)

        xv = h_ref[...]
        r3 = _rstd(xv)
        nb = (xv * r3 * lp_ref[...]).astype(BF)
        n_ref[...] = nb
        pb = p_ref[...].astype(BF)
        pb_ref[...] = pb
        pgpre = jnp.zeros((tm, d), F32)
        for j in range(N_CHIPS):
            pgpre = pgpre + _dot(nb[:, j * rg:(j + 1) * rg], wg_ref[j])
            pe_s[:, j * cp:(j + 1) * cp] = _dot(pb, wp_ref[j])
        pg = _sigmoid(pgpre)
        pe = pe_s[...]
        h4 = xv + pg * pe
        r4 = _rstd(h4)
        err = h4 * r4 * lf_ref[...] - t_ref[...]
        loss_ref[...] += 0.5 * jnp.sum(jnp.sum(err * err, axis=1, keepdims=True), axis=0, keepdims=True) / d
        dh4, dlf = _rms_bwd(err * (1.0 / d), h4, r4, lf_ref[...])
        dlf_ref[...] += dlf
        dpe_ref[...] = (dh4 * pg).astype(BF)
        dgp = (dh4 * pe * pg * (1.0 - pg)).astype(BF)
        dgp_ref[...] = dgp
        for j in range(N_CHIPS):
            dn_s[:, j * rg:(j + 1) * rg] = _dot_nt(dgp, wg_ref[j])
        dx, dlp = _rms_bwd(dn_s[...], xv, r3, lp_ref[...])
        dlp_ref[...] += dlp
        dh_ref[...] = dh4 + dx

    row = lambda c: pl.BlockSpec((tm, c), lambda i: (i, 0))
    full = lambda *s: pl.BlockSpec(s, lambda i: (0,) * len(s))
    sds = lambda c, dt: jax.ShapeDtypeStruct((t_tok, c), dt)
    vec = jax.ShapeDtypeStruct((1, d), F32)
    return _pcall(
        body, [h, p, target, ln_ple, ln_fin, wpg4, wpl4], name="tail", grid=(t_tok // tm,),
        out_shape=[sds(d, F32), sds(d, BF), sds(d, BF), sds(d, BF), sds(pd, BF),
                   jax.ShapeDtypeStruct((1, 128), F32), vec, vec],
        in_specs=[row(d), row(pd), row(d), full(1, d), full(1, d), full(N_CHIPS, rg, d), full(N_CHIPS, pd, cp)],
        out_specs=[row(d), row(d), row(d), row(d), row(pd), full(1, 128), full(1, d), full(1, d)],
        scratch=[pltpu.VMEM((tm, d), F32), pltpu.VMEM((tm, d), F32)], comm=comm)


BIG = ["w_ffn1_gate", "w_ffn1_up", "w_ffn1_down", "w_in", "w_merge", "w_ret_out", "w_fox_out", "w_out",
       "w_ffn2_gate", "w_ffn2_up", "w_ffn2_down", "w_ple", "w_ple_gate"]
SMALL = ["ln_ffn1", "ln_mix", "b_forget", "b_merge", "ln_ffn2", "ln_ple", "ln_final"]
WEIGHTS = ["ln_ffn1", "w_ffn1_gate", "w_ffn1_up", "w_ffn1_down", "ln_mix", "w_in", "b_forget", "w_merge", "b_merge",
           "w_ret_out", "w_fox_out", "w_out", "ln_ffn2", "w_ffn2_gate", "w_ffn2_up", "w_ffn2_down", "ln_ple",
           "w_ple", "w_ple_gate", "ln_final"]


TRANSPOSED = {"w_ffn1_gate", "w_ffn1_up", "w_ffn2_gate", "w_ffn2_up", "w_in"}
IN_ROWS_PAD = -(-(IN_COLS // N_CHIPS) // 32) * 32


def _pack_small(vals, loss_row):
    rows = [loss_row]
    for name in SMALL:
        v = vals[name].reshape(-1)
        n = -(-v.shape[0] // 128) * 128
        rows.append(jnp.pad(v, (0, n - v.shape[0])).reshape(n // 128, 128))
    packed = jnp.concatenate(rows, axis=0)
    pad = -packed.shape[0] % 8
    return jnp.pad(packed, ((0, pad), (0, 0)))


def _unpack_small(packed, sizes):
    out, r = {}, 1
    for name in SMALL:
        n = sizes[name]
        nr = -(-n // 128)
        out[name] = packed[r:r + nr].reshape(1, nr * 128)[:, :n]
        r += nr
    return out


class _Stage:
    def __init__(self, comm, finish):
        self.comm, self.finish, self.result = comm, finish, None


def _hosted(fn, *a, stages=()):
    if not stages:
        return fn(*a)
    outs, couts = fn(*a, comm=_merge([st.comm for st in stages]))
    for st, o in zip(stages, _split_outs([st.comm for st in stages], couts)):
        st.result = st.finish(o)
    return outs


class _Reducer:
    def __init__(self):
        self.done = {}

    def swap(self, grads):
        names = list(grads)
        return _Stage(_c_half_swap([grads[n] for n in names]),
                      lambda outs: {n: _add_halves(grads[n], o) for n, o in zip(names, outs)})

    def exchange(self, parts):
        names = list(parts)
        return _Stage(_c_chip_exchange([parts[n] for n in names]),
                      lambda outs: {n: _sum_chips(parts[n], o) for n, o in zip(names, outs)})

    def join(self, halves):
        names = list(halves)
        return _Stage(_c_join([halves[n] for n in names]),
                      lambda outs: self.done.update({n: (halves[n], o) for n, o in zip(names, outs)}))


def kernel(x, p, positions, ln_ffn1, w_ffn1_gate, w_ffn1_up, w_ffn1_down, ln_mix, w_in, b_forget, w_merge, b_merge, w_ret_out, w_fox_out, w_out, ln_ffn2, w_ffn2_gate, w_ffn2_up, w_ffn2_down, ln_ple, w_ple, w_ple_gate, ln_final, loss_target, m_ln_ffn1, m_w_ffn1_gate, m_w_ffn1_up, m_w_ffn1_down, m_ln_mix, m_w_in, m_b_forget, m_w_merge, m_b_merge, m_w_ret_out, m_w_fox_out, m_w_out, m_ln_ffn2, m_w_ffn2_gate, m_w_ffn2_up, m_w_ffn2_down, m_ln_ple, m_w_ple, m_w_ple_gate, m_ln_final, v_ln_ffn1, v_w_ffn1_gate, v_w_ffn1_up, v_w_ffn1_down, v_ln_mix, v_w_in, v_b_forget, v_w_merge, v_b_merge, v_w_ret_out, v_w_fox_out, v_w_out, v_ln_ffn2, v_w_ffn2_gate, v_w_ffn2_up, v_w_ffn2_down, v_ln_ple, v_w_ple, v_w_ple_gate, v_ln_final):
    args = dict(locals())
    w = {n: args[n] for n in WEIGHTS}
    m = {n: args["m_" + n] for n in WEIGHTS}
    v = {n: args["v_" + n] for n in WEIGHTS}
    d = x.shape[-1]
    t_tok = x.shape[1]
    xs, ps, target = x[0], p[0, 0], loss_target[0]
    small = {n: w[n].reshape(1, -1) for n in SMALL}

    def to2d(n, a):
        if n in TRANSPOSED:
            return a[0].T
        return a.reshape(a.shape[-2], a.shape[-1]) if a.ndim == 3 else a.reshape(1, -1)

    def from2d(n, a):
        return a.T[None] if n in TRANSPOSED else a.reshape(w[n].shape)

    def padded(n, a):
        return jnp.pad(a, ((0, IN_ROWS_PAD - a.shape[0]), (0, 0))) if n == "w_in" else a

    core = lax.axis_index("c")
    me = 2 * lax.axis_index("x") + lax.axis_index("y")
    shard = {}
    for n in BIG:
        s2 = padded(n, to2d(n, w[n]).astype(BF))
        shard[n] = s2.reshape(1, 2, s2.shape[0] // 2, s2.shape[1])
    full = {}

    def gather(names):
        bufs = [lax.dynamic_update_slice(jnp.zeros((N_CHIPS,) + shard[n].shape[1:], BF), shard[n], (me, 0, 0, 0))
                for n in names]

        def finish(outs):
            full.update({n: o.reshape(N_CHIPS, 2 * o.shape[2], o.shape[3]) for n, o in zip(names, outs)})

        return _Stage(_c_all_gather(bufs), finish)

    half = RET_DIM // 2
    inv_freq = 1.0 / (ROPE_BASE ** (jnp.arange(half, dtype=F32) / half))
    cos_t, sin_t = _hosted(_rope_tables, positions[0].astype(F32).reshape(t_tok, 1),
                           jnp.repeat(inv_freq, 2).reshape(1, RET_DIM),
                           stages=[gather(["w_ffn1_gate", "w_ffn1_up", "w_ffn1_down"])])
    consts = _ret_consts()
    b_pad = jnp.pad(small["b_forget"], ((0, 0), (0, 128 - FOX_HEADS)))

    h1, n1, g1, u1 = _hosted(
        _ffn_fwd, xs, small["ln_ffn1"], full["w_ffn1_gate"], full["w_ffn1_up"], full["w_ffn1_down"],
        stages=[gather(["w_in", "w_merge", "w_ret_out", "w_fox_out", "w_out", "w_ple_gate", "w_ple"])])
    w_in_full = jnp.pad(full["w_in"][:, :IN_COLS // N_CHIPS].reshape(IN_COLS, d), ((0, IN_PAD - IN_COLS), (0, 0)))
    u, rq, rk, rv, rg, fq, fk, fv, ffl, ga, gb = _mix_in(
        h1, small["ln_mix"], w_in_full, full["w_merge"], small["b_merge"], cos_t, sin_t)
    aq, ak = _forget_fwd(ffl, b_pad)
    y_raw, y_ret, states = _ret_fwd(rq, rk, rv, rg, consts)
    y_fox, y_fox32, aqb = _hosted(_fox_fwd, fq, fk, fv, aq, ak,
                                  stages=[gather(["w_ffn2_gate", "w_ffn2_up", "w_ffn2_down"])])
    h2, za, zb, mix = _mix_out(h1, y_ret, y_fox, ga, gb, full["w_ret_out"], full["w_fox_out"], full["w_out"])
    h3, n2, g2, u2 = _ffn_fwd(h2, small["ln_ffn2"], full["w_ffn2_gate"], full["w_ffn2_up"], full["w_ffn2_down"])

    red = _Reducer()
    dh3, n3, dpgpre, dpe, pb, loss, dln_final, dln_ple = _tail(
        h3, ps, target, small["ln_ple"], small["ln_final"], full["w_ple_gate"], full["w_ple"])
    g_f2 = dict(w_ple_gate=_wgrad_rows("wgrad_ple_gate", n3, dpgpre, N_CHIPS),
                w_ple=_wgrad_cols("wgrad_ple", pb, dpe, N_CHIPS))
    dh2, dln_ffn2, dg2, du2, a2, dhb3 = _ffn_bwd(
        dh3, h2, small["ln_ffn2"], g2, u2, full["w_ffn2_gate"], full["w_ffn2_up"], full["w_ffn2_down"])
    g_f2["w_ffn2_gate"] = _wgrad_b_shared("wgrad_ffn2_gate", dg2, n2)
    g_f2["w_ffn2_up"] = _wgrad_b_shared("wgrad_ffn2_up", du2, n2)
    g_f2["w_ffn2_down"] = _wgrad_b_shared("wgrad_ffn2_down", a2, dhb3)

    sw_f2 = red.swap(g_f2)
    dhb2, dgpre, dza, dzb, dy_ret, dy_fox, ad, db_merge = _hosted(
        _mix_out_bwd, dh2, za, zb, ga, gb, y_fox32, full["w_ret_out"], full["w_fox_out"], full["w_out"],
        stages=[sw_f2])
    g_br = dict(w_out=_wgrad_rows("wgrad_out", mix, dhb2, N_CHIPS),
                w_ret_out=_wgrad_cols("wgrad_ret_out", y_ret, dza, N_CHIPS),
                w_fox_out=_wgrad_cols("wgrad_fox_out", y_fox, dzb, N_CHIPS))

    sw_br = red.swap(g_br)
    drq, drk, drv, drg = _hosted(_ret_bwd, rq, rk, rv, rg, y_raw, dy_ret, states, consts, cos_t, sin_t,
                                 stages=[sw_br])
    ex_f2, ex_br = red.exchange(sw_f2.result), red.exchange(sw_br.result)
    dfq, dfk, dfv, dcum_t3, dcum_q = _hosted(_fox_bwd, fq, fk, fv, dy_fox, aqb, ak, ad, stages=[ex_f2, ex_br])
    dff, db_forget = _forget_bwd(dcum_t3.reshape(FOX_HEADS, t_tok), dcum_q, ffl, b_pad)
    dh1, dln_mix, dproj = _hosted(
        _mix_in_bwd, dh2, h1, small["ln_mix"], (drq, drk, drv, drg, dfq, dfk, dfv), dff, dgpre, w_in_full,
        full["w_merge"], stages=[red.join(ex_f2.result), red.join(ex_br.result)])

    results = {}
    for names in (["w_ffn2_gate", "w_ffn2_up", "w_ffn2_down"], ["w_out", "w_ple_gate"], ["w_ret_out", "w_fox_out"],
                  ["w_ple"]):
        res = _sc_adamw_halves([(to2d(n, w[n]), *red.done[n], to2d(n, m[n]), to2d(n, v[n])) for n in names])
        for q, n in enumerate(names):
            results[n] = tuple(from2d(n, a) for a in res[4 * q:4 * q + 4])

    dx, dln_ffn1, dg1, du1, a1, dhb1 = _ffn_bwd(
        dh1, xs, small["ln_ffn1"], g1, u1, full["w_ffn1_gate"], full["w_ffn1_up"], full["w_ffn1_down"])
    g_f1g = _wgrad_b_shared("wgrad_ffn1_gate", dg1, n1)
    sw_f1g = red.swap(dict(w_ffn1_gate=g_f1g))
    g_f1u = _hosted(_wgrad_b_shared, "wgrad_ffn1_up", du1, n1, stages=[sw_f1g])
    ex_f1g, sw_f1u = red.exchange(sw_f1g.result), red.swap(dict(w_ffn1_up=g_f1u))
    g_f1d = _hosted(_wgrad_b_shared, "wgrad_ffn1_down", a1, dhb1, stages=[ex_f1g, sw_f1u])

    ex_f1u, sw_f1d = red.exchange(sw_f1u.result), red.swap(dict(w_ffn1_down=g_f1d))
    g_in = _hosted(_wgrad_rows, "wgrad_in", dproj, u, IN_PAD // 512,
                   stages=[ex_f1u, sw_f1d, red.join(ex_f1g.result)])
    g_in = g_in.reshape(IN_PAD, d)[:IN_COLS].reshape(N_CHIPS, IN_COLS // N_CHIPS, d)
    g_in = jnp.pad(g_in, ((0, 0), (0, IN_ROWS_PAD - IN_COLS // N_CHIPS), (0, 0)))
    ex_f1d, sw_in = red.exchange(sw_f1d.result), red.swap(dict(w_in=g_in))
    g_mrg = _hosted(_wgrad_cols, "wgrad_merge", u, dgpre, N_CHIPS,
                    stages=[ex_f1d, sw_in, red.join(ex_f1u.result)])

    small_grads = dict(ln_ffn1=dln_ffn1, ln_mix=dln_mix, b_forget=db_forget[:, :FOX_HEADS], b_merge=db_merge,
                       ln_ffn2=dln_ffn2, ln_ple=dln_ple, ln_final=dln_final)
    sizes = {n: w[n].size for n in SMALL}
    ex_in, sw_mrg = red.exchange(sw_in.result), red.swap(dict(w_merge=g_mrg))
    reduced = _hosted(_all_reduce_small, _pack_small(small_grads, loss),
                      stages=[ex_in, sw_mrg, red.join(ex_f1d.result)])
    gsum = _unpack_small(reduced, sizes)
    loss = reduced[0, 0]
    ex_mrg = red.exchange(sw_mrg.result)
    _hosted(_exchange_only, stages=[ex_mrg, red.join(ex_in.result)])
    _hosted(_exchange_only, stages=[red.join(ex_mrg.result)])

    def update(names, stages=()):
        w2, m2, v2 = ([to2d(n, a[n]) for n in names] for a in (w, m, v))
        n = names[0]
        if n in gsum or n == "w_in":
            if n in gsum:
                g2 = gsum[n]
            else:
                mine, other = red.done[n]
                g2 = jnp.where(core == 0, jnp.concatenate([mine, other]), jnp.concatenate([other, mine]))
                g2 = g2[:w2[0].shape[0]]
            if n == "w_in":
                rows3 = lambda a: jnp.transpose(a, (2, 0, 1))
                g3 = g2.reshape(g2.shape[0], 1, g2.shape[1])
                res = [g3] + _hosted(_adamw, rows3(w[n]), g3, rows3(m[n]), rows3(v[n]), stages=stages)
                results[n] = tuple(jnp.transpose(a, (1, 2, 0)) for a in res)
                return
            res = [g2] + _hosted(_adamw, w2[0], g2, m2[0], v2[0], stages=stages)
        else:
            res = _hosted(_adamw_halves, [(w2[q], *red.done[names[q]], m2[q], v2[q]) for q in range(len(names))],
                          stages=stages)
        for q, name in enumerate(names):
            results[name] = tuple(from2d(name, a) for a in res[4 * q:4 * q + 4])

    update(["w_ffn1_gate", "w_ffn1_up", "w_ffn1_down"])
    for n in WEIGHTS:
        if n not in results:
            update([n])

    outs = [[results[n][k] for n in WEIGHTS] for k in range(4)]
    return (loss, dx[None], *outs[0], *outs[1], *outs[2], *outs[3])
```

```python
import functools
import operator

import jax
import jax.numpy as jnp
from jax import lax
from jax.experimental import pallas as pl
from jax.experimental.pallas import tpu as pltpu
from jax.experimental.pallas import tpu_sc as plsc

F32 = jnp.float32
BF = jnp.bfloat16
MESH = pl.DeviceIdType.MESH

EPS = 1e-6
ROPE_BASE = 10000.0
N_CHIPS = 4
RET_HEADS = 4
RET_DIM = 128
RET_WIDTH = RET_HEADS * RET_DIM
RET_CHUNK = 128
RET_SCALE = RET_DIM ** -0.5
FOX_HEADS = 8
FOX_DIM = 64
FOX_WIDTH = FOX_HEADS * FOX_DIM
FOX_SCALE = FOX_DIM ** -0.5
IN_COLS = 4 * RET_WIDTH + 3 * FOX_WIDTH + FOX_HEADS
IN_PAD = 4096
FF_COL = 4 * RET_WIDTH + 3 * FOX_WIDTH
NEG = -1e30

ADAM_LR = 0.001
ADAM_B1 = 0.9
ADAM_B2 = 0.999
ADAM_EPS = 1e-08
ADAM_WD = 0.01
ADAM_STEP = 10

VMEM_LIMIT = 52 * 1024 * 1024

RELAY_MIN_STEPS = 16

NT = (((1,), (1,)), ((), ()))
TN = (((0,), (0,)), ((), ()))

HBM_SPEC = pl.BlockSpec(memory_space=pltpu.HBM)
VMEM_SPEC = pl.BlockSpec(memory_space=pltpu.VMEM)


def _dot(a, b):
    return jnp.dot(a, b, preferred_element_type=F32)


def _dot_nt(a, b):
    return lax.dot_general(a, b, NT, preferred_element_type=F32)


def _dot_tn(a, b):
    return lax.dot_general(a, b, TN, preferred_element_type=F32)


def _rstd(xv):
    return lax.rsqrt(jnp.mean(xv * xv, axis=-1, keepdims=True) + EPS)


def _rms_bwd(dn, xv, r, ln):
    xh = xv * r
    dxh = dn * ln
    dx = r * (dxh - xh * jnp.mean(dxh * xh, axis=-1, keepdims=True))
    return dx, jnp.sum(dn * xh, axis=0, keepdims=True)


def _sigmoid(x):
    return jax.nn.sigmoid(x)


def _tile(n, pref):
    return pref if n % pref == 0 else n


def _row_tile(n, cap):
    best = [t for t in range(16, min(n, cap) + 1, 16) if n % t == 0]
    return best[-1] if best else n


class _Comm:
    def __init__(self, ins, out_shapes, sems, start, wait, aliases=None, relay=None):
        self.ins, self.out_shapes, self.sems, self.start, self.wait = list(ins), list(out_shapes), list(sems), start, wait
        self.aliases = dict(aliases or {})
        self.relay = relay


def _merge(comms):
    comms = [c for c in comms if c is not None]
    if not comms:
        return None
    bounds, ni, no, ns = [], 0, 0, 0
    for c in comms:
        bounds.append((ni, no, ns))
        ni, no, ns = ni + len(c.ins), no + len(c.out_shapes), ns + len(c.sems)

    def run(which):
        def f(ins, outs, sems, **kw):
            for c, (i, o, s) in zip(comms, bounds):
                fn = getattr(c, which)
                if fn is not None:
                    fn(ins[i:i + len(c.ins)], outs[o:o + len(c.out_shapes)], sems[s:s + len(c.sems)],
                       **(kw if c.relay is not None else {}))
        return f

    aliases = {i + a: o + b for c, (i, o, _) in zip(comms, bounds) for a, b in c.aliases.items()}
    relay = run("relay") if any(c.relay is not None for c in comms) else None
    return _Comm([a for c in comms for a in c.ins], [a for c in comms for a in c.out_shapes],
                 [a for c in comms for a in c.sems], run("start"), run("wait"), aliases, relay)


def _split_outs(comms, outs):
    res, o = [], 0
    for c in comms:
        if c is not None:
            res.append(list(outs[o:o + len(c.out_shapes)]))
            o += len(c.out_shapes)
    return res


def _pcall(body, args, *, name, out_shape, grid=(), in_specs=None, out_specs=None, scratch=(), comm=None,
           prefetch=()):
    many = isinstance(out_shape, (list, tuple))
    outs = list(out_shape) if many else [out_shape]
    n_pre, n_in, n_out, n_scr = len(prefetch), len(args), len(outs), len(scratch)
    if in_specs is None:
        in_specs, out_specs = [VMEM_SPEC] * n_in, [VMEM_SPEC] * n_out
    else:
        in_specs, out_specs = list(in_specs), (list(out_specs) if many else [out_specs])
    params = pltpu.CompilerParams(dimension_semantics=("arbitrary",) * len(grid), vmem_limit_bytes=VMEM_LIMIT)
    scalars = [jnp.reshape(s, (1,)).astype(jnp.int32) for s in prefetch]
    ci, co = (len(comm.ins), len(comm.out_shapes)) if comm is not None else (0, 0)

    def wrapped(*refs):
        pre, refs = refs[:n_pre], refs[n_pre:]
        a, ca = refs[:n_in], refs[n_in:n_in + ci]
        o = refs[n_in + ci:n_in + ci + n_out]
        cout = refs[n_in + ci + n_out:n_in + ci + n_out + co]
        s = refs[n_in + ci + n_out + co:n_in + ci + n_out + co + n_scr]
        csem = refs[n_in + ci + n_out + co + n_scr:]
        if comm is None:
            body(*pre, *a, *o, *s)
        elif grid:
            step = functools.reduce(lambda acc, k: acc * grid[k] + pl.program_id(k), range(len(grid)), 0)
            n_steps = functools.reduce(operator.mul, grid)
            relayed = comm.relay is not None and n_steps >= RELAY_MIN_STEPS
            pl.when(step == 0)(lambda: comm.start(ca, cout, csem))
            if relayed:
                pl.when(step == n_steps - n_steps // 8)(lambda: comm.relay(ca, cout, csem))
            body(*pre, *a, *o, *s)
            pl.when(step == n_steps - 1)(lambda: comm.wait(ca, cout, csem, **({"relayed": True} if relayed else {})))
        else:
            comm.start(ca, cout, csem)
            body(*pre, *a, *o, *s)
            comm.wait(ca, cout, csem)

    c_ins, c_outs, c_sems, aliases = ([], [], [], {}) if comm is None else (
        comm.ins, comm.out_shapes, comm.sems, {n_pre + n_in + i: n_out + o for i, o in comm.aliases.items()})
    all_in, all_out = in_specs + [HBM_SPEC] * ci, out_specs + [HBM_SPEC] * co
    all_scr = list(scratch) + c_sems
    if grid:
        args = [pltpu.with_memory_space_constraint(a, pltpu.HBM) for a in args]
    c_ins = [pltpu.with_memory_space_constraint(a, pltpu.HBM) for a in c_ins]
    if n_pre:
        spec = dict(grid_spec=pltpu.PrefetchScalarGridSpec(
            num_scalar_prefetch=n_pre, grid=grid, in_specs=all_in, out_specs=all_out, scratch_shapes=all_scr))
    else:
        spec = dict(grid=grid, in_specs=all_in, out_specs=all_out, scratch_shapes=all_scr)
    res = pl.pallas_call(wrapped, name=name, out_shape=outs + c_outs, input_output_aliases=aliases,
                         compiler_params=params, **spec)(*scalars, *args, *c_ins)
    mine = list(res[:n_out])
    mine = mine if many else mine[0]
    return mine if comm is None else (mine, list(res[n_out:]))


def _peer_chips(x, y):
    return [(1 - x, y), (x, 1 - y), (1 - x, 1 - y)]


def _c_all_gather(bufs):
    n = len(bufs)

    def copies(ins, outs, sems):
        send_sems, recv_sems, fwd_send, fwd_recv = sems
        x, y, c = lax.axis_index("x"), lax.axis_index("y"), lax.axis_index("c")
        me = 2 * x + y
        peers = _peer_chips(x, y)
        chip = [2 * px + py for px, py in peers]

        def ici(g, j, slot):
            return pltpu.make_async_remote_copy(
                src_ref=outs[g].at[me, c], dst_ref=outs[g].at[slot, c], send_sem=send_sems.at[g, j],
                recv_sem=recv_sems.at[g, j], device_id=(*peers[j], c), device_id_type=MESH)

        def d2d(g, j, half):
            return pltpu.make_async_remote_copy(
                src_ref=outs[g].at[chip[j], half], dst_ref=outs[g].at[chip[j], half], send_sem=fwd_send.at[g, j],
                recv_sem=fwd_recv.at[g, j], device_id=(x, y, 1 - c), device_id_type=MESH)

        pairs = [(g, j) for g in range(n) for j in range(3)]
        sends = [ici(g, j, me) for g, j in pairs]
        recvs = [ici(g, j, chip[j]) for g, j in pairs]
        passes = [d2d(g, j, c) for g, j in pairs]
        passed = [d2d(g, j, 1 - c) for g, j in pairs]
        return sends, recvs, passes, passed

    def start(ins, outs, sems):
        for cp in copies(ins, outs, sems)[0]:
            cp.start()

    def relay(ins, outs, sems):
        _, recvs, passes, _ = copies(ins, outs, sems)
        for rcv, fwd in zip(recvs, passes):
            rcv.wait_recv()
            fwd.start()

    def wait(ins, outs, sems, relayed=False):
        if not relayed:
            relay(ins, outs, sems)
        sends, _, passes, passed = copies(ins, outs, sems)
        for cp in passed:
            cp.wait_recv()
        for cp in sends + passes:
            cp.wait_send()

    pair_sems = pltpu.SemaphoreType.DMA((n, 3))
    return _Comm(bufs, [jax.ShapeDtypeStruct(s.shape, s.dtype) for s in bufs], [pair_sems] * 4, start, wait,
                 aliases={g: g for g in range(n)}, relay=relay)


def _start_wait(copies):
    def start(ins, outs, sems):
        local, sends, _ = copies(ins, outs, sems)
        for cp in local + sends:
            cp.start()

    def wait(ins, outs, sems):
        local, sends, recvs = copies(ins, outs, sems)
        for cp in recvs:
            cp.wait_recv()
        for cp in sends:
            cp.wait_send()
        for cp in local:
            cp.wait()

    return start, wait


def _c_half_swap(grads):
    n = len(grads)

    def copies(ins, outs, sems):
        send_sems, recv_sems = sems
        x, y, c = lax.axis_index("x"), lax.axis_index("y"), lax.axis_index("c")
        sends = []
        for g in range(n):
            half = ins[g].shape[1] // 2
            sends.append(pltpu.make_async_remote_copy(
                src_ref=ins[g].at[:, pl.ds((1 - c) * half, half), :], dst_ref=outs[g],
                send_sem=send_sems.at[g], recv_sem=recv_sems.at[g], device_id=(x, y, 1 - c), device_id_type=MESH))
        return [], sends, sends

    return _Comm(
        grads, [jax.ShapeDtypeStruct((N_CHIPS, s.shape[1] // 2, s.shape[2]), s.dtype) for s in grads],
        [pltpu.SemaphoreType.DMA((n,)), pltpu.SemaphoreType.DMA((n,))], *_start_wait(copies))


def _c_chip_exchange(parts):
    n = len(parts)

    def copies(ins, outs, sems):
        send_sems, recv_sems = sems
        x, y, c = lax.axis_index("x"), lax.axis_index("y"), lax.axis_index("c")
        peers = _peer_chips(x, y)

        def remote(g, j):
            return pltpu.make_async_remote_copy(
                src_ref=ins[g].at[2 * peers[j][0] + peers[j][1]], dst_ref=outs[g].at[j],
                send_sem=send_sems.at[g, j], recv_sem=recv_sems.at[g, j], device_id=(*peers[j], c),
                device_id_type=MESH)

        sends = [remote(g, j) for g in range(n) for j in range(3)]
        return [], sends, sends

    return _Comm(
        parts, [jax.ShapeDtypeStruct((3,) + s.shape[1:], s.dtype) for s in parts],
        [pltpu.SemaphoreType.DMA((n, 3)), pltpu.SemaphoreType.DMA((n, 3))], *_start_wait(copies))


def _c_join(halves):
    n = len(halves)

    def copies(ins, outs, sems):
        send_sems, recv_sems = sems
        x, y, c = lax.axis_index("x"), lax.axis_index("y"), lax.axis_index("c")
        sends = [pltpu.make_async_remote_copy(
            src_ref=ins[g], dst_ref=outs[g], send_sem=send_sems.at[g], recv_sem=recv_sems.at[g],
            device_id=(x, y, 1 - c), device_id_type=MESH) for g in range(n)]
        return [], sends, sends

    return _Comm(
        halves, [jax.ShapeDtypeStruct(s.shape, s.dtype) for s in halves],
        [pltpu.SemaphoreType.DMA((n,)), pltpu.SemaphoreType.DMA((n,))], *_start_wait(copies))


def _exchange_only(comm=None):
    def body(o_ref):
        o_ref[...] = jnp.zeros_like(o_ref)

    return _pcall(body, [], name="exchange_only", out_shape=jax.ShapeDtypeStruct((8, 128), F32), comm=comm)


def _all_reduce_small(v, comm=None):
    rows = v.shape[0]

    def body(v_ref, out_ref, buf, send_sems, recv_sems):
        x, y, c = lax.axis_index("x"), lax.axis_index("y"), lax.axis_index("c")
        me = 4 * x + 2 * y + c
        buf[me] = v_ref[...]
        flips = [(fx, fy, fc) for fx in (0, 1) for fy in (0, 1) for fc in (0, 1)][1:]

        def peer(k):
            fx, fy, fc = flips[k]
            px, py, pc = x ^ fx, y ^ fy, c ^ fc
            return (px, py, pc), 4 * px + 2 * py + pc

        def copy(k, slot):
            return pltpu.make_async_remote_copy(
                src_ref=buf.at[slot], dst_ref=buf.at[slot], send_sem=send_sems.at[k],
                recv_sem=recv_sems.at[k], device_id=peer(k)[0], device_id_type=MESH)

        sends = [copy(k, me) for k in range(7)]
        for cp in sends:
            cp.start()
        for k in range(7):
            copy(k, peer(k)[1]).wait_recv()
        for cp in sends:
            cp.wait_send()
        acc = buf[0]
        for d in range(1, 8):
            acc = acc + buf[d]
        out_ref[...] = acc

    return _pcall(body, [v], name="all_reduce_small", out_shape=jax.ShapeDtypeStruct((rows, 128), F32),
                  scratch=[pltpu.VMEM((8, rows, 128), F32), pltpu.SemaphoreType.DMA((7,)),
                           pltpu.SemaphoreType.DMA((7,))], comm=comm)


def _add_halves(g, got):
    _, h, c = got.shape
    th = _row_tile(h, 512)
    nh = h // th
    half = lax.axis_index("c") * nh

    def body(h_ref, a_ref, b_ref, o_ref):
        o_ref[...] = (a_ref[...].astype(F32) + b_ref[...].astype(F32)).astype(o_ref.dtype)

    spec = pl.BlockSpec((1, th, c), lambda j, i, h_ref: (j, i, 0))
    mine = pl.BlockSpec((1, th, c), lambda j, i, h_ref: (j, h_ref[0] + i, 0))
    return _pcall(body, [g, got], name="add_halves", grid=(N_CHIPS, nh), prefetch=[half],
                  out_shape=jax.ShapeDtypeStruct(got.shape, BF), in_specs=[mine, spec], out_specs=spec)


def _sum_chips(parts, recv):
    _, h, c = parts.shape
    th = _row_tile(h, 512)
    me = 2 * lax.axis_index("x") + lax.axis_index("y")

    def body(me_ref, p_ref, r_ref, o_ref):
        acc = p_ref[0].astype(F32)
        for s in range(N_CHIPS - 1):
            acc = acc + r_ref[s].astype(F32)
        o_ref[...] = acc

    return _pcall(body, [parts, recv], name="sum_chips", grid=(h // th,), prefetch=[me],
                  out_shape=jax.ShapeDtypeStruct((h, c), F32),
                  in_specs=[pl.BlockSpec((1, th, c), lambda i, me_ref: (me_ref[0], i, 0)),
                            pl.BlockSpec((N_CHIPS - 1, th, c), lambda i, me_ref: (0, i, 0))],
                  out_specs=pl.BlockSpec((th, c), lambda i, me_ref: (i, 0)))


def _adam_update(w, gv, m, v, d_ref, nm_ref, nv_ref):
    c1 = 1.0 / (1.0 - ADAM_B1 ** ADAM_STEP)
    c2 = 1.0 / (1.0 - ADAM_B2 ** ADAM_STEP)
    nm = ADAM_B1 * m + (1.0 - ADAM_B1) * gv
    nv = ADAM_B2 * v + (1.0 - ADAM_B2) * (gv * gv)
    nm_ref[...] = nm
    nv_ref[...] = nv
    d_ref[...] = -ADAM_LR * ((nm * c1) / (jnp.sqrt(nv * c2) + ADAM_EPS) + ADAM_WD * w)


def _adamw(w, g, m, v, comm=None):
    r, c = w.shape[0], w.shape[-1]
    tr = _row_tile(r, 512)

    def body(w_ref, g_ref, m_ref, v_ref, d_ref, nm_ref, nv_ref):
        _adam_update(w_ref[...], g_ref[...], m_ref[...], v_ref[...], d_ref, nm_ref, nv_ref)

    mid = (1,) * (w.ndim - 2)
    spec = pl.BlockSpec((tr,) + mid + (c,), lambda i: (i,) + (0,) * (w.ndim - 1))
    sds = jax.ShapeDtypeStruct(w.shape, F32)
    return _pcall(body, [w, g, m, v], name="adamw", grid=(r // tr,), out_shape=[sds, sds, sds],
                  in_specs=[spec] * 4, out_specs=[spec] * 3, comm=comm)


def _adamw_halves(items, comm=None):
    k = len(items)
    r, c = items[0][0].shape
    h = r // 2
    tr = _row_tile(h, min(512, (VMEM_LIMIT * 3 // 4) // (k * 9 * 2 * 4 * c)))
    nb = h // tr
    core = lax.axis_index("c")

    def body(c_ref, *refs):
        ins, outs = refs[:5 * k], refs[5 * k:]
        for q in range(k):
            w_ref, gm_ref, go_ref, m_ref, v_ref = ins[5 * q:5 * q + 5]
            g_ref, d_ref, nm_ref, nv_ref = outs[4 * q:4 * q + 4]
            gv = jnp.where(pl.program_id(0) == c_ref[0], gm_ref[...], go_ref[...])
            g_ref[...] = gv
            _adam_update(w_ref[...], gv, m_ref[...], v_ref[...], d_ref, nm_ref, nv_ref)

    full = pl.BlockSpec((tr, c), lambda hh, i, c_ref: (hh * nb + i, 0))
    half = pl.BlockSpec((tr, c), lambda hh, i, c_ref: (i, 0))
    sds = jax.ShapeDtypeStruct((r, c), F32)
    return _pcall(body, [a for it in items for a in it], name="adamw_halves", grid=(2, nb), prefetch=[core],
                  out_shape=[sds] * (4 * k), in_specs=[full, half, half, full, full] * k, out_specs=[full] * (4 * k),
                  comm=comm)


SC_CORES, SC_TILES, SC_LANES = 2, 16, 16
SC_BLOCK_ROWS, SC_BLOCK_COLS = 8, 512


def _sc_adamw_halves(items):
    k = len(items)
    r, c = items[0][0].shape
    h = r // 2
    bc = min(c, SC_BLOCK_COLS)
    c1 = 1.0 / (1.0 - ADAM_B1 ** ADAM_STEP)
    c2 = 1.0 / (1.0 - ADAM_B2 ** ADAM_STEP)
    mesh = plsc.VectorSubcoreMesh(core_axis_name="sc_core", subcore_axis_name="sc_tile",
                                  num_cores=SC_CORES, num_subcores=SC_TILES)
    spec = pl.BlockSpec(block_shape=(SC_BLOCK_ROWS, bc), index_map=lambda i, j: (i, j))

    def block(w_v, gin_v, m_v, v_v, g_v, d_v, nm_v, nv_v):
        @pl.loop(0, SC_BLOCK_ROWS)
        def _(row):
            @pl.loop(0, bc, step=SC_LANES)
            def _(col):
                at = (pl.ds(row, 1), pl.ds(col, SC_LANES))
                gv = gin_v.at[*at][...]
                nm = ADAM_B1 * m_v.at[*at][...] + (1.0 - ADAM_B1) * gv
                nv = ADAM_B2 * v_v.at[*at][...] + (1.0 - ADAM_B2) * (gv * gv)
                g_v.at[*at][...] = gv
                nm_v.at[*at][...] = nm
                nv_v.at[*at][...] = nv
                d_v.at[*at][...] = -ADAM_LR * ((nm * c1) / (jnp.sqrt(nv * c2) + ADAM_EPS) + ADAM_WD * w_v.at[*at][...])

    def kern(*refs):
        ins, outs = refs[:5 * k], refs[5 * k:]
        core = lax.axis_index("c")

        def half(q, hh, mine):
            w_hbm, gm_hbm, go_hbm, m_hbm, v_hbm = ins[5 * q:5 * q + 5]
            rows = pl.ds(hh * h, h)
            pltpu.emit_pipeline(
                block, grid=(h // SC_BLOCK_ROWS, c // bc), in_specs=[spec] * 4, out_specs=[spec] * 4,
                core_axis_name=("sc_core", "sc_tile"), dimension_semantics=(pltpu.PARALLEL, pltpu.PARALLEL),
                trace_scopes=False,
            )(w_hbm.at[rows, :], gm_hbm if mine else go_hbm, m_hbm.at[rows, :], v_hbm.at[rows, :],
              *(o.at[rows, :] for o in outs[4 * q:4 * q + 4]))

        for q in range(k):
            for hh in range(2):
                pl.when(core == hh)(lambda q=q, hh=hh: half(q, hh, True))
                pl.when(core != hh)(lambda q=q, hh=hh: half(q, hh, False))

    sds = jax.ShapeDtypeStruct((r, c), F32)
    return pl.kernel(kern, out_type=[sds] * (4 * k), mesh=mesh, scratch_types=[], name="sc_adamw_halves")(
        *(a for it in items for a in it))


def _wgrad(name, a, b, a_spec, b_spec, m, n, nb, comm):
    def body(a_ref, b_ref, o_ref):
        o_ref[...] = _dot_tn(a_ref[...], b_ref[...]).astype(o_ref.dtype)

    return _pcall(body, [a, b], name=name, grid=(nb,), out_shape=jax.ShapeDtypeStruct((nb, m, n), BF),
                  in_specs=[a_spec, b_spec], out_specs=pl.BlockSpec((None, m, n), lambda j: (j, 0, 0)), comm=comm)


def _wgrad_cols(name, a, b, nb, comm=None):
    t_tok, m = a.shape
    n = b.shape[1] // nb
    return _wgrad(name, a, b, pl.BlockSpec((t_tok, m), lambda j: (0, 0)), pl.BlockSpec((t_tok, n), lambda j: (0, j)),
                  m, n, nb, comm)


def _wgrad_rows(name, a, b, nb, comm=None):
    t_tok, n = b.shape
    m = a.shape[1] // nb
    return _wgrad(name, a, b, pl.BlockSpec((t_tok, m), lambda j: (0, j)), pl.BlockSpec((t_tok, n), lambda j: (0, 0)),
                  m, n, nb, comm)


def _wgrad_a_shared(name, a, b4, comm=None):
    t_tok, m = a.shape
    nb, _, n = b4.shape
    return _wgrad(name, a, b4, pl.BlockSpec((t_tok, m), lambda j: (0, 0)),
                  pl.BlockSpec((None, t_tok, n), lambda j: (j, 0, 0)), m, n, nb, comm)


def _wgrad_b_shared(name, a4, b, comm=None):
    nb, t_tok, m = a4.shape
    n = b.shape[1]
    return _wgrad(name, a4, b, pl.BlockSpec((None, t_tok, m), lambda j: (j, 0, 0)),
                  pl.BlockSpec((t_tok, n), lambda j: (0, 0)), m, n, nb, comm)


def _w4_spec(r, c):
    return pl.BlockSpec((None, r, c), lambda i, j: (j, 0, 0))


FFN_ROW_CHUNK = 256


def _row_chunks(tm):
    rc = FFN_ROW_CHUNK if tm % FFN_ROW_CHUNK == 0 else tm
    return [slice(r, r + rc) for r in range(0, tm, rc)]


def _ffn_fwd(h, ln, wg4, wu4, wd4, comm=None):
    t_tok, d = h.shape
    f = wg4.shape[-2]
    tm = _tile(t_tok, 512)

    def body(h_ref, ln_ref, wg_ref, wu_ref, wd_ref, ho_ref, n_ref, g_ref, u_ref, n_s, acc):
        j = pl.program_id(1)

        @pl.when(j == 0)
        def _():
            xv = h_ref[...]
            nv = (xv * _rstd(xv) * ln_ref[...]).astype(BF)
            n_s[...] = nv
            n_ref[...] = nv
            acc[...] = jnp.zeros_like(acc)

        nv = n_s[...]
        g = _dot_nt(nv, wg_ref[...])
        u = _dot_nt(nv, wu_ref[...])
        g_ref[...] = g.astype(BF)
        u_ref[...] = u.astype(BF)
        a = (g * _sigmoid(g) * u).astype(BF)
        acc[...] += _dot(a, wd_ref[...])

        @pl.when(j == N_CHIPS - 1)
        def _():
            ho_ref[...] = h_ref[...] + 0.5 * acc[...]

    row = pl.BlockSpec((tm, d), lambda i, j: (i, 0))
    gu = pl.BlockSpec((None, tm, f), lambda i, j: (j, i, 0))
    gu_sds = jax.ShapeDtypeStruct((N_CHIPS, t_tok, f), BF)
    return _pcall(
        body, [h, ln, wg4, wu4, wd4], name="ffn_fwd", grid=(t_tok // tm, N_CHIPS),
        out_shape=[jax.ShapeDtypeStruct((t_tok, d), F32), jax.ShapeDtypeStruct((t_tok, d), BF), gu_sds, gu_sds],
        in_specs=[row, pl.BlockSpec((1, d), lambda i, j: (0, 0)), _w4_spec(f, d), _w4_spec(f, d), _w4_spec(f, d)],
        out_specs=[row, row, gu, gu],
        scratch=[pltpu.VMEM((tm, d), BF), pltpu.VMEM((tm, d), F32)], comm=comm)


def _ffn_bwd(dho, h, ln, g4, u4, wg4, wu4, wd4, comm=None):
    t_tok, d = h.shape
    f = wg4.shape[-2]
    tm = _tile(t_tok, 512)

    def body(dho_ref, h_ref, ln_ref, g_ref, u_ref, wg_ref, wu_ref, wd_ref,
             dhi_ref, dln_ref, dg_ref, du_ref, a_ref, dhb_ref, dhb_s, dn_acc):
        i, j = pl.program_id(0), pl.program_id(1)

        @pl.when(j == 0)
        def _():
            dhb = (0.5 * dho_ref[...]).astype(BF)
            dhb_s[...] = dhb
            dhb_ref[...] = dhb
            dn_acc[...] = jnp.zeros_like(dn_acc)

        @pl.when((i == 0) & (j == 0))
        def _():
            dln_ref[...] = jnp.zeros_like(dln_ref)

        for rows in _row_chunks(tm):
            g = g_ref[rows, :].astype(F32)
            u = u_ref[rows, :].astype(F32)
            s = _sigmoid(g)
            sg = g * s
            a_ref[rows, :] = (sg * u).astype(BF)
            da = _dot_nt(dhb_s[rows, :], wd_ref[...])
            dg = (da * u * (s * (1.0 + g * (1.0 - s)))).astype(BF)
            du = (da * sg).astype(BF)
            dg_ref[rows, :] = dg
            du_ref[rows, :] = du
            dn_acc[rows, :] += _dot(dg, wg_ref[...]) + _dot(du, wu_ref[...])

        @pl.when(j == N_CHIPS - 1)
        def _():
            xv = h_ref[...]
            dx, dln = _rms_bwd(dn_acc[...], xv, _rstd(xv), ln_ref[...])
            dln_ref[...] += dln
            dhi_ref[...] = dho_ref[...] + dx

    row = pl.BlockSpec((tm, d), lambda i, j: (i, 0))
    vec = pl.BlockSpec((1, d), lambda i, j: (0, 0))
    gu = pl.BlockSpec((None, tm, f), lambda i, j: (j, i, 0))
    gu_sds = jax.ShapeDtypeStruct((N_CHIPS, t_tok, f), BF)
    return _pcall(
        body, [dho, h, ln, g4, u4, wg4, wu4, wd4], name="ffn_bwd", grid=(t_tok // tm, N_CHIPS),
        out_shape=[jax.ShapeDtypeStruct((t_tok, d), F32), jax.ShapeDtypeStruct((1, d), F32),
                   gu_sds, gu_sds, gu_sds, jax.ShapeDtypeStruct((t_tok, d), BF)],
        in_specs=[row, row, vec, gu, gu, _w4_spec(f, d), _w4_spec(f, d), _w4_spec(f, d)],
        out_specs=[row, vec, gu, gu, gu, row],
        scratch=[pltpu.VMEM((tm, d), BF), pltpu.VMEM((tm, d), F32)], comm=comm)


def _rope_tables(pos_col, inv_freq2, comm=None):
    t_tok = pos_col.shape[0]

    def body(p_ref, f_ref, cos_ref, sin_ref):
        ang = p_ref[...] * f_ref[...]
        lane = lax.broadcasted_iota(jnp.int32, ang.shape, 1)
        s = jnp.sin(ang)
        cos_ref[...] = jnp.cos(ang)
        sin_ref[...] = jnp.where((lane & 1) == 0, -s, s)

    sds = jax.ShapeDtypeStruct((t_tok, 128), F32)
    return _pcall(body, [pos_col, inv_freq2], name="rope_tables", out_shape=[sds, sds], comm=comm)


def _swap_pairs(x):
    lane = lax.broadcasted_iota(jnp.int32, x.shape, 1)
    return jnp.where((lane & 1) == 0, pltpu.roll(x, 127, 1), pltpu.roll(x, 1, 1))


def _mix_in(h, ln, w_in, wm4, b_m, cos_t, sin_t, comm=None):
    t_tok, d = h.shape
    cm = wm4.shape[-1]
    tm = _tile(t_tok, 256)

    def body(h_ref, ln_ref, win_ref, wm_ref, bm_ref, cos_ref, sin_ref,
             u_ref, rq_ref, rk_ref, rv_ref, rg_ref, fq_ref, fk_ref, fv_ref, ff_ref, ga_ref, gb_ref):
        xv = h_ref[...]
        ub = (xv * _rstd(xv) * ln_ref[...]).astype(BF)
        u_ref[...] = ub
        cosv, sinv = cos_ref[...], sin_ref[...]

        def sec(k):
            return _dot_nt(ub, win_ref[k * 512:(k + 1) * 512, :])

        def rot(xh):
            return xh * cosv + _swap_pairs(xh) * sinv

        pq, pk = sec(0), sec(1)
        for hh in range(RET_HEADS):
            sl = slice(hh * RET_DIM, (hh + 1) * RET_DIM)
            rq_ref[:, sl] = rot(pq[:, sl]).astype(BF)
            rk_ref[:, sl] = (rot(pk[:, sl]) * RET_SCALE).astype(BF)
        rv_ref[...] = sec(2).astype(BF)
        rg_ref[...] = sec(3).astype(BF)
        fq_ref[...] = (sec(4) * FOX_SCALE).astype(BF)
        fk_ref[...] = sec(5).astype(BF)
        fv_ref[...] = sec(6).astype(BF)
        ff_ref[...] = _dot_nt(ub, win_ref[FF_COL:FF_COL + 128, :])
        for j in range(N_CHIPS):
            gs = _sigmoid(_dot(ub, wm_ref[j]) + bm_ref[:, j * cm:(j + 1) * cm]).astype(BF)
            col = j * cm
            if col < d:
                ga_ref[:, col:col + cm] = gs
            else:
                gb_ref[:, col - d:col - d + cm] = gs

    row = lambda c: pl.BlockSpec((tm, c), lambda i: (i, 0))
    full = lambda *s: pl.BlockSpec(s, lambda i: (0,) * len(s))
    sds = lambda c, dt: jax.ShapeDtypeStruct((t_tok, c), dt)
    return _pcall(
        body, [h, ln, w_in, wm4, b_m, cos_t, sin_t], name="mix_in", grid=(t_tok // tm,),
        out_shape=[sds(d, BF)] + [sds(512, BF)] * 7 + [sds(128, F32), sds(d, BF), sds(d, BF)],
        in_specs=[row(d), full(1, d), full(IN_PAD, d), full(N_CHIPS, d, cm), full(1, 2 * d), row(128), row(128)],
        out_specs=[row(d)] + [row(512)] * 7 + [row(128), row(d), row(d)], comm=comm)


def _split3(x):
    hi = x.astype(BF)
    r1 = x - hi.astype(F32)
    mid = r1.astype(BF)
    lo = (r1 - mid.astype(F32)).astype(BF)
    return hi, mid, lo


def _aug_lane():
    return lax.broadcasted_iota(jnp.int32, (1, 128), 1) & (FOX_DIM - 1)


def _aug_put(base, k0, parts):
    w = _aug_lane()
    for i, part in enumerate(parts):
        base = jnp.where(w == k0 + i, part, base)
    return base


def _forget_fwd(ffl, b_pad):
    t_tok = ffl.shape[0]
    tb = _tile(t_tok, 256)

    def body(ff_ref, b_ref, aq_ref, ak_ref, cum_s):
        r = lax.broadcasted_iota(jnp.int32, (tb, tb), 0)
        c = lax.broadcasted_iota(jnp.int32, (tb, tb), 1)
        tri = jnp.where(c <= r, 1.0, 0.0).astype(BF)
        carry = jnp.zeros((1, 128), F32)
        for i in range(t_tok // tb):
            z = ff_ref[i * tb:(i + 1) * tb, :] + b_ref[...]
            lf = jnp.minimum(z, 0.0) - jnp.log(1.0 + jnp.exp(-jnp.abs(z)))
            hi, mid, lo = _split3(lf)
            cs = _dot(tri, hi) + _dot(tri, mid) + _dot(tri, lo) + carry
            cum_s[i * tb:(i + 1) * tb, :] = cs
            carry = cs[tb - 1:tb, :]
        x = cum_s[...]
        first = lax.broadcasted_iota(jnp.int32, (1, 128), 1) < FOX_DIM
        w = _aug_lane()
        one = jnp.ones((t_tok, 128), BF)
        zero = jnp.zeros((t_tok, 128), BF)
        for pp in range(FOX_HEADS // 2):
            other = jnp.where(first, x[:, 2 * pp + 1:2 * pp + 2], x[:, 2 * pp:2 * pp + 1])
            parts = _split3(other)
            aq = jnp.where((w >= 3) & (w < 6), one, zero)
            ak = jnp.where((w < 3) | ((w >= 6) & (w < 9)), one, zero)
            aq_ref[:, pp * 128:(pp + 1) * 128] = _aug_put(aq, 0, parts)
            ak_ref[:, pp * 128:(pp + 1) * 128] = _aug_put(ak, 3, [-q for q in parts])

    sds = jax.ShapeDtypeStruct((t_tok, FOX_WIDTH), BF)
    return _pcall(body, [ffl, b_pad], name="forget_fwd", out_shape=[sds, sds],
                  scratch=[pltpu.VMEM((t_tok, 128), F32)])


def _forget_bwd(dcum_t, dcum_q, ffl, b_pad):
    t_tok = ffl.shape[0]
    tb = _tile(t_tok, 256)

    def body(dc_ref, dq_ref, ff_ref, b_ref, dff_ref, db_ref, pad_s, d_s):
        pad_s[...] = jnp.zeros_like(pad_s)
        pad_s[0:FOX_HEADS, :] = dc_ref[...]
        dsum = pad_s[...].T
        lane = lax.broadcasted_iota(jnp.int32, (t_tok, 128), 1)
        for hh in range(FOX_HEADS):
            dsum = dsum + jnp.where(lane == hh, dq_ref[:, hh * FOX_DIM:hh * FOX_DIM + 1], 0.0)
        d_s[...] = dsum
        r = lax.broadcasted_iota(jnp.int32, (tb, tb), 0)
        c = lax.broadcasted_iota(jnp.int32, (tb, tb), 1)
        tri = jnp.where(c >= r, 1.0, 0.0).astype(BF)
        carry = jnp.zeros((1, 128), F32)
        db = jnp.zeros((1, 128), F32)
        for i in reversed(range(t_tok // tb)):
            hi, mid, lo = _split3(d_s[i * tb:(i + 1) * tb, :])
            dlf = _dot(tri, hi) + _dot(tri, mid) + _dot(tri, lo) + carry
            carry = dlf[0:1, :]
            z = ff_ref[i * tb:(i + 1) * tb, :] + b_ref[...]
            dff = dlf * _sigmoid(-z)
            dff_ref[i * tb:(i + 1) * tb, :] = dff.astype(BF)
            db = db + jnp.sum(dff, axis=0, keepdims=True)
        db_ref[...] = db

    return _pcall(
        body, [dcum_t, dcum_q, ffl, b_pad], name="forget_bwd",
        out_shape=[jax.ShapeDtypeStruct((t_tok, 128), BF), jax.ShapeDtypeStruct((1, 128), F32)],
        scratch=[pltpu.VMEM((128, t_tok), F32), pltpu.VMEM((t_tok, 128), F32)])


def _first_half():
    return lax.broadcasted_iota(jnp.int32, (1, 128), 1) < FOX_DIM


def _head_rows(x2, a2, hh):
    return jnp.where(_first_half(), x2, a2) if hh == 0 else jnp.where(_first_half(), a2, x2)


def _head_only(x2, hh):
    zero = jnp.zeros_like(x2)
    return jnp.where(_first_half(), x2, zero) if hh == 0 else jnp.where(_first_half(), zero, x2)


def _causal_diag(s):
    rows = lax.broadcasted_iota(jnp.int32, s.shape, 0)
    cols = lax.broadcasted_iota(jnp.int32, s.shape, 1)
    return jnp.where(cols <= rows, s, NEG)


def _diag_or_below(qi, ki, step):
    pl.when(ki < qi)(lambda: step(False))
    pl.when(ki == qi)(lambda: step(True))


def _tri_rows(s, n):
    qi = sum((s >= r * (r + 1) // 2).astype(jnp.int32) for r in range(1, n))
    return qi, s - (qi * (qi + 1)) // 2


def _tri_cols(s, n):
    ki = sum((s >= k * n - k * (k - 1) // 2).astype(jnp.int32) for k in range(1, n))
    return ki, ki + s - (ki * n - (ki * (ki - 1)) // 2)


def _fox_fwd(fq, fk, fv, aq, ak, comm=None):
    t_tok = fq.shape[0]
    t = _tile(t_tok, 512)
    nq = t_tok // t
    npair = FOX_HEADS // 2

    def body(q_ref, k_ref, v_ref, aq_ref, ak_ref, o_ref, of_ref, aqb_ref, m_s, l_s, acc_s):
        qi, ki = _tri_rows(pl.program_id(1), nq)

        @pl.when(ki == 0)
        def _():
            m_s[...] = jnp.full_like(m_s, NEG)
            l_s[...] = jnp.zeros_like(l_s)
            acc_s[...] = jnp.zeros_like(acc_s)

        def step(diag):
            q2, k2, v2, aq2, ak2 = q_ref[...], k_ref[...], v_ref[...], aq_ref[...], ak_ref[...]
            for hh in range(2):
                s = _dot_nt(_head_rows(q2, aq2, hh), _head_rows(k2, ak2, hh))
                if diag:
                    s = _causal_diag(s)
                m_prev = m_s[hh]
                m_new = jnp.maximum(m_prev, jnp.max(s, axis=1, keepdims=True))
                alpha = jnp.exp(m_prev - m_new)
                p = jnp.exp(s - jnp.tile(m_new, (1, t // 128)))
                l_s[hh] = alpha * l_s[hh] + jnp.sum(p, axis=1, keepdims=True)
                acc_s[hh] = alpha * acc_s[hh] + _dot(p.astype(BF), v2)
                m_s[hh] = m_new

        _diag_or_below(qi, ki, step)

        @pl.when(ki == qi)
        def _():
            first = _first_half()
            o = jnp.where(first, acc_s[0] / l_s[0], acc_s[1] / l_s[1])
            o_ref[...] = o.astype(BF)
            of_ref[...] = o
            other = jnp.where(first, m_s[1] + jnp.log(l_s[1]), m_s[0] + jnp.log(l_s[0]))
            aqb_ref[...] = _aug_put(aq_ref[...], 6, _split3(-other))

    qs = pl.BlockSpec((t, 128), lambda p, s: (_tri_rows(s, nq)[0], p))
    ks = pl.BlockSpec((t, 128), lambda p, s: (_tri_rows(s, nq)[1], p))
    stat = pltpu.VMEM((2, t, 128), F32)
    return _pcall(
        body, [fq, fk, fv, aq, ak], name="fox_fwd", grid=(npair, nq * (nq + 1) // 2),
        out_shape=[jax.ShapeDtypeStruct((t_tok, FOX_WIDTH), BF), jax.ShapeDtypeStruct((t_tok, FOX_WIDTH), F32),
                   jax.ShapeDtypeStruct((t_tok, FOX_WIDTH), BF)],
        in_specs=[qs, ks, ks, qs, ks], out_specs=[qs, qs, qs], scratch=[stat, stat, stat], comm=comm)


def _fox_ds(q2, k2, v2, do2, aq2, ak2, ad2, hh, diag):
    s = _dot_nt(_head_rows(q2, aq2, hh), _head_rows(k2, ak2, hh))
    if diag:
        s = _causal_diag(s)
    p = jnp.exp(s)
    av = jnp.where(_aug_lane() < 3, 1.0, 0.0).astype(BF)
    dp = _dot_nt(_head_rows(do2, ad2, hh), _head_rows(v2, jnp.broadcast_to(av, v2.shape), hh))
    return p, p * dp


def _fox_bwd(fq, fk, fv, do, aqb, ak, ad, comm=None):
    t_tok = fq.shape[0]
    t = _tile(t_tok, 512)
    nq = t_tok // t
    npair = FOX_HEADS // 2
    n_steps = nq * (nq + 1) // 2

    def body(q_ref, k_ref, v_ref, do_ref, aq_ref, ak_ref, ad_ref, dq_ref, dk_ref, dv_ref, dck_ref, dcq_ref,
             dk_s, dv_s, dq_s, rs_s):
        step_id = pl.program_id(1)
        ki, qi = _tri_cols(step_id, nq)

        @pl.when(step_id == 0)
        def _():
            dq_s[...] = jnp.zeros_like(dq_s)
            rs_s[...] = jnp.zeros_like(rs_s)

        @pl.when(qi == ki)
        def _():
            dk_s[...] = jnp.zeros_like(dk_s)
            dv_s[...] = jnp.zeros_like(dv_s)
            dck_ref[...] = jnp.zeros_like(dck_ref)

        rows = pl.ds(qi * t if isinstance(qi, int) else pl.multiple_of(qi * t, t), t)

        def step(diag):
            q2, k2, v2, do2 = q_ref[...], k_ref[...], v_ref[...], do_ref[...]
            dq = []
            for hh in range(2):
                p, ds = _fox_ds(q2, k2, v2, do2, aq_ref[...], ak_ref[...], ad_ref[...], hh, diag)
                dsb = ds.astype(BF)
                dv_s[...] += _dot_tn(p.astype(BF), _head_only(do2, hh))
                dk_s[...] += _dot_tn(dsb, _head_only(q2, hh))
                dq.append(_dot(dsb, k2))
                dck_ref[hh] = dck_ref[hh] - jnp.sum(ds, axis=0, keepdims=True)
                rs_s[hh, rows, :] = rs_s[hh, rows, :] + jnp.sum(ds, axis=1, keepdims=True)
            dq_s[rows, :] = dq_s[rows, :] + jnp.where(_first_half(), dq[0], dq[1])

        _diag_or_below(qi, ki, step)

        @pl.when(qi == nq - 1)
        def _():
            dk_ref[...] = dk_s[...].astype(BF)
            dv_ref[...] = dv_s[...].astype(BF)

        @pl.when(step_id == n_steps - 1)
        def _():
            dq_ref[...] = (dq_s[...] * FOX_SCALE).astype(BF)
            dcq_ref[...] = jnp.where(_first_half(), rs_s[0], rs_s[1])

    qs = pl.BlockSpec((t, 128), lambda p, s: (_tri_cols(s, nq)[1], p))
    ks = pl.BlockSpec((t, 128), lambda p, s: (_tri_cols(s, nq)[0], p))
    cks = pl.BlockSpec((2, 1, t), lambda p, s: (p, 0, _tri_cols(s, nq)[0]))
    seq = pl.BlockSpec((t_tok, 128), lambda p, s: (0, p))
    sds = jax.ShapeDtypeStruct((t_tok, FOX_WIDTH), BF)
    return _pcall(
        body, [fq, fk, fv, do, aqb, ak, ad], name="fox_bwd", grid=(npair, n_steps),
        out_shape=[sds, sds, sds, jax.ShapeDtypeStruct((FOX_HEADS, 1, t_tok), F32),
                   jax.ShapeDtypeStruct((t_tok, FOX_WIDTH), F32)],
        in_specs=[qs, ks, ks, qs, qs, ks, qs], out_specs=[seq, ks, ks, cks, seq],
        scratch=[pltpu.VMEM((t, 128), F32), pltpu.VMEM((t, 128), F32), pltpu.VMEM((t_tok, 128), F32),
                 pltpu.VMEM((2, t_tok, 128), F32)], comm=comm)


def _ret_consts():
    c = RET_CHUNK
    log_gamma = jnp.log1p(-jnp.exp2(-5.0 - jnp.arange(RET_HEADS, dtype=F32)))
    idx = jnp.arange(c, dtype=F32)
    diff = idx[:, None] - idx[None, :]
    dmask = jnp.where(diff >= 0, jnp.exp(log_gamma[:, None, None] * jnp.maximum(diff, 0.0)), 0.0)
    qdec = jnp.exp(log_gamma[:, None] * (idx + 1.0))
    kdec = jnp.exp(log_gamma[:, None] * (c - 1 - idx))
    cdec = jnp.exp(log_gamma * c)
    bc = lambda v: jnp.broadcast_to(v[:, :, None], (RET_HEADS, c, RET_DIM))
    return dmask, bc(qdec), bc(kdec), jnp.broadcast_to(cdec[:, None, None], (RET_HEADS, c, RET_DIM))


def _group_norm(y):
    mu = jnp.mean(y, axis=-1, keepdims=True)
    yc = y - mu
    r = lax.rsqrt(jnp.mean(yc * yc, axis=-1, keepdims=True) + EPS)
    return yc * r, r


def _ret_fwd(rq, rk, rv, rg, consts, comm=None):
    t_tok = rq.shape[0]
    nb = 4 if t_tok % (4 * RET_CHUNK) == 0 else 1
    tr = nb * RET_CHUNK
    n_steps = t_tok // tr
    c = RET_CHUNK

    def body(q_ref, k_ref, v_ref, g_ref, dm_ref, qd_ref, kd_ref, cd_ref, y_ref, yo_ref, st_ref, s_s):
        @pl.when(pl.program_id(0) == 0)
        def _():
            s_s[...] = jnp.zeros_like(s_s)

        for b in range(nb):
            rows = slice(b * c, (b + 1) * c)
            for hh in range(RET_HEADS):
                cols = slice(hh * RET_DIM, (hh + 1) * RET_DIM)
                q, k, v = q_ref[rows, cols], k_ref[rows, cols], v_ref[rows, cols]
                state = s_s[hh]
                st_ref[hh, b] = state
                sc = (_dot_nt(q, k) * dm_ref[hh]).astype(BF)
                y = _dot(sc, v) + _dot((q.astype(F32) * qd_ref[hh]).astype(BF), state.astype(BF))
                s_s[hh] = cd_ref[hh] * state + _dot_tn((k.astype(F32) * kd_ref[hh]).astype(BF), v)
                y_ref[rows, cols] = y
                yn, _ = _group_norm(y)
                gate = g_ref[rows, cols].astype(F32)
                yo_ref[rows, cols] = (yn * (gate * _sigmoid(gate))).astype(BF)

    blk = pl.BlockSpec((tr, RET_WIDTH), lambda i: (i, 0))
    cst = pl.BlockSpec((RET_HEADS, c, RET_DIM), lambda i: (0, 0, 0))
    return _pcall(
        body, [rq, rk, rv, rg, *consts], name="ret_fwd", grid=(n_steps,),
        out_shape=[jax.ShapeDtypeStruct((t_tok, RET_WIDTH), F32), jax.ShapeDtypeStruct((t_tok, RET_WIDTH), BF),
                   jax.ShapeDtypeStruct((RET_HEADS, t_tok // c, RET_DIM, RET_DIM), F32)],
        in_specs=[blk] * 4 + [cst] * 4,
        out_specs=[blk, blk, pl.BlockSpec((RET_HEADS, nb, RET_DIM, RET_DIM), lambda i: (0, i, 0, 0))],
        scratch=[pltpu.VMEM((RET_HEADS, RET_DIM, RET_DIM), F32)], comm=comm)


def _ret_bwd(rq, rk, rv, rg, y_raw, dyo, states, consts, cos_t, sin_t, comm=None):
    t_tok = rq.shape[0]
    nb = 4 if t_tok % (4 * RET_CHUNK) == 0 else 1
    tr = nb * RET_CHUNK
    n_steps = t_tok // tr
    c = RET_CHUNK

    def body(q_ref, k_ref, v_ref, g_ref, y_ref, dyo_ref, st_ref, dm_ref, qd_ref, kd_ref, cd_ref,
             cos_ref, sin_ref, dq_ref, dk_ref, dv_ref, dg_ref, ds_s):
        @pl.when(pl.program_id(0) == 0)
        def _():
            ds_s[...] = jnp.zeros_like(ds_s)

        for b in reversed(range(nb)):
            rows = slice(b * c, (b + 1) * c)
            cosv, sinv = cos_ref[rows, :], sin_ref[rows, :]
            for hh in range(RET_HEADS):
                cols = slice(hh * RET_DIM, (hh + 1) * RET_DIM)
                dm, qd, kd, cd = dm_ref[hh], qd_ref[hh], kd_ref[hh], cd_ref[hh]
                q, k, v = q_ref[rows, cols], k_ref[rows, cols], v_ref[rows, cols]
                yn, r = _group_norm(y_ref[rows, cols])
                gate = g_ref[rows, cols].astype(F32)
                sg = _sigmoid(gate)
                dyo = dyo_ref[rows, cols]
                dg_ref[rows, cols] = (dyo * yn * (sg * (1.0 + gate * (1.0 - sg)))).astype(BF)
                dyn = dyo * (gate * sg)
                dy = r * (dyn - jnp.mean(dyn, axis=-1, keepdims=True)
                          - yn * jnp.mean(dyn * yn, axis=-1, keepdims=True))
                dyb = dy.astype(BF)
                state_b = st_ref[hh, b].astype(BF)
                dstate = ds_s[hh]
                dstate_b = dstate.astype(BF)
                qdb = (q.astype(F32) * qd).astype(BF)
                kdb = (k.astype(F32) * kd).astype(BF)
                sc = (_dot_nt(q, k) * dm).astype(BF)
                dv = _dot_tn(sc, dyb) + _dot(kdb, dstate_b)
                dp = (_dot_nt(dyb, v) * dm).astype(BF)
                dq = _dot(dp, k) + _dot_nt(dyb, state_b) * qd
                dk = (_dot_tn(dp, q) + _dot_nt(v, dstate_b) * kd) * RET_SCALE
                ds_s[hh] = cd * dstate + _dot_tn(qdb, dyb)
                dv_ref[rows, cols] = dv.astype(BF)
                dq_ref[rows, cols] = (dq * cosv - _swap_pairs(dq) * sinv).astype(BF)
                dk_ref[rows, cols] = (dk * cosv - _swap_pairs(dk) * sinv).astype(BF)

    rev = lambda i: n_steps - 1 - i
    blk = pl.BlockSpec((tr, RET_WIDTH), lambda i: (rev(i), 0))
    tab = pl.BlockSpec((tr, RET_DIM), lambda i: (rev(i), 0))
    cst = pl.BlockSpec((RET_HEADS, c, RET_DIM), lambda i: (0, 0, 0))
    sds = jax.ShapeDtypeStruct((t_tok, RET_WIDTH), BF)
    return _pcall(
        body, [rq, rk, rv, rg, y_raw, dyo, states, *consts, cos_t, sin_t], name="ret_bwd",
        grid=(n_steps,), out_shape=[sds] * 4,
        in_specs=[blk] * 6 + [pl.BlockSpec((RET_HEADS, nb, RET_DIM, RET_DIM), lambda i: (0, rev(i), 0, 0))]
        + [cst] * 4 + [tab, tab],
        out_specs=[blk] * 4, scratch=[pltpu.VMEM((RET_HEADS, RET_DIM, RET_DIM), F32)], comm=comm)


def _mix_out(h, y_ret, y_fox, ga, gb, wr4, wf4, wo4, comm=None):
    t_tok, d = h.shape
    cz = wr4.shape[-1]
    ro = wo4.shape[-2]
    tm = _tile(t_tok, 512)

    def body(h_ref, yr_ref, yf_ref, ga_ref, gb_ref, wr_ref, wf_ref, wo_ref, ho_ref, za_ref, zb_ref, mix_ref):
        yr, yf = yr_ref[...], yf_ref[...]
        for j in range(N_CHIPS):
            sl = slice(j * cz, (j + 1) * cz)
            za = _dot(yr, wr_ref[j])
            zb = _dot(yf, wf_ref[j])
            za_ref[:, sl] = za.astype(BF)
            zb_ref[:, sl] = zb.astype(BF)
            mix_ref[:, sl] = (ga_ref[:, sl].astype(F32) * za + gb_ref[:, sl].astype(F32) * zb).astype(BF)
        acc = h_ref[...]
        for j in range(N_CHIPS):
            acc = acc + _dot(mix_ref[:, j * ro:(j + 1) * ro], wo_ref[j])
        ho_ref[...] = acc

    row = lambda c: pl.BlockSpec((tm, c), lambda i: (i, 0))
    full = lambda *s: pl.BlockSpec(s, lambda i: (0,) * len(s))
    sds = lambda dt: jax.ShapeDtypeStruct((t_tok, d), dt)
    return _pcall(
        body, [h, y_ret, y_fox, ga, gb, wr4, wf4, wo4], name="mix_out", grid=(t_tok // tm,),
        out_shape=[sds(F32), sds(BF), sds(BF), sds(BF)],
        in_specs=[row(d), row(RET_WIDTH), row(FOX_WIDTH), row(d), row(d),
                  full(N_CHIPS, RET_WIDTH, cz), full(N_CHIPS, FOX_WIDTH, cz), full(N_CHIPS, ro, d)],
        out_specs=[row(d)] * 4, comm=comm)


def _mix_out_bwd(dh, za, zb, ga, gb, y_fox, wr4, wf4, wo4, comm=None):
    t_tok, d = dh.shape
    cz = wr4.shape[-1]
    ro = wo4.shape[-2]
    tm = _tile(t_tok, 256)

    def body(dh_ref, za_ref, zb_ref, ga_ref, gb_ref, yf_ref, wr_ref, wf_ref, wo_ref,
             dhb_ref, dgp_ref, dza_ref, dzb_ref, dyr_ref, dyf_ref, dl_ref, db_ref):
        @pl.when(pl.program_id(0) == 0)
        def _():
            db_ref[...] = jnp.zeros_like(db_ref)

        dhb = dh_ref[...].astype(BF)
        dhb_ref[...] = dhb
        dyr = jnp.zeros((tm, RET_WIDTH), F32)
        dyf = jnp.zeros((tm, FOX_WIDTH), F32)
        for j in range(N_CHIPS):
            sl = slice(j * ro, (j + 1) * ro)
            dmix = _dot_nt(dhb, wo_ref[j])
            ga, gb = ga_ref[:, sl].astype(F32), gb_ref[:, sl].astype(F32)
            dza = (dmix * ga).astype(BF)
            dzb = (dmix * gb).astype(BF)
            dza_ref[:, sl] = dza
            dzb_ref[:, sl] = dzb
            dga = dmix * za_ref[:, sl].astype(F32) * ga * (1.0 - ga)
            dgb = dmix * zb_ref[:, sl].astype(F32) * gb * (1.0 - gb)
            dgp_ref[:, sl] = dga.astype(BF)
            dgp_ref[:, d + j * ro:d + (j + 1) * ro] = dgb.astype(BF)
            db_ref[:, sl] += jnp.sum(dga, axis=0, keepdims=True)
            db_ref[:, d + j * ro:d + (j + 1) * ro] += jnp.sum(dgb, axis=0, keepdims=True)
        for j in range(N_CHIPS):
            sl = slice(j * cz, (j + 1) * cz)
            dyr = dyr + _dot_nt(dza_ref[:, sl], wr_ref[j])
            dyf = dyf + _dot_nt(dzb_ref[:, sl], wf_ref[j])
        dyr_ref[...] = dyr
        dyfb = dyf.astype(BF)
        dyf_ref[...] = dyfb
        prod = dyfb.astype(F32) * yf_ref[...]
        first = _first_half()
        for pp in range(FOX_HEADS // 2):
            blk = prod[:, pp * 128:(pp + 1) * 128]
            s0 = jnp.sum(jnp.where(first, blk, 0.0), axis=1, keepdims=True)
            s1 = jnp.sum(jnp.where(first, 0.0, blk), axis=1, keepdims=True)
            parts = _split3(-jnp.where(first, s1, s0))
            dl_ref[:, pp * 128:(pp + 1) * 128] = _aug_put(jnp.zeros((tm, 128), BF), 0, parts)

    row = lambda c: pl.BlockSpec((tm, c), lambda i: (i, 0))
    full = lambda *s: pl.BlockSpec(s, lambda i: (0,) * len(s))
    sds = lambda c, dt: jax.ShapeDtypeStruct((t_tok, c), dt)
    return _pcall(
        body, [dh, za, zb, ga, gb, y_fox, wr4, wf4, wo4], name="mix_out_bwd", grid=(t_tok // tm,),
        out_shape=[sds(d, BF), sds(2 * d, BF), sds(d, BF), sds(d, BF), sds(RET_WIDTH, F32),
                   sds(FOX_WIDTH, BF), sds(FOX_WIDTH, BF), jax.ShapeDtypeStruct((1, 2 * d), F32)],
        in_specs=[row(d)] * 5 + [row(FOX_WIDTH), full(N_CHIPS, RET_WIDTH, cz), full(N_CHIPS, FOX_WIDTH, cz),
                                 full(N_CHIPS, ro, d)],
        out_specs=[row(d), row(2 * d), row(d), row(d), row(RET_WIDTH), row(FOX_WIDTH), row(FOX_WIDTH),
                   full(1, 2 * d)],
        comm=comm)


def _mix_in_bwd(dh, h, ln, parts, dff, dgpre, w_in, wm4, comm=None):
    t_tok, d = h.shape
    cm = wm4.shape[-1]
    tm = _tile(t_tok, 256)

    def body(dh_ref, h_ref, ln_ref, p0, p1, p2, p3, p4, p5, p6, dff_ref, dgp_ref, win_ref, wm_ref,
             dhi_ref, dln_ref, dproj_ref):
        @pl.when(pl.program_id(0) == 0)
        def _():
            dln_ref[...] = jnp.zeros_like(dln_ref)

        for k, pr in enumerate((p0, p1, p2, p3, p4, p5, p6)):
            dproj_ref[:, k * 512:(k + 1) * 512] = pr[...]
        dproj_ref[:, FF_COL:FF_COL + 128] = dff_ref[...]
        dproj_ref[:, FF_COL + 128:] = jnp.zeros((tm, IN_PAD - FF_COL - 128), BF)
        du = _dot(dproj_ref[...], win_ref[...])
        for j in range(N_CHIPS):
            du = du + _dot_nt(dgp_ref[:, j * cm:(j + 1) * cm], wm_ref[j])
        xv = h_ref[...]
        dx, dln = _rms_bwd(du, xv, _rstd(xv), ln_ref[...])
        dln_ref[...] += dln
        dhi_ref[...] = dh_ref[...] + dx

    row = lambda c: pl.BlockSpec((tm, c), lambda i: (i, 0))
    full = lambda *s: pl.BlockSpec(s, lambda i: (0,) * len(s))
    return _pcall(
        body, [dh, h, ln, *parts, dff, dgpre, w_in, wm4], name="mix_in_bwd", grid=(t_tok // tm,),
        out_shape=[jax.ShapeDtypeStruct((t_tok, d), F32), jax.ShapeDtypeStruct((1, d), F32),
                   jax.ShapeDtypeStruct((t_tok, IN_PAD), BF)],
        in_specs=[row(d), row(d), full(1, d)] + [row(512)] * 7 + [row(128), row(2 * d), full(IN_PAD, d),
                                                                   full(N_CHIPS, d, cm)],
        out_specs=[row(d), full(1, d), row(IN_PAD)], comm=comm)


def _tail(h, p, target, ln_ple, ln_fin, wpg4, wpl4, comm=None):
    t_tok, d = h.shape
    pd = p.shape[1]
    rg = wpg4.shape[-2]
    cp = wpl4.shape[-1]
    tm = _tile(t_tok, 256)

    def body(h_ref, p_ref, t_ref, lp_ref, lf_ref, wg_ref, wp_ref,
             dh_ref, n_ref, dgp_ref, dpe_ref, pb_ref, loss_ref, dlf_ref, dlp_ref, pe_s, dn_s):
        @pl.when(pl.program_id(0) == 0)
        def _():
            loss_ref[...] = jnp.zeros_like(loss_ref)
            dlf_ref[...] = jnp.zeros_like(dlf_ref)
            dlp_ref[...] = jnp.zeros_like(dlp_ref)

        xv = h_ref[...]
        r3 = _rstd(xv)
        nb = (xv * r3 * lp_ref[...]).astype(BF)
        n_ref[...] = nb
        pb = p_ref[...].astype(BF)
        pb_ref[...] = pb
        pgpre = jnp.zeros((tm, d), F32)
        for j in range(N_CHIPS):
            pgpre = pgpre + _dot(nb[:, j * rg:(j + 1) * rg], wg_ref[j])
            pe_s[:, j * cp:(j + 1) * cp] = _dot(pb, wp_ref[j])
        pg = _sigmoid(pgpre)
        pe = pe_s[...]
        h4 = xv + pg * pe
        r4 = _rstd(h4)
        err = h4 * r4 * lf_ref[...] - t_ref[...]
        loss_ref[...] += 0.5 * jnp.sum(jnp.sum(err * err, axis=1, keepdims=True), axis=0, keepdims=True) / d
        dh4, dlf = _rms_bwd(err * (1.0 / d), h4, r4, lf_ref[...])
        dlf_ref[...] += dlf
        dpe_ref[...] = (dh4 * pg).astype(BF)
        dgp = (dh4 * pe * pg * (1.0 - pg)).astype(BF)
        dgp_ref[...] = dgp
        for j in range(N_CHIPS):
            dn_s[:, j * rg:(j + 1) * rg] = _dot_nt(dgp, wg_ref[j])
        dx, dlp = _rms_bwd(dn_s[...], xv, r3, lp_ref[...])
        dlp_ref[...] += dlp
        dh_ref[...] = dh4 + dx

    row = lambda c: pl.BlockSpec((tm, c), lambda i: (i, 0))
    full = lambda *s: pl.BlockSpec(s, lambda i: (0,) * len(s))
    sds = lambda c, dt: jax.ShapeDtypeStruct((t_tok, c), dt)
    vec = jax.ShapeDtypeStruct((1, d), F32)
    return _pcall(
        body, [h, p, target, ln_ple, ln_fin, wpg4, wpl4], name="tail", grid=(t_tok // tm,),
        out_shape=[sds(d, F32), sds(d, BF), sds(d, BF), sds(d, BF), sds(pd, BF),
                   jax.ShapeDtypeStruct((1, 128), F32), vec, vec],
        in_specs=[row(d), row(pd), row(d), full(1, d), full(1, d), full(N_CHIPS, rg, d), full(N_CHIPS, pd, cp)],
        out_specs=[row(d), row(d), row(d), row(d), row(pd), full(1, 128), full(1, d), full(1, d)],
        scratch=[pltpu.VMEM((tm, d), F32), pltpu.VMEM((tm, d), F32)], comm=comm)


BIG = ["w_ffn1_gate", "w_ffn1_up", "w_ffn1_down", "w_in", "w_merge", "w_ret_out", "w_fox_out", "w_out",
       "w_ffn2_gate", "w_ffn2_up", "w_ffn2_down", "w_ple", "w_ple_gate"]
SMALL = ["ln_ffn1", "ln_mix", "b_forget", "b_merge", "ln_ffn2", "ln_ple", "ln_final"]
WEIGHTS = ["ln_ffn1", "w_ffn1_gate", "w_ffn1_up", "w_ffn1_down", "ln_mix", "w_in", "b_forget", "w_merge", "b_merge",
           "w_ret_out", "w_fox_out", "w_out", "ln_ffn2", "w_ffn2_gate", "w_ffn2_up", "w_ffn2_down", "ln_ple",
           "w_ple", "w_ple_gate", "ln_final"]


TRANSPOSED = {"w_ffn1_gate", "w_ffn1_up", "w_ffn2_gate", "w_ffn2_up", "w_in"}
IN_ROWS_PAD = -(-(IN_COLS // N_CHIPS) // 32) * 32


def _pack_small(vals, loss_row):
    rows = [loss_row]
    for name in SMALL:
        v = vals[name].reshape(-1)
        n = -(-v.shape[0] // 128) * 128
        rows.append(jnp.pad(v, (0, n - v.shape[0])).reshape(n // 128, 128))
    packed = jnp.concatenate(rows, axis=0)
    pad = -packed.shape[0] % 8
    return jnp.pad(packed, ((0, pad), (0, 0)))


def _unpack_small(packed, sizes):
    out, r = {}, 1
    for name in SMALL:
        n = sizes[name]
        nr = -(-n // 128)
        out[name] = packed[r:r + nr].reshape(1, nr * 128)[:, :n]
        r += nr
    return out


class _Stage:
    def __init__(self, comm, finish):
        self.comm, self.finish, self.result = comm, finish, None


def _hosted(fn, *a, stages=()):
    if not stages:
        return fn(*a)
    outs, couts = fn(*a, comm=_merge([st.comm for st in stages]))
    for st, o in zip(stages, _split_outs([st.comm for st in stages], couts)):
        st.result = st.finish(o)
    return outs


class _Reducer:
    def __init__(self):
        self.done = {}

    def swap(self, grads):
        names = list(grads)
        return _Stage(_c_half_swap([grads[n] for n in names]),
                      lambda outs: {n: _add_halves(grads[n], o) for n, o in zip(names, outs)})

    def exchange(self, parts):
        names = list(parts)
        return _Stage(_c_chip_exchange([parts[n] for n in names]),
                      lambda outs: {n: _sum_chips(parts[n], o) for n, o in zip(names, outs)})

    def join(self, halves):
        names = list(halves)
        return _Stage(_c_join([halves[n] for n in names]),
                      lambda outs: self.done.update({n: (halves[n], o) for n, o in zip(names, outs)}))


def kernel(x, p, positions, ln_ffn1, w_ffn1_gate, w_ffn1_up, w_ffn1_down, ln_mix, w_in, b_forget, w_merge, b_merge, w_ret_out, w_fox_out, w_out, ln_ffn2, w_ffn2_gate, w_ffn2_up, w_ffn2_down, ln_ple, w_ple, w_ple_gate, ln_final, loss_target, m_ln_ffn1, m_w_ffn1_gate, m_w_ffn1_up, m_w_ffn1_down, m_ln_mix, m_w_in, m_b_forget, m_w_merge, m_b_merge, m_w_ret_out, m_w_fox_out, m_w_out, m_ln_ffn2, m_w_ffn2_gate, m_w_ffn2_up, m_w_ffn2_down, m_ln_ple, m_w_ple, m_w_ple_gate, m_ln_final, v_ln_ffn1, v_w_ffn1_gate, v_w_ffn1_up, v_w_ffn1_down, v_ln_mix, v_w_in, v_b_forget, v_w_merge, v_b_merge, v_w_ret_out, v_w_fox_out, v_w_out, v_ln_ffn2, v_w_ffn2_gate, v_w_ffn2_up, v_w_ffn2_down, v_ln_ple, v_w_ple, v_w_ple_gate, v_ln_final):
    args = dict(locals())
    w = {n: args[n] for n in WEIGHTS}
    m = {n: args["m_" + n] for n in WEIGHTS}
    v = {n: args["v_" + n] for n in WEIGHTS}
    d = x.shape[-1]
    t_tok = x.shape[1]
    xs, ps, target = x[0], p[0, 0], loss_target[0]
    small = {n: w[n].reshape(1, -1) for n in SMALL}

    def to2d(n, a):
        if n in TRANSPOSED:
            return a[0].T
        return a.reshape(a.shape[-2], a.shape[-1]) if a.ndim == 3 else a.reshape(1, -1)

    def from2d(n, a):
        return a.T[None] if n in TRANSPOSED else a.reshape(w[n].shape)

    def padded(n, a):
        return jnp.pad(a, ((0, IN_ROWS_PAD - a.shape[0]), (0, 0))) if n == "w_in" else a

    core = lax.axis_index("c")
    me = 2 * lax.axis_index("x") + lax.axis_index("y")
    shard = {}
    for n in BIG:
        s2 = padded(n, to2d(n, w[n]).astype(BF))
        shard[n] = s2.reshape(1, 2, s2.shape[0] // 2, s2.shape[1])
    full = {}

    def gather(names):
        bufs = [lax.dynamic_update_slice(jnp.zeros((N_CHIPS,) + shard[n].shape[1:], BF), shard[n], (me, 0, 0, 0))
                for n in names]

        def finish(outs):
            full.update({n: o.reshape(N_CHIPS, 2 * o.shape[2], o.shape[3]) for n, o in zip(names, outs)})

        return _Stage(_c_all_gather(bufs), finish)

    half = RET_DIM // 2
    inv_freq = 1.0 / (ROPE_BASE ** (jnp.arange(half, dtype=F32) / half))
    cos_t, sin_t = _hosted(_rope_tables, positions[0].astype(F32).reshape(t_tok, 1),
                           jnp.repeat(inv_freq, 2).reshape(1, RET_DIM),
                           stages=[gather(["w_ffn1_gate", "w_ffn1_up", "w_ffn1_down"])])
    consts = _ret_consts()
    b_pad = jnp.pad(small["b_forget"], ((0, 0), (0, 128 - FOX_HEADS)))

    h1, n1, g1, u1 = _hosted(
        _ffn_fwd, xs, small["ln_ffn1"], full["w_ffn1_gate"], full["w_ffn1_up"], full["w_ffn1_down"],
        stages=[gather(["w_in", "w_merge", "w_ret_out", "w_fox_out", "w_out", "w_ple_gate", "w_ple"])])
    w_in_full = jnp.pad(full["w_in"][:, :IN_COLS // N_CHIPS].reshape(IN_COLS, d), ((0, IN_PAD - IN_COLS), (0, 0)))
    u, rq, rk, rv, rg, fq, fk, fv, ffl, ga, gb = _mix_in(
        h1, small["ln_mix"], w_in_full, full["w_merge"], small["b_merge"], cos_t, sin_t)
    aq, ak = _forget_fwd(ffl, b_pad)
    y_raw, y_ret, states = _ret_fwd(rq, rk, rv, rg, consts)
    y_fox, y_fox32, aqb = _hosted(_fox_fwd, fq, fk, fv, aq, ak,
                                  stages=[gather(["w_ffn2_gate", "w_ffn2_up", "w_ffn2_down"])])
    h2, za, zb, mix = _mix_out(h1, y_ret, y_fox, ga, gb, full["w_ret_out"], full["w_fox_out"], full["w_out"])
    h3, n2, g2, u2 = _ffn_fwd(h2, small["ln_ffn2"], full["w_ffn2_gate"], full["w_ffn2_up"], full["w_ffn2_down"])

    red = _Reducer()
    dh3, n3, dpgpre, dpe, pb, loss, dln_final, dln_ple = _tail(
        h3, ps, target, small["ln_ple"], small["ln_final"], full["w_ple_gate"], full["w_ple"])
    g_f2 = dict(w_ple_gate=_wgrad_rows("wgrad_ple_gate", n3, dpgpre, N_CHIPS),
                w_ple=_wgrad_cols("wgrad_ple", pb, dpe, N_CHIPS))
    dh2, dln_ffn2, dg2, du2, a2, dhb3 = _ffn_bwd(
        dh3, h2, small["ln_ffn2"], g2, u2, full["w_ffn2_gate"], full["w_ffn2_up"], full["w_ffn2_down"])
    g_f2["w_ffn2_gate"] = _wgrad_b_shared("wgrad_ffn2_gate", dg2, n2)
    g_f2["w_ffn2_up"] = _wgrad_b_shared("wgrad_ffn2_up", du2, n2)
    g_f2["w_ffn2_down"] = _wgrad_b_shared("wgrad_ffn2_down", a2, dhb3)

    sw_f2 = red.swap(g_f2)
    dhb2, dgpre, dza, dzb, dy_ret, dy_fox, ad, db_merge = _hosted(
        _mix_out_bwd, dh2, za, zb, ga, gb, y_fox32, full["w_ret_out"], full["w_fox_out"], full["w_out"],
        stages=[sw_f2])
    g_br = dict(w_out=_wgrad_rows("wgrad_out", mix, dhb2, N_CHIPS),
                w_ret_out=_wgrad_cols("wgrad_ret_out", y_ret, dza, N_CHIPS),
                w_fox_out=_wgrad_cols("wgrad_fox_out", y_fox, dzb, N_CHIPS))

    sw_br = red.swap(g_br)
    drq, drk, drv, drg = _hosted(_ret_bwd, rq, rk, rv, rg, y_raw, dy_ret, states, consts, cos_t, sin_t,
                                 stages=[sw_br])
    ex_f2, ex_br = red.exchange(sw_f2.result), red.exchange(sw_br.result)
    dfq, dfk, dfv, dcum_t3, dcum_q = _hosted(_fox_bwd, fq, fk, fv, dy_fox, aqb, ak, ad, stages=[ex_f2, ex_br])
    dff, db_forget = _forget_bwd(dcum_t3.reshape(FOX_HEADS, t_tok), dcum_q, ffl, b_pad)
    dh1, dln_mix, dproj = _hosted(
        _mix_in_bwd, dh2, h1, small["ln_mix"], (drq, drk, drv, drg, dfq, dfk, dfv), dff, dgpre, w_in_full,
        full["w_merge"], stages=[red.join(ex_f2.result), red.join(ex_br.result)])

    results = {}
    for names in (["w_ffn2_gate", "w_ffn2_up", "w_ffn2_down"], ["w_out", "w_ple_gate"], ["w_ret_out", "w_fox_out"],
                  ["w_ple"]):
        res = _sc_adamw_halves([(to2d(n, w[n]), *red.done[n], to2d(n, m[n]), to2d(n, v[n])) for n in names])
        for q, n in enumerate(names):
            results[n] = tuple(from2d(n, a) for a in res[4 * q:4 * q + 4])

    dx, dln_ffn1, dg1, du1, a1, dhb1 = _ffn_bwd(
        dh1, xs, small["ln_ffn1"], g1, u1, full["w_ffn1_gate"], full["w_ffn1_up"], full["w_ffn1_down"])
    g_f1g = _wgrad_b_shared("wgrad_ffn1_gate", dg1, n1)
    sw_f1g = red.swap(dict(w_ffn1_gate=g_f1g))
    g_f1u = _hosted(_wgrad_b_shared, "wgrad_ffn1_up", du1, n1, stages=[sw_f1g])
    ex_f1g, sw_f1u = red.exchange(sw_f1g.result), red.swap(dict(w_ffn1_up=g_f1u))
    g_f1d = _hosted(_wgrad_b_shared, "wgrad_ffn1_down", a1, dhb1, stages=[ex_f1g, sw_f1u])

    ex_f1u, sw_f1d = red.exchange(sw_f1u.result), red.swap(dict(w_ffn1_down=g_f1d))
    g_in = _hosted(_wgrad_rows, "wgrad_in", dproj, u, IN_PAD // 512,
                   stages=[ex_f1u, sw_f1d, red.join(ex_f1g.result)])
    g_in = g_in.reshape(IN_PAD, d)[:IN_COLS].reshape(N_CHIPS, IN_COLS // N_CHIPS, d)
    g_in = jnp.pad(g_in, ((0, 0), (0, IN_ROWS_PAD - IN_COLS // N_CHIPS), (0, 0)))
    ex_f1d, sw_in = red.exchange(sw_f1d.result), red.swap(dict(w_in=g_in))
    g_mrg = _hosted(_wgrad_cols, "wgrad_merge", u, dgpre, N_CHIPS,
                    stages=[ex_f1d, sw_in, red.join(ex_f1u.result)])

    small_grads = dict(ln_ffn1=dln_ffn1, ln_mix=dln_mix, b_forget=db_forget[:, :FOX_HEADS], b_merge=db_merge,
                       ln_ffn2=dln_ffn2, ln_ple=dln_ple, ln_final=dln_final)
    sizes = {n: w[n].size for n in SMALL}
    ex_in, sw_mrg = red.exchange(sw_in.result), red.swap(dict(w_merge=g_mrg))
    reduced = _hosted(_all_reduce_small, _pack_small(small_grads, loss),
                      stages=[ex_in, sw_mrg, red.join(ex_f1d.result)])
    gsum = _unpack_small(reduced, sizes)
    loss = reduced[0, 0]
    ex_mrg = red.exchange(sw_mrg.result)
    _hosted(_exchange_only, stages=[ex_mrg, red.join(ex_in.result)])
    _hosted(_exchange_only, stages=[red.join(ex_mrg.result)])

    def update(names, stages=()):
        w2, m2, v2 = ([to2d(n, a[n]) for n in names] for a in (w, m, v))
        n = names[0]
        if n in gsum or n == "w_in":
            if n in gsum:
                g2 = gsum[n]
            else:
                mine, other = red.done[n]
                g2 = jnp.where(core == 0, jnp.concatenate([mine, other]), jnp.concatenate([other, mine]))
                g2 = g2[:w2[0].shape[0]]
            if n == "w_in":
                rows3 = lambda a: jnp.transpose(a, (2, 0, 1))
                g3 = g2.reshape(g2.shape[0], 1, g2.shape[1])
                res = [g3] + _hosted(_adamw, rows3(w[n]), g3, rows3(m[n]), rows3(v[n]), stages=stages)
                results[n] = tuple(jnp.transpose(a, (1, 2, 0)) for a in res)
                return
            res = [g2] + _hosted(_adamw, w2[0], g2, m2[0], v2[0], stages=stages)
        else:
            res = _hosted(_adamw_halves, [(w2[q], *red.done[names[q]], m2[q], v2[q]) for q in range(len(names))],
                          stages=stages)
        for q, name in enumerate(names):
            results[name] = tuple(from2d(name, a) for a in res[4 * q:4 * q + 4])

    update(["w_ffn1_gate", "w_ffn1_up", "w_ffn1_down"])
    for n in WEIGHTS:
        if n not in results:
            update([n])

    outs = [[results[n][k] for n in WEIGHTS] for k in range(4)]
    return (loss, dx[None], *outs[0], *outs[1], *outs[2], *outs[3])
```

```python
import functools
import operator

import jax
import jax.numpy as jnp
from jax import lax
from jax.experimental import pallas as pl
from jax.experimental.pallas import tpu as pltpu
from jax.experimental.pallas import tpu_sc as plsc

F32 = jnp.float32
BF = jnp.bfloat16
MESH = pl.DeviceIdType.MESH

EPS = 1e-6
ROPE_BASE = 10000.0
N_CHIPS = 4
RET_HEADS = 4
RET_DIM = 128
RET_WIDTH = RET_HEADS * RET_DIM
RET_CHUNK = 128
RET_SCALE = RET_DIM ** -0.5
FOX_HEADS = 8
FOX_DIM = 64
FOX_WIDTH = FOX_HEADS * FOX_DIM
FOX_SCALE = FOX_DIM ** -0.5
IN_COLS = 4 * RET_WIDTH + 3 * FOX_WIDTH + FOX_HEADS
IN_PAD = 4096
FF_COL = 4 * RET_WIDTH + 3 * FOX_WIDTH
NEG = -1e30

ADAM_LR = 0.001
ADAM_B1 = 0.9
ADAM_B2 = 0.999
ADAM_EPS = 1e-08
ADAM_WD = 0.01
ADAM_STEP = 10

VMEM_LIMIT = 52 * 1024 * 1024

RELAY_MIN_STEPS = 16

NT = (((1,), (1,)), ((), ()))
TN = (((0,), (0,)), ((), ()))

HBM_SPEC = pl.BlockSpec(memory_space=pltpu.HBM)
VMEM_SPEC = pl.BlockSpec(memory_space=pltpu.VMEM)


def _dot(a, b):
    return jnp.dot(a, b, preferred_element_type=F32)


def _dot_nt(a, b):
    return lax.dot_general(a, b, NT, preferred_element_type=F32)


def _dot_tn(a, b):
    return lax.dot_general(a, b, TN, preferred_element_type=F32)


def _rstd(xv):
    return lax.rsqrt(jnp.mean(xv * xv, axis=-1, keepdims=True) + EPS)


def _rms_bwd(dn, xv, r, ln):
    xh = xv * r
    dxh = dn * ln
    dx = r * (dxh - xh * jnp.mean(dxh * xh, axis=-1, keepdims=True))
    return dx, jnp.sum(dn * xh, axis=0, keepdims=True)


def _sigmoid(x):
    return jax.nn.sigmoid(x)


def _tile(n, pref):
    return pref if n % pref == 0 else n


def _row_tile(n, cap):
    best = [t for t in range(16, min(n, cap) + 1, 16) if n % t == 0]
    return best[-1] if best else n


class _Comm:
    def __init__(self, ins, out_shapes, sems, start, wait, aliases=None, relay=None):
        self.ins, self.out_shapes, self.sems, self.start, self.wait = list(ins), list(out_shapes), list(sems), start, wait
        self.aliases = dict(aliases or {})
        self.relay = relay


def _merge(comms):
    comms = [c for c in comms if c is not None]
    if not comms:
        return None
    bounds, ni, no, ns = [], 0, 0, 0
    for c in comms:
        bounds.append((ni, no, ns))
        ni, no, ns = ni + len(c.ins), no + len(c.out_shapes), ns + len(c.sems)

    def run(which):
        def f(ins, outs, sems, **kw):
            for c, (i, o, s) in zip(comms, bounds):
                fn = getattr(c, which)
                if fn is not None:
                    fn(ins[i:i + len(c.ins)], outs[o:o + len(c.out_shapes)], sems[s:s + len(c.sems)],
                       **(kw if c.relay is not None else {}))
        return f

    aliases = {i + a: o + b for c, (i, o, _) in zip(comms, bounds) for a, b in c.aliases.items()}
    relay = run("relay") if any(c.relay is not None for c in comms) else None
    return _Comm([a for c in comms for a in c.ins], [a for c in comms for a in c.out_shapes],
                 [a for c in comms for a in c.sems], run("start"), run("wait"), aliases, relay)


def _split_outs(comms, outs):
    res, o = [], 0
    for c in comms:
        if c is not None:
            res.append(list(outs[o:o + len(c.out_shapes)]))
            o += len(c.out_shapes)
    return res


def _pcall(body, args, *, name, out_shape, grid=(), in_specs=None, out_specs=None, scratch=(), comm=None,
           prefetch=()):
    many = isinstance(out_shape, (list, tuple))
    outs = list(out_shape) if many else [out_shape]
    n_pre, n_in, n_out, n_scr = len(prefetch), len(args), len(outs), len(scratch)
    if in_specs is None:
        in_specs, out_specs = [VMEM_SPEC] * n_in, [VMEM_SPEC] * n_out
    else:
        in_specs, out_specs = list(in_specs), (list(out_specs) if many else [out_specs])
    params = pltpu.CompilerParams(dimension_semantics=("arbitrary",) * len(grid), vmem_limit_bytes=VMEM_LIMIT)
    scalars = [jnp.reshape(s, (1,)).astype(jnp.int32) for s in prefetch]
    ci, co = (len(comm.ins), len(comm.out_shapes)) if comm is not None else (0, 0)

    def wrapped(*refs):
        pre, refs = refs[:n_pre], refs[n_pre:]
        a, ca = refs[:n_in], refs[n_in:n_in + ci]
        o = refs[n_in + ci:n_in + ci + n_out]
        cout = refs[n_in + ci + n_out:n_in + ci + n_out + co]
        s = refs[n_in + ci + n_out + co:n_in + ci + n_out + co + n_scr]
        csem = refs[n_in + ci + n_out + co + n_scr:]
        if comm is None:
            body(*pre, *a, *o, *s)
        elif grid:
            step = functools.reduce(lambda acc, k: acc * grid[k] + pl.program_id(k), range(len(grid)), 0)
            n_steps = functools.reduce(operator.mul, grid)
            relayed = comm.relay is not None and n_steps >= RELAY_MIN_STEPS
            pl.when(step == 0)(lambda: comm.start(ca, cout, csem))
            if relayed:
                pl.when(step == n_steps - n_steps // 8)(lambda: comm.relay(ca, cout, csem))
            body(*pre, *a, *o, *s)
            pl.when(step == n_steps - 1)(lambda: comm.wait(ca, cout, csem, **({"relayed": True} if relayed else {})))
        else:
            comm.start(ca, cout, csem)
            body(*pre, *a, *o, *s)
            comm.wait(ca, cout, csem)

    c_ins, c_outs, c_sems, aliases = ([], [], [], {}) if comm is None else (
        comm.ins, comm.out_shapes, comm.sems, {n_pre + n_in + i: n_out + o for i, o in comm.aliases.items()})
    all_in, all_out = in_specs + [HBM_SPEC] * ci, out_specs + [HBM_SPEC] * co
    all_scr = list(scratch) + c_sems
    if grid:
        args = [pltpu.with_memory_space_constraint(a, pltpu.HBM) for a in args]
    c_ins = [pltpu.with_memory_space_constraint(a, pltpu.HBM) for a in c_ins]
    if n_pre:
        spec = dict(grid_spec=pltpu.PrefetchScalarGridSpec(
            num_scalar_prefetch=n_pre, grid=grid, in_specs=all_in, out_specs=all_out, scratch_shapes=all_scr))
    else:
        spec = dict(grid=grid, in_specs=all_in, out_specs=all_out, scratch_shapes=all_scr)
    res = pl.pallas_call(wrapped, name=name, out_shape=outs + c_outs, input_output_aliases=aliases,
                         compiler_params=params, **spec)(*scalars, *args, *c_ins)
    mine = list(res[:n_out])
    mine = mine if many else mine[0]
    return mine if comm is None else (mine, list(res[n_out:]))


def _peer_chips(x, y):
    return [(1 - x, y), (x, 1 - y), (1 - x, 1 - y)]


def _c_all_gather(bufs):
    n = len(bufs)

    def copies(ins, outs, sems):
        send_sems, recv_sems, fwd_send, fwd_recv = sems
        x, y, c = lax.axis_index("x"), lax.axis_index("y"), lax.axis_index("c")
        me = 2 * x + y
        peers = _peer_chips(x, y)
        chip = [2 * px + py for px, py in peers]

        def ici(g, j, slot):
            return pltpu.make_async_remote_copy(
                src_ref=outs[g].at[me, c], dst_ref=outs[g].at[slot, c], send_sem=send_sems.at[g, j],
                recv_sem=recv_sems.at[g, j], device_id=(*peers[j], c), device_id_type=MESH)

        def d2d(g, j, half):
            return pltpu.make_async_remote_copy(
                src_ref=outs[g].at[chip[j], half], dst_ref=outs[g].at[chip[j], half], send_sem=fwd_send.at[g, j],
                recv_sem=fwd_recv.at[g, j], device_id=(x, y, 1 - c), device_id_type=MESH)

        pairs = [(g, j) for g in range(n) for j in range(3)]
        sends = [ici(g, j, me) for g, j in pairs]
        recvs = [ici(g, j, chip[j]) for g, j in pairs]
        passes = [d2d(g, j, c) for g, j in pairs]
        passed = [d2d(g, j, 1 - c) for g, j in pairs]
        return sends, recvs, passes, passed

    def start(ins, outs, sems):
        for cp in copies(ins, outs, sems)[0]:
            cp.start()

    def relay(ins, outs, sems):
        _, recvs, passes, _ = copies(ins, outs, sems)
        for rcv, fwd in zip(recvs, passes):
            rcv.wait_recv()
            fwd.start()

    def wait(ins, outs, sems, relayed=False):
        if not relayed:
            relay(ins, outs, sems)
        sends, _, passes, passed = copies(ins, outs, sems)
        for cp in passed:
            cp.wait_recv()
        for cp in sends + passes:
            cp.wait_send()

    pair_sems = pltpu.SemaphoreType.DMA((n, 3))
    return _Comm(bufs, [jax.ShapeDtypeStruct(s.shape, s.dtype) for s in bufs], [pair_sems] * 4, start, wait,
                 aliases={g: g for g in range(n)}, relay=relay)


def _start_wait(copies):
    def start(ins, outs, sems):
        local, sends, _ = copies(ins, outs, sems)
        for cp in local + sends:
            cp.start()

    def wait(ins, outs, sems):
        local, sends, recvs = copies(ins, outs, sems)
        for cp in recvs:
            cp.wait_recv()
        for cp in sends:
            cp.wait_send()
        for cp in local:
            cp.wait()

    return start, wait


def _c_half_swap(grads):
    n = len(grads)

    def copies(ins, outs, sems):
        send_sems, recv_sems = sems
        x, y, c = lax.axis_index("x"), lax.axis_index("y"), lax.axis_index("c")
        sends = []
        for g in range(n):
            half = ins[g].shape[1] // 2
            sends.append(pltpu.make_async_remote_copy(
                src_ref=ins[g].at[:, pl.ds((1 - c) * half, half), :], dst_ref=outs[g],
                send_sem=send_sems.at[g], recv_sem=recv_sems.at[g], device_id=(x, y, 1 - c), device_id_type=MESH))
        return [], sends, sends

    return _Comm(
        grads, [jax.ShapeDtypeStruct((N_CHIPS, s.shape[1] // 2, s.shape[2]), s.dtype) for s in grads],
        [pltpu.SemaphoreType.DMA((n,)), pltpu.SemaphoreType.DMA((n,))], *_start_wait(copies))


def _c_chip_exchange(parts):
    n = len(parts)

    def copies(ins, outs, sems):
        send_sems, recv_sems = sems
        x, y, c = lax.axis_index("x"), lax.axis_index("y"), lax.axis_index("c")
        peers = _peer_chips(x, y)

        def remote(g, j):
            return pltpu.make_async_remote_copy(
                src_ref=ins[g].at[2 * peers[j][0] + peers[j][1]], dst_ref=outs[g].at[j],
                send_sem=send_sems.at[g, j], recv_sem=recv_sems.at[g, j], device_id=(*peers[j], c),
                device_id_type=MESH)

        sends = [remote(g, j) for g in range(n) for j in range(3)]
        return [], sends, sends

    return _Comm(
        parts, [jax.ShapeDtypeStruct((3,) + s.shape[1:], s.dtype) for s in parts],
        [pltpu.SemaphoreType.DMA((n, 3)), pltpu.SemaphoreType.DMA((n, 3))], *_start_wait(copies))


def _c_join(halves):
    n = len(halves)

    def copies(ins, outs, sems):
        send_sems, recv_sems = sems
        x, y, c = lax.axis_index("x"), lax.axis_index("y"), lax.axis_index("c")
        sends = [pltpu.make_async_remote_copy(
            src_ref=ins[g], dst_ref=outs[g], send_sem=send_sems.at[g], recv_sem=recv_sems.at[g],
            device_id=(x, y, 1 - c), device_id_type=MESH) for g in range(n)]
        return [], sends, sends

    return _Comm(
        halves, [jax.ShapeDtypeStruct(s.shape, s.dtype) for s in halves],
        [pltpu.SemaphoreType.DMA((n,)), pltpu.SemaphoreType.DMA((n,))], *_start_wait(copies))


def _exchange_only(comm=None):
    def body(o_ref):
        o_ref[...] = jnp.zeros_like(o_ref)

    return _pcall(body, [], name="exchange_only", out_shape=jax.ShapeDtypeStruct((8, 128), F32), comm=comm)


def _all_reduce_small(v, comm=None):
    rows = v.shape[0]

    def body(v_ref, out_ref, buf, send_sems, recv_sems):
        x, y, c = lax.axis_index("x"), lax.axis_index("y"), lax.axis_index("c")
        me = 4 * x + 2 * y + c
        buf[me] = v_ref[...]
        flips = [(fx, fy, fc) for fx in (0, 1) for fy in (0, 1) for fc in (0, 1)][1:]

        def peer(k):
            fx, fy, fc = flips[k]
            px, py, pc = x ^ fx, y ^ fy, c ^ fc
            return (px, py, pc), 4 * px + 2 * py + pc

        def copy(k, slot):
            return pltpu.make_async_remote_copy(
                src_ref=buf.at[slot], dst_ref=buf.at[slot], send_sem=send_sems.at[k],
                recv_sem=recv_sems.at[k], device_id=peer(k)[0], device_id_type=MESH)

        sends = [copy(k, me) for k in range(7)]
        for cp in sends:
            cp.start()
        for k in range(7):
            copy(k, peer(k)[1]).wait_recv()
        for cp in sends:
            cp.wait_send()
        acc = buf[0]
        for d in range(1, 8):
            acc = acc + buf[d]
        out_ref[...] = acc

    return _pcall(body, [v], name="all_reduce_small", out_shape=jax.ShapeDtypeStruct((rows, 128), F32),
                  scratch=[pltpu.VMEM((8, rows, 128), F32), pltpu.SemaphoreType.DMA((7,)),
                           pltpu.SemaphoreType.DMA((7,))], comm=comm)


def _add_halves(pairs):
    k = len(pairs)

    def body(h_ref, *refs):
        for a_ref, b_ref, o_ref in zip(refs[0:2 * k:2], refs[1:2 * k:2], refs[2 * k:]):
            o_ref[...] = (a_ref[...].astype(F32) + b_ref[...].astype(F32)).astype(o_ref.dtype)

    in_specs, out_specs = [], []
    for _, got in pairs:
        _, h, c = got.shape
        spec = pl.BlockSpec((1, h, c), lambda j, h_ref: (j, 0, 0))
        in_specs += [pl.BlockSpec((1, h, c), lambda j, h_ref: (j, h_ref[0], 0)), spec]
        out_specs.append(spec)
    return _pcall(body, [a for pair in pairs for a in pair], name="add_halves", grid=(N_CHIPS,),
                  prefetch=[lax.axis_index("c")], in_specs=in_specs, out_specs=out_specs,
                  out_shape=[jax.ShapeDtypeStruct(got.shape, BF) for _, got in pairs])


def _sum_chips(pairs):
    k = len(pairs)
    n_steps = 2 if all(parts.shape[1] % 32 == 0 for parts, _ in pairs) else 1
    me = 2 * lax.axis_index("x") + lax.axis_index("y")

    def body(me_ref, *refs):
        for p_ref, r_ref, o_ref in zip(refs[0:2 * k:2], refs[1:2 * k:2], refs[2 * k:]):
            acc = p_ref[0].astype(F32)
            for s in range(N_CHIPS - 1):
                acc = acc + r_ref[s].astype(F32)
            o_ref[...] = acc

    in_specs, out_specs = [], []
    for parts, _ in pairs:
        _, h, c = parts.shape
        th = h // n_steps
        in_specs += [pl.BlockSpec((1, th, c), lambda i, me_ref: (me_ref[0], i, 0)),
                     pl.BlockSpec((N_CHIPS - 1, th, c), lambda i, me_ref: (0, i, 0))]
        out_specs.append(pl.BlockSpec((th, c), lambda i, me_ref: (i, 0)))
    return _pcall(body, [a for pair in pairs for a in pair], name="sum_chips", grid=(n_steps,), prefetch=[me],
                  in_specs=in_specs, out_specs=out_specs,
                  out_shape=[jax.ShapeDtypeStruct(parts.shape[1:], F32) for parts, _ in pairs])


def _adam_update(w, gv, m, v, d_ref, nm_ref, nv_ref):
    c1 = 1.0 / (1.0 - ADAM_B1 ** ADAM_STEP)
    c2 = 1.0 / (1.0 - ADAM_B2 ** ADAM_STEP)
    nm = ADAM_B1 * m + (1.0 - ADAM_B1) * gv
    nv = ADAM_B2 * v + (1.0 - ADAM_B2) * (gv * gv)
    nm_ref[...] = nm
    nv_ref[...] = nv
    d_ref[...] = -ADAM_LR * ((nm * c1) / (jnp.sqrt(nv * c2) + ADAM_EPS) + ADAM_WD * w)


def _adamw(w, g, m, v, comm=None):
    r, c = w.shape[0], w.shape[-1]
    tr = _row_tile(r, 512)

    def body(w_ref, g_ref, m_ref, v_ref, d_ref, nm_ref, nv_ref):
        _adam_update(w_ref[...], g_ref[...], m_ref[...], v_ref[...], d_ref, nm_ref, nv_ref)

    mid = (1,) * (w.ndim - 2)
    spec = pl.BlockSpec((tr,) + mid + (c,), lambda i: (i,) + (0,) * (w.ndim - 1))
    sds = jax.ShapeDtypeStruct(w.shape, F32)
    return _pcall(body, [w, g, m, v], name="adamw", grid=(r // tr,), out_shape=[sds, sds, sds],
                  in_specs=[spec] * 4, out_specs=[spec] * 3, comm=comm)


def _adamw_halves(items, comm=None):
    k = len(items)
    r, c = items[0][0].shape
    h = r // 2
    tr = _row_tile(h, min(512, (VMEM_LIMIT * 3 // 4) // (k * 9 * 2 * 4 * c)))
    nb = h // tr
    core = lax.axis_index("c")

    def body(c_ref, *refs):
        ins, outs = refs[:5 * k], refs[5 * k:]
        for q in range(k):
            w_ref, gm_ref, go_ref, m_ref, v_ref = ins[5 * q:5 * q + 5]
            g_ref, d_ref, nm_ref, nv_ref = outs[4 * q:4 * q + 4]
            gv = jnp.where(pl.program_id(0) == c_ref[0], gm_ref[...], go_ref[...])
            g_ref[...] = gv
            _adam_update(w_ref[...], gv, m_ref[...], v_ref[...], d_ref, nm_ref, nv_ref)

    full = pl.BlockSpec((tr, c), lambda hh, i, c_ref: (hh * nb + i, 0))
    half = pl.BlockSpec((tr, c), lambda hh, i, c_ref: (i, 0))
    sds = jax.ShapeDtypeStruct((r, c), F32)
    return _pcall(body, [a for it in items for a in it], name="adamw_halves", grid=(2, nb), prefetch=[core],
                  out_shape=[sds] * (4 * k), in_specs=[full, half, half, full, full] * k, out_specs=[full] * (4 * k),
                  comm=comm)


SC_CORES, SC_TILES, SC_LANES = 2, 16, 16
SC_BLOCK_ROWS, SC_BLOCK_COLS = 8, 512


def _sc_adamw_halves(items):
    k = len(items)
    r, c = items[0][0].shape
    h = r // 2
    bc = min(c, SC_BLOCK_COLS)
    c1 = 1.0 / (1.0 - ADAM_B1 ** ADAM_STEP)
    c2 = 1.0 / (1.0 - ADAM_B2 ** ADAM_STEP)
    mesh = plsc.VectorSubcoreMesh(core_axis_name="sc_core", subcore_axis_name="sc_tile",
                                  num_cores=SC_CORES, num_subcores=SC_TILES)
    spec = pl.BlockSpec(block_shape=(SC_BLOCK_ROWS, bc), index_map=lambda i, j: (i, j))

    def block(w_v, gin_v, m_v, v_v, g_v, d_v, nm_v, nv_v):
        @pl.loop(0, SC_BLOCK_ROWS)
        def _(row):
            @pl.loop(0, bc, step=SC_LANES)
            def _(col):
                at = (pl.ds(row, 1), pl.ds(col, SC_LANES))
                gv = gin_v.at[*at][...]
                nm = ADAM_B1 * m_v.at[*at][...] + (1.0 - ADAM_B1) * gv
                nv = ADAM_B2 * v_v.at[*at][...] + (1.0 - ADAM_B2) * (gv * gv)
                g_v.at[*at][...] = gv
                nm_v.at[*at][...] = nm
                nv_v.at[*at][...] = nv
                d_v.at[*at][...] = -ADAM_LR * ((nm * c1) / (jnp.sqrt(nv * c2) + ADAM_EPS) + ADAM_WD * w_v.at[*at][...])

    def kern(*refs):
        ins, outs = refs[:5 * k], refs[5 * k:]
        core = lax.axis_index("c")

        def half(q, hh, mine):
            w_hbm, gm_hbm, go_hbm, m_hbm, v_hbm = ins[5 * q:5 * q + 5]
            rows = pl.ds(hh * h, h)
            pltpu.emit_pipeline(
                block, grid=(h // SC_BLOCK_ROWS, c // bc), in_specs=[spec] * 4, out_specs=[spec] * 4,
                core_axis_name=("sc_core", "sc_tile"), dimension_semantics=(pltpu.PARALLEL, pltpu.PARALLEL),
                trace_scopes=False,
            )(w_hbm.at[rows, :], gm_hbm if mine else go_hbm, m_hbm.at[rows, :], v_hbm.at[rows, :],
              *(o.at[rows, :] for o in outs[4 * q:4 * q + 4]))

        for q in range(k):
            for hh in range(2):
                pl.when(core == hh)(lambda q=q, hh=hh: half(q, hh, True))
                pl.when(core != hh)(lambda q=q, hh=hh: half(q, hh, False))

    sds = jax.ShapeDtypeStruct((r, c), F32)
    return pl.kernel(kern, out_type=[sds] * (4 * k), mesh=mesh, scratch_types=[], name="sc_adamw_halves")(
        *(a for it in items for a in it))


def _wgrad(name, a, b, a_spec, b_spec, m, n, nb, comm):
    def body(a_ref, b_ref, o_ref):
        o_ref[...] = _dot_tn(a_ref[...], b_ref[...]).astype(o_ref.dtype)

    return _pcall(body, [a, b], name=name, grid=(nb,), out_shape=jax.ShapeDtypeStruct((nb, m, n), BF),
                  in_specs=[a_spec, b_spec], out_specs=pl.BlockSpec((None, m, n), lambda j: (j, 0, 0)), comm=comm)


def _wgrad_cols(name, a, b, nb, comm=None):
    t_tok, m = a.shape
    n = b.shape[1] // nb
    return _wgrad(name, a, b, pl.BlockSpec((t_tok, m), lambda j: (0, 0)), pl.BlockSpec((t_tok, n), lambda j: (0, j)),
                  m, n, nb, comm)


def _wgrad_rows(name, a, b, nb, comm=None):
    t_tok, n = b.shape
    m = a.shape[1] // nb
    return _wgrad(name, a, b, pl.BlockSpec((t_tok, m), lambda j: (0, j)), pl.BlockSpec((t_tok, n), lambda j: (0, 0)),
                  m, n, nb, comm)


def _wgrad_a_shared(name, a, b4, comm=None):
    t_tok, m = a.shape
    nb, _, n = b4.shape
    return _wgrad(name, a, b4, pl.BlockSpec((t_tok, m), lambda j: (0, 0)),
                  pl.BlockSpec((None, t_tok, n), lambda j: (j, 0, 0)), m, n, nb, comm)


def _wgrad_b_shared(name, a4, b, comm=None):
    nb, t_tok, m = a4.shape
    n = b.shape[1]
    return _wgrad(name, a4, b, pl.BlockSpec((None, t_tok, m), lambda j: (j, 0, 0)),
                  pl.BlockSpec((t_tok, n), lambda j: (0, 0)), m, n, nb, comm)


def _w4_spec(r, c):
    return pl.BlockSpec((None, r, c), lambda i, j: (j, 0, 0))


FFN_ROW_CHUNK = 256


def _row_chunks(tm):
    rc = FFN_ROW_CHUNK if tm % FFN_ROW_CHUNK == 0 else tm
    return [slice(r, r + rc) for r in range(0, tm, rc)]


def _ffn_fwd(h, ln, wg4, wu4, wd4, comm=None):
    t_tok, d = h.shape
    f = wg4.shape[-2]
    tm = _tile(t_tok, 512)

    def body(h_ref, ln_ref, wg_ref, wu_ref, wd_ref, ho_ref, n_ref, g_ref, u_ref, n_s, acc):
        j = pl.program_id(1)

        @pl.when(j == 0)
        def _():
            xv = h_ref[...]
            nv = (xv * _rstd(xv) * ln_ref[...]).astype(BF)
            n_s[...] = nv
            n_ref[...] = nv
            acc[...] = jnp.zeros_like(acc)

        nv = n_s[...]
        g = _dot_nt(nv, wg_ref[...])
        u = _dot_nt(nv, wu_ref[...])
        g_ref[...] = g.astype(BF)
        u_ref[...] = u.astype(BF)
        a = (g * _sigmoid(g) * u).astype(BF)
        acc[...] += _dot(a, wd_ref[...])

        @pl.when(j == N_CHIPS - 1)
        def _():
            ho_ref[...] = h_ref[...] + 0.5 * acc[...]

    row = pl.BlockSpec((tm, d), lambda i, j: (i, 0))
    gu = pl.BlockSpec((None, tm, f), lambda i, j: (j, i, 0))
    gu_sds = jax.ShapeDtypeStruct((N_CHIPS, t_tok, f), BF)
    return _pcall(
        body, [h, ln, wg4, wu4, wd4], name="ffn_fwd", grid=(t_tok // tm, N_CHIPS),
        out_shape=[jax.ShapeDtypeStruct((t_tok, d), F32), jax.ShapeDtypeStruct((t_tok, d), BF), gu_sds, gu_sds],
        in_specs=[row, pl.BlockSpec((1, d), lambda i, j: (0, 0)), _w4_spec(f, d), _w4_spec(f, d), _w4_spec(f, d)],
        out_specs=[row, row, gu, gu],
        scratch=[pltpu.VMEM((tm, d), BF), pltpu.VMEM((tm, d), F32)], comm=comm)


def _ffn_bwd(dho, h, ln, g4, u4, wg4, wu4, wd4, comm=None):
    t_tok, d = h.shape
    f = wg4.shape[-2]
    tm = _tile(t_tok, 512)

    def body(dho_ref, h_ref, ln_ref, g_ref, u_ref, wg_ref, wu_ref, wd_ref,
             dhi_ref, dln_ref, dg_ref, du_ref, a_ref, dhb_ref, dhb_s, dn_acc):
        i, j = pl.program_id(0), pl.program_id(1)

        @pl.when(j == 0)
        def _():
            dhb = (0.5 * dho_ref[...]).astype(BF)
            dhb_s[...] = dhb
            dhb_ref[...] = dhb
            dn_acc[...] = jnp.zeros_like(dn_acc)

        @pl.when((i == 0) & (j == 0))
        def _():
            dln_ref[...] = jnp.zeros_like(dln_ref)

        for rows in _row_chunks(tm):
            g = g_ref[rows, :].astype(F32)
            u = u_ref[rows, :].astype(F32)
            s = _sigmoid(g)
            sg = g * s
            a_ref[rows, :] = (sg * u).astype(BF)
            da = _dot_nt(dhb_s[rows, :], wd_ref[...])
            dg = (da * u * (s * (1.0 + g * (1.0 - s)))).astype(BF)
            du = (da * sg).astype(BF)
            dg_ref[rows, :] = dg
            du_ref[rows, :] = du
            dn_acc[rows, :] += _dot(dg, wg_ref[...]) + _dot(du, wu_ref[...])

        @pl.when(j == N_CHIPS - 1)
        def _():
            xv = h_ref[...]
            dx, dln = _rms_bwd(dn_acc[...], xv, _rstd(xv), ln_ref[...])
            dln_ref[...] += dln
            dhi_ref[...] = dho_ref[...] + dx

    row = pl.BlockSpec((tm, d), lambda i, j: (i, 0))
    vec = pl.BlockSpec((1, d), lambda i, j: (0, 0))
    gu = pl.BlockSpec((None, tm, f), lambda i, j: (j, i, 0))
    gu_sds = jax.ShapeDtypeStruct((N_CHIPS, t_tok, f), BF)
    return _pcall(
        body, [dho, h, ln, g4, u4, wg4, wu4, wd4], name="ffn_bwd", grid=(t_tok // tm, N_CHIPS),
        out_shape=[jax.ShapeDtypeStruct((t_tok, d), F32), jax.ShapeDtypeStruct((1, d), F32),
                   gu_sds, gu_sds, gu_sds, jax.ShapeDtypeStruct((t_tok, d), BF)],
        in_specs=[row, row, vec, gu, gu, _w4_spec(f, d), _w4_spec(f, d), _w4_spec(f, d)],
        out_specs=[row, vec, gu, gu, gu, row],
        scratch=[pltpu.VMEM((tm, d), BF), pltpu.VMEM((tm, d), F32)], comm=comm)


def _rope_tables(pos_col, inv_freq2, comm=None):
    t_tok = pos_col.shape[0]

    def body(p_ref, f_ref, cos_ref, sin_ref):
        ang = p_ref[...] * f_ref[...]
        lane = lax.broadcasted_iota(jnp.int32, ang.shape, 1)
        s = jnp.sin(ang)
        cos_ref[...] = jnp.cos(ang)
        sin_ref[...] = jnp.where((lane & 1) == 0, -s, s)

    sds = jax.ShapeDtypeStruct((t_tok, 128), F32)
    return _pcall(body, [pos_col, inv_freq2], name="rope_tables", out_shape=[sds, sds], comm=comm)


def _swap_pairs(x):
    lane = lax.broadcasted_iota(jnp.int32, x.shape, 1)
    return jnp.where((lane & 1) == 0, pltpu.roll(x, 127, 1), pltpu.roll(x, 1, 1))


def _mix_in(h, ln, w_in, wm4, b_m, cos_t, sin_t, comm=None):
    t_tok, d = h.shape
    cm = wm4.shape[-1]
    tm = _tile(t_tok, 256)

    def body(h_ref, ln_ref, win_ref, wm_ref, bm_ref, cos_ref, sin_ref,
             u_ref, rq_ref, rk_ref, rv_ref, rg_ref, fq_ref, fk_ref, fv_ref, ff_ref, ga_ref, gb_ref):
        xv = h_ref[...]
        ub = (xv * _rstd(xv) * ln_ref[...]).astype(BF)
        u_ref[...] = ub
        cosv, sinv = cos_ref[...], sin_ref[...]

        def sec(k):
            return _dot_nt(ub, win_ref[k * 512:(k + 1) * 512, :])

        def rot(xh):
            return xh * cosv + _swap_pairs(xh) * sinv

        pq, pk = sec(0), sec(1)
        for hh in range(RET_HEADS):
            sl = slice(hh * RET_DIM, (hh + 1) * RET_DIM)
            rq_ref[:, sl] = rot(pq[:, sl]).astype(BF)
            rk_ref[:, sl] = (rot(pk[:, sl]) * RET_SCALE).astype(BF)
        rv_ref[...] = sec(2).astype(BF)
        rg_ref[...] = sec(3).astype(BF)
        fq_ref[...] = (sec(4) * FOX_SCALE).astype(BF)
        fk_ref[...] = sec(5).astype(BF)
        fv_ref[...] = sec(6).astype(BF)
        ff_ref[...] = _dot_nt(ub, win_ref[FF_COL:FF_COL + 128, :])
        for j in range(N_CHIPS):
            gs = _sigmoid(_dot(ub, wm_ref[j]) + bm_ref[:, j * cm:(j + 1) * cm]).astype(BF)
            col = j * cm
            if col < d:
                ga_ref[:, col:col + cm] = gs
            else:
                gb_ref[:, col - d:col - d + cm] = gs

    row = lambda c: pl.BlockSpec((tm, c), lambda i: (i, 0))
    full = lambda *s: pl.BlockSpec(s, lambda i: (0,) * len(s))
    sds = lambda c, dt: jax.ShapeDtypeStruct((t_tok, c), dt)
    return _pcall(
        body, [h, ln, w_in, wm4, b_m, cos_t, sin_t], name="mix_in", grid=(t_tok // tm,),
        out_shape=[sds(d, BF)] + [sds(512, BF)] * 7 + [sds(128, F32), sds(d, BF), sds(d, BF)],
        in_specs=[row(d), full(1, d), full(IN_PAD, d), full(N_CHIPS, d, cm), full(1, 2 * d), row(128), row(128)],
        out_specs=[row(d)] + [row(512)] * 7 + [row(128), row(d), row(d)], comm=comm)


def _split3(x):
    hi = x.astype(BF)
    r1 = x - hi.astype(F32)
    mid = r1.astype(BF)
    lo = (r1 - mid.astype(F32)).astype(BF)
    return hi, mid, lo


def _aug_lane():
    return lax.broadcasted_iota(jnp.int32, (1, 128), 1) & (FOX_DIM - 1)


def _aug_put(base, k0, parts):
    w = _aug_lane()
    for i, part in enumerate(parts):
        base = jnp.where(w == k0 + i, part, base)
    return base


def _forget_fwd(ffl, b_pad):
    t_tok = ffl.shape[0]
    tb = _tile(t_tok, 256)

    def body(ff_ref, b_ref, aq_ref, ak_ref, cum_s):
        r = lax.broadcasted_iota(jnp.int32, (tb, tb), 0)
        c = lax.broadcasted_iota(jnp.int32, (tb, tb), 1)
        tri = jnp.where(c <= r, 1.0, 0.0).astype(BF)
        carry = jnp.zeros((1, 128), F32)
        for i in range(t_tok // tb):
            z = ff_ref[i * tb:(i + 1) * tb, :] + b_ref[...]
            lf = jnp.minimum(z, 0.0) - jnp.log(1.0 + jnp.exp(-jnp.abs(z)))
            hi, mid, lo = _split3(lf)
            cs = _dot(tri, hi) + _dot(tri, mid) + _dot(tri, lo) + carry
            cum_s[i * tb:(i + 1) * tb, :] = cs
            carry = cs[tb - 1:tb, :]
        x = cum_s[...]
        first = lax.broadcasted_iota(jnp.int32, (1, 128), 1) < FOX_DIM
        w = _aug_lane()
        one = jnp.ones((t_tok, 128), BF)
        zero = jnp.zeros((t_tok, 128), BF)
        for pp in range(FOX_HEADS // 2):
            other = jnp.where(first, x[:, 2 * pp + 1:2 * pp + 2], x[:, 2 * pp:2 * pp + 1])
            parts = _split3(other)
            aq = jnp.where((w >= 3) & (w < 6), one, zero)
            ak = jnp.where((w < 3) | ((w >= 6) & (w < 9)), one, zero)
            aq_ref[:, pp * 128:(pp + 1) * 128] = _aug_put(aq, 0, parts)
            ak_ref[:, pp * 128:(pp + 1) * 128] = _aug_put(ak, 3, [-q for q in parts])

    sds = jax.ShapeDtypeStruct((t_tok, FOX_WIDTH), BF)
    return _pcall(body, [ffl, b_pad], name="forget_fwd", out_shape=[sds, sds],
                  scratch=[pltpu.VMEM((t_tok, 128), F32)])


def _forget_bwd(dcum_t, dcum_q, ffl, b_pad):
    t_tok = ffl.shape[0]
    tb = _tile(t_tok, 256)

    def body(dc_ref, dq_ref, ff_ref, b_ref, dff_ref, db_ref, pad_s, d_s):
        pad_s[...] = jnp.zeros_like(pad_s)
        pad_s[0:FOX_HEADS, :] = dc_ref[...]
        dsum = pad_s[...].T
        lane = lax.broadcasted_iota(jnp.int32, (t_tok, 128), 1)
        for hh in range(FOX_HEADS):
            dsum = dsum + jnp.where(lane == hh, dq_ref[:, hh * FOX_DIM:hh * FOX_DIM + 1], 0.0)
        d_s[...] = dsum
        r = lax.broadcasted_iota(jnp.int32, (tb, tb), 0)
        c = lax.broadcasted_iota(jnp.int32, (tb, tb), 1)
        tri = jnp.where(c >= r, 1.0, 0.0).astype(BF)
        carry = jnp.zeros((1, 128), F32)
        db = jnp.zeros((1, 128), F32)
        for i in reversed(range(t_tok // tb)):
            hi, mid, lo = _split3(d_s[i * tb:(i + 1) * tb, :])
            dlf = _dot(tri, hi) + _dot(tri, mid) + _dot(tri, lo) + carry
            carry = dlf[0:1, :]
            z = ff_ref[i * tb:(i + 1) * tb, :] + b_ref[...]
            dff = dlf * _sigmoid(-z)
            dff_ref[i * tb:(i + 1) * tb, :] = dff.astype(BF)
            db = db + jnp.sum(dff, axis=0, keepdims=True)
        db_ref[...] = db

    return _pcall(
        body, [dcum_t, dcum_q, ffl, b_pad], name="forget_bwd",
        out_shape=[jax.ShapeDtypeStruct((t_tok, 128), BF), jax.ShapeDtypeStruct((1, 128), F32)],
        scratch=[pltpu.VMEM((128, t_tok), F32), pltpu.VMEM((t_tok, 128), F32)])


def _first_half():
    return lax.broadcasted_iota(jnp.int32, (1, 128), 1) < FOX_DIM


def _head_rows(x2, a2, hh):
    return jnp.where(_first_half(), x2, a2) if hh == 0 else jnp.where(_first_half(), a2, x2)


def _head_only(x2, hh):
    zero = jnp.zeros_like(x2)
    return jnp.where(_first_half(), x2, zero) if hh == 0 else jnp.where(_first_half(), zero, x2)


def _causal_diag(s):
    rows = lax.broadcasted_iota(jnp.int32, s.shape, 0)
    cols = lax.broadcasted_iota(jnp.int32, s.shape, 1)
    return jnp.where(cols <= rows, s, NEG)


def _diag_or_below(qi, ki, step):
    pl.when(ki < qi)(lambda: step(False))
    pl.when(ki == qi)(lambda: step(True))


def _tri_rows(s, n):
    qi = sum((s >= r * (r + 1) // 2).astype(jnp.int32) for r in range(1, n))
    return qi, s - (qi * (qi + 1)) // 2


def _tri_cols(s, n):
    ki = sum((s >= k * n - k * (k - 1) // 2).astype(jnp.int32) for k in range(1, n))
    return ki, ki + s - (ki * n - (ki * (ki - 1)) // 2)


def _fox_fwd(fq, fk, fv, aq, ak, comm=None):
    t_tok = fq.shape[0]
    t = _tile(t_tok, 512)
    nq = t_tok // t
    npair = FOX_HEADS // 2

    def body(q_ref, k_ref, v_ref, aq_ref, ak_ref, o_ref, of_ref, aqb_ref, m_s, l_s, acc_s):
        qi, ki = _tri_rows(pl.program_id(1), nq)

        @pl.when(ki == 0)
        def _():
            m_s[...] = jnp.full_like(m_s, NEG)
            l_s[...] = jnp.zeros_like(l_s)
            acc_s[...] = jnp.zeros_like(acc_s)

        def step(diag):
            q2, k2, v2, aq2, ak2 = q_ref[...], k_ref[...], v_ref[...], aq_ref[...], ak_ref[...]
            for hh in range(2):
                s = _dot_nt(_head_rows(q2, aq2, hh), _head_rows(k2, ak2, hh))
                if diag:
                    s = _causal_diag(s)
                m_prev = m_s[hh]
                m_new = jnp.maximum(m_prev, jnp.max(s, axis=1, keepdims=True))
                alpha = jnp.exp(m_prev - m_new)
                p = jnp.exp(s - jnp.tile(m_new, (1, t // 128)))
                l_s[hh] = alpha * l_s[hh] + jnp.sum(p, axis=1, keepdims=True)
                acc_s[hh] = alpha * acc_s[hh] + _dot(p.astype(BF), v2)
                m_s[hh] = m_new

        _diag_or_below(qi, ki, step)

        @pl.when(ki == qi)
        def _():
            first = _first_half()
            o = jnp.where(first, acc_s[0] / l_s[0], acc_s[1] / l_s[1])
            o_ref[...] = o.astype(BF)
            of_ref[...] = o
            other = jnp.where(first, m_s[1] + jnp.log(l_s[1]), m_s[0] + jnp.log(l_s[0]))
            aqb_ref[...] = _aug_put(aq_ref[...], 6, _split3(-other))

    qs = pl.BlockSpec((t, 128), lambda p, s: (_tri_rows(s, nq)[0], p))
    ks = pl.BlockSpec((t, 128), lambda p, s: (_tri_rows(s, nq)[1], p))
    stat = pltpu.VMEM((2, t, 128), F32)
    return _pcall(
        body, [fq, fk, fv, aq, ak], name="fox_fwd", grid=(npair, nq * (nq + 1) // 2),
        out_shape=[jax.ShapeDtypeStruct((t_tok, FOX_WIDTH), BF), jax.ShapeDtypeStruct((t_tok, FOX_WIDTH), F32),
                   jax.ShapeDtypeStruct((t_tok, FOX_WIDTH), BF)],
        in_specs=[qs, ks, ks, qs, ks], out_specs=[qs, qs, qs], scratch=[stat, stat, stat], comm=comm)


def _fox_ds(q2, k2, v2, do2, aq2, ak2, ad2, hh, diag):
    s = _dot_nt(_head_rows(q2, aq2, hh), _head_rows(k2, ak2, hh))
    if diag:
        s = _causal_diag(s)
    p = jnp.exp(s)
    av = jnp.where(_aug_lane() < 3, 1.0, 0.0).astype(BF)
    dp = _dot_nt(_head_rows(do2, ad2, hh), _head_rows(v2, jnp.broadcast_to(av, v2.shape), hh))
    return p, p * dp


def _fox_bwd(fq, fk, fv, do, aqb, ak, ad, comm=None):
    t_tok = fq.shape[0]
    t = _tile(t_tok, 512)
    nq = t_tok // t
    npair = FOX_HEADS // 2
    n_steps = nq * (nq + 1) // 2

    def body(q_ref, k_ref, v_ref, do_ref, aq_ref, ak_ref, ad_ref, dq_ref, dk_ref, dv_ref, dck_ref, dcq_ref,
             dk_s, dv_s, dq_s, rs_s):
        step_id = pl.program_id(1)
        ki, qi = _tri_cols(step_id, nq)

        @pl.when(step_id == 0)
        def _():
            dq_s[...] = jnp.zeros_like(dq_s)
            rs_s[...] = jnp.zeros_like(rs_s)

        @pl.when(qi == ki)
        def _():
            dk_s[...] = jnp.zeros_like(dk_s)
            dv_s[...] = jnp.zeros_like(dv_s)
            dck_ref[...] = jnp.zeros_like(dck_ref)

        rows = pl.ds(qi * t if isinstance(qi, int) else pl.multiple_of(qi * t, t), t)

        def step(diag):
            q2, k2, v2, do2 = q_ref[...], k_ref[...], v_ref[...], do_ref[...]
            dq = []
            for hh in range(2):
                p, ds = _fox_ds(q2, k2, v2, do2, aq_ref[...], ak_ref[...], ad_ref[...], hh, diag)
                dsb = ds.astype(BF)
                dv_s[...] += _dot_tn(p.astype(BF), _head_only(do2, hh))
                dk_s[...] += _dot_tn(dsb, _head_only(q2, hh))
                dq.append(_dot(dsb, k2))
                dck_ref[hh] = dck_ref[hh] - jnp.sum(ds, axis=0, keepdims=True)
                rs_s[hh, rows, :] = rs_s[hh, rows, :] + jnp.sum(ds, axis=1, keepdims=True)
            dq_s[rows, :] = dq_s[rows, :] + jnp.where(_first_half(), dq[0], dq[1])

        _diag_or_below(qi, ki, step)

        @pl.when(qi == nq - 1)
        def _():
            dk_ref[...] = dk_s[...].astype(BF)
            dv_ref[...] = dv_s[...].astype(BF)

        @pl.when(step_id == n_steps - 1)
        def _():
            dq_ref[...] = (dq_s[...] * FOX_SCALE).astype(BF)
            dcq_ref[...] = jnp.where(_first_half(), rs_s[0], rs_s[1])

    qs = pl.BlockSpec((t, 128), lambda p, s: (_tri_cols(s, nq)[1], p))
    ks = pl.BlockSpec((t, 128), lambda p, s: (_tri_cols(s, nq)[0], p))
    cks = pl.BlockSpec((2, 1, t), lambda p, s: (p, 0, _tri_cols(s, nq)[0]))
    seq = pl.BlockSpec((t_tok, 128), lambda p, s: (0, p))
    sds = jax.ShapeDtypeStruct((t_tok, FOX_WIDTH), BF)
    return _pcall(
        body, [fq, fk, fv, do, aqb, ak, ad], name="fox_bwd", grid=(npair, n_steps),
        out_shape=[sds, sds, sds, jax.ShapeDtypeStruct((FOX_HEADS, 1, t_tok), F32),
                   jax.ShapeDtypeStruct((t_tok, FOX_WIDTH), F32)],
        in_specs=[qs, ks, ks, qs, qs, ks, qs], out_specs=[seq, ks, ks, cks, seq],
        scratch=[pltpu.VMEM((t, 128), F32), pltpu.VMEM((t, 128), F32), pltpu.VMEM((t_tok, 128), F32),
                 pltpu.VMEM((2, t_tok, 128), F32)], comm=comm)


def _ret_consts():
    c = RET_CHUNK
    log_gamma = jnp.log1p(-jnp.exp2(-5.0 - jnp.arange(RET_HEADS, dtype=F32)))
    idx = jnp.arange(c, dtype=F32)
    diff = idx[:, None] - idx[None, :]
    dmask = jnp.where(diff >= 0, jnp.exp(log_gamma[:, None, None] * jnp.maximum(diff, 0.0)), 0.0)
    qdec = jnp.exp(log_gamma[:, None] * (idx + 1.0))
    kdec = jnp.exp(log_gamma[:, None] * (c - 1 - idx))
    cdec = jnp.exp(log_gamma * c)
    bc = lambda v: jnp.broadcast_to(v[:, :, None], (RET_HEADS, c, RET_DIM))
    return dmask, bc(qdec), bc(kdec), jnp.broadcast_to(cdec[:, None, None], (RET_HEADS, c, RET_DIM))


def _group_norm(y):
    mu = jnp.mean(y, axis=-1, keepdims=True)
    yc = y - mu
    r = lax.rsqrt(jnp.mean(yc * yc, axis=-1, keepdims=True) + EPS)
    return yc * r, r


def _ret_fwd(rq, rk, rv, rg, consts, comm=None):
    t_tok = rq.shape[0]
    nb = 4 if t_tok % (4 * RET_CHUNK) == 0 else 1
    tr = nb * RET_CHUNK
    n_steps = t_tok // tr
    c = RET_CHUNK

    def body(q_ref, k_ref, v_ref, g_ref, dm_ref, qd_ref, kd_ref, cd_ref, y_ref, yo_ref, st_ref, s_s):
        @pl.when(pl.program_id(0) == 0)
        def _():
            s_s[...] = jnp.zeros_like(s_s)

        for b in range(nb):
            rows = slice(b * c, (b + 1) * c)
            for hh in range(RET_HEADS):
                cols = slice(hh * RET_DIM, (hh + 1) * RET_DIM)
                q, k, v = q_ref[rows, cols], k_ref[rows, cols], v_ref[rows, cols]
                state = s_s[hh]
                st_ref[hh, b] = state
                sc = (_dot_nt(q, k) * dm_ref[hh]).astype(BF)
                y = _dot(sc, v) + _dot((q.astype(F32) * qd_ref[hh]).astype(BF), state.astype(BF))
                s_s[hh] = cd_ref[hh] * state + _dot_tn((k.astype(F32) * kd_ref[hh]).astype(BF), v)
                y_ref[rows, cols] = y
                yn, _ = _group_norm(y)
                gate = g_ref[rows, cols].astype(F32)
                yo_ref[rows, cols] = (yn * (gate * _sigmoid(gate))).astype(BF)

    blk = pl.BlockSpec((tr, RET_WIDTH), lambda i: (i, 0))
    cst = pl.BlockSpec((RET_HEADS, c, RET_DIM), lambda i: (0, 0, 0))
    return _pcall(
        body, [rq, rk, rv, rg, *consts], name="ret_fwd", grid=(n_steps,),
        out_shape=[jax.ShapeDtypeStruct((t_tok, RET_WIDTH), F32), jax.ShapeDtypeStruct((t_tok, RET_WIDTH), BF),
                   jax.ShapeDtypeStruct((RET_HEADS, t_tok // c, RET_DIM, RET_DIM), F32)],
        in_specs=[blk] * 4 + [cst] * 4,
        out_specs=[blk, blk, pl.BlockSpec((RET_HEADS, nb, RET_DIM, RET_DIM), lambda i: (0, i, 0, 0))],
        scratch=[pltpu.VMEM((RET_HEADS, RET_DIM, RET_DIM), F32)], comm=comm)


def _ret_bwd(rq, rk, rv, rg, y_raw, dyo, states, consts, cos_t, sin_t, comm=None):
    t_tok = rq.shape[0]
    nb = 4 if t_tok % (4 * RET_CHUNK) == 0 else 1
    tr = nb * RET_CHUNK
    n_steps = t_tok // tr
    c = RET_CHUNK

    def body(q_ref, k_ref, v_ref, g_ref, y_ref, dyo_ref, st_ref, dm_ref, qd_ref, kd_ref, cd_ref,
             cos_ref, sin_ref, dq_ref, dk_ref, dv_ref, dg_ref, ds_s):
        @pl.when(pl.program_id(0) == 0)
        def _():
            ds_s[...] = jnp.zeros_like(ds_s)

        for b in reversed(range(nb)):
            rows = slice(b * c, (b + 1) * c)
            cosv, sinv = cos_ref[rows, :], sin_ref[rows, :]
            for hh in range(RET_HEADS):
                cols = slice(hh * RET_DIM, (hh + 1) * RET_DIM)
                dm, qd, kd, cd = dm_ref[hh], qd_ref[hh], kd_ref[hh], cd_ref[hh]
                q, k, v = q_ref[rows, cols], k_ref[rows, cols], v_ref[rows, cols]
                yn, r = _group_norm(y_ref[rows, cols])
                gate = g_ref[rows, cols].astype(F32)
                sg = _sigmoid(gate)
                dyo = dyo_ref[rows, cols]
                dg_ref[rows, cols] = (dyo * yn * (sg * (1.0 + gate * (1.0 - sg)))).astype(BF)
                dyn = dyo * (gate * sg)
                dy = r * (dyn - jnp.mean(dyn, axis=-1, keepdims=True)
                          - yn * jnp.mean(dyn * yn, axis=-1, keepdims=True))
                dyb = dy.astype(BF)
                state_b = st_ref[hh, b].astype(BF)
                dstate = ds_s[hh]
                dstate_b = dstate.astype(BF)
                qdb = (q.astype(F32) * qd).astype(BF)
                kdb = (k.astype(F32) * kd).astype(BF)
                sc = (_dot_nt(q, k) * dm).astype(BF)
                dv = _dot_tn(sc, dyb) + _dot(kdb, dstate_b)
                dp = (_dot_nt(dyb, v) * dm).astype(BF)
                dq = _dot(dp, k) + _dot_nt(dyb, state_b) * qd
                dk = (_dot_tn(dp, q) + _dot_nt(v, dstate_b) * kd) * RET_SCALE
                ds_s[hh] = cd * dstate + _dot_tn(qdb, dyb)
                dv_ref[rows, cols] = dv.astype(BF)
                dq_ref[rows, cols] = (dq * cosv - _swap_pairs(dq) * sinv).astype(BF)
                dk_ref[rows, cols] = (dk * cosv - _swap_pairs(dk) * sinv).astype(BF)

    rev = lambda i: n_steps - 1 - i
    blk = pl.BlockSpec((tr, RET_WIDTH), lambda i: (rev(i), 0))
    tab = pl.BlockSpec((tr, RET_DIM), lambda i: (rev(i), 0))
    cst = pl.BlockSpec((RET_HEADS, c, RET_DIM), lambda i: (0, 0, 0))
    sds = jax.ShapeDtypeStruct((t_tok, RET_WIDTH), BF)
    return _pcall(
        body, [rq, rk, rv, rg, y_raw, dyo, states, *consts, cos_t, sin_t], name="ret_bwd",
        grid=(n_steps,), out_shape=[sds] * 4,
        in_specs=[blk] * 6 + [pl.BlockSpec((RET_HEADS, nb, RET_DIM, RET_DIM), lambda i: (0, rev(i), 0, 0))]
        + [cst] * 4 + [tab, tab],
        out_specs=[blk] * 4, scratch=[pltpu.VMEM((RET_HEADS, RET_DIM, RET_DIM), F32)], comm=comm)


def _mix_out(h, y_ret, y_fox, ga, gb, wr4, wf4, wo4, comm=None):
    t_tok, d = h.shape
    cz = wr4.shape[-1]
    ro = wo4.shape[-2]
    tm = _tile(t_tok, 512)

    def body(h_ref, yr_ref, yf_ref, ga_ref, gb_ref, wr_ref, wf_ref, wo_ref, ho_ref, za_ref, zb_ref, mix_ref):
        yr, yf = yr_ref[...], yf_ref[...]
        for j in range(N_CHIPS):
            sl = slice(j * cz, (j + 1) * cz)
            za = _dot(yr, wr_ref[j])
            zb = _dot(yf, wf_ref[j])
            za_ref[:, sl] = za.astype(BF)
            zb_ref[:, sl] = zb.astype(BF)
            mix_ref[:, sl] = (ga_ref[:, sl].astype(F32) * za + gb_ref[:, sl].astype(F32) * zb).astype(BF)
        acc = h_ref[...]
        for j in range(N_CHIPS):
            acc = acc + _dot(mix_ref[:, j * ro:(j + 1) * ro], wo_ref[j])
        ho_ref[...] = acc

    row = lambda c: pl.BlockSpec((tm, c), lambda i: (i, 0))
    full = lambda *s: pl.BlockSpec(s, lambda i: (0,) * len(s))
    sds = lambda dt: jax.ShapeDtypeStruct((t_tok, d), dt)
    return _pcall(
        body, [h, y_ret, y_fox, ga, gb, wr4, wf4, wo4], name="mix_out", grid=(t_tok // tm,),
        out_shape=[sds(F32), sds(BF), sds(BF), sds(BF)],
        in_specs=[row(d), row(RET_WIDTH), row(FOX_WIDTH), row(d), row(d),
                  full(N_CHIPS, RET_WIDTH, cz), full(N_CHIPS, FOX_WIDTH, cz), full(N_CHIPS, ro, d)],
        out_specs=[row(d)] * 4, comm=comm)


def _mix_out_bwd(dh, za, zb, ga, gb, y_fox, wr4, wf4, wo4, comm=None):
    t_tok, d = dh.shape
    cz = wr4.shape[-1]
    ro = wo4.shape[-2]
    tm = _tile(t_tok, 256)

    def body(dh_ref, za_ref, zb_ref, ga_ref, gb_ref, yf_ref, wr_ref, wf_ref, wo_ref,
             dhb_ref, dgp_ref, dza_ref, dzb_ref, dyr_ref, dyf_ref, dl_ref, db_ref):
        @pl.when(pl.program_id(0) == 0)
        def _():
            db_ref[...] = jnp.zeros_like(db_ref)

        dhb = dh_ref[...].astype(BF)
        dhb_ref[...] = dhb
        dyr = jnp.zeros((tm, RET_WIDTH), F32)
        dyf = jnp.zeros((tm, FOX_WIDTH), F32)
        for j in range(N_CHIPS):
            sl = slice(j * ro, (j + 1) * ro)
            dmix = _dot_nt(dhb, wo_ref[j])
            ga, gb = ga_ref[:, sl].astype(F32), gb_ref[:, sl].astype(F32)
            dza = (dmix * ga).astype(BF)
            dzb = (dmix * gb).astype(BF)
            dza_ref[:, sl] = dza
            dzb_ref[:, sl] = dzb
            dga = dmix * za_ref[:, sl].astype(F32) * ga * (1.0 - ga)
            dgb = dmix * zb_ref[:, sl].astype(F32) * gb * (1.0 - gb)
            dgp_ref[:, sl] = dga.astype(BF)
            dgp_ref[:, d + j * ro:d + (j + 1) * ro] = dgb.astype(BF)
            db_ref[:, sl] += jnp.sum(dga, axis=0, keepdims=True)
            db_ref[:, d + j * ro:d + (j + 1) * ro] += jnp.sum(dgb, axis=0, keepdims=True)
        for j in range(N_CHIPS):
            sl = slice(j * cz, (j + 1) * cz)
            dyr = dyr + _dot_nt(dza_ref[:, sl], wr_ref[j])
            dyf = dyf + _dot_nt(dzb_ref[:, sl], wf_ref[j])
        dyr_ref[...] = dyr
        dyfb = dyf.astype(BF)
        dyf_ref[...] = dyfb
        prod = dyfb.astype(F32) * yf_ref[...]
        first = _first_half()
        for pp in range(FOX_HEADS // 2):
            blk = prod[:, pp * 128:(pp + 1) * 128]
            s0 = jnp.sum(jnp.where(first, blk, 0.0), axis=1, keepdims=True)
            s1 = jnp.sum(jnp.where(first, 0.0, blk), axis=1, keepdims=True)
            parts = _split3(-jnp.where(first, s1, s0))
            dl_ref[:, pp * 128:(pp + 1) * 128] = _aug_put(jnp.zeros((tm, 128), BF), 0, parts)

    row = lambda c: pl.BlockSpec((tm, c), lambda i: (i, 0))
    full = lambda *s: pl.BlockSpec(s, lambda i: (0,) * len(s))
    sds = lambda c, dt: jax.ShapeDtypeStruct((t_tok, c), dt)
    return _pcall(
        body, [dh, za, zb, ga, gb, y_fox, wr4, wf4, wo4], name="mix_out_bwd", grid=(t_tok // tm,),
        out_shape=[sds(d, BF), sds(2 * d, BF), sds(d, BF), sds(d, BF), sds(RET_WIDTH, F32),
                   sds(FOX_WIDTH, BF), sds(FOX_WIDTH, BF), jax.ShapeDtypeStruct((1, 2 * d), F32)],
        in_specs=[row(d)] * 5 + [row(FOX_WIDTH), full(N_CHIPS, RET_WIDTH, cz), full(N_CHIPS, FOX_WIDTH, cz),
                                 full(N_CHIPS, ro, d)],
        out_specs=[row(d), row(2 * d), row(d), row(d), row(RET_WIDTH), row(FOX_WIDTH), row(FOX_WIDTH),
                   full(1, 2 * d)],
        comm=comm)


def _mix_in_bwd(dh, h, ln, parts, dff, dgpre, w_in, wm4, comm=None):
    t_tok, d = h.shape
    cm = wm4.shape[-1]
    tm = _tile(t_tok, 256)

    def body(dh_ref, h_ref, ln_ref, p0, p1, p2, p3, p4, p5, p6, dff_ref, dgp_ref, win_ref, wm_ref,
             dhi_ref, dln_ref, dproj_ref):
        @pl.when(pl.program_id(0) == 0)
        def _():
            dln_ref[...] = jnp.zeros_like(dln_ref)

        for k, pr in enumerate((p0, p1, p2, p3, p4, p5, p6)):
            dproj_ref[:, k * 512:(k + 1) * 512] = pr[...]
        dproj_ref[:, FF_COL:FF_COL + 128] = dff_ref[...]
        dproj_ref[:, FF_COL + 128:] = jnp.zeros((tm, IN_PAD - FF_COL - 128), BF)
        du = _dot(dproj_ref[...], win_ref[...])
        for j in range(N_CHIPS):
            du = du + _dot_nt(dgp_ref[:, j * cm:(j + 1) * cm], wm_ref[j])
        xv = h_ref[...]
        dx, dln = _rms_bwd(du, xv, _rstd(xv), ln_ref[...])
        dln_ref[...] += dln
        dhi_ref[...] = dh_ref[...] + dx

    row = lambda c: pl.BlockSpec((tm, c), lambda i: (i, 0))
    full = lambda *s: pl.BlockSpec(s, lambda i: (0,) * len(s))
    return _pcall(
        body, [dh, h, ln, *parts, dff, dgpre, w_in, wm4], name="mix_in_bwd", grid=(t_tok // tm,),
        out_shape=[jax.ShapeDtypeStruct((t_tok, d), F32), jax.ShapeDtypeStruct((1, d), F32),
                   jax.ShapeDtypeStruct((t_tok, IN_PAD), BF)],
        in_specs=[row(d), row(d), full(1, d)] + [row(512)] * 7 + [row(128), row(2 * d), full(IN_PAD, d),
                                                                   full(N_CHIPS, d, cm)],
        out_specs=[row(d), full(1, d), row(IN_PAD)], comm=comm)


def _tail(h, p, target, ln_ple, ln_fin, wpg4, wpl4, comm=None):
    t_tok, d = h.shape
    pd = p.shape[1]
    rg = wpg4.shape[-2]
    cp = wpl4.shape[-1]
    tm = _tile(t_tok, 256)

    def body(h_ref, p_ref, t_ref, lp_ref, lf_ref, wg_ref, wp_ref,
             dh_ref, n_ref, dgp_ref, dpe_ref, pb_ref, loss_ref, dlf_ref, dlp_ref, pe_s, dn_s):
        @pl.when(pl.program_id(0) == 0)
        def _():
            loss_ref[...] = jnp.zeros_like(loss_ref)
            dlf_ref[...] = jnp.zeros_like(dlf_ref)
            dlp_ref[...] = jnp.zeros_like(dlp_ref)

        xv = h_ref[...]
        r3 = _rstd(xv)
        nb = (xv * r3 * lp_ref[...]).astype(BF)
        n_ref[...] = nb
        pb = p_ref[...].astype(BF)
        pb_ref[...] = pb
        pgpre = jnp.zeros((tm, d), F32)
        for j in range(N_CHIPS):
            pgpre = pgpre + _dot(nb[:, j * rg:(j + 1) * rg], wg_ref[j])
            pe_s[:, j * cp:(j + 1) * cp] = _dot(pb, wp_ref[j])
        pg = _sigmoid(pgpre)
        pe = pe_s[...]
        h4 = xv + pg * pe
        r4 = _rstd(h4)
        err = h4 * r4 * lf_ref[...] - t_ref[...]
        loss_ref[...] += 0.5 * jnp.sum(jnp.sum(err * err, axis=1, keepdims=True), axis=0, keepdims=True) / d
        dh4, dlf = _rms_bwd(err * (1.0 / d), h4, r4, lf_ref[...])
        dlf_ref[...] += dlf
        dpe_ref[...] = (dh4 * pg).astype(BF)
        dgp = (dh4 * pe * pg * (1.0 - pg)).astype(BF)
        dgp_ref[...] = dgp
        for j in range(N_CHIPS):
            dn_s[:, j * rg:(j + 1) * rg] = _dot_nt(dgp, wg_ref[j])
        dx, dlp = _rms_bwd(dn_s[...], xv, r3, lp_ref[...])
        dlp_ref[...] += dlp
        dh_ref[...] = dh4 + dx

    row = lambda c: pl.BlockSpec((tm, c), lambda i: (i, 0))
    full = lambda *s: pl.BlockSpec(s, lambda i: (0,) * len(s))
    sds = lambda c, dt: jax.ShapeDtypeStruct((t_tok, c), dt)
    vec = jax.ShapeDtypeStruct((1, d), F32)
    return _pcall(
        body, [h, p, target, ln_ple, ln_fin, wpg4, wpl4], name="tail", grid=(t_tok // tm,),
        out_shape=[sds(d, F32), sds(d, BF), sds(d, BF), sds(d, BF), sds(pd, BF),
                   jax.ShapeDtypeStruct((1, 128), F32), vec, vec],
        in_specs=[row(d), row(pd), row(d), full(1, d), full(1, d), full(N_CHIPS, rg, d), full(N_CHIPS, pd, cp)],
        out_specs=[row(d), row(d), row(d), row(d), row(pd), full(1, 128), full(1, d), full(1, d)],
        scratch=[pltpu.VMEM((tm, d), F32), pltpu.VMEM((tm, d), F32)], comm=comm)


BIG = ["w_ffn1_gate", "w_ffn1_up", "w_ffn1_down", "w_in", "w_merge", "w_ret_out", "w_fox_out", "w_out",
       "w_ffn2_gate", "w_ffn2_up", "w_ffn2_down", "w_ple", "w_ple_gate"]
SMALL = ["ln_ffn1", "ln_mix", "b_forget", "b_merge", "ln_ffn2", "ln_ple", "ln_final"]
WEIGHTS = ["ln_ffn1", "w_ffn1_gate", "w_ffn1_up", "w_ffn1_down", "ln_mix", "w_in", "b_forget", "w_merge", "b_merge",
           "w_ret_out", "w_fox_out", "w_out", "ln_ffn2", "w_ffn2_gate", "w_ffn2_up", "w_ffn2_down", "ln_ple",
           "w_ple", "w_ple_gate", "ln_final"]


TRANSPOSED = {"w_ffn1_gate", "w_ffn1_up", "w_ffn2_gate", "w_ffn2_up", "w_in"}
IN_ROWS_PAD = -(-(IN_COLS // N_CHIPS) // 32) * 32


def _pack_small(vals, loss_row):
    rows = [loss_row]
    for name in SMALL:
        v = vals[name].reshape(-1)
        n = -(-v.shape[0] // 128) * 128
        rows.append(jnp.pad(v, (0, n - v.shape[0])).reshape(n // 128, 128))
    packed = jnp.concatenate(rows, axis=0)
    pad = -packed.shape[0] % 8
    return jnp.pad(packed, ((0, pad), (0, 0)))


def _unpack_small(packed, sizes):
    out, r = {}, 1
    for name in SMALL:
        n = sizes[name]
        nr = -(-n // 128)
        out[name] = packed[r:r + nr].reshape(1, nr * 128)[:, :n]
        r += nr
    return out


class _Stage:
    def __init__(self, comm, finish):
        self.comm, self.finish, self.result = comm, finish, None


def _hosted(fn, *a, stages=()):
    if not stages:
        return fn(*a)
    outs, couts = fn(*a, comm=_merge([st.comm for st in stages]))
    for st, o in zip(stages, _split_outs([st.comm for st in stages], couts)):
        st.result = st.finish(o)
    return outs


class _Reducer:
    def __init__(self):
        self.done = {}

    def swap(self, grads):
        names = list(grads)
        return _Stage(_c_half_swap([grads[n] for n in names]),
                      lambda outs: dict(zip(names, _add_halves([(grads[n], o) for n, o in zip(names, outs)]))))

    def exchange(self, parts):
        names = list(parts)
        return _Stage(_c_chip_exchange([parts[n] for n in names]),
                      lambda outs: dict(zip(names, _sum_chips([(parts[n], o) for n, o in zip(names, outs)]))))

    def join(self, halves):
        names = list(halves)
        return _Stage(_c_join([halves[n] for n in names]),
                      lambda outs: self.done.update({n: (halves[n], o) for n, o in zip(names, outs)}))


def kernel(x, p, positions, ln_ffn1, w_ffn1_gate, w_ffn1_up, w_ffn1_down, ln_mix, w_in, b_forget, w_merge, b_merge, w_ret_out, w_fox_out, w_out, ln_ffn2, w_ffn2_gate, w_ffn2_up, w_ffn2_down, ln_ple, w_ple, w_ple_gate, ln_final, loss_target, m_ln_ffn1, m_w_ffn1_gate, m_w_ffn1_up, m_w_ffn1_down, m_ln_mix, m_w_in, m_b_forget, m_w_merge, m_b_merge, m_w_ret_out, m_w_fox_out, m_w_out, m_ln_ffn2, m_w_ffn2_gate, m_w_ffn2_up, m_w_ffn2_down, m_ln_ple, m_w_ple, m_w_ple_gate, m_ln_final, v_ln_ffn1, v_w_ffn1_gate, v_w_ffn1_up, v_w_ffn1_down, v_ln_mix, v_w_in, v_b_forget, v_w_merge, v_b_merge, v_w_ret_out, v_w_fox_out, v_w_out, v_ln_ffn2, v_w_ffn2_gate, v_w_ffn2_up, v_w_ffn2_down, v_ln_ple, v_w_ple, v_w_ple_gate, v_ln_final):
    args = dict(locals())
    w = {n: args[n] for n in WEIGHTS}
    m = {n: args["m_" + n] for n in WEIGHTS}
    v = {n: args["v_" + n] for n in WEIGHTS}
    d = x.shape[-1]
    t_tok = x.shape[1]
    xs, ps, target = x[0], p[0, 0], loss_target[0]
    small = {n: w[n].reshape(1, -1) for n in SMALL}

    def to2d(n, a):
        if n in TRANSPOSED:
            return a[0].T
        return a.reshape(a.shape[-2], a.shape[-1]) if a.ndim == 3 else a.reshape(1, -1)

    def from2d(n, a):
        return a.T[None] if n in TRANSPOSED else a.reshape(w[n].shape)

    def padded(n, a):
        return jnp.pad(a, ((0, IN_ROWS_PAD - a.shape[0]), (0, 0))) if n == "w_in" else a

    core = lax.axis_index("c")
    me = 2 * lax.axis_index("x") + lax.axis_index("y")
    shard = {}
    for n in BIG:
        s2 = padded(n, to2d(n, w[n]).astype(BF))
        shard[n] = s2.reshape(1, 2, s2.shape[0] // 2, s2.shape[1])
    full = {}

    def gather(names):
        bufs = [lax.dynamic_update_slice(jnp.zeros((N_CHIPS,) + shard[n].shape[1:], BF), shard[n], (me, 0, 0, 0))
                for n in names]

        def finish(outs):
            full.update({n: o.reshape(N_CHIPS, 2 * o.shape[2], o.shape[3]) for n, o in zip(names, outs)})

        return _Stage(_c_all_gather(bufs), finish)

    half = RET_DIM // 2
    inv_freq = 1.0 / (ROPE_BASE ** (jnp.arange(half, dtype=F32) / half))
    cos_t, sin_t = _hosted(_rope_tables, positions[0].astype(F32).reshape(t_tok, 1),
                           jnp.repeat(inv_freq, 2).reshape(1, RET_DIM),
                           stages=[gather(["w_ffn1_gate", "w_ffn1_up", "w_ffn1_down"])])
    consts = _ret_consts()
    b_pad = jnp.pad(small["b_forget"], ((0, 0), (0, 128 - FOX_HEADS)))

    h1, n1, g1, u1 = _hosted(
        _ffn_fwd, xs, small["ln_ffn1"], full["w_ffn1_gate"], full["w_ffn1_up"], full["w_ffn1_down"],
        stages=[gather(["w_in", "w_merge", "w_ret_out", "w_fox_out", "w_out", "w_ple_gate", "w_ple"])])
    w_in_full = jnp.pad(full["w_in"][:, :IN_COLS // N_CHIPS].reshape(IN_COLS, d), ((0, IN_PAD - IN_COLS), (0, 0)))
    u, rq, rk, rv, rg, fq, fk, fv, ffl, ga, gb = _mix_in(
        h1, small["ln_mix"], w_in_full, full["w_merge"], small["b_merge"], cos_t, sin_t)
    aq, ak = _forget_fwd(ffl, b_pad)
    y_raw, y_ret, states = _ret_fwd(rq, rk, rv, rg, consts)
    y_fox, y_fox32, aqb = _hosted(_fox_fwd, fq, fk, fv, aq, ak,
                                  stages=[gather(["w_ffn2_gate", "w_ffn2_up", "w_ffn2_down"])])
    h2, za, zb, mix = _mix_out(h1, y_ret, y_fox, ga, gb, full["w_ret_out"], full["w_fox_out"], full["w_out"])
    h3, n2, g2, u2 = _ffn_fwd(h2, small["ln_ffn2"], full["w_ffn2_gate"], full["w_ffn2_up"], full["w_ffn2_down"])

    red = _Reducer()
    dh3, n3, dpgpre, dpe, pb, loss, dln_final, dln_ple = _tail(
        h3, ps, target, small["ln_ple"], small["ln_final"], full["w_ple_gate"], full["w_ple"])
    g_f2 = dict(w_ple_gate=_wgrad_rows("wgrad_ple_gate", n3, dpgpre, N_CHIPS),
                w_ple=_wgrad_cols("wgrad_ple", pb, dpe, N_CHIPS))
    dh2, dln_ffn2, dg2, du2, a2, dhb3 = _ffn_bwd(
        dh3, h2, small["ln_ffn2"], g2, u2, full["w_ffn2_gate"], full["w_ffn2_up"], full["w_ffn2_down"])
    g_f2["w_ffn2_gate"] = _wgrad_b_shared("wgrad_ffn2_gate", dg2, n2)
    g_f2["w_ffn2_up"] = _wgrad_b_shared("wgrad_ffn2_up", du2, n2)
    g_f2["w_ffn2_down"] = _wgrad_b_shared("wgrad_ffn2_down", a2, dhb3)

    sw_f2 = red.swap(g_f2)
    dhb2, dgpre, dza, dzb, dy_ret, dy_fox, ad, db_merge = _hosted(
        _mix_out_bwd, dh2, za, zb, ga, gb, y_fox32, full["w_ret_out"], full["w_fox_out"], full["w_out"],
        stages=[sw_f2])
    g_br = dict(w_out=_wgrad_rows("wgrad_out", mix, dhb2, N_CHIPS),
                w_ret_out=_wgrad_cols("wgrad_ret_out", y_ret, dza, N_CHIPS),
                w_fox_out=_wgrad_cols("wgrad_fox_out", y_fox, dzb, N_CHIPS))

    sw_br = red.swap(g_br)
    drq, drk, drv, drg = _hosted(_ret_bwd, rq, rk, rv, rg, y_raw, dy_ret, states, consts, cos_t, sin_t,
                                 stages=[sw_br])
    ex_f2, ex_br = red.exchange(sw_f2.result), red.exchange(sw_br.result)
    dfq, dfk, dfv, dcum_t3, dcum_q = _hosted(_fox_bwd, fq, fk, fv, dy_fox, aqb, ak, ad, stages=[ex_f2, ex_br])
    dff, db_forget = _forget_bwd(dcum_t3.reshape(FOX_HEADS, t_tok), dcum_q, ffl, b_pad)
    dh1, dln_mix, dproj = _hosted(
        _mix_in_bwd, dh2, h1, small["ln_mix"], (drq, drk, drv, drg, dfq, dfk, dfv), dff, dgpre, w_in_full,
        full["w_merge"], stages=[red.join(ex_f2.result), red.join(ex_br.result)])

    results = {}
    for names in (["w_ffn2_gate", "w_ffn2_up", "w_ffn2_down"], ["w_out", "w_ple_gate"], ["w_ret_out", "w_fox_out"],
                  ["w_ple"]):
        res = _sc_adamw_halves([(to2d(n, w[n]), *red.done[n], to2d(n, m[n]), to2d(n, v[n])) for n in names])
        for q, n in enumerate(names):
            results[n] = tuple(from2d(n, a) for a in res[4 * q:4 * q + 4])

    dx, dln_ffn1, dg1, du1, a1, dhb1 = _ffn_bwd(
        dh1, xs, small["ln_ffn1"], g1, u1, full["w_ffn1_gate"], full["w_ffn1_up"], full["w_ffn1_down"])
    g_f1g = _wgrad_b_shared("wgrad_ffn1_gate", dg1, n1)
    sw_f1g = red.swap(dict(w_ffn1_gate=g_f1g))
    g_f1u = _hosted(_wgrad_b_shared, "wgrad_ffn1_up", du1, n1, stages=[sw_f1g])
    ex_f1g, sw_f1u = red.exchange(sw_f1g.result), red.swap(dict(w_ffn1_up=g_f1u))
    g_f1d = _hosted(_wgrad_b_shared, "wgrad_ffn1_down", a1, dhb1, stages=[ex_f1g, sw_f1u])

    ex_f1u, sw_f1d = red.exchange(sw_f1u.result), red.swap(dict(w_ffn1_down=g_f1d))
    g_in = _hosted(_wgrad_rows, "wgrad_in", dproj, u, IN_PAD // 512,
                   stages=[ex_f1u, sw_f1d, red.join(ex_f1g.result)])
    g_in = g_in.reshape(IN_PAD, d)[:IN_COLS].reshape(N_CHIPS, IN_COLS // N_CHIPS, d)
    g_in = jnp.pad(g_in, ((0, 0), (0, IN_ROWS_PAD - IN_COLS // N_CHIPS), (0, 0)))
    ex_f1d, sw_in = red.exchange(sw_f1d.result), red.swap(dict(w_in=g_in))
    g_mrg = _hosted(_wgrad_cols, "wgrad_merge", u, dgpre, N_CHIPS,
                    stages=[ex_f1d, sw_in, red.join(ex_f1u.result)])

    small_grads = dict(ln_ffn1=dln_ffn1, ln_mix=dln_mix, b_forget=db_forget[:, :FOX_HEADS], b_merge=db_merge,
                       ln_ffn2=dln_ffn2, ln_ple=dln_ple, ln_final=dln_final)
    sizes = {n: w[n].size for n in SMALL}
    ex_in, sw_mrg = red.exchange(sw_in.result), red.swap(dict(w_merge=g_mrg))
    reduced = _hosted(_all_reduce_small, _pack_small(small_grads, loss),
                      stages=[ex_in, sw_mrg, red.join(ex_f1d.result)])
    gsum = _unpack_small(reduced, sizes)
    loss = reduced[0, 0]
    ex_mrg = red.exchange(sw_mrg.result)
    _hosted(_exchange_only, stages=[ex_mrg, red.join(ex_in.result)])
    _hosted(_exchange_only, stages=[red.join(ex_mrg.result)])

    def update(names, stages=()):
        w2, m2, v2 = ([to2d(n, a[n]) for n in names] for a in (w, m, v))
        n = names[0]
        if n in gsum or n == "w_in":
            if n in gsum:
                g2 = gsum[n]
            else:
                mine, other = red.done[n]
                g2 = jnp.where(core == 0, jnp.concatenate([mine, other]), jnp.concatenate([other, mine]))
                g2 = g2[:w2[0].shape[0]]
            if n == "w_in":
                rows3 = lambda a: jnp.transpose(a, (2, 0, 1))
                g3 = g2.reshape(g2.shape[0], 1, g2.shape[1])
                res = [g3] + _hosted(_adamw, rows3(w[n]), g3, rows3(m[n]), rows3(v[n]), stages=stages)
                results[n] = tuple(jnp.transpose(a, (1, 2, 0)) for a in res)
                return
            res = [g2] + _hosted(_adamw, w2[0], g2, m2[0], v2[0], stages=stages)
        else:
            res = _hosted(_adamw_halves, [(w2[q], *red.done[names[q]], m2[q], v2[q]) for q in range(len(names))],
                          stages=stages)
        for q, name in enumerate(names):
            results[name] = tuple(from2d(name, a) for a in res[4 * q:4 * q + 4])

    update(["w_ffn1_gate", "w_ffn1_up", "w_ffn1_down"])
    for n in WEIGHTS:
        if n not in results:
            update([n])

    outs = [[results[n][k] for n in WEIGHTS] for k in range(4)]
    return (loss, dx[None], *outs[0], *outs[1], *outs[2], *outs[3])
```

```python
import functools
import operator

import jax
import jax.numpy as jnp
from jax import lax
from jax.experimental import pallas as pl
from jax.experimental.pallas import tpu as pltpu
from jax.experimental.pallas import tpu_sc as plsc

F32 = jnp.float32
BF = jnp.bfloat16
MESH = pl.DeviceIdType.MESH

EPS = 1e-6
ROPE_BASE = 10000.0
N_CHIPS = 4
RET_HEADS = 4
RET_DIM = 128
RET_WIDTH = RET_HEADS * RET_DIM
RET_CHUNK = 128
RET_SCALE = RET_DIM ** -0.5
FOX_HEADS = 8
FOX_DIM = 64
FOX_WIDTH = FOX_HEADS * FOX_DIM
FOX_SCALE = FOX_DIM ** -0.5
IN_COLS = 4 * RET_WIDTH + 3 * FOX_WIDTH + FOX_HEADS
IN_PAD = 4096
FF_COL = 4 * RET_WIDTH + 3 * FOX_WIDTH
NEG = -1e30

ADAM_LR = 0.001
ADAM_B1 = 0.9
ADAM_B2 = 0.999
ADAM_EPS = 1e-08
ADAM_WD = 0.01
ADAM_STEP = 10

VMEM_LIMIT = 52 * 1024 * 1024

RELAY_MIN_STEPS = 16

NT = (((1,), (1,)), ((), ()))
TN = (((0,), (0,)), ((), ()))

HBM_SPEC = pl.BlockSpec(memory_space=pltpu.HBM)
VMEM_SPEC = pl.BlockSpec(memory_space=pltpu.VMEM)


def _dot(a, b):
    return jnp.dot(a, b, preferred_element_type=F32)


def _dot_nt(a, b):
    return lax.dot_general(a, b, NT, preferred_element_type=F32)


def _dot_tn(a, b):
    return lax.dot_general(a, b, TN, preferred_element_type=F32)


def _rstd(xv):
    return lax.rsqrt(jnp.mean(xv * xv, axis=-1, keepdims=True) + EPS)


def _rms_bwd(dn, xv, r, ln):
    xh = xv * r
    dxh = dn * ln
    dx = r * (dxh - xh * jnp.mean(dxh * xh, axis=-1, keepdims=True))
    return dx, jnp.sum(dn * xh, axis=0, keepdims=True)


def _sigmoid(x):
    return jax.nn.sigmoid(x)


def _tile(n, pref):
    return pref if n % pref == 0 else n


def _row_tile(n, cap):
    best = [t for t in range(16, min(n, cap) + 1, 16) if n % t == 0]
    return best[-1] if best else n


class _Comm:
    def __init__(self, ins, out_shapes, sems, start, wait, aliases=None, relay=None):
        self.ins, self.out_shapes, self.sems, self.start, self.wait = list(ins), list(out_shapes), list(sems), start, wait
        self.aliases = dict(aliases or {})
        self.relay = relay


def _merge(comms):
    comms = [c for c in comms if c is not None]
    if not comms:
        return None
    bounds, ni, no, ns = [], 0, 0, 0
    for c in comms:
        bounds.append((ni, no, ns))
        ni, no, ns = ni + len(c.ins), no + len(c.out_shapes), ns + len(c.sems)

    def run(which):
        def f(ins, outs, sems, **kw):
            for c, (i, o, s) in zip(comms, bounds):
                fn = getattr(c, which)
                if fn is not None:
                    fn(ins[i:i + len(c.ins)], outs[o:o + len(c.out_shapes)], sems[s:s + len(c.sems)],
                       **(kw if c.relay is not None else {}))
        return f

    aliases = {i + a: o + b for c, (i, o, _) in zip(comms, bounds) for a, b in c.aliases.items()}
    relay = run("relay") if any(c.relay is not None for c in comms) else None
    return _Comm([a for c in comms for a in c.ins], [a for c in comms for a in c.out_shapes],
                 [a for c in comms for a in c.sems], run("start"), run("wait"), aliases, relay)


def _split_outs(comms, outs):
    res, o = [], 0
    for c in comms:
        if c is not None:
            res.append(list(outs[o:o + len(c.out_shapes)]))
            o += len(c.out_shapes)
    return res


def _pcall(body, args, *, name, out_shape, grid=(), in_specs=None, out_specs=None, scratch=(), comm=None,
           prefetch=()):
    many = isinstance(out_shape, (list, tuple))
    outs = list(out_shape) if many else [out_shape]
    n_pre, n_in, n_out, n_scr = len(prefetch), len(args), len(outs), len(scratch)
    if in_specs is None:
        in_specs, out_specs = [VMEM_SPEC] * n_in, [VMEM_SPEC] * n_out
    else:
        in_specs, out_specs = list(in_specs), (list(out_specs) if many else [out_specs])
    params = pltpu.CompilerParams(dimension_semantics=("arbitrary",) * len(grid), vmem_limit_bytes=VMEM_LIMIT)
    scalars = [jnp.reshape(s, (1,)).astype(jnp.int32) for s in prefetch]
    ci, co = (len(comm.ins), len(comm.out_shapes)) if comm is not None else (0, 0)

    def wrapped(*refs):
        pre, refs = refs[:n_pre], refs[n_pre:]
        a, ca = refs[:n_in], refs[n_in:n_in + ci]
        o = refs[n_in + ci:n_in + ci + n_out]
        cout = refs[n_in + ci + n_out:n_in + ci + n_out + co]
        s = refs[n_in + ci + n_out + co:n_in + ci + n_out + co + n_scr]
        csem = refs[n_in + ci + n_out + co + n_scr:]
        if comm is None:
            body(*pre, *a, *o, *s)
        elif grid:
            step = functools.reduce(lambda acc, k: acc * grid[k] + pl.program_id(k), range(len(grid)), 0)
            n_steps = functools.reduce(operator.mul, grid)
            relayed = comm.relay is not None and n_steps >= RELAY_MIN_STEPS
            pl.when(step == 0)(lambda: comm.start(ca, cout, csem))
            if relayed:
                pl.when(step == n_steps - n_steps // 8)(lambda: comm.relay(ca, cout, csem))
            body(*pre, *a, *o, *s)
            pl.when(step == n_steps - 1)(lambda: comm.wait(ca, cout, csem, **({"relayed": True} if relayed else {})))
        else:
            comm.start(ca, cout, csem)
            body(*pre, *a, *o, *s)
            comm.wait(ca, cout, csem)

    c_ins, c_outs, c_sems, aliases = ([], [], [], {}) if comm is None else (
        comm.ins, comm.out_shapes, comm.sems, {n_pre + n_in + i: n_out + o for i, o in comm.aliases.items()})
    all_in, all_out = in_specs + [HBM_SPEC] * ci, out_specs + [HBM_SPEC] * co
    all_scr = list(scratch) + c_sems
    if grid:
        args = [pltpu.with_memory_space_constraint(a, pltpu.HBM) for a in args]
    c_ins = [pltpu.with_memory_space_constraint(a, pltpu.HBM) for a in c_ins]
    if n_pre:
        spec = dict(grid_spec=pltpu.PrefetchScalarGridSpec(
            num_scalar_prefetch=n_pre, grid=grid, in_specs=all_in, out_specs=all_out, scratch_shapes=all_scr))
    else:
        spec = dict(grid=grid, in_specs=all_in, out_specs=all_out, scratch_shapes=all_scr)
    res = pl.pallas_call(wrapped, name=name, out_shape=outs + c_outs, input_output_aliases=aliases,
                         compiler_params=params, **spec)(*scalars, *args, *c_ins)
    mine = list(res[:n_out])
    mine = mine if many else mine[0]
    return mine if comm is None else (mine, list(res[n_out:]))


def _peer_chips(x, y):
    return [(1 - x, y), (x, 1 - y), (1 - x, 1 - y)]


def _c_all_gather(bufs):
    n = len(bufs)

    def copies(ins, outs, sems):
        send_sems, recv_sems, fwd_send, fwd_recv = sems
        x, y, c = lax.axis_index("x"), lax.axis_index("y"), lax.axis_index("c")
        me = 2 * x + y
        peers = _peer_chips(x, y)
        chip = [2 * px + py for px, py in peers]

        def ici(g, j, slot):
            return pltpu.make_async_remote_copy(
                src_ref=outs[g].at[me, c], dst_ref=outs[g].at[slot, c], send_sem=send_sems.at[g, j],
                recv_sem=recv_sems.at[g, j], device_id=(*peers[j], c), device_id_type=MESH)

        def d2d(g, j, half):
            return pltpu.make_async_remote_copy(
                src_ref=outs[g].at[chip[j], half], dst_ref=outs[g].at[chip[j], half], send_sem=fwd_send.at[g, j],
                recv_sem=fwd_recv.at[g, j], device_id=(x, y, 1 - c), device_id_type=MESH)

        pairs = [(g, j) for g in range(n) for j in range(3)]
        sends = [ici(g, j, me) for g, j in pairs]
        recvs = [ici(g, j, chip[j]) for g, j in pairs]
        passes = [d2d(g, j, c) for g, j in pairs]
        passed = [d2d(g, j, 1 - c) for g, j in pairs]
        return sends, recvs, passes, passed

    def start(ins, outs, sems):
        for cp in copies(ins, outs, sems)[0]:
            cp.start()

    def relay(ins, outs, sems):
        _, recvs, passes, _ = copies(ins, outs, sems)
        for rcv, fwd in zip(recvs, passes):
            rcv.wait_recv()
            fwd.start()

    def wait(ins, outs, sems, relayed=False):
        if not relayed:
            relay(ins, outs, sems)
        sends, _, passes, passed = copies(ins, outs, sems)
        for cp in passed:
            cp.wait_recv()
        for cp in sends + passes:
            cp.wait_send()

    pair_sems = pltpu.SemaphoreType.DMA((n, 3))
    return _Comm(bufs, [jax.ShapeDtypeStruct(s.shape, s.dtype) for s in bufs], [pair_sems] * 4, start, wait,
                 aliases={g: g for g in range(n)}, relay=relay)


def _start_wait(copies):
    def start(ins, outs, sems):
        local, sends, _ = copies(ins, outs, sems)
        for cp in local + sends:
            cp.start()

    def wait(ins, outs, sems):
        local, sends, recvs = copies(ins, outs, sems)
        for cp in recvs:
            cp.wait_recv()
        for cp in sends:
            cp.wait_send()
        for cp in local:
            cp.wait()

    return start, wait


def _c_half_swap(grads):
    n = len(grads)

    def copies(ins, outs, sems):
        send_sems, recv_sems = sems
        x, y, c = lax.axis_index("x"), lax.axis_index("y"), lax.axis_index("c")
        sends = []
        for g in range(n):
            half = ins[g].shape[1] // 2
            sends.append(pltpu.make_async_remote_copy(
                src_ref=ins[g].at[:, pl.ds((1 - c) * half, half), :], dst_ref=outs[g],
                send_sem=send_sems.at[g], recv_sem=recv_sems.at[g], device_id=(x, y, 1 - c), device_id_type=MESH))
        return [], sends, sends

    return _Comm(
        grads, [jax.ShapeDtypeStruct((N_CHIPS, s.shape[1] // 2, s.shape[2]), s.dtype) for s in grads],
        [pltpu.SemaphoreType.DMA((n,)), pltpu.SemaphoreType.DMA((n,))], *_start_wait(copies))


def _c_chip_exchange(parts):
    n = len(parts)

    def copies(ins, outs, sems):
        send_sems, recv_sems = sems
        x, y, c = lax.axis_index("x"), lax.axis_index("y"), lax.axis_index("c")
        peers = _peer_chips(x, y)

        def remote(g, j):
            return pltpu.make_async_remote_copy(
                src_ref=ins[g].at[2 * peers[j][0] + peers[j][1]], dst_ref=outs[g].at[j],
                send_sem=send_sems.at[g, j], recv_sem=recv_sems.at[g, j], device_id=(*peers[j], c),
                device_id_type=MESH)

        sends = [remote(g, j) for g in range(n) for j in range(3)]
        return [], sends, sends

    return _Comm(
        parts, [jax.ShapeDtypeStruct((3,) + s.shape[1:], s.dtype) for s in parts],
        [pltpu.SemaphoreType.DMA((n, 3)), pltpu.SemaphoreType.DMA((n, 3))], *_start_wait(copies))


def _c_join(halves):
    n = len(halves)

    def copies(ins, outs, sems):
        send_sems, recv_sems = sems
        x, y, c = lax.axis_index("x"), lax.axis_index("y"), lax.axis_index("c")
        sends = [pltpu.make_async_remote_copy(
            src_ref=ins[g], dst_ref=outs[g], send_sem=send_sems.at[g], recv_sem=recv_sems.at[g],
            device_id=(x, y, 1 - c), device_id_type=MESH) for g in range(n)]
        return [], sends, sends

    return _Comm(
        halves, [jax.ShapeDtypeStruct(s.shape, s.dtype) for s in halves],
        [pltpu.SemaphoreType.DMA((n,)), pltpu.SemaphoreType.DMA((n,))], *_start_wait(copies))


def _exchange_only(comm=None):
    def body(o_ref):
        o_ref[...] = jnp.zeros_like(o_ref)

    return _pcall(body, [], name="exchange_only", out_shape=jax.ShapeDtypeStruct((8, 128), F32), comm=comm)


def _all_reduce_small(v, comm=None):
    rows = v.shape[0]

    def body(v_ref, out_ref, buf, send_sems, recv_sems):
        x, y, c = lax.axis_index("x"), lax.axis_index("y"), lax.axis_index("c")
        me = 4 * x + 2 * y + c
        buf[me] = v_ref[...]
        flips = [(fx, fy, fc) for fx in (0, 1) for fy in (0, 1) for fc in (0, 1)][1:]

        def peer(k):
            fx, fy, fc = flips[k]
            px, py, pc = x ^ fx, y ^ fy, c ^ fc
            return (px, py, pc), 4 * px + 2 * py + pc

        def copy(k, slot):
            return pltpu.make_async_remote_copy(
                src_ref=buf.at[slot], dst_ref=buf.at[slot], send_sem=send_sems.at[k],
                recv_sem=recv_sems.at[k], device_id=peer(k)[0], device_id_type=MESH)

        sends = [copy(k, me) for k in range(7)]
        for cp in sends:
            cp.start()
        for k in range(7):
            copy(k, peer(k)[1]).wait_recv()
        for cp in sends:
            cp.wait_send()
        acc = buf[0]
        for d in range(1, 8):
            acc = acc + buf[d]
        out_ref[...] = acc

    return _pcall(body, [v], name="all_reduce_small", out_shape=jax.ShapeDtypeStruct((rows, 128), F32),
                  scratch=[pltpu.VMEM((8, rows, 128), F32), pltpu.SemaphoreType.DMA((7,)),
                           pltpu.SemaphoreType.DMA((7,))], comm=comm)


def _add_halves(pairs):
    k = len(pairs)

    def body(h_ref, *refs):
        for a_ref, b_ref, o_ref in zip(refs[0:2 * k:2], refs[1:2 * k:2], refs[2 * k:]):
            o_ref[...] = (a_ref[...].astype(F32) + b_ref[...].astype(F32)).astype(o_ref.dtype)

    in_specs, out_specs = [], []
    for _, got in pairs:
        _, h, c = got.shape
        spec = pl.BlockSpec((1, h, c), lambda j, h_ref: (j, 0, 0))
        in_specs += [pl.BlockSpec((1, h, c), lambda j, h_ref: (j, h_ref[0], 0)), spec]
        out_specs.append(spec)
    return _pcall(body, [a for pair in pairs for a in pair], name="add_halves", grid=(N_CHIPS,),
                  prefetch=[lax.axis_index("c")], in_specs=in_specs, out_specs=out_specs,
                  out_shape=[jax.ShapeDtypeStruct(got.shape, BF) for _, got in pairs])


def _sum_chips(pairs):
    k = len(pairs)
    n_steps = 2 if all(parts.shape[1] % 32 == 0 for parts, _ in pairs) else 1
    me = 2 * lax.axis_index("x") + lax.axis_index("y")

    def body(me_ref, *refs):
        for p_ref, r_ref, o_ref in zip(refs[0:2 * k:2], refs[1:2 * k:2], refs[2 * k:]):
            acc = p_ref[0].astype(F32)
            for s in range(N_CHIPS - 1):
                acc = acc + r_ref[s].astype(F32)
            o_ref[...] = acc

    in_specs, out_specs = [], []
    for parts, _ in pairs:
        _, h, c = parts.shape
        th = h // n_steps
        in_specs += [pl.BlockSpec((1, th, c), lambda i, me_ref: (me_ref[0], i, 0)),
                     pl.BlockSpec((N_CHIPS - 1, th, c), lambda i, me_ref: (0, i, 0))]
        out_specs.append(pl.BlockSpec((th, c), lambda i, me_ref: (i, 0)))
    return _pcall(body, [a for pair in pairs for a in pair], name="sum_chips", grid=(n_steps,), prefetch=[me],
                  in_specs=in_specs, out_specs=out_specs,
                  out_shape=[jax.ShapeDtypeStruct(parts.shape[1:], F32) for parts, _ in pairs])


def _adam_update(w, gv, m, v, d_ref, nm_ref, nv_ref):
    c1 = 1.0 / (1.0 - ADAM_B1 ** ADAM_STEP)
    c2 = 1.0 / (1.0 - ADAM_B2 ** ADAM_STEP)
    nm = ADAM_B1 * m + (1.0 - ADAM_B1) * gv
    nv = ADAM_B2 * v + (1.0 - ADAM_B2) * (gv * gv)
    nm_ref[...] = nm
    nv_ref[...] = nv
    d_ref[...] = -ADAM_LR * ((nm * c1) / (jnp.sqrt(nv * c2) + ADAM_EPS) + ADAM_WD * w)


def _adamw(w, g, m, v, comm=None):
    r, c = w.shape[0], w.shape[-1]
    tr = _row_tile(r, 512)

    def body(w_ref, g_ref, m_ref, v_ref, d_ref, nm_ref, nv_ref):
        _adam_update(w_ref[...], g_ref[...], m_ref[...], v_ref[...], d_ref, nm_ref, nv_ref)

    mid = (1,) * (w.ndim - 2)
    spec = pl.BlockSpec((tr,) + mid + (c,), lambda i: (i,) + (0,) * (w.ndim - 1))
    sds = jax.ShapeDtypeStruct(w.shape, F32)
    return _pcall(body, [w, g, m, v], name="adamw", grid=(r // tr,), out_shape=[sds, sds, sds],
                  in_specs=[spec] * 4, out_specs=[spec] * 3, comm=comm)


def _adamw_vectors(items):
    k = len(items)

    def body(*refs):
        ins, outs = refs[:4 * k], refs[4 * k:]
        for q in range(k):
            w_ref, g_ref, m_ref, v_ref = ins[4 * q:4 * q + 4]
            _adam_update(w_ref[...], g_ref[...], m_ref[...], v_ref[...], *outs[3 * q:3 * q + 3])

    return _pcall(body, [a for it in items for a in it], name="adamw_vectors",
                  out_shape=[jax.ShapeDtypeStruct(it[0].shape, F32) for it in items for _ in range(3)])


def _adamw_halves(items, comm=None):
    k = len(items)
    r, c = items[0][0].shape
    h = r // 2
    tr = _row_tile(h, min(512, (VMEM_LIMIT * 3 // 4) // (k * 9 * 2 * 4 * c)))
    nb = h // tr
    core = lax.axis_index("c")

    def body(c_ref, *refs):
        ins, outs = refs[:5 * k], refs[5 * k:]
        for q in range(k):
            w_ref, gm_ref, go_ref, m_ref, v_ref = ins[5 * q:5 * q + 5]
            g_ref, d_ref, nm_ref, nv_ref = outs[4 * q:4 * q + 4]
            gv = jnp.where(pl.program_id(0) == c_ref[0], gm_ref[...], go_ref[...])
            g_ref[...] = gv
            _adam_update(w_ref[...], gv, m_ref[...], v_ref[...], d_ref, nm_ref, nv_ref)

    full = pl.BlockSpec((tr, c), lambda hh, i, c_ref: (hh * nb + i, 0))
    half = pl.BlockSpec((tr, c), lambda hh, i, c_ref: (i, 0))
    sds = jax.ShapeDtypeStruct((r, c), F32)
    return _pcall(body, [a for it in items for a in it], name="adamw_halves", grid=(2, nb), prefetch=[core],
                  out_shape=[sds] * (4 * k), in_specs=[full, half, half, full, full] * k, out_specs=[full] * (4 * k),
                  comm=comm)


SC_CORES, SC_TILES, SC_LANES = 2, 16, 16
SC_BLOCK_ROWS, SC_BLOCK_COLS = 8, 512


def _sc_adamw_halves(items):
    k = len(items)
    r, c = items[0][0].shape
    h = r // 2
    bc = min(c, SC_BLOCK_COLS)
    c1 = 1.0 / (1.0 - ADAM_B1 ** ADAM_STEP)
    c2 = 1.0 / (1.0 - ADAM_B2 ** ADAM_STEP)
    mesh = plsc.VectorSubcoreMesh(core_axis_name="sc_core", subcore_axis_name="sc_tile",
                                  num_cores=SC_CORES, num_subcores=SC_TILES)
    spec = pl.BlockSpec(block_shape=(SC_BLOCK_ROWS, bc), index_map=lambda i, j: (i, j))

    def block(w_v, gin_v, m_v, v_v, g_v, d_v, nm_v, nv_v):
        @pl.loop(0, SC_BLOCK_ROWS)
        def _(row):
            @pl.loop(0, bc, step=SC_LANES)
            def _(col):
                at = (pl.ds(row, 1), pl.ds(col, SC_LANES))
                gv = gin_v.at[*at][...]
                nm = ADAM_B1 * m_v.at[*at][...] + (1.0 - ADAM_B1) * gv
                nv = ADAM_B2 * v_v.at[*at][...] + (1.0 - ADAM_B2) * (gv * gv)
                g_v.at[*at][...] = gv
                nm_v.at[*at][...] = nm
                nv_v.at[*at][...] = nv
                d_v.at[*at][...] = -ADAM_LR * ((nm * c1) / (jnp.sqrt(nv * c2) + ADAM_EPS) + ADAM_WD * w_v.at[*at][...])

    def kern(*refs):
        ins, outs = refs[:5 * k], refs[5 * k:]
        core = lax.axis_index("c")

        def half(q, hh, mine):
            w_hbm, gm_hbm, go_hbm, m_hbm, v_hbm = ins[5 * q:5 * q + 5]
            rows = pl.ds(hh * h, h)
            pltpu.emit_pipeline(
                block, grid=(h // SC_BLOCK_ROWS, c // bc), in_specs=[spec] * 4, out_specs=[spec] * 4,
                core_axis_name=("sc_core", "sc_tile"), dimension_semantics=(pltpu.PARALLEL, pltpu.PARALLEL),
                trace_scopes=False,
            )(w_hbm.at[rows, :], gm_hbm if mine else go_hbm, m_hbm.at[rows, :], v_hbm.at[rows, :],
              *(o.at[rows, :] for o in outs[4 * q:4 * q + 4]))

        for q in range(k):
            for hh in range(2):
                pl.when(core == hh)(lambda q=q, hh=hh: half(q, hh, True))
                pl.when(core != hh)(lambda q=q, hh=hh: half(q, hh, False))

    sds = jax.ShapeDtypeStruct((r, c), F32)
    return pl.kernel(kern, out_type=[sds] * (4 * k), mesh=mesh, scratch_types=[], name="sc_adamw_halves")(
        *(a for it in items for a in it))


def _wgrad(name, a, b, a_spec, b_spec, m, n, nb, comm):
    def body(a_ref, b_ref, o_ref):
        o_ref[...] = _dot_tn(a_ref[...], b_ref[...]).astype(o_ref.dtype)

    return _pcall(body, [a, b], name=name, grid=(nb,), out_shape=jax.ShapeDtypeStruct((nb, m, n), BF),
                  in_specs=[a_spec, b_spec], out_specs=pl.BlockSpec((None, m, n), lambda j: (j, 0, 0)), comm=comm)


def _wgrad_cols(name, a, b, nb, comm=None):
    t_tok, m = a.shape
    n = b.shape[1] // nb
    return _wgrad(name, a, b, pl.BlockSpec((t_tok, m), lambda j: (0, 0)), pl.BlockSpec((t_tok, n), lambda j: (0, j)),
                  m, n, nb, comm)


def _wgrad_rows(name, a, b, nb, comm=None):
    t_tok, n = b.shape
    m = a.shape[1] // nb
    return _wgrad(name, a, b, pl.BlockSpec((t_tok, m), lambda j: (0, j)), pl.BlockSpec((t_tok, n), lambda j: (0, 0)),
                  m, n, nb, comm)


def _wgrad_a_shared(name, a, b4, comm=None):
    t_tok, m = a.shape
    nb, _, n = b4.shape
    return _wgrad(name, a, b4, pl.BlockSpec((t_tok, m), lambda j: (0, 0)),
                  pl.BlockSpec((None, t_tok, n), lambda j: (j, 0, 0)), m, n, nb, comm)


def _wgrad_b_shared(name, a4, b, comm=None):
    nb, t_tok, m = a4.shape
    n = b.shape[1]
    return _wgrad(name, a4, b, pl.BlockSpec((None, t_tok, m), lambda j: (j, 0, 0)),
                  pl.BlockSpec((t_tok, n), lambda j: (0, 0)), m, n, nb, comm)


def _w4_spec(r, c):
    return pl.BlockSpec((None, r, c), lambda i, j: (j, 0, 0))


FFN_ROW_CHUNK = 256


def _row_chunks(tm):
    rc = FFN_ROW_CHUNK if tm % FFN_ROW_CHUNK == 0 else tm
    return [slice(r, r + rc) for r in range(0, tm, rc)]


def _ffn_fwd(h, ln, wg4, wu4, wd4, comm=None):
    t_tok, d = h.shape
    f = wg4.shape[-2]
    tm = _tile(t_tok, 512)

    def body(h_ref, ln_ref, wg_ref, wu_ref, wd_ref, ho_ref, n_ref, g_ref, u_ref, n_s, acc):
        j = pl.program_id(1)

        @pl.when(j == 0)
        def _():
            xv = h_ref[...]
            nv = (xv * _rstd(xv) * ln_ref[...]).astype(BF)
            n_s[...] = nv
            n_ref[...] = nv
            acc[...] = jnp.zeros_like(acc)

        nv = n_s[...]
        g = _dot_nt(nv, wg_ref[...])
        u = _dot_nt(nv, wu_ref[...])
        g_ref[...] = g.astype(BF)
        u_ref[...] = u.astype(BF)
        a = (g * _sigmoid(g) * u).astype(BF)
        acc[...] += _dot(a, wd_ref[...])

        @pl.when(j == N_CHIPS - 1)
        def _():
            ho_ref[...] = h_ref[...] + 0.5 * acc[...]

    row = pl.BlockSpec((tm, d), lambda i, j: (i, 0))
    gu = pl.BlockSpec((None, tm, f), lambda i, j: (j, i, 0))
    gu_sds = jax.ShapeDtypeStruct((N_CHIPS, t_tok, f), BF)
    return _pcall(
        body, [h, ln, wg4, wu4, wd4], name="ffn_fwd", grid=(t_tok // tm, N_CHIPS),
        out_shape=[jax.ShapeDtypeStruct((t_tok, d), F32), jax.ShapeDtypeStruct((t_tok, d), BF), gu_sds, gu_sds],
        in_specs=[row, pl.BlockSpec((1, d), lambda i, j: (0, 0)), _w4_spec(f, d), _w4_spec(f, d), _w4_spec(f, d)],
        out_specs=[row, row, gu, gu],
        scratch=[pltpu.VMEM((tm, d), BF), pltpu.VMEM((tm, d), F32)], comm=comm)


def _ffn_bwd(dho, h, ln, g4, u4, wg4, wu4, wd4, comm=None):
    t_tok, d = h.shape
    f = wg4.shape[-2]
    tm = _tile(t_tok, 512)

    def body(dho_ref, h_ref, ln_ref, g_ref, u_ref, wg_ref, wu_ref, wd_ref,
             dhi_ref, dln_ref, dg_ref, du_ref, a_ref, dhb_ref, dhb_s, dn_acc):
        i, j = pl.program_id(0), pl.program_id(1)

        @pl.when(j == 0)
        def _():
            dhb = (0.5 * dho_ref[...]).astype(BF)
            dhb_s[...] = dhb
            dhb_ref[...] = dhb
            dn_acc[...] = jnp.zeros_like(dn_acc)

        @pl.when((i == 0) & (j == 0))
        def _():
            dln_ref[...] = jnp.zeros_like(dln_ref)

        for rows in _row_chunks(tm):
            g = g_ref[rows, :].astype(F32)
            u = u_ref[rows, :].astype(F32)
            s = _sigmoid(g)
            sg = g * s
            a_ref[rows, :] = (sg * u).astype(BF)
            da = _dot_nt(dhb_s[rows, :], wd_ref[...])
            dg = (da * u * (s * (1.0 + g * (1.0 - s)))).astype(BF)
            du = (da * sg).astype(BF)
            dg_ref[rows, :] = dg
            du_ref[rows, :] = du
            dn_acc[rows, :] += _dot(dg, wg_ref[...]) + _dot(du, wu_ref[...])

        @pl.when(j == N_CHIPS - 1)
        def _():
            xv = h_ref[...]
            dx, dln = _rms_bwd(dn_acc[...], xv, _rstd(xv), ln_ref[...])
            dln_ref[...] += dln
            dhi_ref[...] = dho_ref[...] + dx

    row = pl.BlockSpec((tm, d), lambda i, j: (i, 0))
    vec = pl.BlockSpec((1, d), lambda i, j: (0, 0))
    gu = pl.BlockSpec((None, tm, f), lambda i, j: (j, i, 0))
    gu_sds = jax.ShapeDtypeStruct((N_CHIPS, t_tok, f), BF)
    return _pcall(
        body, [dho, h, ln, g4, u4, wg4, wu4, wd4], name="ffn_bwd", grid=(t_tok // tm, N_CHIPS),
        out_shape=[jax.ShapeDtypeStruct((t_tok, d), F32), jax.ShapeDtypeStruct((1, d), F32),
                   gu_sds, gu_sds, gu_sds, jax.ShapeDtypeStruct((t_tok, d), BF)],
        in_specs=[row, row, vec, gu, gu, _w4_spec(f, d), _w4_spec(f, d), _w4_spec(f, d)],
        out_specs=[row, vec, gu, gu, gu, row],
        scratch=[pltpu.VMEM((tm, d), BF), pltpu.VMEM((tm, d), F32)], comm=comm)


def _rope_tables(pos_col, inv_freq2, comm=None):
    t_tok = pos_col.shape[0]

    def body(p_ref, f_ref, cos_ref, sin_ref):
        ang = p_ref[...] * f_ref[...]
        lane = lax.broadcasted_iota(jnp.int32, ang.shape, 1)
        s = jnp.sin(ang)
        cos_ref[...] = jnp.cos(ang)
        sin_ref[...] = jnp.where((lane & 1) == 0, -s, s)

    sds = jax.ShapeDtypeStruct((t_tok, 128), F32)
    return _pcall(body, [pos_col, inv_freq2], name="rope_tables", out_shape=[sds, sds], comm=comm)


def _swap_pairs(x):
    lane = lax.broadcasted_iota(jnp.int32, x.shape, 1)
    return jnp.where((lane & 1) == 0, pltpu.roll(x, 127, 1), pltpu.roll(x, 1, 1))


def _mix_in(h, ln, w_in, wm4, b_m, cos_t, sin_t, comm=None):
    t_tok, d = h.shape
    cm = wm4.shape[-1]
    tm = _tile(t_tok, 256)

    def body(h_ref, ln_ref, win_ref, wm_ref, bm_ref, cos_ref, sin_ref,
             u_ref, rq_ref, rk_ref, rv_ref, rg_ref, fq_ref, fk_ref, fv_ref, ff_ref, ga_ref, gb_ref):
        xv = h_ref[...]
        ub = (xv * _rstd(xv) * ln_ref[...]).astype(BF)
        u_ref[...] = ub
        cosv, sinv = cos_ref[...], sin_ref[...]

        def sec(k):
            return _dot_nt(ub, win_ref[k * 512:(k + 1) * 512, :])

        def rot(xh):
            return xh * cosv + _swap_pairs(xh) * sinv

        pq, pk = sec(0), sec(1)
        for hh in range(RET_HEADS):
            sl = slice(hh * RET_DIM, (hh + 1) * RET_DIM)
            rq_ref[:, sl] = rot(pq[:, sl]).astype(BF)
            rk_ref[:, sl] = (rot(pk[:, sl]) * RET_SCALE).astype(BF)
        rv_ref[...] = sec(2).astype(BF)
        rg_ref[...] = sec(3).astype(BF)
        fq_ref[...] = (sec(4) * FOX_SCALE).astype(BF)
        fk_ref[...] = sec(5).astype(BF)
        fv_ref[...] = sec(6).astype(BF)
        ff_ref[...] = _dot_nt(ub, win_ref[FF_COL:FF_COL + 128, :])
        for j in range(N_CHIPS):
            gs = _sigmoid(_dot(ub, wm_ref[j]) + bm_ref[:, j * cm:(j + 1) * cm]).astype(BF)
            col = j * cm
            if col < d:
                ga_ref[:, col:col + cm] = gs
            else:
                gb_ref[:, col - d:col - d + cm] = gs

    row = lambda c: pl.BlockSpec((tm, c), lambda i: (i, 0))
    full = lambda *s: pl.BlockSpec(s, lambda i: (0,) * len(s))
    sds = lambda c, dt: jax.ShapeDtypeStruct((t_tok, c), dt)
    return _pcall(
        body, [h, ln, w_in, wm4, b_m, cos_t, sin_t], name="mix_in", grid=(t_tok // tm,),
        out_shape=[sds(d, BF)] + [sds(512, BF)] * 7 + [sds(128, F32), sds(d, BF), sds(d, BF)],
        in_specs=[row(d), full(1, d), full(IN_PAD, d), full(N_CHIPS, d, cm), full(1, 2 * d), row(128), row(128)],
        out_specs=[row(d)] + [row(512)] * 7 + [row(128), row(d), row(d)], comm=comm)


def _split3(x):
    hi = x.astype(BF)
    r1 = x - hi.astype(F32)
    mid = r1.astype(BF)
    lo = (r1 - mid.astype(F32)).astype(BF)
    return hi, mid, lo


def _aug_lane():
    return lax.broadcasted_iota(jnp.int32, (1, 128), 1) & (FOX_DIM - 1)


def _aug_put(base, k0, parts):
    w = _aug_lane()
    for i, part in enumerate(parts):
        base = jnp.where(w == k0 + i, part, base)
    return base


def _forget_fwd(ffl, b_pad):
    t_tok = ffl.shape[0]
    tb = _tile(t_tok, 256)

    def body(ff_ref, b_ref, aq_ref, ak_ref, cum_s):
        r = lax.broadcasted_iota(jnp.int32, (tb, tb), 0)
        c = lax.broadcasted_iota(jnp.int32, (tb, tb), 1)
        tri = jnp.where(c <= r, 1.0, 0.0).astype(BF)
        carry = jnp.zeros((1, 128), F32)
        for i in range(t_tok // tb):
            z = ff_ref[i * tb:(i + 1) * tb, :] + b_ref[...]
            lf = jnp.minimum(z, 0.0) - jnp.log(1.0 + jnp.exp(-jnp.abs(z)))
            hi, mid, lo = _split3(lf)
            cs = _dot(tri, hi) + _dot(tri, mid) + _dot(tri, lo) + carry
            cum_s[i * tb:(i + 1) * tb, :] = cs
            carry = cs[tb - 1:tb, :]
        x = cum_s[...]
        first = lax.broadcasted_iota(jnp.int32, (1, 128), 1) < FOX_DIM
        w = _aug_lane()
        one = jnp.ones((t_tok, 128), BF)
        zero = jnp.zeros((t_tok, 128), BF)
        for pp in range(FOX_HEADS // 2):
            other = jnp.where(first, x[:, 2 * pp + 1:2 * pp + 2], x[:, 2 * pp:2 * pp + 1])
            parts = _split3(other)
            aq = jnp.where((w >= 3) & (w < 6), one, zero)
            ak = jnp.where((w < 3) | ((w >= 6) & (w < 9)), one, zero)
            aq_ref[:, pp * 128:(pp + 1) * 128] = _aug_put(aq, 0, parts)
            ak_ref[:, pp * 128:(pp + 1) * 128] = _aug_put(ak, 3, [-q for q in parts])

    sds = jax.ShapeDtypeStruct((t_tok, FOX_WIDTH), BF)
    return _pcall(body, [ffl, b_pad], name="forget_fwd", out_shape=[sds, sds],
                  scratch=[pltpu.VMEM((t_tok, 128), F32)])


def _forget_bwd(dcum_t, dcum_q, ffl, b_pad):
    t_tok = ffl.shape[0]
    tb = _tile(t_tok, 256)

    def body(dc_ref, dq_ref, ff_ref, b_ref, dff_ref, db_ref, pad_s, d_s):
        pad_s[...] = jnp.zeros_like(pad_s)
        pad_s[0:FOX_HEADS, :] = dc_ref[...]
        dsum = pad_s[...].T
        lane = lax.broadcasted_iota(jnp.int32, (t_tok, 128), 1)
        for hh in range(FOX_HEADS):
            dsum = dsum + jnp.where(lane == hh, dq_ref[:, hh * FOX_DIM:hh * FOX_DIM + 1], 0.0)
        d_s[...] = dsum
        r = lax.broadcasted_iota(jnp.int32, (tb, tb), 0)
        c = lax.broadcasted_iota(jnp.int32, (tb, tb), 1)
        tri = jnp.where(c >= r, 1.0, 0.0).astype(BF)
        carry = jnp.zeros((1, 128), F32)
        db = jnp.zeros((1, 128), F32)
        for i in reversed(range(t_tok // tb)):
            hi, mid, lo = _split3(d_s[i * tb:(i + 1) * tb, :])
            dlf = _dot(tri, hi) + _dot(tri, mid) + _dot(tri, lo) + carry
            carry = dlf[0:1, :]
            z = ff_ref[i * tb:(i + 1) * tb, :] + b_ref[...]
            dff = dlf * _sigmoid(-z)
            dff_ref[i * tb:(i + 1) * tb, :] = dff.astype(BF)
            db = db + jnp.sum(dff, axis=0, keepdims=True)
        db_ref[...] = db

    return _pcall(
        body, [dcum_t, dcum_q, ffl, b_pad], name="forget_bwd",
        out_shape=[jax.ShapeDtypeStruct((t_tok, 128), BF), jax.ShapeDtypeStruct((1, 128), F32)],
        scratch=[pltpu.VMEM((128, t_tok), F32), pltpu.VMEM((t_tok, 128), F32)])


def _first_half():
    return lax.broadcasted_iota(jnp.int32, (1, 128), 1) < FOX_DIM


def _head_rows(x2, a2, hh):
    return jnp.where(_first_half(), x2, a2) if hh == 0 else jnp.where(_first_half(), a2, x2)


def _head_only(x2, hh):
    zero = jnp.zeros_like(x2)
    return jnp.where(_first_half(), x2, zero) if hh == 0 else jnp.where(_first_half(), zero, x2)


def _causal_diag(s):
    rows = lax.broadcasted_iota(jnp.int32, s.shape, 0)
    cols = lax.broadcasted_iota(jnp.int32, s.shape, 1)
    return jnp.where(cols <= rows, s, NEG)


def _diag_or_below(qi, ki, step):
    pl.when(ki < qi)(lambda: step(False))
    pl.when(ki == qi)(lambda: step(True))


def _tri_rows(s, n):
    qi = sum((s >= r * (r + 1) // 2).astype(jnp.int32) for r in range(1, n))
    return qi, s - (qi * (qi + 1)) // 2


def _tri_cols(s, n):
    ki = sum((s >= k * n - k * (k - 1) // 2).astype(jnp.int32) for k in range(1, n))
    return ki, ki + s - (ki * n - (ki * (ki - 1)) // 2)


def _fox_fwd(fq, fk, fv, aq, ak, comm=None):
    t_tok = fq.shape[0]
    t = _tile(t_tok, 512)
    nq = t_tok // t
    npair = FOX_HEADS // 2

    def body(q_ref, k_ref, v_ref, aq_ref, ak_ref, o_ref, of_ref, aqb_ref, m_s, l_s, acc_s):
        qi, ki = _tri_rows(pl.program_id(1), nq)

        @pl.when(ki == 0)
        def _():
            m_s[...] = jnp.full_like(m_s, NEG)
            l_s[...] = jnp.zeros_like(l_s)
            acc_s[...] = jnp.zeros_like(acc_s)

        def step(diag):
            q2, k2, v2, aq2, ak2 = q_ref[...], k_ref[...], v_ref[...], aq_ref[...], ak_ref[...]
            for hh in range(2):
                s = _dot_nt(_head_rows(q2, aq2, hh), _head_rows(k2, ak2, hh))
                if diag:
                    s = _causal_diag(s)
                m_prev = m_s[hh]
                m_new = jnp.maximum(m_prev, jnp.max(s, axis=1, keepdims=True))
                alpha = jnp.exp(m_prev - m_new)
                p = jnp.exp(s - jnp.tile(m_new, (1, t // 128)))
                l_s[hh] = alpha * l_s[hh] + jnp.sum(p, axis=1, keepdims=True)
                acc_s[hh] = alpha * acc_s[hh] + _dot(p.astype(BF), v2)
                m_s[hh] = m_new

        _diag_or_below(qi, ki, step)

        @pl.when(ki == qi)
        def _():
            first = _first_half()
            o = jnp.where(first, acc_s[0] / l_s[0], acc_s[1] / l_s[1])
            o_ref[...] = o.astype(BF)
            of_ref[...] = o
            other = jnp.where(first, m_s[1] + jnp.log(l_s[1]), m_s[0] + jnp.log(l_s[0]))
            aqb_ref[...] = _aug_put(aq_ref[...], 6, _split3(-other))

    qs = pl.BlockSpec((t, 128), lambda p, s: (_tri_rows(s, nq)[0], p))
    ks = pl.BlockSpec((t, 128), lambda p, s: (_tri_rows(s, nq)[1], p))
    stat = pltpu.VMEM((2, t, 128), F32)
    return _pcall(
        body, [fq, fk, fv, aq, ak], name="fox_fwd", grid=(npair, nq * (nq + 1) // 2),
        out_shape=[jax.ShapeDtypeStruct((t_tok, FOX_WIDTH), BF), jax.ShapeDtypeStruct((t_tok, FOX_WIDTH), F32),
                   jax.ShapeDtypeStruct((t_tok, FOX_WIDTH), BF)],
        in_specs=[qs, ks, ks, qs, ks], out_specs=[qs, qs, qs], scratch=[stat, stat, stat], comm=comm)


def _fox_ds(q2, k2, v2, do2, aq2, ak2, ad2, hh, diag):
    s = _dot_nt(_head_rows(q2, aq2, hh), _head_rows(k2, ak2, hh))
    if diag:
        s = _causal_diag(s)
    p = jnp.exp(s)
    av = jnp.where(_aug_lane() < 3, 1.0, 0.0).astype(BF)
    dp = _dot_nt(_head_rows(do2, ad2, hh), _head_rows(v2, jnp.broadcast_to(av, v2.shape), hh))
    return p, p * dp


def _fox_bwd(fq, fk, fv, do, aqb, ak, ad, comm=None):
    t_tok = fq.shape[0]
    t = _tile(t_tok, 512)
    nq = t_tok // t
    npair = FOX_HEADS // 2
    n_steps = nq * (nq + 1) // 2

    def body(q_ref, k_ref, v_ref, do_ref, aq_ref, ak_ref, ad_ref, dq_ref, dk_ref, dv_ref, dck_ref, dcq_ref,
             dk_s, dv_s, dq_s, rs_s):
        step_id = pl.program_id(1)
        ki, qi = _tri_cols(step_id, nq)

        @pl.when(step_id == 0)
        def _():
            dq_s[...] = jnp.zeros_like(dq_s)
            rs_s[...] = jnp.zeros_like(rs_s)

        @pl.when(qi == ki)
        def _():
            dk_s[...] = jnp.zeros_like(dk_s)
            dv_s[...] = jnp.zeros_like(dv_s)
            dck_ref[...] = jnp.zeros_like(dck_ref)

        rows = pl.ds(qi * t if isinstance(qi, int) else pl.multiple_of(qi * t, t), t)

        def step(diag):
            q2, k2, v2, do2 = q_ref[...], k_ref[...], v_ref[...], do_ref[...]
            dq = []
            for hh in range(2):
                p, ds = _fox_ds(q2, k2, v2, do2, aq_ref[...], ak_ref[...], ad_ref[...], hh, diag)
                dsb = ds.astype(BF)
                dv_s[...] += _dot_tn(p.astype(BF), _head_only(do2, hh))
                dk_s[...] += _dot_tn(dsb, _head_only(q2, hh))
                dq.append(_dot(dsb, k2))
                dck_ref[hh] = dck_ref[hh] - jnp.sum(ds, axis=0, keepdims=True)
                rs_s[hh, rows, :] = rs_s[hh, rows, :] + jnp.sum(ds, axis=1, keepdims=True)
            dq_s[rows, :] = dq_s[rows, :] + jnp.where(_first_half(), dq[0], dq[1])

        _diag_or_below(qi, ki, step)

        @pl.when(qi == nq - 1)
        def _():
            dk_ref[...] = dk_s[...].astype(BF)
            dv_ref[...] = dv_s[...].astype(BF)

        @pl.when(step_id == n_steps - 1)
        def _():
            dq_ref[...] = (dq_s[...] * FOX_SCALE).astype(BF)
            dcq_ref[...] = jnp.where(_first_half(), rs_s[0], rs_s[1])

    qs = pl.BlockSpec((t, 128), lambda p, s: (_tri_cols(s, nq)[1], p))
    ks = pl.BlockSpec((t, 128), lambda p, s: (_tri_cols(s, nq)[0], p))
    cks = pl.BlockSpec((2, 1, t), lambda p, s: (p, 0, _tri_cols(s, nq)[0]))
    seq = pl.BlockSpec((t_tok, 128), lambda p, s: (0, p))
    sds = jax.ShapeDtypeStruct((t_tok, FOX_WIDTH), BF)
    return _pcall(
        body, [fq, fk, fv, do, aqb, ak, ad], name="fox_bwd", grid=(npair, n_steps),
        out_shape=[sds, sds, sds, jax.ShapeDtypeStruct((FOX_HEADS, 1, t_tok), F32),
                   jax.ShapeDtypeStruct((t_tok, FOX_WIDTH), F32)],
        in_specs=[qs, ks, ks, qs, qs, ks, qs], out_specs=[seq, ks, ks, cks, seq],
        scratch=[pltpu.VMEM((t, 128), F32), pltpu.VMEM((t, 128), F32), pltpu.VMEM((t_tok, 128), F32),
                 pltpu.VMEM((2, t_tok, 128), F32)], comm=comm)


def _ret_consts():
    c = RET_CHUNK
    log_gamma = jnp.log1p(-jnp.exp2(-5.0 - jnp.arange(RET_HEADS, dtype=F32)))
    idx = jnp.arange(c, dtype=F32)
    diff = idx[:, None] - idx[None, :]
    dmask = jnp.where(diff >= 0, jnp.exp(log_gamma[:, None, None] * jnp.maximum(diff, 0.0)), 0.0)
    qdec = jnp.exp(log_gamma[:, None] * (idx + 1.0))
    kdec = jnp.exp(log_gamma[:, None] * (c - 1 - idx))
    cdec = jnp.exp(log_gamma * c)
    bc = lambda v: jnp.broadcast_to(v[:, :, None], (RET_HEADS, c, RET_DIM))
    return dmask, bc(qdec), bc(kdec), jnp.broadcast_to(cdec[:, None, None], (RET_HEADS, c, RET_DIM))


def _group_norm(y):
    mu = jnp.mean(y, axis=-1, keepdims=True)
    yc = y - mu
    r = lax.rsqrt(jnp.mean(yc * yc, axis=-1, keepdims=True) + EPS)
    return yc * r, r


def _ret_fwd(rq, rk, rv, rg, consts, comm=None):
    t_tok = rq.shape[0]
    nb = 4 if t_tok % (4 * RET_CHUNK) == 0 else 1
    tr = nb * RET_CHUNK
    n_steps = t_tok // tr
    c = RET_CHUNK

    def body(q_ref, k_ref, v_ref, g_ref, dm_ref, qd_ref, kd_ref, cd_ref, y_ref, yo_ref, st_ref, s_s):
        @pl.when(pl.program_id(0) == 0)
        def _():
            s_s[...] = jnp.zeros_like(s_s)

        for b in range(nb):
            rows = slice(b * c, (b + 1) * c)
            for hh in range(RET_HEADS):
                cols = slice(hh * RET_DIM, (hh + 1) * RET_DIM)
                q, k, v = q_ref[rows, cols], k_ref[rows, cols], v_ref[rows, cols]
                state = s_s[hh]
                st_ref[hh, b] = state
                sc = (_dot_nt(q, k) * dm_ref[hh]).astype(BF)
                y = _dot(sc, v) + _dot((q.astype(F32) * qd_ref[hh]).astype(BF), state.astype(BF))
                s_s[hh] = cd_ref[hh] * state + _dot_tn((k.astype(F32) * kd_ref[hh]).astype(BF), v)
                y_ref[rows, cols] = y
                yn, _ = _group_norm(y)
                gate = g_ref[rows, cols].astype(F32)
                yo_ref[rows, cols] = (yn * (gate * _sigmoid(gate))).astype(BF)

    blk = pl.BlockSpec((tr, RET_WIDTH), lambda i: (i, 0))
    cst = pl.BlockSpec((RET_HEADS, c, RET_DIM), lambda i: (0, 0, 0))
    return _pcall(
        body, [rq, rk, rv, rg, *consts], name="ret_fwd", grid=(n_steps,),
        out_shape=[jax.ShapeDtypeStruct((t_tok, RET_WIDTH), F32), jax.ShapeDtypeStruct((t_tok, RET_WIDTH), BF),
                   jax.ShapeDtypeStruct((RET_HEADS, t_tok // c, RET_DIM, RET_DIM), F32)],
        in_specs=[blk] * 4 + [cst] * 4,
        out_specs=[blk, blk, pl.BlockSpec((RET_HEADS, nb, RET_DIM, RET_DIM), lambda i: (0, i, 0, 0))],
        scratch=[pltpu.VMEM((RET_HEADS, RET_DIM, RET_DIM), F32)], comm=comm)


def _ret_bwd(rq, rk, rv, rg, y_raw, dyo, states, consts, cos_t, sin_t, comm=None):
    t_tok = rq.shape[0]
    nb = 4 if t_tok % (4 * RET_CHUNK) == 0 else 1
    tr = nb * RET_CHUNK
    n_steps = t_tok // tr
    c = RET_CHUNK

    def body(q_ref, k_ref, v_ref, g_ref, y_ref, dyo_ref, st_ref, dm_ref, qd_ref, kd_ref, cd_ref,
             cos_ref, sin_ref, dq_ref, dk_ref, dv_ref, dg_ref, ds_s):
        @pl.when(pl.program_id(0) == 0)
        def _():
            ds_s[...] = jnp.zeros_like(ds_s)

        for b in reversed(range(nb)):
            rows = slice(b * c, (b + 1) * c)
            cosv, sinv = cos_ref[rows, :], sin_ref[rows, :]
            for hh in range(RET_HEADS):
                cols = slice(hh * RET_DIM, (hh + 1) * RET_DIM)
                dm, qd, kd, cd = dm_ref[hh], qd_ref[hh], kd_ref[hh], cd_ref[hh]
                q, k, v = q_ref[rows, cols], k_ref[rows, cols], v_ref[rows, cols]
                yn, r = _group_norm(y_ref[rows, cols])
                gate = g_ref[rows, cols].astype(F32)
                sg = _sigmoid(gate)
                dyo = dyo_ref[rows, cols]
                dg_ref[rows, cols] = (dyo * yn * (sg * (1.0 + gate * (1.0 - sg)))).astype(BF)
                dyn = dyo * (gate * sg)
                dy = r * (dyn - jnp.mean(dyn, axis=-1, keepdims=True)
                          - yn * jnp.mean(dyn * yn, axis=-1, keepdims=True))
                dyb = dy.astype(BF)
                state_b = st_ref[hh, b].astype(BF)
                dstate = ds_s[hh]
                dstate_b = dstate.astype(BF)
                qdb = (q.astype(F32) * qd).astype(BF)
                kdb = (k.astype(F32) * kd).astype(BF)
                sc = (_dot_nt(q, k) * dm).astype(BF)
                dv = _dot_tn(sc, dyb) + _dot(kdb, dstate_b)
                dp = (_dot_nt(dyb, v) * dm).astype(BF)
                dq = _dot(dp, k) + _dot_nt(dyb, state_b) * qd
                dk = (_dot_tn(dp, q) + _dot_nt(v, dstate_b) * kd) * RET_SCALE
                ds_s[hh] = cd * dstate + _dot_tn(qdb, dyb)
                dv_ref[rows, cols] = dv.astype(BF)
                dq_ref[rows, cols] = (dq * cosv - _swap_pairs(dq) * sinv).astype(BF)
                dk_ref[rows, cols] = (dk * cosv - _swap_pairs(dk) * sinv).astype(BF)

    rev = lambda i: n_steps - 1 - i
    blk = pl.BlockSpec((tr, RET_WIDTH), lambda i: (rev(i), 0))
    tab = pl.BlockSpec((tr, RET_DIM), lambda i: (rev(i), 0))
    cst = pl.BlockSpec((RET_HEADS, c, RET_DIM), lambda i: (0, 0, 0))
    sds = jax.ShapeDtypeStruct((t_tok, RET_WIDTH), BF)
    return _pcall(
        body, [rq, rk, rv, rg, y_raw, dyo, states, *consts, cos_t, sin_t], name="ret_bwd",
        grid=(n_steps,), out_shape=[sds] * 4,
        in_specs=[blk] * 6 + [pl.BlockSpec((RET_HEADS, nb, RET_DIM, RET_DIM), lambda i: (0, rev(i), 0, 0))]
        + [cst] * 4 + [tab, tab],
        out_specs=[blk] * 4, scratch=[pltpu.VMEM((RET_HEADS, RET_DIM, RET_DIM), F32)], comm=comm)


def _mix_out(h, y_ret, y_fox, ga, gb, wr4, wf4, wo4, comm=None):
    t_tok, d = h.shape
    cz = wr4.shape[-1]
    ro = wo4.shape[-2]
    tm = _tile(t_tok, 512)

    def body(h_ref, yr_ref, yf_ref, ga_ref, gb_ref, wr_ref, wf_ref, wo_ref, ho_ref, za_ref, zb_ref, mix_ref):
        yr, yf = yr_ref[...], yf_ref[...]
        for j in range(N_CHIPS):
            sl = slice(j * cz, (j + 1) * cz)
            za = _dot(yr, wr_ref[j])
            zb = _dot(yf, wf_ref[j])
            za_ref[:, sl] = za.astype(BF)
            zb_ref[:, sl] = zb.astype(BF)
            mix_ref[:, sl] = (ga_ref[:, sl].astype(F32) * za + gb_ref[:, sl].astype(F32) * zb).astype(BF)
        acc = h_ref[...]
        for j in range(N_CHIPS):
            acc = acc + _dot(mix_ref[:, j * ro:(j + 1) * ro], wo_ref[j])
        ho_ref[...] = acc

    row = lambda c: pl.BlockSpec((tm, c), lambda i: (i, 0))
    full = lambda *s: pl.BlockSpec(s, lambda i: (0,) * len(s))
    sds = lambda dt: jax.ShapeDtypeStruct((t_tok, d), dt)
    return _pcall(
        body, [h, y_ret, y_fox, ga, gb, wr4, wf4, wo4], name="mix_out", grid=(t_tok // tm,),
        out_shape=[sds(F32), sds(BF), sds(BF), sds(BF)],
        in_specs=[row(d), row(RET_WIDTH), row(FOX_WIDTH), row(d), row(d),
                  full(N_CHIPS, RET_WIDTH, cz), full(N_CHIPS, FOX_WIDTH, cz), full(N_CHIPS, ro, d)],
        out_specs=[row(d)] * 4, comm=comm)


def _mix_out_bwd(dh, za, zb, ga, gb, y_fox, wr4, wf4, wo4, comm=None):
    t_tok, d = dh.shape
    cz = wr4.shape[-1]
    ro = wo4.shape[-2]
    tm = _tile(t_tok, 256)

    def body(dh_ref, za_ref, zb_ref, ga_ref, gb_ref, yf_ref, wr_ref, wf_ref, wo_ref,
             dhb_ref, dgp_ref, dza_ref, dzb_ref, dyr_ref, dyf_ref, dl_ref, db_ref):
        @pl.when(pl.program_id(0) == 0)
        def _():
            db_ref[...] = jnp.zeros_like(db_ref)

        dhb = dh_ref[...].astype(BF)
        dhb_ref[...] = dhb
        dyr = jnp.zeros((tm, RET_WIDTH), F32)
        dyf = jnp.zeros((tm, FOX_WIDTH), F32)
        for j in range(N_CHIPS):
            sl = slice(j * ro, (j + 1) * ro)
            dmix = _dot_nt(dhb, wo_ref[j])
            ga, gb = ga_ref[:, sl].astype(F32), gb_ref[:, sl].astype(F32)
            dza = (dmix * ga).astype(BF)
            dzb = (dmix * gb).astype(BF)
            dza_ref[:, sl] = dza
            dzb_ref[:, sl] = dzb
            dga = dmix * za_ref[:, sl].astype(F32) * ga * (1.0 - ga)
            dgb = dmix * zb_ref[:, sl].astype(F32) * gb * (1.0 - gb)
            dgp_ref[:, sl] = dga.astype(BF)
            dgp_ref[:, d + j * ro:d + (j + 1) * ro] = dgb.astype(BF)
            db_ref[:, sl] += jnp.sum(dga, axis=0, keepdims=True)
            db_ref[:, d + j * ro:d + (j + 1) * ro] += jnp.sum(dgb, axis=0, keepdims=True)
        for j in range(N_CHIPS):
            sl = slice(j * cz, (j + 1) * cz)
            dyr = dyr + _dot_nt(dza_ref[:, sl], wr_ref[j])
            dyf = dyf + _dot_nt(dzb_ref[:, sl], wf_ref[j])
        dyr_ref[...] = dyr
        dyfb = dyf.astype(BF)
        dyf_ref[...] = dyfb
        prod = dyfb.astype(F32) * yf_ref[...]
        first = _first_half()
        for pp in range(FOX_HEADS // 2):
            blk = prod[:, pp * 128:(pp + 1) * 128]
            s0 = jnp.sum(jnp.where(first, blk, 0.0), axis=1, keepdims=True)
            s1 = jnp.sum(jnp.where(first, 0.0, blk), axis=1, keepdims=True)
            parts = _split3(-jnp.where(first, s1, s0))
            dl_ref[:, pp * 128:(pp + 1) * 128] = _aug_put(jnp.zeros((tm, 128), BF), 0, parts)

    row = lambda c: pl.BlockSpec((tm, c), lambda i: (i, 0))
    full = lambda *s: pl.BlockSpec(s, lambda i: (0,) * len(s))
    sds = lambda c, dt: jax.ShapeDtypeStruct((t_tok, c), dt)
    return _pcall(
        body, [dh, za, zb, ga, gb, y_fox, wr4, wf4, wo4], name="mix_out_bwd", grid=(t_tok // tm,),
        out_shape=[sds(d, BF), sds(2 * d, BF), sds(d, BF), sds(d, BF), sds(RET_WIDTH, F32),
                   sds(FOX_WIDTH, BF), sds(FOX_WIDTH, BF), jax.ShapeDtypeStruct((1, 2 * d), F32)],
        in_specs=[row(d)] * 5 + [row(FOX_WIDTH), full(N_CHIPS, RET_WIDTH, cz), full(N_CHIPS, FOX_WIDTH, cz),
                                 full(N_CHIPS, ro, d)],
        out_specs=[row(d), row(2 * d), row(d), row(d), row(RET_WIDTH), row(FOX_WIDTH), row(FOX_WIDTH),
                   full(1, 2 * d)],
        comm=comm)


def _mix_in_bwd(dh, h, ln, parts, dff, dgpre, w_in, wm4, comm=None):
    t_tok, d = h.shape
    cm = wm4.shape[-1]
    tm = _tile(t_tok, 256)

    def body(dh_ref, h_ref, ln_ref, p0, p1, p2, p3, p4, p5, p6, dff_ref, dgp_ref, win_ref, wm_ref,
             dhi_ref, dln_ref, dproj_ref):
        @pl.when(pl.program_id(0) == 0)
        def _():
            dln_ref[...] = jnp.zeros_like(dln_ref)

        for k, pr in enumerate((p0, p1, p2, p3, p4, p5, p6)):
            dproj_ref[:, k * 512:(k + 1) * 512] = pr[...]
        dproj_ref[:, FF_COL:FF_COL + 128] = dff_ref[...]
        dproj_ref[:, FF_COL + 128:] = jnp.zeros((tm, IN_PAD - FF_COL - 128), BF)
        du = _dot(dproj_ref[...], win_ref[...])
        for j in range(N_CHIPS):
            du = du + _dot_nt(dgp_ref[:, j * cm:(j + 1) * cm], wm_ref[j])
        xv = h_ref[...]
        dx, dln = _rms_bwd(du, xv, _rstd(xv), ln_ref[...])
        dln_ref[...] += dln
        dhi_ref[...] = dh_ref[...] + dx

    row = lambda c: pl.BlockSpec((tm, c), lambda i: (i, 0))
    full = lambda *s: pl.BlockSpec(s, lambda i: (0,) * len(s))
    return _pcall(
        body, [dh, h, ln, *parts, dff, dgpre, w_in, wm4], name="mix_in_bwd", grid=(t_tok // tm,),
        out_shape=[jax.ShapeDtypeStruct((t_tok, d), F32), jax.ShapeDtypeStruct((1, d), F32),
                   jax.ShapeDtypeStruct((t_tok, IN_PAD), BF)],
        in_specs=[row(d), row(d), full(1, d)] + [row(512)] * 7 + [row(128), row(2 * d), full(IN_PAD, d),
                                                                   full(N_CHIPS, d, cm)],
        out_specs=[row(d), full(1, d), row(IN_PAD)], comm=comm)


def _tail(h, p, target, ln_ple, ln_fin, wpg4, wpl4, comm=None):
    t_tok, d = h.shape
    pd = p.shape[1]
    rg = wpg4.shape[-2]
    cp = wpl4.shape[-1]
    tm = _tile(t_tok, 256)

    def body(h_ref, p_ref, t_ref, lp_ref, lf_ref, wg_ref, wp_ref,
             dh_ref, n_ref, dgp_ref, dpe_ref, pb_ref, loss_ref, dlf_ref, dlp_ref, pe_s, dn_s):
        @pl.when(pl.program_id(0) == 0)
        def _():
            loss_ref[...] = jnp.zeros_like(loss_ref)
            dlf_ref[...] = jnp.zeros_like(dlf_ref)
            dlp_ref[...] = jnp.zeros_like(dlp_ref)

        xv = h_ref[...]
        r3 = _rstd(xv)
        nb = (xv * r3 * lp_ref[...]).astype(BF)
        n_ref[...] = nb
        pb = p_ref[...].astype(BF)
        pb_ref[...] = pb
        pgpre = jnp.zeros((tm, d), F32)
        for j in range(N_CHIPS):
            pgpre = pgpre + _dot(nb[:, j * rg:(j + 1) * rg], wg_ref[j])
            pe_s[:, j * cp:(j + 1) * cp] = _dot(pb, wp_ref[j])
        pg = _sigmoid(pgpre)
        pe = pe_s[...]
        h4 = xv + pg * pe
        r4 = _rstd(h4)
        err = h4 * r4 * lf_ref[...] - t_ref[...]
        loss_ref[...] += 0.5 * jnp.sum(jnp.sum(err * err, axis=1, keepdims=True), axis=0, keepdims=True) / d
        dh4, dlf = _rms_bwd(err * (1.0 / d), h4, r4, lf_ref[...])
        dlf_ref[...] += dlf
        dpe_ref[...] = (dh4 * pg).astype(BF)
        dgp = (dh4 * pe * pg * (1.0 - pg)).astype(BF)
        dgp_ref[...] = dgp
        for j in range(N_CHIPS):
            dn_s[:, j * rg:(j + 1) * rg] = _dot_nt(dgp, wg_ref[j])
        dx, dlp = _rms_bwd(dn_s[...], xv, r3, lp_ref[...])
        dlp_ref[...] += dlp
        dh_ref[...] = dh4 + dx

    row = lambda c: pl.BlockSpec((tm, c), lambda i: (i, 0))
    full = lambda *s: pl.BlockSpec(s, lambda i: (0,) * len(s))
    sds = lambda c, dt: jax.ShapeDtypeStruct((t_tok, c), dt)
    vec = jax.ShapeDtypeStruct((1, d), F32)
    return _pcall(
        body, [h, p, target, ln_ple, ln_fin, wpg4, wpl4], name="tail", grid=(t_tok // tm,),
        out_shape=[sds(d, F32), sds(d, BF), sds(d, BF), sds(d, BF), sds(pd, BF),
                   jax.ShapeDtypeStruct((1, 128), F32), vec, vec],
        in_specs=[row(d), row(pd), row(d), full(1, d), full(1, d), full(N_CHIPS, rg, d), full(N_CHIPS, pd, cp)],
        out_specs=[row(d), row(d), row(d), row(d), row(pd), full(1, 128), full(1, d), full(1, d)],
        scratch=[pltpu.VMEM((tm, d), F32), pltpu.VMEM((tm, d), F32)], comm=comm)


BIG = ["w_ffn1_gate", "w_ffn1_up", "w_ffn1_down", "w_in", "w_merge", "w_ret_out", "w_fox_out", "w_out",
       "w_ffn2_gate", "w_ffn2_up", "w_ffn2_down", "w_ple", "w_ple_gate"]
SMALL = ["ln_ffn1", "ln_mix", "b_forget", "b_merge", "ln_ffn2", "ln_ple", "ln_final"]
WEIGHTS = ["ln_ffn1", "w_ffn1_gate", "w_ffn1_up", "w_ffn1_down", "ln_mix", "w_in", "b_forget", "w_merge", "b_merge",
           "w_ret_out", "w_fox_out", "w_out", "ln_ffn2", "w_ffn2_gate", "w_ffn2_up", "w_ffn2_down", "ln_ple",
           "w_ple", "w_ple_gate", "ln_final"]


TRANSPOSED = {"w_ffn1_gate", "w_ffn1_up", "w_ffn2_gate", "w_ffn2_up", "w_in"}
IN_ROWS_PAD = -(-(IN_COLS // N_CHIPS) // 32) * 32


def _pack_small(vals, loss_row):
    rows = [loss_row]
    for name in SMALL:
        v = vals[name].reshape(-1)
        n = -(-v.shape[0] // 128) * 128
        rows.append(jnp.pad(v, (0, n - v.shape[0])).reshape(n // 128, 128))
    packed = jnp.concatenate(rows, axis=0)
    pad = -packed.shape[0] % 8
    return jnp.pad(packed, ((0, pad), (0, 0)))


def _unpack_small(packed, sizes):
    out, r = {}, 1
    for name in SMALL:
        n = sizes[name]
        nr = -(-n // 128)
        out[name] = packed[r:r + nr].reshape(1, nr * 128)[:, :n]
        r += nr
    return out


class _Stage:
    def __init__(self, comm, finish):
        self.comm, self.finish, self.result = comm, finish, None


def _hosted(fn, *a, stages=()):
    if not stages:
        return fn(*a)
    outs, couts = fn(*a, comm=_merge([st.comm for st in stages]))
    for st, o in zip(stages, _split_outs([st.comm for st in stages], couts)):
        st.result = st.finish(o)
    return outs


class _Reducer:
    def __init__(self):
        self.done = {}

    def swap(self, grads):
        names = list(grads)
        return _Stage(_c_half_swap([grads[n] for n in names]),
                      lambda outs: dict(zip(names, _add_halves([(grads[n], o) for n, o in zip(names, outs)]))))

    def exchange(self, parts):
        names = list(parts)
        return _Stage(_c_chip_exchange([parts[n] for n in names]),
                      lambda outs: dict(zip(names, _sum_chips([(parts[n], o) for n, o in zip(names, outs)]))))

    def join(self, halves):
        names = list(halves)
        return _Stage(_c_join([halves[n] for n in names]),
                      lambda outs: self.done.update({n: (halves[n], o) for n, o in zip(names, outs)}))


def kernel(x, p, positions, ln_ffn1, w_ffn1_gate, w_ffn1_up, w_ffn1_down, ln_mix, w_in, b_forget, w_merge, b_merge, w_ret_out, w_fox_out, w_out, ln_ffn2, w_ffn2_gate, w_ffn2_up, w_ffn2_down, ln_ple, w_ple, w_ple_gate, ln_final, loss_target, m_ln_ffn1, m_w_ffn1_gate, m_w_ffn1_up, m_w_ffn1_down, m_ln_mix, m_w_in, m_b_forget, m_w_merge, m_b_merge, m_w_ret_out, m_w_fox_out, m_w_out, m_ln_ffn2, m_w_ffn2_gate, m_w_ffn2_up, m_w_ffn2_down, m_ln_ple, m_w_ple, m_w_ple_gate, m_ln_final, v_ln_ffn1, v_w_ffn1_gate, v_w_ffn1_up, v_w_ffn1_down, v_ln_mix, v_w_in, v_b_forget, v_w_merge, v_b_merge, v_w_ret_out, v_w_fox_out, v_w_out, v_ln_ffn2, v_w_ffn2_gate, v_w_ffn2_up, v_w_ffn2_down, v_ln_ple, v_w_ple, v_w_ple_gate, v_ln_final):
    args = dict(locals())
    w = {n: args[n] for n in WEIGHTS}
    m = {n: args["m_" + n] for n in WEIGHTS}
    v = {n: args["v_" + n] for n in WEIGHTS}
    d = x.shape[-1]
    t_tok = x.shape[1]
    xs, ps, target = x[0], p[0, 0], loss_target[0]
    small = {n: w[n].reshape(1, -1) for n in SMALL}

    def to2d(n, a):
        if n in TRANSPOSED:
            return a[0].T
        return a.reshape(a.shape[-2], a.shape[-1]) if a.ndim == 3 else a.reshape(1, -1)

    def from2d(n, a):
        return a.T[None] if n in TRANSPOSED else a.reshape(w[n].shape)

    def padded(n, a):
        return jnp.pad(a, ((0, IN_ROWS_PAD - a.shape[0]), (0, 0))) if n == "w_in" else a

    core = lax.axis_index("c")
    me = 2 * lax.axis_index("x") + lax.axis_index("y")
    shard = {}
    for n in BIG:
        s2 = padded(n, to2d(n, w[n]).astype(BF))
        shard[n] = s2.reshape(1, 2, s2.shape[0] // 2, s2.shape[1])
    full = {}

    def gather(names):
        bufs = [lax.dynamic_update_slice(jnp.zeros((N_CHIPS,) + shard[n].shape[1:], BF), shard[n], (me, 0, 0, 0))
                for n in names]

        def finish(outs):
            full.update({n: o.reshape(N_CHIPS, 2 * o.shape[2], o.shape[3]) for n, o in zip(names, outs)})

        return _Stage(_c_all_gather(bufs), finish)

    half = RET_DIM // 2
    inv_freq = 1.0 / (ROPE_BASE ** (jnp.arange(half, dtype=F32) / half))
    cos_t, sin_t = _hosted(_rope_tables, positions[0].astype(F32).reshape(t_tok, 1),
                           jnp.repeat(inv_freq, 2).reshape(1, RET_DIM),
                           stages=[gather(["w_ffn1_gate", "w_ffn1_up", "w_ffn1_down"])])
    consts = _ret_consts()
    b_pad = jnp.pad(small["b_forget"], ((0, 0), (0, 128 - FOX_HEADS)))

    h1, n1, g1, u1 = _hosted(
        _ffn_fwd, xs, small["ln_ffn1"], full["w_ffn1_gate"], full["w_ffn1_up"], full["w_ffn1_down"],
        stages=[gather(["w_in", "w_merge", "w_ret_out", "w_fox_out", "w_out", "w_ple_gate", "w_ple"])])
    w_in_full = jnp.pad(full["w_in"][:, :IN_COLS // N_CHIPS].reshape(IN_COLS, d), ((0, IN_PAD - IN_COLS), (0, 0)))
    u, rq, rk, rv, rg, fq, fk, fv, ffl, ga, gb = _mix_in(
        h1, small["ln_mix"], w_in_full, full["w_merge"], small["b_merge"], cos_t, sin_t)
    aq, ak = _forget_fwd(ffl, b_pad)
    y_raw, y_ret, states = _ret_fwd(rq, rk, rv, rg, consts)
    y_fox, y_fox32, aqb = _hosted(_fox_fwd, fq, fk, fv, aq, ak,
                                  stages=[gather(["w_ffn2_gate", "w_ffn2_up", "w_ffn2_down"])])
    h2, za, zb, mix = _mix_out(h1, y_ret, y_fox, ga, gb, full["w_ret_out"], full["w_fox_out"], full["w_out"])
    h3, n2, g2, u2 = _ffn_fwd(h2, small["ln_ffn2"], full["w_ffn2_gate"], full["w_ffn2_up"], full["w_ffn2_down"])

    red = _Reducer()
    dh3, n3, dpgpre, dpe, pb, loss, dln_final, dln_ple = _tail(
        h3, ps, target, small["ln_ple"], small["ln_final"], full["w_ple_gate"], full["w_ple"])
    g_f2 = dict(w_ple_gate=_wgrad_rows("wgrad_ple_gate", n3, dpgpre, N_CHIPS),
                w_ple=_wgrad_cols("wgrad_ple", pb, dpe, N_CHIPS))
    dh2, dln_ffn2, dg2, du2, a2, dhb3 = _ffn_bwd(
        dh3, h2, small["ln_ffn2"], g2, u2, full["w_ffn2_gate"], full["w_ffn2_up"], full["w_ffn2_down"])
    g_f2["w_ffn2_gate"] = _wgrad_b_shared("wgrad_ffn2_gate", dg2, n2)
    g_f2["w_ffn2_up"] = _wgrad_b_shared("wgrad_ffn2_up", du2, n2)
    g_f2["w_ffn2_down"] = _wgrad_b_shared("wgrad_ffn2_down", a2, dhb3)

    sw_f2 = red.swap(g_f2)
    dhb2, dgpre, dza, dzb, dy_ret, dy_fox, ad, db_merge = _hosted(
        _mix_out_bwd, dh2, za, zb, ga, gb, y_fox32, full["w_ret_out"], full["w_fox_out"], full["w_out"],
        stages=[sw_f2])
    g_br = dict(w_out=_wgrad_rows("wgrad_out", mix, dhb2, N_CHIPS),
                w_ret_out=_wgrad_cols("wgrad_ret_out", y_ret, dza, N_CHIPS),
                w_fox_out=_wgrad_cols("wgrad_fox_out", y_fox, dzb, N_CHIPS))

    sw_br = red.swap(g_br)
    drq, drk, drv, drg = _hosted(_ret_bwd, rq, rk, rv, rg, y_raw, dy_ret, states, consts, cos_t, sin_t,
                                 stages=[sw_br])
    ex_f2, ex_br = red.exchange(sw_f2.result), red.exchange(sw_br.result)
    dfq, dfk, dfv, dcum_t3, dcum_q = _hosted(_fox_bwd, fq, fk, fv, dy_fox, aqb, ak, ad, stages=[ex_f2, ex_br])
    dff, db_forget = _forget_bwd(dcum_t3.reshape(FOX_HEADS, t_tok), dcum_q, ffl, b_pad)
    dh1, dln_mix, dproj = _hosted(
        _mix_in_bwd, dh2, h1, small["ln_mix"], (drq, drk, drv, drg, dfq, dfk, dfv), dff, dgpre, w_in_full,
        full["w_merge"], stages=[red.join(ex_f2.result), red.join(ex_br.result)])

    results = {}
    for names in (["w_ffn2_gate", "w_ffn2_up", "w_ffn2_down"], ["w_out", "w_ple_gate"], ["w_ret_out", "w_fox_out"],
                  ["w_ple"]):
        res = _sc_adamw_halves([(to2d(n, w[n]), *red.done[n], to2d(n, m[n]), to2d(n, v[n])) for n in names])
        for q, n in enumerate(names):
            results[n] = tuple(from2d(n, a) for a in res[4 * q:4 * q + 4])

    dx, dln_ffn1, dg1, du1, a1, dhb1 = _ffn_bwd(
        dh1, xs, small["ln_ffn1"], g1, u1, full["w_ffn1_gate"], full["w_ffn1_up"], full["w_ffn1_down"])
    g_f1g = _wgrad_b_shared("wgrad_ffn1_gate", dg1, n1)
    sw_f1g = red.swap(dict(w_ffn1_gate=g_f1g))
    g_f1u = _hosted(_wgrad_b_shared, "wgrad_ffn1_up", du1, n1, stages=[sw_f1g])
    ex_f1g, sw_f1u = red.exchange(sw_f1g.result), red.swap(dict(w_ffn1_up=g_f1u))
    g_f1d = _hosted(_wgrad_b_shared, "wgrad_ffn1_down", a1, dhb1, stages=[ex_f1g, sw_f1u])

    ex_f1u, sw_f1d = red.exchange(sw_f1u.result), red.swap(dict(w_ffn1_down=g_f1d))
    g_in = _hosted(_wgrad_rows, "wgrad_in", dproj, u, IN_PAD // 512,
                   stages=[ex_f1u, sw_f1d, red.join(ex_f1g.result)])
    g_in = g_in.reshape(IN_PAD, d)[:IN_COLS].reshape(N_CHIPS, IN_COLS // N_CHIPS, d)
    g_in = jnp.pad(g_in, ((0, 0), (0, IN_ROWS_PAD - IN_COLS // N_CHIPS), (0, 0)))
    ex_f1d, sw_in = red.exchange(sw_f1d.result), red.swap(dict(w_in=g_in))
    g_mrg = _hosted(_wgrad_cols, "wgrad_merge", u, dgpre, N_CHIPS,
                    stages=[ex_f1d, sw_in, red.join(ex_f1u.result)])

    small_grads = dict(ln_ffn1=dln_ffn1, ln_mix=dln_mix, b_forget=db_forget[:, :FOX_HEADS], b_merge=db_merge,
                       ln_ffn2=dln_ffn2, ln_ple=dln_ple, ln_final=dln_final)
    sizes = {n: w[n].size for n in SMALL}
    ex_in, sw_mrg = red.exchange(sw_in.result), red.swap(dict(w_merge=g_mrg))
    reduced = _hosted(_all_reduce_small, _pack_small(small_grads, loss),
                      stages=[ex_in, sw_mrg, red.join(ex_f1d.result)])
    gsum = _unpack_small(reduced, sizes)
    loss = reduced[0, 0]
    ex_mrg = red.exchange(sw_mrg.result)
    _hosted(_exchange_only, stages=[ex_mrg, red.join(ex_in.result)])
    _hosted(_exchange_only, stages=[red.join(ex_mrg.result)])

    def update(names):
        w2, m2, v2 = ([to2d(n, a[n]) for n in names] for a in (w, m, v))
        n = names[0]
        if n == "w_in":
            mine, other = red.done[n]
            g2 = jnp.where(core == 0, jnp.concatenate([mine, other]), jnp.concatenate([other, mine]))
            g2 = g2[:w2[0].shape[0]]
            rows3 = lambda a: jnp.transpose(a, (2, 0, 1))
            g3 = g2.reshape(g2.shape[0], 1, g2.shape[1])
            res = [g3] + _adamw(rows3(w[n]), g3, rows3(m[n]), rows3(v[n]))
            results[n] = tuple(jnp.transpose(a, (1, 2, 0)) for a in res)
            return
        res = _adamw_halves([(w2[q], *red.done[names[q]], m2[q], v2[q]) for q in range(len(names))])
        for q, name in enumerate(names):
            results[name] = tuple(from2d(name, a) for a in res[4 * q:4 * q + 4])

    res = _adamw_vectors([(to2d(n, w[n]), gsum[n], to2d(n, m[n]), to2d(n, v[n])) for n in SMALL])
    for q, n in enumerate(SMALL):
        results[n] = tuple(from2d(n, a) for a in [gsum[n]] + res[3 * q:3 * q + 3])
    update(["w_ffn1_gate", "w_ffn1_up", "w_ffn1_down"])
    for n in WEIGHTS:
        if n not in results:
            update([n])

    outs = [[results[n][k] for n in WEIGHTS] for k in range(4)]
    return (loss, dx[None], *outs[0], *outs[1], *outs[2], *outs[3])
```

```python
import functools
import operator

import jax
import jax.numpy as jnp
from jax import lax
from jax.experimental import pallas as pl
from jax.experimental.pallas import tpu as pltpu
from jax.experimental.pallas import tpu_sc as plsc

F32 = jnp.float32
BF = jnp.bfloat16
MESH = pl.DeviceIdType.MESH

EPS = 1e-6
ROPE_BASE = 10000.0
N_CHIPS = 4
RET_HEADS = 4
RET_DIM = 128
RET_WIDTH = RET_HEADS * RET_DIM
RET_CHUNK = 128
RET_SCALE = RET_DIM ** -0.5
FOX_HEADS = 8
FOX_DIM = 64
FOX_WIDTH = FOX_HEADS * FOX_DIM
FOX_SCALE = FOX_DIM ** -0.5
IN_COLS = 4 * RET_WIDTH + 3 * FOX_WIDTH + FOX_HEADS
IN_PAD = 4096
FF_COL = 4 * RET_WIDTH + 3 * FOX_WIDTH
NEG = -1e30

ADAM_LR = 0.001
ADAM_B1 = 0.9
ADAM_B2 = 0.999
ADAM_EPS = 1e-08
ADAM_WD = 0.01
ADAM_STEP = 10

VMEM_LIMIT = 52 * 1024 * 1024

RELAY_MIN_STEPS = 16

NT = (((1,), (1,)), ((), ()))
TN = (((0,), (0,)), ((), ()))

HBM_SPEC = pl.BlockSpec(memory_space=pltpu.HBM)
VMEM_SPEC = pl.BlockSpec(memory_space=pltpu.VMEM)


def _dot(a, b):
    return jnp.dot(a, b, preferred_element_type=F32)


def _dot_nt(a, b):
    return lax.dot_general(a, b, NT, preferred_element_type=F32)


def _dot_tn(a, b):
    return lax.dot_general(a, b, TN, preferred_element_type=F32)


def _rstd(xv):
    return lax.rsqrt(jnp.mean(xv * xv, axis=-1, keepdims=True) + EPS)


def _rms_bwd(dn, xv, r, ln):
    xh = xv * r
    dxh = dn * ln
    dx = r * (dxh - xh * jnp.mean(dxh * xh, axis=-1, keepdims=True))
    return dx, jnp.sum(dn * xh, axis=0, keepdims=True)


def _sigmoid(x):
    return jax.nn.sigmoid(x)


def _tile(n, pref):
    return pref if n % pref == 0 else n


def _row_tile(n, cap):
    best = [t for t in range(16, min(n, cap) + 1, 16) if n % t == 0]
    return best[-1] if best else n


class _Comm:
    def __init__(self, ins, out_shapes, sems, start, wait, aliases=None, relay=None):
        self.ins, self.out_shapes, self.sems, self.start, self.wait = list(ins), list(out_shapes), list(sems), start, wait
        self.aliases = dict(aliases or {})
        self.relay = relay


def _merge(comms):
    comms = [c for c in comms if c is not None]
    if not comms:
        return None
    bounds, ni, no, ns = [], 0, 0, 0
    for c in comms:
        bounds.append((ni, no, ns))
        ni, no, ns = ni + len(c.ins), no + len(c.out_shapes), ns + len(c.sems)

    def run(which):
        def f(ins, outs, sems, **kw):
            for c, (i, o, s) in zip(comms, bounds):
                fn = getattr(c, which)
                if fn is not None:
                    fn(ins[i:i + len(c.ins)], outs[o:o + len(c.out_shapes)], sems[s:s + len(c.sems)],
                       **(kw if c.relay is not None else {}))
        return f

    aliases = {i + a: o + b for c, (i, o, _) in zip(comms, bounds) for a, b in c.aliases.items()}
    relay = run("relay") if any(c.relay is not None for c in comms) else None
    return _Comm([a for c in comms for a in c.ins], [a for c in comms for a in c.out_shapes],
                 [a for c in comms for a in c.sems], run("start"), run("wait"), aliases, relay)


def _split_outs(comms, outs):
    res, o = [], 0
    for c in comms:
        if c is not None:
            res.append(list(outs[o:o + len(c.out_shapes)]))
            o += len(c.out_shapes)
    return res


def _pcall(body, args, *, name, out_shape, grid=(), in_specs=None, out_specs=None, scratch=(), comm=None,
           prefetch=()):
    many = isinstance(out_shape, (list, tuple))
    outs = list(out_shape) if many else [out_shape]
    n_pre, n_in, n_out, n_scr = len(prefetch), len(args), len(outs), len(scratch)
    if in_specs is None:
        in_specs, out_specs = [VMEM_SPEC] * n_in, [VMEM_SPEC] * n_out
    else:
        in_specs, out_specs = list(in_specs), (list(out_specs) if many else [out_specs])
    params = pltpu.CompilerParams(dimension_semantics=("arbitrary",) * len(grid), vmem_limit_bytes=VMEM_LIMIT)
    scalars = [jnp.reshape(s, (1,)).astype(jnp.int32) for s in prefetch]
    ci, co = (len(comm.ins), len(comm.out_shapes)) if comm is not None else (0, 0)

    def wrapped(*refs):
        pre, refs = refs[:n_pre], refs[n_pre:]
        a, ca = refs[:n_in], refs[n_in:n_in + ci]
        o = refs[n_in + ci:n_in + ci + n_out]
        cout = refs[n_in + ci + n_out:n_in + ci + n_out + co]
        s = refs[n_in + ci + n_out + co:n_in + ci + n_out + co + n_scr]
        csem = refs[n_in + ci + n_out + co + n_scr:]
        if comm is None:
            body(*pre, *a, *o, *s)
        elif grid:
            step = functools.reduce(lambda acc, k: acc * grid[k] + pl.program_id(k), range(len(grid)), 0)
            n_steps = functools.reduce(operator.mul, grid)
            relayed = comm.relay is not None and n_steps >= RELAY_MIN_STEPS
            pl.when(step == 0)(lambda: comm.start(ca, cout, csem))
            if relayed:
                pl.when(step == n_steps - n_steps // 8)(lambda: comm.relay(ca, cout, csem))
            body(*pre, *a, *o, *s)
            pl.when(step == n_steps - 1)(lambda: comm.wait(ca, cout, csem, **({"relayed": True} if relayed else {})))
        else:
            comm.start(ca, cout, csem)
            body(*pre, *a, *o, *s)
            comm.wait(ca, cout, csem)

    c_ins, c_outs, c_sems, aliases = ([], [], [], {}) if comm is None else (
        comm.ins, comm.out_shapes, comm.sems, {n_pre + n_in + i: n_out + o for i, o in comm.aliases.items()})
    all_in, all_out = in_specs + [HBM_SPEC] * ci, out_specs + [HBM_SPEC] * co
    all_scr = list(scratch) + c_sems
    if grid:
        args = [pltpu.with_memory_space_constraint(a, pltpu.HBM) for a in args]
    c_ins = [pltpu.with_memory_space_constraint(a, pltpu.HBM) for a in c_ins]
    if n_pre:
        spec = dict(grid_spec=pltpu.PrefetchScalarGridSpec(
            num_scalar_prefetch=n_pre, grid=grid, in_specs=all_in, out_specs=all_out, scratch_shapes=all_scr))
    else:
        spec = dict(grid=grid, in_specs=all_in, out_specs=all_out, scratch_shapes=all_scr)
    res = pl.pallas_call(wrapped, name=name, out_shape=outs + c_outs, input_output_aliases=aliases,
                         compiler_params=params, **spec)(*scalars, *args, *c_ins)
    mine = list(res[:n_out])
    mine = mine if many else mine[0]
    return mine if comm is None else (mine, list(res[n_out:]))


def _peer_chips(x, y):
    return [(1 - x, y), (x, 1 - y), (1 - x, 1 - y)]


def _c_all_gather(bufs):
    n = len(bufs)

    def copies(ins, outs, sems):
        send_sems, recv_sems, fwd_send, fwd_recv = sems
        x, y, c = lax.axis_index("x"), lax.axis_index("y"), lax.axis_index("c")
        me = 2 * x + y
        peers = _peer_chips(x, y)
        chip = [2 * px + py for px, py in peers]

        def ici(g, j, slot):
            return pltpu.make_async_remote_copy(
                src_ref=outs[g].at[me, c], dst_ref=outs[g].at[slot, c], send_sem=send_sems.at[g, j],
                recv_sem=recv_sems.at[g, j], device_id=(*peers[j], c), device_id_type=MESH)

        def d2d(g, j, half):
            return pltpu.make_async_remote_copy(
                src_ref=outs[g].at[chip[j], half], dst_ref=outs[g].at[chip[j], half], send_sem=fwd_send.at[g, j],
                recv_sem=fwd_recv.at[g, j], device_id=(x, y, 1 - c), device_id_type=MESH)

        pairs = [(g, j) for g in range(n) for j in range(3)]
        sends = [ici(g, j, me) for g, j in pairs]
        recvs = [ici(g, j, chip[j]) for g, j in pairs]
        passes = [d2d(g, j, c) for g, j in pairs]
        passed = [d2d(g, j, 1 - c) for g, j in pairs]
        return sends, recvs, passes, passed

    def start(ins, outs, sems):
        for cp in copies(ins, outs, sems)[0]:
            cp.start()

    def relay(ins, outs, sems):
        _, recvs, passes, _ = copies(ins, outs, sems)
        for rcv, fwd in zip(recvs, passes):
            rcv.wait_recv()
            fwd.start()

    def wait(ins, outs, sems, relayed=False):
        if not relayed:
            relay(ins, outs, sems)
        sends, _, passes, passed = copies(ins, outs, sems)
        for cp in passed:
            cp.wait_recv()
        for cp in sends + passes:
            cp.wait_send()

    pair_sems = pltpu.SemaphoreType.DMA((n, 3))
    return _Comm(bufs, [jax.ShapeDtypeStruct(s.shape, s.dtype) for s in bufs], [pair_sems] * 4, start, wait,
                 aliases={g: g for g in range(n)}, relay=relay)


def _start_wait(copies):
    def start(ins, outs, sems):
        local, sends, _ = copies(ins, outs, sems)
        for cp in local + sends:
            cp.start()

    def wait(ins, outs, sems):
        local, sends, recvs = copies(ins, outs, sems)
        for cp in recvs:
            cp.wait_recv()
        for cp in sends:
            cp.wait_send()
        for cp in local:
            cp.wait()

    return start, wait


def _c_half_swap(grads):
    n = len(grads)

    def copies(ins, outs, sems):
        send_sems, recv_sems = sems
        x, y, c = lax.axis_index("x"), lax.axis_index("y"), lax.axis_index("c")
        sends = []
        for g in range(n):
            half = ins[g].shape[1] // 2
            sends.append(pltpu.make_async_remote_copy(
                src_ref=ins[g].at[:, pl.ds((1 - c) * half, half), :], dst_ref=outs[g],
                send_sem=send_sems.at[g], recv_sem=recv_sems.at[g], device_id=(x, y, 1 - c), device_id_type=MESH))
        return [], sends, sends

    return _Comm(
        grads, [jax.ShapeDtypeStruct((N_CHIPS, s.shape[1] // 2, s.shape[2]), s.dtype) for s in grads],
        [pltpu.SemaphoreType.DMA((n,)), pltpu.SemaphoreType.DMA((n,))], *_start_wait(copies))


def _c_chip_exchange(parts):
    n = len(parts)

    def copies(ins, outs, sems):
        send_sems, recv_sems = sems
        x, y, c = lax.axis_index("x"), lax.axis_index("y"), lax.axis_index("c")
        peers = _peer_chips(x, y)

        def remote(g, j):
            return pltpu.make_async_remote_copy(
                src_ref=ins[g].at[2 * peers[j][0] + peers[j][1]], dst_ref=outs[g].at[j],
                send_sem=send_sems.at[g, j], recv_sem=recv_sems.at[g, j], device_id=(*peers[j], c),
                device_id_type=MESH)

        sends = [remote(g, j) for g in range(n) for j in range(3)]
        return [], sends, sends

    return _Comm(
        parts, [jax.ShapeDtypeStruct((3,) + s.shape[1:], s.dtype) for s in parts],
        [pltpu.SemaphoreType.DMA((n, 3)), pltpu.SemaphoreType.DMA((n, 3))], *_start_wait(copies))


def _c_join(halves):
    n = len(halves)

    def copies(ins, outs, sems):
        send_sems, recv_sems = sems
        x, y, c = lax.axis_index("x"), lax.axis_index("y"), lax.axis_index("c")
        sends = [pltpu.make_async_remote_copy(
            src_ref=ins[g], dst_ref=outs[g], send_sem=send_sems.at[g], recv_sem=recv_sems.at[g],
            device_id=(x, y, 1 - c), device_id_type=MESH) for g in range(n)]
        return [], sends, sends

    return _Comm(
        halves, [jax.ShapeDtypeStruct(s.shape, s.dtype) for s in halves],
        [pltpu.SemaphoreType.DMA((n,)), pltpu.SemaphoreType.DMA((n,))], *_start_wait(copies))


def _exchange_only(comm=None):
    def body(o_ref):
        o_ref[...] = jnp.zeros_like(o_ref)

    return _pcall(body, [], name="exchange_only", out_shape=jax.ShapeDtypeStruct((8, 128), F32), comm=comm)


def _all_reduce_small(v, comm=None):
    rows = v.shape[0]

    def body(v_ref, out_ref, buf, send_sems, recv_sems):
        x, y, c = lax.axis_index("x"), lax.axis_index("y"), lax.axis_index("c")
        me = 4 * x + 2 * y + c
        buf[me] = v_ref[...]
        flips = [(fx, fy, fc) for fx in (0, 1) for fy in (0, 1) for fc in (0, 1)][1:]

        def peer(k):
            fx, fy, fc = flips[k]
            px, py, pc = x ^ fx, y ^ fy, c ^ fc
            return (px, py, pc), 4 * px + 2 * py + pc

        def copy(k, slot):
            return pltpu.make_async_remote_copy(
                src_ref=buf.at[slot], dst_ref=buf.at[slot], send_sem=send_sems.at[k],
                recv_sem=recv_sems.at[k], device_id=peer(k)[0], device_id_type=MESH)

        sends = [copy(k, me) for k in range(7)]
        for cp in sends:
            cp.start()
        for k in range(7):
            copy(k, peer(k)[1]).wait_recv()
        for cp in sends:
            cp.wait_send()
        acc = buf[0]
        for d in range(1, 8):
            acc = acc + buf[d]
        out_ref[...] = acc

    return _pcall(body, [v], name="all_reduce_small", out_shape=jax.ShapeDtypeStruct((rows, 128), F32),
                  scratch=[pltpu.VMEM((8, rows, 128), F32), pltpu.SemaphoreType.DMA((7,)),
                           pltpu.SemaphoreType.DMA((7,))], comm=comm)


def _add_halves(pairs):
    k = len(pairs)

    def body(h_ref, *refs):
        for a_ref, b_ref, o_ref in zip(refs[0:2 * k:2], refs[1:2 * k:2], refs[2 * k:]):
            o_ref[...] = (a_ref[...].astype(F32) + b_ref[...].astype(F32)).astype(o_ref.dtype)

    in_specs, out_specs = [], []
    for _, got in pairs:
        _, h, c = got.shape
        spec = pl.BlockSpec((1, h, c), lambda j, h_ref: (j, 0, 0))
        in_specs += [pl.BlockSpec((1, h, c), lambda j, h_ref: (j, h_ref[0], 0)), spec]
        out_specs.append(spec)
    return _pcall(body, [a for pair in pairs for a in pair], name="add_halves", grid=(N_CHIPS,),
                  prefetch=[lax.axis_index("c")], in_specs=in_specs, out_specs=out_specs,
                  out_shape=[jax.ShapeDtypeStruct(got.shape, BF) for _, got in pairs])


def _sum_chips(pairs):
    k = len(pairs)
    n_steps = 2 if all(parts.shape[1] % 32 == 0 for parts, _ in pairs) else 1
    me = 2 * lax.axis_index("x") + lax.axis_index("y")

    def body(me_ref, *refs):
        for p_ref, r_ref, o_ref in zip(refs[0:2 * k:2], refs[1:2 * k:2], refs[2 * k:]):
            acc = p_ref[0].astype(F32)
            for s in range(N_CHIPS - 1):
                acc = acc + r_ref[s].astype(F32)
            o_ref[...] = acc

    in_specs, out_specs = [], []
    for parts, _ in pairs:
        _, h, c = parts.shape
        th = h // n_steps
        in_specs += [pl.BlockSpec((1, th, c), lambda i, me_ref: (me_ref[0], i, 0)),
                     pl.BlockSpec((N_CHIPS - 1, th, c), lambda i, me_ref: (0, i, 0))]
        out_specs.append(pl.BlockSpec((th, c), lambda i, me_ref: (i, 0)))
    return _pcall(body, [a for pair in pairs for a in pair], name="sum_chips", grid=(n_steps,), prefetch=[me],
                  in_specs=in_specs, out_specs=out_specs,
                  out_shape=[jax.ShapeDtypeStruct(parts.shape[1:], F32) for parts, _ in pairs])


def _adam_update(w, gv, m, v, d_ref, nm_ref, nv_ref):
    c1 = 1.0 / (1.0 - ADAM_B1 ** ADAM_STEP)
    c2 = 1.0 / (1.0 - ADAM_B2 ** ADAM_STEP)
    nm = ADAM_B1 * m + (1.0 - ADAM_B1) * gv
    nv = ADAM_B2 * v + (1.0 - ADAM_B2) * (gv * gv)
    nm_ref[...] = nm
    nv_ref[...] = nv
    d_ref[...] = -ADAM_LR * ((nm * c1) / (jnp.sqrt(nv * c2) + ADAM_EPS) + ADAM_WD * w)


def _adamw(w, g, m, v, comm=None):
    r, c = w.shape[0], w.shape[-1]
    tr = _row_tile(r, 512)

    def body(w_ref, g_ref, m_ref, v_ref, d_ref, nm_ref, nv_ref):
        _adam_update(w_ref[...], g_ref[...], m_ref[...], v_ref[...], d_ref, nm_ref, nv_ref)

    mid = (1,) * (w.ndim - 2)
    spec = pl.BlockSpec((tr,) + mid + (c,), lambda i: (i,) + (0,) * (w.ndim - 1))
    sds = jax.ShapeDtypeStruct(w.shape, F32)
    return _pcall(body, [w, g, m, v], name="adamw", grid=(r // tr,), out_shape=[sds, sds, sds],
                  in_specs=[spec] * 4, out_specs=[spec] * 3, comm=comm)


def _adamw_vectors(items):
    k = len(items)

    def body(*refs):
        ins, outs = refs[:4 * k], refs[4 * k:]
        for q in range(k):
            w_ref, g_ref, m_ref, v_ref = ins[4 * q:4 * q + 4]
            _adam_update(w_ref[...], g_ref[...], m_ref[...], v_ref[...], *outs[3 * q:3 * q + 3])

    return _pcall(body, [a for it in items for a in it], name="adamw_vectors",
                  out_shape=[jax.ShapeDtypeStruct(it[0].shape, F32) for it in items for _ in range(3)])


def _adamw_halves(items, comm=None):
    k = len(items)
    r, c = items[0][0].shape
    h = r // 2
    tr = _row_tile(h, min(512, (VMEM_LIMIT * 3 // 4) // (k * 9 * 2 * 4 * c)))
    nb = h // tr
    core = lax.axis_index("c")

    def body(c_ref, *refs):
        ins, outs = refs[:5 * k], refs[5 * k:]
        for q in range(k):
            w_ref, gm_ref, go_ref, m_ref, v_ref = ins[5 * q:5 * q + 5]
            g_ref, d_ref, nm_ref, nv_ref = outs[4 * q:4 * q + 4]
            gv = jnp.where(pl.program_id(0) == c_ref[0], gm_ref[...], go_ref[...])
            g_ref[...] = gv
            _adam_update(w_ref[...], gv, m_ref[...], v_ref[...], d_ref, nm_ref, nv_ref)

    full = pl.BlockSpec((tr, c), lambda hh, i, c_ref: (hh * nb + i, 0))
    half = pl.BlockSpec((tr, c), lambda hh, i, c_ref: (i, 0))
    sds = jax.ShapeDtypeStruct((r, c), F32)
    return _pcall(body, [a for it in items for a in it], name="adamw_halves", grid=(2, nb), prefetch=[core],
                  out_shape=[sds] * (4 * k), in_specs=[full, half, half, full, full] * k, out_specs=[full] * (4 * k),
                  comm=comm)


SC_CORES, SC_TILES, SC_LANES = 2, 16, 16
SC_BLOCK_ROWS, SC_BLOCK_COLS = 8, 512


def _sc_adamw_halves(items):
    k = len(items)
    r, c = items[0][0].shape
    h = r // 2
    bc = min(c, SC_BLOCK_COLS)
    c1 = 1.0 / (1.0 - ADAM_B1 ** ADAM_STEP)
    c2 = 1.0 / (1.0 - ADAM_B2 ** ADAM_STEP)
    mesh = plsc.VectorSubcoreMesh(core_axis_name="sc_core", subcore_axis_name="sc_tile",
                                  num_cores=SC_CORES, num_subcores=SC_TILES)
    spec = pl.BlockSpec(block_shape=(SC_BLOCK_ROWS, bc), index_map=lambda i, j: (i, j))

    def block(w_v, gin_v, m_v, v_v, g_v, d_v, nm_v, nv_v):
        @pl.loop(0, SC_BLOCK_ROWS)
        def _(row):
            @pl.loop(0, bc, step=SC_LANES)
            def _(col):
                at = (pl.ds(row, 1), pl.ds(col, SC_LANES))
                gv = gin_v.at[*at][...]
                nm = ADAM_B1 * m_v.at[*at][...] + (1.0 - ADAM_B1) * gv
                nv = ADAM_B2 * v_v.at[*at][...] + (1.0 - ADAM_B2) * (gv * gv)
                g_v.at[*at][...] = gv
                nm_v.at[*at][...] = nm
                nv_v.at[*at][...] = nv
                d_v.at[*at][...] = -ADAM_LR * ((nm * c1) / (jnp.sqrt(nv * c2) + ADAM_EPS) + ADAM_WD * w_v.at[*at][...])

    def kern(*refs):
        ins, outs = refs[:5 * k], refs[5 * k:]
        core = lax.axis_index("c")

        def half(q, hh, mine):
            w_hbm, gm_hbm, go_hbm, m_hbm, v_hbm = ins[5 * q:5 * q + 5]
            rows = pl.ds(hh * h, h)
            pltpu.emit_pipeline(
                block, grid=(h // SC_BLOCK_ROWS, c // bc), in_specs=[spec] * 4, out_specs=[spec] * 4,
                core_axis_name=("sc_core", "sc_tile"), dimension_semantics=(pltpu.PARALLEL, pltpu.PARALLEL),
                trace_scopes=False,
            )(w_hbm.at[rows, :], gm_hbm if mine else go_hbm, m_hbm.at[rows, :], v_hbm.at[rows, :],
              *(o.at[rows, :] for o in outs[4 * q:4 * q + 4]))

        for q in range(k):
            for hh in range(2):
                pl.when(core == hh)(lambda q=q, hh=hh: half(q, hh, True))
                pl.when(core != hh)(lambda q=q, hh=hh: half(q, hh, False))

    sds = jax.ShapeDtypeStruct((r, c), F32)
    return pl.kernel(kern, out_type=[sds] * (4 * k), mesh=mesh, scratch_types=[], name="sc_adamw_halves")(
        *(a for it in items for a in it))


def _wgrad(name, a, b, a_spec, b_spec, m, n, nb, comm):
    def body(a_ref, b_ref, o_ref):
        o_ref[...] = _dot_tn(a_ref[...], b_ref[...]).astype(o_ref.dtype)

    return _pcall(body, [a, b], name=name, grid=(nb,), out_shape=jax.ShapeDtypeStruct((nb, m, n), BF),
                  in_specs=[a_spec, b_spec], out_specs=pl.BlockSpec((None, m, n), lambda j: (j, 0, 0)), comm=comm)


def _wgrad_cols(name, a, b, nb, comm=None):
    t_tok, m = a.shape
    n = b.shape[1] // nb
    return _wgrad(name, a, b, pl.BlockSpec((t_tok, m), lambda j: (0, 0)), pl.BlockSpec((t_tok, n), lambda j: (0, j)),
                  m, n, nb, comm)


def _wgrad_rows(name, a, b, nb, comm=None):
    t_tok, n = b.shape
    m = a.shape[1] // nb
    return _wgrad(name, a, b, pl.BlockSpec((t_tok, m), lambda j: (0, j)), pl.BlockSpec((t_tok, n), lambda j: (0, 0)),
                  m, n, nb, comm)


def _wgrad_a_shared(name, a, b4, comm=None):
    t_tok, m = a.shape
    nb, _, n = b4.shape
    return _wgrad(name, a, b4, pl.BlockSpec((t_tok, m), lambda j: (0, 0)),
                  pl.BlockSpec((None, t_tok, n), lambda j: (j, 0, 0)), m, n, nb, comm)


def _wgrad_b_shared(name, a4, b, comm=None):
    nb, t_tok, m = a4.shape
    n = b.shape[1]
    return _wgrad(name, a4, b, pl.BlockSpec((None, t_tok, m), lambda j: (j, 0, 0)),
                  pl.BlockSpec((t_tok, n), lambda j: (0, 0)), m, n, nb, comm)


def _w4_spec(r, c):
    return pl.BlockSpec((None, r, c), lambda i, j: (j, 0, 0))


FFN_ROW_CHUNK = 256


def _row_chunks(tm):
    rc = FFN_ROW_CHUNK if tm % FFN_ROW_CHUNK == 0 else tm
    return [slice(r, r + rc) for r in range(0, tm, rc)]


def _ffn_fwd(h, ln, wg4, wu4, wd4, comm=None):
    t_tok, d = h.shape
    f = wg4.shape[-2]
    tm = _tile(t_tok, 512)

    def body(h_ref, ln_ref, wg_ref, wu_ref, wd_ref, ho_ref, n_ref, g_ref, u_ref, n_s, acc):
        j = pl.program_id(1)

        @pl.when(j == 0)
        def _():
            xv = h_ref[...]
            nv = (xv * _rstd(xv) * ln_ref[...]).astype(BF)
            n_s[...] = nv
            n_ref[...] = nv
            acc[...] = jnp.zeros_like(acc)

        nv = n_s[...]
        g = _dot_nt(nv, wg_ref[...])
        u = _dot_nt(nv, wu_ref[...])
        g_ref[...] = g.astype(BF)
        u_ref[...] = u.astype(BF)
        a = (g * _sigmoid(g) * u).astype(BF)
        acc[...] += _dot(a, wd_ref[...])

        @pl.when(j == N_CHIPS - 1)
        def _():
            ho_ref[...] = h_ref[...] + 0.5 * acc[...]

    row = pl.BlockSpec((tm, d), lambda i, j: (i, 0))
    gu = pl.BlockSpec((None, tm, f), lambda i, j: (j, i, 0))
    gu_sds = jax.ShapeDtypeStruct((N_CHIPS, t_tok, f), BF)
    return _pcall(
        body, [h, ln, wg4, wu4, wd4], name="ffn_fwd", grid=(t_tok // tm, N_CHIPS),
        out_shape=[jax.ShapeDtypeStruct((t_tok, d), F32), jax.ShapeDtypeStruct((t_tok, d), BF), gu_sds, gu_sds],
        in_specs=[row, pl.BlockSpec((1, d), lambda i, j: (0, 0)), _w4_spec(f, d), _w4_spec(f, d), _w4_spec(f, d)],
        out_specs=[row, row, gu, gu],
        scratch=[pltpu.VMEM((tm, d), BF), pltpu.VMEM((tm, d), F32)], comm=comm)


def _ffn_bwd(dho, h, ln, g4, u4, wg4, wu4, wd4, comm=None):
    t_tok, d = h.shape
    f = wg4.shape[-2]
    tm = _tile(t_tok, 512)

    def body(dho_ref, h_ref, ln_ref, g_ref, u_ref, wg_ref, wu_ref, wd_ref,
             dhi_ref, dln_ref, dg_ref, du_ref, a_ref, dhb_ref, dhb_s, dn_acc):
        i, j = pl.program_id(0), pl.program_id(1)

        @pl.when(j == 0)
        def _():
            dhb = (0.5 * dho_ref[...]).astype(BF)
            dhb_s[...] = dhb
            dhb_ref[...] = dhb
            dn_acc[...] = jnp.zeros_like(dn_acc)

        @pl.when((i == 0) & (j == 0))
        def _():
            dln_ref[...] = jnp.zeros_like(dln_ref)

        for rows in _row_chunks(tm):
            g = g_ref[rows, :].astype(F32)
            u = u_ref[rows, :].astype(F32)
            s = _sigmoid(g)
            sg = g * s
            a_ref[rows, :] = (sg * u).astype(BF)
            da = _dot_nt(dhb_s[rows, :], wd_ref[...])
            dg = (da * u * (s * (1.0 + g * (1.0 - s)))).astype(BF)
            du = (da * sg).astype(BF)
            dg_ref[rows, :] = dg
            du_ref[rows, :] = du
            dn_acc[rows, :] += _dot(dg, wg_ref[...]) + _dot(du, wu_ref[...])

        @pl.when(j == N_CHIPS - 1)
        def _():
            xv = h_ref[...]
            dx, dln = _rms_bwd(dn_acc[...], xv, _rstd(xv), ln_ref[...])
            dln_ref[...] += dln
            dhi_ref[...] = dho_ref[...] + dx

    row = pl.BlockSpec((tm, d), lambda i, j: (i, 0))
    vec = pl.BlockSpec((1, d), lambda i, j: (0, 0))
    gu = pl.BlockSpec((None, tm, f), lambda i, j: (j, i, 0))
    gu_sds = jax.ShapeDtypeStruct((N_CHIPS, t_tok, f), BF)
    return _pcall(
        body, [dho, h, ln, g4, u4, wg4, wu4, wd4], name="ffn_bwd", grid=(t_tok // tm, N_CHIPS),
        out_shape=[jax.ShapeDtypeStruct((t_tok, d), F32), jax.ShapeDtypeStruct((1, d), F32),
                   gu_sds, gu_sds, gu_sds, jax.ShapeDtypeStruct((t_tok, d), BF)],
        in_specs=[row, row, vec, gu, gu, _w4_spec(f, d), _w4_spec(f, d), _w4_spec(f, d)],
        out_specs=[row, vec, gu, gu, gu, row],
        scratch=[pltpu.VMEM((tm, d), BF), pltpu.VMEM((tm, d), F32)], comm=comm)


def _rope_tables(pos_col, inv_freq2, comm=None):
    t_tok = pos_col.shape[0]

    def body(p_ref, f_ref, cos_ref, sin_ref):
        ang = p_ref[...] * f_ref[...]
        lane = lax.broadcasted_iota(jnp.int32, ang.shape, 1)
        s = jnp.sin(ang)
        cos_ref[...] = jnp.cos(ang)
        sin_ref[...] = jnp.where((lane & 1) == 0, -s, s)

    sds = jax.ShapeDtypeStruct((t_tok, 128), F32)
    return _pcall(body, [pos_col, inv_freq2], name="rope_tables", out_shape=[sds, sds], comm=comm)


def _swap_pairs(x):
    lane = lax.broadcasted_iota(jnp.int32, x.shape, 1)
    return jnp.where((lane & 1) == 0, pltpu.roll(x, 127, 1), pltpu.roll(x, 1, 1))


def _mix_in(h, ln, w_in, wm4, b_m, cos_t, sin_t, comm=None):
    t_tok, d = h.shape
    cm = wm4.shape[-1]
    tm = _tile(t_tok, 256)

    def body(h_ref, ln_ref, win_ref, wm_ref, bm_ref, cos_ref, sin_ref,
             u_ref, rq_ref, rk_ref, rv_ref, rg_ref, fq_ref, fk_ref, fv_ref, ff_ref, ga_ref, gb_ref):
        xv = h_ref[...]
        ub = (xv * _rstd(xv) * ln_ref[...]).astype(BF)
        u_ref[...] = ub
        cosv, sinv = cos_ref[...], sin_ref[...]

        def sec(k):
            return _dot_nt(ub, win_ref[k * 512:(k + 1) * 512, :])

        def rot(xh):
            return xh * cosv + _swap_pairs(xh) * sinv

        pq, pk = sec(0), sec(1)
        for hh in range(RET_HEADS):
            sl = slice(hh * RET_DIM, (hh + 1) * RET_DIM)
            rq_ref[:, sl] = rot(pq[:, sl]).astype(BF)
            rk_ref[:, sl] = (rot(pk[:, sl]) * RET_SCALE).astype(BF)
        rv_ref[...] = sec(2).astype(BF)
        rg_ref[...] = sec(3).astype(BF)
        fq_ref[...] = (sec(4) * FOX_SCALE).astype(BF)
        fk_ref[...] = sec(5).astype(BF)
        fv_ref[...] = sec(6).astype(BF)
        ff_ref[...] = _dot_nt(ub, win_ref[FF_COL:FF_COL + 128, :])
        for j in range(N_CHIPS):
            gs = _sigmoid(_dot(ub, wm_ref[j]) + bm_ref[:, j * cm:(j + 1) * cm]).astype(BF)
            col = j * cm
            if col < d:
                ga_ref[:, col:col + cm] = gs
            else:
                gb_ref[:, col - d:col - d + cm] = gs

    row = lambda c: pl.BlockSpec((tm, c), lambda i: (i, 0))
    full = lambda *s: pl.BlockSpec(s, lambda i: (0,) * len(s))
    sds = lambda c, dt: jax.ShapeDtypeStruct((t_tok, c), dt)
    return _pcall(
        body, [h, ln, w_in, wm4, b_m, cos_t, sin_t], name="mix_in", grid=(t_tok // tm,),
        out_shape=[sds(d, BF)] + [sds(512, BF)] * 7 + [sds(128, F32), sds(d, BF), sds(d, BF)],
        in_specs=[row(d), full(1, d), full(IN_PAD, d), full(N_CHIPS, d, cm), full(1, 2 * d), row(128), row(128)],
        out_specs=[row(d)] + [row(512)] * 7 + [row(128), row(d), row(d)], comm=comm)


def _split3(x):
    hi = x.astype(BF)
    r1 = x - hi.astype(F32)
    mid = r1.astype(BF)
    lo = (r1 - mid.astype(F32)).astype(BF)
    return hi, mid, lo


def _aug_lane():
    return lax.broadcasted_iota(jnp.int32, (1, 128), 1) & (FOX_DIM - 1)


def _aug_put(base, k0, parts):
    w = _aug_lane()
    for i, part in enumerate(parts):
        base = jnp.where(w == k0 + i, part, base)
    return base


def _forget_fwd(ffl, b_pad):
    t_tok = ffl.shape[0]
    tb = _tile(t_tok, 256)

    def body(ff_ref, b_ref, aq_ref, ak_ref, cum_s):
        r = lax.broadcasted_iota(jnp.int32, (tb, tb), 0)
        c = lax.broadcasted_iota(jnp.int32, (tb, tb), 1)
        tri = jnp.where(c <= r, 1.0, 0.0).astype(BF)
        carry = jnp.zeros((1, 128), F32)
        for i in range(t_tok // tb):
            z = ff_ref[i * tb:(i + 1) * tb, :] + b_ref[...]
            lf = jnp.minimum(z, 0.0) - jnp.log(1.0 + jnp.exp(-jnp.abs(z)))
            hi, mid, lo = _split3(lf)
            cs = _dot(tri, hi) + _dot(tri, mid) + _dot(tri, lo) + carry
            cum_s[i * tb:(i + 1) * tb, :] = cs
            carry = cs[tb - 1:tb, :]
        x = cum_s[...]
        first = lax.broadcasted_iota(jnp.int32, (1, 128), 1) < FOX_DIM
        w = _aug_lane()
        one = jnp.ones((t_tok, 128), BF)
        zero = jnp.zeros((t_tok, 128), BF)
        for pp in range(FOX_HEADS // 2):
            other = jnp.where(first, x[:, 2 * pp + 1:2 * pp + 2], x[:, 2 * pp:2 * pp + 1])
            parts = _split3(other)
            aq = jnp.where((w >= 3) & (w < 6), one, zero)
            ak = jnp.where((w < 3) | ((w >= 6) & (w < 9)), one, zero)
            aq_ref[:, pp * 128:(pp + 1) * 128] = _aug_put(aq, 0, parts)
            ak_ref[:, pp * 128:(pp + 1) * 128] = _aug_put(ak, 3, [-q for q in parts])

    sds = jax.ShapeDtypeStruct((t_tok, FOX_WIDTH), BF)
    return _pcall(body, [ffl, b_pad], name="forget_fwd", out_shape=[sds, sds],
                  scratch=[pltpu.VMEM((t_tok, 128), F32)])


def _forget_bwd(dcum_t, dcum_q, ffl, b_pad):
    t_tok = ffl.shape[0]
    tb = _tile(t_tok, 256)

    def body(dc_ref, dq_ref, ff_ref, b_ref, dff_ref, db_ref, pad_s, d_s):
        pad_s[...] = jnp.zeros_like(pad_s)
        pad_s[0:FOX_HEADS, :] = dc_ref[...]
        dsum = pad_s[...].T
        lane = lax.broadcasted_iota(jnp.int32, (t_tok, 128), 1)
        for hh in range(FOX_HEADS):
            dsum = dsum + jnp.where(lane == hh, dq_ref[:, hh * FOX_DIM:hh * FOX_DIM + 1], 0.0)
        d_s[...] = dsum
        r = lax.broadcasted_iota(jnp.int32, (tb, tb), 0)
        c = lax.broadcasted_iota(jnp.int32, (tb, tb), 1)
        tri = jnp.where(c >= r, 1.0, 0.0).astype(BF)
        carry = jnp.zeros((1, 128), F32)
        db = jnp.zeros((1, 128), F32)
        for i in reversed(range(t_tok // tb)):
            hi, mid, lo = _split3(d_s[i * tb:(i + 1) * tb, :])
            dlf = _dot(tri, hi) + _dot(tri, mid) + _dot(tri, lo) + carry
            carry = dlf[0:1, :]
            z = ff_ref[i * tb:(i + 1) * tb, :] + b_ref[...]
            dff = dlf * _sigmoid(-z)
            dff_ref[i * tb:(i + 1) * tb, :] = dff.astype(BF)
            db = db + jnp.sum(dff, axis=0, keepdims=True)
        db_ref[...] = db

    return _pcall(
        body, [dcum_t, dcum_q, ffl, b_pad], name="forget_bwd",
        out_shape=[jax.ShapeDtypeStruct((t_tok, 128), BF), jax.ShapeDtypeStruct((1, 128), F32)],
        scratch=[pltpu.VMEM((128, t_tok), F32), pltpu.VMEM((t_tok, 128), F32)])


def _first_half():
    return lax.broadcasted_iota(jnp.int32, (1, 128), 1) < FOX_DIM


def _head_rows(x2, a2, hh):
    return jnp.where(_first_half(), x2, a2) if hh == 0 else jnp.where(_first_half(), a2, x2)


def _head_only(x2, hh):
    zero = jnp.zeros_like(x2)
    return jnp.where(_first_half(), x2, zero) if hh == 0 else jnp.where(_first_half(), zero, x2)


def _causal_diag(s):
    rows = lax.broadcasted_iota(jnp.int32, s.shape, 0)
    cols = lax.broadcasted_iota(jnp.int32, s.shape, 1)
    return jnp.where(cols <= rows, s, NEG)


def _diag_or_below(qi, ki, step):
    pl.when(ki < qi)(lambda: step(False))
    pl.when(ki == qi)(lambda: step(True))


def _tri_rows(s, n):
    qi = sum((s >= r * (r + 1) // 2).astype(jnp.int32) for r in range(1, n))
    return qi, s - (qi * (qi + 1)) // 2


def _tri_cols(s, n):
    ki = sum((s >= k * n - k * (k - 1) // 2).astype(jnp.int32) for k in range(1, n))
    return ki, ki + s - (ki * n - (ki * (ki - 1)) // 2)


def _fox_fwd(fq, fk, fv, aq, ak, comm=None):
    t_tok = fq.shape[0]
    t = _tile(t_tok, 512)
    nq = t_tok // t
    npair = FOX_HEADS // 2

    def body(q_ref, k_ref, v_ref, aq_ref, ak_ref, o_ref, of_ref, aqb_ref, m_s, l_s, acc_s):
        qi, ki = _tri_rows(pl.program_id(1), nq)

        @pl.when(ki == 0)
        def _():
            m_s[...] = jnp.full_like(m_s, NEG)
            l_s[...] = jnp.zeros_like(l_s)
            acc_s[...] = jnp.zeros_like(acc_s)

        def step(diag):
            q2, k2, v2, aq2, ak2 = q_ref[...], k_ref[...], v_ref[...], aq_ref[...], ak_ref[...]
            for hh in range(2):
                s = _dot_nt(_head_rows(q2, aq2, hh), _head_rows(k2, ak2, hh))
                if diag:
                    s = _causal_diag(s)
                m_prev = m_s[hh]
                m_new = jnp.maximum(m_prev, jnp.max(s, axis=1, keepdims=True))
                alpha = jnp.exp(m_prev - m_new)
                p = jnp.exp(s - jnp.tile(m_new, (1, t // 128)))
                l_s[hh] = alpha * l_s[hh] + jnp.sum(p, axis=1, keepdims=True)
                acc_s[hh] = alpha * acc_s[hh] + _dot(p.astype(BF), v2)
                m_s[hh] = m_new

        _diag_or_below(qi, ki, step)

        @pl.when(ki == qi)
        def _():
            first = _first_half()
            o = jnp.where(first, acc_s[0] / l_s[0], acc_s[1] / l_s[1])
            o_ref[...] = o.astype(BF)
            of_ref[...] = o
            other = jnp.where(first, m_s[1] + jnp.log(l_s[1]), m_s[0] + jnp.log(l_s[0]))
            aqb_ref[...] = _aug_put(aq_ref[...], 6, _split3(-other))

    qs = pl.BlockSpec((t, 128), lambda p, s: (_tri_rows(s, nq)[0], p))
    ks = pl.BlockSpec((t, 128), lambda p, s: (_tri_rows(s, nq)[1], p))
    stat = pltpu.VMEM((2, t, 128), F32)
    return _pcall(
        body, [fq, fk, fv, aq, ak], name="fox_fwd", grid=(npair, nq * (nq + 1) // 2),
        out_shape=[jax.ShapeDtypeStruct((t_tok, FOX_WIDTH), BF), jax.ShapeDtypeStruct((t_tok, FOX_WIDTH), F32),
                   jax.ShapeDtypeStruct((t_tok, FOX_WIDTH), BF)],
        in_specs=[qs, ks, ks, qs, ks], out_specs=[qs, qs, qs], scratch=[stat, stat, stat], comm=comm)


def _fox_ds(q2, k2, v2, do2, aq2, ak2, ad2, hh, diag):
    s = _dot_nt(_head_rows(q2, aq2, hh), _head_rows(k2, ak2, hh))
    if diag:
        s = _causal_diag(s)
    p = jnp.exp(s)
    av = jnp.where(_aug_lane() < 3, 1.0, 0.0).astype(BF)
    dp = _dot_nt(_head_rows(do2, ad2, hh), _head_rows(v2, jnp.broadcast_to(av, v2.shape), hh))
    return p, p * dp


def _fox_bwd(fq, fk, fv, do, aqb, ak, ad, comm=None):
    t_tok = fq.shape[0]
    t = _tile(t_tok, 512)
    nq = t_tok // t
    npair = FOX_HEADS // 2
    n_steps = nq * (nq + 1) // 2

    def body(q_ref, k_ref, v_ref, do_ref, aq_ref, ak_ref, ad_ref, dq_ref, dk_ref, dv_ref, dck_ref, dcq_ref,
             dk_s, dv_s, dq_s, rs_s):
        step_id = pl.program_id(1)
        ki, qi = _tri_cols(step_id, nq)

        @pl.when(step_id == 0)
        def _():
            dq_s[...] = jnp.zeros_like(dq_s)
            rs_s[...] = jnp.zeros_like(rs_s)

        @pl.when(qi == ki)
        def _():
            dk_s[...] = jnp.zeros_like(dk_s)
            dv_s[...] = jnp.zeros_like(dv_s)
            dck_ref[...] = jnp.zeros_like(dck_ref)

        rows = pl.ds(qi * t if isinstance(qi, int) else pl.multiple_of(qi * t, t), t)

        def step(diag):
            q2, k2, v2, do2 = q_ref[...], k_ref[...], v_ref[...], do_ref[...]
            dq = []
            for hh in range(2):
                p, ds = _fox_ds(q2, k2, v2, do2, aq_ref[...], ak_ref[...], ad_ref[...], hh, diag)
                dsb = ds.astype(BF)
                dv_s[...] += _dot_tn(p.astype(BF), _head_only(do2, hh))
                dk_s[...] += _dot_tn(dsb, _head_only(q2, hh))
                dq.append(_dot(dsb, k2))
                dck_ref[hh] = dck_ref[hh] - jnp.sum(ds, axis=0, keepdims=True)
                rs_s[hh, rows, :] = rs_s[hh, rows, :] + jnp.sum(ds, axis=1, keepdims=True)
            dq_s[rows, :] = dq_s[rows, :] + jnp.where(_first_half(), dq[0], dq[1])

        _diag_or_below(qi, ki, step)

        @pl.when(qi == nq - 1)
        def _():
            dk_ref[...] = dk_s[...].astype(BF)
            dv_ref[...] = dv_s[...].astype(BF)

        @pl.when(step_id == n_steps - 1)
        def _():
            dq_ref[...] = (dq_s[...] * FOX_SCALE).astype(BF)
            dcq_ref[...] = jnp.where(_first_half(), rs_s[0], rs_s[1])

    qs = pl.BlockSpec((t, 128), lambda p, s: (_tri_cols(s, nq)[1], p))
    ks = pl.BlockSpec((t, 128), lambda p, s: (_tri_cols(s, nq)[0], p))
    cks = pl.BlockSpec((2, 1, t), lambda p, s: (p, 0, _tri_cols(s, nq)[0]))
    seq = pl.BlockSpec((t_tok, 128), lambda p, s: (0, p))
    sds = jax.ShapeDtypeStruct((t_tok, FOX_WIDTH), BF)
    return _pcall(
        body, [fq, fk, fv, do, aqb, ak, ad], name="fox_bwd", grid=(npair, n_steps),
        out_shape=[sds, sds, sds, jax.ShapeDtypeStruct((FOX_HEADS, 1, t_tok), F32),
                   jax.ShapeDtypeStruct((t_tok, FOX_WIDTH), F32)],
        in_specs=[qs, ks, ks, qs, qs, ks, qs], out_specs=[seq, ks, ks, cks, seq],
        scratch=[pltpu.VMEM((t, 128), F32), pltpu.VMEM((t, 128), F32), pltpu.VMEM((t_tok, 128), F32),
                 pltpu.VMEM((2, t_tok, 128), F32)], comm=comm)


def _ret_consts():
    c = RET_CHUNK
    log_gamma = jnp.log1p(-jnp.exp2(-5.0 - jnp.arange(RET_HEADS, dtype=F32)))
    idx = jnp.arange(c, dtype=F32)
    diff = idx[:, None] - idx[None, :]
    dmask = jnp.where(diff >= 0, jnp.exp(log_gamma[:, None, None] * jnp.maximum(diff, 0.0)), 0.0)
    qdec = jnp.exp(log_gamma[:, None] * (idx + 1.0))
    kdec = jnp.exp(log_gamma[:, None] * (c - 1 - idx))
    cdec = jnp.exp(log_gamma * c)
    bc = lambda v: jnp.broadcast_to(v[:, :, None], (RET_HEADS, c, RET_DIM))
    return dmask, bc(qdec), bc(kdec), jnp.broadcast_to(cdec[:, None, None], (RET_HEADS, c, RET_DIM))


def _group_norm(y):
    mu = jnp.mean(y, axis=-1, keepdims=True)
    yc = y - mu
    r = lax.rsqrt(jnp.mean(yc * yc, axis=-1, keepdims=True) + EPS)
    return yc * r, r


def _ret_fwd(rq, rk, rv, rg, consts, comm=None):
    t_tok = rq.shape[0]
    nb = 4 if t_tok % (4 * RET_CHUNK) == 0 else 1
    tr = nb * RET_CHUNK
    n_steps = t_tok // tr
    c = RET_CHUNK

    def body(q_ref, k_ref, v_ref, g_ref, dm_ref, qd_ref, kd_ref, cd_ref, y_ref, yo_ref, st_ref, s_s):
        @pl.when(pl.program_id(0) == 0)
        def _():
            s_s[...] = jnp.zeros_like(s_s)

        for b in range(nb):
            rows = slice(b * c, (b + 1) * c)
            for hh in range(RET_HEADS):
                cols = slice(hh * RET_DIM, (hh + 1) * RET_DIM)
                q, k, v = q_ref[rows, cols], k_ref[rows, cols], v_ref[rows, cols]
                state = s_s[hh]
                st_ref[hh, b] = state
                sc = (_dot_nt(q, k) * dm_ref[hh]).astype(BF)
                y = _dot(sc, v) + _dot((q.astype(F32) * qd_ref[hh]).astype(BF), state.astype(BF))
                s_s[hh] = cd_ref[hh] * state + _dot_tn((k.astype(F32) * kd_ref[hh]).astype(BF), v)
                y_ref[rows, cols] = y
                yn, _ = _group_norm(y)
                gate = g_ref[rows, cols].astype(F32)
                yo_ref[rows, cols] = (yn * (gate * _sigmoid(gate))).astype(BF)

    blk = pl.BlockSpec((tr, RET_WIDTH), lambda i: (i, 0))
    cst = pl.BlockSpec((RET_HEADS, c, RET_DIM), lambda i: (0, 0, 0))
    return _pcall(
        body, [rq, rk, rv, rg, *consts], name="ret_fwd", grid=(n_steps,),
        out_shape=[jax.ShapeDtypeStruct((t_tok, RET_WIDTH), F32), jax.ShapeDtypeStruct((t_tok, RET_WIDTH), BF),
                   jax.ShapeDtypeStruct((RET_HEADS, t_tok // c, RET_DIM, RET_DIM), F32)],
        in_specs=[blk] * 4 + [cst] * 4,
        out_specs=[blk, blk, pl.BlockSpec((RET_HEADS, nb, RET_DIM, RET_DIM), lambda i: (0, i, 0, 0))],
        scratch=[pltpu.VMEM((RET_HEADS, RET_DIM, RET_DIM), F32)], comm=comm)


def _ret_bwd(rq, rk, rv, rg, y_raw, dyo, states, consts, cos_t, sin_t, comm=None):
    t_tok = rq.shape[0]
    nb = 4 if t_tok % (4 * RET_CHUNK) == 0 else 1
    tr = nb * RET_CHUNK
    n_steps = t_tok // tr
    c = RET_CHUNK

    def body(q_ref, k_ref, v_ref, g_ref, y_ref, dyo_ref, st_ref, dm_ref, qd_ref, kd_ref, cd_ref,
             cos_ref, sin_ref, dq_ref, dk_ref, dv_ref, dg_ref, ds_s):
        @pl.when(pl.program_id(0) == 0)
        def _():
            ds_s[...] = jnp.zeros_like(ds_s)

        for b in reversed(range(nb)):
            rows = slice(b * c, (b + 1) * c)
            cosv, sinv = cos_ref[rows, :], sin_ref[rows, :]
            for hh in range(RET_HEADS):
                cols = slice(hh * RET_DIM, (hh + 1) * RET_DIM)
                dm, qd, kd, cd = dm_ref[hh], qd_ref[hh], kd_ref[hh], cd_ref[hh]
                q, k, v = q_ref[rows, cols], k_ref[rows, cols], v_ref[rows, cols]
                yn, r = _group_norm(y_ref[rows, cols])
                gate = g_ref[rows, cols].astype(F32)
                sg = _sigmoid(gate)
                dyo = dyo_ref[rows, cols]
                dg_ref[rows, cols] = (dyo * yn * (sg * (1.0 + gate * (1.0 - sg)))).astype(BF)
                dyn = dyo * (gate * sg)
                dy = r * (dyn - jnp.mean(dyn, axis=-1, keepdims=True)
                          - yn * jnp.mean(dyn * yn, axis=-1, keepdims=True))
                dyb = dy.astype(BF)
                state_b = st_ref[hh, b].astype(BF)
                dstate = ds_s[hh]
                dstate_b = dstate.astype(BF)
                qdb = (q.astype(F32) * qd).astype(BF)
                kdb = (k.astype(F32) * kd).astype(BF)
                sc = (_dot_nt(q, k) * dm).astype(BF)
                dv = _dot_tn(sc, dyb) + _dot(kdb, dstate_b)
                dp = (_dot_nt(dyb, v) * dm).astype(BF)
                dq = _dot(dp, k) + _dot_nt(dyb, state_b) * qd
                dk = (_dot_tn(dp, q) + _dot_nt(v, dstate_b) * kd) * RET_SCALE
                ds_s[hh] = cd * dstate + _dot_tn(qdb, dyb)
                dv_ref[rows, cols] = dv.astype(BF)
                dq_ref[rows, cols] = (dq * cosv - _swap_pairs(dq) * sinv).astype(BF)
                dk_ref[rows, cols] = (dk * cosv - _swap_pairs(dk) * sinv).astype(BF)

    rev = lambda i: n_steps - 1 - i
    blk = pl.BlockSpec((tr, RET_WIDTH), lambda i: (rev(i), 0))
    tab = pl.BlockSpec((tr, RET_DIM), lambda i: (rev(i), 0))
    cst = pl.BlockSpec((RET_HEADS, c, RET_DIM), lambda i: (0, 0, 0))
    sds = jax.ShapeDtypeStruct((t_tok, RET_WIDTH), BF)
    return _pcall(
        body, [rq, rk, rv, rg, y_raw, dyo, states, *consts, cos_t, sin_t], name="ret_bwd",
        grid=(n_steps,), out_shape=[sds] * 4,
        in_specs=[blk] * 6 + [pl.BlockSpec((RET_HEADS, nb, RET_DIM, RET_DIM), lambda i: (0, rev(i), 0, 0))]
        + [cst] * 4 + [tab, tab],
        out_specs=[blk] * 4, scratch=[pltpu.VMEM((RET_HEADS, RET_DIM, RET_DIM), F32)], comm=comm)


def _mix_out(h, y_ret, y_fox, ga, gb, wr4, wf4, wo4, comm=None):
    t_tok, d = h.shape
    cz = wr4.shape[-1]
    ro = wo4.shape[-2]
    tm = _tile(t_tok, 512)

    def body(h_ref, yr_ref, yf_ref, ga_ref, gb_ref, wr_ref, wf_ref, wo_ref, ho_ref, za_ref, zb_ref, mix_ref):
        yr, yf = yr_ref[...], yf_ref[...]
        for j in range(N_CHIPS):
            sl = slice(j * cz, (j + 1) * cz)
            za = _dot(yr, wr_ref[j])
            zb = _dot(yf, wf_ref[j])
            za_ref[:, sl] = za.astype(BF)
            zb_ref[:, sl] = zb.astype(BF)
            mix_ref[:, sl] = (ga_ref[:, sl].astype(F32) * za + gb_ref[:, sl].astype(F32) * zb).astype(BF)
        acc = h_ref[...]
        for j in range(N_CHIPS):
            acc = acc + _dot(mix_ref[:, j * ro:(j + 1) * ro], wo_ref[j])
        ho_ref[...] = acc

    row = lambda c: pl.BlockSpec((tm, c), lambda i: (i, 0))
    full = lambda *s: pl.BlockSpec(s, lambda i: (0,) * len(s))
    sds = lambda dt: jax.ShapeDtypeStruct((t_tok, d), dt)
    return _pcall(
        body, [h, y_ret, y_fox, ga, gb, wr4, wf4, wo4], name="mix_out", grid=(t_tok // tm,),
        out_shape=[sds(F32), sds(BF), sds(BF), sds(BF)],
        in_specs=[row(d), row(RET_WIDTH), row(FOX_WIDTH), row(d), row(d),
                  full(N_CHIPS, RET_WIDTH, cz), full(N_CHIPS, FOX_WIDTH, cz), full(N_CHIPS, ro, d)],
        out_specs=[row(d)] * 4, comm=comm)


def _mix_out_bwd(dh, za, zb, ga, gb, y_fox, wr4, wf4, wo4, comm=None):
    t_tok, d = dh.shape
    cz = wr4.shape[-1]
    ro = wo4.shape[-2]
    tm = _tile(t_tok, 256)

    def body(dh_ref, za_ref, zb_ref, ga_ref, gb_ref, yf_ref, wr_ref, wf_ref, wo_ref,
             dhb_ref, dgp_ref, dza_ref, dzb_ref, dyr_ref, dyf_ref, dl_ref, db_ref):
        @pl.when(pl.program_id(0) == 0)
        def _():
            db_ref[...] = jnp.zeros_like(db_ref)

        dhb = dh_ref[...].astype(BF)
        dhb_ref[...] = dhb
        dyr = jnp.zeros((tm, RET_WIDTH), F32)
        dyf = jnp.zeros((tm, FOX_WIDTH), F32)
        for j in range(N_CHIPS):
            sl = slice(j * ro, (j + 1) * ro)
            dmix = _dot_nt(dhb, wo_ref[j])
            ga, gb = ga_ref[:, sl].astype(F32), gb_ref[:, sl].astype(F32)
            dza = (dmix * ga).astype(BF)
            dzb = (dmix * gb).astype(BF)
            dza_ref[:, sl] = dza
            dzb_ref[:, sl] = dzb
            dga = dmix * za_ref[:, sl].astype(F32) * ga * (1.0 - ga)
            dgb = dmix * zb_ref[:, sl].astype(F32) * gb * (1.0 - gb)
            dgp_ref[:, sl] = dga.astype(BF)
            dgp_ref[:, d + j * ro:d + (j + 1) * ro] = dgb.astype(BF)
            db_ref[:, sl] += jnp.sum(dga, axis=0, keepdims=True)
            db_ref[:, d + j * ro:d + (j + 1) * ro] += jnp.sum(dgb, axis=0, keepdims=True)
        for j in range(N_CHIPS):
            sl = slice(j * cz, (j + 1) * cz)
            dyr = dyr + _dot_nt(dza_ref[:, sl], wr_ref[j])
            dyf = dyf + _dot_nt(dzb_ref[:, sl], wf_ref[j])
        dyr_ref[...] = dyr
        dyfb = dyf.astype(BF)
        dyf_ref[...] = dyfb
        prod = dyfb.astype(F32) * yf_ref[...]
        first = _first_half()
        for pp in range(FOX_HEADS // 2):
            blk = prod[:, pp * 128:(pp + 1) * 128]
            s0 = jnp.sum(jnp.where(first, blk, 0.0), axis=1, keepdims=True)
            s1 = jnp.sum(jnp.where(first, 0.0, blk), axis=1, keepdims=True)
            parts = _split3(-jnp.where(first, s1, s0))
            dl_ref[:, pp * 128:(pp + 1) * 128] = _aug_put(jnp.zeros((tm, 128), BF), 0, parts)

    row = lambda c: pl.BlockSpec((tm, c), lambda i: (i, 0))
    full = lambda *s: pl.BlockSpec(s, lambda i: (0,) * len(s))
    sds = lambda c, dt: jax.ShapeDtypeStruct((t_tok, c), dt)
    return _pcall(
        body, [dh, za, zb, ga, gb, y_fox, wr4, wf4, wo4], name="mix_out_bwd", grid=(t_tok // tm,),
        out_shape=[sds(d, BF), sds(2 * d, BF), sds(d, BF), sds(d, BF), sds(RET_WIDTH, F32),
                   sds(FOX_WIDTH, BF), sds(FOX_WIDTH, BF), jax.ShapeDtypeStruct((1, 2 * d), F32)],
        in_specs=[row(d)] * 5 + [row(FOX_WIDTH), full(N_CHIPS, RET_WIDTH, cz), full(N_CHIPS, FOX_WIDTH, cz),
                                 full(N_CHIPS, ro, d)],
        out_specs=[row(d), row(2 * d), row(d), row(d), row(RET_WIDTH), row(FOX_WIDTH), row(FOX_WIDTH),
                   full(1, 2 * d)],
        comm=comm)


def _mix_in_bwd(dh, h, ln, parts, dff, dgpre, w_in, wm4, comm=None):
    t_tok, d = h.shape
    cm = wm4.shape[-1]
    tm = _tile(t_tok, 256)

    def body(dh_ref, h_ref, ln_ref, p0, p1, p2, p3, p4, p5, p6, dff_ref, dgp_ref, win_ref, wm_ref,
             dhi_ref, dln_ref, dproj_ref):
        @pl.when(pl.program_id(0) == 0)
        def _():
            dln_ref[...] = jnp.zeros_like(dln_ref)

        for k, pr in enumerate((p0, p1, p2, p3, p4, p5, p6)):
            dproj_ref[:, k * 512:(k + 1) * 512] = pr[...]
        dproj_ref[:, FF_COL:FF_COL + 128] = dff_ref[...]
        dproj_ref[:, FF_COL + 128:] = jnp.zeros((tm, IN_PAD - FF_COL - 128), BF)
        du = _dot(dproj_ref[...], win_ref[...])
        for j in range(N_CHIPS):
            du = du + _dot_nt(dgp_ref[:, j * cm:(j + 1) * cm], wm_ref[j])
        xv = h_ref[...]
        dx, dln = _rms_bwd(du, xv, _rstd(xv), ln_ref[...])
        dln_ref[...] += dln
        dhi_ref[...] = dh_ref[...] + dx

    row = lambda c: pl.BlockSpec((tm, c), lambda i: (i, 0))
    full = lambda *s: pl.BlockSpec(s, lambda i: (0,) * len(s))
    return _pcall(
        body, [dh, h, ln, *parts, dff, dgpre, w_in, wm4], name="mix_in_bwd", grid=(t_tok // tm,),
        out_shape=[jax.ShapeDtypeStruct((t_tok, d), F32), jax.ShapeDtypeStruct((1, d), F32),
                   jax.ShapeDtypeStruct((t_tok, IN_PAD), BF)],
        in_specs=[row(d), row(d), full(1, d)] + [row(512)] * 7 + [row(128), row(2 * d), full(IN_PAD, d),
                                                                   full(N_CHIPS, d, cm)],
        out_specs=[row(d), full(1, d), row(IN_PAD)], comm=comm)


def _tail(h, p, target, ln_ple, ln_fin, wpg4, wpl4, comm=None):
    t_tok, d = h.shape
    pd = p.shape[1]
    rg = wpg4.shape[-2]
    cp = wpl4.shape[-1]
    tm = _tile(t_tok, 256)

    def body(h_ref, p_ref, t_ref, lp_ref, lf_ref, wg_ref, wp_ref,
             dh_ref, n_ref, dgp_ref, dpe_ref, pb_ref, loss_ref, dlf_ref, dlp_ref, pe_s, dn_s):
        @pl.when(pl.program_id(0) == 0)
        def _():
            loss_ref[...] = jnp.zeros_like(loss_ref)
            dlf_ref[...] = jnp.zeros_like(dlf_ref)
            dlp_ref[...] = jnp.zeros_like(dlp_ref)

        xv = h_ref[...]
        r3 = _rstd(xv)
        nb = (xv * r3 * lp_ref[...]).astype(BF)
        n_ref[...] = nb
        pb = p_ref[...].astype(BF)
        pb_ref[...] = pb
        pgpre = jnp.zeros((tm, d), F32)
        for j in range(N_CHIPS):
            pgpre = pgpre + _dot(nb[:, j * rg:(j + 1) * rg], wg_ref[j])
            pe_s[:, j * cp:(j + 1) * cp] = _dot(pb, wp_ref[j])
        pg = _sigmoid(pgpre)
        pe = pe_s[...]
        h4 = xv + pg * pe
        r4 = _rstd(h4)
        err = h4 * r4 * lf_ref[...] - t_ref[...]
        loss_ref[...] += 0.5 * jnp.sum(jnp.sum(err * err, axis=1, keepdims=True), axis=0, keepdims=True) / d
        dh4, dlf = _rms_bwd(err * (1.0 / d), h4, r4, lf_ref[...])
        dlf_ref[...] += dlf
        dpe_ref[...] = (dh4 * pg).astype(BF)
        dgp = (dh4 * pe * pg * (1.0 - pg)).astype(BF)
        dgp_ref[...] = dgp
        for j in range(N_CHIPS):
            dn_s[:, j * rg:(j + 1) * rg] = _dot_nt(dgp, wg_ref[j])
        dx, dlp = _rms_bwd(dn_s[...], xv, r3, lp_ref[...])
        dlp_ref[...] += dlp
        dh_ref[...] = dh4 + dx

    row = lambda c: pl.BlockSpec((tm, c), lambda i: (i, 0))
    full = lambda *s: pl.BlockSpec(s, lambda i: (0,) * len(s))
    sds = lambda c, dt: jax.ShapeDtypeStruct((t_tok, c), dt)
    vec = jax.ShapeDtypeStruct((1, d), F32)
    return _pcall(
        body, [h, p, target, ln_ple, ln_fin, wpg4, wpl4], name="tail", grid=(t_tok // tm,),
        out_shape=[sds(d, F32), sds(d, BF), sds(d, BF), sds(d, BF), sds(pd, BF),
                   jax.ShapeDtypeStruct((1, 128), F32), vec, vec],
        in_specs=[row(d), row(pd), row(d), full(1, d), full(1, d), full(N_CHIPS, rg, d), full(N_CHIPS, pd, cp)],
        out_specs=[row(d), row(d), row(d), row(d), row(pd), full(1, 128), full(1, d), full(1, d)],
        scratch=[pltpu.VMEM((tm, d), F32), pltpu.VMEM((tm, d), F32)], comm=comm)


BIG = ["w_ffn1_gate", "w_ffn1_up", "w_ffn1_down", "w_in", "w_merge", "w_ret_out", "w_fox_out", "w_out",
       "w_ffn2_gate", "w_ffn2_up", "w_ffn2_down", "w_ple", "w_ple_gate"]
SMALL = ["ln_ffn1", "ln_mix", "b_forget", "b_merge", "ln_ffn2", "ln_ple", "ln_final"]
WEIGHTS = ["ln_ffn1", "w_ffn1_gate", "w_ffn1_up", "w_ffn1_down", "ln_mix", "w_in", "b_forget", "w_merge", "b_merge",
           "w_ret_out", "w_fox_out", "w_out", "ln_ffn2", "w_ffn2_gate", "w_ffn2_up", "w_ffn2_down", "ln_ple",
           "w_ple", "w_ple_gate", "ln_final"]


TRANSPOSED = {"w_ffn1_gate", "w_ffn1_up", "w_ffn2_gate", "w_ffn2_up", "w_in"}
IN_ROWS_PAD = -(-(IN_COLS // N_CHIPS) // 32) * 32


def _pack_small(vals, loss_row):
    rows = [loss_row]
    for name in SMALL:
        v = vals[name].reshape(-1)
        n = -(-v.shape[0] // 128) * 128
        rows.append(jnp.pad(v, (0, n - v.shape[0])).reshape(n // 128, 128))
    packed = jnp.concatenate(rows, axis=0)
    pad = -packed.shape[0] % 8
    return jnp.pad(packed, ((0, pad), (0, 0)))


def _unpack_small(packed, sizes):
    out, r = {}, 1
    for name in SMALL:
        n = sizes[name]
        nr = -(-n // 128)
        out[name] = packed[r:r + nr].reshape(1, nr * 128)[:, :n]
        r += nr
    return out


class _Stage:
    def __init__(self, comm, finish):
        self.comm, self.finish, self.result = comm, finish, None


def _hosted(fn, *a, stages=()):
    if not stages:
        return fn(*a)
    outs, couts = fn(*a, comm=_merge([st.comm for st in stages]))
    for st, o in zip(stages, _split_outs([st.comm for st in stages], couts)):
        st.result = st.finish(o)
    return outs


class _Reducer:
    def __init__(self):
        self.done = {}

    def swap(self, grads):
        names = list(grads)
        return _Stage(_c_half_swap([grads[n] for n in names]),
                      lambda outs: dict(zip(names, _add_halves([(grads[n], o) for n, o in zip(names, outs)]))))

    def exchange(self, parts):
        names = list(parts)
        return _Stage(_c_chip_exchange([parts[n] for n in names]),
                      lambda outs: dict(zip(names, _sum_chips([(parts[n], o) for n, o in zip(names, outs)]))))

    def join(self, halves):
        names = list(halves)
        return _Stage(_c_join([halves[n] for n in names]),
                      lambda outs: self.done.update({n: (halves[n], o) for n, o in zip(names, outs)}))


def kernel(x, p, positions, ln_ffn1, w_ffn1_gate, w_ffn1_up, w_ffn1_down, ln_mix, w_in, b_forget, w_merge, b_merge, w_ret_out, w_fox_out, w_out, ln_ffn2, w_ffn2_gate, w_ffn2_up, w_ffn2_down, ln_ple, w_ple, w_ple_gate, ln_final, loss_target, m_ln_ffn1, m_w_ffn1_gate, m_w_ffn1_up, m_w_ffn1_down, m_ln_mix, m_w_in, m_b_forget, m_w_merge, m_b_merge, m_w_ret_out, m_w_fox_out, m_w_out, m_ln_ffn2, m_w_ffn2_gate, m_w_ffn2_up, m_w_ffn2_down, m_ln_ple, m_w_ple, m_w_ple_gate, m_ln_final, v_ln_ffn1, v_w_ffn1_gate, v_w_ffn1_up, v_w_ffn1_down, v_ln_mix, v_w_in, v_b_forget, v_w_merge, v_b_merge, v_w_ret_out, v_w_fox_out, v_w_out, v_ln_ffn2, v_w_ffn2_gate, v_w_ffn2_up, v_w_ffn2_down, v_ln_ple, v_w_ple, v_w_ple_gate, v_ln_final):
    args = dict(locals())
    w = {n: args[n] for n in WEIGHTS}
    m = {n: args["m_" + n] for n in WEIGHTS}
    v = {n: args["v_" + n] for n in WEIGHTS}
    d = x.shape[-1]
    t_tok = x.shape[1]
    xs, ps, target = x[0], p[0, 0], loss_target[0]
    small = {n: w[n].reshape(1, -1) for n in SMALL}

    def to2d(n, a):
        if n in TRANSPOSED:
            return a[0].T
        return a.reshape(a.shape[-2], a.shape[-1]) if a.ndim == 3 else a.reshape(1, -1)

    def from2d(n, a):
        return a.T[None] if n in TRANSPOSED else a.reshape(w[n].shape)

    def padded(n, a):
        return jnp.pad(a, ((0, IN_ROWS_PAD - a.shape[0]), (0, 0))) if n == "w_in" else a

    core = lax.axis_index("c")
    me = 2 * lax.axis_index("x") + lax.axis_index("y")
    shard = {}
    for n in BIG:
        s2 = padded(n, to2d(n, w[n]).astype(BF))
        shard[n] = s2.reshape(1, 2, s2.shape[0] // 2, s2.shape[1])
    full = {}

    def gather(names):
        bufs = [lax.dynamic_update_slice(jnp.zeros((N_CHIPS,) + shard[n].shape[1:], BF), shard[n], (me, 0, 0, 0))
                for n in names]

        def finish(outs):
            full.update({n: o.reshape(N_CHIPS, 2 * o.shape[2], o.shape[3]) for n, o in zip(names, outs)})

        return _Stage(_c_all_gather(bufs), finish)

    half = RET_DIM // 2
    inv_freq = 1.0 / (ROPE_BASE ** (jnp.arange(half, dtype=F32) / half))
    cos_t, sin_t = _hosted(_rope_tables, positions[0].astype(F32).reshape(t_tok, 1),
                           jnp.repeat(inv_freq, 2).reshape(1, RET_DIM),
                           stages=[gather(["w_ffn1_gate", "w_ffn1_up", "w_ffn1_down"])])
    consts = _ret_consts()
    b_pad = jnp.pad(small["b_forget"], ((0, 0), (0, 128 - FOX_HEADS)))

    h1, n1, g1, u1 = _hosted(
        _ffn_fwd, xs, small["ln_ffn1"], full["w_ffn1_gate"], full["w_ffn1_up"], full["w_ffn1_down"],
        stages=[gather(["w_in", "w_merge", "w_ret_out", "w_fox_out", "w_out", "w_ple_gate", "w_ple"])])
    w_in_full = jnp.pad(full["w_in"][:, :IN_COLS // N_CHIPS].reshape(IN_COLS, d), ((0, IN_PAD - IN_COLS), (0, 0)))
    u, rq, rk, rv, rg, fq, fk, fv, ffl, ga, gb = _mix_in(
        h1, small["ln_mix"], w_in_full, full["w_merge"], small["b_merge"], cos_t, sin_t)
    aq, ak = _forget_fwd(ffl, b_pad)
    y_raw, y_ret, states = _ret_fwd(rq, rk, rv, rg, consts)
    y_fox, y_fox32, aqb = _hosted(_fox_fwd, fq, fk, fv, aq, ak,
                                  stages=[gather(["w_ffn2_gate", "w_ffn2_up", "w_ffn2_down"])])
    h2, za, zb, mix = _mix_out(h1, y_ret, y_fox, ga, gb, full["w_ret_out"], full["w_fox_out"], full["w_out"])
    h3, n2, g2, u2 = _ffn_fwd(h2, small["ln_ffn2"], full["w_ffn2_gate"], full["w_ffn2_up"], full["w_ffn2_down"])

    red = _Reducer()
    dh3, n3, dpgpre, dpe, pb, loss, dln_final, dln_ple = _tail(
        h3, ps, target, small["ln_ple"], small["ln_final"], full["w_ple_gate"], full["w_ple"])
    g_f2 = dict(w_ple=_wgrad_cols("wgrad_ple", pb, dpe, N_CHIPS))
    dh2, dln_ffn2, dg2, du2, a2, dhb3 = _ffn_bwd(
        dh3, h2, small["ln_ffn2"], g2, u2, full["w_ffn2_gate"], full["w_ffn2_up"], full["w_ffn2_down"])
    g_f2["w_ffn2_gate"] = _wgrad_b_shared("wgrad_ffn2_gate", dg2, n2)
    g_f2["w_ffn2_up"] = _wgrad_b_shared("wgrad_ffn2_up", du2, n2)
    g_f2["w_ffn2_down"] = _wgrad_b_shared("wgrad_ffn2_down", a2, dhb3)

    sw_f2 = red.swap(g_f2)
    dhb2, dgpre, dza, dzb, dy_ret, dy_fox, ad, db_merge = _hosted(
        _mix_out_bwd, dh2, za, zb, ga, gb, y_fox32, full["w_ret_out"], full["w_fox_out"], full["w_out"],
        stages=[sw_f2])
    g_br = dict(w_ret_out=_wgrad_cols("wgrad_ret_out", y_ret, dza, N_CHIPS),
                w_fox_out=_wgrad_cols("wgrad_fox_out", y_fox, dzb, N_CHIPS))

    sw_br = red.swap(g_br)
    drq, drk, drv, drg = _hosted(_ret_bwd, rq, rk, rv, rg, y_raw, dy_ret, states, consts, cos_t, sin_t,
                                 stages=[sw_br])
    ex_f2, ex_br = red.exchange(sw_f2.result), red.exchange(sw_br.result)
    dfq, dfk, dfv, dcum_t3, dcum_q = _hosted(_fox_bwd, fq, fk, fv, dy_fox, aqb, ak, ad, stages=[ex_f2, ex_br])
    dff, db_forget = _forget_bwd(dcum_t3.reshape(FOX_HEADS, t_tok), dcum_q, ffl, b_pad)
    dh1, dln_mix, dproj = _hosted(
        _mix_in_bwd, dh2, h1, small["ln_mix"], (drq, drk, drv, drg, dfq, dfk, dfv), dff, dgpre, w_in_full,
        full["w_merge"], stages=[red.join(ex_f2.result), red.join(ex_br.result)])

    results = {}
    for names in (["w_ffn2_gate", "w_ffn2_up", "w_ffn2_down"], ["w_ret_out", "w_fox_out"], ["w_ple"]):
        res = _sc_adamw_halves([(to2d(n, w[n]), *red.done[n], to2d(n, m[n]), to2d(n, v[n])) for n in names])
        for q, n in enumerate(names):
            results[n] = tuple(from2d(n, a) for a in res[4 * q:4 * q + 4])

    dx, dln_ffn1, dg1, du1, a1, dhb1 = _ffn_bwd(
        dh1, xs, small["ln_ffn1"], g1, u1, full["w_ffn1_gate"], full["w_ffn1_up"], full["w_ffn1_down"])
    g_f1g = _wgrad_b_shared("wgrad_ffn1_gate", dg1, n1)
    sw_f1g = red.swap(dict(w_ffn1_gate=g_f1g))
    g_f1u = _hosted(_wgrad_b_shared, "wgrad_ffn1_up", du1, n1, stages=[sw_f1g])
    ex_f1g, sw_f1u = red.exchange(sw_f1g.result), red.swap(dict(w_ffn1_up=g_f1u))
    g_f1d = _hosted(_wgrad_b_shared, "wgrad_ffn1_down", a1, dhb1, stages=[ex_f1g, sw_f1u])

    ex_f1u, sw_f1d = red.exchange(sw_f1u.result), red.swap(dict(w_ffn1_down=g_f1d))
    g_in = _hosted(_wgrad_rows, "wgrad_in", dproj, u, IN_PAD // 512,
                   stages=[ex_f1u, sw_f1d, red.join(ex_f1g.result)])
    g_in = g_in.reshape(IN_PAD, d)[:IN_COLS].reshape(N_CHIPS, IN_COLS // N_CHIPS, d)
    g_in = jnp.pad(g_in, ((0, 0), (0, IN_ROWS_PAD - IN_COLS // N_CHIPS), (0, 0)))
    ex_f1d, sw_in = red.exchange(sw_f1d.result), red.swap(dict(w_in=g_in))
    g_mrg = _hosted(_wgrad_cols, "wgrad_merge", u, dgpre, N_CHIPS,
                    stages=[ex_f1d, sw_in, red.join(ex_f1u.result)])

    small_grads = dict(ln_ffn1=dln_ffn1, ln_mix=dln_mix, b_forget=db_forget[:, :FOX_HEADS], b_merge=db_merge,
                       ln_ffn2=dln_ffn2, ln_ple=dln_ple, ln_final=dln_final)
    sizes = {n: w[n].size for n in SMALL}
    ex_in, sw_mrg = red.exchange(sw_in.result), red.swap(dict(w_merge=g_mrg))
    g_out = _hosted(_wgrad_rows, "wgrad_out", mix, dhb2, N_CHIPS, stages=[ex_in, sw_mrg, red.join(ex_f1d.result)])
    ex_mrg, sw_out = red.exchange(sw_mrg.result), red.swap(dict(w_out=g_out))
    g_pg = _hosted(_wgrad_rows, "wgrad_ple_gate", n3, dpgpre, N_CHIPS,
                   stages=[ex_mrg, sw_out, red.join(ex_in.result)])
    ex_out, sw_pg = red.exchange(sw_out.result), red.swap(dict(w_ple_gate=g_pg))
    reduced = _hosted(_all_reduce_small, _pack_small(small_grads, loss),
                      stages=[ex_out, sw_pg, red.join(ex_mrg.result)])
    gsum = _unpack_small(reduced, sizes)
    loss = reduced[0, 0]
    ex_pg = red.exchange(sw_pg.result)
    _hosted(_exchange_only, stages=[ex_pg, red.join(ex_out.result)])
    _hosted(_exchange_only, stages=[red.join(ex_pg.result)])

    def update(names):
        w2, m2, v2 = ([to2d(n, a[n]) for n in names] for a in (w, m, v))
        n = names[0]
        if n == "w_in":
            mine, other = red.done[n]
            g2 = jnp.where(core == 0, jnp.concatenate([mine, other]), jnp.concatenate([other, mine]))
            g2 = g2[:w2[0].shape[0]]
            rows3 = lambda a: jnp.transpose(a, (2, 0, 1))
            g3 = g2.reshape(g2.shape[0], 1, g2.shape[1])
            res = [g3] + _adamw(rows3(w[n]), g3, rows3(m[n]), rows3(v[n]))
            results[n] = tuple(jnp.transpose(a, (1, 2, 0)) for a in res)
            return
        res = _adamw_halves([(w2[q], *red.done[names[q]], m2[q], v2[q]) for q in range(len(names))])
        for q, name in enumerate(names):
            results[name] = tuple(from2d(name, a) for a in res[4 * q:4 * q + 4])

    res = _adamw_vectors([(to2d(n, w[n]), gsum[n], to2d(n, m[n]), to2d(n, v[n])) for n in SMALL])
    for q, n in enumerate(SMALL):
        results[n] = tuple(from2d(n, a) for a in [gsum[n]] + res[3 * q:3 * q + 3])
    update(["w_ffn1_gate", "w_ffn1_up", "w_ffn1_down"])
    update(["w_out", "w_ple_gate"])
    for n in WEIGHTS:
        if n not in results:
            update([n])

    outs = [[results[n][k] for n in WEIGHTS] for k in range(4)]
    return (loss, dx[None], *outs[0], *outs[1], *outs[2], *outs[3])
```

```python
import functools
import operator

import jax
import jax.numpy as jnp
from jax import lax
from jax.experimental import pallas as pl
from jax.experimental.pallas import tpu as pltpu
from jax.experimental.pallas import tpu_sc as plsc

F32 = jnp.float32
BF = jnp.bfloat16
MESH = pl.DeviceIdType.MESH

EPS = 1e-6
ROPE_BASE = 10000.0
N_CHIPS = 4
RET_HEADS = 4
RET_DIM = 128
RET_WIDTH = RET_HEADS * RET_DIM
RET_CHUNK = 128
RET_SCALE = RET_DIM ** -0.5
FOX_HEADS = 8
FOX_DIM = 64
FOX_WIDTH = FOX_HEADS * FOX_DIM
FOX_SCALE = FOX_DIM ** -0.5
IN_COLS = 4 * RET_WIDTH + 3 * FOX_WIDTH + FOX_HEADS
IN_PAD = 4096
FF_COL = 4 * RET_WIDTH + 3 * FOX_WIDTH
NEG = -1e30

ADAM_LR = 0.001
ADAM_B1 = 0.9
ADAM_B2 = 0.999
ADAM_EPS = 1e-08
ADAM_WD = 0.01
ADAM_STEP = 10

VMEM_LIMIT = 52 * 1024 * 1024

RELAY_MIN_STEPS = 16

NT = (((1,), (1,)), ((), ()))
TN = (((0,), (0,)), ((), ()))

HBM_SPEC = pl.BlockSpec(memory_space=pltpu.HBM)
VMEM_SPEC = pl.BlockSpec(memory_space=pltpu.VMEM)


def _dot(a, b):
    return jnp.dot(a, b, preferred_element_type=F32)


def _dot_nt(a, b):
    return lax.dot_general(a, b, NT, preferred_element_type=F32)


def _dot_tn(a, b):
    return lax.dot_general(a, b, TN, preferred_element_type=F32)


def _rstd(xv):
    return lax.rsqrt(jnp.mean(xv * xv, axis=-1, keepdims=True) + EPS)


def _rms_bwd(dn, xv, r, ln):
    xh = xv * r
    dxh = dn * ln
    dx = r * (dxh - xh * jnp.mean(dxh * xh, axis=-1, keepdims=True))
    return dx, jnp.sum(dn * xh, axis=0, keepdims=True)


def _sigmoid(x):
    return jax.nn.sigmoid(x)


def _tile(n, pref):
    return pref if n % pref == 0 else n


def _row_tile(n, cap):
    best = [t for t in range(16, min(n, cap) + 1, 16) if n % t == 0]
    return best[-1] if best else n


class _Comm:
    def __init__(self, ins, out_shapes, sems, start, wait, aliases=None, relay=None):
        self.ins, self.out_shapes, self.sems, self.start, self.wait = list(ins), list(out_shapes), list(sems), start, wait
        self.aliases = dict(aliases or {})
        self.relay = relay


def _merge(comms):
    comms = [c for c in comms if c is not None]
    if not comms:
        return None
    bounds, ni, no, ns = [], 0, 0, 0
    for c in comms:
        bounds.append((ni, no, ns))
        ni, no, ns = ni + len(c.ins), no + len(c.out_shapes), ns + len(c.sems)

    def run(which):
        def f(ins, outs, sems, **kw):
            for c, (i, o, s) in zip(comms, bounds):
                fn = getattr(c, which)
                if fn is not None:
                    fn(ins[i:i + len(c.ins)], outs[o:o + len(c.out_shapes)], sems[s:s + len(c.sems)],
                       **(kw if c.relay is not None else {}))
        return f

    aliases = {i + a: o + b for c, (i, o, _) in zip(comms, bounds) for a, b in c.aliases.items()}
    relay = run("relay") if any(c.relay is not None for c in comms) else None
    return _Comm([a for c in comms for a in c.ins], [a for c in comms for a in c.out_shapes],
                 [a for c in comms for a in c.sems], run("start"), run("wait"), aliases, relay)


def _split_outs(comms, outs):
    res, o = [], 0
    for c in comms:
        if c is not None:
            res.append(list(outs[o:o + len(c.out_shapes)]))
            o += len(c.out_shapes)
    return res


def _pcall(body, args, *, name, out_shape, grid=(), in_specs=None, out_specs=None, scratch=(), comm=None,
           prefetch=()):
    many = isinstance(out_shape, (list, tuple))
    outs = list(out_shape) if many else [out_shape]
    n_pre, n_in, n_out, n_scr = len(prefetch), len(args), len(outs), len(scratch)
    if in_specs is None:
        in_specs, out_specs = [VMEM_SPEC] * n_in, [VMEM_SPEC] * n_out
    else:
        in_specs, out_specs = list(in_specs), (list(out_specs) if many else [out_specs])
    params = pltpu.CompilerParams(dimension_semantics=("arbitrary",) * len(grid), vmem_limit_bytes=VMEM_LIMIT)
    scalars = [jnp.reshape(s, (1,)).astype(jnp.int32) for s in prefetch]
    ci, co = (len(comm.ins), len(comm.out_shapes)) if comm is not None else (0, 0)

    def wrapped(*refs):
        pre, refs = refs[:n_pre], refs[n_pre:]
        a, ca = refs[:n_in], refs[n_in:n_in + ci]
        o = refs[n_in + ci:n_in + ci + n_out]
        cout = refs[n_in + ci + n_out:n_in + ci + n_out + co]
        s = refs[n_in + ci + n_out + co:n_in + ci + n_out + co + n_scr]
        csem = refs[n_in + ci + n_out + co + n_scr:]
        if comm is None:
            body(*pre, *a, *o, *s)
        elif grid:
            step = functools.reduce(lambda acc, k: acc * grid[k] + pl.program_id(k), range(len(grid)), 0)
            n_steps = functools.reduce(operator.mul, grid)
            relayed = comm.relay is not None and n_steps >= RELAY_MIN_STEPS
            pl.when(step == 0)(lambda: comm.start(ca, cout, csem))
            if relayed:
                pl.when(step == n_steps - n_steps // 8)(lambda: comm.relay(ca, cout, csem))
            body(*pre, *a, *o, *s)
            pl.when(step == n_steps - 1)(lambda: comm.wait(ca, cout, csem, **({"relayed": True} if relayed else {})))
        else:
            comm.start(ca, cout, csem)
            body(*pre, *a, *o, *s)
            comm.wait(ca, cout, csem)

    c_ins, c_outs, c_sems, aliases = ([], [], [], {}) if comm is None else (
        comm.ins, comm.out_shapes, comm.sems, {n_pre + n_in + i: n_out + o for i, o in comm.aliases.items()})
    all_in, all_out = in_specs + [HBM_SPEC] * ci, out_specs + [HBM_SPEC] * co
    all_scr = list(scratch) + c_sems
    if grid:
        args = [pltpu.with_memory_space_constraint(a, pltpu.HBM) for a in args]
    c_ins = [pltpu.with_memory_space_constraint(a, pltpu.HBM) for a in c_ins]
    if n_pre:
        spec = dict(grid_spec=pltpu.PrefetchScalarGridSpec(
            num_scalar_prefetch=n_pre, grid=grid, in_specs=all_in, out_specs=all_out, scratch_shapes=all_scr))
    else:
        spec = dict(grid=grid, in_specs=all_in, out_specs=all_out, scratch_shapes=all_scr)
    res = pl.pallas_call(wrapped, name=name, out_shape=outs + c_outs, input_output_aliases=aliases,
                         compiler_params=params, **spec)(*scalars, *args, *c_ins)
    mine = list(res[:n_out])
    mine = mine if many else mine[0]
    return mine if comm is None else (mine, list(res[n_out:]))


def _peer_chips(x, y):
    return [(1 - x, y), (x, 1 - y), (1 - x, 1 - y)]


def _c_all_gather(bufs):
    n = len(bufs)

    def copies(ins, outs, sems):
        send_sems, recv_sems, fwd_send, fwd_recv = sems
        x, y, c = lax.axis_index("x"), lax.axis_index("y"), lax.axis_index("c")
        me = 2 * x + y
        peers = _peer_chips(x, y)
        chip = [2 * px + py for px, py in peers]

        def ici(g, j, slot):
            return pltpu.make_async_remote_copy(
                src_ref=outs[g].at[me, c], dst_ref=outs[g].at[slot, c], send_sem=send_sems.at[g, j],
                recv_sem=recv_sems.at[g, j], device_id=(*peers[j], c), device_id_type=MESH)

        def d2d(g, j, half):
            return pltpu.make_async_remote_copy(
                src_ref=outs[g].at[chip[j], half], dst_ref=outs[g].at[chip[j], half], send_sem=fwd_send.at[g, j],
                recv_sem=fwd_recv.at[g, j], device_id=(x, y, 1 - c), device_id_type=MESH)

        pairs = [(g, j) for g in range(n) for j in range(3)]
        sends = [ici(g, j, me) for g, j in pairs]
        recvs = [ici(g, j, chip[j]) for g, j in pairs]
        passes = [d2d(g, j, c) for g, j in pairs]
        passed = [d2d(g, j, 1 - c) for g, j in pairs]
        return sends, recvs, passes, passed

    def start(ins, outs, sems):
        for cp in copies(ins, outs, sems)[0]:
            cp.start()

    def relay(ins, outs, sems):
        _, recvs, passes, _ = copies(ins, outs, sems)
        for rcv, fwd in zip(recvs, passes):
            rcv.wait_recv()
            fwd.start()

    def wait(ins, outs, sems, relayed=False):
        if not relayed:
            relay(ins, outs, sems)
        sends, _, passes, passed = copies(ins, outs, sems)
        for cp in passed:
            cp.wait_recv()
        for cp in sends + passes:
            cp.wait_send()

    pair_sems = pltpu.SemaphoreType.DMA((n, 3))
    return _Comm(bufs, [jax.ShapeDtypeStruct(s.shape, s.dtype) for s in bufs], [pair_sems] * 4, start, wait,
                 aliases={g: g for g in range(n)}, relay=relay)


def _start_wait(copies):
    def start(ins, outs, sems):
        local, sends, _ = copies(ins, outs, sems)
        for cp in local + sends:
            cp.start()

    def wait(ins, outs, sems):
        local, sends, recvs = copies(ins, outs, sems)
        for cp in recvs:
            cp.wait_recv()
        for cp in sends:
            cp.wait_send()
        for cp in local:
            cp.wait()

    return start, wait


def _c_half_swap(grads):
    n = len(grads)

    def copies(ins, outs, sems):
        send_sems, recv_sems = sems
        x, y, c = lax.axis_index("x"), lax.axis_index("y"), lax.axis_index("c")
        sends = []
        for g in range(n):
            half = ins[g].shape[1] // 2
            sends.append(pltpu.make_async_remote_copy(
                src_ref=ins[g].at[:, pl.ds((1 - c) * half, half), :], dst_ref=outs[g],
                send_sem=send_sems.at[g], recv_sem=recv_sems.at[g], device_id=(x, y, 1 - c), device_id_type=MESH))
        return [], sends, sends

    return _Comm(
        grads, [jax.ShapeDtypeStruct((N_CHIPS, s.shape[1] // 2, s.shape[2]), s.dtype) for s in grads],
        [pltpu.SemaphoreType.DMA((n,)), pltpu.SemaphoreType.DMA((n,))], *_start_wait(copies))


def _c_chip_exchange(parts):
    n = len(parts)

    def copies(ins, outs, sems):
        send_sems, recv_sems = sems
        x, y, c = lax.axis_index("x"), lax.axis_index("y"), lax.axis_index("c")
        peers = _peer_chips(x, y)

        def remote(g, j):
            return pltpu.make_async_remote_copy(
                src_ref=ins[g].at[2 * peers[j][0] + peers[j][1]], dst_ref=outs[g].at[j],
                send_sem=send_sems.at[g, j], recv_sem=recv_sems.at[g, j], device_id=(*peers[j], c),
                device_id_type=MESH)

        sends = [remote(g, j) for g in range(n) for j in range(3)]
        return [], sends, sends

    return _Comm(
        parts, [jax.ShapeDtypeStruct((3,) + s.shape[1:], s.dtype) for s in parts],
        [pltpu.SemaphoreType.DMA((n, 3)), pltpu.SemaphoreType.DMA((n, 3))], *_start_wait(copies))


def _c_join(halves):
    n = len(halves)

    def copies(ins, outs, sems):
        send_sems, recv_sems = sems
        x, y, c = lax.axis_index("x"), lax.axis_index("y"), lax.axis_index("c")
        sends = [pltpu.make_async_remote_copy(
            src_ref=ins[g], dst_ref=outs[g], send_sem=send_sems.at[g], recv_sem=recv_sems.at[g],
            device_id=(x, y, 1 - c), device_id_type=MESH) for g in range(n)]
        return [], sends, sends

    return _Comm(
        halves, [jax.ShapeDtypeStruct(s.shape, s.dtype) for s in halves],
        [pltpu.SemaphoreType.DMA((n,)), pltpu.SemaphoreType.DMA((n,))], *_start_wait(copies))


def _exchange_only(comm=None):
    def body(o_ref):
        o_ref[...] = jnp.zeros_like(o_ref)

    return _pcall(body, [], name="exchange_only", out_shape=jax.ShapeDtypeStruct((8, 128), F32), comm=comm)


def _all_reduce_small(v, comm=None):
    rows = v.shape[0]

    def body(v_ref, out_ref, buf, send_sems, recv_sems):
        x, y, c = lax.axis_index("x"), lax.axis_index("y"), lax.axis_index("c")
        me = 4 * x + 2 * y + c
        buf[me] = v_ref[...]
        flips = [(fx, fy, fc) for fx in (0, 1) for fy in (0, 1) for fc in (0, 1)][1:]

        def peer(k):
            fx, fy, fc = flips[k]
            px, py, pc = x ^ fx, y ^ fy, c ^ fc
            return (px, py, pc), 4 * px + 2 * py + pc

        def copy(k, slot):
            return pltpu.make_async_remote_copy(
                src_ref=buf.at[slot], dst_ref=buf.at[slot], send_sem=send_sems.at[k],
                recv_sem=recv_sems.at[k], device_id=peer(k)[0], device_id_type=MESH)

        sends = [copy(k, me) for k in range(7)]
        for cp in sends:
            cp.start()
        for k in range(7):
            copy(k, peer(k)[1]).wait_recv()
        for cp in sends:
            cp.wait_send()
        acc = buf[0]
        for d in range(1, 8):
            acc = acc + buf[d]
        out_ref[...] = acc

    return _pcall(body, [v], name="all_reduce_small", out_shape=jax.ShapeDtypeStruct((rows, 128), F32),
                  scratch=[pltpu.VMEM((8, rows, 128), F32), pltpu.SemaphoreType.DMA((7,)),
                           pltpu.SemaphoreType.DMA((7,))], comm=comm)


def _add_halves(pairs):
    k = len(pairs)

    def body(h_ref, *refs):
        for a_ref, b_ref, o_ref in zip(refs[0:2 * k:2], refs[1:2 * k:2], refs[2 * k:]):
            o_ref[...] = (a_ref[...].astype(F32) + b_ref[...].astype(F32)).astype(o_ref.dtype)

    in_specs, out_specs = [], []
    for _, got in pairs:
        _, h, c = got.shape
        spec = pl.BlockSpec((1, h, c), lambda j, h_ref: (j, 0, 0))
        in_specs += [pl.BlockSpec((1, h, c), lambda j, h_ref: (j, h_ref[0], 0)), spec]
        out_specs.append(spec)
    return _pcall(body, [a for pair in pairs for a in pair], name="add_halves", grid=(N_CHIPS,),
                  prefetch=[lax.axis_index("c")], in_specs=in_specs, out_specs=out_specs,
                  out_shape=[jax.ShapeDtypeStruct(got.shape, BF) for _, got in pairs])


def _sum_chips(pairs):
    k = len(pairs)
    n_steps = 2 if all(parts.shape[1] % 32 == 0 for parts, _ in pairs) else 1
    me = 2 * lax.axis_index("x") + lax.axis_index("y")

    def body(me_ref, *refs):
        for p_ref, r_ref, o_ref in zip(refs[0:2 * k:2], refs[1:2 * k:2], refs[2 * k:]):
            acc = p_ref[0].astype(F32)
            for s in range(N_CHIPS - 1):
                acc = acc + r_ref[s].astype(F32)
            o_ref[...] = acc

    in_specs, out_specs = [], []
    for parts, _ in pairs:
        _, h, c = parts.shape
        th = h // n_steps
        in_specs += [pl.BlockSpec((1, th, c), lambda i, me_ref: (me_ref[0], i, 0)),
                     pl.BlockSpec((N_CHIPS - 1, th, c), lambda i, me_ref: (0, i, 0))]
        out_specs.append(pl.BlockSpec((th, c), lambda i, me_ref: (i, 0)))
    return _pcall(body, [a for pair in pairs for a in pair], name="sum_chips", grid=(n_steps,), prefetch=[me],
                  in_specs=in_specs, out_specs=out_specs,
                  out_shape=[jax.ShapeDtypeStruct(parts.shape[1:], F32) for parts, _ in pairs])


def _adam_update(w, gv, m, v, d_ref, nm_ref, nv_ref):
    c1 = 1.0 / (1.0 - ADAM_B1 ** ADAM_STEP)
    c2 = 1.0 / (1.0 - ADAM_B2 ** ADAM_STEP)
    nm = ADAM_B1 * m + (1.0 - ADAM_B1) * gv
    nv = ADAM_B2 * v + (1.0 - ADAM_B2) * (gv * gv)
    nm_ref[...] = nm
    nv_ref[...] = nv
    d_ref[...] = -ADAM_LR * ((nm * c1) / (jnp.sqrt(nv * c2) + ADAM_EPS) + ADAM_WD * w)


def _adamw(w, g, m, v, comm=None):
    r, c = w.shape[0], w.shape[-1]
    tr = _row_tile(r, 512)

    def body(w_ref, g_ref, m_ref, v_ref, d_ref, nm_ref, nv_ref):
        _adam_update(w_ref[...], g_ref[...], m_ref[...], v_ref[...], d_ref, nm_ref, nv_ref)

    mid = (1,) * (w.ndim - 2)
    spec = pl.BlockSpec((tr,) + mid + (c,), lambda i: (i,) + (0,) * (w.ndim - 1))
    sds = jax.ShapeDtypeStruct(w.shape, F32)
    return _pcall(body, [w, g, m, v], name="adamw", grid=(r // tr,), out_shape=[sds, sds, sds],
                  in_specs=[spec] * 4, out_specs=[spec] * 3, comm=comm)


def _adamw_vectors(items):
    k = len(items)

    def body(*refs):
        ins, outs = refs[:4 * k], refs[4 * k:]
        for q in range(k):
            w_ref, g_ref, m_ref, v_ref = ins[4 * q:4 * q + 4]
            _adam_update(w_ref[...], g_ref[...], m_ref[...], v_ref[...], *outs[3 * q:3 * q + 3])

    return _pcall(body, [a for it in items for a in it], name="adamw_vectors",
                  out_shape=[jax.ShapeDtypeStruct(it[0].shape, F32) for it in items for _ in range(3)])


def _adamw_halves(items, comm=None):
    k = len(items)
    r, c = items[0][0].shape
    h = r // 2
    tr = _row_tile(h, min(512, (VMEM_LIMIT * 3 // 4) // (k * 9 * 2 * 4 * c)))
    nb = h // tr
    core = lax.axis_index("c")

    def body(c_ref, *refs):
        ins, outs = refs[:5 * k], refs[5 * k:]
        for q in range(k):
            w_ref, gm_ref, go_ref, m_ref, v_ref = ins[5 * q:5 * q + 5]
            g_ref, d_ref, nm_ref, nv_ref = outs[4 * q:4 * q + 4]
            gv = jnp.where(pl.program_id(0) == c_ref[0], gm_ref[...], go_ref[...])
            g_ref[...] = gv
            _adam_update(w_ref[...], gv, m_ref[...], v_ref[...], d_ref, nm_ref, nv_ref)

    full = pl.BlockSpec((tr, c), lambda hh, i, c_ref: (hh * nb + i, 0))
    half = pl.BlockSpec((tr, c), lambda hh, i, c_ref: (i, 0))
    sds = jax.ShapeDtypeStruct((r, c), F32)
    return _pcall(body, [a for it in items for a in it], name="adamw_halves", grid=(2, nb), prefetch=[core],
                  out_shape=[sds] * (4 * k), in_specs=[full, half, half, full, full] * k, out_specs=[full] * (4 * k),
                  comm=comm)


SC_CORES, SC_TILES, SC_LANES = 2, 16, 16
SC_BLOCK_ROWS, SC_BLOCK_COLS = 8, 512


def _sc_adamw_halves(items):
    k = len(items)
    r, c = items[0][0].shape
    h = r // 2
    bc = min(c, SC_BLOCK_COLS)
    c1 = 1.0 / (1.0 - ADAM_B1 ** ADAM_STEP)
    c2 = 1.0 / (1.0 - ADAM_B2 ** ADAM_STEP)
    mesh = plsc.VectorSubcoreMesh(core_axis_name="sc_core", subcore_axis_name="sc_tile",
                                  num_cores=SC_CORES, num_subcores=SC_TILES)
    spec = pl.BlockSpec(block_shape=(SC_BLOCK_ROWS, bc), index_map=lambda i, j: (i, j))

    def block(w_v, gin_v, m_v, v_v, g_v, d_v, nm_v, nv_v):
        @pl.loop(0, SC_BLOCK_ROWS)
        def _(row):
            @pl.loop(0, bc, step=SC_LANES)
            def _(col):
                at = (pl.ds(row, 1), pl.ds(col, SC_LANES))
                gv = gin_v.at[*at][...]
                nm = ADAM_B1 * m_v.at[*at][...] + (1.0 - ADAM_B1) * gv
                nv = ADAM_B2 * v_v.at[*at][...] + (1.0 - ADAM_B2) * (gv * gv)
                g_v.at[*at][...] = gv
                nm_v.at[*at][...] = nm
                nv_v.at[*at][...] = nv
                d_v.at[*at][...] = -ADAM_LR * ((nm * c1) / (jnp.sqrt(nv * c2) + ADAM_EPS) + ADAM_WD * w_v.at[*at][...])

    def kern(*refs):
        ins, outs = refs[:5 * k], refs[5 * k:]
        core = lax.axis_index("c")

        def half(q, hh, mine):
            w_hbm, gm_hbm, go_hbm, m_hbm, v_hbm = ins[5 * q:5 * q + 5]
            rows = pl.ds(hh * h, h)
            pltpu.emit_pipeline(
                block, grid=(h // SC_BLOCK_ROWS, c // bc), in_specs=[spec] * 4, out_specs=[spec] * 4,
                core_axis_name=("sc_core", "sc_tile"), dimension_semantics=(pltpu.PARALLEL, pltpu.PARALLEL),
                trace_scopes=False,
            )(w_hbm.at[rows, :], gm_hbm if mine else go_hbm, m_hbm.at[rows, :], v_hbm.at[rows, :],
              *(o.at[rows, :] for o in outs[4 * q:4 * q + 4]))

        for q in range(k):
            for hh in range(2):
                pl.when(core == hh)(lambda q=q, hh=hh: half(q, hh, True))
                pl.when(core != hh)(lambda q=q, hh=hh: half(q, hh, False))

    sds = jax.ShapeDtypeStruct((r, c), F32)
    return pl.kernel(kern, out_type=[sds] * (4 * k), mesh=mesh, scratch_types=[], name="sc_adamw_halves")(
        *(a for it in items for a in it))


def _wgrad(name, a, b, a_spec, b_spec, m, n, nb, comm):
    def body(a_ref, b_ref, o_ref):
        o_ref[...] = _dot_tn(a_ref[...], b_ref[...]).astype(o_ref.dtype)

    return _pcall(body, [a, b], name=name, grid=(nb,), out_shape=jax.ShapeDtypeStruct((nb, m, n), BF),
                  in_specs=[a_spec, b_spec], out_specs=pl.BlockSpec((None, m, n), lambda j: (j, 0, 0)), comm=comm)


def _wgrad_cols(name, a, b, nb, comm=None):
    t_tok, m = a.shape
    n = b.shape[1] // nb
    return _wgrad(name, a, b, pl.BlockSpec((t_tok, m), lambda j: (0, 0)), pl.BlockSpec((t_tok, n), lambda j: (0, j)),
                  m, n, nb, comm)


def _wgrad_rows(name, a, b, nb, comm=None):
    t_tok, n = b.shape
    m = a.shape[1] // nb
    return _wgrad(name, a, b, pl.BlockSpec((t_tok, m), lambda j: (0, j)), pl.BlockSpec((t_tok, n), lambda j: (0, 0)),
                  m, n, nb, comm)


def _wgrad_a_shared(name, a, b4, comm=None):
    t_tok, m = a.shape
    nb, _, n = b4.shape
    return _wgrad(name, a, b4, pl.BlockSpec((t_tok, m), lambda j: (0, 0)),
                  pl.BlockSpec((None, t_tok, n), lambda j: (j, 0, 0)), m, n, nb, comm)


def _wgrad_b_shared(name, a4, b, comm=None):
    nb, t_tok, m = a4.shape
    n = b.shape[1]
    return _wgrad(name, a4, b, pl.BlockSpec((None, t_tok, m), lambda j: (j, 0, 0)),
                  pl.BlockSpec((t_tok, n), lambda j: (0, 0)), m, n, nb, comm)


def _w4_spec(r, c):
    return pl.BlockSpec((None, r, c), lambda i, j: (j, 0, 0))


FFN_ROW_CHUNK = 256


def _row_chunks(tm):
    rc = FFN_ROW_CHUNK if tm % FFN_ROW_CHUNK == 0 else tm
    return [slice(r, r + rc) for r in range(0, tm, rc)]


def _ffn_fwd(h, ln, wg4, wu4, wd4, comm=None):
    t_tok, d = h.shape
    f = wg4.shape[-2]
    tm = _tile(t_tok, 512)

    def body(h_ref, ln_ref, wg_ref, wu_ref, wd_ref, ho_ref, n_ref, g_ref, u_ref, n_s, acc):
        j = pl.program_id(1)

        @pl.when(j == 0)
        def _():
            xv = h_ref[...]
            nv = (xv * _rstd(xv) * ln_ref[...]).astype(BF)
            n_s[...] = nv
            n_ref[...] = nv
            acc[...] = jnp.zeros_like(acc)

        nv = n_s[...]
        g = _dot_nt(nv, wg_ref[...])
        u = _dot_nt(nv, wu_ref[...])
        g_ref[...] = g.astype(BF)
        u_ref[...] = u.astype(BF)
        a = (g * _sigmoid(g) * u).astype(BF)
        acc[...] += _dot(a, wd_ref[...])

        @pl.when(j == N_CHIPS - 1)
        def _():
            ho_ref[...] = h_ref[...] + 0.5 * acc[...]

    row = pl.BlockSpec((tm, d), lambda i, j: (i, 0))
    gu = pl.BlockSpec((None, tm, f), lambda i, j: (j, i, 0))
    gu_sds = jax.ShapeDtypeStruct((N_CHIPS, t_tok, f), BF)
    return _pcall(
        body, [h, ln, wg4, wu4, wd4], name="ffn_fwd", grid=(t_tok // tm, N_CHIPS),
        out_shape=[jax.ShapeDtypeStruct((t_tok, d), F32), jax.ShapeDtypeStruct((t_tok, d), BF), gu_sds, gu_sds],
        in_specs=[row, pl.BlockSpec((1, d), lambda i, j: (0, 0)), _w4_spec(f, d), _w4_spec(f, d), _w4_spec(f, d)],
        out_specs=[row, row, gu, gu],
        scratch=[pltpu.VMEM((tm, d), BF), pltpu.VMEM((tm, d), F32)], comm=comm)


def _ffn_bwd(dho, h, ln, g4, u4, wg4, wu4, wd4, comm=None):
    t_tok, d = h.shape
    f = wg4.shape[-2]
    tm = _tile(t_tok, 512)

    def body(dho_ref, h_ref, ln_ref, g_ref, u_ref, wg_ref, wu_ref, wd_ref,
             dhi_ref, dln_ref, dg_ref, du_ref, a_ref, dhb_ref, dhb_s, dn_acc):
        i, j = pl.program_id(0), pl.program_id(1)

        @pl.when(j == 0)
        def _():
            dhb = (0.5 * dho_ref[...]).astype(BF)
            dhb_s[...] = dhb
            dhb_ref[...] = dhb
            dn_acc[...] = jnp.zeros_like(dn_acc)

        @pl.when((i == 0) & (j == 0))
        def _():
            dln_ref[...] = jnp.zeros_like(dln_ref)

        for rows in _row_chunks(tm):
            g = g_ref[rows, :].astype(F32)
            u = u_ref[rows, :].astype(F32)
            s = _sigmoid(g)
            sg = g * s
            a_ref[rows, :] = (sg * u).astype(BF)
            da = _dot_nt(dhb_s[rows, :], wd_ref[...])
            dg = (da * u * (s * (1.0 + g * (1.0 - s)))).astype(BF)
            du = (da * sg).astype(BF)
            dg_ref[rows, :] = dg
            du_ref[rows, :] = du
            dn_acc[rows, :] += _dot(dg, wg_ref[...]) + _dot(du, wu_ref[...])

        @pl.when(j == N_CHIPS - 1)
        def _():
            xv = h_ref[...]
            dx, dln = _rms_bwd(dn_acc[...], xv, _rstd(xv), ln_ref[...])
            dln_ref[...] += dln
            dhi_ref[...] = dho_ref[...] + dx

    row = pl.BlockSpec((tm, d), lambda i, j: (i, 0))
    vec = pl.BlockSpec((1, d), lambda i, j: (0, 0))
    gu = pl.BlockSpec((None, tm, f), lambda i, j: (j, i, 0))
    gu_sds = jax.ShapeDtypeStruct((N_CHIPS, t_tok, f), BF)
    return _pcall(
        body, [dho, h, ln, g4, u4, wg4, wu4, wd4], name="ffn_bwd", grid=(t_tok // tm, N_CHIPS),
        out_shape=[jax.ShapeDtypeStruct((t_tok, d), F32), jax.ShapeDtypeStruct((1, d), F32),
                   gu_sds, gu_sds, gu_sds, jax.ShapeDtypeStruct((t_tok, d), BF)],
        in_specs=[row, row, vec, gu, gu, _w4_spec(f, d), _w4_spec(f, d), _w4_spec(f, d)],
        out_specs=[row, vec, gu, gu, gu, row],
        scratch=[pltpu.VMEM((tm, d), BF), pltpu.VMEM((tm, d), F32)], comm=comm)


def _rope_tables(pos_col, inv_freq2, blocks, comm=None):
    t_tok = pos_col.shape[0]
    k = len(blocks)
    n_steps = 4
    assert all(b.shape[0] % (16 * n_steps) == 0 for b in blocks)

    def body(p_ref, f_ref, *refs):
        cos_ref, sin_ref = refs[k:k + 2]
        ang = p_ref[...] * f_ref[...]
        lane = lax.broadcasted_iota(jnp.int32, ang.shape, 1)
        s = jnp.sin(ang)
        cos_ref[...] = jnp.cos(ang)
        sin_ref[...] = jnp.where((lane & 1) == 0, -s, s)
        for a_ref, o_ref in zip(refs[:k], refs[k + 2:]):
            o_ref[...] = a_ref[...].astype(BF)

    rows = lambda r, c: pl.BlockSpec((r // n_steps, c), lambda i: (i, 0))
    casts = [rows(*b.shape) for b in blocks]
    sds = jax.ShapeDtypeStruct((t_tok, 128), F32)
    return _pcall(body, [pos_col, inv_freq2, *blocks], name="rope_tables", grid=(n_steps,),
                  out_shape=[sds, sds] + [jax.ShapeDtypeStruct(b.shape, BF) for b in blocks],
                  in_specs=[rows(t_tok, 1), pl.BlockSpec((1, 128), lambda i: (0, 0))] + casts,
                  out_specs=[rows(t_tok, 128)] * 2 + casts, comm=comm)


def _swap_pairs(x):
    lane = lax.broadcasted_iota(jnp.int32, x.shape, 1)
    return jnp.where((lane & 1) == 0, pltpu.roll(x, 127, 1), pltpu.roll(x, 1, 1))


def _mix_in(h, ln, w_in, wm4, b_m, cos_t, sin_t, comm=None):
    t_tok, d = h.shape
    cm = wm4.shape[-1]
    tm = _tile(t_tok, 256)

    def body(h_ref, ln_ref, win_ref, wm_ref, bm_ref, cos_ref, sin_ref,
             u_ref, rq_ref, rk_ref, rv_ref, rg_ref, fq_ref, fk_ref, fv_ref, ff_ref, ga_ref, gb_ref):
        xv = h_ref[...]
        ub = (xv * _rstd(xv) * ln_ref[...]).astype(BF)
        u_ref[...] = ub
        cosv, sinv = cos_ref[...], sin_ref[...]

        def sec(k):
            return _dot_nt(ub, win_ref[k * 512:(k + 1) * 512, :])

        def rot(xh):
            return xh * cosv + _swap_pairs(xh) * sinv

        pq, pk = sec(0), sec(1)
        for hh in range(RET_HEADS):
            sl = slice(hh * RET_DIM, (hh + 1) * RET_DIM)
            rq_ref[:, sl] = rot(pq[:, sl]).astype(BF)
            rk_ref[:, sl] = (rot(pk[:, sl]) * RET_SCALE).astype(BF)
        rv_ref[...] = sec(2).astype(BF)
        rg_ref[...] = sec(3).astype(BF)
        fq_ref[...] = (sec(4) * FOX_SCALE).astype(BF)
        fk_ref[...] = sec(5).astype(BF)
        fv_ref[...] = sec(6).astype(BF)
        ff_ref[...] = _dot_nt(ub, win_ref[FF_COL:FF_COL + 128, :])
        for j in range(N_CHIPS):
            gs = _sigmoid(_dot(ub, wm_ref[j]) + bm_ref[:, j * cm:(j + 1) * cm]).astype(BF)
            col = j * cm
            if col < d:
                ga_ref[:, col:col + cm] = gs
            else:
                gb_ref[:, col - d:col - d + cm] = gs

    row = lambda c: pl.BlockSpec((tm, c), lambda i: (i, 0))
    full = lambda *s: pl.BlockSpec(s, lambda i: (0,) * len(s))
    sds = lambda c, dt: jax.ShapeDtypeStruct((t_tok, c), dt)
    return _pcall(
        body, [h, ln, w_in, wm4, b_m, cos_t, sin_t], name="mix_in", grid=(t_tok // tm,),
        out_shape=[sds(d, BF)] + [sds(512, BF)] * 7 + [sds(128, F32), sds(d, BF), sds(d, BF)],
        in_specs=[row(d), full(1, d), full(IN_PAD, d), full(N_CHIPS, d, cm), full(1, 2 * d), row(128), row(128)],
        out_specs=[row(d)] + [row(512)] * 7 + [row(128), row(d), row(d)], comm=comm)


def _split3(x):
    hi = x.astype(BF)
    r1 = x - hi.astype(F32)
    mid = r1.astype(BF)
    lo = (r1 - mid.astype(F32)).astype(BF)
    return hi, mid, lo


def _aug_lane():
    return lax.broadcasted_iota(jnp.int32, (1, 128), 1) & (FOX_DIM - 1)


def _aug_put(base, k0, parts):
    w = _aug_lane()
    for i, part in enumerate(parts):
        base = jnp.where(w == k0 + i, part, base)
    return base


def _forget_fwd(ffl, b_pad):
    t_tok = ffl.shape[0]
    tb = _tile(t_tok, 256)

    def body(ff_ref, b_ref, aq_ref, ak_ref, cum_s):
        r = lax.broadcasted_iota(jnp.int32, (tb, tb), 0)
        c = lax.broadcasted_iota(jnp.int32, (tb, tb), 1)
        tri = jnp.where(c <= r, 1.0, 0.0).astype(BF)
        carry = jnp.zeros((1, 128), F32)
        for i in range(t_tok // tb):
            z = ff_ref[i * tb:(i + 1) * tb, :] + b_ref[...]
            lf = jnp.minimum(z, 0.0) - jnp.log(1.0 + jnp.exp(-jnp.abs(z)))
            hi, mid, lo = _split3(lf)
            cs = _dot(tri, hi) + _dot(tri, mid) + _dot(tri, lo) + carry
            cum_s[i * tb:(i + 1) * tb, :] = cs
            carry = cs[tb - 1:tb, :]
        x = cum_s[...]
        first = lax.broadcasted_iota(jnp.int32, (1, 128), 1) < FOX_DIM
        w = _aug_lane()
        one = jnp.ones((t_tok, 128), BF)
        zero = jnp.zeros((t_tok, 128), BF)
        for pp in range(FOX_HEADS // 2):
            other = jnp.where(first, x[:, 2 * pp + 1:2 * pp + 2], x[:, 2 * pp:2 * pp + 1])
            parts = _split3(other)
            aq = jnp.where((w >= 3) & (w < 6), one, zero)
            ak = jnp.where((w < 3) | ((w >= 6) & (w < 9)), one, zero)
            aq_ref[:, pp * 128:(pp + 1) * 128] = _aug_put(aq, 0, parts)
            ak_ref[:, pp * 128:(pp + 1) * 128] = _aug_put(ak, 3, [-q for q in parts])

    sds = jax.ShapeDtypeStruct((t_tok, FOX_WIDTH), BF)
    return _pcall(body, [ffl, b_pad], name="forget_fwd", out_shape=[sds, sds],
                  scratch=[pltpu.VMEM((t_tok, 128), F32)])


def _forget_bwd(dcum_t, dcum_q, ffl, b_pad):
    t_tok = ffl.shape[0]
    tb = _tile(t_tok, 256)

    def body(dc_ref, dq_ref, ff_ref, b_ref, dff_ref, db_ref, pad_s, d_s):
        pad_s[...] = jnp.zeros_like(pad_s)
        pad_s[0:FOX_HEADS, :] = dc_ref[...]
        dsum = pad_s[...].T
        lane = lax.broadcasted_iota(jnp.int32, (t_tok, 128), 1)
        for hh in range(FOX_HEADS):
            dsum = dsum + jnp.where(lane == hh, dq_ref[:, hh * FOX_DIM:hh * FOX_DIM + 1], 0.0)
        d_s[...] = dsum
        r = lax.broadcasted_iota(jnp.int32, (tb, tb), 0)
        c = lax.broadcasted_iota(jnp.int32, (tb, tb), 1)
        tri = jnp.where(c >= r, 1.0, 0.0).astype(BF)
        carry = jnp.zeros((1, 128), F32)
        db = jnp.zeros((1, 128), F32)
        for i in reversed(range(t_tok // tb)):
            hi, mid, lo = _split3(d_s[i * tb:(i + 1) * tb, :])
            dlf = _dot(tri, hi) + _dot(tri, mid) + _dot(tri, lo) + carry
            carry = dlf[0:1, :]
            z = ff_ref[i * tb:(i + 1) * tb, :] + b_ref[...]
            dff = dlf * _sigmoid(-z)
            dff_ref[i * tb:(i + 1) * tb, :] = dff.astype(BF)
            db = db + jnp.sum(dff, axis=0, keepdims=True)
        db_ref[...] = db

    return _pcall(
        body, [dcum_t, dcum_q, ffl, b_pad], name="forget_bwd",
        out_shape=[jax.ShapeDtypeStruct((t_tok, 128), BF), jax.ShapeDtypeStruct((1, 128), F32)],
        scratch=[pltpu.VMEM((128, t_tok), F32), pltpu.VMEM((t_tok, 128), F32)])


def _first_half():
    return lax.broadcasted_iota(jnp.int32, (1, 128), 1) < FOX_DIM


def _head_rows(x2, a2, hh):
    return jnp.where(_first_half(), x2, a2) if hh == 0 else jnp.where(_first_half(), a2, x2)


def _head_only(x2, hh):
    zero = jnp.zeros_like(x2)
    return jnp.where(_first_half(), x2, zero) if hh == 0 else jnp.where(_first_half(), zero, x2)


def _causal_diag(s):
    rows = lax.broadcasted_iota(jnp.int32, s.shape, 0)
    cols = lax.broadcasted_iota(jnp.int32, s.shape, 1)
    return jnp.where(cols <= rows, s, NEG)


def _diag_or_below(qi, ki, step):
    pl.when(ki < qi)(lambda: step(False))
    pl.when(ki == qi)(lambda: step(True))


def _tri_rows(s, n):
    qi = sum((s >= r * (r + 1) // 2).astype(jnp.int32) for r in range(1, n))
    return qi, s - (qi * (qi + 1)) // 2


def _tri_cols(s, n):
    ki = sum((s >= k * n - k * (k - 1) // 2).astype(jnp.int32) for k in range(1, n))
    return ki, ki + s - (ki * n - (ki * (ki - 1)) // 2)


def _fox_fwd(fq, fk, fv, aq, ak, comm=None):
    t_tok = fq.shape[0]
    t = _tile(t_tok, 512)
    nq = t_tok // t
    npair = FOX_HEADS // 2

    def body(q_ref, k_ref, v_ref, aq_ref, ak_ref, o_ref, of_ref, aqb_ref, m_s, l_s, acc_s):
        qi, ki = _tri_rows(pl.program_id(1), nq)

        @pl.when(ki == 0)
        def _():
            m_s[...] = jnp.full_like(m_s, NEG)
            l_s[...] = jnp.zeros_like(l_s)
            acc_s[...] = jnp.zeros_like(acc_s)

        def step(diag):
            q2, k2, v2, aq2, ak2 = q_ref[...], k_ref[...], v_ref[...], aq_ref[...], ak_ref[...]
            for hh in range(2):
                s = _dot_nt(_head_rows(q2, aq2, hh), _head_rows(k2, ak2, hh))
                if diag:
                    s = _causal_diag(s)
                m_prev = m_s[hh]
                m_new = jnp.maximum(m_prev, jnp.max(s, axis=1, keepdims=True))
                alpha = jnp.exp(m_prev - m_new)
                p = jnp.exp(s - jnp.tile(m_new, (1, t // 128)))
                l_s[hh] = alpha * l_s[hh] + jnp.sum(p, axis=1, keepdims=True)
                acc_s[hh] = alpha * acc_s[hh] + _dot(p.astype(BF), v2)
                m_s[hh] = m_new

        _diag_or_below(qi, ki, step)

        @pl.when(ki == qi)
        def _():
            first = _first_half()
            o = jnp.where(first, acc_s[0] / l_s[0], acc_s[1] / l_s[1])
            o_ref[...] = o.astype(BF)
            of_ref[...] = o
            other = jnp.where(first, m_s[1] + jnp.log(l_s[1]), m_s[0] + jnp.log(l_s[0]))
            aqb_ref[...] = _aug_put(aq_ref[...], 6, _split3(-other))

    qs = pl.BlockSpec((t, 128), lambda p, s: (_tri_rows(s, nq)[0], p))
    ks = pl.BlockSpec((t, 128), lambda p, s: (_tri_rows(s, nq)[1], p))
    stat = pltpu.VMEM((2, t, 128), F32)
    return _pcall(
        body, [fq, fk, fv, aq, ak], name="fox_fwd", grid=(npair, nq * (nq + 1) // 2),
        out_shape=[jax.ShapeDtypeStruct((t_tok, FOX_WIDTH), BF), jax.ShapeDtypeStruct((t_tok, FOX_WIDTH), F32),
                   jax.ShapeDtypeStruct((t_tok, FOX_WIDTH), BF)],
        in_specs=[qs, ks, ks, qs, ks], out_specs=[qs, qs, qs], scratch=[stat, stat, stat], comm=comm)


def _fox_ds(q2, k2, v2, do2, aq2, ak2, ad2, hh, diag):
    s = _dot_nt(_head_rows(q2, aq2, hh), _head_rows(k2, ak2, hh))
    if diag:
        s = _causal_diag(s)
    p = jnp.exp(s)
    av = jnp.where(_aug_lane() < 3, 1.0, 0.0).astype(BF)
    dp = _dot_nt(_head_rows(do2, ad2, hh), _head_rows(v2, jnp.broadcast_to(av, v2.shape), hh))
    return p, p * dp


def _fox_bwd(fq, fk, fv, do, aqb, ak, ad, comm=None):
    t_tok = fq.shape[0]
    t = _tile(t_tok, 512)
    nq = t_tok // t
    npair = FOX_HEADS // 2
    n_steps = nq * (nq + 1) // 2

    def body(q_ref, k_ref, v_ref, do_ref, aq_ref, ak_ref, ad_ref, dq_ref, dk_ref, dv_ref, dck_ref, dcq_ref,
             dk_s, dv_s, dq_s, rs_s):
        step_id = pl.program_id(1)
        ki, qi = _tri_cols(step_id, nq)

        @pl.when(step_id == 0)
        def _():
            dq_s[...] = jnp.zeros_like(dq_s)
            rs_s[...] = jnp.zeros_like(rs_s)

        @pl.when(qi == ki)
        def _():
            dk_s[...] = jnp.zeros_like(dk_s)
            dv_s[...] = jnp.zeros_like(dv_s)
            dck_ref[...] = jnp.zeros_like(dck_ref)

        rows = pl.ds(qi * t if isinstance(qi, int) else pl.multiple_of(qi * t, t), t)

        def step(diag):
            q2, k2, v2, do2 = q_ref[...], k_ref[...], v_ref[...], do_ref[...]
            dq = []
            for hh in range(2):
                p, ds = _fox_ds(q2, k2, v2, do2, aq_ref[...], ak_ref[...], ad_ref[...], hh, diag)
                dsb = ds.astype(BF)
                dv_s[...] += _dot_tn(p.astype(BF), _head_only(do2, hh))
                dk_s[...] += _dot_tn(dsb, _head_only(q2, hh))
                dq.append(_dot(dsb, k2))
                dck_ref[hh] = dck_ref[hh] - jnp.sum(ds, axis=0, keepdims=True)
                rs_s[hh, rows, :] = rs_s[hh, rows, :] + jnp.sum(ds, axis=1, keepdims=True)
            dq_s[rows, :] = dq_s[rows, :] + jnp.where(_first_half(), dq[0], dq[1])

        _diag_or_below(qi, ki, step)

        @pl.when(qi == nq - 1)
        def _():
            dk_ref[...] = dk_s[...].astype(BF)
            dv_ref[...] = dv_s[...].astype(BF)

        @pl.when(step_id == n_steps - 1)
        def _():
            dq_ref[...] = (dq_s[...] * FOX_SCALE).astype(BF)
            dcq_ref[...] = jnp.where(_first_half(), rs_s[0], rs_s[1])

    qs = pl.BlockSpec((t, 128), lambda p, s: (_tri_cols(s, nq)[1], p))
    ks = pl.BlockSpec((t, 128), lambda p, s: (_tri_cols(s, nq)[0], p))
    cks = pl.BlockSpec((2, 1, t), lambda p, s: (p, 0, _tri_cols(s, nq)[0]))
    seq = pl.BlockSpec((t_tok, 128), lambda p, s: (0, p))
    sds = jax.ShapeDtypeStruct((t_tok, FOX_WIDTH), BF)
    return _pcall(
        body, [fq, fk, fv, do, aqb, ak, ad], name="fox_bwd", grid=(npair, n_steps),
        out_shape=[sds, sds, sds, jax.ShapeDtypeStruct((FOX_HEADS, 1, t_tok), F32),
                   jax.ShapeDtypeStruct((t_tok, FOX_WIDTH), F32)],
        in_specs=[qs, ks, ks, qs, qs, ks, qs], out_specs=[seq, ks, ks, cks, seq],
        scratch=[pltpu.VMEM((t, 128), F32), pltpu.VMEM((t, 128), F32), pltpu.VMEM((t_tok, 128), F32),
                 pltpu.VMEM((2, t_tok, 128), F32)], comm=comm)


def _ret_consts():
    c = RET_CHUNK
    log_gamma = jnp.log1p(-jnp.exp2(-5.0 - jnp.arange(RET_HEADS, dtype=F32)))
    idx = jnp.arange(c, dtype=F32)
    diff = idx[:, None] - idx[None, :]
    dmask = jnp.where(diff >= 0, jnp.exp(log_gamma[:, None, None] * jnp.maximum(diff, 0.0)), 0.0)
    qdec = jnp.exp(log_gamma[:, None] * (idx + 1.0))
    kdec = jnp.exp(log_gamma[:, None] * (c - 1 - idx))
    cdec = jnp.exp(log_gamma * c)
    bc = lambda v: jnp.broadcast_to(v[:, :, None], (RET_HEADS, c, RET_DIM))
    return dmask, bc(qdec), bc(kdec), jnp.broadcast_to(cdec[:, None, None], (RET_HEADS, c, RET_DIM))


def _group_norm(y):
    mu = jnp.mean(y, axis=-1, keepdims=True)
    yc = y - mu
    r = lax.rsqrt(jnp.mean(yc * yc, axis=-1, keepdims=True) + EPS)
    return yc * r, r


def _ret_fwd(rq, rk, rv, rg, consts, comm=None):
    t_tok = rq.shape[0]
    nb = 4 if t_tok % (4 * RET_CHUNK) == 0 else 1
    tr = nb * RET_CHUNK
    n_steps = t_tok // tr
    c = RET_CHUNK

    def body(q_ref, k_ref, v_ref, g_ref, dm_ref, qd_ref, kd_ref, cd_ref, y_ref, yo_ref, st_ref, s_s):
        @pl.when(pl.program_id(0) == 0)
        def _():
            s_s[...] = jnp.zeros_like(s_s)

        for b in range(nb):
            rows = slice(b * c, (b + 1) * c)
            for hh in range(RET_HEADS):
                cols = slice(hh * RET_DIM, (hh + 1) * RET_DIM)
                q, k, v = q_ref[rows, cols], k_ref[rows, cols], v_ref[rows, cols]
                state = s_s[hh]
                st_ref[hh, b] = state
                sc = (_dot_nt(q, k) * dm_ref[hh]).astype(BF)
                y = _dot(sc, v) + _dot((q.astype(F32) * qd_ref[hh]).astype(BF), state.astype(BF))
                s_s[hh] = cd_ref[hh] * state + _dot_tn((k.astype(F32) * kd_ref[hh]).astype(BF), v)
                y_ref[rows, cols] = y
                yn, _ = _group_norm(y)
                gate = g_ref[rows, cols].astype(F32)
                yo_ref[rows, cols] = (yn * (gate * _sigmoid(gate))).astype(BF)

    blk = pl.BlockSpec((tr, RET_WIDTH), lambda i: (i, 0))
    cst = pl.BlockSpec((RET_HEADS, c, RET_DIM), lambda i: (0, 0, 0))
    return _pcall(
        body, [rq, rk, rv, rg, *consts], name="ret_fwd", grid=(n_steps,),
        out_shape=[jax.ShapeDtypeStruct((t_tok, RET_WIDTH), F32), jax.ShapeDtypeStruct((t_tok, RET_WIDTH), BF),
                   jax.ShapeDtypeStruct((RET_HEADS, t_tok // c, RET_DIM, RET_DIM), F32)],
        in_specs=[blk] * 4 + [cst] * 4,
        out_specs=[blk, blk, pl.BlockSpec((RET_HEADS, nb, RET_DIM, RET_DIM), lambda i: (0, i, 0, 0))],
        scratch=[pltpu.VMEM((RET_HEADS, RET_DIM, RET_DIM), F32)], comm=comm)


def _ret_bwd(rq, rk, rv, rg, y_raw, dyo, states, consts, cos_t, sin_t, comm=None):
    t_tok = rq.shape[0]
    nb = 4 if t_tok % (4 * RET_CHUNK) == 0 else 1
    tr = nb * RET_CHUNK
    n_steps = t_tok // tr
    c = RET_CHUNK

    def body(q_ref, k_ref, v_ref, g_ref, y_ref, dyo_ref, st_ref, dm_ref, qd_ref, kd_ref, cd_ref,
             cos_ref, sin_ref, dq_ref, dk_ref, dv_ref, dg_ref, ds_s):
        @pl.when(pl.program_id(0) == 0)
        def _():
            ds_s[...] = jnp.zeros_like(ds_s)

        for b in reversed(range(nb)):
            rows = slice(b * c, (b + 1) * c)
            cosv, sinv = cos_ref[rows, :], sin_ref[rows, :]
            for hh in range(RET_HEADS):
                cols = slice(hh * RET_DIM, (hh + 1) * RET_DIM)
                dm, qd, kd, cd = dm_ref[hh], qd_ref[hh], kd_ref[hh], cd_ref[hh]
                q, k, v = q_ref[rows, cols], k_ref[rows, cols], v_ref[rows, cols]
                yn, r = _group_norm(y_ref[rows, cols])
                gate = g_ref[rows, cols].astype(F32)
                sg = _sigmoid(gate)
                dyo = dyo_ref[rows, cols]
                dg_ref[rows, cols] = (dyo * yn * (sg * (1.0 + gate * (1.0 - sg)))).astype(BF)
                dyn = dyo * (gate * sg)
                dy = r * (dyn - jnp.mean(dyn, axis=-1, keepdims=True)
                          - yn * jnp.mean(dyn * yn, axis=-1, keepdims=True))
                dyb = dy.astype(BF)
                state_b = st_ref[hh, b].astype(BF)
                dstate = ds_s[hh]
                dstate_b = dstate.astype(BF)
                qdb = (q.astype(F32) * qd).astype(BF)
                kdb = (k.astype(F32) * kd).astype(BF)
                sc = (_dot_nt(q, k) * dm).astype(BF)
                dv = _dot_tn(sc, dyb) + _dot(kdb, dstate_b)
                dp = (_dot_nt(dyb, v) * dm).astype(BF)
                dq = _dot(dp, k) + _dot_nt(dyb, state_b) * qd
                dk = (_dot_tn(dp, q) + _dot_nt(v, dstate_b) * kd) * RET_SCALE
                ds_s[hh] = cd * dstate + _dot_tn(qdb, dyb)
                dv_ref[rows, cols] = dv.astype(BF)
                dq_ref[rows, cols] = (dq * cosv - _swap_pairs(dq) * sinv).astype(BF)
                dk_ref[rows, cols] = (dk * cosv - _swap_pairs(dk) * sinv).astype(BF)

    rev = lambda i: n_steps - 1 - i
    blk = pl.BlockSpec((tr, RET_WIDTH), lambda i: (rev(i), 0))
    tab = pl.BlockSpec((tr, RET_DIM), lambda i: (rev(i), 0))
    cst = pl.BlockSpec((RET_HEADS, c, RET_DIM), lambda i: (0, 0, 0))
    sds = jax.ShapeDtypeStruct((t_tok, RET_WIDTH), BF)
    return _pcall(
        body, [rq, rk, rv, rg, y_raw, dyo, states, *consts, cos_t, sin_t], name="ret_bwd",
        grid=(n_steps,), out_shape=[sds] * 4,
        in_specs=[blk] * 6 + [pl.BlockSpec((RET_HEADS, nb, RET_DIM, RET_DIM), lambda i: (0, rev(i), 0, 0))]
        + [cst] * 4 + [tab, tab],
        out_specs=[blk] * 4, scratch=[pltpu.VMEM((RET_HEADS, RET_DIM, RET_DIM), F32)], comm=comm)


def _mix_out(h, y_ret, y_fox, ga, gb, wr4, wf4, wo4, comm=None):
    t_tok, d = h.shape
    cz = wr4.shape[-1]
    ro = wo4.shape[-2]
    tm = _tile(t_tok, 512)

    def body(h_ref, yr_ref, yf_ref, ga_ref, gb_ref, wr_ref, wf_ref, wo_ref, ho_ref, za_ref, zb_ref, mix_ref):
        yr, yf = yr_ref[...], yf_ref[...]
        for j in range(N_CHIPS):
            sl = slice(j * cz, (j + 1) * cz)
            za = _dot(yr, wr_ref[j])
            zb = _dot(yf, wf_ref[j])
            za_ref[:, sl] = za.astype(BF)
            zb_ref[:, sl] = zb.astype(BF)
            mix_ref[:, sl] = (ga_ref[:, sl].astype(F32) * za + gb_ref[:, sl].astype(F32) * zb).astype(BF)
        acc = h_ref[...]
        for j in range(N_CHIPS):
            acc = acc + _dot(mix_ref[:, j * ro:(j + 1) * ro], wo_ref[j])
        ho_ref[...] = acc

    row = lambda c: pl.BlockSpec((tm, c), lambda i: (i, 0))
    full = lambda *s: pl.BlockSpec(s, lambda i: (0,) * len(s))
    sds = lambda dt: jax.ShapeDtypeStruct((t_tok, d), dt)
    return _pcall(
        body, [h, y_ret, y_fox, ga, gb, wr4, wf4, wo4], name="mix_out", grid=(t_tok // tm,),
        out_shape=[sds(F32), sds(BF), sds(BF), sds(BF)],
        in_specs=[row(d), row(RET_WIDTH), row(FOX_WIDTH), row(d), row(d),
                  full(N_CHIPS, RET_WIDTH, cz), full(N_CHIPS, FOX_WIDTH, cz), full(N_CHIPS, ro, d)],
        out_specs=[row(d)] * 4, comm=comm)


def _mix_out_bwd(dh, za, zb, ga, gb, y_fox, wr4, wf4, wo4, comm=None):
    t_tok, d = dh.shape
    cz = wr4.shape[-1]
    ro = wo4.shape[-2]
    tm = _tile(t_tok, 256)

    def body(dh_ref, za_ref, zb_ref, ga_ref, gb_ref, yf_ref, wr_ref, wf_ref, wo_ref,
             dhb_ref, dgp_ref, dza_ref, dzb_ref, dyr_ref, dyf_ref, dl_ref, db_ref):
        @pl.when(pl.program_id(0) == 0)
        def _():
            db_ref[...] = jnp.zeros_like(db_ref)

        dhb = dh_ref[...].astype(BF)
        dhb_ref[...] = dhb
        dyr = jnp.zeros((tm, RET_WIDTH), F32)
        dyf = jnp.zeros((tm, FOX_WIDTH), F32)
        for j in range(N_CHIPS):
            sl = slice(j * ro, (j + 1) * ro)
            dmix = _dot_nt(dhb, wo_ref[j])
            ga, gb = ga_ref[:, sl].astype(F32), gb_ref[:, sl].astype(F32)
            dza = (dmix * ga).astype(BF)
            dzb = (dmix * gb).astype(BF)
            dza_ref[:, sl] = dza
            dzb_ref[:, sl] = dzb
            dga = dmix * za_ref[:, sl].astype(F32) * ga * (1.0 - ga)
            dgb = dmix * zb_ref[:, sl].astype(F32) * gb * (1.0 - gb)
            dgp_ref[:, sl] = dga.astype(BF)
            dgp_ref[:, d + j * ro:d + (j + 1) * ro] = dgb.astype(BF)
            db_ref[:, sl] += jnp.sum(dga, axis=0, keepdims=True)
            db_ref[:, d + j * ro:d + (j + 1) * ro] += jnp.sum(dgb, axis=0, keepdims=True)
        for j in range(N_CHIPS):
            sl = slice(j * cz, (j + 1) * cz)
            dyr = dyr + _dot_nt(dza_ref[:, sl], wr_ref[j])
            dyf = dyf + _dot_nt(dzb_ref[:, sl], wf_ref[j])
        dyr_ref[...] = dyr
        dyfb = dyf.astype(BF)
        dyf_ref[...] = dyfb
        prod = dyfb.astype(F32) * yf_ref[...]
        first = _first_half()
        for pp in range(FOX_HEADS // 2):
            blk = prod[:, pp * 128:(pp + 1) * 128]
            s0 = jnp.sum(jnp.where(first, blk, 0.0), axis=1, keepdims=True)
            s1 = jnp.sum(jnp.where(first, 0.0, blk), axis=1, keepdims=True)
            parts = _split3(-jnp.where(first, s1, s0))
            dl_ref[:, pp * 128:(pp + 1) * 128] = _aug_put(jnp.zeros((tm, 128), BF), 0, parts)

    row = lambda c: pl.BlockSpec((tm, c), lambda i: (i, 0))
    full = lambda *s: pl.BlockSpec(s, lambda i: (0,) * len(s))
    sds = lambda c, dt: jax.ShapeDtypeStruct((t_tok, c), dt)
    return _pcall(
        body, [dh, za, zb, ga, gb, y_fox, wr4, wf4, wo4], name="mix_out_bwd", grid=(t_tok // tm,),
        out_shape=[sds(d, BF), sds(2 * d, BF), sds(d, BF), sds(d, BF), sds(RET_WIDTH, F32),
                   sds(FOX_WIDTH, BF), sds(FOX_WIDTH, BF), jax.ShapeDtypeStruct((1, 2 * d), F32)],
        in_specs=[row(d)] * 5 + [row(FOX_WIDTH), full(N_CHIPS, RET_WIDTH, cz), full(N_CHIPS, FOX_WIDTH, cz),
                                 full(N_CHIPS, ro, d)],
        out_specs=[row(d), row(2 * d), row(d), row(d), row(RET_WIDTH), row(FOX_WIDTH), row(FOX_WIDTH),
                   full(1, 2 * d)],
        comm=comm)


def _mix_in_bwd(dh, h, ln, parts, dff, dgpre, w_in, wm4, comm=None):
    t_tok, d = h.shape
    cm = wm4.shape[-1]
    tm = _tile(t_tok, 256)

    def body(dh_ref, h_ref, ln_ref, p0, p1, p2, p3, p4, p5, p6, dff_ref, dgp_ref, win_ref, wm_ref,
             dhi_ref, dln_ref, dproj_ref):
        @pl.when(pl.program_id(0) == 0)
        def _():
            dln_ref[...] = jnp.zeros_like(dln_ref)

        for k, pr in enumerate((p0, p1, p2, p3, p4, p5, p6)):
            dproj_ref[:, k * 512:(k + 1) * 512] = pr[...]
        dproj_ref[:, FF_COL:FF_COL + 128] = dff_ref[...]
        dproj_ref[:, FF_COL + 128:] = jnp.zeros((tm, IN_PAD - FF_COL - 128), BF)
        du = _dot(dproj_ref[...], win_ref[...])
        for j in range(N_CHIPS):
            du = du + _dot_nt(dgp_ref[:, j * cm:(j + 1) * cm], wm_ref[j])
        xv = h_ref[...]
        dx, dln = _rms_bwd(du, xv, _rstd(xv), ln_ref[...])
        dln_ref[...] += dln
        dhi_ref[...] = dh_ref[...] + dx

    row = lambda c: pl.BlockSpec((tm, c), lambda i: (i, 0))
    full = lambda *s: pl.BlockSpec(s, lambda i: (0,) * len(s))
    return _pcall(
        body, [dh, h, ln, *parts, dff, dgpre, w_in, wm4], name="mix_in_bwd", grid=(t_tok // tm,),
        out_shape=[jax.ShapeDtypeStruct((t_tok, d), F32), jax.ShapeDtypeStruct((1, d), F32),
                   jax.ShapeDtypeStruct((t_tok, IN_PAD), BF)],
        in_specs=[row(d), row(d), full(1, d)] + [row(512)] * 7 + [row(128), row(2 * d), full(IN_PAD, d),
                                                                   full(N_CHIPS, d, cm)],
        out_specs=[row(d), full(1, d), row(IN_PAD)], comm=comm)


def _tail(h, p, target, ln_ple, ln_fin, wpg4, wpl4, comm=None):
    t_tok, d = h.shape
    pd = p.shape[1]
    rg = wpg4.shape[-2]
    cp = wpl4.shape[-1]
    tm = _tile(t_tok, 256)

    def body(h_ref, p_ref, t_ref, lp_ref, lf_ref, wg_ref, wp_ref,
             dh_ref, n_ref, dgp_ref, dpe_ref, pb_ref, loss_ref, dlf_ref, dlp_ref, pe_s, dn_s):
        @pl.when(pl.program_id(0) == 0)
        def _():
            loss_ref[...] = jnp.zeros_like(loss_ref)
            dlf_ref[...] = jnp.zeros_like(dlf_ref)
            dlp_ref[...] = jnp.zeros_like(dlp_ref)

        xv = h_ref[...]
        r3 = _rstd(xv)
        nb = (xv * r3 * lp_ref[...]).astype(BF)
        n_ref[...] = nb
        pb = p_ref[...].astype(BF)
        pb_ref[...] = pb
        pgpre = jnp.zeros((tm, d), F32)
        for j in range(N_CHIPS):
            pgpre = pgpre + _dot(nb[:, j * rg:(j + 1) * rg], wg_ref[j])
            pe_s[:, j * cp:(j + 1) * cp] = _dot(pb, wp_ref[j])
        pg = _sigmoid(pgpre)
        pe = pe_s[...]
        h4 = xv + pg * pe
        r4 = _rstd(h4)
        err = h4 * r4 * lf_ref[...] - t_ref[...]
        loss_ref[...] += 0.5 * jnp.sum(jnp.sum(err * err, axis=1, keepdims=True), axis=0, keepdims=True) / d
        dh4, dlf = _rms_bwd(err * (1.0 / d), h4, r4, lf_ref[...])
        dlf_ref[...] += dlf
        dpe_ref[...] = (dh4 * pg).astype(BF)
        dgp = (dh4 * pe * pg * (1.0 - pg)).astype(BF)
        dgp_ref[...] = dgp
        for j in range(N_CHIPS):
            dn_s[:, j * rg:(j + 1) * rg] = _dot_nt(dgp, wg_ref[j])
        dx, dlp = _rms_bwd(dn_s[...], xv, r3, lp_ref[...])
        dlp_ref[...] += dlp
        dh_ref[...] = dh4 + dx

    row = lambda c: pl.BlockSpec((tm, c), lambda i: (i, 0))
    full = lambda *s: pl.BlockSpec(s, lambda i: (0,) * len(s))
    sds = lambda c, dt: jax.ShapeDtypeStruct((t_tok, c), dt)
    vec = jax.ShapeDtypeStruct((1, d), F32)
    return _pcall(
        body, [h, p, target, ln_ple, ln_fin, wpg4, wpl4], name="tail", grid=(t_tok // tm,),
        out_shape=[sds(d, F32), sds(d, BF), sds(d, BF), sds(d, BF), sds(pd, BF),
                   jax.ShapeDtypeStruct((1, 128), F32), vec, vec],
        in_specs=[row(d), row(pd), row(d), full(1, d), full(1, d), full(N_CHIPS, rg, d), full(N_CHIPS, pd, cp)],
        out_specs=[row(d), row(d), row(d), row(d), row(pd), full(1, 128), full(1, d), full(1, d)],
        scratch=[pltpu.VMEM((tm, d), F32), pltpu.VMEM((tm, d), F32)], comm=comm)


BIG = ["w_ffn1_gate", "w_ffn1_up", "w_ffn1_down", "w_in", "w_merge", "w_ret_out", "w_fox_out", "w_out",
       "w_ffn2_gate", "w_ffn2_up", "w_ffn2_down", "w_ple", "w_ple_gate"]
SMALL = ["ln_ffn1", "ln_mix", "b_forget", "b_merge", "ln_ffn2", "ln_ple", "ln_final"]
WEIGHTS = ["ln_ffn1", "w_ffn1_gate", "w_ffn1_up", "w_ffn1_down", "ln_mix", "w_in", "b_forget", "w_merge", "b_merge",
           "w_ret_out", "w_fox_out", "w_out", "ln_ffn2", "w_ffn2_gate", "w_ffn2_up", "w_ffn2_down", "ln_ple",
           "w_ple", "w_ple_gate", "ln_final"]


TRANSPOSED = {"w_ffn1_gate", "w_ffn1_up", "w_ffn2_gate", "w_ffn2_up", "w_in"}
IN_ROWS_PAD = -(-(IN_COLS // N_CHIPS) // 32) * 32


def _pack_small(vals, loss_row):
    rows = [loss_row]
    for name in SMALL:
        v = vals[name].reshape(-1)
        n = -(-v.shape[0] // 128) * 128
        rows.append(jnp.pad(v, (0, n - v.shape[0])).reshape(n // 128, 128))
    packed = jnp.concatenate(rows, axis=0)
    pad = -packed.shape[0] % 8
    return jnp.pad(packed, ((0, pad), (0, 0)))


def _unpack_small(packed, sizes):
    out, r = {}, 1
    for name in SMALL:
        n = sizes[name]
        nr = -(-n // 128)
        out[name] = packed[r:r + nr].reshape(1, nr * 128)[:, :n]
        r += nr
    return out


class _Stage:
    def __init__(self, comm, finish):
        self.comm, self.finish, self.result = comm, finish, None


def _hosted(fn, *a, stages=()):
    if not stages:
        return fn(*a)
    outs, couts = fn(*a, comm=_merge([st.comm for st in stages]))
    for st, o in zip(stages, _split_outs([st.comm for st in stages], couts)):
        st.result = st.finish(o)
    return outs


class _Reducer:
    def __init__(self):
        self.done = {}

    def swap(self, grads):
        names = list(grads)
        return _Stage(_c_half_swap([grads[n] for n in names]),
                      lambda outs: dict(zip(names, _add_halves([(grads[n], o) for n, o in zip(names, outs)]))))

    def exchange(self, parts):
        names = list(parts)
        return _Stage(_c_chip_exchange([parts[n] for n in names]),
                      lambda outs: dict(zip(names, _sum_chips([(parts[n], o) for n, o in zip(names, outs)]))))

    def join(self, halves):
        names = list(halves)
        return _Stage(_c_join([halves[n] for n in names]),
                      lambda outs: self.done.update({n: (halves[n], o) for n, o in zip(names, outs)}))


def kernel(x, p, positions, ln_ffn1, w_ffn1_gate, w_ffn1_up, w_ffn1_down, ln_mix, w_in, b_forget, w_merge, b_merge, w_ret_out, w_fox_out, w_out, ln_ffn2, w_ffn2_gate, w_ffn2_up, w_ffn2_down, ln_ple, w_ple, w_ple_gate, ln_final, loss_target, m_ln_ffn1, m_w_ffn1_gate, m_w_ffn1_up, m_w_ffn1_down, m_ln_mix, m_w_in, m_b_forget, m_w_merge, m_b_merge, m_w_ret_out, m_w_fox_out, m_w_out, m_ln_ffn2, m_w_ffn2_gate, m_w_ffn2_up, m_w_ffn2_down, m_ln_ple, m_w_ple, m_w_ple_gate, m_ln_final, v_ln_ffn1, v_w_ffn1_gate, v_w_ffn1_up, v_w_ffn1_down, v_ln_mix, v_w_in, v_b_forget, v_w_merge, v_b_merge, v_w_ret_out, v_w_fox_out, v_w_out, v_ln_ffn2, v_w_ffn2_gate, v_w_ffn2_up, v_w_ffn2_down, v_ln_ple, v_w_ple, v_w_ple_gate, v_ln_final):
    args = dict(locals())
    w = {n: args[n] for n in WEIGHTS}
    m = {n: args["m_" + n] for n in WEIGHTS}
    v = {n: args["v_" + n] for n in WEIGHTS}
    d = x.shape[-1]
    t_tok = x.shape[1]
    xs, ps, target = x[0], p[0, 0], loss_target[0]
    small = {n: w[n].reshape(1, -1) for n in SMALL}

    def to2d(n, a):
        if n in TRANSPOSED:
            return a[0].T
        return a.reshape(a.shape[-2], a.shape[-1]) if a.ndim == 3 else a.reshape(1, -1)

    def from2d(n, a):
        return a.T[None] if n in TRANSPOSED else a.reshape(w[n].shape)

    def padded(n, a):
        return jnp.pad(a, ((0, IN_ROWS_PAD - a.shape[0]), (0, 0))) if n == "w_in" else a

    core = lax.axis_index("c")
    me = 2 * lax.axis_index("x") + lax.axis_index("y")
    shard = {}

    def set_shard(n, s2):
        s2 = padded(n, s2)
        shard[n] = s2.reshape(1, 2, s2.shape[0] // 2, s2.shape[1])

    first = ["w_ffn1_gate", "w_ffn1_up", "w_ffn1_down"]
    for n in first + ["w_in"]:
        set_shard(n, to2d(n, w[n]).astype(BF))
    full = {}

    def gather(names):
        bufs = [lax.dynamic_update_slice(jnp.zeros((N_CHIPS,) + shard[n].shape[1:], BF), shard[n], (me, 0, 0, 0))
                for n in names]

        def finish(outs):
            full.update({n: o.reshape(N_CHIPS, 2 * o.shape[2], o.shape[3]) for n, o in zip(names, outs)})

        return _Stage(_c_all_gather(bufs), finish)

    half = RET_DIM // 2
    inv_freq = 1.0 / (ROPE_BASE ** (jnp.arange(half, dtype=F32) / half))
    later = [n for n in BIG if n not in shard]
    cos_t, sin_t, *cast = _hosted(_rope_tables, positions[0].astype(F32).reshape(t_tok, 1),
                                  jnp.repeat(inv_freq, 2).reshape(1, RET_DIM), [to2d(n, w[n]) for n in later],
                                  stages=[gather(first)])
    for n, s2 in zip(later, cast):
        set_shard(n, s2)
    consts = _ret_consts()
    b_pad = jnp.pad(small["b_forget"], ((0, 0), (0, 128 - FOX_HEADS)))

    h1, n1, g1, u1 = _hosted(
        _ffn_fwd, xs, small["ln_ffn1"], full["w_ffn1_gate"], full["w_ffn1_up"], full["w_ffn1_down"],
        stages=[gather(["w_in", "w_merge", "w_ret_out", "w_fox_out", "w_out", "w_ple_gate", "w_ple"])])
    w_in_full = jnp.pad(full["w_in"][:, :IN_COLS // N_CHIPS].reshape(IN_COLS, d), ((0, IN_PAD - IN_COLS), (0, 0)))
    u, rq, rk, rv, rg, fq, fk, fv, ffl, ga, gb = _mix_in(
        h1, small["ln_mix"], w_in_full, full["w_merge"], small["b_merge"], cos_t, sin_t)
    aq, ak = _forget_fwd(ffl, b_pad)
    y_raw, y_ret, states = _ret_fwd(rq, rk, rv, rg, consts)
    y_fox, y_fox32, aqb = _hosted(_fox_fwd, fq, fk, fv, aq, ak,
                                  stages=[gather(["w_ffn2_gate", "w_ffn2_up", "w_ffn2_down"])])
    h2, za, zb, mix = _mix_out(h1, y_ret, y_fox, ga, gb, full["w_ret_out"], full["w_fox_out"], full["w_out"])
    h3, n2, g2, u2 = _ffn_fwd(h2, small["ln_ffn2"], full["w_ffn2_gate"], full["w_ffn2_up"], full["w_ffn2_down"])

    red = _Reducer()
    dh3, n3, dpgpre, dpe, pb, loss, dln_final, dln_ple = _tail(
        h3, ps, target, small["ln_ple"], small["ln_final"], full["w_ple_gate"], full["w_ple"])
    g_f2 = dict(w_ple_gate=_wgrad_rows("wgrad_ple_gate", n3, dpgpre, N_CHIPS),
                w_ple=_wgrad_cols("wgrad_ple", pb, dpe, N_CHIPS))
    dh2, dln_ffn2, dg2, du2, a2, dhb3 = _ffn_bwd(
        dh3, h2, small["ln_ffn2"], g2, u2, full["w_ffn2_gate"], full["w_ffn2_up"], full["w_ffn2_down"])
    g_f2["w_ffn2_gate"] = _wgrad_b_shared("wgrad_ffn2_gate", dg2, n2)
    g_f2["w_ffn2_up"] = _wgrad_b_shared("wgrad_ffn2_up", du2, n2)
    g_f2["w_ffn2_down"] = _wgrad_b_shared("wgrad_ffn2_down", a2, dhb3)

    sw_f2 = red.swap(g_f2)
    dhb2, dgpre, dza, dzb, dy_ret, dy_fox, ad, db_merge = _hosted(
        _mix_out_bwd, dh2, za, zb, ga, gb, y_fox32, full["w_ret_out"], full["w_fox_out"], full["w_out"],
        stages=[sw_f2])
    g_br = dict(w_out=_wgrad_rows("wgrad_out", mix, dhb2, N_CHIPS),
                w_ret_out=_wgrad_cols("wgrad_ret_out", y_ret, dza, N_CHIPS),
                w_fox_out=_wgrad_cols("wgrad_fox_out", y_fox, dzb, N_CHIPS))

    sw_br = red.swap(g_br)
    drq, drk, drv, drg = _hosted(_ret_bwd, rq, rk, rv, rg, y_raw, dy_ret, states, consts, cos_t, sin_t,
                                 stages=[sw_br])
    ex_f2, ex_br = red.exchange(sw_f2.result), red.exchange(sw_br.result)
    dfq, dfk, dfv, dcum_t3, dcum_q = _hosted(_fox_bwd, fq, fk, fv, dy_fox, aqb, ak, ad, stages=[ex_f2, ex_br])
    dff, db_forget = _forget_bwd(dcum_t3.reshape(FOX_HEADS, t_tok), dcum_q, ffl, b_pad)
    dh1, dln_mix, dproj = _hosted(
        _mix_in_bwd, dh2, h1, small["ln_mix"], (drq, drk, drv, drg, dfq, dfk, dfv), dff, dgpre, w_in_full,
        full["w_merge"], stages=[red.join(ex_f2.result), red.join(ex_br.result)])

    results = {}
    for names in (["w_ffn2_gate", "w_ffn2_up", "w_ffn2_down"], ["w_out", "w_ple_gate"], ["w_ret_out", "w_fox_out"],
                  ["w_ple"]):
        res = _sc_adamw_halves([(to2d(n, w[n]), *red.done[n], to2d(n, m[n]), to2d(n, v[n])) for n in names])
        for q, n in enumerate(names):
            results[n] = tuple(from2d(n, a) for a in res[4 * q:4 * q + 4])

    dx, dln_ffn1, dg1, du1, a1, dhb1 = _ffn_bwd(
        dh1, xs, small["ln_ffn1"], g1, u1, full["w_ffn1_gate"], full["w_ffn1_up"], full["w_ffn1_down"])
    g_f1g = _wgrad_b_shared("wgrad_ffn1_gate", dg1, n1)
    sw_f1g = red.swap(dict(w_ffn1_gate=g_f1g))
    g_f1u = _hosted(_wgrad_b_shared, "wgrad_ffn1_up", du1, n1, stages=[sw_f1g])
    ex_f1g, sw_f1u = red.exchange(sw_f1g.result), red.swap(dict(w_ffn1_up=g_f1u))
    g_f1d = _hosted(_wgrad_b_shared, "wgrad_ffn1_down", a1, dhb1, stages=[ex_f1g, sw_f1u])

    ex_f1u, sw_f1d = red.exchange(sw_f1u.result), red.swap(dict(w_ffn1_down=g_f1d))
    g_in = _hosted(_wgrad_rows, "wgrad_in", dproj, u, IN_PAD // 512,
                   stages=[ex_f1u, sw_f1d, red.join(ex_f1g.result)])
    g_in = g_in.reshape(IN_PAD, d)[:IN_COLS].reshape(N_CHIPS, IN_COLS // N_CHIPS, d)
    g_in = jnp.pad(g_in, ((0, 0), (0, IN_ROWS_PAD - IN_COLS // N_CHIPS), (0, 0)))
    ex_f1d, sw_in = red.exchange(sw_f1d.result), red.swap(dict(w_in=g_in))
    g_mrg = _hosted(_wgrad_cols, "wgrad_merge", u, dgpre, N_CHIPS,
                    stages=[ex_f1d, sw_in, red.join(ex_f1u.result)])

    small_grads = dict(ln_ffn1=dln_ffn1, ln_mix=dln_mix, b_forget=db_forget[:, :FOX_HEADS], b_merge=db_merge,
                       ln_ffn2=dln_ffn2, ln_ple=dln_ple, ln_final=dln_final)
    sizes = {n: w[n].size for n in SMALL}
    ex_in, sw_mrg = red.exchange(sw_in.result), red.swap(dict(w_merge=g_mrg))
    reduced = _hosted(_all_reduce_small, _pack_small(small_grads, loss),
                      stages=[ex_in, sw_mrg, red.join(ex_f1d.result)])
    gsum = _unpack_small(reduced, sizes)
    loss = reduced[0, 0]
    ex_mrg = red.exchange(sw_mrg.result)
    _hosted(_exchange_only, stages=[ex_mrg, red.join(ex_in.result)])
    _hosted(_exchange_only, stages=[red.join(ex_mrg.result)])

    def update(names):
        w2, m2, v2 = ([to2d(n, a[n]) for n in names] for a in (w, m, v))
        n = names[0]
        if n == "w_in":
            mine, other = red.done[n]
            g2 = jnp.where(core == 0, jnp.concatenate([mine, other]), jnp.concatenate([other, mine]))
            g2 = g2[:w2[0].shape[0]]
            rows3 = lambda a: jnp.transpose(a, (2, 0, 1))
            g3 = g2.reshape(g2.shape[0], 1, g2.shape[1])
            res = [g3] + _adamw(rows3(w[n]), g3, rows3(m[n]), rows3(v[n]))
            results[n] = tuple(jnp.transpose(a, (1, 2, 0)) for a in res)
            return
        res = _adamw_halves([(w2[q], *red.done[names[q]], m2[q], v2[q]) for q in range(len(names))])
        for q, name in enumerate(names):
            results[name] = tuple(from2d(name, a) for a in res[4 * q:4 * q + 4])

    res = _adamw_vectors([(to2d(n, w[n]), gsum[n], to2d(n, m[n]), to2d(n, v[n])) for n in SMALL])
    for q, n in enumerate(SMALL):
        results[n] = tuple(from2d(n, a) for a in [gsum[n]] + res[3 * q:3 * q + 3])
    update(["w_ffn1_gate", "w_ffn1_up", "w_ffn1_down"])
    for n in WEIGHTS:
        if n not in results:
            update([n])

    outs = [[results[n][k] for n in WEIGHTS] for k in range(4)]
    return (loss, dx[None], *outs[0], *outs[1], *outs[2], *outs[3])
```

```python
import functools
import operator

import jax
import jax.numpy as jnp
from jax import lax
from jax.experimental import pallas as pl
from jax.experimental.pallas import tpu as pltpu
from jax.experimental.pallas import tpu_sc as plsc

F32 = jnp.float32
BF = jnp.bfloat16
MESH = pl.DeviceIdType.MESH

EPS = 1e-6
ROPE_BASE = 10000.0
N_CHIPS = 4
RET_HEADS = 4
RET_DIM = 128
RET_WIDTH = RET_HEADS * RET_DIM
RET_CHUNK = 128
RET_SCALE = RET_DIM ** -0.5
FOX_HEADS = 8
FOX_DIM = 64
FOX_WIDTH = FOX_HEADS * FOX_DIM
FOX_SCALE = FOX_DIM ** -0.5
IN_COLS = 4 * RET_WIDTH + 3 * FOX_WIDTH + FOX_HEADS
IN_PAD = 4096
FF_COL = 4 * RET_WIDTH + 3 * FOX_WIDTH
NEG = -1e30

ADAM_LR = 0.001
ADAM_B1 = 0.9
ADAM_B2 = 0.999
ADAM_EPS = 1e-08
ADAM_WD = 0.01
ADAM_STEP = 10

VMEM_LIMIT = 52 * 1024 * 1024

RELAY_MIN_STEPS = 16

NT = (((1,), (1,)), ((), ()))
TN = (((0,), (0,)), ((), ()))

HBM_SPEC = pl.BlockSpec(memory_space=pltpu.HBM)
VMEM_SPEC = pl.BlockSpec(memory_space=pltpu.VMEM)


def _dot(a, b):
    return jnp.dot(a, b, preferred_element_type=F32)


def _dot_nt(a, b):
    return lax.dot_general(a, b, NT, preferred_element_type=F32)


def _dot_tn(a, b):
    return lax.dot_general(a, b, TN, preferred_element_type=F32)


def _rstd(xv):
    return lax.rsqrt(jnp.mean(xv * xv, axis=-1, keepdims=True) + EPS)


def _rms_bwd(dn, xv, r, ln):
    xh = xv * r
    dxh = dn * ln
    dx = r * (dxh - xh * jnp.mean(dxh * xh, axis=-1, keepdims=True))
    return dx, jnp.sum(dn * xh, axis=0, keepdims=True)


def _sigmoid(x):
    return jax.nn.sigmoid(x)


def _tile(n, pref):
    return pref if n % pref == 0 else n


def _row_tile(n, cap):
    best = [t for t in range(16, min(n, cap) + 1, 16) if n % t == 0]
    return best[-1] if best else n


class _Comm:
    def __init__(self, ins, out_shapes, sems, start, wait, aliases=None, relay=None):
        self.ins, self.out_shapes, self.sems, self.start, self.wait = list(ins), list(out_shapes), list(sems), start, wait
        self.aliases = dict(aliases or {})
        self.relay = relay


def _merge(comms):
    comms = [c for c in comms if c is not None]
    if not comms:
        return None
    bounds, ni, no, ns = [], 0, 0, 0
    for c in comms:
        bounds.append((ni, no, ns))
        ni, no, ns = ni + len(c.ins), no + len(c.out_shapes), ns + len(c.sems)

    def run(which):
        def f(ins, outs, sems, **kw):
            for c, (i, o, s) in zip(comms, bounds):
                fn = getattr(c, which)
                if fn is not None:
                    fn(ins[i:i + len(c.ins)], outs[o:o + len(c.out_shapes)], sems[s:s + len(c.sems)],
                       **(kw if c.relay is not None else {}))
        return f

    aliases = {i + a: o + b for c, (i, o, _) in zip(comms, bounds) for a, b in c.aliases.items()}
    relay = run("relay") if any(c.relay is not None for c in comms) else None
    return _Comm([a for c in comms for a in c.ins], [a for c in comms for a in c.out_shapes],
                 [a for c in comms for a in c.sems], run("start"), run("wait"), aliases, relay)


def _split_outs(comms, outs):
    res, o = [], 0
    for c in comms:
        if c is not None:
            res.append(list(outs[o:o + len(c.out_shapes)]))
            o += len(c.out_shapes)
    return res


def _pcall(body, args, *, name, out_shape, grid=(), in_specs=None, out_specs=None, scratch=(), comm=None,
           prefetch=()):
    many = isinstance(out_shape, (list, tuple))
    outs = list(out_shape) if many else [out_shape]
    n_pre, n_in, n_out, n_scr = len(prefetch), len(args), len(outs), len(scratch)
    if in_specs is None:
        in_specs, out_specs = [VMEM_SPEC] * n_in, [VMEM_SPEC] * n_out
    else:
        in_specs, out_specs = list(in_specs), (list(out_specs) if many else [out_specs])
    params = pltpu.CompilerParams(dimension_semantics=("arbitrary",) * len(grid), vmem_limit_bytes=VMEM_LIMIT)
    scalars = [jnp.reshape(s, (1,)).astype(jnp.int32) for s in prefetch]
    ci, co = (len(comm.ins), len(comm.out_shapes)) if comm is not None else (0, 0)

    def wrapped(*refs):
        pre, refs = refs[:n_pre], refs[n_pre:]
        a, ca = refs[:n_in], refs[n_in:n_in + ci]
        o = refs[n_in + ci:n_in + ci + n_out]
        cout = refs[n_in + ci + n_out:n_in + ci + n_out + co]
        s = refs[n_in + ci + n_out + co:n_in + ci + n_out + co + n_scr]
        csem = refs[n_in + ci + n_out + co + n_scr:]
        if comm is None:
            body(*pre, *a, *o, *s)
        elif grid:
            step = functools.reduce(lambda acc, k: acc * grid[k] + pl.program_id(k), range(len(grid)), 0)
            n_steps = functools.reduce(operator.mul, grid)
            relayed = comm.relay is not None and n_steps >= RELAY_MIN_STEPS
            pl.when(step == 0)(lambda: comm.start(ca, cout, csem))
            if relayed:
                pl.when(step == n_steps - n_steps // 8)(lambda: comm.relay(ca, cout, csem))
            body(*pre, *a, *o, *s)
            pl.when(step == n_steps - 1)(lambda: comm.wait(ca, cout, csem, **({"relayed": True} if relayed else {})))
        else:
            comm.start(ca, cout, csem)
            body(*pre, *a, *o, *s)
            comm.wait(ca, cout, csem)

    c_ins, c_outs, c_sems, aliases = ([], [], [], {}) if comm is None else (
        comm.ins, comm.out_shapes, comm.sems, {n_pre + n_in + i: n_out + o for i, o in comm.aliases.items()})
    all_in, all_out = in_specs + [HBM_SPEC] * ci, out_specs + [HBM_SPEC] * co
    all_scr = list(scratch) + c_sems
    if grid:
        args = [pltpu.with_memory_space_constraint(a, pltpu.HBM) for a in args]
    c_ins = [pltpu.with_memory_space_constraint(a, pltpu.HBM) for a in c_ins]
    if n_pre:
        spec = dict(grid_spec=pltpu.PrefetchScalarGridSpec(
            num_scalar_prefetch=n_pre, grid=grid, in_specs=all_in, out_specs=all_out, scratch_shapes=all_scr))
    else:
        spec = dict(grid=grid, in_specs=all_in, out_specs=all_out, scratch_shapes=all_scr)
    res = pl.pallas_call(wrapped, name=name, out_shape=outs + c_outs, input_output_aliases=aliases,
                         compiler_params=params, **spec)(*scalars, *args, *c_ins)
    mine = list(res[:n_out])
    mine = mine if many else mine[0]
    return mine if comm is None else (mine, list(res[n_out:]))


def _peer_chips(x, y):
    return [(1 - x, y), (x, 1 - y), (1 - x, 1 - y)]


def _c_all_gather(bufs):
    n = len(bufs)

    def copies(ins, outs, sems):
        send_sems, recv_sems, fwd_send, fwd_recv = sems
        x, y, c = lax.axis_index("x"), lax.axis_index("y"), lax.axis_index("c")
        me = 2 * x + y
        peers = _peer_chips(x, y)
        chip = [2 * px + py for px, py in peers]

        def ici(g, j, slot):
            return pltpu.make_async_remote_copy(
                src_ref=outs[g].at[me, c], dst_ref=outs[g].at[slot, c], send_sem=send_sems.at[g, j],
                recv_sem=recv_sems.at[g, j], device_id=(*peers[j], c), device_id_type=MESH)

        def d2d(g, j, half):
            return pltpu.make_async_remote_copy(
                src_ref=outs[g].at[chip[j], half], dst_ref=outs[g].at[chip[j], half], send_sem=fwd_send.at[g, j],
                recv_sem=fwd_recv.at[g, j], device_id=(x, y, 1 - c), device_id_type=MESH)

        pairs = [(g, j) for g in range(n) for j in range(3)]
        sends = [ici(g, j, me) for g, j in pairs]
        recvs = [ici(g, j, chip[j]) for g, j in pairs]
        passes = [d2d(g, j, c) for g, j in pairs]
        passed = [d2d(g, j, 1 - c) for g, j in pairs]
        return sends, recvs, passes, passed

    def start(ins, outs, sems):
        for cp in copies(ins, outs, sems)[0]:
            cp.start()

    def relay(ins, outs, sems):
        _, recvs, passes, _ = copies(ins, outs, sems)
        for rcv, fwd in zip(recvs, passes):
            rcv.wait_recv()
            fwd.start()

    def wait(ins, outs, sems, relayed=False):
        if not relayed:
            relay(ins, outs, sems)
        sends, _, passes, passed = copies(ins, outs, sems)
        for cp in passed:
            cp.wait_recv()
        for cp in sends + passes:
            cp.wait_send()

    pair_sems = pltpu.SemaphoreType.DMA((n, 3))
    return _Comm(bufs, [jax.ShapeDtypeStruct(s.shape, s.dtype) for s in bufs], [pair_sems] * 4, start, wait,
                 aliases={g: g for g in range(n)}, relay=relay)


def _start_wait(copies):
    def start(ins, outs, sems):
        local, sends, _ = copies(ins, outs, sems)
        for cp in local + sends:
            cp.start()

    def wait(ins, outs, sems):
        local, sends, recvs = copies(ins, outs, sems)
        for cp in recvs:
            cp.wait_recv()
        for cp in sends:
            cp.wait_send()
        for cp in local:
            cp.wait()

    return start, wait


def _c_half_swap(grads):
    n = len(grads)

    def copies(ins, outs, sems):
        send_sems, recv_sems = sems
        x, y, c = lax.axis_index("x"), lax.axis_index("y"), lax.axis_index("c")
        sends = []
        for g in range(n):
            half = ins[g].shape[1] // 2
            sends.append(pltpu.make_async_remote_copy(
                src_ref=ins[g].at[:, pl.ds((1 - c) * half, half), :], dst_ref=outs[g],
                send_sem=send_sems.at[g], recv_sem=recv_sems.at[g], device_id=(x, y, 1 - c), device_id_type=MESH))
        return [], sends, sends

    return _Comm(
        grads, [jax.ShapeDtypeStruct((N_CHIPS, s.shape[1] // 2, s.shape[2]), s.dtype) for s in grads],
        [pltpu.SemaphoreType.DMA((n,)), pltpu.SemaphoreType.DMA((n,))], *_start_wait(copies))


def _c_chip_exchange(parts):
    n = len(parts)

    def copies(ins, outs, sems):
        send_sems, recv_sems = sems
        x, y, c = lax.axis_index("x"), lax.axis_index("y"), lax.axis_index("c")
        peers = _peer_chips(x, y)

        def remote(g, j):
            return pltpu.make_async_remote_copy(
                src_ref=ins[g].at[2 * peers[j][0] + peers[j][1]], dst_ref=outs[g].at[j],
                send_sem=send_sems.at[g, j], recv_sem=recv_sems.at[g, j], device_id=(*peers[j], c),
                device_id_type=MESH)

        sends = [remote(g, j) for g in range(n) for j in range(3)]
        return [], sends, sends

    return _Comm(
        parts, [jax.ShapeDtypeStruct((3,) + s.shape[1:], s.dtype) for s in parts],
        [pltpu.SemaphoreType.DMA((n, 3)), pltpu.SemaphoreType.DMA((n, 3))], *_start_wait(copies))


def _c_join(halves):
    n = len(halves)

    def copies(ins, outs, sems):
        send_sems, recv_sems = sems
        x, y, c = lax.axis_index("x"), lax.axis_index("y"), lax.axis_index("c")
        sends = [pltpu.make_async_remote_copy(
            src_ref=ins[g], dst_ref=outs[g], send_sem=send_sems.at[g], recv_sem=recv_sems.at[g],
            device_id=(x, y, 1 - c), device_id_type=MESH) for g in range(n)]
        return [], sends, sends

    return _Comm(
        halves, [jax.ShapeDtypeStruct(s.shape, s.dtype) for s in halves],
        [pltpu.SemaphoreType.DMA((n,)), pltpu.SemaphoreType.DMA((n,))], *_start_wait(copies))


def _exchange_only(comm=None):
    def body(o_ref):
        o_ref[...] = jnp.zeros_like(o_ref)

    return _pcall(body, [], name="exchange_only", out_shape=jax.ShapeDtypeStruct((8, 128), F32), comm=comm)


def _all_reduce_small(v, comm=None):
    rows = v.shape[0]

    def body(v_ref, out_ref, buf, send_sems, recv_sems):
        x, y, c = lax.axis_index("x"), lax.axis_index("y"), lax.axis_index("c")
        me = 4 * x + 2 * y + c
        buf[me] = v_ref[...]
        flips = [(fx, fy, fc) for fx in (0, 1) for fy in (0, 1) for fc in (0, 1)][1:]

        def peer(k):
            fx, fy, fc = flips[k]
            px, py, pc = x ^ fx, y ^ fy, c ^ fc
            return (px, py, pc), 4 * px + 2 * py + pc

        def copy(k, slot):
            return pltpu.make_async_remote_copy(
                src_ref=buf.at[slot], dst_ref=buf.at[slot], send_sem=send_sems.at[k],
                recv_sem=recv_sems.at[k], device_id=peer(k)[0], device_id_type=MESH)

        sends = [copy(k, me) for k in range(7)]
        for cp in sends:
            cp.start()
        for k in range(7):
            copy(k, peer(k)[1]).wait_recv()
        for cp in sends:
            cp.wait_send()
        acc = buf[0]
        for d in range(1, 8):
            acc = acc + buf[d]
        out_ref[...] = acc

    return _pcall(body, [v], name="all_reduce_small", out_shape=jax.ShapeDtypeStruct((rows, 128), F32),
                  scratch=[pltpu.VMEM((8, rows, 128), F32), pltpu.SemaphoreType.DMA((7,)),
                           pltpu.SemaphoreType.DMA((7,))], comm=comm)


def _add_halves(pairs):
    k = len(pairs)

    def body(h_ref, *refs):
        for a_ref, b_ref, o_ref in zip(refs[0:2 * k:2], refs[1:2 * k:2], refs[2 * k:]):
            o_ref[...] = (a_ref[...].astype(F32) + b_ref[...].astype(F32)).astype(o_ref.dtype)

    in_specs, out_specs = [], []
    for _, got in pairs:
        _, h, c = got.shape
        spec = pl.BlockSpec((1, h, c), lambda j, h_ref: (j, 0, 0))
        in_specs += [pl.BlockSpec((1, h, c), lambda j, h_ref: (j, h_ref[0], 0)), spec]
        out_specs.append(spec)
    return _pcall(body, [a for pair in pairs for a in pair], name="add_halves", grid=(N_CHIPS,),
                  prefetch=[lax.axis_index("c")], in_specs=in_specs, out_specs=out_specs,
                  out_shape=[jax.ShapeDtypeStruct(got.shape, BF) for _, got in pairs])


def _sum_chips(pairs):
    k = len(pairs)
    n_steps = 2 if all(parts.shape[1] % 32 == 0 for parts, _ in pairs) else 1
    me = 2 * lax.axis_index("x") + lax.axis_index("y")

    def body(me_ref, *refs):
        for p_ref, r_ref, o_ref in zip(refs[0:2 * k:2], refs[1:2 * k:2], refs[2 * k:]):
            acc = p_ref[0].astype(F32)
            for s in range(N_CHIPS - 1):
                acc = acc + r_ref[s].astype(F32)
            o_ref[...] = acc

    in_specs, out_specs = [], []
    for parts, _ in pairs:
        _, h, c = parts.shape
        th = h // n_steps
        in_specs += [pl.BlockSpec((1, th, c), lambda i, me_ref: (me_ref[0], i, 0)),
                     pl.BlockSpec((N_CHIPS - 1, th, c), lambda i, me_ref: (0, i, 0))]
        out_specs.append(pl.BlockSpec((th, c), lambda i, me_ref: (i, 0)))
    return _pcall(body, [a for pair in pairs for a in pair], name="sum_chips", grid=(n_steps,), prefetch=[me],
                  in_specs=in_specs, out_specs=out_specs,
                  out_shape=[jax.ShapeDtypeStruct(parts.shape[1:], F32) for parts, _ in pairs])


def _adam_update(w, gv, m, v, d_ref, nm_ref, nv_ref):
    c1 = 1.0 / (1.0 - ADAM_B1 ** ADAM_STEP)
    c2 = 1.0 / (1.0 - ADAM_B2 ** ADAM_STEP)
    nm = ADAM_B1 * m + (1.0 - ADAM_B1) * gv
    nv = ADAM_B2 * v + (1.0 - ADAM_B2) * (gv * gv)
    nm_ref[...] = nm
    nv_ref[...] = nv
    d_ref[...] = -ADAM_LR * ((nm * c1) / (jnp.sqrt(nv * c2) + ADAM_EPS) + ADAM_WD * w)


def _adamw(w, g, m, v, comm=None):
    r, c = w.shape[0], w.shape[-1]
    tr = _row_tile(r, 512)

    def body(w_ref, g_ref, m_ref, v_ref, d_ref, nm_ref, nv_ref):
        _adam_update(w_ref[...], g_ref[...], m_ref[...], v_ref[...], d_ref, nm_ref, nv_ref)

    mid = (1,) * (w.ndim - 2)
    spec = pl.BlockSpec((tr,) + mid + (c,), lambda i: (i,) + (0,) * (w.ndim - 1))
    sds = jax.ShapeDtypeStruct(w.shape, F32)
    return _pcall(body, [w, g, m, v], name="adamw", grid=(r // tr,), out_shape=[sds, sds, sds],
                  in_specs=[spec] * 4, out_specs=[spec] * 3, comm=comm)


def _adamw_vectors(items):
    k = len(items)

    def body(*refs):
        ins, outs = refs[:4 * k], refs[4 * k:]
        for q in range(k):
            w_ref, g_ref, m_ref, v_ref = ins[4 * q:4 * q + 4]
            _adam_update(w_ref[...], g_ref[...], m_ref[...], v_ref[...], *outs[3 * q:3 * q + 3])

    return _pcall(body, [a for it in items for a in it], name="adamw_vectors",
                  out_shape=[jax.ShapeDtypeStruct(it[0].shape, F32) for it in items for _ in range(3)])


def _adamw_halves(items, comm=None):
    k = len(items)
    r, c = items[0][0].shape
    h = r // 2
    tr = _row_tile(h, min(512, (VMEM_LIMIT * 3 // 4) // (k * 9 * 2 * 4 * c)))
    nb = h // tr
    core = lax.axis_index("c")

    def body(c_ref, *refs):
        ins, outs = refs[:5 * k], refs[5 * k:]
        for q in range(k):
            w_ref, gm_ref, go_ref, m_ref, v_ref = ins[5 * q:5 * q + 5]
            g_ref, d_ref, nm_ref, nv_ref = outs[4 * q:4 * q + 4]
            gv = jnp.where(pl.program_id(0) == c_ref[0], gm_ref[...], go_ref[...])
            g_ref[...] = gv
            _adam_update(w_ref[...], gv, m_ref[...], v_ref[...], d_ref, nm_ref, nv_ref)

    full = pl.BlockSpec((tr, c), lambda hh, i, c_ref: (hh * nb + i, 0))
    half = pl.BlockSpec((tr, c), lambda hh, i, c_ref: (i, 0))
    sds = jax.ShapeDtypeStruct((r, c), F32)
    return _pcall(body, [a for it in items for a in it], name="adamw_halves", grid=(2, nb), prefetch=[core],
                  out_shape=[sds] * (4 * k), in_specs=[full, half, half, full, full] * k, out_specs=[full] * (4 * k),
                  comm=comm)


SC_CORES, SC_TILES, SC_LANES = 2, 16, 16
SC_BLOCK_ROWS, SC_BLOCK_COLS = 8, 512


def _sc_adamw_halves(items):
    k = len(items)
    r, c = items[0][0].shape
    h = r // 2
    bc = min(c, SC_BLOCK_COLS)
    c1 = 1.0 / (1.0 - ADAM_B1 ** ADAM_STEP)
    c2 = 1.0 / (1.0 - ADAM_B2 ** ADAM_STEP)
    mesh = plsc.VectorSubcoreMesh(core_axis_name="sc_core", subcore_axis_name="sc_tile",
                                  num_cores=SC_CORES, num_subcores=SC_TILES)
    spec = pl.BlockSpec(block_shape=(SC_BLOCK_ROWS, bc), index_map=lambda i, j: (i, j))

    def block(w_v, gin_v, m_v, v_v, g_v, d_v, nm_v, nv_v):
        @pl.loop(0, SC_BLOCK_ROWS)
        def _(row):
            @pl.loop(0, bc, step=SC_LANES)
            def _(col):
                at = (pl.ds(row, 1), pl.ds(col, SC_LANES))
                gv = gin_v.at[*at][...]
                nm = ADAM_B1 * m_v.at[*at][...] + (1.0 - ADAM_B1) * gv
                nv = ADAM_B2 * v_v.at[*at][...] + (1.0 - ADAM_B2) * (gv * gv)
                g_v.at[*at][...] = gv
                nm_v.at[*at][...] = nm
                nv_v.at[*at][...] = nv
                d_v.at[*at][...] = -ADAM_LR * ((nm * c1) / (jnp.sqrt(nv * c2) + ADAM_EPS) + ADAM_WD * w_v.at[*at][...])

    def kern(*refs):
        ins, outs = refs[:5 * k], refs[5 * k:]
        core = lax.axis_index("c")

        def half(q, hh, mine):
            w_hbm, gm_hbm, go_hbm, m_hbm, v_hbm = ins[5 * q:5 * q + 5]
            rows = pl.ds(hh * h, h)
            pltpu.emit_pipeline(
                block, grid=(h // SC_BLOCK_ROWS, c // bc), in_specs=[spec] * 4, out_specs=[spec] * 4,
                core_axis_name=("sc_core", "sc_tile"), dimension_semantics=(pltpu.PARALLEL, pltpu.PARALLEL),
                trace_scopes=False,
            )(w_hbm.at[rows, :], gm_hbm if mine else go_hbm, m_hbm.at[rows, :], v_hbm.at[rows, :],
              *(o.at[rows, :] for o in outs[4 * q:4 * q + 4]))

        for q in range(k):
            for hh in range(2):
                pl.when(core == hh)(lambda q=q, hh=hh: half(q, hh, True))
                pl.when(core != hh)(lambda q=q, hh=hh: half(q, hh, False))

    sds = jax.ShapeDtypeStruct((r, c), F32)
    return pl.kernel(kern, out_type=[sds] * (4 * k), mesh=mesh, scratch_types=[], name="sc_adamw_halves")(
        *(a for it in items for a in it))


def _wgrad(name, a, b, a_spec, b_spec, m, n, nb, comm):
    def body(a_ref, b_ref, o_ref):
        o_ref[...] = _dot_tn(a_ref[...], b_ref[...]).astype(o_ref.dtype)

    return _pcall(body, [a, b], name=name, grid=(nb,), out_shape=jax.ShapeDtypeStruct((nb, m, n), BF),
                  in_specs=[a_spec, b_spec], out_specs=pl.BlockSpec((None, m, n), lambda j: (j, 0, 0)), comm=comm)


def _wgrad_cols(name, a, b, nb, comm=None):
    t_tok, m = a.shape
    n = b.shape[1] // nb
    return _wgrad(name, a, b, pl.BlockSpec((t_tok, m), lambda j: (0, 0)), pl.BlockSpec((t_tok, n), lambda j: (0, j)),
                  m, n, nb, comm)


def _wgrad_rows(name, a, b, nb, comm=None):
    t_tok, n = b.shape
    m = a.shape[1] // nb
    return _wgrad(name, a, b, pl.BlockSpec((t_tok, m), lambda j: (0, j)), pl.BlockSpec((t_tok, n), lambda j: (0, 0)),
                  m, n, nb, comm)


def _wgrad_a_shared(name, a, b4, comm=None):
    t_tok, m = a.shape
    nb, _, n = b4.shape
    return _wgrad(name, a, b4, pl.BlockSpec((t_tok, m), lambda j: (0, 0)),
                  pl.BlockSpec((None, t_tok, n), lambda j: (j, 0, 0)), m, n, nb, comm)


def _wgrad_b_shared(name, a4, b, comm=None):
    nb, t_tok, m = a4.shape
    n = b.shape[1]
    return _wgrad(name, a4, b, pl.BlockSpec((None, t_tok, m), lambda j: (j, 0, 0)),
                  pl.BlockSpec((t_tok, n), lambda j: (0, 0)), m, n, nb, comm)


def _w4_spec(r, c):
    return pl.BlockSpec((None, r, c), lambda i, j: (j, 0, 0))


FFN_ROW_CHUNK = 256


def _row_chunks(tm):
    rc = FFN_ROW_CHUNK if tm % FFN_ROW_CHUNK == 0 else tm
    return [slice(r, r + rc) for r in range(0, tm, rc)]


def _ffn_fwd(h, ln, wg4, wu4, wd4, comm=None):
    t_tok, d = h.shape
    f = wg4.shape[-2]
    tm = _tile(t_tok, 512)

    def body(h_ref, ln_ref, wg_ref, wu_ref, wd_ref, ho_ref, n_ref, g_ref, u_ref, n_s, acc):
        j = pl.program_id(1)

        @pl.when(j == 0)
        def _():
            xv = h_ref[...]
            nv = (xv * _rstd(xv) * ln_ref[...]).astype(BF)
            n_s[...] = nv
            n_ref[...] = nv
            acc[...] = jnp.zeros_like(acc)

        nv = n_s[...]
        g = _dot_nt(nv, wg_ref[...])
        u = _dot_nt(nv, wu_ref[...])
        g_ref[...] = g.astype(BF)
        u_ref[...] = u.astype(BF)
        a = (g * _sigmoid(g) * u).astype(BF)
        acc[...] += _dot(a, wd_ref[...])

        @pl.when(j == N_CHIPS - 1)
        def _():
            ho_ref[...] = h_ref[...] + 0.5 * acc[...]

    row = pl.BlockSpec((tm, d), lambda i, j: (i, 0))
    gu = pl.BlockSpec((None, tm, f), lambda i, j: (j, i, 0))
    gu_sds = jax.ShapeDtypeStruct((N_CHIPS, t_tok, f), BF)
    return _pcall(
        body, [h, ln, wg4, wu4, wd4], name="ffn_fwd", grid=(t_tok // tm, N_CHIPS),
        out_shape=[jax.ShapeDtypeStruct((t_tok, d), F32), jax.ShapeDtypeStruct((t_tok, d), BF), gu_sds, gu_sds],
        in_specs=[row, pl.BlockSpec((1, d), lambda i, j: (0, 0)), _w4_spec(f, d), _w4_spec(f, d), _w4_spec(f, d)],
        out_specs=[row, row, gu, gu],
        scratch=[pltpu.VMEM((tm, d), BF), pltpu.VMEM((tm, d), F32)], comm=comm)


def _ffn_bwd(dho, h, ln, g4, u4, wg4, wu4, wd4, comm=None):
    t_tok, d = h.shape
    f = wg4.shape[-2]
    tm = _tile(t_tok, 512)

    def body(dho_ref, h_ref, ln_ref, g_ref, u_ref, wg_ref, wu_ref, wd_ref,
             dhi_ref, dln_ref, dg_ref, du_ref, a_ref, dhb_ref, dhb_s, dn_acc):
        i, j = pl.program_id(0), pl.program_id(1)

        @pl.when(j == 0)
        def _():
            dhb = (0.5 * dho_ref[...]).astype(BF)
            dhb_s[...] = dhb
            dhb_ref[...] = dhb
            dn_acc[...] = jnp.zeros_like(dn_acc)

        @pl.when((i == 0) & (j == 0))
        def _():
            dln_ref[...] = jnp.zeros_like(dln_ref)

        for rows in _row_chunks(tm):
            g = g_ref[rows, :].astype(F32)
            u = u_ref[rows, :].astype(F32)
            s = _sigmoid(g)
            sg = g * s
            a_ref[rows, :] = (sg * u).astype(BF)
            da = _dot_nt(dhb_s[rows, :], wd_ref[...])
            dg = (da * u * (s * (1.0 + g * (1.0 - s)))).astype(BF)
            du = (da * sg).astype(BF)
            dg_ref[rows, :] = dg
            du_ref[rows, :] = du
            dn_acc[rows, :] += _dot(dg, wg_ref[...]) + _dot(du, wu_ref[...])

        @pl.when(j == N_CHIPS - 1)
        def _():
            xv = h_ref[...]
            dx, dln = _rms_bwd(dn_acc[...], xv, _rstd(xv), ln_ref[...])
            dln_ref[...] += dln
            dhi_ref[...] = dho_ref[...] + dx

    row = pl.BlockSpec((tm, d), lambda i, j: (i, 0))
    vec = pl.BlockSpec((1, d), lambda i, j: (0, 0))
    gu = pl.BlockSpec((None, tm, f), lambda i, j: (j, i, 0))
    gu_sds = jax.ShapeDtypeStruct((N_CHIPS, t_tok, f), BF)
    return _pcall(
        body, [dho, h, ln, g4, u4, wg4, wu4, wd4], name="ffn_bwd", grid=(t_tok // tm, N_CHIPS),
        out_shape=[jax.ShapeDtypeStruct((t_tok, d), F32), jax.ShapeDtypeStruct((1, d), F32),
                   gu_sds, gu_sds, gu_sds, jax.ShapeDtypeStruct((t_tok, d), BF)],
        in_specs=[row, row, vec, gu, gu, _w4_spec(f, d), _w4_spec(f, d), _w4_spec(f, d)],
        out_specs=[row, vec, gu, gu, gu, row],
        scratch=[pltpu.VMEM((tm, d), BF), pltpu.VMEM((tm, d), F32)], comm=comm)


def _rope_tables(pos_col, inv_freq2, blocks, comm=None):
    t_tok = pos_col.shape[0]
    k = len(blocks)
    n_steps = 4
    assert all(b.shape[0] % (16 * n_steps) == 0 for b in blocks)

    def body(p_ref, f_ref, *refs):
        cos_ref, sin_ref = refs[k:k + 2]
        ang = p_ref[...] * f_ref[...]
        lane = lax.broadcasted_iota(jnp.int32, ang.shape, 1)
        s = jnp.sin(ang)
        cos_ref[...] = jnp.cos(ang)
        sin_ref[...] = jnp.where((lane & 1) == 0, -s, s)
        for a_ref, o_ref in zip(refs[:k], refs[k + 2:]):
            o_ref[...] = a_ref[...].astype(BF)

    rows = lambda r, c: pl.BlockSpec((r // n_steps, c), lambda i: (i, 0))
    casts = [rows(*b.shape) for b in blocks]
    sds = jax.ShapeDtypeStruct((t_tok, 128), F32)
    return _pcall(body, [pos_col, inv_freq2, *blocks], name="rope_tables", grid=(n_steps,),
                  out_shape=[sds, sds] + [jax.ShapeDtypeStruct(b.shape, BF) for b in blocks],
                  in_specs=[rows(t_tok, 1), pl.BlockSpec((1, 128), lambda i: (0, 0))] + casts,
                  out_specs=[rows(t_tok, 128)] * 2 + casts, comm=comm)


def _swap_pairs(x):
    lane = lax.broadcasted_iota(jnp.int32, x.shape, 1)
    return jnp.where((lane & 1) == 0, pltpu.roll(x, 127, 1), pltpu.roll(x, 1, 1))


def _mix_in(h, ln, w_in4, wm4, b_m, cos_t, sin_t, comm=None):
    t_tok, d = h.shape
    cm = wm4.shape[-1]
    tm = _tile(t_tok, 256)
    rows = IN_COLS // N_CHIPS

    def body(h_ref, ln_ref, win4_ref, wm_ref, bm_ref, cos_ref, sin_ref,
             u_ref, rq_ref, rk_ref, rv_ref, rg_ref, fq_ref, fk_ref, fv_ref, ff_ref, ga_ref, gb_ref, win_ref):
        @pl.when(pl.program_id(0) == 0)
        def _():
            for j in range(N_CHIPS):
                win_ref[j * rows:(j + 1) * rows, :] = win4_ref[j, :rows, :]
            win_ref[IN_COLS:, :] = jnp.zeros((IN_PAD - IN_COLS, d), BF)

        xv = h_ref[...]
        ub = (xv * _rstd(xv) * ln_ref[...]).astype(BF)
        u_ref[...] = ub
        cosv, sinv = cos_ref[...], sin_ref[...]

        def sec(k):
            return _dot_nt(ub, win_ref[k * 512:(k + 1) * 512, :])

        def rot(xh):
            return xh * cosv + _swap_pairs(xh) * sinv

        pq, pk = sec(0), sec(1)
        for hh in range(RET_HEADS):
            sl = slice(hh * RET_DIM, (hh + 1) * RET_DIM)
            rq_ref[:, sl] = rot(pq[:, sl]).astype(BF)
            rk_ref[:, sl] = (rot(pk[:, sl]) * RET_SCALE).astype(BF)
        rv_ref[...] = sec(2).astype(BF)
        rg_ref[...] = sec(3).astype(BF)
        fq_ref[...] = (sec(4) * FOX_SCALE).astype(BF)
        fk_ref[...] = sec(5).astype(BF)
        fv_ref[...] = sec(6).astype(BF)
        ff_ref[...] = _dot_nt(ub, win_ref[FF_COL:FF_COL + 128, :])
        for j in range(N_CHIPS):
            gs = _sigmoid(_dot(ub, wm_ref[j]) + bm_ref[:, j * cm:(j + 1) * cm]).astype(BF)
            col = j * cm
            if col < d:
                ga_ref[:, col:col + cm] = gs
            else:
                gb_ref[:, col - d:col - d + cm] = gs

    row = lambda c: pl.BlockSpec((tm, c), lambda i: (i, 0))
    full = lambda *s: pl.BlockSpec(s, lambda i: (0,) * len(s))
    sds = lambda c, dt: jax.ShapeDtypeStruct((t_tok, c), dt)
    return _pcall(
        body, [h, ln, w_in4, wm4, b_m, cos_t, sin_t], name="mix_in", grid=(t_tok // tm,),
        out_shape=[sds(d, BF)] + [sds(512, BF)] * 7 + [sds(128, F32), sds(d, BF), sds(d, BF),
                                                       jax.ShapeDtypeStruct((IN_PAD, d), BF)],
        in_specs=[row(d), full(1, d), full(*w_in4.shape), full(N_CHIPS, d, cm), full(1, 2 * d), row(128),
                  row(128)],
        out_specs=[row(d)] + [row(512)] * 7 + [row(128), row(d), row(d), full(IN_PAD, d)], comm=comm)


def _split3(x):
    hi = x.astype(BF)
    r1 = x - hi.astype(F32)
    mid = r1.astype(BF)
    lo = (r1 - mid.astype(F32)).astype(BF)
    return hi, mid, lo


def _aug_lane():
    return lax.broadcasted_iota(jnp.int32, (1, 128), 1) & (FOX_DIM - 1)


def _aug_put(base, k0, parts):
    w = _aug_lane()
    for i, part in enumerate(parts):
        base = jnp.where(w == k0 + i, part, base)
    return base


def _forget_fwd(ffl, b_pad):
    t_tok = ffl.shape[0]
    tb = _tile(t_tok, 256)

    def body(ff_ref, b_ref, aq_ref, ak_ref, cum_s):
        r = lax.broadcasted_iota(jnp.int32, (tb, tb), 0)
        c = lax.broadcasted_iota(jnp.int32, (tb, tb), 1)
        tri = jnp.where(c <= r, 1.0, 0.0).astype(BF)
        carry = jnp.zeros((1, 128), F32)
        for i in range(t_tok // tb):
            z = ff_ref[i * tb:(i + 1) * tb, :] + b_ref[...]
            lf = jnp.minimum(z, 0.0) - jnp.log(1.0 + jnp.exp(-jnp.abs(z)))
            hi, mid, lo = _split3(lf)
            cs = _dot(tri, hi) + _dot(tri, mid) + _dot(tri, lo) + carry
            cum_s[i * tb:(i + 1) * tb, :] = cs
            carry = cs[tb - 1:tb, :]
        x = cum_s[...]
        first = lax.broadcasted_iota(jnp.int32, (1, 128), 1) < FOX_DIM
        w = _aug_lane()
        one = jnp.ones((t_tok, 128), BF)
        zero = jnp.zeros((t_tok, 128), BF)
        for pp in range(FOX_HEADS // 2):
            other = jnp.where(first, x[:, 2 * pp + 1:2 * pp + 2], x[:, 2 * pp:2 * pp + 1])
            parts = _split3(other)
            aq = jnp.where((w >= 3) & (w < 6), one, zero)
            ak = jnp.where((w < 3) | ((w >= 6) & (w < 9)), one, zero)
            aq_ref[:, pp * 128:(pp + 1) * 128] = _aug_put(aq, 0, parts)
            ak_ref[:, pp * 128:(pp + 1) * 128] = _aug_put(ak, 3, [-q for q in parts])

    sds = jax.ShapeDtypeStruct((t_tok, FOX_WIDTH), BF)
    return _pcall(body, [ffl, b_pad], name="forget_fwd", out_shape=[sds, sds],
                  scratch=[pltpu.VMEM((t_tok, 128), F32)])


def _forget_bwd(dcum_t, dcum_q, ffl, b_pad):
    t_tok = ffl.shape[0]
    tb = _tile(t_tok, 256)

    def body(dc_ref, dq_ref, ff_ref, b_ref, dff_ref, db_ref, pad_s, d_s):
        pad_s[...] = jnp.zeros_like(pad_s)
        pad_s[0:FOX_HEADS, :] = dc_ref[...]
        dsum = pad_s[...].T
        lane = lax.broadcasted_iota(jnp.int32, (t_tok, 128), 1)
        for hh in range(FOX_HEADS):
            dsum = dsum + jnp.where(lane == hh, dq_ref[:, hh * FOX_DIM:hh * FOX_DIM + 1], 0.0)
        d_s[...] = dsum
        r = lax.broadcasted_iota(jnp.int32, (tb, tb), 0)
        c = lax.broadcasted_iota(jnp.int32, (tb, tb), 1)
        tri = jnp.where(c >= r, 1.0, 0.0).astype(BF)
        carry = jnp.zeros((1, 128), F32)
        db = jnp.zeros((1, 128), F32)
        for i in reversed(range(t_tok // tb)):
            hi, mid, lo = _split3(d_s[i * tb:(i + 1) * tb, :])
            dlf = _dot(tri, hi) + _dot(tri, mid) + _dot(tri, lo) + carry
            carry = dlf[0:1, :]
            z = ff_ref[i * tb:(i + 1) * tb, :] + b_ref[...]
            dff = dlf * _sigmoid(-z)
            dff_ref[i * tb:(i + 1) * tb, :] = dff.astype(BF)
            db = db + jnp.sum(dff, axis=0, keepdims=True)
        db_ref[...] = db

    return _pcall(
        body, [dcum_t, dcum_q, ffl, b_pad], name="forget_bwd",
        out_shape=[jax.ShapeDtypeStruct((t_tok, 128), BF), jax.ShapeDtypeStruct((1, 128), F32)],
        scratch=[pltpu.VMEM((128, t_tok), F32), pltpu.VMEM((t_tok, 128), F32)])


def _first_half():
    return lax.broadcasted_iota(jnp.int32, (1, 128), 1) < FOX_DIM


def _head_rows(x2, a2, hh):
    return jnp.where(_first_half(), x2, a2) if hh == 0 else jnp.where(_first_half(), a2, x2)


def _head_only(x2, hh):
    zero = jnp.zeros_like(x2)
    return jnp.where(_first_half(), x2, zero) if hh == 0 else jnp.where(_first_half(), zero, x2)


def _causal_diag(s):
    rows = lax.broadcasted_iota(jnp.int32, s.shape, 0)
    cols = lax.broadcasted_iota(jnp.int32, s.shape, 1)
    return jnp.where(cols <= rows, s, NEG)


def _diag_or_below(qi, ki, step):
    pl.when(ki < qi)(lambda: step(False))
    pl.when(ki == qi)(lambda: step(True))


def _tri_rows(s, n):
    qi = sum((s >= r * (r + 1) // 2).astype(jnp.int32) for r in range(1, n))
    return qi, s - (qi * (qi + 1)) // 2


def _tri_cols(s, n):
    ki = sum((s >= k * n - k * (k - 1) // 2).astype(jnp.int32) for k in range(1, n))
    return ki, ki + s - (ki * n - (ki * (ki - 1)) // 2)


def _fox_fwd(fq, fk, fv, aq, ak, comm=None):
    t_tok = fq.shape[0]
    t = _tile(t_tok, 512)
    nq = t_tok // t
    npair = FOX_HEADS // 2

    def body(q_ref, k_ref, v_ref, aq_ref, ak_ref, o_ref, of_ref, aqb_ref, m_s, l_s, acc_s):
        qi, ki = _tri_rows(pl.program_id(1), nq)

        @pl.when(ki == 0)
        def _():
            m_s[...] = jnp.full_like(m_s, NEG)
            l_s[...] = jnp.zeros_like(l_s)
            acc_s[...] = jnp.zeros_like(acc_s)

        def step(diag):
            q2, k2, v2, aq2, ak2 = q_ref[...], k_ref[...], v_ref[...], aq_ref[...], ak_ref[...]
            for hh in range(2):
                s = _dot_nt(_head_rows(q2, aq2, hh), _head_rows(k2, ak2, hh))
                if diag:
                    s = _causal_diag(s)
                m_prev = m_s[hh]
                m_new = jnp.maximum(m_prev, jnp.max(s, axis=1, keepdims=True))
                alpha = jnp.exp(m_prev - m_new)
                p = jnp.exp(s - jnp.tile(m_new, (1, t // 128)))
                l_s[hh] = alpha * l_s[hh] + jnp.sum(p, axis=1, keepdims=True)
                acc_s[hh] = alpha * acc_s[hh] + _dot(p.astype(BF), v2)
                m_s[hh] = m_new

        _diag_or_below(qi, ki, step)

        @pl.when(ki == qi)
        def _():
            first = _first_half()
            o = jnp.where(first, acc_s[0] / l_s[0], acc_s[1] / l_s[1])
            o_ref[...] = o.astype(BF)
            of_ref[...] = o
            other = jnp.where(first, m_s[1] + jnp.log(l_s[1]), m_s[0] + jnp.log(l_s[0]))
            aqb_ref[...] = _aug_put(aq_ref[...], 6, _split3(-other))

    qs = pl.BlockSpec((t, 128), lambda p, s: (_tri_rows(s, nq)[0], p))
    ks = pl.BlockSpec((t, 128), lambda p, s: (_tri_rows(s, nq)[1], p))
    stat = pltpu.VMEM((2, t, 128), F32)
    return _pcall(
        body, [fq, fk, fv, aq, ak], name="fox_fwd", grid=(npair, nq * (nq + 1) // 2),
        out_shape=[jax.ShapeDtypeStruct((t_tok, FOX_WIDTH), BF), jax.ShapeDtypeStruct((t_tok, FOX_WIDTH), F32),
                   jax.ShapeDtypeStruct((t_tok, FOX_WIDTH), BF)],
        in_specs=[qs, ks, ks, qs, ks], out_specs=[qs, qs, qs], scratch=[stat, stat, stat], comm=comm)


def _fox_ds(q2, k2, v2, do2, aq2, ak2, ad2, hh, diag):
    s = _dot_nt(_head_rows(q2, aq2, hh), _head_rows(k2, ak2, hh))
    if diag:
        s = _causal_diag(s)
    p = jnp.exp(s)
    av = jnp.where(_aug_lane() < 3, 1.0, 0.0).astype(BF)
    dp = _dot_nt(_head_rows(do2, ad2, hh), _head_rows(v2, jnp.broadcast_to(av, v2.shape), hh))
    return p, p * dp


def _fox_bwd(fq, fk, fv, do, aqb, ak, ad, comm=None):
    t_tok = fq.shape[0]
    t = _tile(t_tok, 512)
    nq = t_tok // t
    npair = FOX_HEADS // 2
    n_steps = nq * (nq + 1) // 2

    def body(q_ref, k_ref, v_ref, do_ref, aq_ref, ak_ref, ad_ref, dq_ref, dk_ref, dv_ref, dck_ref, dcq_ref,
             dk_s, dv_s, dq_s, rs_s):
        step_id = pl.program_id(1)
        ki, qi = _tri_cols(step_id, nq)

        @pl.when(step_id == 0)
        def _():
            dq_s[...] = jnp.zeros_like(dq_s)
            rs_s[...] = jnp.zeros_like(rs_s)

        @pl.when(qi == ki)
        def _():
            dk_s[...] = jnp.zeros_like(dk_s)
            dv_s[...] = jnp.zeros_like(dv_s)
            dck_ref[...] = jnp.zeros_like(dck_ref)

        rows = pl.ds(qi * t if isinstance(qi, int) else pl.multiple_of(qi * t, t), t)

        def step(diag):
            q2, k2, v2, do2 = q_ref[...], k_ref[...], v_ref[...], do_ref[...]
            dq = []
            for hh in range(2):
                p, ds = _fox_ds(q2, k2, v2, do2, aq_ref[...], ak_ref[...], ad_ref[...], hh, diag)
                dsb = ds.astype(BF)
                dv_s[...] += _dot_tn(p.astype(BF), _head_only(do2, hh))
                dk_s[...] += _dot_tn(dsb, _head_only(q2, hh))
                dq.append(_dot(dsb, k2))
                dck_ref[hh] = dck_ref[hh] - jnp.sum(ds, axis=0, keepdims=True)
                rs_s[hh, rows, :] = rs_s[hh, rows, :] + jnp.sum(ds, axis=1, keepdims=True)
            dq_s[rows, :] = dq_s[rows, :] + jnp.where(_first_half(), dq[0], dq[1])

        _diag_or_below(qi, ki, step)

        @pl.when(qi == nq - 1)
        def _():
            dk_ref[...] = dk_s[...].astype(BF)
            dv_ref[...] = dv_s[...].astype(BF)

        @pl.when(step_id == n_steps - 1)
        def _():
            dq_ref[...] = (dq_s[...] * FOX_SCALE).astype(BF)
            dcq_ref[...] = jnp.where(_first_half(), rs_s[0], rs_s[1])

    qs = pl.BlockSpec((t, 128), lambda p, s: (_tri_cols(s, nq)[1], p))
    ks = pl.BlockSpec((t, 128), lambda p, s: (_tri_cols(s, nq)[0], p))
    cks = pl.BlockSpec((2, 1, t), lambda p, s: (p, 0, _tri_cols(s, nq)[0]))
    seq = pl.BlockSpec((t_tok, 128), lambda p, s: (0, p))
    sds = jax.ShapeDtypeStruct((t_tok, FOX_WIDTH), BF)
    return _pcall(
        body, [fq, fk, fv, do, aqb, ak, ad], name="fox_bwd", grid=(npair, n_steps),
        out_shape=[sds, sds, sds, jax.ShapeDtypeStruct((FOX_HEADS, 1, t_tok), F32),
                   jax.ShapeDtypeStruct((t_tok, FOX_WIDTH), F32)],
        in_specs=[qs, ks, ks, qs, qs, ks, qs], out_specs=[seq, ks, ks, cks, seq],
        scratch=[pltpu.VMEM((t, 128), F32), pltpu.VMEM((t, 128), F32), pltpu.VMEM((t_tok, 128), F32),
                 pltpu.VMEM((2, t_tok, 128), F32)], comm=comm)


def _ret_consts():
    c = RET_CHUNK
    log_gamma = jnp.log1p(-jnp.exp2(-5.0 - jnp.arange(RET_HEADS, dtype=F32)))
    idx = jnp.arange(c, dtype=F32)
    diff = idx[:, None] - idx[None, :]
    dmask = jnp.where(diff >= 0, jnp.exp(log_gamma[:, None, None] * jnp.maximum(diff, 0.0)), 0.0)
    qdec = jnp.exp(log_gamma[:, None] * (idx + 1.0))
    kdec = jnp.exp(log_gamma[:, None] * (c - 1 - idx))
    cdec = jnp.exp(log_gamma * c)
    bc = lambda v: jnp.broadcast_to(v[:, :, None], (RET_HEADS, c, RET_DIM))
    return dmask, bc(qdec), bc(kdec), jnp.broadcast_to(cdec[:, None, None], (RET_HEADS, c, RET_DIM))


def _group_norm(y):
    mu = jnp.mean(y, axis=-1, keepdims=True)
    yc = y - mu
    r = lax.rsqrt(jnp.mean(yc * yc, axis=-1, keepdims=True) + EPS)
    return yc * r, r


def _ret_fwd(rq, rk, rv, rg, consts, comm=None):
    t_tok = rq.shape[0]
    nb = 4 if t_tok % (4 * RET_CHUNK) == 0 else 1
    tr = nb * RET_CHUNK
    n_steps = t_tok // tr
    c = RET_CHUNK

    def body(q_ref, k_ref, v_ref, g_ref, dm_ref, qd_ref, kd_ref, cd_ref, y_ref, yo_ref, st_ref, s_s):
        @pl.when(pl.program_id(0) == 0)
        def _():
            s_s[...] = jnp.zeros_like(s_s)

        for b in range(nb):
            rows = slice(b * c, (b + 1) * c)
            for hh in range(RET_HEADS):
                cols = slice(hh * RET_DIM, (hh + 1) * RET_DIM)
                q, k, v = q_ref[rows, cols], k_ref[rows, cols], v_ref[rows, cols]
                state = s_s[hh]
                st_ref[hh, b] = state
                sc = (_dot_nt(q, k) * dm_ref[hh]).astype(BF)
                y = _dot(sc, v) + _dot((q.astype(F32) * qd_ref[hh]).astype(BF), state.astype(BF))
                s_s[hh] = cd_ref[hh] * state + _dot_tn((k.astype(F32) * kd_ref[hh]).astype(BF), v)
                y_ref[rows, cols] = y
                yn, _ = _group_norm(y)
                gate = g_ref[rows, cols].astype(F32)
                yo_ref[rows, cols] = (yn * (gate * _sigmoid(gate))).astype(BF)

    blk = pl.BlockSpec((tr, RET_WIDTH), lambda i: (i, 0))
    cst = pl.BlockSpec((RET_HEADS, c, RET_DIM), lambda i: (0, 0, 0))
    return _pcall(
        body, [rq, rk, rv, rg, *consts], name="ret_fwd", grid=(n_steps,),
        out_shape=[jax.ShapeDtypeStruct((t_tok, RET_WIDTH), F32), jax.ShapeDtypeStruct((t_tok, RET_WIDTH), BF),
                   jax.ShapeDtypeStruct((RET_HEADS, t_tok // c, RET_DIM, RET_DIM), F32)],
        in_specs=[blk] * 4 + [cst] * 4,
        out_specs=[blk, blk, pl.BlockSpec((RET_HEADS, nb, RET_DIM, RET_DIM), lambda i: (0, i, 0, 0))],
        scratch=[pltpu.VMEM((RET_HEADS, RET_DIM, RET_DIM), F32)], comm=comm)


def _ret_bwd(rq, rk, rv, rg, y_raw, dyo, states, consts, cos_t, sin_t, comm=None):
    t_tok = rq.shape[0]
    nb = 4 if t_tok % (4 * RET_CHUNK) == 0 else 1
    tr = nb * RET_CHUNK
    n_steps = t_tok // tr
    c = RET_CHUNK

    def body(q_ref, k_ref, v_ref, g_ref, y_ref, dyo_ref, st_ref, dm_ref, qd_ref, kd_ref, cd_ref,
             cos_ref, sin_ref, dq_ref, dk_ref, dv_ref, dg_ref, ds_s):
        @pl.when(pl.program_id(0) == 0)
        def _():
            ds_s[...] = jnp.zeros_like(ds_s)

        for b in reversed(range(nb)):
            rows = slice(b * c, (b + 1) * c)
            cosv, sinv = cos_ref[rows, :], sin_ref[rows, :]
            for hh in range(RET_HEADS):
                cols = slice(hh * RET_DIM, (hh + 1) * RET_DIM)
                dm, qd, kd, cd = dm_ref[hh], qd_ref[hh], kd_ref[hh], cd_ref[hh]
                q, k, v = q_ref[rows, cols], k_ref[rows, cols], v_ref[rows, cols]
                yn, r = _group_norm(y_ref[rows, cols])
                gate = g_ref[rows, cols].astype(F32)
                sg = _sigmoid(gate)
                dyo = dyo_ref[rows, cols]
                dg_ref[rows, cols] = (dyo * yn * (sg * (1.0 + gate * (1.0 - sg)))).astype(BF)
                dyn = dyo * (gate * sg)
                dy = r * (dyn - jnp.mean(dyn, axis=-1, keepdims=True)
                          - yn * jnp.mean(dyn * yn, axis=-1, keepdims=True))
                dyb = dy.astype(BF)
                state_b = st_ref[hh, b].astype(BF)
                dstate = ds_s[hh]
                dstate_b = dstate.astype(BF)
                qdb = (q.astype(F32) * qd).astype(BF)
                kdb = (k.astype(F32) * kd).astype(BF)
                sc = (_dot_nt(q, k) * dm).astype(BF)
                dv = _dot_tn(sc, dyb) + _dot(kdb, dstate_b)
                dp = (_dot_nt(dyb, v) * dm).astype(BF)
                dq = _dot(dp, k) + _dot_nt(dyb, state_b) * qd
                dk = (_dot_tn(dp, q) + _dot_nt(v, dstate_b) * kd) * RET_SCALE
                ds_s[hh] = cd * dstate + _dot_tn(qdb, dyb)
                dv_ref[rows, cols] = dv.astype(BF)
                dq_ref[rows, cols] = (dq * cosv - _swap_pairs(dq) * sinv).astype(BF)
                dk_ref[rows, cols] = (dk * cosv - _swap_pairs(dk) * sinv).astype(BF)

    rev = lambda i: n_steps - 1 - i
    blk = pl.BlockSpec((tr, RET_WIDTH), lambda i: (rev(i), 0))
    tab = pl.BlockSpec((tr, RET_DIM), lambda i: (rev(i), 0))
    cst = pl.BlockSpec((RET_HEADS, c, RET_DIM), lambda i: (0, 0, 0))
    sds = jax.ShapeDtypeStruct((t_tok, RET_WIDTH), BF)
    return _pcall(
        body, [rq, rk, rv, rg, y_raw, dyo, states, *consts, cos_t, sin_t], name="ret_bwd",
        grid=(n_steps,), out_shape=[sds] * 4,
        in_specs=[blk] * 6 + [pl.BlockSpec((RET_HEADS, nb, RET_DIM, RET_DIM), lambda i: (0, rev(i), 0, 0))]
        + [cst] * 4 + [tab, tab],
        out_specs=[blk] * 4, scratch=[pltpu.VMEM((RET_HEADS, RET_DIM, RET_DIM), F32)], comm=comm)


def _mix_out(h, y_ret, y_fox, ga, gb, wr4, wf4, wo4, comm=None):
    t_tok, d = h.shape
    cz = wr4.shape[-1]
    ro = wo4.shape[-2]
    tm = _tile(t_tok, 512)

    def body(h_ref, yr_ref, yf_ref, ga_ref, gb_ref, wr_ref, wf_ref, wo_ref, ho_ref, za_ref, zb_ref, mix_ref):
        yr, yf = yr_ref[...], yf_ref[...]
        for j in range(N_CHIPS):
            sl = slice(j * cz, (j + 1) * cz)
            za = _dot(yr, wr_ref[j])
            zb = _dot(yf, wf_ref[j])
            za_ref[:, sl] = za.astype(BF)
            zb_ref[:, sl] = zb.astype(BF)
            mix_ref[:, sl] = (ga_ref[:, sl].astype(F32) * za + gb_ref[:, sl].astype(F32) * zb).astype(BF)
        acc = h_ref[...]
        for j in range(N_CHIPS):
            acc = acc + _dot(mix_ref[:, j * ro:(j + 1) * ro], wo_ref[j])
        ho_ref[...] = acc

    row = lambda c: pl.BlockSpec((tm, c), lambda i: (i, 0))
    full = lambda *s: pl.BlockSpec(s, lambda i: (0,) * len(s))
    sds = lambda dt: jax.ShapeDtypeStruct((t_tok, d), dt)
    return _pcall(
        body, [h, y_ret, y_fox, ga, gb, wr4, wf4, wo4], name="mix_out", grid=(t_tok // tm,),
        out_shape=[sds(F32), sds(BF), sds(BF), sds(BF)],
        in_specs=[row(d), row(RET_WIDTH), row(FOX_WIDTH), row(d), row(d),
                  full(N_CHIPS, RET_WIDTH, cz), full(N_CHIPS, FOX_WIDTH, cz), full(N_CHIPS, ro, d)],
        out_specs=[row(d)] * 4, comm=comm)


def _mix_out_bwd(dh, za, zb, ga, gb, y_fox, wr4, wf4, wo4, comm=None):
    t_tok, d = dh.shape
    cz = wr4.shape[-1]
    ro = wo4.shape[-2]
    tm = _tile(t_tok, 256)

    def body(dh_ref, za_ref, zb_ref, ga_ref, gb_ref, yf_ref, wr_ref, wf_ref, wo_ref,
             dhb_ref, dgp_ref, dza_ref, dzb_ref, dyr_ref, dyf_ref, dl_ref, db_ref):
        @pl.when(pl.program_id(0) == 0)
        def _():
            db_ref[...] = jnp.zeros_like(db_ref)

        dhb = dh_ref[...].astype(BF)
        dhb_ref[...] = dhb
        dyr = jnp.zeros((tm, RET_WIDTH), F32)
        dyf = jnp.zeros((tm, FOX_WIDTH), F32)
        for j in range(N_CHIPS):
            sl = slice(j * ro, (j + 1) * ro)
            dmix = _dot_nt(dhb, wo_ref[j])
            ga, gb = ga_ref[:, sl].astype(F32), gb_ref[:, sl].astype(F32)
            dza = (dmix * ga).astype(BF)
            dzb = (dmix * gb).astype(BF)
            dza_ref[:, sl] = dza
            dzb_ref[:, sl] = dzb
            dga = dmix * za_ref[:, sl].astype(F32) * ga * (1.0 - ga)
            dgb = dmix * zb_ref[:, sl].astype(F32) * gb * (1.0 - gb)
            dgp_ref[:, sl] = dga.astype(BF)
            dgp_ref[:, d + j * ro:d + (j + 1) * ro] = dgb.astype(BF)
            db_ref[:, sl] += jnp.sum(dga, axis=0, keepdims=True)
            db_ref[:, d + j * ro:d + (j + 1) * ro] += jnp.sum(dgb, axis=0, keepdims=True)
        for j in range(N_CHIPS):
            sl = slice(j * cz, (j + 1) * cz)
            dyr = dyr + _dot_nt(dza_ref[:, sl], wr_ref[j])
            dyf = dyf + _dot_nt(dzb_ref[:, sl], wf_ref[j])
        dyr_ref[...] = dyr
        dyfb = dyf.astype(BF)
        dyf_ref[...] = dyfb
        prod = dyfb.astype(F32) * yf_ref[...]
        first = _first_half()
        for pp in range(FOX_HEADS // 2):
            blk = prod[:, pp * 128:(pp + 1) * 128]
            s0 = jnp.sum(jnp.where(first, blk, 0.0), axis=1, keepdims=True)
            s1 = jnp.sum(jnp.where(first, 0.0, blk), axis=1, keepdims=True)
            parts = _split3(-jnp.where(first, s1, s0))
            dl_ref[:, pp * 128:(pp + 1) * 128] = _aug_put(jnp.zeros((tm, 128), BF), 0, parts)

    row = lambda c: pl.BlockSpec((tm, c), lambda i: (i, 0))
    full = lambda *s: pl.BlockSpec(s, lambda i: (0,) * len(s))
    sds = lambda c, dt: jax.ShapeDtypeStruct((t_tok, c), dt)
    return _pcall(
        body, [dh, za, zb, ga, gb, y_fox, wr4, wf4, wo4], name="mix_out_bwd", grid=(t_tok // tm,),
        out_shape=[sds(d, BF), sds(2 * d, BF), sds(d, BF), sds(d, BF), sds(RET_WIDTH, F32),
                   sds(FOX_WIDTH, BF), sds(FOX_WIDTH, BF), jax.ShapeDtypeStruct((1, 2 * d), F32)],
        in_specs=[row(d)] * 5 + [row(FOX_WIDTH), full(N_CHIPS, RET_WIDTH, cz), full(N_CHIPS, FOX_WIDTH, cz),
                                 full(N_CHIPS, ro, d)],
        out_specs=[row(d), row(2 * d), row(d), row(d), row(RET_WIDTH), row(FOX_WIDTH), row(FOX_WIDTH),
                   full(1, 2 * d)],
        comm=comm)


def _mix_in_bwd(dh, h, ln, parts, dff, dgpre, w_in, wm4, comm=None):
    t_tok, d = h.shape
    cm = wm4.shape[-1]
    tm = _tile(t_tok, 256)

    def body(dh_ref, h_ref, ln_ref, p0, p1, p2, p3, p4, p5, p6, dff_ref, dgp_ref, win_ref, wm_ref,
             dhi_ref, dln_ref, dproj_ref):
        @pl.when(pl.program_id(0) == 0)
        def _():
            dln_ref[...] = jnp.zeros_like(dln_ref)

        for k, pr in enumerate((p0, p1, p2, p3, p4, p5, p6)):
            dproj_ref[:, k * 512:(k + 1) * 512] = pr[...]
        dproj_ref[:, FF_COL:FF_COL + 128] = dff_ref[...]
        dproj_ref[:, FF_COL + 128:] = jnp.zeros((tm, IN_PAD - FF_COL - 128), BF)
        du = _dot(dproj_ref[...], win_ref[...])
        for j in range(N_CHIPS):
            du = du + _dot_nt(dgp_ref[:, j * cm:(j + 1) * cm], wm_ref[j])
        xv = h_ref[...]
        dx, dln = _rms_bwd(du, xv, _rstd(xv), ln_ref[...])
        dln_ref[...] += dln
        dhi_ref[...] = dh_ref[...] + dx

    row = lambda c: pl.BlockSpec((tm, c), lambda i: (i, 0))
    full = lambda *s: pl.BlockSpec(s, lambda i: (0,) * len(s))
    return _pcall(
        body, [dh, h, ln, *parts, dff, dgpre, w_in, wm4], name="mix_in_bwd", grid=(t_tok // tm,),
        out_shape=[jax.ShapeDtypeStruct((t_tok, d), F32), jax.ShapeDtypeStruct((1, d), F32),
                   jax.ShapeDtypeStruct((t_tok, IN_PAD), BF)],
        in_specs=[row(d), row(d), full(1, d)] + [row(512)] * 7 + [row(128), row(2 * d), full(IN_PAD, d),
                                                                   full(N_CHIPS, d, cm)],
        out_specs=[row(d), full(1, d), row(IN_PAD)], comm=comm)


def _tail(h, p, target, ln_ple, ln_fin, wpg4, wpl4, comm=None):
    t_tok, d = h.shape
    pd = p.shape[1]
    rg = wpg4.shape[-2]
    cp = wpl4.shape[-1]
    tm = _tile(t_tok, 256)

    def body(h_ref, p_ref, t_ref, lp_ref, lf_ref, wg_ref, wp_ref,
             dh_ref, n_ref, dgp_ref, dpe_ref, pb_ref, loss_ref, dlf_ref, dlp_ref, pe_s, dn_s):
        @pl.when(pl.program_id(0) == 0)
        def _():
            loss_ref[...] = jnp.zeros_like(loss_ref)
            dlf_ref[...] = jnp.zeros_like(dlf_ref)
            dlp_ref[...] = jnp.zeros_like(dlp_ref)

        xv = h_ref[...]
        r3 = _rstd(xv)
        nb = (xv * r3 * lp_ref[...]).astype(BF)
        n_ref[...] = nb
        pb = p_ref[...].astype(BF)
        pb_ref[...] = pb
        pgpre = jnp.zeros((tm, d), F32)
        for j in range(N_CHIPS):
            pgpre = pgpre + _dot(nb[:, j * rg:(j + 1) * rg], wg_ref[j])
            pe_s[:, j * cp:(j + 1) * cp] = _dot(pb, wp_ref[j])
        pg = _sigmoid(pgpre)
        pe = pe_s[...]
        h4 = xv + pg * pe
        r4 = _rstd(h4)
        err = h4 * r4 * lf_ref[...] - t_ref[...]
        loss_ref[...] += 0.5 * jnp.sum(jnp.sum(err * err, axis=1, keepdims=True), axis=0, keepdims=True) / d
        dh4, dlf = _rms_bwd(err * (1.0 / d), h4, r4, lf_ref[...])
        dlf_ref[...] += dlf
        dpe_ref[...] = (dh4 * pg).astype(BF)
        dgp = (dh4 * pe * pg * (1.0 - pg)).astype(BF)
        dgp_ref[...] = dgp
        for j in range(N_CHIPS):
            dn_s[:, j * rg:(j + 1) * rg] = _dot_nt(dgp, wg_ref[j])
        dx, dlp = _rms_bwd(dn_s[...], xv, r3, lp_ref[...])
        dlp_ref[...] += dlp
        dh_ref[...] = dh4 + dx

    row = lambda c: pl.BlockSpec((tm, c), lambda i: (i, 0))
    full = lambda *s: pl.BlockSpec(s, lambda i: (0,) * len(s))
    sds = lambda c, dt: jax.ShapeDtypeStruct((t_tok, c), dt)
    vec = jax.ShapeDtypeStruct((1, d), F32)
    return _pcall(
        body, [h, p, target, ln_ple, ln_fin, wpg4, wpl4], name="tail", grid=(t_tok // tm,),
        out_shape=[sds(d, F32), sds(d, BF), sds(d, BF), sds(d, BF), sds(pd, BF),
                   jax.ShapeDtypeStruct((1, 128), F32), vec, vec],
        in_specs=[row(d), row(pd), row(d), full(1, d), full(1, d), full(N_CHIPS, rg, d), full(N_CHIPS, pd, cp)],
        out_specs=[row(d), row(d), row(d), row(d), row(pd), full(1, 128), full(1, d), full(1, d)],
        scratch=[pltpu.VMEM((tm, d), F32), pltpu.VMEM((tm, d), F32)], comm=comm)


BIG = ["w_ffn1_gate", "w_ffn1_up", "w_ffn1_down", "w_in", "w_merge", "w_ret_out", "w_fox_out", "w_out",
       "w_ffn2_gate", "w_ffn2_up", "w_ffn2_down", "w_ple", "w_ple_gate"]
SMALL = ["ln_ffn1", "ln_mix", "b_forget", "b_merge", "ln_ffn2", "ln_ple", "ln_final"]
WEIGHTS = ["ln_ffn1", "w_ffn1_gate", "w_ffn1_up", "w_ffn1_down", "ln_mix", "w_in", "b_forget", "w_merge", "b_merge",
           "w_ret_out", "w_fox_out", "w_out", "ln_ffn2", "w_ffn2_gate", "w_ffn2_up", "w_ffn2_down", "ln_ple",
           "w_ple", "w_ple_gate", "ln_final"]


TRANSPOSED = {"w_ffn1_gate", "w_ffn1_up", "w_ffn2_gate", "w_ffn2_up", "w_in"}
IN_ROWS_PAD = -(-(IN_COLS // N_CHIPS) // 32) * 32


def _pack_small(vals, loss_row):
    rows = [loss_row]
    for name in SMALL:
        v = vals[name].reshape(-1)
        n = -(-v.shape[0] // 128) * 128
        rows.append(jnp.pad(v, (0, n - v.shape[0])).reshape(n // 128, 128))
    packed = jnp.concatenate(rows, axis=0)
    pad = -packed.shape[0] % 8
    return jnp.pad(packed, ((0, pad), (0, 0)))


def _unpack_small(packed, sizes):
    out, r = {}, 1
    for name in SMALL:
        n = sizes[name]
        nr = -(-n // 128)
        out[name] = packed[r:r + nr].reshape(1, nr * 128)[:, :n]
        r += nr
    return out


class _Stage:
    def __init__(self, comm, finish):
        self.comm, self.finish, self.result = comm, finish, None


def _hosted(fn, *a, stages=()):
    if not stages:
        return fn(*a)
    outs, couts = fn(*a, comm=_merge([st.comm for st in stages]))
    for st, o in zip(stages, _split_outs([st.comm for st in stages], couts)):
        st.result = st.finish(o)
    return outs


class _Reducer:
    def __init__(self):
        self.done = {}

    def swap(self, grads):
        names = list(grads)
        return _Stage(_c_half_swap([grads[n] for n in names]),
                      lambda outs: dict(zip(names, _add_halves([(grads[n], o) for n, o in zip(names, outs)]))))

    def exchange(self, parts):
        names = list(parts)
        return _Stage(_c_chip_exchange([parts[n] for n in names]),
                      lambda outs: dict(zip(names, _sum_chips([(parts[n], o) for n, o in zip(names, outs)]))))

    def join(self, halves):
        names = list(halves)
        return _Stage(_c_join([halves[n] for n in names]),
                      lambda outs: self.done.update({n: (halves[n], o) for n, o in zip(names, outs)}))


def kernel(x, p, positions, ln_ffn1, w_ffn1_gate, w_ffn1_up, w_ffn1_down, ln_mix, w_in, b_forget, w_merge, b_merge, w_ret_out, w_fox_out, w_out, ln_ffn2, w_ffn2_gate, w_ffn2_up, w_ffn2_down, ln_ple, w_ple, w_ple_gate, ln_final, loss_target, m_ln_ffn1, m_w_ffn1_gate, m_w_ffn1_up, m_w_ffn1_down, m_ln_mix, m_w_in, m_b_forget, m_w_merge, m_b_merge, m_w_ret_out, m_w_fox_out, m_w_out, m_ln_ffn2, m_w_ffn2_gate, m_w_ffn2_up, m_w_ffn2_down, m_ln_ple, m_w_ple, m_w_ple_gate, m_ln_final, v_ln_ffn1, v_w_ffn1_gate, v_w_ffn1_up, v_w_ffn1_down, v_ln_mix, v_w_in, v_b_forget, v_w_merge, v_b_merge, v_w_ret_out, v_w_fox_out, v_w_out, v_ln_ffn2, v_w_ffn2_gate, v_w_ffn2_up, v_w_ffn2_down, v_ln_ple, v_w_ple, v_w_ple_gate, v_ln_final):
    args = dict(locals())
    w = {n: args[n] for n in WEIGHTS}
    m = {n: args["m_" + n] for n in WEIGHTS}
    v = {n: args["v_" + n] for n in WEIGHTS}
    d = x.shape[-1]
    t_tok = x.shape[1]
    xs, ps, target = x[0], p[0, 0], loss_target[0]
    small = {n: w[n].reshape(1, -1) for n in SMALL}

    def to2d(n, a):
        if n in TRANSPOSED:
            return a[0].T
        return a.reshape(a.shape[-2], a.shape[-1]) if a.ndim == 3 else a.reshape(1, -1)

    def from2d(n, a):
        return a.T[None] if n in TRANSPOSED else a.reshape(w[n].shape)

    def padded(n, a):
        return jnp.pad(a, ((0, IN_ROWS_PAD - a.shape[0]), (0, 0))) if n == "w_in" else a

    core = lax.axis_index("c")
    me = 2 * lax.axis_index("x") + lax.axis_index("y")
    shard = {}

    def set_shard(n, s2):
        s2 = padded(n, s2)
        shard[n] = s2.reshape(1, 2, s2.shape[0] // 2, s2.shape[1])

    first = ["w_ffn1_gate", "w_ffn1_up", "w_ffn1_down"]
    for n in first + ["w_in"]:
        set_shard(n, to2d(n, w[n]).astype(BF))
    full = {}

    def gather(names):
        bufs = [lax.dynamic_update_slice(jnp.zeros((N_CHIPS,) + shard[n].shape[1:], BF), shard[n], (me, 0, 0, 0))
                for n in names]

        def finish(outs):
            full.update({n: o.reshape(N_CHIPS, 2 * o.shape[2], o.shape[3]) for n, o in zip(names, outs)})

        return _Stage(_c_all_gather(bufs), finish)

    half = RET_DIM // 2
    inv_freq = 1.0 / (ROPE_BASE ** (jnp.arange(half, dtype=F32) / half))
    later = [n for n in BIG if n not in shard]
    cos_t, sin_t, *cast = _hosted(_rope_tables, positions[0].astype(F32).reshape(t_tok, 1),
                                  jnp.repeat(inv_freq, 2).reshape(1, RET_DIM), [to2d(n, w[n]) for n in later],
                                  stages=[gather(first)])
    for n, s2 in zip(later, cast):
        set_shard(n, s2)
    consts = _ret_consts()
    b_pad = jnp.pad(small["b_forget"], ((0, 0), (0, 128 - FOX_HEADS)))

    h1, n1, g1, u1 = _hosted(
        _ffn_fwd, xs, small["ln_ffn1"], full["w_ffn1_gate"], full["w_ffn1_up"], full["w_ffn1_down"],
        stages=[gather(["w_in", "w_merge", "w_ret_out", "w_fox_out", "w_out", "w_ple_gate", "w_ple"])])
    u, rq, rk, rv, rg, fq, fk, fv, ffl, ga, gb, w_in_full = _mix_in(
        h1, small["ln_mix"], full["w_in"], full["w_merge"], small["b_merge"], cos_t, sin_t)
    aq, ak = _forget_fwd(ffl, b_pad)
    y_raw, y_ret, states = _ret_fwd(rq, rk, rv, rg, consts)
    y_fox, y_fox32, aqb = _hosted(_fox_fwd, fq, fk, fv, aq, ak,
                                  stages=[gather(["w_ffn2_gate", "w_ffn2_up", "w_ffn2_down"])])
    h2, za, zb, mix = _mix_out(h1, y_ret, y_fox, ga, gb, full["w_ret_out"], full["w_fox_out"], full["w_out"])
    h3, n2, g2, u2 = _ffn_fwd(h2, small["ln_ffn2"], full["w_ffn2_gate"], full["w_ffn2_up"], full["w_ffn2_down"])

    red = _Reducer()
    dh3, n3, dpgpre, dpe, pb, loss, dln_final, dln_ple = _tail(
        h3, ps, target, small["ln_ple"], small["ln_final"], full["w_ple_gate"], full["w_ple"])
    g_f2 = dict(w_ple_gate=_wgrad_rows("wgrad_ple_gate", n3, dpgpre, N_CHIPS),
                w_ple=_wgrad_cols("wgrad_ple", pb, dpe, N_CHIPS))
    dh2, dln_ffn2, dg2, du2, a2, dhb3 = _ffn_bwd(
        dh3, h2, small["ln_ffn2"], g2, u2, full["w_ffn2_gate"], full["w_ffn2_up"], full["w_ffn2_down"])
    g_f2["w_ffn2_gate"] = _wgrad_b_shared("wgrad_ffn2_gate", dg2, n2)
    g_f2["w_ffn2_up"] = _wgrad_b_shared("wgrad_ffn2_up", du2, n2)
    g_f2["w_ffn2_down"] = _wgrad_b_shared("wgrad_ffn2_down", a2, dhb3)

    sw_f2 = red.swap(g_f2)
    dhb2, dgpre, dza, dzb, dy_ret, dy_fox, ad, db_merge = _hosted(
        _mix_out_bwd, dh2, za, zb, ga, gb, y_fox32, full["w_ret_out"], full["w_fox_out"], full["w_out"],
        stages=[sw_f2])
    g_br = dict(w_out=_wgrad_rows("wgrad_out", mix, dhb2, N_CHIPS),
                w_ret_out=_wgrad_cols("wgrad_ret_out", y_ret, dza, N_CHIPS),
                w_fox_out=_wgrad_cols("wgrad_fox_out", y_fox, dzb, N_CHIPS))

    sw_br = red.swap(g_br)
    drq, drk, drv, drg = _hosted(_ret_bwd, rq, rk, rv, rg, y_raw, dy_ret, states, consts, cos_t, sin_t,
                                 stages=[sw_br])
    ex_f2, ex_br = red.exchange(sw_f2.result), red.exchange(sw_br.result)
    dfq, dfk, dfv, dcum_t3, dcum_q = _hosted(_fox_bwd, fq, fk, fv, dy_fox, aqb, ak, ad, stages=[ex_f2, ex_br])
    dff, db_forget = _forget_bwd(dcum_t3.reshape(FOX_HEADS, t_tok), dcum_q, ffl, b_pad)
    dh1, dln_mix, dproj = _hosted(
        _mix_in_bwd, dh2, h1, small["ln_mix"], (drq, drk, drv, drg, dfq, dfk, dfv), dff, dgpre, w_in_full,
        full["w_merge"], stages=[red.join(ex_f2.result), red.join(ex_br.result)])

    results = {}
    for names in (["w_ffn2_gate", "w_ffn2_up", "w_ffn2_down"], ["w_out", "w_ple_gate"], ["w_ret_out", "w_fox_out"],
                  ["w_ple"]):
        res = _sc_adamw_halves([(to2d(n, w[n]), *red.done[n], to2d(n, m[n]), to2d(n, v[n])) for n in names])
        for q, n in enumerate(names):
            results[n] = tuple(from2d(n, a) for a in res[4 * q:4 * q + 4])

    dx, dln_ffn1, dg1, du1, a1, dhb1 = _ffn_bwd(
        dh1, xs, small["ln_ffn1"], g1, u1, full["w_ffn1_gate"], full["w_ffn1_up"], full["w_ffn1_down"])
    g_f1g = _wgrad_b_shared("wgrad_ffn1_gate", dg1, n1)
    sw_f1g = red.swap(dict(w_ffn1_gate=g_f1g))
    g_f1u = _hosted(_wgrad_b_shared, "wgrad_ffn1_up", du1, n1, stages=[sw_f1g])
    ex_f1g, sw_f1u = red.exchange(sw_f1g.result), red.swap(dict(w_ffn1_up=g_f1u))
    g_f1d = _hosted(_wgrad_b_shared, "wgrad_ffn1_down", a1, dhb1, stages=[ex_f1g, sw_f1u])

    ex_f1u, sw_f1d = red.exchange(sw_f1u.result), red.swap(dict(w_ffn1_down=g_f1d))
    g_in = _hosted(_wgrad_rows, "wgrad_in", dproj, u, IN_PAD // 512,
                   stages=[ex_f1u, sw_f1d, red.join(ex_f1g.result)])
    g_in = g_in.reshape(IN_PAD, d)[:IN_COLS].reshape(N_CHIPS, IN_COLS // N_CHIPS, d)
    g_in = jnp.pad(g_in, ((0, 0), (0, IN_ROWS_PAD - IN_COLS // N_CHIPS), (0, 0)))
    ex_f1d, sw_in = red.exchange(sw_f1d.result), red.swap(dict(w_in=g_in))
    g_mrg = _hosted(_wgrad_cols, "wgrad_merge", u, dgpre, N_CHIPS,
                    stages=[ex_f1d, sw_in, red.join(ex_f1u.result)])

    small_grads = dict(ln_ffn1=dln_ffn1, ln_mix=dln_mix, b_forget=db_forget[:, :FOX_HEADS], b_merge=db_merge,
                       ln_ffn2=dln_ffn2, ln_ple=dln_ple, ln_final=dln_final)
    sizes = {n: w[n].size for n in SMALL}
    ex_in, sw_mrg = red.exchange(sw_in.result), red.swap(dict(w_merge=g_mrg))
    reduced = _hosted(_all_reduce_small, _pack_small(small_grads, loss),
                      stages=[ex_in, sw_mrg, red.join(ex_f1d.result)])
    gsum = _unpack_small(reduced, sizes)
    loss = reduced[0, 0]
    ex_mrg = red.exchange(sw_mrg.result)
    _hosted(_exchange_only, stages=[ex_mrg, red.join(ex_in.result)])
    _hosted(_exchange_only, stages=[red.join(ex_mrg.result)])

    def update(names):
        w2, m2, v2 = ([to2d(n, a[n]) for n in names] for a in (w, m, v))
        n = names[0]
        if n == "w_in":
            mine, other = red.done[n]
            g2 = jnp.where(core == 0, jnp.concatenate([mine, other]), jnp.concatenate([other, mine]))
            g2 = g2[:w2[0].shape[0]]
            rows3 = lambda a: jnp.transpose(a, (2, 0, 1))
            g3 = g2.reshape(g2.shape[0], 1, g2.shape[1])
            res = [g3] + _adamw(rows3(w[n]), g3, rows3(m[n]), rows3(v[n]))
            results[n] = tuple(jnp.transpose(a, (1, 2, 0)) for a in res)
            return
        res = _adamw_halves([(w2[q], *red.done[names[q]], m2[q], v2[q]) for q in range(len(names))])
        for q, name in enumerate(names):
            results[name] = tuple(from2d(name, a) for a in res[4 * q:4 * q + 4])

    res = _adamw_vectors([(to2d(n, w[n]), gsum[n], to2d(n, m[n]), to2d(n, v[n])) for n in SMALL])
    for q, n in enumerate(SMALL):
        results[n] = tuple(from2d(n, a) for a in [gsum[n]] + res[3 * q:3 * q + 3])
    update(["w_ffn1_gate", "w_ffn1_up", "w_ffn1_down"])
    for n in WEIGHTS:
        if n not in results:
            update([n])

    outs = [[results[n][k] for n in WEIGHTS] for k in range(4)]
    return (loss, dx[None], *outs[0], *outs[1], *outs[2], *outs[3])
```

```python
import functools
import operator

import jax
import jax.numpy as jnp
from jax import lax
from jax.experimental import pallas as pl
from jax.experimental.pallas import tpu as pltpu
from jax.experimental.pallas import tpu_sc as plsc

F32 = jnp.float32
BF = jnp.bfloat16
MESH = pl.DeviceIdType.MESH

EPS = 1e-6
ROPE_BASE = 10000.0
N_CHIPS = 4
RET_HEADS = 4
RET_DIM = 128
RET_WIDTH = RET_HEADS * RET_DIM
RET_CHUNK = 128
RET_SCALE = RET_DIM ** -0.5
FOX_HEADS = 8
FOX_DIM = 64
FOX_WIDTH = FOX_HEADS * FOX_DIM
FOX_SCALE = FOX_DIM ** -0.5
IN_COLS = 4 * RET_WIDTH + 3 * FOX_WIDTH + FOX_HEADS
IN_PAD = 4096
FF_COL = 4 * RET_WIDTH + 3 * FOX_WIDTH
NEG = -1e30

ADAM_LR = 0.001
ADAM_B1 = 0.9
ADAM_B2 = 0.999
ADAM_EPS = 1e-08
ADAM_WD = 0.01
ADAM_STEP = 10

VMEM_LIMIT = 52 * 1024 * 1024

RELAY_MIN_STEPS = 16

NT = (((1,), (1,)), ((), ()))
TN = (((0,), (0,)), ((), ()))

HBM_SPEC = pl.BlockSpec(memory_space=pltpu.HBM)
VMEM_SPEC = pl.BlockSpec(memory_space=pltpu.VMEM)


def _dot(a, b):
    return jnp.dot(a, b, preferred_element_type=F32)


def _dot_nt(a, b):
    return lax.dot_general(a, b, NT, preferred_element_type=F32)


def _dot_tn(a, b):
    return lax.dot_general(a, b, TN, preferred_element_type=F32)


def _rstd(xv):
    return lax.rsqrt(jnp.mean(xv * xv, axis=-1, keepdims=True) + EPS)


def _rms_bwd(dn, xv, r, ln):
    xh = xv * r
    dxh = dn * ln
    dx = r * (dxh - xh * jnp.mean(dxh * xh, axis=-1, keepdims=True))
    return dx, jnp.sum(dn * xh, axis=0, keepdims=True)


def _sigmoid(x):
    return jax.nn.sigmoid(x)


def _tile(n, pref):
    return pref if n % pref == 0 else n


def _row_tile(n, cap):
    best = [t for t in range(16, min(n, cap) + 1, 16) if n % t == 0]
    return best[-1] if best else n


class _Comm:
    def __init__(self, ins, out_shapes, sems, start, wait, aliases=None, relay=None):
        self.ins, self.out_shapes, self.sems, self.start, self.wait = list(ins), list(out_shapes), list(sems), start, wait
        self.aliases = dict(aliases or {})
        self.relay = relay


def _merge(comms):
    comms = [c for c in comms if c is not None]
    if not comms:
        return None
    bounds, ni, no, ns = [], 0, 0, 0
    for c in comms:
        bounds.append((ni, no, ns))
        ni, no, ns = ni + len(c.ins), no + len(c.out_shapes), ns + len(c.sems)

    def run(which):
        def f(ins, outs, sems, **kw):
            for c, (i, o, s) in zip(comms, bounds):
                fn = getattr(c, which)
                if fn is not None:
                    fn(ins[i:i + len(c.ins)], outs[o:o + len(c.out_shapes)], sems[s:s + len(c.sems)],
                       **(kw if c.relay is not None else {}))
        return f

    aliases = {i + a: o + b for c, (i, o, _) in zip(comms, bounds) for a, b in c.aliases.items()}
    relay = run("relay") if any(c.relay is not None for c in comms) else None
    return _Comm([a for c in comms for a in c.ins], [a for c in comms for a in c.out_shapes],
                 [a for c in comms for a in c.sems], run("start"), run("wait"), aliases, relay)


def _split_outs(comms, outs):
    res, o = [], 0
    for c in comms:
        if c is not None:
            res.append(list(outs[o:o + len(c.out_shapes)]))
            o += len(c.out_shapes)
    return res


def _pcall(body, args, *, name, out_shape, grid=(), in_specs=None, out_specs=None, scratch=(), comm=None,
           prefetch=()):
    many = isinstance(out_shape, (list, tuple))
    outs = list(out_shape) if many else [out_shape]
    n_pre, n_in, n_out, n_scr = len(prefetch), len(args), len(outs), len(scratch)
    if in_specs is None:
        in_specs, out_specs = [VMEM_SPEC] * n_in, [VMEM_SPEC] * n_out
    else:
        in_specs, out_specs = list(in_specs), (list(out_specs) if many else [out_specs])
    params = pltpu.CompilerParams(dimension_semantics=("arbitrary",) * len(grid), vmem_limit_bytes=VMEM_LIMIT)
    scalars = [jnp.reshape(s, (1,)).astype(jnp.int32) for s in prefetch]
    ci, co = (len(comm.ins), len(comm.out_shapes)) if comm is not None else (0, 0)

    def wrapped(*refs):
        pre, refs = refs[:n_pre], refs[n_pre:]
        a, ca = refs[:n_in], refs[n_in:n_in + ci]
        o = refs[n_in + ci:n_in + ci + n_out]
        cout = refs[n_in + ci + n_out:n_in + ci + n_out + co]
        s = refs[n_in + ci + n_out + co:n_in + ci + n_out + co + n_scr]
        csem = refs[n_in + ci + n_out + co + n_scr:]
        if comm is None:
            body(*pre, *a, *o, *s)
        elif grid:
            step = functools.reduce(lambda acc, k: acc * grid[k] + pl.program_id(k), range(len(grid)), 0)
            n_steps = functools.reduce(operator.mul, grid)
            relayed = comm.relay is not None and n_steps >= RELAY_MIN_STEPS
            pl.when(step == 0)(lambda: comm.start(ca, cout, csem))
            if relayed:
                pl.when(step == n_steps - n_steps // 8)(lambda: comm.relay(ca, cout, csem))
            body(*pre, *a, *o, *s)
            pl.when(step == n_steps - 1)(lambda: comm.wait(ca, cout, csem, **({"relayed": True} if relayed else {})))
        else:
            comm.start(ca, cout, csem)
            body(*pre, *a, *o, *s)
            comm.wait(ca, cout, csem)

    c_ins, c_outs, c_sems, aliases = ([], [], [], {}) if comm is None else (
        comm.ins, comm.out_shapes, comm.sems, {n_pre + n_in + i: n_out + o for i, o in comm.aliases.items()})
    all_in, all_out = in_specs + [HBM_SPEC] * ci, out_specs + [HBM_SPEC] * co
    all_scr = list(scratch) + c_sems
    if grid:
        args = [pltpu.with_memory_space_constraint(a, pltpu.HBM) for a in args]
    c_ins = [pltpu.with_memory_space_constraint(a, pltpu.HBM) for a in c_ins]
    if n_pre:
        spec = dict(grid_spec=pltpu.PrefetchScalarGridSpec(
            num_scalar_prefetch=n_pre, grid=grid, in_specs=all_in, out_specs=all_out, scratch_shapes=all_scr))
    else:
        spec = dict(grid=grid, in_specs=all_in, out_specs=all_out, scratch_shapes=all_scr)
    res = pl.pallas_call(wrapped, name=name, out_shape=outs + c_outs, input_output_aliases=aliases,
                         compiler_params=params, **spec)(*scalars, *args, *c_ins)
    mine = list(res[:n_out])
    mine = mine if many else mine[0]
    return mine if comm is None else (mine, list(res[n_out:]))


def _peer_chips(x, y):
    return [(1 - x, y), (x, 1 - y), (1 - x, 1 - y)]


def _c_all_gather(bufs):
    n = len(bufs)

    def copies(ins, outs, sems):
        send_sems, recv_sems, fwd_send, fwd_recv = sems
        x, y, c = lax.axis_index("x"), lax.axis_index("y"), lax.axis_index("c")
        me = 2 * x + y
        peers = _peer_chips(x, y)
        chip = [2 * px + py for px, py in peers]

        def ici(g, j, slot):
            return pltpu.make_async_remote_copy(
                src_ref=outs[g].at[me, c], dst_ref=outs[g].at[slot, c], send_sem=send_sems.at[g, j],
                recv_sem=recv_sems.at[g, j], device_id=(*peers[j], c), device_id_type=MESH)

        def d2d(g, j, half):
            return pltpu.make_async_remote_copy(
                src_ref=outs[g].at[chip[j], half], dst_ref=outs[g].at[chip[j], half], send_sem=fwd_send.at[g, j],
                recv_sem=fwd_recv.at[g, j], device_id=(x, y, 1 - c), device_id_type=MESH)

        pairs = [(g, j) for g in range(n) for j in range(3)]
        sends = [ici(g, j, me) for g, j in pairs]
        recvs = [ici(g, j, chip[j]) for g, j in pairs]
        passes = [d2d(g, j, c) for g, j in pairs]
        passed = [d2d(g, j, 1 - c) for g, j in pairs]
        return sends, recvs, passes, passed

    def start(ins, outs, sems):
        for cp in copies(ins, outs, sems)[0]:
            cp.start()

    def relay(ins, outs, sems):
        _, recvs, passes, _ = copies(ins, outs, sems)
        for rcv, fwd in zip(recvs, passes):
            rcv.wait_recv()
            fwd.start()

    def wait(ins, outs, sems, relayed=False):
        if not relayed:
            relay(ins, outs, sems)
        sends, _, passes, passed = copies(ins, outs, sems)
        for cp in passed:
            cp.wait_recv()
        for cp in sends + passes:
            cp.wait_send()

    pair_sems = pltpu.SemaphoreType.DMA((n, 3))
    return _Comm(bufs, [jax.ShapeDtypeStruct(s.shape, s.dtype) for s in bufs], [pair_sems] * 4, start, wait,
                 aliases={g: g for g in range(n)}, relay=relay)


def _start_wait(copies):
    def start(ins, outs, sems):
        local, sends, _ = copies(ins, outs, sems)
        for cp in local + sends:
            cp.start()

    def wait(ins, outs, sems):
        local, sends, recvs = copies(ins, outs, sems)
        for cp in recvs:
            cp.wait_recv()
        for cp in sends:
            cp.wait_send()
        for cp in local:
            cp.wait()

    return start, wait


def _c_half_swap(grads):
    n = len(grads)

    def copies(ins, outs, sems):
        send_sems, recv_sems = sems
        x, y, c = lax.axis_index("x"), lax.axis_index("y"), lax.axis_index("c")
        sends = []
        for g in range(n):
            half = ins[g].shape[1] // 2
            sends.append(pltpu.make_async_remote_copy(
                src_ref=ins[g].at[:, pl.ds((1 - c) * half, half), :], dst_ref=outs[g],
                send_sem=send_sems.at[g], recv_sem=recv_sems.at[g], device_id=(x, y, 1 - c), device_id_type=MESH))
        return [], sends, sends

    return _Comm(
        grads, [jax.ShapeDtypeStruct((N_CHIPS, s.shape[1] // 2, s.shape[2]), s.dtype) for s in grads],
        [pltpu.SemaphoreType.DMA((n,)), pltpu.SemaphoreType.DMA((n,))], *_start_wait(copies))


def _c_chip_exchange(parts):
    n = len(parts)

    def copies(ins, outs, sems):
        send_sems, recv_sems = sems
        x, y, c = lax.axis_index("x"), lax.axis_index("y"), lax.axis_index("c")
        peers = _peer_chips(x, y)

        def remote(g, j):
            return pltpu.make_async_remote_copy(
                src_ref=ins[g].at[2 * peers[j][0] + peers[j][1]], dst_ref=outs[g].at[j],
                send_sem=send_sems.at[g, j], recv_sem=recv_sems.at[g, j], device_id=(*peers[j], c),
                device_id_type=MESH)

        sends = [remote(g, j) for g in range(n) for j in range(3)]
        return [], sends, sends

    return _Comm(
        parts, [jax.ShapeDtypeStruct((3,) + s.shape[1:], s.dtype) for s in parts],
        [pltpu.SemaphoreType.DMA((n, 3)), pltpu.SemaphoreType.DMA((n, 3))], *_start_wait(copies))


def _c_join(halves):
    n = len(halves)

    def copies(ins, outs, sems):
        send_sems, recv_sems = sems
        x, y, c = lax.axis_index("x"), lax.axis_index("y"), lax.axis_index("c")
        sends = [pltpu.make_async_remote_copy(
            src_ref=ins[g], dst_ref=outs[g], send_sem=send_sems.at[g], recv_sem=recv_sems.at[g],
            device_id=(x, y, 1 - c), device_id_type=MESH) for g in range(n)]
        return [], sends, sends

    return _Comm(
        halves, [jax.ShapeDtypeStruct(s.shape, s.dtype) for s in halves],
        [pltpu.SemaphoreType.DMA((n,)), pltpu.SemaphoreType.DMA((n,))], *_start_wait(copies))


def _exchange_only(comm=None):
    def body(o_ref):
        o_ref[...] = jnp.zeros_like(o_ref)

    return _pcall(body, [], name="exchange_only", out_shape=jax.ShapeDtypeStruct((8, 128), F32), comm=comm)


def _all_reduce_small(v, comm=None):
    rows = v.shape[0]

    def body(v_ref, out_ref, buf, send_sems, recv_sems):
        x, y, c = lax.axis_index("x"), lax.axis_index("y"), lax.axis_index("c")
        me = 4 * x + 2 * y + c
        buf[me] = v_ref[...]
        flips = [(fx, fy, fc) for fx in (0, 1) for fy in (0, 1) for fc in (0, 1)][1:]

        def peer(k):
            fx, fy, fc = flips[k]
            px, py, pc = x ^ fx, y ^ fy, c ^ fc
            return (px, py, pc), 4 * px + 2 * py + pc

        def copy(k, slot):
            return pltpu.make_async_remote_copy(
                src_ref=buf.at[slot], dst_ref=buf.at[slot], send_sem=send_sems.at[k],
                recv_sem=recv_sems.at[k], device_id=peer(k)[0], device_id_type=MESH)

        sends = [copy(k, me) for k in range(7)]
        for cp in sends:
            cp.start()
        for k in range(7):
            copy(k, peer(k)[1]).wait_recv()
        for cp in sends:
            cp.wait_send()
        acc = buf[0]
        for d in range(1, 8):
            acc = acc + buf[d]
        out_ref[...] = acc

    return _pcall(body, [v], name="all_reduce_small", out_shape=jax.ShapeDtypeStruct((rows, 128), F32),
                  scratch=[pltpu.VMEM((8, rows, 128), F32), pltpu.SemaphoreType.DMA((7,)),
                           pltpu.SemaphoreType.DMA((7,))], comm=comm)


def _add_halves(pairs):
    k = len(pairs)

    def body(h_ref, *refs):
        for a_ref, b_ref, o_ref in zip(refs[0:2 * k:2], refs[1:2 * k:2], refs[2 * k:]):
            o_ref[...] = (a_ref[...].astype(F32) + b_ref[...].astype(F32)).astype(o_ref.dtype)

    in_specs, out_specs = [], []
    for _, got in pairs:
        _, h, c = got.shape
        spec = pl.BlockSpec((1, h, c), lambda j, h_ref: (j, 0, 0))
        in_specs += [pl.BlockSpec((1, h, c), lambda j, h_ref: (j, h_ref[0], 0)), spec]
        out_specs.append(spec)
    return _pcall(body, [a for pair in pairs for a in pair], name="add_halves", grid=(N_CHIPS,),
                  prefetch=[lax.axis_index("c")], in_specs=in_specs, out_specs=out_specs,
                  out_shape=[jax.ShapeDtypeStruct(got.shape, BF) for _, got in pairs])


def _sum_chips(pairs):
    k = len(pairs)
    n_steps = 2 if all(parts.shape[1] % 32 == 0 for parts, _ in pairs) else 1
    me = 2 * lax.axis_index("x") + lax.axis_index("y")

    def body(me_ref, *refs):
        for p_ref, r_ref, o_ref in zip(refs[0:2 * k:2], refs[1:2 * k:2], refs[2 * k:]):
            acc = p_ref[0].astype(F32)
            for s in range(N_CHIPS - 1):
                acc = acc + r_ref[s].astype(F32)
            o_ref[...] = acc

    in_specs, out_specs = [], []
    for parts, _ in pairs:
        _, h, c = parts.shape
        th = h // n_steps
        in_specs += [pl.BlockSpec((1, th, c), lambda i, me_ref: (me_ref[0], i, 0)),
                     pl.BlockSpec((N_CHIPS - 1, th, c), lambda i, me_ref: (0, i, 0))]
        out_specs.append(pl.BlockSpec((th, c), lambda i, me_ref: (i, 0)))
    return _pcall(body, [a for pair in pairs for a in pair], name="sum_chips", grid=(n_steps,), prefetch=[me],
                  in_specs=in_specs, out_specs=out_specs,
                  out_shape=[jax.ShapeDtypeStruct(parts.shape[1:], F32) for parts, _ in pairs])


def _adam_update(w, gv, m, v, d_ref, nm_ref, nv_ref):
    c1 = 1.0 / (1.0 - ADAM_B1 ** ADAM_STEP)
    c2 = 1.0 / (1.0 - ADAM_B2 ** ADAM_STEP)
    nm = ADAM_B1 * m + (1.0 - ADAM_B1) * gv
    nv = ADAM_B2 * v + (1.0 - ADAM_B2) * (gv * gv)
    nm_ref[...] = nm
    nv_ref[...] = nv
    d_ref[...] = -ADAM_LR * ((nm * c1) / (jnp.sqrt(nv * c2) + ADAM_EPS) + ADAM_WD * w)


def _adamw(w, g, m, v, comm=None):
    r, c = w.shape[0], w.shape[-1]
    tr = _row_tile(r, 512)

    def body(w_ref, g_ref, m_ref, v_ref, d_ref, nm_ref, nv_ref):
        _adam_update(w_ref[...], g_ref[...], m_ref[...], v_ref[...], d_ref, nm_ref, nv_ref)

    mid = (1,) * (w.ndim - 2)
    spec = pl.BlockSpec((tr,) + mid + (c,), lambda i: (i,) + (0,) * (w.ndim - 1))
    sds = jax.ShapeDtypeStruct(w.shape, F32)
    return _pcall(body, [w, g, m, v], name="adamw", grid=(r // tr,), out_shape=[sds, sds, sds],
                  in_specs=[spec] * 4, out_specs=[spec] * 3, comm=comm)


def _adamw_vectors(items):
    k = len(items)

    def body(*refs):
        ins, outs = refs[:4 * k], refs[4 * k:]
        for q in range(k):
            w_ref, g_ref, m_ref, v_ref = ins[4 * q:4 * q + 4]
            _adam_update(w_ref[...], g_ref[...], m_ref[...], v_ref[...], *outs[3 * q:3 * q + 3])

    return _pcall(body, [a for it in items for a in it], name="adamw_vectors",
                  out_shape=[jax.ShapeDtypeStruct(it[0].shape, F32) for it in items for _ in range(3)])


def _adamw_halves(items, comm=None):
    k = len(items)
    r, c = items[0][0].shape
    h = r // 2
    tr = _row_tile(h, min(512, (VMEM_LIMIT * 3 // 4) // (k * 9 * 2 * 4 * c)))
    nb = h // tr
    core = lax.axis_index("c")

    def body(c_ref, *refs):
        ins, outs = refs[:5 * k], refs[5 * k:]
        for q in range(k):
            w_ref, gm_ref, go_ref, m_ref, v_ref = ins[5 * q:5 * q + 5]
            g_ref, d_ref, nm_ref, nv_ref = outs[4 * q:4 * q + 4]
            gv = jnp.where(pl.program_id(0) == c_ref[0], gm_ref[...], go_ref[...])
            g_ref[...] = gv
            _adam_update(w_ref[...], gv, m_ref[...], v_ref[...], d_ref, nm_ref, nv_ref)

    full = pl.BlockSpec((tr, c), lambda hh, i, c_ref: (hh * nb + i, 0))
    half = pl.BlockSpec((tr, c), lambda hh, i, c_ref: (i, 0))
    sds = jax.ShapeDtypeStruct((r, c), F32)
    return _pcall(body, [a for it in items for a in it], name="adamw_halves", grid=(2, nb), prefetch=[core],
                  out_shape=[sds] * (4 * k), in_specs=[full, half, half, full, full] * k, out_specs=[full] * (4 * k),
                  comm=comm)


SC_CORES, SC_TILES, SC_LANES = 2, 16, 16
SC_BLOCK_ROWS, SC_BLOCK_COLS = 8, 512


def _sc_adamw_halves(items):
    k = len(items)
    r, c = items[0][0].shape
    h = r // 2
    bc = min(c, SC_BLOCK_COLS)
    c1 = 1.0 / (1.0 - ADAM_B1 ** ADAM_STEP)
    c2 = 1.0 / (1.0 - ADAM_B2 ** ADAM_STEP)
    mesh = plsc.VectorSubcoreMesh(core_axis_name="sc_core", subcore_axis_name="sc_tile",
                                  num_cores=SC_CORES, num_subcores=SC_TILES)
    spec = pl.BlockSpec(block_shape=(SC_BLOCK_ROWS, bc), index_map=lambda i, j: (i, j))

    def block(w_v, gin_v, m_v, v_v, g_v, d_v, nm_v, nv_v):
        @pl.loop(0, SC_BLOCK_ROWS)
        def _(row):
            @pl.loop(0, bc, step=SC_LANES)
            def _(col):
                at = (pl.ds(row, 1), pl.ds(col, SC_LANES))
                gv = gin_v.at[*at][...]
                nm = ADAM_B1 * m_v.at[*at][...] + (1.0 - ADAM_B1) * gv
                nv = ADAM_B2 * v_v.at[*at][...] + (1.0 - ADAM_B2) * (gv * gv)
                g_v.at[*at][...] = gv
                nm_v.at[*at][...] = nm
                nv_v.at[*at][...] = nv
                d_v.at[*at][...] = -ADAM_LR * ((nm * c1) / (jnp.sqrt(nv * c2) + ADAM_EPS) + ADAM_WD * w_v.at[*at][...])

    def kern(*refs):
        ins, outs = refs[:5 * k], refs[5 * k:]
        core = lax.axis_index("c")

        def half(q, hh, mine):
            w_hbm, gm_hbm, go_hbm, m_hbm, v_hbm = ins[5 * q:5 * q + 5]
            rows = pl.ds(hh * h, h)
            pltpu.emit_pipeline(
                block, grid=(h // SC_BLOCK_ROWS, c // bc), in_specs=[spec] * 4, out_specs=[spec] * 4,
                core_axis_name=("sc_core", "sc_tile"), dimension_semantics=(pltpu.PARALLEL, pltpu.PARALLEL),
                trace_scopes=False,
            )(w_hbm.at[rows, :], gm_hbm if mine else go_hbm, m_hbm.at[rows, :], v_hbm.at[rows, :],
              *(o.at[rows, :] for o in outs[4 * q:4 * q + 4]))

        for q in range(k):
            for hh in range(2):
                pl.when(core == hh)(lambda q=q, hh=hh: half(q, hh, True))
                pl.when(core != hh)(lambda q=q, hh=hh: half(q, hh, False))

    sds = jax.ShapeDtypeStruct((r, c), F32)
    return pl.kernel(kern, out_type=[sds] * (4 * k), mesh=mesh, scratch_types=[], name="sc_adamw_halves")(
        *(a for it in items for a in it))


def _wgrad(name, a, b, a_spec, b_spec, m, n, nb, comm):
    def body(a_ref, b_ref, o_ref):
        o_ref[...] = _dot_tn(a_ref[...], b_ref[...]).astype(o_ref.dtype)

    return _pcall(body, [a, b], name=name, grid=(nb,), out_shape=jax.ShapeDtypeStruct((nb, m, n), BF),
                  in_specs=[a_spec, b_spec], out_specs=pl.BlockSpec((None, m, n), lambda j: (j, 0, 0)), comm=comm)


def _wgrad_cols(name, a, b, nb, comm=None):
    t_tok, m = a.shape
    n = b.shape[1] // nb
    return _wgrad(name, a, b, pl.BlockSpec((t_tok, m), lambda j: (0, 0)), pl.BlockSpec((t_tok, n), lambda j: (0, j)),
                  m, n, nb, comm)


def _wgrad_rows(name, a, b, nb, comm=None):
    t_tok, n = b.shape
    m = a.shape[1] // nb
    return _wgrad(name, a, b, pl.BlockSpec((t_tok, m), lambda j: (0, j)), pl.BlockSpec((t_tok, n), lambda j: (0, 0)),
                  m, n, nb, comm)


def _wgrad_in(a, b, comm=None):
    t_tok, d = b.shape
    m = 512
    nb = IN_PAD // m
    rows = IN_COLS // N_CHIPS

    def body(a_ref, b_ref, o_ref, g_ref):
        @pl.when(pl.program_id(0) == 0)
        def _():
            for j in range(N_CHIPS):
                o_ref[j, rows:, :] = jnp.zeros((IN_ROWS_PAD - rows, d), BF)

        g_ref[...] = _dot_tn(a_ref[...], b_ref[...]).astype(BF)
        for i in range(nb):
            @pl.when(pl.program_id(0) == i)
            def _(i=i):
                lo, hi = i * m, min((i + 1) * m, IN_COLS)
                while lo < hi:
                    j = lo // rows
                    end = min(hi, (j + 1) * rows)
                    o_ref[j, lo - j * rows:end - j * rows, :] = g_ref[lo - i * m:end - i * m, :]
                    lo = end

    return _pcall(body, [a, b], name="wgrad_in", grid=(nb,),
                  out_shape=jax.ShapeDtypeStruct((N_CHIPS, IN_ROWS_PAD, d), BF),
                  in_specs=[pl.BlockSpec((t_tok, m), lambda i: (0, i)), pl.BlockSpec((t_tok, d), lambda i: (0, 0))],
                  out_specs=pl.BlockSpec((N_CHIPS, IN_ROWS_PAD, d), lambda i: (0, 0, 0)),
                  scratch=[pltpu.VMEM((m, d), BF)], comm=comm)


def _wgrad_a_shared(name, a, b4, comm=None):
    t_tok, m = a.shape
    nb, _, n = b4.shape
    return _wgrad(name, a, b4, pl.BlockSpec((t_tok, m), lambda j: (0, 0)),
                  pl.BlockSpec((None, t_tok, n), lambda j: (j, 0, 0)), m, n, nb, comm)


def _wgrad_b_shared(name, a4, b, comm=None):
    nb, t_tok, m = a4.shape
    n = b.shape[1]
    return _wgrad(name, a4, b, pl.BlockSpec((None, t_tok, m), lambda j: (j, 0, 0)),
                  pl.BlockSpec((t_tok, n), lambda j: (0, 0)), m, n, nb, comm)


def _w4_spec(r, c):
    return pl.BlockSpec((None, r, c), lambda i, j: (j, 0, 0))


FFN_ROW_CHUNK = 256


def _row_chunks(tm):
    rc = FFN_ROW_CHUNK if tm % FFN_ROW_CHUNK == 0 else tm
    return [slice(r, r + rc) for r in range(0, tm, rc)]


def _ffn_fwd(h, ln, wg4, wu4, wd4, comm=None):
    t_tok, d = h.shape
    f = wg4.shape[-2]
    tm = _tile(t_tok, 512)

    def body(h_ref, ln_ref, wg_ref, wu_ref, wd_ref, ho_ref, n_ref, g_ref, u_ref, n_s, acc):
        j = pl.program_id(1)

        @pl.when(j == 0)
        def _():
            xv = h_ref[...]
            nv = (xv * _rstd(xv) * ln_ref[...]).astype(BF)
            n_s[...] = nv
            n_ref[...] = nv
            acc[...] = jnp.zeros_like(acc)

        nv = n_s[...]
        g = _dot_nt(nv, wg_ref[...])
        u = _dot_nt(nv, wu_ref[...])
        g_ref[...] = g.astype(BF)
        u_ref[...] = u.astype(BF)
        a = (g * _sigmoid(g) * u).astype(BF)
        acc[...] += _dot(a, wd_ref[...])

        @pl.when(j == N_CHIPS - 1)
        def _():
            ho_ref[...] = h_ref[...] + 0.5 * acc[...]

    row = pl.BlockSpec((tm, d), lambda i, j: (i, 0))
    gu = pl.BlockSpec((None, tm, f), lambda i, j: (j, i, 0))
    gu_sds = jax.ShapeDtypeStruct((N_CHIPS, t_tok, f), BF)
    return _pcall(
        body, [h, ln, wg4, wu4, wd4], name="ffn_fwd", grid=(t_tok // tm, N_CHIPS),
        out_shape=[jax.ShapeDtypeStruct((t_tok, d), F32), jax.ShapeDtypeStruct((t_tok, d), BF), gu_sds, gu_sds],
        in_specs=[row, pl.BlockSpec((1, d), lambda i, j: (0, 0)), _w4_spec(f, d), _w4_spec(f, d), _w4_spec(f, d)],
        out_specs=[row, row, gu, gu],
        scratch=[pltpu.VMEM((tm, d), BF), pltpu.VMEM((tm, d), F32)], comm=comm)


def _ffn_bwd(dho, h, ln, g4, u4, wg4, wu4, wd4, comm=None):
    t_tok, d = h.shape
    f = wg4.shape[-2]
    tm = _tile(t_tok, 512)

    def body(dho_ref, h_ref, ln_ref, g_ref, u_ref, wg_ref, wu_ref, wd_ref,
             dhi_ref, dln_ref, dg_ref, du_ref, a_ref, dhb_ref, dhb_s, dn_acc):
        i, j = pl.program_id(0), pl.program_id(1)

        @pl.when(j == 0)
        def _():
            dhb = (0.5 * dho_ref[...]).astype(BF)
            dhb_s[...] = dhb
            dhb_ref[...] = dhb
            dn_acc[...] = jnp.zeros_like(dn_acc)

        @pl.when((i == 0) & (j == 0))
        def _():
            dln_ref[...] = jnp.zeros_like(dln_ref)

        for rows in _row_chunks(tm):
            g = g_ref[rows, :].astype(F32)
            u = u_ref[rows, :].astype(F32)
            s = _sigmoid(g)
            sg = g * s
            a_ref[rows, :] = (sg * u).astype(BF)
            da = _dot_nt(dhb_s[rows, :], wd_ref[...])
            dg = (da * u * (s * (1.0 + g * (1.0 - s)))).astype(BF)
            du = (da * sg).astype(BF)
            dg_ref[rows, :] = dg
            du_ref[rows, :] = du
            dn_acc[rows, :] += _dot(dg, wg_ref[...]) + _dot(du, wu_ref[...])

        @pl.when(j == N_CHIPS - 1)
        def _():
            xv = h_ref[...]
            dx, dln = _rms_bwd(dn_acc[...], xv, _rstd(xv), ln_ref[...])
            dln_ref[...] += dln
            dhi_ref[...] = dho_ref[...] + dx

    row = pl.BlockSpec((tm, d), lambda i, j: (i, 0))
    vec = pl.BlockSpec((1, d), lambda i, j: (0, 0))
    gu = pl.BlockSpec((None, tm, f), lambda i, j: (j, i, 0))
    gu_sds = jax.ShapeDtypeStruct((N_CHIPS, t_tok, f), BF)
    return _pcall(
        body, [dho, h, ln, g4, u4, wg4, wu4, wd4], name="ffn_bwd", grid=(t_tok // tm, N_CHIPS),
        out_shape=[jax.ShapeDtypeStruct((t_tok, d), F32), jax.ShapeDtypeStruct((1, d), F32),
                   gu_sds, gu_sds, gu_sds, jax.ShapeDtypeStruct((t_tok, d), BF)],
        in_specs=[row, row, vec, gu, gu, _w4_spec(f, d), _w4_spec(f, d), _w4_spec(f, d)],
        out_specs=[row, vec, gu, gu, gu, row],
        scratch=[pltpu.VMEM((tm, d), BF), pltpu.VMEM((tm, d), F32)], comm=comm)


def _rope_tables(pos_col, inv_freq2, blocks, comm=None):
    t_tok = pos_col.shape[0]
    k = len(blocks)
    n_steps = 4
    assert all(b.shape[0] % (16 * n_steps) == 0 for b in blocks)

    def body(p_ref, f_ref, *refs):
        cos_ref, sin_ref = refs[k:k + 2]
        ang = p_ref[...] * f_ref[...]
        lane = lax.broadcasted_iota(jnp.int32, ang.shape, 1)
        s = jnp.sin(ang)
        cos_ref[...] = jnp.cos(ang)
        sin_ref[...] = jnp.where((lane & 1) == 0, -s, s)
        for a_ref, o_ref in zip(refs[:k], refs[k + 2:]):
            o_ref[...] = a_ref[...].astype(BF)

    rows = lambda r, c: pl.BlockSpec((r // n_steps, c), lambda i: (i, 0))
    casts = [rows(*b.shape) for b in blocks]
    sds = jax.ShapeDtypeStruct((t_tok, 128), F32)
    return _pcall(body, [pos_col, inv_freq2, *blocks], name="rope_tables", grid=(n_steps,),
                  out_shape=[sds, sds] + [jax.ShapeDtypeStruct(b.shape, BF) for b in blocks],
                  in_specs=[rows(t_tok, 1), pl.BlockSpec((1, 128), lambda i: (0, 0))] + casts,
                  out_specs=[rows(t_tok, 128)] * 2 + casts, comm=comm)


def _swap_pairs(x):
    lane = lax.broadcasted_iota(jnp.int32, x.shape, 1)
    return jnp.where((lane & 1) == 0, pltpu.roll(x, 127, 1), pltpu.roll(x, 1, 1))


def _mix_in(h, ln, w_in4, wm4, b_m, cos_t, sin_t, comm=None):
    t_tok, d = h.shape
    cm = wm4.shape[-1]
    tm = _tile(t_tok, 256)
    rows = IN_COLS // N_CHIPS

    def body(h_ref, ln_ref, win4_ref, wm_ref, bm_ref, cos_ref, sin_ref,
             u_ref, rq_ref, rk_ref, rv_ref, rg_ref, fq_ref, fk_ref, fv_ref, ff_ref, ga_ref, gb_ref, win_ref):
        @pl.when(pl.program_id(0) == 0)
        def _():
            for j in range(N_CHIPS):
                win_ref[j * rows:(j + 1) * rows, :] = win4_ref[j, :rows, :]
            win_ref[IN_COLS:, :] = jnp.zeros((IN_PAD - IN_COLS, d), BF)

        xv = h_ref[...]
        ub = (xv * _rstd(xv) * ln_ref[...]).astype(BF)
        u_ref[...] = ub
        cosv, sinv = cos_ref[...], sin_ref[...]

        def sec(k):
            return _dot_nt(ub, win_ref[k * 512:(k + 1) * 512, :])

        def rot(xh):
            return xh * cosv + _swap_pairs(xh) * sinv

        pq, pk = sec(0), sec(1)
        for hh in range(RET_HEADS):
            sl = slice(hh * RET_DIM, (hh + 1) * RET_DIM)
            rq_ref[:, sl] = rot(pq[:, sl]).astype(BF)
            rk_ref[:, sl] = (rot(pk[:, sl]) * RET_SCALE).astype(BF)
        rv_ref[...] = sec(2).astype(BF)
        rg_ref[...] = sec(3).astype(BF)
        fq_ref[...] = (sec(4) * FOX_SCALE).astype(BF)
        fk_ref[...] = sec(5).astype(BF)
        fv_ref[...] = sec(6).astype(BF)
        ff_ref[...] = _dot_nt(ub, win_ref[FF_COL:FF_COL + 128, :])
        for j in range(N_CHIPS):
            gs = _sigmoid(_dot(ub, wm_ref[j]) + bm_ref[:, j * cm:(j + 1) * cm]).astype(BF)
            col = j * cm
            if col < d:
                ga_ref[:, col:col + cm] = gs
            else:
                gb_ref[:, col - d:col - d + cm] = gs

    row = lambda c: pl.BlockSpec((tm, c), lambda i: (i, 0))
    full = lambda *s: pl.BlockSpec(s, lambda i: (0,) * len(s))
    sds = lambda c, dt: jax.ShapeDtypeStruct((t_tok, c), dt)
    return _pcall(
        body, [h, ln, w_in4, wm4, b_m, cos_t, sin_t], name="mix_in", grid=(t_tok // tm,),
        out_shape=[sds(d, BF)] + [sds(512, BF)] * 7 + [sds(128, F32), sds(d, BF), sds(d, BF),
                                                       jax.ShapeDtypeStruct((IN_PAD, d), BF)],
        in_specs=[row(d), full(1, d), full(*w_in4.shape), full(N_CHIPS, d, cm), full(1, 2 * d), row(128),
                  row(128)],
        out_specs=[row(d)] + [row(512)] * 7 + [row(128), row(d), row(d), full(IN_PAD, d)], comm=comm)


def _split3(x):
    hi = x.astype(BF)
    r1 = x - hi.astype(F32)
    mid = r1.astype(BF)
    lo = (r1 - mid.astype(F32)).astype(BF)
    return hi, mid, lo


def _aug_lane():
    return lax.broadcasted_iota(jnp.int32, (1, 128), 1) & (FOX_DIM - 1)


def _aug_put(base, k0, parts):
    w = _aug_lane()
    for i, part in enumerate(parts):
        base = jnp.where(w == k0 + i, part, base)
    return base


def _forget_fwd(ffl, b_pad):
    t_tok = ffl.shape[0]
    tb = _tile(t_tok, 256)

    def body(ff_ref, b_ref, aq_ref, ak_ref, cum_s):
        r = lax.broadcasted_iota(jnp.int32, (tb, tb), 0)
        c = lax.broadcasted_iota(jnp.int32, (tb, tb), 1)
        tri = jnp.where(c <= r, 1.0, 0.0).astype(BF)
        carry = jnp.zeros((1, 128), F32)
        for i in range(t_tok // tb):
            z = ff_ref[i * tb:(i + 1) * tb, :] + b_ref[...]
            lf = jnp.minimum(z, 0.0) - jnp.log(1.0 + jnp.exp(-jnp.abs(z)))
            hi, mid, lo = _split3(lf)
            cs = _dot(tri, hi) + _dot(tri, mid) + _dot(tri, lo) + carry
            cum_s[i * tb:(i + 1) * tb, :] = cs
            carry = cs[tb - 1:tb, :]
        x = cum_s[...]
        first = lax.broadcasted_iota(jnp.int32, (1, 128), 1) < FOX_DIM
        w = _aug_lane()
        one = jnp.ones((t_tok, 128), BF)
        zero = jnp.zeros((t_tok, 128), BF)
        for pp in range(FOX_HEADS // 2):
            other = jnp.where(first, x[:, 2 * pp + 1:2 * pp + 2], x[:, 2 * pp:2 * pp + 1])
            parts = _split3(other)
            aq = jnp.where((w >= 3) & (w < 6), one, zero)
            ak = jnp.where((w < 3) | ((w >= 6) & (w < 9)), one, zero)
            aq_ref[:, pp * 128:(pp + 1) * 128] = _aug_put(aq, 0, parts)
            ak_ref[:, pp * 128:(pp + 1) * 128] = _aug_put(ak, 3, [-q for q in parts])

    sds = jax.ShapeDtypeStruct((t_tok, FOX_WIDTH), BF)
    return _pcall(body, [ffl, b_pad], name="forget_fwd", out_shape=[sds, sds],
                  scratch=[pltpu.VMEM((t_tok, 128), F32)])


def _forget_bwd(dcum_t, dcum_q, ffl, b_pad):
    t_tok = ffl.shape[0]
    tb = _tile(t_tok, 256)

    def body(dc_ref, dq_ref, ff_ref, b_ref, dff_ref, db_ref, pad_s, d_s):
        pad_s[...] = jnp.zeros_like(pad_s)
        pad_s[0:FOX_HEADS, :] = dc_ref[...]
        dsum = pad_s[...].T
        lane = lax.broadcasted_iota(jnp.int32, (t_tok, 128), 1)
        for hh in range(FOX_HEADS):
            dsum = dsum + jnp.where(lane == hh, dq_ref[:, hh * FOX_DIM:hh * FOX_DIM + 1], 0.0)
        d_s[...] = dsum
        r = lax.broadcasted_iota(jnp.int32, (tb, tb), 0)
        c = lax.broadcasted_iota(jnp.int32, (tb, tb), 1)
        tri = jnp.where(c >= r, 1.0, 0.0).astype(BF)
        carry = jnp.zeros((1, 128), F32)
        db = jnp.zeros((1, 128), F32)
        for i in reversed(range(t_tok // tb)):
            hi, mid, lo = _split3(d_s[i * tb:(i + 1) * tb, :])
            dlf = _dot(tri, hi) + _dot(tri, mid) + _dot(tri, lo) + carry
            carry = dlf[0:1, :]
            z = ff_ref[i * tb:(i + 1) * tb, :] + b_ref[...]
            dff = dlf * _sigmoid(-z)
            dff_ref[i * tb:(i + 1) * tb, :] = dff.astype(BF)
            db = db + jnp.sum(dff, axis=0, keepdims=True)
        db_ref[...] = db

    return _pcall(
        body, [dcum_t, dcum_q, ffl, b_pad], name="forget_bwd",
        out_shape=[jax.ShapeDtypeStruct((t_tok, 128), BF), jax.ShapeDtypeStruct((1, 128), F32)],
        scratch=[pltpu.VMEM((128, t_tok), F32), pltpu.VMEM((t_tok, 128), F32)])


def _first_half():
    return lax.broadcasted_iota(jnp.int32, (1, 128), 1) < FOX_DIM


def _head_rows(x2, a2, hh):
    return jnp.where(_first_half(), x2, a2) if hh == 0 else jnp.where(_first_half(), a2, x2)


def _head_only(x2, hh):
    zero = jnp.zeros_like(x2)
    return jnp.where(_first_half(), x2, zero) if hh == 0 else jnp.where(_first_half(), zero, x2)


def _causal_diag(s):
    rows = lax.broadcasted_iota(jnp.int32, s.shape, 0)
    cols = lax.broadcasted_iota(jnp.int32, s.shape, 1)
    return jnp.where(cols <= rows, s, NEG)


def _diag_or_below(qi, ki, step):
    pl.when(ki < qi)(lambda: step(False))
    pl.when(ki == qi)(lambda: step(True))


def _tri_rows(s, n):
    qi = sum((s >= r * (r + 1) // 2).astype(jnp.int32) for r in range(1, n))
    return qi, s - (qi * (qi + 1)) // 2


def _tri_cols(s, n):
    ki = sum((s >= k * n - k * (k - 1) // 2).astype(jnp.int32) for k in range(1, n))
    return ki, ki + s - (ki * n - (ki * (ki - 1)) // 2)


def _fox_fwd(fq, fk, fv, aq, ak, comm=None):
    t_tok = fq.shape[0]
    t = _tile(t_tok, 512)
    nq = t_tok // t
    npair = FOX_HEADS // 2

    def body(q_ref, k_ref, v_ref, aq_ref, ak_ref, o_ref, of_ref, aqb_ref, m_s, l_s, acc_s):
        qi, ki = _tri_rows(pl.program_id(1), nq)

        @pl.when(ki == 0)
        def _():
            m_s[...] = jnp.full_like(m_s, NEG)
            l_s[...] = jnp.zeros_like(l_s)
            acc_s[...] = jnp.zeros_like(acc_s)

        def step(diag):
            q2, k2, v2, aq2, ak2 = q_ref[...], k_ref[...], v_ref[...], aq_ref[...], ak_ref[...]
            for hh in range(2):
                s = _dot_nt(_head_rows(q2, aq2, hh), _head_rows(k2, ak2, hh))
                if diag:
                    s = _causal_diag(s)
                m_prev = m_s[hh]
                m_new = jnp.maximum(m_prev, jnp.max(s, axis=1, keepdims=True))
                alpha = jnp.exp(m_prev - m_new)
                p = jnp.exp(s - jnp.tile(m_new, (1, t // 128)))
                l_s[hh] = alpha * l_s[hh] + jnp.sum(p, axis=1, keepdims=True)
                acc_s[hh] = alpha * acc_s[hh] + _dot(p.astype(BF), v2)
                m_s[hh] = m_new

        _diag_or_below(qi, ki, step)

        @pl.when(ki == qi)
        def _():
            first = _first_half()
            o = jnp.where(first, acc_s[0] / l_s[0], acc_s[1] / l_s[1])
            o_ref[...] = o.astype(BF)
            of_ref[...] = o
            other = jnp.where(first, m_s[1] + jnp.log(l_s[1]), m_s[0] + jnp.log(l_s[0]))
            aqb_ref[...] = _aug_put(aq_ref[...], 6, _split3(-other))

    qs = pl.BlockSpec((t, 128), lambda p, s: (_tri_rows(s, nq)[0], p))
    ks = pl.BlockSpec((t, 128), lambda p, s: (_tri_rows(s, nq)[1], p))
    stat = pltpu.VMEM((2, t, 128), F32)
    return _pcall(
        body, [fq, fk, fv, aq, ak], name="fox_fwd", grid=(npair, nq * (nq + 1) // 2),
        out_shape=[jax.ShapeDtypeStruct((t_tok, FOX_WIDTH), BF), jax.ShapeDtypeStruct((t_tok, FOX_WIDTH), F32),
                   jax.ShapeDtypeStruct((t_tok, FOX_WIDTH), BF)],
        in_specs=[qs, ks, ks, qs, ks], out_specs=[qs, qs, qs], scratch=[stat, stat, stat], comm=comm)


def _fox_ds(q2, k2, v2, do2, aq2, ak2, ad2, hh, diag):
    s = _dot_nt(_head_rows(q2, aq2, hh), _head_rows(k2, ak2, hh))
    if diag:
        s = _causal_diag(s)
    p = jnp.exp(s)
    av = jnp.where(_aug_lane() < 3, 1.0, 0.0).astype(BF)
    dp = _dot_nt(_head_rows(do2, ad2, hh), _head_rows(v2, jnp.broadcast_to(av, v2.shape), hh))
    return p, p * dp


def _fox_bwd(fq, fk, fv, do, aqb, ak, ad, comm=None):
    t_tok = fq.shape[0]
    t = _tile(t_tok, 512)
    nq = t_tok // t
    npair = FOX_HEADS // 2
    n_steps = nq * (nq + 1) // 2

    def body(q_ref, k_ref, v_ref, do_ref, aq_ref, ak_ref, ad_ref, dq_ref, dk_ref, dv_ref, dck_ref, dcq_ref,
             dk_s, dv_s, dq_s, rs_s):
        step_id = pl.program_id(1)
        ki, qi = _tri_cols(step_id, nq)

        @pl.when(step_id == 0)
        def _():
            dq_s[...] = jnp.zeros_like(dq_s)
            rs_s[...] = jnp.zeros_like(rs_s)

        @pl.when(qi == ki)
        def _():
            dk_s[...] = jnp.zeros_like(dk_s)
            dv_s[...] = jnp.zeros_like(dv_s)
            dck_ref[...] = jnp.zeros_like(dck_ref)

        rows = pl.ds(qi * t if isinstance(qi, int) else pl.multiple_of(qi * t, t), t)

        def step(diag):
            q2, k2, v2, do2 = q_ref[...], k_ref[...], v_ref[...], do_ref[...]
            dq = []
            for hh in range(2):
                p, ds = _fox_ds(q2, k2, v2, do2, aq_ref[...], ak_ref[...], ad_ref[...], hh, diag)
                dsb = ds.astype(BF)
                dv_s[...] += _dot_tn(p.astype(BF), _head_only(do2, hh))
                dk_s[...] += _dot_tn(dsb, _head_only(q2, hh))
                dq.append(_dot(dsb, k2))
                dck_ref[hh] = dck_ref[hh] - jnp.sum(ds, axis=0, keepdims=True)
                rs_s[hh, rows, :] = rs_s[hh, rows, :] + jnp.sum(ds, axis=1, keepdims=True)
            dq_s[rows, :] = dq_s[rows, :] + jnp.where(_first_half(), dq[0], dq[1])

        _diag_or_below(qi, ki, step)

        @pl.when(qi == nq - 1)
        def _():
            dk_ref[...] = dk_s[...].astype(BF)
            dv_ref[...] = dv_s[...].astype(BF)

        @pl.when(step_id == n_steps - 1)
        def _():
            dq_ref[...] = (dq_s[...] * FOX_SCALE).astype(BF)
            dcq_ref[...] = jnp.where(_first_half(), rs_s[0], rs_s[1])

    qs = pl.BlockSpec((t, 128), lambda p, s: (_tri_cols(s, nq)[1], p))
    ks = pl.BlockSpec((t, 128), lambda p, s: (_tri_cols(s, nq)[0], p))
    cks = pl.BlockSpec((2, 1, t), lambda p, s: (p, 0, _tri_cols(s, nq)[0]))
    seq = pl.BlockSpec((t_tok, 128), lambda p, s: (0, p))
    sds = jax.ShapeDtypeStruct((t_tok, FOX_WIDTH), BF)
    return _pcall(
        body, [fq, fk, fv, do, aqb, ak, ad], name="fox_bwd", grid=(npair, n_steps),
        out_shape=[sds, sds, sds, jax.ShapeDtypeStruct((FOX_HEADS, 1, t_tok), F32),
                   jax.ShapeDtypeStruct((t_tok, FOX_WIDTH), F32)],
        in_specs=[qs, ks, ks, qs, qs, ks, qs], out_specs=[seq, ks, ks, cks, seq],
        scratch=[pltpu.VMEM((t, 128), F32), pltpu.VMEM((t, 128), F32), pltpu.VMEM((t_tok, 128), F32),
                 pltpu.VMEM((2, t_tok, 128), F32)], comm=comm)


def _ret_consts():
    c = RET_CHUNK
    log_gamma = jnp.log1p(-jnp.exp2(-5.0 - jnp.arange(RET_HEADS, dtype=F32)))
    idx = jnp.arange(c, dtype=F32)
    diff = idx[:, None] - idx[None, :]
    dmask = jnp.where(diff >= 0, jnp.exp(log_gamma[:, None, None] * jnp.maximum(diff, 0.0)), 0.0)
    qdec = jnp.exp(log_gamma[:, None] * (idx + 1.0))
    kdec = jnp.exp(log_gamma[:, None] * (c - 1 - idx))
    cdec = jnp.exp(log_gamma * c)
    bc = lambda v: jnp.broadcast_to(v[:, :, None], (RET_HEADS, c, RET_DIM))
    return dmask, bc(qdec), bc(kdec), jnp.broadcast_to(cdec[:, None, None], (RET_HEADS, c, RET_DIM))


def _group_norm(y):
    mu = jnp.mean(y, axis=-1, keepdims=True)
    yc = y - mu
    r = lax.rsqrt(jnp.mean(yc * yc, axis=-1, keepdims=True) + EPS)
    return yc * r, r


def _ret_fwd(rq, rk, rv, rg, consts, comm=None):
    t_tok = rq.shape[0]
    nb = 4 if t_tok % (4 * RET_CHUNK) == 0 else 1
    tr = nb * RET_CHUNK
    n_steps = t_tok // tr
    c = RET_CHUNK

    def body(q_ref, k_ref, v_ref, g_ref, dm_ref, qd_ref, kd_ref, cd_ref, y_ref, yo_ref, st_ref, s_s):
        @pl.when(pl.program_id(0) == 0)
        def _():
            s_s[...] = jnp.zeros_like(s_s)

        for b in range(nb):
            rows = slice(b * c, (b + 1) * c)
            for hh in range(RET_HEADS):
                cols = slice(hh * RET_DIM, (hh + 1) * RET_DIM)
                q, k, v = q_ref[rows, cols], k_ref[rows, cols], v_ref[rows, cols]
                state = s_s[hh]
                st_ref[hh, b] = state
                sc = (_dot_nt(q, k) * dm_ref[hh]).astype(BF)
                y = _dot(sc, v) + _dot((q.astype(F32) * qd_ref[hh]).astype(BF), state.astype(BF))
                s_s[hh] = cd_ref[hh] * state + _dot_tn((k.astype(F32) * kd_ref[hh]).astype(BF), v)
                y_ref[rows, cols] = y
                yn, _ = _group_norm(y)
                gate = g_ref[rows, cols].astype(F32)
                yo_ref[rows, cols] = (yn * (gate * _sigmoid(gate))).astype(BF)

    blk = pl.BlockSpec((tr, RET_WIDTH), lambda i: (i, 0))
    cst = pl.BlockSpec((RET_HEADS, c, RET_DIM), lambda i: (0, 0, 0))
    return _pcall(
        body, [rq, rk, rv, rg, *consts], name="ret_fwd", grid=(n_steps,),
        out_shape=[jax.ShapeDtypeStruct((t_tok, RET_WIDTH), F32), jax.ShapeDtypeStruct((t_tok, RET_WIDTH), BF),
                   jax.ShapeDtypeStruct((RET_HEADS, t_tok // c, RET_DIM, RET_DIM), F32)],
        in_specs=[blk] * 4 + [cst] * 4,
        out_specs=[blk, blk, pl.BlockSpec((RET_HEADS, nb, RET_DIM, RET_DIM), lambda i: (0, i, 0, 0))],
        scratch=[pltpu.VMEM((RET_HEADS, RET_DIM, RET_DIM), F32)], comm=comm)


def _ret_bwd(rq, rk, rv, rg, y_raw, dyo, states, consts, cos_t, sin_t, comm=None):
    t_tok = rq.shape[0]
    nb = 4 if t_tok % (4 * RET_CHUNK) == 0 else 1
    tr = nb * RET_CHUNK
    n_steps = t_tok // tr
    c = RET_CHUNK

    def body(q_ref, k_ref, v_ref, g_ref, y_ref, dyo_ref, st_ref, dm_ref, qd_ref, kd_ref, cd_ref,
             cos_ref, sin_ref, dq_ref, dk_ref, dv_ref, dg_ref, ds_s):
        @pl.when(pl.program_id(0) == 0)
        def _():
            ds_s[...] = jnp.zeros_like(ds_s)

        for b in reversed(range(nb)):
            rows = slice(b * c, (b + 1) * c)
            cosv, sinv = cos_ref[rows, :], sin_ref[rows, :]
            for hh in range(RET_HEADS):
                cols = slice(hh * RET_DIM, (hh + 1) * RET_DIM)
                dm, qd, kd, cd = dm_ref[hh], qd_ref[hh], kd_ref[hh], cd_ref[hh]
                q, k, v = q_ref[rows, cols], k_ref[rows, cols], v_ref[rows, cols]
                yn, r = _group_norm(y_ref[rows, cols])
                gate = g_ref[rows, cols].astype(F32)
                sg = _sigmoid(gate)
                dyo = dyo_ref[rows, cols]
                dg_ref[rows, cols] = (dyo * yn * (sg * (1.0 + gate * (1.0 - sg)))).astype(BF)
                dyn = dyo * (gate * sg)
                dy = r * (dyn - jnp.mean(dyn, axis=-1, keepdims=True)
                          - yn * jnp.mean(dyn * yn, axis=-1, keepdims=True))
                dyb = dy.astype(BF)
                state_b = st_ref[hh, b].astype(BF)
                dstate = ds_s[hh]
                dstate_b = dstate.astype(BF)
                qdb = (q.astype(F32) * qd).astype(BF)
                kdb = (k.astype(F32) * kd).astype(BF)
                sc = (_dot_nt(q, k) * dm).astype(BF)
                dv = _dot_tn(sc, dyb) + _dot(kdb, dstate_b)
                dp = (_dot_nt(dyb, v) * dm).astype(BF)
                dq = _dot(dp, k) + _dot_nt(dyb, state_b) * qd
                dk = (_dot_tn(dp, q) + _dot_nt(v, dstate_b) * kd) * RET_SCALE
                ds_s[hh] = cd * dstate + _dot_tn(qdb, dyb)
                dv_ref[rows, cols] = dv.astype(BF)
                dq_ref[rows, cols] = (dq * cosv - _swap_pairs(dq) * sinv).astype(BF)
                dk_ref[rows, cols] = (dk * cosv - _swap_pairs(dk) * sinv).astype(BF)

    rev = lambda i: n_steps - 1 - i
    blk = pl.BlockSpec((tr, RET_WIDTH), lambda i: (rev(i), 0))
    tab = pl.BlockSpec((tr, RET_DIM), lambda i: (rev(i), 0))
    cst = pl.BlockSpec((RET_HEADS, c, RET_DIM), lambda i: (0, 0, 0))
    sds = jax.ShapeDtypeStruct((t_tok, RET_WIDTH), BF)
    return _pcall(
        body, [rq, rk, rv, rg, y_raw, dyo, states, *consts, cos_t, sin_t], name="ret_bwd",
        grid=(n_steps,), out_shape=[sds] * 4,
        in_specs=[blk] * 6 + [pl.BlockSpec((RET_HEADS, nb, RET_DIM, RET_DIM), lambda i: (0, rev(i), 0, 0))]
        + [cst] * 4 + [tab, tab],
        out_specs=[blk] * 4, scratch=[pltpu.VMEM((RET_HEADS, RET_DIM, RET_DIM), F32)], comm=comm)


def _mix_out(h, y_ret, y_fox, ga, gb, wr4, wf4, wo4, comm=None):
    t_tok, d = h.shape
    cz = wr4.shape[-1]
    ro = wo4.shape[-2]
    tm = _tile(t_tok, 512)

    def body(h_ref, yr_ref, yf_ref, ga_ref, gb_ref, wr_ref, wf_ref, wo_ref, ho_ref, za_ref, zb_ref, mix_ref):
        yr, yf = yr_ref[...], yf_ref[...]
        for j in range(N_CHIPS):
            sl = slice(j * cz, (j + 1) * cz)
            za = _dot(yr, wr_ref[j])
            zb = _dot(yf, wf_ref[j])
            za_ref[:, sl] = za.astype(BF)
            zb_ref[:, sl] = zb.astype(BF)
            mix_ref[:, sl] = (ga_ref[:, sl].astype(F32) * za + gb_ref[:, sl].astype(F32) * zb).astype(BF)
        acc = h_ref[...]
        for j in range(N_CHIPS):
            acc = acc + _dot(mix_ref[:, j * ro:(j + 1) * ro], wo_ref[j])
        ho_ref[...] = acc

    row = lambda c: pl.BlockSpec((tm, c), lambda i: (i, 0))
    full = lambda *s: pl.BlockSpec(s, lambda i: (0,) * len(s))
    sds = lambda dt: jax.ShapeDtypeStruct((t_tok, d), dt)
    return _pcall(
        body, [h, y_ret, y_fox, ga, gb, wr4, wf4, wo4], name="mix_out", grid=(t_tok // tm,),
        out_shape=[sds(F32), sds(BF), sds(BF), sds(BF)],
        in_specs=[row(d), row(RET_WIDTH), row(FOX_WIDTH), row(d), row(d),
                  full(N_CHIPS, RET_WIDTH, cz), full(N_CHIPS, FOX_WIDTH, cz), full(N_CHIPS, ro, d)],
        out_specs=[row(d)] * 4, comm=comm)


def _mix_out_bwd(dh, za, zb, ga, gb, y_fox, wr4, wf4, wo4, comm=None):
    t_tok, d = dh.shape
    cz = wr4.shape[-1]
    ro = wo4.shape[-2]
    tm = _tile(t_tok, 256)

    def body(dh_ref, za_ref, zb_ref, ga_ref, gb_ref, yf_ref, wr_ref, wf_ref, wo_ref,
             dhb_ref, dgp_ref, dza_ref, dzb_ref, dyr_ref, dyf_ref, dl_ref, db_ref):
        @pl.when(pl.program_id(0) == 0)
        def _():
            db_ref[...] = jnp.zeros_like(db_ref)

        dhb = dh_ref[...].astype(BF)
        dhb_ref[...] = dhb
        dyr = jnp.zeros((tm, RET_WIDTH), F32)
        dyf = jnp.zeros((tm, FOX_WIDTH), F32)
        for j in range(N_CHIPS):
            sl = slice(j * ro, (j + 1) * ro)
            dmix = _dot_nt(dhb, wo_ref[j])
            ga, gb = ga_ref[:, sl].astype(F32), gb_ref[:, sl].astype(F32)
            dza = (dmix * ga).astype(BF)
            dzb = (dmix * gb).astype(BF)
            dza_ref[:, sl] = dza
            dzb_ref[:, sl] = dzb
            dga = dmix * za_ref[:, sl].astype(F32) * ga * (1.0 - ga)
            dgb = dmix * zb_ref[:, sl].astype(F32) * gb * (1.0 - gb)
            dgp_ref[:, sl] = dga.astype(BF)
            dgp_ref[:, d + j * ro:d + (j + 1) * ro] = dgb.astype(BF)
            db_ref[:, sl] += jnp.sum(dga, axis=0, keepdims=True)
            db_ref[:, d + j * ro:d + (j + 1) * ro] += jnp.sum(dgb, axis=0, keepdims=True)
        for j in range(N_CHIPS):
            sl = slice(j * cz, (j + 1) * cz)
            dyr = dyr + _dot_nt(dza_ref[:, sl], wr_ref[j])
            dyf = dyf + _dot_nt(dzb_ref[:, sl], wf_ref[j])
        dyr_ref[...] = dyr
        dyfb = dyf.astype(BF)
        dyf_ref[...] = dyfb
        prod = dyfb.astype(F32) * yf_ref[...]
        first = _first_half()
        for pp in range(FOX_HEADS // 2):
            blk = prod[:, pp * 128:(pp + 1) * 128]
            s0 = jnp.sum(jnp.where(first, blk, 0.0), axis=1, keepdims=True)
            s1 = jnp.sum(jnp.where(first, 0.0, blk), axis=1, keepdims=True)
            parts = _split3(-jnp.where(first, s1, s0))
            dl_ref[:, pp * 128:(pp + 1) * 128] = _aug_put(jnp.zeros((tm, 128), BF), 0, parts)

    row = lambda c: pl.BlockSpec((tm, c), lambda i: (i, 0))
    full = lambda *s: pl.BlockSpec(s, lambda i: (0,) * len(s))
    sds = lambda c, dt: jax.ShapeDtypeStruct((t_tok, c), dt)
    return _pcall(
        body, [dh, za, zb, ga, gb, y_fox, wr4, wf4, wo4], name="mix_out_bwd", grid=(t_tok // tm,),
        out_shape=[sds(d, BF), sds(2 * d, BF), sds(d, BF), sds(d, BF), sds(RET_WIDTH, F32),
                   sds(FOX_WIDTH, BF), sds(FOX_WIDTH, BF), jax.ShapeDtypeStruct((1, 2 * d), F32)],
        in_specs=[row(d)] * 5 + [row(FOX_WIDTH), full(N_CHIPS, RET_WIDTH, cz), full(N_CHIPS, FOX_WIDTH, cz),
                                 full(N_CHIPS, ro, d)],
        out_specs=[row(d), row(2 * d), row(d), row(d), row(RET_WIDTH), row(FOX_WIDTH), row(FOX_WIDTH),
                   full(1, 2 * d)],
        comm=comm)


def _mix_in_bwd(dh, h, ln, parts, dff, dgpre, w_in, wm4, comm=None):
    t_tok, d = h.shape
    cm = wm4.shape[-1]
    tm = _tile(t_tok, 256)

    def body(dh_ref, h_ref, ln_ref, p0, p1, p2, p3, p4, p5, p6, dff_ref, dgp_ref, win_ref, wm_ref,
             dhi_ref, dln_ref, dproj_ref):
        @pl.when(pl.program_id(0) == 0)
        def _():
            dln_ref[...] = jnp.zeros_like(dln_ref)

        for k, pr in enumerate((p0, p1, p2, p3, p4, p5, p6)):
            dproj_ref[:, k * 512:(k + 1) * 512] = pr[...]
        dproj_ref[:, FF_COL:FF_COL + 128] = dff_ref[...]
        dproj_ref[:, FF_COL + 128:] = jnp.zeros((tm, IN_PAD - FF_COL - 128), BF)
        du = _dot(dproj_ref[...], win_ref[...])
        for j in range(N_CHIPS):
            du = du + _dot_nt(dgp_ref[:, j * cm:(j + 1) * cm], wm_ref[j])
        xv = h_ref[...]
        dx, dln = _rms_bwd(du, xv, _rstd(xv), ln_ref[...])
        dln_ref[...] += dln
        dhi_ref[...] = dh_ref[...] + dx

    row = lambda c: pl.BlockSpec((tm, c), lambda i: (i, 0))
    full = lambda *s: pl.BlockSpec(s, lambda i: (0,) * len(s))
    return _pcall(
        body, [dh, h, ln, *parts, dff, dgpre, w_in, wm4], name="mix_in_bwd", grid=(t_tok // tm,),
        out_shape=[jax.ShapeDtypeStruct((t_tok, d), F32), jax.ShapeDtypeStruct((1, d), F32),
                   jax.ShapeDtypeStruct((t_tok, IN_PAD), BF)],
        in_specs=[row(d), row(d), full(1, d)] + [row(512)] * 7 + [row(128), row(2 * d), full(IN_PAD, d),
                                                                   full(N_CHIPS, d, cm)],
        out_specs=[row(d), full(1, d), row(IN_PAD)], comm=comm)


def _tail(h, p, target, ln_ple, ln_fin, wpg4, wpl4, comm=None):
    t_tok, d = h.shape
    pd = p.shape[1]
    rg = wpg4.shape[-2]
    cp = wpl4.shape[-1]
    tm = _tile(t_tok, 256)

    def body(h_ref, p_ref, t_ref, lp_ref, lf_ref, wg_ref, wp_ref,
             dh_ref, n_ref, dgp_ref, dpe_ref, pb_ref, loss_ref, dlf_ref, dlp_ref, pe_s, dn_s):
        @pl.when(pl.program_id(0) == 0)
        def _():
            loss_ref[...] = jnp.zeros_like(loss_ref)
            dlf_ref[...] = jnp.zeros_like(dlf_ref)
            dlp_ref[...] = jnp.zeros_like(dlp_ref)

        xv = h_ref[...]
        r3 = _rstd(xv)
        nb = (xv * r3 * lp_ref[...]).astype(BF)
        n_ref[...] = nb
        pb = p_ref[...].astype(BF)
        pb_ref[...] = pb
        pgpre = jnp.zeros((tm, d), F32)
        for j in range(N_CHIPS):
            pgpre = pgpre + _dot(nb[:, j * rg:(j + 1) * rg], wg_ref[j])
            pe_s[:, j * cp:(j + 1) * cp] = _dot(pb, wp_ref[j])
        pg = _sigmoid(pgpre)
        pe = pe_s[...]
        h4 = xv + pg * pe
        r4 = _rstd(h4)
        err = h4 * r4 * lf_ref[...] - t_ref[...]
        loss_ref[...] += 0.5 * jnp.sum(jnp.sum(err * err, axis=1, keepdims=True), axis=0, keepdims=True) / d
        dh4, dlf = _rms_bwd(err * (1.0 / d), h4, r4, lf_ref[...])
        dlf_ref[...] += dlf
        dpe_ref[...] = (dh4 * pg).astype(BF)
        dgp = (dh4 * pe * pg * (1.0 - pg)).astype(BF)
        dgp_ref[...] = dgp
        for j in range(N_CHIPS):
            dn_s[:, j * rg:(j + 1) * rg] = _dot_nt(dgp, wg_ref[j])
        dx, dlp = _rms_bwd(dn_s[...], xv, r3, lp_ref[...])
        dlp_ref[...] += dlp
        dh_ref[...] = dh4 + dx

    row = lambda c: pl.BlockSpec((tm, c), lambda i: (i, 0))
    full = lambda *s: pl.BlockSpec(s, lambda i: (0,) * len(s))
    sds = lambda c, dt: jax.ShapeDtypeStruct((t_tok, c), dt)
    vec = jax.ShapeDtypeStruct((1, d), F32)
    return _pcall(
        body, [h, p, target, ln_ple, ln_fin, wpg4, wpl4], name="tail", grid=(t_tok // tm,),
        out_shape=[sds(d, F32), sds(d, BF), sds(d, BF), sds(d, BF), sds(pd, BF),
                   jax.ShapeDtypeStruct((1, 128), F32), vec, vec],
        in_specs=[row(d), row(pd), row(d), full(1, d), full(1, d), full(N_CHIPS, rg, d), full(N_CHIPS, pd, cp)],
        out_specs=[row(d), row(d), row(d), row(d), row(pd), full(1, 128), full(1, d), full(1, d)],
        scratch=[pltpu.VMEM((tm, d), F32), pltpu.VMEM((tm, d), F32)], comm=comm)


BIG = ["w_ffn1_gate", "w_ffn1_up", "w_ffn1_down", "w_in", "w_merge", "w_ret_out", "w_fox_out", "w_out",
       "w_ffn2_gate", "w_ffn2_up", "w_ffn2_down", "w_ple", "w_ple_gate"]
SMALL = ["ln_ffn1", "ln_mix", "b_forget", "b_merge", "ln_ffn2", "ln_ple", "ln_final"]
WEIGHTS = ["ln_ffn1", "w_ffn1_gate", "w_ffn1_up", "w_ffn1_down", "ln_mix", "w_in", "b_forget", "w_merge", "b_merge",
           "w_ret_out", "w_fox_out", "w_out", "ln_ffn2", "w_ffn2_gate", "w_ffn2_up", "w_ffn2_down", "ln_ple",
           "w_ple", "w_ple_gate", "ln_final"]


TRANSPOSED = {"w_ffn1_gate", "w_ffn1_up", "w_ffn2_gate", "w_ffn2_up", "w_in"}
IN_ROWS_PAD = -(-(IN_COLS // N_CHIPS) // 32) * 32


def _pack_small(vals, loss_row):
    rows = [loss_row]
    for name in SMALL:
        v = vals[name].reshape(-1)
        n = -(-v.shape[0] // 128) * 128
        rows.append(jnp.pad(v, (0, n - v.shape[0])).reshape(n // 128, 128))
    packed = jnp.concatenate(rows, axis=0)
    pad = -packed.shape[0] % 8
    return jnp.pad(packed, ((0, pad), (0, 0)))


def _unpack_small(packed, sizes):
    out, r = {}, 1
    for name in SMALL:
        n = sizes[name]
        nr = -(-n // 128)
        out[name] = packed[r:r + nr].reshape(1, nr * 128)[:, :n]
        r += nr
    return out


class _Stage:
    def __init__(self, comm, finish):
        self.comm, self.finish, self.result = comm, finish, None


def _hosted(fn, *a, stages=()):
    if not stages:
        return fn(*a)
    outs, couts = fn(*a, comm=_merge([st.comm for st in stages]))
    for st, o in zip(stages, _split_outs([st.comm for st in stages], couts)):
        st.result = st.finish(o)
    return outs


class _Reducer:
    def __init__(self):
        self.done = {}

    def swap(self, grads):
        names = list(grads)
        return _Stage(_c_half_swap([grads[n] for n in names]),
                      lambda outs: dict(zip(names, _add_halves([(grads[n], o) for n, o in zip(names, outs)]))))

    def exchange(self, parts):
        names = list(parts)
        return _Stage(_c_chip_exchange([parts[n] for n in names]),
                      lambda outs: dict(zip(names, _sum_chips([(parts[n], o) for n, o in zip(names, outs)]))))

    def join(self, halves):
        names = list(halves)
        return _Stage(_c_join([halves[n] for n in names]),
                      lambda outs: self.done.update({n: (halves[n], o) for n, o in zip(names, outs)}))


def kernel(x, p, positions, ln_ffn1, w_ffn1_gate, w_ffn1_up, w_ffn1_down, ln_mix, w_in, b_forget, w_merge, b_merge, w_ret_out, w_fox_out, w_out, ln_ffn2, w_ffn2_gate, w_ffn2_up, w_ffn2_down, ln_ple, w_ple, w_ple_gate, ln_final, loss_target, m_ln_ffn1, m_w_ffn1_gate, m_w_ffn1_up, m_w_ffn1_down, m_ln_mix, m_w_in, m_b_forget, m_w_merge, m_b_merge, m_w_ret_out, m_w_fox_out, m_w_out, m_ln_ffn2, m_w_ffn2_gate, m_w_ffn2_up, m_w_ffn2_down, m_ln_ple, m_w_ple, m_w_ple_gate, m_ln_final, v_ln_ffn1, v_w_ffn1_gate, v_w_ffn1_up, v_w_ffn1_down, v_ln_mix, v_w_in, v_b_forget, v_w_merge, v_b_merge, v_w_ret_out, v_w_fox_out, v_w_out, v_ln_ffn2, v_w_ffn2_gate, v_w_ffn2_up, v_w_ffn2_down, v_ln_ple, v_w_ple, v_w_ple_gate, v_ln_final):
    args = dict(locals())
    w = {n: args[n] for n in WEIGHTS}
    m = {n: args["m_" + n] for n in WEIGHTS}
    v = {n: args["v_" + n] for n in WEIGHTS}
    d = x.shape[-1]
    t_tok = x.shape[1]
    xs, ps, target = x[0], p[0, 0], loss_target[0]
    small = {n: w[n].reshape(1, -1) for n in SMALL}

    def to2d(n, a):
        if n in TRANSPOSED:
            return a[0].T
        return a.reshape(a.shape[-2], a.shape[-1]) if a.ndim == 3 else a.reshape(1, -1)

    def from2d(n, a):
        return a.T[None] if n in TRANSPOSED else a.reshape(w[n].shape)

    def padded(n, a):
        return jnp.pad(a, ((0, IN_ROWS_PAD - a.shape[0]), (0, 0))) if n == "w_in" else a

    core = lax.axis_index("c")
    me = 2 * lax.axis_index("x") + lax.axis_index("y")
    shard = {}

    def set_shard(n, s2):
        s2 = padded(n, s2)
        shard[n] = s2.reshape(1, 2, s2.shape[0] // 2, s2.shape[1])

    first = ["w_ffn1_gate", "w_ffn1_up", "w_ffn1_down"]
    for n in first + ["w_in"]:
        set_shard(n, to2d(n, w[n]).astype(BF))
    full = {}

    def gather(names):
        bufs = [lax.dynamic_update_slice(jnp.zeros((N_CHIPS,) + shard[n].shape[1:], BF), shard[n], (me, 0, 0, 0))
                for n in names]

        def finish(outs):
            full.update({n: o.reshape(N_CHIPS, 2 * o.shape[2], o.shape[3]) for n, o in zip(names, outs)})

        return _Stage(_c_all_gather(bufs), finish)

    half = RET_DIM // 2
    inv_freq = 1.0 / (ROPE_BASE ** (jnp.arange(half, dtype=F32) / half))
    later = [n for n in BIG if n not in shard]
    cos_t, sin_t, *cast = _hosted(_rope_tables, positions[0].astype(F32).reshape(t_tok, 1),
                                  jnp.repeat(inv_freq, 2).reshape(1, RET_DIM), [to2d(n, w[n]) for n in later],
                                  stages=[gather(first)])
    for n, s2 in zip(later, cast):
        set_shard(n, s2)
    consts = _ret_consts()
    b_pad = jnp.pad(small["b_forget"], ((0, 0), (0, 128 - FOX_HEADS)))

    h1, n1, g1, u1 = _hosted(
        _ffn_fwd, xs, small["ln_ffn1"], full["w_ffn1_gate"], full["w_ffn1_up"], full["w_ffn1_down"],
        stages=[gather(["w_in", "w_merge", "w_ret_out", "w_fox_out", "w_out", "w_ple_gate", "w_ple"])])
    u, rq, rk, rv, rg, fq, fk, fv, ffl, ga, gb, w_in_full = _mix_in(
        h1, small["ln_mix"], full["w_in"], full["w_merge"], small["b_merge"], cos_t, sin_t)
    aq, ak = _forget_fwd(ffl, b_pad)
    y_raw, y_ret, states = _ret_fwd(rq, rk, rv, rg, consts)
    y_fox, y_fox32, aqb = _hosted(_fox_fwd, fq, fk, fv, aq, ak,
                                  stages=[gather(["w_ffn2_gate", "w_ffn2_up", "w_ffn2_down"])])
    h2, za, zb, mix = _mix_out(h1, y_ret, y_fox, ga, gb, full["w_ret_out"], full["w_fox_out"], full["w_out"])
    h3, n2, g2, u2 = _ffn_fwd(h2, small["ln_ffn2"], full["w_ffn2_gate"], full["w_ffn2_up"], full["w_ffn2_down"])

    red = _Reducer()
    dh3, n3, dpgpre, dpe, pb, loss, dln_final, dln_ple = _tail(
        h3, ps, target, small["ln_ple"], small["ln_final"], full["w_ple_gate"], full["w_ple"])
    g_f2 = dict(w_ple_gate=_wgrad_rows("wgrad_ple_gate", n3, dpgpre, N_CHIPS),
                w_ple=_wgrad_cols("wgrad_ple", pb, dpe, N_CHIPS))
    dh2, dln_ffn2, dg2, du2, a2, dhb3 = _ffn_bwd(
        dh3, h2, small["ln_ffn2"], g2, u2, full["w_ffn2_gate"], full["w_ffn2_up"], full["w_ffn2_down"])
    g_f2["w_ffn2_gate"] = _wgrad_b_shared("wgrad_ffn2_gate", dg2, n2)
    g_f2["w_ffn2_up"] = _wgrad_b_shared("wgrad_ffn2_up", du2, n2)
    g_f2["w_ffn2_down"] = _wgrad_b_shared("wgrad_ffn2_down", a2, dhb3)

    sw_f2 = red.swap(g_f2)
    dhb2, dgpre, dza, dzb, dy_ret, dy_fox, ad, db_merge = _hosted(
        _mix_out_bwd, dh2, za, zb, ga, gb, y_fox32, full["w_ret_out"], full["w_fox_out"], full["w_out"],
        stages=[sw_f2])
    g_br = dict(w_out=_wgrad_rows("wgrad_out", mix, dhb2, N_CHIPS),
                w_ret_out=_wgrad_cols("wgrad_ret_out", y_ret, dza, N_CHIPS),
                w_fox_out=_wgrad_cols("wgrad_fox_out", y_fox, dzb, N_CHIPS))

    sw_br = red.swap(g_br)
    drq, drk, drv, drg = _hosted(_ret_bwd, rq, rk, rv, rg, y_raw, dy_ret, states, consts, cos_t, sin_t,
                                 stages=[sw_br])
    ex_f2, ex_br = red.exchange(sw_f2.result), red.exchange(sw_br.result)
    dfq, dfk, dfv, dcum_t3, dcum_q = _hosted(_fox_bwd, fq, fk, fv, dy_fox, aqb, ak, ad, stages=[ex_f2, ex_br])
    dff, db_forget = _forget_bwd(dcum_t3.reshape(FOX_HEADS, t_tok), dcum_q, ffl, b_pad)
    dh1, dln_mix, dproj = _hosted(
        _mix_in_bwd, dh2, h1, small["ln_mix"], (drq, drk, drv, drg, dfq, dfk, dfv), dff, dgpre, w_in_full,
        full["w_merge"], stages=[red.join(ex_f2.result), red.join(ex_br.result)])

    results = {}
    for names in (["w_ffn2_gate", "w_ffn2_up", "w_ffn2_down"], ["w_out", "w_ple_gate"], ["w_ret_out", "w_fox_out"],
                  ["w_ple"]):
        res = _sc_adamw_halves([(to2d(n, w[n]), *red.done[n], to2d(n, m[n]), to2d(n, v[n])) for n in names])
        for q, n in enumerate(names):
            results[n] = tuple(from2d(n, a) for a in res[4 * q:4 * q + 4])

    dx, dln_ffn1, dg1, du1, a1, dhb1 = _ffn_bwd(
        dh1, xs, small["ln_ffn1"], g1, u1, full["w_ffn1_gate"], full["w_ffn1_up"], full["w_ffn1_down"])
    g_f1g = _wgrad_b_shared("wgrad_ffn1_gate", dg1, n1)
    sw_f1g = red.swap(dict(w_ffn1_gate=g_f1g))
    g_f1u = _hosted(_wgrad_b_shared, "wgrad_ffn1_up", du1, n1, stages=[sw_f1g])
    ex_f1g, sw_f1u = red.exchange(sw_f1g.result), red.swap(dict(w_ffn1_up=g_f1u))
    g_f1d = _hosted(_wgrad_b_shared, "wgrad_ffn1_down", a1, dhb1, stages=[ex_f1g, sw_f1u])

    ex_f1u, sw_f1d = red.exchange(sw_f1u.result), red.swap(dict(w_ffn1_down=g_f1d))
    g_in = _hosted(_wgrad_in, dproj, u, stages=[ex_f1u, sw_f1d, red.join(ex_f1g.result)])
    ex_f1d, sw_in = red.exchange(sw_f1d.result), red.swap(dict(w_in=g_in))
    g_mrg = _hosted(_wgrad_cols, "wgrad_merge", u, dgpre, N_CHIPS,
                    stages=[ex_f1d, sw_in, red.join(ex_f1u.result)])

    small_grads = dict(ln_ffn1=dln_ffn1, ln_mix=dln_mix, b_forget=db_forget[:, :FOX_HEADS], b_merge=db_merge,
                       ln_ffn2=dln_ffn2, ln_ple=dln_ple, ln_final=dln_final)
    sizes = {n: w[n].size for n in SMALL}
    ex_in, sw_mrg = red.exchange(sw_in.result), red.swap(dict(w_merge=g_mrg))
    reduced = _hosted(_all_reduce_small, _pack_small(small_grads, loss),
                      stages=[ex_in, sw_mrg, red.join(ex_f1d.result)])
    gsum = _unpack_small(reduced, sizes)
    loss = reduced[0, 0]
    ex_mrg = red.exchange(sw_mrg.result)
    _hosted(_exchange_only, stages=[ex_mrg, red.join(ex_in.result)])
    _hosted(_exchange_only, stages=[red.join(ex_mrg.result)])

    def update(names):
        w2, m2, v2 = ([to2d(n, a[n]) for n in names] for a in (w, m, v))
        n = names[0]
        if n == "w_in":
            mine, other = red.done[n]
            g2 = jnp.where(core == 0, jnp.concatenate([mine, other]), jnp.concatenate([other, mine]))
            g2 = g2[:w2[0].shape[0]]
            rows3 = lambda a: jnp.transpose(a, (2, 0, 1))
            g3 = g2.reshape(g2.shape[0], 1, g2.shape[1])
            res = [g3] + _adamw(rows3(w[n]), g3, rows3(m[n]), rows3(v[n]))
            results[n] = tuple(jnp.transpose(a, (1, 2, 0)) for a in res)
            return
        res = _adamw_halves([(w2[q], *red.done[names[q]], m2[q], v2[q]) for q in range(len(names))])
        for q, name in enumerate(names):
            results[name] = tuple(from2d(name, a) for a in res[4 * q:4 * q + 4])

    res = _adamw_vectors([(to2d(n, w[n]), gsum[n], to2d(n, m[n]), to2d(n, v[n])) for n in SMALL])
    for q, n in enumerate(SMALL):
        results[n] = tuple(from2d(n, a) for a in [gsum[n]] + res[3 * q:3 * q + 3])
    update(["w_ffn1_gate", "w_ffn1_up", "w_ffn1_down"])
    for n in WEIGHTS:
        if n not in results:
            update([n])

    outs = [[results[n][k] for n in WEIGHTS] for k in range(4)]
    return (loss, dx[None], *outs[0], *outs[1], *outs[2], *outs[3])
```

```python
import functools
import operator

import jax
import jax.numpy as jnp
from jax import lax
from jax.experimental import pallas as pl
from jax.experimental.pallas import tpu as pltpu
from jax.experimental.pallas import tpu_sc as plsc

F32 = jnp.float32
BF = jnp.bfloat16
MESH = pl.DeviceIdType.MESH

EPS = 1e-6
ROPE_BASE = 10000.0
N_CHIPS = 4
RET_HEADS = 4
RET_DIM = 128
RET_WIDTH = RET_HEADS * RET_DIM
RET_CHUNK = 128
RET_SCALE = RET_DIM ** -0.5
FOX_HEADS = 8
FOX_DIM = 64
FOX_WIDTH = FOX_HEADS * FOX_DIM
FOX_SCALE = FOX_DIM ** -0.5
IN_COLS = 4 * RET_WIDTH + 3 * FOX_WIDTH + FOX_HEADS
IN_PAD = 4096
FF_COL = 4 * RET_WIDTH + 3 * FOX_WIDTH
NEG = -1e30

ADAM_LR = 0.001
ADAM_B1 = 0.9
ADAM_B2 = 0.999
ADAM_EPS = 1e-08
ADAM_WD = 0.01
ADAM_STEP = 10

VMEM_LIMIT = 52 * 1024 * 1024

RELAY_MIN_STEPS = 16

NT = (((1,), (1,)), ((), ()))
TN = (((0,), (0,)), ((), ()))

HBM_SPEC = pl.BlockSpec(memory_space=pltpu.HBM)
VMEM_SPEC = pl.BlockSpec(memory_space=pltpu.VMEM)


def _dot(a, b):
    return jnp.dot(a, b, preferred_element_type=F32)


def _dot_nt(a, b):
    return lax.dot_general(a, b, NT, preferred_element_type=F32)


def _dot_tn(a, b):
    return lax.dot_general(a, b, TN, preferred_element_type=F32)


def _rstd(xv):
    return lax.rsqrt(jnp.mean(xv * xv, axis=-1, keepdims=True) + EPS)


def _rms_bwd(dn, xv, r, ln):
    xh = xv * r
    dxh = dn * ln
    dx = r * (dxh - xh * jnp.mean(dxh * xh, axis=-1, keepdims=True))
    return dx, jnp.sum(dn * xh, axis=0, keepdims=True)


def _sigmoid(x):
    return jax.nn.sigmoid(x)


def _tile(n, pref):
    return pref if n % pref == 0 else n


def _row_tile(n, cap):
    best = [t for t in range(16, min(n, cap) + 1, 16) if n % t == 0]
    return best[-1] if best else n


class _Comm:
    def __init__(self, ins, out_shapes, sems, start, wait, aliases=None, relay=None):
        self.ins, self.out_shapes, self.sems, self.start, self.wait = list(ins), list(out_shapes), list(sems), start, wait
        self.aliases = dict(aliases or {})
        self.relay = relay


def _merge(comms):
    comms = [c for c in comms if c is not None]
    if not comms:
        return None
    bounds, ni, no, ns = [], 0, 0, 0
    for c in comms:
        bounds.append((ni, no, ns))
        ni, no, ns = ni + len(c.ins), no + len(c.out_shapes), ns + len(c.sems)

    def run(which):
        def f(ins, outs, sems, **kw):
            for c, (i, o, s) in zip(comms, bounds):
                fn = getattr(c, which)
                if fn is not None:
                    fn(ins[i:i + len(c.ins)], outs[o:o + len(c.out_shapes)], sems[s:s + len(c.sems)],
                       **(kw if c.relay is not None else {}))
        return f

    aliases = {i + a: o + b for c, (i, o, _) in zip(comms, bounds) for a, b in c.aliases.items()}
    relay = run("relay") if any(c.relay is not None for c in comms) else None
    return _Comm([a for c in comms for a in c.ins], [a for c in comms for a in c.out_shapes],
                 [a for c in comms for a in c.sems], run("start"), run("wait"), aliases, relay)


def _split_outs(comms, outs):
    res, o = [], 0
    for c in comms:
        if c is not None:
            res.append(list(outs[o:o + len(c.out_shapes)]))
            o += len(c.out_shapes)
    return res


def _pcall(body, args, *, name, out_shape, grid=(), in_specs=None, out_specs=None, scratch=(), comm=None,
           prefetch=()):
    many = isinstance(out_shape, (list, tuple))
    outs = list(out_shape) if many else [out_shape]
    n_pre, n_in, n_out, n_scr = len(prefetch), len(args), len(outs), len(scratch)
    if in_specs is None:
        in_specs, out_specs = [VMEM_SPEC] * n_in, [VMEM_SPEC] * n_out
    else:
        in_specs, out_specs = list(in_specs), (list(out_specs) if many else [out_specs])
    params = pltpu.CompilerParams(dimension_semantics=("arbitrary",) * len(grid), vmem_limit_bytes=VMEM_LIMIT)
    scalars = [jnp.reshape(s, (1,)).astype(jnp.int32) for s in prefetch]
    ci, co = (len(comm.ins), len(comm.out_shapes)) if comm is not None else (0, 0)

    def wrapped(*refs):
        pre, refs = refs[:n_pre], refs[n_pre:]
        a, ca = refs[:n_in], refs[n_in:n_in + ci]
        o = refs[n_in + ci:n_in + ci + n_out]
        cout = refs[n_in + ci + n_out:n_in + ci + n_out + co]
        s = refs[n_in + ci + n_out + co:n_in + ci + n_out + co + n_scr]
        csem = refs[n_in + ci + n_out + co + n_scr:]
        if comm is None:
            body(*pre, *a, *o, *s)
        elif grid:
            step = functools.reduce(lambda acc, k: acc * grid[k] + pl.program_id(k), range(len(grid)), 0)
            n_steps = functools.reduce(operator.mul, grid)
            relayed = comm.relay is not None and n_steps >= RELAY_MIN_STEPS
            pl.when(step == 0)(lambda: comm.start(ca, cout, csem))
            if relayed:
                pl.when(step == n_steps - n_steps // 8)(lambda: comm.relay(ca, cout, csem))
            body(*pre, *a, *o, *s)
            pl.when(step == n_steps - 1)(lambda: comm.wait(ca, cout, csem, **({"relayed": True} if relayed else {})))
        else:
            comm.start(ca, cout, csem)
            body(*pre, *a, *o, *s)
            comm.wait(ca, cout, csem)

    c_ins, c_outs, c_sems, aliases = ([], [], [], {}) if comm is None else (
        comm.ins, comm.out_shapes, comm.sems, {n_pre + n_in + i: n_out + o for i, o in comm.aliases.items()})
    all_in, all_out = in_specs + [HBM_SPEC] * ci, out_specs + [HBM_SPEC] * co
    all_scr = list(scratch) + c_sems
    if grid:
        args = [pltpu.with_memory_space_constraint(a, pltpu.HBM) for a in args]
    c_ins = [pltpu.with_memory_space_constraint(a, pltpu.HBM) for a in c_ins]
    if n_pre:
        spec = dict(grid_spec=pltpu.PrefetchScalarGridSpec(
            num_scalar_prefetch=n_pre, grid=grid, in_specs=all_in, out_specs=all_out, scratch_shapes=all_scr))
    else:
        spec = dict(grid=grid, in_specs=all_in, out_specs=all_out, scratch_shapes=all_scr)
    res = pl.pallas_call(wrapped, name=name, out_shape=outs + c_outs, input_output_aliases=aliases,
                         compiler_params=params, **spec)(*scalars, *args, *c_ins)
    mine = list(res[:n_out])
    mine = mine if many else mine[0]
    return mine if comm is None else (mine, list(res[n_out:]))


def _peer_chips(x, y):
    return [(1 - x, y), (x, 1 - y), (1 - x, 1 - y)]


def _c_all_gather(bufs):
    n = len(bufs)

    def copies(ins, outs, sems):
        send_sems, recv_sems, fwd_send, fwd_recv = sems
        x, y, c = lax.axis_index("x"), lax.axis_index("y"), lax.axis_index("c")
        me = 2 * x + y
        peers = _peer_chips(x, y)
        chip = [2 * px + py for px, py in peers]

        def ici(g, j, slot):
            return pltpu.make_async_remote_copy(
                src_ref=outs[g].at[me, c], dst_ref=outs[g].at[slot, c], send_sem=send_sems.at[g, j],
                recv_sem=recv_sems.at[g, j], device_id=(*peers[j], c), device_id_type=MESH)

        def d2d(g, j, half):
            return pltpu.make_async_remote_copy(
                src_ref=outs[g].at[chip[j], half], dst_ref=outs[g].at[chip[j], half], send_sem=fwd_send.at[g, j],
                recv_sem=fwd_recv.at[g, j], device_id=(x, y, 1 - c), device_id_type=MESH)

        pairs = [(g, j) for g in range(n) for j in range(3)]
        sends = [ici(g, j, me) for g, j in pairs]
        recvs = [ici(g, j, chip[j]) for g, j in pairs]
        passes = [d2d(g, j, c) for g, j in pairs]
        passed = [d2d(g, j, 1 - c) for g, j in pairs]
        return sends, recvs, passes, passed

    def start(ins, outs, sems):
        for cp in copies(ins, outs, sems)[0]:
            cp.start()

    def relay(ins, outs, sems):
        _, recvs, passes, _ = copies(ins, outs, sems)
        for rcv, fwd in zip(recvs, passes):
            rcv.wait_recv()
            fwd.start()

    def wait(ins, outs, sems, relayed=False):
        if not relayed:
            relay(ins, outs, sems)
        sends, _, passes, passed = copies(ins, outs, sems)
        for cp in passed:
            cp.wait_recv()
        for cp in sends + passes:
            cp.wait_send()

    pair_sems = pltpu.SemaphoreType.DMA((n, 3))
    return _Comm(bufs, [jax.ShapeDtypeStruct(s.shape, s.dtype) for s in bufs], [pair_sems] * 4, start, wait,
                 aliases={g: g for g in range(n)}, relay=relay)


def _start_wait(copies):
    def start(ins, outs, sems):
        local, sends, _ = copies(ins, outs, sems)
        for cp in local + sends:
            cp.start()

    def wait(ins, outs, sems):
        local, sends, recvs = copies(ins, outs, sems)
        for cp in recvs:
            cp.wait_recv()
        for cp in sends:
            cp.wait_send()
        for cp in local:
            cp.wait()

    return start, wait


def _c_half_swap(grads):
    n = len(grads)

    def copies(ins, outs, sems):
        send_sems, recv_sems = sems
        x, y, c = lax.axis_index("x"), lax.axis_index("y"), lax.axis_index("c")
        sends = []
        for g in range(n):
            half = ins[g].shape[1] // 2
            sends.append(pltpu.make_async_remote_copy(
                src_ref=ins[g].at[:, pl.ds((1 - c) * half, half), :], dst_ref=outs[g],
                send_sem=send_sems.at[g], recv_sem=recv_sems.at[g], device_id=(x, y, 1 - c), device_id_type=MESH))
        return [], sends, sends

    return _Comm(
        grads, [jax.ShapeDtypeStruct((N_CHIPS, s.shape[1] // 2, s.shape[2]), s.dtype) for s in grads],
        [pltpu.SemaphoreType.DMA((n,)), pltpu.SemaphoreType.DMA((n,))], *_start_wait(copies))


def _c_chip_exchange(parts):
    n = len(parts)

    def copies(ins, outs, sems):
        send_sems, recv_sems = sems
        x, y, c = lax.axis_index("x"), lax.axis_index("y"), lax.axis_index("c")
        peers = _peer_chips(x, y)

        def remote(g, j):
            return pltpu.make_async_remote_copy(
                src_ref=ins[g].at[2 * peers[j][0] + peers[j][1]], dst_ref=outs[g].at[j],
                send_sem=send_sems.at[g, j], recv_sem=recv_sems.at[g, j], device_id=(*peers[j], c),
                device_id_type=MESH)

        sends = [remote(g, j) for g in range(n) for j in range(3)]
        return [], sends, sends

    return _Comm(
        parts, [jax.ShapeDtypeStruct((3,) + s.shape[1:], s.dtype) for s in parts],
        [pltpu.SemaphoreType.DMA((n, 3)), pltpu.SemaphoreType.DMA((n, 3))], *_start_wait(copies))


def _c_join(halves):
    n = len(halves)

    def copies(ins, outs, sems):
        send_sems, recv_sems = sems
        x, y, c = lax.axis_index("x"), lax.axis_index("y"), lax.axis_index("c")
        sends = [pltpu.make_async_remote_copy(
            src_ref=ins[g], dst_ref=outs[g], send_sem=send_sems.at[g], recv_sem=recv_sems.at[g],
            device_id=(x, y, 1 - c), device_id_type=MESH) for g in range(n)]
        return [], sends, sends

    return _Comm(
        halves, [jax.ShapeDtypeStruct(s.shape, s.dtype) for s in halves],
        [pltpu.SemaphoreType.DMA((n,)), pltpu.SemaphoreType.DMA((n,))], *_start_wait(copies))


def _exchange_only(comm=None):
    def body(o_ref):
        o_ref[...] = jnp.zeros_like(o_ref)

    return _pcall(body, [], name="exchange_only", out_shape=jax.ShapeDtypeStruct((8, 128), F32), comm=comm)


def _all_reduce_small(v, comm=None):
    rows = v.shape[0]

    def body(v_ref, out_ref, buf, send_sems, recv_sems):
        x, y, c = lax.axis_index("x"), lax.axis_index("y"), lax.axis_index("c")
        me = 4 * x + 2 * y + c
        buf[me] = v_ref[...]
        flips = [(fx, fy, fc) for fx in (0, 1) for fy in (0, 1) for fc in (0, 1)][1:]

        def peer(k):
            fx, fy, fc = flips[k]
            px, py, pc = x ^ fx, y ^ fy, c ^ fc
            return (px, py, pc), 4 * px + 2 * py + pc

        def copy(k, slot):
            return pltpu.make_async_remote_copy(
                src_ref=buf.at[slot], dst_ref=buf.at[slot], send_sem=send_sems.at[k],
                recv_sem=recv_sems.at[k], device_id=peer(k)[0], device_id_type=MESH)

        sends = [copy(k, me) for k in range(7)]
        for cp in sends:
            cp.start()
        for k in range(7):
            copy(k, peer(k)[1]).wait_recv()
        for cp in sends:
            cp.wait_send()
        acc = buf[0]
        for d in range(1, 8):
            acc = acc + buf[d]
        out_ref[...] = acc

    return _pcall(body, [v], name="all_reduce_small", out_shape=jax.ShapeDtypeStruct((rows, 128), F32),
                  scratch=[pltpu.VMEM((8, rows, 128), F32), pltpu.SemaphoreType.DMA((7,)),
                           pltpu.SemaphoreType.DMA((7,))], comm=comm)


def _add_halves(pairs):
    k = len(pairs)

    def body(h_ref, *refs):
        for a_ref, b_ref, o_ref in zip(refs[0:2 * k:2], refs[1:2 * k:2], refs[2 * k:]):
            o_ref[...] = (a_ref[...].astype(F32) + b_ref[...].astype(F32)).astype(o_ref.dtype)

    in_specs, out_specs = [], []
    for _, got in pairs:
        _, h, c = got.shape
        spec = pl.BlockSpec((1, h, c), lambda j, h_ref: (j, 0, 0))
        in_specs += [pl.BlockSpec((1, h, c), lambda j, h_ref: (j, h_ref[0], 0)), spec]
        out_specs.append(spec)
    return _pcall(body, [a for pair in pairs for a in pair], name="add_halves", grid=(N_CHIPS,),
                  prefetch=[lax.axis_index("c")], in_specs=in_specs, out_specs=out_specs,
                  out_shape=[jax.ShapeDtypeStruct(got.shape, BF) for _, got in pairs])


def _sum_chips(pairs):
    k = len(pairs)
    n_steps = 2 if all(parts.shape[1] % 32 == 0 for parts, _ in pairs) else 1
    me = 2 * lax.axis_index("x") + lax.axis_index("y")

    def body(me_ref, *refs):
        for p_ref, r_ref, o_ref in zip(refs[0:2 * k:2], refs[1:2 * k:2], refs[2 * k:]):
            acc = p_ref[0].astype(F32)
            for s in range(N_CHIPS - 1):
                acc = acc + r_ref[s].astype(F32)
            o_ref[...] = acc

    in_specs, out_specs = [], []
    for parts, _ in pairs:
        _, h, c = parts.shape
        th = h // n_steps
        in_specs += [pl.BlockSpec((1, th, c), lambda i, me_ref: (me_ref[0], i, 0)),
                     pl.BlockSpec((N_CHIPS - 1, th, c), lambda i, me_ref: (0, i, 0))]
        out_specs.append(pl.BlockSpec((th, c), lambda i, me_ref: (i, 0)))
    return _pcall(body, [a for pair in pairs for a in pair], name="sum_chips", grid=(n_steps,), prefetch=[me],
                  in_specs=in_specs, out_specs=out_specs,
                  out_shape=[jax.ShapeDtypeStruct(parts.shape[1:], F32) for parts, _ in pairs])


def _adam_update(w, gv, m, v, d_ref, nm_ref, nv_ref):
    c1 = 1.0 / (1.0 - ADAM_B1 ** ADAM_STEP)
    c2 = 1.0 / (1.0 - ADAM_B2 ** ADAM_STEP)
    nm = ADAM_B1 * m + (1.0 - ADAM_B1) * gv
    nv = ADAM_B2 * v + (1.0 - ADAM_B2) * (gv * gv)
    nm_ref[...] = nm
    nv_ref[...] = nv
    d_ref[...] = -ADAM_LR * ((nm * c1) / (jnp.sqrt(nv * c2) + ADAM_EPS) + ADAM_WD * w)


def _adamw(w, g, m, v, comm=None):
    r, c = w.shape[0], w.shape[-1]
    tr = _row_tile(r, 512)

    def body(w_ref, g_ref, m_ref, v_ref, d_ref, nm_ref, nv_ref):
        _adam_update(w_ref[...], g_ref[...], m_ref[...], v_ref[...], d_ref, nm_ref, nv_ref)

    mid = (1,) * (w.ndim - 2)
    spec = pl.BlockSpec((tr,) + mid + (c,), lambda i: (i,) + (0,) * (w.ndim - 1))
    sds = jax.ShapeDtypeStruct(w.shape, F32)
    return _pcall(body, [w, g, m, v], name="adamw", grid=(r // tr,), out_shape=[sds, sds, sds],
                  in_specs=[spec] * 4, out_specs=[spec] * 3, comm=comm)


def _adamw_vectors(items):
    k = len(items)

    def body(*refs):
        ins, outs = refs[:4 * k], refs[4 * k:]
        for q in range(k):
            w_ref, g_ref, m_ref, v_ref = ins[4 * q:4 * q + 4]
            _adam_update(w_ref[...], g_ref[...], m_ref[...], v_ref[...], *outs[3 * q:3 * q + 3])

    return _pcall(body, [a for it in items for a in it], name="adamw_vectors",
                  out_shape=[jax.ShapeDtypeStruct(it[0].shape, F32) for it in items for _ in range(3)])


def _adamw_halves(items, comm=None):
    k = len(items)
    r, c = items[0][0].shape
    h = r // 2
    tr = _row_tile(h, min(512, (VMEM_LIMIT * 3 // 4) // (k * 9 * 2 * 4 * c)))
    nb = h // tr
    core = lax.axis_index("c")

    def body(c_ref, *refs):
        ins, outs = refs[:5 * k], refs[5 * k:]
        for q in range(k):
            w_ref, gm_ref, go_ref, m_ref, v_ref = ins[5 * q:5 * q + 5]
            g_ref, d_ref, nm_ref, nv_ref = outs[4 * q:4 * q + 4]
            gv = jnp.where(pl.program_id(0) == c_ref[0], gm_ref[...], go_ref[...])
            g_ref[...] = gv
            _adam_update(w_ref[...], gv, m_ref[...], v_ref[...], d_ref, nm_ref, nv_ref)

    full = pl.BlockSpec((tr, c), lambda hh, i, c_ref: (hh * nb + i, 0))
    half = pl.BlockSpec((tr, c), lambda hh, i, c_ref: (i, 0))
    sds = jax.ShapeDtypeStruct((r, c), F32)
    return _pcall(body, [a for it in items for a in it], name="adamw_halves", grid=(2, nb), prefetch=[core],
                  out_shape=[sds] * (4 * k), in_specs=[full, half, half, full, full] * k, out_specs=[full] * (4 * k),
                  comm=comm)


SC_CORES, SC_TILES, SC_LANES = 2, 16, 16
SC_BLOCK_ROWS, SC_BLOCK_COLS = 8, 512


def _sc_adamw_halves(items):
    k = len(items)
    r, c = items[0][0].shape
    h = r // 2
    bc = min(c, SC_BLOCK_COLS)
    c1 = 1.0 / (1.0 - ADAM_B1 ** ADAM_STEP)
    c2 = 1.0 / (1.0 - ADAM_B2 ** ADAM_STEP)
    mesh = plsc.VectorSubcoreMesh(core_axis_name="sc_core", subcore_axis_name="sc_tile",
                                  num_cores=SC_CORES, num_subcores=SC_TILES)
    spec = pl.BlockSpec(block_shape=(SC_BLOCK_ROWS, bc), index_map=lambda i, j: (i, j))

    def block(w_v, gin_v, m_v, v_v, g_v, d_v, nm_v, nv_v):
        @pl.loop(0, SC_BLOCK_ROWS)
        def _(row):
            @pl.loop(0, bc, step=SC_LANES)
            def _(col):
                at = (pl.ds(row, 1), pl.ds(col, SC_LANES))
                gv = gin_v.at[*at][...]
                nm = ADAM_B1 * m_v.at[*at][...] + (1.0 - ADAM_B1) * gv
                nv = ADAM_B2 * v_v.at[*at][...] + (1.0 - ADAM_B2) * (gv * gv)
                g_v.at[*at][...] = gv
                nm_v.at[*at][...] = nm
                nv_v.at[*at][...] = nv
                d_v.at[*at][...] = -ADAM_LR * ((nm * c1) / (jnp.sqrt(nv * c2) + ADAM_EPS) + ADAM_WD * w_v.at[*at][...])

    def kern(*refs):
        ins, outs = refs[:5 * k], refs[5 * k:]
        core = lax.axis_index("c")

        def half(q, hh, mine):
            w_hbm, gm_hbm, go_hbm, m_hbm, v_hbm = ins[5 * q:5 * q + 5]
            rows = pl.ds(hh * h, h)
            pltpu.emit_pipeline(
                block, grid=(h // SC_BLOCK_ROWS, c // bc), in_specs=[spec] * 4, out_specs=[spec] * 4,
                core_axis_name=("sc_core", "sc_tile"), dimension_semantics=(pltpu.PARALLEL, pltpu.PARALLEL),
                trace_scopes=False,
            )(w_hbm.at[rows, :], gm_hbm if mine else go_hbm, m_hbm.at[rows, :], v_hbm.at[rows, :],
              *(o.at[rows, :] for o in outs[4 * q:4 * q + 4]))

        for q in range(k):
            for hh in range(2):
                pl.when(core == hh)(lambda q=q, hh=hh: half(q, hh, True))
                pl.when(core != hh)(lambda q=q, hh=hh: half(q, hh, False))

    sds = jax.ShapeDtypeStruct((r, c), F32)
    return pl.kernel(kern, out_type=[sds] * (4 * k), mesh=mesh, scratch_types=[], name="sc_adamw_halves")(
        *(a for it in items for a in it))


def _wgrad(name, a, b, a_spec, b_spec, m, n, nb, comm):
    def body(a_ref, b_ref, o_ref):
        o_ref[...] = _dot_tn(a_ref[...], b_ref[...]).astype(o_ref.dtype)

    return _pcall(body, [a, b], name=name, grid=(nb,), out_shape=jax.ShapeDtypeStruct((nb, m, n), BF),
                  in_specs=[a_spec, b_spec], out_specs=pl.BlockSpec((None, m, n), lambda j: (j, 0, 0)), comm=comm)


def _wgrad_cols(name, a, b, nb, comm=None):
    t_tok, m = a.shape
    n = b.shape[1] // nb
    return _wgrad(name, a, b, pl.BlockSpec((t_tok, m), lambda j: (0, 0)), pl.BlockSpec((t_tok, n), lambda j: (0, j)),
                  m, n, nb, comm)


def _wgrad_rows(name, a, b, nb, comm=None):
    t_tok, n = b.shape
    m = a.shape[1] // nb
    return _wgrad(name, a, b, pl.BlockSpec((t_tok, m), lambda j: (0, j)), pl.BlockSpec((t_tok, n), lambda j: (0, 0)),
                  m, n, nb, comm)


def _wgrad_in(a, b, comm=None):
    t_tok, d = b.shape
    m = 512
    nb = IN_PAD // m
    rows = IN_COLS // N_CHIPS

    def body(a_ref, b_ref, o_ref, g_ref):
        @pl.when(pl.program_id(0) == 0)
        def _():
            for j in range(N_CHIPS):
                o_ref[j, rows:, :] = jnp.zeros((IN_ROWS_PAD - rows, d), BF)

        g_ref[...] = _dot_tn(a_ref[...], b_ref[...]).astype(BF)
        for i in range(nb):
            @pl.when(pl.program_id(0) == i)
            def _(i=i):
                lo, hi = i * m, min((i + 1) * m, IN_COLS)
                while lo < hi:
                    j = lo // rows
                    end = min(hi, (j + 1) * rows)
                    o_ref[j, lo - j * rows:end - j * rows, :] = g_ref[lo - i * m:end - i * m, :]
                    lo = end

    return _pcall(body, [a, b], name="wgrad_in", grid=(nb,),
                  out_shape=jax.ShapeDtypeStruct((N_CHIPS, IN_ROWS_PAD, d), BF),
                  in_specs=[pl.BlockSpec((t_tok, m), lambda i: (0, i)), pl.BlockSpec((t_tok, d), lambda i: (0, 0))],
                  out_specs=pl.BlockSpec((N_CHIPS, IN_ROWS_PAD, d), lambda i: (0, 0, 0)),
                  scratch=[pltpu.VMEM((m, d), BF)], comm=comm)


def _wgrad_a_shared(name, a, b4, comm=None):
    t_tok, m = a.shape
    nb, _, n = b4.shape
    return _wgrad(name, a, b4, pl.BlockSpec((t_tok, m), lambda j: (0, 0)),
                  pl.BlockSpec((None, t_tok, n), lambda j: (j, 0, 0)), m, n, nb, comm)


def _wgrad_b_shared(name, a4, b, comm=None):
    nb, t_tok, m = a4.shape
    n = b.shape[1]
    return _wgrad(name, a4, b, pl.BlockSpec((None, t_tok, m), lambda j: (j, 0, 0)),
                  pl.BlockSpec((t_tok, n), lambda j: (0, 0)), m, n, nb, comm)


def _w4_spec(r, c):
    return pl.BlockSpec((None, r, c), lambda i, j: (j, 0, 0))


FFN_ROW_CHUNK = 256


def _row_chunks(tm):
    rc = FFN_ROW_CHUNK if tm % FFN_ROW_CHUNK == 0 else tm
    return [slice(r, r + rc) for r in range(0, tm, rc)]


def _ffn_fwd(h, ln, wg4, wu4, wd4, comm=None):
    t_tok, d = h.shape
    f = wg4.shape[-2]
    tm = _tile(t_tok, 512)

    def body(h_ref, ln_ref, wg_ref, wu_ref, wd_ref, ho_ref, n_ref, g_ref, u_ref, n_s, acc):
        j = pl.program_id(1)

        @pl.when(j == 0)
        def _():
            xv = h_ref[...]
            nv = (xv * _rstd(xv) * ln_ref[...]).astype(BF)
            n_s[...] = nv
            n_ref[...] = nv
            acc[...] = jnp.zeros_like(acc)

        nv = n_s[...]
        g = _dot_nt(nv, wg_ref[...])
        u = _dot_nt(nv, wu_ref[...])
        g_ref[...] = g.astype(BF)
        u_ref[...] = u.astype(BF)
        a = (g * _sigmoid(g) * u).astype(BF)
        acc[...] += _dot(a, wd_ref[...])

        @pl.when(j == N_CHIPS - 1)
        def _():
            ho_ref[...] = h_ref[...] + 0.5 * acc[...]

    row = pl.BlockSpec((tm, d), lambda i, j: (i, 0))
    gu = pl.BlockSpec((None, tm, f), lambda i, j: (j, i, 0))
    gu_sds = jax.ShapeDtypeStruct((N_CHIPS, t_tok, f), BF)
    return _pcall(
        body, [h, ln, wg4, wu4, wd4], name="ffn_fwd", grid=(t_tok // tm, N_CHIPS),
        out_shape=[jax.ShapeDtypeStruct((t_tok, d), F32), jax.ShapeDtypeStruct((t_tok, d), BF), gu_sds, gu_sds],
        in_specs=[row, pl.BlockSpec((1, d), lambda i, j: (0, 0)), _w4_spec(f, d), _w4_spec(f, d), _w4_spec(f, d)],
        out_specs=[row, row, gu, gu],
        scratch=[pltpu.VMEM((tm, d), BF), pltpu.VMEM((tm, d), F32)], comm=comm)


def _ffn_bwd(dho, h, ln, g4, u4, wg4, wu4, wd4, comm=None):
    t_tok, d = h.shape
    f = wg4.shape[-2]
    tm = _tile(t_tok, 512)

    def body(dho_ref, h_ref, ln_ref, g_ref, u_ref, wg_ref, wu_ref, wd_ref,
             dhi_ref, dln_ref, dg_ref, du_ref, a_ref, dhb_ref, dhb_s, dn_acc):
        i, j = pl.program_id(0), pl.program_id(1)

        @pl.when(j == 0)
        def _():
            dhb = (0.5 * dho_ref[...]).astype(BF)
            dhb_s[...] = dhb
            dhb_ref[...] = dhb
            dn_acc[...] = jnp.zeros_like(dn_acc)

        @pl.when((i == 0) & (j == 0))
        def _():
            dln_ref[...] = jnp.zeros_like(dln_ref)

        for rows in _row_chunks(tm):
            g = g_ref[rows, :].astype(F32)
            u = u_ref[rows, :].astype(F32)
            s = _sigmoid(g)
            sg = g * s
            a_ref[rows, :] = (sg * u).astype(BF)
            da = _dot_nt(dhb_s[rows, :], wd_ref[...])
            dg = (da * u * (s * (1.0 + g * (1.0 - s)))).astype(BF)
            du = (da * sg).astype(BF)
            dg_ref[rows, :] = dg
            du_ref[rows, :] = du
            dn_acc[rows, :] += _dot(dg, wg_ref[...]) + _dot(du, wu_ref[...])

        @pl.when(j == N_CHIPS - 1)
        def _():
            xv = h_ref[...]
            dx, dln = _rms_bwd(dn_acc[...], xv, _rstd(xv), ln_ref[...])
            dln_ref[...] += dln
            dhi_ref[...] = dho_ref[...] + dx

    row = pl.BlockSpec((tm, d), lambda i, j: (i, 0))
    vec = pl.BlockSpec((1, d), lambda i, j: (0, 0))
    gu = pl.BlockSpec((None, tm, f), lambda i, j: (j, i, 0))
    gu_sds = jax.ShapeDtypeStruct((N_CHIPS, t_tok, f), BF)
    return _pcall(
        body, [dho, h, ln, g4, u4, wg4, wu4, wd4], name="ffn_bwd", grid=(t_tok // tm, N_CHIPS),
        out_shape=[jax.ShapeDtypeStruct((t_tok, d), F32), jax.ShapeDtypeStruct((1, d), F32),
                   gu_sds, gu_sds, gu_sds, jax.ShapeDtypeStruct((t_tok, d), BF)],
        in_specs=[row, row, vec, gu, gu, _w4_spec(f, d), _w4_spec(f, d), _w4_spec(f, d)],
        out_specs=[row, vec, gu, gu, gu, row],
        scratch=[pltpu.VMEM((tm, d), BF), pltpu.VMEM((tm, d), F32)], comm=comm)


def _rope_tables(pos_col, inv_freq2, blocks, comm=None):
    t_tok = pos_col.shape[0]
    k = len(blocks)
    n_steps = 4
    assert all(b.shape[0] % (16 * n_steps) == 0 for b in blocks)

    def body(p_ref, f_ref, *refs):
        cos_ref, sin_ref = refs[k:k + 2]
        ang = p_ref[...] * f_ref[...]
        lane = lax.broadcasted_iota(jnp.int32, ang.shape, 1)
        s = jnp.sin(ang)
        cos_ref[...] = jnp.cos(ang)
        sin_ref[...] = jnp.where((lane & 1) == 0, -s, s)
        for a_ref, o_ref in zip(refs[:k], refs[k + 2:]):
            o_ref[...] = a_ref[...].astype(BF)

    rows = lambda r, c: pl.BlockSpec((r // n_steps, c), lambda i: (i, 0))
    casts = [rows(*b.shape) for b in blocks]
    sds = jax.ShapeDtypeStruct((t_tok, 128), F32)
    return _pcall(body, [pos_col, inv_freq2, *blocks], name="rope_tables", grid=(n_steps,),
                  out_shape=[sds, sds] + [jax.ShapeDtypeStruct(b.shape, BF) for b in blocks],
                  in_specs=[rows(t_tok, 1), pl.BlockSpec((1, 128), lambda i: (0, 0))] + casts,
                  out_specs=[rows(t_tok, 128)] * 2 + casts, comm=comm)


def _swap_pairs(x):
    lane = lax.broadcasted_iota(jnp.int32, x.shape, 1)
    return jnp.where((lane & 1) == 0, pltpu.roll(x, 127, 1), pltpu.roll(x, 1, 1))


def _mix_in(h, ln, w_in4, wm4, b_m, cos_t, sin_t, comm=None):
    t_tok, d = h.shape
    cm = wm4.shape[-1]
    tm = _tile(t_tok, 256)
    rows = IN_COLS // N_CHIPS

    def body(h_ref, ln_ref, win4_ref, wm_ref, bm_ref, cos_ref, sin_ref,
             u_ref, rq_ref, rk_ref, rv_ref, rg_ref, fq_ref, fk_ref, fv_ref, ff_ref, ga_ref, gb_ref, win_ref):
        @pl.when(pl.program_id(0) == 0)
        def _():
            for j in range(N_CHIPS):
                win_ref[j * rows:(j + 1) * rows, :] = win4_ref[j, :rows, :]
            win_ref[IN_COLS:, :] = jnp.zeros((IN_PAD - IN_COLS, d), BF)

        xv = h_ref[...]
        ub = (xv * _rstd(xv) * ln_ref[...]).astype(BF)
        u_ref[...] = ub
        cosv, sinv = cos_ref[...], sin_ref[...]

        def sec(k):
            return _dot_nt(ub, win_ref[k * 512:(k + 1) * 512, :])

        def rot(xh):
            return xh * cosv + _swap_pairs(xh) * sinv

        pq, pk = sec(0), sec(1)
        for hh in range(RET_HEADS):
            sl = slice(hh * RET_DIM, (hh + 1) * RET_DIM)
            rq_ref[:, sl] = rot(pq[:, sl]).astype(BF)
            rk_ref[:, sl] = (rot(pk[:, sl]) * RET_SCALE).astype(BF)
        rv_ref[...] = sec(2).astype(BF)
        rg_ref[...] = sec(3).astype(BF)
        fq_ref[...] = (sec(4) * FOX_SCALE).astype(BF)
        fk_ref[...] = sec(5).astype(BF)
        fv_ref[...] = sec(6).astype(BF)
        ff_ref[...] = _dot_nt(ub, win_ref[FF_COL:FF_COL + 128, :])
        for j in range(N_CHIPS):
            gs = _sigmoid(_dot(ub, wm_ref[j]) + bm_ref[:, j * cm:(j + 1) * cm]).astype(BF)
            col = j * cm
            if col < d:
                ga_ref[:, col:col + cm] = gs
            else:
                gb_ref[:, col - d:col - d + cm] = gs

    row = lambda c: pl.BlockSpec((tm, c), lambda i: (i, 0))
    full = lambda *s: pl.BlockSpec(s, lambda i: (0,) * len(s))
    sds = lambda c, dt: jax.ShapeDtypeStruct((t_tok, c), dt)
    return _pcall(
        body, [h, ln, w_in4, wm4, b_m, cos_t, sin_t], name="mix_in", grid=(t_tok // tm,),
        out_shape=[sds(d, BF)] + [sds(512, BF)] * 7 + [sds(128, F32), sds(d, BF), sds(d, BF),
                                                       jax.ShapeDtypeStruct((IN_PAD, d), BF)],
        in_specs=[row(d), full(1, d), full(*w_in4.shape), full(N_CHIPS, d, cm), full(1, 2 * d), row(128),
                  row(128)],
        out_specs=[row(d)] + [row(512)] * 7 + [row(128), row(d), row(d), full(IN_PAD, d)], comm=comm)


def _split3(x):
    hi = x.astype(BF)
    r1 = x - hi.astype(F32)
    mid = r1.astype(BF)
    lo = (r1 - mid.astype(F32)).astype(BF)
    return hi, mid, lo


def _aug_lane():
    return lax.broadcasted_iota(jnp.int32, (1, 128), 1) & (FOX_DIM - 1)


def _aug_put(base, k0, parts):
    w = _aug_lane()
    for i, part in enumerate(parts):
        base = jnp.where(w == k0 + i, part, base)
    return base


def _forget_fwd(ffl, b_pad):
    t_tok = ffl.shape[0]
    tb = _tile(t_tok, 256)

    def body(ff_ref, b_ref, aq_ref, ak_ref, cum_s):
        r = lax.broadcasted_iota(jnp.int32, (tb, tb), 0)
        c = lax.broadcasted_iota(jnp.int32, (tb, tb), 1)
        tri = jnp.where(c <= r, 1.0, 0.0).astype(BF)
        carry = jnp.zeros((1, 128), F32)
        for i in range(t_tok // tb):
            z = ff_ref[i * tb:(i + 1) * tb, :] + b_ref[...]
            lf = jnp.minimum(z, 0.0) - jnp.log(1.0 + jnp.exp(-jnp.abs(z)))
            hi, mid, lo = _split3(lf)
            cs = _dot(tri, hi) + _dot(tri, mid) + _dot(tri, lo) + carry
            cum_s[i * tb:(i + 1) * tb, :] = cs
            carry = cs[tb - 1:tb, :]
        x = cum_s[...]
        first = lax.broadcasted_iota(jnp.int32, (1, 128), 1) < FOX_DIM
        w = _aug_lane()
        one = jnp.ones((t_tok, 128), BF)
        zero = jnp.zeros((t_tok, 128), BF)
        for pp in range(FOX_HEADS // 2):
            other = jnp.where(first, x[:, 2 * pp + 1:2 * pp + 2], x[:, 2 * pp:2 * pp + 1])
            parts = _split3(other)
            aq = jnp.where((w >= 3) & (w < 6), one, zero)
            ak = jnp.where((w < 3) | ((w >= 6) & (w < 9)), one, zero)
            aq_ref[:, pp * 128:(pp + 1) * 128] = _aug_put(aq, 0, parts)
            ak_ref[:, pp * 128:(pp + 1) * 128] = _aug_put(ak, 3, [-q for q in parts])

    sds = jax.ShapeDtypeStruct((t_tok, FOX_WIDTH), BF)
    return _pcall(body, [ffl, b_pad], name="forget_fwd", out_shape=[sds, sds],
                  scratch=[pltpu.VMEM((t_tok, 128), F32)])


def _forget_bwd(dcum_t, dcum_q, ffl, b_pad):
    t_tok = ffl.shape[0]
    tb = _tile(t_tok, 256)

    def body(dc_ref, dq_ref, ff_ref, b_ref, dff_ref, db_ref, pad_s, d_s):
        pad_s[...] = jnp.zeros_like(pad_s)
        pad_s[0:FOX_HEADS, :] = dc_ref[...]
        dsum = pad_s[...].T
        lane = lax.broadcasted_iota(jnp.int32, (t_tok, 128), 1)
        for hh in range(FOX_HEADS):
            dsum = dsum + jnp.where(lane == hh, dq_ref[:, hh * FOX_DIM:hh * FOX_DIM + 1], 0.0)
        d_s[...] = dsum
        r = lax.broadcasted_iota(jnp.int32, (tb, tb), 0)
        c = lax.broadcasted_iota(jnp.int32, (tb, tb), 1)
        tri = jnp.where(c >= r, 1.0, 0.0).astype(BF)
        carry = jnp.zeros((1, 128), F32)
        db = jnp.zeros((1, 128), F32)
        for i in reversed(range(t_tok // tb)):
            hi, mid, lo = _split3(d_s[i * tb:(i + 1) * tb, :])
            dlf = _dot(tri, hi) + _dot(tri, mid) + _dot(tri, lo) + carry
            carry = dlf[0:1, :]
            z = ff_ref[i * tb:(i + 1) * tb, :] + b_ref[...]
            dff = dlf * _sigmoid(-z)
            dff_ref[i * tb:(i + 1) * tb, :] = dff.astype(BF)
            db = db + jnp.sum(dff, axis=0, keepdims=True)
        db_ref[...] = db

    return _pcall(
        body, [dcum_t, dcum_q, ffl, b_pad], name="forget_bwd",
        out_shape=[jax.ShapeDtypeStruct((t_tok, 128), BF), jax.ShapeDtypeStruct((1, 128), F32)],
        scratch=[pltpu.VMEM((128, t_tok), F32), pltpu.VMEM((t_tok, 128), F32)])


def _first_half():
    return lax.broadcasted_iota(jnp.int32, (1, 128), 1) < FOX_DIM


def _head_rows(x2, a2, hh):
    return jnp.where(_first_half(), x2, a2) if hh == 0 else jnp.where(_first_half(), a2, x2)


def _head_only(x2, hh):
    zero = jnp.zeros_like(x2)
    return jnp.where(_first_half(), x2, zero) if hh == 0 else jnp.where(_first_half(), zero, x2)


def _causal_diag(s):
    rows = lax.broadcasted_iota(jnp.int32, s.shape, 0)
    cols = lax.broadcasted_iota(jnp.int32, s.shape, 1)
    return jnp.where(cols <= rows, s, NEG)


def _diag_or_below(qi, ki, step):
    pl.when(ki < qi)(lambda: step(False))
    pl.when(ki == qi)(lambda: step(True))


def _tri_rows(s, n):
    qi = sum((s >= r * (r + 1) // 2).astype(jnp.int32) for r in range(1, n))
    return qi, s - (qi * (qi + 1)) // 2


def _tri_cols(s, n):
    ki = sum((s >= k * n - k * (k - 1) // 2).astype(jnp.int32) for k in range(1, n))
    return ki, ki + s - (ki * n - (ki * (ki - 1)) // 2)


def _fox_fwd(fq, fk, fv, aq, ak, comm=None):
    t_tok = fq.shape[0]
    t = _tile(t_tok, 512)
    nq = t_tok // t
    npair = FOX_HEADS // 2

    def body(q_ref, k_ref, v_ref, aq_ref, ak_ref, o_ref, of_ref, aqb_ref, m_s, l_s, acc_s):
        qi, ki = _tri_rows(pl.program_id(1), nq)

        @pl.when(ki == 0)
        def _():
            m_s[...] = jnp.full_like(m_s, NEG)
            l_s[...] = jnp.zeros_like(l_s)
            acc_s[...] = jnp.zeros_like(acc_s)

        def step(diag):
            q2, k2, v2, aq2, ak2 = q_ref[...], k_ref[...], v_ref[...], aq_ref[...], ak_ref[...]
            for hh in range(2):
                s = _dot_nt(_head_rows(q2, aq2, hh), _head_rows(k2, ak2, hh))
                if diag:
                    s = _causal_diag(s)
                m_prev = m_s[hh]
                m_new = jnp.maximum(m_prev, jnp.max(s, axis=1, keepdims=True))
                alpha = jnp.exp(m_prev - m_new)
                p = jnp.exp(s - jnp.tile(m_new, (1, t // 128)))
                l_s[hh] = alpha * l_s[hh] + jnp.sum(p, axis=1, keepdims=True)
                acc_s[hh] = alpha * acc_s[hh] + _dot(p.astype(BF), v2)
                m_s[hh] = m_new

        _diag_or_below(qi, ki, step)

        @pl.when(ki == qi)
        def _():
            first = _first_half()
            o = jnp.where(first, acc_s[0] / l_s[0], acc_s[1] / l_s[1])
            o_ref[...] = o.astype(BF)
            of_ref[...] = o
            other = jnp.where(first, m_s[1] + jnp.log(l_s[1]), m_s[0] + jnp.log(l_s[0]))
            aqb_ref[...] = _aug_put(aq_ref[...], 6, _split3(-other))

    qs = pl.BlockSpec((t, 128), lambda p, s: (_tri_rows(s, nq)[0], p))
    ks = pl.BlockSpec((t, 128), lambda p, s: (_tri_rows(s, nq)[1], p))
    stat = pltpu.VMEM((2, t, 128), F32)
    return _pcall(
        body, [fq, fk, fv, aq, ak], name="fox_fwd", grid=(npair, nq * (nq + 1) // 2),
        out_shape=[jax.ShapeDtypeStruct((t_tok, FOX_WIDTH), BF), jax.ShapeDtypeStruct((t_tok, FOX_WIDTH), F32),
                   jax.ShapeDtypeStruct((t_tok, FOX_WIDTH), BF)],
        in_specs=[qs, ks, ks, qs, ks], out_specs=[qs, qs, qs], scratch=[stat, stat, stat], comm=comm)


def _fox_ds(q2, k2, v2, do2, aq2, ak2, ad2, hh, diag):
    s = _dot_nt(_head_rows(q2, aq2, hh), _head_rows(k2, ak2, hh))
    if diag:
        s = _causal_diag(s)
    p = jnp.exp(s)
    av = jnp.where(_aug_lane() < 3, 1.0, 0.0).astype(BF)
    dp = _dot_nt(_head_rows(do2, ad2, hh), _head_rows(v2, jnp.broadcast_to(av, v2.shape), hh))
    return p, p * dp


def _fox_bwd(fq, fk, fv, do, aqb, ak, ad, comm=None):
    t_tok = fq.shape[0]
    t = _tile(t_tok, 512)
    nq = t_tok // t
    npair = FOX_HEADS // 2
    n_steps = nq * (nq + 1) // 2

    def body(q_ref, k_ref, v_ref, do_ref, aq_ref, ak_ref, ad_ref, dq_ref, dk_ref, dv_ref, dck_ref, dcq_ref,
             dk_s, dv_s, dq_s, rs_s):
        step_id = pl.program_id(1)
        ki, qi = _tri_cols(step_id, nq)

        @pl.when(step_id == 0)
        def _():
            dq_s[...] = jnp.zeros_like(dq_s)
            rs_s[...] = jnp.zeros_like(rs_s)

        @pl.when(qi == ki)
        def _():
            dk_s[...] = jnp.zeros_like(dk_s)
            dv_s[...] = jnp.zeros_like(dv_s)
            dck_ref[...] = jnp.zeros_like(dck_ref)

        rows = pl.ds(qi * t if isinstance(qi, int) else pl.multiple_of(qi * t, t), t)

        def step(diag):
            q2, k2, v2, do2 = q_ref[...], k_ref[...], v_ref[...], do_ref[...]
            dq = []
            for hh in range(2):
                p, ds = _fox_ds(q2, k2, v2, do2, aq_ref[...], ak_ref[...], ad_ref[...], hh, diag)
                dsb = ds.astype(BF)
                dv_s[...] += _dot_tn(p.astype(BF), _head_only(do2, hh))
                dk_s[...] += _dot_tn(dsb, _head_only(q2, hh))
                dq.append(_dot(dsb, k2))
                dck_ref[hh] = dck_ref[hh] - jnp.sum(ds, axis=0, keepdims=True)
                rs_s[hh, rows, :] = rs_s[hh, rows, :] + jnp.sum(ds, axis=1, keepdims=True)
            dq_s[rows, :] = dq_s[rows, :] + jnp.where(_first_half(), dq[0], dq[1])

        _diag_or_below(qi, ki, step)

        @pl.when(qi == nq - 1)
        def _():
            dk_ref[...] = dk_s[...].astype(BF)
            dv_ref[...] = dv_s[...].astype(BF)

        @pl.when(step_id == n_steps - 1)
        def _():
            dq_ref[...] = (dq_s[...] * FOX_SCALE).astype(BF)
            dcq_ref[...] = jnp.where(_first_half(), rs_s[0], rs_s[1])

    qs = pl.BlockSpec((t, 128), lambda p, s: (_tri_cols(s, nq)[1], p))
    ks = pl.BlockSpec((t, 128), lambda p, s: (_tri_cols(s, nq)[0], p))
    cks = pl.BlockSpec((2, 1, t), lambda p, s: (p, 0, _tri_cols(s, nq)[0]))
    seq = pl.BlockSpec((t_tok, 128), lambda p, s: (0, p))
    sds = jax.ShapeDtypeStruct((t_tok, FOX_WIDTH), BF)
    return _pcall(
        body, [fq, fk, fv, do, aqb, ak, ad], name="fox_bwd", grid=(npair, n_steps),
        out_shape=[sds, sds, sds, jax.ShapeDtypeStruct((FOX_HEADS, 1, t_tok), F32),
                   jax.ShapeDtypeStruct((t_tok, FOX_WIDTH), F32)],
        in_specs=[qs, ks, ks, qs, qs, ks, qs], out_specs=[seq, ks, ks, cks, seq],
        scratch=[pltpu.VMEM((t, 128), F32), pltpu.VMEM((t, 128), F32), pltpu.VMEM((t_tok, 128), F32),
                 pltpu.VMEM((2, t_tok, 128), F32)], comm=comm)


def _ret_consts():
    c = RET_CHUNK
    log_gamma = jnp.log1p(-jnp.exp2(-5.0 - jnp.arange(RET_HEADS, dtype=F32)))
    idx = jnp.arange(c, dtype=F32)
    diff = idx[:, None] - idx[None, :]
    dmask = jnp.where(diff >= 0, jnp.exp(log_gamma[:, None, None] * jnp.maximum(diff, 0.0)), 0.0)
    qdec = jnp.exp(log_gamma[:, None] * (idx + 1.0))
    kdec = jnp.exp(log_gamma[:, None] * (c - 1 - idx))
    cdec = jnp.exp(log_gamma * c)
    bc = lambda v: jnp.broadcast_to(v[:, :, None], (RET_HEADS, c, RET_DIM))
    return dmask, bc(qdec), bc(kdec), jnp.broadcast_to(cdec[:, None, None], (RET_HEADS, c, RET_DIM))


def _group_norm(y):
    mu = jnp.mean(y, axis=-1, keepdims=True)
    yc = y - mu
    r = lax.rsqrt(jnp.mean(yc * yc, axis=-1, keepdims=True) + EPS)
    return yc * r, r


def _ret_fwd(rq, rk, rv, rg, consts, comm=None):
    t_tok = rq.shape[0]
    nb = 4 if t_tok % (4 * RET_CHUNK) == 0 else 1
    tr = nb * RET_CHUNK
    n_steps = t_tok // tr
    c = RET_CHUNK

    def body(q_ref, k_ref, v_ref, g_ref, dm_ref, qd_ref, kd_ref, cd_ref, y_ref, yo_ref, st_ref, s_s):
        @pl.when(pl.program_id(0) == 0)
        def _():
            s_s[...] = jnp.zeros_like(s_s)

        for b in range(nb):
            rows = slice(b * c, (b + 1) * c)
            for hh in range(RET_HEADS):
                cols = slice(hh * RET_DIM, (hh + 1) * RET_DIM)
                q, k, v = q_ref[rows, cols], k_ref[rows, cols], v_ref[rows, cols]
                state = s_s[hh]
                st_ref[hh, b] = state
                sc = (_dot_nt(q, k) * dm_ref[hh]).astype(BF)
                y = _dot(sc, v) + _dot((q.astype(F32) * qd_ref[hh]).astype(BF), state.astype(BF))
                s_s[hh] = cd_ref[hh] * state + _dot_tn((k.astype(F32) * kd_ref[hh]).astype(BF), v)
                y_ref[rows, cols] = y
                yn, _ = _group_norm(y)
                gate = g_ref[rows, cols].astype(F32)
                yo_ref[rows, cols] = (yn * (gate * _sigmoid(gate))).astype(BF)

    blk = pl.BlockSpec((tr, RET_WIDTH), lambda i: (i, 0))
    cst = pl.BlockSpec((RET_HEADS, c, RET_DIM), lambda i: (0, 0, 0))
    return _pcall(
        body, [rq, rk, rv, rg, *consts], name="ret_fwd", grid=(n_steps,),
        out_shape=[jax.ShapeDtypeStruct((t_tok, RET_WIDTH), F32), jax.ShapeDtypeStruct((t_tok, RET_WIDTH), BF),
                   jax.ShapeDtypeStruct((RET_HEADS, t_tok // c, RET_DIM, RET_DIM), F32)],
        in_specs=[blk] * 4 + [cst] * 4,
        out_specs=[blk, blk, pl.BlockSpec((RET_HEADS, nb, RET_DIM, RET_DIM), lambda i: (0, i, 0, 0))],
        scratch=[pltpu.VMEM((RET_HEADS, RET_DIM, RET_DIM), F32)], comm=comm)


def _ret_bwd(rq, rk, rv, rg, y_raw, dyo, states, consts, cos_t, sin_t, comm=None):
    t_tok = rq.shape[0]
    nb = 4 if t_tok % (4 * RET_CHUNK) == 0 else 1
    tr = nb * RET_CHUNK
    n_steps = t_tok // tr
    c = RET_CHUNK

    def body(q_ref, k_ref, v_ref, g_ref, y_ref, dyo_ref, st_ref, dm_ref, qd_ref, kd_ref, cd_ref,
             cos_ref, sin_ref, dq_ref, dk_ref, dv_ref, dg_ref, ds_s):
        @pl.when(pl.program_id(0) == 0)
        def _():
            ds_s[...] = jnp.zeros_like(ds_s)

        for b in reversed(range(nb)):
            rows = slice(b * c, (b + 1) * c)
            cosv, sinv = cos_ref[rows, :], sin_ref[rows, :]
            for hh in range(RET_HEADS):
                cols = slice(hh * RET_DIM, (hh + 1) * RET_DIM)
                dm, qd, kd, cd = dm_ref[hh], qd_ref[hh], kd_ref[hh], cd_ref[hh]
                q, k, v = q_ref[rows, cols], k_ref[rows, cols], v_ref[rows, cols]
                yn, r = _group_norm(y_ref[rows, cols])
                gate = g_ref[rows, cols].astype(F32)
                sg = _sigmoid(gate)
                dyo = dyo_ref[rows, cols]
                dg_ref[rows, cols] = (dyo * yn * (sg * (1.0 + gate * (1.0 - sg)))).astype(BF)
                dyn = dyo * (gate * sg)
                dy = r * (dyn - jnp.mean(dyn, axis=-1, keepdims=True)
                          - yn * jnp.mean(dyn * yn, axis=-1, keepdims=True))
                dyb = dy.astype(BF)
                state_b = st_ref[hh, b].astype(BF)
                dstate = ds_s[hh]
                dstate_b = dstate.astype(BF)
                qdb = (q.astype(F32) * qd).astype(BF)
                kdb = (k.astype(F32) * kd).astype(BF)
                sc = (_dot_nt(q, k) * dm).astype(BF)
                dv = _dot_tn(sc, dyb) + _dot(kdb, dstate_b)
                dp = (_dot_nt(dyb, v) * dm).astype(BF)
                dq = _dot(dp, k) + _dot_nt(dyb, state_b) * qd
                dk = (_dot_tn(dp, q) + _dot_nt(v, dstate_b) * kd) * RET_SCALE
                ds_s[hh] = cd * dstate + _dot_tn(qdb, dyb)
                dv_ref[rows, cols] = dv.astype(BF)
                dq_ref[rows, cols] = (dq * cosv - _swap_pairs(dq) * sinv).astype(BF)
                dk_ref[rows, cols] = (dk * cosv - _swap_pairs(dk) * sinv).astype(BF)

    rev = lambda i: n_steps - 1 - i
    blk = pl.BlockSpec((tr, RET_WIDTH), lambda i: (rev(i), 0))
    tab = pl.BlockSpec((tr, RET_DIM), lambda i: (rev(i), 0))
    cst = pl.BlockSpec((RET_HEADS, c, RET_DIM), lambda i: (0, 0, 0))
    sds = jax.ShapeDtypeStruct((t_tok, RET_WIDTH), BF)
    return _pcall(
        body, [rq, rk, rv, rg, y_raw, dyo, states, *consts, cos_t, sin_t], name="ret_bwd",
        grid=(n_steps,), out_shape=[sds] * 4,
        in_specs=[blk] * 6 + [pl.BlockSpec((RET_HEADS, nb, RET_DIM, RET_DIM), lambda i: (0, rev(i), 0, 0))]
        + [cst] * 4 + [tab, tab],
        out_specs=[blk] * 4, scratch=[pltpu.VMEM((RET_HEADS, RET_DIM, RET_DIM), F32)], comm=comm)


def _mix_out(h, y_ret, y_fox, ga, gb, wr4, wf4, wo4, comm=None):
    t_tok, d = h.shape
    cz = wr4.shape[-1]
    ro = wo4.shape[-2]
    tm = _tile(t_tok, 512)

    def body(h_ref, yr_ref, yf_ref, ga_ref, gb_ref, wr_ref, wf_ref, wo_ref, ho_ref, za_ref, zb_ref, mix_ref):
        yr, yf = yr_ref[...], yf_ref[...]
        for j in range(N_CHIPS):
            sl = slice(j * cz, (j + 1) * cz)
            za = _dot(yr, wr_ref[j])
            zb = _dot(yf, wf_ref[j])
            za_ref[:, sl] = za.astype(BF)
            zb_ref[:, sl] = zb.astype(BF)
            mix_ref[:, sl] = (ga_ref[:, sl].astype(F32) * za + gb_ref[:, sl].astype(F32) * zb).astype(BF)
        acc = h_ref[...]
        for j in range(N_CHIPS):
            acc = acc + _dot(mix_ref[:, j * ro:(j + 1) * ro], wo_ref[j])
        ho_ref[...] = acc

    row = lambda c: pl.BlockSpec((tm, c), lambda i: (i, 0))
    full = lambda *s: pl.BlockSpec(s, lambda i: (0,) * len(s))
    sds = lambda dt: jax.ShapeDtypeStruct((t_tok, d), dt)
    return _pcall(
        body, [h, y_ret, y_fox, ga, gb, wr4, wf4, wo4], name="mix_out", grid=(t_tok // tm,),
        out_shape=[sds(F32), sds(BF), sds(BF), sds(BF)],
        in_specs=[row(d), row(RET_WIDTH), row(FOX_WIDTH), row(d), row(d),
                  full(N_CHIPS, RET_WIDTH, cz), full(N_CHIPS, FOX_WIDTH, cz), full(N_CHIPS, ro, d)],
        out_specs=[row(d)] * 4, comm=comm)


def _mix_out_bwd(dh, za, zb, ga, gb, y_fox, wr4, wf4, wo4, comm=None):
    t_tok, d = dh.shape
    cz = wr4.shape[-1]
    ro = wo4.shape[-2]
    tm = _tile(t_tok, 256)

    def body(dh_ref, za_ref, zb_ref, ga_ref, gb_ref, yf_ref, wr_ref, wf_ref, wo_ref,
             dhb_ref, dgp_ref, dza_ref, dzb_ref, dyr_ref, dyf_ref, dl_ref, db_ref):
        @pl.when(pl.program_id(0) == 0)
        def _():
            db_ref[...] = jnp.zeros_like(db_ref)

        dhb = dh_ref[...].astype(BF)
        dhb_ref[...] = dhb
        dyr = jnp.zeros((tm, RET_WIDTH), F32)
        dyf = jnp.zeros((tm, FOX_WIDTH), F32)
        for j in range(N_CHIPS):
            sl = slice(j * ro, (j + 1) * ro)
            dmix = _dot_nt(dhb, wo_ref[j])
            ga, gb = ga_ref[:, sl].astype(F32), gb_ref[:, sl].astype(F32)
            dza = (dmix * ga).astype(BF)
            dzb = (dmix * gb).astype(BF)
            dza_ref[:, sl] = dza
            dzb_ref[:, sl] = dzb
            dga = dmix * za_ref[:, sl].astype(F32) * ga * (1.0 - ga)
            dgb = dmix * zb_ref[:, sl].astype(F32) * gb * (1.0 - gb)
            dgp_ref[:, sl] = dga.astype(BF)
            dgp_ref[:, d + j * ro:d + (j + 1) * ro] = dgb.astype(BF)
            db_ref[:, sl] += jnp.sum(dga, axis=0, keepdims=True)
            db_ref[:, d + j * ro:d + (j + 1) * ro] += jnp.sum(dgb, axis=0, keepdims=True)
        for j in range(N_CHIPS):
            sl = slice(j * cz, (j + 1) * cz)
            dyr = dyr + _dot_nt(dza_ref[:, sl], wr_ref[j])
            dyf = dyf + _dot_nt(dzb_ref[:, sl], wf_ref[j])
        dyr_ref[...] = dyr
        dyfb = dyf.astype(BF)
        dyf_ref[...] = dyfb
        prod = dyfb.astype(F32) * yf_ref[...]
        first = _first_half()
        for pp in range(FOX_HEADS // 2):
            blk = prod[:, pp * 128:(pp + 1) * 128]
            s0 = jnp.sum(jnp.where(first, blk, 0.0), axis=1, keepdims=True)
            s1 = jnp.sum(jnp.where(first, 0.0, blk), axis=1, keepdims=True)
            parts = _split3(-jnp.where(first, s1, s0))
            dl_ref[:, pp * 128:(pp + 1) * 128] = _aug_put(jnp.zeros((tm, 128), BF), 0, parts)

    row = lambda c: pl.BlockSpec((tm, c), lambda i: (i, 0))
    full = lambda *s: pl.BlockSpec(s, lambda i: (0,) * len(s))
    sds = lambda c, dt: jax.ShapeDtypeStruct((t_tok, c), dt)
    return _pcall(
        body, [dh, za, zb, ga, gb, y_fox, wr4, wf4, wo4], name="mix_out_bwd", grid=(t_tok // tm,),
        out_shape=[sds(d, BF), sds(2 * d, BF), sds(d, BF), sds(d, BF), sds(RET_WIDTH, F32),
                   sds(FOX_WIDTH, BF), sds(FOX_WIDTH, BF), jax.ShapeDtypeStruct((1, 2 * d), F32)],
        in_specs=[row(d)] * 5 + [row(FOX_WIDTH), full(N_CHIPS, RET_WIDTH, cz), full(N_CHIPS, FOX_WIDTH, cz),
                                 full(N_CHIPS, ro, d)],
        out_specs=[row(d), row(2 * d), row(d), row(d), row(RET_WIDTH), row(FOX_WIDTH), row(FOX_WIDTH),
                   full(1, 2 * d)],
        comm=comm)


def _mix_in_bwd(dh, h, ln, parts, dff, dgpre, w_in, wm4, comm=None):
    t_tok, d = h.shape
    cm = wm4.shape[-1]
    tm = _tile(t_tok, 256)

    def body(dh_ref, h_ref, ln_ref, p0, p1, p2, p3, p4, p5, p6, dff_ref, dgp_ref, win_ref, wm_ref,
             dhi_ref, dln_ref, dproj_ref):
        @pl.when(pl.program_id(0) == 0)
        def _():
            dln_ref[...] = jnp.zeros_like(dln_ref)

        for k, pr in enumerate((p0, p1, p2, p3, p4, p5, p6)):
            dproj_ref[:, k * 512:(k + 1) * 512] = pr[...]
        dproj_ref[:, FF_COL:FF_COL + 128] = dff_ref[...]
        dproj_ref[:, FF_COL + 128:] = jnp.zeros((tm, IN_PAD - FF_COL - 128), BF)
        du = _dot(dproj_ref[...], win_ref[...])
        for j in range(N_CHIPS):
            du = du + _dot_nt(dgp_ref[:, j * cm:(j + 1) * cm], wm_ref[j])
        xv = h_ref[...]
        dx, dln = _rms_bwd(du, xv, _rstd(xv), ln_ref[...])
        dln_ref[...] += dln
        dhi_ref[...] = dh_ref[...] + dx

    row = lambda c: pl.BlockSpec((tm, c), lambda i: (i, 0))
    full = lambda *s: pl.BlockSpec(s, lambda i: (0,) * len(s))
    return _pcall(
        body, [dh, h, ln, *parts, dff, dgpre, w_in, wm4], name="mix_in_bwd", grid=(t_tok // tm,),
        out_shape=[jax.ShapeDtypeStruct((t_tok, d), F32), jax.ShapeDtypeStruct((1, d), F32),
                   jax.ShapeDtypeStruct((t_tok, IN_PAD), BF)],
        in_specs=[row(d), row(d), full(1, d)] + [row(512)] * 7 + [row(128), row(2 * d), full(IN_PAD, d),
                                                                   full(N_CHIPS, d, cm)],
        out_specs=[row(d), full(1, d), row(IN_PAD)], comm=comm)


def _tail(h, p, target, ln_ple, ln_fin, wpg4, wpl4, comm=None):
    t_tok, d = h.shape
    pd = p.shape[1]
    rg = wpg4.shape[-2]
    cp = wpl4.shape[-1]
    tm = _tile(t_tok, 256)

    def body(h_ref, p_ref, t_ref, lp_ref, lf_ref, wg_ref, wp_ref,
             dh_ref, n_ref, dgp_ref, dpe_ref, pb_ref, loss_ref, dlf_ref, dlp_ref, pe_s, dn_s):
        @pl.when(pl.program_id(0) == 0)
        def _():
            loss_ref[...] = jnp.zeros_like(loss_ref)
            dlf_ref[...] = jnp.zeros_like(dlf_ref)
            dlp_ref[...] = jnp.zeros_like(dlp_ref)

        xv = h_ref[...]
        r3 = _rstd(xv)
        nb = (xv * r3 * lp_ref[...]).astype(BF)
        n_ref[...] = nb
        pb = p_ref[...].astype(BF)
        pb_ref[...] = pb
        pgpre = jnp.zeros((tm, d), F32)
        for j in range(N_CHIPS):
            pgpre = pgpre + _dot(nb[:, j * rg:(j + 1) * rg], wg_ref[j])
            pe_s[:, j * cp:(j + 1) * cp] = _dot(pb, wp_ref[j])
        pg = _sigmoid(pgpre)
        pe = pe_s[...]
        h4 = xv + pg * pe
        r4 = _rstd(h4)
        err = h4 * r4 * lf_ref[...] - t_ref[...]
        loss_ref[...] += 0.5 * jnp.sum(jnp.sum(err * err, axis=1, keepdims=True), axis=0, keepdims=True) / d
        dh4, dlf = _rms_bwd(err * (1.0 / d), h4, r4, lf_ref[...])
        dlf_ref[...] += dlf
        dpe_ref[...] = (dh4 * pg).astype(BF)
        dgp = (dh4 * pe * pg * (1.0 - pg)).astype(BF)
        dgp_ref[...] = dgp
        for j in range(N_CHIPS):
            dn_s[:, j * rg:(j + 1) * rg] = _dot_nt(dgp, wg_ref[j])
        dx, dlp = _rms_bwd(dn_s[...], xv, r3, lp_ref[...])
        dlp_ref[...] += dlp
        dh_ref[...] = dh4 + dx

    row = lambda c: pl.BlockSpec((tm, c), lambda i: (i, 0))
    full = lambda *s: pl.BlockSpec(s, lambda i: (0,) * len(s))
    sds = lambda c, dt: jax.ShapeDtypeStruct((t_tok, c), dt)
    vec = jax.ShapeDtypeStruct((1, d), F32)
    return _pcall(
        body, [h, p, target, ln_ple, ln_fin, wpg4, wpl4], name="tail", grid=(t_tok // tm,),
        out_shape=[sds(d, F32), sds(d, BF), sds(d, BF), sds(d, BF), sds(pd, BF),
                   jax.ShapeDtypeStruct((1, 128), F32), vec, vec],
        in_specs=[row(d), row(pd), row(d), full(1, d), full(1, d), full(N_CHIPS, rg, d), full(N_CHIPS, pd, cp)],
        out_specs=[row(d), row(d), row(d), row(d), row(pd), full(1, 128), full(1, d), full(1, d)],
        scratch=[pltpu.VMEM((tm, d), F32), pltpu.VMEM((tm, d), F32)], comm=comm)


BIG = ["w_ffn1_gate", "w_ffn1_up", "w_ffn1_down", "w_in", "w_merge", "w_ret_out", "w_fox_out", "w_out",
       "w_ffn2_gate", "w_ffn2_up", "w_ffn2_down", "w_ple", "w_ple_gate"]
SMALL = ["ln_ffn1", "ln_mix", "b_forget", "b_merge", "ln_ffn2", "ln_ple", "ln_final"]
WEIGHTS = ["ln_ffn1", "w_ffn1_gate", "w_ffn1_up", "w_ffn1_down", "ln_mix", "w_in", "b_forget", "w_merge", "b_merge",
           "w_ret_out", "w_fox_out", "w_out", "ln_ffn2", "w_ffn2_gate", "w_ffn2_up", "w_ffn2_down", "ln_ple",
           "w_ple", "w_ple_gate", "ln_final"]


TRANSPOSED = {"w_ffn1_gate", "w_ffn1_up", "w_ffn2_gate", "w_ffn2_up", "w_in"}
IN_ROWS_PAD = -(-(IN_COLS // N_CHIPS) // 32) * 32


def _pack_small(vals, loss_row):
    rows = [loss_row]
    for name in SMALL:
        v = vals[name].reshape(-1)
        n = -(-v.shape[0] // 128) * 128
        rows.append(jnp.pad(v, (0, n - v.shape[0])).reshape(n // 128, 128))
    packed = jnp.concatenate(rows, axis=0)
    pad = -packed.shape[0] % 8
    return jnp.pad(packed, ((0, pad), (0, 0)))


def _unpack_small(packed, sizes):
    out, r = {}, 1
    for name in SMALL:
        n = sizes[name]
        nr = -(-n // 128)
        out[name] = packed[r:r + nr].reshape(1, nr * 128)[:, :n]
        r += nr
    return out


class _Stage:
    def __init__(self, comm, finish):
        self.comm, self.finish, self.result = comm, finish, None


def _hosted(fn, *a, stages=()):
    if not stages:
        return fn(*a)
    outs, couts = fn(*a, comm=_merge([st.comm for st in stages]))
    for st, o in zip(stages, _split_outs([st.comm for st in stages], couts)):
        st.result = st.finish(o)
    return outs


class _Reducer:
    def __init__(self):
        self.done = {}

    def swap(self, grads):
        names = list(grads)
        return _Stage(_c_half_swap([grads[n] for n in names]),
                      lambda outs: dict(zip(names, _add_halves([(grads[n], o) for n, o in zip(names, outs)]))))

    def exchange(self, parts):
        names = list(parts)
        return _Stage(_c_chip_exchange([parts[n] for n in names]),
                      lambda outs: dict(zip(names, _sum_chips([(parts[n], o) for n, o in zip(names, outs)]))))

    def join(self, halves):
        names = list(halves)
        return _Stage(_c_join([halves[n] for n in names]),
                      lambda outs: self.done.update({n: (halves[n], o) for n, o in zip(names, outs)}))


def kernel(x, p, positions, ln_ffn1, w_ffn1_gate, w_ffn1_up, w_ffn1_down, ln_mix, w_in, b_forget, w_merge, b_merge, w_ret_out, w_fox_out, w_out, ln_ffn2, w_ffn2_gate, w_ffn2_up, w_ffn2_down, ln_ple, w_ple, w_ple_gate, ln_final, loss_target, m_ln_ffn1, m_w_ffn1_gate, m_w_ffn1_up, m_w_ffn1_down, m_ln_mix, m_w_in, m_b_forget, m_w_merge, m_b_merge, m_w_ret_out, m_w_fox_out, m_w_out, m_ln_ffn2, m_w_ffn2_gate, m_w_ffn2_up, m_w_ffn2_down, m_ln_ple, m_w_ple, m_w_ple_gate, m_ln_final, v_ln_ffn1, v_w_ffn1_gate, v_w_ffn1_up, v_w_ffn1_down, v_ln_mix, v_w_in, v_b_forget, v_w_merge, v_b_merge, v_w_ret_out, v_w_fox_out, v_w_out, v_ln_ffn2, v_w_ffn2_gate, v_w_ffn2_up, v_w_ffn2_down, v_ln_ple, v_w_ple, v_w_ple_gate, v_ln_final):
    args = dict(locals())
    w = {n: args[n] for n in WEIGHTS}
    m = {n: args["m_" + n] for n in WEIGHTS}
    v = {n: args["v_" + n] for n in WEIGHTS}
    d = x.shape[-1]
    t_tok = x.shape[1]
    xs, ps, target = x[0], p[0, 0], loss_target[0]
    small = {n: w[n].reshape(1, -1) for n in SMALL}

    def to2d(n, a):
        if n in TRANSPOSED:
            return a[0].T
        return a.reshape(a.shape[-2], a.shape[-1]) if a.ndim == 3 else a.reshape(1, -1)

    def from2d(n, a):
        return a.T[None] if n in TRANSPOSED else a.reshape(w[n].shape)

    def padded(n, a):
        return jnp.pad(a, ((0, IN_ROWS_PAD - a.shape[0]), (0, 0))) if n == "w_in" else a

    core = lax.axis_index("c")
    me = 2 * lax.axis_index("x") + lax.axis_index("y")
    shard = {}

    def set_shard(n, s2):
        s2 = padded(n, s2)
        shard[n] = s2.reshape(1, 2, s2.shape[0] // 2, s2.shape[1])

    first = ["w_ffn1_gate", "w_ffn1_up", "w_ffn1_down"]
    for n in first + ["w_in"]:
        set_shard(n, to2d(n, w[n]).astype(BF))
    full = {}

    def gather(names):
        bufs = [lax.dynamic_update_slice(jnp.zeros((N_CHIPS,) + shard[n].shape[1:], BF), shard[n], (me, 0, 0, 0))
                for n in names]

        def finish(outs):
            full.update({n: o.reshape(N_CHIPS, 2 * o.shape[2], o.shape[3]) for n, o in zip(names, outs)})

        return _Stage(_c_all_gather(bufs), finish)

    half = RET_DIM // 2
    inv_freq = 1.0 / (ROPE_BASE ** (jnp.arange(half, dtype=F32) / half))
    later = [n for n in BIG if n not in shard]
    cos_t, sin_t, *cast = _hosted(_rope_tables, positions[0].astype(F32).reshape(t_tok, 1),
                                  jnp.repeat(inv_freq, 2).reshape(1, RET_DIM), [to2d(n, w[n]) for n in later],
                                  stages=[gather(first)])
    for n, s2 in zip(later, cast):
        set_shard(n, s2)
    consts = _ret_consts()
    b_pad = jnp.pad(small["b_forget"], ((0, 0), (0, 128 - FOX_HEADS)))

    h1, n1, g1, u1 = _hosted(
        _ffn_fwd, xs, small["ln_ffn1"], full["w_ffn1_gate"], full["w_ffn1_up"], full["w_ffn1_down"],
        stages=[gather(["w_in", "w_merge", "w_ret_out", "w_fox_out", "w_out", "w_ple_gate", "w_ple"])])
    u, rq, rk, rv, rg, fq, fk, fv, ffl, ga, gb, w_in_full = _mix_in(
        h1, small["ln_mix"], full["w_in"], full["w_merge"], small["b_merge"], cos_t, sin_t)
    aq, ak = _forget_fwd(ffl, b_pad)
    y_raw, y_ret, states = _ret_fwd(rq, rk, rv, rg, consts)
    y_fox, y_fox32, aqb = _hosted(_fox_fwd, fq, fk, fv, aq, ak,
                                  stages=[gather(["w_ffn2_gate", "w_ffn2_up", "w_ffn2_down"])])
    h2, za, zb, mix = _mix_out(h1, y_ret, y_fox, ga, gb, full["w_ret_out"], full["w_fox_out"], full["w_out"])
    h3, n2, g2, u2 = _ffn_fwd(h2, small["ln_ffn2"], full["w_ffn2_gate"], full["w_ffn2_up"], full["w_ffn2_down"])

    red = _Reducer()
    dh3, n3, dpgpre, dpe, pb, loss, dln_final, dln_ple = _tail(
        h3, ps, target, small["ln_ple"], small["ln_final"], full["w_ple_gate"], full["w_ple"])
    g_f2 = dict(w_ple_gate=_wgrad_rows("wgrad_ple_gate", n3, dpgpre, N_CHIPS),
                w_ple=_wgrad_cols("wgrad_ple", pb, dpe, N_CHIPS))
    dh2, dln_ffn2, dg2, du2, a2, dhb3 = _ffn_bwd(
        dh3, h2, small["ln_ffn2"], g2, u2, full["w_ffn2_gate"], full["w_ffn2_up"], full["w_ffn2_down"])
    g_f2["w_ffn2_gate"] = _wgrad_b_shared("wgrad_ffn2_gate", dg2, n2)
    g_f2["w_ffn2_up"] = _wgrad_b_shared("wgrad_ffn2_up", du2, n2)
    g_f2["w_ffn2_down"] = _wgrad_b_shared("wgrad_ffn2_down", a2, dhb3)

    sw_f2 = red.swap(g_f2)
    dhb2, dgpre, dza, dzb, dy_ret, dy_fox, ad, db_merge = _hosted(
        _mix_out_bwd, dh2, za, zb, ga, gb, y_fox32, full["w_ret_out"], full["w_fox_out"], full["w_out"],
        stages=[sw_f2])
    g_br = dict(w_out=_wgrad_rows("wgrad_out", mix, dhb2, N_CHIPS),
                w_ret_out=_wgrad_cols("wgrad_ret_out", y_ret, dza, N_CHIPS),
                w_fox_out=_wgrad_cols("wgrad_fox_out", y_fox, dzb, N_CHIPS))

    sw_br = red.swap(g_br)
    drq, drk, drv, drg = _hosted(_ret_bwd, rq, rk, rv, rg, y_raw, dy_ret, states, consts, cos_t, sin_t,
                                 stages=[sw_br])
    ex_f2, ex_br = red.exchange(sw_f2.result), red.exchange(sw_br.result)
    dfq, dfk, dfv, dcum_t3, dcum_q = _hosted(_fox_bwd, fq, fk, fv, dy_fox, aqb, ak, ad, stages=[ex_f2, ex_br])
    dff, db_forget = _forget_bwd(dcum_t3.reshape(FOX_HEADS, t_tok), dcum_q, ffl, b_pad)
    dh1, dln_mix, dproj = _hosted(
        _mix_in_bwd, dh2, h1, small["ln_mix"], (drq, drk, drv, drg, dfq, dfk, dfv), dff, dgpre, w_in_full,
        full["w_merge"], stages=[red.join(ex_f2.result), red.join(ex_br.result)])

    results = {}

    def sc_update(names):
        res = _sc_adamw_halves([(to2d(n, w[n]), *red.done[n], to2d(n, m[n]), to2d(n, v[n])) for n in names])
        for q, n in enumerate(names):
            results[n] = tuple(from2d(n, a) for a in res[4 * q:4 * q + 4])

    for names in (["w_ffn2_gate", "w_ffn2_up", "w_ffn2_down"], ["w_out", "w_ple_gate"], ["w_ret_out", "w_fox_out"],
                  ["w_ple"]):
        sc_update(names)

    dx, dln_ffn1, dg1, du1, a1, dhb1 = _ffn_bwd(
        dh1, xs, small["ln_ffn1"], g1, u1, full["w_ffn1_gate"], full["w_ffn1_up"], full["w_ffn1_down"])
    g_f1g = _wgrad_b_shared("wgrad_ffn1_gate", dg1, n1)
    sw_f1g = red.swap(dict(w_ffn1_gate=g_f1g))
    g_f1u = _hosted(_wgrad_b_shared, "wgrad_ffn1_up", du1, n1, stages=[sw_f1g])
    ex_f1g, sw_f1u = red.exchange(sw_f1g.result), red.swap(dict(w_ffn1_up=g_f1u))
    g_f1d = _hosted(_wgrad_b_shared, "wgrad_ffn1_down", a1, dhb1, stages=[ex_f1g, sw_f1u])

    ex_f1u, sw_f1d = red.exchange(sw_f1u.result), red.swap(dict(w_ffn1_down=g_f1d))
    g_in = _hosted(_wgrad_in, dproj, u, stages=[ex_f1u, sw_f1d, red.join(ex_f1g.result)])
    sc_update(["w_ffn1_gate"])
    ex_f1d, sw_in = red.exchange(sw_f1d.result), red.swap(dict(w_in=g_in))
    g_mrg = _hosted(_wgrad_cols, "wgrad_merge", u, dgpre, N_CHIPS,
                    stages=[ex_f1d, sw_in, red.join(ex_f1u.result)])
    sc_update(["w_ffn1_up"])

    small_grads = dict(ln_ffn1=dln_ffn1, ln_mix=dln_mix, b_forget=db_forget[:, :FOX_HEADS], b_merge=db_merge,
                       ln_ffn2=dln_ffn2, ln_ple=dln_ple, ln_final=dln_final)
    sizes = {n: w[n].size for n in SMALL}
    ex_in, sw_mrg = red.exchange(sw_in.result), red.swap(dict(w_merge=g_mrg))
    reduced = _hosted(_all_reduce_small, _pack_small(small_grads, loss),
                      stages=[ex_in, sw_mrg, red.join(ex_f1d.result)])
    gsum = _unpack_small(reduced, sizes)
    loss = reduced[0, 0]
    ex_mrg = red.exchange(sw_mrg.result)
    _hosted(_exchange_only, stages=[ex_mrg, red.join(ex_in.result)])
    _hosted(_exchange_only, stages=[red.join(ex_mrg.result)])

    def update(names):
        w2, m2, v2 = ([to2d(n, a[n]) for n in names] for a in (w, m, v))
        n = names[0]
        if n == "w_in":
            mine, other = red.done[n]
            g2 = jnp.where(core == 0, jnp.concatenate([mine, other]), jnp.concatenate([other, mine]))
            g2 = g2[:w2[0].shape[0]]
            rows3 = lambda a: jnp.transpose(a, (2, 0, 1))
            g3 = g2.reshape(g2.shape[0], 1, g2.shape[1])
            res = [g3] + _adamw(rows3(w[n]), g3, rows3(m[n]), rows3(v[n]))
            results[n] = tuple(jnp.transpose(a, (1, 2, 0)) for a in res)
            return
        res = _adamw_halves([(w2[q], *red.done[names[q]], m2[q], v2[q]) for q in range(len(names))])
        for q, name in enumerate(names):
            results[name] = tuple(from2d(name, a) for a in res[4 * q:4 * q + 4])

    res = _adamw_vectors([(to2d(n, w[n]), gsum[n], to2d(n, m[n]), to2d(n, v[n])) for n in SMALL])
    for q, n in enumerate(SMALL):
        results[n] = tuple(from2d(n, a) for a in [gsum[n]] + res[3 * q:3 * q + 3])
    for n in WEIGHTS:
        if n not in results:
            update([n])

    outs = [[results[n][k] for n in WEIGHTS] for k in range(4)]
    return (loss, dx[None], *outs[0], *outs[1], *outs[2], *outs[3])
```

```python
import functools
import operator

import jax
import jax.numpy as jnp
from jax import lax
from jax.experimental import pallas as pl
from jax.experimental.pallas import tpu as pltpu
from jax.experimental.pallas import tpu_sc as plsc

F32 = jnp.float32
BF = jnp.bfloat16
MESH = pl.DeviceIdType.MESH

EPS = 1e-6
ROPE_BASE = 10000.0
N_CHIPS = 4
RET_HEADS = 4
RET_DIM = 128
RET_WIDTH = RET_HEADS * RET_DIM
RET_CHUNK = 128
RET_SCALE = RET_DIM ** -0.5
FOX_HEADS = 8
FOX_DIM = 64
FOX_WIDTH = FOX_HEADS * FOX_DIM
FOX_SCALE = FOX_DIM ** -0.5
IN_COLS = 4 * RET_WIDTH + 3 * FOX_WIDTH + FOX_HEADS
IN_PAD = 4096
FF_COL = 4 * RET_WIDTH + 3 * FOX_WIDTH
NEG = -1e30

ADAM_LR = 0.001
ADAM_B1 = 0.9
ADAM_B2 = 0.999
ADAM_EPS = 1e-08
ADAM_WD = 0.01
ADAM_STEP = 10

VMEM_LIMIT = 52 * 1024 * 1024

RELAY_MIN_STEPS = 16

NT = (((1,), (1,)), ((), ()))
TN = (((0,), (0,)), ((), ()))

HBM_SPEC = pl.BlockSpec(memory_space=pltpu.HBM)
VMEM_SPEC = pl.BlockSpec(memory_space=pltpu.VMEM)


def _dot(a, b):
    return jnp.dot(a, b, preferred_element_type=F32)


def _dot_nt(a, b):
    return lax.dot_general(a, b, NT, preferred_element_type=F32)


def _dot_tn(a, b):
    return lax.dot_general(a, b, TN, preferred_element_type=F32)


def _rstd(xv):
    return lax.rsqrt(jnp.mean(xv * xv, axis=-1, keepdims=True) + EPS)


def _rms_bwd(dn, xv, r, ln):
    xh = xv * r
    dxh = dn * ln
    dx = r * (dxh - xh * jnp.mean(dxh * xh, axis=-1, keepdims=True))
    return dx, jnp.sum(dn * xh, axis=0, keepdims=True)


def _sigmoid(x):
    return jax.nn.sigmoid(x)


def _tile(n, pref):
    return pref if n % pref == 0 else n


def _row_tile(n, cap):
    best = [t for t in range(16, min(n, cap) + 1, 16) if n % t == 0]
    return best[-1] if best else n


class _Comm:
    def __init__(self, ins, out_shapes, sems, start, wait, aliases=None, relay=None):
        self.ins, self.out_shapes, self.sems, self.start, self.wait = list(ins), list(out_shapes), list(sems), start, wait
        self.aliases = dict(aliases or {})
        self.relay = relay


def _merge(comms):
    comms = [c for c in comms if c is not None]
    if not comms:
        return None
    bounds, ni, no, ns = [], 0, 0, 0
    for c in comms:
        bounds.append((ni, no, ns))
        ni, no, ns = ni + len(c.ins), no + len(c.out_shapes), ns + len(c.sems)

    def run(which):
        def f(ins, outs, sems, **kw):
            for c, (i, o, s) in zip(comms, bounds):
                fn = getattr(c, which)
                if fn is not None:
                    fn(ins[i:i + len(c.ins)], outs[o:o + len(c.out_shapes)], sems[s:s + len(c.sems)],
                       **(kw if c.relay is not None else {}))
        return f

    aliases = {i + a: o + b for c, (i, o, _) in zip(comms, bounds) for a, b in c.aliases.items()}
    relay = run("relay") if any(c.relay is not None for c in comms) else None
    return _Comm([a for c in comms for a in c.ins], [a for c in comms for a in c.out_shapes],
                 [a for c in comms for a in c.sems], run("start"), run("wait"), aliases, relay)


def _split_outs(comms, outs):
    res, o = [], 0
    for c in comms:
        if c is not None:
            res.append(list(outs[o:o + len(c.out_shapes)]))
            o += len(c.out_shapes)
    return res


def _pcall(body, args, *, name, out_shape, grid=(), in_specs=None, out_specs=None, scratch=(), comm=None,
           prefetch=()):
    many = isinstance(out_shape, (list, tuple))
    outs = list(out_shape) if many else [out_shape]
    n_pre, n_in, n_out, n_scr = len(prefetch), len(args), len(outs), len(scratch)
    if in_specs is None:
        in_specs, out_specs = [VMEM_SPEC] * n_in, [VMEM_SPEC] * n_out
    else:
        in_specs, out_specs = list(in_specs), (list(out_specs) if many else [out_specs])
    params = pltpu.CompilerParams(dimension_semantics=("arbitrary",) * len(grid), vmem_limit_bytes=VMEM_LIMIT)
    scalars = [jnp.reshape(s, (1,)).astype(jnp.int32) for s in prefetch]
    ci, co = (len(comm.ins), len(comm.out_shapes)) if comm is not None else (0, 0)

    def wrapped(*refs):
        pre, refs = refs[:n_pre], refs[n_pre:]
        a, ca = refs[:n_in], refs[n_in:n_in + ci]
        o = refs[n_in + ci:n_in + ci + n_out]
        cout = refs[n_in + ci + n_out:n_in + ci + n_out + co]
        s = refs[n_in + ci + n_out + co:n_in + ci + n_out + co + n_scr]
        csem = refs[n_in + ci + n_out + co + n_scr:]
        if comm is None:
            body(*pre, *a, *o, *s)
        elif grid:
            step = functools.reduce(lambda acc, k: acc * grid[k] + pl.program_id(k), range(len(grid)), 0)
            n_steps = functools.reduce(operator.mul, grid)
            relayed = comm.relay is not None and n_steps >= RELAY_MIN_STEPS
            pl.when(step == 0)(lambda: comm.start(ca, cout, csem))
            if relayed:
                pl.when(step == n_steps - n_steps // 8)(lambda: comm.relay(ca, cout, csem))
            body(*pre, *a, *o, *s)
            pl.when(step == n_steps - 1)(lambda: comm.wait(ca, cout, csem, **({"relayed": True} if relayed else {})))
        else:
            comm.start(ca, cout, csem)
            body(*pre, *a, *o, *s)
            comm.wait(ca, cout, csem)

    c_ins, c_outs, c_sems, aliases = ([], [], [], {}) if comm is None else (
        comm.ins, comm.out_shapes, comm.sems, {n_pre + n_in + i: n_out + o for i, o in comm.aliases.items()})
    all_in, all_out = in_specs + [HBM_SPEC] * ci, out_specs + [HBM_SPEC] * co
    all_scr = list(scratch) + c_sems
    if grid:
        args = [pltpu.with_memory_space_constraint(a, pltpu.HBM) for a in args]
    c_ins = [pltpu.with_memory_space_constraint(a, pltpu.HBM) for a in c_ins]
    if n_pre:
        spec = dict(grid_spec=pltpu.PrefetchScalarGridSpec(
            num_scalar_prefetch=n_pre, grid=grid, in_specs=all_in, out_specs=all_out, scratch_shapes=all_scr))
    else:
        spec = dict(grid=grid, in_specs=all_in, out_specs=all_out, scratch_shapes=all_scr)
    res = pl.pallas_call(wrapped, name=name, out_shape=outs + c_outs, input_output_aliases=aliases,
                         compiler_params=params, **spec)(*scalars, *args, *c_ins)
    mine = list(res[:n_out])
    mine = mine if many else mine[0]
    return mine if comm is None else (mine, list(res[n_out:]))


def _peer_chips(x, y):
    return [(1 - x, y), (x, 1 - y), (1 - x, 1 - y)]


def _c_all_gather(bufs):
    n = len(bufs)

    def copies(ins, outs, sems):
        send_sems, recv_sems, fwd_send, fwd_recv = sems
        x, y, c = lax.axis_index("x"), lax.axis_index("y"), lax.axis_index("c")
        me = 2 * x + y
        peers = _peer_chips(x, y)
        chip = [2 * px + py for px, py in peers]

        def ici(g, j, slot):
            return pltpu.make_async_remote_copy(
                src_ref=outs[g].at[me, c], dst_ref=outs[g].at[slot, c], send_sem=send_sems.at[g, j],
                recv_sem=recv_sems.at[g, j], device_id=(*peers[j], c), device_id_type=MESH)

        def d2d(g, j, half):
            return pltpu.make_async_remote_copy(
                src_ref=outs[g].at[chip[j], half], dst_ref=outs[g].at[chip[j], half], send_sem=fwd_send.at[g, j],
                recv_sem=fwd_recv.at[g, j], device_id=(x, y, 1 - c), device_id_type=MESH)

        pairs = [(g, j) for g in range(n) for j in range(3)]
        sends = [ici(g, j, me) for g, j in pairs]
        recvs = [ici(g, j, chip[j]) for g, j in pairs]
        passes = [d2d(g, j, c) for g, j in pairs]
        passed = [d2d(g, j, 1 - c) for g, j in pairs]
        return sends, recvs, passes, passed

    def start(ins, outs, sems):
        for cp in copies(ins, outs, sems)[0]:
            cp.start()

    def relay(ins, outs, sems):
        _, recvs, passes, _ = copies(ins, outs, sems)
        for rcv, fwd in zip(recvs, passes):
            rcv.wait_recv()
            fwd.start()

    def wait(ins, outs, sems, relayed=False):
        if not relayed:
            relay(ins, outs, sems)
        sends, _, passes, passed = copies(ins, outs, sems)
        for cp in passed:
            cp.wait_recv()
        for cp in sends + passes:
            cp.wait_send()

    pair_sems = pltpu.SemaphoreType.DMA((n, 3))
    return _Comm(bufs, [jax.ShapeDtypeStruct(s.shape, s.dtype) for s in bufs], [pair_sems] * 4, start, wait,
                 aliases={g: g for g in range(n)}, relay=relay)


def _start_wait(copies):
    def start(ins, outs, sems):
        local, sends, _ = copies(ins, outs, sems)
        for cp in local + sends:
            cp.start()

    def wait(ins, outs, sems):
        local, sends, recvs = copies(ins, outs, sems)
        for cp in recvs:
            cp.wait_recv()
        for cp in sends:
            cp.wait_send()
        for cp in local:
            cp.wait()

    return start, wait


def _c_half_swap(grads):
    n = len(grads)

    def copies(ins, outs, sems):
        send_sems, recv_sems = sems
        x, y, c = lax.axis_index("x"), lax.axis_index("y"), lax.axis_index("c")
        sends = []
        for g in range(n):
            half = ins[g].shape[1] // 2
            sends.append(pltpu.make_async_remote_copy(
                src_ref=ins[g].at[:, pl.ds((1 - c) * half, half), :], dst_ref=outs[g],
                send_sem=send_sems.at[g], recv_sem=recv_sems.at[g], device_id=(x, y, 1 - c), device_id_type=MESH))
        return [], sends, sends

    return _Comm(
        grads, [jax.ShapeDtypeStruct((N_CHIPS, s.shape[1] // 2, s.shape[2]), s.dtype) for s in grads],
        [pltpu.SemaphoreType.DMA((n,)), pltpu.SemaphoreType.DMA((n,))], *_start_wait(copies))


def _c_chip_exchange(parts):
    n = len(parts)

    def copies(ins, outs, sems):
        send_sems, recv_sems = sems
        x, y, c = lax.axis_index("x"), lax.axis_index("y"), lax.axis_index("c")
        peers = _peer_chips(x, y)

        def remote(g, j):
            return pltpu.make_async_remote_copy(
                src_ref=ins[g].at[2 * peers[j][0] + peers[j][1]], dst_ref=outs[g].at[j],
                send_sem=send_sems.at[g, j], recv_sem=recv_sems.at[g, j], device_id=(*peers[j], c),
                device_id_type=MESH)

        sends = [remote(g, j) for g in range(n) for j in range(3)]
        return [], sends, sends

    return _Comm(
        parts, [jax.ShapeDtypeStruct((3,) + s.shape[1:], s.dtype) for s in parts],
        [pltpu.SemaphoreType.DMA((n, 3)), pltpu.SemaphoreType.DMA((n, 3))], *_start_wait(copies))


def _c_join(halves):
    n = len(halves)

    def copies(ins, outs, sems):
        send_sems, recv_sems = sems
        x, y, c = lax.axis_index("x"), lax.axis_index("y"), lax.axis_index("c")
        sends = [pltpu.make_async_remote_copy(
            src_ref=ins[g], dst_ref=outs[g], send_sem=send_sems.at[g], recv_sem=recv_sems.at[g],
            device_id=(x, y, 1 - c), device_id_type=MESH) for g in range(n)]
        return [], sends, sends

    return _Comm(
        halves, [jax.ShapeDtypeStruct(s.shape, s.dtype) for s in halves],
        [pltpu.SemaphoreType.DMA((n,)), pltpu.SemaphoreType.DMA((n,))], *_start_wait(copies))


def _exchange_only(comm=None):
    def body(o_ref):
        o_ref[...] = jnp.zeros_like(o_ref)

    return _pcall(body, [], name="exchange_only", out_shape=jax.ShapeDtypeStruct((8, 128), F32), comm=comm)


def _all_reduce_small(v, comm=None):
    rows = v.shape[0]

    def body(v_ref, out_ref, buf, send_sems, recv_sems):
        x, y, c = lax.axis_index("x"), lax.axis_index("y"), lax.axis_index("c")
        me = 4 * x + 2 * y + c
        buf[me] = v_ref[...]
        flips = [(fx, fy, fc) for fx in (0, 1) for fy in (0, 1) for fc in (0, 1)][1:]

        def peer(k):
            fx, fy, fc = flips[k]
            px, py, pc = x ^ fx, y ^ fy, c ^ fc
            return (px, py, pc), 4 * px + 2 * py + pc

        def copy(k, slot):
            return pltpu.make_async_remote_copy(
                src_ref=buf.at[slot], dst_ref=buf.at[slot], send_sem=send_sems.at[k],
                recv_sem=recv_sems.at[k], device_id=peer(k)[0], device_id_type=MESH)

        sends = [copy(k, me) for k in range(7)]
        for cp in sends:
            cp.start()
        for k in range(7):
            copy(k, peer(k)[1]).wait_recv()
        for cp in sends:
            cp.wait_send()
        acc = buf[0]
        for d in range(1, 8):
            acc = acc + buf[d]
        out_ref[...] = acc

    return _pcall(body, [v], name="all_reduce_small", out_shape=jax.ShapeDtypeStruct((rows, 128), F32),
                  scratch=[pltpu.VMEM((8, rows, 128), F32), pltpu.SemaphoreType.DMA((7,)),
                           pltpu.SemaphoreType.DMA((7,))], comm=comm)


def _add_halves(pairs):
    k = len(pairs)

    def body(h_ref, *refs):
        for a_ref, b_ref, o_ref in zip(refs[0:2 * k:2], refs[1:2 * k:2], refs[2 * k:]):
            o_ref[...] = (a_ref[...].astype(F32) + b_ref[...].astype(F32)).astype(o_ref.dtype)

    in_specs, out_specs = [], []
    for _, got in pairs:
        _, h, c = got.shape
        spec = pl.BlockSpec((1, h, c), lambda j, h_ref: (j, 0, 0))
        in_specs += [pl.BlockSpec((1, h, c), lambda j, h_ref: (j, h_ref[0], 0)), spec]
        out_specs.append(spec)
    return _pcall(body, [a for pair in pairs for a in pair], name="add_halves", grid=(N_CHIPS,),
                  prefetch=[lax.axis_index("c")], in_specs=in_specs, out_specs=out_specs,
                  out_shape=[jax.ShapeDtypeStruct(got.shape, BF) for _, got in pairs])


def _sum_chips(pairs):
    k = len(pairs)
    n_steps = 2 if all(parts.shape[1] % 32 == 0 for parts, _ in pairs) else 1
    me = 2 * lax.axis_index("x") + lax.axis_index("y")

    def body(me_ref, *refs):
        for p_ref, r_ref, o_ref in zip(refs[0:2 * k:2], refs[1:2 * k:2], refs[2 * k:]):
            acc = p_ref[0].astype(F32)
            for s in range(N_CHIPS - 1):
                acc = acc + r_ref[s].astype(F32)
            o_ref[...] = acc

    in_specs, out_specs = [], []
    for parts, _ in pairs:
        _, h, c = parts.shape
        th = h // n_steps
        in_specs += [pl.BlockSpec((1, th, c), lambda i, me_ref: (me_ref[0], i, 0)),
                     pl.BlockSpec((N_CHIPS - 1, th, c), lambda i, me_ref: (0, i, 0))]
        out_specs.append(pl.BlockSpec((th, c), lambda i, me_ref: (i, 0)))
    return _pcall(body, [a for pair in pairs for a in pair], name="sum_chips", grid=(n_steps,), prefetch=[me],
                  in_specs=in_specs, out_specs=out_specs,
                  out_shape=[jax.ShapeDtypeStruct(parts.shape[1:], F32) for parts, _ in pairs])


def _adam_update(w, gv, m, v, d_ref, nm_ref, nv_ref):
    c1 = 1.0 / (1.0 - ADAM_B1 ** ADAM_STEP)
    c2 = 1.0 / (1.0 - ADAM_B2 ** ADAM_STEP)
    nm = ADAM_B1 * m + (1.0 - ADAM_B1) * gv
    nv = ADAM_B2 * v + (1.0 - ADAM_B2) * (gv * gv)
    nm_ref[...] = nm
    nv_ref[...] = nv
    d_ref[...] = -ADAM_LR * ((nm * c1) / (jnp.sqrt(nv * c2) + ADAM_EPS) + ADAM_WD * w)


def _adamw(w, g, m, v, comm=None):
    r, c = w.shape[0], w.shape[-1]
    tr = _row_tile(r, 512)

    def body(w_ref, g_ref, m_ref, v_ref, d_ref, nm_ref, nv_ref):
        _adam_update(w_ref[...], g_ref[...], m_ref[...], v_ref[...], d_ref, nm_ref, nv_ref)

    mid = (1,) * (w.ndim - 2)
    spec = pl.BlockSpec((tr,) + mid + (c,), lambda i: (i,) + (0,) * (w.ndim - 1))
    sds = jax.ShapeDtypeStruct(w.shape, F32)
    return _pcall(body, [w, g, m, v], name="adamw", grid=(r // tr,), out_shape=[sds, sds, sds],
                  in_specs=[spec] * 4, out_specs=[spec] * 3, comm=comm)


def _adamw_vectors(items):
    k = len(items)

    def body(*refs):
        ins, outs = refs[:4 * k], refs[4 * k:]
        for q in range(k):
            w_ref, g_ref, m_ref, v_ref = ins[4 * q:4 * q + 4]
            _adam_update(w_ref[...], g_ref[...], m_ref[...], v_ref[...], *outs[3 * q:3 * q + 3])

    return _pcall(body, [a for it in items for a in it], name="adamw_vectors",
                  out_shape=[jax.ShapeDtypeStruct(it[0].shape, F32) for it in items for _ in range(3)])


def _adamw_halves(items, comm=None):
    k = len(items)
    r, c = items[0][0].shape
    h = r // 2
    tr = _row_tile(h, min(512, (VMEM_LIMIT * 3 // 4) // (k * 9 * 2 * 4 * c)))
    nb = h // tr
    core = lax.axis_index("c")

    def body(c_ref, *refs):
        ins, outs = refs[:5 * k], refs[5 * k:]
        for q in range(k):
            w_ref, gm_ref, go_ref, m_ref, v_ref = ins[5 * q:5 * q + 5]
            g_ref, d_ref, nm_ref, nv_ref = outs[4 * q:4 * q + 4]
            gv = jnp.where(pl.program_id(0) == c_ref[0], gm_ref[...], go_ref[...])
            g_ref[...] = gv
            _adam_update(w_ref[...], gv, m_ref[...], v_ref[...], d_ref, nm_ref, nv_ref)

    full = pl.BlockSpec((tr, c), lambda hh, i, c_ref: (hh * nb + i, 0))
    half = pl.BlockSpec((tr, c), lambda hh, i, c_ref: (i, 0))
    sds = jax.ShapeDtypeStruct((r, c), F32)
    return _pcall(body, [a for it in items for a in it], name="adamw_halves", grid=(2, nb), prefetch=[core],
                  out_shape=[sds] * (4 * k), in_specs=[full, half, half, full, full] * k, out_specs=[full] * (4 * k),
                  comm=comm)


SC_CORES, SC_TILES, SC_LANES = 2, 16, 16
SC_BLOCK_ROWS, SC_BLOCK_COLS = 8, 512


def _sc_adamw_halves(items):
    k = len(items)
    r, c = items[0][0].shape
    h = r // 2
    bc = min(c, SC_BLOCK_COLS)
    c1 = 1.0 / (1.0 - ADAM_B1 ** ADAM_STEP)
    c2 = 1.0 / (1.0 - ADAM_B2 ** ADAM_STEP)
    mesh = plsc.VectorSubcoreMesh(core_axis_name="sc_core", subcore_axis_name="sc_tile",
                                  num_cores=SC_CORES, num_subcores=SC_TILES)
    spec = pl.BlockSpec(block_shape=(SC_BLOCK_ROWS, bc), index_map=lambda i, j: (i, j))

    def block(w_v, gin_v, m_v, v_v, g_v, d_v, nm_v, nv_v):
        @pl.loop(0, SC_BLOCK_ROWS)
        def _(row):
            @pl.loop(0, bc, step=SC_LANES)
            def _(col):
                at = (pl.ds(row, 1), pl.ds(col, SC_LANES))
                gv = gin_v.at[*at][...]
                nm = ADAM_B1 * m_v.at[*at][...] + (1.0 - ADAM_B1) * gv
                nv = ADAM_B2 * v_v.at[*at][...] + (1.0 - ADAM_B2) * (gv * gv)
                g_v.at[*at][...] = gv
                nm_v.at[*at][...] = nm
                nv_v.at[*at][...] = nv
                d_v.at[*at][...] = -ADAM_LR * ((nm * c1) / (jnp.sqrt(nv * c2) + ADAM_EPS) + ADAM_WD * w_v.at[*at][...])

    def kern(*refs):
        ins, outs = refs[:5 * k], refs[5 * k:]
        core = lax.axis_index("c")

        def half(q, hh, mine):
            w_hbm, gm_hbm, go_hbm, m_hbm, v_hbm = ins[5 * q:5 * q + 5]
            rows = pl.ds(hh * h, h)
            pltpu.emit_pipeline(
                block, grid=(h // SC_BLOCK_ROWS, c // bc), in_specs=[spec] * 4, out_specs=[spec] * 4,
                core_axis_name=("sc_core", "sc_tile"), dimension_semantics=(pltpu.PARALLEL, pltpu.PARALLEL),
                trace_scopes=False,
            )(w_hbm.at[rows, :], gm_hbm if mine else go_hbm, m_hbm.at[rows, :], v_hbm.at[rows, :],
              *(o.at[rows, :] for o in outs[4 * q:4 * q + 4]))

        for q in range(k):
            for hh in range(2):
                pl.when(core == hh)(lambda q=q, hh=hh: half(q, hh, True))
                pl.when(core != hh)(lambda q=q, hh=hh: half(q, hh, False))

    sds = jax.ShapeDtypeStruct((r, c), F32)
    return pl.kernel(kern, out_type=[sds] * (4 * k), mesh=mesh, scratch_types=[], name="sc_adamw_halves")(
        *(a for it in items for a in it))


def _wgrad(name, a, b, a_spec, b_spec, m, n, nb, comm):
    def body(a_ref, b_ref, o_ref):
        o_ref[...] = _dot_tn(a_ref[...], b_ref[...]).astype(o_ref.dtype)

    return _pcall(body, [a, b], name=name, grid=(nb,), out_shape=jax.ShapeDtypeStruct((nb, m, n), BF),
                  in_specs=[a_spec, b_spec], out_specs=pl.BlockSpec((None, m, n), lambda j: (j, 0, 0)), comm=comm)


def _wgrad_cols(name, a, b, nb, comm=None):
    t_tok, m = a.shape
    n = b.shape[1] // nb
    return _wgrad(name, a, b, pl.BlockSpec((t_tok, m), lambda j: (0, 0)), pl.BlockSpec((t_tok, n), lambda j: (0, j)),
                  m, n, nb, comm)


def _wgrad_rows(name, a, b, nb, comm=None):
    t_tok, n = b.shape
    m = a.shape[1] // nb
    return _wgrad(name, a, b, pl.BlockSpec((t_tok, m), lambda j: (0, j)), pl.BlockSpec((t_tok, n), lambda j: (0, 0)),
                  m, n, nb, comm)


def _wgrad_in(a, b, comm=None):
    t_tok, d = b.shape
    m = 512
    nb = IN_PAD // m
    rows = IN_COLS // N_CHIPS

    def body(a_ref, b_ref, o_ref, g_ref):
        @pl.when(pl.program_id(0) == 0)
        def _():
            for j in range(N_CHIPS):
                o_ref[j, rows:, :] = jnp.zeros((IN_ROWS_PAD - rows, d), BF)

        g_ref[...] = _dot_tn(a_ref[...], b_ref[...]).astype(BF)
        for i in range(nb):
            @pl.when(pl.program_id(0) == i)
            def _(i=i):
                lo, hi = i * m, min((i + 1) * m, IN_COLS)
                while lo < hi:
                    j = lo // rows
                    end = min(hi, (j + 1) * rows)
                    o_ref[j, lo - j * rows:end - j * rows, :] = g_ref[lo - i * m:end - i * m, :]
                    lo = end

    return _pcall(body, [a, b], name="wgrad_in", grid=(nb,),
                  out_shape=jax.ShapeDtypeStruct((N_CHIPS, IN_ROWS_PAD, d), BF),
                  in_specs=[pl.BlockSpec((t_tok, m), lambda i: (0, i)), pl.BlockSpec((t_tok, d), lambda i: (0, 0))],
                  out_specs=pl.BlockSpec((N_CHIPS, IN_ROWS_PAD, d), lambda i: (0, 0, 0)),
                  scratch=[pltpu.VMEM((m, d), BF)], comm=comm)


def _wgrad_a_shared(name, a, b4, comm=None):
    t_tok, m = a.shape
    nb, _, n = b4.shape
    return _wgrad(name, a, b4, pl.BlockSpec((t_tok, m), lambda j: (0, 0)),
                  pl.BlockSpec((None, t_tok, n), lambda j: (j, 0, 0)), m, n, nb, comm)


def _wgrad_b_shared(name, a4, b, comm=None):
    nb, t_tok, m = a4.shape
    n = b.shape[1]
    return _wgrad(name, a4, b, pl.BlockSpec((None, t_tok, m), lambda j: (j, 0, 0)),
                  pl.BlockSpec((t_tok, n), lambda j: (0, 0)), m, n, nb, comm)


def _w4_spec(r, c):
    return pl.BlockSpec((None, r, c), lambda i, j: (j, 0, 0))


FFN_ROW_CHUNK = 256


def _row_chunks(tm):
    rc = FFN_ROW_CHUNK if tm % FFN_ROW_CHUNK == 0 else tm
    return [slice(r, r + rc) for r in range(0, tm, rc)]


def _ffn_fwd(h, ln, wg4, wu4, wd4, comm=None):
    t_tok, d = h.shape
    f = wg4.shape[-2]
    tm = _tile(t_tok, 512)

    def body(h_ref, ln_ref, wg_ref, wu_ref, wd_ref, ho_ref, n_ref, g_ref, u_ref, n_s, acc):
        j = pl.program_id(1)

        @pl.when(j == 0)
        def _():
            xv = h_ref[...]
            nv = (xv * _rstd(xv) * ln_ref[...]).astype(BF)
            n_s[...] = nv
            n_ref[...] = nv
            acc[...] = jnp.zeros_like(acc)

        nv = n_s[...]
        g = _dot_nt(nv, wg_ref[...])
        u = _dot_nt(nv, wu_ref[...])
        g_ref[...] = g.astype(BF)
        u_ref[...] = u.astype(BF)
        a = (g * _sigmoid(g) * u).astype(BF)
        acc[...] += _dot(a, wd_ref[...])

        @pl.when(j == N_CHIPS - 1)
        def _():
            ho_ref[...] = h_ref[...] + 0.5 * acc[...]

    row = pl.BlockSpec((tm, d), lambda i, j: (i, 0))
    gu = pl.BlockSpec((None, tm, f), lambda i, j: (j, i, 0))
    gu_sds = jax.ShapeDtypeStruct((N_CHIPS, t_tok, f), BF)
    return _pcall(
        body, [h, ln, wg4, wu4, wd4], name="ffn_fwd", grid=(t_tok // tm, N_CHIPS),
        out_shape=[jax.ShapeDtypeStruct((t_tok, d), F32), jax.ShapeDtypeStruct((t_tok, d), BF), gu_sds, gu_sds],
        in_specs=[row, pl.BlockSpec((1, d), lambda i, j: (0, 0)), _w4_spec(f, d), _w4_spec(f, d), _w4_spec(f, d)],
        out_specs=[row, row, gu, gu],
        scratch=[pltpu.VMEM((tm, d), BF), pltpu.VMEM((tm, d), F32)], comm=comm)


def _ffn_bwd(dho, h, ln, g4, u4, wg4, wu4, wd4, comm=None):
    t_tok, d = h.shape
    f = wg4.shape[-2]
    tm = _tile(t_tok, 512)

    def body(dho_ref, h_ref, ln_ref, g_ref, u_ref, wg_ref, wu_ref, wd_ref,
             dhi_ref, dln_ref, dg_ref, du_ref, a_ref, dhb_ref, dhb_s, dn_acc):
        i, j = pl.program_id(0), pl.program_id(1)

        @pl.when(j == 0)
        def _():
            dhb = (0.5 * dho_ref[...]).astype(BF)
            dhb_s[...] = dhb
            dhb_ref[...] = dhb
            dn_acc[...] = jnp.zeros_like(dn_acc)

        @pl.when((i == 0) & (j == 0))
        def _():
            dln_ref[...] = jnp.zeros_like(dln_ref)

        for rows in _row_chunks(tm):
            g = g_ref[rows, :].astype(F32)
            u = u_ref[rows, :].astype(F32)
            s = _sigmoid(g)
            sg = g * s
            a_ref[rows, :] = (sg * u).astype(BF)
            da = _dot_nt(dhb_s[rows, :], wd_ref[...])
            dg = (da * u * (s * (1.0 + g * (1.0 - s)))).astype(BF)
            du = (da * sg).astype(BF)
            dg_ref[rows, :] = dg
            du_ref[rows, :] = du
            dn_acc[rows, :] += _dot(dg, wg_ref[...]) + _dot(du, wu_ref[...])

        @pl.when(j == N_CHIPS - 1)
        def _():
            xv = h_ref[...]
            dx, dln = _rms_bwd(dn_acc[...], xv, _rstd(xv), ln_ref[...])
            dln_ref[...] += dln
            dhi_ref[...] = dho_ref[...] + dx

    row = pl.BlockSpec((tm, d), lambda i, j: (i, 0))
    vec = pl.BlockSpec((1, d), lambda i, j: (0, 0))
    gu = pl.BlockSpec((None, tm, f), lambda i, j: (j, i, 0))
    gu_sds = jax.ShapeDtypeStruct((N_CHIPS, t_tok, f), BF)
    return _pcall(
        body, [dho, h, ln, g4, u4, wg4, wu4, wd4], name="ffn_bwd", grid=(t_tok // tm, N_CHIPS),
        out_shape=[jax.ShapeDtypeStruct((t_tok, d), F32), jax.ShapeDtypeStruct((1, d), F32),
                   gu_sds, gu_sds, gu_sds, jax.ShapeDtypeStruct((t_tok, d), BF)],
        in_specs=[row, row, vec, gu, gu, _w4_spec(f, d), _w4_spec(f, d), _w4_spec(f, d)],
        out_specs=[row, vec, gu, gu, gu, row],
        scratch=[pltpu.VMEM((tm, d), BF), pltpu.VMEM((tm, d), F32)], comm=comm)


def _rope_tables(pos_col, inv_freq2, blocks, comm=None):
    t_tok = pos_col.shape[0]
    k = len(blocks)
    n_steps = 4
    assert all(b.shape[0] % (16 * n_steps) == 0 for b in blocks)

    def body(p_ref, f_ref, *refs):
        cos_ref, sin_ref = refs[k:k + 2]
        ang = p_ref[...] * f_ref[...]
        lane = lax.broadcasted_iota(jnp.int32, ang.shape, 1)
        s = jnp.sin(ang)
        cos_ref[...] = jnp.cos(ang)
        sin_ref[...] = jnp.where((lane & 1) == 0, -s, s)
        for a_ref, o_ref in zip(refs[:k], refs[k + 2:]):
            o_ref[...] = a_ref[...].astype(BF)

    rows = lambda r, c: pl.BlockSpec((r // n_steps, c), lambda i: (i, 0))
    casts = [rows(*b.shape) for b in blocks]
    sds = jax.ShapeDtypeStruct((t_tok, 128), F32)
    return _pcall(body, [pos_col, inv_freq2, *blocks], name="rope_tables", grid=(n_steps,),
                  out_shape=[sds, sds] + [jax.ShapeDtypeStruct(b.shape, BF) for b in blocks],
                  in_specs=[rows(t_tok, 1), pl.BlockSpec((1, 128), lambda i: (0, 0))] + casts,
                  out_specs=[rows(t_tok, 128)] * 2 + casts, comm=comm)


def _swap_pairs(x):
    lane = lax.broadcasted_iota(jnp.int32, x.shape, 1)
    return jnp.where((lane & 1) == 0, pltpu.roll(x, 127, 1), pltpu.roll(x, 1, 1))


def _mix_in(h, ln, w_in4, wm4, b_m, cos_t, sin_t, comm=None):
    t_tok, d = h.shape
    cm = wm4.shape[-1]
    tm = _tile(t_tok, 256)
    rows = IN_COLS // N_CHIPS

    def body(h_ref, ln_ref, win4_ref, wm_ref, bm_ref, cos_ref, sin_ref,
             u_ref, rq_ref, rk_ref, rv_ref, rg_ref, fq_ref, fk_ref, fv_ref, ff_ref, ga_ref, gb_ref, win_ref):
        @pl.when(pl.program_id(0) == 0)
        def _():
            for j in range(N_CHIPS):
                win_ref[j * rows:(j + 1) * rows, :] = win4_ref[j, :rows, :]
            win_ref[IN_COLS:, :] = jnp.zeros((IN_PAD - IN_COLS, d), BF)

        xv = h_ref[...]
        ub = (xv * _rstd(xv) * ln_ref[...]).astype(BF)
        u_ref[...] = ub
        cosv, sinv = cos_ref[...], sin_ref[...]

        def sec(k):
            return _dot_nt(ub, win_ref[k * 512:(k + 1) * 512, :])

        def rot(xh):
            return xh * cosv + _swap_pairs(xh) * sinv

        pq, pk = sec(0), sec(1)
        for hh in range(RET_HEADS):
            sl = slice(hh * RET_DIM, (hh + 1) * RET_DIM)
            rq_ref[:, sl] = rot(pq[:, sl]).astype(BF)
            rk_ref[:, sl] = (rot(pk[:, sl]) * RET_SCALE).astype(BF)
        rv_ref[...] = sec(2).astype(BF)
        rg_ref[...] = sec(3).astype(BF)
        fq_ref[...] = (sec(4) * FOX_SCALE).astype(BF)
        fk_ref[...] = sec(5).astype(BF)
        fv_ref[...] = sec(6).astype(BF)
        ff_ref[...] = _dot_nt(ub, win_ref[FF_COL:FF_COL + 128, :])
        for j in range(N_CHIPS):
            gs = _sigmoid(_dot(ub, wm_ref[j]) + bm_ref[:, j * cm:(j + 1) * cm]).astype(BF)
            col = j * cm
            if col < d:
                ga_ref[:, col:col + cm] = gs
            else:
                gb_ref[:, col - d:col - d + cm] = gs

    row = lambda c: pl.BlockSpec((tm, c), lambda i: (i, 0))
    full = lambda *s: pl.BlockSpec(s, lambda i: (0,) * len(s))
    sds = lambda c, dt: jax.ShapeDtypeStruct((t_tok, c), dt)
    return _pcall(
        body, [h, ln, w_in4, wm4, b_m, cos_t, sin_t], name="mix_in", grid=(t_tok // tm,),
        out_shape=[sds(d, BF)] + [sds(512, BF)] * 7 + [sds(128, F32), sds(d, BF), sds(d, BF),
                                                       jax.ShapeDtypeStruct((IN_PAD, d), BF)],
        in_specs=[row(d), full(1, d), full(*w_in4.shape), full(N_CHIPS, d, cm), full(1, 2 * d), row(128),
                  row(128)],
        out_specs=[row(d)] + [row(512)] * 7 + [row(128), row(d), row(d), full(IN_PAD, d)], comm=comm)


def _split3(x):
    hi = x.astype(BF)
    r1 = x - hi.astype(F32)
    mid = r1.astype(BF)
    lo = (r1 - mid.astype(F32)).astype(BF)
    return hi, mid, lo


def _aug_lane():
    return lax.broadcasted_iota(jnp.int32, (1, 128), 1) & (FOX_DIM - 1)


def _aug_put(base, k0, parts):
    w = _aug_lane()
    for i, part in enumerate(parts):
        base = jnp.where(w == k0 + i, part, base)
    return base


def _forget_fwd(ffl, b_pad):
    t_tok = ffl.shape[0]
    tb = _tile(t_tok, 256)

    def body(ff_ref, b_ref, aq_ref, ak_ref, cum_s):
        r = lax.broadcasted_iota(jnp.int32, (tb, tb), 0)
        c = lax.broadcasted_iota(jnp.int32, (tb, tb), 1)
        tri = jnp.where(c <= r, 1.0, 0.0).astype(BF)
        carry = jnp.zeros((1, 128), F32)
        for i in range(t_tok // tb):
            z = ff_ref[i * tb:(i + 1) * tb, :] + b_ref[...]
            lf = jnp.minimum(z, 0.0) - jnp.log(1.0 + jnp.exp(-jnp.abs(z)))
            hi, mid, lo = _split3(lf)
            cs = _dot(tri, hi) + _dot(tri, mid) + _dot(tri, lo) + carry
            cum_s[i * tb:(i + 1) * tb, :] = cs
            carry = cs[tb - 1:tb, :]
        x = cum_s[...]
        first = lax.broadcasted_iota(jnp.int32, (1, 128), 1) < FOX_DIM
        w = _aug_lane()
        one = jnp.ones((t_tok, 128), BF)
        zero = jnp.zeros((t_tok, 128), BF)
        for pp in range(FOX_HEADS // 2):
            other = jnp.where(first, x[:, 2 * pp + 1:2 * pp + 2], x[:, 2 * pp:2 * pp + 1])
            parts = _split3(other)
            aq = jnp.where((w >= 3) & (w < 6), one, zero)
            ak = jnp.where((w < 3) | ((w >= 6) & (w < 9)), one, zero)
            aq_ref[:, pp * 128:(pp + 1) * 128] = _aug_put(aq, 0, parts)
            ak_ref[:, pp * 128:(pp + 1) * 128] = _aug_put(ak, 3, [-q for q in parts])

    sds = jax.ShapeDtypeStruct((t_tok, FOX_WIDTH), BF)
    return _pcall(body, [ffl, b_pad], name="forget_fwd", out_shape=[sds, sds],
                  scratch=[pltpu.VMEM((t_tok, 128), F32)])


def _forget_bwd(dcum_t, dcum_q, ffl, b_pad):
    t_tok = ffl.shape[0]
    tb = _tile(t_tok, 256)

    def body(dc_ref, dq_ref, ff_ref, b_ref, dff_ref, db_ref, pad_s, d_s):
        pad_s[...] = jnp.zeros_like(pad_s)
        pad_s[0:FOX_HEADS, :] = dc_ref[...]
        dsum = pad_s[...].T
        lane = lax.broadcasted_iota(jnp.int32, (t_tok, 128), 1)
        for hh in range(FOX_HEADS):
            dsum = dsum + jnp.where(lane == hh, dq_ref[:, hh * FOX_DIM:hh * FOX_DIM + 1], 0.0)
        d_s[...] = dsum
        r = lax.broadcasted_iota(jnp.int32, (tb, tb), 0)
        c = lax.broadcasted_iota(jnp.int32, (tb, tb), 1)
        tri = jnp.where(c >= r, 1.0, 0.0).astype(BF)
        carry = jnp.zeros((1, 128), F32)
        db = jnp.zeros((1, 128), F32)
        for i in reversed(range(t_tok // tb)):
            hi, mid, lo = _split3(d_s[i * tb:(i + 1) * tb, :])
            dlf = _dot(tri, hi) + _dot(tri, mid) + _dot(tri, lo) + carry
            carry = dlf[0:1, :]
            z = ff_ref[i * tb:(i + 1) * tb, :] + b_ref[...]
            dff = dlf * _sigmoid(-z)
            dff_ref[i * tb:(i + 1) * tb, :] = dff.astype(BF)
            db = db + jnp.sum(dff, axis=0, keepdims=True)
        db_ref[...] = db

    return _pcall(
        body, [dcum_t, dcum_q, ffl, b_pad], name="forget_bwd",
        out_shape=[jax.ShapeDtypeStruct((t_tok, 128), BF), jax.ShapeDtypeStruct((1, 128), F32)],
        scratch=[pltpu.VMEM((128, t_tok), F32), pltpu.VMEM((t_tok, 128), F32)])


def _first_half():
    return lax.broadcasted_iota(jnp.int32, (1, 128), 1) < FOX_DIM


def _head_rows(x2, a2, hh):
    return jnp.where(_first_half(), x2, a2) if hh == 0 else jnp.where(_first_half(), a2, x2)


def _head_only(x2, hh):
    zero = jnp.zeros_like(x2)
    return jnp.where(_first_half(), x2, zero) if hh == 0 else jnp.where(_first_half(), zero, x2)


def _causal_diag(s):
    rows = lax.broadcasted_iota(jnp.int32, s.shape, 0)
    cols = lax.broadcasted_iota(jnp.int32, s.shape, 1)
    return jnp.where(cols <= rows, s, NEG)


def _diag_or_below(qi, ki, step):
    pl.when(ki < qi)(lambda: step(False))
    pl.when(ki == qi)(lambda: step(True))


def _tri_rows(s, n):
    qi = sum((s >= r * (r + 1) // 2).astype(jnp.int32) for r in range(1, n))
    return qi, s - (qi * (qi + 1)) // 2


def _tri_cols(s, n):
    ki = sum((s >= k * n - k * (k - 1) // 2).astype(jnp.int32) for k in range(1, n))
    return ki, ki + s - (ki * n - (ki * (ki - 1)) // 2)


def _fox_fwd(fq, fk, fv, aq, ak, comm=None):
    t_tok = fq.shape[0]
    t = _tile(t_tok, 512)
    nq = t_tok // t
    npair = FOX_HEADS // 2

    def body(q_ref, k_ref, v_ref, aq_ref, ak_ref, o_ref, of_ref, aqb_ref, m_s, l_s, acc_s):
        qi, ki = _tri_rows(pl.program_id(1), nq)

        @pl.when(ki == 0)
        def _():
            m_s[...] = jnp.full_like(m_s, NEG)
            l_s[...] = jnp.zeros_like(l_s)
            acc_s[...] = jnp.zeros_like(acc_s)

        def step(diag):
            q2, k2, v2, aq2, ak2 = q_ref[...], k_ref[...], v_ref[...], aq_ref[...], ak_ref[...]
            for hh in range(2):
                s = _dot_nt(_head_rows(q2, aq2, hh), _head_rows(k2, ak2, hh))
                if diag:
                    s = _causal_diag(s)
                m_prev = m_s[hh]
                m_new = jnp.maximum(m_prev, jnp.max(s, axis=1, keepdims=True))
                alpha = jnp.exp(m_prev - m_new)
                p = jnp.exp(s - jnp.tile(m_new, (1, t // 128)))
                l_s[hh] = alpha * l_s[hh] + jnp.sum(p, axis=1, keepdims=True)
                acc_s[hh] = alpha * acc_s[hh] + _dot(p.astype(BF), v2)
                m_s[hh] = m_new

        _diag_or_below(qi, ki, step)

        @pl.when(ki == qi)
        def _():
            first = _first_half()
            o = jnp.where(first, acc_s[0] / l_s[0], acc_s[1] / l_s[1])
            o_ref[...] = o.astype(BF)
            of_ref[...] = o
            other = jnp.where(first, m_s[1] + jnp.log(l_s[1]), m_s[0] + jnp.log(l_s[0]))
            aqb_ref[...] = _aug_put(aq_ref[...], 6, _split3(-other))

    qs = pl.BlockSpec((t, 128), lambda p, s: (_tri_rows(s, nq)[0], p))
    ks = pl.BlockSpec((t, 128), lambda p, s: (_tri_rows(s, nq)[1], p))
    stat = pltpu.VMEM((2, t, 128), F32)
    return _pcall(
        body, [fq, fk, fv, aq, ak], name="fox_fwd", grid=(npair, nq * (nq + 1) // 2),
        out_shape=[jax.ShapeDtypeStruct((t_tok, FOX_WIDTH), BF), jax.ShapeDtypeStruct((t_tok, FOX_WIDTH), F32),
                   jax.ShapeDtypeStruct((t_tok, FOX_WIDTH), BF)],
        in_specs=[qs, ks, ks, qs, ks], out_specs=[qs, qs, qs], scratch=[stat, stat, stat], comm=comm)


def _fox_ds(q2, k2, v2, do2, aq2, ak2, ad2, hh, diag):
    s = _dot_nt(_head_rows(q2, aq2, hh), _head_rows(k2, ak2, hh))
    if diag:
        s = _causal_diag(s)
    p = jnp.exp(s)
    av = jnp.where(_aug_lane() < 3, 1.0, 0.0).astype(BF)
    dp = _dot_nt(_head_rows(do2, ad2, hh), _head_rows(v2, jnp.broadcast_to(av, v2.shape), hh))
    return p, p * dp


def _fox_bwd(fq, fk, fv, do, aqb, ak, ad, comm=None):
    t_tok = fq.shape[0]
    t = _tile(t_tok, 512)
    nq = t_tok // t
    npair = FOX_HEADS // 2
    n_steps = nq * (nq + 1) // 2

    def body(q_ref, k_ref, v_ref, do_ref, aq_ref, ak_ref, ad_ref, dq_ref, dk_ref, dv_ref, dck_ref, dcq_ref,
             dk_s, dv_s, dq_s, rs_s):
        step_id = pl.program_id(1)
        ki, qi = _tri_cols(step_id, nq)

        @pl.when(step_id == 0)
        def _():
            dq_s[...] = jnp.zeros_like(dq_s)
            rs_s[...] = jnp.zeros_like(rs_s)

        @pl.when(qi == ki)
        def _():
            dk_s[...] = jnp.zeros_like(dk_s)
            dv_s[...] = jnp.zeros_like(dv_s)
            dck_ref[...] = jnp.zeros_like(dck_ref)

        rows = pl.ds(qi * t if isinstance(qi, int) else pl.multiple_of(qi * t, t), t)

        def step(diag):
            q2, k2, v2, do2 = q_ref[...], k_ref[...], v_ref[...], do_ref[...]
            dq = []
            for hh in range(2):
                p, ds = _fox_ds(q2, k2, v2, do2, aq_ref[...], ak_ref[...], ad_ref[...], hh, diag)
                dsb = ds.astype(BF)
                dv_s[...] += _dot_tn(p.astype(BF), _head_only(do2, hh))
                dk_s[...] += _dot_tn(dsb, _head_only(q2, hh))
                dq.append(_dot(dsb, k2))
                dck_ref[hh] = dck_ref[hh] - jnp.sum(ds, axis=0, keepdims=True)
                rs_s[hh, rows, :] = rs_s[hh, rows, :] + jnp.sum(ds, axis=1, keepdims=True)
            dq_s[rows, :] = dq_s[rows, :] + jnp.where(_first_half(), dq[0], dq[1])

        _diag_or_below(qi, ki, step)

        @pl.when(qi == nq - 1)
        def _():
            dk_ref[...] = dk_s[...].astype(BF)
            dv_ref[...] = dv_s[...].astype(BF)

        @pl.when(step_id == n_steps - 1)
        def _():
            dq_ref[...] = (dq_s[...] * FOX_SCALE).astype(BF)
            dcq_ref[...] = jnp.where(_first_half(), rs_s[0], rs_s[1])

    qs = pl.BlockSpec((t, 128), lambda p, s: (_tri_cols(s, nq)[1], p))
    ks = pl.BlockSpec((t, 128), lambda p, s: (_tri_cols(s, nq)[0], p))
    cks = pl.BlockSpec((2, 1, t), lambda p, s: (p, 0, _tri_cols(s, nq)[0]))
    seq = pl.BlockSpec((t_tok, 128), lambda p, s: (0, p))
    sds = jax.ShapeDtypeStruct((t_tok, FOX_WIDTH), BF)
    return _pcall(
        body, [fq, fk, fv, do, aqb, ak, ad], name="fox_bwd", grid=(npair, n_steps),
        out_shape=[sds, sds, sds, jax.ShapeDtypeStruct((FOX_HEADS, 1, t_tok), F32),
                   jax.ShapeDtypeStruct((t_tok, FOX_WIDTH), F32)],
        in_specs=[qs, ks, ks, qs, qs, ks, qs], out_specs=[seq, ks, ks, cks, seq],
        scratch=[pltpu.VMEM((t, 128), F32), pltpu.VMEM((t, 128), F32), pltpu.VMEM((t_tok, 128), F32),
                 pltpu.VMEM((2, t_tok, 128), F32)], comm=comm)


def _ret_consts():
    c = RET_CHUNK
    log_gamma = jnp.log1p(-jnp.exp2(-5.0 - jnp.arange(RET_HEADS, dtype=F32)))
    idx = jnp.arange(c, dtype=F32)
    diff = idx[:, None] - idx[None, :]
    dmask = jnp.where(diff >= 0, jnp.exp(log_gamma[:, None, None] * jnp.maximum(diff, 0.0)), 0.0)
    qdec = jnp.exp(log_gamma[:, None] * (idx + 1.0))
    kdec = jnp.exp(log_gamma[:, None] * (c - 1 - idx))
    cdec = jnp.exp(log_gamma * c)
    bc = lambda v: jnp.broadcast_to(v[:, :, None], (RET_HEADS, c, RET_DIM))
    return dmask, bc(qdec), bc(kdec), jnp.broadcast_to(cdec[:, None, None], (RET_HEADS, c, RET_DIM))


def _group_norm(y):
    mu = jnp.mean(y, axis=-1, keepdims=True)
    yc = y - mu
    r = lax.rsqrt(jnp.mean(yc * yc, axis=-1, keepdims=True) + EPS)
    return yc * r, r


def _ret_fwd(rq, rk, rv, rg, consts, comm=None):
    t_tok = rq.shape[0]
    nb = 4 if t_tok % (4 * RET_CHUNK) == 0 else 1
    tr = nb * RET_CHUNK
    n_steps = t_tok // tr
    c = RET_CHUNK

    def body(q_ref, k_ref, v_ref, g_ref, dm_ref, qd_ref, kd_ref, cd_ref, y_ref, yo_ref, st_ref, s_s):
        @pl.when(pl.program_id(0) == 0)
        def _():
            s_s[...] = jnp.zeros_like(s_s)

        for b in range(nb):
            rows = slice(b * c, (b + 1) * c)
            for hh in range(RET_HEADS):
                cols = slice(hh * RET_DIM, (hh + 1) * RET_DIM)
                q, k, v = q_ref[rows, cols], k_ref[rows, cols], v_ref[rows, cols]
                state = s_s[hh]
                st_ref[hh, b] = state
                sc = (_dot_nt(q, k) * dm_ref[hh]).astype(BF)
                y = _dot(sc, v) + _dot((q.astype(F32) * qd_ref[hh]).astype(BF), state.astype(BF))
                s_s[hh] = cd_ref[hh] * state + _dot_tn((k.astype(F32) * kd_ref[hh]).astype(BF), v)
                y_ref[rows, cols] = y
                yn, _ = _group_norm(y)
                gate = g_ref[rows, cols].astype(F32)
                yo_ref[rows, cols] = (yn * (gate * _sigmoid(gate))).astype(BF)

    blk = pl.BlockSpec((tr, RET_WIDTH), lambda i: (i, 0))
    cst = pl.BlockSpec((RET_HEADS, c, RET_DIM), lambda i: (0, 0, 0))
    return _pcall(
        body, [rq, rk, rv, rg, *consts], name="ret_fwd", grid=(n_steps,),
        out_shape=[jax.ShapeDtypeStruct((t_tok, RET_WIDTH), F32), jax.ShapeDtypeStruct((t_tok, RET_WIDTH), BF),
                   jax.ShapeDtypeStruct((RET_HEADS, t_tok // c, RET_DIM, RET_DIM), F32)],
        in_specs=[blk] * 4 + [cst] * 4,
        out_specs=[blk, blk, pl.BlockSpec((RET_HEADS, nb, RET_DIM, RET_DIM), lambda i: (0, i, 0, 0))],
        scratch=[pltpu.VMEM((RET_HEADS, RET_DIM, RET_DIM), F32)], comm=comm)


def _ret_bwd(rq, rk, rv, rg, y_raw, dyo, states, consts, cos_t, sin_t, comm=None):
    t_tok = rq.shape[0]
    nb = 4 if t_tok % (4 * RET_CHUNK) == 0 else 1
    tr = nb * RET_CHUNK
    n_steps = t_tok // tr
    c = RET_CHUNK

    def body(q_ref, k_ref, v_ref, g_ref, y_ref, dyo_ref, st_ref, dm_ref, qd_ref, kd_ref, cd_ref,
             cos_ref, sin_ref, dq_ref, dk_ref, dv_ref, dg_ref, ds_s):
        @pl.when(pl.program_id(0) == 0)
        def _():
            ds_s[...] = jnp.zeros_like(ds_s)

        for b in reversed(range(nb)):
            rows = slice(b * c, (b + 1) * c)
            cosv, sinv = cos_ref[rows, :], sin_ref[rows, :]
            for hh in range(RET_HEADS):
                cols = slice(hh * RET_DIM, (hh + 1) * RET_DIM)
                dm, qd, kd, cd = dm_ref[hh], qd_ref[hh], kd_ref[hh], cd_ref[hh]
                q, k, v = q_ref[rows, cols], k_ref[rows, cols], v_ref[rows, cols]
                yn, r = _group_norm(y_ref[rows, cols])
                gate = g_ref[rows, cols].astype(F32)
                sg = _sigmoid(gate)
                dyo = dyo_ref[rows, cols]
                dg_ref[rows, cols] = (dyo * yn * (sg * (1.0 + gate * (1.0 - sg)))).astype(BF)
                dyn = dyo * (gate * sg)
                dy = r * (dyn - jnp.mean(dyn, axis=-1, keepdims=True)
                          - yn * jnp.mean(dyn * yn, axis=-1, keepdims=True))
                dyb = dy.astype(BF)
                state_b = st_ref[hh, b].astype(BF)
                dstate = ds_s[hh]
                dstate_b = dstate.astype(BF)
                qdb = (q.astype(F32) * qd).astype(BF)
                kdb = (k.astype(F32) * kd).astype(BF)
                sc = (_dot_nt(q, k) * dm).astype(BF)
                dv = _dot_tn(sc, dyb) + _dot(kdb, dstate_b)
                dp = (_dot_nt(dyb, v) * dm).astype(BF)
                dq = _dot(dp, k) + _dot_nt(dyb, state_b) * qd
                dk = (_dot_tn(dp, q) + _dot_nt(v, dstate_b) * kd) * RET_SCALE
                ds_s[hh] = cd * dstate + _dot_tn(qdb, dyb)
                dv_ref[rows, cols] = dv.astype(BF)
                dq_ref[rows, cols] = (dq * cosv - _swap_pairs(dq) * sinv).astype(BF)
                dk_ref[rows, cols] = (dk * cosv - _swap_pairs(dk) * sinv).astype(BF)

    rev = lambda i: n_steps - 1 - i
    blk = pl.BlockSpec((tr, RET_WIDTH), lambda i: (rev(i), 0))
    tab = pl.BlockSpec((tr, RET_DIM), lambda i: (rev(i), 0))
    cst = pl.BlockSpec((RET_HEADS, c, RET_DIM), lambda i: (0, 0, 0))
    sds = jax.ShapeDtypeStruct((t_tok, RET_WIDTH), BF)
    return _pcall(
        body, [rq, rk, rv, rg, y_raw, dyo, states, *consts, cos_t, sin_t], name="ret_bwd",
        grid=(n_steps,), out_shape=[sds] * 4,
        in_specs=[blk] * 6 + [pl.BlockSpec((RET_HEADS, nb, RET_DIM, RET_DIM), lambda i: (0, rev(i), 0, 0))]
        + [cst] * 4 + [tab, tab],
        out_specs=[blk] * 4, scratch=[pltpu.VMEM((RET_HEADS, RET_DIM, RET_DIM), F32)], comm=comm)


def _mix_out(h, y_ret, y_fox, ga, gb, wr4, wf4, wo4, comm=None):
    t_tok, d = h.shape
    cz = wr4.shape[-1]
    ro = wo4.shape[-2]
    tm = _tile(t_tok, 512)

    def body(h_ref, yr_ref, yf_ref, ga_ref, gb_ref, wr_ref, wf_ref, wo_ref, ho_ref, za_ref, zb_ref, mix_ref):
        yr, yf = yr_ref[...], yf_ref[...]
        for j in range(N_CHIPS):
            sl = slice(j * cz, (j + 1) * cz)
            za = _dot(yr, wr_ref[j])
            zb = _dot(yf, wf_ref[j])
            za_ref[:, sl] = za.astype(BF)
            zb_ref[:, sl] = zb.astype(BF)
            mix_ref[:, sl] = (ga_ref[:, sl].astype(F32) * za + gb_ref[:, sl].astype(F32) * zb).astype(BF)
        acc = h_ref[...]
        for j in range(N_CHIPS):
            acc = acc + _dot(mix_ref[:, j * ro:(j + 1) * ro], wo_ref[j])
        ho_ref[...] = acc

    row = lambda c: pl.BlockSpec((tm, c), lambda i: (i, 0))
    full = lambda *s: pl.BlockSpec(s, lambda i: (0,) * len(s))
    sds = lambda dt: jax.ShapeDtypeStruct((t_tok, d), dt)
    return _pcall(
        body, [h, y_ret, y_fox, ga, gb, wr4, wf4, wo4], name="mix_out", grid=(t_tok // tm,),
        out_shape=[sds(F32), sds(BF), sds(BF), sds(BF)],
        in_specs=[row(d), row(RET_WIDTH), row(FOX_WIDTH), row(d), row(d),
                  full(N_CHIPS, RET_WIDTH, cz), full(N_CHIPS, FOX_WIDTH, cz), full(N_CHIPS, ro, d)],
        out_specs=[row(d)] * 4, comm=comm)


def _mix_out_bwd(dh, za, zb, ga, gb, y_fox, wr4, wf4, wo4, comm=None):
    t_tok, d = dh.shape
    cz = wr4.shape[-1]
    ro = wo4.shape[-2]
    tm = _tile(t_tok, 256)

    def body(dh_ref, za_ref, zb_ref, ga_ref, gb_ref, yf_ref, wr_ref, wf_ref, wo_ref,
             dhb_ref, dgp_ref, dza_ref, dzb_ref, dyr_ref, dyf_ref, dl_ref, db_ref):
        @pl.when(pl.program_id(0) == 0)
        def _():
            db_ref[...] = jnp.zeros_like(db_ref)

        dhb = dh_ref[...].astype(BF)
        dhb_ref[...] = dhb
        dyr = jnp.zeros((tm, RET_WIDTH), F32)
        dyf = jnp.zeros((tm, FOX_WIDTH), F32)
        for j in range(N_CHIPS):
            sl = slice(j * ro, (j + 1) * ro)
            dmix = _dot_nt(dhb, wo_ref[j])
            ga, gb = ga_ref[:, sl].astype(F32), gb_ref[:, sl].astype(F32)
            dza = (dmix * ga).astype(BF)
            dzb = (dmix * gb).astype(BF)
            dza_ref[:, sl] = dza
            dzb_ref[:, sl] = dzb
            dga = dmix * za_ref[:, sl].astype(F32) * ga * (1.0 - ga)
            dgb = dmix * zb_ref[:, sl].astype(F32) * gb * (1.0 - gb)
            dgp_ref[:, sl] = dga.astype(BF)
            dgp_ref[:, d + j * ro:d + (j + 1) * ro] = dgb.astype(BF)
            db_ref[:, sl] += jnp.sum(dga, axis=0, keepdims=True)
            db_ref[:, d + j * ro:d + (j + 1) * ro] += jnp.sum(dgb, axis=0, keepdims=True)
        for j in range(N_CHIPS):
            sl = slice(j * cz, (j + 1) * cz)
            dyr = dyr + _dot_nt(dza_ref[:, sl], wr_ref[j])
            dyf = dyf + _dot_nt(dzb_ref[:, sl], wf_ref[j])
        dyr_ref[...] = dyr
        dyfb = dyf.astype(BF)
        dyf_ref[...] = dyfb
        prod = dyfb.astype(F32) * yf_ref[...]
        first = _first_half()
        for pp in range(FOX_HEADS // 2):
            blk = prod[:, pp * 128:(pp + 1) * 128]
            s0 = jnp.sum(jnp.where(first, blk, 0.0), axis=1, keepdims=True)
            s1 = jnp.sum(jnp.where(first, 0.0, blk), axis=1, keepdims=True)
            parts = _split3(-jnp.where(first, s1, s0))
            dl_ref[:, pp * 128:(pp + 1) * 128] = _aug_put(jnp.zeros((tm, 128), BF), 0, parts)

    row = lambda c: pl.BlockSpec((tm, c), lambda i: (i, 0))
    full = lambda *s: pl.BlockSpec(s, lambda i: (0,) * len(s))
    sds = lambda c, dt: jax.ShapeDtypeStruct((t_tok, c), dt)
    return _pcall(
        body, [dh, za, zb, ga, gb, y_fox, wr4, wf4, wo4], name="mix_out_bwd", grid=(t_tok // tm,),
        out_shape=[sds(d, BF), sds(2 * d, BF), sds(d, BF), sds(d, BF), sds(RET_WIDTH, F32),
                   sds(FOX_WIDTH, BF), sds(FOX_WIDTH, BF), jax.ShapeDtypeStruct((1, 2 * d), F32)],
        in_specs=[row(d)] * 5 + [row(FOX_WIDTH), full(N_CHIPS, RET_WIDTH, cz), full(N_CHIPS, FOX_WIDTH, cz),
                                 full(N_CHIPS, ro, d)],
        out_specs=[row(d), row(2 * d), row(d), row(d), row(RET_WIDTH), row(FOX_WIDTH), row(FOX_WIDTH),
                   full(1, 2 * d)],
        comm=comm)


def _mix_in_bwd(dh, h, ln, parts, dff, dgpre, w_in, wm4, comm=None):
    t_tok, d = h.shape
    cm = wm4.shape[-1]
    tm = _tile(t_tok, 256)

    def body(dh_ref, h_ref, ln_ref, p0, p1, p2, p3, p4, p5, p6, dff_ref, dgp_ref, win_ref, wm_ref,
             dhi_ref, dln_ref, dproj_ref):
        @pl.when(pl.program_id(0) == 0)
        def _():
            dln_ref[...] = jnp.zeros_like(dln_ref)

        for k, pr in enumerate((p0, p1, p2, p3, p4, p5, p6)):
            dproj_ref[:, k * 512:(k + 1) * 512] = pr[...]
        dproj_ref[:, FF_COL:FF_COL + 128] = dff_ref[...]
        dproj_ref[:, FF_COL + 128:] = jnp.zeros((tm, IN_PAD - FF_COL - 128), BF)
        du = _dot(dproj_ref[...], win_ref[...])
        for j in range(N_CHIPS):
            du = du + _dot_nt(dgp_ref[:, j * cm:(j + 1) * cm], wm_ref[j])
        xv = h_ref[...]
        dx, dln = _rms_bwd(du, xv, _rstd(xv), ln_ref[...])
        dln_ref[...] += dln
        dhi_ref[...] = dh_ref[...] + dx

    row = lambda c: pl.BlockSpec((tm, c), lambda i: (i, 0))
    full = lambda *s: pl.BlockSpec(s, lambda i: (0,) * len(s))
    return _pcall(
        body, [dh, h, ln, *parts, dff, dgpre, w_in, wm4], name="mix_in_bwd", grid=(t_tok // tm,),
        out_shape=[jax.ShapeDtypeStruct((t_tok, d), F32), jax.ShapeDtypeStruct((1, d), F32),
                   jax.ShapeDtypeStruct((t_tok, IN_PAD), BF)],
        in_specs=[row(d), row(d), full(1, d)] + [row(512)] * 7 + [row(128), row(2 * d), full(IN_PAD, d),
                                                                   full(N_CHIPS, d, cm)],
        out_specs=[row(d), full(1, d), row(IN_PAD)], comm=comm)


def _tail(h, p, target, ln_ple, ln_fin, wpg4, wpl4, comm=None):
    t_tok, d = h.shape
    pd = p.shape[1]
    rg = wpg4.shape[-2]
    cp = wpl4.shape[-1]
    tm = _tile(t_tok, 256)

    def body(h_ref, p_ref, t_ref, lp_ref, lf_ref, wg_ref, wp_ref,
             dh_ref, n_ref, dgp_ref, dpe_ref, pb_ref, loss_ref, dlf_ref, dlp_ref, pe_s, dn_s):
        @pl.when(pl.program_id(0) == 0)
        def _():
            loss_ref[...] = jnp.zeros_like(loss_ref)
            dlf_ref[...] = jnp.zeros_like(dlf_ref)
            dlp_ref[...] = jnp.zeros_like(dlp_ref)

        xv = h_ref[...]
        r3 = _rstd(xv)
        nb = (xv * r3 * lp_ref[...]).astype(BF)
        n_ref[...] = nb
        pb = p_ref[...].astype(BF)
        pb_ref[...] = pb
        pgpre = jnp.zeros((tm, d), F32)
        for j in range(N_CHIPS):
            pgpre = pgpre + _dot(nb[:, j * rg:(j + 1) * rg], wg_ref[j])
            pe_s[:, j * cp:(j + 1) * cp] = _dot(pb, wp_ref[j])
        pg = _sigmoid(pgpre)
        pe = pe_s[...]
        h4 = xv + pg * pe
        r4 = _rstd(h4)
        err = h4 * r4 * lf_ref[...] - t_ref[...]
        loss_ref[...] += 0.5 * jnp.sum(jnp.sum(err * err, axis=1, keepdims=True), axis=0, keepdims=True) / d
        dh4, dlf = _rms_bwd(err * (1.0 / d), h4, r4, lf_ref[...])
        dlf_ref[...] += dlf
        dpe_ref[...] = (dh4 * pg).astype(BF)
        dgp = (dh4 * pe * pg * (1.0 - pg)).astype(BF)
        dgp_ref[...] = dgp
        for j in range(N_CHIPS):
            dn_s[:, j * rg:(j + 1) * rg] = _dot_nt(dgp, wg_ref[j])
        dx, dlp = _rms_bwd(dn_s[...], xv, r3, lp_ref[...])
        dlp_ref[...] += dlp
        dh_ref[...] = dh4 + dx

    row = lambda c: pl.BlockSpec((tm, c), lambda i: (i, 0))
    full = lambda *s: pl.BlockSpec(s, lambda i: (0,) * len(s))
    sds = lambda c, dt: jax.ShapeDtypeStruct((t_tok, c), dt)
    vec = jax.ShapeDtypeStruct((1, d), F32)
    return _pcall(
        body, [h, p, target, ln_ple, ln_fin, wpg4, wpl4], name="tail", grid=(t_tok // tm,),
        out_shape=[sds(d, F32), sds(d, BF), sds(d, BF), sds(d, BF), sds(pd, BF),
                   jax.ShapeDtypeStruct((1, 128), F32), vec, vec],
        in_specs=[row(d), row(pd), row(d), full(1, d), full(1, d), full(N_CHIPS, rg, d), full(N_CHIPS, pd, cp)],
        out_specs=[row(d), row(d), row(d), row(d), row(pd), full(1, 128), full(1, d), full(1, d)],
        scratch=[pltpu.VMEM((tm, d), F32), pltpu.VMEM((tm, d), F32)], comm=comm)


BIG = ["w_ffn1_gate", "w_ffn1_up", "w_ffn1_down", "w_in", "w_merge", "w_ret_out", "w_fox_out", "w_out",
       "w_ffn2_gate", "w_ffn2_up", "w_ffn2_down", "w_ple", "w_ple_gate"]
SMALL = ["ln_ffn1", "ln_mix", "b_forget", "b_merge", "ln_ffn2", "ln_ple", "ln_final"]
WEIGHTS = ["ln_ffn1", "w_ffn1_gate", "w_ffn1_up", "w_ffn1_down", "ln_mix", "w_in", "b_forget", "w_merge", "b_merge",
           "w_ret_out", "w_fox_out", "w_out", "ln_ffn2", "w_ffn2_gate", "w_ffn2_up", "w_ffn2_down", "ln_ple",
           "w_ple", "w_ple_gate", "ln_final"]


TRANSPOSED = {"w_ffn1_gate", "w_ffn1_up", "w_ffn2_gate", "w_ffn2_up", "w_in"}
IN_ROWS_PAD = -(-(IN_COLS // N_CHIPS) // 32) * 32


def _pack_small(vals, loss_row):
    rows = [loss_row]
    for name in SMALL:
        v = vals[name].reshape(-1)
        n = -(-v.shape[0] // 128) * 128
        rows.append(jnp.pad(v, (0, n - v.shape[0])).reshape(n // 128, 128))
    packed = jnp.concatenate(rows, axis=0)
    pad = -packed.shape[0] % 8
    return jnp.pad(packed, ((0, pad), (0, 0)))


def _unpack_small(packed, sizes):
    out, r = {}, 1
    for name in SMALL:
        n = sizes[name]
        nr = -(-n // 128)
        out[name] = packed[r:r + nr].reshape(1, nr * 128)[:, :n]
        r += nr
    return out


class _Stage:
    def __init__(self, comm, finish):
        self.comm, self.finish, self.result = comm, finish, None


def _hosted(fn, *a, stages=()):
    if not stages:
        return fn(*a)
    outs, couts = fn(*a, comm=_merge([st.comm for st in stages]))
    for st, o in zip(stages, _split_outs([st.comm for st in stages], couts)):
        st.result = st.finish(o)
    return outs


class _Reducer:
    def __init__(self):
        self.done = {}

    def swap(self, grads):
        names = list(grads)
        return _Stage(_c_half_swap([grads[n] for n in names]),
                      lambda outs: dict(zip(names, _add_halves([(grads[n], o) for n, o in zip(names, outs)]))))

    def exchange(self, parts):
        names = list(parts)
        return _Stage(_c_chip_exchange([parts[n] for n in names]),
                      lambda outs: dict(zip(names, _sum_chips([(parts[n], o) for n, o in zip(names, outs)]))))

    def join(self, halves):
        names = list(halves)
        return _Stage(_c_join([halves[n] for n in names]),
                      lambda outs: self.done.update({n: (halves[n], o) for n, o in zip(names, outs)}))


def kernel(x, p, positions, ln_ffn1, w_ffn1_gate, w_ffn1_up, w_ffn1_down, ln_mix, w_in, b_forget, w_merge, b_merge, w_ret_out, w_fox_out, w_out, ln_ffn2, w_ffn2_gate, w_ffn2_up, w_ffn2_down, ln_ple, w_ple, w_ple_gate, ln_final, loss_target, m_ln_ffn1, m_w_ffn1_gate, m_w_ffn1_up, m_w_ffn1_down, m_ln_mix, m_w_in, m_b_forget, m_w_merge, m_b_merge, m_w_ret_out, m_w_fox_out, m_w_out, m_ln_ffn2, m_w_ffn2_gate, m_w_ffn2_up, m_w_ffn2_down, m_ln_ple, m_w_ple, m_w_ple_gate, m_ln_final, v_ln_ffn1, v_w_ffn1_gate, v_w_ffn1_up, v_w_ffn1_down, v_ln_mix, v_w_in, v_b_forget, v_w_merge, v_b_merge, v_w_ret_out, v_w_fox_out, v_w_out, v_ln_ffn2, v_w_ffn2_gate, v_w_ffn2_up, v_w_ffn2_down, v_ln_ple, v_w_ple, v_w_ple_gate, v_ln_final):
    args = dict(locals())
    w = {n: args[n] for n in WEIGHTS}
    m = {n: args["m_" + n] for n in WEIGHTS}
    v = {n: args["v_" + n] for n in WEIGHTS}
    d = x.shape[-1]
    t_tok = x.shape[1]
    xs, ps, target = x[0], p[0, 0], loss_target[0]
    small = {n: w[n].reshape(1, -1) for n in SMALL}

    def to2d(n, a):
        if n in TRANSPOSED:
            return a[0].T
        return a.reshape(a.shape[-2], a.shape[-1]) if a.ndim == 3 else a.reshape(1, -1)

    def from2d(n, a):
        return a.T[None] if n in TRANSPOSED else a.reshape(w[n].shape)

    def padded(n, a):
        return jnp.pad(a, ((0, IN_ROWS_PAD - a.shape[0]), (0, 0))) if n == "w_in" else a

    core = lax.axis_index("c")
    me = 2 * lax.axis_index("x") + lax.axis_index("y")
    shard = {}

    def set_shard(n, s2):
        s2 = padded(n, s2)
        shard[n] = s2.reshape(1, 2, s2.shape[0] // 2, s2.shape[1])

    first = ["w_ffn1_gate", "w_ffn1_up", "w_ffn1_down"]
    for n in first + ["w_in"]:
        set_shard(n, to2d(n, w[n]).astype(BF))
    full = {}

    def gather(names):
        bufs = [lax.dynamic_update_slice(jnp.zeros((N_CHIPS,) + shard[n].shape[1:], BF), shard[n], (me, 0, 0, 0))
                for n in names]

        def finish(outs):
            full.update({n: o.reshape(N_CHIPS, 2 * o.shape[2], o.shape[3]) for n, o in zip(names, outs)})

        return _Stage(_c_all_gather(bufs), finish)

    half = RET_DIM // 2
    inv_freq = 1.0 / (ROPE_BASE ** (jnp.arange(half, dtype=F32) / half))
    later = [n for n in BIG if n not in shard]
    cos_t, sin_t, *cast = _hosted(_rope_tables, positions[0].astype(F32).reshape(t_tok, 1),
                                  jnp.repeat(inv_freq, 2).reshape(1, RET_DIM), [to2d(n, w[n]) for n in later],
                                  stages=[gather(first)])
    for n, s2 in zip(later, cast):
        set_shard(n, s2)
    consts = _ret_consts()
    b_pad = jnp.pad(small["b_forget"], ((0, 0), (0, 128 - FOX_HEADS)))

    h1, n1, g1, u1 = _hosted(
        _ffn_fwd, xs, small["ln_ffn1"], full["w_ffn1_gate"], full["w_ffn1_up"], full["w_ffn1_down"],
        stages=[gather(["w_in", "w_merge", "w_ret_out", "w_fox_out", "w_out", "w_ple_gate", "w_ple"])])
    u, rq, rk, rv, rg, fq, fk, fv, ffl, ga, gb, w_in_full = _mix_in(
        h1, small["ln_mix"], full["w_in"], full["w_merge"], small["b_merge"], cos_t, sin_t)
    aq, ak = _forget_fwd(ffl, b_pad)
    y_raw, y_ret, states = _ret_fwd(rq, rk, rv, rg, consts)
    y_fox, y_fox32, aqb = _hosted(_fox_fwd, fq, fk, fv, aq, ak,
                                  stages=[gather(["w_ffn2_gate", "w_ffn2_up", "w_ffn2_down"])])
    h2, za, zb, mix = _mix_out(h1, y_ret, y_fox, ga, gb, full["w_ret_out"], full["w_fox_out"], full["w_out"])
    h3, n2, g2, u2 = _ffn_fwd(h2, small["ln_ffn2"], full["w_ffn2_gate"], full["w_ffn2_up"], full["w_ffn2_down"])

    red = _Reducer()
    dh3, n3, dpgpre, dpe, pb, loss, dln_final, dln_ple = _tail(
        h3, ps, target, small["ln_ple"], small["ln_final"], full["w_ple_gate"], full["w_ple"])
    g_f2 = dict(w_ple_gate=_wgrad_rows("wgrad_ple_gate", n3, dpgpre, N_CHIPS),
                w_ple=_wgrad_cols("wgrad_ple", pb, dpe, N_CHIPS))
    dh2, dln_ffn2, dg2, du2, a2, dhb3 = _ffn_bwd(
        dh3, h2, small["ln_ffn2"], g2, u2, full["w_ffn2_gate"], full["w_ffn2_up"], full["w_ffn2_down"])
    g_f2["w_ffn2_gate"] = _wgrad_b_shared("wgrad_ffn2_gate", dg2, n2)
    g_f2["w_ffn2_up"] = _wgrad_b_shared("wgrad_ffn2_up", du2, n2)
    g_f2["w_ffn2_down"] = _wgrad_b_shared("wgrad_ffn2_down", a2, dhb3)

    sw_f2 = red.swap(g_f2)
    dhb2, dgpre, dza, dzb, dy_ret, dy_fox, ad, db_merge = _hosted(
        _mix_out_bwd, dh2, za, zb, ga, gb, y_fox32, full["w_ret_out"], full["w_fox_out"], full["w_out"],
        stages=[sw_f2])
    g_br = dict(w_out=_wgrad_rows("wgrad_out", mix, dhb2, N_CHIPS),
                w_ret_out=_wgrad_cols("wgrad_ret_out", y_ret, dza, N_CHIPS),
                w_fox_out=_wgrad_cols("wgrad_fox_out", y_fox, dzb, N_CHIPS))

    sw_br = red.swap(g_br)
    drq, drk, drv, drg = _hosted(_ret_bwd, rq, rk, rv, rg, y_raw, dy_ret, states, consts, cos_t, sin_t,
                                 stages=[sw_br])
    ex_f2, ex_br = red.exchange(sw_f2.result), red.exchange(sw_br.result)
    dfq, dfk, dfv, dcum_t3, dcum_q = _hosted(_fox_bwd, fq, fk, fv, dy_fox, aqb, ak, ad, stages=[ex_f2, ex_br])
    dff, db_forget = _forget_bwd(dcum_t3.reshape(FOX_HEADS, t_tok), dcum_q, ffl, b_pad)
    dh1, dln_mix, dproj = _hosted(
        _mix_in_bwd, dh2, h1, small["ln_mix"], (drq, drk, drv, drg, dfq, dfk, dfv), dff, dgpre, w_in_full,
        full["w_merge"], stages=[red.join(ex_f2.result), red.join(ex_br.result)])

    results = {}

    def sc_update(names):
        res = _sc_adamw_halves([(to2d(n, w[n]), *red.done[n], to2d(n, m[n]), to2d(n, v[n])) for n in names])
        for q, n in enumerate(names):
            results[n] = tuple(from2d(n, a) for a in res[4 * q:4 * q + 4])

    for names in (["w_ffn2_gate", "w_ffn2_up", "w_ffn2_down"], ["w_out", "w_ple_gate"], ["w_ret_out", "w_fox_out"],
                  ["w_ple"]):
        sc_update(names)

    dx, dln_ffn1, dg1, du1, a1, dhb1 = _ffn_bwd(
        dh1, xs, small["ln_ffn1"], g1, u1, full["w_ffn1_gate"], full["w_ffn1_up"], full["w_ffn1_down"])
    g_f1g = _wgrad_b_shared("wgrad_ffn1_gate", dg1, n1)
    sw_f1g = red.swap(dict(w_ffn1_gate=g_f1g))
    g_f1u = _hosted(_wgrad_b_shared, "wgrad_ffn1_up", du1, n1, stages=[sw_f1g])
    ex_f1g, sw_f1u = red.exchange(sw_f1g.result), red.swap(dict(w_ffn1_up=g_f1u))
    g_f1d = _hosted(_wgrad_b_shared, "wgrad_ffn1_down", a1, dhb1, stages=[ex_f1g, sw_f1u])

    ex_f1u, sw_f1d = red.exchange(sw_f1u.result), red.swap(dict(w_ffn1_down=g_f1d))
    g_in = _hosted(_wgrad_in, dproj, u, stages=[ex_f1u, sw_f1d, red.join(ex_f1g.result)])
    sc_update(["w_ffn1_gate"])
    ex_f1d, sw_in = red.exchange(sw_f1d.result), red.swap(dict(w_in=g_in))
    g_mrg = _hosted(_wgrad_cols, "wgrad_merge", u, dgpre, N_CHIPS,
                    stages=[ex_f1d, sw_in, red.join(ex_f1u.result)])
    sc_update(["w_ffn1_up"])

    small_grads = dict(ln_ffn1=dln_ffn1, ln_mix=dln_mix, b_forget=db_forget[:, :FOX_HEADS], b_merge=db_merge,
                       ln_ffn2=dln_ffn2, ln_ple=dln_ple, ln_final=dln_final)
    sizes = {n: w[n].size for n in SMALL}
    ex_in, sw_mrg = red.exchange(sw_in.result), red.swap(dict(w_merge=g_mrg))
    reduced = _hosted(_all_reduce_small, _pack_small(small_grads, loss),
                      stages=[ex_in, sw_mrg, red.join(ex_f1d.result)])
    sc_update(["w_ffn1_down"])
    gsum = _unpack_small(reduced, sizes)
    loss = reduced[0, 0]
    ex_mrg = red.exchange(sw_mrg.result)
    _hosted(_exchange_only, stages=[ex_mrg, red.join(ex_in.result)])
    _hosted(_exchange_only, stages=[red.join(ex_mrg.result)])

    def update(names):
        w2, m2, v2 = ([to2d(n, a[n]) for n in names] for a in (w, m, v))
        n = names[0]
        if n == "w_in":
            mine, other = red.done[n]
            g2 = jnp.where(core == 0, jnp.concatenate([mine, other]), jnp.concatenate([other, mine]))
            g2 = g2[:w2[0].shape[0]]
            rows3 = lambda a: jnp.transpose(a, (2, 0, 1))
            g3 = g2.reshape(g2.shape[0], 1, g2.shape[1])
            res = [g3] + _adamw(rows3(w[n]), g3, rows3(m[n]), rows3(v[n]))
            results[n] = tuple(jnp.transpose(a, (1, 2, 0)) for a in res)
            return
        res = _adamw_halves([(w2[q], *red.done[names[q]], m2[q], v2[q]) for q in range(len(names))])
        for q, name in enumerate(names):
            results[name] = tuple(from2d(name, a) for a in res[4 * q:4 * q + 4])

    res = _adamw_vectors([(to2d(n, w[n]), gsum[n], to2d(n, m[n]), to2d(n, v[n])) for n in SMALL])
    for q, n in enumerate(SMALL):
        results[n] = tuple(from2d(n, a) for a in [gsum[n]] + res[3 * q:3 * q + 3])
    for n in WEIGHTS:
        if n not in results:
            update([n])

    outs = [[results[n][k] for n in WEIGHTS] for k in range(4)]
    return (loss, dx[None], *outs[0], *outs[1], *outs[2], *outs[3])
```
